```python
import jax, jax.numpy as jnp
from jax import lax
import numpy as np

D_MODEL = 1024
BATCH = 8
SEQ = 2048
DEPTH = 1

EPS = 1e-6
MLA_HEADS = 8
MLA_NOPE = 64
MLA_ROPE = 32
MLA_VDIM = 64
MLA_Q_RANK = 384
MLA_KV_RANK = 256
MLA_QK = MLA_NOPE + MLA_ROPE
MLA_WIDTH = MLA_HEADS * MLA_VDIM
ROPE_THETA = 10000.0
Q_BLOCK = 128
GLA_HEADS = 4
GLA_DK = D_MODEL // 2
GLA_DV = D_MODEL
GLA_HK = GLA_DK // GLA_HEADS
GLA_HV = GLA_DV // GLA_HEADS
GLA_GATE_RANK = 16
GLA_GATE_NORM = 16.0
GLA_CHUNK = 64
SPLITS = (MLA_Q_RANK, MLA_KV_RANK, MLA_ROPE, MLA_WIDTH,
          GLA_DK, GLA_DK, GLA_DV, GLA_GATE_RANK, GLA_DV,
          D_MODEL, D_MODEL)
D_IN = sum(SPLITS)

kernel_name = 'hybrid_mla_gla_block'


def rmsnorm(x, g):
    xf = x.astype(jnp.float32)
    y = xf * lax.rsqrt(jnp.mean(xf * xf, axis=-1, keepdims=True) + EPS)
    return (y * g.astype(jnp.float32)).astype(x.dtype)


def rope_tables(positions):
    half = MLA_ROPE // 2
    freqs = ROPE_THETA ** (-jnp.arange(half, dtype=jnp.float32) / half)
    ang = positions.astype(jnp.float32)[..., None] * freqs
    return jnp.cos(ang)[:, :, None, :], jnp.sin(ang)[:, :, None, :]


def apply_rope(x, cos, sin):
    half = MLA_ROPE // 2
    x1 = x[..., :half].astype(jnp.float32)
    x2 = x[..., half:].astype(jnp.float32)
    return jnp.concatenate([x1 * cos - x2 * sin, x2 * cos + x1 * sin], axis=-1).astype(x.dtype)


def mla_attention(q, k, v):
    B, S, H, _ = q.shape
    nb = S // Q_BLOCK
    scale = MLA_QK ** -0.5
    qb = q.reshape(B, nb, Q_BLOCK, H, MLA_QK).transpose(1, 0, 2, 3, 4)
    kpos = jnp.arange(S)

    def block(args):
        i, qi = args
        s = jnp.einsum('bqhd,bkhd->bhqk', qi, k, preferred_element_type=jnp.float32) * scale
        qpos = i * Q_BLOCK + jnp.arange(Q_BLOCK)
        s = jnp.where(kpos[None, :] <= qpos[:, None], s, -jnp.inf)
        p = jax.nn.softmax(s, axis=-1)
        return jnp.einsum('bhqk,bkhd->bqhd', p.astype(v.dtype), v)

    o = lax.map(block, (jnp.arange(nb), qb))
    return o.transpose(1, 0, 2, 3, 4).reshape(B, S, H, MLA_VDIM)


def gla_chunked(q, k, v, log_a):
    B, S, H, DK = q.shape
    DV = v.shape[-1]
    C = GLA_CHUNK
    N = S // C

    def chunks(t):
        return t.reshape(B, N, C, H, t.shape[-1]).transpose(1, 0, 3, 2, 4).astype(jnp.float32)

    qc = chunks(q) * (DK ** -0.5)
    kc, vc, gc = chunks(k), chunks(v), chunks(log_a)
    b = jnp.cumsum(gc, axis=3)
    b_last = b[:, :, :, -1:, :]
    q_in = qc * jnp.exp(b)
    k_in = kc * jnp.exp(-b)
    k_st = kc * jnp.exp(b_last - b)
    causal = jnp.tril(jnp.ones((C, C), jnp.float32))
    attn = jnp.einsum('nbhid,nbhjd->nbhij', q_in, k_in) * causal
    o_intra = jnp.einsum('nbhij,nbhjv->nbhiv', attn, vc)

    def step(state, inp):
        q_i, k_i, v_i, dec = inp
        o = jnp.einsum('bhid,bhdv->bhiv', q_i, state)
        state = dec[:, :, 0, :, None] * state + jnp.einsum('bhjd,bhjv->bhdv', k_i, v_i)
        return state, o

    s0 = jnp.zeros((B, H, DK, DV), jnp.float32)
    _, o_inter = lax.scan(step, s0, (q_in, k_st, vc, jnp.exp(b_last)))
    o = o_intra + o_inter
    return o.transpose(1, 0, 3, 2, 4).reshape(B, S, H, DV)


def _fwd_setup_inputs(seed: int = 0) -> dict:
    key = jax.random.key(seed)
    ks = jax.random.split(key, 20)

    def w(k, shape, fan_in):
        return jax.random.normal(k, shape, jnp.float32) * fan_in ** -0.5

    def gain(k, shape):
        return 1.0 + 0.02 * jax.random.normal(k, shape, jnp.float32)

    x = jax.random.normal(ks[0], (BATCH, SEQ, D_MODEL), jnp.float32)
    positions = (jnp.arange(SEQ, dtype=jnp.int32)[None, :]
                 + jax.random.randint(ks[1], (BATCH, 1), 0, 64, dtype=jnp.int32))
    return {
        'x': x,
        'positions': positions,
        'g_in': gain(ks[2], (DEPTH, D_MODEL)),
        'w_in': w(ks[3], (DEPTH, D_MODEL, D_IN), D_MODEL),
        'g_q': gain(ks[4], (DEPTH, MLA_Q_RANK)),
        'w_uq': w(ks[5], (DEPTH, MLA_Q_RANK, MLA_HEADS * MLA_QK), MLA_Q_RANK),
        'g_kv': gain(ks[6], (DEPTH, MLA_KV_RANK)),
        'w_ukv': w(ks[7], (DEPTH, MLA_KV_RANK, MLA_HEADS * (MLA_NOPE + MLA_VDIM)), MLA_KV_RANK),
        'w_gla_gate': w(ks[8], (DEPTH, GLA_GATE_RANK, GLA_DK), GLA_GATE_RANK),
        'b_gla_gate': 0.01 * jax.random.normal(ks[9], (DEPTH, GLA_DK), jnp.float32),
        'g_gla': gain(ks[10], (DEPTH, GLA_HV)),
        'w_proj_mla': w(ks[11], (DEPTH, MLA_WIDTH, D_MODEL), MLA_WIDTH),
        'w_proj_gla': w(ks[12], (DEPTH, GLA_DV, D_MODEL), GLA_DV),
        'w_out': w(ks[13], (DEPTH, D_MODEL, D_MODEL), D_MODEL),
        'g_final': gain(ks[14], (D_MODEL,)),
    }


def _fwd_reference(x, positions, g_in, w_in, g_q, w_uq, g_kv, w_ukv, w_gla_gate, b_gla_gate,
              g_gla, w_proj_mla, w_proj_gla, w_out, g_final):
    B, S, _ = x.shape
    cos, sin = rope_tables(positions)
    split_points = [int(p) for p in np.cumsum(SPLITS)[:-1]]
    for l in range(DEPTH):
        h = rmsnorm(x, g_in[l])
        proj = h @ w_in[l]
        (c_q, c_kv, k_r, z_mla, q_g, k_g, v_g, a_lr, z_gla,
         gate_mla, gate_gla) = jnp.split(proj, split_points, axis=-1)

        q = (rmsnorm(c_q, g_q[l]) @ w_uq[l]).reshape(B, S, MLA_HEADS, MLA_QK)
        q = jnp.concatenate([q[..., :MLA_NOPE], apply_rope(q[..., MLA_NOPE:], cos, sin)], axis=-1)
        kv = (rmsnorm(c_kv, g_kv[l]) @ w_ukv[l]).reshape(B, S, MLA_HEADS, MLA_NOPE + MLA_VDIM)
        k_nope, v_mla = kv[..., :MLA_NOPE], kv[..., MLA_NOPE:]
        k_rope = apply_rope(k_r[:, :, None, :], cos, sin)
        k = jnp.concatenate([k_nope, jnp.broadcast_to(k_rope, (B, S, MLA_HEADS, MLA_ROPE))], axis=-1)
        o_mla = mla_attention(q, k, v_mla).reshape(B, S, MLA_WIDTH)
        y_mla = (o_mla * jax.nn.silu(z_mla)) @ w_proj_mla[l]

        log_a = jax.nn.log_sigmoid((a_lr @ w_gla_gate[l] + b_gla_gate[l]).astype(jnp.float32)) / GLA_GATE_NORM
        o_gla = gla_chunked(q_g.reshape(B, S, GLA_HEADS, GLA_HK),
                            k_g.reshape(B, S, GLA_HEADS, GLA_HK),
                            v_g.reshape(B, S, GLA_HEADS, GLA_HV),
                            log_a.reshape(B, S, GLA_HEADS, GLA_HK))
        o_gla = rmsnorm(o_gla, g_gla[l]).astype(x.dtype).reshape(B, S, GLA_DV)
        y_gla = (o_gla * jax.nn.silu(z_gla)) @ w_proj_gla[l]

        merged = jax.nn.sigmoid(gate_mla) * y_mla + jax.nn.sigmoid(gate_gla) * y_gla
        x = x + merged @ w_out[l]
    return rmsnorm(x, g_final)


import jax as _jax
import jax.numpy as _jnp

TWIN_FORMAT = 'train_step'
FWD_PARAMS = ['x', 'positions', 'g_in', 'w_in', 'g_q', 'w_uq', 'g_kv', 'w_ukv', 'w_gla_gate', 'b_gla_gate', 'g_gla', 'w_proj_mla', 'w_proj_gla', 'w_out', 'g_final']
TWIN_WEIGHTS = ['g_in', 'w_in', 'g_q', 'w_uq', 'g_kv', 'w_ukv', 'w_gla_gate', 'b_gla_gate', 'g_gla', 'w_proj_mla', 'w_proj_gla', 'w_out', 'g_final']
TWIN_DIFF_INPUT = 'x'
TWIN_INPUTS = ['x', 'positions', 'g_in', 'w_in', 'g_q', 'w_uq', 'g_kv', 'w_ukv', 'w_gla_gate', 'b_gla_gate', 'g_gla', 'w_proj_mla', 'w_proj_gla', 'w_out', 'g_final', 'loss_target', 'm_g_in', 'm_w_in', 'm_g_q', 'm_w_uq', 'm_g_kv', 'm_w_ukv', 'm_w_gla_gate', 'm_b_gla_gate', 'm_g_gla', 'm_w_proj_mla', 'm_w_proj_gla', 'm_w_out', 'm_g_final', 'v_g_in', 'v_w_in', 'v_g_q', 'v_w_uq', 'v_g_kv', 'v_w_ukv', 'v_w_gla_gate', 'v_b_gla_gate', 'v_g_gla', 'v_w_proj_mla', 'v_w_proj_gla', 'v_w_out', 'v_g_final']
TWIN_OUTPUTS = ['loss', 'grad_x', 'grad_g_in', 'grad_w_in', 'grad_g_q', 'grad_w_uq', 'grad_g_kv', 'grad_w_ukv', 'grad_w_gla_gate', 'grad_b_gla_gate', 'grad_g_gla', 'grad_w_proj_mla', 'grad_w_proj_gla', 'grad_w_out', 'grad_g_final', 'delta_g_in', 'delta_w_in', 'delta_g_q', 'delta_w_uq', 'delta_g_kv', 'delta_w_ukv', 'delta_w_gla_gate', 'delta_b_gla_gate', 'delta_g_gla', 'delta_w_proj_mla', 'delta_w_proj_gla', 'delta_w_out', 'delta_g_final', 'new_m_g_in', 'new_m_w_in', 'new_m_g_q', 'new_m_w_uq', 'new_m_g_kv', 'new_m_w_ukv', 'new_m_w_gla_gate', 'new_m_b_gla_gate', 'new_m_g_gla', 'new_m_w_proj_mla', 'new_m_w_proj_gla', 'new_m_w_out', 'new_m_g_final', 'new_v_g_in', 'new_v_w_in', 'new_v_g_q', 'new_v_w_uq', 'new_v_g_kv', 'new_v_w_ukv', 'new_v_w_gla_gate', 'new_v_b_gla_gate', 'new_v_g_gla', 'new_v_w_proj_mla', 'new_v_w_proj_gla', 'new_v_w_out', 'new_v_g_final']
TWIN_LEAF_KINDS = {'loss': 'loss', 'grad_x': 'grad_x', 'grad_g_in': 'grad_w', 'grad_w_in': 'grad_w', 'grad_g_q': 'grad_w', 'grad_w_uq': 'grad_w', 'grad_g_kv': 'grad_w', 'grad_w_ukv': 'grad_w', 'grad_w_gla_gate': 'grad_w', 'grad_b_gla_gate': 'grad_w', 'grad_g_gla': 'grad_w', 'grad_w_proj_mla': 'grad_w', 'grad_w_proj_gla': 'grad_w', 'grad_w_out': 'grad_w', 'grad_g_final': 'grad_w', 'delta_g_in': 'delta_w', 'delta_w_in': 'delta_w', 'delta_g_q': 'delta_w', 'delta_w_uq': 'delta_w', 'delta_g_kv': 'delta_w', 'delta_w_ukv': 'delta_w', 'delta_w_gla_gate': 'delta_w', 'delta_b_gla_gate': 'delta_w', 'delta_g_gla': 'delta_w', 'delta_w_proj_mla': 'delta_w', 'delta_w_proj_gla': 'delta_w', 'delta_w_out': 'delta_w', 'delta_g_final': 'delta_w', 'new_m_g_in': 'new_m', 'new_m_w_in': 'new_m', 'new_m_g_q': 'new_m', 'new_m_w_uq': 'new_m', 'new_m_g_kv': 'new_m', 'new_m_w_ukv': 'new_m', 'new_m_w_gla_gate': 'new_m', 'new_m_b_gla_gate': 'new_m', 'new_m_g_gla': 'new_m', 'new_m_w_proj_mla': 'new_m', 'new_m_w_proj_gla': 'new_m', 'new_m_w_out': 'new_m', 'new_m_g_final': 'new_m', 'new_v_g_in': 'new_v', 'new_v_w_in': 'new_v', 'new_v_g_q': 'new_v', 'new_v_w_uq': 'new_v', 'new_v_g_kv': 'new_v', 'new_v_w_ukv': 'new_v', 'new_v_w_gla_gate': 'new_v', 'new_v_b_gla_gate': 'new_v', 'new_v_g_gla': 'new_v', 'new_v_w_proj_mla': 'new_v', 'new_v_w_proj_gla': 'new_v', 'new_v_w_out': 'new_v', 'new_v_g_final': 'new_v'}


def _forward(args):
    return _fwd_reference(*[args[k] for k in FWD_PARAMS])


def _output_shape():
    out = _jax.eval_shape(lambda: _forward(_fwd_setup_inputs(0)))
    return out.shape, out.dtype

N_MICROBATCH = 1
ADAM_LR = 0.001
ADAM_B1 = 0.9
ADAM_B2 = 0.999
ADAM_EPS = 1e-08
ADAM_WD = 0.01
ADAM_STEP = 10
PER_EXAMPLE_BATCH_AXIS = {'x': 0, 'positions': 0, 'loss_target': 0}
SHARED_INPUTS = []
_WEIGHT_DTYPES = {'g_in': _jnp.float32, 'w_in': _jnp.float32, 'g_q': _jnp.float32, 'w_uq': _jnp.float32, 'g_kv': _jnp.float32, 'w_ukv': _jnp.float32, 'w_gla_gate': _jnp.float32, 'b_gla_gate': _jnp.float32, 'g_gla': _jnp.float32, 'w_proj_mla': _jnp.float32, 'w_proj_gla': _jnp.float32, 'w_out': _jnp.float32, 'g_final': _jnp.float32}
MOMENT_SCALE = {'g_in': 8.335195e-02, 'w_in': 3.276804e-02, 'g_q': 1.141767e-02, 'w_uq': 8.014699e-03, 'g_kv': 2.106288e-02, 'w_ukv': 1.036651e-02, 'w_gla_gate': 6.448058e-03, 'b_gla_gate': 2.640116e-02, 'g_gla': 7.593079e-02, 'w_proj_mla': 8.333278e-03, 'w_proj_gla': 3.856489e-02, 'w_out': 3.921646e-02, 'g_final': 1.599615e+01}


def _to_microbatches(a, axis):
    t = _jnp.moveaxis(a, axis, 0)
    t = t.reshape((N_MICROBATCH, t.shape[0] // N_MICROBATCH) + t.shape[1:])
    return _jnp.moveaxis(t, 1, axis + 1)


def setup_inputs(seed: int = 0) -> dict:
    inp = _fwd_setup_inputs(seed)
    key = _jax.random.fold_in(_jax.random.key(seed), 7919)
    shape, _ = _output_shape()
    out = dict(inp)
    out["loss_target"] = _jax.random.normal(_jax.random.fold_in(key, 0), shape, _jnp.float32)
    for i, name in enumerate(TWIN_WEIGHTS):
        w = inp[name].astype(_jnp.float32)
        if MOMENT_SCALE is None:
            s = _jnp.sqrt(_jnp.mean(_jnp.square(w)) + 1e-30)
        else:
            s = MOMENT_SCALE[name]
        km, kv = _jax.random.split(_jax.random.fold_in(key, i + 1))
        out[name] = w
        out["m_" + name] = s * _jax.random.normal(km, w.shape, _jnp.float32)
        out["v_" + name] = (s * s) * _jax.random.uniform(kv, w.shape, _jnp.float32, 0.5, 1.5)
    if N_MICROBATCH > 1:
        for name, axis in PER_EXAMPLE_BATCH_AXIS.items():
            out[name] = _to_microbatches(out[name], axis)
    return {'x': out['x'], 'positions': out['positions'], 'g_in': out['g_in'], 'w_in': out['w_in'], 'g_q': out['g_q'], 'w_uq': out['w_uq'], 'g_kv': out['g_kv'], 'w_ukv': out['w_ukv'], 'w_gla_gate': out['w_gla_gate'], 'b_gla_gate': out['b_gla_gate'], 'g_gla': out['g_gla'], 'w_proj_mla': out['w_proj_mla'], 'w_proj_gla': out['w_proj_gla'], 'w_out': out['w_out'], 'g_final': out['g_final'], 'loss_target': out['loss_target'], 'm_g_in': out['m_g_in'], 'm_w_in': out['m_w_in'], 'm_g_q': out['m_g_q'], 'm_w_uq': out['m_w_uq'], 'm_g_kv': out['m_g_kv'], 'm_w_ukv': out['m_w_ukv'], 'm_w_gla_gate': out['m_w_gla_gate'], 'm_b_gla_gate': out['m_b_gla_gate'], 'm_g_gla': out['m_g_gla'], 'm_w_proj_mla': out['m_w_proj_mla'], 'm_w_proj_gla': out['m_w_proj_gla'], 'm_w_out': out['m_w_out'], 'm_g_final': out['m_g_final'], 'v_g_in': out['v_g_in'], 'v_w_in': out['v_w_in'], 'v_g_q': out['v_g_q'], 'v_w_uq': out['v_w_uq'], 'v_g_kv': out['v_g_kv'], 'v_w_ukv': out['v_w_ukv'], 'v_w_gla_gate': out['v_w_gla_gate'], 'v_b_gla_gate': out['v_b_gla_gate'], 'v_g_gla': out['v_g_gla'], 'v_w_proj_mla': out['v_w_proj_mla'], 'v_w_proj_gla': out['v_w_proj_gla'], 'v_w_out': out['v_w_out'], 'v_g_final': out['v_g_final']}


def _loss(weights, diff, rest, loss_target):
    with _jax.named_scope("forward"):
        args = {**rest, TWIN_DIFF_INPUT: diff, **{k: w.astype(_WEIGHT_DTYPES[k]) for k, w in weights.items()}}
        y = _forward(args)
    with _jax.named_scope("loss_head"):
        err = _jnp.square(y.astype(_jnp.float32) - loss_target)
        return 0.5 * _jnp.sum(_jnp.mean(err, axis=-1)) if err.ndim else 0.5 * err


def _adamw(w, g, m, v):
    m = ADAM_B1 * m + (1.0 - ADAM_B1) * g
    v = ADAM_B2 * v + (1.0 - ADAM_B2) * _jnp.square(g)
    m_hat = m / (1.0 - ADAM_B1 ** ADAM_STEP)
    v_hat = v / (1.0 - ADAM_B2 ** ADAM_STEP)
    delta = -ADAM_LR * (m_hat / (_jnp.sqrt(v_hat) + ADAM_EPS) + ADAM_WD * w)
    return delta, m, v


def reference(x, positions, g_in, w_in, g_q, w_uq, g_kv, w_ukv, w_gla_gate, b_gla_gate, g_gla, w_proj_mla, w_proj_gla, w_out, g_final, loss_target, m_g_in, m_w_in, m_g_q, m_w_uq, m_g_kv, m_w_ukv, m_w_gla_gate, m_b_gla_gate, m_g_gla, m_w_proj_mla, m_w_proj_gla, m_w_out, m_g_final, v_g_in, v_w_in, v_g_q, v_w_uq, v_g_kv, v_w_ukv, v_w_gla_gate, v_b_gla_gate, v_g_gla, v_w_proj_mla, v_w_proj_gla, v_w_out, v_g_final):
    given = dict(x=x, positions=positions, g_in=g_in, w_in=w_in, g_q=g_q, w_uq=w_uq, g_kv=g_kv, w_ukv=w_ukv, w_gla_gate=w_gla_gate, b_gla_gate=b_gla_gate, g_gla=g_gla, w_proj_mla=w_proj_mla, w_proj_gla=w_proj_gla, w_out=w_out, g_final=g_final, loss_target=loss_target, m_g_in=m_g_in, m_w_in=m_w_in, m_g_q=m_g_q, m_w_uq=m_w_uq, m_g_kv=m_g_kv, m_w_ukv=m_w_ukv, m_w_gla_gate=m_w_gla_gate, m_b_gla_gate=m_b_gla_gate, m_g_gla=m_g_gla, m_w_proj_mla=m_w_proj_mla, m_w_proj_gla=m_w_proj_gla, m_w_out=m_w_out, m_g_final=m_g_final, v_g_in=v_g_in, v_w_in=v_w_in, v_g_q=v_g_q, v_w_uq=v_w_uq, v_g_kv=v_g_kv, v_w_ukv=v_w_ukv, v_w_gla_gate=v_w_gla_gate, v_b_gla_gate=v_b_gla_gate, v_g_gla=v_g_gla, v_w_proj_mla=v_w_proj_mla, v_w_proj_gla=v_w_proj_gla, v_w_out=v_w_out, v_g_final=v_g_final)
    weights = {n: given[n] for n in TWIN_WEIGHTS}
    shared = {n: given[n] for n in SHARED_INPUTS}
    per_example = {n: given[n] for n in ['x', 'positions']}
    grad_fn = _jax.value_and_grad(_loss, argnums=(0, 1))

    def one_microbatch(ex, loss_target):
        ex = dict(ex)
        diff = ex.pop(TWIN_DIFF_INPUT)
        return grad_fn(weights, diff, {**shared, **ex}, loss_target)

    if N_MICROBATCH == 1:
        loss, (grad_w, grad_x) = one_microbatch(per_example, given["loss_target"])
    else:
        def body(carry, xs):
            loss_sum, grad_sum = carry
            l_k, (gw_k, gx_k) = one_microbatch(xs[0], xs[1])
            with _jax.named_scope("update"):
                return (loss_sum + l_k, _jax.tree.map(_jnp.add, grad_sum, gw_k)), gx_k

        init = (_jnp.zeros((), _jnp.float32), _jax.tree.map(_jnp.zeros_like, weights))
        (loss, grad_w), grad_x = _jax.lax.scan(body, init, (per_example, given["loss_target"]))
    with _jax.named_scope("update"):
        delta_w, new_m, new_v = {}, {}, {}
        for n in TWIN_WEIGHTS:
            delta_w[n], new_m[n], new_v[n] = _adamw(weights[n], grad_w[n], given["m_" + n], given["v_" + n])
    return (loss, grad_x, *[grad_w[n] for n in TWIN_WEIGHTS], *[delta_w[n] for n in TWIN_WEIGHTS],
            *[new_m[n] for n in TWIN_WEIGHTS], *[new_v[n] for n in TWIN_WEIGHTS])
```

```python
import functools

import jax
import jax.numpy as jnp
import numpy as np
from jax import lax
from jax.experimental import pallas as pl
from jax.experimental.pallas import tpu as pltpu

F32 = jnp.float32
BF16 = jnp.bfloat16
MESH = pl.DeviceIdType.MESH
N_DEV = 8

D_MODEL = 1024
EPS = 1e-6
MLA_HEADS = 8
MLA_NOPE = 64
MLA_ROPE = 32
MLA_VDIM = 64
MLA_Q_RANK = 384
MLA_KV_RANK = 256
MLA_QK = MLA_NOPE + MLA_ROPE
MLA_WIDTH = MLA_HEADS * MLA_VDIM
ROPE_THETA = 10000.0
GLA_HEADS = 4
GLA_DK = 512
GLA_DV = 1024
GLA_HK = 128
GLA_HV = 256
GLA_GATE_RANK = 16
GLA_GATE_NORM = 16.0
GLA_CHUNK = 64
D_IN = 6320

ADAM_LR = 0.001
ADAM_B1 = 0.9
ADAM_B2 = 0.999
ADAM_EPS = 1e-08
ADAM_WD = 0.01
ADAM_STEP = 10

LANE = 128
HEAD_PAD = 128
VMEM_LIMIT = 48 * 1024 * 1024

P_ZGLA, P_GMLA, P_GGLA, P_VG = 0, 1024, 2048, 3072
P_ZMLA, P_QG, P_KG = 4096, 4608, 5120
P_CQ, P_CKV, P_MISC = 5632, 6144, 6400
P_TOTAL = 6528
MISC_KR = 64
MISC_ALR = 96

PACK_ROWS = (6320, 288, 256, 8, 512, 1024, 1024)
PACK_TOTAL = sum(PACK_ROWS)
PACK_PAD = 9440
SMALL_SIZES = (1024, 384, 256, 512, 256, 1024)
SMALL_ROWS = 32


def _cparams(sem=None):
    if sem is None:
        return pltpu.CompilerParams(vmem_limit_bytes=VMEM_LIMIT)
    return pltpu.CompilerParams(dimension_semantics=sem, vmem_limit_bytes=VMEM_LIMIT)


def _sigmoid(v):
    return 1.0 / (1.0 + jnp.exp(-v))


def _dot(a, b):
    return jnp.dot(a, b, preferred_element_type=F32)


def _dot_nt(a, b):
    return lax.dot_general(a, b, (((1,), (1,)), ((), ())), preferred_element_type=F32)


def _dot_tn(a, b):
    return lax.dot_general(a, b, (((0,), (0,)), ((), ())), preferred_element_type=F32)


def _dot_exact(a, b):
    return jnp.dot(a, b, preferred_element_type=F32, precision=lax.Precision.HIGHEST)


def _rope_fwd(blk, c, sn, sp):
    return blk * c + pltpu.roll(blk, LANE - 16, 1) * sn + pltpu.roll(blk, 16, 1) * sp


def _rope_bwd(blk, c, sn, sp):
    return blk * c + pltpu.roll(blk * sn, 16, 1) + pltpu.roll(blk * sp, LANE - 16, 1)


def _mesh_pos():
    return lax.axis_index("x"), lax.axis_index("y"), lax.axis_index("c")


def _all_gather(shard):
    rows, lanes = shard.shape

    def body(x_ref, out_ref, send_sems, recv_sems, local_sem):
        x, y, c = _mesh_pos()
        me, sibling = (x, y, c), (x, y, 1 - c)
        chips = [(1 - x, y), (x, 1 - y), (1 - x, 1 - y)]

        def slot(px, py, pc):
            return out_ref.at[4 * px + 2 * py + pc]

        def copy(k, block, to, src=None):
            return pltpu.make_async_remote_copy(
                src_ref=slot(*block) if src is None else src, dst_ref=slot(*block),
                send_sem=send_sems.at[k], recv_sem=recv_sems.at[k],
                device_id=to, device_id_type=MESH)

        mine = pltpu.make_async_copy(x_ref, slot(*me), local_sem)
        mine.start()
        first = [copy(0, me, sibling, src=x_ref)]
        first += [copy(1 + j, me, (*chip, c), src=x_ref) for j, chip in enumerate(chips)]
        for cp in first:
            cp.start()
        passed = [copy(4 + j, (*chip, c), sibling) for j, chip in enumerate(chips)]
        for j, chip in enumerate(chips):
            copy(1 + j, (*chip, c), me).wait_recv()
            passed[j].start()
        copy(0, sibling, me).wait_recv()
        for j, chip in enumerate(chips):
            copy(4 + j, (*chip, 1 - c), me).wait_recv()
        for cp in first + passed:
            cp.wait_send()
        mine.wait()

    return pl.pallas_call(
        body, name="all_gather_weights",
        out_shape=jax.ShapeDtypeStruct((N_DEV, rows, lanes), shard.dtype),
        in_specs=[pl.BlockSpec(memory_space=pltpu.HBM)],
        out_specs=pl.BlockSpec(memory_space=pltpu.HBM),
        scratch_shapes=[pltpu.SemaphoreType.DMA((7,)), pltpu.SemaphoreType.DMA((7,)),
                        pltpu.SemaphoreType.DMA],
    )(shard)


def _exchange_grads(parts, small):
    _, rows, lanes = parts.shape
    srows = small.shape[0]

    def body(p_ref, s_ref, out_ref, sout_ref, send_sems, recv_sems, ssend_sems, srecv_sems,
             local_sems):
        x, y, c = _mesh_pos()
        me = 4 * x + 2 * y + c
        own = pltpu.make_async_copy(p_ref.at[me], out_ref.at[me], local_sems.at[0])
        sown = pltpu.make_async_copy(s_ref, sout_ref.at[me], local_sems.at[1])
        own.start()
        sown.start()
        big, tiny = [], []
        for k in range(1, N_DEV):
            px, py, pc = x ^ (k >> 2), y ^ ((k >> 1) & 1), c ^ (k & 1)
            peer = 4 * px + 2 * py + pc
            big.append(pltpu.make_async_remote_copy(
                src_ref=p_ref.at[peer], dst_ref=out_ref.at[me],
                send_sem=send_sems.at[k - 1], recv_sem=recv_sems.at[k - 1],
                device_id=(px, py, pc), device_id_type=MESH))
            tiny.append(pltpu.make_async_remote_copy(
                src_ref=s_ref, dst_ref=sout_ref.at[me],
                send_sem=ssend_sems.at[k - 1], recv_sem=srecv_sems.at[k - 1],
                device_id=(px, py, pc), device_id_type=MESH))
        for cp in tiny + big:
            cp.start()
        for cp in tiny + big:
            cp.wait_recv()
        for cp in tiny + big:
            cp.wait_send()
        own.wait()
        sown.wait()

    return pl.pallas_call(
        body, name="exchange_grads",
        out_shape=(jax.ShapeDtypeStruct((N_DEV, rows, lanes), parts.dtype),
                   jax.ShapeDtypeStruct((N_DEV, srows, lanes), small.dtype)),
        in_specs=[pl.BlockSpec(memory_space=pltpu.HBM), pl.BlockSpec(memory_space=pltpu.HBM)],
        out_specs=(pl.BlockSpec(memory_space=pltpu.HBM), pl.BlockSpec(memory_space=pltpu.HBM)),
        scratch_shapes=[pltpu.SemaphoreType.DMA((7,)), pltpu.SemaphoreType.DMA((7,)),
                        pltpu.SemaphoreType.DMA((7,)), pltpu.SemaphoreType.DMA((7,)),
                        pltpu.SemaphoreType.DMA((2,))],
    )(parts, small)


def _inproj(x, g_in, w_p):
    t = x.shape[0]
    tm = min(256, t)
    nj = 3
    tn = P_TOTAL // nj

    def body(x_ref, g_ref, w_ref, proj_ref, h_ref, r_ref):
        xf = x_ref[...]
        r = lax.rsqrt(jnp.mean(xf * xf, axis=-1, keepdims=True) + EPS)
        h = ((xf * r) * g_ref[...]).astype(BF16)
        proj_ref[...] = _dot(h, w_ref[...])

        @pl.when(pl.program_id(0) == 0)
        def _():
            h_ref[...] = h
            r_ref[...] = r

    first = lambda j, i: (jnp.where(j == 0, i, t // tm - 1), 0)
    return pl.pallas_call(
        body, name="inproj",
        grid=(nj, t // tm),
        in_specs=[pl.BlockSpec((tm, D_MODEL), lambda j, i: (i, 0)),
                  pl.BlockSpec((1, D_MODEL), lambda j, i: (0, 0)),
                  pl.BlockSpec((D_MODEL, tn), lambda j, i: (0, j))],
        out_specs=(pl.BlockSpec((tm, tn), lambda j, i: (i, j)),
                   pl.BlockSpec((tm, D_MODEL), first),
                   pl.BlockSpec((tm, 1), first)),
        out_shape=(jax.ShapeDtypeStruct((t, P_TOTAL), F32),
                   jax.ShapeDtypeStruct((t, D_MODEL), BF16),
                   jax.ShapeDtypeStruct((t, 1), F32)),
        compiler_params=_cparams(("arbitrary", "arbitrary")),
    )(x, g_in, w_p)


def _mla_prep(proj, g_q, g_kv, w_uq_p, w_k_p, w_v, w_gate_p, b_gate, rc, rsn, rsp):
    t = proj.shape[0]
    tm = min(256, t)
    hq = MLA_HEADS * HEAD_PAD

    def body(cq_ref, ckv_ref, misc_ref, gq_ref, gkv_ref, wuq_ref, wk_ref, wv_ref, wg_ref, bg_ref,
             c_ref, sn_ref, sp_ref,
             q_ref, k_ref, v_ref, la_ref, pre_ref, cqn_ref, ckvn_ref, rq_ref, rkv_ref):
        c, sn, sp = c_ref[...], sn_ref[...], sp_ref[...]
        cq = cq_ref[:, :MLA_Q_RANK]
        rq = lax.rsqrt(jnp.mean(cq * cq, axis=-1, keepdims=True) + EPS)
        cqn = ((cq * rq) * gq_ref[...]).astype(BF16)
        cqn_ref[...] = cqn
        rq_ref[...] = rq
        qpre = _dot(cqn, wuq_ref[...])
        ckv = ckv_ref[...]
        rkv = lax.rsqrt(jnp.mean(ckv * ckv, axis=-1, keepdims=True) + EPS)
        ckvn = ((ckv * rkv) * gkv_ref[...]).astype(BF16)
        ckvn_ref[...] = ckvn
        rkv_ref[...] = rkv
        kn = _dot(ckvn, wk_ref[...])
        v_ref[...] = _dot(ckvn, wv_ref[...]).astype(BF16)
        misc = misc_ref[...]
        krope = _rope_fwd(misc, c, sn, sp)
        for h in range(MLA_HEADS):
            sl = slice(h * HEAD_PAD, (h + 1) * HEAD_PAD)
            q_ref[:, sl] = _rope_fwd(qpre[:, sl], c, sn, sp).astype(BF16)
            k_ref[:, sl] = (kn[:, sl] + krope).astype(BF16)
        pre = _dot(misc.astype(BF16), wg_ref[...]) + bg_ref[...]
        pre_ref[...] = pre
        la_ref[...] = (jnp.minimum(pre, 0.0) - jnp.log(1.0 + jnp.exp(-jnp.abs(pre)))) / GLA_GATE_NORM

    row = lambda w: pl.BlockSpec((tm, w), lambda i: (i, 0))
    full = lambda a: pl.BlockSpec(a.shape, lambda i: (0, 0))
    return pl.pallas_call(
        body, name="mla_prep",
        grid=(t // tm,),
        in_specs=[pl.BlockSpec((tm, 512), lambda i: (i, P_CQ // 512)),
                  pl.BlockSpec((tm, MLA_KV_RANK), lambda i: (i, P_CKV // MLA_KV_RANK)),
                  pl.BlockSpec((tm, LANE), lambda i: (i, P_MISC // LANE)),
                  full(g_q), full(g_kv), full(w_uq_p), full(w_k_p), full(w_v), full(w_gate_p),
                  full(b_gate), row(LANE), row(LANE), row(LANE)],
        out_specs=(row(hq), row(hq), row(MLA_WIDTH), row(GLA_DK), row(GLA_DK),
                   row(MLA_Q_RANK), row(MLA_KV_RANK), row(1), row(1)),
        out_shape=(jax.ShapeDtypeStruct((t, hq), BF16), jax.ShapeDtypeStruct((t, hq), BF16),
                   jax.ShapeDtypeStruct((t, MLA_WIDTH), BF16),
                   jax.ShapeDtypeStruct((t, GLA_DK), F32), jax.ShapeDtypeStruct((t, GLA_DK), F32),
                   jax.ShapeDtypeStruct((t, MLA_Q_RANK), BF16),
                   jax.ShapeDtypeStruct((t, MLA_KV_RANK), BF16),
                   jax.ShapeDtypeStruct((t, 1), F32), jax.ShapeDtypeStruct((t, 1), F32)),
        compiler_params=_cparams(("arbitrary",)),
    )(proj, proj, proj, g_q, g_kv, w_uq_p, w_k_p, w_v, w_gate_p, b_gate, rc, rsn, rsp)


def _attn_masks(tq, t):
    rows = pl.program_id(1) * tq + lax.broadcasted_iota(jnp.int32, (tq, t), 0)
    cols = lax.broadcasted_iota(jnp.int32, (tq, t), 1)
    lane = lax.broadcasted_iota(jnp.int32, (tq, LANE), 1)
    return cols <= rows, lane < MLA_VDIM


def _mla_attn_fwd(q, k, v):
    t = q.shape[0]
    tq = min(256, t)
    scale = MLA_QK ** -0.5

    def body(q_ref, k_ref, v_ref, o_ref, lse_ref):
        causal, low = _attn_masks(tq, t)
        vp = v_ref[...]
        acc = jnp.zeros((tq, LANE), F32)
        for hh in range(2):
            sl = slice(hh * HEAD_PAD, (hh + 1) * HEAD_PAD)
            s = _dot_nt(q_ref[:, sl], k_ref[:, sl]) * scale
            s = jnp.where(causal, s, -jnp.inf)
            m = jnp.max(s, axis=-1, keepdims=True)
            e = jnp.exp(s - m)
            l = jnp.sum(e, axis=-1, keepdims=True)
            o = _dot(e.astype(BF16), vp) / l
            acc = jnp.where(low if hh == 0 else jnp.logical_not(low), o, acc)
            lse_ref[hh] = m + jnp.log(l)
        o_ref[...] = acc

    return pl.pallas_call(
        body, name="mla_attn_fwd",
        grid=(MLA_HEADS // 2, t // tq),
        in_specs=[pl.BlockSpec((tq, 2 * HEAD_PAD), lambda p, i: (i, p)),
                  pl.BlockSpec((t, 2 * HEAD_PAD), lambda p, i: (0, p)),
                  pl.BlockSpec((t, LANE), lambda p, i: (0, p))],
        out_specs=(pl.BlockSpec((tq, LANE), lambda p, i: (i, p)),
                   pl.BlockSpec((2, tq, 1), lambda p, i: (p, i, 0))),
        out_shape=(jax.ShapeDtypeStruct((t, MLA_WIDTH), F32),
                   jax.ShapeDtypeStruct((MLA_HEADS, t, 1), F32)),
        compiler_params=_cparams(("arbitrary", "arbitrary")),
    )(q, k, v)


def _mla_attn_bwd(q, k, v, o, do, lse):
    t = q.shape[0]
    tq = min(256, t)
    scale = MLA_QK ** -0.5

    def body(q_ref, k_ref, v_ref, o_ref, do_ref, lse_ref, dq_ref, dk_ref, dv_ref):
        @pl.when(pl.program_id(1) == 0)
        def _():
            dk_ref[...] = jnp.zeros_like(dk_ref)
            dv_ref[...] = jnp.zeros_like(dv_ref)

        causal, low = _attn_masks(tq, t)
        vp = v_ref[...]
        do_all = do_ref[...]
        o_all = o_ref[...]
        dv_acc = jnp.zeros((t, LANE), F32)
        for hh in range(2):
            sl = slice(hh * HEAD_PAD, (hh + 1) * HEAD_PAD)
            do_h = jnp.where(low if hh == 0 else jnp.logical_not(low), do_all, 0.0)
            dsum = jnp.sum(do_h * o_all, axis=-1, keepdims=True)
            qh = q_ref[:, sl]
            kh = k_ref[:, sl]
            s = _dot_nt(qh, kh) * scale
            p = jnp.where(causal, jnp.exp(s - lse_ref[hh]), 0.0)
            do_b = do_h.astype(BF16)
            dp = _dot_nt(do_b, vp)
            ds = (p * (dp - dsum) * scale).astype(BF16)
            dq_ref[:, sl] = _dot(ds, kh).astype(BF16)
            dk_ref[:, sl] += _dot_tn(ds, qh)
            dv_acc = dv_acc + _dot_tn(p.astype(BF16), do_b)
        dv_ref[...] += dv_acc

    return pl.pallas_call(
        body, name="mla_attn_bwd",
        grid=(MLA_HEADS // 2, t // tq),
        in_specs=[pl.BlockSpec((tq, 2 * HEAD_PAD), lambda p, i: (i, p)),
                  pl.BlockSpec((t, 2 * HEAD_PAD), lambda p, i: (0, p)),
                  pl.BlockSpec((t, LANE), lambda p, i: (0, p)),
                  pl.BlockSpec((tq, LANE), lambda p, i: (i, p)),
                  pl.BlockSpec((tq, LANE), lambda p, i: (i, p)),
                  pl.BlockSpec((2, tq, 1), lambda p, i: (p, i, 0))],
        out_specs=(pl.BlockSpec((tq, 2 * HEAD_PAD), lambda p, i: (i, p)),
                   pl.BlockSpec((t, 2 * HEAD_PAD), lambda p, i: (0, p)),
                   pl.BlockSpec((t, LANE), lambda p, i: (0, p))),
        out_shape=(jax.ShapeDtypeStruct((t, MLA_HEADS * HEAD_PAD), BF16),
                   jax.ShapeDtypeStruct((t, MLA_HEADS * HEAD_PAD), F32),
                   jax.ShapeDtypeStruct((t, MLA_WIDTH), F32)),
        compiler_params=_cparams(("arbitrary", "arbitrary")),
    )(q, k, v, o, do, lse)


def _gla_chunk_terms(q_ref, k_ref, la_ref, h, tri):
    sl = slice(h * GLA_HK, (h + 1) * GLA_HK)
    b = _dot_exact(tri, la_ref[:, sl])
    bl = b[GLA_CHUNK - 1:GLA_CHUNK, :]
    kc = k_ref[:, sl]
    q_in = (q_ref[:, sl] * (GLA_HK ** -0.5)) * jnp.exp(b)
    k_in = kc * jnp.exp(-b)
    k_st = kc * jnp.exp(bl - b)
    return b, bl, q_in, k_in, k_st


def _tri(c, lower):
    r = lax.broadcasted_iota(jnp.int32, (c, c), 0)
    cc = lax.broadcasted_iota(jnp.int32, (c, c), 1)
    return jnp.where(r >= cc if lower else r <= cc, 1.0, 0.0).astype(F32)


def _gla_fwd(proj, log_a):
    t = proj.shape[0]
    c = GLA_CHUNK
    n = t // c

    def body(q_ref, k_ref, v_ref, la_ref, o_ref, sp_ref, st_ref):
        @pl.when(pl.program_id(0) == 0)
        def _():
            st_ref[...] = jnp.zeros_like(st_ref)

        tri = _tri(c, True)
        for h in range(GLA_HEADS):
            _, bl, q_in, k_in, k_st = _gla_chunk_terms(q_ref, k_ref, la_ref, h, tri)
            vs = slice(h * GLA_HV, (h + 1) * GLA_HV)
            vv = v_ref[:, vs].astype(BF16)
            qb = q_in.astype(BF16)
            attn = _dot_nt(qb, k_in.astype(BF16)) * tri
            st = st_ref[h]
            sp_ref[0, h] = st
            o_ref[:, vs] = _dot(attn.astype(BF16), vv) + _dot_nt(qb, st.astype(BF16))
            st_ref[h] = st * jnp.exp(bl) + _dot_tn(vv, k_st.astype(BF16))

    return pl.pallas_call(
        body, name="gla_fwd",
        grid=(n,),
        in_specs=[pl.BlockSpec((c, GLA_DK), lambda i: (i, P_QG // GLA_DK)),
                  pl.BlockSpec((c, GLA_DK), lambda i: (i, P_KG // GLA_DK)),
                  pl.BlockSpec((c, GLA_DV), lambda i: (i, P_VG // GLA_DV)),
                  pl.BlockSpec((c, GLA_DK), lambda i: (i, 0))],
        out_specs=(pl.BlockSpec((c, GLA_DV), lambda i: (i, 0)),
                   pl.BlockSpec((1, GLA_HEADS, GLA_HV, GLA_HK), lambda i: (i, 0, 0, 0))),
        out_shape=(jax.ShapeDtypeStruct((t, GLA_DV), F32),
                   jax.ShapeDtypeStruct((n, GLA_HEADS, GLA_HV, GLA_HK), F32)),
        scratch_shapes=[pltpu.VMEM((GLA_HEADS, GLA_HV, GLA_HK), F32)],
        compiler_params=_cparams(("arbitrary",)),
    )(proj, proj, proj, log_a)


def _gla_bwd(proj, log_a, do, states):
    t = proj.shape[0]
    c = GLA_CHUNK
    n = t // c

    def body(q_ref, k_ref, v_ref, la_ref, do_ref, sp_ref, dq_ref, dk_ref, dv_ref, dla_ref, ds_ref):
        @pl.when(pl.program_id(0) == 0)
        def _():
            ds_ref[...] = jnp.zeros_like(ds_ref)

        tri = _tri(c, True)
        tri_t = _tri(c, False)
        for h in range(GLA_HEADS):
            b, bl, q_in, k_in, k_st = _gla_chunk_terms(q_ref, k_ref, la_ref, h, tri)
            ks_ = slice(h * GLA_HK, (h + 1) * GLA_HK)
            vs = slice(h * GLA_HV, (h + 1) * GLA_HV)
            vv = v_ref[:, vs].astype(BF16)
            do_h = do_ref[:, vs]
            qb, kb, ksb = q_in.astype(BF16), k_in.astype(BF16), k_st.astype(BF16)
            attn = (_dot_nt(qb, kb) * tri).astype(BF16)
            st = sp_ref[0, h]
            dst = ds_ref[h]
            dstb = dst.astype(BF16)
            dattn = (_dot_nt(do_h, vv) * tri).astype(BF16)
            dv_ref[:, vs] = (_dot_tn(attn, do_h) + _dot_nt(ksb, dstb)).astype(BF16)
            dq_in = _dot(dattn, kb) + _dot(do_h, st.astype(BF16))
            dk_in = _dot_tn(dattn, qb)
            dk_st = _dot(vv, dstb)
            ebl = jnp.exp(bl)
            d_ebl = jnp.sum(st * dst, axis=0, keepdims=True)
            ds_ref[h] = _dot_tn(do_h, qb) + dst * ebl
            dq_ref[:, ks_] = (dq_in * (GLA_HK ** -0.5) * jnp.exp(b)).astype(BF16)
            dk_ref[:, ks_] = (dk_in * jnp.exp(-b) + dk_st * jnp.exp(bl - b)).astype(BF16)
            db = dq_in * q_in - dk_in * k_in - dk_st * k_st
            dbl = jnp.sum(dk_st * k_st, axis=0, keepdims=True) + d_ebl * ebl
            dla_ref[:, ks_] = _dot_exact(tri_t, db) + dbl

    rev = lambda i: n - 1 - i
    return pl.pallas_call(
        body, name="gla_bwd",
        grid=(n,),
        in_specs=[pl.BlockSpec((c, GLA_DK), lambda i: (rev(i), P_QG // GLA_DK)),
                  pl.BlockSpec((c, GLA_DK), lambda i: (rev(i), P_KG // GLA_DK)),
                  pl.BlockSpec((c, GLA_DV), lambda i: (rev(i), P_VG // GLA_DV)),
                  pl.BlockSpec((c, GLA_DK), lambda i: (rev(i), 0)),
                  pl.BlockSpec((c, GLA_DV), lambda i: (rev(i), 0)),
                  pl.BlockSpec((1, GLA_HEADS, GLA_HV, GLA_HK), lambda i: (rev(i), 0, 0, 0))],
        out_specs=(pl.BlockSpec((c, GLA_DK), lambda i: (rev(i), 0)),
                   pl.BlockSpec((c, GLA_DK), lambda i: (rev(i), 0)),
                   pl.BlockSpec((c, GLA_DV), lambda i: (rev(i), 0)),
                   pl.BlockSpec((c, GLA_DK), lambda i: (rev(i), 0))),
        out_shape=(jax.ShapeDtypeStruct((t, GLA_DK), BF16), jax.ShapeDtypeStruct((t, GLA_DK), BF16),
                   jax.ShapeDtypeStruct((t, GLA_DV), BF16), jax.ShapeDtypeStruct((t, GLA_DK), F32)),
        scratch_shapes=[pltpu.VMEM((GLA_HEADS, GLA_HV, GLA_HK), F32)],
        compiler_params=_cparams(("arbitrary",)),
    )(proj, proj, proj, log_a, do, states)


def _post(o_mla, proj, o_gla, x, target, g_gla, g_final, w_pm, w_pg, w_o):
    t = x.shape[0]
    tm = min(128, t)

    def body(om_ref, zg_ref, gm_ref, gg_ref, zm_ref, og_ref, x_ref, tg_ref, ggla_ref, gf_ref,
             wpm_ref, wpg_ref, wo_ref,
             dx2_ref, dom_ref, dog_ref, dzm_ref, dzg_ref, dgm_ref, dgg_ref,
             mg_ref, um_ref, ug_ref, dym_ref, dyg_ref, loss_ref, dgf_ref, dggla_ref):
        @pl.when(pl.program_id(0) == 0)
        def _():
            loss_ref[...] = jnp.zeros_like(loss_ref)
            dgf_ref[...] = jnp.zeros_like(dgf_ref)
            dggla_ref[...] = jnp.zeros_like(dggla_ref)

        om = om_ref[...]
        zm = zm_ref[...]
        sm = _sigmoid(zm)
        silu_m = zm * sm
        um = (om * silu_m).astype(BF16)
        um_ref[...] = um
        ym = _dot(um, wpm_ref[...])

        ggla = ggla_ref[...]
        zg = zg_ref[...]
        sg = _sigmoid(zg)
        silu_g = zg * sg
        xhat, rstd, on = [], [], []
        for h in range(GLA_HEADS):
            blk = og_ref[:, h * GLA_HV:(h + 1) * GLA_HV]
            r = lax.rsqrt(jnp.mean(blk * blk, axis=-1, keepdims=True) + EPS)
            xhat.append(blk * r)
            rstd.append(r)
            on.append(xhat[h] * ggla)
        on = jnp.concatenate(on, axis=-1)
        ug = (on * silu_g).astype(BF16)
        ug_ref[...] = ug
        yg = _dot(ug, wpg_ref[...])

        sgm = _sigmoid(gm_ref[...])
        sgg = _sigmoid(gg_ref[...])
        merged = (sgm * ym + sgg * yg).astype(BF16)
        mg_ref[...] = merged
        x2 = x_ref[...] + _dot(merged, wo_ref[...])
        gf = gf_ref[...]
        rf = lax.rsqrt(jnp.mean(x2 * x2, axis=-1, keepdims=True) + EPS)
        xh = x2 * rf
        err = xh * gf - tg_ref[...]
        loss_ref[...] += 0.5 * jnp.sum(jnp.mean(err * err, axis=-1, keepdims=True))

        dy = err * (1.0 / D_MODEL)
        dgf_ref[...] += jnp.sum(dy * xh, axis=0, keepdims=True)
        dxh = dy * gf
        dx2 = rf * (dxh - xh * jnp.mean(dxh * xh, axis=-1, keepdims=True))
        dx2_ref[...] = dx2
        dmerged = _dot_nt(dx2.astype(BF16), wo_ref[...])
        dym = (dmerged * sgm).astype(BF16)
        dyg = (dmerged * sgg).astype(BF16)
        dym_ref[...] = dym
        dyg_ref[...] = dyg
        dgm_ref[...] = (dmerged * ym * sgm * (1.0 - sgm)).astype(BF16)
        dgg_ref[...] = (dmerged * yg * sgg * (1.0 - sgg)).astype(BF16)
        dum = _dot_nt(dym, wpm_ref[...])
        dom_ref[...] = dum * silu_m
        dzm_ref[...] = (dum * om * (sm * (1.0 + zm * (1.0 - sm)))).astype(BF16)
        dug = _dot_nt(dyg, wpg_ref[...])
        dzg_ref[...] = (dug * on * (sg * (1.0 + zg * (1.0 - sg)))).astype(BF16)
        don = dug * silu_g
        dggla = jnp.zeros((1, GLA_HV), F32)
        for h in range(GLA_HEADS):
            hs = slice(h * GLA_HV, (h + 1) * GLA_HV)
            don_h = don[:, hs]
            dggla = dggla + jnp.sum(don_h * xhat[h], axis=0, keepdims=True)
            dxh_h = don_h * ggla
            dog_ref[:, hs] = (rstd[h] * (dxh_h - xhat[h] * jnp.mean(dxh_h * xhat[h], axis=-1,
                                                                     keepdims=True))).astype(BF16)
        dggla_ref[...] += dggla

    row = lambda w: pl.BlockSpec((tm, w), lambda i: (i, 0))
    pcol = lambda w, off: pl.BlockSpec((tm, w), lambda i: (i, off // w))
    full = lambda a: pl.BlockSpec(a.shape, lambda i: (0, 0))
    sds = jax.ShapeDtypeStruct
    return pl.pallas_call(
        body, name="post_fwd_bwd",
        grid=(t // tm,),
        in_specs=[row(MLA_WIDTH), pcol(GLA_DV, P_ZGLA), pcol(D_MODEL, P_GMLA), pcol(D_MODEL, P_GGLA),
                  pcol(MLA_WIDTH, P_ZMLA), row(GLA_DV), row(D_MODEL), row(D_MODEL),
                  full(g_gla), full(g_final), full(w_pm), full(w_pg), full(w_o)],
        out_specs=(row(D_MODEL), row(MLA_WIDTH), row(GLA_DV), row(MLA_WIDTH), row(GLA_DV),
                   row(D_MODEL), row(D_MODEL), row(D_MODEL), row(MLA_WIDTH), row(GLA_DV),
                   row(D_MODEL), row(D_MODEL),
                   pl.BlockSpec((1, LANE), lambda i: (0, 0)),
                   pl.BlockSpec((1, D_MODEL), lambda i: (0, 0)),
                   pl.BlockSpec((1, GLA_HV), lambda i: (0, 0))),
        out_shape=(sds((t, D_MODEL), F32), sds((t, MLA_WIDTH), F32), sds((t, GLA_DV), BF16),
                   sds((t, MLA_WIDTH), BF16), sds((t, GLA_DV), BF16), sds((t, D_MODEL), BF16),
                   sds((t, D_MODEL), BF16), sds((t, D_MODEL), BF16), sds((t, MLA_WIDTH), BF16),
                   sds((t, GLA_DV), BF16), sds((t, D_MODEL), BF16), sds((t, D_MODEL), BF16),
                   sds((1, LANE), F32), sds((1, D_MODEL), F32), sds((1, GLA_HV), F32)),
        compiler_params=_cparams(("arbitrary",)),
    )(o_mla, proj, proj, proj, proj, o_gla, x, target, g_gla, g_final, w_pm, w_pg, w_o)


def _mla_prep_bwd(dq, dk, dv, dla, pre, proj, rq, rkv, g_q, g_kv, w_uq_p, w_k_p, w_v, w_gate_p,
                  rc, rsn, rsp):
    t = proj.shape[0]
    tm = min(256, t)

    def body(dq_ref, dk_ref, dv_ref, dla_ref, pre_ref, cq_ref, ckv_ref, rq_ref, rkv_ref,
             gq_ref, gkv_ref, wuq_ref, wk_ref, wv_ref, wg_ref, c_ref, sn_ref, sp_ref,
             dcq_ref, dckv_ref, dmisc_ref, dqpre_ref, dpre_ref, dgq_ref, dgkv_ref, dbg_ref):
        @pl.when(pl.program_id(0) == 0)
        def _():
            dgq_ref[...] = jnp.zeros_like(dgq_ref)
            dgkv_ref[...] = jnp.zeros_like(dgkv_ref)
            dbg_ref[...] = jnp.zeros_like(dbg_ref)

        c, sn, sp = c_ref[...], sn_ref[...], sp_ref[...]
        dkr = jnp.zeros((tm, LANE), F32)
        for h in range(MLA_HEADS):
            sl = slice(h * HEAD_PAD, (h + 1) * HEAD_PAD)
            dqpre_ref[:, sl] = _rope_bwd(dq_ref[:, sl].astype(F32), c, sn, sp).astype(BF16)
            dkr = dkr + dk_ref[:, sl]
        dcqn = _dot_nt(dqpre_ref[...], wuq_ref[...])
        rq = rq_ref[...]
        xh = cq_ref[:, :MLA_Q_RANK] * rq
        dgq_ref[...] += jnp.sum(dcqn * xh, axis=0, keepdims=True)
        dxh = dcqn * gq_ref[...]
        dcq = rq * (dxh - xh * jnp.mean(dxh * xh, axis=-1, keepdims=True))
        dcq_ref[...] = jnp.concatenate(
            [dcq, jnp.zeros((tm, 512 - MLA_Q_RANK), F32)], axis=-1).astype(BF16)

        dckvn = _dot_nt(dk_ref[...].astype(BF16), wk_ref[...]) + \
            _dot_nt(dv_ref[...].astype(BF16), wv_ref[...])
        rkv = rkv_ref[...]
        xh = ckv_ref[...] * rkv
        dgkv_ref[...] += jnp.sum(dckvn * xh, axis=0, keepdims=True)
        dxh = dckvn * gkv_ref[...]
        dckv_ref[...] = (rkv * (dxh - xh * jnp.mean(dxh * xh, axis=-1, keepdims=True))).astype(BF16)

        dpre = dla_ref[...] * (1.0 / GLA_GATE_NORM) * (1.0 - _sigmoid(pre_ref[...]))
        dbg_ref[...] += jnp.sum(dpre, axis=0, keepdims=True)
        dpre = dpre.astype(BF16)
        dpre_ref[...] = dpre
        lane = lax.broadcasted_iota(jnp.int32, (tm, LANE), 1)
        in_kr = jnp.logical_and(lane >= MISC_KR, lane < MISC_KR + MLA_ROPE)
        dmisc = jnp.where(in_kr, _rope_bwd(dkr, c, sn, sp), 0.0) + _dot_nt(dpre, wg_ref[...])
        dmisc_ref[...] = dmisc.astype(BF16)

    hq = MLA_HEADS * HEAD_PAD
    row = lambda w: pl.BlockSpec((tm, w), lambda i: (i, 0))
    full = lambda a: pl.BlockSpec(a.shape, lambda i: (0, 0))
    acc = lambda w: pl.BlockSpec((1, w), lambda i: (0, 0))
    sds = jax.ShapeDtypeStruct
    return pl.pallas_call(
        body, name="mla_prep_bwd",
        grid=(t // tm,),
        in_specs=[row(hq), row(hq), row(MLA_WIDTH), row(GLA_DK), row(GLA_DK),
                  pl.BlockSpec((tm, 512), lambda i: (i, P_CQ // 512)),
                  pl.BlockSpec((tm, MLA_KV_RANK), lambda i: (i, P_CKV // MLA_KV_RANK)),
                  row(1), row(1), full(g_q), full(g_kv), full(w_uq_p), full(w_k_p), full(w_v),
                  full(w_gate_p), row(LANE), row(LANE), row(LANE)],
        out_specs=(row(512), row(MLA_KV_RANK), row(LANE), row(hq), row(GLA_DK),
                   acc(MLA_Q_RANK), acc(MLA_KV_RANK), acc(GLA_DK)),
        out_shape=(sds((t, 512), BF16), sds((t, MLA_KV_RANK), BF16), sds((t, LANE), BF16),
                   sds((t, hq), BF16), sds((t, GLA_DK), BF16),
                   sds((1, MLA_Q_RANK), F32), sds((1, MLA_KV_RANK), F32), sds((1, GLA_DK), F32)),
        compiler_params=_cparams(("arbitrary",)),
    )(dq, dk, dv, dla, pre, proj, proj, rq, rkv, g_q, g_kv, w_uq_p, w_k_p, w_v, w_gate_p,
      rc, rsn, rsp)


def _inproj_bwd(dproj, w_p, x, rstd, g_in, dx2):
    t = x.shape[0]
    tm = min(256, t)
    nk = 3
    tk = P_TOTAL // nk

    def body(dp_ref, w_ref, x_ref, r_ref, g_ref, dx2_ref, dx_ref, dg_ref, acc_ref):
        k = pl.program_id(0)
        i = pl.program_id(1)
        rows = pl.ds(pl.multiple_of(i * tm, tm), tm)
        part = _dot_nt(dp_ref[...], w_ref[...])

        @pl.when(k == 0)
        def _():
            acc_ref[rows, :] = part

        @pl.when(jnp.logical_and(k > 0, k < nk - 1))
        def _():
            acc_ref[rows, :] += part

        @pl.when(k == nk - 1)
        def _():
            @pl.when(i == 0)
            def _():
                dg_ref[...] = jnp.zeros_like(dg_ref)

            dh = acc_ref[rows, :] + part
            r = r_ref[...]
            xh = x_ref[...] * r
            dg_ref[...] += jnp.sum(dh * xh, axis=0, keepdims=True)
            dxh = dh * g_ref[...]
            dx_ref[...] = dx2_ref[...] + r * (dxh - xh * jnp.mean(dxh * xh, axis=-1, keepdims=True))

    last = lambda k, i: (jnp.where(k == nk - 1, i, 0), 0)
    return pl.pallas_call(
        body, name="inproj_bwd",
        grid=(nk, t // tm),
        in_specs=[pl.BlockSpec((tm, tk), lambda k, i: (i, k)),
                  pl.BlockSpec((D_MODEL, tk), lambda k, i: (0, k)),
                  pl.BlockSpec((tm, D_MODEL), last),
                  pl.BlockSpec((tm, 1), last),
                  pl.BlockSpec((1, D_MODEL), lambda k, i: (0, 0)),
                  pl.BlockSpec((tm, D_MODEL), last)],
        out_specs=(pl.BlockSpec((tm, D_MODEL), last),
                   pl.BlockSpec((1, D_MODEL), lambda k, i: (0, 0))),
        out_shape=(jax.ShapeDtypeStruct((t, D_MODEL), F32),
                   jax.ShapeDtypeStruct((1, D_MODEL), F32)),
        scratch_shapes=[pltpu.VMEM((t, D_MODEL), F32)],
        compiler_params=_cparams(("arbitrary", "arbitrary")),
    )(dproj, w_p, x, rstd, g_in, dx2)


def _matmul(name, a, b, tm, tn):
    m, kk = a.shape
    n = b.shape[1]

    def body(a_ref, b_ref, o_ref):
        o_ref[...] = _dot(a_ref[...].astype(BF16), b_ref[...].astype(BF16))

    return pl.pallas_call(
        body, name=name,
        grid=(n // tn, m // tm),
        in_specs=[pl.BlockSpec((tm, kk), lambda j, i: (i, 0)),
                  pl.BlockSpec((kk, tn), lambda j, i: (0, j))],
        out_specs=pl.BlockSpec((tm, tn), lambda j, i: (i, j)),
        out_shape=jax.ShapeDtypeStruct((m, n), F32),
        compiler_params=_cparams(("arbitrary", "arbitrary")),
    )(a, b)


def _adamw(parts, w, m, v, tr):
    rows, lanes = w.shape

    def body(p_ref, w_ref, m_ref, v_ref, g_ref, d_ref, nm_ref, nv_ref):
        g = p_ref[0].astype(F32)
        for q in range(1, N_DEV):
            g = g + p_ref[q].astype(F32)
        m_new = ADAM_B1 * m_ref[...] + (1.0 - ADAM_B1) * g
        v_new = ADAM_B2 * v_ref[...] + (1.0 - ADAM_B2) * (g * g)
        m_hat = m_new / (1.0 - ADAM_B1 ** ADAM_STEP)
        v_hat = v_new / (1.0 - ADAM_B2 ** ADAM_STEP)
        g_ref[...] = g
        nm_ref[...] = m_new
        nv_ref[...] = v_new
        d_ref[...] = -ADAM_LR * (m_hat / (jnp.sqrt(v_hat) + ADAM_EPS) + ADAM_WD * w_ref[...])

    blk = pl.BlockSpec((tr, lanes), lambda i: (i, 0))
    out = jax.ShapeDtypeStruct((rows, lanes), F32)
    return pl.pallas_call(
        body, name="adamw",
        grid=(rows // tr,),
        in_specs=[pl.BlockSpec((N_DEV, tr, lanes), lambda i: (0, i, 0)), blk, blk, blk],
        out_specs=(blk, blk, blk, blk),
        out_shape=(out, out, out, out),
        compiler_params=_cparams(("arbitrary",)),
    )(parts, w, m, v)


def _pack_shards(ws):
    flat = jnp.concatenate([w.reshape(-1, LANE) for w in ws], axis=0)
    return jnp.pad(flat, ((0, PACK_PAD - PACK_TOTAL), (0, 0)))


def _unpack_shards(packed):
    w_in, w_uq, w_ukv, w_gate, w_pm, w_pg, w_o = jnp.split(
        packed[:PACK_TOTAL], list(np.cumsum(PACK_ROWS)[:-1]), axis=0)
    return (w_in.reshape(1, D_MODEL, D_IN // N_DEV), w_uq.reshape(1, MLA_Q_RANK, MLA_QK),
            w_ukv.reshape(1, MLA_KV_RANK, MLA_NOPE + MLA_VDIM),
            w_gate.reshape(1, GLA_GATE_RANK, GLA_DK // N_DEV),
            w_pm.reshape(1, MLA_WIDTH, D_MODEL // N_DEV), w_pg.reshape(1, GLA_DV // N_DEV, D_MODEL),
            w_o.reshape(1, D_MODEL // N_DEV, D_MODEL))


def _cols_from_gathered(g, k, n):
    return g.reshape(N_DEV, k, n // N_DEV).transpose(1, 0, 2).reshape(k, n)


def _cols_to_parts(w):
    k, n = w.shape
    return w.reshape(k, N_DEV, n // N_DEV).transpose(1, 0, 2).reshape(N_DEV, -1, LANE)


def _win_to_p(w):
    c_q, c_kv, k_r, z_mla, q_g, k_g, v_g, a_lr, z_gla, g_mla, g_gla = jnp.split(
        w, [384, 640, 672, 1184, 1696, 2208, 3232, 3248, 4272, 5296], axis=1)
    z = lambda n: jnp.zeros((w.shape[0], n), w.dtype)
    return jnp.concatenate([z_gla, g_mla, g_gla, v_g, z_mla, q_g, k_g, c_q, z(128), c_kv,
                            z(MISC_KR), k_r, a_lr, z(LANE - MISC_ALR - GLA_GATE_RANK)], axis=1)


def _p_to_win(g):
    s = lambda off, n: g[:, off:off + n]
    return jnp.concatenate(
        [s(P_CQ, 384), s(P_CKV, 256), s(P_MISC + MISC_KR, MLA_ROPE), s(P_ZMLA, 512), s(P_QG, 512),
         s(P_KG, 512), s(P_VG, 1024), s(P_MISC + MISC_ALR, GLA_GATE_RANK), s(P_ZGLA, 1024),
         s(P_GMLA, 1024), s(P_GGLA, 1024)], axis=1)


def _rope_tables(positions):
    half = MLA_ROPE // 2
    freqs = ROPE_THETA ** (-jnp.arange(half, dtype=F32) / half)
    ang = positions.astype(F32).reshape(-1, 1) * freqs
    cos, sin = jnp.cos(ang), jnp.sin(ang)
    t = ang.shape[0]
    one, zero = jnp.ones((t, MLA_NOPE), F32), jnp.zeros((t, half), F32)
    tail = jnp.zeros((t, LANE - MLA_QK), F32)
    rc = jnp.concatenate([one, cos, cos, tail], axis=1)
    rsn = jnp.concatenate([0.0 * one, -sin, zero, tail], axis=1)
    rsp = jnp.concatenate([0.0 * one, zero, sin, tail], axis=1)
    return rc, rsn, rsp


def kernel(x, positions, g_in, w_in, g_q, w_uq, g_kv, w_ukv, w_gla_gate, b_gla_gate, g_gla, w_proj_mla, w_proj_gla, w_out, g_final, loss_target, m_g_in, m_w_in, m_g_q, m_w_uq, m_g_kv, m_w_ukv, m_w_gla_gate, m_b_gla_gate, m_g_gla, m_w_proj_mla, m_w_proj_gla, m_w_out, m_g_final, v_g_in, v_w_in, v_g_q, v_w_uq, v_g_kv, v_w_ukv, v_w_gla_gate, v_b_gla_gate, v_g_gla, v_w_proj_mla, v_w_proj_gla, v_w_out, v_g_final):
    t = x.shape[1]
    x2d = x.reshape(t, D_MODEL)
    tgt = loss_target.reshape(t, D_MODEL)
    g_final2 = g_final.reshape(1, D_MODEL)

    w_pack = _pack_shards([w_in, w_uq, w_ukv, w_gla_gate, w_proj_mla, w_proj_gla, w_out])
    gathered = _all_gather(w_pack.astype(BF16))
    o0, o1, o2, o3, o4, o5, o6 = [int(v) for v in np.cumsum((0,) + PACK_ROWS[:-1])]
    w_in_p = _win_to_p(_cols_from_gathered(gathered[:, o0:o0 + 6320], D_MODEL, D_IN))
    w_uq_f = _cols_from_gathered(gathered[:, o1:o1 + 288], MLA_Q_RANK, MLA_HEADS * MLA_QK)
    w_uq_p = jnp.pad(w_uq_f.reshape(MLA_Q_RANK, MLA_HEADS, MLA_QK),
                     ((0, 0), (0, 0), (0, HEAD_PAD - MLA_QK))).reshape(MLA_Q_RANK, -1)
    w_ukv_f = _cols_from_gathered(gathered[:, o2:o2 + 256], MLA_KV_RANK, 1024).reshape(
        MLA_KV_RANK, MLA_HEADS, MLA_NOPE + MLA_VDIM)
    w_k_p = jnp.pad(w_ukv_f[:, :, :MLA_NOPE],
                    ((0, 0), (0, 0), (0, HEAD_PAD - MLA_NOPE))).reshape(MLA_KV_RANK, -1)
    w_v = w_ukv_f[:, :, MLA_NOPE:].reshape(MLA_KV_RANK, MLA_WIDTH)
    w_gate_f = _cols_from_gathered(gathered[:, o3:o3 + 8], GLA_GATE_RANK, GLA_DK)
    w_gate_p = jnp.pad(w_gate_f, ((MISC_ALR, LANE - MISC_ALR - GLA_GATE_RANK), (0, 0)))
    w_pm = _cols_from_gathered(gathered[:, o4:o4 + 512], MLA_WIDTH, D_MODEL)
    w_pg = gathered[:, o5:o5 + 1024].reshape(GLA_DV, D_MODEL)
    w_o = gathered[:, o6:o6 + 1024].reshape(D_MODEL, D_MODEL)
    rc, rsn, rsp = _rope_tables(positions)

    proj, h, rstd = _inproj(x2d, g_in, w_in_p)
    q, k, v, log_a, pre, cqn, ckvn, rq, rkv = _mla_prep(
        proj, g_q, g_kv, w_uq_p, w_k_p, w_v, w_gate_p, b_gla_gate, rc, rsn, rsp)
    o_mla, lse = _mla_attn_fwd(q, k, v)
    o_gla, states = _gla_fwd(proj, log_a)

    (dx2, do_mla, do_gla, dzm, dzg, dgm, dgg, merged, um, ug, dym, dyg, loss_p, dg_final,
     dg_gla) = _post(o_mla, proj, o_gla, x2d, tgt, g_gla, g_final2, w_pm, w_pg, w_o)

    dq, dk, dv = _mla_attn_bwd(q, k, v, o_mla, do_mla, lse)
    dqg, dkg, dvg, dla = _gla_bwd(proj, log_a, do_gla, states)
    dcq, dckv, dmisc, dqpre, dpre, dg_q, dg_kv, db_gate = _mla_prep_bwd(
        dq, dk, dv, dla, pre, proj, rq, rkv, g_q, g_kv, w_uq_p, w_k_p, w_v, w_gate_p, rc, rsn, rsp)

    dproj = jnp.concatenate([dzg, dgm, dgg, dvg, dzm, dqg, dkg, dcq, dckv, dmisc], axis=1)
    grad_x, dg_in = _inproj_bwd(dproj, w_in_p, x2d, rstd, g_in, dx2)

    dw_in = _p_to_win(_matmul("dw_in", h.T, dproj, 512, P_TOTAL // 3))
    dw_uq = _matmul("dw_uq", cqn.T, dqpre, MLA_Q_RANK, 512).reshape(MLA_Q_RANK, MLA_HEADS, HEAD_PAD)[:, :, :MLA_QK]
    dw_k = _matmul("dw_uk", ckvn.T, dk, MLA_KV_RANK, 512).reshape(MLA_KV_RANK, MLA_HEADS, HEAD_PAD)[:, :, :MLA_NOPE]
    dw_v = _matmul("dw_uv", ckvn.T, dv, MLA_KV_RANK, 512).reshape(MLA_KV_RANK, MLA_HEADS, MLA_VDIM)
    dw_ukv = jnp.concatenate([dw_k, dw_v], axis=2).reshape(MLA_KV_RANK, -1)
    misc_t = proj[:, P_MISC:P_MISC + LANE].T
    dw_gate = _matmul("dw_gate", misc_t, dpre, LANE, 512)[MISC_ALR:MISC_ALR + GLA_GATE_RANK]
    dw_pm = _matmul("dw_proj_mla", um.T, dym, 512, 512)
    dw_pg = _matmul("dw_proj_gla", ug.T, dyg, 512, 512)
    dw_o = _matmul("dw_out", merged.T, dx2, 512, 512)

    parts = jnp.concatenate(
        [_cols_to_parts(dw_in), _cols_to_parts(dw_uq.reshape(MLA_Q_RANK, -1)),
         _cols_to_parts(dw_ukv), _cols_to_parts(dw_gate), _cols_to_parts(dw_pm),
         dw_pg.reshape(N_DEV, -1, LANE), dw_o.reshape(N_DEV, -1, LANE),
         jnp.zeros((N_DEV, PACK_PAD - PACK_TOTAL, LANE), F32)], axis=1).astype(BF16)
    small = jnp.concatenate([dg_in.reshape(-1), dg_q.reshape(-1), dg_kv.reshape(-1),
                             db_gate.reshape(-1), dg_gla.reshape(-1), dg_final.reshape(-1),
                             loss_p[0, :1]])
    small = jnp.pad(small, (0, SMALL_ROWS * LANE - small.shape[0])).reshape(SMALL_ROWS, LANE)

    recv, small_recv = _exchange_grads(parts, small)
    sharded = [(w_in, m_w_in, v_w_in), (w_uq, m_w_uq, v_w_uq), (w_ukv, m_w_ukv, v_w_ukv),
               (w_gla_gate, m_w_gla_gate, v_w_gla_gate), (w_proj_mla, m_w_proj_mla, v_w_proj_mla),
               (w_proj_gla, m_w_proj_gla, v_w_proj_gla), (w_out, m_w_out, v_w_out)]
    packs = [_pack_shards([s[j] for s in sharded]) for j in range(3)]
    big = _adamw(recv, packs[0], packs[1], packs[2], PACK_PAD // 5)
    replicated = [(g_in, m_g_in, v_g_in), (g_q, m_g_q, v_g_q), (g_kv, m_g_kv, v_g_kv),
                  (b_gla_gate, m_b_gla_gate, v_b_gla_gate), (g_gla, m_g_gla, v_g_gla),
                  (g_final, m_g_final, v_g_final)]
    spacks = [jnp.pad(jnp.concatenate([s[j].reshape(-1) for s in replicated]),
                      (0, SMALL_ROWS * LANE - sum(SMALL_SIZES))).reshape(SMALL_ROWS, LANE)
              for j in range(3)]
    tiny = _adamw(small_recv, spacks[0], spacks[1], spacks[2], SMALL_ROWS)

    outs = {}
    for kind, bp, sp in zip(("grad", "delta", "new_m", "new_v"), big, tiny):
        (outs[kind, "w_in"], outs[kind, "w_uq"], outs[kind, "w_ukv"], outs[kind, "w_gla_gate"],
         outs[kind, "w_proj_mla"], outs[kind, "w_proj_gla"], outs[kind, "w_out"]) = _unpack_shards(bp)
        flat = sp.reshape(-1)
        off = 0
        for name, size in zip(("g_in", "g_q", "g_kv", "b_gla_gate", "g_gla", "g_final"), SMALL_SIZES):
            shape = (size,) if name == "g_final" else (1, size)
            outs[kind, name] = flat[off:off + size].reshape(shape)
            off += size
    loss = tiny[0].reshape(-1)[sum(SMALL_SIZES)]
    order = ("g_in", "w_in", "g_q", "w_uq", "g_kv", "w_ukv", "w_gla_gate", "b_gla_gate", "g_gla",
             "w_proj_mla", "w_proj_gla", "w_out", "g_final")
    result = [loss, grad_x.reshape(1, t, D_MODEL)]
    for kind in ("grad", "delta", "new_m", "new_v"):
        result += [outs[kind, name] for name in order]
    return tuple(result)
```

```python
import jax
import jax.numpy as jnp
from jax import lax
from jax.experimental import pallas as pl
from jax.experimental.pallas import tpu as pltpu

F32 = jnp.float32
BF16 = jnp.bfloat16
MESH = pl.DeviceIdType.MESH
N_DEV = 8

D_MODEL = 1024
EPS = 1e-6
MLA_HEADS = 8
MLA_NOPE = 64
MLA_ROPE = 32
MLA_VDIM = 64
MLA_Q_RANK = 384
MLA_KV_RANK = 256
MLA_QK = MLA_NOPE + MLA_ROPE
MLA_WIDTH = MLA_HEADS * MLA_VDIM
ROPE_THETA = 10000.0
GLA_HEADS = 4
GLA_DK = 512
GLA_DV = 1024
GLA_HK = 128
GLA_HV = 256
GLA_GATE_RANK = 16
GLA_GATE_NORM = 16.0
GLA_CHUNK = 64
D_IN = 6320

ADAM_LR = 0.001
ADAM_B1 = 0.9
ADAM_B2 = 0.999
ADAM_EPS = 1e-08
ADAM_WD = 0.01
ADAM_STEP = 10

LANE = 128
HEAD_PAD = 128
VMEM_LIMIT = 48 * 1024 * 1024

P_VG, P_QG, P_KG = 0, 1024, 1536
P_ZGLA, P_GMLA, P_GGLA, P_ZMLA = 2048, 3072, 4096, 5120
P_CQ, P_CKV, P_MISC = 5632, 6144, 6400
P_TOTAL = 6528
P_GROUPS = ((0, 2048), (2048, 3584), (5632, 896))
MISC_KR = 64
MISC_ALR = 96
SHARD_COLS = D_IN // N_DEV
P_COMPONENTS = ((0, 384, P_CQ), (384, 256, P_CKV), (640, 32, P_MISC + MISC_KR), (672, 512, P_ZMLA),
                (1184, 512, P_QG), (1696, 512, P_KG), (2208, 1024, P_VG),
                (3232, 16, P_MISC + MISC_ALR), (3248, 1024, P_ZGLA), (4272, 1024, P_GMLA),
                (5296, 1024, P_GGLA))

SMALL_SIZES = (1024, 384, 256, 512, 256, 1024)
SMALL_ROWS = 32


def _segments():
    segs = []
    for g0, n, p0 in P_COMPONENTS:
        g = g0
        while g < g0 + n:
            d = g // SHARD_COLS
            end = min(g0 + n, (d + 1) * SHARD_COLS)
            segs.append((d, g - d * SHARD_COLS, end - g, p0 + g - g0))
            g = end
    return segs


def _cparams(sem=None):
    if sem is None:
        return pltpu.CompilerParams(vmem_limit_bytes=VMEM_LIMIT)
    return pltpu.CompilerParams(dimension_semantics=sem, vmem_limit_bytes=VMEM_LIMIT)


def _sigmoid(v):
    return 1.0 / (1.0 + jnp.exp(-v))


def _dot(a, b):
    return jnp.dot(a, b, preferred_element_type=F32)


def _dot_nt(a, b):
    return lax.dot_general(a, b, (((1,), (1,)), ((), ())), preferred_element_type=F32)


def _dot_tn(a, b):
    return lax.dot_general(a, b, (((0,), (0,)), ((), ())), preferred_element_type=F32)


def _dot_exact(a, b):
    return jnp.dot(a, b, preferred_element_type=F32, precision=lax.Precision.HIGHEST)


def _rope_fwd(blk, c, sn, sp):
    return blk * c + pltpu.roll(blk, LANE - 16, 1) * sn + pltpu.roll(blk, 16, 1) * sp


def _rope_bwd(blk, c, sn, sp):
    return blk * c + pltpu.roll(blk * sn, 16, 1) + pltpu.roll(blk * sp, LANE - 16, 1)


def _mesh_pos():
    return lax.axis_index("x"), lax.axis_index("y"), lax.axis_index("c")


def _hbm_specs(n):
    return [pl.BlockSpec(memory_space=pltpu.HBM) for _ in range(n)]


def _all_gather(shards):
    n = len(shards)

    def body(*refs):
        x_refs, out_refs = refs[:n], refs[n:2 * n]
        send_sems, recv_sems, local_sems = refs[2 * n:]
        x, y, c = _mesh_pos()
        me, sibling = (x, y, c), (x, y, 1 - c)
        chips = [(1 - x, y), (x, 1 - y), (1 - x, 1 - y)]

        def slot(a, px, py, pc):
            return out_refs[a].at[4 * px + 2 * py + pc]

        def copies(k, block, to, own=False):
            return [pltpu.make_async_remote_copy(
                src_ref=x_refs[a] if own else slot(a, *block), dst_ref=slot(a, *block),
                send_sem=send_sems.at[k, a], recv_sem=recv_sems.at[k, a],
                device_id=to, device_id_type=MESH) for a in range(n)]

        mine = [pltpu.make_async_copy(x_refs[a], slot(a, *me), local_sems.at[a]) for a in range(n)]
        for cp in mine:
            cp.start()
        first = copies(0, me, sibling, own=True)
        for j, chip in enumerate(chips):
            first += copies(1 + j, me, (*chip, c), own=True)
        for cp in first:
            cp.start()
        passed = []
        for j, chip in enumerate(chips):
            for cp in copies(1 + j, (*chip, c), me):
                cp.wait_recv()
            fwd = copies(4 + j, (*chip, c), sibling)
            for cp in fwd:
                cp.start()
            passed += fwd
        for cp in copies(0, sibling, me):
            cp.wait_recv()
        for j, chip in enumerate(chips):
            for cp in copies(4 + j, (*chip, 1 - c), me):
                cp.wait_recv()
        for cp in first + passed:
            cp.wait_send()
        for cp in mine:
            cp.wait()

    return pl.pallas_call(
        body, name="all_gather_weights",
        out_shape=tuple(jax.ShapeDtypeStruct((N_DEV,) + s.shape, s.dtype) for s in shards),
        in_specs=_hbm_specs(n), out_specs=tuple(_hbm_specs(n)),
        scratch_shapes=[pltpu.SemaphoreType.DMA((7, n)), pltpu.SemaphoreType.DMA((7, n)),
                        pltpu.SemaphoreType.DMA((n,))],
    )(*shards)


def _exchange_grads(parts, small):
    n = len(parts)

    def body(*refs):
        p_refs, s_ref = refs[:n], refs[n]
        out_refs, sout_ref = refs[n + 1:2 * n + 1], refs[2 * n + 1]
        send_sems, recv_sems, local_sems = refs[2 * n + 2:]
        x, y, c = _mesh_pos()
        me = 4 * x + 2 * y + c
        own = [pltpu.make_async_copy(p_refs[a].at[me], out_refs[a].at[me], local_sems.at[a])
               for a in range(n)]
        own.append(pltpu.make_async_copy(s_ref, sout_ref.at[me], local_sems.at[n]))
        for cp in own:
            cp.start()
        sends = []
        for k in range(1, N_DEV):
            px, py, pc = x ^ (k >> 2), y ^ ((k >> 1) & 1), c ^ (k & 1)
            peer = 4 * px + 2 * py + pc
            sends.append(pltpu.make_async_remote_copy(
                src_ref=s_ref, dst_ref=sout_ref.at[me],
                send_sem=send_sems.at[k - 1, n], recv_sem=recv_sems.at[k - 1, n],
                device_id=(px, py, pc), device_id_type=MESH))
            for a in range(n):
                sends.append(pltpu.make_async_remote_copy(
                    src_ref=p_refs[a].at[peer], dst_ref=out_refs[a].at[me],
                    send_sem=send_sems.at[k - 1, a], recv_sem=recv_sems.at[k - 1, a],
                    device_id=(px, py, pc), device_id_type=MESH))
        for cp in sends:
            cp.start()
        for cp in sends:
            cp.wait_recv()
        for cp in sends:
            cp.wait_send()
        for cp in own:
            cp.wait()

    outs = [jax.ShapeDtypeStruct(p.shape, p.dtype) for p in parts]
    outs.append(jax.ShapeDtypeStruct((N_DEV,) + small.shape, small.dtype))
    return pl.pallas_call(
        body, name="exchange_grads",
        out_shape=tuple(outs),
        in_specs=_hbm_specs(n + 1), out_specs=tuple(_hbm_specs(n + 1)),
        scratch_shapes=[pltpu.SemaphoreType.DMA((7, n + 1)), pltpu.SemaphoreType.DMA((7, n + 1)),
                        pltpu.SemaphoreType.DMA((n + 1,))],
    )(*parts, small)


def _weights_to_p(gathered):
    tm = 64
    segs = sorted(_segments(), key=lambda s: s[3])

    def body(g_ref, o_ref):
        pieces, pos = [], 0
        for d, c0, n, p0 in segs:
            if p0 > pos:
                pieces.append(jnp.zeros((tm, p0 - pos), F32))
            pieces.append(g_ref[d, :, c0:c0 + n].astype(F32))
            pos = p0 + n
        pieces.append(jnp.zeros((tm, P_TOTAL - pos), F32))
        o_ref[...] = jnp.concatenate(pieces, axis=-1).astype(BF16)

    return pl.pallas_call(
        body, name="weights_to_layout",
        grid=(D_MODEL // tm,),
        in_specs=[pl.BlockSpec((N_DEV, tm, SHARD_COLS), lambda i: (0, i, 0))],
        out_specs=pl.BlockSpec((tm, P_TOTAL), lambda i: (i, 0)),
        out_shape=jax.ShapeDtypeStruct((D_MODEL, P_TOTAL), BF16),
        compiler_params=_cparams(("arbitrary",)),
    )(gathered)


def _grads_to_shards(groups):
    tm = 64
    segs = _segments()

    def body(*refs):
        g_refs, o_ref = refs[:3], refs[3]
        for d in range(N_DEV):
            pieces = []
            for _, c0, n, p0 in sorted([s for s in segs if s[0] == d], key=lambda s: s[1]):
                gi = max(i for i, (off, _) in enumerate(P_GROUPS) if off <= p0)
                lo = p0 - P_GROUPS[gi][0]
                pieces.append(g_refs[gi][:, lo:lo + n])
            o_ref[d] = jnp.concatenate(pieces, axis=-1).astype(BF16)

    return pl.pallas_call(
        body, name="grads_to_shards",
        grid=(D_MODEL // tm,),
        in_specs=[pl.BlockSpec((tm, w), lambda i: (i, 0)) for _, w in P_GROUPS],
        out_specs=pl.BlockSpec((N_DEV, tm, SHARD_COLS), lambda i: (0, i, 0)),
        out_shape=jax.ShapeDtypeStruct((N_DEV, D_MODEL, SHARD_COLS), BF16),
        compiler_params=_cparams(("arbitrary",)),
    )(*groups)


def _inproj(x, g_in, w_p):
    t = x.shape[0]
    tm = min(256, t)
    nj = 3
    tn = P_TOTAL // nj

    def body(x_ref, g_ref, w_ref, proj_ref, h_ref, r_ref):
        xf = x_ref[...]
        r = lax.rsqrt(jnp.mean(xf * xf, axis=-1, keepdims=True) + EPS)
        h = ((xf * r) * g_ref[...]).astype(BF16)
        proj_ref[...] = _dot(h, w_ref[...])

        @pl.when(pl.program_id(0) == 0)
        def _():
            h_ref[...] = h
            r_ref[...] = r

    first = lambda j, i: (jnp.where(j == 0, i, t // tm - 1), 0)
    return pl.pallas_call(
        body, name="inproj",
        grid=(nj, t // tm),
        in_specs=[pl.BlockSpec((tm, D_MODEL), lambda j, i: (i, 0)),
                  pl.BlockSpec((1, D_MODEL), lambda j, i: (0, 0)),
                  pl.BlockSpec((D_MODEL, tn), lambda j, i: (0, j))],
        out_specs=(pl.BlockSpec((tm, tn), lambda j, i: (i, j)),
                   pl.BlockSpec((tm, D_MODEL), first),
                   pl.BlockSpec((tm, 1), first)),
        out_shape=(jax.ShapeDtypeStruct((t, P_TOTAL), F32),
                   jax.ShapeDtypeStruct((t, D_MODEL), BF16),
                   jax.ShapeDtypeStruct((t, 1), F32)),
        compiler_params=_cparams(("arbitrary", "arbitrary")),
    )(x, g_in, w_p)


def _mla_prep(proj, g_q, g_kv, w_uq_p, w_k_p, w_v, w_gate_p, b_gate, rc, rsn, rsp):
    t = proj.shape[0]
    tm = min(256, t)
    hq = MLA_HEADS * HEAD_PAD

    def body(cq_ref, ckv_ref, misc_ref, gq_ref, gkv_ref, wuq_ref, wk_ref, wv_ref, wg_ref, bg_ref,
             c_ref, sn_ref, sp_ref,
             q_ref, k_ref, v_ref, la_ref, pre_ref, cqn_ref, ckvn_ref, rq_ref, rkv_ref):
        c, sn, sp = c_ref[...], sn_ref[...], sp_ref[...]
        cq = cq_ref[:, :MLA_Q_RANK]
        rq = lax.rsqrt(jnp.mean(cq * cq, axis=-1, keepdims=True) + EPS)
        cqn = ((cq * rq) * gq_ref[...]).astype(BF16)
        cqn_ref[...] = cqn
        rq_ref[...] = rq
        qpre = _dot(cqn, wuq_ref[...])
        ckv = ckv_ref[...]
        rkv = lax.rsqrt(jnp.mean(ckv * ckv, axis=-1, keepdims=True) + EPS)
        ckvn = ((ckv * rkv) * gkv_ref[...]).astype(BF16)
        ckvn_ref[...] = ckvn
        rkv_ref[...] = rkv
        kn = _dot(ckvn, wk_ref[...])
        v_ref[...] = _dot(ckvn, wv_ref[...]).astype(BF16)
        misc = misc_ref[...]
        krope = _rope_fwd(misc, c, sn, sp)
        for h in range(MLA_HEADS):
            sl = slice(h * HEAD_PAD, (h + 1) * HEAD_PAD)
            q_ref[:, sl] = _rope_fwd(qpre[:, sl], c, sn, sp).astype(BF16)
            k_ref[:, sl] = (kn[:, sl] + krope).astype(BF16)
        pre = _dot(misc.astype(BF16), wg_ref[...]) + bg_ref[...]
        pre_ref[...] = pre
        la_ref[...] = (jnp.minimum(pre, 0.0) - jnp.log(1.0 + jnp.exp(-jnp.abs(pre)))) / GLA_GATE_NORM

    row = lambda w: pl.BlockSpec((tm, w), lambda i: (i, 0))
    full = lambda a: pl.BlockSpec(a.shape, lambda i: (0, 0))
    return pl.pallas_call(
        body, name="mla_prep",
        grid=(t // tm,),
        in_specs=[pl.BlockSpec((tm, 512), lambda i: (i, P_CQ // 512)),
                  pl.BlockSpec((tm, MLA_KV_RANK), lambda i: (i, P_CKV // MLA_KV_RANK)),
                  pl.BlockSpec((tm, LANE), lambda i: (i, P_MISC // LANE)),
                  full(g_q), full(g_kv), full(w_uq_p), full(w_k_p), full(w_v), full(w_gate_p),
                  full(b_gate), row(LANE), row(LANE), row(LANE)],
        out_specs=(row(hq), row(hq), row(MLA_WIDTH), row(GLA_DK), row(GLA_DK),
                   row(MLA_Q_RANK), row(MLA_KV_RANK), row(1), row(1)),
        out_shape=(jax.ShapeDtypeStruct((t, hq), BF16), jax.ShapeDtypeStruct((t, hq), BF16),
                   jax.ShapeDtypeStruct((t, MLA_WIDTH), BF16),
                   jax.ShapeDtypeStruct((t, GLA_DK), F32), jax.ShapeDtypeStruct((t, GLA_DK), F32),
                   jax.ShapeDtypeStruct((t, MLA_Q_RANK), BF16),
                   jax.ShapeDtypeStruct((t, MLA_KV_RANK), BF16),
                   jax.ShapeDtypeStruct((t, 1), F32), jax.ShapeDtypeStruct((t, 1), F32)),
        compiler_params=_cparams(("arbitrary",)),
    )(proj, proj, proj, g_q, g_kv, w_uq_p, w_k_p, w_v, w_gate_p, b_gate, rc, rsn, rsp)


def _attn_masks(tq, t):
    rows = pl.program_id(1) * tq + lax.broadcasted_iota(jnp.int32, (tq, t), 0)
    cols = lax.broadcasted_iota(jnp.int32, (tq, t), 1)
    lane = lax.broadcasted_iota(jnp.int32, (tq, LANE), 1)
    return cols <= rows, lane < MLA_VDIM


def _mla_attn_fwd(q, k, v):
    t = q.shape[0]
    tq = min(256, t)
    scale = MLA_QK ** -0.5

    def body(q_ref, k_ref, v_ref, o_ref, lse_ref):
        causal, low = _attn_masks(tq, t)
        vp = v_ref[...]
        acc = jnp.zeros((tq, LANE), F32)
        for hh in range(2):
            sl = slice(hh * HEAD_PAD, (hh + 1) * HEAD_PAD)
            s = _dot_nt(q_ref[:, sl], k_ref[:, sl]) * scale
            s = jnp.where(causal, s, -jnp.inf)
            m = jnp.max(s, axis=-1, keepdims=True)
            e = jnp.exp(s - m)
            l = jnp.sum(e, axis=-1, keepdims=True)
            o = _dot(e.astype(BF16), vp) / l
            acc = jnp.where(low if hh == 0 else jnp.logical_not(low), o, acc)
            lse_ref[hh] = m + jnp.log(l)
        o_ref[...] = acc

    return pl.pallas_call(
        body, name="mla_attn_fwd",
        grid=(MLA_HEADS // 2, t // tq),
        in_specs=[pl.BlockSpec((tq, 2 * HEAD_PAD), lambda p, i: (i, p)),
                  pl.BlockSpec((t, 2 * HEAD_PAD), lambda p, i: (0, p)),
                  pl.BlockSpec((t, LANE), lambda p, i: (0, p))],
        out_specs=(pl.BlockSpec((tq, LANE), lambda p, i: (i, p)),
                   pl.BlockSpec((2, tq, 1), lambda p, i: (p, i, 0))),
        out_shape=(jax.ShapeDtypeStruct((t, MLA_WIDTH), F32),
                   jax.ShapeDtypeStruct((MLA_HEADS, t, 1), F32)),
        compiler_params=_cparams(("arbitrary", "arbitrary")),
    )(q, k, v)


def _mla_attn_bwd(q, k, v, o, do, lse):
    t = q.shape[0]
    tq = min(256, t)
    scale = MLA_QK ** -0.5

    def body(q_ref, k_ref, v_ref, o_ref, do_ref, lse_ref, dq_ref, dk_ref, dv_ref):
        @pl.when(pl.program_id(1) == 0)
        def _():
            dk_ref[...] = jnp.zeros_like(dk_ref)
            dv_ref[...] = jnp.zeros_like(dv_ref)

        causal, low = _attn_masks(tq, t)
        vp = v_ref[...]
        do_all = do_ref[...]
        o_all = o_ref[...]
        dv_acc = jnp.zeros((t, LANE), F32)
        for hh in range(2):
            sl = slice(hh * HEAD_PAD, (hh + 1) * HEAD_PAD)
            do_h = jnp.where(low if hh == 0 else jnp.logical_not(low), do_all, 0.0)
            dsum = jnp.sum(do_h * o_all, axis=-1, keepdims=True)
            qh = q_ref[:, sl]
            kh = k_ref[:, sl]
            s = _dot_nt(qh, kh) * scale
            p = jnp.where(causal, jnp.exp(s - lse_ref[hh]), 0.0)
            do_b = do_h.astype(BF16)
            dp = _dot_nt(do_b, vp)
            ds = (p * (dp - dsum) * scale).astype(BF16)
            dq_ref[:, sl] = _dot(ds, kh).astype(BF16)
            dk_ref[:, sl] += _dot_tn(ds, qh)
            dv_acc = dv_acc + _dot_tn(p.astype(BF16), do_b)
        dv_ref[...] += dv_acc

    return pl.pallas_call(
        body, name="mla_attn_bwd",
        grid=(MLA_HEADS // 2, t // tq),
        in_specs=[pl.BlockSpec((tq, 2 * HEAD_PAD), lambda p, i: (i, p)),
                  pl.BlockSpec((t, 2 * HEAD_PAD), lambda p, i: (0, p)),
                  pl.BlockSpec((t, LANE), lambda p, i: (0, p)),
                  pl.BlockSpec((tq, LANE), lambda p, i: (i, p)),
                  pl.BlockSpec((tq, LANE), lambda p, i: (i, p)),
                  pl.BlockSpec((2, tq, 1), lambda p, i: (p, i, 0))],
        out_specs=(pl.BlockSpec((tq, 2 * HEAD_PAD), lambda p, i: (i, p)),
                   pl.BlockSpec((t, 2 * HEAD_PAD), lambda p, i: (0, p)),
                   pl.BlockSpec((t, LANE), lambda p, i: (0, p))),
        out_shape=(jax.ShapeDtypeStruct((t, MLA_HEADS * HEAD_PAD), BF16),
                   jax.ShapeDtypeStruct((t, MLA_HEADS * HEAD_PAD), F32),
                   jax.ShapeDtypeStruct((t, MLA_WIDTH), F32)),
        compiler_params=_cparams(("arbitrary", "arbitrary")),
    )(q, k, v, o, do, lse)


def _gla_chunk_terms(q_ref, k_ref, la_ref, h, tri):
    sl = slice(h * GLA_HK, (h + 1) * GLA_HK)
    b = _dot_exact(tri, la_ref[:, sl])
    bl = b[GLA_CHUNK - 1:GLA_CHUNK, :]
    kc = k_ref[:, sl]
    q_in = (q_ref[:, sl] * (GLA_HK ** -0.5)) * jnp.exp(b)
    k_in = kc * jnp.exp(-b)
    k_st = kc * jnp.exp(bl - b)
    return b, bl, q_in, k_in, k_st


def _tri(c, lower):
    r = lax.broadcasted_iota(jnp.int32, (c, c), 0)
    cc = lax.broadcasted_iota(jnp.int32, (c, c), 1)
    return jnp.where(r >= cc if lower else r <= cc, 1.0, 0.0).astype(F32)


def _gla_fwd(proj, log_a):
    t = proj.shape[0]
    c = GLA_CHUNK
    n = t // c

    def body(q_ref, k_ref, v_ref, la_ref, o_ref, sp_ref, st_ref):
        @pl.when(pl.program_id(0) == 0)
        def _():
            st_ref[...] = jnp.zeros_like(st_ref)

        tri = _tri(c, True)
        for h in range(GLA_HEADS):
            _, bl, q_in, k_in, k_st = _gla_chunk_terms(q_ref, k_ref, la_ref, h, tri)
            vs = slice(h * GLA_HV, (h + 1) * GLA_HV)
            vv = v_ref[:, vs].astype(BF16)
            qb = q_in.astype(BF16)
            attn = _dot_nt(qb, k_in.astype(BF16)) * tri
            st = st_ref[h]
            sp_ref[0, h] = st
            o_ref[:, vs] = _dot(attn.astype(BF16), vv) + _dot_nt(qb, st.astype(BF16))
            st_ref[h] = st * jnp.exp(bl) + _dot_tn(vv, k_st.astype(BF16))

    return pl.pallas_call(
        body, name="gla_fwd",
        grid=(n,),
        in_specs=[pl.BlockSpec((c, GLA_DK), lambda i: (i, P_QG // GLA_DK)),
                  pl.BlockSpec((c, GLA_DK), lambda i: (i, P_KG // GLA_DK)),
                  pl.BlockSpec((c, GLA_DV), lambda i: (i, P_VG // GLA_DV)),
                  pl.BlockSpec((c, GLA_DK), lambda i: (i, 0))],
        out_specs=(pl.BlockSpec((c, GLA_DV), lambda i: (i, 0)),
                   pl.BlockSpec((1, GLA_HEADS, GLA_HV, GLA_HK), lambda i: (i, 0, 0, 0))),
        out_shape=(jax.ShapeDtypeStruct((t, GLA_DV), F32),
                   jax.ShapeDtypeStruct((n, GLA_HEADS, GLA_HV, GLA_HK), F32)),
        scratch_shapes=[pltpu.VMEM((GLA_HEADS, GLA_HV, GLA_HK), F32)],
        compiler_params=_cparams(("arbitrary",)),
    )(proj, proj, proj, log_a)


def _gla_bwd(proj, log_a, do, states):
    t = proj.shape[0]
    c = GLA_CHUNK
    n = t // c

    def body(q_ref, k_ref, v_ref, la_ref, do_ref, sp_ref, dg_ref, dla_ref, ds_ref):
        @pl.when(pl.program_id(0) == 0)
        def _():
            ds_ref[...] = jnp.zeros_like(ds_ref)

        tri = _tri(c, True)
        tri_t = _tri(c, False)
        for h in range(GLA_HEADS):
            b, bl, q_in, k_in, k_st = _gla_chunk_terms(q_ref, k_ref, la_ref, h, tri)
            ks_ = slice(h * GLA_HK, (h + 1) * GLA_HK)
            vs = slice(h * GLA_HV, (h + 1) * GLA_HV)
            vv = v_ref[:, vs].astype(BF16)
            do_h = do_ref[:, vs]
            qb, kb, ksb = q_in.astype(BF16), k_in.astype(BF16), k_st.astype(BF16)
            attn = (_dot_nt(qb, kb) * tri).astype(BF16)
            st = sp_ref[0, h]
            dst = ds_ref[h]
            dstb = dst.astype(BF16)
            dattn = (_dot_nt(do_h, vv) * tri).astype(BF16)
            dg_ref[:, P_VG + h * GLA_HV:P_VG + (h + 1) * GLA_HV] = (
                _dot_tn(attn, do_h) + _dot_nt(ksb, dstb)).astype(BF16)
            dq_in = _dot(dattn, kb) + _dot(do_h, st.astype(BF16))
            dk_in = _dot_tn(dattn, qb)
            dk_st = _dot(vv, dstb)
            ebl = jnp.exp(bl)
            d_ebl = jnp.sum(st * dst, axis=0, keepdims=True)
            ds_ref[h] = _dot_tn(do_h, qb) + dst * ebl
            dg_ref[:, P_QG + h * GLA_HK:P_QG + (h + 1) * GLA_HK] = (
                dq_in * (GLA_HK ** -0.5) * jnp.exp(b)).astype(BF16)
            dg_ref[:, P_KG + h * GLA_HK:P_KG + (h + 1) * GLA_HK] = (
                dk_in * jnp.exp(-b) + dk_st * jnp.exp(bl - b)).astype(BF16)
            db = dq_in * q_in - dk_in * k_in - dk_st * k_st
            dbl = jnp.sum(dk_st * k_st, axis=0, keepdims=True) + d_ebl * ebl
            dla_ref[:, ks_] = _dot_exact(tri_t, db) + dbl

    rev = lambda i: n - 1 - i
    gw = P_GROUPS[0][1]
    return pl.pallas_call(
        body, name="gla_bwd",
        grid=(n,),
        in_specs=[pl.BlockSpec((c, GLA_DK), lambda i: (rev(i), P_QG // GLA_DK)),
                  pl.BlockSpec((c, GLA_DK), lambda i: (rev(i), P_KG // GLA_DK)),
                  pl.BlockSpec((c, GLA_DV), lambda i: (rev(i), P_VG // GLA_DV)),
                  pl.BlockSpec((c, GLA_DK), lambda i: (rev(i), 0)),
                  pl.BlockSpec((c, GLA_DV), lambda i: (rev(i), 0)),
                  pl.BlockSpec((1, GLA_HEADS, GLA_HV, GLA_HK), lambda i: (rev(i), 0, 0, 0))],
        out_specs=(pl.BlockSpec((c, gw), lambda i: (rev(i), 0)),
                   pl.BlockSpec((c, GLA_DK), lambda i: (rev(i), 0))),
        out_shape=(jax.ShapeDtypeStruct((t, gw), BF16), jax.ShapeDtypeStruct((t, GLA_DK), F32)),
        scratch_shapes=[pltpu.VMEM((GLA_HEADS, GLA_HV, GLA_HK), F32)],
        compiler_params=_cparams(("arbitrary",)),
    )(proj, proj, proj, log_a, do, states)


def _post(o_mla, proj, o_gla, x, target, g_gla, g_final, w_pm, w_pg, w_o):
    t = x.shape[0]
    tm = min(128, t)
    g0, gw = P_GROUPS[1]

    def body(om_ref, zg_ref, gm_ref, gg_ref, zm_ref, og_ref, x_ref, tg_ref, ggla_ref, gf_ref,
             wpm_ref, wpg_ref, wo_ref,
             dx2_ref, dom_ref, dog_ref, dg_ref,
             mg_ref, um_ref, ug_ref, dym_ref, dyg_ref, loss_ref, dgf_ref, dggla_ref):
        @pl.when(pl.program_id(0) == 0)
        def _():
            loss_ref[...] = jnp.zeros_like(loss_ref)
            dgf_ref[...] = jnp.zeros_like(dgf_ref)
            dggla_ref[...] = jnp.zeros_like(dggla_ref)

        om = om_ref[...]
        zm = zm_ref[...]
        sm = _sigmoid(zm)
        silu_m = zm * sm
        um = (om * silu_m).astype(BF16)
        um_ref[...] = um
        ym = _dot(um, wpm_ref[...])

        ggla = ggla_ref[...]
        zg = zg_ref[...]
        sg = _sigmoid(zg)
        silu_g = zg * sg
        xhat, rstd, on = [], [], []
        for h in range(GLA_HEADS):
            blk = og_ref[:, h * GLA_HV:(h + 1) * GLA_HV]
            r = lax.rsqrt(jnp.mean(blk * blk, axis=-1, keepdims=True) + EPS)
            xhat.append(blk * r)
            rstd.append(r)
            on.append(xhat[h] * ggla)
        on = jnp.concatenate(on, axis=-1)
        ug = (on * silu_g).astype(BF16)
        ug_ref[...] = ug
        yg = _dot(ug, wpg_ref[...])

        sgm = _sigmoid(gm_ref[...])
        sgg = _sigmoid(gg_ref[...])
        merged = (sgm * ym + sgg * yg).astype(BF16)
        mg_ref[...] = merged
        x2 = x_ref[...] + _dot(merged, wo_ref[...])
        gf = gf_ref[...]
        rf = lax.rsqrt(jnp.mean(x2 * x2, axis=-1, keepdims=True) + EPS)
        xh = x2 * rf
        err = xh * gf - tg_ref[...]
        loss_ref[...] += 0.5 * jnp.sum(jnp.mean(err * err, axis=-1, keepdims=True))

        dy = err * (1.0 / D_MODEL)
        dgf_ref[...] += jnp.sum(dy * xh, axis=0, keepdims=True)
        dxh = dy * gf
        dx2 = rf * (dxh - xh * jnp.mean(dxh * xh, axis=-1, keepdims=True))
        dx2_ref[...] = dx2
        dmerged = _dot_nt(dx2.astype(BF16), wo_ref[...])
        dym = (dmerged * sgm).astype(BF16)
        dyg = (dmerged * sgg).astype(BF16)
        dym_ref[...] = dym
        dyg_ref[...] = dyg
        dg_ref[:, P_GMLA - g0:P_GMLA - g0 + D_MODEL] = (dmerged * ym * sgm * (1.0 - sgm)).astype(BF16)
        dg_ref[:, P_GGLA - g0:P_GGLA - g0 + D_MODEL] = (dmerged * yg * sgg * (1.0 - sgg)).astype(BF16)
        dum = _dot_nt(dym, wpm_ref[...])
        dom_ref[...] = dum * silu_m
        dg_ref[:, P_ZMLA - g0:P_ZMLA - g0 + MLA_WIDTH] = (
            dum * om * (sm * (1.0 + zm * (1.0 - sm)))).astype(BF16)
        dug = _dot_nt(dyg, wpg_ref[...])
        dg_ref[:, P_ZGLA - g0:P_ZGLA - g0 + GLA_DV] = (
            dug * on * (sg * (1.0 + zg * (1.0 - sg)))).astype(BF16)
        don = dug * silu_g
        dggla = jnp.zeros((1, GLA_HV), F32)
        for h in range(GLA_HEADS):
            hs = slice(h * GLA_HV, (h + 1) * GLA_HV)
            don_h = don[:, hs]
            dggla = dggla + jnp.sum(don_h * xhat[h], axis=0, keepdims=True)
            dxh_h = don_h * ggla
            dog_ref[:, hs] = (rstd[h] * (dxh_h - xhat[h] * jnp.mean(dxh_h * xhat[h], axis=-1,
                                                                     keepdims=True))).astype(BF16)
        dggla_ref[...] += dggla

    row = lambda w: pl.BlockSpec((tm, w), lambda i: (i, 0))
    pcol = lambda w, off: pl.BlockSpec((tm, w), lambda i: (i, off // w))
    full = lambda a: pl.BlockSpec(a.shape, lambda i: (0, 0))
    sds = jax.ShapeDtypeStruct
    return pl.pallas_call(
        body, name="post_fwd_bwd",
        grid=(t // tm,),
        in_specs=[row(MLA_WIDTH), pcol(GLA_DV, P_ZGLA), pcol(D_MODEL, P_GMLA), pcol(D_MODEL, P_GGLA),
                  pcol(MLA_WIDTH, P_ZMLA), row(GLA_DV), row(D_MODEL), row(D_MODEL),
                  full(g_gla), full(g_final), full(w_pm), full(w_pg), full(w_o)],
        out_specs=(row(D_MODEL), row(MLA_WIDTH), row(GLA_DV), row(gw),
                   row(D_MODEL), row(MLA_WIDTH), row(GLA_DV), row(D_MODEL), row(D_MODEL),
                   pl.BlockSpec((1, LANE), lambda i: (0, 0)),
                   pl.BlockSpec((1, D_MODEL), lambda i: (0, 0)),
                   pl.BlockSpec((1, GLA_HV), lambda i: (0, 0))),
        out_shape=(sds((t, D_MODEL), F32), sds((t, MLA_WIDTH), F32), sds((t, GLA_DV), BF16),
                   sds((t, gw), BF16),
                   sds((t, D_MODEL), BF16), sds((t, MLA_WIDTH), BF16), sds((t, GLA_DV), BF16),
                   sds((t, D_MODEL), BF16), sds((t, D_MODEL), BF16),
                   sds((1, LANE), F32), sds((1, D_MODEL), F32), sds((1, GLA_HV), F32)),
        compiler_params=_cparams(("arbitrary",)),
    )(o_mla, proj, proj, proj, proj, o_gla, x, target, g_gla, g_final, w_pm, w_pg, w_o)


def _mla_prep_bwd(dq, dk, dv, dla, pre, proj, rq, rkv, g_q, g_kv, w_uq_p, w_k_p, w_v, w_gate_p,
                  rc, rsn, rsp):
    t = proj.shape[0]
    tm = min(256, t)
    gw = P_GROUPS[2][1]

    def body(dq_ref, dk_ref, dv_ref, dla_ref, pre_ref, cq_ref, ckv_ref, rq_ref, rkv_ref,
             gq_ref, gkv_ref, wuq_ref, wk_ref, wv_ref, wg_ref, c_ref, sn_ref, sp_ref,
             dg_ref, dqpre_ref, dpre_ref, dgq_ref, dgkv_ref, dbg_ref):
        @pl.when(pl.program_id(0) == 0)
        def _():
            dgq_ref[...] = jnp.zeros_like(dgq_ref)
            dgkv_ref[...] = jnp.zeros_like(dgkv_ref)
            dbg_ref[...] = jnp.zeros_like(dbg_ref)

        c, sn, sp = c_ref[...], sn_ref[...], sp_ref[...]
        dkr = jnp.zeros((tm, LANE), F32)
        for h in range(MLA_HEADS):
            sl = slice(h * HEAD_PAD, (h + 1) * HEAD_PAD)
            dqpre_ref[:, sl] = _rope_bwd(dq_ref[:, sl].astype(F32), c, sn, sp).astype(BF16)
            dkr = dkr + dk_ref[:, sl]
        dcqn = _dot_nt(dqpre_ref[...], wuq_ref[...])
        rq = rq_ref[...]
        xh = cq_ref[:, :MLA_Q_RANK] * rq
        dgq_ref[...] += jnp.sum(dcqn * xh, axis=0, keepdims=True)
        dxh = dcqn * gq_ref[...]
        dcq = rq * (dxh - xh * jnp.mean(dxh * xh, axis=-1, keepdims=True))
        dg_ref[:, :MLA_Q_RANK] = dcq.astype(BF16)
        dg_ref[:, MLA_Q_RANK:512] = jnp.zeros((tm, 512 - MLA_Q_RANK), BF16)

        dckvn = _dot_nt(dk_ref[...].astype(BF16), wk_ref[...]) + \
            _dot_nt(dv_ref[...].astype(BF16), wv_ref[...])
        rkv = rkv_ref[...]
        xh = ckv_ref[...] * rkv
        dgkv_ref[...] += jnp.sum(dckvn * xh, axis=0, keepdims=True)
        dxh = dckvn * gkv_ref[...]
        dg_ref[:, P_CKV - P_CQ:P_CKV - P_CQ + MLA_KV_RANK] = (
            rkv * (dxh - xh * jnp.mean(dxh * xh, axis=-1, keepdims=True))).astype(BF16)

        dpre = dla_ref[...] * (1.0 / GLA_GATE_NORM) * (1.0 - _sigmoid(pre_ref[...]))
        dbg_ref[...] += jnp.sum(dpre, axis=0, keepdims=True)
        dpre = dpre.astype(BF16)
        dpre_ref[...] = dpre
        lane = lax.broadcasted_iota(jnp.int32, (tm, LANE), 1)
        in_kr = jnp.logical_and(lane >= MISC_KR, lane < MISC_KR + MLA_ROPE)
        dmisc = jnp.where(in_kr, _rope_bwd(dkr, c, sn, sp), 0.0) + _dot_nt(dpre, wg_ref[...])
        dg_ref[:, P_MISC - P_CQ:P_MISC - P_CQ + LANE] = dmisc.astype(BF16)

    hq = MLA_HEADS * HEAD_PAD
    row = lambda w: pl.BlockSpec((tm, w), lambda i: (i, 0))
    full = lambda a: pl.BlockSpec(a.shape, lambda i: (0, 0))
    acc = lambda w: pl.BlockSpec((1, w), lambda i: (0, 0))
    sds = jax.ShapeDtypeStruct
    return pl.pallas_call(
        body, name="mla_prep_bwd",
        grid=(t // tm,),
        in_specs=[row(hq), row(hq), row(MLA_WIDTH), row(GLA_DK), row(GLA_DK),
                  pl.BlockSpec((tm, 512), lambda i: (i, P_CQ // 512)),
                  pl.BlockSpec((tm, MLA_KV_RANK), lambda i: (i, P_CKV // MLA_KV_RANK)),
                  row(1), row(1), full(g_q), full(g_kv), full(w_uq_p), full(w_k_p), full(w_v),
                  full(w_gate_p), row(LANE), row(LANE), row(LANE)],
        out_specs=(row(gw), row(hq), row(GLA_DK),
                   acc(MLA_Q_RANK), acc(MLA_KV_RANK), acc(GLA_DK)),
        out_shape=(sds((t, gw), BF16), sds((t, hq), BF16), sds((t, GLA_DK), BF16),
                   sds((1, MLA_Q_RANK), F32), sds((1, MLA_KV_RANK), F32), sds((1, GLA_DK), F32)),
        compiler_params=_cparams(("arbitrary",)),
    )(dq, dk, dv, dla, pre, proj, proj, rq, rkv, g_q, g_kv, w_uq_p, w_k_p, w_v, w_gate_p,
      rc, rsn, rsp)


def _inproj_bwd(dgroups, w_p, x, rstd, g_in, dx2):
    t = x.shape[0]
    tm = min(256, t)

    def body(d0_ref, d1_ref, d2_ref, w_ref, x_ref, r_ref, g_ref, dx2_ref, dx_ref, dg_ref):
        @pl.when(pl.program_id(0) == 0)
        def _():
            dg_ref[...] = jnp.zeros_like(dg_ref)

        dh = jnp.zeros((tm, D_MODEL), F32)
        for d_ref, (off, width) in zip((d0_ref, d1_ref, d2_ref), P_GROUPS):
            dh = dh + _dot_nt(d_ref[...], w_ref[:, off:off + width])
        r = r_ref[...]
        xh = x_ref[...] * r
        dg_ref[...] += jnp.sum(dh * xh, axis=0, keepdims=True)
        dxh = dh * g_ref[...]
        dx_ref[...] = dx2_ref[...] + r * (dxh - xh * jnp.mean(dxh * xh, axis=-1, keepdims=True))

    row = lambda w: pl.BlockSpec((tm, w), lambda i: (i, 0))
    return pl.pallas_call(
        body, name="inproj_bwd",
        grid=(t // tm,),
        in_specs=[row(P_GROUPS[0][1]), row(P_GROUPS[1][1]), row(P_GROUPS[2][1]),
                  pl.BlockSpec((D_MODEL, P_TOTAL), lambda i: (0, 0)),
                  row(D_MODEL), row(1), pl.BlockSpec((1, D_MODEL), lambda i: (0, 0)), row(D_MODEL)],
        out_specs=(row(D_MODEL), pl.BlockSpec((1, D_MODEL), lambda i: (0, 0))),
        out_shape=(jax.ShapeDtypeStruct((t, D_MODEL), F32),
                   jax.ShapeDtypeStruct((1, D_MODEL), F32)),
        compiler_params=_cparams(("arbitrary",)),
    )(*dgroups, w_p, x, rstd, g_in, dx2)


def _matmul(name, a, b, tm, tn, dtype=F32):
    m, kk = a.shape
    n = b.shape[1]

    def body(a_ref, b_ref, o_ref):
        o_ref[...] = _dot(a_ref[...].astype(BF16), b_ref[...].astype(BF16)).astype(dtype)

    return pl.pallas_call(
        body, name=name,
        grid=(n // tn, m // tm),
        in_specs=[pl.BlockSpec((tm, kk), lambda j, i: (i, 0)),
                  pl.BlockSpec((kk, tn), lambda j, i: (0, j))],
        out_specs=pl.BlockSpec((tm, tn), lambda j, i: (i, j)),
        out_shape=jax.ShapeDtypeStruct((m, n), dtype),
        compiler_params=_cparams(("arbitrary", "arbitrary")),
    )(a, b)


def _adamw_update(p_ref, w_ref, m_ref, v_ref, g_ref, d_ref, nm_ref, nv_ref):
    g = p_ref[0].astype(F32)
    for q in range(1, N_DEV):
        g = g + p_ref[q].astype(F32)
    m_new = ADAM_B1 * m_ref[...] + (1.0 - ADAM_B1) * g
    v_new = ADAM_B2 * v_ref[...] + (1.0 - ADAM_B2) * (g * g)
    m_hat = m_new / (1.0 - ADAM_B1 ** ADAM_STEP)
    v_hat = v_new / (1.0 - ADAM_B2 ** ADAM_STEP)
    g_ref[...] = g
    nm_ref[...] = m_new
    nv_ref[...] = v_new
    d_ref[...] = -ADAM_LR * (m_hat / (jnp.sqrt(v_hat) + ADAM_EPS) + ADAM_WD * w_ref[...])


def _adamw_rows(name, parts, w, m, v, tr):
    rows, cols = w.shape

    def body(*refs):
        _adamw_update(*refs)

    blk = pl.BlockSpec((tr, cols), lambda i: (i, 0))
    out = jax.ShapeDtypeStruct((rows, cols), F32)
    return pl.pallas_call(
        body, name=name,
        grid=(rows // tr,),
        in_specs=[pl.BlockSpec((N_DEV, tr, cols), lambda i: (0, i, 0)), blk, blk, blk],
        out_specs=(blk, blk, blk, blk),
        out_shape=(out, out, out, out),
        compiler_params=_cparams(("arbitrary",)),
    )(parts, w, m, v)


def _adamw_group(parts, ws, ms, vs):
    n = len(ws)

    def body(*refs):
        ins, outs = refs[:4 * n], refs[4 * n:]
        for a in range(n):
            _adamw_update(ins[a], ins[n + a], ins[2 * n + a], ins[3 * n + a], *outs[4 * a:4 * a + 4])

    vmem = lambda k: [pl.BlockSpec(memory_space=pltpu.VMEM) for _ in range(k)]
    out_shape = []
    for w in ws:
        out_shape += [jax.ShapeDtypeStruct(w.shape, F32)] * 4
    res = pl.pallas_call(
        body, name="adamw_small_weights",
        in_specs=vmem(4 * n), out_specs=tuple(vmem(4 * n)), out_shape=tuple(out_shape),
        compiler_params=_cparams(),
    )(*parts, *ws, *ms, *vs)
    return [res[4 * a:4 * a + 4] for a in range(n)]


def _rope_tables(positions):
    half = MLA_ROPE // 2
    freqs = ROPE_THETA ** (-jnp.arange(half, dtype=F32) / half)
    ang = positions.astype(F32).reshape(-1, 1) * freqs
    cos, sin = jnp.cos(ang), jnp.sin(ang)
    t = ang.shape[0]
    one, zero = jnp.ones((t, MLA_NOPE), F32), jnp.zeros((t, half), F32)
    tail = jnp.zeros((t, LANE - MLA_QK), F32)
    rc = jnp.concatenate([one, cos, cos, tail], axis=1)
    rsn = jnp.concatenate([0.0 * one, -sin, zero, tail], axis=1)
    rsp = jnp.concatenate([0.0 * one, zero, sin, tail], axis=1)
    return rc, rsn, rsp


def _cols_full(g):
    return g.transpose(1, 0, 2)


def kernel(x, positions, g_in, w_in, g_q, w_uq, g_kv, w_ukv, w_gla_gate, b_gla_gate, g_gla, w_proj_mla, w_proj_gla, w_out, g_final, loss_target, m_g_in, m_w_in, m_g_q, m_w_uq, m_g_kv, m_w_ukv, m_w_gla_gate, m_b_gla_gate, m_g_gla, m_w_proj_mla, m_w_proj_gla, m_w_out, m_g_final, v_g_in, v_w_in, v_g_q, v_w_uq, v_g_kv, v_w_ukv, v_w_gla_gate, v_b_gla_gate, v_g_gla, v_w_proj_mla, v_w_proj_gla, v_w_out, v_g_final):
    t = x.shape[1]
    x2d = x.reshape(t, D_MODEL)
    tgt = loss_target.reshape(t, D_MODEL)
    g_final2 = g_final.reshape(1, D_MODEL)
    sharded = [(w_in, m_w_in, v_w_in), (w_uq, m_w_uq, v_w_uq), (w_ukv, m_w_ukv, v_w_ukv),
               (w_gla_gate, m_w_gla_gate, v_w_gla_gate), (w_proj_mla, m_w_proj_mla, v_w_proj_mla),
               (w_proj_gla, m_w_proj_gla, v_w_proj_gla), (w_out, m_w_out, v_w_out)]
    sharded = [tuple(a[0] for a in s) for s in sharded]

    g_w_in, g_uq, g_ukv, g_gate, g_pm, g_pg, g_o = _all_gather([s[0].astype(BF16) for s in sharded])
    w_in_p = _weights_to_p(g_w_in)
    w_uq_p = jnp.pad(_cols_full(g_uq), ((0, 0), (0, 0), (0, HEAD_PAD - MLA_QK))).reshape(
        MLA_Q_RANK, MLA_HEADS * HEAD_PAD)
    ukv = _cols_full(g_ukv)
    w_k_p = jnp.pad(ukv[:, :, :MLA_NOPE], ((0, 0), (0, 0), (0, HEAD_PAD - MLA_NOPE))).reshape(
        MLA_KV_RANK, MLA_HEADS * HEAD_PAD)
    w_v = ukv[:, :, MLA_NOPE:].reshape(MLA_KV_RANK, MLA_WIDTH)
    w_gate_p = jnp.pad(_cols_full(g_gate).reshape(GLA_GATE_RANK, GLA_DK),
                       ((MISC_ALR, LANE - MISC_ALR - GLA_GATE_RANK), (0, 0)))
    w_pm = _cols_full(g_pm).reshape(MLA_WIDTH, D_MODEL)
    w_pg = g_pg.reshape(GLA_DV, D_MODEL)
    w_o = g_o.reshape(D_MODEL, D_MODEL)
    rc, rsn, rsp = _rope_tables(positions)

    proj, h, rstd = _inproj(x2d, g_in, w_in_p)
    q, k, v, log_a, pre, cqn, ckvn, rq, rkv = _mla_prep(
        proj, g_q, g_kv, w_uq_p, w_k_p, w_v, w_gate_p, b_gla_gate, rc, rsn, rsp)
    o_mla, lse = _mla_attn_fwd(q, k, v)
    o_gla, states = _gla_fwd(proj, log_a)

    (dx2, do_mla, do_gla, d_out, merged, um, ug, dym, dyg, loss_p, dg_final,
     dg_gla) = _post(o_mla, proj, o_gla, x2d, tgt, g_gla, g_final2, w_pm, w_pg, w_o)

    dq, dk, dv = _mla_attn_bwd(q, k, v, o_mla, do_mla, lse)
    d_gla, dla = _gla_bwd(proj, log_a, do_gla, states)
    d_lat, dqpre, dpre, dg_q, dg_kv, db_gate = _mla_prep_bwd(
        dq, dk, dv, dla, pre, proj, rq, rkv, g_q, g_kv, w_uq_p, w_k_p, w_v, w_gate_p, rc, rsn, rsp)

    dgroups = (d_gla, d_out, d_lat)
    grad_x, dg_in = _inproj_bwd(dgroups, w_in_p, x2d, rstd, g_in, dx2)

    h_t = h.T
    dw_groups = [_matmul("dw_in_%d" % i, h_t, dgroups[i], 512, tn)
                 for i, tn in enumerate((512, 512, 896))]
    p_in = _grads_to_shards(dw_groups)
    dw_uq = _matmul("dw_uq", cqn.T, dqpre, MLA_Q_RANK, 512, BF16)
    p_uq = dw_uq.reshape(MLA_Q_RANK, MLA_HEADS, HEAD_PAD)[:, :, :MLA_QK].transpose(1, 0, 2)
    ckvn_t = ckvn.T
    dw_k = _matmul("dw_uk", ckvn_t, dk, MLA_KV_RANK, 512, BF16)
    dw_v = _matmul("dw_uv", ckvn_t, dv, MLA_KV_RANK, 512, BF16)
    p_ukv = jnp.concatenate(
        [dw_k.reshape(MLA_KV_RANK, MLA_HEADS, HEAD_PAD)[:, :, :MLA_NOPE],
         dw_v.reshape(MLA_KV_RANK, MLA_HEADS, MLA_VDIM)], axis=2).transpose(1, 0, 2)
    misc_t = proj[:, P_MISC:P_MISC + LANE].T
    dw_gate = _matmul("dw_gate", misc_t, dpre, LANE, 512, BF16)
    p_gate = dw_gate[MISC_ALR:MISC_ALR + GLA_GATE_RANK].reshape(
        GLA_GATE_RANK, N_DEV, GLA_DK // N_DEV).transpose(1, 0, 2)
    p_pm = _matmul("dw_proj_mla", um.T, dym, 512, 512, BF16).reshape(
        MLA_WIDTH, N_DEV, D_MODEL // N_DEV).transpose(1, 0, 2)
    p_pg = _matmul("dw_proj_gla", ug.T, dyg, 512, 512, BF16).reshape(N_DEV, -1, D_MODEL)
    p_o = _matmul("dw_out", merged.T, dx2, 512, 512, BF16).reshape(N_DEV, -1, D_MODEL)
    small = jnp.concatenate([dg_in.reshape(-1), dg_q.reshape(-1), dg_kv.reshape(-1),
                             db_gate.reshape(-1), dg_gla.reshape(-1), dg_final.reshape(-1),
                             loss_p[0, :1]])
    small = jnp.pad(small, (0, SMALL_ROWS * LANE - small.shape[0])).reshape(SMALL_ROWS, LANE)

    recv = _exchange_grads([p_in, p_uq, p_ukv, p_gate, p_pm, p_pg, p_o], small)
    big = [_adamw_rows("adamw_w_in", recv[0], *sharded[0], 128)]
    big += _adamw_group(recv[1:7], *[[s[j] for s in sharded[1:]] for j in range(3)])
    replicated = [(g_in, m_g_in, v_g_in), (g_q, m_g_q, v_g_q), (g_kv, m_g_kv, v_g_kv),
                  (b_gla_gate, m_b_gla_gate, v_b_gla_gate), (g_gla, m_g_gla, v_g_gla),
                  (g_final, m_g_final, v_g_final)]
    spacks = [jnp.pad(jnp.concatenate([s[j].reshape(-1) for s in replicated]),
                      (0, SMALL_ROWS * LANE - sum(SMALL_SIZES))).reshape(SMALL_ROWS, LANE)
              for j in range(3)]
    tiny = _adamw_rows("adamw_gains", recv[7], spacks[0], spacks[1], spacks[2], SMALL_ROWS)

    outs = {}
    names = ("w_in", "w_uq", "w_ukv", "w_gla_gate", "w_proj_mla", "w_proj_gla", "w_out")
    for j, kind in enumerate(("grad", "delta", "new_m", "new_v")):
        for name, res in zip(names, big):
            outs[kind, name] = res[j][None]
        flat = tiny[j].reshape(-1)
        off = 0
        for name, size in zip(("g_in", "g_q", "g_kv", "b_gla_gate", "g_gla", "g_final"), SMALL_SIZES):
            shape = (size,) if name == "g_final" else (1, size)
            outs[kind, name] = flat[off:off + size].reshape(shape)
            off += size
    loss = tiny[0].reshape(-1)[sum(SMALL_SIZES)]
    order = ("g_in", "w_in", "g_q", "w_uq", "g_kv", "w_ukv", "w_gla_gate", "b_gla_gate", "g_gla",
             "w_proj_mla", "w_proj_gla", "w_out", "g_final")
    result = [loss, grad_x.reshape(1, t, D_MODEL)]
    for kind in ("grad", "delta", "new_m", "new_v"):
        result += [outs[kind, name] for name in order]
    return tuple(result)
```

```python
import math

import jax
import jax.numpy as jnp
from jax import lax
from jax.experimental import pallas as pl
from jax.experimental.pallas import tpu as pltpu

F32 = jnp.float32
BF16 = jnp.bfloat16
MESH = pl.DeviceIdType.MESH
N_DEV = 8

D_MODEL = 1024
EPS = 1e-6
MLA_HEADS = 8
MLA_NOPE = 64
MLA_ROPE = 32
MLA_VDIM = 64
MLA_Q_RANK = 384
MLA_KV_RANK = 256
MLA_QK = MLA_NOPE + MLA_ROPE
MLA_WIDTH = MLA_HEADS * MLA_VDIM
ROPE_THETA = 10000.0
GLA_HEADS = 4
GLA_DK = 512
GLA_DV = 1024
GLA_HK = 128
GLA_HV = 256
GLA_GATE_RANK = 16
GLA_GATE_NORM = 16.0
GLA_CHUNK = 64
D_IN = 6320

ADAM_LR = 0.001
ADAM_B1 = 0.9
ADAM_B2 = 0.999
ADAM_EPS = 1e-08
ADAM_WD = 0.01
ADAM_STEP = 10

LANE = 128
HEAD_PAD = 128
VMEM_LIMIT = 48 * 1024 * 1024

P_VG, P_QG, P_KG = 0, 1024, 1536
P_ZGLA, P_GMLA, P_GGLA, P_ZMLA = 2048, 3072, 4096, 5120
P_CQ, P_CKV, P_MISC = 5632, 6144, 6400
P_TOTAL = 6528
P_GROUPS = ((0, 2048), (2048, 3584), (5632, 896))
MISC_KR = 64
MISC_ALR = 96
SHARD_COLS = D_IN // N_DEV
P_COMPONENTS = ((0, 384, P_CQ), (384, 256, P_CKV), (640, 32, P_MISC + MISC_KR), (672, 512, P_ZMLA),
                (1184, 512, P_QG), (1696, 512, P_KG), (2208, 1024, P_VG),
                (3232, 16, P_MISC + MISC_ALR), (3248, 1024, P_ZGLA), (4272, 1024, P_GMLA),
                (5296, 1024, P_GGLA))

SMALL_SIZES = (1024, 384, 256, 512, 256, 1024)
SMALL_ROWS = 32


def _segments():
    segs = []
    for g0, n, p0 in P_COMPONENTS:
        g = g0
        while g < g0 + n:
            d = g // SHARD_COLS
            end = min(g0 + n, (d + 1) * SHARD_COLS)
            segs.append((d, g - d * SHARD_COLS, end - g, p0 + g - g0))
            g = end
    return segs


def _cparams(sem=None):
    if sem is None:
        return pltpu.CompilerParams(vmem_limit_bytes=VMEM_LIMIT)
    return pltpu.CompilerParams(dimension_semantics=sem, vmem_limit_bytes=VMEM_LIMIT)


def _sigmoid(v):
    return 1.0 / (1.0 + jnp.exp(-v))


def _dot(a, b):
    return jnp.dot(a, b, preferred_element_type=F32)


def _dot_nt(a, b):
    return lax.dot_general(a, b, (((1,), (1,)), ((), ())), preferred_element_type=F32)


def _dot_tn(a, b):
    return lax.dot_general(a, b, (((0,), (0,)), ((), ())), preferred_element_type=F32)


def _dot_exact(a, b):
    return jnp.dot(a, b, preferred_element_type=F32, precision=lax.Precision.HIGHEST)


def _rope_fwd(blk, c, sn, sp):
    return blk * c + pltpu.roll(blk, LANE - 16, 1) * sn + pltpu.roll(blk, 16, 1) * sp


def _rope_bwd(blk, c, sn, sp):
    return blk * c + pltpu.roll(blk * sn, 16, 1) + pltpu.roll(blk * sp, LANE - 16, 1)


def _mesh_pos():
    return lax.axis_index("x"), lax.axis_index("y"), lax.axis_index("c")


def _hbm_specs(n):
    return [pl.BlockSpec(memory_space=pltpu.HBM) for _ in range(n)]


def _all_gather(shards):
    n = len(shards)

    def body(*refs):
        x_refs, out_refs = refs[:n], refs[n:2 * n]
        send_sems, recv_sems, local_sems = refs[2 * n:]
        x, y, c = _mesh_pos()
        me, sibling = (x, y, c), (x, y, 1 - c)
        chips = [(1 - x, y), (x, 1 - y), (1 - x, 1 - y)]

        def slot(a, px, py, pc):
            return out_refs[a].at[4 * px + 2 * py + pc]

        def copies(k, block, to, own=False):
            return [pltpu.make_async_remote_copy(
                src_ref=x_refs[a] if own else slot(a, *block), dst_ref=slot(a, *block),
                send_sem=send_sems.at[k, a], recv_sem=recv_sems.at[k, a],
                device_id=to, device_id_type=MESH) for a in range(n)]

        mine = [pltpu.make_async_copy(x_refs[a], slot(a, *me), local_sems.at[a]) for a in range(n)]
        for cp in mine:
            cp.start()
        first = copies(0, me, sibling, own=True)
        for j, chip in enumerate(chips):
            first += copies(1 + j, me, (*chip, c), own=True)
        for cp in first:
            cp.start()
        passed = []
        for j, chip in enumerate(chips):
            for cp in copies(1 + j, (*chip, c), me):
                cp.wait_recv()
            fwd = copies(4 + j, (*chip, c), sibling)
            for cp in fwd:
                cp.start()
            passed += fwd
        for cp in copies(0, sibling, me):
            cp.wait_recv()
        for j, chip in enumerate(chips):
            for cp in copies(4 + j, (*chip, 1 - c), me):
                cp.wait_recv()
        for cp in first + passed:
            cp.wait_send()
        for cp in mine:
            cp.wait()

    return pl.pallas_call(
        body, name="all_gather_weights",
        out_shape=tuple(jax.ShapeDtypeStruct((N_DEV,) + s.shape, s.dtype) for s in shards),
        in_specs=_hbm_specs(n), out_specs=tuple(_hbm_specs(n)),
        scratch_shapes=[pltpu.SemaphoreType.DMA((7, n)), pltpu.SemaphoreType.DMA((7, n)),
                        pltpu.SemaphoreType.DMA((n,))],
    )(*shards)


def _exchange_grads(parts, small):
    n = len(parts)
    rows_per_add = 256

    def body(*refs):
        p_refs, s_ref = refs[:n], refs[n]
        out_refs, sout_ref = refs[n + 1:2 * n + 1], refs[2 * n + 1]
        mine_v, recv_v = refs[2 * n + 2:3 * n + 2], refs[3 * n + 2:4 * n + 2]
        (d2d_send, d2d_recv, ici_send, ici_recv, s_send, s_recv, load_sems, own_sems,
         sown_sem) = refs[4 * n + 2:]
        x, y, c = _mesh_pos()
        me = 4 * x + 2 * y + c
        chips = [(x, y), (1 - x, y), (x, 1 - y), (1 - x, 1 - y)]

        sown = pltpu.make_async_copy(s_ref, sout_ref.at[me], sown_sem)
        sown.start()
        tiny = []
        for k in range(1, N_DEV):
            peer = (x ^ (k >> 2), y ^ ((k >> 1) & 1), c ^ (k & 1))
            tiny.append(pltpu.make_async_remote_copy(
                src_ref=s_ref, dst_ref=sout_ref.at[me], send_sem=s_send.at[k - 1],
                recv_sem=s_recv.at[k - 1], device_id=peer, device_id_type=MESH))
        for cp in tiny:
            cp.start()

        loads, swaps = [], []
        for a in range(n):
            for j, (px, py) in enumerate(chips):
                loads.append(pltpu.make_async_copy(
                    p_refs[a].at[4 * px + 2 * py + c], mine_v[a].at[j], load_sems.at[a, j]))
                swaps.append(pltpu.make_async_remote_copy(
                    src_ref=p_refs[a].at[4 * px + 2 * py + 1 - c], dst_ref=recv_v[a].at[j],
                    send_sem=d2d_send.at[a, j], recv_sem=d2d_recv.at[a, j],
                    device_id=(x, y, 1 - c), device_id_type=MESH))
        for cp in swaps + loads:
            cp.start()

        outgoing, own = [], []
        for a in range(n):
            rows = p_refs[a].shape[1]
            step = math.gcd(rows, rows_per_add)
            for j in range(4):
                loads[4 * a + j].wait()
                swaps[4 * a + j].wait_recv()

                @pl.loop(0, rows // step)
                def _(i):
                    rs = pl.ds(pl.multiple_of(i * step, step), step)
                    mine_v[a][j, rs, :] = (mine_v[a][j, rs, :].astype(F32)
                                           + recv_v[a][j, rs, :].astype(F32)).astype(BF16)

                if j == 0:
                    own.append(pltpu.make_async_copy(mine_v[a].at[0], out_refs[a].at[0],
                                                     own_sems.at[a]))
                    own[-1].start()
                else:
                    outgoing.append(pltpu.make_async_remote_copy(
                        src_ref=mine_v[a].at[j], dst_ref=out_refs[a].at[j],
                        send_sem=ici_send.at[a, j - 1], recv_sem=ici_recv.at[a, j - 1],
                        device_id=(*chips[j], c), device_id_type=MESH))
                    outgoing[-1].start()

        for cp in tiny + outgoing:
            cp.wait_recv()
        for cp in tiny + outgoing + swaps:
            cp.wait_send()
        for cp in own:
            cp.wait()
        sown.wait()

    outs = [jax.ShapeDtypeStruct((4,) + p.shape[1:], p.dtype) for p in parts]
    outs.append(jax.ShapeDtypeStruct((N_DEV,) + small.shape, small.dtype))
    stage = [pltpu.VMEM((4,) + p.shape[1:], p.dtype) for p in parts]
    return pl.pallas_call(
        body, name="exchange_grads",
        out_shape=tuple(outs),
        in_specs=_hbm_specs(n + 1), out_specs=tuple(_hbm_specs(n + 1)),
        scratch_shapes=stage + stage + [
            pltpu.SemaphoreType.DMA((n, 4)), pltpu.SemaphoreType.DMA((n, 4)),
            pltpu.SemaphoreType.DMA((n, 3)), pltpu.SemaphoreType.DMA((n, 3)),
            pltpu.SemaphoreType.DMA((7,)), pltpu.SemaphoreType.DMA((7,)),
            pltpu.SemaphoreType.DMA((n, 4)), pltpu.SemaphoreType.DMA((n,)),
            pltpu.SemaphoreType.DMA],
        compiler_params=_cparams(),
    )(*parts, small)


def _weights_to_p(gathered):
    tm = 64
    segs = sorted(_segments(), key=lambda s: s[3])

    def body(g_ref, o_ref):
        pieces, pos = [], 0
        for d, c0, n, p0 in segs:
            if p0 > pos:
                pieces.append(jnp.zeros((tm, p0 - pos), F32))
            pieces.append(g_ref[d, :, c0:c0 + n].astype(F32))
            pos = p0 + n
        pieces.append(jnp.zeros((tm, P_TOTAL - pos), F32))
        o_ref[...] = jnp.concatenate(pieces, axis=-1).astype(BF16)

    return pl.pallas_call(
        body, name="weights_to_layout",
        grid=(D_MODEL // tm,),
        in_specs=[pl.BlockSpec((N_DEV, tm, SHARD_COLS), lambda i: (0, i, 0))],
        out_specs=pl.BlockSpec((tm, P_TOTAL), lambda i: (i, 0)),
        out_shape=jax.ShapeDtypeStruct((D_MODEL, P_TOTAL), BF16),
        compiler_params=_cparams(("arbitrary",)),
    )(gathered)


def _grads_to_shards(groups):
    tm = 64
    segs = _segments()

    def body(*refs):
        g_refs, o_ref = refs[:3], refs[3]
        for d in range(N_DEV):
            pieces = []
            for _, c0, n, p0 in sorted([s for s in segs if s[0] == d], key=lambda s: s[1]):
                gi = max(i for i, (off, _) in enumerate(P_GROUPS) if off <= p0)
                lo = p0 - P_GROUPS[gi][0]
                pieces.append(g_refs[gi][:, lo:lo + n])
            o_ref[d] = jnp.concatenate(pieces, axis=-1).astype(BF16)

    return pl.pallas_call(
        body, name="grads_to_shards",
        grid=(D_MODEL // tm,),
        in_specs=[pl.BlockSpec((tm, w), lambda i: (i, 0)) for _, w in P_GROUPS],
        out_specs=pl.BlockSpec((N_DEV, tm, SHARD_COLS), lambda i: (0, i, 0)),
        out_shape=jax.ShapeDtypeStruct((N_DEV, D_MODEL, SHARD_COLS), BF16),
        compiler_params=_cparams(("arbitrary",)),
    )(*groups)


def _inproj(x, g_in, w_p):
    t = x.shape[0]
    tm = min(256, t)
    nj = 3
    tn = P_TOTAL // nj

    def body(x_ref, g_ref, w_ref, proj_ref, h_ref, r_ref):
        xf = x_ref[...]
        r = lax.rsqrt(jnp.mean(xf * xf, axis=-1, keepdims=True) + EPS)
        h = ((xf * r) * g_ref[...]).astype(BF16)
        proj_ref[...] = _dot(h, w_ref[...])

        @pl.when(pl.program_id(0) == 0)
        def _():
            h_ref[...] = h
            r_ref[...] = r

    first = lambda j, i: (jnp.where(j == 0, i, t // tm - 1), 0)
    return pl.pallas_call(
        body, name="inproj",
        grid=(nj, t // tm),
        in_specs=[pl.BlockSpec((tm, D_MODEL), lambda j, i: (i, 0)),
                  pl.BlockSpec((1, D_MODEL), lambda j, i: (0, 0)),
                  pl.BlockSpec((D_MODEL, tn), lambda j, i: (0, j))],
        out_specs=(pl.BlockSpec((tm, tn), lambda j, i: (i, j)),
                   pl.BlockSpec((tm, D_MODEL), first),
                   pl.BlockSpec((tm, 1), first)),
        out_shape=(jax.ShapeDtypeStruct((t, P_TOTAL), F32),
                   jax.ShapeDtypeStruct((t, D_MODEL), BF16),
                   jax.ShapeDtypeStruct((t, 1), F32)),
        compiler_params=_cparams(("arbitrary", "arbitrary")),
    )(x, g_in, w_p)


def _mla_prep(proj, g_q, g_kv, w_uq_p, w_k_p, w_v, w_gate_p, b_gate, rc, rsn, rsp):
    t = proj.shape[0]
    tm = min(256, t)
    hq = MLA_HEADS * HEAD_PAD

    def body(cq_ref, ckv_ref, misc_ref, gq_ref, gkv_ref, wuq_ref, wk_ref, wv_ref, wg_ref, bg_ref,
             c_ref, sn_ref, sp_ref,
             q_ref, k_ref, v_ref, la_ref, pre_ref, cqn_ref, ckvn_ref, rq_ref, rkv_ref, mb_ref):
        c, sn, sp = c_ref[...], sn_ref[...], sp_ref[...]
        cq = cq_ref[:, :MLA_Q_RANK]
        rq = lax.rsqrt(jnp.mean(cq * cq, axis=-1, keepdims=True) + EPS)
        cqn = ((cq * rq) * gq_ref[...]).astype(BF16)
        cqn_ref[...] = cqn
        rq_ref[...] = rq
        qpre = _dot(cqn, wuq_ref[...])
        ckv = ckv_ref[...]
        rkv = lax.rsqrt(jnp.mean(ckv * ckv, axis=-1, keepdims=True) + EPS)
        ckvn = ((ckv * rkv) * gkv_ref[...]).astype(BF16)
        ckvn_ref[...] = ckvn
        rkv_ref[...] = rkv
        kn = _dot(ckvn, wk_ref[...])
        v_ref[...] = _dot(ckvn, wv_ref[...]).astype(BF16)
        misc = misc_ref[...]
        krope = _rope_fwd(misc, c, sn, sp)
        for h in range(MLA_HEADS):
            sl = slice(h * HEAD_PAD, (h + 1) * HEAD_PAD)
            q_ref[:, sl] = _rope_fwd(qpre[:, sl], c, sn, sp).astype(BF16)
            k_ref[:, sl] = (kn[:, sl] + krope).astype(BF16)
        mb_ref[...] = misc.astype(BF16)
        pre = _dot(mb_ref[...], wg_ref[...]) + bg_ref[...]
        pre_ref[...] = pre
        la_ref[...] = (jnp.minimum(pre, 0.0) - jnp.log(1.0 + jnp.exp(-jnp.abs(pre)))) / GLA_GATE_NORM

    row = lambda w: pl.BlockSpec((tm, w), lambda i: (i, 0))
    full = lambda a: pl.BlockSpec(a.shape, lambda i: (0, 0))
    return pl.pallas_call(
        body, name="mla_prep",
        grid=(t // tm,),
        in_specs=[pl.BlockSpec((tm, 512), lambda i: (i, P_CQ // 512)),
                  pl.BlockSpec((tm, MLA_KV_RANK), lambda i: (i, P_CKV // MLA_KV_RANK)),
                  pl.BlockSpec((tm, LANE), lambda i: (i, P_MISC // LANE)),
                  full(g_q), full(g_kv), full(w_uq_p), full(w_k_p), full(w_v), full(w_gate_p),
                  full(b_gate), row(LANE), row(LANE), row(LANE)],
        out_specs=(row(hq), row(hq), row(MLA_WIDTH), row(GLA_DK), row(GLA_DK),
                   row(MLA_Q_RANK), row(MLA_KV_RANK), row(1), row(1), row(LANE)),
        out_shape=(jax.ShapeDtypeStruct((t, hq), BF16), jax.ShapeDtypeStruct((t, hq), BF16),
                   jax.ShapeDtypeStruct((t, MLA_WIDTH), BF16),
                   jax.ShapeDtypeStruct((t, GLA_DK), F32), jax.ShapeDtypeStruct((t, GLA_DK), F32),
                   jax.ShapeDtypeStruct((t, MLA_Q_RANK), BF16),
                   jax.ShapeDtypeStruct((t, MLA_KV_RANK), BF16),
                   jax.ShapeDtypeStruct((t, 1), F32), jax.ShapeDtypeStruct((t, 1), F32),
                   jax.ShapeDtypeStruct((t, LANE), BF16)),
        compiler_params=_cparams(("arbitrary",)),
    )(proj, proj, proj, g_q, g_kv, w_uq_p, w_k_p, w_v, w_gate_p, b_gate, rc, rsn, rsp)


def _attn_masks(tq, t):
    rows = pl.program_id(1) * tq + lax.broadcasted_iota(jnp.int32, (tq, t), 0)
    cols = lax.broadcasted_iota(jnp.int32, (tq, t), 1)
    lane = lax.broadcasted_iota(jnp.int32, (tq, LANE), 1)
    return cols <= rows, lane < MLA_VDIM


def _mla_attn_fwd(q, k, v):
    t = q.shape[0]
    tq = min(256, t)
    scale = MLA_QK ** -0.5

    def body(q_ref, k_ref, v_ref, o_ref, lse_ref):
        causal, low = _attn_masks(tq, t)
        vp = v_ref[...]
        acc = jnp.zeros((tq, LANE), F32)
        for hh in range(2):
            sl = slice(hh * HEAD_PAD, (hh + 1) * HEAD_PAD)
            s = _dot_nt(q_ref[:, sl], k_ref[:, sl]) * scale
            s = jnp.where(causal, s, -jnp.inf)
            m = jnp.max(s, axis=-1, keepdims=True)
            e = jnp.exp(s - m)
            l = jnp.sum(e, axis=-1, keepdims=True)
            o = _dot(e.astype(BF16), vp) / l
            acc = jnp.where(low if hh == 0 else jnp.logical_not(low), o, acc)
            lse_ref[hh] = m + jnp.log(l)
        o_ref[...] = acc

    return pl.pallas_call(
        body, name="mla_attn_fwd",
        grid=(MLA_HEADS // 2, t // tq),
        in_specs=[pl.BlockSpec((tq, 2 * HEAD_PAD), lambda p, i: (i, p)),
                  pl.BlockSpec((t, 2 * HEAD_PAD), lambda p, i: (0, p)),
                  pl.BlockSpec((t, LANE), lambda p, i: (0, p))],
        out_specs=(pl.BlockSpec((tq, LANE), lambda p, i: (i, p)),
                   pl.BlockSpec((2, tq, 1), lambda p, i: (p, i, 0))),
        out_shape=(jax.ShapeDtypeStruct((t, MLA_WIDTH), F32),
                   jax.ShapeDtypeStruct((MLA_HEADS, t, 1), F32)),
        compiler_params=_cparams(("arbitrary", "arbitrary")),
    )(q, k, v)


def _mla_attn_bwd(q, k, v, o, do, lse):
    t = q.shape[0]
    tq = min(256, t)
    scale = MLA_QK ** -0.5

    def body(q_ref, k_ref, v_ref, o_ref, do_ref, lse_ref, dq_ref, dk_ref, dv_ref):
        @pl.when(pl.program_id(1) == 0)
        def _():
            dk_ref[...] = jnp.zeros_like(dk_ref)
            dv_ref[...] = jnp.zeros_like(dv_ref)

        causal, low = _attn_masks(tq, t)
        vp = v_ref[...]
        do_all = do_ref[...]
        o_all = o_ref[...]
        dv_acc = jnp.zeros((t, LANE), F32)
        for hh in range(2):
            sl = slice(hh * HEAD_PAD, (hh + 1) * HEAD_PAD)
            do_h = jnp.where(low if hh == 0 else jnp.logical_not(low), do_all, 0.0)
            dsum = jnp.sum(do_h * o_all, axis=-1, keepdims=True)
            qh = q_ref[:, sl]
            kh = k_ref[:, sl]
            s = _dot_nt(qh, kh) * scale
            p = jnp.where(causal, jnp.exp(s - lse_ref[hh]), 0.0)
            do_b = do_h.astype(BF16)
            dp = _dot_nt(do_b, vp)
            ds = (p * (dp - dsum) * scale).astype(BF16)
            dq_ref[:, sl] = _dot(ds, kh).astype(BF16)
            dk_ref[:, sl] += _dot_tn(ds, qh)
            dv_acc = dv_acc + _dot_tn(p.astype(BF16), do_b)
        dv_ref[...] += dv_acc

    return pl.pallas_call(
        body, name="mla_attn_bwd",
        grid=(MLA_HEADS // 2, t // tq),
        in_specs=[pl.BlockSpec((tq, 2 * HEAD_PAD), lambda p, i: (i, p)),
                  pl.BlockSpec((t, 2 * HEAD_PAD), lambda p, i: (0, p)),
                  pl.BlockSpec((t, LANE), lambda p, i: (0, p)),
                  pl.BlockSpec((tq, LANE), lambda p, i: (i, p)),
                  pl.BlockSpec((tq, LANE), lambda p, i: (i, p)),
                  pl.BlockSpec((2, tq, 1), lambda p, i: (p, i, 0))],
        out_specs=(pl.BlockSpec((tq, 2 * HEAD_PAD), lambda p, i: (i, p)),
                   pl.BlockSpec((t, 2 * HEAD_PAD), lambda p, i: (0, p)),
                   pl.BlockSpec((t, LANE), lambda p, i: (0, p))),
        out_shape=(jax.ShapeDtypeStruct((t, MLA_HEADS * HEAD_PAD), BF16),
                   jax.ShapeDtypeStruct((t, MLA_HEADS * HEAD_PAD), F32),
                   jax.ShapeDtypeStruct((t, MLA_WIDTH), F32)),
        compiler_params=_cparams(("arbitrary", "arbitrary")),
    )(q, k, v, o, do, lse)


def _gla_chunk_terms(q_ref, k_ref, la_ref, h, tri):
    sl = slice(h * GLA_HK, (h + 1) * GLA_HK)
    b = _dot_exact(tri, la_ref[:, sl])
    bl = b[GLA_CHUNK - 1:GLA_CHUNK, :]
    kc = k_ref[:, sl]
    q_in = (q_ref[:, sl] * (GLA_HK ** -0.5)) * jnp.exp(b)
    k_in = kc * jnp.exp(-b)
    k_st = kc * jnp.exp(bl - b)
    return b, bl, q_in, k_in, k_st


def _tri(c, lower):
    r = lax.broadcasted_iota(jnp.int32, (c, c), 0)
    cc = lax.broadcasted_iota(jnp.int32, (c, c), 1)
    return jnp.where(r >= cc if lower else r <= cc, 1.0, 0.0).astype(F32)


def _gla_fwd(proj, log_a):
    t = proj.shape[0]
    c = GLA_CHUNK
    n = t // c

    def body(q_ref, k_ref, v_ref, la_ref, o_ref, sp_ref, st_ref):
        @pl.when(pl.program_id(0) == 0)
        def _():
            st_ref[...] = jnp.zeros_like(st_ref)

        tri = _tri(c, True)
        for h in range(GLA_HEADS):
            _, bl, q_in, k_in, k_st = _gla_chunk_terms(q_ref, k_ref, la_ref, h, tri)
            vs = slice(h * GLA_HV, (h + 1) * GLA_HV)
            vv = v_ref[:, vs].astype(BF16)
            qb = q_in.astype(BF16)
            attn = _dot_nt(qb, k_in.astype(BF16)) * tri
            st = st_ref[h]
            sp_ref[0, h] = st
            o_ref[:, vs] = _dot(attn.astype(BF16), vv) + _dot_nt(qb, st.astype(BF16))
            st_ref[h] = st * jnp.exp(bl) + _dot_tn(vv, k_st.astype(BF16))

    return pl.pallas_call(
        body, name="gla_fwd",
        grid=(n,),
        in_specs=[pl.BlockSpec((c, GLA_DK), lambda i: (i, P_QG // GLA_DK)),
                  pl.BlockSpec((c, GLA_DK), lambda i: (i, P_KG // GLA_DK)),
                  pl.BlockSpec((c, GLA_DV), lambda i: (i, P_VG // GLA_DV)),
                  pl.BlockSpec((c, GLA_DK), lambda i: (i, 0))],
        out_specs=(pl.BlockSpec((c, GLA_DV), lambda i: (i, 0)),
                   pl.BlockSpec((1, GLA_HEADS, GLA_HV, GLA_HK), lambda i: (i, 0, 0, 0))),
        out_shape=(jax.ShapeDtypeStruct((t, GLA_DV), F32),
                   jax.ShapeDtypeStruct((n, GLA_HEADS, GLA_HV, GLA_HK), F32)),
        scratch_shapes=[pltpu.VMEM((GLA_HEADS, GLA_HV, GLA_HK), F32)],
        compiler_params=_cparams(("arbitrary",)),
    )(proj, proj, proj, log_a)


def _gla_bwd(proj, log_a, do, states):
    t = proj.shape[0]
    c = GLA_CHUNK
    n = t // c

    def body(q_ref, k_ref, v_ref, la_ref, do_ref, sp_ref, dg_ref, dla_ref, ds_ref):
        @pl.when(pl.program_id(0) == 0)
        def _():
            ds_ref[...] = jnp.zeros_like(ds_ref)

        tri = _tri(c, True)
        tri_t = _tri(c, False)
        for h in range(GLA_HEADS):
            b, bl, q_in, k_in, k_st = _gla_chunk_terms(q_ref, k_ref, la_ref, h, tri)
            ks_ = slice(h * GLA_HK, (h + 1) * GLA_HK)
            vs = slice(h * GLA_HV, (h + 1) * GLA_HV)
            vv = v_ref[:, vs].astype(BF16)
            do_h = do_ref[:, vs]
            qb, kb, ksb = q_in.astype(BF16), k_in.astype(BF16), k_st.astype(BF16)
            attn = (_dot_nt(qb, kb) * tri).astype(BF16)
            st = sp_ref[0, h]
            dst = ds_ref[h]
            dstb = dst.astype(BF16)
            dattn = (_dot_nt(do_h, vv) * tri).astype(BF16)
            dg_ref[:, P_VG + h * GLA_HV:P_VG + (h + 1) * GLA_HV] = (
                _dot_tn(attn, do_h) + _dot_nt(ksb, dstb)).astype(BF16)
            dq_in = _dot(dattn, kb) + _dot(do_h, st.astype(BF16))
            dk_in = _dot_tn(dattn, qb)
            dk_st = _dot(vv, dstb)
            ebl = jnp.exp(bl)
            d_ebl = jnp.sum(st * dst, axis=0, keepdims=True)
            ds_ref[h] = _dot_tn(do_h, qb) + dst * ebl
            dg_ref[:, P_QG + h * GLA_HK:P_QG + (h + 1) * GLA_HK] = (
                dq_in * (GLA_HK ** -0.5) * jnp.exp(b)).astype(BF16)
            dg_ref[:, P_KG + h * GLA_HK:P_KG + (h + 1) * GLA_HK] = (
                dk_in * jnp.exp(-b) + dk_st * jnp.exp(bl - b)).astype(BF16)
            db = dq_in * q_in - dk_in * k_in - dk_st * k_st
            dbl = jnp.sum(dk_st * k_st, axis=0, keepdims=True) + d_ebl * ebl
            dla_ref[:, ks_] = _dot_exact(tri_t, db) + dbl

    rev = lambda i: n - 1 - i
    gw = P_GROUPS[0][1]
    return pl.pallas_call(
        body, name="gla_bwd",
        grid=(n,),
        in_specs=[pl.BlockSpec((c, GLA_DK), lambda i: (rev(i), P_QG // GLA_DK)),
                  pl.BlockSpec((c, GLA_DK), lambda i: (rev(i), P_KG // GLA_DK)),
                  pl.BlockSpec((c, GLA_DV), lambda i: (rev(i), P_VG // GLA_DV)),
                  pl.BlockSpec((c, GLA_DK), lambda i: (rev(i), 0)),
                  pl.BlockSpec((c, GLA_DV), lambda i: (rev(i), 0)),
                  pl.BlockSpec((1, GLA_HEADS, GLA_HV, GLA_HK), lambda i: (rev(i), 0, 0, 0))],
        out_specs=(pl.BlockSpec((c, gw), lambda i: (rev(i), 0)),
                   pl.BlockSpec((c, GLA_DK), lambda i: (rev(i), 0))),
        out_shape=(jax.ShapeDtypeStruct((t, gw), BF16), jax.ShapeDtypeStruct((t, GLA_DK), F32)),
        scratch_shapes=[pltpu.VMEM((GLA_HEADS, GLA_HV, GLA_HK), F32)],
        compiler_params=_cparams(("arbitrary",)),
    )(proj, proj, proj, log_a, do, states)


def _post(o_mla, proj, o_gla, x, target, g_gla, g_final, w_pm, w_pg, w_o):
    t = x.shape[0]
    tm = min(128, t)
    g0, gw = P_GROUPS[1]

    def body(om_ref, zg_ref, gm_ref, gg_ref, zm_ref, og_ref, x_ref, tg_ref, ggla_ref, gf_ref,
             wpm_ref, wpg_ref, wo_ref,
             dx2_ref, dom_ref, dog_ref, dg_ref,
             mg_ref, um_ref, ug_ref, dym_ref, dyg_ref, loss_ref, dgf_ref, dggla_ref):
        @pl.when(pl.program_id(0) == 0)
        def _():
            loss_ref[...] = jnp.zeros_like(loss_ref)
            dgf_ref[...] = jnp.zeros_like(dgf_ref)
            dggla_ref[...] = jnp.zeros_like(dggla_ref)

        om = om_ref[...]
        zm = zm_ref[...]
        sm = _sigmoid(zm)
        silu_m = zm * sm
        um = (om * silu_m).astype(BF16)
        um_ref[...] = um
        ym = _dot(um, wpm_ref[...])

        ggla = ggla_ref[...]
        zg = zg_ref[...]
        sg = _sigmoid(zg)
        silu_g = zg * sg
        xhat, rstd, on = [], [], []
        for h in range(GLA_HEADS):
            blk = og_ref[:, h * GLA_HV:(h + 1) * GLA_HV]
            r = lax.rsqrt(jnp.mean(blk * blk, axis=-1, keepdims=True) + EPS)
            xhat.append(blk * r)
            rstd.append(r)
            on.append(xhat[h] * ggla)
        on = jnp.concatenate(on, axis=-1)
        ug = (on * silu_g).astype(BF16)
        ug_ref[...] = ug
        yg = _dot(ug, wpg_ref[...])

        sgm = _sigmoid(gm_ref[...])
        sgg = _sigmoid(gg_ref[...])
        merged = (sgm * ym + sgg * yg).astype(BF16)
        mg_ref[...] = merged
        x2 = x_ref[...] + _dot(merged, wo_ref[...])
        gf = gf_ref[...]
        rf = lax.rsqrt(jnp.mean(x2 * x2, axis=-1, keepdims=True) + EPS)
        xh = x2 * rf
        err = xh * gf - tg_ref[...]
        loss_ref[...] += 0.5 * jnp.sum(jnp.mean(err * err, axis=-1, keepdims=True))

        dy = err * (1.0 / D_MODEL)
        dgf_ref[...] += jnp.sum(dy * xh, axis=0, keepdims=True)
        dxh = dy * gf
        dx2 = rf * (dxh - xh * jnp.mean(dxh * xh, axis=-1, keepdims=True))
        dx2_ref[...] = dx2
        dmerged = _dot_nt(dx2.astype(BF16), wo_ref[...])
        dym = (dmerged * sgm).astype(BF16)
        dyg = (dmerged * sgg).astype(BF16)
        dym_ref[...] = dym
        dyg_ref[...] = dyg
        dg_ref[:, P_GMLA - g0:P_GMLA - g0 + D_MODEL] = (dmerged * ym * sgm * (1.0 - sgm)).astype(BF16)
        dg_ref[:, P_GGLA - g0:P_GGLA - g0 + D_MODEL] = (dmerged * yg * sgg * (1.0 - sgg)).astype(BF16)
        dum = _dot_nt(dym, wpm_ref[...])
        dom_ref[...] = dum * silu_m
        dg_ref[:, P_ZMLA - g0:P_ZMLA - g0 + MLA_WIDTH] = (
            dum * om * (sm * (1.0 + zm * (1.0 - sm)))).astype(BF16)
        dug = _dot_nt(dyg, wpg_ref[...])
        dg_ref[:, P_ZGLA - g0:P_ZGLA - g0 + GLA_DV] = (
            dug * on * (sg * (1.0 + zg * (1.0 - sg)))).astype(BF16)
        don = dug * silu_g
        dggla = jnp.zeros((1, GLA_HV), F32)
        for h in range(GLA_HEADS):
            hs = slice(h * GLA_HV, (h + 1) * GLA_HV)
            don_h = don[:, hs]
            dggla = dggla + jnp.sum(don_h * xhat[h], axis=0, keepdims=True)
            dxh_h = don_h * ggla
            dog_ref[:, hs] = (rstd[h] * (dxh_h - xhat[h] * jnp.mean(dxh_h * xhat[h], axis=-1,
                                                                     keepdims=True))).astype(BF16)
        dggla_ref[...] += dggla

    row = lambda w: pl.BlockSpec((tm, w), lambda i: (i, 0))
    pcol = lambda w, off: pl.BlockSpec((tm, w), lambda i: (i, off // w))
    full = lambda a: pl.BlockSpec(a.shape, lambda i: (0, 0))
    sds = jax.ShapeDtypeStruct
    return pl.pallas_call(
        body, name="post_fwd_bwd",
        grid=(t // tm,),
        in_specs=[row(MLA_WIDTH), pcol(GLA_DV, P_ZGLA), pcol(D_MODEL, P_GMLA), pcol(D_MODEL, P_GGLA),
                  pcol(MLA_WIDTH, P_ZMLA), row(GLA_DV), row(D_MODEL), row(D_MODEL),
                  full(g_gla), full(g_final), full(w_pm), full(w_pg), full(w_o)],
        out_specs=(row(D_MODEL), row(MLA_WIDTH), row(GLA_DV), row(gw),
                   row(D_MODEL), row(MLA_WIDTH), row(GLA_DV), row(D_MODEL), row(D_MODEL),
                   pl.BlockSpec((1, LANE), lambda i: (0, 0)),
                   pl.BlockSpec((1, D_MODEL), lambda i: (0, 0)),
                   pl.BlockSpec((1, GLA_HV), lambda i: (0, 0))),
        out_shape=(sds((t, D_MODEL), F32), sds((t, MLA_WIDTH), F32), sds((t, GLA_DV), BF16),
                   sds((t, gw), BF16),
                   sds((t, D_MODEL), BF16), sds((t, MLA_WIDTH), BF16), sds((t, GLA_DV), BF16),
                   sds((t, D_MODEL), BF16), sds((t, D_MODEL), BF16),
                   sds((1, LANE), F32), sds((1, D_MODEL), F32), sds((1, GLA_HV), F32)),
        compiler_params=_cparams(("arbitrary",)),
    )(o_mla, proj, proj, proj, proj, o_gla, x, target, g_gla, g_final, w_pm, w_pg, w_o)


def _mla_prep_bwd(dq, dk, dv, dla, pre, proj, rq, rkv, g_q, g_kv, w_uq_p, w_k_p, w_v, w_gate_p,
                  rc, rsn, rsp):
    t = proj.shape[0]
    tm = min(256, t)
    gw = P_GROUPS[2][1]

    def body(dq_ref, dk_ref, dv_ref, dla_ref, pre_ref, cq_ref, ckv_ref, rq_ref, rkv_ref,
             gq_ref, gkv_ref, wuq_ref, wk_ref, wv_ref, wg_ref, c_ref, sn_ref, sp_ref,
             dg_ref, dqpre_ref, dpre_ref, dgq_ref, dgkv_ref, dbg_ref):
        @pl.when(pl.program_id(0) == 0)
        def _():
            dgq_ref[...] = jnp.zeros_like(dgq_ref)
            dgkv_ref[...] = jnp.zeros_like(dgkv_ref)
            dbg_ref[...] = jnp.zeros_like(dbg_ref)

        c, sn, sp = c_ref[...], sn_ref[...], sp_ref[...]
        dkr = jnp.zeros((tm, LANE), F32)
        for h in range(MLA_HEADS):
            sl = slice(h * HEAD_PAD, (h + 1) * HEAD_PAD)
            dqpre_ref[:, sl] = _rope_bwd(dq_ref[:, sl].astype(F32), c, sn, sp).astype(BF16)
            dkr = dkr + dk_ref[:, sl]
        dcqn = _dot_nt(dqpre_ref[...], wuq_ref[...])
        rq = rq_ref[...]
        xh = cq_ref[:, :MLA_Q_RANK] * rq
        dgq_ref[...] += jnp.sum(dcqn * xh, axis=0, keepdims=True)
        dxh = dcqn * gq_ref[...]
        dcq = rq * (dxh - xh * jnp.mean(dxh * xh, axis=-1, keepdims=True))
        dg_ref[:, :MLA_Q_RANK] = dcq.astype(BF16)
        dg_ref[:, MLA_Q_RANK:512] = jnp.zeros((tm, 512 - MLA_Q_RANK), BF16)

        dckvn = _dot_nt(dk_ref[...].astype(BF16), wk_ref[...]) + \
            _dot_nt(dv_ref[...].astype(BF16), wv_ref[...])
        rkv = rkv_ref[...]
        xh = ckv_ref[...] * rkv
        dgkv_ref[...] += jnp.sum(dckvn * xh, axis=0, keepdims=True)
        dxh = dckvn * gkv_ref[...]
        dg_ref[:, P_CKV - P_CQ:P_CKV - P_CQ + MLA_KV_RANK] = (
            rkv * (dxh - xh * jnp.mean(dxh * xh, axis=-1, keepdims=True))).astype(BF16)

        dpre = dla_ref[...] * (1.0 / GLA_GATE_NORM) * (1.0 - _sigmoid(pre_ref[...]))
        dbg_ref[...] += jnp.sum(dpre, axis=0, keepdims=True)
        dpre = dpre.astype(BF16)
        dpre_ref[...] = dpre
        lane = lax.broadcasted_iota(jnp.int32, (tm, LANE), 1)
        in_kr = jnp.logical_and(lane >= MISC_KR, lane < MISC_KR + MLA_ROPE)
        dmisc = jnp.where(in_kr, _rope_bwd(dkr, c, sn, sp), 0.0) + _dot_nt(dpre, wg_ref[...])
        dg_ref[:, P_MISC - P_CQ:P_MISC - P_CQ + LANE] = dmisc.astype(BF16)

    hq = MLA_HEADS * HEAD_PAD
    row = lambda w: pl.BlockSpec((tm, w), lambda i: (i, 0))
    full = lambda a: pl.BlockSpec(a.shape, lambda i: (0, 0))
    acc = lambda w: pl.BlockSpec((1, w), lambda i: (0, 0))
    sds = jax.ShapeDtypeStruct
    return pl.pallas_call(
        body, name="mla_prep_bwd",
        grid=(t // tm,),
        in_specs=[row(hq), row(hq), row(MLA_WIDTH), row(GLA_DK), row(GLA_DK),
                  pl.BlockSpec((tm, 512), lambda i: (i, P_CQ // 512)),
                  pl.BlockSpec((tm, MLA_KV_RANK), lambda i: (i, P_CKV // MLA_KV_RANK)),
                  row(1), row(1), full(g_q), full(g_kv), full(w_uq_p), full(w_k_p), full(w_v),
                  full(w_gate_p), row(LANE), row(LANE), row(LANE)],
        out_specs=(row(gw), row(hq), row(GLA_DK),
                   acc(MLA_Q_RANK), acc(MLA_KV_RANK), acc(GLA_DK)),
        out_shape=(sds((t, gw), BF16), sds((t, hq), BF16), sds((t, GLA_DK), BF16),
                   sds((1, MLA_Q_RANK), F32), sds((1, MLA_KV_RANK), F32), sds((1, GLA_DK), F32)),
        compiler_params=_cparams(("arbitrary",)),
    )(dq, dk, dv, dla, pre, proj, proj, rq, rkv, g_q, g_kv, w_uq_p, w_k_p, w_v, w_gate_p,
      rc, rsn, rsp)


def _inproj_bwd(dgroups, w_p, x, rstd, g_in, dx2):
    t = x.shape[0]
    tm = min(256, t)

    def body(d0_ref, d1_ref, d2_ref, w_ref, x_ref, r_ref, g_ref, dx2_ref, dx_ref, dg_ref):
        @pl.when(pl.program_id(0) == 0)
        def _():
            dg_ref[...] = jnp.zeros_like(dg_ref)

        dh = jnp.zeros((tm, D_MODEL), F32)
        for d_ref, (off, width) in zip((d0_ref, d1_ref, d2_ref), P_GROUPS):
            dh = dh + _dot_nt(d_ref[...], w_ref[:, off:off + width])
        r = r_ref[...]
        xh = x_ref[...] * r
        dg_ref[...] += jnp.sum(dh * xh, axis=0, keepdims=True)
        dxh = dh * g_ref[...]
        dx_ref[...] = dx2_ref[...] + r * (dxh - xh * jnp.mean(dxh * xh, axis=-1, keepdims=True))

    row = lambda w: pl.BlockSpec((tm, w), lambda i: (i, 0))
    return pl.pallas_call(
        body, name="inproj_bwd",
        grid=(t // tm,),
        in_specs=[row(P_GROUPS[0][1]), row(P_GROUPS[1][1]), row(P_GROUPS[2][1]),
                  pl.BlockSpec((D_MODEL, P_TOTAL), lambda i: (0, 0)),
                  row(D_MODEL), row(1), pl.BlockSpec((1, D_MODEL), lambda i: (0, 0)), row(D_MODEL)],
        out_specs=(row(D_MODEL), pl.BlockSpec((1, D_MODEL), lambda i: (0, 0))),
        out_shape=(jax.ShapeDtypeStruct((t, D_MODEL), F32),
                   jax.ShapeDtypeStruct((1, D_MODEL), F32)),
        compiler_params=_cparams(("arbitrary",)),
    )(*dgroups, w_p, x, rstd, g_in, dx2)


def _matmul(name, a, b, tm, tn, dtype=F32):
    kk, m = a.shape
    n = b.shape[1]

    def body(a_ref, b_ref, o_ref):
        o_ref[...] = _dot_tn(a_ref[...].astype(BF16), b_ref[...].astype(BF16)).astype(dtype)

    return pl.pallas_call(
        body, name=name,
        grid=(n // tn, m // tm),
        in_specs=[pl.BlockSpec((kk, tm), lambda j, i: (0, i)),
                  pl.BlockSpec((kk, tn), lambda j, i: (0, j))],
        out_specs=pl.BlockSpec((tm, tn), lambda j, i: (i, j)),
        out_shape=jax.ShapeDtypeStruct((m, n), dtype),
        compiler_params=_cparams(("arbitrary", "arbitrary")),
    )(a, b)


def _adamw_update(p_ref, w_ref, m_ref, v_ref, g_ref, d_ref, nm_ref, nv_ref):
    g = p_ref[0].astype(F32)
    for q in range(1, p_ref.shape[0]):
        g = g + p_ref[q].astype(F32)
    m_new = ADAM_B1 * m_ref[...] + (1.0 - ADAM_B1) * g
    v_new = ADAM_B2 * v_ref[...] + (1.0 - ADAM_B2) * (g * g)
    m_hat = m_new / (1.0 - ADAM_B1 ** ADAM_STEP)
    v_hat = v_new / (1.0 - ADAM_B2 ** ADAM_STEP)
    g_ref[...] = g
    nm_ref[...] = m_new
    nv_ref[...] = v_new
    d_ref[...] = -ADAM_LR * (m_hat / (jnp.sqrt(v_hat) + ADAM_EPS) + ADAM_WD * w_ref[...])


def _adamw_rows(name, parts, w, m, v, tr):
    _, rows, cols = w.shape

    def body(*refs):
        _adamw_update(*refs)

    blk = pl.BlockSpec((None, tr, cols), lambda i: (0, i, 0))
    out = jax.ShapeDtypeStruct((1, rows, cols), F32)
    return pl.pallas_call(
        body, name=name,
        grid=(rows // tr,),
        in_specs=[pl.BlockSpec((parts.shape[0], tr, cols), lambda i: (0, i, 0)), blk, blk, blk],
        out_specs=(blk, blk, blk, blk),
        out_shape=(out, out, out, out),
        compiler_params=_cparams(("arbitrary",)),
    )(parts, w, m, v)


def _adamw_group(parts, ws, ms, vs):
    n = len(ws)

    def body(*refs):
        ins, outs = refs[:4 * n], refs[4 * n:]
        for a in range(n):
            _adamw_update(ins[a], *[r.at[0] for r in (ins[n + a], ins[2 * n + a], ins[3 * n + a])],
                          *[r.at[0] for r in outs[4 * a:4 * a + 4]])

    vmem = lambda k: [pl.BlockSpec(memory_space=pltpu.VMEM) for _ in range(k)]
    out_shape = []
    for w in ws:
        out_shape += [jax.ShapeDtypeStruct(w.shape, F32)] * 4
    res = pl.pallas_call(
        body, name="adamw_small_weights",
        in_specs=vmem(4 * n), out_specs=tuple(vmem(4 * n)), out_shape=tuple(out_shape),
        compiler_params=_cparams(),
    )(*parts, *ws, *ms, *vs)
    return [res[4 * a:4 * a + 4] for a in range(n)]


def _rope_tables(positions):
    half = MLA_ROPE // 2
    freqs = ROPE_THETA ** (-jnp.arange(half, dtype=F32) / half)
    ang = positions.astype(F32).reshape(-1, 1) * freqs
    cos, sin = jnp.cos(ang), jnp.sin(ang)
    t = ang.shape[0]
    one, zero = jnp.ones((t, MLA_NOPE), F32), jnp.zeros((t, half), F32)
    tail = jnp.zeros((t, LANE - MLA_QK), F32)
    rc = jnp.concatenate([one, cos, cos, tail], axis=1)
    rsn = jnp.concatenate([0.0 * one, -sin, zero, tail], axis=1)
    rsp = jnp.concatenate([0.0 * one, zero, sin, tail], axis=1)
    return rc, rsn, rsp


def _cols_full(g):
    return g.transpose(1, 0, 2)


def kernel(x, positions, g_in, w_in, g_q, w_uq, g_kv, w_ukv, w_gla_gate, b_gla_gate, g_gla, w_proj_mla, w_proj_gla, w_out, g_final, loss_target, m_g_in, m_w_in, m_g_q, m_w_uq, m_g_kv, m_w_ukv, m_w_gla_gate, m_b_gla_gate, m_g_gla, m_w_proj_mla, m_w_proj_gla, m_w_out, m_g_final, v_g_in, v_w_in, v_g_q, v_w_uq, v_g_kv, v_w_ukv, v_w_gla_gate, v_b_gla_gate, v_g_gla, v_w_proj_mla, v_w_proj_gla, v_w_out, v_g_final):
    t = x.shape[1]
    x2d = x.reshape(t, D_MODEL)
    tgt = loss_target.reshape(t, D_MODEL)
    g_final2 = g_final.reshape(1, D_MODEL)
    sharded = [(w_in, m_w_in, v_w_in), (w_uq, m_w_uq, v_w_uq), (w_ukv, m_w_ukv, v_w_ukv),
               (w_gla_gate, m_w_gla_gate, v_w_gla_gate), (w_proj_mla, m_w_proj_mla, v_w_proj_mla),
               (w_proj_gla, m_w_proj_gla, v_w_proj_gla), (w_out, m_w_out, v_w_out)]

    g_w_in, g_uq, g_ukv, g_gate, g_pm, g_pg, g_o = _all_gather(
        [s[0][0].astype(BF16) for s in sharded])
    w_in_p = _weights_to_p(g_w_in)
    w_uq_p = jnp.pad(_cols_full(g_uq), ((0, 0), (0, 0), (0, HEAD_PAD - MLA_QK))).reshape(
        MLA_Q_RANK, MLA_HEADS * HEAD_PAD)
    ukv = _cols_full(g_ukv)
    w_k_p = jnp.pad(ukv[:, :, :MLA_NOPE], ((0, 0), (0, 0), (0, HEAD_PAD - MLA_NOPE))).reshape(
        MLA_KV_RANK, MLA_HEADS * HEAD_PAD)
    w_v = ukv[:, :, MLA_NOPE:].reshape(MLA_KV_RANK, MLA_WIDTH)
    w_gate_p = jnp.pad(_cols_full(g_gate).reshape(GLA_GATE_RANK, GLA_DK),
                       ((MISC_ALR, LANE - MISC_ALR - GLA_GATE_RANK), (0, 0)))
    w_pm = _cols_full(g_pm).reshape(MLA_WIDTH, D_MODEL)
    w_pg = g_pg.reshape(GLA_DV, D_MODEL)
    w_o = g_o.reshape(D_MODEL, D_MODEL)
    rc, rsn, rsp = _rope_tables(positions)

    proj, h, rstd = _inproj(x2d, g_in, w_in_p)
    q, k, v, log_a, pre, cqn, ckvn, rq, rkv, misc = _mla_prep(
        proj, g_q, g_kv, w_uq_p, w_k_p, w_v, w_gate_p, b_gla_gate, rc, rsn, rsp)
    o_mla, lse = _mla_attn_fwd(q, k, v)
    o_gla, states = _gla_fwd(proj, log_a)

    (dx2, do_mla, do_gla, d_out, merged, um, ug, dym, dyg, loss_p, dg_final,
     dg_gla) = _post(o_mla, proj, o_gla, x2d, tgt, g_gla, g_final2, w_pm, w_pg, w_o)

    dq, dk, dv = _mla_attn_bwd(q, k, v, o_mla, do_mla, lse)
    d_gla, dla = _gla_bwd(proj, log_a, do_gla, states)
    d_lat, dqpre, dpre, dg_q, dg_kv, db_gate = _mla_prep_bwd(
        dq, dk, dv, dla, pre, proj, rq, rkv, g_q, g_kv, w_uq_p, w_k_p, w_v, w_gate_p, rc, rsn, rsp)

    dgroups = (d_gla, d_out, d_lat)
    grad_x, dg_in = _inproj_bwd(dgroups, w_in_p, x2d, rstd, g_in, dx2)

    dw_groups = [_matmul("dw_in_%d" % i, h, dgroups[i], 512, tn)
                 for i, tn in enumerate((512, 512, 896))]
    p_in = _grads_to_shards(dw_groups)
    dw_uq = _matmul("dw_uq", cqn, dqpre, MLA_Q_RANK, 512, BF16)
    p_uq = dw_uq.reshape(MLA_Q_RANK, MLA_HEADS, HEAD_PAD)[:, :, :MLA_QK].transpose(1, 0, 2)
    dw_k = _matmul("dw_uk", ckvn, dk, MLA_KV_RANK, 512, BF16)
    dw_v = _matmul("dw_uv", ckvn, dv, MLA_KV_RANK, 512, BF16)
    p_ukv = jnp.concatenate(
        [dw_k.reshape(MLA_KV_RANK, MLA_HEADS, HEAD_PAD)[:, :, :MLA_NOPE],
         dw_v.reshape(MLA_KV_RANK, MLA_HEADS, MLA_VDIM)], axis=2).transpose(1, 0, 2)
    dw_gate = _matmul("dw_gate", misc, dpre, LANE, 512, BF16)
    p_gate = dw_gate[MISC_ALR:MISC_ALR + GLA_GATE_RANK].reshape(
        GLA_GATE_RANK, N_DEV, GLA_DK // N_DEV).transpose(1, 0, 2)
    p_pm = _matmul("dw_proj_mla", um, dym, 512, 512, BF16).reshape(
        MLA_WIDTH, N_DEV, D_MODEL // N_DEV).transpose(1, 0, 2)
    p_pg = _matmul("dw_proj_gla", ug, dyg, 512, 512, BF16).reshape(N_DEV, -1, D_MODEL)
    p_o = _matmul("dw_out", merged, dx2, 512, 512, BF16).reshape(N_DEV, -1, D_MODEL)
    small = jnp.concatenate([dg_in.reshape(-1), dg_q.reshape(-1), dg_kv.reshape(-1),
                             db_gate.reshape(-1), dg_gla.reshape(-1), dg_final.reshape(-1),
                             loss_p[0, :1]])
    small = jnp.pad(small, (0, SMALL_ROWS * LANE - small.shape[0])).reshape(SMALL_ROWS, LANE)

    recv = _exchange_grads([p_in, p_uq, p_ukv, p_gate, p_pm, p_pg, p_o], small)
    big = [_adamw_rows("adamw_w_in", recv[0], *sharded[0], 128)]
    big += _adamw_group(recv[1:7], *[[s[j] for s in sharded[1:]] for j in range(3)])
    replicated = [(g_in, m_g_in, v_g_in), (g_q, m_g_q, v_g_q), (g_kv, m_g_kv, v_g_kv),
                  (b_gla_gate, m_b_gla_gate, v_b_gla_gate), (g_gla, m_g_gla, v_g_gla),
                  (g_final, m_g_final, v_g_final)]
    spacks = [jnp.pad(jnp.concatenate([s[j].reshape(-1) for s in replicated]),
                      (0, SMALL_ROWS * LANE - sum(SMALL_SIZES))).reshape(1, SMALL_ROWS, LANE)
              for j in range(3)]
    tiny = _adamw_rows("adamw_gains", recv[7], spacks[0], spacks[1], spacks[2], SMALL_ROWS)

    outs = {}
    names = ("w_in", "w_uq", "w_ukv", "w_gla_gate", "w_proj_mla", "w_proj_gla", "w_out")
    for j, kind in enumerate(("grad", "delta", "new_m", "new_v")):
        for name, res in zip(names, big):
            outs[kind, name] = res[j]
        flat = tiny[j].reshape(-1)
        off = 0
        for name, size in zip(("g_in", "g_q", "g_kv", "b_gla_gate", "g_gla", "g_final"), SMALL_SIZES):
            shape = (size,) if name == "g_final" else (1, size)
            outs[kind, name] = flat[off:off + size].reshape(shape)
            off += size
    loss = tiny[0].reshape(-1)[sum(SMALL_SIZES)]
    order = ("g_in", "w_in", "g_q", "w_uq", "g_kv", "w_ukv", "w_gla_gate", "b_gla_gate", "g_gla",
             "w_proj_mla", "w_proj_gla", "w_out", "g_final")
    result = [loss, grad_x.reshape(1, t, D_MODEL)]
    for kind in ("grad", "delta", "new_m", "new_v"):
        result += [outs[kind, name] for name in order]
    return tuple(result)
```

```python
import math

import jax
import jax.numpy as jnp
from jax import lax
from jax.experimental import pallas as pl
from jax.experimental.pallas import tpu as pltpu

F32 = jnp.float32
BF16 = jnp.bfloat16
MESH = pl.DeviceIdType.MESH
N_DEV = 8

D_MODEL = 1024
EPS = 1e-6
MLA_HEADS = 8
MLA_NOPE = 64
MLA_ROPE = 32
MLA_VDIM = 64
MLA_Q_RANK = 384
MLA_KV_RANK = 256
MLA_QK = MLA_NOPE + MLA_ROPE
MLA_WIDTH = MLA_HEADS * MLA_VDIM
ROPE_THETA = 10000.0
GLA_HEADS = 4
GLA_DK = 512
GLA_DV = 1024
GLA_HK = 128
GLA_HV = 256
GLA_GATE_RANK = 16
GLA_GATE_NORM = 16.0
GLA_CHUNK = 64
D_IN = 6320

ADAM_LR = 0.001
ADAM_B1 = 0.9
ADAM_B2 = 0.999
ADAM_EPS = 1e-08
ADAM_WD = 0.01
ADAM_STEP = 10

LANE = 128
HEAD_PAD = 128
VMEM_LIMIT = 48 * 1024 * 1024

P_VG, P_QG, P_KG = 0, 1024, 1536
P_ZGLA, P_GMLA, P_GGLA, P_ZMLA = 2048, 3072, 4096, 5120
P_CQ, P_CKV, P_MISC = 5632, 6144, 6400
P_TOTAL = 6528
P_GROUPS = ((0, 2048), (2048, 3584), (5632, 896))
MISC_KR = 64
MISC_ALR = 96
SHARD_COLS = D_IN // N_DEV
P_COMPONENTS = ((0, 384, P_CQ), (384, 256, P_CKV), (640, 32, P_MISC + MISC_KR), (672, 512, P_ZMLA),
                (1184, 512, P_QG), (1696, 512, P_KG), (2208, 1024, P_VG),
                (3232, 16, P_MISC + MISC_ALR), (3248, 1024, P_ZGLA), (4272, 1024, P_GMLA),
                (5296, 1024, P_GGLA))

SMALL_SIZES = (1024, 384, 256, 512, 256, 1024)
SMALL_ROWS = 32


def _segments():
    segs = []
    for g0, n, p0 in P_COMPONENTS:
        g = g0
        while g < g0 + n:
            d = g // SHARD_COLS
            end = min(g0 + n, (d + 1) * SHARD_COLS)
            segs.append((d, g - d * SHARD_COLS, end - g, p0 + g - g0))
            g = end
    return segs


def _cparams(sem=None):
    if sem is None:
        return pltpu.CompilerParams(vmem_limit_bytes=VMEM_LIMIT)
    return pltpu.CompilerParams(dimension_semantics=sem, vmem_limit_bytes=VMEM_LIMIT)


def _sigmoid(v):
    return 1.0 / (1.0 + jnp.exp(-v))


def _dot(a, b):
    return jnp.dot(a, b, preferred_element_type=F32)


def _dot_nt(a, b):
    return lax.dot_general(a, b, (((1,), (1,)), ((), ())), preferred_element_type=F32)


def _dot_tn(a, b):
    return lax.dot_general(a, b, (((0,), (0,)), ((), ())), preferred_element_type=F32)


def _dot_exact(a, b):
    return jnp.dot(a, b, preferred_element_type=F32, precision=lax.Precision.HIGHEST)


def _rope_fwd(blk, c, sn, sp):
    return blk * c + pltpu.roll(blk, LANE - 16, 1) * sn + pltpu.roll(blk, 16, 1) * sp


def _rope_bwd(blk, c, sn, sp):
    return blk * c + pltpu.roll(blk * sn, 16, 1) + pltpu.roll(blk * sp, LANE - 16, 1)


def _mesh_pos():
    return lax.axis_index("x"), lax.axis_index("y"), lax.axis_index("c")


def _hbm_specs(n):
    return [pl.BlockSpec(memory_space=pltpu.HBM) for _ in range(n)]


def _all_gather(shards):
    n = len(shards)

    def body(*refs):
        x_refs, out_refs = refs[:n], refs[n:2 * n]
        send_sems, recv_sems, local_sems = refs[2 * n:]
        x, y, c = _mesh_pos()
        me, sibling = (x, y, c), (x, y, 1 - c)
        chips = [(1 - x, y), (x, 1 - y), (1 - x, 1 - y)]

        def slot(a, px, py, pc):
            return out_refs[a].at[4 * px + 2 * py + pc]

        def copies(k, block, to, own=False):
            return [pltpu.make_async_remote_copy(
                src_ref=x_refs[a] if own else slot(a, *block), dst_ref=slot(a, *block),
                send_sem=send_sems.at[k, a], recv_sem=recv_sems.at[k, a],
                device_id=to, device_id_type=MESH) for a in range(n)]

        mine = [pltpu.make_async_copy(x_refs[a], slot(a, *me), local_sems.at[a]) for a in range(n)]
        for cp in mine:
            cp.start()
        first = copies(0, me, sibling, own=True)
        for j, chip in enumerate(chips):
            first += copies(1 + j, me, (*chip, c), own=True)
        for cp in first:
            cp.start()
        passed = []
        for j, chip in enumerate(chips):
            for cp in copies(1 + j, (*chip, c), me):
                cp.wait_recv()
            fwd = copies(4 + j, (*chip, c), sibling)
            for cp in fwd:
                cp.start()
            passed += fwd
        for cp in copies(0, sibling, me):
            cp.wait_recv()
        for j, chip in enumerate(chips):
            for cp in copies(4 + j, (*chip, 1 - c), me):
                cp.wait_recv()
        for cp in first + passed:
            cp.wait_send()
        for cp in mine:
            cp.wait()

    return pl.pallas_call(
        body, name="all_gather_weights",
        out_shape=tuple(jax.ShapeDtypeStruct((N_DEV,) + s.shape, s.dtype) for s in shards),
        in_specs=_hbm_specs(n), out_specs=tuple(_hbm_specs(n)),
        scratch_shapes=[pltpu.SemaphoreType.DMA((7, n)), pltpu.SemaphoreType.DMA((7, n)),
                        pltpu.SemaphoreType.DMA((n,))],
    )(*shards)


def _exchange_grads(parts, small):
    n = len(parts)
    rows_per_add = 256

    def body(*refs):
        p_refs, s_ref = refs[:n], refs[n]
        out_refs, sout_ref = refs[n + 1:2 * n + 1], refs[2 * n + 1]
        mine_v, recv_v = refs[2 * n + 2:3 * n + 2], refs[3 * n + 2:4 * n + 2]
        (d2d_send, d2d_recv, ici_send, ici_recv, s_send, s_recv, load_sems, own_sems,
         sown_sem) = refs[4 * n + 2:]
        x, y, c = _mesh_pos()
        me = 4 * x + 2 * y + c
        chips = [(x, y), (1 - x, y), (x, 1 - y), (1 - x, 1 - y)]

        sown = pltpu.make_async_copy(s_ref, sout_ref.at[me], sown_sem)
        sown.start()
        tiny = []
        for k in range(1, N_DEV):
            peer = (x ^ (k >> 2), y ^ ((k >> 1) & 1), c ^ (k & 1))
            tiny.append(pltpu.make_async_remote_copy(
                src_ref=s_ref, dst_ref=sout_ref.at[me], send_sem=s_send.at[k - 1],
                recv_sem=s_recv.at[k - 1], device_id=peer, device_id_type=MESH))
        for cp in tiny:
            cp.start()

        loads, swaps = [], []
        for a in range(n):
            for j, (px, py) in enumerate(chips):
                loads.append(pltpu.make_async_copy(
                    p_refs[a].at[4 * px + 2 * py + c], mine_v[a].at[j], load_sems.at[a, j]))
                swaps.append(pltpu.make_async_remote_copy(
                    src_ref=p_refs[a].at[4 * px + 2 * py + 1 - c], dst_ref=recv_v[a].at[j],
                    send_sem=d2d_send.at[a, j], recv_sem=d2d_recv.at[a, j],
                    device_id=(x, y, 1 - c), device_id_type=MESH))
        for cp in swaps + loads:
            cp.start()

        outgoing, own = [], []
        for a in range(n):
            rows = p_refs[a].shape[1]
            step = math.gcd(rows, rows_per_add)
            for j in range(4):
                loads[4 * a + j].wait()
                swaps[4 * a + j].wait_recv()

                @pl.loop(0, rows // step)
                def _(i):
                    rs = pl.ds(pl.multiple_of(i * step, step), step)
                    mine_v[a][j, rs, :] = (mine_v[a][j, rs, :].astype(F32)
                                           + recv_v[a][j, rs, :].astype(F32)).astype(BF16)

                if j == 0:
                    own.append(pltpu.make_async_copy(mine_v[a].at[0], out_refs[a].at[0],
                                                     own_sems.at[a]))
                    own[-1].start()
                else:
                    outgoing.append(pltpu.make_async_remote_copy(
                        src_ref=mine_v[a].at[j], dst_ref=out_refs[a].at[j],
                        send_sem=ici_send.at[a, j - 1], recv_sem=ici_recv.at[a, j - 1],
                        device_id=(*chips[j], c), device_id_type=MESH))
                    outgoing[-1].start()

        for cp in tiny + outgoing:
            cp.wait_recv()
        for cp in tiny + outgoing + swaps:
            cp.wait_send()
        for cp in own:
            cp.wait()
        sown.wait()

    outs = [jax.ShapeDtypeStruct((4,) + p.shape[1:], p.dtype) for p in parts]
    outs.append(jax.ShapeDtypeStruct((N_DEV,) + small.shape, small.dtype))
    stage = [pltpu.VMEM((4,) + p.shape[1:], p.dtype) for p in parts]
    return pl.pallas_call(
        body, name="exchange_grads",
        out_shape=tuple(outs),
        in_specs=_hbm_specs(n + 1), out_specs=tuple(_hbm_specs(n + 1)),
        scratch_shapes=stage + stage + [
            pltpu.SemaphoreType.DMA((n, 4)), pltpu.SemaphoreType.DMA((n, 4)),
            pltpu.SemaphoreType.DMA((n, 3)), pltpu.SemaphoreType.DMA((n, 3)),
            pltpu.SemaphoreType.DMA((7,)), pltpu.SemaphoreType.DMA((7,)),
            pltpu.SemaphoreType.DMA((n, 4)), pltpu.SemaphoreType.DMA((n,)),
            pltpu.SemaphoreType.DMA],
        compiler_params=_cparams(),
    )(*parts, small)


def _weights_to_p(gathered):
    tm = 64
    segs = sorted(_segments(), key=lambda s: s[3])

    def body(g_ref, o_ref):
        pieces, pos = [], 0
        for d, c0, n, p0 in segs:
            if p0 > pos:
                pieces.append(jnp.zeros((tm, p0 - pos), F32))
            pieces.append(g_ref[d, :, c0:c0 + n].astype(F32))
            pos = p0 + n
        pieces.append(jnp.zeros((tm, P_TOTAL - pos), F32))
        o_ref[...] = jnp.concatenate(pieces, axis=-1).astype(BF16)

    return pl.pallas_call(
        body, name="weights_to_layout",
        grid=(D_MODEL // tm,),
        in_specs=[pl.BlockSpec((N_DEV, tm, SHARD_COLS), lambda i: (0, i, 0))],
        out_specs=pl.BlockSpec((tm, P_TOTAL), lambda i: (i, 0)),
        out_shape=jax.ShapeDtypeStruct((D_MODEL, P_TOTAL), BF16),
        compiler_params=_cparams(("arbitrary",)),
    )(gathered)


def _grads_to_shards(groups):
    tm = 64
    segs = _segments()

    def body(*refs):
        g_refs, o_ref = refs[:3], refs[3]
        for d in range(N_DEV):
            pieces = []
            for _, c0, n, p0 in sorted([s for s in segs if s[0] == d], key=lambda s: s[1]):
                gi = max(i for i, (off, _) in enumerate(P_GROUPS) if off <= p0)
                lo = p0 - P_GROUPS[gi][0]
                pieces.append(g_refs[gi][:, lo:lo + n])
            o_ref[d] = jnp.concatenate(pieces, axis=-1).astype(BF16)

    return pl.pallas_call(
        body, name="grads_to_shards",
        grid=(D_MODEL // tm,),
        in_specs=[pl.BlockSpec((tm, w), lambda i: (i, 0)) for _, w in P_GROUPS],
        out_specs=pl.BlockSpec((N_DEV, tm, SHARD_COLS), lambda i: (0, i, 0)),
        out_shape=jax.ShapeDtypeStruct((N_DEV, D_MODEL, SHARD_COLS), BF16),
        compiler_params=_cparams(("arbitrary",)),
    )(*groups)


def _inproj(x, g_in, w_p):
    t = x.shape[0]
    tm = min(256, t)
    nj = 3
    tn = P_TOTAL // nj

    def body(x_ref, g_ref, w_ref, proj_ref, h_ref, r_ref):
        xf = x_ref[...]
        r = lax.rsqrt(jnp.mean(xf * xf, axis=-1, keepdims=True) + EPS)
        h = ((xf * r) * g_ref[...]).astype(BF16)
        proj_ref[...] = _dot(h, w_ref[...])

        @pl.when(pl.program_id(0) == 0)
        def _():
            h_ref[...] = h
            r_ref[...] = r

    first = lambda j, i: (jnp.where(j == 0, i, t // tm - 1), 0)
    return pl.pallas_call(
        body, name="inproj",
        grid=(nj, t // tm),
        in_specs=[pl.BlockSpec((tm, D_MODEL), lambda j, i: (i, 0)),
                  pl.BlockSpec((1, D_MODEL), lambda j, i: (0, 0)),
                  pl.BlockSpec((D_MODEL, tn), lambda j, i: (0, j))],
        out_specs=(pl.BlockSpec((tm, tn), lambda j, i: (i, j)),
                   pl.BlockSpec((tm, D_MODEL), first),
                   pl.BlockSpec((tm, 1), first)),
        out_shape=(jax.ShapeDtypeStruct((t, P_TOTAL), F32),
                   jax.ShapeDtypeStruct((t, D_MODEL), BF16),
                   jax.ShapeDtypeStruct((t, 1), F32)),
        compiler_params=_cparams(("arbitrary", "arbitrary")),
    )(x, g_in, w_p)


def _mla_prep(proj, g_q, g_kv, w_uq_p, w_k_p, w_v, w_gate_p, b_gate, rc, rsn, rsp):
    t = proj.shape[0]
    tm = min(256, t)
    hq = MLA_HEADS * HEAD_PAD

    def body(cq_ref, ckv_ref, misc_ref, gq_ref, gkv_ref, wuq_ref, wk_ref, wv_ref, wg_ref, bg_ref,
             c_ref, sn_ref, sp_ref,
             q_ref, k_ref, v_ref, la_ref, pre_ref, cqn_ref, ckvn_ref, rq_ref, rkv_ref, mb_ref):
        c, sn, sp = c_ref[...], sn_ref[...], sp_ref[...]
        cq = cq_ref[:, :MLA_Q_RANK]
        rq = lax.rsqrt(jnp.mean(cq * cq, axis=-1, keepdims=True) + EPS)
        cqn = ((cq * rq) * gq_ref[...]).astype(BF16)
        cqn_ref[...] = cqn
        rq_ref[...] = rq
        qpre = _dot(cqn, wuq_ref[...])
        ckv = ckv_ref[...]
        rkv = lax.rsqrt(jnp.mean(ckv * ckv, axis=-1, keepdims=True) + EPS)
        ckvn = ((ckv * rkv) * gkv_ref[...]).astype(BF16)
        ckvn_ref[...] = ckvn
        rkv_ref[...] = rkv
        kn = _dot(ckvn, wk_ref[...])
        v_ref[...] = _dot(ckvn, wv_ref[...]).astype(BF16)
        misc = misc_ref[...]
        krope = _rope_fwd(misc, c, sn, sp)
        for h in range(MLA_HEADS):
            sl = slice(h * HEAD_PAD, (h + 1) * HEAD_PAD)
            q_ref[:, sl] = _rope_fwd(qpre[:, sl], c, sn, sp).astype(BF16)
            k_ref[:, sl] = (kn[:, sl] + krope).astype(BF16)
        mb_ref[...] = misc.astype(BF16)
        pre = _dot(mb_ref[...], wg_ref[...]) + bg_ref[...]
        pre_ref[...] = pre
        la_ref[...] = (jnp.minimum(pre, 0.0) - jnp.log(1.0 + jnp.exp(-jnp.abs(pre)))) / GLA_GATE_NORM

    row = lambda w: pl.BlockSpec((tm, w), lambda i: (i, 0))
    full = lambda a: pl.BlockSpec(a.shape, lambda i: (0, 0))
    return pl.pallas_call(
        body, name="mla_prep",
        grid=(t // tm,),
        in_specs=[pl.BlockSpec((tm, 512), lambda i: (i, P_CQ // 512)),
                  pl.BlockSpec((tm, MLA_KV_RANK), lambda i: (i, P_CKV // MLA_KV_RANK)),
                  pl.BlockSpec((tm, LANE), lambda i: (i, P_MISC // LANE)),
                  full(g_q), full(g_kv), full(w_uq_p), full(w_k_p), full(w_v), full(w_gate_p),
                  full(b_gate), row(LANE), row(LANE), row(LANE)],
        out_specs=(row(hq), row(hq), row(MLA_WIDTH), row(GLA_DK), row(GLA_DK),
                   row(MLA_Q_RANK), row(MLA_KV_RANK), row(1), row(1), row(LANE)),
        out_shape=(jax.ShapeDtypeStruct((t, hq), BF16), jax.ShapeDtypeStruct((t, hq), BF16),
                   jax.ShapeDtypeStruct((t, MLA_WIDTH), BF16),
                   jax.ShapeDtypeStruct((t, GLA_DK), F32), jax.ShapeDtypeStruct((t, GLA_DK), F32),
                   jax.ShapeDtypeStruct((t, MLA_Q_RANK), BF16),
                   jax.ShapeDtypeStruct((t, MLA_KV_RANK), BF16),
                   jax.ShapeDtypeStruct((t, 1), F32), jax.ShapeDtypeStruct((t, 1), F32),
                   jax.ShapeDtypeStruct((t, LANE), BF16)),
        compiler_params=_cparams(("arbitrary",)),
    )(proj, proj, proj, g_q, g_kv, w_uq_p, w_k_p, w_v, w_gate_p, b_gate, rc, rsn, rsp)


def _attn_masks(tq, j):
    rows = pl.program_id(1) * tq + lax.broadcasted_iota(jnp.int32, (tq, tq), 0)
    cols = j * tq + lax.broadcasted_iota(jnp.int32, (tq, tq), 1)
    lane = lax.broadcasted_iota(jnp.int32, (tq, LANE), 1)
    return cols <= rows, lane < MLA_VDIM


def _mla_attn_fwd(q, k, v):
    t = q.shape[0]
    tq = min(256, t)
    scale = MLA_QK ** -0.5

    def body(q_ref, k_ref, v_ref, o_ref, lse_ref):
        def step(j, carry):
            causal, _ = _attn_masks(tq, j)
            rows = pl.ds(pl.multiple_of(j * tq, tq), tq)
            vp = v_ref[rows, :]
            out = []
            for hh in range(2):
                m, l, acc = carry[hh]
                sl = slice(hh * HEAD_PAD, (hh + 1) * HEAD_PAD)
                s = _dot_nt(q_ref[:, sl], k_ref[rows, sl]) * scale
                s = jnp.where(causal, s, -jnp.inf)
                m_new = jnp.maximum(m, jnp.max(s, axis=-1, keepdims=True))
                alpha = jnp.exp(m - m_new)
                e = jnp.exp(s - m_new)
                out.append((m_new, alpha * l + jnp.sum(e, axis=-1, keepdims=True),
                            alpha * acc + _dot(e.astype(BF16), vp)))
            return tuple(out)

        init = (jnp.full((tq, 1), -jnp.inf, F32), jnp.zeros((tq, 1), F32), jnp.zeros((tq, LANE), F32))
        res = lax.fori_loop(0, pl.program_id(1) + 1, step, (init, init))
        _, low = _attn_masks(tq, 0)
        o_ref[...] = jnp.where(low, res[0][2] / res[0][1], res[1][2] / res[1][1])
        for hh in range(2):
            lse_ref[hh] = res[hh][0] + jnp.log(res[hh][1])

    return pl.pallas_call(
        body, name="mla_attn_fwd",
        grid=(MLA_HEADS // 2, t // tq),
        in_specs=[pl.BlockSpec((tq, 2 * HEAD_PAD), lambda p, i: (i, p)),
                  pl.BlockSpec((t, 2 * HEAD_PAD), lambda p, i: (0, p)),
                  pl.BlockSpec((t, LANE), lambda p, i: (0, p))],
        out_specs=(pl.BlockSpec((tq, LANE), lambda p, i: (i, p)),
                   pl.BlockSpec((2, tq, 1), lambda p, i: (p, i, 0))),
        out_shape=(jax.ShapeDtypeStruct((t, MLA_WIDTH), F32),
                   jax.ShapeDtypeStruct((MLA_HEADS, t, 1), F32)),
        compiler_params=_cparams(("arbitrary", "arbitrary")),
    )(q, k, v)


def _mla_attn_bwd(q, k, v, o, do, lse):
    t = q.shape[0]
    tq = min(256, t)
    scale = MLA_QK ** -0.5

    def body(q_ref, k_ref, v_ref, o_ref, do_ref, lse_ref, dq_ref, dk_ref, dv_ref):
        @pl.when(pl.program_id(1) == 0)
        def _():
            dk_ref[...] = jnp.zeros_like(dk_ref)
            dv_ref[...] = jnp.zeros_like(dv_ref)

        _, low = _attn_masks(tq, 0)
        do_all = do_ref[...]
        o_all = o_ref[...]
        do_b, dsum = [], []
        for hh in range(2):
            do_h = jnp.where(low if hh == 0 else jnp.logical_not(low), do_all, 0.0)
            dsum.append(jnp.sum(do_h * o_all, axis=-1, keepdims=True))
            do_b.append(do_h.astype(BF16))

        def step(j, dq):
            causal, _ = _attn_masks(tq, j)
            rows = pl.ds(pl.multiple_of(j * tq, tq), tq)
            vp = v_ref[rows, :]
            dv_j = jnp.zeros((tq, LANE), F32)
            out = []
            for hh in range(2):
                sl = slice(hh * HEAD_PAD, (hh + 1) * HEAD_PAD)
                qh = q_ref[:, sl]
                kh = k_ref[rows, sl]
                s = _dot_nt(qh, kh) * scale
                p = jnp.where(causal, jnp.exp(s - lse_ref[hh]), 0.0)
                dp = _dot_nt(do_b[hh], vp)
                ds = (p * (dp - dsum[hh]) * scale).astype(BF16)
                out.append(dq[hh] + _dot(ds, kh))
                dk_ref[rows, sl] += _dot_tn(ds, qh)
                dv_j = dv_j + _dot_tn(p.astype(BF16), do_b[hh])
            dv_ref[rows, :] += dv_j
            return tuple(out)

        zero = jnp.zeros((tq, HEAD_PAD), F32)
        dq = lax.fori_loop(0, pl.program_id(1) + 1, step, (zero, zero))
        for hh in range(2):
            dq_ref[:, hh * HEAD_PAD:(hh + 1) * HEAD_PAD] = dq[hh].astype(BF16)

    return pl.pallas_call(
        body, name="mla_attn_bwd",
        grid=(MLA_HEADS // 2, t // tq),
        in_specs=[pl.BlockSpec((tq, 2 * HEAD_PAD), lambda p, i: (i, p)),
                  pl.BlockSpec((t, 2 * HEAD_PAD), lambda p, i: (0, p)),
                  pl.BlockSpec((t, LANE), lambda p, i: (0, p)),
                  pl.BlockSpec((tq, LANE), lambda p, i: (i, p)),
                  pl.BlockSpec((tq, LANE), lambda p, i: (i, p)),
                  pl.BlockSpec((2, tq, 1), lambda p, i: (p, i, 0))],
        out_specs=(pl.BlockSpec((tq, 2 * HEAD_PAD), lambda p, i: (i, p)),
                   pl.BlockSpec((t, 2 * HEAD_PAD), lambda p, i: (0, p)),
                   pl.BlockSpec((t, LANE), lambda p, i: (0, p))),
        out_shape=(jax.ShapeDtypeStruct((t, MLA_HEADS * HEAD_PAD), BF16),
                   jax.ShapeDtypeStruct((t, MLA_HEADS * HEAD_PAD), F32),
                   jax.ShapeDtypeStruct((t, MLA_WIDTH), F32)),
        compiler_params=_cparams(("arbitrary", "arbitrary")),
    )(q, k, v, o, do, lse)


def _gla_chunk_terms(q_ref, k_ref, la_ref, h, tri):
    sl = slice(h * GLA_HK, (h + 1) * GLA_HK)
    b = _dot_exact(tri, la_ref[:, sl])
    bl = b[GLA_CHUNK - 1:GLA_CHUNK, :]
    kc = k_ref[:, sl]
    q_in = (q_ref[:, sl] * (GLA_HK ** -0.5)) * jnp.exp(b)
    k_in = kc * jnp.exp(-b)
    k_st = kc * jnp.exp(bl - b)
    return b, bl, q_in, k_in, k_st


def _tri(c, lower):
    r = lax.broadcasted_iota(jnp.int32, (c, c), 0)
    cc = lax.broadcasted_iota(jnp.int32, (c, c), 1)
    return jnp.where(r >= cc if lower else r <= cc, 1.0, 0.0).astype(F32)


def _gla_fwd(proj, log_a):
    t = proj.shape[0]
    c = GLA_CHUNK
    n = t // c

    def body(q_ref, k_ref, v_ref, la_ref, o_ref, sp_ref, st_ref):
        @pl.when(pl.program_id(0) == 0)
        def _():
            st_ref[...] = jnp.zeros_like(st_ref)

        tri = _tri(c, True)
        for h in range(GLA_HEADS):
            _, bl, q_in, k_in, k_st = _gla_chunk_terms(q_ref, k_ref, la_ref, h, tri)
            vs = slice(h * GLA_HV, (h + 1) * GLA_HV)
            vv = v_ref[:, vs].astype(BF16)
            qb = q_in.astype(BF16)
            attn = _dot_nt(qb, k_in.astype(BF16)) * tri
            st = st_ref[h]
            sp_ref[0, h] = st
            o_ref[:, vs] = _dot(attn.astype(BF16), vv) + _dot_nt(qb, st.astype(BF16))
            st_ref[h] = st * jnp.exp(bl) + _dot_tn(vv, k_st.astype(BF16))

    return pl.pallas_call(
        body, name="gla_fwd",
        grid=(n,),
        in_specs=[pl.BlockSpec((c, GLA_DK), lambda i: (i, P_QG // GLA_DK)),
                  pl.BlockSpec((c, GLA_DK), lambda i: (i, P_KG // GLA_DK)),
                  pl.BlockSpec((c, GLA_DV), lambda i: (i, P_VG // GLA_DV)),
                  pl.BlockSpec((c, GLA_DK), lambda i: (i, 0))],
        out_specs=(pl.BlockSpec((c, GLA_DV), lambda i: (i, 0)),
                   pl.BlockSpec((1, GLA_HEADS, GLA_HV, GLA_HK), lambda i: (i, 0, 0, 0))),
        out_shape=(jax.ShapeDtypeStruct((t, GLA_DV), F32),
                   jax.ShapeDtypeStruct((n, GLA_HEADS, GLA_HV, GLA_HK), F32)),
        scratch_shapes=[pltpu.VMEM((GLA_HEADS, GLA_HV, GLA_HK), F32)],
        compiler_params=_cparams(("arbitrary",)),
    )(proj, proj, proj, log_a)


def _gla_bwd(proj, log_a, do, states):
    t = proj.shape[0]
    c = GLA_CHUNK
    n = t // c

    def body(q_ref, k_ref, v_ref, la_ref, do_ref, sp_ref, dg_ref, dla_ref, ds_ref):
        @pl.when(pl.program_id(0) == 0)
        def _():
            ds_ref[...] = jnp.zeros_like(ds_ref)

        tri = _tri(c, True)
        tri_t = _tri(c, False)
        for h in range(GLA_HEADS):
            b, bl, q_in, k_in, k_st = _gla_chunk_terms(q_ref, k_ref, la_ref, h, tri)
            ks_ = slice(h * GLA_HK, (h + 1) * GLA_HK)
            vs = slice(h * GLA_HV, (h + 1) * GLA_HV)
            vv = v_ref[:, vs].astype(BF16)
            do_h = do_ref[:, vs]
            qb, kb, ksb = q_in.astype(BF16), k_in.astype(BF16), k_st.astype(BF16)
            attn = (_dot_nt(qb, kb) * tri).astype(BF16)
            st = sp_ref[0, h]
            dst = ds_ref[h]
            dstb = dst.astype(BF16)
            dattn = (_dot_nt(do_h, vv) * tri).astype(BF16)
            dg_ref[:, P_VG + h * GLA_HV:P_VG + (h + 1) * GLA_HV] = (
                _dot_tn(attn, do_h) + _dot_nt(ksb, dstb)).astype(BF16)
            dq_in = _dot(dattn, kb) + _dot(do_h, st.astype(BF16))
            dk_in = _dot_tn(dattn, qb)
            dk_st = _dot(vv, dstb)
            ebl = jnp.exp(bl)
            d_ebl = jnp.sum(st * dst, axis=0, keepdims=True)
            ds_ref[h] = _dot_tn(do_h, qb) + dst * ebl
            dg_ref[:, P_QG + h * GLA_HK:P_QG + (h + 1) * GLA_HK] = (
                dq_in * (GLA_HK ** -0.5) * jnp.exp(b)).astype(BF16)
            dg_ref[:, P_KG + h * GLA_HK:P_KG + (h + 1) * GLA_HK] = (
                dk_in * jnp.exp(-b) + dk_st * jnp.exp(bl - b)).astype(BF16)
            db = dq_in * q_in - dk_in * k_in - dk_st * k_st
            dbl = jnp.sum(dk_st * k_st, axis=0, keepdims=True) + d_ebl * ebl
            dla_ref[:, ks_] = _dot_exact(tri_t, db) + dbl

    rev = lambda i: n - 1 - i
    gw = P_GROUPS[0][1]
    return pl.pallas_call(
        body, name="gla_bwd",
        grid=(n,),
        in_specs=[pl.BlockSpec((c, GLA_DK), lambda i: (rev(i), P_QG // GLA_DK)),
                  pl.BlockSpec((c, GLA_DK), lambda i: (rev(i), P_KG // GLA_DK)),
                  pl.BlockSpec((c, GLA_DV), lambda i: (rev(i), P_VG // GLA_DV)),
                  pl.BlockSpec((c, GLA_DK), lambda i: (rev(i), 0)),
                  pl.BlockSpec((c, GLA_DV), lambda i: (rev(i), 0)),
                  pl.BlockSpec((1, GLA_HEADS, GLA_HV, GLA_HK), lambda i: (rev(i), 0, 0, 0))],
        out_specs=(pl.BlockSpec((c, gw), lambda i: (rev(i), 0)),
                   pl.BlockSpec((c, GLA_DK), lambda i: (rev(i), 0))),
        out_shape=(jax.ShapeDtypeStruct((t, gw), BF16), jax.ShapeDtypeStruct((t, GLA_DK), F32)),
        scratch_shapes=[pltpu.VMEM((GLA_HEADS, GLA_HV, GLA_HK), F32)],
        compiler_params=_cparams(("arbitrary",)),
    )(proj, proj, proj, log_a, do, states)


def _post(o_mla, proj, o_gla, x, target, g_gla, g_final, w_pm, w_pg, w_o):
    t = x.shape[0]
    tm = min(128, t)
    g0, gw = P_GROUPS[1]

    def body(om_ref, zg_ref, gm_ref, gg_ref, zm_ref, og_ref, x_ref, tg_ref, ggla_ref, gf_ref,
             wpm_ref, wpg_ref, wo_ref,
             dx2_ref, dom_ref, dog_ref, dg_ref,
             mg_ref, um_ref, ug_ref, dym_ref, dyg_ref, loss_ref, dgf_ref, dggla_ref):
        @pl.when(pl.program_id(0) == 0)
        def _():
            loss_ref[...] = jnp.zeros_like(loss_ref)
            dgf_ref[...] = jnp.zeros_like(dgf_ref)
            dggla_ref[...] = jnp.zeros_like(dggla_ref)

        om = om_ref[...]
        zm = zm_ref[...]
        sm = _sigmoid(zm)
        silu_m = zm * sm
        um = (om * silu_m).astype(BF16)
        um_ref[...] = um
        ym = _dot(um, wpm_ref[...])

        ggla = ggla_ref[...]
        zg = zg_ref[...]
        sg = _sigmoid(zg)
        silu_g = zg * sg
        xhat, rstd, on = [], [], []
        for h in range(GLA_HEADS):
            blk = og_ref[:, h * GLA_HV:(h + 1) * GLA_HV]
            r = lax.rsqrt(jnp.mean(blk * blk, axis=-1, keepdims=True) + EPS)
            xhat.append(blk * r)
            rstd.append(r)
            on.append(xhat[h] * ggla)
        on = jnp.concatenate(on, axis=-1)
        ug = (on * silu_g).astype(BF16)
        ug_ref[...] = ug
        yg = _dot(ug, wpg_ref[...])

        sgm = _sigmoid(gm_ref[...])
        sgg = _sigmoid(gg_ref[...])
        merged = (sgm * ym + sgg * yg).astype(BF16)
        mg_ref[...] = merged
        x2 = x_ref[...] + _dot(merged, wo_ref[...])
        gf = gf_ref[...]
        rf = lax.rsqrt(jnp.mean(x2 * x2, axis=-1, keepdims=True) + EPS)
        xh = x2 * rf
        err = xh * gf - tg_ref[...]
        loss_ref[...] += 0.5 * jnp.sum(jnp.mean(err * err, axis=-1, keepdims=True))

        dy = err * (1.0 / D_MODEL)
        dgf_ref[...] += jnp.sum(dy * xh, axis=0, keepdims=True)
        dxh = dy * gf
        dx2 = rf * (dxh - xh * jnp.mean(dxh * xh, axis=-1, keepdims=True))
        dx2_ref[...] = dx2
        dmerged = _dot_nt(dx2.astype(BF16), wo_ref[...])
        dym = (dmerged * sgm).astype(BF16)
        dyg = (dmerged * sgg).astype(BF16)
        dym_ref[...] = dym
        dyg_ref[...] = dyg
        dg_ref[:, P_GMLA - g0:P_GMLA - g0 + D_MODEL] = (dmerged * ym * sgm * (1.0 - sgm)).astype(BF16)
        dg_ref[:, P_GGLA - g0:P_GGLA - g0 + D_MODEL] = (dmerged * yg * sgg * (1.0 - sgg)).astype(BF16)
        dum = _dot_nt(dym, wpm_ref[...])
        dom_ref[...] = dum * silu_m
        dg_ref[:, P_ZMLA - g0:P_ZMLA - g0 + MLA_WIDTH] = (
            dum * om * (sm * (1.0 + zm * (1.0 - sm)))).astype(BF16)
        dug = _dot_nt(dyg, wpg_ref[...])
        dg_ref[:, P_ZGLA - g0:P_ZGLA - g0 + GLA_DV] = (
            dug * on * (sg * (1.0 + zg * (1.0 - sg)))).astype(BF16)
        don = dug * silu_g
        dggla = jnp.zeros((1, GLA_HV), F32)
        for h in range(GLA_HEADS):
            hs = slice(h * GLA_HV, (h + 1) * GLA_HV)
            don_h = don[:, hs]
            dggla = dggla + jnp.sum(don_h * xhat[h], axis=0, keepdims=True)
            dxh_h = don_h * ggla
            dog_ref[:, hs] = (rstd[h] * (dxh_h - xhat[h] * jnp.mean(dxh_h * xhat[h], axis=-1,
                                                                     keepdims=True))).astype(BF16)
        dggla_ref[...] += dggla

    row = lambda w: pl.BlockSpec((tm, w), lambda i: (i, 0))
    pcol = lambda w, off: pl.BlockSpec((tm, w), lambda i: (i, off // w))
    full = lambda a: pl.BlockSpec(a.shape, lambda i: (0, 0))
    sds = jax.ShapeDtypeStruct
    return pl.pallas_call(
        body, name="post_fwd_bwd",
        grid=(t // tm,),
        in_specs=[row(MLA_WIDTH), pcol(GLA_DV, P_ZGLA), pcol(D_MODEL, P_GMLA), pcol(D_MODEL, P_GGLA),
                  pcol(MLA_WIDTH, P_ZMLA), row(GLA_DV), row(D_MODEL), row(D_MODEL),
                  full(g_gla), full(g_final), full(w_pm), full(w_pg), full(w_o)],
        out_specs=(row(D_MODEL), row(MLA_WIDTH), row(GLA_DV), row(gw),
                   row(D_MODEL), row(MLA_WIDTH), row(GLA_DV), row(D_MODEL), row(D_MODEL),
                   pl.BlockSpec((1, LANE), lambda i: (0, 0)),
                   pl.BlockSpec((1, D_MODEL), lambda i: (0, 0)),
                   pl.BlockSpec((1, GLA_HV), lambda i: (0, 0))),
        out_shape=(sds((t, D_MODEL), F32), sds((t, MLA_WIDTH), F32), sds((t, GLA_DV), BF16),
                   sds((t, gw), BF16),
                   sds((t, D_MODEL), BF16), sds((t, MLA_WIDTH), BF16), sds((t, GLA_DV), BF16),
                   sds((t, D_MODEL), BF16), sds((t, D_MODEL), BF16),
                   sds((1, LANE), F32), sds((1, D_MODEL), F32), sds((1, GLA_HV), F32)),
        compiler_params=_cparams(("arbitrary",)),
    )(o_mla, proj, proj, proj, proj, o_gla, x, target, g_gla, g_final, w_pm, w_pg, w_o)


def _mla_prep_bwd(dq, dk, dv, dla, pre, proj, rq, rkv, g_q, g_kv, w_uq_p, w_k_p, w_v, w_gate_p,
                  rc, rsn, rsp):
    t = proj.shape[0]
    tm = min(256, t)
    gw = P_GROUPS[2][1]

    def body(dq_ref, dk_ref, dv_ref, dla_ref, pre_ref, cq_ref, ckv_ref, rq_ref, rkv_ref,
             gq_ref, gkv_ref, wuq_ref, wk_ref, wv_ref, wg_ref, c_ref, sn_ref, sp_ref,
             dg_ref, dqpre_ref, dpre_ref, dgq_ref, dgkv_ref, dbg_ref):
        @pl.when(pl.program_id(0) == 0)
        def _():
            dgq_ref[...] = jnp.zeros_like(dgq_ref)
            dgkv_ref[...] = jnp.zeros_like(dgkv_ref)
            dbg_ref[...] = jnp.zeros_like(dbg_ref)

        c, sn, sp = c_ref[...], sn_ref[...], sp_ref[...]
        dkr = jnp.zeros((tm, LANE), F32)
        for h in range(MLA_HEADS):
            sl = slice(h * HEAD_PAD, (h + 1) * HEAD_PAD)
            dqpre_ref[:, sl] = _rope_bwd(dq_ref[:, sl].astype(F32), c, sn, sp).astype(BF16)
            dkr = dkr + dk_ref[:, sl]
        dcqn = _dot_nt(dqpre_ref[...], wuq_ref[...])
        rq = rq_ref[...]
        xh = cq_ref[:, :MLA_Q_RANK] * rq
        dgq_ref[...] += jnp.sum(dcqn * xh, axis=0, keepdims=True)
        dxh = dcqn * gq_ref[...]
        dcq = rq * (dxh - xh * jnp.mean(dxh * xh, axis=-1, keepdims=True))
        dg_ref[:, :MLA_Q_RANK] = dcq.astype(BF16)
        dg_ref[:, MLA_Q_RANK:512] = jnp.zeros((tm, 512 - MLA_Q_RANK), BF16)

        dckvn = _dot_nt(dk_ref[...].astype(BF16), wk_ref[...]) + \
            _dot_nt(dv_ref[...].astype(BF16), wv_ref[...])
        rkv = rkv_ref[...]
        xh = ckv_ref[...] * rkv
        dgkv_ref[...] += jnp.sum(dckvn * xh, axis=0, keepdims=True)
        dxh = dckvn * gkv_ref[...]
        dg_ref[:, P_CKV - P_CQ:P_CKV - P_CQ + MLA_KV_RANK] = (
            rkv * (dxh - xh * jnp.mean(dxh * xh, axis=-1, keepdims=True))).astype(BF16)

        dpre = dla_ref[...] * (1.0 / GLA_GATE_NORM) * (1.0 - _sigmoid(pre_ref[...]))
        dbg_ref[...] += jnp.sum(dpre, axis=0, keepdims=True)
        dpre = dpre.astype(BF16)
        dpre_ref[...] = dpre
        lane = lax.broadcasted_iota(jnp.int32, (tm, LANE), 1)
        in_kr = jnp.logical_and(lane >= MISC_KR, lane < MISC_KR + MLA_ROPE)
        dmisc = jnp.where(in_kr, _rope_bwd(dkr, c, sn, sp), 0.0) + _dot_nt(dpre, wg_ref[...])
        dg_ref[:, P_MISC - P_CQ:P_MISC - P_CQ + LANE] = dmisc.astype(BF16)

    hq = MLA_HEADS * HEAD_PAD
    row = lambda w: pl.BlockSpec((tm, w), lambda i: (i, 0))
    full = lambda a: pl.BlockSpec(a.shape, lambda i: (0, 0))
    acc = lambda w: pl.BlockSpec((1, w), lambda i: (0, 0))
    sds = jax.ShapeDtypeStruct
    return pl.pallas_call(
        body, name="mla_prep_bwd",
        grid=(t // tm,),
        in_specs=[row(hq), row(hq), row(MLA_WIDTH), row(GLA_DK), row(GLA_DK),
                  pl.BlockSpec((tm, 512), lambda i: (i, P_CQ // 512)),
                  pl.BlockSpec((tm, MLA_KV_RANK), lambda i: (i, P_CKV // MLA_KV_RANK)),
                  row(1), row(1), full(g_q), full(g_kv), full(w_uq_p), full(w_k_p), full(w_v),
                  full(w_gate_p), row(LANE), row(LANE), row(LANE)],
        out_specs=(row(gw), row(hq), row(GLA_DK),
                   acc(MLA_Q_RANK), acc(MLA_KV_RANK), acc(GLA_DK)),
        out_shape=(sds((t, gw), BF16), sds((t, hq), BF16), sds((t, GLA_DK), BF16),
                   sds((1, MLA_Q_RANK), F32), sds((1, MLA_KV_RANK), F32), sds((1, GLA_DK), F32)),
        compiler_params=_cparams(("arbitrary",)),
    )(dq, dk, dv, dla, pre, proj, proj, rq, rkv, g_q, g_kv, w_uq_p, w_k_p, w_v, w_gate_p,
      rc, rsn, rsp)


def _inproj_bwd(dgroups, w_p, x, rstd, g_in, dx2):
    t = x.shape[0]
    tm = min(256, t)

    def body(d0_ref, d1_ref, d2_ref, w_ref, x_ref, r_ref, g_ref, dx2_ref, dx_ref, dg_ref):
        @pl.when(pl.program_id(0) == 0)
        def _():
            dg_ref[...] = jnp.zeros_like(dg_ref)

        dh = jnp.zeros((tm, D_MODEL), F32)
        for d_ref, (off, width) in zip((d0_ref, d1_ref, d2_ref), P_GROUPS):
            dh = dh + _dot_nt(d_ref[...], w_ref[:, off:off + width])
        r = r_ref[...]
        xh = x_ref[...] * r
        dg_ref[...] += jnp.sum(dh * xh, axis=0, keepdims=True)
        dxh = dh * g_ref[...]
        dx_ref[...] = dx2_ref[...] + r * (dxh - xh * jnp.mean(dxh * xh, axis=-1, keepdims=True))

    row = lambda w: pl.BlockSpec((tm, w), lambda i: (i, 0))
    return pl.pallas_call(
        body, name="inproj_bwd",
        grid=(t // tm,),
        in_specs=[row(P_GROUPS[0][1]), row(P_GROUPS[1][1]), row(P_GROUPS[2][1]),
                  pl.BlockSpec((D_MODEL, P_TOTAL), lambda i: (0, 0)),
                  row(D_MODEL), row(1), pl.BlockSpec((1, D_MODEL), lambda i: (0, 0)), row(D_MODEL)],
        out_specs=(row(D_MODEL), pl.BlockSpec((1, D_MODEL), lambda i: (0, 0))),
        out_shape=(jax.ShapeDtypeStruct((t, D_MODEL), F32),
                   jax.ShapeDtypeStruct((1, D_MODEL), F32)),
        compiler_params=_cparams(("arbitrary",)),
    )(*dgroups, w_p, x, rstd, g_in, dx2)


def _matmul(name, a, b, tm, tn, dtype=F32):
    kk, m = a.shape
    n = b.shape[1]

    def body(a_ref, b_ref, o_ref):
        o_ref[...] = _dot_tn(a_ref[...].astype(BF16), b_ref[...].astype(BF16)).astype(dtype)

    return pl.pallas_call(
        body, name=name,
        grid=(n // tn, m // tm),
        in_specs=[pl.BlockSpec((kk, tm), lambda j, i: (0, i)),
                  pl.BlockSpec((kk, tn), lambda j, i: (0, j))],
        out_specs=pl.BlockSpec((tm, tn), lambda j, i: (i, j)),
        out_shape=jax.ShapeDtypeStruct((m, n), dtype),
        compiler_params=_cparams(("arbitrary", "arbitrary")),
    )(a, b)


def _adamw_update(p_ref, w_ref, m_ref, v_ref, g_ref, d_ref, nm_ref, nv_ref):
    g = p_ref[0].astype(F32)
    for q in range(1, p_ref.shape[0]):
        g = g + p_ref[q].astype(F32)
    m_new = ADAM_B1 * m_ref[...] + (1.0 - ADAM_B1) * g
    v_new = ADAM_B2 * v_ref[...] + (1.0 - ADAM_B2) * (g * g)
    m_hat = m_new / (1.0 - ADAM_B1 ** ADAM_STEP)
    v_hat = v_new / (1.0 - ADAM_B2 ** ADAM_STEP)
    g_ref[...] = g
    nm_ref[...] = m_new
    nv_ref[...] = v_new
    d_ref[...] = -ADAM_LR * (m_hat / (jnp.sqrt(v_hat) + ADAM_EPS) + ADAM_WD * w_ref[...])


def _adamw_rows(name, parts, w, m, v, tr):
    _, rows, cols = w.shape

    def body(*refs):
        _adamw_update(*refs)

    blk = pl.BlockSpec((None, tr, cols), lambda i: (0, i, 0))
    out = jax.ShapeDtypeStruct((1, rows, cols), F32)
    return pl.pallas_call(
        body, name=name,
        grid=(rows // tr,),
        in_specs=[pl.BlockSpec((parts.shape[0], tr, cols), lambda i: (0, i, 0)), blk, blk, blk],
        out_specs=(blk, blk, blk, blk),
        out_shape=(out, out, out, out),
        compiler_params=_cparams(("arbitrary",)),
    )(parts, w, m, v)


def _adamw_group(parts, ws, ms, vs):
    n = len(ws)

    def body(*refs):
        ins, outs = refs[:4 * n], refs[4 * n:]
        for a in range(n):
            _adamw_update(ins[a], *[r.at[0] for r in (ins[n + a], ins[2 * n + a], ins[3 * n + a])],
                          *[r.at[0] for r in outs[4 * a:4 * a + 4]])

    vmem = lambda k: [pl.BlockSpec(memory_space=pltpu.VMEM) for _ in range(k)]
    out_shape = []
    for w in ws:
        out_shape += [jax.ShapeDtypeStruct(w.shape, F32)] * 4
    res = pl.pallas_call(
        body, name="adamw_small_weights",
        in_specs=vmem(4 * n), out_specs=tuple(vmem(4 * n)), out_shape=tuple(out_shape),
        compiler_params=_cparams(),
    )(*parts, *ws, *ms, *vs)
    return [res[4 * a:4 * a + 4] for a in range(n)]


def _rope_tables(positions):
    half = MLA_ROPE // 2
    freqs = ROPE_THETA ** (-jnp.arange(half, dtype=F32) / half)
    ang = positions.astype(F32).reshape(-1, 1) * freqs
    cos, sin = jnp.cos(ang), jnp.sin(ang)
    t = ang.shape[0]
    one, zero = jnp.ones((t, MLA_NOPE), F32), jnp.zeros((t, half), F32)
    tail = jnp.zeros((t, LANE - MLA_QK), F32)
    rc = jnp.concatenate([one, cos, cos, tail], axis=1)
    rsn = jnp.concatenate([0.0 * one, -sin, zero, tail], axis=1)
    rsp = jnp.concatenate([0.0 * one, zero, sin, tail], axis=1)
    return rc, rsn, rsp


def _cols_full(g):
    return g.transpose(1, 0, 2)


def kernel(x, positions, g_in, w_in, g_q, w_uq, g_kv, w_ukv, w_gla_gate, b_gla_gate, g_gla, w_proj_mla, w_proj_gla, w_out, g_final, loss_target, m_g_in, m_w_in, m_g_q, m_w_uq, m_g_kv, m_w_ukv, m_w_gla_gate, m_b_gla_gate, m_g_gla, m_w_proj_mla, m_w_proj_gla, m_w_out, m_g_final, v_g_in, v_w_in, v_g_q, v_w_uq, v_g_kv, v_w_ukv, v_w_gla_gate, v_b_gla_gate, v_g_gla, v_w_proj_mla, v_w_proj_gla, v_w_out, v_g_final):
    t = x.shape[1]
    x2d = x.reshape(t, D_MODEL)
    tgt = loss_target.reshape(t, D_MODEL)
    g_final2 = g_final.reshape(1, D_MODEL)
    sharded = [(w_in, m_w_in, v_w_in), (w_uq, m_w_uq, v_w_uq), (w_ukv, m_w_ukv, v_w_ukv),
               (w_gla_gate, m_w_gla_gate, v_w_gla_gate), (w_proj_mla, m_w_proj_mla, v_w_proj_mla),
               (w_proj_gla, m_w_proj_gla, v_w_proj_gla), (w_out, m_w_out, v_w_out)]

    g_w_in, g_uq, g_ukv, g_gate, g_pm, g_pg, g_o = _all_gather(
        [s[0][0].astype(BF16) for s in sharded])
    w_in_p = _weights_to_p(g_w_in)
    w_uq_p = jnp.pad(_cols_full(g_uq), ((0, 0), (0, 0), (0, HEAD_PAD - MLA_QK))).reshape(
        MLA_Q_RANK, MLA_HEADS * HEAD_PAD)
    ukv = _cols_full(g_ukv)
    w_k_p = jnp.pad(ukv[:, :, :MLA_NOPE], ((0, 0), (0, 0), (0, HEAD_PAD - MLA_NOPE))).reshape(
        MLA_KV_RANK, MLA_HEADS * HEAD_PAD)
    w_v = ukv[:, :, MLA_NOPE:].reshape(MLA_KV_RANK, MLA_WIDTH)
    w_gate_p = jnp.pad(_cols_full(g_gate).reshape(GLA_GATE_RANK, GLA_DK),
                       ((MISC_ALR, LANE - MISC_ALR - GLA_GATE_RANK), (0, 0)))
    w_pm = _cols_full(g_pm).reshape(MLA_WIDTH, D_MODEL)
    w_pg = g_pg.reshape(GLA_DV, D_MODEL)
    w_o = g_o.reshape(D_MODEL, D_MODEL)
    rc, rsn, rsp = _rope_tables(positions)

    proj, h, rstd = _inproj(x2d, g_in, w_in_p)
    q, k, v, log_a, pre, cqn, ckvn, rq, rkv, misc = _mla_prep(
        proj, g_q, g_kv, w_uq_p, w_k_p, w_v, w_gate_p, b_gla_gate, rc, rsn, rsp)
    o_mla, lse = _mla_attn_fwd(q, k, v)
    o_gla, states = _gla_fwd(proj, log_a)

    (dx2, do_mla, do_gla, d_out, merged, um, ug, dym, dyg, loss_p, dg_final,
     dg_gla) = _post(o_mla, proj, o_gla, x2d, tgt, g_gla, g_final2, w_pm, w_pg, w_o)

    dq, dk, dv = _mla_attn_bwd(q, k, v, o_mla, do_mla, lse)
    d_gla, dla = _gla_bwd(proj, log_a, do_gla, states)
    d_lat, dqpre, dpre, dg_q, dg_kv, db_gate = _mla_prep_bwd(
        dq, dk, dv, dla, pre, proj, rq, rkv, g_q, g_kv, w_uq_p, w_k_p, w_v, w_gate_p, rc, rsn, rsp)

    dgroups = (d_gla, d_out, d_lat)
    grad_x, dg_in = _inproj_bwd(dgroups, w_in_p, x2d, rstd, g_in, dx2)

    dw_groups = [_matmul("dw_in_%d" % i, h, dgroups[i], 512, tn)
                 for i, tn in enumerate((512, 512, 896))]
    p_in = _grads_to_shards(dw_groups)
    dw_uq = _matmul("dw_uq", cqn, dqpre, MLA_Q_RANK, 512, BF16)
    p_uq = dw_uq.reshape(MLA_Q_RANK, MLA_HEADS, HEAD_PAD)[:, :, :MLA_QK].transpose(1, 0, 2)
    dw_k = _matmul("dw_uk", ckvn, dk, MLA_KV_RANK, 512, BF16)
    dw_v = _matmul("dw_uv", ckvn, dv, MLA_KV_RANK, 512, BF16)
    p_ukv = jnp.concatenate(
        [dw_k.reshape(MLA_KV_RANK, MLA_HEADS, HEAD_PAD)[:, :, :MLA_NOPE],
         dw_v.reshape(MLA_KV_RANK, MLA_HEADS, MLA_VDIM)], axis=2).transpose(1, 0, 2)
    dw_gate = _matmul("dw_gate", misc, dpre, LANE, 512, BF16)
    p_gate = dw_gate[MISC_ALR:MISC_ALR + GLA_GATE_RANK].reshape(
        GLA_GATE_RANK, N_DEV, GLA_DK // N_DEV).transpose(1, 0, 2)
    p_pm = _matmul("dw_proj_mla", um, dym, 512, 512, BF16).reshape(
        MLA_WIDTH, N_DEV, D_MODEL // N_DEV).transpose(1, 0, 2)
    p_pg = _matmul("dw_proj_gla", ug, dyg, 512, 512, BF16).reshape(N_DEV, -1, D_MODEL)
    p_o = _matmul("dw_out", merged, dx2, 512, 512, BF16).reshape(N_DEV, -1, D_MODEL)
    small = jnp.concatenate([dg_in.reshape(-1), dg_q.reshape(-1), dg_kv.reshape(-1),
                             db_gate.reshape(-1), dg_gla.reshape(-1), dg_final.reshape(-1),
                             loss_p[0, :1]])
    small = jnp.pad(small, (0, SMALL_ROWS * LANE - small.shape[0])).reshape(SMALL_ROWS, LANE)

    recv = _exchange_grads([p_in, p_uq, p_ukv, p_gate, p_pm, p_pg, p_o], small)
    big = [_adamw_rows("adamw_w_in", recv[0], *sharded[0], 128)]
    big += _adamw_group(recv[1:7], *[[s[j] for s in sharded[1:]] for j in range(3)])
    replicated = [(g_in, m_g_in, v_g_in), (g_q, m_g_q, v_g_q), (g_kv, m_g_kv, v_g_kv),
                  (b_gla_gate, m_b_gla_gate, v_b_gla_gate), (g_gla, m_g_gla, v_g_gla),
                  (g_final, m_g_final, v_g_final)]
    spacks = [jnp.pad(jnp.concatenate([s[j].reshape(-1) for s in replicated]),
                      (0, SMALL_ROWS * LANE - sum(SMALL_SIZES))).reshape(1, SMALL_ROWS, LANE)
              for j in range(3)]
    tiny = _adamw_rows("adamw_gains", recv[7], spacks[0], spacks[1], spacks[2], SMALL_ROWS)

    outs = {}
    names = ("w_in", "w_uq", "w_ukv", "w_gla_gate", "w_proj_mla", "w_proj_gla", "w_out")
    for j, kind in enumerate(("grad", "delta", "new_m", "new_v")):
        for name, res in zip(names, big):
            outs[kind, name] = res[j]
        flat = tiny[j].reshape(-1)
        off = 0
        for name, size in zip(("g_in", "g_q", "g_kv", "b_gla_gate", "g_gla", "g_final"), SMALL_SIZES):
            shape = (size,) if name == "g_final" else (1, size)
            outs[kind, name] = flat[off:off + size].reshape(shape)
            off += size
    loss = tiny[0].reshape(-1)[sum(SMALL_SIZES)]
    order = ("g_in", "w_in", "g_q", "w_uq", "g_kv", "w_ukv", "w_gla_gate", "b_gla_gate", "g_gla",
             "w_proj_mla", "w_proj_gla", "w_out", "g_final")
    result = [loss, grad_x.reshape(1, t, D_MODEL)]
    for kind in ("grad", "delta", "new_m", "new_v"):
        result += [outs[kind, name] for name in order]
    return tuple(result)
```

```python
import math

import jax
import jax.numpy as jnp
from jax import lax
from jax.experimental import pallas as pl
from jax.experimental.pallas import tpu as pltpu

F32 = jnp.float32
BF16 = jnp.bfloat16
MESH = pl.DeviceIdType.MESH
N_DEV = 8

D_MODEL = 1024
EPS = 1e-6
MLA_HEADS = 8
MLA_NOPE = 64
MLA_ROPE = 32
MLA_VDIM = 64
MLA_Q_RANK = 384
MLA_KV_RANK = 256
MLA_QK = MLA_NOPE + MLA_ROPE
MLA_WIDTH = MLA_HEADS * MLA_VDIM
ROPE_THETA = 10000.0
GLA_HEADS = 4
GLA_DK = 512
GLA_DV = 1024
GLA_HK = 128
GLA_HV = 256
GLA_GATE_RANK = 16
GLA_GATE_NORM = 16.0
GLA_CHUNK = 64
D_IN = 6320

ADAM_LR = 0.001
ADAM_B1 = 0.9
ADAM_B2 = 0.999
ADAM_EPS = 1e-08
ADAM_WD = 0.01
ADAM_STEP = 10

LANE = 128
HEAD_PAD = 128
VMEM_LIMIT = 48 * 1024 * 1024

P_VG, P_QG, P_KG = 0, 1024, 1536
P_ZGLA, P_GMLA, P_GGLA, P_ZMLA = 2048, 3072, 4096, 5120
P_CQ, P_CKV, P_MISC = 5632, 6144, 6400
P_TOTAL = 6528
P_GROUPS = ((0, 2048), (2048, 3584), (5632, 896))
MISC_KR = 64
MISC_ALR = 96
SHARD_COLS = D_IN // N_DEV
P_COMPONENTS = ((0, 384, P_CQ), (384, 256, P_CKV), (640, 32, P_MISC + MISC_KR), (672, 512, P_ZMLA),
                (1184, 512, P_QG), (1696, 512, P_KG), (2208, 1024, P_VG),
                (3232, 16, P_MISC + MISC_ALR), (3248, 1024, P_ZGLA), (4272, 1024, P_GMLA),
                (5296, 1024, P_GGLA))

SMALL_SIZES = (1024, 384, 256, 512, 256, 1024)
SMALL_ROWS = 32


def _segments():
    segs = []
    for g0, n, p0 in P_COMPONENTS:
        g = g0
        while g < g0 + n:
            d = g // SHARD_COLS
            end = min(g0 + n, (d + 1) * SHARD_COLS)
            segs.append((d, g - d * SHARD_COLS, end - g, p0 + g - g0))
            g = end
    return segs


def _cparams(sem=None):
    if sem is None:
        return pltpu.CompilerParams(vmem_limit_bytes=VMEM_LIMIT)
    return pltpu.CompilerParams(dimension_semantics=sem, vmem_limit_bytes=VMEM_LIMIT)


def _sigmoid(v):
    return 1.0 / (1.0 + jnp.exp(-v))


def _dot(a, b):
    return jnp.dot(a, b, preferred_element_type=F32)


def _dot_nt(a, b):
    return lax.dot_general(a, b, (((1,), (1,)), ((), ())), preferred_element_type=F32)


def _dot_tn(a, b):
    return lax.dot_general(a, b, (((0,), (0,)), ((), ())), preferred_element_type=F32)


def _dot_exact(a, b):
    return jnp.dot(a, b, preferred_element_type=F32, precision=lax.Precision.HIGHEST)


def _rope_fwd(blk, c, sn, sp):
    return blk * c + pltpu.roll(blk, LANE - 16, 1) * sn + pltpu.roll(blk, 16, 1) * sp


def _rope_bwd(blk, c, sn, sp):
    return blk * c + pltpu.roll(blk * sn, 16, 1) + pltpu.roll(blk * sp, LANE - 16, 1)


def _mesh_pos():
    return lax.axis_index("x"), lax.axis_index("y"), lax.axis_index("c")


def _hbm_specs(n):
    return [pl.BlockSpec(memory_space=pltpu.HBM) for _ in range(n)]


def _all_gather(shards):
    n = len(shards)

    def body(*refs):
        x_refs, out_refs = refs[:n], refs[n:2 * n]
        send_sems, recv_sems, local_sems = refs[2 * n:]
        x, y, c = _mesh_pos()
        me, sibling = (x, y, c), (x, y, 1 - c)
        chips = [(1 - x, y), (x, 1 - y), (1 - x, 1 - y)]

        def slot(a, px, py, pc):
            return out_refs[a].at[4 * px + 2 * py + pc]

        def copies(k, block, to, own=False):
            return [pltpu.make_async_remote_copy(
                src_ref=x_refs[a] if own else slot(a, *block), dst_ref=slot(a, *block),
                send_sem=send_sems.at[k, a], recv_sem=recv_sems.at[k, a],
                device_id=to, device_id_type=MESH) for a in range(n)]

        mine = [pltpu.make_async_copy(x_refs[a], slot(a, *me), local_sems.at[a]) for a in range(n)]
        for cp in mine:
            cp.start()
        first = copies(0, me, sibling, own=True)
        for j, chip in enumerate(chips):
            first += copies(1 + j, me, (*chip, c), own=True)
        for cp in first:
            cp.start()
        passed = []
        for j, chip in enumerate(chips):
            for cp in copies(1 + j, (*chip, c), me):
                cp.wait_recv()
            fwd = copies(4 + j, (*chip, c), sibling)
            for cp in fwd:
                cp.start()
            passed += fwd
        for cp in copies(0, sibling, me):
            cp.wait_recv()
        for j, chip in enumerate(chips):
            for cp in copies(4 + j, (*chip, 1 - c), me):
                cp.wait_recv()
        for cp in first + passed:
            cp.wait_send()
        for cp in mine:
            cp.wait()

    return pl.pallas_call(
        body, name="all_gather_weights",
        out_shape=tuple(jax.ShapeDtypeStruct((N_DEV,) + s.shape, s.dtype) for s in shards),
        in_specs=_hbm_specs(n), out_specs=tuple(_hbm_specs(n)),
        scratch_shapes=[pltpu.SemaphoreType.DMA((7, n)), pltpu.SemaphoreType.DMA((7, n)),
                        pltpu.SemaphoreType.DMA((n,))],
    )(*shards)


def _exchange_grads(parts, small):
    n = len(parts)
    rows_per_add = 256

    def body(*refs):
        p_refs, s_ref = refs[:n], refs[n]
        out_refs, sout_ref = refs[n + 1:2 * n + 1], refs[2 * n + 1]
        mine_v, recv_v = refs[2 * n + 2:3 * n + 2], refs[3 * n + 2:4 * n + 2]
        (d2d_send, d2d_recv, ici_send, ici_recv, s_send, s_recv, load_sems, own_sems,
         sown_sem) = refs[4 * n + 2:]
        x, y, c = _mesh_pos()
        me = 4 * x + 2 * y + c
        chips = [(x, y), (1 - x, y), (x, 1 - y), (1 - x, 1 - y)]

        sown = pltpu.make_async_copy(s_ref, sout_ref.at[me], sown_sem)
        sown.start()
        tiny = []
        for k in range(1, N_DEV):
            peer = (x ^ (k >> 2), y ^ ((k >> 1) & 1), c ^ (k & 1))
            tiny.append(pltpu.make_async_remote_copy(
                src_ref=s_ref, dst_ref=sout_ref.at[me], send_sem=s_send.at[k - 1],
                recv_sem=s_recv.at[k - 1], device_id=peer, device_id_type=MESH))
        for cp in tiny:
            cp.start()

        loads, swaps = [], []
        for a in range(n):
            for j, (px, py) in enumerate(chips):
                loads.append(pltpu.make_async_copy(
                    p_refs[a].at[4 * px + 2 * py + c], mine_v[a].at[j], load_sems.at[a, j]))
                swaps.append(pltpu.make_async_remote_copy(
                    src_ref=p_refs[a].at[4 * px + 2 * py + 1 - c], dst_ref=recv_v[a].at[j],
                    send_sem=d2d_send.at[a, j], recv_sem=d2d_recv.at[a, j],
                    device_id=(x, y, 1 - c), device_id_type=MESH))
        for cp in swaps + loads:
            cp.start()

        outgoing, own = [], []
        for a in range(n):
            rows = p_refs[a].shape[1]
            step = math.gcd(rows, rows_per_add)
            for j in range(4):
                loads[4 * a + j].wait()
                swaps[4 * a + j].wait_recv()

                @pl.loop(0, rows // step)
                def _(i):
                    rs = pl.ds(pl.multiple_of(i * step, step), step)
                    mine_v[a][j, rs, :] = (mine_v[a][j, rs, :].astype(F32)
                                           + recv_v[a][j, rs, :].astype(F32)).astype(BF16)

                if j == 0:
                    own.append(pltpu.make_async_copy(mine_v[a].at[0], out_refs[a].at[0],
                                                     own_sems.at[a]))
                    own[-1].start()
                else:
                    outgoing.append(pltpu.make_async_remote_copy(
                        src_ref=mine_v[a].at[j], dst_ref=out_refs[a].at[j],
                        send_sem=ici_send.at[a, j - 1], recv_sem=ici_recv.at[a, j - 1],
                        device_id=(*chips[j], c), device_id_type=MESH))
                    outgoing[-1].start()

        for cp in tiny + outgoing:
            cp.wait_recv()
        for cp in tiny + outgoing + swaps:
            cp.wait_send()
        for cp in own:
            cp.wait()
        sown.wait()

    outs = [jax.ShapeDtypeStruct((4,) + p.shape[1:], p.dtype) for p in parts]
    outs.append(jax.ShapeDtypeStruct((N_DEV,) + small.shape, small.dtype))
    stage = [pltpu.VMEM((4,) + p.shape[1:], p.dtype) for p in parts]
    return pl.pallas_call(
        body, name="exchange_grads",
        out_shape=tuple(outs),
        in_specs=_hbm_specs(n + 1), out_specs=tuple(_hbm_specs(n + 1)),
        scratch_shapes=stage + stage + [
            pltpu.SemaphoreType.DMA((n, 4)), pltpu.SemaphoreType.DMA((n, 4)),
            pltpu.SemaphoreType.DMA((n, 3)), pltpu.SemaphoreType.DMA((n, 3)),
            pltpu.SemaphoreType.DMA((7,)), pltpu.SemaphoreType.DMA((7,)),
            pltpu.SemaphoreType.DMA((n, 4)), pltpu.SemaphoreType.DMA((n,)),
            pltpu.SemaphoreType.DMA],
        compiler_params=_cparams(),
    )(*parts, small)


def _weights_to_p(gathered):
    tm = 64
    segs = sorted(_segments(), key=lambda s: s[3])

    def body(g_ref, o_ref):
        pieces, pos = [], 0
        for d, c0, n, p0 in segs:
            if p0 > pos:
                pieces.append(jnp.zeros((tm, p0 - pos), F32))
            pieces.append(g_ref[d, :, c0:c0 + n].astype(F32))
            pos = p0 + n
        pieces.append(jnp.zeros((tm, P_TOTAL - pos), F32))
        o_ref[...] = jnp.concatenate(pieces, axis=-1).astype(BF16)

    return pl.pallas_call(
        body, name="weights_to_layout",
        grid=(D_MODEL // tm,),
        in_specs=[pl.BlockSpec((N_DEV, tm, SHARD_COLS), lambda i: (0, i, 0))],
        out_specs=pl.BlockSpec((tm, P_TOTAL), lambda i: (i, 0)),
        out_shape=jax.ShapeDtypeStruct((D_MODEL, P_TOTAL), BF16),
        compiler_params=_cparams(("arbitrary",)),
    )(gathered)


def _grads_to_shards(groups):
    tm = 64
    segs = _segments()

    def body(*refs):
        g_refs, o_ref = refs[:3], refs[3]
        for d in range(N_DEV):
            pieces = []
            for _, c0, n, p0 in sorted([s for s in segs if s[0] == d], key=lambda s: s[1]):
                gi = max(i for i, (off, _) in enumerate(P_GROUPS) if off <= p0)
                lo = p0 - P_GROUPS[gi][0]
                pieces.append(g_refs[gi][:, lo:lo + n])
            o_ref[d] = jnp.concatenate(pieces, axis=-1).astype(BF16)

    return pl.pallas_call(
        body, name="grads_to_shards",
        grid=(D_MODEL // tm,),
        in_specs=[pl.BlockSpec((tm, w), lambda i: (i, 0)) for _, w in P_GROUPS],
        out_specs=pl.BlockSpec((N_DEV, tm, SHARD_COLS), lambda i: (0, i, 0)),
        out_shape=jax.ShapeDtypeStruct((N_DEV, D_MODEL, SHARD_COLS), BF16),
        compiler_params=_cparams(("arbitrary",)),
    )(*groups)


def _inproj(x, g_in, w_p):
    t = x.shape[0]
    tm = min(256, t)
    nj = 3
    tn = P_TOTAL // nj

    def body(x_ref, g_ref, w_ref, proj_ref, h_ref, r_ref):
        xf = x_ref[...]
        r = lax.rsqrt(jnp.mean(xf * xf, axis=-1, keepdims=True) + EPS)
        h = ((xf * r) * g_ref[...]).astype(BF16)
        proj_ref[...] = _dot(h, w_ref[...])

        @pl.when(pl.program_id(0) == 0)
        def _():
            h_ref[...] = h
            r_ref[...] = r

    first = lambda j, i: (jnp.where(j == 0, i, t // tm - 1), 0)
    return pl.pallas_call(
        body, name="inproj",
        grid=(nj, t // tm),
        in_specs=[pl.BlockSpec((tm, D_MODEL), lambda j, i: (i, 0)),
                  pl.BlockSpec((1, D_MODEL), lambda j, i: (0, 0)),
                  pl.BlockSpec((D_MODEL, tn), lambda j, i: (0, j))],
        out_specs=(pl.BlockSpec((tm, tn), lambda j, i: (i, j)),
                   pl.BlockSpec((tm, D_MODEL), first),
                   pl.BlockSpec((tm, 1), first)),
        out_shape=(jax.ShapeDtypeStruct((t, P_TOTAL), F32),
                   jax.ShapeDtypeStruct((t, D_MODEL), BF16),
                   jax.ShapeDtypeStruct((t, 1), F32)),
        compiler_params=_cparams(("arbitrary", "arbitrary")),
    )(x, g_in, w_p)


def _mla_prep(proj, g_q, g_kv, w_uq_p, w_k_p, w_v, w_gate_p, b_gate, rc, rsn, rsp):
    t = proj.shape[0]
    tm = min(256, t)
    hq = MLA_HEADS * HEAD_PAD

    def body(cq_ref, ckv_ref, misc_ref, gq_ref, gkv_ref, wuq_ref, wk_ref, wv_ref, wg_ref, bg_ref,
             c_ref, sn_ref, sp_ref,
             q_ref, k_ref, v_ref, la_ref, pre_ref, cqn_ref, ckvn_ref, rq_ref, rkv_ref, mb_ref):
        c, sn, sp = c_ref[...], sn_ref[...], sp_ref[...]
        cq = cq_ref[:, :MLA_Q_RANK]
        rq = lax.rsqrt(jnp.mean(cq * cq, axis=-1, keepdims=True) + EPS)
        cqn = ((cq * rq) * gq_ref[...]).astype(BF16)
        cqn_ref[...] = cqn
        rq_ref[...] = rq
        qpre = _dot(cqn, wuq_ref[...])
        ckv = ckv_ref[...]
        rkv = lax.rsqrt(jnp.mean(ckv * ckv, axis=-1, keepdims=True) + EPS)
        ckvn = ((ckv * rkv) * gkv_ref[...]).astype(BF16)
        ckvn_ref[...] = ckvn
        rkv_ref[...] = rkv
        kn = _dot(ckvn, wk_ref[...])
        v_ref[...] = _dot(ckvn, wv_ref[...]).astype(BF16)
        misc = misc_ref[...]
        krope = _rope_fwd(misc, c, sn, sp)
        for h in range(MLA_HEADS):
            sl = slice(h * HEAD_PAD, (h + 1) * HEAD_PAD)
            q_ref[:, sl] = _rope_fwd(qpre[:, sl], c, sn, sp).astype(BF16)
            k_ref[:, sl] = (kn[:, sl] + krope).astype(BF16)
        mb_ref[...] = misc.astype(BF16)
        pre = _dot(mb_ref[...], wg_ref[...]) + bg_ref[...]
        pre_ref[...] = pre
        la_ref[...] = (jnp.minimum(pre, 0.0) - jnp.log(1.0 + jnp.exp(-jnp.abs(pre)))) / GLA_GATE_NORM

    row = lambda w: pl.BlockSpec((tm, w), lambda i: (i, 0))
    full = lambda a: pl.BlockSpec(a.shape, lambda i: (0, 0))
    return pl.pallas_call(
        body, name="mla_prep",
        grid=(t // tm,),
        in_specs=[pl.BlockSpec((tm, 512), lambda i: (i, P_CQ // 512)),
                  pl.BlockSpec((tm, MLA_KV_RANK), lambda i: (i, P_CKV // MLA_KV_RANK)),
                  pl.BlockSpec((tm, LANE), lambda i: (i, P_MISC // LANE)),
                  full(g_q), full(g_kv), full(w_uq_p), full(w_k_p), full(w_v), full(w_gate_p),
                  full(b_gate), row(LANE), row(LANE), row(LANE)],
        out_specs=(row(hq), row(hq), row(MLA_WIDTH), row(GLA_DK), row(GLA_DK),
                   row(MLA_Q_RANK), row(MLA_KV_RANK), row(1), row(1), row(LANE)),
        out_shape=(jax.ShapeDtypeStruct((t, hq), BF16), jax.ShapeDtypeStruct((t, hq), BF16),
                   jax.ShapeDtypeStruct((t, MLA_WIDTH), BF16),
                   jax.ShapeDtypeStruct((t, GLA_DK), F32), jax.ShapeDtypeStruct((t, GLA_DK), F32),
                   jax.ShapeDtypeStruct((t, MLA_Q_RANK), BF16),
                   jax.ShapeDtypeStruct((t, MLA_KV_RANK), BF16),
                   jax.ShapeDtypeStruct((t, 1), F32), jax.ShapeDtypeStruct((t, 1), F32),
                   jax.ShapeDtypeStruct((t, LANE), BF16)),
        compiler_params=_cparams(("arbitrary",)),
    )(proj, proj, proj, g_q, g_kv, w_uq_p, w_k_p, w_v, w_gate_p, b_gate, rc, rsn, rsp)


def _attn_masks(tq, i):
    keys = (i + 1) * tq
    rows = i * tq + lax.broadcasted_iota(jnp.int32, (tq, keys), 0)
    cols = lax.broadcasted_iota(jnp.int32, (tq, keys), 1)
    lane = lax.broadcasted_iota(jnp.int32, (tq, LANE), 1)
    return cols <= rows, lane < MLA_VDIM


def _for_each_query_tile(n_tiles, fn):
    for i in range(n_tiles):
        pl.when(pl.program_id(1) == i)(lambda i=i: fn(i))


def _mla_attn_fwd(q, k, v):
    t = q.shape[0]
    tq = min(256, t)
    scale = MLA_QK ** -0.5

    def body(q_ref, k_ref, v_ref, o_ref, lse_ref):
        def tile(i):
            keys = (i + 1) * tq
            causal, low = _attn_masks(tq, i)
            vp = v_ref[0:keys, :]
            acc = jnp.zeros((tq, LANE), F32)
            for hh in range(2):
                sl = slice(hh * HEAD_PAD, (hh + 1) * HEAD_PAD)
                s = _dot_nt(q_ref[:, sl], k_ref[0:keys, sl]) * scale
                s = jnp.where(causal, s, -jnp.inf)
                m = jnp.max(s, axis=-1, keepdims=True)
                e = jnp.exp(s - m)
                l = jnp.sum(e, axis=-1, keepdims=True)
                o = _dot(e.astype(BF16), vp) / l
                acc = jnp.where(low if hh == 0 else jnp.logical_not(low), o, acc)
                lse_ref[hh] = m + jnp.log(l)
            o_ref[...] = acc

        _for_each_query_tile(t // tq, tile)

    return pl.pallas_call(
        body, name="mla_attn_fwd",
        grid=(MLA_HEADS // 2, t // tq),
        in_specs=[pl.BlockSpec((tq, 2 * HEAD_PAD), lambda p, i: (i, p)),
                  pl.BlockSpec((t, 2 * HEAD_PAD), lambda p, i: (0, p)),
                  pl.BlockSpec((t, LANE), lambda p, i: (0, p))],
        out_specs=(pl.BlockSpec((tq, LANE), lambda p, i: (i, p)),
                   pl.BlockSpec((2, tq, 1), lambda p, i: (p, i, 0))),
        out_shape=(jax.ShapeDtypeStruct((t, MLA_WIDTH), F32),
                   jax.ShapeDtypeStruct((MLA_HEADS, t, 1), F32)),
        compiler_params=_cparams(("arbitrary", "arbitrary")),
    )(q, k, v)


def _mla_attn_bwd(q, k, v, o, do, lse):
    t = q.shape[0]
    tq = min(256, t)
    scale = MLA_QK ** -0.5

    def body(q_ref, k_ref, v_ref, o_ref, do_ref, lse_ref, dq_ref, dk_ref, dv_ref):
        @pl.when(pl.program_id(1) == 0)
        def _():
            dk_ref[...] = jnp.zeros_like(dk_ref)
            dv_ref[...] = jnp.zeros_like(dv_ref)

        def tile(i):
            keys = (i + 1) * tq
            causal, low = _attn_masks(tq, i)
            vp = v_ref[0:keys, :]
            do_all = do_ref[...]
            o_all = o_ref[...]
            dv_acc = jnp.zeros((keys, LANE), F32)
            for hh in range(2):
                sl = slice(hh * HEAD_PAD, (hh + 1) * HEAD_PAD)
                do_h = jnp.where(low if hh == 0 else jnp.logical_not(low), do_all, 0.0)
                dsum = jnp.sum(do_h * o_all, axis=-1, keepdims=True)
                qh = q_ref[:, sl]
                kh = k_ref[0:keys, sl]
                s = _dot_nt(qh, kh) * scale
                p = jnp.where(causal, jnp.exp(s - lse_ref[hh]), 0.0)
                do_b = do_h.astype(BF16)
                dp = _dot_nt(do_b, vp)
                ds = (p * (dp - dsum) * scale).astype(BF16)
                dq_ref[:, sl] = _dot(ds, kh).astype(BF16)
                dk_ref[0:keys, sl] += _dot_tn(ds, qh)
                dv_acc = dv_acc + _dot_tn(p.astype(BF16), do_b)
            dv_ref[0:keys, :] += dv_acc

        _for_each_query_tile(t // tq, tile)

    return pl.pallas_call(
        body, name="mla_attn_bwd",
        grid=(MLA_HEADS // 2, t // tq),
        in_specs=[pl.BlockSpec((tq, 2 * HEAD_PAD), lambda p, i: (i, p)),
                  pl.BlockSpec((t, 2 * HEAD_PAD), lambda p, i: (0, p)),
                  pl.BlockSpec((t, LANE), lambda p, i: (0, p)),
                  pl.BlockSpec((tq, LANE), lambda p, i: (i, p)),
                  pl.BlockSpec((tq, LANE), lambda p, i: (i, p)),
                  pl.BlockSpec((2, tq, 1), lambda p, i: (p, i, 0))],
        out_specs=(pl.BlockSpec((tq, 2 * HEAD_PAD), lambda p, i: (i, p)),
                   pl.BlockSpec((t, 2 * HEAD_PAD), lambda p, i: (0, p)),
                   pl.BlockSpec((t, LANE), lambda p, i: (0, p))),
        out_shape=(jax.ShapeDtypeStruct((t, MLA_HEADS * HEAD_PAD), BF16),
                   jax.ShapeDtypeStruct((t, MLA_HEADS * HEAD_PAD), F32),
                   jax.ShapeDtypeStruct((t, MLA_WIDTH), F32)),
        compiler_params=_cparams(("arbitrary", "arbitrary")),
    )(q, k, v, o, do, lse)


def _gla_chunk_terms(q_ref, k_ref, la_ref, h, tri):
    sl = slice(h * GLA_HK, (h + 1) * GLA_HK)
    b = _dot_exact(tri, la_ref[:, sl])
    bl = b[GLA_CHUNK - 1:GLA_CHUNK, :]
    kc = k_ref[:, sl]
    q_in = (q_ref[:, sl] * (GLA_HK ** -0.5)) * jnp.exp(b)
    k_in = kc * jnp.exp(-b)
    k_st = kc * jnp.exp(bl - b)
    return b, bl, q_in, k_in, k_st


def _tri(c, lower):
    r = lax.broadcasted_iota(jnp.int32, (c, c), 0)
    cc = lax.broadcasted_iota(jnp.int32, (c, c), 1)
    return jnp.where(r >= cc if lower else r <= cc, 1.0, 0.0).astype(F32)


def _gla_fwd(proj, log_a):
    t = proj.shape[0]
    c = GLA_CHUNK
    n = t // c

    def body(q_ref, k_ref, v_ref, la_ref, o_ref, sp_ref, st_ref):
        @pl.when(pl.program_id(0) == 0)
        def _():
            st_ref[...] = jnp.zeros_like(st_ref)

        tri = _tri(c, True)
        for h in range(GLA_HEADS):
            _, bl, q_in, k_in, k_st = _gla_chunk_terms(q_ref, k_ref, la_ref, h, tri)
            vs = slice(h * GLA_HV, (h + 1) * GLA_HV)
            vv = v_ref[:, vs].astype(BF16)
            qb = q_in.astype(BF16)
            attn = _dot_nt(qb, k_in.astype(BF16)) * tri
            st = st_ref[h]
            sp_ref[0, h] = st
            o_ref[:, vs] = _dot(attn.astype(BF16), vv) + _dot_nt(qb, st.astype(BF16))
            st_ref[h] = st * jnp.exp(bl) + _dot_tn(vv, k_st.astype(BF16))

    return pl.pallas_call(
        body, name="gla_fwd",
        grid=(n,),
        in_specs=[pl.BlockSpec((c, GLA_DK), lambda i: (i, P_QG // GLA_DK)),
                  pl.BlockSpec((c, GLA_DK), lambda i: (i, P_KG // GLA_DK)),
                  pl.BlockSpec((c, GLA_DV), lambda i: (i, P_VG // GLA_DV)),
                  pl.BlockSpec((c, GLA_DK), lambda i: (i, 0))],
        out_specs=(pl.BlockSpec((c, GLA_DV), lambda i: (i, 0)),
                   pl.BlockSpec((1, GLA_HEADS, GLA_HV, GLA_HK), lambda i: (i, 0, 0, 0))),
        out_shape=(jax.ShapeDtypeStruct((t, GLA_DV), F32),
                   jax.ShapeDtypeStruct((n, GLA_HEADS, GLA_HV, GLA_HK), F32)),
        scratch_shapes=[pltpu.VMEM((GLA_HEADS, GLA_HV, GLA_HK), F32)],
        compiler_params=_cparams(("arbitrary",)),
    )(proj, proj, proj, log_a)


def _gla_bwd(proj, log_a, do, states):
    t = proj.shape[0]
    c = GLA_CHUNK
    n = t // c

    def body(q_ref, k_ref, v_ref, la_ref, do_ref, sp_ref, dg_ref, dla_ref, ds_ref):
        @pl.when(pl.program_id(0) == 0)
        def _():
            ds_ref[...] = jnp.zeros_like(ds_ref)

        tri = _tri(c, True)
        tri_t = _tri(c, False)
        for h in range(GLA_HEADS):
            b, bl, q_in, k_in, k_st = _gla_chunk_terms(q_ref, k_ref, la_ref, h, tri)
            ks_ = slice(h * GLA_HK, (h + 1) * GLA_HK)
            vs = slice(h * GLA_HV, (h + 1) * GLA_HV)
            vv = v_ref[:, vs].astype(BF16)
            do_h = do_ref[:, vs]
            qb, kb, ksb = q_in.astype(BF16), k_in.astype(BF16), k_st.astype(BF16)
            attn = (_dot_nt(qb, kb) * tri).astype(BF16)
            st = sp_ref[0, h]
            dst = ds_ref[h]
            dstb = dst.astype(BF16)
            dattn = (_dot_nt(do_h, vv) * tri).astype(BF16)
            dg_ref[:, P_VG + h * GLA_HV:P_VG + (h + 1) * GLA_HV] = (
                _dot_tn(attn, do_h) + _dot_nt(ksb, dstb)).astype(BF16)
            dq_in = _dot(dattn, kb) + _dot(do_h, st.astype(BF16))
            dk_in = _dot_tn(dattn, qb)
            dk_st = _dot(vv, dstb)
            ebl = jnp.exp(bl)
            d_ebl = jnp.sum(st * dst, axis=0, keepdims=True)
            ds_ref[h] = _dot_tn(do_h, qb) + dst * ebl
            dg_ref[:, P_QG + h * GLA_HK:P_QG + (h + 1) * GLA_HK] = (
                dq_in * (GLA_HK ** -0.5) * jnp.exp(b)).astype(BF16)
            dg_ref[:, P_KG + h * GLA_HK:P_KG + (h + 1) * GLA_HK] = (
                dk_in * jnp.exp(-b) + dk_st * jnp.exp(bl - b)).astype(BF16)
            db = dq_in * q_in - dk_in * k_in - dk_st * k_st
            dbl = jnp.sum(dk_st * k_st, axis=0, keepdims=True) + d_ebl * ebl
            dla_ref[:, ks_] = _dot_exact(tri_t, db) + dbl

    rev = lambda i: n - 1 - i
    gw = P_GROUPS[0][1]
    return pl.pallas_call(
        body, name="gla_bwd",
        grid=(n,),
        in_specs=[pl.BlockSpec((c, GLA_DK), lambda i: (rev(i), P_QG // GLA_DK)),
                  pl.BlockSpec((c, GLA_DK), lambda i: (rev(i), P_KG // GLA_DK)),
                  pl.BlockSpec((c, GLA_DV), lambda i: (rev(i), P_VG // GLA_DV)),
                  pl.BlockSpec((c, GLA_DK), lambda i: (rev(i), 0)),
                  pl.BlockSpec((c, GLA_DV), lambda i: (rev(i), 0)),
                  pl.BlockSpec((1, GLA_HEADS, GLA_HV, GLA_HK), lambda i: (rev(i), 0, 0, 0))],
        out_specs=(pl.BlockSpec((c, gw), lambda i: (rev(i), 0)),
                   pl.BlockSpec((c, GLA_DK), lambda i: (rev(i), 0))),
        out_shape=(jax.ShapeDtypeStruct((t, gw), BF16), jax.ShapeDtypeStruct((t, GLA_DK), F32)),
        scratch_shapes=[pltpu.VMEM((GLA_HEADS, GLA_HV, GLA_HK), F32)],
        compiler_params=_cparams(("arbitrary",)),
    )(proj, proj, proj, log_a, do, states)


def _post(o_mla, proj, o_gla, x, target, g_gla, g_final, w_pm, w_pg, w_o):
    t = x.shape[0]
    tm = min(128, t)
    g0, gw = P_GROUPS[1]

    def body(om_ref, zg_ref, gm_ref, gg_ref, zm_ref, og_ref, x_ref, tg_ref, ggla_ref, gf_ref,
             wpm_ref, wpg_ref, wo_ref,
             dx2_ref, dom_ref, dog_ref, dg_ref,
             mg_ref, um_ref, ug_ref, dym_ref, dyg_ref, loss_ref, dgf_ref, dggla_ref):
        @pl.when(pl.program_id(0) == 0)
        def _():
            loss_ref[...] = jnp.zeros_like(loss_ref)
            dgf_ref[...] = jnp.zeros_like(dgf_ref)
            dggla_ref[...] = jnp.zeros_like(dggla_ref)

        om = om_ref[...]
        zm = zm_ref[...]
        sm = _sigmoid(zm)
        silu_m = zm * sm
        um = (om * silu_m).astype(BF16)
        um_ref[...] = um
        ym = _dot(um, wpm_ref[...])

        ggla = ggla_ref[...]
        zg = zg_ref[...]
        sg = _sigmoid(zg)
        silu_g = zg * sg
        xhat, rstd, on = [], [], []
        for h in range(GLA_HEADS):
            blk = og_ref[:, h * GLA_HV:(h + 1) * GLA_HV]
            r = lax.rsqrt(jnp.mean(blk * blk, axis=-1, keepdims=True) + EPS)
            xhat.append(blk * r)
            rstd.append(r)
            on.append(xhat[h] * ggla)
        on = jnp.concatenate(on, axis=-1)
        ug = (on * silu_g).astype(BF16)
        ug_ref[...] = ug
        yg = _dot(ug, wpg_ref[...])

        sgm = _sigmoid(gm_ref[...])
        sgg = _sigmoid(gg_ref[...])
        merged = (sgm * ym + sgg * yg).astype(BF16)
        mg_ref[...] = merged
        x2 = x_ref[...] + _dot(merged, wo_ref[...])
        gf = gf_ref[...]
        rf = lax.rsqrt(jnp.mean(x2 * x2, axis=-1, keepdims=True) + EPS)
        xh = x2 * rf
        err = xh * gf - tg_ref[...]
        loss_ref[...] += 0.5 * jnp.sum(jnp.mean(err * err, axis=-1, keepdims=True))

        dy = err * (1.0 / D_MODEL)
        dgf_ref[...] += jnp.sum(dy * xh, axis=0, keepdims=True)
        dxh = dy * gf
        dx2 = rf * (dxh - xh * jnp.mean(dxh * xh, axis=-1, keepdims=True))
        dx2_ref[...] = dx2
        dmerged = _dot_nt(dx2.astype(BF16), wo_ref[...])
        dym = (dmerged * sgm).astype(BF16)
        dyg = (dmerged * sgg).astype(BF16)
        dym_ref[...] = dym
        dyg_ref[...] = dyg
        dg_ref[:, P_GMLA - g0:P_GMLA - g0 + D_MODEL] = (dmerged * ym * sgm * (1.0 - sgm)).astype(BF16)
        dg_ref[:, P_GGLA - g0:P_GGLA - g0 + D_MODEL] = (dmerged * yg * sgg * (1.0 - sgg)).astype(BF16)
        dum = _dot_nt(dym, wpm_ref[...])
        dom_ref[...] = dum * silu_m
        dg_ref[:, P_ZMLA - g0:P_ZMLA - g0 + MLA_WIDTH] = (
            dum * om * (sm * (1.0 + zm * (1.0 - sm)))).astype(BF16)
        dug = _dot_nt(dyg, wpg_ref[...])
        dg_ref[:, P_ZGLA - g0:P_ZGLA - g0 + GLA_DV] = (
            dug * on * (sg * (1.0 + zg * (1.0 - sg)))).astype(BF16)
        don = dug * silu_g
        dggla = jnp.zeros((1, GLA_HV), F32)
        for h in range(GLA_HEADS):
            hs = slice(h * GLA_HV, (h + 1) * GLA_HV)
            don_h = don[:, hs]
            dggla = dggla + jnp.sum(don_h * xhat[h], axis=0, keepdims=True)
            dxh_h = don_h * ggla
            dog_ref[:, hs] = (rstd[h] * (dxh_h - xhat[h] * jnp.mean(dxh_h * xhat[h], axis=-1,
                                                                     keepdims=True))).astype(BF16)
        dggla_ref[...] += dggla

    row = lambda w: pl.BlockSpec((tm, w), lambda i: (i, 0))
    pcol = lambda w, off: pl.BlockSpec((tm, w), lambda i: (i, off // w))
    full = lambda a: pl.BlockSpec(a.shape, lambda i: (0, 0))
    sds = jax.ShapeDtypeStruct
    return pl.pallas_call(
        body, name="post_fwd_bwd",
        grid=(t // tm,),
        in_specs=[row(MLA_WIDTH), pcol(GLA_DV, P_ZGLA), pcol(D_MODEL, P_GMLA), pcol(D_MODEL, P_GGLA),
                  pcol(MLA_WIDTH, P_ZMLA), row(GLA_DV), row(D_MODEL), row(D_MODEL),
                  full(g_gla), full(g_final), full(w_pm), full(w_pg), full(w_o)],
        out_specs=(row(D_MODEL), row(MLA_WIDTH), row(GLA_DV), row(gw),
                   row(D_MODEL), row(MLA_WIDTH), row(GLA_DV), row(D_MODEL), row(D_MODEL),
                   pl.BlockSpec((1, LANE), lambda i: (0, 0)),
                   pl.BlockSpec((1, D_MODEL), lambda i: (0, 0)),
                   pl.BlockSpec((1, GLA_HV), lambda i: (0, 0))),
        out_shape=(sds((t, D_MODEL), F32), sds((t, MLA_WIDTH), F32), sds((t, GLA_DV), BF16),
                   sds((t, gw), BF16),
                   sds((t, D_MODEL), BF16), sds((t, MLA_WIDTH), BF16), sds((t, GLA_DV), BF16),
                   sds((t, D_MODEL), BF16), sds((t, D_MODEL), BF16),
                   sds((1, LANE), F32), sds((1, D_MODEL), F32), sds((1, GLA_HV), F32)),
        compiler_params=_cparams(("arbitrary",)),
    )(o_mla, proj, proj, proj, proj, o_gla, x, target, g_gla, g_final, w_pm, w_pg, w_o)


def _mla_prep_bwd(dq, dk, dv, dla, pre, proj, rq, rkv, g_q, g_kv, w_uq_p, w_k_p, w_v, w_gate_p,
                  rc, rsn, rsp):
    t = proj.shape[0]
    tm = min(256, t)
    gw = P_GROUPS[2][1]

    def body(dq_ref, dk_ref, dv_ref, dla_ref, pre_ref, cq_ref, ckv_ref, rq_ref, rkv_ref,
             gq_ref, gkv_ref, wuq_ref, wk_ref, wv_ref, wg_ref, c_ref, sn_ref, sp_ref,
             dg_ref, dqpre_ref, dpre_ref, dgq_ref, dgkv_ref, dbg_ref):
        @pl.when(pl.program_id(0) == 0)
        def _():
            dgq_ref[...] = jnp.zeros_like(dgq_ref)
            dgkv_ref[...] = jnp.zeros_like(dgkv_ref)
            dbg_ref[...] = jnp.zeros_like(dbg_ref)

        c, sn, sp = c_ref[...], sn_ref[...], sp_ref[...]
        dkr = jnp.zeros((tm, LANE), F32)
        for h in range(MLA_HEADS):
            sl = slice(h * HEAD_PAD, (h + 1) * HEAD_PAD)
            dqpre_ref[:, sl] = _rope_bwd(dq_ref[:, sl].astype(F32), c, sn, sp).astype(BF16)
            dkr = dkr + dk_ref[:, sl]
        dcqn = _dot_nt(dqpre_ref[...], wuq_ref[...])
        rq = rq_ref[...]
        xh = cq_ref[:, :MLA_Q_RANK] * rq
        dgq_ref[...] += jnp.sum(dcqn * xh, axis=0, keepdims=True)
        dxh = dcqn * gq_ref[...]
        dcq = rq * (dxh - xh * jnp.mean(dxh * xh, axis=-1, keepdims=True))
        dg_ref[:, :MLA_Q_RANK] = dcq.astype(BF16)
        dg_ref[:, MLA_Q_RANK:512] = jnp.zeros((tm, 512 - MLA_Q_RANK), BF16)

        dckvn = _dot_nt(dk_ref[...].astype(BF16), wk_ref[...]) + \
            _dot_nt(dv_ref[...].astype(BF16), wv_ref[...])
        rkv = rkv_ref[...]
        xh = ckv_ref[...] * rkv
        dgkv_ref[...] += jnp.sum(dckvn * xh, axis=0, keepdims=True)
        dxh = dckvn * gkv_ref[...]
        dg_ref[:, P_CKV - P_CQ:P_CKV - P_CQ + MLA_KV_RANK] = (
            rkv * (dxh - xh * jnp.mean(dxh * xh, axis=-1, keepdims=True))).astype(BF16)

        dpre = dla_ref[...] * (1.0 / GLA_GATE_NORM) * (1.0 - _sigmoid(pre_ref[...]))
        dbg_ref[...] += jnp.sum(dpre, axis=0, keepdims=True)
        dpre = dpre.astype(BF16)
        dpre_ref[...] = dpre
        lane = lax.broadcasted_iota(jnp.int32, (tm, LANE), 1)
        in_kr = jnp.logical_and(lane >= MISC_KR, lane < MISC_KR + MLA_ROPE)
        dmisc = jnp.where(in_kr, _rope_bwd(dkr, c, sn, sp), 0.0) + _dot_nt(dpre, wg_ref[...])
        dg_ref[:, P_MISC - P_CQ:P_MISC - P_CQ + LANE] = dmisc.astype(BF16)

    hq = MLA_HEADS * HEAD_PAD
    row = lambda w: pl.BlockSpec((tm, w), lambda i: (i, 0))
    full = lambda a: pl.BlockSpec(a.shape, lambda i: (0, 0))
    acc = lambda w: pl.BlockSpec((1, w), lambda i: (0, 0))
    sds = jax.ShapeDtypeStruct
    return pl.pallas_call(
        body, name="mla_prep_bwd",
        grid=(t // tm,),
        in_specs=[row(hq), row(hq), row(MLA_WIDTH), row(GLA_DK), row(GLA_DK),
                  pl.BlockSpec((tm, 512), lambda i: (i, P_CQ // 512)),
                  pl.BlockSpec((tm, MLA_KV_RANK), lambda i: (i, P_CKV // MLA_KV_RANK)),
                  row(1), row(1), full(g_q), full(g_kv), full(w_uq_p), full(w_k_p), full(w_v),
                  full(w_gate_p), row(LANE), row(LANE), row(LANE)],
        out_specs=(row(gw), row(hq), row(GLA_DK),
                   acc(MLA_Q_RANK), acc(MLA_KV_RANK), acc(GLA_DK)),
        out_shape=(sds((t, gw), BF16), sds((t, hq), BF16), sds((t, GLA_DK), BF16),
                   sds((1, MLA_Q_RANK), F32), sds((1, MLA_KV_RANK), F32), sds((1, GLA_DK), F32)),
        compiler_params=_cparams(("arbitrary",)),
    )(dq, dk, dv, dla, pre, proj, proj, rq, rkv, g_q, g_kv, w_uq_p, w_k_p, w_v, w_gate_p,
      rc, rsn, rsp)


def _inproj_bwd(dgroups, w_p, x, rstd, g_in, dx2):
    t = x.shape[0]
    tm = min(256, t)

    def body(d0_ref, d1_ref, d2_ref, w_ref, x_ref, r_ref, g_ref, dx2_ref, dx_ref, dg_ref):
        @pl.when(pl.program_id(0) == 0)
        def _():
            dg_ref[...] = jnp.zeros_like(dg_ref)

        dh = jnp.zeros((tm, D_MODEL), F32)
        for d_ref, (off, width) in zip((d0_ref, d1_ref, d2_ref), P_GROUPS):
            dh = dh + _dot_nt(d_ref[...], w_ref[:, off:off + width])
        r = r_ref[...]
        xh = x_ref[...] * r
        dg_ref[...] += jnp.sum(dh * xh, axis=0, keepdims=True)
        dxh = dh * g_ref[...]
        dx_ref[...] = dx2_ref[...] + r * (dxh - xh * jnp.mean(dxh * xh, axis=-1, keepdims=True))

    row = lambda w: pl.BlockSpec((tm, w), lambda i: (i, 0))
    return pl.pallas_call(
        body, name="inproj_bwd",
        grid=(t // tm,),
        in_specs=[row(P_GROUPS[0][1]), row(P_GROUPS[1][1]), row(P_GROUPS[2][1]),
                  pl.BlockSpec((D_MODEL, P_TOTAL), lambda i: (0, 0)),
                  row(D_MODEL), row(1), pl.BlockSpec((1, D_MODEL), lambda i: (0, 0)), row(D_MODEL)],
        out_specs=(row(D_MODEL), pl.BlockSpec((1, D_MODEL), lambda i: (0, 0))),
        out_shape=(jax.ShapeDtypeStruct((t, D_MODEL), F32),
                   jax.ShapeDtypeStruct((1, D_MODEL), F32)),
        compiler_params=_cparams(("arbitrary",)),
    )(*dgroups, w_p, x, rstd, g_in, dx2)


def _matmul(name, a, b, tm, tn, dtype=F32):
    kk, m = a.shape
    n = b.shape[1]

    def body(a_ref, b_ref, o_ref):
        o_ref[...] = _dot_tn(a_ref[...].astype(BF16), b_ref[...].astype(BF16)).astype(dtype)

    return pl.pallas_call(
        body, name=name,
        grid=(n // tn, m // tm),
        in_specs=[pl.BlockSpec((kk, tm), lambda j, i: (0, i)),
                  pl.BlockSpec((kk, tn), lambda j, i: (0, j))],
        out_specs=pl.BlockSpec((tm, tn), lambda j, i: (i, j)),
        out_shape=jax.ShapeDtypeStruct((m, n), dtype),
        compiler_params=_cparams(("arbitrary", "arbitrary")),
    )(a, b)


def _adamw_update(p_ref, w_ref, m_ref, v_ref, g_ref, d_ref, nm_ref, nv_ref):
    g = p_ref[0].astype(F32)
    for q in range(1, p_ref.shape[0]):
        g = g + p_ref[q].astype(F32)
    m_new = ADAM_B1 * m_ref[...] + (1.0 - ADAM_B1) * g
    v_new = ADAM_B2 * v_ref[...] + (1.0 - ADAM_B2) * (g * g)
    m_hat = m_new / (1.0 - ADAM_B1 ** ADAM_STEP)
    v_hat = v_new / (1.0 - ADAM_B2 ** ADAM_STEP)
    g_ref[...] = g
    nm_ref[...] = m_new
    nv_ref[...] = v_new
    d_ref[...] = -ADAM_LR * (m_hat / (jnp.sqrt(v_hat) + ADAM_EPS) + ADAM_WD * w_ref[...])


def _adamw_rows(name, parts, w, m, v, tr):
    _, rows, cols = w.shape

    def body(*refs):
        _adamw_update(*refs)

    blk = pl.BlockSpec((None, tr, cols), lambda i: (0, i, 0))
    out = jax.ShapeDtypeStruct((1, rows, cols), F32)
    return pl.pallas_call(
        body, name=name,
        grid=(rows // tr,),
        in_specs=[pl.BlockSpec((parts.shape[0], tr, cols), lambda i: (0, i, 0)), blk, blk, blk],
        out_specs=(blk, blk, blk, blk),
        out_shape=(out, out, out, out),
        compiler_params=_cparams(("arbitrary",)),
    )(parts, w, m, v)


def _adamw_group(parts, ws, ms, vs):
    n = len(ws)

    def body(*refs):
        ins, outs = refs[:4 * n], refs[4 * n:]
        for a in range(n):
            _adamw_update(ins[a], *[r.at[0] for r in (ins[n + a], ins[2 * n + a], ins[3 * n + a])],
                          *[r.at[0] for r in outs[4 * a:4 * a + 4]])

    vmem = lambda k: [pl.BlockSpec(memory_space=pltpu.VMEM) for _ in range(k)]
    out_shape = []
    for w in ws:
        out_shape += [jax.ShapeDtypeStruct(w.shape, F32)] * 4
    res = pl.pallas_call(
        body, name="adamw_small_weights",
        in_specs=vmem(4 * n), out_specs=tuple(vmem(4 * n)), out_shape=tuple(out_shape),
        compiler_params=_cparams(),
    )(*parts, *ws, *ms, *vs)
    return [res[4 * a:4 * a + 4] for a in range(n)]


def _rope_tables(positions):
    half = MLA_ROPE // 2
    freqs = ROPE_THETA ** (-jnp.arange(half, dtype=F32) / half)
    ang = positions.astype(F32).reshape(-1, 1) * freqs
    cos, sin = jnp.cos(ang), jnp.sin(ang)
    t = ang.shape[0]
    one, zero = jnp.ones((t, MLA_NOPE), F32), jnp.zeros((t, half), F32)
    tail = jnp.zeros((t, LANE - MLA_QK), F32)
    rc = jnp.concatenate([one, cos, cos, tail], axis=1)
    rsn = jnp.concatenate([0.0 * one, -sin, zero, tail], axis=1)
    rsp = jnp.concatenate([0.0 * one, zero, sin, tail], axis=1)
    return rc, rsn, rsp


def _cols_full(g):
    return g.transpose(1, 0, 2)


def kernel(x, positions, g_in, w_in, g_q, w_uq, g_kv, w_ukv, w_gla_gate, b_gla_gate, g_gla, w_proj_mla, w_proj_gla, w_out, g_final, loss_target, m_g_in, m_w_in, m_g_q, m_w_uq, m_g_kv, m_w_ukv, m_w_gla_gate, m_b_gla_gate, m_g_gla, m_w_proj_mla, m_w_proj_gla, m_w_out, m_g_final, v_g_in, v_w_in, v_g_q, v_w_uq, v_g_kv, v_w_ukv, v_w_gla_gate, v_b_gla_gate, v_g_gla, v_w_proj_mla, v_w_proj_gla, v_w_out, v_g_final):
    t = x.shape[1]
    x2d = x.reshape(t, D_MODEL)
    tgt = loss_target.reshape(t, D_MODEL)
    g_final2 = g_final.reshape(1, D_MODEL)
    sharded = [(w_in, m_w_in, v_w_in), (w_uq, m_w_uq, v_w_uq), (w_ukv, m_w_ukv, v_w_ukv),
               (w_gla_gate, m_w_gla_gate, v_w_gla_gate), (w_proj_mla, m_w_proj_mla, v_w_proj_mla),
               (w_proj_gla, m_w_proj_gla, v_w_proj_gla), (w_out, m_w_out, v_w_out)]

    g_w_in, g_uq, g_ukv, g_gate, g_pm, g_pg, g_o = _all_gather(
        [s[0][0].astype(BF16) for s in sharded])
    w_in_p = _weights_to_p(g_w_in)
    w_uq_p = jnp.pad(_cols_full(g_uq), ((0, 0), (0, 0), (0, HEAD_PAD - MLA_QK))).reshape(
        MLA_Q_RANK, MLA_HEADS * HEAD_PAD)
    ukv = _cols_full(g_ukv)
    w_k_p = jnp.pad(ukv[:, :, :MLA_NOPE], ((0, 0), (0, 0), (0, HEAD_PAD - MLA_NOPE))).reshape(
        MLA_KV_RANK, MLA_HEADS * HEAD_PAD)
    w_v = ukv[:, :, MLA_NOPE:].reshape(MLA_KV_RANK, MLA_WIDTH)
    w_gate_p = jnp.pad(_cols_full(g_gate).reshape(GLA_GATE_RANK, GLA_DK),
                       ((MISC_ALR, LANE - MISC_ALR - GLA_GATE_RANK), (0, 0)))
    w_pm = _cols_full(g_pm).reshape(MLA_WIDTH, D_MODEL)
    w_pg = g_pg.reshape(GLA_DV, D_MODEL)
    w_o = g_o.reshape(D_MODEL, D_MODEL)
    rc, rsn, rsp = _rope_tables(positions)

    proj, h, rstd = _inproj(x2d, g_in, w_in_p)
    q, k, v, log_a, pre, cqn, ckvn, rq, rkv, misc = _mla_prep(
        proj, g_q, g_kv, w_uq_p, w_k_p, w_v, w_gate_p, b_gla_gate, rc, rsn, rsp)
    o_mla, lse = _mla_attn_fwd(q, k, v)
    o_gla, states = _gla_fwd(proj, log_a)

    (dx2, do_mla, do_gla, d_out, merged, um, ug, dym, dyg, loss_p, dg_final,
     dg_gla) = _post(o_mla, proj, o_gla, x2d, tgt, g_gla, g_final2, w_pm, w_pg, w_o)

    dq, dk, dv = _mla_attn_bwd(q, k, v, o_mla, do_mla, lse)
    d_gla, dla = _gla_bwd(proj, log_a, do_gla, states)
    d_lat, dqpre, dpre, dg_q, dg_kv, db_gate = _mla_prep_bwd(
        dq, dk, dv, dla, pre, proj, rq, rkv, g_q, g_kv, w_uq_p, w_k_p, w_v, w_gate_p, rc, rsn, rsp)

    dgroups = (d_gla, d_out, d_lat)
    grad_x, dg_in = _inproj_bwd(dgroups, w_in_p, x2d, rstd, g_in, dx2)

    dw_groups = [_matmul("dw_in_%d" % i, h, dgroups[i], 512, tn)
                 for i, tn in enumerate((512, 512, 896))]
    p_in = _grads_to_shards(dw_groups)
    dw_uq = _matmul("dw_uq", cqn, dqpre, MLA_Q_RANK, 512, BF16)
    p_uq = dw_uq.reshape(MLA_Q_RANK, MLA_HEADS, HEAD_PAD)[:, :, :MLA_QK].transpose(1, 0, 2)
    dw_k = _matmul("dw_uk", ckvn, dk, MLA_KV_RANK, 512, BF16)
    dw_v = _matmul("dw_uv", ckvn, dv, MLA_KV_RANK, 512, BF16)
    p_ukv = jnp.concatenate(
        [dw_k.reshape(MLA_KV_RANK, MLA_HEADS, HEAD_PAD)[:, :, :MLA_NOPE],
         dw_v.reshape(MLA_KV_RANK, MLA_HEADS, MLA_VDIM)], axis=2).transpose(1, 0, 2)
    dw_gate = _matmul("dw_gate", misc, dpre, LANE, 512, BF16)
    p_gate = dw_gate[MISC_ALR:MISC_ALR + GLA_GATE_RANK].reshape(
        GLA_GATE_RANK, N_DEV, GLA_DK // N_DEV).transpose(1, 0, 2)
    p_pm = _matmul("dw_proj_mla", um, dym, 512, 512, BF16).reshape(
        MLA_WIDTH, N_DEV, D_MODEL // N_DEV).transpose(1, 0, 2)
    p_pg = _matmul("dw_proj_gla", ug, dyg, 512, 512, BF16).reshape(N_DEV, -1, D_MODEL)
    p_o = _matmul("dw_out", merged, dx2, 512, 512, BF16).reshape(N_DEV, -1, D_MODEL)
    small = jnp.concatenate([dg_in.reshape(-1), dg_q.reshape(-1), dg_kv.reshape(-1),
                             db_gate.reshape(-1), dg_gla.reshape(-1), dg_final.reshape(-1),
                             loss_p[0, :1]])
    small = jnp.pad(small, (0, SMALL_ROWS * LANE - small.shape[0])).reshape(SMALL_ROWS, LANE)

    recv = _exchange_grads([p_in, p_uq, p_ukv, p_gate, p_pm, p_pg, p_o], small)
    big = [_adamw_rows("adamw_w_in", recv[0], *sharded[0], 128)]
    big += _adamw_group(recv[1:7], *[[s[j] for s in sharded[1:]] for j in range(3)])
    replicated = [(g_in, m_g_in, v_g_in), (g_q, m_g_q, v_g_q), (g_kv, m_g_kv, v_g_kv),
                  (b_gla_gate, m_b_gla_gate, v_b_gla_gate), (g_gla, m_g_gla, v_g_gla),
                  (g_final, m_g_final, v_g_final)]
    spacks = [jnp.pad(jnp.concatenate([s[j].reshape(-1) for s in replicated]),
                      (0, SMALL_ROWS * LANE - sum(SMALL_SIZES))).reshape(1, SMALL_ROWS, LANE)
              for j in range(3)]
    tiny = _adamw_rows("adamw_gains", recv[7], spacks[0], spacks[1], spacks[2], SMALL_ROWS)

    outs = {}
    names = ("w_in", "w_uq", "w_ukv", "w_gla_gate", "w_proj_mla", "w_proj_gla", "w_out")
    for j, kind in enumerate(("grad", "delta", "new_m", "new_v")):
        for name, res in zip(names, big):
            outs[kind, name] = res[j]
        flat = tiny[j].reshape(-1)
        off = 0
        for name, size in zip(("g_in", "g_q", "g_kv", "b_gla_gate", "g_gla", "g_final"), SMALL_SIZES):
            shape = (size,) if name == "g_final" else (1, size)
            outs[kind, name] = flat[off:off + size].reshape(shape)
            off += size
    loss = tiny[0].reshape(-1)[sum(SMALL_SIZES)]
    order = ("g_in", "w_in", "g_q", "w_uq", "g_kv", "w_ukv", "w_gla_gate", "b_gla_gate", "g_gla",
             "w_proj_mla", "w_proj_gla", "w_out", "g_final")
    result = [loss, grad_x.reshape(1, t, D_MODEL)]
    for kind in ("grad", "delta", "new_m", "new_v"):
        result += [outs[kind, name] for name in order]
    return tuple(result)
```

```python
import math

import jax
import jax.numpy as jnp
from jax import lax
from jax.experimental import pallas as pl
from jax.experimental.pallas import tpu as pltpu

F32 = jnp.float32
BF16 = jnp.bfloat16
MESH = pl.DeviceIdType.MESH
N_DEV = 8

D_MODEL = 1024
EPS = 1e-6
MLA_HEADS = 8
MLA_NOPE = 64
MLA_ROPE = 32
MLA_VDIM = 64
MLA_Q_RANK = 384
MLA_KV_RANK = 256
MLA_QK = MLA_NOPE + MLA_ROPE
MLA_WIDTH = MLA_HEADS * MLA_VDIM
ROPE_THETA = 10000.0
GLA_HEADS = 4
GLA_DK = 512
GLA_DV = 1024
GLA_HK = 128
GLA_HV = 256
GLA_GATE_RANK = 16
GLA_GATE_NORM = 16.0
GLA_CHUNK = 64
D_IN = 6320

ADAM_LR = 0.001
ADAM_B1 = 0.9
ADAM_B2 = 0.999
ADAM_EPS = 1e-08
ADAM_WD = 0.01
ADAM_STEP = 10

LANE = 128
HEAD_PAD = 128
VMEM_LIMIT = 48 * 1024 * 1024

P_VG, P_QG, P_KG = 0, 1024, 1536
P_ZGLA, P_GMLA, P_GGLA, P_ZMLA = 2048, 3072, 4096, 5120
P_CQ, P_CKV, P_MISC = 5632, 6144, 6400
P_TOTAL = 6528
P_GROUPS = ((0, 2048), (2048, 3584), (5632, 896))
MISC_KR = 64
MISC_ALR = 96
SHARD_COLS = D_IN // N_DEV
P_COMPONENTS = ((0, 384, P_CQ), (384, 256, P_CKV), (640, 32, P_MISC + MISC_KR), (672, 512, P_ZMLA),
                (1184, 512, P_QG), (1696, 512, P_KG), (2208, 1024, P_VG),
                (3232, 16, P_MISC + MISC_ALR), (3248, 1024, P_ZGLA), (4272, 1024, P_GMLA),
                (5296, 1024, P_GGLA))

SMALL_SIZES = (1024, 384, 256, 512, 256, 1024)
SMALL_ROWS = 32


def _segments():
    segs = []
    for g0, n, p0 in P_COMPONENTS:
        g = g0
        while g < g0 + n:
            d = g // SHARD_COLS
            end = min(g0 + n, (d + 1) * SHARD_COLS)
            segs.append((d, g - d * SHARD_COLS, end - g, p0 + g - g0))
            g = end
    return segs


def _cparams(sem=None):
    if sem is None:
        return pltpu.CompilerParams(vmem_limit_bytes=VMEM_LIMIT)
    return pltpu.CompilerParams(dimension_semantics=sem, vmem_limit_bytes=VMEM_LIMIT)


def _sigmoid(v):
    return 1.0 / (1.0 + jnp.exp(-v))


def _dot(a, b):
    return jnp.dot(a, b, preferred_element_type=F32)


def _dot_nt(a, b):
    return lax.dot_general(a, b, (((1,), (1,)), ((), ())), preferred_element_type=F32)


def _dot_tn(a, b):
    return lax.dot_general(a, b, (((0,), (0,)), ((), ())), preferred_element_type=F32)


def _dot_exact(a, b):
    return jnp.dot(a, b, preferred_element_type=F32, precision=lax.Precision.HIGHEST)


def _rope_fwd(blk, c, sn, sp):
    return blk * c + pltpu.roll(blk, LANE - 16, 1) * sn + pltpu.roll(blk, 16, 1) * sp


def _rope_bwd(blk, c, sn, sp):
    return blk * c + pltpu.roll(blk * sn, 16, 1) + pltpu.roll(blk * sp, LANE - 16, 1)


def _mesh_pos():
    return lax.axis_index("x"), lax.axis_index("y"), lax.axis_index("c")


def _hbm_specs(n):
    return [pl.BlockSpec(memory_space=pltpu.HBM) for _ in range(n)]


def _all_gather(shards):
    n = len(shards)

    def body(*refs):
        x_refs, out_refs = refs[:n], refs[n:2 * n]
        send_sems, recv_sems, local_sems = refs[2 * n:]
        x, y, c = _mesh_pos()
        me, sibling = (x, y, c), (x, y, 1 - c)
        chips = [(1 - x, y), (x, 1 - y), (1 - x, 1 - y)]

        def slot(a, px, py, pc):
            return out_refs[a].at[4 * px + 2 * py + pc]

        def copies(k, block, to, own=False):
            return [pltpu.make_async_remote_copy(
                src_ref=x_refs[a] if own else slot(a, *block), dst_ref=slot(a, *block),
                send_sem=send_sems.at[k, a], recv_sem=recv_sems.at[k, a],
                device_id=to, device_id_type=MESH) for a in range(n)]

        mine = [pltpu.make_async_copy(x_refs[a], slot(a, *me), local_sems.at[a]) for a in range(n)]
        for cp in mine:
            cp.start()
        first = copies(0, me, sibling, own=True)
        for j, chip in enumerate(chips):
            first += copies(1 + j, me, (*chip, c), own=True)
        for cp in first:
            cp.start()
        passed = []
        for j, chip in enumerate(chips):
            for cp in copies(1 + j, (*chip, c), me):
                cp.wait_recv()
            fwd = copies(4 + j, (*chip, c), sibling)
            for cp in fwd:
                cp.start()
            passed += fwd
        for cp in copies(0, sibling, me):
            cp.wait_recv()
        for j, chip in enumerate(chips):
            for cp in copies(4 + j, (*chip, 1 - c), me):
                cp.wait_recv()
        for cp in first + passed:
            cp.wait_send()
        for cp in mine:
            cp.wait()

    return pl.pallas_call(
        body, name="all_gather_weights",
        out_shape=tuple(jax.ShapeDtypeStruct((N_DEV,) + s.shape, s.dtype) for s in shards),
        in_specs=_hbm_specs(n), out_specs=tuple(_hbm_specs(n)),
        scratch_shapes=[pltpu.SemaphoreType.DMA((7, n)), pltpu.SemaphoreType.DMA((7, n)),
                        pltpu.SemaphoreType.DMA((n,))],
    )(*shards)


def _exchange_grads(parts, small):
    n = len(parts)
    rows_per_add = 256

    def body(*refs):
        p_refs, s_ref = refs[:n], refs[n]
        out_refs, sout_ref = refs[n + 1:2 * n + 1], refs[2 * n + 1]
        mine_v, recv_v = refs[2 * n + 2:3 * n + 2], refs[3 * n + 2:4 * n + 2]
        (d2d_send, d2d_recv, ici_send, ici_recv, s_send, s_recv, load_sems, own_sems,
         sown_sem) = refs[4 * n + 2:]
        x, y, c = _mesh_pos()
        me = 4 * x + 2 * y + c
        chips = [(x, y), (1 - x, y), (x, 1 - y), (1 - x, 1 - y)]

        sown = pltpu.make_async_copy(s_ref, sout_ref.at[me], sown_sem)
        sown.start()
        tiny = []
        for k in range(1, N_DEV):
            peer = (x ^ (k >> 2), y ^ ((k >> 1) & 1), c ^ (k & 1))
            tiny.append(pltpu.make_async_remote_copy(
                src_ref=s_ref, dst_ref=sout_ref.at[me], send_sem=s_send.at[k - 1],
                recv_sem=s_recv.at[k - 1], device_id=peer, device_id_type=MESH))
        for cp in tiny:
            cp.start()

        loads, swaps = [], []
        for a in range(n):
            for j, (px, py) in enumerate(chips):
                loads.append(pltpu.make_async_copy(
                    p_refs[a].at[4 * px + 2 * py + c], mine_v[a].at[j], load_sems.at[a, j]))
                swaps.append(pltpu.make_async_remote_copy(
                    src_ref=p_refs[a].at[4 * px + 2 * py + 1 - c], dst_ref=recv_v[a].at[j],
                    send_sem=d2d_send.at[a, j], recv_sem=d2d_recv.at[a, j],
                    device_id=(x, y, 1 - c), device_id_type=MESH))
        for cp in swaps + loads:
            cp.start()

        outgoing, own = [], []
        for a in range(n):
            rows = p_refs[a].shape[1]
            step = math.gcd(rows, rows_per_add)
            for j in range(4):
                loads[4 * a + j].wait()
                swaps[4 * a + j].wait_recv()

                @pl.loop(0, rows // step)
                def _(i):
                    rs = pl.ds(pl.multiple_of(i * step, step), step)
                    mine_v[a][j, rs, :] = (mine_v[a][j, rs, :].astype(F32)
                                           + recv_v[a][j, rs, :].astype(F32)).astype(BF16)

                if j == 0:
                    own.append(pltpu.make_async_copy(mine_v[a].at[0], out_refs[a].at[0],
                                                     own_sems.at[a]))
                    own[-1].start()
                else:
                    outgoing.append(pltpu.make_async_remote_copy(
                        src_ref=mine_v[a].at[j], dst_ref=out_refs[a].at[j],
                        send_sem=ici_send.at[a, j - 1], recv_sem=ici_recv.at[a, j - 1],
                        device_id=(*chips[j], c), device_id_type=MESH))
                    outgoing[-1].start()

        for cp in tiny + outgoing:
            cp.wait_recv()
        for cp in tiny + outgoing + swaps:
            cp.wait_send()
        for cp in own:
            cp.wait()
        sown.wait()

    outs = [jax.ShapeDtypeStruct((4,) + p.shape[1:], p.dtype) for p in parts]
    outs.append(jax.ShapeDtypeStruct((N_DEV,) + small.shape, small.dtype))
    stage = [pltpu.VMEM((4,) + p.shape[1:], p.dtype) for p in parts]
    return pl.pallas_call(
        body, name="exchange_grads",
        out_shape=tuple(outs),
        in_specs=_hbm_specs(n + 1), out_specs=tuple(_hbm_specs(n + 1)),
        scratch_shapes=stage + stage + [
            pltpu.SemaphoreType.DMA((n, 4)), pltpu.SemaphoreType.DMA((n, 4)),
            pltpu.SemaphoreType.DMA((n, 3)), pltpu.SemaphoreType.DMA((n, 3)),
            pltpu.SemaphoreType.DMA((7,)), pltpu.SemaphoreType.DMA((7,)),
            pltpu.SemaphoreType.DMA((n, 4)), pltpu.SemaphoreType.DMA((n,)),
            pltpu.SemaphoreType.DMA],
        compiler_params=_cparams(),
    )(*parts, small)


def _dev(d):
    return d >> 2, (d >> 1) & 1, d & 1


def _swap_add_in(name, parts, own_prev, dests):
    n = len(dests)
    _, rows, cols = parts.shape
    step = 256

    def body(p_ref, own_in_ref, h_ref, own_ref, mine_v, recv_v, send_sems, recv_sems, load_sems,
             store_sems, own_sem):
        del own_in_ref
        x, y, c = _mesh_pos()
        swaps, loads, stores, owns, keeps, mines = [], [], [], [], [], []
        for i, d in enumerate(dests):
            dx, dy, dc = _dev(d)
            keeps.append(c == dc)
            mines.append(jnp.logical_and(c == dc, jnp.logical_and(x == dx, y == dy)))
            swaps.append(pltpu.make_async_remote_copy(
                src_ref=p_ref.at[i], dst_ref=recv_v.at[i], send_sem=send_sems.at[i],
                recv_sem=recv_sems.at[i], device_id=(x, y, 1 - c), device_id_type=MESH))
            loads.append(pltpu.make_async_copy(p_ref.at[i], mine_v.at[i], load_sems.at[i]))
            stores.append(pltpu.make_async_copy(mine_v.at[i], h_ref.at[i], store_sems.at[i]))
            owns.append(pltpu.make_async_copy(mine_v.at[i], own_ref, own_sem))
            pl.when(keeps[i])(loads[i].start)
            pl.when(jnp.logical_not(keeps[i]))(swaps[i].start)
        for i in range(n):
            @pl.when(keeps[i])
            def _(i=i):
                loads[i].wait()
                swaps[i].wait_recv()

                @pl.loop(0, rows // step)
                def _(r):
                    rs = pl.ds(pl.multiple_of(r * step, step), step)
                    mine_v[i, rs, :] = (mine_v[i, rs, :].astype(F32)
                                        + recv_v[i, rs, :].astype(F32)).astype(BF16)

                stores[i].start()
                pl.when(mines[i])(owns[i].start)
        for i in range(n):
            pl.when(jnp.logical_not(keeps[i]))(swaps[i].wait_send)
            pl.when(keeps[i])(stores[i].wait)
            pl.when(mines[i])(owns[i].wait)

    stage = pltpu.VMEM((n, rows, cols), BF16)
    return pl.pallas_call(
        body, name=name,
        out_shape=(jax.ShapeDtypeStruct(parts.shape, BF16),
                   jax.ShapeDtypeStruct(own_prev.shape, BF16)),
        in_specs=_hbm_specs(2), out_specs=tuple(_hbm_specs(2)),
        input_output_aliases={1: 1},
        scratch_shapes=[stage, stage, pltpu.SemaphoreType.DMA((n,)), pltpu.SemaphoreType.DMA((n,)),
                        pltpu.SemaphoreType.DMA((n,)), pltpu.SemaphoreType.DMA((n,)),
                        pltpu.SemaphoreType.DMA],
        compiler_params=_cparams(),
    )(parts, own_prev)


def _ici_copies(h_ref, land_ref, send_sems, recv_sems, dests):
    x, y, c = _mesh_pos()
    sends, arrivals = [], []
    for i, d in enumerate(dests):
        dx, dy, dc = _dev(d)
        j = (x != dx).astype(jnp.int32) + 2 * (y != dy).astype(jnp.int32)
        slot = jnp.maximum(j - 1, 0)
        sends.append((jnp.logical_and(c == dc, j > 0), pltpu.make_async_remote_copy(
            src_ref=h_ref.at[i], dst_ref=land_ref.at[slot], send_sem=send_sems.at[i],
            recv_sem=recv_sems.at[slot], device_id=(dx, dy, dc), device_id_type=MESH)))
        arrivals.append((jnp.logical_and(c == dc, j == 0), [pltpu.make_async_remote_copy(
            src_ref=h_ref.at[i], dst_ref=land_ref.at[r], send_sem=send_sems.at[i],
            recv_sem=recv_sems.at[r], device_id=(dx, dy, dc), device_id_type=MESH)
            for r in range(3)]))
    return sends, arrivals


def _ici_start_in(name, h, land, dests):
    n = len(dests)

    def body(h_ref, land_ref, send_sems, recv_sems, h_thru, land_thru, token):
        del h_thru, land_thru
        sends, _ = _ici_copies(h_ref, land_ref, send_sems, recv_sems, dests)
        for go, cp in sends:
            pl.when(go)(cp.start)
        token[...] = jnp.zeros_like(token)

    hbm, sem = pl.BlockSpec(memory_space=pltpu.HBM), pl.BlockSpec(memory_space=pltpu.SEMAPHORE)
    return pl.pallas_call(
        body, name=name,
        out_shape=(pltpu.SemaphoreType.DMA((n,)), pltpu.SemaphoreType.DMA((3,)),
                   pltpu.HBM(h.shape, h.dtype), pltpu.HBM(land.shape, land.dtype),
                   jax.ShapeDtypeStruct((8, LANE), F32)),
        in_specs=(hbm, hbm),
        out_specs=(sem, sem, hbm, hbm, pl.BlockSpec(memory_space=pltpu.VMEM)),
        input_output_aliases={0: 2, 1: 3},
        compiler_params=pltpu.CompilerParams(
            has_side_effects=pltpu.SideEffectType.DATAFLOW_SIDE_EFFECTING),
    )(pltpu.with_memory_space_constraint(h, pltpu.HBM),
      pltpu.with_memory_space_constraint(land, pltpu.HBM))


def _ici_wait_in(name, started, land, after):
    k = len(started)

    def body(*refs):
        land_ref, after_ref = refs[3 * k], refs[3 * k + 1]
        del after_ref
        for s in range(k):
            h_ref, send_sems, recv_sems = refs[3 * s:3 * s + 3]
            sends, arrivals = _ici_copies(h_ref, land_ref, send_sems, recv_sems, started[s][3])
            for go, cp in sends:
                pl.when(go)(cp.wait_send)
            for here, cps in arrivals:
                for cp in cps:
                    pl.when(here)(cp.wait_recv)

    hbm, sem = pl.BlockSpec(memory_space=pltpu.HBM), pl.BlockSpec(memory_space=pltpu.SEMAPHORE)
    operands, specs = [], []
    for send_sems, recv_sems, h, _ in started:
        operands += [h, send_sems, recv_sems]
        specs += [hbm, sem, sem]
    return pl.pallas_call(
        body, name=name,
        out_shape=pltpu.HBM(land.shape, land.dtype),
        in_specs=tuple(specs) + (hbm, pl.BlockSpec(memory_space=pl.ANY)),
        out_specs=hbm,
        input_output_aliases={3 * k: 0},
        compiler_params=pltpu.CompilerParams(
            has_side_effects=pltpu.SideEffectType.DATAFLOW_SIDE_EFFECTING),
    )(*operands, land, after)


def _weights_to_p(gathered):
    tm = 64
    segs = sorted(_segments(), key=lambda s: s[3])

    def body(g_ref, o_ref):
        pieces, pos = [], 0
        for d, c0, n, p0 in segs:
            if p0 > pos:
                pieces.append(jnp.zeros((tm, p0 - pos), F32))
            pieces.append(g_ref[d, :, c0:c0 + n].astype(F32))
            pos = p0 + n
        pieces.append(jnp.zeros((tm, P_TOTAL - pos), F32))
        o_ref[...] = jnp.concatenate(pieces, axis=-1).astype(BF16)

    return pl.pallas_call(
        body, name="weights_to_layout",
        grid=(D_MODEL // tm,),
        in_specs=[pl.BlockSpec((N_DEV, tm, SHARD_COLS), lambda i: (0, i, 0))],
        out_specs=pl.BlockSpec((tm, P_TOTAL), lambda i: (i, 0)),
        out_shape=jax.ShapeDtypeStruct((D_MODEL, P_TOTAL), BF16),
        compiler_params=_cparams(("arbitrary",)),
    )(gathered)


def _group_of(p0):
    return max(i for i, (off, _) in enumerate(P_GROUPS) if off <= p0)


def _shard_groups(d):
    return sorted({_group_of(s[3]) for s in _segments() if s[0] == d})


def _grads_to_shards(name, groups, dests):
    tm = 64
    segs = _segments()
    used = sorted(groups)

    def body(*refs):
        g_refs, o_ref = dict(zip(used, refs[:-1])), refs[-1]
        for i, d in enumerate(dests):
            pieces = []
            for _, c0, n, p0 in sorted([s for s in segs if s[0] == d], key=lambda s: s[1]):
                gi = _group_of(p0)
                lo = p0 - P_GROUPS[gi][0]
                pieces.append(g_refs[gi][:, lo:lo + n])
            o_ref[i] = jnp.concatenate(pieces, axis=-1).astype(BF16)

    return pl.pallas_call(
        body, name=name,
        grid=(D_MODEL // tm,),
        in_specs=[pl.BlockSpec((tm, P_GROUPS[g][1]), lambda i: (i, 0)) for g in used],
        out_specs=pl.BlockSpec((len(dests), tm, SHARD_COLS), lambda i: (0, i, 0)),
        out_shape=jax.ShapeDtypeStruct((len(dests), D_MODEL, SHARD_COLS), BF16),
        compiler_params=_cparams(("arbitrary",)),
    )(*[groups[g] for g in used])


def _inproj(x, g_in, w_p):
    t = x.shape[0]
    tm = min(256, t)
    nj = 3
    tn = P_TOTAL // nj

    def body(x_ref, g_ref, w_ref, proj_ref, h_ref, r_ref):
        xf = x_ref[...]
        r = lax.rsqrt(jnp.mean(xf * xf, axis=-1, keepdims=True) + EPS)
        h = ((xf * r) * g_ref[...]).astype(BF16)
        proj_ref[...] = _dot(h, w_ref[...])

        @pl.when(pl.program_id(0) == 0)
        def _():
            h_ref[...] = h
            r_ref[...] = r

    first = lambda j, i: (jnp.where(j == 0, i, t // tm - 1), 0)
    return pl.pallas_call(
        body, name="inproj",
        grid=(nj, t // tm),
        in_specs=[pl.BlockSpec((tm, D_MODEL), lambda j, i: (i, 0)),
                  pl.BlockSpec((1, D_MODEL), lambda j, i: (0, 0)),
                  pl.BlockSpec((D_MODEL, tn), lambda j, i: (0, j))],
        out_specs=(pl.BlockSpec((tm, tn), lambda j, i: (i, j)),
                   pl.BlockSpec((tm, D_MODEL), first),
                   pl.BlockSpec((tm, 1), first)),
        out_shape=(jax.ShapeDtypeStruct((t, P_TOTAL), F32),
                   jax.ShapeDtypeStruct((t, D_MODEL), BF16),
                   jax.ShapeDtypeStruct((t, 1), F32)),
        compiler_params=_cparams(("arbitrary", "arbitrary")),
    )(x, g_in, w_p)


def _mla_prep(proj, g_q, g_kv, w_uq_p, w_k_p, w_v, w_gate_p, b_gate, rc, rsn, rsp):
    t = proj.shape[0]
    tm = min(256, t)
    hq = MLA_HEADS * HEAD_PAD

    def body(cq_ref, ckv_ref, misc_ref, gq_ref, gkv_ref, wuq_ref, wk_ref, wv_ref, wg_ref, bg_ref,
             c_ref, sn_ref, sp_ref,
             q_ref, k_ref, v_ref, la_ref, pre_ref, cqn_ref, ckvn_ref, rq_ref, rkv_ref, mb_ref):
        c, sn, sp = c_ref[...], sn_ref[...], sp_ref[...]
        cq = cq_ref[:, :MLA_Q_RANK]
        rq = lax.rsqrt(jnp.mean(cq * cq, axis=-1, keepdims=True) + EPS)
        cqn = ((cq * rq) * gq_ref[...]).astype(BF16)
        cqn_ref[...] = cqn
        rq_ref[...] = rq
        qpre = _dot(cqn, wuq_ref[...])
        ckv = ckv_ref[...]
        rkv = lax.rsqrt(jnp.mean(ckv * ckv, axis=-1, keepdims=True) + EPS)
        ckvn = ((ckv * rkv) * gkv_ref[...]).astype(BF16)
        ckvn_ref[...] = ckvn
        rkv_ref[...] = rkv
        kn = _dot(ckvn, wk_ref[...])
        v_ref[...] = _dot(ckvn, wv_ref[...]).astype(BF16)
        misc = misc_ref[...]
        krope = _rope_fwd(misc, c, sn, sp)
        for h in range(MLA_HEADS):
            sl = slice(h * HEAD_PAD, (h + 1) * HEAD_PAD)
            q_ref[:, sl] = _rope_fwd(qpre[:, sl], c, sn, sp).astype(BF16)
            k_ref[:, sl] = (kn[:, sl] + krope).astype(BF16)
        mb_ref[...] = misc.astype(BF16)
        pre = _dot(mb_ref[...], wg_ref[...]) + bg_ref[...]
        pre_ref[...] = pre
        la_ref[...] = (jnp.minimum(pre, 0.0) - jnp.log(1.0 + jnp.exp(-jnp.abs(pre)))) / GLA_GATE_NORM

    row = lambda w: pl.BlockSpec((tm, w), lambda i: (i, 0))
    full = lambda a: pl.BlockSpec(a.shape, lambda i: (0, 0))
    return pl.pallas_call(
        body, name="mla_prep",
        grid=(t // tm,),
        in_specs=[pl.BlockSpec((tm, 512), lambda i: (i, P_CQ // 512)),
                  pl.BlockSpec((tm, MLA_KV_RANK), lambda i: (i, P_CKV // MLA_KV_RANK)),
                  pl.BlockSpec((tm, LANE), lambda i: (i, P_MISC // LANE)),
                  full(g_q), full(g_kv), full(w_uq_p), full(w_k_p), full(w_v), full(w_gate_p),
                  full(b_gate), row(LANE), row(LANE), row(LANE)],
        out_specs=(row(hq), row(hq), row(MLA_WIDTH), row(GLA_DK), row(GLA_DK),
                   row(MLA_Q_RANK), row(MLA_KV_RANK), row(1), row(1), row(LANE)),
        out_shape=(jax.ShapeDtypeStruct((t, hq), BF16), jax.ShapeDtypeStruct((t, hq), BF16),
                   jax.ShapeDtypeStruct((t, MLA_WIDTH), BF16),
                   jax.ShapeDtypeStruct((t, GLA_DK), F32), jax.ShapeDtypeStruct((t, GLA_DK), F32),
                   jax.ShapeDtypeStruct((t, MLA_Q_RANK), BF16),
                   jax.ShapeDtypeStruct((t, MLA_KV_RANK), BF16),
                   jax.ShapeDtypeStruct((t, 1), F32), jax.ShapeDtypeStruct((t, 1), F32),
                   jax.ShapeDtypeStruct((t, LANE), BF16)),
        compiler_params=_cparams(("arbitrary",)),
    )(proj, proj, proj, g_q, g_kv, w_uq_p, w_k_p, w_v, w_gate_p, b_gate, rc, rsn, rsp)


def _attn_masks(tq, i):
    keys = (i + 1) * tq
    rows = i * tq + lax.broadcasted_iota(jnp.int32, (tq, keys), 0)
    cols = lax.broadcasted_iota(jnp.int32, (tq, keys), 1)
    lane = lax.broadcasted_iota(jnp.int32, (tq, LANE), 1)
    return cols <= rows, lane < MLA_VDIM


def _for_each_query_tile(n_tiles, fn):
    for i in range(n_tiles):
        pl.when(pl.program_id(1) == i)(lambda i=i: fn(i))


def _mla_attn_fwd(q, k, v):
    t = q.shape[0]
    tq = min(256, t)
    scale = MLA_QK ** -0.5

    def body(q_ref, k_ref, v_ref, o_ref, lse_ref):
        def tile(i):
            keys = (i + 1) * tq
            causal, low = _attn_masks(tq, i)
            vp = v_ref[0:keys, :]
            acc = jnp.zeros((tq, LANE), F32)
            for hh in range(2):
                sl = slice(hh * HEAD_PAD, (hh + 1) * HEAD_PAD)
                s = _dot_nt(q_ref[:, sl], k_ref[0:keys, sl]) * scale
                s = jnp.where(causal, s, -jnp.inf)
                m = jnp.max(s, axis=-1, keepdims=True)
                e = jnp.exp(s - m)
                l = jnp.sum(e, axis=-1, keepdims=True)
                o = _dot(e.astype(BF16), vp) / l
                acc = jnp.where(low if hh == 0 else jnp.logical_not(low), o, acc)
                lse_ref[hh] = m + jnp.log(l)
            o_ref[...] = acc

        _for_each_query_tile(t // tq, tile)

    return pl.pallas_call(
        body, name="mla_attn_fwd",
        grid=(MLA_HEADS // 2, t // tq),
        in_specs=[pl.BlockSpec((tq, 2 * HEAD_PAD), lambda p, i: (i, p)),
                  pl.BlockSpec((t, 2 * HEAD_PAD), lambda p, i: (0, p)),
                  pl.BlockSpec((t, LANE), lambda p, i: (0, p))],
        out_specs=(pl.BlockSpec((tq, LANE), lambda p, i: (i, p)),
                   pl.BlockSpec((2, tq, 1), lambda p, i: (p, i, 0))),
        out_shape=(jax.ShapeDtypeStruct((t, MLA_WIDTH), F32),
                   jax.ShapeDtypeStruct((MLA_HEADS, t, 1), F32)),
        compiler_params=_cparams(("arbitrary", "arbitrary")),
    )(q, k, v)


def _mla_attn_bwd(q, k, v, o, do, lse, after):
    t = q.shape[0]
    tq = min(256, t)
    scale = MLA_QK ** -0.5

    def body(q_ref, k_ref, v_ref, o_ref, do_ref, lse_ref, after_ref, dq_ref, dk_ref, dv_ref):
        del after_ref

        @pl.when(pl.program_id(1) == 0)
        def _():
            dk_ref[...] = jnp.zeros_like(dk_ref)
            dv_ref[...] = jnp.zeros_like(dv_ref)

        def tile(i):
            keys = (i + 1) * tq
            causal, low = _attn_masks(tq, i)
            vp = v_ref[0:keys, :]
            do_all = do_ref[...]
            o_all = o_ref[...]
            dv_acc = jnp.zeros((keys, LANE), F32)
            for hh in range(2):
                sl = slice(hh * HEAD_PAD, (hh + 1) * HEAD_PAD)
                do_h = jnp.where(low if hh == 0 else jnp.logical_not(low), do_all, 0.0)
                dsum = jnp.sum(do_h * o_all, axis=-1, keepdims=True)
                qh = q_ref[:, sl]
                kh = k_ref[0:keys, sl]
                s = _dot_nt(qh, kh) * scale
                p = jnp.where(causal, jnp.exp(s - lse_ref[hh]), 0.0)
                do_b = do_h.astype(BF16)
                dp = _dot_nt(do_b, vp)
                ds = (p * (dp - dsum) * scale).astype(BF16)
                dq_ref[:, sl] = _dot(ds, kh).astype(BF16)
                dk_ref[0:keys, sl] += _dot_tn(ds, qh)
                dv_acc = dv_acc + _dot_tn(p.astype(BF16), do_b)
            dv_ref[0:keys, :] += dv_acc

        _for_each_query_tile(t // tq, tile)

    return pl.pallas_call(
        body, name="mla_attn_bwd",
        grid=(MLA_HEADS // 2, t // tq),
        in_specs=[pl.BlockSpec((tq, 2 * HEAD_PAD), lambda p, i: (i, p)),
                  pl.BlockSpec((t, 2 * HEAD_PAD), lambda p, i: (0, p)),
                  pl.BlockSpec((t, LANE), lambda p, i: (0, p)),
                  pl.BlockSpec((tq, LANE), lambda p, i: (i, p)),
                  pl.BlockSpec((tq, LANE), lambda p, i: (i, p)),
                  pl.BlockSpec((2, tq, 1), lambda p, i: (p, i, 0)),
                  pl.BlockSpec(memory_space=pl.ANY)],
        out_specs=(pl.BlockSpec((tq, 2 * HEAD_PAD), lambda p, i: (i, p)),
                   pl.BlockSpec((t, 2 * HEAD_PAD), lambda p, i: (0, p)),
                   pl.BlockSpec((t, LANE), lambda p, i: (0, p))),
        out_shape=(jax.ShapeDtypeStruct((t, MLA_HEADS * HEAD_PAD), BF16),
                   jax.ShapeDtypeStruct((t, MLA_HEADS * HEAD_PAD), F32),
                   jax.ShapeDtypeStruct((t, MLA_WIDTH), F32)),
        compiler_params=_cparams(("arbitrary", "arbitrary")),
    )(q, k, v, o, do, lse, after)


def _gla_chunk_terms(q_ref, k_ref, la_ref, h, tri):
    sl = slice(h * GLA_HK, (h + 1) * GLA_HK)
    b = _dot_exact(tri, la_ref[:, sl])
    bl = b[GLA_CHUNK - 1:GLA_CHUNK, :]
    kc = k_ref[:, sl]
    q_in = (q_ref[:, sl] * (GLA_HK ** -0.5)) * jnp.exp(b)
    k_in = kc * jnp.exp(-b)
    k_st = kc * jnp.exp(bl - b)
    return b, bl, q_in, k_in, k_st


def _tri(c, lower):
    r = lax.broadcasted_iota(jnp.int32, (c, c), 0)
    cc = lax.broadcasted_iota(jnp.int32, (c, c), 1)
    return jnp.where(r >= cc if lower else r <= cc, 1.0, 0.0).astype(F32)


def _gla_fwd(proj, log_a):
    t = proj.shape[0]
    c = GLA_CHUNK
    n = t // c

    def body(q_ref, k_ref, v_ref, la_ref, o_ref, sp_ref, st_ref):
        @pl.when(pl.program_id(0) == 0)
        def _():
            st_ref[...] = jnp.zeros_like(st_ref)

        tri = _tri(c, True)
        for h in range(GLA_HEADS):
            _, bl, q_in, k_in, k_st = _gla_chunk_terms(q_ref, k_ref, la_ref, h, tri)
            vs = slice(h * GLA_HV, (h + 1) * GLA_HV)
            vv = v_ref[:, vs].astype(BF16)
            qb = q_in.astype(BF16)
            attn = _dot_nt(qb, k_in.astype(BF16)) * tri
            st = st_ref[h]
            sp_ref[0, h] = st
            o_ref[:, vs] = _dot(attn.astype(BF16), vv) + _dot_nt(qb, st.astype(BF16))
            st_ref[h] = st * jnp.exp(bl) + _dot_tn(vv, k_st.astype(BF16))

    return pl.pallas_call(
        body, name="gla_fwd",
        grid=(n,),
        in_specs=[pl.BlockSpec((c, GLA_DK), lambda i: (i, P_QG // GLA_DK)),
                  pl.BlockSpec((c, GLA_DK), lambda i: (i, P_KG // GLA_DK)),
                  pl.BlockSpec((c, GLA_DV), lambda i: (i, P_VG // GLA_DV)),
                  pl.BlockSpec((c, GLA_DK), lambda i: (i, 0))],
        out_specs=(pl.BlockSpec((c, GLA_DV), lambda i: (i, 0)),
                   pl.BlockSpec((1, GLA_HEADS, GLA_HV, GLA_HK), lambda i: (i, 0, 0, 0))),
        out_shape=(jax.ShapeDtypeStruct((t, GLA_DV), F32),
                   jax.ShapeDtypeStruct((n, GLA_HEADS, GLA_HV, GLA_HK), F32)),
        scratch_shapes=[pltpu.VMEM((GLA_HEADS, GLA_HV, GLA_HK), F32)],
        compiler_params=_cparams(("arbitrary",)),
    )(proj, proj, proj, log_a)


def _gla_bwd(proj, log_a, do, states, after):
    t = proj.shape[0]
    c = GLA_CHUNK
    n = t // c

    def body(q_ref, k_ref, v_ref, la_ref, do_ref, sp_ref, after_ref, dg_ref, dla_ref, ds_ref):
        del after_ref

        @pl.when(pl.program_id(0) == 0)
        def _():
            ds_ref[...] = jnp.zeros_like(ds_ref)

        tri = _tri(c, True)
        tri_t = _tri(c, False)
        for h in range(GLA_HEADS):
            b, bl, q_in, k_in, k_st = _gla_chunk_terms(q_ref, k_ref, la_ref, h, tri)
            ks_ = slice(h * GLA_HK, (h + 1) * GLA_HK)
            vs = slice(h * GLA_HV, (h + 1) * GLA_HV)
            vv = v_ref[:, vs].astype(BF16)
            do_h = do_ref[:, vs]
            qb, kb, ksb = q_in.astype(BF16), k_in.astype(BF16), k_st.astype(BF16)
            attn = (_dot_nt(qb, kb) * tri).astype(BF16)
            st = sp_ref[0, h]
            dst = ds_ref[h]
            dstb = dst.astype(BF16)
            dattn = (_dot_nt(do_h, vv) * tri).astype(BF16)
            dg_ref[:, P_VG + h * GLA_HV:P_VG + (h + 1) * GLA_HV] = (
                _dot_tn(attn, do_h) + _dot_nt(ksb, dstb)).astype(BF16)
            dq_in = _dot(dattn, kb) + _dot(do_h, st.astype(BF16))
            dk_in = _dot_tn(dattn, qb)
            dk_st = _dot(vv, dstb)
            ebl = jnp.exp(bl)
            d_ebl = jnp.sum(st * dst, axis=0, keepdims=True)
            ds_ref[h] = _dot_tn(do_h, qb) + dst * ebl
            dg_ref[:, P_QG + h * GLA_HK:P_QG + (h + 1) * GLA_HK] = (
                dq_in * (GLA_HK ** -0.5) * jnp.exp(b)).astype(BF16)
            dg_ref[:, P_KG + h * GLA_HK:P_KG + (h + 1) * GLA_HK] = (
                dk_in * jnp.exp(-b) + dk_st * jnp.exp(bl - b)).astype(BF16)
            db = dq_in * q_in - dk_in * k_in - dk_st * k_st
            dbl = jnp.sum(dk_st * k_st, axis=0, keepdims=True) + d_ebl * ebl
            dla_ref[:, ks_] = _dot_exact(tri_t, db) + dbl

    rev = lambda i: n - 1 - i
    gw = P_GROUPS[0][1]
    return pl.pallas_call(
        body, name="gla_bwd",
        grid=(n,),
        in_specs=[pl.BlockSpec((c, GLA_DK), lambda i: (rev(i), P_QG // GLA_DK)),
                  pl.BlockSpec((c, GLA_DK), lambda i: (rev(i), P_KG // GLA_DK)),
                  pl.BlockSpec((c, GLA_DV), lambda i: (rev(i), P_VG // GLA_DV)),
                  pl.BlockSpec((c, GLA_DK), lambda i: (rev(i), 0)),
                  pl.BlockSpec((c, GLA_DV), lambda i: (rev(i), 0)),
                  pl.BlockSpec((1, GLA_HEADS, GLA_HV, GLA_HK), lambda i: (rev(i), 0, 0, 0)),
                  pl.BlockSpec(memory_space=pl.ANY)],
        out_specs=(pl.BlockSpec((c, gw), lambda i: (rev(i), 0)),
                   pl.BlockSpec((c, GLA_DK), lambda i: (rev(i), 0))),
        out_shape=(jax.ShapeDtypeStruct((t, gw), BF16), jax.ShapeDtypeStruct((t, GLA_DK), F32)),
        scratch_shapes=[pltpu.VMEM((GLA_HEADS, GLA_HV, GLA_HK), F32)],
        compiler_params=_cparams(("arbitrary",)),
    )(proj, proj, proj, log_a, do, states, after)


def _post(o_mla, proj, o_gla, x, target, g_gla, g_final, w_pm, w_pg, w_o):
    t = x.shape[0]
    tm = min(128, t)
    g0, gw = P_GROUPS[1]

    def body(om_ref, zg_ref, gm_ref, gg_ref, zm_ref, og_ref, x_ref, tg_ref, ggla_ref, gf_ref,
             wpm_ref, wpg_ref, wo_ref,
             dx2_ref, dom_ref, dog_ref, dg_ref,
             mg_ref, um_ref, ug_ref, dym_ref, dyg_ref, loss_ref, dgf_ref, dggla_ref):
        @pl.when(pl.program_id(0) == 0)
        def _():
            loss_ref[...] = jnp.zeros_like(loss_ref)
            dgf_ref[...] = jnp.zeros_like(dgf_ref)
            dggla_ref[...] = jnp.zeros_like(dggla_ref)

        om = om_ref[...]
        zm = zm_ref[...]
        sm = _sigmoid(zm)
        silu_m = zm * sm
        um = (om * silu_m).astype(BF16)
        um_ref[...] = um
        ym = _dot(um, wpm_ref[...])

        ggla = ggla_ref[...]
        zg = zg_ref[...]
        sg = _sigmoid(zg)
        silu_g = zg * sg
        xhat, rstd, on = [], [], []
        for h in range(GLA_HEADS):
            blk = og_ref[:, h * GLA_HV:(h + 1) * GLA_HV]
            r = lax.rsqrt(jnp.mean(blk * blk, axis=-1, keepdims=True) + EPS)
            xhat.append(blk * r)
            rstd.append(r)
            on.append(xhat[h] * ggla)
        on = jnp.concatenate(on, axis=-1)
        ug = (on * silu_g).astype(BF16)
        ug_ref[...] = ug
        yg = _dot(ug, wpg_ref[...])

        sgm = _sigmoid(gm_ref[...])
        sgg = _sigmoid(gg_ref[...])
        merged = (sgm * ym + sgg * yg).astype(BF16)
        mg_ref[...] = merged
        x2 = x_ref[...] + _dot(merged, wo_ref[...])
        gf = gf_ref[...]
        rf = lax.rsqrt(jnp.mean(x2 * x2, axis=-1, keepdims=True) + EPS)
        xh = x2 * rf
        err = xh * gf - tg_ref[...]
        loss_ref[...] += 0.5 * jnp.sum(jnp.mean(err * err, axis=-1, keepdims=True))

        dy = err * (1.0 / D_MODEL)
        dgf_ref[...] += jnp.sum(dy * xh, axis=0, keepdims=True)
        dxh = dy * gf
        dx2 = rf * (dxh - xh * jnp.mean(dxh * xh, axis=-1, keepdims=True))
        dx2_ref[...] = dx2
        dmerged = _dot_nt(dx2.astype(BF16), wo_ref[...])
        dym = (dmerged * sgm).astype(BF16)
        dyg = (dmerged * sgg).astype(BF16)
        dym_ref[...] = dym
        dyg_ref[...] = dyg
        dg_ref[:, P_GMLA - g0:P_GMLA - g0 + D_MODEL] = (dmerged * ym * sgm * (1.0 - sgm)).astype(BF16)
        dg_ref[:, P_GGLA - g0:P_GGLA - g0 + D_MODEL] = (dmerged * yg * sgg * (1.0 - sgg)).astype(BF16)
        dum = _dot_nt(dym, wpm_ref[...])
        dom_ref[...] = dum * silu_m
        dg_ref[:, P_ZMLA - g0:P_ZMLA - g0 + MLA_WIDTH] = (
            dum * om * (sm * (1.0 + zm * (1.0 - sm)))).astype(BF16)
        dug = _dot_nt(dyg, wpg_ref[...])
        dg_ref[:, P_ZGLA - g0:P_ZGLA - g0 + GLA_DV] = (
            dug * on * (sg * (1.0 + zg * (1.0 - sg)))).astype(BF16)
        don = dug * silu_g
        dggla = jnp.zeros((1, GLA_HV), F32)
        for h in range(GLA_HEADS):
            hs = slice(h * GLA_HV, (h + 1) * GLA_HV)
            don_h = don[:, hs]
            dggla = dggla + jnp.sum(don_h * xhat[h], axis=0, keepdims=True)
            dxh_h = don_h * ggla
            dog_ref[:, hs] = (rstd[h] * (dxh_h - xhat[h] * jnp.mean(dxh_h * xhat[h], axis=-1,
                                                                     keepdims=True))).astype(BF16)
        dggla_ref[...] += dggla

    row = lambda w: pl.BlockSpec((tm, w), lambda i: (i, 0))
    pcol = lambda w, off: pl.BlockSpec((tm, w), lambda i: (i, off // w))
    full = lambda a: pl.BlockSpec(a.shape, lambda i: (0, 0))
    sds = jax.ShapeDtypeStruct
    return pl.pallas_call(
        body, name="post_fwd_bwd",
        grid=(t // tm,),
        in_specs=[row(MLA_WIDTH), pcol(GLA_DV, P_ZGLA), pcol(D_MODEL, P_GMLA), pcol(D_MODEL, P_GGLA),
                  pcol(MLA_WIDTH, P_ZMLA), row(GLA_DV), row(D_MODEL), row(D_MODEL),
                  full(g_gla), full(g_final), full(w_pm), full(w_pg), full(w_o)],
        out_specs=(row(D_MODEL), row(MLA_WIDTH), row(GLA_DV), row(gw),
                   row(D_MODEL), row(MLA_WIDTH), row(GLA_DV), row(D_MODEL), row(D_MODEL),
                   pl.BlockSpec((1, LANE), lambda i: (0, 0)),
                   pl.BlockSpec((1, D_MODEL), lambda i: (0, 0)),
                   pl.BlockSpec((1, GLA_HV), lambda i: (0, 0))),
        out_shape=(sds((t, D_MODEL), F32), sds((t, MLA_WIDTH), F32), sds((t, GLA_DV), BF16),
                   sds((t, gw), BF16),
                   sds((t, D_MODEL), BF16), sds((t, MLA_WIDTH), BF16), sds((t, GLA_DV), BF16),
                   sds((t, D_MODEL), BF16), sds((t, D_MODEL), BF16),
                   sds((1, LANE), F32), sds((1, D_MODEL), F32), sds((1, GLA_HV), F32)),
        compiler_params=_cparams(("arbitrary",)),
    )(o_mla, proj, proj, proj, proj, o_gla, x, target, g_gla, g_final, w_pm, w_pg, w_o)


def _mla_prep_bwd(dq, dk, dv, dla, pre, proj, rq, rkv, g_q, g_kv, w_uq_p, w_k_p, w_v, w_gate_p,
                  rc, rsn, rsp):
    t = proj.shape[0]
    tm = min(256, t)
    gw = P_GROUPS[2][1]

    def body(dq_ref, dk_ref, dv_ref, dla_ref, pre_ref, cq_ref, ckv_ref, rq_ref, rkv_ref,
             gq_ref, gkv_ref, wuq_ref, wk_ref, wv_ref, wg_ref, c_ref, sn_ref, sp_ref,
             dg_ref, dqpre_ref, dpre_ref, dgq_ref, dgkv_ref, dbg_ref):
        @pl.when(pl.program_id(0) == 0)
        def _():
            dgq_ref[...] = jnp.zeros_like(dgq_ref)
            dgkv_ref[...] = jnp.zeros_like(dgkv_ref)
            dbg_ref[...] = jnp.zeros_like(dbg_ref)

        c, sn, sp = c_ref[...], sn_ref[...], sp_ref[...]
        dkr = jnp.zeros((tm, LANE), F32)
        for h in range(MLA_HEADS):
            sl = slice(h * HEAD_PAD, (h + 1) * HEAD_PAD)
            dqpre_ref[:, sl] = _rope_bwd(dq_ref[:, sl].astype(F32), c, sn, sp).astype(BF16)
            dkr = dkr + dk_ref[:, sl]
        dcqn = _dot_nt(dqpre_ref[...], wuq_ref[...])
        rq = rq_ref[...]
        xh = cq_ref[:, :MLA_Q_RANK] * rq
        dgq_ref[...] += jnp.sum(dcqn * xh, axis=0, keepdims=True)
        dxh = dcqn * gq_ref[...]
        dcq = rq * (dxh - xh * jnp.mean(dxh * xh, axis=-1, keepdims=True))
        dg_ref[:, :MLA_Q_RANK] = dcq.astype(BF16)
        dg_ref[:, MLA_Q_RANK:512] = jnp.zeros((tm, 512 - MLA_Q_RANK), BF16)

        dckvn = _dot_nt(dk_ref[...].astype(BF16), wk_ref[...]) + \
            _dot_nt(dv_ref[...].astype(BF16), wv_ref[...])
        rkv = rkv_ref[...]
        xh = ckv_ref[...] * rkv
        dgkv_ref[...] += jnp.sum(dckvn * xh, axis=0, keepdims=True)
        dxh = dckvn * gkv_ref[...]
        dg_ref[:, P_CKV - P_CQ:P_CKV - P_CQ + MLA_KV_RANK] = (
            rkv * (dxh - xh * jnp.mean(dxh * xh, axis=-1, keepdims=True))).astype(BF16)

        dpre = dla_ref[...] * (1.0 / GLA_GATE_NORM) * (1.0 - _sigmoid(pre_ref[...]))
        dbg_ref[...] += jnp.sum(dpre, axis=0, keepdims=True)
        dpre = dpre.astype(BF16)
        dpre_ref[...] = dpre
        lane = lax.broadcasted_iota(jnp.int32, (tm, LANE), 1)
        in_kr = jnp.logical_and(lane >= MISC_KR, lane < MISC_KR + MLA_ROPE)
        dmisc = jnp.where(in_kr, _rope_bwd(dkr, c, sn, sp), 0.0) + _dot_nt(dpre, wg_ref[...])
        dg_ref[:, P_MISC - P_CQ:P_MISC - P_CQ + LANE] = dmisc.astype(BF16)

    hq = MLA_HEADS * HEAD_PAD
    row = lambda w: pl.BlockSpec((tm, w), lambda i: (i, 0))
    full = lambda a: pl.BlockSpec(a.shape, lambda i: (0, 0))
    acc = lambda w: pl.BlockSpec((1, w), lambda i: (0, 0))
    sds = jax.ShapeDtypeStruct
    return pl.pallas_call(
        body, name="mla_prep_bwd",
        grid=(t // tm,),
        in_specs=[row(hq), row(hq), row(MLA_WIDTH), row(GLA_DK), row(GLA_DK),
                  pl.BlockSpec((tm, 512), lambda i: (i, P_CQ // 512)),
                  pl.BlockSpec((tm, MLA_KV_RANK), lambda i: (i, P_CKV // MLA_KV_RANK)),
                  row(1), row(1), full(g_q), full(g_kv), full(w_uq_p), full(w_k_p), full(w_v),
                  full(w_gate_p), row(LANE), row(LANE), row(LANE)],
        out_specs=(row(gw), row(hq), row(GLA_DK),
                   acc(MLA_Q_RANK), acc(MLA_KV_RANK), acc(GLA_DK)),
        out_shape=(sds((t, gw), BF16), sds((t, hq), BF16), sds((t, GLA_DK), BF16),
                   sds((1, MLA_Q_RANK), F32), sds((1, MLA_KV_RANK), F32), sds((1, GLA_DK), F32)),
        compiler_params=_cparams(("arbitrary",)),
    )(dq, dk, dv, dla, pre, proj, proj, rq, rkv, g_q, g_kv, w_uq_p, w_k_p, w_v, w_gate_p,
      rc, rsn, rsp)


def _inproj_bwd(dgroups, w_p, x, rstd, g_in, dx2, after):
    t = x.shape[0]
    tm = min(256, t)

    def body(d0_ref, d1_ref, d2_ref, w_ref, x_ref, r_ref, g_ref, dx2_ref, after_ref, dx_ref, dg_ref):
        del after_ref

        @pl.when(pl.program_id(0) == 0)
        def _():
            dg_ref[...] = jnp.zeros_like(dg_ref)

        dh = jnp.zeros((tm, D_MODEL), F32)
        for d_ref, (off, width) in zip((d0_ref, d1_ref, d2_ref), P_GROUPS):
            dh = dh + _dot_nt(d_ref[...], w_ref[:, off:off + width])
        r = r_ref[...]
        xh = x_ref[...] * r
        dg_ref[...] += jnp.sum(dh * xh, axis=0, keepdims=True)
        dxh = dh * g_ref[...]
        dx_ref[...] = dx2_ref[...] + r * (dxh - xh * jnp.mean(dxh * xh, axis=-1, keepdims=True))

    row = lambda w: pl.BlockSpec((tm, w), lambda i: (i, 0))
    return pl.pallas_call(
        body, name="inproj_bwd",
        grid=(t // tm,),
        in_specs=[row(P_GROUPS[0][1]), row(P_GROUPS[1][1]), row(P_GROUPS[2][1]),
                  pl.BlockSpec((D_MODEL, P_TOTAL), lambda i: (0, 0)),
                  row(D_MODEL), row(1), pl.BlockSpec((1, D_MODEL), lambda i: (0, 0)), row(D_MODEL),
                  pl.BlockSpec(memory_space=pl.ANY)],
        out_specs=(row(D_MODEL), pl.BlockSpec((1, D_MODEL), lambda i: (0, 0))),
        out_shape=(jax.ShapeDtypeStruct((t, D_MODEL), F32),
                   jax.ShapeDtypeStruct((1, D_MODEL), F32)),
        compiler_params=_cparams(("arbitrary",)),
    )(*dgroups, w_p, x, rstd, g_in, dx2, after)


def _matmul(name, a, b, tm, tn, dtype=F32):
    kk, m = a.shape
    n = b.shape[1]

    def body(a_ref, b_ref, o_ref):
        o_ref[...] = _dot_tn(a_ref[...].astype(BF16), b_ref[...].astype(BF16)).astype(dtype)

    return pl.pallas_call(
        body, name=name,
        grid=(n // tn, m // tm),
        in_specs=[pl.BlockSpec((kk, tm), lambda j, i: (0, i)),
                  pl.BlockSpec((kk, tn), lambda j, i: (0, j))],
        out_specs=pl.BlockSpec((tm, tn), lambda j, i: (i, j)),
        out_shape=jax.ShapeDtypeStruct((m, n), dtype),
        compiler_params=_cparams(("arbitrary", "arbitrary")),
    )(a, b)


def _adamw_update(part_refs, w_ref, m_ref, v_ref, g_ref, d_ref, nm_ref, nv_ref):
    g = part_refs[0][...].astype(F32)
    for p_ref in part_refs[1:]:
        g = g + p_ref[...].astype(F32)
    m_new = ADAM_B1 * m_ref[...] + (1.0 - ADAM_B1) * g
    v_new = ADAM_B2 * v_ref[...] + (1.0 - ADAM_B2) * (g * g)
    m_hat = m_new / (1.0 - ADAM_B1 ** ADAM_STEP)
    v_hat = v_new / (1.0 - ADAM_B2 ** ADAM_STEP)
    g_ref[...] = g
    nm_ref[...] = m_new
    nv_ref[...] = v_new
    d_ref[...] = -ADAM_LR * (m_hat / (jnp.sqrt(v_hat) + ADAM_EPS) + ADAM_WD * w_ref[...])


def _adamw_rows(name, parts, w, m, v, tr, first=None):
    _, rows, cols = w.shape
    slots = parts.shape[0]

    def body(*refs):
        lead_refs, p_ref = ([], refs[0]) if first is None else ([refs[0]], refs[1])
        _adamw_update(lead_refs + [p_ref.at[q] for q in range(slots)], *refs[len(lead_refs) + 1:])

    blk = pl.BlockSpec((None, tr, cols), lambda i: (0, i, 0))
    out = jax.ShapeDtypeStruct((1, rows, cols), F32)
    lead = [] if first is None else [pl.BlockSpec((tr, cols), lambda i: (i, 0))]
    return pl.pallas_call(
        body, name=name,
        grid=(rows // tr,),
        in_specs=lead + [pl.BlockSpec((slots, tr, cols), lambda i: (0, i, 0)), blk, blk, blk],
        out_specs=(blk, blk, blk, blk),
        out_shape=(out, out, out, out),
        compiler_params=_cparams(("arbitrary",)),
    )(*([] if first is None else [first]), parts, w, m, v)


def _adamw_group(parts, ws, ms, vs):
    n = len(ws)

    def body(*refs):
        ins, outs = refs[:4 * n], refs[4 * n:]
        for a in range(n):
            _adamw_update([ins[a].at[q] for q in range(ins[a].shape[0])],
                          *[r.at[0] for r in (ins[n + a], ins[2 * n + a], ins[3 * n + a])],
                          *[r.at[0] for r in outs[4 * a:4 * a + 4]])

    vmem = lambda k: [pl.BlockSpec(memory_space=pltpu.VMEM) for _ in range(k)]
    out_shape = []
    for w in ws:
        out_shape += [jax.ShapeDtypeStruct(w.shape, F32)] * 4
    res = pl.pallas_call(
        body, name="adamw_small_weights",
        in_specs=vmem(4 * n), out_specs=tuple(vmem(4 * n)), out_shape=tuple(out_shape),
        compiler_params=_cparams(),
    )(*parts, *ws, *ms, *vs)
    return [res[4 * a:4 * a + 4] for a in range(n)]


def _rope_tables(positions):
    half = MLA_ROPE // 2
    freqs = ROPE_THETA ** (-jnp.arange(half, dtype=F32) / half)
    ang = positions.astype(F32).reshape(-1, 1) * freqs
    cos, sin = jnp.cos(ang), jnp.sin(ang)
    t = ang.shape[0]
    one, zero = jnp.ones((t, MLA_NOPE), F32), jnp.zeros((t, half), F32)
    tail = jnp.zeros((t, LANE - MLA_QK), F32)
    rc = jnp.concatenate([one, cos, cos, tail], axis=1)
    rsn = jnp.concatenate([0.0 * one, -sin, zero, tail], axis=1)
    rsp = jnp.concatenate([0.0 * one, zero, sin, tail], axis=1)
    return rc, rsn, rsp


def _cols_full(g):
    return g.transpose(1, 0, 2)


def kernel(x, positions, g_in, w_in, g_q, w_uq, g_kv, w_ukv, w_gla_gate, b_gla_gate, g_gla, w_proj_mla, w_proj_gla, w_out, g_final, loss_target, m_g_in, m_w_in, m_g_q, m_w_uq, m_g_kv, m_w_ukv, m_w_gla_gate, m_b_gla_gate, m_g_gla, m_w_proj_mla, m_w_proj_gla, m_w_out, m_g_final, v_g_in, v_w_in, v_g_q, v_w_uq, v_g_kv, v_w_ukv, v_w_gla_gate, v_b_gla_gate, v_g_gla, v_w_proj_mla, v_w_proj_gla, v_w_out, v_g_final):
    t = x.shape[1]
    x2d = x.reshape(t, D_MODEL)
    tgt = loss_target.reshape(t, D_MODEL)
    g_final2 = g_final.reshape(1, D_MODEL)
    sharded = [(w_in, m_w_in, v_w_in), (w_uq, m_w_uq, v_w_uq), (w_ukv, m_w_ukv, v_w_ukv),
               (w_gla_gate, m_w_gla_gate, v_w_gla_gate), (w_proj_mla, m_w_proj_mla, v_w_proj_mla),
               (w_proj_gla, m_w_proj_gla, v_w_proj_gla), (w_out, m_w_out, v_w_out)]

    g_w_in, g_uq, g_ukv, g_gate, g_pm, g_pg, g_o = _all_gather(
        [s[0][0].astype(BF16) for s in sharded])
    w_in_p = _weights_to_p(g_w_in)
    w_uq_p = jnp.pad(_cols_full(g_uq), ((0, 0), (0, 0), (0, HEAD_PAD - MLA_QK))).reshape(
        MLA_Q_RANK, MLA_HEADS * HEAD_PAD)
    ukv = _cols_full(g_ukv)
    w_k_p = jnp.pad(ukv[:, :, :MLA_NOPE], ((0, 0), (0, 0), (0, HEAD_PAD - MLA_NOPE))).reshape(
        MLA_KV_RANK, MLA_HEADS * HEAD_PAD)
    w_v = ukv[:, :, MLA_NOPE:].reshape(MLA_KV_RANK, MLA_WIDTH)
    w_gate_p = jnp.pad(_cols_full(g_gate).reshape(GLA_GATE_RANK, GLA_DK),
                       ((MISC_ALR, LANE - MISC_ALR - GLA_GATE_RANK), (0, 0)))
    w_pm = _cols_full(g_pm).reshape(MLA_WIDTH, D_MODEL)
    w_pg = g_pg.reshape(GLA_DV, D_MODEL)
    w_o = g_o.reshape(D_MODEL, D_MODEL)
    rc, rsn, rsp = _rope_tables(positions)

    proj, h, rstd = _inproj(x2d, g_in, w_in_p)
    q, k, v, log_a, pre, cqn, ckvn, rq, rkv, misc = _mla_prep(
        proj, g_q, g_kv, w_uq_p, w_k_p, w_v, w_gate_p, b_gla_gate, rc, rsn, rsp)
    o_mla, lse = _mla_attn_fwd(q, k, v)
    o_gla, states = _gla_fwd(proj, log_a)

    (dx2, do_mla, do_gla, d_out, merged, um, ug, dym, dyg, loss_p, dg_final,
     dg_gla) = _post(o_mla, proj, o_gla, x2d, tgt, g_gla, g_final2, w_pm, w_pg, w_o)

    own = lax.empty((D_MODEL, SHARD_COLS), BF16)
    land = lax.empty((3, D_MODEL, SHARD_COLS), BF16)
    dw_groups, started = {}, []

    def reduce_scatter_stage(s, dests, own, land):
        assert set(g for d in dests for g in _shard_groups(d)) <= set(dw_groups)
        parts = _grads_to_shards("grads_to_shards_%d" % s, dw_groups, dests)
        sums, own = _swap_add_in("swap_add_%d" % s, parts, own, dests)
        send_sems, recv_sems, sums, land, token = _ici_start_in("ici_start_%d" % s, sums, land, dests)
        started.append((send_sems, recv_sems, sums, dests))
        return own, land, token

    dw_groups[1] = _matmul("dw_in_1", h, d_out, 512, 512)
    own, land, token = reduce_scatter_stage(1, (5, 6, 7), own, land)
    d_gla, dla = _gla_bwd(proj, log_a, do_gla, states, token)
    dw_groups[0] = _matmul("dw_in_0", h, d_gla, 512, 512)
    own, land, token = reduce_scatter_stage(2, (1, 2, 3), own, land)
    dq, dk, dv = _mla_attn_bwd(q, k, v, o_mla, do_mla, lse, token)
    d_lat, dqpre, dpre, dg_q, dg_kv, db_gate = _mla_prep_bwd(
        dq, dk, dv, dla, pre, proj, rq, rkv, g_q, g_kv, w_uq_p, w_k_p, w_v, w_gate_p, rc, rsn, rsp)
    dw_groups[2] = _matmul("dw_in_2", h, d_lat, 512, 896)
    own, land, token = reduce_scatter_stage(3, (0, 4), own, land)
    grad_x, dg_in = _inproj_bwd((d_gla, d_out, d_lat), w_in_p, x2d, rstd, g_in, dx2, token)

    dw_uq = _matmul("dw_uq", cqn, dqpre, MLA_Q_RANK, 512, BF16)
    p_uq = dw_uq.reshape(MLA_Q_RANK, MLA_HEADS, HEAD_PAD)[:, :, :MLA_QK].transpose(1, 0, 2)
    dw_k = _matmul("dw_uk", ckvn, dk, MLA_KV_RANK, 512, BF16)
    dw_v = _matmul("dw_uv", ckvn, dv, MLA_KV_RANK, 512, BF16)
    p_ukv = jnp.concatenate(
        [dw_k.reshape(MLA_KV_RANK, MLA_HEADS, HEAD_PAD)[:, :, :MLA_NOPE],
         dw_v.reshape(MLA_KV_RANK, MLA_HEADS, MLA_VDIM)], axis=2).transpose(1, 0, 2)
    dw_gate = _matmul("dw_gate", misc, dpre, LANE, 512, BF16)
    p_gate = dw_gate[MISC_ALR:MISC_ALR + GLA_GATE_RANK].reshape(
        GLA_GATE_RANK, N_DEV, GLA_DK // N_DEV).transpose(1, 0, 2)
    p_pm = _matmul("dw_proj_mla", um, dym, 512, 512, BF16).reshape(
        MLA_WIDTH, N_DEV, D_MODEL // N_DEV).transpose(1, 0, 2)
    p_pg = _matmul("dw_proj_gla", ug, dyg, 512, 512, BF16).reshape(N_DEV, -1, D_MODEL)
    p_o = _matmul("dw_out", merged, dx2, 512, 512, BF16).reshape(N_DEV, -1, D_MODEL)
    small = jnp.concatenate([dg_in.reshape(-1), dg_q.reshape(-1), dg_kv.reshape(-1),
                             db_gate.reshape(-1), dg_gla.reshape(-1), dg_final.reshape(-1),
                             loss_p[0, :1]])
    small = jnp.pad(small, (0, SMALL_ROWS * LANE - small.shape[0])).reshape(SMALL_ROWS, LANE)

    recv = _exchange_grads([p_uq, p_ukv, p_gate, p_pm, p_pg, p_o], small)
    land = _ici_wait_in("ici_wait", started, land, recv[6])
    big = [_adamw_rows("adamw_w_in", land, *sharded[0], 128, first=own)]
    big += _adamw_group(recv[:6], *[[s[j] for s in sharded[1:]] for j in range(3)])
    replicated = [(g_in, m_g_in, v_g_in), (g_q, m_g_q, v_g_q), (g_kv, m_g_kv, v_g_kv),
                  (b_gla_gate, m_b_gla_gate, v_b_gla_gate), (g_gla, m_g_gla, v_g_gla),
                  (g_final, m_g_final, v_g_final)]
    spacks = [jnp.pad(jnp.concatenate([s[j].reshape(-1) for s in replicated]),
                      (0, SMALL_ROWS * LANE - sum(SMALL_SIZES))).reshape(1, SMALL_ROWS, LANE)
              for j in range(3)]
    tiny = _adamw_rows("adamw_gains", recv[6], spacks[0], spacks[1], spacks[2], SMALL_ROWS)

    outs = {}
    names = ("w_in", "w_uq", "w_ukv", "w_gla_gate", "w_proj_mla", "w_proj_gla", "w_out")
    for j, kind in enumerate(("grad", "delta", "new_m", "new_v")):
        for name, res in zip(names, big):
            outs[kind, name] = res[j]
        flat = tiny[j].reshape(-1)
        off = 0
        for name, size in zip(("g_in", "g_q", "g_kv", "b_gla_gate", "g_gla", "g_final"), SMALL_SIZES):
            shape = (size,) if name == "g_final" else (1, size)
            outs[kind, name] = flat[off:off + size].reshape(shape)
            off += size
    loss = tiny[0].reshape(-1)[sum(SMALL_SIZES)]
    order = ("g_in", "w_in", "g_q", "w_uq", "g_kv", "w_ukv", "w_gla_gate", "b_gla_gate", "g_gla",
             "w_proj_mla", "w_proj_gla", "w_out", "g_final")
    result = [loss, grad_x.reshape(1, t, D_MODEL)]
    for kind in ("grad", "delta", "new_m", "new_v"):
        result += [outs[kind, name] for name in order]
    return tuple(result)
```

```python
import math

import jax
import jax.numpy as jnp
from jax import lax
from jax.experimental import pallas as pl
from jax.experimental.pallas import tpu as pltpu

F32 = jnp.float32
BF16 = jnp.bfloat16
MESH = pl.DeviceIdType.MESH
N_DEV = 8

D_MODEL = 1024
EPS = 1e-6
MLA_HEADS = 8
MLA_NOPE = 64
MLA_ROPE = 32
MLA_VDIM = 64
MLA_Q_RANK = 384
MLA_KV_RANK = 256
MLA_QK = MLA_NOPE + MLA_ROPE
MLA_WIDTH = MLA_HEADS * MLA_VDIM
ROPE_THETA = 10000.0
GLA_HEADS = 4
GLA_DK = 512
GLA_DV = 1024
GLA_HK = 128
GLA_HV = 256
GLA_GATE_RANK = 16
GLA_GATE_NORM = 16.0
GLA_CHUNK = 64
D_IN = 6320

ADAM_LR = 0.001
ADAM_B1 = 0.9
ADAM_B2 = 0.999
ADAM_EPS = 1e-08
ADAM_WD = 0.01
ADAM_STEP = 10

LANE = 128
HEAD_PAD = 128
VMEM_LIMIT = 48 * 1024 * 1024

P_VG, P_QG, P_KG = 0, 1024, 1536
P_ZGLA, P_GMLA, P_GGLA, P_ZMLA = 2048, 3072, 4096, 5120
P_CQ, P_CKV, P_MISC = 5632, 6144, 6400
P_TOTAL = 6528
P_GROUPS = ((0, 2048), (2048, 3584), (5632, 896))
MISC_KR = 64
MISC_ALR = 96
SHARD_COLS = D_IN // N_DEV
P_COMPONENTS = ((0, 384, P_CQ), (384, 256, P_CKV), (640, 32, P_MISC + MISC_KR), (672, 512, P_ZMLA),
                (1184, 512, P_QG), (1696, 512, P_KG), (2208, 1024, P_VG),
                (3232, 16, P_MISC + MISC_ALR), (3248, 1024, P_ZGLA), (4272, 1024, P_GMLA),
                (5296, 1024, P_GGLA))

SMALL_SIZES = (1024, 384, 256, 512, 256, 1024)
SMALL_ROWS = 32


def _segments():
    segs = []
    for g0, n, p0 in P_COMPONENTS:
        g = g0
        while g < g0 + n:
            d = g // SHARD_COLS
            end = min(g0 + n, (d + 1) * SHARD_COLS)
            segs.append((d, g - d * SHARD_COLS, end - g, p0 + g - g0))
            g = end
    return segs


def _cparams(sem=None):
    if sem is None:
        return pltpu.CompilerParams(vmem_limit_bytes=VMEM_LIMIT)
    return pltpu.CompilerParams(dimension_semantics=sem, vmem_limit_bytes=VMEM_LIMIT)


def _sigmoid(v):
    return 1.0 / (1.0 + jnp.exp(-v))


def _dot(a, b):
    return jnp.dot(a, b, preferred_element_type=F32)


def _dot_nt(a, b):
    return lax.dot_general(a, b, (((1,), (1,)), ((), ())), preferred_element_type=F32)


def _dot_tn(a, b):
    return lax.dot_general(a, b, (((0,), (0,)), ((), ())), preferred_element_type=F32)


def _dot_exact(a, b):
    return jnp.dot(a, b, preferred_element_type=F32, precision=lax.Precision.HIGHEST)


def _rope_fwd(blk, c, sn, sp):
    return blk * c + pltpu.roll(blk, LANE - 16, 1) * sn + pltpu.roll(blk, 16, 1) * sp


def _rope_bwd(blk, c, sn, sp):
    return blk * c + pltpu.roll(blk * sn, 16, 1) + pltpu.roll(blk * sp, LANE - 16, 1)


def _mesh_pos():
    return lax.axis_index("x"), lax.axis_index("y"), lax.axis_index("c")


def _hbm_specs(n):
    return [pl.BlockSpec(memory_space=pltpu.HBM) for _ in range(n)]


def _all_gather(shards):
    n = len(shards)

    def body(*refs):
        x_refs, out_refs = refs[:n], refs[n:2 * n]
        send_sems, recv_sems, local_sems = refs[2 * n:]
        x, y, c = _mesh_pos()
        me, sibling = (x, y, c), (x, y, 1 - c)
        chips = [(1 - x, y), (x, 1 - y), (1 - x, 1 - y)]

        def slot(a, px, py, pc):
            return out_refs[a].at[4 * px + 2 * py + pc]

        def copies(k, block, to, own=False):
            return [pltpu.make_async_remote_copy(
                src_ref=x_refs[a] if own else slot(a, *block), dst_ref=slot(a, *block),
                send_sem=send_sems.at[k, a], recv_sem=recv_sems.at[k, a],
                device_id=to, device_id_type=MESH) for a in range(n)]

        mine = [pltpu.make_async_copy(x_refs[a], slot(a, *me), local_sems.at[a]) for a in range(n)]
        for cp in mine:
            cp.start()
        first = copies(0, me, sibling, own=True)
        for j, chip in enumerate(chips):
            first += copies(1 + j, me, (*chip, c), own=True)
        for cp in first:
            cp.start()
        passed = []
        for j, chip in enumerate(chips):
            for cp in copies(1 + j, (*chip, c), me):
                cp.wait_recv()
            fwd = copies(4 + j, (*chip, c), sibling)
            for cp in fwd:
                cp.start()
            passed += fwd
        for cp in copies(0, sibling, me):
            cp.wait_recv()
        for j, chip in enumerate(chips):
            for cp in copies(4 + j, (*chip, 1 - c), me):
                cp.wait_recv()
        for cp in first + passed:
            cp.wait_send()
        for cp in mine:
            cp.wait()

    return pl.pallas_call(
        body, name="all_gather_weights",
        out_shape=tuple(jax.ShapeDtypeStruct((N_DEV,) + s.shape, s.dtype) for s in shards),
        in_specs=_hbm_specs(n), out_specs=tuple(_hbm_specs(n)),
        scratch_shapes=[pltpu.SemaphoreType.DMA((7, n)), pltpu.SemaphoreType.DMA((7, n)),
                        pltpu.SemaphoreType.DMA((n,))],
    )(*shards)


def _exchange_grads(parts, small):
    n = len(parts)
    rows_per_add = 256

    def body(*refs):
        p_refs, s_ref = refs[:n], refs[n]
        out_refs, sout_ref = refs[n + 1:2 * n + 1], refs[2 * n + 1]
        mine_v, recv_v = refs[2 * n + 2:3 * n + 2], refs[3 * n + 2:4 * n + 2]
        (d2d_send, d2d_recv, ici_send, ici_recv, s_send, s_recv, load_sems, own_sems,
         sown_sem) = refs[4 * n + 2:]
        x, y, c = _mesh_pos()
        me = 4 * x + 2 * y + c
        chips = [(x, y), (1 - x, y), (x, 1 - y), (1 - x, 1 - y)]

        sown = pltpu.make_async_copy(s_ref, sout_ref.at[me], sown_sem)
        sown.start()
        tiny = []
        for k in range(1, N_DEV):
            peer = (x ^ (k >> 2), y ^ ((k >> 1) & 1), c ^ (k & 1))
            tiny.append(pltpu.make_async_remote_copy(
                src_ref=s_ref, dst_ref=sout_ref.at[me], send_sem=s_send.at[k - 1],
                recv_sem=s_recv.at[k - 1], device_id=peer, device_id_type=MESH))
        for cp in tiny:
            cp.start()

        loads, swaps = [], []
        for a in range(n):
            for j, (px, py) in enumerate(chips):
                loads.append(pltpu.make_async_copy(
                    p_refs[a].at[4 * px + 2 * py + c], mine_v[a].at[j], load_sems.at[a, j]))
                swaps.append(pltpu.make_async_remote_copy(
                    src_ref=p_refs[a].at[4 * px + 2 * py + 1 - c], dst_ref=recv_v[a].at[j],
                    send_sem=d2d_send.at[a, j], recv_sem=d2d_recv.at[a, j],
                    device_id=(x, y, 1 - c), device_id_type=MESH))
        for cp in swaps + loads:
            cp.start()

        outgoing, own = [], []
        for a in range(n):
            rows = p_refs[a].shape[1]
            step = math.gcd(rows, rows_per_add)
            for j in range(4):
                loads[4 * a + j].wait()
                swaps[4 * a + j].wait_recv()

                @pl.loop(0, rows // step)
                def _(i):
                    rs = pl.ds(pl.multiple_of(i * step, step), step)
                    mine_v[a][j, rs, :] = (mine_v[a][j, rs, :].astype(F32)
                                           + recv_v[a][j, rs, :].astype(F32)).astype(BF16)

                if j == 0:
                    own.append(pltpu.make_async_copy(mine_v[a].at[0], out_refs[a].at[0],
                                                     own_sems.at[a]))
                    own[-1].start()
                else:
                    outgoing.append(pltpu.make_async_remote_copy(
                        src_ref=mine_v[a].at[j], dst_ref=out_refs[a].at[j],
                        send_sem=ici_send.at[a, j - 1], recv_sem=ici_recv.at[a, j - 1],
                        device_id=(*chips[j], c), device_id_type=MESH))
                    outgoing[-1].start()

        for cp in tiny + outgoing:
            cp.wait_recv()
        for cp in tiny + outgoing + swaps:
            cp.wait_send()
        for cp in own:
            cp.wait()
        sown.wait()

    outs = [jax.ShapeDtypeStruct((4,) + p.shape[1:], p.dtype) for p in parts]
    outs.append(jax.ShapeDtypeStruct((N_DEV,) + small.shape, small.dtype))
    stage = [pltpu.VMEM((4,) + p.shape[1:], p.dtype) for p in parts]
    return pl.pallas_call(
        body, name="exchange_grads",
        out_shape=tuple(outs),
        in_specs=_hbm_specs(n + 1), out_specs=tuple(_hbm_specs(n + 1)),
        scratch_shapes=stage + stage + [
            pltpu.SemaphoreType.DMA((n, 4)), pltpu.SemaphoreType.DMA((n, 4)),
            pltpu.SemaphoreType.DMA((n, 3)), pltpu.SemaphoreType.DMA((n, 3)),
            pltpu.SemaphoreType.DMA((7,)), pltpu.SemaphoreType.DMA((7,)),
            pltpu.SemaphoreType.DMA((n, 4)), pltpu.SemaphoreType.DMA((n,)),
            pltpu.SemaphoreType.DMA],
        compiler_params=_cparams(),
    )(*parts, small)


def _dev(d):
    return d >> 2, (d >> 1) & 1, d & 1


def _swap_add(name, parts, owns_prev, dests):
    na = len(parts)
    step = 256

    def body(*refs):
        p_refs, h_refs, own_refs = refs[:na], refs[2 * na:3 * na], refs[3 * na:4 * na]
        mine_vs, recv_vs = refs[4 * na:5 * na], refs[5 * na:6 * na]
        send_sems, recv_sems, load_sems, store_sems, own_sems = refs[6 * na:]
        x, y, c = _mesh_pos()
        work = []
        for a in range(na):
            for i, d in enumerate(dests[a]):
                dx, dy, dc = _dev(d)
                keep = c == dc
                mine = jnp.logical_and(keep, jnp.logical_and(x == dx, y == dy))
                swap = pltpu.make_async_remote_copy(
                    src_ref=p_refs[a].at[i], dst_ref=recv_vs[a].at[i], send_sem=send_sems.at[a, i],
                    recv_sem=recv_sems.at[a, i], device_id=(x, y, 1 - c), device_id_type=MESH)
                load = pltpu.make_async_copy(p_refs[a].at[i], mine_vs[a].at[i], load_sems.at[a, i])
                store = pltpu.make_async_copy(mine_vs[a].at[i], h_refs[a].at[i], store_sems.at[a, i])
                own = pltpu.make_async_copy(mine_vs[a].at[i], own_refs[a], own_sems.at[a])
                pl.when(keep)(load.start)
                pl.when(jnp.logical_not(keep))(swap.start)
                work.append((a, i, keep, mine, swap, load, store, own))
        for a, i, keep, mine, swap, load, store, own in work:
            cols = parts[a].shape[2]
            cstep = math.gcd(cols, step)

            @pl.when(keep)
            def _(a=a, i=i, mine=mine, swap=swap, load=load, store=store, own=own, cols=cols,
                  cstep=cstep):
                load.wait()
                swap.wait_recv()

                @pl.loop(0, cols // cstep)
                def _(r):
                    cs = pl.ds(pl.multiple_of(r * cstep, cstep), cstep)
                    mine_vs[a][i, :, cs] = (mine_vs[a][i, :, cs].astype(F32)
                                            + recv_vs[a][i, :, cs].astype(F32)).astype(BF16)

                store.start()
                pl.when(mine)(own.start)
        for a, i, keep, mine, swap, load, store, own in work:
            pl.when(jnp.logical_not(keep))(swap.wait_send)
            pl.when(keep)(store.wait)
            pl.when(mine)(own.wait)

    stage = [pltpu.VMEM(p.shape, BF16) for p in parts]
    most = max(len(d) for d in dests)
    res = pl.pallas_call(
        body, name=name,
        out_shape=tuple([jax.ShapeDtypeStruct(p.shape, BF16) for p in parts]
                        + [jax.ShapeDtypeStruct(o.shape, BF16) for o in owns_prev]),
        in_specs=_hbm_specs(2 * na), out_specs=tuple(_hbm_specs(2 * na)),
        input_output_aliases={na + a: na + a for a in range(na)},
        scratch_shapes=stage + stage + [pltpu.SemaphoreType.DMA((na, most)) for _ in range(4)]
        + [pltpu.SemaphoreType.DMA((na,))],
        compiler_params=_cparams(),
    )(*parts, *owns_prev)
    return res[:na], res[na:]


def _ici_copies(h_ref, land_ref, send_sems, recv_sems, dests):
    x, y, c = _mesh_pos()
    sends, arrivals = [], []
    for i, d in enumerate(dests):
        dx, dy, dc = _dev(d)
        j = (x != dx).astype(jnp.int32) + 2 * (y != dy).astype(jnp.int32)
        slot = jnp.maximum(j - 1, 0)
        sends.append((jnp.logical_and(c == dc, j > 0), pltpu.make_async_remote_copy(
            src_ref=h_ref.at[i], dst_ref=land_ref.at[slot], send_sem=send_sems.at[i],
            recv_sem=recv_sems.at[slot], device_id=(dx, dy, dc), device_id_type=MESH)))
        arrivals.append((jnp.logical_and(c == dc, j == 0), [pltpu.make_async_remote_copy(
            src_ref=h_ref.at[i], dst_ref=land_ref.at[r], send_sem=send_sems.at[i],
            recv_sem=recv_sems.at[r], device_id=(dx, dy, dc), device_id_type=MESH)
            for r in range(3)]))
    return sends, arrivals


def _ici_start(name, hs, lands, dests):
    na = len(hs)

    def body(*refs):
        h_refs, land_refs, sems = refs[:na], refs[na:2 * na], refs[2 * na:4 * na]
        token = refs[-1]
        for a in range(na):
            sends, _ = _ici_copies(h_refs[a], land_refs[a], sems[2 * a], sems[2 * a + 1], dests[a])
            for go, cp in sends:
                pl.when(go)(cp.start)
        token[...] = jnp.zeros_like(token)

    hbm, sem = pl.BlockSpec(memory_space=pltpu.HBM), pl.BlockSpec(memory_space=pltpu.SEMAPHORE)
    sem_shapes = []
    for a in range(na):
        sem_shapes += [pltpu.SemaphoreType.DMA((len(dests[a]),)), pltpu.SemaphoreType.DMA((3,))]
    res = pl.pallas_call(
        body, name=name,
        out_shape=tuple(sem_shapes) + tuple(pltpu.HBM(v.shape, v.dtype) for v in list(hs) + list(lands))
        + (jax.ShapeDtypeStruct((8, LANE), F32),),
        in_specs=(hbm,) * (2 * na),
        out_specs=(sem,) * (2 * na) + (hbm,) * (2 * na) + (pl.BlockSpec(memory_space=pltpu.VMEM),),
        input_output_aliases={i: 2 * na + i for i in range(2 * na)},
        compiler_params=pltpu.CompilerParams(
            has_side_effects=pltpu.SideEffectType.DATAFLOW_SIDE_EFFECTING),
    )(*[pltpu.with_memory_space_constraint(v, pltpu.HBM) for v in list(hs) + list(lands)])
    sems = [(res[2 * a], res[2 * a + 1]) for a in range(na)]
    return sems, res[2 * na:3 * na], res[3 * na:4 * na], res[-1]


def _ici_wait(name, started, lands, after):
    k, nl = len(started), len(lands)

    def body(*refs):
        land_refs = refs[3 * k:3 * k + nl]
        for s in range(k):
            h_ref, send_sems, recv_sems = refs[3 * s:3 * s + 3]
            sends, arrivals = _ici_copies(h_ref, land_refs[started[s][3]], send_sems, recv_sems,
                                          started[s][4])
            for go, cp in sends:
                pl.when(go)(cp.wait_send)
            for here, cps in arrivals:
                for cp in cps:
                    pl.when(here)(cp.wait_recv)

    hbm, sem = pl.BlockSpec(memory_space=pltpu.HBM), pl.BlockSpec(memory_space=pltpu.SEMAPHORE)
    operands, specs = [], []
    for send_sems, recv_sems, h, _, _ in started:
        operands += [h, send_sems, recv_sems]
        specs += [hbm, sem, sem]
    return pl.pallas_call(
        body, name=name,
        out_shape=tuple(pltpu.HBM(v.shape, v.dtype) for v in lands),
        in_specs=tuple(specs) + (hbm,) * nl + (pl.BlockSpec(memory_space=pl.ANY),),
        out_specs=(hbm,) * nl,
        input_output_aliases={3 * k + i: i for i in range(nl)},
        compiler_params=pltpu.CompilerParams(
            has_side_effects=pltpu.SideEffectType.DATAFLOW_SIDE_EFFECTING),
    )(*operands, *lands, after)


def _weights_to_p(gathered):
    tl = 256
    segs = sorted(_segments(), key=lambda s: s[3])

    def body(g_ref, o_ref):
        pieces, pos = [], 0
        for d, c0, n, p0 in segs:
            if p0 > pos:
                pieces.append(jnp.zeros((p0 - pos, tl), F32))
            pieces.append(g_ref[d, c0:c0 + n, :].astype(F32))
            pos = p0 + n
        pieces.append(jnp.zeros((P_TOTAL - pos, tl), F32))
        o_ref[...] = jnp.concatenate(pieces, axis=0).astype(BF16)

    return pl.pallas_call(
        body, name="weights_to_layout",
        grid=(D_MODEL // tl,),
        in_specs=[pl.BlockSpec((N_DEV, SHARD_COLS, tl), lambda i: (0, 0, i))],
        out_specs=pl.BlockSpec((P_TOTAL, tl), lambda i: (0, i)),
        out_shape=jax.ShapeDtypeStruct((P_TOTAL, D_MODEL), BF16),
        compiler_params=_cparams(("arbitrary",)),
    )(gathered)


def _group_of(p0):
    return max(i for i, (off, _) in enumerate(P_GROUPS) if off <= p0)


def _shard_groups(d):
    return sorted({_group_of(s[3]) for s in _segments() if s[0] == d})


def _grads_to_shards(name, groups, dests):
    tl = 256
    segs = _segments()
    used = sorted(groups)

    def body(*refs):
        g_refs, o_ref = dict(zip(used, refs[:-1])), refs[-1]
        for i, d in enumerate(dests):
            pieces = []
            for _, c0, n, p0 in sorted([s for s in segs if s[0] == d], key=lambda s: s[1]):
                gi = _group_of(p0)
                lo = p0 - P_GROUPS[gi][0]
                pieces.append(g_refs[gi][lo:lo + n, :].astype(F32))
            o_ref[i] = jnp.concatenate(pieces, axis=0).astype(BF16)

    return pl.pallas_call(
        body, name=name,
        grid=(D_MODEL // tl,),
        in_specs=[pl.BlockSpec((P_GROUPS[g][1], tl), lambda i: (0, i)) for g in used],
        out_specs=pl.BlockSpec((len(dests), SHARD_COLS, tl), lambda i: (0, 0, i)),
        out_shape=jax.ShapeDtypeStruct((len(dests), SHARD_COLS, D_MODEL), BF16),
        compiler_params=_cparams(("arbitrary",)),
    )(*[groups[g] for g in used])


def _inproj(x, g_in, w_pt):
    t = x.shape[0]
    tm = min(256, t)
    nj = 3
    tn = P_TOTAL // nj

    def body(x_ref, g_ref, w_ref, proj_ref, h_ref, r_ref):
        xf = x_ref[...]
        r = lax.rsqrt(jnp.mean(xf * xf, axis=-1, keepdims=True) + EPS)
        h = ((xf * r) * g_ref[...]).astype(BF16)
        proj_ref[...] = _dot_nt(h, w_ref[...])

        @pl.when(pl.program_id(0) == 0)
        def _():
            h_ref[...] = h
            r_ref[...] = r

    first = lambda j, i: (jnp.where(j == 0, i, t // tm - 1), 0)
    return pl.pallas_call(
        body, name="inproj",
        grid=(nj, t // tm),
        in_specs=[pl.BlockSpec((tm, D_MODEL), lambda j, i: (i, 0)),
                  pl.BlockSpec((1, D_MODEL), lambda j, i: (0, 0)),
                  pl.BlockSpec((tn, D_MODEL), lambda j, i: (j, 0))],
        out_specs=(pl.BlockSpec((tm, tn), lambda j, i: (i, j)),
                   pl.BlockSpec((tm, D_MODEL), first),
                   pl.BlockSpec((tm, 1), first)),
        out_shape=(jax.ShapeDtypeStruct((t, P_TOTAL), F32),
                   jax.ShapeDtypeStruct((t, D_MODEL), BF16),
                   jax.ShapeDtypeStruct((t, 1), F32)),
        compiler_params=_cparams(("arbitrary", "arbitrary")),
    )(x, g_in, w_pt)


def _mla_prep(proj, g_q, g_kv, w_uq_p, w_k_p, w_v, w_gate_p, b_gate, rc, rsn, rsp):
    t = proj.shape[0]
    tm = min(256, t)
    hq = MLA_HEADS * HEAD_PAD

    def body(cq_ref, ckv_ref, misc_ref, gq_ref, gkv_ref, wuq_ref, wk_ref, wv_ref, wg_ref, bg_ref,
             c_ref, sn_ref, sp_ref,
             q_ref, k_ref, v_ref, la_ref, pre_ref, cqn_ref, ckvn_ref, rq_ref, rkv_ref, mb_ref):
        c, sn, sp = c_ref[...], sn_ref[...], sp_ref[...]
        cq = cq_ref[:, :MLA_Q_RANK]
        rq = lax.rsqrt(jnp.mean(cq * cq, axis=-1, keepdims=True) + EPS)
        cqn = ((cq * rq) * gq_ref[...]).astype(BF16)
        cqn_ref[...] = cqn
        rq_ref[...] = rq
        qpre = _dot(cqn, wuq_ref[...])
        ckv = ckv_ref[...]
        rkv = lax.rsqrt(jnp.mean(ckv * ckv, axis=-1, keepdims=True) + EPS)
        ckvn = ((ckv * rkv) * gkv_ref[...]).astype(BF16)
        ckvn_ref[...] = ckvn
        rkv_ref[...] = rkv
        kn = _dot(ckvn, wk_ref[...])
        v_ref[...] = _dot(ckvn, wv_ref[...]).astype(BF16)
        misc = misc_ref[...]
        krope = _rope_fwd(misc, c, sn, sp)
        for h in range(MLA_HEADS):
            sl = slice(h * HEAD_PAD, (h + 1) * HEAD_PAD)
            q_ref[:, sl] = _rope_fwd(qpre[:, sl], c, sn, sp).astype(BF16)
            k_ref[:, sl] = (kn[:, sl] + krope).astype(BF16)
        mb_ref[...] = misc.astype(BF16)
        pre = _dot(mb_ref[...], wg_ref[...]) + bg_ref[...]
        pre_ref[...] = pre
        la_ref[...] = (jnp.minimum(pre, 0.0) - jnp.log(1.0 + jnp.exp(-jnp.abs(pre)))) / GLA_GATE_NORM

    row = lambda w: pl.BlockSpec((tm, w), lambda i: (i, 0))
    full = lambda a: pl.BlockSpec(a.shape, lambda i: (0, 0))
    return pl.pallas_call(
        body, name="mla_prep",
        grid=(t // tm,),
        in_specs=[pl.BlockSpec((tm, 512), lambda i: (i, P_CQ // 512)),
                  pl.BlockSpec((tm, MLA_KV_RANK), lambda i: (i, P_CKV // MLA_KV_RANK)),
                  pl.BlockSpec((tm, LANE), lambda i: (i, P_MISC // LANE)),
                  full(g_q), full(g_kv), full(w_uq_p), full(w_k_p), full(w_v), full(w_gate_p),
                  full(b_gate), row(LANE), row(LANE), row(LANE)],
        out_specs=(row(hq), row(hq), row(MLA_WIDTH), row(GLA_DK), row(GLA_DK),
                   row(MLA_Q_RANK), row(MLA_KV_RANK), row(1), row(1), row(LANE)),
        out_shape=(jax.ShapeDtypeStruct((t, hq), BF16), jax.ShapeDtypeStruct((t, hq), BF16),
                   jax.ShapeDtypeStruct((t, MLA_WIDTH), BF16),
                   jax.ShapeDtypeStruct((t, GLA_DK), F32), jax.ShapeDtypeStruct((t, GLA_DK), F32),
                   jax.ShapeDtypeStruct((t, MLA_Q_RANK), BF16),
                   jax.ShapeDtypeStruct((t, MLA_KV_RANK), BF16),
                   jax.ShapeDtypeStruct((t, 1), F32), jax.ShapeDtypeStruct((t, 1), F32),
                   jax.ShapeDtypeStruct((t, LANE), BF16)),
        compiler_params=_cparams(("arbitrary",)),
    )(proj, proj, proj, g_q, g_kv, w_uq_p, w_k_p, w_v, w_gate_p, b_gate, rc, rsn, rsp)


def _attn_masks(tq, i):
    keys = (i + 1) * tq
    rows = i * tq + lax.broadcasted_iota(jnp.int32, (tq, keys), 0)
    cols = lax.broadcasted_iota(jnp.int32, (tq, keys), 1)
    lane = lax.broadcasted_iota(jnp.int32, (tq, LANE), 1)
    return cols <= rows, lane < MLA_VDIM


def _for_each_query_tile(n_tiles, fn):
    for i in range(n_tiles):
        pl.when(pl.program_id(1) == i)(lambda i=i: fn(i))


def _mla_attn_fwd(q, k, v):
    t = q.shape[0]
    tq = min(256, t)
    scale = MLA_QK ** -0.5

    def body(q_ref, k_ref, v_ref, o_ref, lse_ref):
        def tile(i):
            keys = (i + 1) * tq
            causal, low = _attn_masks(tq, i)
            vp = v_ref[0:keys, :]
            acc = jnp.zeros((tq, LANE), F32)
            for hh in range(2):
                sl = slice(hh * HEAD_PAD, (hh + 1) * HEAD_PAD)
                s = _dot_nt(q_ref[:, sl], k_ref[0:keys, sl]) * scale
                s = jnp.where(causal, s, -jnp.inf)
                m = jnp.max(s, axis=-1, keepdims=True)
                e = jnp.exp(s - m)
                l = jnp.sum(e, axis=-1, keepdims=True)
                o = _dot(e.astype(BF16), vp) / l
                acc = jnp.where(low if hh == 0 else jnp.logical_not(low), o, acc)
                lse_ref[hh] = m + jnp.log(l)
            o_ref[...] = acc

        _for_each_query_tile(t // tq, tile)

    return pl.pallas_call(
        body, name="mla_attn_fwd",
        grid=(MLA_HEADS // 2, t // tq),
        in_specs=[pl.BlockSpec((tq, 2 * HEAD_PAD), lambda p, i: (i, p)),
                  pl.BlockSpec((t, 2 * HEAD_PAD), lambda p, i: (0, p)),
                  pl.BlockSpec((t, LANE), lambda p, i: (0, p))],
        out_specs=(pl.BlockSpec((tq, LANE), lambda p, i: (i, p)),
                   pl.BlockSpec((2, tq, 1), lambda p, i: (p, i, 0))),
        out_shape=(jax.ShapeDtypeStruct((t, MLA_WIDTH), F32),
                   jax.ShapeDtypeStruct((MLA_HEADS, t, 1), F32)),
        compiler_params=_cparams(("arbitrary", "arbitrary")),
    )(q, k, v)


def _mla_attn_bwd(q, k, v, o, do, lse, after):
    t = q.shape[0]
    tq = min(256, t)
    scale = MLA_QK ** -0.5

    def body(q_ref, k_ref, v_ref, o_ref, do_ref, lse_ref, after_ref, dq_ref, dk_ref, dv_ref):
        del after_ref

        @pl.when(pl.program_id(1) == 0)
        def _():
            dk_ref[...] = jnp.zeros_like(dk_ref)
            dv_ref[...] = jnp.zeros_like(dv_ref)

        def tile(i):
            keys = (i + 1) * tq
            causal, low = _attn_masks(tq, i)
            vp = v_ref[0:keys, :]
            do_all = do_ref[...]
            o_all = o_ref[...]
            dv_acc = jnp.zeros((keys, LANE), F32)
            for hh in range(2):
                sl = slice(hh * HEAD_PAD, (hh + 1) * HEAD_PAD)
                do_h = jnp.where(low if hh == 0 else jnp.logical_not(low), do_all, 0.0)
                dsum = jnp.sum(do_h * o_all, axis=-1, keepdims=True)
                qh = q_ref[:, sl]
                kh = k_ref[0:keys, sl]
                s = _dot_nt(qh, kh) * scale
                p = jnp.where(causal, jnp.exp(s - lse_ref[hh]), 0.0)
                do_b = do_h.astype(BF16)
                dp = _dot_nt(do_b, vp)
                ds = (p * (dp - dsum) * scale).astype(BF16)
                dq_ref[:, sl] = _dot(ds, kh).astype(BF16)
                dk_ref[0:keys, sl] += _dot_tn(ds, qh)
                dv_acc = dv_acc + _dot_tn(p.astype(BF16), do_b)
            dv_ref[0:keys, :] += dv_acc

        _for_each_query_tile(t // tq, tile)

    return pl.pallas_call(
        body, name="mla_attn_bwd",
        grid=(MLA_HEADS // 2, t // tq),
        in_specs=[pl.BlockSpec((tq, 2 * HEAD_PAD), lambda p, i: (i, p)),
                  pl.BlockSpec((t, 2 * HEAD_PAD), lambda p, i: (0, p)),
                  pl.BlockSpec((t, LANE), lambda p, i: (0, p)),
                  pl.BlockSpec((tq, LANE), lambda p, i: (i, p)),
                  pl.BlockSpec((tq, LANE), lambda p, i: (i, p)),
                  pl.BlockSpec((2, tq, 1), lambda p, i: (p, i, 0)),
                  pl.BlockSpec(memory_space=pl.ANY)],
        out_specs=(pl.BlockSpec((tq, 2 * HEAD_PAD), lambda p, i: (i, p)),
                   pl.BlockSpec((t, 2 * HEAD_PAD), lambda p, i: (0, p)),
                   pl.BlockSpec((t, LANE), lambda p, i: (0, p))),
        out_shape=(jax.ShapeDtypeStruct((t, MLA_HEADS * HEAD_PAD), BF16),
                   jax.ShapeDtypeStruct((t, MLA_HEADS * HEAD_PAD), F32),
                   jax.ShapeDtypeStruct((t, MLA_WIDTH), F32)),
        compiler_params=_cparams(("arbitrary", "arbitrary")),
    )(q, k, v, o, do, lse, after)


def _gla_chunk_terms(q_ref, k_ref, la_ref, h, tri):
    sl = slice(h * GLA_HK, (h + 1) * GLA_HK)
    b = _dot_exact(tri, la_ref[:, sl])
    bl = b[GLA_CHUNK - 1:GLA_CHUNK, :]
    kc = k_ref[:, sl]
    q_in = (q_ref[:, sl] * (GLA_HK ** -0.5)) * jnp.exp(b)
    k_in = kc * jnp.exp(-b)
    k_st = kc * jnp.exp(bl - b)
    return b, bl, q_in, k_in, k_st


def _tri(c, lower):
    r = lax.broadcasted_iota(jnp.int32, (c, c), 0)
    cc = lax.broadcasted_iota(jnp.int32, (c, c), 1)
    return jnp.where(r >= cc if lower else r <= cc, 1.0, 0.0).astype(F32)


def _gla_fwd(proj, log_a):
    t = proj.shape[0]
    c = GLA_CHUNK
    n = t // c

    def body(q_ref, k_ref, v_ref, la_ref, o_ref, sp_ref, st_ref):
        @pl.when(pl.program_id(0) == 0)
        def _():
            st_ref[...] = jnp.zeros_like(st_ref)

        tri = _tri(c, True)
        for h in range(GLA_HEADS):
            _, bl, q_in, k_in, k_st = _gla_chunk_terms(q_ref, k_ref, la_ref, h, tri)
            vs = slice(h * GLA_HV, (h + 1) * GLA_HV)
            vv = v_ref[:, vs].astype(BF16)
            qb = q_in.astype(BF16)
            attn = _dot_nt(qb, k_in.astype(BF16)) * tri
            st = st_ref[h]
            sp_ref[0, h] = st
            o_ref[:, vs] = _dot(attn.astype(BF16), vv) + _dot_nt(qb, st.astype(BF16))
            st_ref[h] = st * jnp.exp(bl) + _dot_tn(vv, k_st.astype(BF16))

    return pl.pallas_call(
        body, name="gla_fwd",
        grid=(n,),
        in_specs=[pl.BlockSpec((c, GLA_DK), lambda i: (i, P_QG // GLA_DK)),
                  pl.BlockSpec((c, GLA_DK), lambda i: (i, P_KG // GLA_DK)),
                  pl.BlockSpec((c, GLA_DV), lambda i: (i, P_VG // GLA_DV)),
                  pl.BlockSpec((c, GLA_DK), lambda i: (i, 0))],
        out_specs=(pl.BlockSpec((c, GLA_DV), lambda i: (i, 0)),
                   pl.BlockSpec((1, GLA_HEADS, GLA_HV, GLA_HK), lambda i: (i, 0, 0, 0))),
        out_shape=(jax.ShapeDtypeStruct((t, GLA_DV), F32),
                   jax.ShapeDtypeStruct((n, GLA_HEADS, GLA_HV, GLA_HK), F32)),
        scratch_shapes=[pltpu.VMEM((GLA_HEADS, GLA_HV, GLA_HK), F32)],
        compiler_params=_cparams(("arbitrary",)),
    )(proj, proj, proj, log_a)


def _gla_bwd(proj, log_a, do, states, after):
    t = proj.shape[0]
    c = GLA_CHUNK
    n = t // c

    def body(q_ref, k_ref, v_ref, la_ref, do_ref, sp_ref, after_ref, dg_ref, dla_ref, ds_ref):
        del after_ref

        @pl.when(pl.program_id(0) == 0)
        def _():
            ds_ref[...] = jnp.zeros_like(ds_ref)

        tri = _tri(c, True)
        tri_t = _tri(c, False)
        for h in range(GLA_HEADS):
            b, bl, q_in, k_in, k_st = _gla_chunk_terms(q_ref, k_ref, la_ref, h, tri)
            ks_ = slice(h * GLA_HK, (h + 1) * GLA_HK)
            vs = slice(h * GLA_HV, (h + 1) * GLA_HV)
            vv = v_ref[:, vs].astype(BF16)
            do_h = do_ref[:, vs]
            qb, kb, ksb = q_in.astype(BF16), k_in.astype(BF16), k_st.astype(BF16)
            attn = (_dot_nt(qb, kb) * tri).astype(BF16)
            st = sp_ref[0, h]
            dst = ds_ref[h]
            dstb = dst.astype(BF16)
            dattn = (_dot_nt(do_h, vv) * tri).astype(BF16)
            dg_ref[:, P_VG + h * GLA_HV:P_VG + (h + 1) * GLA_HV] = (
                _dot_tn(attn, do_h) + _dot_nt(ksb, dstb)).astype(BF16)
            dq_in = _dot(dattn, kb) + _dot(do_h, st.astype(BF16))
            dk_in = _dot_tn(dattn, qb)
            dk_st = _dot(vv, dstb)
            ebl = jnp.exp(bl)
            d_ebl = jnp.sum(st * dst, axis=0, keepdims=True)
            ds_ref[h] = _dot_tn(do_h, qb) + dst * ebl
            dg_ref[:, P_QG + h * GLA_HK:P_QG + (h + 1) * GLA_HK] = (
                dq_in * (GLA_HK ** -0.5) * jnp.exp(b)).astype(BF16)
            dg_ref[:, P_KG + h * GLA_HK:P_KG + (h + 1) * GLA_HK] = (
                dk_in * jnp.exp(-b) + dk_st * jnp.exp(bl - b)).astype(BF16)
            db = dq_in * q_in - dk_in * k_in - dk_st * k_st
            dbl = jnp.sum(dk_st * k_st, axis=0, keepdims=True) + d_ebl * ebl
            dla_ref[:, ks_] = _dot_exact(tri_t, db) + dbl

    rev = lambda i: n - 1 - i
    gw = P_GROUPS[0][1]
    return pl.pallas_call(
        body, name="gla_bwd",
        grid=(n,),
        in_specs=[pl.BlockSpec((c, GLA_DK), lambda i: (rev(i), P_QG // GLA_DK)),
                  pl.BlockSpec((c, GLA_DK), lambda i: (rev(i), P_KG // GLA_DK)),
                  pl.BlockSpec((c, GLA_DV), lambda i: (rev(i), P_VG // GLA_DV)),
                  pl.BlockSpec((c, GLA_DK), lambda i: (rev(i), 0)),
                  pl.BlockSpec((c, GLA_DV), lambda i: (rev(i), 0)),
                  pl.BlockSpec((1, GLA_HEADS, GLA_HV, GLA_HK), lambda i: (rev(i), 0, 0, 0)),
                  pl.BlockSpec(memory_space=pl.ANY)],
        out_specs=(pl.BlockSpec((c, gw), lambda i: (rev(i), 0)),
                   pl.BlockSpec((c, GLA_DK), lambda i: (rev(i), 0))),
        out_shape=(jax.ShapeDtypeStruct((t, gw), BF16), jax.ShapeDtypeStruct((t, GLA_DK), F32)),
        scratch_shapes=[pltpu.VMEM((GLA_HEADS, GLA_HV, GLA_HK), F32)],
        compiler_params=_cparams(("arbitrary",)),
    )(proj, proj, proj, log_a, do, states, after)


def _post(o_mla, proj, o_gla, x, target, g_gla, g_final, w_pm, w_pg, w_o):
    t = x.shape[0]
    tm = min(128, t)
    g0, gw = P_GROUPS[1]

    def body(om_ref, zg_ref, gm_ref, gg_ref, zm_ref, og_ref, x_ref, tg_ref, ggla_ref, gf_ref,
             wpm_ref, wpg_ref, wo_ref,
             dx2_ref, dom_ref, dog_ref, dg_ref,
             mg_ref, um_ref, ug_ref, dym_ref, dyg_ref, loss_ref, dgf_ref, dggla_ref):
        @pl.when(pl.program_id(0) == 0)
        def _():
            loss_ref[...] = jnp.zeros_like(loss_ref)
            dgf_ref[...] = jnp.zeros_like(dgf_ref)
            dggla_ref[...] = jnp.zeros_like(dggla_ref)

        om = om_ref[...]
        zm = zm_ref[...]
        sm = _sigmoid(zm)
        silu_m = zm * sm
        um = (om * silu_m).astype(BF16)
        um_ref[...] = um
        ym = _dot(um, wpm_ref[...])

        ggla = ggla_ref[...]
        zg = zg_ref[...]
        sg = _sigmoid(zg)
        silu_g = zg * sg
        xhat, rstd, on = [], [], []
        for h in range(GLA_HEADS):
            blk = og_ref[:, h * GLA_HV:(h + 1) * GLA_HV]
            r = lax.rsqrt(jnp.mean(blk * blk, axis=-1, keepdims=True) + EPS)
            xhat.append(blk * r)
            rstd.append(r)
            on.append(xhat[h] * ggla)
        on = jnp.concatenate(on, axis=-1)
        ug = (on * silu_g).astype(BF16)
        ug_ref[...] = ug
        yg = _dot(ug, wpg_ref[...])

        sgm = _sigmoid(gm_ref[...])
        sgg = _sigmoid(gg_ref[...])
        merged = (sgm * ym + sgg * yg).astype(BF16)
        mg_ref[...] = merged
        x2 = x_ref[...] + _dot(merged, wo_ref[...])
        gf = gf_ref[...]
        rf = lax.rsqrt(jnp.mean(x2 * x2, axis=-1, keepdims=True) + EPS)
        xh = x2 * rf
        err = xh * gf - tg_ref[...]
        loss_ref[...] += 0.5 * jnp.sum(jnp.mean(err * err, axis=-1, keepdims=True))

        dy = err * (1.0 / D_MODEL)
        dgf_ref[...] += jnp.sum(dy * xh, axis=0, keepdims=True)
        dxh = dy * gf
        dx2 = rf * (dxh - xh * jnp.mean(dxh * xh, axis=-1, keepdims=True))
        dx2_ref[...] = dx2
        dmerged = _dot_nt(dx2.astype(BF16), wo_ref[...])
        dym = (dmerged * sgm).astype(BF16)
        dyg = (dmerged * sgg).astype(BF16)
        dym_ref[...] = dym
        dyg_ref[...] = dyg
        dg_ref[:, P_GMLA - g0:P_GMLA - g0 + D_MODEL] = (dmerged * ym * sgm * (1.0 - sgm)).astype(BF16)
        dg_ref[:, P_GGLA - g0:P_GGLA - g0 + D_MODEL] = (dmerged * yg * sgg * (1.0 - sgg)).astype(BF16)
        dum = _dot_nt(dym, wpm_ref[...])
        dom_ref[...] = dum * silu_m
        dg_ref[:, P_ZMLA - g0:P_ZMLA - g0 + MLA_WIDTH] = (
            dum * om * (sm * (1.0 + zm * (1.0 - sm)))).astype(BF16)
        dug = _dot_nt(dyg, wpg_ref[...])
        dg_ref[:, P_ZGLA - g0:P_ZGLA - g0 + GLA_DV] = (
            dug * on * (sg * (1.0 + zg * (1.0 - sg)))).astype(BF16)
        don = dug * silu_g
        dggla = jnp.zeros((1, GLA_HV), F32)
        for h in range(GLA_HEADS):
            hs = slice(h * GLA_HV, (h + 1) * GLA_HV)
            don_h = don[:, hs]
            dggla = dggla + jnp.sum(don_h * xhat[h], axis=0, keepdims=True)
            dxh_h = don_h * ggla
            dog_ref[:, hs] = (rstd[h] * (dxh_h - xhat[h] * jnp.mean(dxh_h * xhat[h], axis=-1,
                                                                     keepdims=True))).astype(BF16)
        dggla_ref[...] += dggla

    row = lambda w: pl.BlockSpec((tm, w), lambda i: (i, 0))
    pcol = lambda w, off: pl.BlockSpec((tm, w), lambda i: (i, off // w))
    full = lambda a: pl.BlockSpec(a.shape, lambda i: (0, 0))
    sds = jax.ShapeDtypeStruct
    return pl.pallas_call(
        body, name="post_fwd_bwd",
        grid=(t // tm,),
        in_specs=[row(MLA_WIDTH), pcol(GLA_DV, P_ZGLA), pcol(D_MODEL, P_GMLA), pcol(D_MODEL, P_GGLA),
                  pcol(MLA_WIDTH, P_ZMLA), row(GLA_DV), row(D_MODEL), row(D_MODEL),
                  full(g_gla), full(g_final), full(w_pm), full(w_pg), full(w_o)],
        out_specs=(row(D_MODEL), row(MLA_WIDTH), row(GLA_DV), row(gw),
                   row(D_MODEL), row(MLA_WIDTH), row(GLA_DV), row(D_MODEL), row(D_MODEL),
                   pl.BlockSpec((1, LANE), lambda i: (0, 0)),
                   pl.BlockSpec((1, D_MODEL), lambda i: (0, 0)),
                   pl.BlockSpec((1, GLA_HV), lambda i: (0, 0))),
        out_shape=(sds((t, D_MODEL), F32), sds((t, MLA_WIDTH), F32), sds((t, GLA_DV), BF16),
                   sds((t, gw), BF16),
                   sds((t, D_MODEL), BF16), sds((t, MLA_WIDTH), BF16), sds((t, GLA_DV), BF16),
                   sds((t, D_MODEL), BF16), sds((t, D_MODEL), BF16),
                   sds((1, LANE), F32), sds((1, D_MODEL), F32), sds((1, GLA_HV), F32)),
        compiler_params=_cparams(("arbitrary",)),
    )(o_mla, proj, proj, proj, proj, o_gla, x, target, g_gla, g_final, w_pm, w_pg, w_o)


def _mla_prep_bwd(dq, dk, dv, dla, pre, proj, rq, rkv, g_q, g_kv, w_uq_p, w_k_p, w_v, w_gate_p,
                  rc, rsn, rsp):
    t = proj.shape[0]
    tm = min(256, t)
    gw = P_GROUPS[2][1]

    def body(dq_ref, dk_ref, dv_ref, dla_ref, pre_ref, cq_ref, ckv_ref, rq_ref, rkv_ref,
             gq_ref, gkv_ref, wuq_ref, wk_ref, wv_ref, wg_ref, c_ref, sn_ref, sp_ref,
             dg_ref, dqpre_ref, dpre_ref, dgq_ref, dgkv_ref, dbg_ref):
        @pl.when(pl.program_id(0) == 0)
        def _():
            dgq_ref[...] = jnp.zeros_like(dgq_ref)
            dgkv_ref[...] = jnp.zeros_like(dgkv_ref)
            dbg_ref[...] = jnp.zeros_like(dbg_ref)

        c, sn, sp = c_ref[...], sn_ref[...], sp_ref[...]
        dkr = jnp.zeros((tm, LANE), F32)
        for h in range(MLA_HEADS):
            sl = slice(h * HEAD_PAD, (h + 1) * HEAD_PAD)
            dqpre_ref[:, sl] = _rope_bwd(dq_ref[:, sl].astype(F32), c, sn, sp).astype(BF16)
            dkr = dkr + dk_ref[:, sl]
        dcqn = _dot_nt(dqpre_ref[...], wuq_ref[...])
        rq = rq_ref[...]
        xh = cq_ref[:, :MLA_Q_RANK] * rq
        dgq_ref[...] += jnp.sum(dcqn * xh, axis=0, keepdims=True)
        dxh = dcqn * gq_ref[...]
        dcq = rq * (dxh - xh * jnp.mean(dxh * xh, axis=-1, keepdims=True))
        dg_ref[:, :MLA_Q_RANK] = dcq.astype(BF16)
        dg_ref[:, MLA_Q_RANK:512] = jnp.zeros((tm, 512 - MLA_Q_RANK), BF16)

        dckvn = _dot_nt(dk_ref[...].astype(BF16), wk_ref[...]) + \
            _dot_nt(dv_ref[...].astype(BF16), wv_ref[...])
        rkv = rkv_ref[...]
        xh = ckv_ref[...] * rkv
        dgkv_ref[...] += jnp.sum(dckvn * xh, axis=0, keepdims=True)
        dxh = dckvn * gkv_ref[...]
        dg_ref[:, P_CKV - P_CQ:P_CKV - P_CQ + MLA_KV_RANK] = (
            rkv * (dxh - xh * jnp.mean(dxh * xh, axis=-1, keepdims=True))).astype(BF16)

        dpre = dla_ref[...] * (1.0 / GLA_GATE_NORM) * (1.0 - _sigmoid(pre_ref[...]))
        dbg_ref[...] += jnp.sum(dpre, axis=0, keepdims=True)
        dpre = dpre.astype(BF16)
        dpre_ref[...] = dpre
        lane = lax.broadcasted_iota(jnp.int32, (tm, LANE), 1)
        in_kr = jnp.logical_and(lane >= MISC_KR, lane < MISC_KR + MLA_ROPE)
        dmisc = jnp.where(in_kr, _rope_bwd(dkr, c, sn, sp), 0.0) + _dot_nt(dpre, wg_ref[...])
        dg_ref[:, P_MISC - P_CQ:P_MISC - P_CQ + LANE] = dmisc.astype(BF16)

    hq = MLA_HEADS * HEAD_PAD
    row = lambda w: pl.BlockSpec((tm, w), lambda i: (i, 0))
    full = lambda a: pl.BlockSpec(a.shape, lambda i: (0, 0))
    acc = lambda w: pl.BlockSpec((1, w), lambda i: (0, 0))
    sds = jax.ShapeDtypeStruct
    return pl.pallas_call(
        body, name="mla_prep_bwd",
        grid=(t // tm,),
        in_specs=[row(hq), row(hq), row(MLA_WIDTH), row(GLA_DK), row(GLA_DK),
                  pl.BlockSpec((tm, 512), lambda i: (i, P_CQ // 512)),
                  pl.BlockSpec((tm, MLA_KV_RANK), lambda i: (i, P_CKV // MLA_KV_RANK)),
                  row(1), row(1), full(g_q), full(g_kv), full(w_uq_p), full(w_k_p), full(w_v),
                  full(w_gate_p), row(LANE), row(LANE), row(LANE)],
        out_specs=(row(gw), row(hq), row(GLA_DK),
                   acc(MLA_Q_RANK), acc(MLA_KV_RANK), acc(GLA_DK)),
        out_shape=(sds((t, gw), BF16), sds((t, hq), BF16), sds((t, GLA_DK), BF16),
                   sds((1, MLA_Q_RANK), F32), sds((1, MLA_KV_RANK), F32), sds((1, GLA_DK), F32)),
        compiler_params=_cparams(("arbitrary",)),
    )(dq, dk, dv, dla, pre, proj, proj, rq, rkv, g_q, g_kv, w_uq_p, w_k_p, w_v, w_gate_p,
      rc, rsn, rsp)


def _inproj_bwd(dgroups, w_pt, x, rstd, g_in, dx2, after):
    t = x.shape[0]
    tm = min(256, t)

    def body(d0_ref, d1_ref, d2_ref, w_ref, x_ref, r_ref, g_ref, dx2_ref, after_ref, dx_ref, dg_ref):
        del after_ref

        @pl.when(pl.program_id(0) == 0)
        def _():
            dg_ref[...] = jnp.zeros_like(dg_ref)

        dh = jnp.zeros((tm, D_MODEL), F32)
        for d_ref, (off, width) in zip((d0_ref, d1_ref, d2_ref), P_GROUPS):
            dh = dh + _dot(d_ref[...], w_ref[off:off + width, :])
        r = r_ref[...]
        xh = x_ref[...] * r
        dg_ref[...] += jnp.sum(dh * xh, axis=0, keepdims=True)
        dxh = dh * g_ref[...]
        dx_ref[...] = dx2_ref[...] + r * (dxh - xh * jnp.mean(dxh * xh, axis=-1, keepdims=True))

    row = lambda w: pl.BlockSpec((tm, w), lambda i: (i, 0))
    return pl.pallas_call(
        body, name="inproj_bwd",
        grid=(t // tm,),
        in_specs=[row(P_GROUPS[0][1]), row(P_GROUPS[1][1]), row(P_GROUPS[2][1]),
                  pl.BlockSpec((P_TOTAL, D_MODEL), lambda i: (0, 0)),
                  row(D_MODEL), row(1), pl.BlockSpec((1, D_MODEL), lambda i: (0, 0)), row(D_MODEL),
                  pl.BlockSpec(memory_space=pl.ANY)],
        out_specs=(row(D_MODEL), pl.BlockSpec((1, D_MODEL), lambda i: (0, 0))),
        out_shape=(jax.ShapeDtypeStruct((t, D_MODEL), F32),
                   jax.ShapeDtypeStruct((1, D_MODEL), F32)),
        compiler_params=_cparams(("arbitrary",)),
    )(*dgroups, w_pt, x, rstd, g_in, dx2, after)


def _matmul(name, a, b, tm, tn, dtype=F32):
    kk, m = a.shape
    n = b.shape[1]

    def body(a_ref, b_ref, o_ref):
        o_ref[...] = _dot_tn(a_ref[...].astype(BF16), b_ref[...].astype(BF16)).astype(dtype)

    return pl.pallas_call(
        body, name=name,
        grid=(n // tn, m // tm),
        in_specs=[pl.BlockSpec((kk, tm), lambda j, i: (0, i)),
                  pl.BlockSpec((kk, tn), lambda j, i: (0, j))],
        out_specs=pl.BlockSpec((tm, tn), lambda j, i: (i, j)),
        out_shape=jax.ShapeDtypeStruct((m, n), dtype),
        compiler_params=_cparams(("arbitrary", "arbitrary")),
    )(a, b)


def _adamw_update(part_refs, w_ref, m_ref, v_ref, g_ref, d_ref, nm_ref, nv_ref):
    g = part_refs[0][...].astype(F32)
    for p_ref in part_refs[1:]:
        g = g + p_ref[...].astype(F32)
    m_new = ADAM_B1 * m_ref[...] + (1.0 - ADAM_B1) * g
    v_new = ADAM_B2 * v_ref[...] + (1.0 - ADAM_B2) * (g * g)
    m_hat = m_new / (1.0 - ADAM_B1 ** ADAM_STEP)
    v_hat = v_new / (1.0 - ADAM_B2 ** ADAM_STEP)
    g_ref[...] = g
    nm_ref[...] = m_new
    nv_ref[...] = v_new
    d_ref[...] = -ADAM_LR * (m_hat / (jnp.sqrt(v_hat) + ADAM_EPS) + ADAM_WD * w_ref[...])


def _adamw_rows(name, parts, w, m, v, tr, first=None):
    _, rows, cols = w.shape
    slots = parts.shape[0]

    def body(*refs):
        lead_refs, p_ref = ([], refs[0]) if first is None else ([refs[0]], refs[1])
        _adamw_update(lead_refs + [p_ref.at[q] for q in range(slots)], *refs[len(lead_refs) + 1:])

    blk = pl.BlockSpec((None, tr, cols), lambda i: (0, i, 0))
    out = jax.ShapeDtypeStruct((1, rows, cols), F32)
    lead = [] if first is None else [pl.BlockSpec((tr, cols), lambda i: (i, 0))]
    return pl.pallas_call(
        body, name=name,
        grid=(rows // tr,),
        in_specs=lead + [pl.BlockSpec((slots, tr, cols), lambda i: (0, i, 0)), blk, blk, blk],
        out_specs=(blk, blk, blk, blk),
        out_shape=(out, out, out, out),
        compiler_params=_cparams(("arbitrary",)),
    )(*([] if first is None else [first]), parts, w, m, v)


def _adamw_transposed(name, first, parts, w, m, v, tl):
    _, rows, cols = w.shape
    slots = parts.shape[0]

    def body(f_ref, p_ref, *refs):
        _adamw_update([f_ref] + [p_ref.at[q] for q in range(slots)], *refs)

    blk = pl.BlockSpec((cols, None, tl), lambda i: (0, 0, i))
    out = jax.ShapeDtypeStruct((cols, 1, rows), F32)
    res = pl.pallas_call(
        body, name=name,
        grid=(rows // tl,),
        in_specs=[pl.BlockSpec((cols, tl), lambda i: (0, i)),
                  pl.BlockSpec((slots, cols, tl), lambda i: (0, 0, i)), blk, blk, blk],
        out_specs=(blk, blk, blk, blk),
        out_shape=(out, out, out, out),
        compiler_params=_cparams(("arbitrary",)),
    )(first, parts, *[a.transpose(2, 0, 1) for a in (w, m, v)])
    return [r.transpose(1, 2, 0) for r in res]


def _adamw_group(firsts, parts, ws, ms, vs):
    n = len(ws)

    def body(*refs):
        ins, outs = refs[:5 * n], refs[5 * n:]
        for a in range(n):
            _adamw_update([ins[a]] + [ins[n + a].at[q] for q in range(ins[n + a].shape[0])],
                          *[r.at[0] for r in (ins[2 * n + a], ins[3 * n + a], ins[4 * n + a])],
                          *[r.at[0] for r in outs[4 * a:4 * a + 4]])

    vmem = lambda k: [pl.BlockSpec(memory_space=pltpu.VMEM) for _ in range(k)]
    out_shape = []
    for w in ws:
        out_shape += [jax.ShapeDtypeStruct(w.shape, F32)] * 4
    res = pl.pallas_call(
        body, name="adamw_small_weights",
        in_specs=vmem(5 * n), out_specs=tuple(vmem(4 * n)), out_shape=tuple(out_shape),
        compiler_params=_cparams(),
    )(*firsts, *parts, *ws, *ms, *vs)
    return [res[4 * a:4 * a + 4] for a in range(n)]


def _rope_tables(positions):
    half = MLA_ROPE // 2
    freqs = ROPE_THETA ** (-jnp.arange(half, dtype=F32) / half)
    ang = positions.astype(F32).reshape(-1, 1) * freqs
    cos, sin = jnp.cos(ang), jnp.sin(ang)
    t = ang.shape[0]
    one, zero = jnp.ones((t, MLA_NOPE), F32), jnp.zeros((t, half), F32)
    tail = jnp.zeros((t, LANE - MLA_QK), F32)
    rc = jnp.concatenate([one, cos, cos, tail], axis=1)
    rsn = jnp.concatenate([0.0 * one, -sin, zero, tail], axis=1)
    rsp = jnp.concatenate([0.0 * one, zero, sin, tail], axis=1)
    return rc, rsn, rsp


def _cols_full(g):
    return g.transpose(1, 0, 2)


def kernel(x, positions, g_in, w_in, g_q, w_uq, g_kv, w_ukv, w_gla_gate, b_gla_gate, g_gla, w_proj_mla, w_proj_gla, w_out, g_final, loss_target, m_g_in, m_w_in, m_g_q, m_w_uq, m_g_kv, m_w_ukv, m_w_gla_gate, m_b_gla_gate, m_g_gla, m_w_proj_mla, m_w_proj_gla, m_w_out, m_g_final, v_g_in, v_w_in, v_g_q, v_w_uq, v_g_kv, v_w_ukv, v_w_gla_gate, v_b_gla_gate, v_g_gla, v_w_proj_mla, v_w_proj_gla, v_w_out, v_g_final):
    t = x.shape[1]
    x2d = x.reshape(t, D_MODEL)
    tgt = loss_target.reshape(t, D_MODEL)
    g_final2 = g_final.reshape(1, D_MODEL)
    sharded = [(w_in, m_w_in, v_w_in), (w_uq, m_w_uq, v_w_uq), (w_ukv, m_w_ukv, v_w_ukv),
               (w_gla_gate, m_w_gla_gate, v_w_gla_gate), (w_proj_mla, m_w_proj_mla, v_w_proj_mla),
               (w_proj_gla, m_w_proj_gla, v_w_proj_gla), (w_out, m_w_out, v_w_out)]

    w_in_t = w_in.transpose(2, 0, 1).reshape(SHARD_COLS, D_MODEL)
    g_w_in, g_uq, g_ukv, g_gate, g_pm, g_pg, g_o = _all_gather(
        [w_in_t.astype(BF16)] + [s[0][0].astype(BF16) for s in sharded[1:]])
    w_in_p = _weights_to_p(g_w_in)
    w_uq_p = jnp.pad(_cols_full(g_uq), ((0, 0), (0, 0), (0, HEAD_PAD - MLA_QK))).reshape(
        MLA_Q_RANK, MLA_HEADS * HEAD_PAD)
    ukv = _cols_full(g_ukv)
    w_k_p = jnp.pad(ukv[:, :, :MLA_NOPE], ((0, 0), (0, 0), (0, HEAD_PAD - MLA_NOPE))).reshape(
        MLA_KV_RANK, MLA_HEADS * HEAD_PAD)
    w_v = ukv[:, :, MLA_NOPE:].reshape(MLA_KV_RANK, MLA_WIDTH)
    w_gate_p = jnp.pad(_cols_full(g_gate).reshape(GLA_GATE_RANK, GLA_DK),
                       ((MISC_ALR, LANE - MISC_ALR - GLA_GATE_RANK), (0, 0)))
    w_pm = _cols_full(g_pm).reshape(MLA_WIDTH, D_MODEL)
    w_pg = g_pg.reshape(GLA_DV, D_MODEL)
    w_o = g_o.reshape(D_MODEL, D_MODEL)
    rc, rsn, rsp = _rope_tables(positions)

    proj, h, rstd = _inproj(x2d, g_in, w_in_p)
    q, k, v, log_a, pre, cqn, ckvn, rq, rkv, misc = _mla_prep(
        proj, g_q, g_kv, w_uq_p, w_k_p, w_v, w_gate_p, b_gla_gate, rc, rsn, rsp)
    o_mla, lse = _mla_attn_fwd(q, k, v)
    o_gla, states = _gla_fwd(proj, log_a)

    (dx2, do_mla, do_gla, d_out, merged, um, ug, dym, dyg, loss_p, dg_final,
     dg_gla) = _post(o_mla, proj, o_gla, x2d, tgt, g_gla, g_final2, w_pm, w_pg, w_o)

    everyone = tuple(range(N_DEV))
    p_pm = _matmul("dw_proj_mla", um, dym, 512, 512, BF16).reshape(
        MLA_WIDTH, N_DEV, D_MODEL // N_DEV).transpose(1, 0, 2)
    p_pg = _matmul("dw_proj_gla", ug, dyg, 512, 512, BF16).reshape(N_DEV, -1, D_MODEL)
    p_o = _matmul("dw_out", merged, dx2, 512, 512, BF16).reshape(N_DEV, -1, D_MODEL)
    owns = [lax.empty((SHARD_COLS, D_MODEL), BF16)] + [lax.empty(p.shape[1:], BF16)
                                                       for p in (p_pm, p_pg, p_o)]
    lands = [lax.empty((3,) + o.shape, BF16) for o in owns]
    dw_groups, started = {}, []

    def reduce_scatter_stage(s, dests, extra=()):
        assert set(g for d in dests for g in _shard_groups(d)) <= set(dw_groups)
        k = 1 + len(extra)
        parts = [_grads_to_shards("grads_to_shards_%d" % s, dw_groups, dests)] + list(extra)
        all_dests = [dests] + [everyone] * len(extra)
        sums, owns[:k] = _swap_add("swap_add_%d" % s, parts, owns[:k], all_dests)
        sems, sums, lands[:k], token = _ici_start("ici_start_%d" % s, sums, lands[:k], all_dests)
        started.extend((sems[a][0], sems[a][1], sums[a], a, all_dests[a]) for a in range(k))
        return token

    dw_groups[1] = _matmul("dw_in_1", d_out, h, 512, 512, BF16)
    token = reduce_scatter_stage(1, (5, 6, 7), (p_pm, p_pg, p_o))
    d_gla, dla = _gla_bwd(proj, log_a, do_gla, states, token)
    dw_groups[0] = _matmul("dw_in_0", d_gla, h, 512, 512, BF16)
    token = reduce_scatter_stage(2, (1, 2, 3))
    dq, dk, dv = _mla_attn_bwd(q, k, v, o_mla, do_mla, lse, token)
    d_lat, dqpre, dpre, dg_q, dg_kv, db_gate = _mla_prep_bwd(
        dq, dk, dv, dla, pre, proj, rq, rkv, g_q, g_kv, w_uq_p, w_k_p, w_v, w_gate_p, rc, rsn, rsp)
    dw_groups[2] = _matmul("dw_in_2", d_lat, h, 896, 512, BF16)
    token = reduce_scatter_stage(3, (0, 4))
    grad_x, dg_in = _inproj_bwd((d_gla, d_out, d_lat), w_in_p, x2d, rstd, g_in, dx2, token)

    dw_uq = _matmul("dw_uq", cqn, dqpre, MLA_Q_RANK, 512, BF16)
    p_uq = dw_uq.reshape(MLA_Q_RANK, MLA_HEADS, HEAD_PAD)[:, :, :MLA_QK].transpose(1, 0, 2)
    dw_k = _matmul("dw_uk", ckvn, dk, MLA_KV_RANK, 512, BF16)
    dw_v = _matmul("dw_uv", ckvn, dv, MLA_KV_RANK, 512, BF16)
    p_ukv = jnp.concatenate(
        [dw_k.reshape(MLA_KV_RANK, MLA_HEADS, HEAD_PAD)[:, :, :MLA_NOPE],
         dw_v.reshape(MLA_KV_RANK, MLA_HEADS, MLA_VDIM)], axis=2).transpose(1, 0, 2)
    dw_gate = _matmul("dw_gate", misc, dpre, LANE, 512, BF16)
    p_gate = dw_gate[MISC_ALR:MISC_ALR + GLA_GATE_RANK].reshape(
        GLA_GATE_RANK, N_DEV, GLA_DK // N_DEV).transpose(1, 0, 2)
    small = jnp.concatenate([dg_in.reshape(-1), dg_q.reshape(-1), dg_kv.reshape(-1),
                             db_gate.reshape(-1), dg_gla.reshape(-1), dg_final.reshape(-1),
                             loss_p[0, :1]])
    small = jnp.pad(small, (0, SMALL_ROWS * LANE - small.shape[0])).reshape(SMALL_ROWS, LANE)

    recv = _exchange_grads([p_uq, p_ukv, p_gate], small)
    lands = _ici_wait("ici_wait", started, lands, recv[3])
    big = [_adamw_transposed("adamw_w_in", owns[0], lands[0], *sharded[0], 128)]
    big += _adamw_group([r[0] for r in recv[:3]] + list(owns[1:]),
                        [r[1:] for r in recv[:3]] + list(lands[1:]),
                        *[[s[j] for s in sharded[1:]] for j in range(3)])
    replicated = [(g_in, m_g_in, v_g_in), (g_q, m_g_q, v_g_q), (g_kv, m_g_kv, v_g_kv),
                  (b_gla_gate, m_b_gla_gate, v_b_gla_gate), (g_gla, m_g_gla, v_g_gla),
                  (g_final, m_g_final, v_g_final)]
    spacks = [jnp.pad(jnp.concatenate([s[j].reshape(-1) for s in replicated]),
                      (0, SMALL_ROWS * LANE - sum(SMALL_SIZES))).reshape(1, SMALL_ROWS, LANE)
              for j in range(3)]
    tiny = _adamw_rows("adamw_gains", recv[3], spacks[0], spacks[1], spacks[2], SMALL_ROWS)

    outs = {}
    names = ("w_in", "w_uq", "w_ukv", "w_gla_gate", "w_proj_mla", "w_proj_gla", "w_out")
    for j, kind in enumerate(("grad", "delta", "new_m", "new_v")):
        for name, res in zip(names, big):
            outs[kind, name] = res[j]
        flat = tiny[j].reshape(-1)
        off = 0
        for name, size in zip(("g_in", "g_q", "g_kv", "b_gla_gate", "g_gla", "g_final"), SMALL_SIZES):
            shape = (size,) if name == "g_final" else (1, size)
            outs[kind, name] = flat[off:off + size].reshape(shape)
            off += size
    loss = tiny[0].reshape(-1)[sum(SMALL_SIZES)]
    order = ("g_in", "w_in", "g_q", "w_uq", "g_kv", "w_ukv", "w_gla_gate", "b_gla_gate", "g_gla",
             "w_proj_mla", "w_proj_gla", "w_out", "g_final")
    result = [loss, grad_x.reshape(1, t, D_MODEL)]
    for kind in ("grad", "delta", "new_m", "new_v"):
        result += [outs[kind, name] for name in order]
    return tuple(result)
```

```python
import math

import jax
import jax.numpy as jnp
from jax import lax
from jax.experimental import pallas as pl
from jax.experimental.pallas import tpu as pltpu

F32 = jnp.float32
BF16 = jnp.bfloat16
MESH = pl.DeviceIdType.MESH
N_DEV = 8

D_MODEL = 1024
EPS = 1e-6
MLA_HEADS = 8
MLA_NOPE = 64
MLA_ROPE = 32
MLA_VDIM = 64
MLA_Q_RANK = 384
MLA_KV_RANK = 256
MLA_QK = MLA_NOPE + MLA_ROPE
MLA_WIDTH = MLA_HEADS * MLA_VDIM
ROPE_THETA = 10000.0
GLA_HEADS = 4
GLA_DK = 512
GLA_DV = 1024
GLA_HK = 128
GLA_HV = 256
GLA_GATE_RANK = 16
GLA_GATE_NORM = 16.0
GLA_CHUNK = 64
D_IN = 6320

ADAM_LR = 0.001
ADAM_B1 = 0.9
ADAM_B2 = 0.999
ADAM_EPS = 1e-08
ADAM_WD = 0.01
ADAM_STEP = 10

LANE = 128
HEAD_PAD = 128
VMEM_LIMIT = 48 * 1024 * 1024

P_VG, P_QG, P_KG = 0, 1024, 1536
P_ZGLA, P_GMLA, P_GGLA, P_ZMLA = 2048, 3072, 4096, 5120
P_CQ, P_CKV, P_MISC = 5632, 6144, 6400
P_TOTAL = 6528
P_GROUPS = ((0, 2048), (2048, 3584), (5632, 896))
MISC_KR = 64
MISC_ALR = 96
SHARD_COLS = D_IN // N_DEV
P_COMPONENTS = ((0, 384, P_CQ), (384, 256, P_CKV), (640, 32, P_MISC + MISC_KR), (672, 512, P_ZMLA),
                (1184, 512, P_QG), (1696, 512, P_KG), (2208, 1024, P_VG),
                (3232, 16, P_MISC + MISC_ALR), (3248, 1024, P_ZGLA), (4272, 1024, P_GMLA),
                (5296, 1024, P_GGLA))

SMALL_SIZES = (1024, 384, 256, 512, 256, 1024)
SMALL_ROWS = 32


def _segments():
    segs = []
    for g0, n, p0 in P_COMPONENTS:
        g = g0
        while g < g0 + n:
            d = g // SHARD_COLS
            end = min(g0 + n, (d + 1) * SHARD_COLS)
            segs.append((d, g - d * SHARD_COLS, end - g, p0 + g - g0))
            g = end
    return segs


def _cparams(sem=None):
    if sem is None:
        return pltpu.CompilerParams(vmem_limit_bytes=VMEM_LIMIT)
    return pltpu.CompilerParams(dimension_semantics=sem, vmem_limit_bytes=VMEM_LIMIT)


def _sigmoid(v):
    return 1.0 / (1.0 + jnp.exp(-v))


def _dot(a, b):
    return jnp.dot(a, b, preferred_element_type=F32)


def _dot_nt(a, b):
    return lax.dot_general(a, b, (((1,), (1,)), ((), ())), preferred_element_type=F32)


def _dot_tn(a, b):
    return lax.dot_general(a, b, (((0,), (0,)), ((), ())), preferred_element_type=F32)


def _dot_exact(a, b):
    return jnp.dot(a, b, preferred_element_type=F32, precision=lax.Precision.HIGHEST)


def _rope_fwd(blk, c, sn, sp):
    return blk * c + pltpu.roll(blk, LANE - 16, 1) * sn + pltpu.roll(blk, 16, 1) * sp


def _rope_bwd(blk, c, sn, sp):
    return blk * c + pltpu.roll(blk * sn, 16, 1) + pltpu.roll(blk * sp, LANE - 16, 1)


def _mesh_pos():
    return lax.axis_index("x"), lax.axis_index("y"), lax.axis_index("c")


def _hbm_specs(n):
    return [pl.BlockSpec(memory_space=pltpu.HBM) for _ in range(n)]


def _all_gather(shards):
    n = len(shards)

    def body(*refs):
        x_refs, out_refs = refs[:n], refs[n:2 * n]
        send_sems, recv_sems, local_sems = refs[2 * n:]
        x, y, c = _mesh_pos()
        me, sibling = (x, y, c), (x, y, 1 - c)
        chips = [(1 - x, y), (x, 1 - y), (1 - x, 1 - y)]

        def slot(a, px, py, pc):
            return out_refs[a].at[4 * px + 2 * py + pc]

        def copies(k, block, to, own=False):
            return [pltpu.make_async_remote_copy(
                src_ref=x_refs[a] if own else slot(a, *block), dst_ref=slot(a, *block),
                send_sem=send_sems.at[k, a], recv_sem=recv_sems.at[k, a],
                device_id=to, device_id_type=MESH) for a in range(n)]

        mine = [pltpu.make_async_copy(x_refs[a], slot(a, *me), local_sems.at[a]) for a in range(n)]
        for cp in mine:
            cp.start()
        first = copies(0, me, sibling, own=True)
        for j, chip in enumerate(chips):
            first += copies(1 + j, me, (*chip, c), own=True)
        for cp in first:
            cp.start()
        passed = []
        for j, chip in enumerate(chips):
            for cp in copies(1 + j, (*chip, c), me):
                cp.wait_recv()
            fwd = copies(4 + j, (*chip, c), sibling)
            for cp in fwd:
                cp.start()
            passed += fwd
        for cp in copies(0, sibling, me):
            cp.wait_recv()
        for j, chip in enumerate(chips):
            for cp in copies(4 + j, (*chip, 1 - c), me):
                cp.wait_recv()
        for cp in first + passed:
            cp.wait_send()
        for cp in mine:
            cp.wait()

    return pl.pallas_call(
        body, name="all_gather_weights",
        out_shape=tuple(jax.ShapeDtypeStruct((N_DEV,) + s.shape, s.dtype) for s in shards),
        in_specs=_hbm_specs(n), out_specs=tuple(_hbm_specs(n)),
        scratch_shapes=[pltpu.SemaphoreType.DMA((7, n)), pltpu.SemaphoreType.DMA((7, n)),
                        pltpu.SemaphoreType.DMA((n,))],
    )(*shards)


def _exchange_grads(parts, small):
    n = len(parts)
    rows_per_add = 256

    def body(*refs):
        p_refs, s_ref = refs[:n], refs[n]
        out_refs, sout_ref = refs[n + 1:2 * n + 1], refs[2 * n + 1]
        mine_v, recv_v = refs[2 * n + 2:3 * n + 2], refs[3 * n + 2:4 * n + 2]
        (d2d_send, d2d_recv, ici_send, ici_recv, s_send, s_recv, load_sems, own_sems,
         sown_sem) = refs[4 * n + 2:]
        x, y, c = _mesh_pos()
        me = 4 * x + 2 * y + c
        chips = [(x, y), (1 - x, y), (x, 1 - y), (1 - x, 1 - y)]

        sown = pltpu.make_async_copy(s_ref, sout_ref.at[me], sown_sem)
        sown.start()
        tiny = []
        for k in range(1, N_DEV):
            peer = (x ^ (k >> 2), y ^ ((k >> 1) & 1), c ^ (k & 1))
            tiny.append(pltpu.make_async_remote_copy(
                src_ref=s_ref, dst_ref=sout_ref.at[me], send_sem=s_send.at[k - 1],
                recv_sem=s_recv.at[k - 1], device_id=peer, device_id_type=MESH))
        for cp in tiny:
            cp.start()

        loads, swaps = [], []
        for a in range(n):
            for j, (px, py) in enumerate(chips):
                loads.append(pltpu.make_async_copy(
                    p_refs[a].at[4 * px + 2 * py + c], mine_v[a].at[j], load_sems.at[a, j]))
                swaps.append(pltpu.make_async_remote_copy(
                    src_ref=p_refs[a].at[4 * px + 2 * py + 1 - c], dst_ref=recv_v[a].at[j],
                    send_sem=d2d_send.at[a, j], recv_sem=d2d_recv.at[a, j],
                    device_id=(x, y, 1 - c), device_id_type=MESH))
        for cp in swaps + loads:
            cp.start()

        outgoing, own = [], []
        for a in range(n):
            rows = p_refs[a].shape[1]
            step = math.gcd(rows, rows_per_add)
            for j in range(4):
                loads[4 * a + j].wait()
                swaps[4 * a + j].wait_recv()

                @pl.loop(0, rows // step)
                def _(i):
                    rs = pl.ds(pl.multiple_of(i * step, step), step)
                    mine_v[a][j, rs, :] = (mine_v[a][j, rs, :].astype(F32)
                                           + recv_v[a][j, rs, :].astype(F32)).astype(BF16)

                if j == 0:
                    own.append(pltpu.make_async_copy(mine_v[a].at[0], out_refs[a].at[0],
                                                     own_sems.at[a]))
                    own[-1].start()
                else:
                    outgoing.append(pltpu.make_async_remote_copy(
                        src_ref=mine_v[a].at[j], dst_ref=out_refs[a].at[j],
                        send_sem=ici_send.at[a, j - 1], recv_sem=ici_recv.at[a, j - 1],
                        device_id=(*chips[j], c), device_id_type=MESH))
                    outgoing[-1].start()

        for cp in tiny + outgoing:
            cp.wait_recv()
        for cp in tiny + outgoing + swaps:
            cp.wait_send()
        for cp in own:
            cp.wait()
        sown.wait()

    outs = [jax.ShapeDtypeStruct((4,) + p.shape[1:], p.dtype) for p in parts]
    outs.append(jax.ShapeDtypeStruct((N_DEV,) + small.shape, small.dtype))
    stage = [pltpu.VMEM((4,) + p.shape[1:], p.dtype) for p in parts]
    return pl.pallas_call(
        body, name="exchange_grads",
        out_shape=tuple(outs),
        in_specs=_hbm_specs(n + 1), out_specs=tuple(_hbm_specs(n + 1)),
        scratch_shapes=stage + stage + [
            pltpu.SemaphoreType.DMA((n, 4)), pltpu.SemaphoreType.DMA((n, 4)),
            pltpu.SemaphoreType.DMA((n, 3)), pltpu.SemaphoreType.DMA((n, 3)),
            pltpu.SemaphoreType.DMA((7,)), pltpu.SemaphoreType.DMA((7,)),
            pltpu.SemaphoreType.DMA((n, 4)), pltpu.SemaphoreType.DMA((n,)),
            pltpu.SemaphoreType.DMA],
        compiler_params=_cparams(),
    )(*parts, small)


def _dev(d):
    return d >> 2, (d >> 1) & 1, d & 1


def _swap_add(name, parts, owns_prev, dests):
    na = len(parts)
    step = 256

    def body(*refs):
        p_refs, h_refs, own_refs = refs[:na], refs[2 * na:3 * na], refs[3 * na:4 * na]
        mine_vs, recv_vs = refs[4 * na:5 * na], refs[5 * na:6 * na]
        send_sems, recv_sems, load_sems, store_sems, own_sems = refs[6 * na:]
        x, y, c = _mesh_pos()
        work = []
        for a in range(na):
            for i, d in enumerate(dests[a]):
                dx, dy, dc = _dev(d)
                keep = c == dc
                mine = jnp.logical_and(keep, jnp.logical_and(x == dx, y == dy))
                swap = pltpu.make_async_remote_copy(
                    src_ref=p_refs[a].at[i], dst_ref=recv_vs[a].at[i], send_sem=send_sems.at[a, i],
                    recv_sem=recv_sems.at[a, i], device_id=(x, y, 1 - c), device_id_type=MESH)
                load = pltpu.make_async_copy(p_refs[a].at[i], mine_vs[a].at[i], load_sems.at[a, i])
                store = pltpu.make_async_copy(mine_vs[a].at[i], h_refs[a].at[i], store_sems.at[a, i])
                own = pltpu.make_async_copy(mine_vs[a].at[i], own_refs[a], own_sems.at[a])
                pl.when(keep)(load.start)
                pl.when(jnp.logical_not(keep))(swap.start)
                work.append((a, i, keep, mine, swap, load, store, own))
        for a, i, keep, mine, swap, load, store, own in work:
            cols = parts[a].shape[2]
            cstep = math.gcd(cols, step)

            @pl.when(keep)
            def _(a=a, i=i, mine=mine, swap=swap, load=load, store=store, own=own, cols=cols,
                  cstep=cstep):
                load.wait()
                swap.wait_recv()

                @pl.loop(0, cols // cstep)
                def _(r):
                    cs = pl.ds(pl.multiple_of(r * cstep, cstep), cstep)
                    mine_vs[a][i, :, cs] = (mine_vs[a][i, :, cs].astype(F32)
                                            + recv_vs[a][i, :, cs].astype(F32)).astype(BF16)

                store.start()
                pl.when(mine)(own.start)
        for a, i, keep, mine, swap, load, store, own in work:
            pl.when(jnp.logical_not(keep))(swap.wait_send)
            pl.when(keep)(store.wait)
            pl.when(mine)(own.wait)

    stage = [pltpu.VMEM(p.shape, BF16) for p in parts]
    most = max(len(d) for d in dests)
    res = pl.pallas_call(
        body, name=name,
        out_shape=tuple([jax.ShapeDtypeStruct(p.shape, BF16) for p in parts]
                        + [jax.ShapeDtypeStruct(o.shape, BF16) for o in owns_prev]),
        in_specs=_hbm_specs(2 * na), out_specs=tuple(_hbm_specs(2 * na)),
        input_output_aliases={na + a: na + a for a in range(na)},
        scratch_shapes=stage + stage + [pltpu.SemaphoreType.DMA((na, most)) for _ in range(4)]
        + [pltpu.SemaphoreType.DMA((na,))],
        compiler_params=_cparams(),
    )(*parts, *owns_prev)
    return res[:na], res[na:]


def _ici_copies(h_ref, land_ref, send_sems, recv_sems, dests):
    x, y, c = _mesh_pos()
    sends, arrivals = [], []
    for i, d in enumerate(dests):
        dx, dy, dc = _dev(d)
        j = (x != dx).astype(jnp.int32) + 2 * (y != dy).astype(jnp.int32)
        slot = jnp.maximum(j - 1, 0)
        sends.append((jnp.logical_and(c == dc, j > 0), pltpu.make_async_remote_copy(
            src_ref=h_ref.at[i], dst_ref=land_ref.at[slot], send_sem=send_sems.at[i],
            recv_sem=recv_sems.at[slot], device_id=(dx, dy, dc), device_id_type=MESH)))
        arrivals.append((jnp.logical_and(c == dc, j == 0), [pltpu.make_async_remote_copy(
            src_ref=h_ref.at[i], dst_ref=land_ref.at[r], send_sem=send_sems.at[i],
            recv_sem=recv_sems.at[r], device_id=(dx, dy, dc), device_id_type=MESH)
            for r in range(3)]))
    return sends, arrivals


def _ici_start(name, hs, lands, dests):
    na = len(hs)

    def body(*refs):
        h_refs, land_refs, sems = refs[:na], refs[na:2 * na], refs[2 * na:4 * na]
        token = refs[-1]
        for a in range(na):
            sends, _ = _ici_copies(h_refs[a], land_refs[a], sems[2 * a], sems[2 * a + 1], dests[a])
            for go, cp in sends:
                pl.when(go)(cp.start)
        token[...] = jnp.zeros_like(token)

    hbm, sem = pl.BlockSpec(memory_space=pltpu.HBM), pl.BlockSpec(memory_space=pltpu.SEMAPHORE)
    sem_shapes = []
    for a in range(na):
        sem_shapes += [pltpu.SemaphoreType.DMA((len(dests[a]),)), pltpu.SemaphoreType.DMA((3,))]
    res = pl.pallas_call(
        body, name=name,
        out_shape=tuple(sem_shapes) + tuple(pltpu.HBM(v.shape, v.dtype) for v in list(hs) + list(lands))
        + (jax.ShapeDtypeStruct((8, LANE), F32),),
        in_specs=(hbm,) * (2 * na),
        out_specs=(sem,) * (2 * na) + (hbm,) * (2 * na) + (pl.BlockSpec(memory_space=pltpu.VMEM),),
        input_output_aliases={i: 2 * na + i for i in range(2 * na)},
        compiler_params=pltpu.CompilerParams(
            has_side_effects=pltpu.SideEffectType.DATAFLOW_SIDE_EFFECTING),
    )(*[pltpu.with_memory_space_constraint(v, pltpu.HBM) for v in list(hs) + list(lands)])
    sems = [(res[2 * a], res[2 * a + 1]) for a in range(na)]
    return sems, res[2 * na:3 * na], res[3 * na:4 * na], res[-1]


def _ici_wait(name, started, lands, after):
    k, nl = len(started), len(lands)

    def body(*refs):
        land_refs = refs[3 * k:3 * k + nl]
        for s in range(k):
            h_ref, send_sems, recv_sems = refs[3 * s:3 * s + 3]
            sends, arrivals = _ici_copies(h_ref, land_refs[started[s][3]], send_sems, recv_sems,
                                          started[s][4])
            for go, cp in sends:
                pl.when(go)(cp.wait_send)
            for here, cps in arrivals:
                for cp in cps:
                    pl.when(here)(cp.wait_recv)

    hbm, sem = pl.BlockSpec(memory_space=pltpu.HBM), pl.BlockSpec(memory_space=pltpu.SEMAPHORE)
    operands, specs = [], []
    for send_sems, recv_sems, h, _, _ in started:
        operands += [h, send_sems, recv_sems]
        specs += [hbm, sem, sem]
    return pl.pallas_call(
        body, name=name,
        out_shape=tuple(pltpu.HBM(v.shape, v.dtype) for v in lands),
        in_specs=tuple(specs) + (hbm,) * nl + (pl.BlockSpec(memory_space=pl.ANY),),
        out_specs=(hbm,) * nl,
        input_output_aliases={3 * k + i: i for i in range(nl)},
        compiler_params=pltpu.CompilerParams(
            has_side_effects=pltpu.SideEffectType.DATAFLOW_SIDE_EFFECTING),
    )(*operands, *lands, after)


def _weights_to_p(gathered):
    tl = 256
    segs = sorted(_segments(), key=lambda s: s[3])

    def body(g_ref, o_ref):
        pieces, pos = [], 0
        for d, c0, n, p0 in segs:
            if p0 > pos:
                pieces.append(jnp.zeros((p0 - pos, tl), F32))
            pieces.append(g_ref[d, c0:c0 + n, :].astype(F32))
            pos = p0 + n
        pieces.append(jnp.zeros((P_TOTAL - pos, tl), F32))
        o_ref[...] = jnp.concatenate(pieces, axis=0).astype(BF16)

    return pl.pallas_call(
        body, name="weights_to_layout",
        grid=(D_MODEL // tl,),
        in_specs=[pl.BlockSpec((N_DEV, SHARD_COLS, tl), lambda i: (0, 0, i))],
        out_specs=pl.BlockSpec((P_TOTAL, tl), lambda i: (0, i)),
        out_shape=jax.ShapeDtypeStruct((P_TOTAL, D_MODEL), BF16),
        compiler_params=_cparams(("arbitrary",)),
    )(gathered)


def _group_of(p0):
    return max(i for i, (off, _) in enumerate(P_GROUPS) if off <= p0)


def _shard_groups(d):
    return sorted({_group_of(s[3]) for s in _segments() if s[0] == d})


def _grads_to_shards(name, groups, dests):
    tl = 256
    segs = _segments()
    used = sorted(groups)

    def body(*refs):
        g_refs, o_ref = dict(zip(used, refs[:-1])), refs[-1]
        for i, d in enumerate(dests):
            pieces = []
            for _, c0, n, p0 in sorted([s for s in segs if s[0] == d], key=lambda s: s[1]):
                gi = _group_of(p0)
                lo = p0 - P_GROUPS[gi][0]
                pieces.append(g_refs[gi][lo:lo + n, :].astype(F32))
            o_ref[i] = jnp.concatenate(pieces, axis=0).astype(BF16)

    return pl.pallas_call(
        body, name=name,
        grid=(D_MODEL // tl,),
        in_specs=[pl.BlockSpec((P_GROUPS[g][1], tl), lambda i: (0, i)) for g in used],
        out_specs=pl.BlockSpec((len(dests), SHARD_COLS, tl), lambda i: (0, 0, i)),
        out_shape=jax.ShapeDtypeStruct((len(dests), SHARD_COLS, D_MODEL), BF16),
        compiler_params=_cparams(("arbitrary",)),
    )(*[groups[g] for g in used])


def _inproj(x, g_in, w_pt):
    t = x.shape[0]
    tm = min(256, t)
    nj = 3
    tn = P_TOTAL // nj

    def body(x_ref, g_ref, w_ref, proj_ref, h_ref, r_ref):
        xf = x_ref[...]
        r = lax.rsqrt(jnp.mean(xf * xf, axis=-1, keepdims=True) + EPS)
        h = ((xf * r) * g_ref[...]).astype(BF16)
        proj_ref[...] = _dot_nt(h, w_ref[...])

        @pl.when(pl.program_id(0) == 0)
        def _():
            h_ref[...] = h
            r_ref[...] = r

    first = lambda j, i: (jnp.where(j == 0, i, t // tm - 1), 0)
    return pl.pallas_call(
        body, name="inproj",
        grid=(nj, t // tm),
        in_specs=[pl.BlockSpec((tm, D_MODEL), lambda j, i: (i, 0)),
                  pl.BlockSpec((1, D_MODEL), lambda j, i: (0, 0)),
                  pl.BlockSpec((tn, D_MODEL), lambda j, i: (j, 0))],
        out_specs=(pl.BlockSpec((tm, tn), lambda j, i: (i, j)),
                   pl.BlockSpec((tm, D_MODEL), first),
                   pl.BlockSpec((tm, 1), first)),
        out_shape=(jax.ShapeDtypeStruct((t, P_TOTAL), F32),
                   jax.ShapeDtypeStruct((t, D_MODEL), BF16),
                   jax.ShapeDtypeStruct((t, 1), F32)),
        compiler_params=_cparams(("arbitrary", "arbitrary")),
    )(x, g_in, w_pt)


def _mla_prep(proj, g_q, g_kv, w_uq_p, w_k_p, w_v, w_gate_p, b_gate, rc, rsn, rsp):
    t = proj.shape[0]
    tm = min(256, t)
    hq = MLA_HEADS * HEAD_PAD

    def body(cq_ref, ckv_ref, misc_ref, gq_ref, gkv_ref, wuq_ref, wk_ref, wv_ref, wg_ref, bg_ref,
             c_ref, sn_ref, sp_ref,
             q_ref, k_ref, v_ref, la_ref, pre_ref, cqn_ref, ckvn_ref, rq_ref, rkv_ref, mb_ref):
        c, sn, sp = c_ref[...], sn_ref[...], sp_ref[...]
        cq = cq_ref[:, :MLA_Q_RANK]
        rq = lax.rsqrt(jnp.mean(cq * cq, axis=-1, keepdims=True) + EPS)
        cqn = ((cq * rq) * gq_ref[...]).astype(BF16)
        cqn_ref[...] = cqn
        rq_ref[...] = rq
        qpre = _dot(cqn, wuq_ref[...])
        ckv = ckv_ref[...]
        rkv = lax.rsqrt(jnp.mean(ckv * ckv, axis=-1, keepdims=True) + EPS)
        ckvn = ((ckv * rkv) * gkv_ref[...]).astype(BF16)
        ckvn_ref[...] = ckvn
        rkv_ref[...] = rkv
        kn = _dot(ckvn, wk_ref[...])
        v_ref[...] = _dot(ckvn, wv_ref[...]).astype(BF16)
        misc = misc_ref[...]
        krope = _rope_fwd(misc, c, sn, sp)
        for h in range(MLA_HEADS):
            sl = slice(h * HEAD_PAD, (h + 1) * HEAD_PAD)
            q_ref[:, sl] = _rope_fwd(qpre[:, sl], c, sn, sp).astype(BF16)
            k_ref[:, sl] = (kn[:, sl] + krope).astype(BF16)
        mb_ref[...] = misc.astype(BF16)
        pre = _dot(mb_ref[...], wg_ref[...]) + bg_ref[...]
        pre_ref[...] = pre
        log_a = (jnp.minimum(pre, 0.0) - jnp.log(1.0 + jnp.exp(-jnp.abs(pre)))) / GLA_GATE_NORM
        la_ref[...] = _dot_exact(_chunk_tri(tm, True), log_a)

    row = lambda w: pl.BlockSpec((tm, w), lambda i: (i, 0))
    full = lambda a: pl.BlockSpec(a.shape, lambda i: (0, 0))
    return pl.pallas_call(
        body, name="mla_prep",
        grid=(t // tm,),
        in_specs=[pl.BlockSpec((tm, 512), lambda i: (i, P_CQ // 512)),
                  pl.BlockSpec((tm, MLA_KV_RANK), lambda i: (i, P_CKV // MLA_KV_RANK)),
                  pl.BlockSpec((tm, LANE), lambda i: (i, P_MISC // LANE)),
                  full(g_q), full(g_kv), full(w_uq_p), full(w_k_p), full(w_v), full(w_gate_p),
                  full(b_gate), row(LANE), row(LANE), row(LANE)],
        out_specs=(row(hq), row(hq), row(MLA_WIDTH), row(GLA_DK), row(GLA_DK),
                   row(MLA_Q_RANK), row(MLA_KV_RANK), row(1), row(1), row(LANE)),
        out_shape=(jax.ShapeDtypeStruct((t, hq), BF16), jax.ShapeDtypeStruct((t, hq), BF16),
                   jax.ShapeDtypeStruct((t, MLA_WIDTH), BF16),
                   jax.ShapeDtypeStruct((t, GLA_DK), F32), jax.ShapeDtypeStruct((t, GLA_DK), F32),
                   jax.ShapeDtypeStruct((t, MLA_Q_RANK), BF16),
                   jax.ShapeDtypeStruct((t, MLA_KV_RANK), BF16),
                   jax.ShapeDtypeStruct((t, 1), F32), jax.ShapeDtypeStruct((t, 1), F32),
                   jax.ShapeDtypeStruct((t, LANE), BF16)),
        compiler_params=_cparams(("arbitrary",)),
    )(proj, proj, proj, g_q, g_kv, w_uq_p, w_k_p, w_v, w_gate_p, b_gate, rc, rsn, rsp)


def _attn_masks(tq, i):
    keys = (i + 1) * tq
    rows = i * tq + lax.broadcasted_iota(jnp.int32, (tq, keys), 0)
    cols = lax.broadcasted_iota(jnp.int32, (tq, keys), 1)
    lane = lax.broadcasted_iota(jnp.int32, (tq, LANE), 1)
    return cols <= rows, lane < MLA_VDIM


def _for_each_query_tile(n_tiles, fn):
    for i in range(n_tiles):
        pl.when(pl.program_id(1) == i)(lambda i=i: fn(i))


def _mla_attn_fwd(q, k, v):
    t = q.shape[0]
    tq = min(256, t)
    scale = MLA_QK ** -0.5

    def body(q_ref, k_ref, v_ref, o_ref, lse_ref):
        def tile(i):
            keys = (i + 1) * tq
            causal, low = _attn_masks(tq, i)
            vp = v_ref[0:keys, :]
            acc = jnp.zeros((tq, LANE), F32)
            for hh in range(2):
                sl = slice(hh * HEAD_PAD, (hh + 1) * HEAD_PAD)
                s = _dot_nt(q_ref[:, sl], k_ref[0:keys, sl]) * scale
                s = jnp.where(causal, s, -jnp.inf)
                m = jnp.max(s, axis=-1, keepdims=True)
                e = jnp.exp(s - m)
                l = jnp.sum(e, axis=-1, keepdims=True)
                o = _dot(e.astype(BF16), vp) / l
                acc = jnp.where(low if hh == 0 else jnp.logical_not(low), o, acc)
                lse_ref[hh] = m + jnp.log(l)
            o_ref[...] = acc

        _for_each_query_tile(t // tq, tile)

    return pl.pallas_call(
        body, name="mla_attn_fwd",
        grid=(MLA_HEADS // 2, t // tq),
        in_specs=[pl.BlockSpec((tq, 2 * HEAD_PAD), lambda p, i: (i, p)),
                  pl.BlockSpec((t, 2 * HEAD_PAD), lambda p, i: (0, p)),
                  pl.BlockSpec((t, LANE), lambda p, i: (0, p))],
        out_specs=(pl.BlockSpec((tq, LANE), lambda p, i: (i, p)),
                   pl.BlockSpec((2, tq, 1), lambda p, i: (p, i, 0))),
        out_shape=(jax.ShapeDtypeStruct((t, MLA_WIDTH), F32),
                   jax.ShapeDtypeStruct((MLA_HEADS, t, 1), F32)),
        compiler_params=_cparams(("arbitrary", "arbitrary")),
    )(q, k, v)


def _mla_attn_bwd(q, k, v, o, do, lse, after):
    t = q.shape[0]
    tq = min(256, t)
    scale = MLA_QK ** -0.5

    def body(q_ref, k_ref, v_ref, o_ref, do_ref, lse_ref, after_ref, dq_ref, dk_ref, dv_ref):
        del after_ref

        @pl.when(pl.program_id(1) == 0)
        def _():
            dk_ref[...] = jnp.zeros_like(dk_ref)
            dv_ref[...] = jnp.zeros_like(dv_ref)

        def tile(i):
            keys = (i + 1) * tq
            causal, low = _attn_masks(tq, i)
            vp = v_ref[0:keys, :]
            do_all = do_ref[...]
            o_all = o_ref[...]
            dv_acc = jnp.zeros((keys, LANE), F32)
            for hh in range(2):
                sl = slice(hh * HEAD_PAD, (hh + 1) * HEAD_PAD)
                do_h = jnp.where(low if hh == 0 else jnp.logical_not(low), do_all, 0.0)
                dsum = jnp.sum(do_h * o_all, axis=-1, keepdims=True)
                qh = q_ref[:, sl]
                kh = k_ref[0:keys, sl]
                s = _dot_nt(qh, kh) * scale
                p = jnp.where(causal, jnp.exp(s - lse_ref[hh]), 0.0)
                do_b = do_h.astype(BF16)
                dp = _dot_nt(do_b, vp)
                ds = (p * (dp - dsum) * scale).astype(BF16)
                dq_ref[:, sl] = _dot(ds, kh).astype(BF16)
                dk_ref[0:keys, sl] += _dot_tn(ds, qh)
                dv_acc = dv_acc + _dot_tn(p.astype(BF16), do_b)
            dv_ref[0:keys, :] += dv_acc

        _for_each_query_tile(t // tq, tile)

    return pl.pallas_call(
        body, name="mla_attn_bwd",
        grid=(MLA_HEADS // 2, t // tq),
        in_specs=[pl.BlockSpec((tq, 2 * HEAD_PAD), lambda p, i: (i, p)),
                  pl.BlockSpec((t, 2 * HEAD_PAD), lambda p, i: (0, p)),
                  pl.BlockSpec((t, LANE), lambda p, i: (0, p)),
                  pl.BlockSpec((tq, LANE), lambda p, i: (i, p)),
                  pl.BlockSpec((tq, LANE), lambda p, i: (i, p)),
                  pl.BlockSpec((2, tq, 1), lambda p, i: (p, i, 0)),
                  pl.BlockSpec(memory_space=pl.ANY)],
        out_specs=(pl.BlockSpec((tq, 2 * HEAD_PAD), lambda p, i: (i, p)),
                   pl.BlockSpec((t, 2 * HEAD_PAD), lambda p, i: (0, p)),
                   pl.BlockSpec((t, LANE), lambda p, i: (0, p))),
        out_shape=(jax.ShapeDtypeStruct((t, MLA_HEADS * HEAD_PAD), BF16),
                   jax.ShapeDtypeStruct((t, MLA_HEADS * HEAD_PAD), F32),
                   jax.ShapeDtypeStruct((t, MLA_WIDTH), F32)),
        compiler_params=_cparams(("arbitrary", "arbitrary")),
    )(q, k, v, o, do, lse, after)


def _chunk_tri(n, lower):
    r = lax.broadcasted_iota(jnp.int32, (n, n), 0)
    c = lax.broadcasted_iota(jnp.int32, (n, n), 1)
    same = (r // GLA_CHUNK) == (c // GLA_CHUNK)
    return jnp.where(jnp.logical_and(same, r >= c if lower else r <= c), 1.0, 0.0).astype(F32)


def _gla_chunk_terms(q_ref, k_ref, b_ref, h):
    sl = slice(h * GLA_HK, (h + 1) * GLA_HK)
    b = b_ref[:, sl]
    bl = b[GLA_CHUNK - 1:GLA_CHUNK, :]
    kc = k_ref[:, sl]
    q_in = (q_ref[:, sl] * (GLA_HK ** -0.5)) * jnp.exp(b)
    k_in = kc * jnp.exp(-b)
    k_st = kc * jnp.exp(bl - b)
    return b, bl, q_in, k_in, k_st


def _tri(c, lower):
    r = lax.broadcasted_iota(jnp.int32, (c, c), 0)
    cc = lax.broadcasted_iota(jnp.int32, (c, c), 1)
    return jnp.where(r >= cc if lower else r <= cc, 1.0, 0.0).astype(F32)


def _gla_fwd(proj, log_a):
    t = proj.shape[0]
    c = GLA_CHUNK
    n = t // c

    def body(q_ref, k_ref, v_ref, la_ref, o_ref, sp_ref, st_ref):
        @pl.when(pl.program_id(0) == 0)
        def _():
            st_ref[...] = jnp.zeros_like(st_ref)

        tri = _tri(c, True)
        for h in range(GLA_HEADS):
            _, bl, q_in, k_in, k_st = _gla_chunk_terms(q_ref, k_ref, la_ref, h)
            vs = slice(h * GLA_HV, (h + 1) * GLA_HV)
            vv = v_ref[:, vs].astype(BF16)
            qb = q_in.astype(BF16)
            attn = _dot_nt(qb, k_in.astype(BF16)) * tri
            st = st_ref[h]
            sp_ref[0, h] = st
            o_ref[:, vs] = _dot(attn.astype(BF16), vv) + _dot_nt(qb, st.astype(BF16))
            st_ref[h] = st * jnp.exp(bl) + _dot_tn(vv, k_st.astype(BF16))

    return pl.pallas_call(
        body, name="gla_fwd",
        grid=(n,),
        in_specs=[pl.BlockSpec((c, GLA_DK), lambda i: (i, P_QG // GLA_DK)),
                  pl.BlockSpec((c, GLA_DK), lambda i: (i, P_KG // GLA_DK)),
                  pl.BlockSpec((c, GLA_DV), lambda i: (i, P_VG // GLA_DV)),
                  pl.BlockSpec((c, GLA_DK), lambda i: (i, 0))],
        out_specs=(pl.BlockSpec((c, GLA_DV), lambda i: (i, 0)),
                   pl.BlockSpec((1, GLA_HEADS, GLA_HV, GLA_HK), lambda i: (i, 0, 0, 0))),
        out_shape=(jax.ShapeDtypeStruct((t, GLA_DV), F32),
                   jax.ShapeDtypeStruct((n, GLA_HEADS, GLA_HV, GLA_HK), F32)),
        scratch_shapes=[pltpu.VMEM((GLA_HEADS, GLA_HV, GLA_HK), F32)],
        compiler_params=_cparams(("arbitrary",)),
    )(proj, proj, proj, log_a)


def _gla_bwd(proj, log_a, do, states, after):
    t = proj.shape[0]
    c = GLA_CHUNK
    n = t // c

    def body(q_ref, k_ref, v_ref, la_ref, do_ref, sp_ref, after_ref, dg_ref, dla_ref, ds_ref):
        del after_ref

        @pl.when(pl.program_id(0) == 0)
        def _():
            ds_ref[...] = jnp.zeros_like(ds_ref)

        tri = _tri(c, True)
        last = lax.broadcasted_iota(jnp.int32, (c, GLA_HK), 0) == c - 1
        for h in range(GLA_HEADS):
            b, bl, q_in, k_in, k_st = _gla_chunk_terms(q_ref, k_ref, la_ref, h)
            ks_ = slice(h * GLA_HK, (h + 1) * GLA_HK)
            vs = slice(h * GLA_HV, (h + 1) * GLA_HV)
            vv = v_ref[:, vs].astype(BF16)
            do_h = do_ref[:, vs]
            qb, kb, ksb = q_in.astype(BF16), k_in.astype(BF16), k_st.astype(BF16)
            attn = (_dot_nt(qb, kb) * tri).astype(BF16)
            st = sp_ref[0, h]
            dst = ds_ref[h]
            dstb = dst.astype(BF16)
            dattn = (_dot_nt(do_h, vv) * tri).astype(BF16)
            dg_ref[:, P_VG + h * GLA_HV:P_VG + (h + 1) * GLA_HV] = (
                _dot_tn(attn, do_h) + _dot_nt(ksb, dstb)).astype(BF16)
            dq_in = _dot(dattn, kb) + _dot(do_h, st.astype(BF16))
            dk_in = _dot_tn(dattn, qb)
            dk_st = _dot(vv, dstb)
            ebl = jnp.exp(bl)
            d_ebl = jnp.sum(st * dst, axis=0, keepdims=True)
            ds_ref[h] = _dot_tn(do_h, qb) + dst * ebl
            dg_ref[:, P_QG + h * GLA_HK:P_QG + (h + 1) * GLA_HK] = (
                dq_in * (GLA_HK ** -0.5) * jnp.exp(b)).astype(BF16)
            dg_ref[:, P_KG + h * GLA_HK:P_KG + (h + 1) * GLA_HK] = (
                dk_in * jnp.exp(-b) + dk_st * jnp.exp(bl - b)).astype(BF16)
            db = dq_in * q_in - dk_in * k_in - dk_st * k_st
            dbl = jnp.sum(dk_st * k_st, axis=0, keepdims=True) + d_ebl * ebl
            dla_ref[:, ks_] = db + jnp.where(last, dbl, 0.0)

    rev = lambda i: n - 1 - i
    gw = P_GROUPS[0][1]
    return pl.pallas_call(
        body, name="gla_bwd",
        grid=(n,),
        in_specs=[pl.BlockSpec((c, GLA_DK), lambda i: (rev(i), P_QG // GLA_DK)),
                  pl.BlockSpec((c, GLA_DK), lambda i: (rev(i), P_KG // GLA_DK)),
                  pl.BlockSpec((c, GLA_DV), lambda i: (rev(i), P_VG // GLA_DV)),
                  pl.BlockSpec((c, GLA_DK), lambda i: (rev(i), 0)),
                  pl.BlockSpec((c, GLA_DV), lambda i: (rev(i), 0)),
                  pl.BlockSpec((1, GLA_HEADS, GLA_HV, GLA_HK), lambda i: (rev(i), 0, 0, 0)),
                  pl.BlockSpec(memory_space=pl.ANY)],
        out_specs=(pl.BlockSpec((c, gw), lambda i: (rev(i), 0)),
                   pl.BlockSpec((c, GLA_DK), lambda i: (rev(i), 0))),
        out_shape=(jax.ShapeDtypeStruct((t, gw), BF16), jax.ShapeDtypeStruct((t, GLA_DK), F32)),
        scratch_shapes=[pltpu.VMEM((GLA_HEADS, GLA_HV, GLA_HK), F32)],
        compiler_params=_cparams(("arbitrary",)),
    )(proj, proj, proj, log_a, do, states, after)


def _post(o_mla, proj, o_gla, x, target, g_gla, g_final, w_pm, w_pg, w_o):
    t = x.shape[0]
    tm = min(128, t)
    g0, gw = P_GROUPS[1]

    def body(om_ref, zg_ref, gm_ref, gg_ref, zm_ref, og_ref, x_ref, tg_ref, ggla_ref, gf_ref,
             wpm_ref, wpg_ref, wo_ref,
             dx2_ref, dom_ref, dog_ref, dg_ref,
             mg_ref, um_ref, ug_ref, dym_ref, dyg_ref, loss_ref, dgf_ref, dggla_ref):
        @pl.when(pl.program_id(0) == 0)
        def _():
            loss_ref[...] = jnp.zeros_like(loss_ref)
            dgf_ref[...] = jnp.zeros_like(dgf_ref)
            dggla_ref[...] = jnp.zeros_like(dggla_ref)

        om = om_ref[...]
        zm = zm_ref[...]
        sm = _sigmoid(zm)
        silu_m = zm * sm
        um = (om * silu_m).astype(BF16)
        um_ref[...] = um
        ym = _dot(um, wpm_ref[...])

        ggla = ggla_ref[...]
        zg = zg_ref[...]
        sg = _sigmoid(zg)
        silu_g = zg * sg
        xhat, rstd, on = [], [], []
        for h in range(GLA_HEADS):
            blk = og_ref[:, h * GLA_HV:(h + 1) * GLA_HV]
            r = lax.rsqrt(jnp.mean(blk * blk, axis=-1, keepdims=True) + EPS)
            xhat.append(blk * r)
            rstd.append(r)
            on.append(xhat[h] * ggla)
        on = jnp.concatenate(on, axis=-1)
        ug = (on * silu_g).astype(BF16)
        ug_ref[...] = ug
        yg = _dot(ug, wpg_ref[...])

        sgm = _sigmoid(gm_ref[...])
        sgg = _sigmoid(gg_ref[...])
        merged = (sgm * ym + sgg * yg).astype(BF16)
        mg_ref[...] = merged
        x2 = x_ref[...] + _dot(merged, wo_ref[...])
        gf = gf_ref[...]
        rf = lax.rsqrt(jnp.mean(x2 * x2, axis=-1, keepdims=True) + EPS)
        xh = x2 * rf
        err = xh * gf - tg_ref[...]
        loss_ref[...] += 0.5 * jnp.sum(jnp.mean(err * err, axis=-1, keepdims=True))

        dy = err * (1.0 / D_MODEL)
        dgf_ref[...] += jnp.sum(dy * xh, axis=0, keepdims=True)
        dxh = dy * gf
        dx2 = rf * (dxh - xh * jnp.mean(dxh * xh, axis=-1, keepdims=True))
        dx2_ref[...] = dx2
        dmerged = _dot_nt(dx2.astype(BF16), wo_ref[...])
        dym = (dmerged * sgm).astype(BF16)
        dyg = (dmerged * sgg).astype(BF16)
        dym_ref[...] = dym
        dyg_ref[...] = dyg
        dg_ref[:, P_GMLA - g0:P_GMLA - g0 + D_MODEL] = (dmerged * ym * sgm * (1.0 - sgm)).astype(BF16)
        dg_ref[:, P_GGLA - g0:P_GGLA - g0 + D_MODEL] = (dmerged * yg * sgg * (1.0 - sgg)).astype(BF16)
        dum = _dot_nt(dym, wpm_ref[...])
        dom_ref[...] = dum * silu_m
        dg_ref[:, P_ZMLA - g0:P_ZMLA - g0 + MLA_WIDTH] = (
            dum * om * (sm * (1.0 + zm * (1.0 - sm)))).astype(BF16)
        dug = _dot_nt(dyg, wpg_ref[...])
        dg_ref[:, P_ZGLA - g0:P_ZGLA - g0 + GLA_DV] = (
            dug * on * (sg * (1.0 + zg * (1.0 - sg)))).astype(BF16)
        don = dug * silu_g
        dggla = jnp.zeros((1, GLA_HV), F32)
        for h in range(GLA_HEADS):
            hs = slice(h * GLA_HV, (h + 1) * GLA_HV)
            don_h = don[:, hs]
            dggla = dggla + jnp.sum(don_h * xhat[h], axis=0, keepdims=True)
            dxh_h = don_h * ggla
            dog_ref[:, hs] = (rstd[h] * (dxh_h - xhat[h] * jnp.mean(dxh_h * xhat[h], axis=-1,
                                                                     keepdims=True))).astype(BF16)
        dggla_ref[...] += dggla

    row = lambda w: pl.BlockSpec((tm, w), lambda i: (i, 0))
    pcol = lambda w, off: pl.BlockSpec((tm, w), lambda i: (i, off // w))
    full = lambda a: pl.BlockSpec(a.shape, lambda i: (0, 0))
    sds = jax.ShapeDtypeStruct
    return pl.pallas_call(
        body, name="post_fwd_bwd",
        grid=(t // tm,),
        in_specs=[row(MLA_WIDTH), pcol(GLA_DV, P_ZGLA), pcol(D_MODEL, P_GMLA), pcol(D_MODEL, P_GGLA),
                  pcol(MLA_WIDTH, P_ZMLA), row(GLA_DV), row(D_MODEL), row(D_MODEL),
                  full(g_gla), full(g_final), full(w_pm), full(w_pg), full(w_o)],
        out_specs=(row(D_MODEL), row(MLA_WIDTH), row(GLA_DV), row(gw),
                   row(D_MODEL), row(MLA_WIDTH), row(GLA_DV), row(D_MODEL), row(D_MODEL),
                   pl.BlockSpec((1, LANE), lambda i: (0, 0)),
                   pl.BlockSpec((1, D_MODEL), lambda i: (0, 0)),
                   pl.BlockSpec((1, GLA_HV), lambda i: (0, 0))),
        out_shape=(sds((t, D_MODEL), F32), sds((t, MLA_WIDTH), F32), sds((t, GLA_DV), BF16),
                   sds((t, gw), BF16),
                   sds((t, D_MODEL), BF16), sds((t, MLA_WIDTH), BF16), sds((t, GLA_DV), BF16),
                   sds((t, D_MODEL), BF16), sds((t, D_MODEL), BF16),
                   sds((1, LANE), F32), sds((1, D_MODEL), F32), sds((1, GLA_HV), F32)),
        compiler_params=_cparams(("arbitrary",)),
    )(o_mla, proj, proj, proj, proj, o_gla, x, target, g_gla, g_final, w_pm, w_pg, w_o)


def _mla_prep_bwd(dq, dk, dv, dla, pre, proj, rq, rkv, g_q, g_kv, w_uq_p, w_k_p, w_v, w_gate_p,
                  rc, rsn, rsp):
    t = proj.shape[0]
    tm = min(256, t)
    gw = P_GROUPS[2][1]

    def body(dq_ref, dk_ref, dv_ref, dla_ref, pre_ref, cq_ref, ckv_ref, rq_ref, rkv_ref,
             gq_ref, gkv_ref, wuq_ref, wk_ref, wv_ref, wg_ref, c_ref, sn_ref, sp_ref,
             dg_ref, dqpre_ref, dpre_ref, dgq_ref, dgkv_ref, dbg_ref):
        @pl.when(pl.program_id(0) == 0)
        def _():
            dgq_ref[...] = jnp.zeros_like(dgq_ref)
            dgkv_ref[...] = jnp.zeros_like(dgkv_ref)
            dbg_ref[...] = jnp.zeros_like(dbg_ref)

        c, sn, sp = c_ref[...], sn_ref[...], sp_ref[...]
        dkr = jnp.zeros((tm, LANE), F32)
        for h in range(MLA_HEADS):
            sl = slice(h * HEAD_PAD, (h + 1) * HEAD_PAD)
            dqpre_ref[:, sl] = _rope_bwd(dq_ref[:, sl].astype(F32), c, sn, sp).astype(BF16)
            dkr = dkr + dk_ref[:, sl]
        dcqn = _dot_nt(dqpre_ref[...], wuq_ref[...])
        rq = rq_ref[...]
        xh = cq_ref[:, :MLA_Q_RANK] * rq
        dgq_ref[...] += jnp.sum(dcqn * xh, axis=0, keepdims=True)
        dxh = dcqn * gq_ref[...]
        dcq = rq * (dxh - xh * jnp.mean(dxh * xh, axis=-1, keepdims=True))
        dg_ref[:, :MLA_Q_RANK] = dcq.astype(BF16)
        dg_ref[:, MLA_Q_RANK:512] = jnp.zeros((tm, 512 - MLA_Q_RANK), BF16)

        dckvn = _dot_nt(dk_ref[...].astype(BF16), wk_ref[...]) + \
            _dot_nt(dv_ref[...].astype(BF16), wv_ref[...])
        rkv = rkv_ref[...]
        xh = ckv_ref[...] * rkv
        dgkv_ref[...] += jnp.sum(dckvn * xh, axis=0, keepdims=True)
        dxh = dckvn * gkv_ref[...]
        dg_ref[:, P_CKV - P_CQ:P_CKV - P_CQ + MLA_KV_RANK] = (
            rkv * (dxh - xh * jnp.mean(dxh * xh, axis=-1, keepdims=True))).astype(BF16)

        dlog_a = _dot_exact(_chunk_tri(tm, False), dla_ref[...])
        dpre = dlog_a * (1.0 / GLA_GATE_NORM) * (1.0 - _sigmoid(pre_ref[...]))
        dbg_ref[...] += jnp.sum(dpre, axis=0, keepdims=True)
        dpre = dpre.astype(BF16)
        dpre_ref[...] = dpre
        lane = lax.broadcasted_iota(jnp.int32, (tm, LANE), 1)
        in_kr = jnp.logical_and(lane >= MISC_KR, lane < MISC_KR + MLA_ROPE)
        dmisc = jnp.where(in_kr, _rope_bwd(dkr, c, sn, sp), 0.0) + _dot_nt(dpre, wg_ref[...])
        dg_ref[:, P_MISC - P_CQ:P_MISC - P_CQ + LANE] = dmisc.astype(BF16)

    hq = MLA_HEADS * HEAD_PAD
    row = lambda w: pl.BlockSpec((tm, w), lambda i: (i, 0))
    full = lambda a: pl.BlockSpec(a.shape, lambda i: (0, 0))
    acc = lambda w: pl.BlockSpec((1, w), lambda i: (0, 0))
    sds = jax.ShapeDtypeStruct
    return pl.pallas_call(
        body, name="mla_prep_bwd",
        grid=(t // tm,),
        in_specs=[row(hq), row(hq), row(MLA_WIDTH), row(GLA_DK), row(GLA_DK),
                  pl.BlockSpec((tm, 512), lambda i: (i, P_CQ // 512)),
                  pl.BlockSpec((tm, MLA_KV_RANK), lambda i: (i, P_CKV // MLA_KV_RANK)),
                  row(1), row(1), full(g_q), full(g_kv), full(w_uq_p), full(w_k_p), full(w_v),
                  full(w_gate_p), row(LANE), row(LANE), row(LANE)],
        out_specs=(row(gw), row(hq), row(GLA_DK),
                   acc(MLA_Q_RANK), acc(MLA_KV_RANK), acc(GLA_DK)),
        out_shape=(sds((t, gw), BF16), sds((t, hq), BF16), sds((t, GLA_DK), BF16),
                   sds((1, MLA_Q_RANK), F32), sds((1, MLA_KV_RANK), F32), sds((1, GLA_DK), F32)),
        compiler_params=_cparams(("arbitrary",)),
    )(dq, dk, dv, dla, pre, proj, proj, rq, rkv, g_q, g_kv, w_uq_p, w_k_p, w_v, w_gate_p,
      rc, rsn, rsp)


def _inproj_bwd(dgroups, w_pt, x, rstd, g_in, dx2, after):
    t = x.shape[0]
    tm = min(256, t)

    def body(d0_ref, d1_ref, d2_ref, w_ref, x_ref, r_ref, g_ref, dx2_ref, after_ref, dx_ref, dg_ref):
        del after_ref

        @pl.when(pl.program_id(0) == 0)
        def _():
            dg_ref[...] = jnp.zeros_like(dg_ref)

        dh = jnp.zeros((tm, D_MODEL), F32)
        for d_ref, (off, width) in zip((d0_ref, d1_ref, d2_ref), P_GROUPS):
            dh = dh + _dot(d_ref[...], w_ref[off:off + width, :])
        r = r_ref[...]
        xh = x_ref[...] * r
        dg_ref[...] += jnp.sum(dh * xh, axis=0, keepdims=True)
        dxh = dh * g_ref[...]
        dx_ref[...] = dx2_ref[...] + r * (dxh - xh * jnp.mean(dxh * xh, axis=-1, keepdims=True))

    row = lambda w: pl.BlockSpec((tm, w), lambda i: (i, 0))
    return pl.pallas_call(
        body, name="inproj_bwd",
        grid=(t // tm,),
        in_specs=[row(P_GROUPS[0][1]), row(P_GROUPS[1][1]), row(P_GROUPS[2][1]),
                  pl.BlockSpec((P_TOTAL, D_MODEL), lambda i: (0, 0)),
                  row(D_MODEL), row(1), pl.BlockSpec((1, D_MODEL), lambda i: (0, 0)), row(D_MODEL),
                  pl.BlockSpec(memory_space=pl.ANY)],
        out_specs=(row(D_MODEL), pl.BlockSpec((1, D_MODEL), lambda i: (0, 0))),
        out_shape=(jax.ShapeDtypeStruct((t, D_MODEL), F32),
                   jax.ShapeDtypeStruct((1, D_MODEL), F32)),
        compiler_params=_cparams(("arbitrary",)),
    )(*dgroups, w_pt, x, rstd, g_in, dx2, after)


def _matmul(name, a, b, tm, tn, dtype=F32):
    kk, m = a.shape
    n = b.shape[1]

    def body(a_ref, b_ref, o_ref):
        o_ref[...] = _dot_tn(a_ref[...].astype(BF16), b_ref[...].astype(BF16)).astype(dtype)

    return pl.pallas_call(
        body, name=name,
        grid=(n // tn, m // tm),
        in_specs=[pl.BlockSpec((kk, tm), lambda j, i: (0, i)),
                  pl.BlockSpec((kk, tn), lambda j, i: (0, j))],
        out_specs=pl.BlockSpec((tm, tn), lambda j, i: (i, j)),
        out_shape=jax.ShapeDtypeStruct((m, n), dtype),
        compiler_params=_cparams(("arbitrary", "arbitrary")),
    )(a, b)


def _adamw_update(part_refs, w_ref, m_ref, v_ref, g_ref, d_ref, nm_ref, nv_ref):
    g = part_refs[0][...].astype(F32)
    for p_ref in part_refs[1:]:
        g = g + p_ref[...].astype(F32)
    m_new = ADAM_B1 * m_ref[...] + (1.0 - ADAM_B1) * g
    v_new = ADAM_B2 * v_ref[...] + (1.0 - ADAM_B2) * (g * g)
    m_hat = m_new / (1.0 - ADAM_B1 ** ADAM_STEP)
    v_hat = v_new / (1.0 - ADAM_B2 ** ADAM_STEP)
    g_ref[...] = g
    nm_ref[...] = m_new
    nv_ref[...] = v_new
    d_ref[...] = -ADAM_LR * (m_hat / (jnp.sqrt(v_hat) + ADAM_EPS) + ADAM_WD * w_ref[...])


def _adamw_rows(name, parts, w, m, v, tr, first=None):
    _, rows, cols = w.shape
    slots = parts.shape[0]

    def body(*refs):
        lead_refs, p_ref = ([], refs[0]) if first is None else ([refs[0]], refs[1])
        _adamw_update(lead_refs + [p_ref.at[q] for q in range(slots)], *refs[len(lead_refs) + 1:])

    blk = pl.BlockSpec((None, tr, cols), lambda i: (0, i, 0))
    out = jax.ShapeDtypeStruct((1, rows, cols), F32)
    lead = [] if first is None else [pl.BlockSpec((tr, cols), lambda i: (i, 0))]
    return pl.pallas_call(
        body, name=name,
        grid=(rows // tr,),
        in_specs=lead + [pl.BlockSpec((slots, tr, cols), lambda i: (0, i, 0)), blk, blk, blk],
        out_specs=(blk, blk, blk, blk),
        out_shape=(out, out, out, out),
        compiler_params=_cparams(("arbitrary",)),
    )(*([] if first is None else [first]), parts, w, m, v)


def _adamw_transposed(name, first, parts, w, m, v, tl):
    _, rows, cols = w.shape
    slots = parts.shape[0]

    def body(f_ref, p_ref, *refs):
        _adamw_update([f_ref] + [p_ref.at[q] for q in range(slots)], *refs)

    blk = pl.BlockSpec((cols, None, tl), lambda i: (0, 0, i))
    out = jax.ShapeDtypeStruct((cols, 1, rows), F32)
    res = pl.pallas_call(
        body, name=name,
        grid=(rows // tl,),
        in_specs=[pl.BlockSpec((cols, tl), lambda i: (0, i)),
                  pl.BlockSpec((slots, cols, tl), lambda i: (0, 0, i)), blk, blk, blk],
        out_specs=(blk, blk, blk, blk),
        out_shape=(out, out, out, out),
        compiler_params=_cparams(("arbitrary",)),
    )(first, parts, *[a.transpose(2, 0, 1) for a in (w, m, v)])
    return [r.transpose(1, 2, 0) for r in res]


def _adamw_group(firsts, parts, ws, ms, vs):
    n = len(ws)

    def body(*refs):
        ins, outs = refs[:5 * n], refs[5 * n:]
        for a in range(n):
            _adamw_update([ins[a]] + [ins[n + a].at[q] for q in range(ins[n + a].shape[0])],
                          *[r.at[0] for r in (ins[2 * n + a], ins[3 * n + a], ins[4 * n + a])],
                          *[r.at[0] for r in outs[4 * a:4 * a + 4]])

    vmem = lambda k: [pl.BlockSpec(memory_space=pltpu.VMEM) for _ in range(k)]
    out_shape = []
    for w in ws:
        out_shape += [jax.ShapeDtypeStruct(w.shape, F32)] * 4
    res = pl.pallas_call(
        body, name="adamw_small_weights",
        in_specs=vmem(5 * n), out_specs=tuple(vmem(4 * n)), out_shape=tuple(out_shape),
        compiler_params=_cparams(),
    )(*firsts, *parts, *ws, *ms, *vs)
    return [res[4 * a:4 * a + 4] for a in range(n)]


def _rope_tables(positions):
    half = MLA_ROPE // 2
    freqs = ROPE_THETA ** (-jnp.arange(half, dtype=F32) / half)
    ang = positions.astype(F32).reshape(-1, 1) * freqs
    cos, sin = jnp.cos(ang), jnp.sin(ang)
    t = ang.shape[0]
    one, zero = jnp.ones((t, MLA_NOPE), F32), jnp.zeros((t, half), F32)
    tail = jnp.zeros((t, LANE - MLA_QK), F32)
    rc = jnp.concatenate([one, cos, cos, tail], axis=1)
    rsn = jnp.concatenate([0.0 * one, -sin, zero, tail], axis=1)
    rsp = jnp.concatenate([0.0 * one, zero, sin, tail], axis=1)
    return rc, rsn, rsp


def _cols_full(g):
    return g.transpose(1, 0, 2)


def kernel(x, positions, g_in, w_in, g_q, w_uq, g_kv, w_ukv, w_gla_gate, b_gla_gate, g_gla, w_proj_mla, w_proj_gla, w_out, g_final, loss_target, m_g_in, m_w_in, m_g_q, m_w_uq, m_g_kv, m_w_ukv, m_w_gla_gate, m_b_gla_gate, m_g_gla, m_w_proj_mla, m_w_proj_gla, m_w_out, m_g_final, v_g_in, v_w_in, v_g_q, v_w_uq, v_g_kv, v_w_ukv, v_w_gla_gate, v_b_gla_gate, v_g_gla, v_w_proj_mla, v_w_proj_gla, v_w_out, v_g_final):
    t = x.shape[1]
    x2d = x.reshape(t, D_MODEL)
    tgt = loss_target.reshape(t, D_MODEL)
    g_final2 = g_final.reshape(1, D_MODEL)
    sharded = [(w_in, m_w_in, v_w_in), (w_uq, m_w_uq, v_w_uq), (w_ukv, m_w_ukv, v_w_ukv),
               (w_gla_gate, m_w_gla_gate, v_w_gla_gate), (w_proj_mla, m_w_proj_mla, v_w_proj_mla),
               (w_proj_gla, m_w_proj_gla, v_w_proj_gla), (w_out, m_w_out, v_w_out)]

    w_in_t = w_in.transpose(2, 0, 1).reshape(SHARD_COLS, D_MODEL)
    g_w_in, g_uq, g_ukv, g_gate, g_pm, g_pg, g_o = _all_gather(
        [w_in_t.astype(BF16)] + [s[0][0].astype(BF16) for s in sharded[1:]])
    w_in_p = _weights_to_p(g_w_in)
    w_uq_p = jnp.pad(_cols_full(g_uq), ((0, 0), (0, 0), (0, HEAD_PAD - MLA_QK))).reshape(
        MLA_Q_RANK, MLA_HEADS * HEAD_PAD)
    ukv = _cols_full(g_ukv)
    w_k_p = jnp.pad(ukv[:, :, :MLA_NOPE], ((0, 0), (0, 0), (0, HEAD_PAD - MLA_NOPE))).reshape(
        MLA_KV_RANK, MLA_HEADS * HEAD_PAD)
    w_v = ukv[:, :, MLA_NOPE:].reshape(MLA_KV_RANK, MLA_WIDTH)
    w_gate_p = jnp.pad(_cols_full(g_gate).reshape(GLA_GATE_RANK, GLA_DK),
                       ((MISC_ALR, LANE - MISC_ALR - GLA_GATE_RANK), (0, 0)))
    w_pm = _cols_full(g_pm).reshape(MLA_WIDTH, D_MODEL)
    w_pg = g_pg.reshape(GLA_DV, D_MODEL)
    w_o = g_o.reshape(D_MODEL, D_MODEL)
    rc, rsn, rsp = _rope_tables(positions)

    proj, h, rstd = _inproj(x2d, g_in, w_in_p)
    q, k, v, log_a, pre, cqn, ckvn, rq, rkv, misc = _mla_prep(
        proj, g_q, g_kv, w_uq_p, w_k_p, w_v, w_gate_p, b_gla_gate, rc, rsn, rsp)
    o_mla, lse = _mla_attn_fwd(q, k, v)
    o_gla, states = _gla_fwd(proj, log_a)

    (dx2, do_mla, do_gla, d_out, merged, um, ug, dym, dyg, loss_p, dg_final,
     dg_gla) = _post(o_mla, proj, o_gla, x2d, tgt, g_gla, g_final2, w_pm, w_pg, w_o)

    everyone = tuple(range(N_DEV))
    p_pm = _matmul("dw_proj_mla", um, dym, 512, 512, BF16).reshape(
        MLA_WIDTH, N_DEV, D_MODEL // N_DEV).transpose(1, 0, 2)
    p_pg = _matmul("dw_proj_gla", ug, dyg, 512, 512, BF16).reshape(N_DEV, -1, D_MODEL)
    p_o = _matmul("dw_out", merged, dx2, 512, 512, BF16).reshape(N_DEV, -1, D_MODEL)
    owns = [lax.empty((SHARD_COLS, D_MODEL), BF16)] + [lax.empty(p.shape[1:], BF16)
                                                       for p in (p_pm, p_pg, p_o)]
    lands = [lax.empty((3,) + o.shape, BF16) for o in owns]
    dw_groups, started = {}, []

    def reduce_scatter_stage(s, dests, extra=()):
        assert set(g for d in dests for g in _shard_groups(d)) <= set(dw_groups)
        k = 1 + len(extra)
        parts = [_grads_to_shards("grads_to_shards_%d" % s, dw_groups, dests)] + list(extra)
        all_dests = [dests] + [everyone] * len(extra)
        sums, owns[:k] = _swap_add("swap_add_%d" % s, parts, owns[:k], all_dests)
        sems, sums, lands[:k], token = _ici_start("ici_start_%d" % s, sums, lands[:k], all_dests)
        started.extend((sems[a][0], sems[a][1], sums[a], a, all_dests[a]) for a in range(k))
        return token

    dw_groups[1] = _matmul("dw_in_1", d_out, h, 512, 512, BF16)
    token = reduce_scatter_stage(1, (5, 6, 7), (p_pm, p_pg, p_o))
    d_gla, dla = _gla_bwd(proj, log_a, do_gla, states, token)
    dw_groups[0] = _matmul("dw_in_0", d_gla, h, 512, 512, BF16)
    token = reduce_scatter_stage(2, (1, 2, 3))
    dq, dk, dv = _mla_attn_bwd(q, k, v, o_mla, do_mla, lse, token)
    d_lat, dqpre, dpre, dg_q, dg_kv, db_gate = _mla_prep_bwd(
        dq, dk, dv, dla, pre, proj, rq, rkv, g_q, g_kv, w_uq_p, w_k_p, w_v, w_gate_p, rc, rsn, rsp)
    dw_groups[2] = _matmul("dw_in_2", d_lat, h, 896, 512, BF16)
    token = reduce_scatter_stage(3, (0, 4))
    grad_x, dg_in = _inproj_bwd((d_gla, d_out, d_lat), w_in_p, x2d, rstd, g_in, dx2, token)

    dw_uq = _matmul("dw_uq", cqn, dqpre, MLA_Q_RANK, 512, BF16)
    p_uq = dw_uq.reshape(MLA_Q_RANK, MLA_HEADS, HEAD_PAD)[:, :, :MLA_QK].transpose(1, 0, 2)
    dw_k = _matmul("dw_uk", ckvn, dk, MLA_KV_RANK, 512, BF16)
    dw_v = _matmul("dw_uv", ckvn, dv, MLA_KV_RANK, 512, BF16)
    p_ukv = jnp.concatenate(
        [dw_k.reshape(MLA_KV_RANK, MLA_HEADS, HEAD_PAD)[:, :, :MLA_NOPE],
         dw_v.reshape(MLA_KV_RANK, MLA_HEADS, MLA_VDIM)], axis=2).transpose(1, 0, 2)
    dw_gate = _matmul("dw_gate", misc, dpre, LANE, 512, BF16)
    p_gate = dw_gate[MISC_ALR:MISC_ALR + GLA_GATE_RANK].reshape(
        GLA_GATE_RANK, N_DEV, GLA_DK // N_DEV).transpose(1, 0, 2)
    small = jnp.concatenate([dg_in.reshape(-1), dg_q.reshape(-1), dg_kv.reshape(-1),
                             db_gate.reshape(-1), dg_gla.reshape(-1), dg_final.reshape(-1),
                             loss_p[0, :1]])
    small = jnp.pad(small, (0, SMALL_ROWS * LANE - small.shape[0])).reshape(SMALL_ROWS, LANE)

    recv = _exchange_grads([p_uq, p_ukv, p_gate], small)
    lands = _ici_wait("ici_wait", started, lands, recv[3])
    big = [_adamw_transposed("adamw_w_in", owns[0], lands[0], *sharded[0], 256)]
    big += _adamw_group([r[0] for r in recv[:3]] + list(owns[1:]),
                        [r[1:] for r in recv[:3]] + list(lands[1:]),
                        *[[s[j] for s in sharded[1:]] for j in range(3)])
    replicated = [(g_in, m_g_in, v_g_in), (g_q, m_g_q, v_g_q), (g_kv, m_g_kv, v_g_kv),
                  (b_gla_gate, m_b_gla_gate, v_b_gla_gate), (g_gla, m_g_gla, v_g_gla),
                  (g_final, m_g_final, v_g_final)]
    spacks = [jnp.pad(jnp.concatenate([s[j].reshape(-1) for s in replicated]),
                      (0, SMALL_ROWS * LANE - sum(SMALL_SIZES))).reshape(1, SMALL_ROWS, LANE)
              for j in range(3)]
    tiny = _adamw_rows("adamw_gains", recv[3], spacks[0], spacks[1], spacks[2], SMALL_ROWS)

    outs = {}
    names = ("w_in", "w_uq", "w_ukv", "w_gla_gate", "w_proj_mla", "w_proj_gla", "w_out")
    for j, kind in enumerate(("grad", "delta", "new_m", "new_v")):
        for name, res in zip(names, big):
            outs[kind, name] = res[j]
        flat = tiny[j].reshape(-1)
        off = 0
        for name, size in zip(("g_in", "g_q", "g_kv", "b_gla_gate", "g_gla", "g_final"), SMALL_SIZES):
            shape = (size,) if name == "g_final" else (1, size)
            outs[kind, name] = flat[off:off + size].reshape(shape)
            off += size
    loss = tiny[0].reshape(-1)[sum(SMALL_SIZES)]
    order = ("g_in", "w_in", "g_q", "w_uq", "g_kv", "w_ukv", "w_gla_gate", "b_gla_gate", "g_gla",
             "w_proj_mla", "w_proj_gla", "w_out", "g_final")
    result = [loss, grad_x.reshape(1, t, D_MODEL)]
    for kind in ("grad", "delta", "new_m", "new_v"):
        result += [outs[kind, name] for name in order]
    return tuple(result)
```

```python
import math

import jax
import jax.numpy as jnp
from jax import lax
from jax.experimental import pallas as pl
from jax.experimental.pallas import tpu as pltpu

F32 = jnp.float32
BF16 = jnp.bfloat16
MESH = pl.DeviceIdType.MESH
N_DEV = 8

D_MODEL = 1024
EPS = 1e-6
MLA_HEADS = 8
MLA_NOPE = 64
MLA_ROPE = 32
MLA_VDIM = 64
MLA_Q_RANK = 384
MLA_KV_RANK = 256
MLA_QK = MLA_NOPE + MLA_ROPE
MLA_WIDTH = MLA_HEADS * MLA_VDIM
ROPE_THETA = 10000.0
GLA_HEADS = 4
GLA_DK = 512
GLA_DV = 1024
GLA_HK = 128
GLA_HV = 256
GLA_GATE_RANK = 16
GLA_GATE_NORM = 16.0
GLA_CHUNK = 64
D_IN = 6320

ADAM_LR = 0.001
ADAM_B1 = 0.9
ADAM_B2 = 0.999
ADAM_EPS = 1e-08
ADAM_WD = 0.01
ADAM_STEP = 10

LANE = 128
HEAD_PAD = 128
VMEM_LIMIT = 60 * 1024 * 1024

P_VG, P_QG, P_KG = 0, 1024, 1536
P_ZGLA, P_GMLA, P_GGLA, P_ZMLA = 2048, 3072, 4096, 5120
P_CQ, P_CKV, P_MISC = 5632, 6144, 6400
P_TOTAL = 6528
P_GROUPS = ((0, 2048), (2048, 3584), (5632, 896))
MISC_KR = 64
MISC_ALR = 96
SHARD_COLS = D_IN // N_DEV
P_COMPONENTS = ((0, 384, P_CQ), (384, 256, P_CKV), (640, 32, P_MISC + MISC_KR), (672, 512, P_ZMLA),
                (1184, 512, P_QG), (1696, 512, P_KG), (2208, 1024, P_VG),
                (3232, 16, P_MISC + MISC_ALR), (3248, 1024, P_ZGLA), (4272, 1024, P_GMLA),
                (5296, 1024, P_GGLA))

SMALL_SIZES = (1024, 384, 256, 512, 256, 1024)
SMALL_ROWS = 32


def _segments():
    segs = []
    for g0, n, p0 in P_COMPONENTS:
        g = g0
        while g < g0 + n:
            d = g // SHARD_COLS
            end = min(g0 + n, (d + 1) * SHARD_COLS)
            segs.append((d, g - d * SHARD_COLS, end - g, p0 + g - g0))
            g = end
    return segs


def _cparams(sem=None):
    if sem is None:
        return pltpu.CompilerParams(vmem_limit_bytes=VMEM_LIMIT)
    return pltpu.CompilerParams(dimension_semantics=sem, vmem_limit_bytes=VMEM_LIMIT)


def _sigmoid(v):
    return 1.0 / (1.0 + jnp.exp(-v))


def _dot(a, b):
    return jnp.dot(a, b, preferred_element_type=F32)


def _dot_nt(a, b):
    return lax.dot_general(a, b, (((1,), (1,)), ((), ())), preferred_element_type=F32)


def _dot_tn(a, b):
    return lax.dot_general(a, b, (((0,), (0,)), ((), ())), preferred_element_type=F32)


def _dot_exact(a, b):
    return jnp.dot(a, b, preferred_element_type=F32, precision=lax.Precision.HIGHEST)


def _rope_fwd(blk, c, sn, sp):
    return blk * c + pltpu.roll(blk, LANE - 16, 1) * sn + pltpu.roll(blk, 16, 1) * sp


def _rope_bwd(blk, c, sn, sp):
    return blk * c + pltpu.roll(blk * sn, 16, 1) + pltpu.roll(blk * sp, LANE - 16, 1)


def _mesh_pos():
    return lax.axis_index("x"), lax.axis_index("y"), lax.axis_index("c")


def _hbm_specs(n):
    return [pl.BlockSpec(memory_space=pltpu.HBM) for _ in range(n)]


def _all_gather(shards):
    n = len(shards)

    def body(*refs):
        x_refs, out_refs = refs[:n], refs[n:2 * n]
        send_sems, recv_sems, local_sems = refs[2 * n:]
        x, y, c = _mesh_pos()
        me, sibling = (x, y, c), (x, y, 1 - c)
        chips = [(1 - x, y), (x, 1 - y), (1 - x, 1 - y)]

        def slot(a, px, py, pc):
            return out_refs[a].at[4 * px + 2 * py + pc]

        def copies(k, block, to, own=False):
            return [pltpu.make_async_remote_copy(
                src_ref=x_refs[a] if own else slot(a, *block), dst_ref=slot(a, *block),
                send_sem=send_sems.at[k, a], recv_sem=recv_sems.at[k, a],
                device_id=to, device_id_type=MESH) for a in range(n)]

        mine = [pltpu.make_async_copy(x_refs[a], slot(a, *me), local_sems.at[a]) for a in range(n)]
        for cp in mine:
            cp.start()
        first = copies(0, me, sibling, own=True)
        for j, chip in enumerate(chips):
            first += copies(1 + j, me, (*chip, c), own=True)
        for cp in first:
            cp.start()
        passed = []
        for j, chip in enumerate(chips):
            for cp in copies(1 + j, (*chip, c), me):
                cp.wait_recv()
            fwd = copies(4 + j, (*chip, c), sibling)
            for cp in fwd:
                cp.start()
            passed += fwd
        for cp in copies(0, sibling, me):
            cp.wait_recv()
        for j, chip in enumerate(chips):
            for cp in copies(4 + j, (*chip, 1 - c), me):
                cp.wait_recv()
        for cp in first + passed:
            cp.wait_send()
        for cp in mine:
            cp.wait()

    return pl.pallas_call(
        body, name="all_gather_weights",
        out_shape=tuple(jax.ShapeDtypeStruct((N_DEV,) + s.shape, s.dtype) for s in shards),
        in_specs=_hbm_specs(n), out_specs=tuple(_hbm_specs(n)),
        scratch_shapes=[pltpu.SemaphoreType.DMA((7, n)), pltpu.SemaphoreType.DMA((7, n)),
                        pltpu.SemaphoreType.DMA((n,))],
        compiler_params=_cparams(),
    )(*shards)


def _exchange_grads(parts, small):
    n = len(parts)
    rows_per_add = 256

    def body(*refs):
        p_refs, s_ref = refs[:n], refs[n]
        out_refs, sout_ref = refs[n + 1:2 * n + 1], refs[2 * n + 1]
        mine_v, recv_v = refs[2 * n + 2:3 * n + 2], refs[3 * n + 2:4 * n + 2]
        (d2d_send, d2d_recv, ici_send, ici_recv, s_send, s_recv, load_sems, own_sems,
         sown_sem) = refs[4 * n + 2:]
        x, y, c = _mesh_pos()
        me = 4 * x + 2 * y + c
        chips = [(x, y), (1 - x, y), (x, 1 - y), (1 - x, 1 - y)]

        sown = pltpu.make_async_copy(s_ref, sout_ref.at[me], sown_sem)
        sown.start()
        tiny = []
        for k in range(1, N_DEV):
            peer = (x ^ (k >> 2), y ^ ((k >> 1) & 1), c ^ (k & 1))
            tiny.append(pltpu.make_async_remote_copy(
                src_ref=s_ref, dst_ref=sout_ref.at[me], send_sem=s_send.at[k - 1],
                recv_sem=s_recv.at[k - 1], device_id=peer, device_id_type=MESH))
        for cp in tiny:
            cp.start()

        loads, swaps = [], []
        for a in range(n):
            for j, (px, py) in enumerate(chips):
                loads.append(pltpu.make_async_copy(
                    p_refs[a].at[4 * px + 2 * py + c], mine_v[a].at[j], load_sems.at[a, j]))
                swaps.append(pltpu.make_async_remote_copy(
                    src_ref=p_refs[a].at[4 * px + 2 * py + 1 - c], dst_ref=recv_v[a].at[j],
                    send_sem=d2d_send.at[a, j], recv_sem=d2d_recv.at[a, j],
                    device_id=(x, y, 1 - c), device_id_type=MESH))
        for cp in swaps + loads:
            cp.start()

        outgoing, own = [], []
        for a in range(n):
            rows = p_refs[a].shape[1]
            step = math.gcd(rows, rows_per_add)
            for j in range(4):
                loads[4 * a + j].wait()
                swaps[4 * a + j].wait_recv()

                @pl.loop(0, rows // step)
                def _(i):
                    rs = pl.ds(pl.multiple_of(i * step, step), step)
                    mine_v[a][j, rs, :] = (mine_v[a][j, rs, :].astype(F32)
                                           + recv_v[a][j, rs, :].astype(F32)).astype(BF16)

                if j == 0:
                    own.append(pltpu.make_async_copy(mine_v[a].at[0], out_refs[a].at[0],
                                                     own_sems.at[a]))
                    own[-1].start()
                else:
                    outgoing.append(pltpu.make_async_remote_copy(
                        src_ref=mine_v[a].at[j], dst_ref=out_refs[a].at[j],
                        send_sem=ici_send.at[a, j - 1], recv_sem=ici_recv.at[a, j - 1],
                        device_id=(*chips[j], c), device_id_type=MESH))
                    outgoing[-1].start()

        for cp in tiny + outgoing:
            cp.wait_recv()
        for cp in tiny + outgoing + swaps:
            cp.wait_send()
        for cp in own:
            cp.wait()
        sown.wait()

    outs = [jax.ShapeDtypeStruct((4,) + p.shape[1:], p.dtype) for p in parts]
    outs.append(jax.ShapeDtypeStruct((N_DEV,) + small.shape, small.dtype))
    stage = [pltpu.VMEM((4,) + p.shape[1:], p.dtype) for p in parts]
    return pl.pallas_call(
        body, name="exchange_grads",
        out_shape=tuple(outs),
        in_specs=_hbm_specs(n + 1), out_specs=tuple(_hbm_specs(n + 1)),
        scratch_shapes=stage + stage + [
            pltpu.SemaphoreType.DMA((n, 4)), pltpu.SemaphoreType.DMA((n, 4)),
            pltpu.SemaphoreType.DMA((n, 3)), pltpu.SemaphoreType.DMA((n, 3)),
            pltpu.SemaphoreType.DMA((7,)), pltpu.SemaphoreType.DMA((7,)),
            pltpu.SemaphoreType.DMA((n, 4)), pltpu.SemaphoreType.DMA((n,)),
            pltpu.SemaphoreType.DMA],
        compiler_params=_cparams(),
    )(*parts, small)


def _dev(d):
    return d >> 2, (d >> 1) & 1, d & 1


def _swap_add(name, parts, owns_prev, dests):
    na = len(parts)
    step = 256

    def body(*refs):
        p_refs, h_refs, own_refs = refs[:na], refs[2 * na:3 * na], refs[3 * na:4 * na]
        mine_vs, recv_vs = refs[4 * na:5 * na], refs[5 * na:6 * na]
        send_sems, recv_sems, load_sems, store_sems, own_sems = refs[6 * na:]
        x, y, c = _mesh_pos()
        work = []
        for a in range(na):
            for i, d in enumerate(dests[a]):
                dx, dy, dc = _dev(d)
                keep = c == dc
                mine = jnp.logical_and(keep, jnp.logical_and(x == dx, y == dy))
                swap = pltpu.make_async_remote_copy(
                    src_ref=p_refs[a].at[i], dst_ref=recv_vs[a].at[i], send_sem=send_sems.at[a, i],
                    recv_sem=recv_sems.at[a, i], device_id=(x, y, 1 - c), device_id_type=MESH)
                load = pltpu.make_async_copy(p_refs[a].at[i], mine_vs[a].at[i], load_sems.at[a, i])
                store = pltpu.make_async_copy(mine_vs[a].at[i], h_refs[a].at[i], store_sems.at[a, i])
                own = pltpu.make_async_copy(mine_vs[a].at[i], own_refs[a], own_sems.at[a])
                pl.when(keep)(load.start)
                pl.when(jnp.logical_not(keep))(swap.start)
                work.append((a, i, keep, mine, swap, load, store, own))
        for a, i, keep, mine, swap, load, store, own in work:
            cols = parts[a].shape[2]
            cstep = math.gcd(cols, step)

            @pl.when(keep)
            def _(a=a, i=i, mine=mine, swap=swap, load=load, store=store, own=own, cols=cols,
                  cstep=cstep):
                load.wait()
                swap.wait_recv()

                @pl.loop(0, cols // cstep)
                def _(r):
                    cs = pl.ds(pl.multiple_of(r * cstep, cstep), cstep)
                    mine_vs[a][i, :, cs] = (mine_vs[a][i, :, cs].astype(F32)
                                            + recv_vs[a][i, :, cs].astype(F32)).astype(BF16)

                store.start()
                pl.when(mine)(own.start)
        for a, i, keep, mine, swap, load, store, own in work:
            pl.when(jnp.logical_not(keep))(swap.wait_send)
            pl.when(keep)(store.wait)
            pl.when(mine)(own.wait)

    stage = [pltpu.VMEM(p.shape, BF16) for p in parts]
    most = max(len(d) for d in dests)
    res = pl.pallas_call(
        body, name=name,
        out_shape=tuple([jax.ShapeDtypeStruct(p.shape, BF16) for p in parts]
                        + [jax.ShapeDtypeStruct(o.shape, BF16) for o in owns_prev]),
        in_specs=_hbm_specs(2 * na), out_specs=tuple(_hbm_specs(2 * na)),
        input_output_aliases={na + a: na + a for a in range(na)},
        scratch_shapes=stage + stage + [pltpu.SemaphoreType.DMA((na, most)) for _ in range(4)]
        + [pltpu.SemaphoreType.DMA((na,))],
        compiler_params=_cparams(),
    )(*parts, *owns_prev)
    return res[:na], res[na:]


def _ici_copies(h_ref, land_ref, send_sems, recv_sems, dests):
    x, y, c = _mesh_pos()
    sends, arrivals = [], []
    for i, d in enumerate(dests):
        dx, dy, dc = _dev(d)
        j = (x != dx).astype(jnp.int32) + 2 * (y != dy).astype(jnp.int32)
        slot = jnp.maximum(j - 1, 0)
        sends.append((jnp.logical_and(c == dc, j > 0), pltpu.make_async_remote_copy(
            src_ref=h_ref.at[i], dst_ref=land_ref.at[slot], send_sem=send_sems.at[i],
            recv_sem=recv_sems.at[slot], device_id=(dx, dy, dc), device_id_type=MESH)))
        arrivals.append((jnp.logical_and(c == dc, j == 0), [pltpu.make_async_remote_copy(
            src_ref=h_ref.at[i], dst_ref=land_ref.at[r], send_sem=send_sems.at[i],
            recv_sem=recv_sems.at[r], device_id=(dx, dy, dc), device_id_type=MESH)
            for r in range(3)]))
    return sends, arrivals


def _ici_start(name, hs, lands, dests):
    na = len(hs)

    def body(*refs):
        h_refs, land_refs, sems = refs[:na], refs[na:2 * na], refs[2 * na:4 * na]
        token = refs[-1]
        for a in range(na):
            sends, _ = _ici_copies(h_refs[a], land_refs[a], sems[2 * a], sems[2 * a + 1], dests[a])
            for go, cp in sends:
                pl.when(go)(cp.start)
        token[...] = jnp.zeros_like(token)

    hbm, sem = pl.BlockSpec(memory_space=pltpu.HBM), pl.BlockSpec(memory_space=pltpu.SEMAPHORE)
    sem_shapes = []
    for a in range(na):
        sem_shapes += [pltpu.SemaphoreType.DMA((len(dests[a]),)), pltpu.SemaphoreType.DMA((3,))]
    res = pl.pallas_call(
        body, name=name,
        out_shape=tuple(sem_shapes) + tuple(pltpu.HBM(v.shape, v.dtype) for v in list(hs) + list(lands))
        + (jax.ShapeDtypeStruct((8, LANE), F32),),
        in_specs=(hbm,) * (2 * na),
        out_specs=(sem,) * (2 * na) + (hbm,) * (2 * na) + (pl.BlockSpec(memory_space=pltpu.VMEM),),
        input_output_aliases={i: 2 * na + i for i in range(2 * na)},
        compiler_params=pltpu.CompilerParams(
            has_side_effects=pltpu.SideEffectType.DATAFLOW_SIDE_EFFECTING,
            vmem_limit_bytes=VMEM_LIMIT),
    )(*[pltpu.with_memory_space_constraint(v, pltpu.HBM) for v in list(hs) + list(lands)])
    sems = [(res[2 * a], res[2 * a + 1]) for a in range(na)]
    return sems, res[2 * na:3 * na], res[3 * na:4 * na], res[-1]


def _ici_wait(name, started, lands, after):
    k, nl = len(started), len(lands)

    def body(*refs):
        land_refs = refs[3 * k:3 * k + nl]
        for s in range(k):
            h_ref, send_sems, recv_sems = refs[3 * s:3 * s + 3]
            sends, arrivals = _ici_copies(h_ref, land_refs[started[s][3]], send_sems, recv_sems,
                                          started[s][4])
            for go, cp in sends:
                pl.when(go)(cp.wait_send)
            for here, cps in arrivals:
                for cp in cps:
                    pl.when(here)(cp.wait_recv)

    hbm, sem = pl.BlockSpec(memory_space=pltpu.HBM), pl.BlockSpec(memory_space=pltpu.SEMAPHORE)
    operands, specs = [], []
    for send_sems, recv_sems, h, _, _ in started:
        operands += [h, send_sems, recv_sems]
        specs += [hbm, sem, sem]
    return pl.pallas_call(
        body, name=name,
        out_shape=tuple(pltpu.HBM(v.shape, v.dtype) for v in lands),
        in_specs=tuple(specs) + (hbm,) * nl + (pl.BlockSpec(memory_space=pl.ANY),),
        out_specs=(hbm,) * nl,
        input_output_aliases={3 * k + i: i for i in range(nl)},
        compiler_params=pltpu.CompilerParams(
            has_side_effects=pltpu.SideEffectType.DATAFLOW_SIDE_EFFECTING,
            vmem_limit_bytes=VMEM_LIMIT),
    )(*operands, *lands, after)


def _weights_to_p(gathered):
    tl = 256
    segs = sorted(_segments(), key=lambda s: s[3])

    def body(g_ref, o_ref):
        pieces, pos = [], 0
        for d, c0, n, p0 in segs:
            if p0 > pos:
                pieces.append(jnp.zeros((p0 - pos, tl), F32))
            pieces.append(g_ref[d, c0:c0 + n, :].astype(F32))
            pos = p0 + n
        pieces.append(jnp.zeros((P_TOTAL - pos, tl), F32))
        o_ref[...] = jnp.concatenate(pieces, axis=0).astype(BF16)

    return pl.pallas_call(
        body, name="weights_to_layout",
        grid=(D_MODEL // tl,),
        in_specs=[pl.BlockSpec((N_DEV, SHARD_COLS, tl), lambda i: (0, 0, i))],
        out_specs=pl.BlockSpec((P_TOTAL, tl), lambda i: (0, i)),
        out_shape=jax.ShapeDtypeStruct((P_TOTAL, D_MODEL), BF16),
        compiler_params=_cparams(("arbitrary",)),
    )(gathered)


def _group_of(p0):
    return max(i for i, (off, _) in enumerate(P_GROUPS) if off <= p0)


def _shard_groups(d):
    return sorted({_group_of(s[3]) for s in _segments() if s[0] == d})


def _grads_to_shards(name, groups, dests):
    tl = 256
    segs = _segments()
    used = sorted(groups)

    def body(*refs):
        g_refs, o_ref = dict(zip(used, refs[:-1])), refs[-1]
        for i, d in enumerate(dests):
            pieces = []
            for _, c0, n, p0 in sorted([s for s in segs if s[0] == d], key=lambda s: s[1]):
                gi = _group_of(p0)
                lo = p0 - P_GROUPS[gi][0]
                pieces.append(g_refs[gi][lo:lo + n, :].astype(F32))
            o_ref[i] = jnp.concatenate(pieces, axis=0).astype(BF16)

    return pl.pallas_call(
        body, name=name,
        grid=(D_MODEL // tl,),
        in_specs=[pl.BlockSpec((P_GROUPS[g][1], tl), lambda i: (0, i)) for g in used],
        out_specs=pl.BlockSpec((len(dests), SHARD_COLS, tl), lambda i: (0, 0, i)),
        out_shape=jax.ShapeDtypeStruct((len(dests), SHARD_COLS, D_MODEL), BF16),
        compiler_params=_cparams(("arbitrary",)),
    )(*[groups[g] for g in used])


def _inproj(x, g_in, w_pt):
    t = x.shape[0]
    tm = min(256, t)
    nj = 3
    tn = P_TOTAL // nj

    def body(x_ref, g_ref, w_ref, proj_ref, h_ref, r_ref):
        xf = x_ref[...]
        r = lax.rsqrt(jnp.mean(xf * xf, axis=-1, keepdims=True) + EPS)
        h = ((xf * r) * g_ref[...]).astype(BF16)
        proj_ref[...] = _dot_nt(h, w_ref[...])

        @pl.when(pl.program_id(0) == 0)
        def _():
            h_ref[...] = h
            r_ref[...] = r

    first = lambda j, i: (jnp.where(j == 0, i, t // tm - 1), 0)
    return pl.pallas_call(
        body, name="inproj",
        grid=(nj, t // tm),
        in_specs=[pl.BlockSpec((tm, D_MODEL), lambda j, i: (i, 0)),
                  pl.BlockSpec((1, D_MODEL), lambda j, i: (0, 0)),
                  pl.BlockSpec((tn, D_MODEL), lambda j, i: (j, 0))],
        out_specs=(pl.BlockSpec((tm, tn), lambda j, i: (i, j)),
                   pl.BlockSpec((tm, D_MODEL), first),
                   pl.BlockSpec((tm, 1), first)),
        out_shape=(jax.ShapeDtypeStruct((t, P_TOTAL), F32),
                   jax.ShapeDtypeStruct((t, D_MODEL), BF16),
                   jax.ShapeDtypeStruct((t, 1), F32)),
        compiler_params=_cparams(("arbitrary", "arbitrary")),
    )(x, g_in, w_pt)


def _mla_prep(proj, g_q, g_kv, w_uq_p, w_k_p, w_v, w_gate_p, b_gate, rc, rsn, rsp):
    t = proj.shape[0]
    tm = min(256, t)
    hq = MLA_HEADS * HEAD_PAD

    def body(cq_ref, ckv_ref, misc_ref, gq_ref, gkv_ref, wuq_ref, wk_ref, wv_ref, wg_ref, bg_ref,
             c_ref, sn_ref, sp_ref,
             q_ref, k_ref, v_ref, la_ref, pre_ref, cqn_ref, ckvn_ref, rq_ref, rkv_ref, mb_ref):
        c, sn, sp = c_ref[...], sn_ref[...], sp_ref[...]
        cq = cq_ref[:, :MLA_Q_RANK]
        rq = lax.rsqrt(jnp.mean(cq * cq, axis=-1, keepdims=True) + EPS)
        cqn = ((cq * rq) * gq_ref[...]).astype(BF16)
        cqn_ref[...] = cqn
        rq_ref[...] = rq
        qpre = _dot(cqn, wuq_ref[...])
        ckv = ckv_ref[...]
        rkv = lax.rsqrt(jnp.mean(ckv * ckv, axis=-1, keepdims=True) + EPS)
        ckvn = ((ckv * rkv) * gkv_ref[...]).astype(BF16)
        ckvn_ref[...] = ckvn
        rkv_ref[...] = rkv
        kn = _dot(ckvn, wk_ref[...])
        v_ref[...] = _dot(ckvn, wv_ref[...]).astype(BF16)
        misc = misc_ref[...]
        krope = _rope_fwd(misc, c, sn, sp)
        for h in range(MLA_HEADS):
            sl = slice(h * HEAD_PAD, (h + 1) * HEAD_PAD)
            q_ref[:, sl] = _rope_fwd(qpre[:, sl], c, sn, sp).astype(BF16)
            k_ref[:, sl] = (kn[:, sl] + krope).astype(BF16)
        mb_ref[...] = misc.astype(BF16)
        pre = _dot(mb_ref[...], wg_ref[...]) + bg_ref[...]
        pre_ref[...] = pre
        log_a = (jnp.minimum(pre, 0.0) - jnp.log(1.0 + jnp.exp(-jnp.abs(pre)))) / GLA_GATE_NORM
        la_ref[...] = _dot_exact(_chunk_tri(tm, True), log_a)

    row = lambda w: pl.BlockSpec((tm, w), lambda i: (i, 0))
    full = lambda a: pl.BlockSpec(a.shape, lambda i: (0, 0))
    return pl.pallas_call(
        body, name="mla_prep",
        grid=(t // tm,),
        in_specs=[pl.BlockSpec((tm, 512), lambda i: (i, P_CQ // 512)),
                  pl.BlockSpec((tm, MLA_KV_RANK), lambda i: (i, P_CKV // MLA_KV_RANK)),
                  pl.BlockSpec((tm, LANE), lambda i: (i, P_MISC // LANE)),
                  full(g_q), full(g_kv), full(w_uq_p), full(w_k_p), full(w_v), full(w_gate_p),
                  full(b_gate), row(LANE), row(LANE), row(LANE)],
        out_specs=(row(hq), row(hq), row(MLA_WIDTH), row(GLA_DK), row(GLA_DK),
                   row(MLA_Q_RANK), row(MLA_KV_RANK), row(1), row(1), row(LANE)),
        out_shape=(jax.ShapeDtypeStruct((t, hq), BF16), jax.ShapeDtypeStruct((t, hq), BF16),
                   jax.ShapeDtypeStruct((t, MLA_WIDTH), BF16),
                   jax.ShapeDtypeStruct((t, GLA_DK), F32), jax.ShapeDtypeStruct((t, GLA_DK), F32),
                   jax.ShapeDtypeStruct((t, MLA_Q_RANK), BF16),
                   jax.ShapeDtypeStruct((t, MLA_KV_RANK), BF16),
                   jax.ShapeDtypeStruct((t, 1), F32), jax.ShapeDtypeStruct((t, 1), F32),
                   jax.ShapeDtypeStruct((t, LANE), BF16)),
        compiler_params=_cparams(("arbitrary",)),
    )(proj, proj, proj, g_q, g_kv, w_uq_p, w_k_p, w_v, w_gate_p, b_gate, rc, rsn, rsp)


def _attn_masks(tq, i):
    keys = (i + 1) * tq
    rows = i * tq + lax.broadcasted_iota(jnp.int32, (tq, keys), 0)
    cols = lax.broadcasted_iota(jnp.int32, (tq, keys), 1)
    lane = lax.broadcasted_iota(jnp.int32, (tq, LANE), 1)
    return cols <= rows, lane < MLA_VDIM


def _for_each_query_tile(n_tiles, fn):
    for i in range(n_tiles):
        pl.when(pl.program_id(1) == i)(lambda i=i: fn(i))


def _mla_attn_fwd(q, k, v):
    t = q.shape[0]
    tq = min(256, t)
    scale = MLA_QK ** -0.5

    def body(q_ref, k_ref, v_ref, o_ref, lse_ref):
        def tile(i):
            keys = (i + 1) * tq
            causal, low = _attn_masks(tq, i)
            vp = v_ref[0:keys, :]
            acc = jnp.zeros((tq, LANE), F32)
            for hh in range(2):
                sl = slice(hh * HEAD_PAD, (hh + 1) * HEAD_PAD)
                s = _dot_nt(q_ref[:, sl], k_ref[0:keys, sl]) * scale
                s = jnp.where(causal, s, -jnp.inf)
                m = jnp.max(s, axis=-1, keepdims=True)
                e = jnp.exp(s - m)
                l = jnp.sum(e, axis=-1, keepdims=True)
                o = _dot(e.astype(BF16), vp) / l
                acc = jnp.where(low if hh == 0 else jnp.logical_not(low), o, acc)
                lse_ref[hh] = m + jnp.log(l)
            o_ref[...] = acc

        _for_each_query_tile(t // tq, tile)

    return pl.pallas_call(
        body, name="mla_attn_fwd",
        grid=(MLA_HEADS // 2, t // tq),
        in_specs=[pl.BlockSpec((tq, 2 * HEAD_PAD), lambda p, i: (i, p)),
                  pl.BlockSpec((t, 2 * HEAD_PAD), lambda p, i: (0, p)),
                  pl.BlockSpec((t, LANE), lambda p, i: (0, p))],
        out_specs=(pl.BlockSpec((tq, LANE), lambda p, i: (i, p)),
                   pl.BlockSpec((2, tq, 1), lambda p, i: (p, i, 0))),
        out_shape=(jax.ShapeDtypeStruct((t, MLA_WIDTH), F32),
                   jax.ShapeDtypeStruct((MLA_HEADS, t, 1), F32)),
        compiler_params=_cparams(("arbitrary", "arbitrary")),
    )(q, k, v)


def _mla_attn_bwd(q, k, v, o, do, lse, after):
    t = q.shape[0]
    tq = min(256, t)
    scale = MLA_QK ** -0.5

    def body(q_ref, k_ref, v_ref, o_ref, do_ref, lse_ref, after_ref, dq_ref, dk_ref, dv_ref):
        del after_ref

        @pl.when(pl.program_id(1) == 0)
        def _():
            dk_ref[...] = jnp.zeros_like(dk_ref)
            dv_ref[...] = jnp.zeros_like(dv_ref)

        def tile(i):
            keys = (i + 1) * tq
            causal, low = _attn_masks(tq, i)
            vp = v_ref[0:keys, :]
            do_all = do_ref[...]
            o_all = o_ref[...]
            dv_acc = jnp.zeros((keys, LANE), F32)
            for hh in range(2):
                sl = slice(hh * HEAD_PAD, (hh + 1) * HEAD_PAD)
                do_h = jnp.where(low if hh == 0 else jnp.logical_not(low), do_all, 0.0)
                dsum = jnp.sum(do_h * o_all, axis=-1, keepdims=True)
                qh = q_ref[:, sl]
                kh = k_ref[0:keys, sl]
                s = _dot_nt(qh, kh) * scale
                p = jnp.where(causal, jnp.exp(s - lse_ref[hh]), 0.0)
                do_b = do_h.astype(BF16)
                dp = _dot_nt(do_b, vp)
                ds = (p * (dp - dsum) * scale).astype(BF16)
                dq_ref[:, sl] = _dot(ds, kh).astype(BF16)
                dk_ref[0:keys, sl] += _dot_tn(ds, qh)
                dv_acc = dv_acc + _dot_tn(p.astype(BF16), do_b)
            dv_ref[0:keys, :] += dv_acc

        _for_each_query_tile(t // tq, tile)

    return pl.pallas_call(
        body, name="mla_attn_bwd",
        grid=(MLA_HEADS // 2, t // tq),
        in_specs=[pl.BlockSpec((tq, 2 * HEAD_PAD), lambda p, i: (i, p)),
                  pl.BlockSpec((t, 2 * HEAD_PAD), lambda p, i: (0, p)),
                  pl.BlockSpec((t, LANE), lambda p, i: (0, p)),
                  pl.BlockSpec((tq, LANE), lambda p, i: (i, p)),
                  pl.BlockSpec((tq, LANE), lambda p, i: (i, p)),
                  pl.BlockSpec((2, tq, 1), lambda p, i: (p, i, 0)),
                  pl.BlockSpec(memory_space=pl.ANY)],
        out_specs=(pl.BlockSpec((tq, 2 * HEAD_PAD), lambda p, i: (i, p)),
                   pl.BlockSpec((t, 2 * HEAD_PAD), lambda p, i: (0, p)),
                   pl.BlockSpec((t, LANE), lambda p, i: (0, p))),
        out_shape=(jax.ShapeDtypeStruct((t, MLA_HEADS * HEAD_PAD), BF16),
                   jax.ShapeDtypeStruct((t, MLA_HEADS * HEAD_PAD), F32),
                   jax.ShapeDtypeStruct((t, MLA_WIDTH), F32)),
        compiler_params=_cparams(("arbitrary", "arbitrary")),
    )(q, k, v, o, do, lse, after)


def _chunk_tri(n, lower):
    r = lax.broadcasted_iota(jnp.int32, (n, n), 0)
    c = lax.broadcasted_iota(jnp.int32, (n, n), 1)
    same = (r // GLA_CHUNK) == (c // GLA_CHUNK)
    return jnp.where(jnp.logical_and(same, r >= c if lower else r <= c), 1.0, 0.0).astype(F32)


def _gla_chunk_terms(q_ref, k_ref, b_ref, h):
    sl = slice(h * GLA_HK, (h + 1) * GLA_HK)
    b = b_ref[:, sl]
    bl = b[GLA_CHUNK - 1:GLA_CHUNK, :]
    kc = k_ref[:, sl]
    q_in = (q_ref[:, sl] * (GLA_HK ** -0.5)) * jnp.exp(b)
    k_in = kc * jnp.exp(-b)
    k_st = kc * jnp.exp(bl - b)
    return b, bl, q_in, k_in, k_st


def _tri(c, lower):
    r = lax.broadcasted_iota(jnp.int32, (c, c), 0)
    cc = lax.broadcasted_iota(jnp.int32, (c, c), 1)
    return jnp.where(r >= cc if lower else r <= cc, 1.0, 0.0).astype(F32)


def _gla_fwd(proj, log_a):
    t = proj.shape[0]
    c = GLA_CHUNK
    n = t // c

    def body(q_ref, k_ref, v_ref, la_ref, o_ref, sp_ref, st_ref):
        @pl.when(pl.program_id(0) == 0)
        def _():
            st_ref[...] = jnp.zeros_like(st_ref)

        tri = _tri(c, True)
        for h in range(GLA_HEADS):
            _, bl, q_in, k_in, k_st = _gla_chunk_terms(q_ref, k_ref, la_ref, h)
            vs = slice(h * GLA_HV, (h + 1) * GLA_HV)
            vv = v_ref[:, vs].astype(BF16)
            qb = q_in.astype(BF16)
            attn = _dot_nt(qb, k_in.astype(BF16)) * tri
            st = st_ref[h]
            sp_ref[0, h] = st
            o_ref[:, vs] = _dot(attn.astype(BF16), vv) + _dot_nt(qb, st.astype(BF16))
            st_ref[h] = st * jnp.exp(bl) + _dot_tn(vv, k_st.astype(BF16))

    return pl.pallas_call(
        body, name="gla_fwd",
        grid=(n,),
        in_specs=[pl.BlockSpec((c, GLA_DK), lambda i: (i, P_QG // GLA_DK)),
                  pl.BlockSpec((c, GLA_DK), lambda i: (i, P_KG // GLA_DK)),
                  pl.BlockSpec((c, GLA_DV), lambda i: (i, P_VG // GLA_DV)),
                  pl.BlockSpec((c, GLA_DK), lambda i: (i, 0))],
        out_specs=(pl.BlockSpec((c, GLA_DV), lambda i: (i, 0)),
                   pl.BlockSpec((1, GLA_HEADS, GLA_HV, GLA_HK), lambda i: (i, 0, 0, 0))),
        out_shape=(jax.ShapeDtypeStruct((t, GLA_DV), F32),
                   jax.ShapeDtypeStruct((n, GLA_HEADS, GLA_HV, GLA_HK), F32)),
        scratch_shapes=[pltpu.VMEM((GLA_HEADS, GLA_HV, GLA_HK), F32)],
        compiler_params=_cparams(("arbitrary",)),
    )(proj, proj, proj, log_a)


def _gla_bwd(proj, log_a, do, states, after):
    t = proj.shape[0]
    c = GLA_CHUNK
    n = t // c

    def body(q_ref, k_ref, v_ref, la_ref, do_ref, sp_ref, after_ref, dg_ref, dla_ref, ds_ref):
        del after_ref

        @pl.when(pl.program_id(0) == 0)
        def _():
            ds_ref[...] = jnp.zeros_like(ds_ref)

        tri = _tri(c, True)
        last = lax.broadcasted_iota(jnp.int32, (c, GLA_HK), 0) == c - 1
        for h in range(GLA_HEADS):
            b, bl, q_in, k_in, k_st = _gla_chunk_terms(q_ref, k_ref, la_ref, h)
            ks_ = slice(h * GLA_HK, (h + 1) * GLA_HK)
            vs = slice(h * GLA_HV, (h + 1) * GLA_HV)
            vv = v_ref[:, vs].astype(BF16)
            do_h = do_ref[:, vs]
            qb, kb, ksb = q_in.astype(BF16), k_in.astype(BF16), k_st.astype(BF16)
            attn = (_dot_nt(qb, kb) * tri).astype(BF16)
            st = sp_ref[0, h]
            dst = ds_ref[h]
            dstb = dst.astype(BF16)
            dattn = (_dot_nt(do_h, vv) * tri).astype(BF16)
            dg_ref[:, P_VG + h * GLA_HV:P_VG + (h + 1) * GLA_HV] = (
                _dot_tn(attn, do_h) + _dot_nt(ksb, dstb)).astype(BF16)
            dq_in = _dot(dattn, kb) + _dot(do_h, st.astype(BF16))
            dk_in = _dot_tn(dattn, qb)
            dk_st = _dot(vv, dstb)
            ebl = jnp.exp(bl)
            d_ebl = jnp.sum(st * dst, axis=0, keepdims=True)
            ds_ref[h] = _dot_tn(do_h, qb) + dst * ebl
            dg_ref[:, P_QG + h * GLA_HK:P_QG + (h + 1) * GLA_HK] = (
                dq_in * (GLA_HK ** -0.5) * jnp.exp(b)).astype(BF16)
            dg_ref[:, P_KG + h * GLA_HK:P_KG + (h + 1) * GLA_HK] = (
                dk_in * jnp.exp(-b) + dk_st * jnp.exp(bl - b)).astype(BF16)
            db = dq_in * q_in - dk_in * k_in - dk_st * k_st
            dbl = jnp.sum(dk_st * k_st, axis=0, keepdims=True) + d_ebl * ebl
            dla_ref[:, ks_] = db + jnp.where(last, dbl, 0.0)

    rev = lambda i: n - 1 - i
    gw = P_GROUPS[0][1]
    return pl.pallas_call(
        body, name="gla_bwd",
        grid=(n,),
        in_specs=[pl.BlockSpec((c, GLA_DK), lambda i: (rev(i), P_QG // GLA_DK)),
                  pl.BlockSpec((c, GLA_DK), lambda i: (rev(i), P_KG // GLA_DK)),
                  pl.BlockSpec((c, GLA_DV), lambda i: (rev(i), P_VG // GLA_DV)),
                  pl.BlockSpec((c, GLA_DK), lambda i: (rev(i), 0)),
                  pl.BlockSpec((c, GLA_DV), lambda i: (rev(i), 0)),
                  pl.BlockSpec((1, GLA_HEADS, GLA_HV, GLA_HK), lambda i: (rev(i), 0, 0, 0)),
                  pl.BlockSpec(memory_space=pl.ANY)],
        out_specs=(pl.BlockSpec((c, gw), lambda i: (rev(i), 0)),
                   pl.BlockSpec((c, GLA_DK), lambda i: (rev(i), 0))),
        out_shape=(jax.ShapeDtypeStruct((t, gw), BF16), jax.ShapeDtypeStruct((t, GLA_DK), F32)),
        scratch_shapes=[pltpu.VMEM((GLA_HEADS, GLA_HV, GLA_HK), F32)],
        compiler_params=_cparams(("arbitrary",)),
    )(proj, proj, proj, log_a, do, states, after)


def _post(o_mla, proj, o_gla, x, target, g_gla, g_final, w_pm, w_pg, w_o):
    t = x.shape[0]
    tm = min(128, t)
    g0, gw = P_GROUPS[1]

    def body(om_ref, zg_ref, gm_ref, gg_ref, zm_ref, og_ref, x_ref, tg_ref, ggla_ref, gf_ref,
             wpm_ref, wpg_ref, wo_ref,
             dx2_ref, dom_ref, dog_ref, dg_ref,
             mg_ref, um_ref, ug_ref, dym_ref, dyg_ref, loss_ref, dgf_ref, dggla_ref):
        @pl.when(pl.program_id(0) == 0)
        def _():
            loss_ref[...] = jnp.zeros_like(loss_ref)
            dgf_ref[...] = jnp.zeros_like(dgf_ref)
            dggla_ref[...] = jnp.zeros_like(dggla_ref)

        om = om_ref[...]
        zm = zm_ref[...]
        sm = _sigmoid(zm)
        silu_m = zm * sm
        um = (om * silu_m).astype(BF16)
        um_ref[...] = um
        ym = _dot(um, wpm_ref[...])

        ggla = ggla_ref[...]
        zg = zg_ref[...]
        sg = _sigmoid(zg)
        silu_g = zg * sg
        xhat, rstd, on = [], [], []
        for h in range(GLA_HEADS):
            blk = og_ref[:, h * GLA_HV:(h + 1) * GLA_HV]
            r = lax.rsqrt(jnp.mean(blk * blk, axis=-1, keepdims=True) + EPS)
            xhat.append(blk * r)
            rstd.append(r)
            on.append(xhat[h] * ggla)
        on = jnp.concatenate(on, axis=-1)
        ug = (on * silu_g).astype(BF16)
        ug_ref[...] = ug
        yg = _dot(ug, wpg_ref[...])

        sgm = _sigmoid(gm_ref[...])
        sgg = _sigmoid(gg_ref[...])
        merged = (sgm * ym + sgg * yg).astype(BF16)
        mg_ref[...] = merged
        x2 = x_ref[...] + _dot(merged, wo_ref[...])
        gf = gf_ref[...]
        rf = lax.rsqrt(jnp.mean(x2 * x2, axis=-1, keepdims=True) + EPS)
        xh = x2 * rf
        err = xh * gf - tg_ref[...]
        loss_ref[...] += 0.5 * jnp.sum(jnp.mean(err * err, axis=-1, keepdims=True))

        dy = err * (1.0 / D_MODEL)
        dgf_ref[...] += jnp.sum(dy * xh, axis=0, keepdims=True)
        dxh = dy * gf
        dx2 = rf * (dxh - xh * jnp.mean(dxh * xh, axis=-1, keepdims=True))
        dx2_ref[...] = dx2
        dmerged = _dot_nt(dx2.astype(BF16), wo_ref[...])
        dym = (dmerged * sgm).astype(BF16)
        dyg = (dmerged * sgg).astype(BF16)
        dym_ref[...] = dym
        dyg_ref[...] = dyg
        dg_ref[:, P_GMLA - g0:P_GMLA - g0 + D_MODEL] = (dmerged * ym * sgm * (1.0 - sgm)).astype(BF16)
        dg_ref[:, P_GGLA - g0:P_GGLA - g0 + D_MODEL] = (dmerged * yg * sgg * (1.0 - sgg)).astype(BF16)
        dum = _dot_nt(dym, wpm_ref[...])
        dom_ref[...] = dum * silu_m
        dg_ref[:, P_ZMLA - g0:P_ZMLA - g0 + MLA_WIDTH] = (
            dum * om * (sm * (1.0 + zm * (1.0 - sm)))).astype(BF16)
        dug = _dot_nt(dyg, wpg_ref[...])
        dg_ref[:, P_ZGLA - g0:P_ZGLA - g0 + GLA_DV] = (
            dug * on * (sg * (1.0 + zg * (1.0 - sg)))).astype(BF16)
        don = dug * silu_g
        dggla = jnp.zeros((1, GLA_HV), F32)
        for h in range(GLA_HEADS):
            hs = slice(h * GLA_HV, (h + 1) * GLA_HV)
            don_h = don[:, hs]
            dggla = dggla + jnp.sum(don_h * xhat[h], axis=0, keepdims=True)
            dxh_h = don_h * ggla
            dog_ref[:, hs] = (rstd[h] * (dxh_h - xhat[h] * jnp.mean(dxh_h * xhat[h], axis=-1,
                                                                     keepdims=True))).astype(BF16)
        dggla_ref[...] += dggla

    row = lambda w: pl.BlockSpec((tm, w), lambda i: (i, 0))
    pcol = lambda w, off: pl.BlockSpec((tm, w), lambda i: (i, off // w))
    full = lambda a: pl.BlockSpec(a.shape, lambda i: (0, 0))
    sds = jax.ShapeDtypeStruct
    return pl.pallas_call(
        body, name="post_fwd_bwd",
        grid=(t // tm,),
        in_specs=[row(MLA_WIDTH), pcol(GLA_DV, P_ZGLA), pcol(D_MODEL, P_GMLA), pcol(D_MODEL, P_GGLA),
                  pcol(MLA_WIDTH, P_ZMLA), row(GLA_DV), row(D_MODEL), row(D_MODEL),
                  full(g_gla), full(g_final), full(w_pm), full(w_pg), full(w_o)],
        out_specs=(row(D_MODEL), row(MLA_WIDTH), row(GLA_DV), row(gw),
                   row(D_MODEL), row(MLA_WIDTH), row(GLA_DV), row(D_MODEL), row(D_MODEL),
                   pl.BlockSpec((1, LANE), lambda i: (0, 0)),
                   pl.BlockSpec((1, D_MODEL), lambda i: (0, 0)),
                   pl.BlockSpec((1, GLA_HV), lambda i: (0, 0))),
        out_shape=(sds((t, D_MODEL), F32), sds((t, MLA_WIDTH), F32), sds((t, GLA_DV), BF16),
                   sds((t, gw), BF16),
                   sds((t, D_MODEL), BF16), sds((t, MLA_WIDTH), BF16), sds((t, GLA_DV), BF16),
                   sds((t, D_MODEL), BF16), sds((t, D_MODEL), BF16),
                   sds((1, LANE), F32), sds((1, D_MODEL), F32), sds((1, GLA_HV), F32)),
        compiler_params=_cparams(("arbitrary",)),
    )(o_mla, proj, proj, proj, proj, o_gla, x, target, g_gla, g_final, w_pm, w_pg, w_o)


def _mla_prep_bwd(dq, dk, dv, dla, pre, proj, rq, rkv, g_q, g_kv, w_uq_p, w_k_p, w_v, w_gate_p,
                  rc, rsn, rsp):
    t = proj.shape[0]
    tm = min(256, t)
    gw = P_GROUPS[2][1]

    def body(dq_ref, dk_ref, dv_ref, dla_ref, pre_ref, cq_ref, ckv_ref, rq_ref, rkv_ref,
             gq_ref, gkv_ref, wuq_ref, wk_ref, wv_ref, wg_ref, c_ref, sn_ref, sp_ref,
             dg_ref, dqpre_ref, dpre_ref, dgq_ref, dgkv_ref, dbg_ref):
        @pl.when(pl.program_id(0) == 0)
        def _():
            dgq_ref[...] = jnp.zeros_like(dgq_ref)
            dgkv_ref[...] = jnp.zeros_like(dgkv_ref)
            dbg_ref[...] = jnp.zeros_like(dbg_ref)

        c, sn, sp = c_ref[...], sn_ref[...], sp_ref[...]
        dkr = jnp.zeros((tm, LANE), F32)
        for h in range(MLA_HEADS):
            sl = slice(h * HEAD_PAD, (h + 1) * HEAD_PAD)
            dqpre_ref[:, sl] = _rope_bwd(dq_ref[:, sl].astype(F32), c, sn, sp).astype(BF16)
            dkr = dkr + dk_ref[:, sl]
        dcqn = _dot_nt(dqpre_ref[...], wuq_ref[...])
        rq = rq_ref[...]
        xh = cq_ref[:, :MLA_Q_RANK] * rq
        dgq_ref[...] += jnp.sum(dcqn * xh, axis=0, keepdims=True)
        dxh = dcqn * gq_ref[...]
        dcq = rq * (dxh - xh * jnp.mean(dxh * xh, axis=-1, keepdims=True))
        dg_ref[:, :MLA_Q_RANK] = dcq.astype(BF16)
        dg_ref[:, MLA_Q_RANK:512] = jnp.zeros((tm, 512 - MLA_Q_RANK), BF16)

        dckvn = _dot_nt(dk_ref[...].astype(BF16), wk_ref[...]) + \
            _dot_nt(dv_ref[...].astype(BF16), wv_ref[...])
        rkv = rkv_ref[...]
        xh = ckv_ref[...] * rkv
        dgkv_ref[...] += jnp.sum(dckvn * xh, axis=0, keepdims=True)
        dxh = dckvn * gkv_ref[...]
        dg_ref[:, P_CKV - P_CQ:P_CKV - P_CQ + MLA_KV_RANK] = (
            rkv * (dxh - xh * jnp.mean(dxh * xh, axis=-1, keepdims=True))).astype(BF16)

        dlog_a = _dot_exact(_chunk_tri(tm, False), dla_ref[...])
        dpre = dlog_a * (1.0 / GLA_GATE_NORM) * (1.0 - _sigmoid(pre_ref[...]))
        dbg_ref[...] += jnp.sum(dpre, axis=0, keepdims=True)
        dpre = dpre.astype(BF16)
        dpre_ref[...] = dpre
        lane = lax.broadcasted_iota(jnp.int32, (tm, LANE), 1)
        in_kr = jnp.logical_and(lane >= MISC_KR, lane < MISC_KR + MLA_ROPE)
        dmisc = jnp.where(in_kr, _rope_bwd(dkr, c, sn, sp), 0.0) + _dot_nt(dpre, wg_ref[...])
        dg_ref[:, P_MISC - P_CQ:P_MISC - P_CQ + LANE] = dmisc.astype(BF16)

    hq = MLA_HEADS * HEAD_PAD
    row = lambda w: pl.BlockSpec((tm, w), lambda i: (i, 0))
    full = lambda a: pl.BlockSpec(a.shape, lambda i: (0, 0))
    acc = lambda w: pl.BlockSpec((1, w), lambda i: (0, 0))
    sds = jax.ShapeDtypeStruct
    return pl.pallas_call(
        body, name="mla_prep_bwd",
        grid=(t // tm,),
        in_specs=[row(hq), row(hq), row(MLA_WIDTH), row(GLA_DK), row(GLA_DK),
                  pl.BlockSpec((tm, 512), lambda i: (i, P_CQ // 512)),
                  pl.BlockSpec((tm, MLA_KV_RANK), lambda i: (i, P_CKV // MLA_KV_RANK)),
                  row(1), row(1), full(g_q), full(g_kv), full(w_uq_p), full(w_k_p), full(w_v),
                  full(w_gate_p), row(LANE), row(LANE), row(LANE)],
        out_specs=(row(gw), row(hq), row(GLA_DK),
                   acc(MLA_Q_RANK), acc(MLA_KV_RANK), acc(GLA_DK)),
        out_shape=(sds((t, gw), BF16), sds((t, hq), BF16), sds((t, GLA_DK), BF16),
                   sds((1, MLA_Q_RANK), F32), sds((1, MLA_KV_RANK), F32), sds((1, GLA_DK), F32)),
        compiler_params=_cparams(("arbitrary",)),
    )(dq, dk, dv, dla, pre, proj, proj, rq, rkv, g_q, g_kv, w_uq_p, w_k_p, w_v, w_gate_p,
      rc, rsn, rsp)


def _inproj_bwd(dgroups, w_pt, x, rstd, g_in, dx2, after):
    t = x.shape[0]
    tm = min(256, t)

    def body(d0_ref, d1_ref, d2_ref, w_ref, x_ref, r_ref, g_ref, dx2_ref, after_ref, dx_ref, dg_ref):
        del after_ref

        @pl.when(pl.program_id(0) == 0)
        def _():
            dg_ref[...] = jnp.zeros_like(dg_ref)

        dh = jnp.zeros((tm, D_MODEL), F32)
        for d_ref, (off, width) in zip((d0_ref, d1_ref, d2_ref), P_GROUPS):
            dh = dh + _dot(d_ref[...], w_ref[off:off + width, :])
        r = r_ref[...]
        xh = x_ref[...] * r
        dg_ref[...] += jnp.sum(dh * xh, axis=0, keepdims=True)
        dxh = dh * g_ref[...]
        dx_ref[...] = dx2_ref[...] + r * (dxh - xh * jnp.mean(dxh * xh, axis=-1, keepdims=True))

    row = lambda w: pl.BlockSpec((tm, w), lambda i: (i, 0))
    return pl.pallas_call(
        body, name="inproj_bwd",
        grid=(t // tm,),
        in_specs=[row(P_GROUPS[0][1]), row(P_GROUPS[1][1]), row(P_GROUPS[2][1]),
                  pl.BlockSpec((P_TOTAL, D_MODEL), lambda i: (0, 0)),
                  row(D_MODEL), row(1), pl.BlockSpec((1, D_MODEL), lambda i: (0, 0)), row(D_MODEL),
                  pl.BlockSpec(memory_space=pl.ANY)],
        out_specs=(row(D_MODEL), pl.BlockSpec((1, D_MODEL), lambda i: (0, 0))),
        out_shape=(jax.ShapeDtypeStruct((t, D_MODEL), F32),
                   jax.ShapeDtypeStruct((1, D_MODEL), F32)),
        compiler_params=_cparams(("arbitrary",)),
    )(*dgroups, w_pt, x, rstd, g_in, dx2, after)


def _matmul(name, a, b, tm, tn, dtype=F32):
    kk, m = a.shape
    n = b.shape[1]

    def body(a_ref, b_ref, o_ref):
        o_ref[...] = _dot_tn(a_ref[...].astype(BF16), b_ref[...].astype(BF16)).astype(dtype)

    return pl.pallas_call(
        body, name=name,
        grid=(n // tn, m // tm),
        in_specs=[pl.BlockSpec((kk, tm), lambda j, i: (0, i)),
                  pl.BlockSpec((kk, tn), lambda j, i: (0, j))],
        out_specs=pl.BlockSpec((tm, tn), lambda j, i: (i, j)),
        out_shape=jax.ShapeDtypeStruct((m, n), dtype),
        compiler_params=_cparams(("arbitrary", "arbitrary")),
    )(a, b)


def _adamw_update(part_refs, w_ref, m_ref, v_ref, g_ref, d_ref, nm_ref, nv_ref):
    g = part_refs[0][...].astype(F32)
    for p_ref in part_refs[1:]:
        g = g + p_ref[...].astype(F32)
    m_new = ADAM_B1 * m_ref[...] + (1.0 - ADAM_B1) * g
    v_new = ADAM_B2 * v_ref[...] + (1.0 - ADAM_B2) * (g * g)
    m_hat = m_new / (1.0 - ADAM_B1 ** ADAM_STEP)
    v_hat = v_new / (1.0 - ADAM_B2 ** ADAM_STEP)
    g_ref[...] = g
    nm_ref[...] = m_new
    nv_ref[...] = v_new
    d_ref[...] = -ADAM_LR * (m_hat / (jnp.sqrt(v_hat) + ADAM_EPS) + ADAM_WD * w_ref[...])


def _adamw_rows(name, parts, w, m, v, tr, first=None):
    _, rows, cols = w.shape
    slots = parts.shape[0]

    def body(*refs):
        lead_refs, p_ref = ([], refs[0]) if first is None else ([refs[0]], refs[1])
        _adamw_update(lead_refs + [p_ref.at[q] for q in range(slots)], *refs[len(lead_refs) + 1:])

    blk = pl.BlockSpec((None, tr, cols), lambda i: (0, i, 0))
    out = jax.ShapeDtypeStruct((1, rows, cols), F32)
    lead = [] if first is None else [pl.BlockSpec((tr, cols), lambda i: (i, 0))]
    return pl.pallas_call(
        body, name=name,
        grid=(rows // tr,),
        in_specs=lead + [pl.BlockSpec((slots, tr, cols), lambda i: (0, i, 0)), blk, blk, blk],
        out_specs=(blk, blk, blk, blk),
        out_shape=(out, out, out, out),
        compiler_params=_cparams(("arbitrary",)),
    )(*([] if first is None else [first]), parts, w, m, v)


def _adamw_transposed(name, first, parts, w, m, v, tl):
    _, rows, cols = w.shape
    slots = parts.shape[0]

    def body(f_ref, p_ref, *refs):
        _adamw_update([f_ref] + [p_ref.at[q] for q in range(slots)], *refs)

    blk = pl.BlockSpec((cols, None, tl), lambda i: (0, 0, i))
    out = jax.ShapeDtypeStruct((cols, 1, rows), F32)
    res = pl.pallas_call(
        body, name=name,
        grid=(rows // tl,),
        in_specs=[pl.BlockSpec((cols, tl), lambda i: (0, i)),
                  pl.BlockSpec((slots, cols, tl), lambda i: (0, 0, i)), blk, blk, blk],
        out_specs=(blk, blk, blk, blk),
        out_shape=(out, out, out, out),
        compiler_params=_cparams(("arbitrary",)),
    )(first, parts, *[a.transpose(2, 0, 1) for a in (w, m, v)])
    return [r.transpose(1, 2, 0) for r in res]


def _adamw_group(firsts, parts, ws, ms, vs):
    n = len(ws)

    def body(*refs):
        ins, outs = refs[:5 * n], refs[5 * n:]
        for a in range(n):
            _adamw_update([ins[a]] + [ins[n + a].at[q] for q in range(ins[n + a].shape[0])],
                          *[r.at[0] for r in (ins[2 * n + a], ins[3 * n + a], ins[4 * n + a])],
                          *[r.at[0] for r in outs[4 * a:4 * a + 4]])

    vmem = lambda k: [pl.BlockSpec(memory_space=pltpu.VMEM) for _ in range(k)]
    out_shape = []
    for w in ws:
        out_shape += [jax.ShapeDtypeStruct(w.shape, F32)] * 4
    res = pl.pallas_call(
        body, name="adamw_small_weights",
        in_specs=vmem(5 * n), out_specs=tuple(vmem(4 * n)), out_shape=tuple(out_shape),
        compiler_params=_cparams(),
    )(*firsts, *parts, *ws, *ms, *vs)
    return [res[4 * a:4 * a + 4] for a in range(n)]


def _rope_tables(positions):
    half = MLA_ROPE // 2
    freqs = ROPE_THETA ** (-jnp.arange(half, dtype=F32) / half)
    ang = positions.astype(F32).reshape(-1, 1) * freqs
    cos, sin = jnp.cos(ang), jnp.sin(ang)
    t = ang.shape[0]
    one, zero = jnp.ones((t, MLA_NOPE), F32), jnp.zeros((t, half), F32)
    tail = jnp.zeros((t, LANE - MLA_QK), F32)
    rc = jnp.concatenate([one, cos, cos, tail], axis=1)
    rsn = jnp.concatenate([0.0 * one, -sin, zero, tail], axis=1)
    rsp = jnp.concatenate([0.0 * one, zero, sin, tail], axis=1)
    return rc, rsn, rsp


def _cols_full(g):
    return g.transpose(1, 0, 2)


def kernel(x, positions, g_in, w_in, g_q, w_uq, g_kv, w_ukv, w_gla_gate, b_gla_gate, g_gla, w_proj_mla, w_proj_gla, w_out, g_final, loss_target, m_g_in, m_w_in, m_g_q, m_w_uq, m_g_kv, m_w_ukv, m_w_gla_gate, m_b_gla_gate, m_g_gla, m_w_proj_mla, m_w_proj_gla, m_w_out, m_g_final, v_g_in, v_w_in, v_g_q, v_w_uq, v_g_kv, v_w_ukv, v_w_gla_gate, v_b_gla_gate, v_g_gla, v_w_proj_mla, v_w_proj_gla, v_w_out, v_g_final):
    t = x.shape[1]
    x2d = x.reshape(t, D_MODEL)
    tgt = loss_target.reshape(t, D_MODEL)
    g_final2 = g_final.reshape(1, D_MODEL)
    sharded = [(w_in, m_w_in, v_w_in), (w_uq, m_w_uq, v_w_uq), (w_ukv, m_w_ukv, v_w_ukv),
               (w_gla_gate, m_w_gla_gate, v_w_gla_gate), (w_proj_mla, m_w_proj_mla, v_w_proj_mla),
               (w_proj_gla, m_w_proj_gla, v_w_proj_gla), (w_out, m_w_out, v_w_out)]

    w_in_t = w_in.transpose(2, 0, 1).reshape(SHARD_COLS, D_MODEL)
    g_w_in, g_uq, g_ukv, g_gate, g_pm, g_pg, g_o = _all_gather(
        [w_in_t.astype(BF16)] + [s[0][0].astype(BF16) for s in sharded[1:]])
    w_in_p = _weights_to_p(g_w_in)
    w_uq_p = jnp.pad(_cols_full(g_uq), ((0, 0), (0, 0), (0, HEAD_PAD - MLA_QK))).reshape(
        MLA_Q_RANK, MLA_HEADS * HEAD_PAD)
    ukv = _cols_full(g_ukv)
    w_k_p = jnp.pad(ukv[:, :, :MLA_NOPE], ((0, 0), (0, 0), (0, HEAD_PAD - MLA_NOPE))).reshape(
        MLA_KV_RANK, MLA_HEADS * HEAD_PAD)
    w_v = ukv[:, :, MLA_NOPE:].reshape(MLA_KV_RANK, MLA_WIDTH)
    w_gate_p = jnp.pad(_cols_full(g_gate).reshape(GLA_GATE_RANK, GLA_DK),
                       ((MISC_ALR, LANE - MISC_ALR - GLA_GATE_RANK), (0, 0)))
    w_pm = _cols_full(g_pm).reshape(MLA_WIDTH, D_MODEL)
    w_pg = g_pg.reshape(GLA_DV, D_MODEL)
    w_o = g_o.reshape(D_MODEL, D_MODEL)
    rc, rsn, rsp = _rope_tables(positions)

    proj, h, rstd = _inproj(x2d, g_in, w_in_p)
    q, k, v, log_a, pre, cqn, ckvn, rq, rkv, misc = _mla_prep(
        proj, g_q, g_kv, w_uq_p, w_k_p, w_v, w_gate_p, b_gla_gate, rc, rsn, rsp)
    o_mla, lse = _mla_attn_fwd(q, k, v)
    o_gla, states = _gla_fwd(proj, log_a)

    (dx2, do_mla, do_gla, d_out, merged, um, ug, dym, dyg, loss_p, dg_final,
     dg_gla) = _post(o_mla, proj, o_gla, x2d, tgt, g_gla, g_final2, w_pm, w_pg, w_o)

    everyone = tuple(range(N_DEV))
    p_pm = _matmul("dw_proj_mla", um, dym, 512, 512, BF16).reshape(
        MLA_WIDTH, N_DEV, D_MODEL // N_DEV).transpose(1, 0, 2)
    p_pg = _matmul("dw_proj_gla", ug, dyg, 512, 512, BF16).reshape(N_DEV, -1, D_MODEL)
    p_o = _matmul("dw_out", merged, dx2, 512, 512, BF16).reshape(N_DEV, -1, D_MODEL)
    owns = [lax.empty((SHARD_COLS, D_MODEL), BF16)] + [lax.empty(p.shape[1:], BF16)
                                                       for p in (p_pm, p_pg, p_o)]
    lands = [lax.empty((3,) + o.shape, BF16) for o in owns]
    dw_groups, started = {}, []

    def reduce_scatter_stage(s, dests, extra=()):
        assert set(g for d in dests for g in _shard_groups(d)) <= set(dw_groups)
        k = 1 + len(extra)
        parts = [_grads_to_shards("grads_to_shards_%d" % s, dw_groups, dests)] + list(extra)
        all_dests = [dests] + [everyone] * len(extra)
        sums, owns[:k] = _swap_add("swap_add_%d" % s, parts, owns[:k], all_dests)
        sems, sums, lands[:k], token = _ici_start("ici_start_%d" % s, sums, lands[:k], all_dests)
        started.extend((sems[a][0], sems[a][1], sums[a], a, all_dests[a]) for a in range(k))
        return token

    dw_groups[1] = _matmul("dw_in_1", d_out, h, 512, 512, BF16)
    token = reduce_scatter_stage(1, (5, 6, 7), (p_pm, p_pg, p_o))
    d_gla, dla = _gla_bwd(proj, log_a, do_gla, states, token)
    dw_groups[0] = _matmul("dw_in_0", d_gla, h, 512, 512, BF16)
    token = reduce_scatter_stage(2, (1, 2, 3))
    dq, dk, dv = _mla_attn_bwd(q, k, v, o_mla, do_mla, lse, token)
    d_lat, dqpre, dpre, dg_q, dg_kv, db_gate = _mla_prep_bwd(
        dq, dk, dv, dla, pre, proj, rq, rkv, g_q, g_kv, w_uq_p, w_k_p, w_v, w_gate_p, rc, rsn, rsp)
    dw_groups[2] = _matmul("dw_in_2", d_lat, h, 896, 512, BF16)
    token = reduce_scatter_stage(3, (0, 4))
    grad_x, dg_in = _inproj_bwd((d_gla, d_out, d_lat), w_in_p, x2d, rstd, g_in, dx2, token)

    dw_uq = _matmul("dw_uq", cqn, dqpre, MLA_Q_RANK, 512, BF16)
    p_uq = dw_uq.reshape(MLA_Q_RANK, MLA_HEADS, HEAD_PAD)[:, :, :MLA_QK].transpose(1, 0, 2)
    dw_k = _matmul("dw_uk", ckvn, dk, MLA_KV_RANK, 512, BF16)
    dw_v = _matmul("dw_uv", ckvn, dv, MLA_KV_RANK, 512, BF16)
    p_ukv = jnp.concatenate(
        [dw_k.reshape(MLA_KV_RANK, MLA_HEADS, HEAD_PAD)[:, :, :MLA_NOPE],
         dw_v.reshape(MLA_KV_RANK, MLA_HEADS, MLA_VDIM)], axis=2).transpose(1, 0, 2)
    dw_gate = _matmul("dw_gate", misc, dpre, LANE, 512, BF16)
    p_gate = dw_gate[MISC_ALR:MISC_ALR + GLA_GATE_RANK].reshape(
        GLA_GATE_RANK, N_DEV, GLA_DK // N_DEV).transpose(1, 0, 2)
    small = jnp.concatenate([dg_in.reshape(-1), dg_q.reshape(-1), dg_kv.reshape(-1),
                             db_gate.reshape(-1), dg_gla.reshape(-1), dg_final.reshape(-1),
                             loss_p[0, :1]])
    small = jnp.pad(small, (0, SMALL_ROWS * LANE - small.shape[0])).reshape(SMALL_ROWS, LANE)

    recv = _exchange_grads([p_uq, p_ukv, p_gate], small)
    lands = _ici_wait("ici_wait", started, lands, recv[3])
    big = [_adamw_transposed("adamw_w_in", owns[0], lands[0], *sharded[0], 256)]
    big += _adamw_group([r[0] for r in recv[:3]] + list(owns[1:]),
                        [r[1:] for r in recv[:3]] + list(lands[1:]),
                        *[[s[j] for s in sharded[1:]] for j in range(3)])
    replicated = [(g_in, m_g_in, v_g_in), (g_q, m_g_q, v_g_q), (g_kv, m_g_kv, v_g_kv),
                  (b_gla_gate, m_b_gla_gate, v_b_gla_gate), (g_gla, m_g_gla, v_g_gla),
                  (g_final, m_g_final, v_g_final)]
    spacks = [jnp.pad(jnp.concatenate([s[j].reshape(-1) for s in replicated]),
                      (0, SMALL_ROWS * LANE - sum(SMALL_SIZES))).reshape(1, SMALL_ROWS, LANE)
              for j in range(3)]
    tiny = _adamw_rows("adamw_gains", recv[3], spacks[0], spacks[1], spacks[2], SMALL_ROWS)

    outs = {}
    names = ("w_in", "w_uq", "w_ukv", "w_gla_gate", "w_proj_mla", "w_proj_gla", "w_out")
    for j, kind in enumerate(("grad", "delta", "new_m", "new_v")):
        for name, res in zip(names, big):
            outs[kind, name] = res[j]
        flat = tiny[j].reshape(-1)
        off = 0
        for name, size in zip(("g_in", "g_q", "g_kv", "b_gla_gate", "g_gla", "g_final"), SMALL_SIZES):
            shape = (size,) if name == "g_final" else (1, size)
            outs[kind, name] = flat[off:off + size].reshape(shape)
            off += size
    loss = tiny[0].reshape(-1)[sum(SMALL_SIZES)]
    order = ("g_in", "w_in", "g_q", "w_uq", "g_kv", "w_ukv", "w_gla_gate", "b_gla_gate", "g_gla",
             "w_proj_mla", "w_proj_gla", "w_out", "g_final")
    result = [loss, grad_x.reshape(1, t, D_MODEL)]
    for kind in ("grad", "delta", "new_m", "new_v"):
        result += [outs[kind, name] for name in order]
    return tuple(result)
```

```python
import math

import jax
import jax.numpy as jnp
from jax import lax
from jax.experimental import pallas as pl
from jax.experimental.pallas import tpu as pltpu

F32 = jnp.float32
BF16 = jnp.bfloat16
MESH = pl.DeviceIdType.MESH
N_DEV = 8

D_MODEL = 1024
EPS = 1e-6
MLA_HEADS = 8
MLA_NOPE = 64
MLA_ROPE = 32
MLA_VDIM = 64
MLA_Q_RANK = 384
MLA_KV_RANK = 256
MLA_QK = MLA_NOPE + MLA_ROPE
MLA_WIDTH = MLA_HEADS * MLA_VDIM
ROPE_THETA = 10000.0
GLA_HEADS = 4
GLA_DK = 512
GLA_DV = 1024
GLA_HK = 128
GLA_HV = 256
GLA_GATE_RANK = 16
GLA_GATE_NORM = 16.0
GLA_CHUNK = 64
D_IN = 6320

ADAM_LR = 0.001
ADAM_B1 = 0.9
ADAM_B2 = 0.999
ADAM_EPS = 1e-08
ADAM_WD = 0.01
ADAM_STEP = 10

LANE = 128
HEAD_PAD = 128
VMEM_LIMIT = 48 * 1024 * 1024

P_VG, P_QG, P_KG = 0, 1024, 1536
P_ZGLA, P_GMLA, P_GGLA, P_ZMLA = 2048, 3072, 4096, 5120
P_CQ, P_CKV, P_MISC = 5632, 6144, 6400
P_TOTAL = 6528
P_GROUPS = ((0, 2048), (2048, 3584), (5632, 896))
MISC_KR = 64
MISC_ALR = 96
SHARD_COLS = D_IN // N_DEV
P_COMPONENTS = ((0, 384, P_CQ), (384, 256, P_CKV), (640, 32, P_MISC + MISC_KR), (672, 512, P_ZMLA),
                (1184, 512, P_QG), (1696, 512, P_KG), (2208, 1024, P_VG),
                (3232, 16, P_MISC + MISC_ALR), (3248, 1024, P_ZGLA), (4272, 1024, P_GMLA),
                (5296, 1024, P_GGLA))

SMALL_SIZES = (1024, 384, 256, 512, 256, 1024)
SMALL_ROWS = 32


def _segments():
    segs = []
    for g0, n, p0 in P_COMPONENTS:
        g = g0
        while g < g0 + n:
            d = g // SHARD_COLS
            end = min(g0 + n, (d + 1) * SHARD_COLS)
            segs.append((d, g - d * SHARD_COLS, end - g, p0 + g - g0))
            g = end
    return segs


def _group_of(p0):
    return max(i for i, (off, _) in enumerate(P_GROUPS) if off <= p0)


def _rel(p0):
    return p0 - P_GROUPS[_group_of(p0)][0]


def _cparams(sem=None):
    if sem is None:
        return pltpu.CompilerParams(vmem_limit_bytes=VMEM_LIMIT)
    return pltpu.CompilerParams(dimension_semantics=sem, vmem_limit_bytes=VMEM_LIMIT)


def _sigmoid(v):
    return 1.0 / (1.0 + jnp.exp(-v))


def _dot(a, b):
    return jnp.dot(a, b, preferred_element_type=F32)


def _dot_nt(a, b):
    return lax.dot_general(a, b, (((1,), (1,)), ((), ())), preferred_element_type=F32)


def _dot_tn(a, b):
    return lax.dot_general(a, b, (((0,), (0,)), ((), ())), preferred_element_type=F32)


def _dot_exact(a, b):
    return jnp.dot(a, b, preferred_element_type=F32, precision=lax.Precision.HIGHEST)


def _rope_fwd(blk, c, sn, sp):
    return blk * c + pltpu.roll(blk, LANE - 16, 1) * sn + pltpu.roll(blk, 16, 1) * sp


def _rope_bwd(blk, c, sn, sp):
    return blk * c + pltpu.roll(blk * sn, 16, 1) + pltpu.roll(blk * sp, LANE - 16, 1)


def _mesh_pos():
    return lax.axis_index("x"), lax.axis_index("y"), lax.axis_index("c")


def _hbm_specs(n):
    return [pl.BlockSpec(memory_space=pltpu.HBM) for _ in range(n)]


def _dev(d):
    return d >> 2, (d >> 1) & 1, d & 1


def _gather_plan(shards, sources):
    na, most = len(shards), max(len(s) for s in sources)
    out_shape = [jax.ShapeDtypeStruct((len(srcs),) + s.shape, s.dtype)
                 for s, srcs in zip(shards, sources)]
    sems = [pltpu.SemaphoreType.DMA((na, most)) for _ in range(3)]
    sems += [pltpu.SemaphoreType.DMA((na, most, 3))]
    sems += [pltpu.SemaphoreType.DMA((na, most)) for _ in range(3)]
    return out_shape, sems


def _gather_hooks(x_refs, out_refs, sems, sources):
    local_sems, d2d_send, d2d_recv, ici_send, ici_recv, fwd_send, fwd_recv = sems
    x, y, c = _mesh_pos()
    chips = [(1 - x, y), (x, 1 - y), (1 - x, 1 - y)]
    items = []
    for a, srcs in enumerate(sources):
        for i, d in enumerate(srcs):
            dx, dy, dc = _dev(d)
            near = jnp.logical_and(x == dx, y == dy)
            far = jnp.logical_not(near)
            slot = out_refs[a].at[i]

            def remote(src, to, send_sem, recv_sem, slot=slot):
                return pltpu.make_async_remote_copy(
                    src_ref=src, dst_ref=slot, send_sem=send_sem, recv_sem=recv_sem,
                    device_id=to, device_id_type=MESH)

            items.append(dict(
                me=jnp.logical_and(near, c == dc), sibling=jnp.logical_and(near, c != dc),
                relay=jnp.logical_and(far, c == dc), behind=jnp.logical_and(far, c != dc),
                local=pltpu.make_async_copy(x_refs[a], slot, local_sems.at[a, i]),
                to_sibling=remote(x_refs[a], (x, y, 1 - c), d2d_send.at[a, i], d2d_recv.at[a, i]),
                to_chips=[remote(x_refs[a], (*chip, c), ici_send.at[a, i, j], ici_recv.at[a, i])
                          for j, chip in enumerate(chips)],
                forward=remote(slot, (x, y, 1 - c), fwd_send.at[a, i], fwd_recv.at[a, i])))

    def start():
        for it in items:
            @pl.when(it["me"])
            def _(it=it):
                it["local"].start()
                it["to_sibling"].start()
                for cp in it["to_chips"]:
                    cp.start()

    def finish():
        for it in items:
            @pl.when(it["relay"])
            def _(it=it):
                it["to_chips"][0].wait_recv()
                it["forward"].start()
        for it in items:
            pl.when(it["sibling"])(it["to_sibling"].wait_recv)
            pl.when(it["behind"])(it["forward"].wait_recv)
            pl.when(it["relay"])(it["forward"].wait_send)

            @pl.when(it["me"])
            def _(it=it):
                it["local"].wait()
                it["to_sibling"].wait_send()
                for cp in it["to_chips"]:
                    cp.wait_send()

    return start, finish


def _all_gather(name, shards, sources):
    n = len(shards)
    out_shape, sems = _gather_plan(shards, sources)

    def body(*refs):
        start, finish = _gather_hooks(refs[:n], refs[n:2 * n], refs[2 * n:], sources)
        start()
        finish()

    return pl.pallas_call(
        body, name=name,
        out_shape=tuple(out_shape),
        in_specs=_hbm_specs(n), out_specs=tuple(_hbm_specs(n)),
        scratch_shapes=sems,
        compiler_params=_cparams(),
    )(*shards)


def _exchange_grads(parts, small):
    n = len(parts)
    rows_per_add = 256

    def body(*refs):
        p_refs, s_ref = refs[:n], refs[n]
        out_refs, sout_ref = refs[n + 1:2 * n + 1], refs[2 * n + 1]
        mine_v, recv_v = refs[2 * n + 2:3 * n + 2], refs[3 * n + 2:4 * n + 2]
        (d2d_send, d2d_recv, ici_send, ici_recv, s_send, s_recv, load_sems, own_sems,
         sown_sem) = refs[4 * n + 2:]
        x, y, c = _mesh_pos()
        me = 4 * x + 2 * y + c
        chips = [(x, y), (1 - x, y), (x, 1 - y), (1 - x, 1 - y)]

        sown = pltpu.make_async_copy(s_ref, sout_ref.at[me], sown_sem)
        sown.start()
        tiny = []
        for k in range(1, N_DEV):
            peer = (x ^ (k >> 2), y ^ ((k >> 1) & 1), c ^ (k & 1))
            tiny.append(pltpu.make_async_remote_copy(
                src_ref=s_ref, dst_ref=sout_ref.at[me], send_sem=s_send.at[k - 1],
                recv_sem=s_recv.at[k - 1], device_id=peer, device_id_type=MESH))
        for cp in tiny:
            cp.start()

        loads, swaps = [], []
        for a in range(n):
            for j, (px, py) in enumerate(chips):
                loads.append(pltpu.make_async_copy(
                    p_refs[a].at[4 * px + 2 * py + c], mine_v[a].at[j], load_sems.at[a, j]))
                swaps.append(pltpu.make_async_remote_copy(
                    src_ref=p_refs[a].at[4 * px + 2 * py + 1 - c], dst_ref=recv_v[a].at[j],
                    send_sem=d2d_send.at[a, j], recv_sem=d2d_recv.at[a, j],
                    device_id=(x, y, 1 - c), device_id_type=MESH))
        for cp in swaps + loads:
            cp.start()

        outgoing, own = [], []
        for a in range(n):
            rows = p_refs[a].shape[1]
            step = math.gcd(rows, rows_per_add)
            for j in range(4):
                loads[4 * a + j].wait()
                swaps[4 * a + j].wait_recv()

                @pl.loop(0, rows // step)
                def _(i):
                    rs = pl.ds(pl.multiple_of(i * step, step), step)
                    mine_v[a][j, rs, :] = (mine_v[a][j, rs, :].astype(F32)
                                           + recv_v[a][j, rs, :].astype(F32)).astype(BF16)

                if j == 0:
                    own.append(pltpu.make_async_copy(mine_v[a].at[0], out_refs[a].at[0],
                                                     own_sems.at[a]))
                    own[-1].start()
                else:
                    outgoing.append(pltpu.make_async_remote_copy(
                        src_ref=mine_v[a].at[j], dst_ref=out_refs[a].at[j],
                        send_sem=ici_send.at[a, j - 1], recv_sem=ici_recv.at[a, j - 1],
                        device_id=(*chips[j], c), device_id_type=MESH))
                    outgoing[-1].start()

        for cp in tiny + outgoing:
            cp.wait_recv()
        for cp in tiny + outgoing + swaps:
            cp.wait_send()
        for cp in own:
            cp.wait()
        sown.wait()

    outs = [jax.ShapeDtypeStruct((4,) + p.shape[1:], p.dtype) for p in parts]
    outs.append(jax.ShapeDtypeStruct((N_DEV,) + small.shape, small.dtype))
    stage = [pltpu.VMEM((4,) + p.shape[1:], p.dtype) for p in parts]
    return pl.pallas_call(
        body, name="exchange_grads",
        out_shape=tuple(outs),
        in_specs=_hbm_specs(n + 1), out_specs=tuple(_hbm_specs(n + 1)),
        scratch_shapes=stage + stage + [
            pltpu.SemaphoreType.DMA((n, 4)), pltpu.SemaphoreType.DMA((n, 4)),
            pltpu.SemaphoreType.DMA((n, 3)), pltpu.SemaphoreType.DMA((n, 3)),
            pltpu.SemaphoreType.DMA((7,)), pltpu.SemaphoreType.DMA((7,)),
            pltpu.SemaphoreType.DMA((n, 4)), pltpu.SemaphoreType.DMA((n,)),
            pltpu.SemaphoreType.DMA],
        compiler_params=_cparams(),
    )(*parts, small)


def _swap_add(name, parts, owns_prev, dests):
    na = len(parts)
    step = 256

    def body(*refs):
        p_refs, h_refs, own_refs = refs[:na], refs[2 * na:3 * na], refs[3 * na:4 * na]
        mine_vs, recv_vs = refs[4 * na:5 * na], refs[5 * na:6 * na]
        send_sems, recv_sems, load_sems, store_sems, own_sems = refs[6 * na:]
        x, y, c = _mesh_pos()
        work = []
        for a in range(na):
            for i, d in enumerate(dests[a]):
                dx, dy, dc = _dev(d)
                keep = c == dc
                mine = jnp.logical_and(keep, jnp.logical_and(x == dx, y == dy))
                swap = pltpu.make_async_remote_copy(
                    src_ref=p_refs[a].at[i], dst_ref=recv_vs[a].at[i], send_sem=send_sems.at[a, i],
                    recv_sem=recv_sems.at[a, i], device_id=(x, y, 1 - c), device_id_type=MESH)
                load = pltpu.make_async_copy(p_refs[a].at[i], mine_vs[a].at[i], load_sems.at[a, i])
                store = pltpu.make_async_copy(mine_vs[a].at[i], h_refs[a].at[i], store_sems.at[a, i])
                own = pltpu.make_async_copy(mine_vs[a].at[i], own_refs[a], own_sems.at[a])
                pl.when(keep)(load.start)
                pl.when(jnp.logical_not(keep))(swap.start)
                work.append((a, i, keep, mine, swap, load, store, own))
        for a, i, keep, mine, swap, load, store, own in work:
            cols = parts[a].shape[2]
            cstep = math.gcd(cols, step)

            @pl.when(keep)
            def _(a=a, i=i, mine=mine, swap=swap, load=load, store=store, own=own, cols=cols,
                  cstep=cstep):
                load.wait()
                swap.wait_recv()

                @pl.loop(0, cols // cstep)
                def _(r):
                    cs = pl.ds(pl.multiple_of(r * cstep, cstep), cstep)
                    mine_vs[a][i, :, cs] = (mine_vs[a][i, :, cs].astype(F32)
                                            + recv_vs[a][i, :, cs].astype(F32)).astype(BF16)

                store.start()
                pl.when(mine)(own.start)
        for a, i, keep, mine, swap, load, store, own in work:
            pl.when(jnp.logical_not(keep))(swap.wait_send)
            pl.when(keep)(store.wait)
            pl.when(mine)(own.wait)

    stage = [pltpu.VMEM(p.shape, BF16) for p in parts]
    most = max(len(d) for d in dests)
    res = pl.pallas_call(
        body, name=name,
        out_shape=tuple([jax.ShapeDtypeStruct(p.shape, BF16) for p in parts]
                        + [jax.ShapeDtypeStruct(o.shape, BF16) for o in owns_prev]),
        in_specs=_hbm_specs(2 * na), out_specs=tuple(_hbm_specs(2 * na)),
        input_output_aliases={na + a: na + a for a in range(na)},
        scratch_shapes=stage + stage + [pltpu.SemaphoreType.DMA((na, most)) for _ in range(4)]
        + [pltpu.SemaphoreType.DMA((na,))],
        compiler_params=_cparams(),
    )(*parts, *owns_prev)
    return res[:na], res[na:]


def _ici_copies(h_ref, land_ref, send_sems, recv_sems, dests):
    x, y, c = _mesh_pos()
    sends, arrivals = [], []
    for i, d in enumerate(dests):
        dx, dy, dc = _dev(d)
        j = (x != dx).astype(jnp.int32) + 2 * (y != dy).astype(jnp.int32)
        slot = jnp.maximum(j - 1, 0)
        sends.append((jnp.logical_and(c == dc, j > 0), pltpu.make_async_remote_copy(
            src_ref=h_ref.at[i], dst_ref=land_ref.at[slot], send_sem=send_sems.at[i],
            recv_sem=recv_sems.at[slot], device_id=(dx, dy, dc), device_id_type=MESH)))
        arrivals.append((jnp.logical_and(c == dc, j == 0), [pltpu.make_async_remote_copy(
            src_ref=h_ref.at[i], dst_ref=land_ref.at[r], send_sem=send_sems.at[i],
            recv_sem=recv_sems.at[r], device_id=(dx, dy, dc), device_id_type=MESH)
            for r in range(3)]))
    return sends, arrivals


def _ici_start(name, hs, lands, dests):
    na = len(hs)

    def body(*refs):
        h_refs, land_refs, sems = refs[:na], refs[na:2 * na], refs[2 * na:4 * na]
        token = refs[-1]
        for a in range(na):
            sends, _ = _ici_copies(h_refs[a], land_refs[a], sems[2 * a], sems[2 * a + 1], dests[a])
            for go, cp in sends:
                pl.when(go)(cp.start)
        token[...] = jnp.zeros_like(token)

    hbm, sem = pl.BlockSpec(memory_space=pltpu.HBM), pl.BlockSpec(memory_space=pltpu.SEMAPHORE)
    sem_shapes = []
    for a in range(na):
        sem_shapes += [pltpu.SemaphoreType.DMA((len(dests[a]),)), pltpu.SemaphoreType.DMA((3,))]
    res = pl.pallas_call(
        body, name=name,
        out_shape=tuple(sem_shapes) + tuple(pltpu.HBM(v.shape, v.dtype) for v in list(hs) + list(lands))
        + (jax.ShapeDtypeStruct((8, LANE), F32),),
        in_specs=(hbm,) * (2 * na),
        out_specs=(sem,) * (2 * na) + (hbm,) * (2 * na) + (pl.BlockSpec(memory_space=pltpu.VMEM),),
        input_output_aliases={i: 2 * na + i for i in range(2 * na)},
        compiler_params=pltpu.CompilerParams(
            has_side_effects=pltpu.SideEffectType.DATAFLOW_SIDE_EFFECTING,
            vmem_limit_bytes=VMEM_LIMIT),
    )(*[pltpu.with_memory_space_constraint(v, pltpu.HBM) for v in list(hs) + list(lands)])
    sems = [(res[2 * a], res[2 * a + 1]) for a in range(na)]
    return sems, res[2 * na:3 * na], res[3 * na:4 * na], res[-1]


def _ici_wait(name, started, lands, after):
    k, nl = len(started), len(lands)

    def body(*refs):
        land_refs = refs[3 * k:3 * k + nl]
        for s in range(k):
            h_ref, send_sems, recv_sems = refs[3 * s:3 * s + 3]
            sends, arrivals = _ici_copies(h_ref, land_refs[started[s][3]], send_sems, recv_sems,
                                          started[s][4])
            for go, cp in sends:
                pl.when(go)(cp.wait_send)
            for here, cps in arrivals:
                for cp in cps:
                    pl.when(here)(cp.wait_recv)

    hbm, sem = pl.BlockSpec(memory_space=pltpu.HBM), pl.BlockSpec(memory_space=pltpu.SEMAPHORE)
    operands, specs = [], []
    for send_sems, recv_sems, h, _, _ in started:
        operands += [h, send_sems, recv_sems]
        specs += [hbm, sem, sem]
    return pl.pallas_call(
        body, name=name,
        out_shape=tuple(pltpu.HBM(v.shape, v.dtype) for v in lands),
        in_specs=tuple(specs) + (hbm,) * nl + (pl.BlockSpec(memory_space=pl.ANY),),
        out_specs=(hbm,) * nl,
        input_output_aliases={3 * k + i: i for i in range(nl)},
        compiler_params=pltpu.CompilerParams(
            has_side_effects=pltpu.SideEffectType.DATAFLOW_SIDE_EFFECTING,
            vmem_limit_bytes=VMEM_LIMIT),
    )(*operands, *lands, after)


def _weights_to_p(name, gathered, where, group):
    tl = 256
    off, width = P_GROUPS[group]
    segs = sorted([s for s in _segments() if _group_of(s[3]) == group], key=lambda s: s[3])
    used = sorted({where[s[0]][0] for s in segs})

    def body(*refs):
        g_refs, o_ref = dict(zip(used, refs[:-1])), refs[-1]
        pieces, pos = [], off
        for d, c0, n, p0 in segs:
            if p0 > pos:
                pieces.append(jnp.zeros((p0 - pos, tl), F32))
            k, slot = where[d]
            pieces.append(g_refs[k][slot, c0:c0 + n, :].astype(F32))
            pos = p0 + n
        if off + width > pos:
            pieces.append(jnp.zeros((off + width - pos, tl), F32))
        o_ref[...] = jnp.concatenate(pieces, axis=0).astype(BF16)

    return pl.pallas_call(
        body, name=name,
        grid=(D_MODEL // tl,),
        in_specs=[pl.BlockSpec((gathered[k].shape[0], SHARD_COLS, tl), lambda i: (0, 0, i))
                  for k in used],
        out_specs=pl.BlockSpec((width, tl), lambda i: (0, i)),
        out_shape=jax.ShapeDtypeStruct((width, D_MODEL), BF16),
        compiler_params=_cparams(("arbitrary",)),
    )(*[gathered[k] for k in used])


def _shard_groups(d):
    return sorted({_group_of(s[3]) for s in _segments() if s[0] == d})


def _grads_to_shards(name, groups, dests):
    tl = 256
    segs = _segments()
    used = sorted(groups)

    def body(*refs):
        g_refs, o_ref = dict(zip(used, refs[:-1])), refs[-1]
        for i, d in enumerate(dests):
            pieces = []
            for _, c0, n, p0 in sorted([s for s in segs if s[0] == d], key=lambda s: s[1]):
                gi = _group_of(p0)
                lo = p0 - P_GROUPS[gi][0]
                pieces.append(g_refs[gi][lo:lo + n, :].astype(F32))
            o_ref[i] = jnp.concatenate(pieces, axis=0).astype(BF16)

    return pl.pallas_call(
        body, name=name,
        grid=(D_MODEL // tl,),
        in_specs=[pl.BlockSpec((P_GROUPS[g][1], tl), lambda i: (0, i)) for g in used],
        out_specs=pl.BlockSpec((len(dests), SHARD_COLS, tl), lambda i: (0, 0, i)),
        out_shape=jax.ShapeDtypeStruct((len(dests), SHARD_COLS, D_MODEL), BF16),
        compiler_params=_cparams(("arbitrary",)),
    )(*[groups[g] for g in used])


def _inproj(x, g_in, w_pt):
    t = x.shape[0]
    tm = min(256, t)
    width = w_pt.shape[0]

    def body(x_ref, g_ref, w_ref, proj_ref, h_ref, r_ref):
        xf = x_ref[...]
        r = lax.rsqrt(jnp.mean(xf * xf, axis=-1, keepdims=True) + EPS)
        h = ((xf * r) * g_ref[...]).astype(BF16)
        proj_ref[...] = _dot_nt(h, w_ref[...])
        h_ref[...] = h
        r_ref[...] = r

    row = lambda w: pl.BlockSpec((tm, w), lambda i: (i, 0))
    return pl.pallas_call(
        body, name="inproj_latents",
        grid=(t // tm,),
        in_specs=[row(D_MODEL), pl.BlockSpec((1, D_MODEL), lambda i: (0, 0)),
                  pl.BlockSpec((width, D_MODEL), lambda i: (0, 0))],
        out_specs=(row(width), row(D_MODEL), row(1)),
        out_shape=(jax.ShapeDtypeStruct((t, width), F32),
                   jax.ShapeDtypeStruct((t, D_MODEL), BF16),
                   jax.ShapeDtypeStruct((t, 1), F32)),
        compiler_params=_cparams(("arbitrary",)),
    )(x, g_in, w_pt)


def _proj(name, h, w_pt):
    t = h.shape[0]
    tm = min(256, t)
    width = w_pt.shape[0]

    def body(h_ref, w_ref, o_ref):
        o_ref[...] = _dot_nt(h_ref[...], w_ref[...])

    return pl.pallas_call(
        body, name=name,
        grid=(t // tm,),
        in_specs=[pl.BlockSpec((tm, D_MODEL), lambda i: (i, 0)),
                  pl.BlockSpec((width, D_MODEL), lambda i: (0, 0))],
        out_specs=pl.BlockSpec((tm, width), lambda i: (i, 0)),
        out_shape=jax.ShapeDtypeStruct((t, width), F32),
        compiler_params=_cparams(("arbitrary",)),
    )(h, w_pt)


def _mla_prep(proj, g_q, g_kv, w_uq_p, w_k_p, w_v, w_gate_p, b_gate, rc, rsn, rsp):
    t = proj.shape[0]
    tm = min(256, t)
    hq = MLA_HEADS * HEAD_PAD

    def body(cq_ref, ckv_ref, misc_ref, gq_ref, gkv_ref, wuq_ref, wk_ref, wv_ref, wg_ref, bg_ref,
             c_ref, sn_ref, sp_ref,
             q_ref, k_ref, v_ref, la_ref, pre_ref, cqn_ref, ckvn_ref, rq_ref, rkv_ref, mb_ref):
        c, sn, sp = c_ref[...], sn_ref[...], sp_ref[...]
        cq = cq_ref[:, :MLA_Q_RANK]
        rq = lax.rsqrt(jnp.mean(cq * cq, axis=-1, keepdims=True) + EPS)
        cqn = ((cq * rq) * gq_ref[...]).astype(BF16)
        cqn_ref[...] = cqn
        rq_ref[...] = rq
        qpre = _dot(cqn, wuq_ref[...])
        ckv = ckv_ref[...]
        rkv = lax.rsqrt(jnp.mean(ckv * ckv, axis=-1, keepdims=True) + EPS)
        ckvn = ((ckv * rkv) * gkv_ref[...]).astype(BF16)
        ckvn_ref[...] = ckvn
        rkv_ref[...] = rkv
        kn = _dot(ckvn, wk_ref[...])
        v_ref[...] = _dot(ckvn, wv_ref[...]).astype(BF16)
        misc = misc_ref[...]
        krope = _rope_fwd(misc, c, sn, sp)
        for h in range(MLA_HEADS):
            sl = slice(h * HEAD_PAD, (h + 1) * HEAD_PAD)
            q_ref[:, sl] = _rope_fwd(qpre[:, sl], c, sn, sp).astype(BF16)
            k_ref[:, sl] = (kn[:, sl] + krope).astype(BF16)
        mb_ref[...] = misc.astype(BF16)
        pre = _dot(mb_ref[...], wg_ref[...]) + bg_ref[...]
        pre_ref[...] = pre
        log_a = (jnp.minimum(pre, 0.0) - jnp.log(1.0 + jnp.exp(-jnp.abs(pre)))) / GLA_GATE_NORM
        la_ref[...] = _dot_exact(_chunk_tri(tm, True), log_a)

    row = lambda w: pl.BlockSpec((tm, w), lambda i: (i, 0))
    full = lambda a: pl.BlockSpec(a.shape, lambda i: (0, 0))
    return pl.pallas_call(
        body, name="mla_prep",
        grid=(t // tm,),
        in_specs=[pl.BlockSpec((tm, 512), lambda i: (i, _rel(P_CQ) // 512)),
                  pl.BlockSpec((tm, MLA_KV_RANK), lambda i: (i, _rel(P_CKV) // MLA_KV_RANK)),
                  pl.BlockSpec((tm, LANE), lambda i: (i, _rel(P_MISC) // LANE)),
                  full(g_q), full(g_kv), full(w_uq_p), full(w_k_p), full(w_v), full(w_gate_p),
                  full(b_gate), row(LANE), row(LANE), row(LANE)],
        out_specs=(row(hq), row(hq), row(MLA_WIDTH), row(GLA_DK), row(GLA_DK),
                   row(MLA_Q_RANK), row(MLA_KV_RANK), row(1), row(1), row(LANE)),
        out_shape=(jax.ShapeDtypeStruct((t, hq), BF16), jax.ShapeDtypeStruct((t, hq), BF16),
                   jax.ShapeDtypeStruct((t, MLA_WIDTH), BF16),
                   jax.ShapeDtypeStruct((t, GLA_DK), F32), jax.ShapeDtypeStruct((t, GLA_DK), F32),
                   jax.ShapeDtypeStruct((t, MLA_Q_RANK), BF16),
                   jax.ShapeDtypeStruct((t, MLA_KV_RANK), BF16),
                   jax.ShapeDtypeStruct((t, 1), F32), jax.ShapeDtypeStruct((t, 1), F32),
                   jax.ShapeDtypeStruct((t, LANE), BF16)),
        compiler_params=_cparams(("arbitrary",)),
    )(proj, proj, proj, g_q, g_kv, w_uq_p, w_k_p, w_v, w_gate_p, b_gate, rc, rsn, rsp)


def _attn_masks(tq, i):
    keys = (i + 1) * tq
    rows = i * tq + lax.broadcasted_iota(jnp.int32, (tq, keys), 0)
    cols = lax.broadcasted_iota(jnp.int32, (tq, keys), 1)
    lane = lax.broadcasted_iota(jnp.int32, (tq, LANE), 1)
    return cols <= rows, lane < MLA_VDIM


def _for_each_query_tile(n_tiles, fn):
    for i in range(n_tiles):
        pl.when(pl.program_id(1) == i)(lambda i=i: fn(i))


def _mla_attn_fwd(q, k, v, shards, sources):
    t = q.shape[0]
    tq = min(256, t)
    scale = MLA_QK ** -0.5
    ns = len(shards)
    g_shapes, g_sems = _gather_plan(shards, sources)
    grid = (MLA_HEADS // 2, t // tq)

    def body(q_ref, k_ref, v_ref, *rest):
        o_ref, lse_ref = rest[ns:ns + 2]
        start, finish = _gather_hooks(rest[:ns], rest[ns + 2:2 * ns + 2], rest[2 * ns + 2:], sources)
        step = pl.program_id(0) * grid[1] + pl.program_id(1)
        pl.when(step == 0)(start)

        def tile(i):
            keys = (i + 1) * tq
            causal, low = _attn_masks(tq, i)
            vp = v_ref[0:keys, :]
            acc = jnp.zeros((tq, LANE), F32)
            for hh in range(2):
                sl = slice(hh * HEAD_PAD, (hh + 1) * HEAD_PAD)
                s = _dot_nt(q_ref[:, sl], k_ref[0:keys, sl]) * scale
                s = jnp.where(causal, s, -jnp.inf)
                m = jnp.max(s, axis=-1, keepdims=True)
                e = jnp.exp(s - m)
                l = jnp.sum(e, axis=-1, keepdims=True)
                o = _dot(e.astype(BF16), vp) / l
                acc = jnp.where(low if hh == 0 else jnp.logical_not(low), o, acc)
                lse_ref[hh] = m + jnp.log(l)
            o_ref[...] = acc

        _for_each_query_tile(t // tq, tile)
        pl.when(step == grid[0] * grid[1] - 1)(finish)

    res = pl.pallas_call(
        body, name="mla_attn_fwd",
        grid=grid,
        in_specs=[pl.BlockSpec((tq, 2 * HEAD_PAD), lambda p, i: (i, p)),
                  pl.BlockSpec((t, 2 * HEAD_PAD), lambda p, i: (0, p)),
                  pl.BlockSpec((t, LANE), lambda p, i: (0, p))] + _hbm_specs(ns),
        out_specs=(pl.BlockSpec((tq, LANE), lambda p, i: (i, p)),
                   pl.BlockSpec((2, tq, 1), lambda p, i: (p, i, 0))) + tuple(_hbm_specs(ns)),
        out_shape=(jax.ShapeDtypeStruct((t, MLA_WIDTH), F32),
                   jax.ShapeDtypeStruct((MLA_HEADS, t, 1), F32)) + tuple(g_shapes),
        scratch_shapes=g_sems,
        compiler_params=_cparams(("arbitrary", "arbitrary")),
    )(q, k, v, *shards)
    return res[0], res[1], res[2:]


def _mla_attn_bwd(q, k, v, o, do, lse, after):
    t = q.shape[0]
    tq = min(256, t)
    scale = MLA_QK ** -0.5

    def body(q_ref, k_ref, v_ref, o_ref, do_ref, lse_ref, after_ref, dq_ref, dk_ref, dv_ref):
        del after_ref

        @pl.when(pl.program_id(1) == 0)
        def _():
            dk_ref[...] = jnp.zeros_like(dk_ref)
            dv_ref[...] = jnp.zeros_like(dv_ref)

        def tile(i):
            keys = (i + 1) * tq
            causal, low = _attn_masks(tq, i)
            vp = v_ref[0:keys, :]
            do_all = do_ref[...]
            o_all = o_ref[...]
            dv_acc = jnp.zeros((keys, LANE), F32)
            for hh in range(2):
                sl = slice(hh * HEAD_PAD, (hh + 1) * HEAD_PAD)
                do_h = jnp.where(low if hh == 0 else jnp.logical_not(low), do_all, 0.0)
                dsum = jnp.sum(do_h * o_all, axis=-1, keepdims=True)
                qh = q_ref[:, sl]
                kh = k_ref[0:keys, sl]
                s = _dot_nt(qh, kh) * scale
                p = jnp.where(causal, jnp.exp(s - lse_ref[hh]), 0.0)
                do_b = do_h.astype(BF16)
                dp = _dot_nt(do_b, vp)
                ds = (p * (dp - dsum) * scale).astype(BF16)
                dq_ref[:, sl] = _dot(ds, kh).astype(BF16)
                dk_ref[0:keys, sl] += _dot_tn(ds, qh)
                dv_acc = dv_acc + _dot_tn(p.astype(BF16), do_b)
            dv_ref[0:keys, :] += dv_acc

        _for_each_query_tile(t // tq, tile)

    return pl.pallas_call(
        body, name="mla_attn_bwd",
        grid=(MLA_HEADS // 2, t // tq),
        in_specs=[pl.BlockSpec((tq, 2 * HEAD_PAD), lambda p, i: (i, p)),
                  pl.BlockSpec((t, 2 * HEAD_PAD), lambda p, i: (0, p)),
                  pl.BlockSpec((t, LANE), lambda p, i: (0, p)),
                  pl.BlockSpec((tq, LANE), lambda p, i: (i, p)),
                  pl.BlockSpec((tq, LANE), lambda p, i: (i, p)),
                  pl.BlockSpec((2, tq, 1), lambda p, i: (p, i, 0)),
                  pl.BlockSpec(memory_space=pl.ANY)],
        out_specs=(pl.BlockSpec((tq, 2 * HEAD_PAD), lambda p, i: (i, p)),
                   pl.BlockSpec((t, 2 * HEAD_PAD), lambda p, i: (0, p)),
                   pl.BlockSpec((t, LANE), lambda p, i: (0, p))),
        out_shape=(jax.ShapeDtypeStruct((t, MLA_HEADS * HEAD_PAD), BF16),
                   jax.ShapeDtypeStruct((t, MLA_HEADS * HEAD_PAD), F32),
                   jax.ShapeDtypeStruct((t, MLA_WIDTH), F32)),
        compiler_params=_cparams(("arbitrary", "arbitrary")),
    )(q, k, v, o, do, lse, after)


def _chunk_tri(n, lower):
    r = lax.broadcasted_iota(jnp.int32, (n, n), 0)
    c = lax.broadcasted_iota(jnp.int32, (n, n), 1)
    same = (r // GLA_CHUNK) == (c // GLA_CHUNK)
    return jnp.where(jnp.logical_and(same, r >= c if lower else r <= c), 1.0, 0.0).astype(F32)


def _gla_chunk_terms(q_ref, k_ref, b_ref, h):
    sl = slice(h * GLA_HK, (h + 1) * GLA_HK)
    b = b_ref[:, sl]
    bl = b[GLA_CHUNK - 1:GLA_CHUNK, :]
    kc = k_ref[:, sl]
    q_in = (q_ref[:, sl] * (GLA_HK ** -0.5)) * jnp.exp(b)
    k_in = kc * jnp.exp(-b)
    k_st = kc * jnp.exp(bl - b)
    return b, bl, q_in, k_in, k_st


def _tri(c, lower):
    r = lax.broadcasted_iota(jnp.int32, (c, c), 0)
    cc = lax.broadcasted_iota(jnp.int32, (c, c), 1)
    return jnp.where(r >= cc if lower else r <= cc, 1.0, 0.0).astype(F32)


def _gla_fwd(proj, log_a, shards, sources):
    t = proj.shape[0]
    c = GLA_CHUNK
    n = t // c
    ns = len(shards)
    g_shapes, g_sems = _gather_plan(shards, sources)

    def body(q_ref, k_ref, v_ref, la_ref, *rest):
        o_ref, sp_ref = rest[ns:ns + 2]
        st_ref = rest[2 * ns + 2]
        start, finish = _gather_hooks(rest[:ns], rest[ns + 2:2 * ns + 2], rest[2 * ns + 3:], sources)

        @pl.when(pl.program_id(0) == 0)
        def _():
            st_ref[...] = jnp.zeros_like(st_ref)
            start()

        tri = _tri(c, True)
        for h in range(GLA_HEADS):
            _, bl, q_in, k_in, k_st = _gla_chunk_terms(q_ref, k_ref, la_ref, h)
            vs = slice(h * GLA_HV, (h + 1) * GLA_HV)
            vv = v_ref[:, vs].astype(BF16)
            qb = q_in.astype(BF16)
            attn = _dot_nt(qb, k_in.astype(BF16)) * tri
            st = st_ref[h]
            sp_ref[0, h] = st
            o_ref[:, vs] = _dot(attn.astype(BF16), vv) + _dot_nt(qb, st.astype(BF16))
            st_ref[h] = st * jnp.exp(bl) + _dot_tn(vv, k_st.astype(BF16))

        pl.when(pl.program_id(0) == n - 1)(finish)

    res = pl.pallas_call(
        body, name="gla_fwd",
        grid=(n,),
        in_specs=[pl.BlockSpec((c, GLA_DK), lambda i: (i, P_QG // GLA_DK)),
                  pl.BlockSpec((c, GLA_DK), lambda i: (i, P_KG // GLA_DK)),
                  pl.BlockSpec((c, GLA_DV), lambda i: (i, P_VG // GLA_DV)),
                  pl.BlockSpec((c, GLA_DK), lambda i: (i, 0))] + _hbm_specs(ns),
        out_specs=(pl.BlockSpec((c, GLA_DV), lambda i: (i, 0)),
                   pl.BlockSpec((1, GLA_HEADS, GLA_HV, GLA_HK), lambda i: (i, 0, 0, 0)))
        + tuple(_hbm_specs(ns)),
        out_shape=(jax.ShapeDtypeStruct((t, GLA_DV), F32),
                   jax.ShapeDtypeStruct((n, GLA_HEADS, GLA_HV, GLA_HK), F32)) + tuple(g_shapes),
        scratch_shapes=[pltpu.VMEM((GLA_HEADS, GLA_HV, GLA_HK), F32)] + g_sems,
        compiler_params=_cparams(("arbitrary",)),
    )(proj, proj, proj, log_a, *shards)
    return res[0], res[1], res[2:]


def _gla_bwd(proj, log_a, do, states, after):
    t = proj.shape[0]
    c = GLA_CHUNK
    n = t // c

    def body(q_ref, k_ref, v_ref, la_ref, do_ref, sp_ref, after_ref, dg_ref, dla_ref, ds_ref):
        del after_ref

        @pl.when(pl.program_id(0) == 0)
        def _():
            ds_ref[...] = jnp.zeros_like(ds_ref)

        tri = _tri(c, True)
        last = lax.broadcasted_iota(jnp.int32, (c, GLA_HK), 0) == c - 1
        for h in range(GLA_HEADS):
            b, bl, q_in, k_in, k_st = _gla_chunk_terms(q_ref, k_ref, la_ref, h)
            ks_ = slice(h * GLA_HK, (h + 1) * GLA_HK)
            vs = slice(h * GLA_HV, (h + 1) * GLA_HV)
            vv = v_ref[:, vs].astype(BF16)
            do_h = do_ref[:, vs]
            qb, kb, ksb = q_in.astype(BF16), k_in.astype(BF16), k_st.astype(BF16)
            attn = (_dot_nt(qb, kb) * tri).astype(BF16)
            st = sp_ref[0, h]
            dst = ds_ref[h]
            dstb = dst.astype(BF16)
            dattn = (_dot_nt(do_h, vv) * tri).astype(BF16)
            dg_ref[:, P_VG + h * GLA_HV:P_VG + (h + 1) * GLA_HV] = (
                _dot_tn(attn, do_h) + _dot_nt(ksb, dstb)).astype(BF16)
            dq_in = _dot(dattn, kb) + _dot(do_h, st.astype(BF16))
            dk_in = _dot_tn(dattn, qb)
            dk_st = _dot(vv, dstb)
            ebl = jnp.exp(bl)
            d_ebl = jnp.sum(st * dst, axis=0, keepdims=True)
            ds_ref[h] = _dot_tn(do_h, qb) + dst * ebl
            dg_ref[:, P_QG + h * GLA_HK:P_QG + (h + 1) * GLA_HK] = (
                dq_in * (GLA_HK ** -0.5) * jnp.exp(b)).astype(BF16)
            dg_ref[:, P_KG + h * GLA_HK:P_KG + (h + 1) * GLA_HK] = (
                dk_in * jnp.exp(-b) + dk_st * jnp.exp(bl - b)).astype(BF16)
            db = dq_in * q_in - dk_in * k_in - dk_st * k_st
            dbl = jnp.sum(dk_st * k_st, axis=0, keepdims=True) + d_ebl * ebl
            dla_ref[:, ks_] = db + jnp.where(last, dbl, 0.0)

    rev = lambda i: n - 1 - i
    gw = P_GROUPS[0][1]
    return pl.pallas_call(
        body, name="gla_bwd",
        grid=(n,),
        in_specs=[pl.BlockSpec((c, GLA_DK), lambda i: (rev(i), P_QG // GLA_DK)),
                  pl.BlockSpec((c, GLA_DK), lambda i: (rev(i), P_KG // GLA_DK)),
                  pl.BlockSpec((c, GLA_DV), lambda i: (rev(i), P_VG // GLA_DV)),
                  pl.BlockSpec((c, GLA_DK), lambda i: (rev(i), 0)),
                  pl.BlockSpec((c, GLA_DV), lambda i: (rev(i), 0)),
                  pl.BlockSpec((1, GLA_HEADS, GLA_HV, GLA_HK), lambda i: (rev(i), 0, 0, 0)),
                  pl.BlockSpec(memory_space=pl.ANY)],
        out_specs=(pl.BlockSpec((c, gw), lambda i: (rev(i), 0)),
                   pl.BlockSpec((c, GLA_DK), lambda i: (rev(i), 0))),
        out_shape=(jax.ShapeDtypeStruct((t, gw), BF16), jax.ShapeDtypeStruct((t, GLA_DK), F32)),
        scratch_shapes=[pltpu.VMEM((GLA_HEADS, GLA_HV, GLA_HK), F32)],
        compiler_params=_cparams(("arbitrary",)),
    )(proj, proj, proj, log_a, do, states, after)


def _post(o_mla, proj, o_gla, x, target, g_gla, g_final, w_pm, w_pg, w_o):
    t = x.shape[0]
    tm = min(128, t)
    g0, gw = P_GROUPS[1]

    def body(om_ref, zg_ref, gm_ref, gg_ref, zm_ref, og_ref, x_ref, tg_ref, ggla_ref, gf_ref,
             wpm_ref, wpg_ref, wo_ref,
             dx2_ref, dom_ref, dog_ref, dg_ref,
             mg_ref, um_ref, ug_ref, dym_ref, dyg_ref, loss_ref, dgf_ref, dggla_ref):
        @pl.when(pl.program_id(0) == 0)
        def _():
            loss_ref[...] = jnp.zeros_like(loss_ref)
            dgf_ref[...] = jnp.zeros_like(dgf_ref)
            dggla_ref[...] = jnp.zeros_like(dggla_ref)

        om = om_ref[...]
        zm = zm_ref[...]
        sm = _sigmoid(zm)
        silu_m = zm * sm
        um = (om * silu_m).astype(BF16)
        um_ref[...] = um
        ym = _dot(um, wpm_ref[...])

        ggla = ggla_ref[...]
        zg = zg_ref[...]
        sg = _sigmoid(zg)
        silu_g = zg * sg
        xhat, rstd, on = [], [], []
        for h in range(GLA_HEADS):
            blk = og_ref[:, h * GLA_HV:(h + 1) * GLA_HV]
            r = lax.rsqrt(jnp.mean(blk * blk, axis=-1, keepdims=True) + EPS)
            xhat.append(blk * r)
            rstd.append(r)
            on.append(xhat[h] * ggla)
        on = jnp.concatenate(on, axis=-1)
        ug = (on * silu_g).astype(BF16)
        ug_ref[...] = ug
        yg = _dot(ug, wpg_ref[...])

        sgm = _sigmoid(gm_ref[...])
        sgg = _sigmoid(gg_ref[...])
        merged = (sgm * ym + sgg * yg).astype(BF16)
        mg_ref[...] = merged
        x2 = x_ref[...] + _dot(merged, wo_ref[...])
        gf = gf_ref[...]
        rf = lax.rsqrt(jnp.mean(x2 * x2, axis=-1, keepdims=True) + EPS)
        xh = x2 * rf
        err = xh * gf - tg_ref[...]
        loss_ref[...] += 0.5 * jnp.sum(jnp.mean(err * err, axis=-1, keepdims=True))

        dy = err * (1.0 / D_MODEL)
        dgf_ref[...] += jnp.sum(dy * xh, axis=0, keepdims=True)
        dxh = dy * gf
        dx2 = rf * (dxh - xh * jnp.mean(dxh * xh, axis=-1, keepdims=True))
        dx2_ref[...] = dx2
        dmerged = _dot_nt(dx2.astype(BF16), wo_ref[...])
        dym = (dmerged * sgm).astype(BF16)
        dyg = (dmerged * sgg).astype(BF16)
        dym_ref[...] = dym
        dyg_ref[...] = dyg
        dg_ref[:, P_GMLA - g0:P_GMLA - g0 + D_MODEL] = (dmerged * ym * sgm * (1.0 - sgm)).astype(BF16)
        dg_ref[:, P_GGLA - g0:P_GGLA - g0 + D_MODEL] = (dmerged * yg * sgg * (1.0 - sgg)).astype(BF16)
        dum = _dot_nt(dym, wpm_ref[...])
        dom_ref[...] = dum * silu_m
        dg_ref[:, P_ZMLA - g0:P_ZMLA - g0 + MLA_WIDTH] = (
            dum * om * (sm * (1.0 + zm * (1.0 - sm)))).astype(BF16)
        dug = _dot_nt(dyg, wpg_ref[...])
        dg_ref[:, P_ZGLA - g0:P_ZGLA - g0 + GLA_DV] = (
            dug * on * (sg * (1.0 + zg * (1.0 - sg)))).astype(BF16)
        don = dug * silu_g
        dggla = jnp.zeros((1, GLA_HV), F32)
        for h in range(GLA_HEADS):
            hs = slice(h * GLA_HV, (h + 1) * GLA_HV)
            don_h = don[:, hs]
            dggla = dggla + jnp.sum(don_h * xhat[h], axis=0, keepdims=True)
            dxh_h = don_h * ggla
            dog_ref[:, hs] = (rstd[h] * (dxh_h - xhat[h] * jnp.mean(dxh_h * xhat[h], axis=-1,
                                                                     keepdims=True))).astype(BF16)
        dggla_ref[...] += dggla

    row = lambda w: pl.BlockSpec((tm, w), lambda i: (i, 0))
    pcol = lambda w, off: pl.BlockSpec((tm, w), lambda i: (i, _rel(off) // w))
    full = lambda a: pl.BlockSpec(a.shape, lambda i: (0, 0))
    sds = jax.ShapeDtypeStruct
    return pl.pallas_call(
        body, name="post_fwd_bwd",
        grid=(t // tm,),
        in_specs=[row(MLA_WIDTH), pcol(GLA_DV, P_ZGLA), pcol(D_MODEL, P_GMLA), pcol(D_MODEL, P_GGLA),
                  pcol(MLA_WIDTH, P_ZMLA), row(GLA_DV), row(D_MODEL), row(D_MODEL),
                  full(g_gla), full(g_final), full(w_pm), full(w_pg), full(w_o)],
        out_specs=(row(D_MODEL), row(MLA_WIDTH), row(GLA_DV), row(gw),
                   row(D_MODEL), row(MLA_WIDTH), row(GLA_DV), row(D_MODEL), row(D_MODEL),
                   pl.BlockSpec((1, LANE), lambda i: (0, 0)),
                   pl.BlockSpec((1, D_MODEL), lambda i: (0, 0)),
                   pl.BlockSpec((1, GLA_HV), lambda i: (0, 0))),
        out_shape=(sds((t, D_MODEL), F32), sds((t, MLA_WIDTH), F32), sds((t, GLA_DV), BF16),
                   sds((t, gw), BF16),
                   sds((t, D_MODEL), BF16), sds((t, MLA_WIDTH), BF16), sds((t, GLA_DV), BF16),
                   sds((t, D_MODEL), BF16), sds((t, D_MODEL), BF16),
                   sds((1, LANE), F32), sds((1, D_MODEL), F32), sds((1, GLA_HV), F32)),
        compiler_params=_cparams(("arbitrary",)),
    )(o_mla, proj, proj, proj, proj, o_gla, x, target, g_gla, g_final, w_pm, w_pg, w_o)


def _mla_prep_bwd(dq, dk, dv, dla, pre, proj, rq, rkv, g_q, g_kv, w_uq_p, w_k_p, w_v, w_gate_p,
                  rc, rsn, rsp):
    t = proj.shape[0]
    tm = min(256, t)
    gw = P_GROUPS[2][1]

    def body(dq_ref, dk_ref, dv_ref, dla_ref, pre_ref, cq_ref, ckv_ref, rq_ref, rkv_ref,
             gq_ref, gkv_ref, wuq_ref, wk_ref, wv_ref, wg_ref, c_ref, sn_ref, sp_ref,
             dg_ref, dqpre_ref, dpre_ref, dgq_ref, dgkv_ref, dbg_ref):
        @pl.when(pl.program_id(0) == 0)
        def _():
            dgq_ref[...] = jnp.zeros_like(dgq_ref)
            dgkv_ref[...] = jnp.zeros_like(dgkv_ref)
            dbg_ref[...] = jnp.zeros_like(dbg_ref)

        c, sn, sp = c_ref[...], sn_ref[...], sp_ref[...]
        dkr = jnp.zeros((tm, LANE), F32)
        for h in range(MLA_HEADS):
            sl = slice(h * HEAD_PAD, (h + 1) * HEAD_PAD)
            dqpre_ref[:, sl] = _rope_bwd(dq_ref[:, sl].astype(F32), c, sn, sp).astype(BF16)
            dkr = dkr + dk_ref[:, sl]
        dcqn = _dot_nt(dqpre_ref[...], wuq_ref[...])
        rq = rq_ref[...]
        xh = cq_ref[:, :MLA_Q_RANK] * rq
        dgq_ref[...] += jnp.sum(dcqn * xh, axis=0, keepdims=True)
        dxh = dcqn * gq_ref[...]
        dcq = rq * (dxh - xh * jnp.mean(dxh * xh, axis=-1, keepdims=True))
        dg_ref[:, :MLA_Q_RANK] = dcq.astype(BF16)
        dg_ref[:, MLA_Q_RANK:512] = jnp.zeros((tm, 512 - MLA_Q_RANK), BF16)

        dckvn = _dot_nt(dk_ref[...].astype(BF16), wk_ref[...]) + \
            _dot_nt(dv_ref[...].astype(BF16), wv_ref[...])
        rkv = rkv_ref[...]
        xh = ckv_ref[...] * rkv
        dgkv_ref[...] += jnp.sum(dckvn * xh, axis=0, keepdims=True)
        dxh = dckvn * gkv_ref[...]
        dg_ref[:, P_CKV - P_CQ:P_CKV - P_CQ + MLA_KV_RANK] = (
            rkv * (dxh - xh * jnp.mean(dxh * xh, axis=-1, keepdims=True))).astype(BF16)

        dlog_a = _dot_exact(_chunk_tri(tm, False), dla_ref[...])
        dpre = dlog_a * (1.0 / GLA_GATE_NORM) * (1.0 - _sigmoid(pre_ref[...]))
        dbg_ref[...] += jnp.sum(dpre, axis=0, keepdims=True)
        dpre = dpre.astype(BF16)
        dpre_ref[...] = dpre
        lane = lax.broadcasted_iota(jnp.int32, (tm, LANE), 1)
        in_kr = jnp.logical_and(lane >= MISC_KR, lane < MISC_KR + MLA_ROPE)
        dmisc = jnp.where(in_kr, _rope_bwd(dkr, c, sn, sp), 0.0) + _dot_nt(dpre, wg_ref[...])
        dg_ref[:, P_MISC - P_CQ:P_MISC - P_CQ + LANE] = dmisc.astype(BF16)

    hq = MLA_HEADS * HEAD_PAD
    row = lambda w: pl.BlockSpec((tm, w), lambda i: (i, 0))
    full = lambda a: pl.BlockSpec(a.shape, lambda i: (0, 0))
    acc = lambda w: pl.BlockSpec((1, w), lambda i: (0, 0))
    sds = jax.ShapeDtypeStruct
    return pl.pallas_call(
        body, name="mla_prep_bwd",
        grid=(t // tm,),
        in_specs=[row(hq), row(hq), row(MLA_WIDTH), row(GLA_DK), row(GLA_DK),
                  pl.BlockSpec((tm, 512), lambda i: (i, _rel(P_CQ) // 512)),
                  pl.BlockSpec((tm, MLA_KV_RANK), lambda i: (i, _rel(P_CKV) // MLA_KV_RANK)),
                  row(1), row(1), full(g_q), full(g_kv), full(w_uq_p), full(w_k_p), full(w_v),
                  full(w_gate_p), row(LANE), row(LANE), row(LANE)],
        out_specs=(row(gw), row(hq), row(GLA_DK),
                   acc(MLA_Q_RANK), acc(MLA_KV_RANK), acc(GLA_DK)),
        out_shape=(sds((t, gw), BF16), sds((t, hq), BF16), sds((t, GLA_DK), BF16),
                   sds((1, MLA_Q_RANK), F32), sds((1, MLA_KV_RANK), F32), sds((1, GLA_DK), F32)),
        compiler_params=_cparams(("arbitrary",)),
    )(dq, dk, dv, dla, pre, proj, proj, rq, rkv, g_q, g_kv, w_uq_p, w_k_p, w_v, w_gate_p,
      rc, rsn, rsp)


def _inproj_bwd(dgroups, w_pts, x, rstd, g_in, dx2, after):
    t = x.shape[0]
    tm = min(256, t)

    def body(d0_ref, d1_ref, d2_ref, w0_ref, w1_ref, w2_ref, x_ref, r_ref, g_ref, dx2_ref, after_ref,
             dx_ref, dg_ref):
        del after_ref

        @pl.when(pl.program_id(0) == 0)
        def _():
            dg_ref[...] = jnp.zeros_like(dg_ref)

        dh = jnp.zeros((tm, D_MODEL), F32)
        for d_ref, w_ref in zip((d0_ref, d1_ref, d2_ref), (w0_ref, w1_ref, w2_ref)):
            dh = dh + _dot(d_ref[...], w_ref[...])
        r = r_ref[...]
        xh = x_ref[...] * r
        dg_ref[...] += jnp.sum(dh * xh, axis=0, keepdims=True)
        dxh = dh * g_ref[...]
        dx_ref[...] = dx2_ref[...] + r * (dxh - xh * jnp.mean(dxh * xh, axis=-1, keepdims=True))

    row = lambda w: pl.BlockSpec((tm, w), lambda i: (i, 0))
    return pl.pallas_call(
        body, name="inproj_bwd",
        grid=(t // tm,),
        in_specs=[row(w) for _, w in P_GROUPS]
        + [pl.BlockSpec((w, D_MODEL), lambda i: (0, 0)) for _, w in P_GROUPS]
        + [row(D_MODEL), row(1), pl.BlockSpec((1, D_MODEL), lambda i: (0, 0)), row(D_MODEL),
           pl.BlockSpec(memory_space=pl.ANY)],
        out_specs=(row(D_MODEL), pl.BlockSpec((1, D_MODEL), lambda i: (0, 0))),
        out_shape=(jax.ShapeDtypeStruct((t, D_MODEL), F32),
                   jax.ShapeDtypeStruct((1, D_MODEL), F32)),
        compiler_params=_cparams(("arbitrary",)),
    )(*dgroups, *w_pts, x, rstd, g_in, dx2, after)


def _matmul(name, a, b, tm, tn, dtype=F32):
    kk, m = a.shape
    n = b.shape[1]

    def body(a_ref, b_ref, o_ref):
        o_ref[...] = _dot_tn(a_ref[...].astype(BF16), b_ref[...].astype(BF16)).astype(dtype)

    return pl.pallas_call(
        body, name=name,
        grid=(n // tn, m // tm),
        in_specs=[pl.BlockSpec((kk, tm), lambda j, i: (0, i)),
                  pl.BlockSpec((kk, tn), lambda j, i: (0, j))],
        out_specs=pl.BlockSpec((tm, tn), lambda j, i: (i, j)),
        out_shape=jax.ShapeDtypeStruct((m, n), dtype),
        compiler_params=_cparams(("arbitrary", "arbitrary")),
    )(a, b)


def _adamw_update(part_refs, w_ref, m_ref, v_ref, g_ref, d_ref, nm_ref, nv_ref):
    g = part_refs[0][...].astype(F32)
    for p_ref in part_refs[1:]:
        g = g + p_ref[...].astype(F32)
    m_new = ADAM_B1 * m_ref[...] + (1.0 - ADAM_B1) * g
    v_new = ADAM_B2 * v_ref[...] + (1.0 - ADAM_B2) * (g * g)
    m_hat = m_new / (1.0 - ADAM_B1 ** ADAM_STEP)
    v_hat = v_new / (1.0 - ADAM_B2 ** ADAM_STEP)
    g_ref[...] = g
    nm_ref[...] = m_new
    nv_ref[...] = v_new
    d_ref[...] = -ADAM_LR * (m_hat / (jnp.sqrt(v_hat) + ADAM_EPS) + ADAM_WD * w_ref[...])


def _adamw_rows(name, parts, w, m, v, tr, first=None):
    _, rows, cols = w.shape
    slots = parts.shape[0]

    def body(*refs):
        lead_refs, p_ref = ([], refs[0]) if first is None else ([refs[0]], refs[1])
        _adamw_update(lead_refs + [p_ref.at[q] for q in range(slots)], *refs[len(lead_refs) + 1:])

    blk = pl.BlockSpec((None, tr, cols), lambda i: (0, i, 0))
    out = jax.ShapeDtypeStruct((1, rows, cols), F32)
    lead = [] if first is None else [pl.BlockSpec((tr, cols), lambda i: (i, 0))]
    return pl.pallas_call(
        body, name=name,
        grid=(rows // tr,),
        in_specs=lead + [pl.BlockSpec((slots, tr, cols), lambda i: (0, i, 0)), blk, blk, blk],
        out_specs=(blk, blk, blk, blk),
        out_shape=(out, out, out, out),
        compiler_params=_cparams(("arbitrary",)),
    )(*([] if first is None else [first]), parts, w, m, v)


def _adamw_transposed(name, first, parts, w, m, v, tl):
    _, rows, cols = w.shape
    slots = parts.shape[0]

    def body(f_ref, p_ref, *refs):
        _adamw_update([f_ref] + [p_ref.at[q] for q in range(slots)], *refs)

    blk = pl.BlockSpec((cols, None, tl), lambda i: (0, 0, i))
    out = jax.ShapeDtypeStruct((cols, 1, rows), F32)
    res = pl.pallas_call(
        body, name=name,
        grid=(rows // tl,),
        in_specs=[pl.BlockSpec((cols, tl), lambda i: (0, i)),
                  pl.BlockSpec((slots, cols, tl), lambda i: (0, 0, i)), blk, blk, blk],
        out_specs=(blk, blk, blk, blk),
        out_shape=(out, out, out, out),
        compiler_params=_cparams(("arbitrary",)),
    )(first, parts, *[a.transpose(2, 0, 1) for a in (w, m, v)])
    return [r.transpose(1, 2, 0) for r in res]


def _adamw_group(firsts, parts, ws, ms, vs):
    n = len(ws)

    def body(*refs):
        ins, outs = refs[:5 * n], refs[5 * n:]
        for a in range(n):
            _adamw_update([ins[a]] + [ins[n + a].at[q] for q in range(ins[n + a].shape[0])],
                          *[r.at[0] for r in (ins[2 * n + a], ins[3 * n + a], ins[4 * n + a])],
                          *[r.at[0] for r in outs[4 * a:4 * a + 4]])

    vmem = lambda k: [pl.BlockSpec(memory_space=pltpu.VMEM) for _ in range(k)]
    out_shape = []
    for w in ws:
        out_shape += [jax.ShapeDtypeStruct(w.shape, F32)] * 4
    res = pl.pallas_call(
        body, name="adamw_small_weights",
        in_specs=vmem(5 * n), out_specs=tuple(vmem(4 * n)), out_shape=tuple(out_shape),
        compiler_params=_cparams(),
    )(*firsts, *parts, *ws, *ms, *vs)
    return [res[4 * a:4 * a + 4] for a in range(n)]


def _rope_tables(positions):
    half = MLA_ROPE // 2
    freqs = ROPE_THETA ** (-jnp.arange(half, dtype=F32) / half)
    ang = positions.astype(F32).reshape(-1, 1) * freqs
    cos, sin = jnp.cos(ang), jnp.sin(ang)
    t = ang.shape[0]
    one, zero = jnp.ones((t, MLA_NOPE), F32), jnp.zeros((t, half), F32)
    tail = jnp.zeros((t, LANE - MLA_QK), F32)
    rc = jnp.concatenate([one, cos, cos, tail], axis=1)
    rsn = jnp.concatenate([0.0 * one, -sin, zero, tail], axis=1)
    rsp = jnp.concatenate([0.0 * one, zero, sin, tail], axis=1)
    return rc, rsn, rsp


def _cols_full(g):
    return g.transpose(1, 0, 2)


def kernel(x, positions, g_in, w_in, g_q, w_uq, g_kv, w_ukv, w_gla_gate, b_gla_gate, g_gla, w_proj_mla, w_proj_gla, w_out, g_final, loss_target, m_g_in, m_w_in, m_g_q, m_w_uq, m_g_kv, m_w_ukv, m_w_gla_gate, m_b_gla_gate, m_g_gla, m_w_proj_mla, m_w_proj_gla, m_w_out, m_g_final, v_g_in, v_w_in, v_g_q, v_w_uq, v_g_kv, v_w_ukv, v_w_gla_gate, v_b_gla_gate, v_g_gla, v_w_proj_mla, v_w_proj_gla, v_w_out, v_g_final):
    t = x.shape[1]
    x2d = x.reshape(t, D_MODEL)
    tgt = loss_target.reshape(t, D_MODEL)
    g_final2 = g_final.reshape(1, D_MODEL)
    sharded = [(w_in, m_w_in, v_w_in), (w_uq, m_w_uq, v_w_uq), (w_ukv, m_w_ukv, v_w_ukv),
               (w_gla_gate, m_w_gla_gate, v_w_gla_gate), (w_proj_mla, m_w_proj_mla, v_w_proj_mla),
               (w_proj_gla, m_w_proj_gla, v_w_proj_gla), (w_out, m_w_out, v_w_out)]

    w_in_t = w_in.transpose(2, 0, 1).reshape(SHARD_COLS, D_MODEL)
    everyone = tuple(range(N_DEV))
    w_in_b = w_in_t.astype(BF16)
    b_uq, b_ukv, b_gate, b_pm, b_pg, b_o = [s[0][0].astype(BF16) for s in sharded[1:]]
    stages = ((0, 4), (1, 2, 3), (5, 6, 7))
    where = {d: (k, i) for k, srcs in enumerate(stages) for i, d in enumerate(srcs)}
    g_in_1, g_uq, g_ukv, g_gate = _all_gather(
        "all_gather_first", [w_in_b, b_uq, b_ukv, b_gate], [stages[0]] + [everyone] * 3)
    w_uq_p = jnp.pad(_cols_full(g_uq), ((0, 0), (0, 0), (0, HEAD_PAD - MLA_QK))).reshape(
        MLA_Q_RANK, MLA_HEADS * HEAD_PAD)
    ukv = _cols_full(g_ukv)
    w_k_p = jnp.pad(ukv[:, :, :MLA_NOPE], ((0, 0), (0, 0), (0, HEAD_PAD - MLA_NOPE))).reshape(
        MLA_KV_RANK, MLA_HEADS * HEAD_PAD)
    w_v = ukv[:, :, MLA_NOPE:].reshape(MLA_KV_RANK, MLA_WIDTH)
    w_gate_p = jnp.pad(_cols_full(g_gate).reshape(GLA_GATE_RANK, GLA_DK),
                       ((MISC_ALR, LANE - MISC_ALR - GLA_GATE_RANK), (0, 0)))
    rc, rsn, rsp = _rope_tables(positions)

    w_lat = _weights_to_p("weights_latents", [g_in_1], where, 2)
    proj_lat, h, rstd = _inproj(x2d, g_in, w_lat)
    q, k, v, log_a, pre, cqn, ckvn, rq, rkv, misc = _mla_prep(
        proj_lat, g_q, g_kv, w_uq_p, w_k_p, w_v, w_gate_p, b_gla_gate, rc, rsn, rsp)
    o_mla, lse, (g_in_2,) = _mla_attn_fwd(q, k, v, [w_in_b], [stages[1]])
    w_gla = _weights_to_p("weights_gla", [g_in_1, g_in_2], where, 0)
    proj_gla = _proj("inproj_gla", h, w_gla)
    o_gla, states, (g_in_3, g_pm, g_pg, g_o) = _gla_fwd(
        proj_gla, log_a, [w_in_b, b_pm, b_pg, b_o], [stages[2]] + [everyone] * 3)
    w_out_path = _weights_to_p("weights_out_path", [g_in_1, g_in_2, g_in_3], where, 1)
    proj_out = _proj("inproj_out_path", h, w_out_path)
    w_in_p = (w_gla, w_out_path, w_lat)
    w_pm = _cols_full(g_pm).reshape(MLA_WIDTH, D_MODEL)
    w_pg = g_pg.reshape(GLA_DV, D_MODEL)
    w_o = g_o.reshape(D_MODEL, D_MODEL)

    (dx2, do_mla, do_gla, d_out, merged, um, ug, dym, dyg, loss_p, dg_final,
     dg_gla) = _post(o_mla, proj_out, o_gla, x2d, tgt, g_gla, g_final2, w_pm, w_pg, w_o)

    p_pm = _matmul("dw_proj_mla", um, dym, 512, 512, BF16).reshape(
        MLA_WIDTH, N_DEV, D_MODEL // N_DEV).transpose(1, 0, 2)
    p_pg = _matmul("dw_proj_gla", ug, dyg, 512, 512, BF16).reshape(N_DEV, -1, D_MODEL)
    p_o = _matmul("dw_out", merged, dx2, 512, 512, BF16).reshape(N_DEV, -1, D_MODEL)
    owns = [lax.empty((SHARD_COLS, D_MODEL), BF16)] + [lax.empty(p.shape[1:], BF16)
                                                       for p in (p_pm, p_pg, p_o)]
    lands = [lax.empty((3,) + o.shape, BF16) for o in owns]
    dw_groups, started = {}, []

    def reduce_scatter_stage(s, dests, extra=()):
        assert set(g for d in dests for g in _shard_groups(d)) <= set(dw_groups)
        k = 1 + len(extra)
        parts = [_grads_to_shards("grads_to_shards_%d" % s, dw_groups, dests)] + list(extra)
        all_dests = [dests] + [everyone] * len(extra)
        sums, owns[:k] = _swap_add("swap_add_%d" % s, parts, owns[:k], all_dests)
        sems, sums, lands[:k], token = _ici_start("ici_start_%d" % s, sums, lands[:k], all_dests)
        started.extend((sems[a][0], sems[a][1], sums[a], a, all_dests[a]) for a in range(k))
        return token

    dw_groups[1] = _matmul("dw_in_1", d_out, h, 512, 512, BF16)
    token = reduce_scatter_stage(1, (5, 6, 7), (p_pm, p_pg, p_o))
    d_gla, dla = _gla_bwd(proj_gla, log_a, do_gla, states, token)
    dw_groups[0] = _matmul("dw_in_0", d_gla, h, 512, 512, BF16)
    token = reduce_scatter_stage(2, (1, 2, 3))
    dq, dk, dv = _mla_attn_bwd(q, k, v, o_mla, do_mla, lse, token)
    d_lat, dqpre, dpre, dg_q, dg_kv, db_gate = _mla_prep_bwd(
        dq, dk, dv, dla, pre, proj_lat, rq, rkv, g_q, g_kv, w_uq_p, w_k_p, w_v, w_gate_p, rc, rsn, rsp)
    dw_groups[2] = _matmul("dw_in_2", d_lat, h, 896, 512, BF16)
    token = reduce_scatter_stage(3, (0, 4))
    grad_x, dg_in = _inproj_bwd((d_gla, d_out, d_lat), w_in_p, x2d, rstd, g_in, dx2, token)

    dw_uq = _matmul("dw_uq", cqn, dqpre, MLA_Q_RANK, 512, BF16)
    p_uq = dw_uq.reshape(MLA_Q_RANK, MLA_HEADS, HEAD_PAD)[:, :, :MLA_QK].transpose(1, 0, 2)
    dw_k = _matmul("dw_uk", ckvn, dk, MLA_KV_RANK, 512, BF16)
    dw_v = _matmul("dw_uv", ckvn, dv, MLA_KV_RANK, 512, BF16)
    p_ukv = jnp.concatenate(
        [dw_k.reshape(MLA_KV_RANK, MLA_HEADS, HEAD_PAD)[:, :, :MLA_NOPE],
         dw_v.reshape(MLA_KV_RANK, MLA_HEADS, MLA_VDIM)], axis=2).transpose(1, 0, 2)
    dw_gate = _matmul("dw_gate", misc, dpre, LANE, 512, BF16)
    p_gate = dw_gate[MISC_ALR:MISC_ALR + GLA_GATE_RANK].reshape(
        GLA_GATE_RANK, N_DEV, GLA_DK // N_DEV).transpose(1, 0, 2)
    small = jnp.concatenate([dg_in.reshape(-1), dg_q.reshape(-1), dg_kv.reshape(-1),
                             db_gate.reshape(-1), dg_gla.reshape(-1), dg_final.reshape(-1),
                             loss_p[0, :1]])
    small = jnp.pad(small, (0, SMALL_ROWS * LANE - small.shape[0])).reshape(SMALL_ROWS, LANE)

    recv = _exchange_grads([p_uq, p_ukv, p_gate], small)
    lands = _ici_wait("ici_wait", started, lands, recv[3])
    big = [_adamw_transposed("adamw_w_in", owns[0], lands[0], *sharded[0], 256)]
    big += _adamw_group([r[0] for r in recv[:3]] + list(owns[1:]),
                        [r[1:] for r in recv[:3]] + list(lands[1:]),
                        *[[s[j] for s in sharded[1:]] for j in range(3)])
    replicated = [(g_in, m_g_in, v_g_in), (g_q, m_g_q, v_g_q), (g_kv, m_g_kv, v_g_kv),
                  (b_gla_gate, m_b_gla_gate, v_b_gla_gate), (g_gla, m_g_gla, v_g_gla),
                  (g_final, m_g_final, v_g_final)]
    spacks = [jnp.pad(jnp.concatenate([s[j].reshape(-1) for s in replicated]),
                      (0, SMALL_ROWS * LANE - sum(SMALL_SIZES))).reshape(1, SMALL_ROWS, LANE)
              for j in range(3)]
    tiny = _adamw_rows("adamw_gains", recv[3], spacks[0], spacks[1], spacks[2], SMALL_ROWS)

    outs = {}
    names = ("w_in", "w_uq", "w_ukv", "w_gla_gate", "w_proj_mla", "w_proj_gla", "w_out")
    for j, kind in enumerate(("grad", "delta", "new_m", "new_v")):
        for name, res in zip(names, big):
            outs[kind, name] = res[j]
        flat = tiny[j].reshape(-1)
        off = 0
        for name, size in zip(("g_in", "g_q", "g_kv", "b_gla_gate", "g_gla", "g_final"), SMALL_SIZES):
            shape = (size,) if name == "g_final" else (1, size)
            outs[kind, name] = flat[off:off + size].reshape(shape)
            off += size
    loss = tiny[0].reshape(-1)[sum(SMALL_SIZES)]
    order = ("g_in", "w_in", "g_q", "w_uq", "g_kv", "w_ukv", "w_gla_gate", "b_gla_gate", "g_gla",
             "w_proj_mla", "w_proj_gla", "w_out", "g_final")
    result = [loss, grad_x.reshape(1, t, D_MODEL)]
    for kind in ("grad", "delta", "new_m", "new_v"):
        result += [outs[kind, name] for name in order]
    return tuple(result)
```

```python
import math

import jax
import jax.numpy as jnp
from jax import lax
from jax.experimental import pallas as pl
from jax.experimental.pallas import tpu as pltpu

F32 = jnp.float32
BF16 = jnp.bfloat16
MESH = pl.DeviceIdType.MESH
N_DEV = 8

D_MODEL = 1024
EPS = 1e-6
MLA_HEADS = 8
MLA_NOPE = 64
MLA_ROPE = 32
MLA_VDIM = 64
MLA_Q_RANK = 384
MLA_KV_RANK = 256
MLA_QK = MLA_NOPE + MLA_ROPE
MLA_WIDTH = MLA_HEADS * MLA_VDIM
ROPE_THETA = 10000.0
GLA_HEADS = 4
GLA_DK = 512
GLA_DV = 1024
GLA_HK = 128
GLA_HV = 256
GLA_GATE_RANK = 16
GLA_GATE_NORM = 16.0
GLA_CHUNK = 64
D_IN = 6320

ADAM_LR = 0.001
ADAM_B1 = 0.9
ADAM_B2 = 0.999
ADAM_EPS = 1e-08
ADAM_WD = 0.01
ADAM_STEP = 10

LANE = 128
HEAD_PAD = 128
VMEM_LIMIT = 48 * 1024 * 1024

P_VG, P_QG, P_KG = 0, 1024, 1536
P_ZGLA, P_GMLA, P_GGLA, P_ZMLA = 2048, 3072, 4096, 5120
P_CQ, P_CKV, P_MISC = 5632, 6144, 6400
P_TOTAL = 6528
P_GROUPS = ((0, 2048), (2048, 3584), (5632, 896))
MISC_KR = 64
MISC_ALR = 96
SHARD_COLS = D_IN // N_DEV
P_COMPONENTS = ((0, 384, P_CQ), (384, 256, P_CKV), (640, 32, P_MISC + MISC_KR), (672, 512, P_ZMLA),
                (1184, 512, P_QG), (1696, 512, P_KG), (2208, 1024, P_VG),
                (3232, 16, P_MISC + MISC_ALR), (3248, 1024, P_ZGLA), (4272, 1024, P_GMLA),
                (5296, 1024, P_GGLA))

SMALL_SIZES = (1024, 384, 256, 512, 256, 1024)
SMALL_ROWS = 32


def _segments():
    segs = []
    for g0, n, p0 in P_COMPONENTS:
        g = g0
        while g < g0 + n:
            d = g // SHARD_COLS
            end = min(g0 + n, (d + 1) * SHARD_COLS)
            segs.append((d, g - d * SHARD_COLS, end - g, p0 + g - g0))
            g = end
    return segs


def _group_of(p0):
    return max(i for i, (off, _) in enumerate(P_GROUPS) if off <= p0)


def _rel(p0):
    return p0 - P_GROUPS[_group_of(p0)][0]


def _cparams(sem=None):
    if sem is None:
        return pltpu.CompilerParams(vmem_limit_bytes=VMEM_LIMIT)
    return pltpu.CompilerParams(dimension_semantics=sem, vmem_limit_bytes=VMEM_LIMIT)


def _sigmoid(v):
    return 1.0 / (1.0 + jnp.exp(-v))


def _dot(a, b):
    return jnp.dot(a, b, preferred_element_type=F32)


def _dot_nt(a, b):
    return lax.dot_general(a, b, (((1,), (1,)), ((), ())), preferred_element_type=F32)


def _dot_tn(a, b):
    return lax.dot_general(a, b, (((0,), (0,)), ((), ())), preferred_element_type=F32)


def _dot_exact(a, b):
    return jnp.dot(a, b, preferred_element_type=F32, precision=lax.Precision.HIGHEST)


def _rope_fwd(blk, c, sn, sp):
    return blk * c + pltpu.roll(blk, LANE - 16, 1) * sn + pltpu.roll(blk, 16, 1) * sp


def _rope_bwd(blk, c, sn, sp):
    return blk * c + pltpu.roll(blk * sn, 16, 1) + pltpu.roll(blk * sp, LANE - 16, 1)


def _mesh_pos():
    return lax.axis_index("x"), lax.axis_index("y"), lax.axis_index("c")


def _hbm_specs(n):
    return [pl.BlockSpec(memory_space=pltpu.HBM) for _ in range(n)]


def _dev(d):
    return d >> 2, (d >> 1) & 1, d & 1


def _gather_plan(shards, sources):
    na, most = len(shards), max(len(s) for s in sources)
    out_shape = [jax.ShapeDtypeStruct((len(srcs),) + s.shape, s.dtype)
                 for s, srcs in zip(shards, sources)]
    sems = [pltpu.SemaphoreType.DMA((na, most)) for _ in range(3)]
    sems += [pltpu.SemaphoreType.DMA((na, most, 3))]
    sems += [pltpu.SemaphoreType.DMA((na, most)) for _ in range(3)]
    return out_shape, sems


def _gather_hooks(x_refs, out_refs, sems, sources):
    local_sems, d2d_send, d2d_recv, ici_send, ici_recv, fwd_send, fwd_recv = sems
    x, y, c = _mesh_pos()
    chips = [(1 - x, y), (x, 1 - y), (1 - x, 1 - y)]
    items = []
    for a, srcs in enumerate(sources):
        for i, d in enumerate(srcs):
            dx, dy, dc = _dev(d)
            near = jnp.logical_and(x == dx, y == dy)
            far = jnp.logical_not(near)
            slot = out_refs[a].at[i]

            def remote(src, to, send_sem, recv_sem, slot=slot):
                return pltpu.make_async_remote_copy(
                    src_ref=src, dst_ref=slot, send_sem=send_sem, recv_sem=recv_sem,
                    device_id=to, device_id_type=MESH)

            items.append(dict(
                me=jnp.logical_and(near, c == dc), sibling=jnp.logical_and(near, c != dc),
                relay=jnp.logical_and(far, c == dc), behind=jnp.logical_and(far, c != dc),
                local=pltpu.make_async_copy(x_refs[a], slot, local_sems.at[a, i]),
                to_sibling=remote(x_refs[a], (x, y, 1 - c), d2d_send.at[a, i], d2d_recv.at[a, i]),
                to_chips=[remote(x_refs[a], (*chip, c), ici_send.at[a, i, j], ici_recv.at[a, i])
                          for j, chip in enumerate(chips)],
                forward=remote(slot, (x, y, 1 - c), fwd_send.at[a, i], fwd_recv.at[a, i])))

    def start():
        for it in items:
            @pl.when(it["me"])
            def _(it=it):
                it["local"].start()
                it["to_sibling"].start()
                for cp in it["to_chips"]:
                    cp.start()

    def finish():
        for it in items:
            @pl.when(it["relay"])
            def _(it=it):
                it["to_chips"][0].wait_recv()
                it["forward"].start()
        for it in items:
            pl.when(it["sibling"])(it["to_sibling"].wait_recv)
            pl.when(it["behind"])(it["forward"].wait_recv)
            pl.when(it["relay"])(it["forward"].wait_send)

            @pl.when(it["me"])
            def _(it=it):
                it["local"].wait()
                it["to_sibling"].wait_send()
                for cp in it["to_chips"]:
                    cp.wait_send()

    return start, finish


def _all_gather(name, shards, sources):
    n = len(shards)
    out_shape, sems = _gather_plan(shards, sources)

    def body(*refs):
        start, finish = _gather_hooks(refs[:n], refs[n:2 * n], refs[2 * n:], sources)
        start()
        finish()

    return pl.pallas_call(
        body, name=name,
        out_shape=tuple(out_shape),
        in_specs=_hbm_specs(n), out_specs=tuple(_hbm_specs(n)),
        scratch_shapes=sems,
        compiler_params=_cparams(),
    )(*shards)


def _exchange_grads(parts, small):
    n = len(parts)
    rows_per_add = 256

    def body(*refs):
        p_refs, s_ref = refs[:n], refs[n]
        out_refs, sout_ref = refs[n + 1:2 * n + 1], refs[2 * n + 1]
        mine_v, recv_v = refs[2 * n + 2:3 * n + 2], refs[3 * n + 2:4 * n + 2]
        (d2d_send, d2d_recv, ici_send, ici_recv, s_send, s_recv, load_sems, own_sems,
         sown_sem) = refs[4 * n + 2:]
        x, y, c = _mesh_pos()
        me = 4 * x + 2 * y + c
        chips = [(x, y), (1 - x, y), (x, 1 - y), (1 - x, 1 - y)]

        sown = pltpu.make_async_copy(s_ref, sout_ref.at[me], sown_sem)
        sown.start()
        tiny = []
        for k in range(1, N_DEV):
            peer = (x ^ (k >> 2), y ^ ((k >> 1) & 1), c ^ (k & 1))
            tiny.append(pltpu.make_async_remote_copy(
                src_ref=s_ref, dst_ref=sout_ref.at[me], send_sem=s_send.at[k - 1],
                recv_sem=s_recv.at[k - 1], device_id=peer, device_id_type=MESH))
        for cp in tiny:
            cp.start()

        loads, swaps = [], []
        for a in range(n):
            for j, (px, py) in enumerate(chips):
                loads.append(pltpu.make_async_copy(
                    p_refs[a].at[4 * px + 2 * py + c], mine_v[a].at[j], load_sems.at[a, j]))
                swaps.append(pltpu.make_async_remote_copy(
                    src_ref=p_refs[a].at[4 * px + 2 * py + 1 - c], dst_ref=recv_v[a].at[j],
                    send_sem=d2d_send.at[a, j], recv_sem=d2d_recv.at[a, j],
                    device_id=(x, y, 1 - c), device_id_type=MESH))
        for cp in swaps + loads:
            cp.start()

        outgoing, own = [], []
        for a in range(n):
            rows = p_refs[a].shape[1]
            step = math.gcd(rows, rows_per_add)
            for j in range(4):
                loads[4 * a + j].wait()
                swaps[4 * a + j].wait_recv()

                @pl.loop(0, rows // step)
                def _(i):
                    rs = pl.ds(pl.multiple_of(i * step, step), step)
                    mine_v[a][j, rs, :] = (mine_v[a][j, rs, :].astype(F32)
                                           + recv_v[a][j, rs, :].astype(F32)).astype(BF16)

                if j == 0:
                    own.append(pltpu.make_async_copy(mine_v[a].at[0], out_refs[a].at[0],
                                                     own_sems.at[a]))
                    own[-1].start()
                else:
                    outgoing.append(pltpu.make_async_remote_copy(
                        src_ref=mine_v[a].at[j], dst_ref=out_refs[a].at[j],
                        send_sem=ici_send.at[a, j - 1], recv_sem=ici_recv.at[a, j - 1],
                        device_id=(*chips[j], c), device_id_type=MESH))
                    outgoing[-1].start()

        for cp in tiny + outgoing:
            cp.wait_recv()
        for cp in tiny + outgoing + swaps:
            cp.wait_send()
        for cp in own:
            cp.wait()
        sown.wait()

    outs = [jax.ShapeDtypeStruct((4,) + p.shape[1:], p.dtype) for p in parts]
    outs.append(jax.ShapeDtypeStruct((N_DEV,) + small.shape, small.dtype))
    stage = [pltpu.VMEM((4,) + p.shape[1:], p.dtype) for p in parts]
    return pl.pallas_call(
        body, name="exchange_grads",
        out_shape=tuple(outs),
        in_specs=_hbm_specs(n + 1), out_specs=tuple(_hbm_specs(n + 1)),
        scratch_shapes=stage + stage + [
            pltpu.SemaphoreType.DMA((n, 4)), pltpu.SemaphoreType.DMA((n, 4)),
            pltpu.SemaphoreType.DMA((n, 3)), pltpu.SemaphoreType.DMA((n, 3)),
            pltpu.SemaphoreType.DMA((7,)), pltpu.SemaphoreType.DMA((7,)),
            pltpu.SemaphoreType.DMA((n, 4)), pltpu.SemaphoreType.DMA((n,)),
            pltpu.SemaphoreType.DMA],
        compiler_params=_cparams(),
    )(*parts, small)


def _swap_add(name, parts, owns_prev, dests):
    na = len(parts)
    step = 256

    def body(*refs):
        p_refs, h_refs, own_refs = refs[:na], refs[2 * na:3 * na], refs[3 * na:4 * na]
        mine_vs, recv_vs = refs[4 * na:5 * na], refs[5 * na:6 * na]
        send_sems, recv_sems, load_sems, store_sems, own_sems = refs[6 * na:]
        x, y, c = _mesh_pos()
        work = []
        for a in range(na):
            for i, d in enumerate(dests[a]):
                dx, dy, dc = _dev(d)
                keep = c == dc
                mine = jnp.logical_and(keep, jnp.logical_and(x == dx, y == dy))
                swap = pltpu.make_async_remote_copy(
                    src_ref=p_refs[a].at[i], dst_ref=recv_vs[a].at[i], send_sem=send_sems.at[a, i],
                    recv_sem=recv_sems.at[a, i], device_id=(x, y, 1 - c), device_id_type=MESH)
                load = pltpu.make_async_copy(p_refs[a].at[i], mine_vs[a].at[i], load_sems.at[a, i])
                store = pltpu.make_async_copy(mine_vs[a].at[i], h_refs[a].at[i], store_sems.at[a, i])
                own = pltpu.make_async_copy(mine_vs[a].at[i], own_refs[a], own_sems.at[a])
                pl.when(keep)(load.start)
                pl.when(jnp.logical_not(keep))(swap.start)
                work.append((a, i, keep, mine, swap, load, store, own))
        for a, i, keep, mine, swap, load, store, own in work:
            cols = parts[a].shape[2]
            cstep = math.gcd(cols, step)

            @pl.when(keep)
            def _(a=a, i=i, mine=mine, swap=swap, load=load, store=store, own=own, cols=cols,
                  cstep=cstep):
                load.wait()
                swap.wait_recv()

                @pl.loop(0, cols // cstep)
                def _(r):
                    cs = pl.ds(pl.multiple_of(r * cstep, cstep), cstep)
                    mine_vs[a][i, :, cs] = (mine_vs[a][i, :, cs].astype(F32)
                                            + recv_vs[a][i, :, cs].astype(F32)).astype(BF16)

                store.start()
                pl.when(mine)(own.start)
        for a, i, keep, mine, swap, load, store, own in work:
            pl.when(jnp.logical_not(keep))(swap.wait_send)
            pl.when(keep)(store.wait)
            pl.when(mine)(own.wait)

    stage = [pltpu.VMEM(p.shape, BF16) for p in parts]
    most = max(len(d) for d in dests)
    res = pl.pallas_call(
        body, name=name,
        out_shape=tuple([jax.ShapeDtypeStruct(p.shape, BF16) for p in parts]
                        + [jax.ShapeDtypeStruct(o.shape, BF16) for o in owns_prev]),
        in_specs=_hbm_specs(2 * na), out_specs=tuple(_hbm_specs(2 * na)),
        input_output_aliases={na + a: na + a for a in range(na)},
        scratch_shapes=stage + stage + [pltpu.SemaphoreType.DMA((na, most)) for _ in range(4)]
        + [pltpu.SemaphoreType.DMA((na,))],
        compiler_params=_cparams(),
    )(*parts, *owns_prev)
    return res[:na], res[na:]


def _ici_copies(h_ref, land_ref, send_sems, recv_sems, dests):
    x, y, c = _mesh_pos()
    sends, arrivals = [], []
    for i, d in enumerate(dests):
        dx, dy, dc = _dev(d)
        j = (x != dx).astype(jnp.int32) + 2 * (y != dy).astype(jnp.int32)
        slot = jnp.maximum(j - 1, 0)
        sends.append((jnp.logical_and(c == dc, j > 0), pltpu.make_async_remote_copy(
            src_ref=h_ref.at[i], dst_ref=land_ref.at[slot], send_sem=send_sems.at[i],
            recv_sem=recv_sems.at[slot], device_id=(dx, dy, dc), device_id_type=MESH)))
        arrivals.append((jnp.logical_and(c == dc, j == 0), [pltpu.make_async_remote_copy(
            src_ref=h_ref.at[i], dst_ref=land_ref.at[r], send_sem=send_sems.at[i],
            recv_sem=recv_sems.at[r], device_id=(dx, dy, dc), device_id_type=MESH)
            for r in range(3)]))
    return sends, arrivals


def _ici_start(name, hs, lands, dests):
    na = len(hs)

    def body(*refs):
        h_refs, land_refs, sems = refs[:na], refs[na:2 * na], refs[2 * na:4 * na]
        token = refs[-1]
        for a in range(na):
            sends, _ = _ici_copies(h_refs[a], land_refs[a], sems[2 * a], sems[2 * a + 1], dests[a])
            for go, cp in sends:
                pl.when(go)(cp.start)
        token[...] = jnp.zeros_like(token)

    hbm, sem = pl.BlockSpec(memory_space=pltpu.HBM), pl.BlockSpec(memory_space=pltpu.SEMAPHORE)
    sem_shapes = []
    for a in range(na):
        sem_shapes += [pltpu.SemaphoreType.DMA((len(dests[a]),)), pltpu.SemaphoreType.DMA((3,))]
    res = pl.pallas_call(
        body, name=name,
        out_shape=tuple(sem_shapes) + tuple(pltpu.HBM(v.shape, v.dtype) for v in list(hs) + list(lands))
        + (jax.ShapeDtypeStruct((8, LANE), F32),),
        in_specs=(hbm,) * (2 * na),
        out_specs=(sem,) * (2 * na) + (hbm,) * (2 * na) + (pl.BlockSpec(memory_space=pltpu.VMEM),),
        input_output_aliases={i: 2 * na + i for i in range(2 * na)},
        compiler_params=pltpu.CompilerParams(
            has_side_effects=pltpu.SideEffectType.DATAFLOW_SIDE_EFFECTING,
            vmem_limit_bytes=VMEM_LIMIT),
    )(*[pltpu.with_memory_space_constraint(v, pltpu.HBM) for v in list(hs) + list(lands)])
    sems = [(res[2 * a], res[2 * a + 1]) for a in range(na)]
    return sems, res[2 * na:3 * na], res[3 * na:4 * na], res[-1]


def _ici_wait(name, started, lands, after):
    k, nl = len(started), len(lands)

    def body(*refs):
        land_refs = refs[3 * k:3 * k + nl]
        for s in range(k):
            h_ref, send_sems, recv_sems = refs[3 * s:3 * s + 3]
            sends, arrivals = _ici_copies(h_ref, land_refs[started[s][3]], send_sems, recv_sems,
                                          started[s][4])
            for go, cp in sends:
                pl.when(go)(cp.wait_send)
            for here, cps in arrivals:
                for cp in cps:
                    pl.when(here)(cp.wait_recv)

    hbm, sem = pl.BlockSpec(memory_space=pltpu.HBM), pl.BlockSpec(memory_space=pltpu.SEMAPHORE)
    operands, specs = [], []
    for send_sems, recv_sems, h, _, _ in started:
        operands += [h, send_sems, recv_sems]
        specs += [hbm, sem, sem]
    return pl.pallas_call(
        body, name=name,
        out_shape=tuple(pltpu.HBM(v.shape, v.dtype) for v in lands),
        in_specs=tuple(specs) + (hbm,) * nl + (pl.BlockSpec(memory_space=pl.ANY),),
        out_specs=(hbm,) * nl,
        input_output_aliases={3 * k + i: i for i in range(nl)},
        compiler_params=pltpu.CompilerParams(
            has_side_effects=pltpu.SideEffectType.DATAFLOW_SIDE_EFFECTING,
            vmem_limit_bytes=VMEM_LIMIT),
    )(*operands, *lands, after)


def _weights_to_p(name, gathered, where, group):
    tl = 256
    off, width = P_GROUPS[group]
    segs = sorted([s for s in _segments() if _group_of(s[3]) == group], key=lambda s: s[3])
    used = sorted({where[s[0]][0] for s in segs})

    def body(*refs):
        g_refs, o_ref = dict(zip(used, refs[:-1])), refs[-1]
        pieces, pos = [], off
        for d, c0, n, p0 in segs:
            if p0 > pos:
                pieces.append(jnp.zeros((p0 - pos, tl), F32))
            k, slot = where[d]
            pieces.append(g_refs[k][slot, c0:c0 + n, :].astype(F32))
            pos = p0 + n
        if off + width > pos:
            pieces.append(jnp.zeros((off + width - pos, tl), F32))
        o_ref[...] = jnp.concatenate(pieces, axis=0).astype(BF16)

    return pl.pallas_call(
        body, name=name,
        grid=(D_MODEL // tl,),
        in_specs=[pl.BlockSpec((gathered[k].shape[0], SHARD_COLS, tl), lambda i: (0, 0, i))
                  for k in used],
        out_specs=pl.BlockSpec((width, tl), lambda i: (0, i)),
        out_shape=jax.ShapeDtypeStruct((width, D_MODEL), BF16),
        compiler_params=_cparams(("arbitrary",)),
    )(*[gathered[k] for k in used])


def _shard_groups(d):
    return sorted({_group_of(s[3]) for s in _segments() if s[0] == d})


def _grads_to_shards(name, groups, dests):
    tl = 256
    segs = _segments()
    used = sorted(groups)

    def body(*refs):
        g_refs, o_ref = dict(zip(used, refs[:-1])), refs[-1]
        for i, d in enumerate(dests):
            pieces = []
            for _, c0, n, p0 in sorted([s for s in segs if s[0] == d], key=lambda s: s[1]):
                gi = _group_of(p0)
                lo = p0 - P_GROUPS[gi][0]
                pieces.append(g_refs[gi][lo:lo + n, :].astype(F32))
            o_ref[i] = jnp.concatenate(pieces, axis=0).astype(BF16)

    return pl.pallas_call(
        body, name=name,
        grid=(D_MODEL // tl,),
        in_specs=[pl.BlockSpec((P_GROUPS[g][1], tl), lambda i: (0, i)) for g in used],
        out_specs=pl.BlockSpec((len(dests), SHARD_COLS, tl), lambda i: (0, 0, i)),
        out_shape=jax.ShapeDtypeStruct((len(dests), SHARD_COLS, D_MODEL), BF16),
        compiler_params=_cparams(("arbitrary",)),
    )(*[groups[g] for g in used])


def _inproj(x, g_in, w_pt):
    t = x.shape[0]
    tm = min(256, t)
    width = w_pt.shape[0]

    def body(x_ref, g_ref, w_ref, proj_ref, h_ref, r_ref):
        xf = x_ref[...]
        r = lax.rsqrt(jnp.mean(xf * xf, axis=-1, keepdims=True) + EPS)
        h = ((xf * r) * g_ref[...]).astype(BF16)
        proj_ref[...] = _dot_nt(h, w_ref[...])
        h_ref[...] = h
        r_ref[...] = r

    row = lambda w: pl.BlockSpec((tm, w), lambda i: (i, 0))
    return pl.pallas_call(
        body, name="inproj_latents",
        grid=(t // tm,),
        in_specs=[row(D_MODEL), pl.BlockSpec((1, D_MODEL), lambda i: (0, 0)),
                  pl.BlockSpec((width, D_MODEL), lambda i: (0, 0))],
        out_specs=(row(width), row(D_MODEL), row(1)),
        out_shape=(jax.ShapeDtypeStruct((t, width), F32),
                   jax.ShapeDtypeStruct((t, D_MODEL), BF16),
                   jax.ShapeDtypeStruct((t, 1), F32)),
        compiler_params=_cparams(("arbitrary",)),
    )(x, g_in, w_pt)


def _proj(name, h, w_pt):
    t = h.shape[0]
    tm = min(256, t)
    width = w_pt.shape[0]

    def body(h_ref, w_ref, o_ref):
        o_ref[...] = _dot_nt(h_ref[...], w_ref[...])

    return pl.pallas_call(
        body, name=name,
        grid=(t // tm,),
        in_specs=[pl.BlockSpec((tm, D_MODEL), lambda i: (i, 0)),
                  pl.BlockSpec((width, D_MODEL), lambda i: (0, 0))],
        out_specs=pl.BlockSpec((tm, width), lambda i: (i, 0)),
        out_shape=jax.ShapeDtypeStruct((t, width), F32),
        compiler_params=_cparams(("arbitrary",)),
    )(h, w_pt)


def _mla_prep(proj, g_q, g_kv, w_uq_p, w_k_p, w_v, w_gate_p, b_gate, rc, rsn, rsp):
    t = proj.shape[0]
    tm = min(256, t)
    hq = MLA_HEADS * HEAD_PAD

    def body(cq_ref, ckv_ref, misc_ref, gq_ref, gkv_ref, wuq_ref, wk_ref, wv_ref, wg_ref, bg_ref,
             c_ref, sn_ref, sp_ref,
             q_ref, k_ref, v_ref, la_ref, pre_ref, cqn_ref, ckvn_ref, rq_ref, rkv_ref, mb_ref):
        c, sn, sp = c_ref[...], sn_ref[...], sp_ref[...]
        cq = cq_ref[:, :MLA_Q_RANK]
        rq = lax.rsqrt(jnp.mean(cq * cq, axis=-1, keepdims=True) + EPS)
        cqn = ((cq * rq) * gq_ref[...]).astype(BF16)
        cqn_ref[...] = cqn
        rq_ref[...] = rq
        qpre = _dot(cqn, wuq_ref[...])
        ckv = ckv_ref[...]
        rkv = lax.rsqrt(jnp.mean(ckv * ckv, axis=-1, keepdims=True) + EPS)
        ckvn = ((ckv * rkv) * gkv_ref[...]).astype(BF16)
        ckvn_ref[...] = ckvn
        rkv_ref[...] = rkv
        kn = _dot(ckvn, wk_ref[...])
        v_ref[...] = _dot(ckvn, wv_ref[...]).astype(BF16)
        misc = misc_ref[...]
        krope = _rope_fwd(misc, c, sn, sp)
        for h in range(MLA_HEADS):
            sl = slice(h * HEAD_PAD, (h + 1) * HEAD_PAD)
            q_ref[:, sl] = _rope_fwd(qpre[:, sl], c, sn, sp).astype(BF16)
            k_ref[:, sl] = (kn[:, sl] + krope).astype(BF16)
        mb_ref[...] = misc.astype(BF16)
        pre = _dot(mb_ref[...], wg_ref[...]) + bg_ref[...]
        pre_ref[...] = pre
        log_a = (jnp.minimum(pre, 0.0) - jnp.log(1.0 + jnp.exp(-jnp.abs(pre)))) / GLA_GATE_NORM
        la_ref[...] = _dot_exact(_chunk_tri(tm, True), log_a)

    row = lambda w: pl.BlockSpec((tm, w), lambda i: (i, 0))
    full = lambda a: pl.BlockSpec(a.shape, lambda i: (0, 0))
    return pl.pallas_call(
        body, name="mla_prep",
        grid=(t // tm,),
        in_specs=[pl.BlockSpec((tm, 512), lambda i: (i, _rel(P_CQ) // 512)),
                  pl.BlockSpec((tm, MLA_KV_RANK), lambda i: (i, _rel(P_CKV) // MLA_KV_RANK)),
                  pl.BlockSpec((tm, LANE), lambda i: (i, _rel(P_MISC) // LANE)),
                  full(g_q), full(g_kv), full(w_uq_p), full(w_k_p), full(w_v), full(w_gate_p),
                  full(b_gate), row(LANE), row(LANE), row(LANE)],
        out_specs=(row(hq), row(hq), row(MLA_WIDTH), row(GLA_DK), row(GLA_DK),
                   row(MLA_Q_RANK), row(MLA_KV_RANK), row(1), row(1), row(LANE)),
        out_shape=(jax.ShapeDtypeStruct((t, hq), BF16), jax.ShapeDtypeStruct((t, hq), BF16),
                   jax.ShapeDtypeStruct((t, MLA_WIDTH), BF16),
                   jax.ShapeDtypeStruct((t, GLA_DK), F32), jax.ShapeDtypeStruct((t, GLA_DK), F32),
                   jax.ShapeDtypeStruct((t, MLA_Q_RANK), BF16),
                   jax.ShapeDtypeStruct((t, MLA_KV_RANK), BF16),
                   jax.ShapeDtypeStruct((t, 1), F32), jax.ShapeDtypeStruct((t, 1), F32),
                   jax.ShapeDtypeStruct((t, LANE), BF16)),
        compiler_params=_cparams(("arbitrary",)),
    )(proj, proj, proj, g_q, g_kv, w_uq_p, w_k_p, w_v, w_gate_p, b_gate, rc, rsn, rsp)


def _attn_masks(tq, i):
    keys = (i + 1) * tq
    rows = i * tq + lax.broadcasted_iota(jnp.int32, (tq, keys), 0)
    cols = lax.broadcasted_iota(jnp.int32, (tq, keys), 1)
    lane = lax.broadcasted_iota(jnp.int32, (tq, LANE), 1)
    return cols <= rows, lane < MLA_VDIM


def _for_each_query_tile(n_tiles, fn):
    for i in range(n_tiles):
        pl.when(pl.program_id(1) == i)(lambda i=i: fn(i))


def _mla_attn_fwd(q, k, v, shards, sources):
    t = q.shape[0]
    tq = min(256, t)
    scale = MLA_QK ** -0.5
    ns = len(shards)
    g_shapes, g_sems = _gather_plan(shards, sources)
    grid = (MLA_HEADS // 2, t // tq)

    def body(q_ref, k_ref, v_ref, *rest):
        o_ref, lse_ref = rest[ns:ns + 2]
        start, finish = _gather_hooks(rest[:ns], rest[ns + 2:2 * ns + 2], rest[2 * ns + 2:], sources)
        step = pl.program_id(0) * grid[1] + pl.program_id(1)
        pl.when(step == 0)(start)

        def tile(i):
            keys = (i + 1) * tq
            causal, low = _attn_masks(tq, i)
            vp = v_ref[0:keys, :]
            acc = jnp.zeros((tq, LANE), F32)
            for hh in range(2):
                sl = slice(hh * HEAD_PAD, (hh + 1) * HEAD_PAD)
                s = _dot_nt(q_ref[:, sl], k_ref[0:keys, sl]) * scale
                s = jnp.where(causal, s, -jnp.inf)
                m = jnp.max(s, axis=-1, keepdims=True)
                e = jnp.exp(s - m)
                l = jnp.sum(e, axis=-1, keepdims=True)
                o = _dot(e.astype(BF16), vp) / l
                acc = jnp.where(low if hh == 0 else jnp.logical_not(low), o, acc)
                lse_ref[hh] = m + jnp.log(l)
            o_ref[...] = acc

        _for_each_query_tile(t // tq, tile)
        pl.when(step == grid[0] * grid[1] - 1)(finish)

    res = pl.pallas_call(
        body, name="mla_attn_fwd",
        grid=grid,
        in_specs=[pl.BlockSpec((tq, 2 * HEAD_PAD), lambda p, i: (i, p)),
                  pl.BlockSpec((t, 2 * HEAD_PAD), lambda p, i: (0, p)),
                  pl.BlockSpec((t, LANE), lambda p, i: (0, p))] + _hbm_specs(ns),
        out_specs=(pl.BlockSpec((tq, LANE), lambda p, i: (i, p)),
                   pl.BlockSpec((2, tq, 1), lambda p, i: (p, i, 0))) + tuple(_hbm_specs(ns)),
        out_shape=(jax.ShapeDtypeStruct((t, MLA_WIDTH), F32),
                   jax.ShapeDtypeStruct((MLA_HEADS, t, 1), F32)) + tuple(g_shapes),
        scratch_shapes=g_sems,
        compiler_params=_cparams(("arbitrary", "arbitrary")),
    )(q, k, v, *shards)
    return res[0], res[1], res[2:]


def _mla_attn_bwd(q, k, v, o, do, lse, after):
    t = q.shape[0]
    tq = min(256, t)
    scale = MLA_QK ** -0.5

    def body(q_ref, k_ref, v_ref, o_ref, do_ref, lse_ref, after_ref, dq_ref, dk_ref, dv_ref):
        del after_ref

        @pl.when(pl.program_id(1) == 0)
        def _():
            dk_ref[...] = jnp.zeros_like(dk_ref)
            dv_ref[...] = jnp.zeros_like(dv_ref)

        def tile(i):
            keys = (i + 1) * tq
            causal, low = _attn_masks(tq, i)
            vp = v_ref[0:keys, :]
            do_all = do_ref[...]
            o_all = o_ref[...]
            dv_acc = jnp.zeros((keys, LANE), F32)
            for hh in range(2):
                sl = slice(hh * HEAD_PAD, (hh + 1) * HEAD_PAD)
                do_h = jnp.where(low if hh == 0 else jnp.logical_not(low), do_all, 0.0)
                dsum = jnp.sum(do_h * o_all, axis=-1, keepdims=True)
                qh = q_ref[:, sl]
                kh = k_ref[0:keys, sl]
                s = _dot_nt(qh, kh) * scale
                p = jnp.where(causal, jnp.exp(s - lse_ref[hh]), 0.0)
                do_b = do_h.astype(BF16)
                dp = _dot_nt(do_b, vp)
                ds = (p * (dp - dsum) * scale).astype(BF16)
                dq_ref[:, sl] = _dot(ds, kh).astype(BF16)
                dk_ref[0:keys, sl] += _dot_tn(ds, qh)
                dv_acc = dv_acc + _dot_tn(p.astype(BF16), do_b)
            dv_ref[0:keys, :] += dv_acc

        _for_each_query_tile(t // tq, tile)

    return pl.pallas_call(
        body, name="mla_attn_bwd",
        grid=(MLA_HEADS // 2, t // tq),
        in_specs=[pl.BlockSpec((tq, 2 * HEAD_PAD), lambda p, i: (i, p)),
                  pl.BlockSpec((t, 2 * HEAD_PAD), lambda p, i: (0, p)),
                  pl.BlockSpec((t, LANE), lambda p, i: (0, p)),
                  pl.BlockSpec((tq, LANE), lambda p, i: (i, p)),
                  pl.BlockSpec((tq, LANE), lambda p, i: (i, p)),
                  pl.BlockSpec((2, tq, 1), lambda p, i: (p, i, 0)),
                  pl.BlockSpec(memory_space=pl.ANY)],
        out_specs=(pl.BlockSpec((tq, 2 * HEAD_PAD), lambda p, i: (i, p)),
                   pl.BlockSpec((t, 2 * HEAD_PAD), lambda p, i: (0, p)),
                   pl.BlockSpec((t, LANE), lambda p, i: (0, p))),
        out_shape=(jax.ShapeDtypeStruct((t, MLA_HEADS * HEAD_PAD), BF16),
                   jax.ShapeDtypeStruct((t, MLA_HEADS * HEAD_PAD), F32),
                   jax.ShapeDtypeStruct((t, MLA_WIDTH), F32)),
        compiler_params=_cparams(("arbitrary", "arbitrary")),
    )(q, k, v, o, do, lse, after)


def _chunk_tri(n, lower):
    r = lax.broadcasted_iota(jnp.int32, (n, n), 0)
    c = lax.broadcasted_iota(jnp.int32, (n, n), 1)
    same = (r // GLA_CHUNK) == (c // GLA_CHUNK)
    return jnp.where(jnp.logical_and(same, r >= c if lower else r <= c), 1.0, 0.0).astype(F32)


def _gla_chunk_terms(q_ref, k_ref, b_ref, h):
    sl = slice(h * GLA_HK, (h + 1) * GLA_HK)
    b = b_ref[:, sl]
    bl = b[GLA_CHUNK - 1:GLA_CHUNK, :]
    kc = k_ref[:, sl]
    q_in = (q_ref[:, sl] * (GLA_HK ** -0.5)) * jnp.exp(b)
    k_in = kc * jnp.exp(-b)
    k_st = kc * jnp.exp(bl - b)
    return b, bl, q_in, k_in, k_st


def _tri(c, lower):
    r = lax.broadcasted_iota(jnp.int32, (c, c), 0)
    cc = lax.broadcasted_iota(jnp.int32, (c, c), 1)
    return jnp.where(r >= cc if lower else r <= cc, 1.0, 0.0).astype(F32)


def _gla_fwd(proj, log_a):
    t = proj.shape[0]
    c = GLA_CHUNK
    n = t // c

    def body(q_ref, k_ref, v_ref, la_ref, o_ref, sp_ref, st_ref):
        @pl.when(pl.program_id(0) == 0)
        def _():
            st_ref[...] = jnp.zeros_like(st_ref)

        tri = _tri(c, True)
        for h in range(GLA_HEADS):
            _, bl, q_in, k_in, k_st = _gla_chunk_terms(q_ref, k_ref, la_ref, h)
            vs = slice(h * GLA_HV, (h + 1) * GLA_HV)
            vv = v_ref[:, vs].astype(BF16)
            qb = q_in.astype(BF16)
            attn = _dot_nt(qb, k_in.astype(BF16)) * tri
            st = st_ref[h]
            sp_ref[0, h] = st
            o_ref[:, vs] = _dot(attn.astype(BF16), vv) + _dot_nt(qb, st.astype(BF16))
            st_ref[h] = st * jnp.exp(bl) + _dot_tn(vv, k_st.astype(BF16))

    return pl.pallas_call(
        body, name="gla_fwd",
        grid=(n,),
        in_specs=[pl.BlockSpec((c, GLA_DK), lambda i: (i, P_QG // GLA_DK)),
                  pl.BlockSpec((c, GLA_DK), lambda i: (i, P_KG // GLA_DK)),
                  pl.BlockSpec((c, GLA_DV), lambda i: (i, P_VG // GLA_DV)),
                  pl.BlockSpec((c, GLA_DK), lambda i: (i, 0))],
        out_specs=(pl.BlockSpec((c, GLA_DV), lambda i: (i, 0)),
                   pl.BlockSpec((1, GLA_HEADS, GLA_HV, GLA_HK), lambda i: (i, 0, 0, 0))),
        out_shape=(jax.ShapeDtypeStruct((t, GLA_DV), F32),
                   jax.ShapeDtypeStruct((n, GLA_HEADS, GLA_HV, GLA_HK), F32)),
        scratch_shapes=[pltpu.VMEM((GLA_HEADS, GLA_HV, GLA_HK), F32)],
        compiler_params=_cparams(("arbitrary",)),
    )(proj, proj, proj, log_a)


def _gla_bwd(proj, log_a, do, states, after):
    t = proj.shape[0]
    c = GLA_CHUNK
    n = t // c

    def body(q_ref, k_ref, v_ref, la_ref, do_ref, sp_ref, after_ref, dg_ref, dla_ref, ds_ref):
        del after_ref

        @pl.when(pl.program_id(0) == 0)
        def _():
            ds_ref[...] = jnp.zeros_like(ds_ref)

        tri = _tri(c, True)
        last = lax.broadcasted_iota(jnp.int32, (c, GLA_HK), 0) == c - 1
        for h in range(GLA_HEADS):
            b, bl, q_in, k_in, k_st = _gla_chunk_terms(q_ref, k_ref, la_ref, h)
            ks_ = slice(h * GLA_HK, (h + 1) * GLA_HK)
            vs = slice(h * GLA_HV, (h + 1) * GLA_HV)
            vv = v_ref[:, vs].astype(BF16)
            do_h = do_ref[:, vs]
            qb, kb, ksb = q_in.astype(BF16), k_in.astype(BF16), k_st.astype(BF16)
            attn = (_dot_nt(qb, kb) * tri).astype(BF16)
            st = sp_ref[0, h]
            dst = ds_ref[h]
            dstb = dst.astype(BF16)
            dattn = (_dot_nt(do_h, vv) * tri).astype(BF16)
            dg_ref[:, P_VG + h * GLA_HV:P_VG + (h + 1) * GLA_HV] = (
                _dot_tn(attn, do_h) + _dot_nt(ksb, dstb)).astype(BF16)
            dq_in = _dot(dattn, kb) + _dot(do_h, st.astype(BF16))
            dk_in = _dot_tn(dattn, qb)
            dk_st = _dot(vv, dstb)
            ebl = jnp.exp(bl)
            d_ebl = jnp.sum(st * dst, axis=0, keepdims=True)
            ds_ref[h] = _dot_tn(do_h, qb) + dst * ebl
            dg_ref[:, P_QG + h * GLA_HK:P_QG + (h + 1) * GLA_HK] = (
                dq_in * (GLA_HK ** -0.5) * jnp.exp(b)).astype(BF16)
            dg_ref[:, P_KG + h * GLA_HK:P_KG + (h + 1) * GLA_HK] = (
                dk_in * jnp.exp(-b) + dk_st * jnp.exp(bl - b)).astype(BF16)
            db = dq_in * q_in - dk_in * k_in - dk_st * k_st
            dbl = jnp.sum(dk_st * k_st, axis=0, keepdims=True) + d_ebl * ebl
            dla_ref[:, ks_] = db + jnp.where(last, dbl, 0.0)

    rev = lambda i: n - 1 - i
    gw = P_GROUPS[0][1]
    return pl.pallas_call(
        body, name="gla_bwd",
        grid=(n,),
        in_specs=[pl.BlockSpec((c, GLA_DK), lambda i: (rev(i), P_QG // GLA_DK)),
                  pl.BlockSpec((c, GLA_DK), lambda i: (rev(i), P_KG // GLA_DK)),
                  pl.BlockSpec((c, GLA_DV), lambda i: (rev(i), P_VG // GLA_DV)),
                  pl.BlockSpec((c, GLA_DK), lambda i: (rev(i), 0)),
                  pl.BlockSpec((c, GLA_DV), lambda i: (rev(i), 0)),
                  pl.BlockSpec((1, GLA_HEADS, GLA_HV, GLA_HK), lambda i: (rev(i), 0, 0, 0)),
                  pl.BlockSpec(memory_space=pl.ANY)],
        out_specs=(pl.BlockSpec((c, gw), lambda i: (rev(i), 0)),
                   pl.BlockSpec((c, GLA_DK), lambda i: (rev(i), 0))),
        out_shape=(jax.ShapeDtypeStruct((t, gw), BF16), jax.ShapeDtypeStruct((t, GLA_DK), F32)),
        scratch_shapes=[pltpu.VMEM((GLA_HEADS, GLA_HV, GLA_HK), F32)],
        compiler_params=_cparams(("arbitrary",)),
    )(proj, proj, proj, log_a, do, states, after)


def _post(o_mla, proj, o_gla, x, target, g_gla, g_final, w_pm, w_pg, w_o):
    t = x.shape[0]
    tm = min(128, t)
    g0, gw = P_GROUPS[1]

    def body(om_ref, zg_ref, gm_ref, gg_ref, zm_ref, og_ref, x_ref, tg_ref, ggla_ref, gf_ref,
             wpm_ref, wpg_ref, wo_ref,
             dx2_ref, dom_ref, dog_ref, dg_ref,
             mg_ref, um_ref, ug_ref, dym_ref, dyg_ref, loss_ref, dgf_ref, dggla_ref):
        @pl.when(pl.program_id(0) == 0)
        def _():
            loss_ref[...] = jnp.zeros_like(loss_ref)
            dgf_ref[...] = jnp.zeros_like(dgf_ref)
            dggla_ref[...] = jnp.zeros_like(dggla_ref)

        om = om_ref[...]
        zm = zm_ref[...]
        sm = _sigmoid(zm)
        silu_m = zm * sm
        um = (om * silu_m).astype(BF16)
        um_ref[...] = um
        ym = _dot(um, wpm_ref[...])

        ggla = ggla_ref[...]
        zg = zg_ref[...]
        sg = _sigmoid(zg)
        silu_g = zg * sg
        xhat, rstd, on = [], [], []
        for h in range(GLA_HEADS):
            blk = og_ref[:, h * GLA_HV:(h + 1) * GLA_HV]
            r = lax.rsqrt(jnp.mean(blk * blk, axis=-1, keepdims=True) + EPS)
            xhat.append(blk * r)
            rstd.append(r)
            on.append(xhat[h] * ggla)
        on = jnp.concatenate(on, axis=-1)
        ug = (on * silu_g).astype(BF16)
        ug_ref[...] = ug
        yg = _dot(ug, wpg_ref[...])

        sgm = _sigmoid(gm_ref[...])
        sgg = _sigmoid(gg_ref[...])
        merged = (sgm * ym + sgg * yg).astype(BF16)
        mg_ref[...] = merged
        x2 = x_ref[...] + _dot(merged, wo_ref[...])
        gf = gf_ref[...]
        rf = lax.rsqrt(jnp.mean(x2 * x2, axis=-1, keepdims=True) + EPS)
        xh = x2 * rf
        err = xh * gf - tg_ref[...]
        loss_ref[...] += 0.5 * jnp.sum(jnp.mean(err * err, axis=-1, keepdims=True))

        dy = err * (1.0 / D_MODEL)
        dgf_ref[...] += jnp.sum(dy * xh, axis=0, keepdims=True)
        dxh = dy * gf
        dx2 = rf * (dxh - xh * jnp.mean(dxh * xh, axis=-1, keepdims=True))
        dx2_ref[...] = dx2
        dmerged = _dot_nt(dx2.astype(BF16), wo_ref[...])
        dym = (dmerged * sgm).astype(BF16)
        dyg = (dmerged * sgg).astype(BF16)
        dym_ref[...] = dym
        dyg_ref[...] = dyg
        dg_ref[:, P_GMLA - g0:P_GMLA - g0 + D_MODEL] = (dmerged * ym * sgm * (1.0 - sgm)).astype(BF16)
        dg_ref[:, P_GGLA - g0:P_GGLA - g0 + D_MODEL] = (dmerged * yg * sgg * (1.0 - sgg)).astype(BF16)
        dum = _dot_nt(dym, wpm_ref[...])
        dom_ref[...] = dum * silu_m
        dg_ref[:, P_ZMLA - g0:P_ZMLA - g0 + MLA_WIDTH] = (
            dum * om * (sm * (1.0 + zm * (1.0 - sm)))).astype(BF16)
        dug = _dot_nt(dyg, wpg_ref[...])
        dg_ref[:, P_ZGLA - g0:P_ZGLA - g0 + GLA_DV] = (
            dug * on * (sg * (1.0 + zg * (1.0 - sg)))).astype(BF16)
        don = dug * silu_g
        dggla = jnp.zeros((1, GLA_HV), F32)
        for h in range(GLA_HEADS):
            hs = slice(h * GLA_HV, (h + 1) * GLA_HV)
            don_h = don[:, hs]
            dggla = dggla + jnp.sum(don_h * xhat[h], axis=0, keepdims=True)
            dxh_h = don_h * ggla
            dog_ref[:, hs] = (rstd[h] * (dxh_h - xhat[h] * jnp.mean(dxh_h * xhat[h], axis=-1,
                                                                     keepdims=True))).astype(BF16)
        dggla_ref[...] += dggla

    row = lambda w: pl.BlockSpec((tm, w), lambda i: (i, 0))
    pcol = lambda w, off: pl.BlockSpec((tm, w), lambda i: (i, _rel(off) // w))
    full = lambda a: pl.BlockSpec(a.shape, lambda i: (0, 0))
    sds = jax.ShapeDtypeStruct
    return pl.pallas_call(
        body, name="post_fwd_bwd",
        grid=(t // tm,),
        in_specs=[row(MLA_WIDTH), pcol(GLA_DV, P_ZGLA), pcol(D_MODEL, P_GMLA), pcol(D_MODEL, P_GGLA),
                  pcol(MLA_WIDTH, P_ZMLA), row(GLA_DV), row(D_MODEL), row(D_MODEL),
                  full(g_gla), full(g_final), full(w_pm), full(w_pg), full(w_o)],
        out_specs=(row(D_MODEL), row(MLA_WIDTH), row(GLA_DV), row(gw),
                   row(D_MODEL), row(MLA_WIDTH), row(GLA_DV), row(D_MODEL), row(D_MODEL),
                   pl.BlockSpec((1, LANE), lambda i: (0, 0)),
                   pl.BlockSpec((1, D_MODEL), lambda i: (0, 0)),
                   pl.BlockSpec((1, GLA_HV), lambda i: (0, 0))),
        out_shape=(sds((t, D_MODEL), F32), sds((t, MLA_WIDTH), F32), sds((t, GLA_DV), BF16),
                   sds((t, gw), BF16),
                   sds((t, D_MODEL), BF16), sds((t, MLA_WIDTH), BF16), sds((t, GLA_DV), BF16),
                   sds((t, D_MODEL), BF16), sds((t, D_MODEL), BF16),
                   sds((1, LANE), F32), sds((1, D_MODEL), F32), sds((1, GLA_HV), F32)),
        compiler_params=_cparams(("arbitrary",)),
    )(o_mla, proj, proj, proj, proj, o_gla, x, target, g_gla, g_final, w_pm, w_pg, w_o)


def _mla_prep_bwd(dq, dk, dv, dla, pre, proj, rq, rkv, g_q, g_kv, w_uq_p, w_k_p, w_v, w_gate_p,
                  rc, rsn, rsp):
    t = proj.shape[0]
    tm = min(256, t)
    gw = P_GROUPS[2][1]

    def body(dq_ref, dk_ref, dv_ref, dla_ref, pre_ref, cq_ref, ckv_ref, rq_ref, rkv_ref,
             gq_ref, gkv_ref, wuq_ref, wk_ref, wv_ref, wg_ref, c_ref, sn_ref, sp_ref,
             dg_ref, dqpre_ref, dpre_ref, dgq_ref, dgkv_ref, dbg_ref):
        @pl.when(pl.program_id(0) == 0)
        def _():
            dgq_ref[...] = jnp.zeros_like(dgq_ref)
            dgkv_ref[...] = jnp.zeros_like(dgkv_ref)
            dbg_ref[...] = jnp.zeros_like(dbg_ref)

        c, sn, sp = c_ref[...], sn_ref[...], sp_ref[...]
        dkr = jnp.zeros((tm, LANE), F32)
        for h in range(MLA_HEADS):
            sl = slice(h * HEAD_PAD, (h + 1) * HEAD_PAD)
            dqpre_ref[:, sl] = _rope_bwd(dq_ref[:, sl].astype(F32), c, sn, sp).astype(BF16)
            dkr = dkr + dk_ref[:, sl]
        dcqn = _dot_nt(dqpre_ref[...], wuq_ref[...])
        rq = rq_ref[...]
        xh = cq_ref[:, :MLA_Q_RANK] * rq
        dgq_ref[...] += jnp.sum(dcqn * xh, axis=0, keepdims=True)
        dxh = dcqn * gq_ref[...]
        dcq = rq * (dxh - xh * jnp.mean(dxh * xh, axis=-1, keepdims=True))
        dg_ref[:, :MLA_Q_RANK] = dcq.astype(BF16)
        dg_ref[:, MLA_Q_RANK:512] = jnp.zeros((tm, 512 - MLA_Q_RANK), BF16)

        dckvn = _dot_nt(dk_ref[...].astype(BF16), wk_ref[...]) + \
            _dot_nt(dv_ref[...].astype(BF16), wv_ref[...])
        rkv = rkv_ref[...]
        xh = ckv_ref[...] * rkv
        dgkv_ref[...] += jnp.sum(dckvn * xh, axis=0, keepdims=True)
        dxh = dckvn * gkv_ref[...]
        dg_ref[:, P_CKV - P_CQ:P_CKV - P_CQ + MLA_KV_RANK] = (
            rkv * (dxh - xh * jnp.mean(dxh * xh, axis=-1, keepdims=True))).astype(BF16)

        dlog_a = _dot_exact(_chunk_tri(tm, False), dla_ref[...])
        dpre = dlog_a * (1.0 / GLA_GATE_NORM) * (1.0 - _sigmoid(pre_ref[...]))
        dbg_ref[...] += jnp.sum(dpre, axis=0, keepdims=True)
        dpre = dpre.astype(BF16)
        dpre_ref[...] = dpre
        lane = lax.broadcasted_iota(jnp.int32, (tm, LANE), 1)
        in_kr = jnp.logical_and(lane >= MISC_KR, lane < MISC_KR + MLA_ROPE)
        dmisc = jnp.where(in_kr, _rope_bwd(dkr, c, sn, sp), 0.0) + _dot_nt(dpre, wg_ref[...])
        dg_ref[:, P_MISC - P_CQ:P_MISC - P_CQ + LANE] = dmisc.astype(BF16)

    hq = MLA_HEADS * HEAD_PAD
    row = lambda w: pl.BlockSpec((tm, w), lambda i: (i, 0))
    full = lambda a: pl.BlockSpec(a.shape, lambda i: (0, 0))
    acc = lambda w: pl.BlockSpec((1, w), lambda i: (0, 0))
    sds = jax.ShapeDtypeStruct
    return pl.pallas_call(
        body, name="mla_prep_bwd",
        grid=(t // tm,),
        in_specs=[row(hq), row(hq), row(MLA_WIDTH), row(GLA_DK), row(GLA_DK),
                  pl.BlockSpec((tm, 512), lambda i: (i, _rel(P_CQ) // 512)),
                  pl.BlockSpec((tm, MLA_KV_RANK), lambda i: (i, _rel(P_CKV) // MLA_KV_RANK)),
                  row(1), row(1), full(g_q), full(g_kv), full(w_uq_p), full(w_k_p), full(w_v),
                  full(w_gate_p), row(LANE), row(LANE), row(LANE)],
        out_specs=(row(gw), row(hq), row(GLA_DK),
                   acc(MLA_Q_RANK), acc(MLA_KV_RANK), acc(GLA_DK)),
        out_shape=(sds((t, gw), BF16), sds((t, hq), BF16), sds((t, GLA_DK), BF16),
                   sds((1, MLA_Q_RANK), F32), sds((1, MLA_KV_RANK), F32), sds((1, GLA_DK), F32)),
        compiler_params=_cparams(("arbitrary",)),
    )(dq, dk, dv, dla, pre, proj, proj, rq, rkv, g_q, g_kv, w_uq_p, w_k_p, w_v, w_gate_p,
      rc, rsn, rsp)


def _inproj_bwd(dgroups, w_pts, x, rstd, g_in, dx2, after):
    t = x.shape[0]
    tm = min(256, t)

    def body(d0_ref, d1_ref, d2_ref, w0_ref, w1_ref, w2_ref, x_ref, r_ref, g_ref, dx2_ref, after_ref,
             dx_ref, dg_ref):
        del after_ref

        @pl.when(pl.program_id(0) == 0)
        def _():
            dg_ref[...] = jnp.zeros_like(dg_ref)

        dh = jnp.zeros((tm, D_MODEL), F32)
        for d_ref, w_ref in zip((d0_ref, d1_ref, d2_ref), (w0_ref, w1_ref, w2_ref)):
            dh = dh + _dot(d_ref[...], w_ref[...])
        r = r_ref[...]
        xh = x_ref[...] * r
        dg_ref[...] += jnp.sum(dh * xh, axis=0, keepdims=True)
        dxh = dh * g_ref[...]
        dx_ref[...] = dx2_ref[...] + r * (dxh - xh * jnp.mean(dxh * xh, axis=-1, keepdims=True))

    row = lambda w: pl.BlockSpec((tm, w), lambda i: (i, 0))
    return pl.pallas_call(
        body, name="inproj_bwd",
        grid=(t // tm,),
        in_specs=[row(w) for _, w in P_GROUPS]
        + [pl.BlockSpec((w, D_MODEL), lambda i: (0, 0)) for _, w in P_GROUPS]
        + [row(D_MODEL), row(1), pl.BlockSpec((1, D_MODEL), lambda i: (0, 0)), row(D_MODEL),
           pl.BlockSpec(memory_space=pl.ANY)],
        out_specs=(row(D_MODEL), pl.BlockSpec((1, D_MODEL), lambda i: (0, 0))),
        out_shape=(jax.ShapeDtypeStruct((t, D_MODEL), F32),
                   jax.ShapeDtypeStruct((1, D_MODEL), F32)),
        compiler_params=_cparams(("arbitrary",)),
    )(*dgroups, *w_pts, x, rstd, g_in, dx2, after)


def _matmul(name, a, b, tm, tn, dtype=F32):
    kk, m = a.shape
    n = b.shape[1]

    def body(a_ref, b_ref, o_ref):
        o_ref[...] = _dot_tn(a_ref[...].astype(BF16), b_ref[...].astype(BF16)).astype(dtype)

    return pl.pallas_call(
        body, name=name,
        grid=(n // tn, m // tm),
        in_specs=[pl.BlockSpec((kk, tm), lambda j, i: (0, i)),
                  pl.BlockSpec((kk, tn), lambda j, i: (0, j))],
        out_specs=pl.BlockSpec((tm, tn), lambda j, i: (i, j)),
        out_shape=jax.ShapeDtypeStruct((m, n), dtype),
        compiler_params=_cparams(("arbitrary", "arbitrary")),
    )(a, b)


def _adamw_update(part_refs, w_ref, m_ref, v_ref, g_ref, d_ref, nm_ref, nv_ref):
    g = part_refs[0][...].astype(F32)
    for p_ref in part_refs[1:]:
        g = g + p_ref[...].astype(F32)
    m_new = ADAM_B1 * m_ref[...] + (1.0 - ADAM_B1) * g
    v_new = ADAM_B2 * v_ref[...] + (1.0 - ADAM_B2) * (g * g)
    m_hat = m_new / (1.0 - ADAM_B1 ** ADAM_STEP)
    v_hat = v_new / (1.0 - ADAM_B2 ** ADAM_STEP)
    g_ref[...] = g
    nm_ref[...] = m_new
    nv_ref[...] = v_new
    d_ref[...] = -ADAM_LR * (m_hat / (jnp.sqrt(v_hat) + ADAM_EPS) + ADAM_WD * w_ref[...])


def _adamw_rows(name, parts, w, m, v, tr, first=None):
    _, rows, cols = w.shape
    slots = parts.shape[0]

    def body(*refs):
        lead_refs, p_ref = ([], refs[0]) if first is None else ([refs[0]], refs[1])
        _adamw_update(lead_refs + [p_ref.at[q] for q in range(slots)], *refs[len(lead_refs) + 1:])

    blk = pl.BlockSpec((None, tr, cols), lambda i: (0, i, 0))
    out = jax.ShapeDtypeStruct((1, rows, cols), F32)
    lead = [] if first is None else [pl.BlockSpec((tr, cols), lambda i: (i, 0))]
    return pl.pallas_call(
        body, name=name,
        grid=(rows // tr,),
        in_specs=lead + [pl.BlockSpec((slots, tr, cols), lambda i: (0, i, 0)), blk, blk, blk],
        out_specs=(blk, blk, blk, blk),
        out_shape=(out, out, out, out),
        compiler_params=_cparams(("arbitrary",)),
    )(*([] if first is None else [first]), parts, w, m, v)


def _adamw_transposed(name, first, parts, w, m, v, tl):
    _, rows, cols = w.shape
    slots = parts.shape[0]

    def body(f_ref, p_ref, *refs):
        _adamw_update([f_ref] + [p_ref.at[q] for q in range(slots)], *refs)

    blk = pl.BlockSpec((cols, None, tl), lambda i: (0, 0, i))
    out = jax.ShapeDtypeStruct((cols, 1, rows), F32)
    res = pl.pallas_call(
        body, name=name,
        grid=(rows // tl,),
        in_specs=[pl.BlockSpec((cols, tl), lambda i: (0, i)),
                  pl.BlockSpec((slots, cols, tl), lambda i: (0, 0, i)), blk, blk, blk],
        out_specs=(blk, blk, blk, blk),
        out_shape=(out, out, out, out),
        compiler_params=_cparams(("arbitrary",)),
    )(first, parts, *[a.transpose(2, 0, 1) for a in (w, m, v)])
    return [r.transpose(1, 2, 0) for r in res]


def _adamw_group(firsts, parts, ws, ms, vs):
    n = len(ws)

    def body(*refs):
        ins, outs = refs[:5 * n], refs[5 * n:]
        for a in range(n):
            _adamw_update([ins[a]] + [ins[n + a].at[q] for q in range(ins[n + a].shape[0])],
                          *[r.at[0] for r in (ins[2 * n + a], ins[3 * n + a], ins[4 * n + a])],
                          *[r.at[0] for r in outs[4 * a:4 * a + 4]])

    vmem = lambda k: [pl.BlockSpec(memory_space=pltpu.VMEM) for _ in range(k)]
    out_shape = []
    for w in ws:
        out_shape += [jax.ShapeDtypeStruct(w.shape, F32)] * 4
    res = pl.pallas_call(
        body, name="adamw_small_weights",
        in_specs=vmem(5 * n), out_specs=tuple(vmem(4 * n)), out_shape=tuple(out_shape),
        compiler_params=_cparams(),
    )(*firsts, *parts, *ws, *ms, *vs)
    return [res[4 * a:4 * a + 4] for a in range(n)]


def _rope_tables(positions):
    half = MLA_ROPE // 2
    freqs = ROPE_THETA ** (-jnp.arange(half, dtype=F32) / half)
    ang = positions.astype(F32).reshape(-1, 1) * freqs
    cos, sin = jnp.cos(ang), jnp.sin(ang)
    t = ang.shape[0]
    one, zero = jnp.ones((t, MLA_NOPE), F32), jnp.zeros((t, half), F32)
    tail = jnp.zeros((t, LANE - MLA_QK), F32)
    rc = jnp.concatenate([one, cos, cos, tail], axis=1)
    rsn = jnp.concatenate([0.0 * one, -sin, zero, tail], axis=1)
    rsp = jnp.concatenate([0.0 * one, zero, sin, tail], axis=1)
    return rc, rsn, rsp


def _cols_full(g):
    return g.transpose(1, 0, 2)


def kernel(x, positions, g_in, w_in, g_q, w_uq, g_kv, w_ukv, w_gla_gate, b_gla_gate, g_gla, w_proj_mla, w_proj_gla, w_out, g_final, loss_target, m_g_in, m_w_in, m_g_q, m_w_uq, m_g_kv, m_w_ukv, m_w_gla_gate, m_b_gla_gate, m_g_gla, m_w_proj_mla, m_w_proj_gla, m_w_out, m_g_final, v_g_in, v_w_in, v_g_q, v_w_uq, v_g_kv, v_w_ukv, v_w_gla_gate, v_b_gla_gate, v_g_gla, v_w_proj_mla, v_w_proj_gla, v_w_out, v_g_final):
    t = x.shape[1]
    x2d = x.reshape(t, D_MODEL)
    tgt = loss_target.reshape(t, D_MODEL)
    g_final2 = g_final.reshape(1, D_MODEL)
    sharded = [(w_in, m_w_in, v_w_in), (w_uq, m_w_uq, v_w_uq), (w_ukv, m_w_ukv, v_w_ukv),
               (w_gla_gate, m_w_gla_gate, v_w_gla_gate), (w_proj_mla, m_w_proj_mla, v_w_proj_mla),
               (w_proj_gla, m_w_proj_gla, v_w_proj_gla), (w_out, m_w_out, v_w_out)]

    w_in_t = w_in.transpose(2, 0, 1).reshape(SHARD_COLS, D_MODEL)
    everyone = tuple(range(N_DEV))
    w_in_b = w_in_t.astype(BF16)
    b_uq, b_ukv, b_gate, b_pm, b_pg, b_o = [s[0][0].astype(BF16) for s in sharded[1:]]
    stages = ((0, 2, 4, 6), (1, 3, 5, 7))
    where = {d: (k, i) for k, srcs in enumerate(stages) for i, d in enumerate(srcs)}
    g_in_1, g_uq, g_ukv, g_gate = _all_gather(
        "all_gather_first", [w_in_b, b_uq, b_ukv, b_gate], [stages[0]] + [everyone] * 3)
    w_uq_p = jnp.pad(_cols_full(g_uq), ((0, 0), (0, 0), (0, HEAD_PAD - MLA_QK))).reshape(
        MLA_Q_RANK, MLA_HEADS * HEAD_PAD)
    ukv = _cols_full(g_ukv)
    w_k_p = jnp.pad(ukv[:, :, :MLA_NOPE], ((0, 0), (0, 0), (0, HEAD_PAD - MLA_NOPE))).reshape(
        MLA_KV_RANK, MLA_HEADS * HEAD_PAD)
    w_v = ukv[:, :, MLA_NOPE:].reshape(MLA_KV_RANK, MLA_WIDTH)
    w_gate_p = jnp.pad(_cols_full(g_gate).reshape(GLA_GATE_RANK, GLA_DK),
                       ((MISC_ALR, LANE - MISC_ALR - GLA_GATE_RANK), (0, 0)))
    rc, rsn, rsp = _rope_tables(positions)

    w_lat = _weights_to_p("weights_latents", [g_in_1], where, 2)
    proj_lat, h, rstd = _inproj(x2d, g_in, w_lat)
    q, k, v, log_a, pre, cqn, ckvn, rq, rkv, misc = _mla_prep(
        proj_lat, g_q, g_kv, w_uq_p, w_k_p, w_v, w_gate_p, b_gla_gate, rc, rsn, rsp)
    o_mla, lse, (g_in_2, g_pm, g_pg, g_o) = _mla_attn_fwd(
        q, k, v, [w_in_b, b_pm, b_pg, b_o], [stages[1]] + [everyone] * 3)
    w_gla = _weights_to_p("weights_gla", [g_in_1, g_in_2], where, 0)
    proj_gla = _proj("inproj_gla", h, w_gla)
    o_gla, states = _gla_fwd(proj_gla, log_a)
    w_out_path = _weights_to_p("weights_out_path", [g_in_1, g_in_2], where, 1)
    proj_out = _proj("inproj_out_path", h, w_out_path)
    w_in_p = (w_gla, w_out_path, w_lat)
    w_pm = _cols_full(g_pm).reshape(MLA_WIDTH, D_MODEL)
    w_pg = g_pg.reshape(GLA_DV, D_MODEL)
    w_o = g_o.reshape(D_MODEL, D_MODEL)

    (dx2, do_mla, do_gla, d_out, merged, um, ug, dym, dyg, loss_p, dg_final,
     dg_gla) = _post(o_mla, proj_out, o_gla, x2d, tgt, g_gla, g_final2, w_pm, w_pg, w_o)

    p_pm = _matmul("dw_proj_mla", um, dym, 512, 512, BF16).reshape(
        MLA_WIDTH, N_DEV, D_MODEL // N_DEV).transpose(1, 0, 2)
    p_pg = _matmul("dw_proj_gla", ug, dyg, 512, 512, BF16).reshape(N_DEV, -1, D_MODEL)
    p_o = _matmul("dw_out", merged, dx2, 512, 512, BF16).reshape(N_DEV, -1, D_MODEL)
    owns = [lax.empty((SHARD_COLS, D_MODEL), BF16)] + [lax.empty(p.shape[1:], BF16)
                                                       for p in (p_pm, p_pg, p_o)]
    lands = [lax.empty((3,) + o.shape, BF16) for o in owns]
    dw_groups, started = {}, []

    def reduce_scatter_stage(s, dests, extra=()):
        assert set(g for d in dests for g in _shard_groups(d)) <= set(dw_groups)
        k = 1 + len(extra)
        parts = [_grads_to_shards("grads_to_shards_%d" % s, dw_groups, dests)] + list(extra)
        all_dests = [dests] + [everyone] * len(extra)
        sums, owns[:k] = _swap_add("swap_add_%d" % s, parts, owns[:k], all_dests)
        sems, sums, lands[:k], token = _ici_start("ici_start_%d" % s, sums, lands[:k], all_dests)
        started.extend((sems[a][0], sems[a][1], sums[a], a, all_dests[a]) for a in range(k))
        return token

    dw_groups[1] = _matmul("dw_in_1", d_out, h, 512, 512, BF16)
    token = reduce_scatter_stage(1, (5, 6, 7), (p_pm, p_pg, p_o))
    d_gla, dla = _gla_bwd(proj_gla, log_a, do_gla, states, token)
    dw_groups[0] = _matmul("dw_in_0", d_gla, h, 512, 512, BF16)
    token = reduce_scatter_stage(2, (1, 2, 3))
    dq, dk, dv = _mla_attn_bwd(q, k, v, o_mla, do_mla, lse, token)
    d_lat, dqpre, dpre, dg_q, dg_kv, db_gate = _mla_prep_bwd(
        dq, dk, dv, dla, pre, proj_lat, rq, rkv, g_q, g_kv, w_uq_p, w_k_p, w_v, w_gate_p, rc, rsn, rsp)
    dw_groups[2] = _matmul("dw_in_2", d_lat, h, 896, 512, BF16)
    token = reduce_scatter_stage(3, (0, 4))
    grad_x, dg_in = _inproj_bwd((d_gla, d_out, d_lat), w_in_p, x2d, rstd, g_in, dx2, token)

    dw_uq = _matmul("dw_uq", cqn, dqpre, MLA_Q_RANK, 512, BF16)
    p_uq = dw_uq.reshape(MLA_Q_RANK, MLA_HEADS, HEAD_PAD)[:, :, :MLA_QK].transpose(1, 0, 2)
    dw_k = _matmul("dw_uk", ckvn, dk, MLA_KV_RANK, 512, BF16)
    dw_v = _matmul("dw_uv", ckvn, dv, MLA_KV_RANK, 512, BF16)
    p_ukv = jnp.concatenate(
        [dw_k.reshape(MLA_KV_RANK, MLA_HEADS, HEAD_PAD)[:, :, :MLA_NOPE],
         dw_v.reshape(MLA_KV_RANK, MLA_HEADS, MLA_VDIM)], axis=2).transpose(1, 0, 2)
    dw_gate = _matmul("dw_gate", misc, dpre, LANE, 512, BF16)
    p_gate = dw_gate[MISC_ALR:MISC_ALR + GLA_GATE_RANK].reshape(
        GLA_GATE_RANK, N_DEV, GLA_DK // N_DEV).transpose(1, 0, 2)
    small = jnp.concatenate([dg_in.reshape(-1), dg_q.reshape(-1), dg_kv.reshape(-1),
                             db_gate.reshape(-1), dg_gla.reshape(-1), dg_final.reshape(-1),
                             loss_p[0, :1]])
    small = jnp.pad(small, (0, SMALL_ROWS * LANE - small.shape[0])).reshape(SMALL_ROWS, LANE)

    recv = _exchange_grads([p_uq, p_ukv, p_gate], small)
    lands = _ici_wait("ici_wait", started, lands, recv[3])
    big = [_adamw_transposed("adamw_w_in", owns[0], lands[0], *sharded[0], 256)]
    big += _adamw_group([r[0] for r in recv[:3]] + list(owns[1:]),
                        [r[1:] for r in recv[:3]] + list(lands[1:]),
                        *[[s[j] for s in sharded[1:]] for j in range(3)])
    replicated = [(g_in, m_g_in, v_g_in), (g_q, m_g_q, v_g_q), (g_kv, m_g_kv, v_g_kv),
                  (b_gla_gate, m_b_gla_gate, v_b_gla_gate), (g_gla, m_g_gla, v_g_gla),
                  (g_final, m_g_final, v_g_final)]
    spacks = [jnp.pad(jnp.concatenate([s[j].reshape(-1) for s in replicated]),
                      (0, SMALL_ROWS * LANE - sum(SMALL_SIZES))).reshape(1, SMALL_ROWS, LANE)
              for j in range(3)]
    tiny = _adamw_rows("adamw_gains", recv[3], spacks[0], spacks[1], spacks[2], SMALL_ROWS)

    outs = {}
    names = ("w_in", "w_uq", "w_ukv", "w_gla_gate", "w_proj_mla", "w_proj_gla", "w_out")
    for j, kind in enumerate(("grad", "delta", "new_m", "new_v")):
        for name, res in zip(names, big):
            outs[kind, name] = res[j]
        flat = tiny[j].reshape(-1)
        off = 0
        for name, size in zip(("g_in", "g_q", "g_kv", "b_gla_gate", "g_gla", "g_final"), SMALL_SIZES):
            shape = (size,) if name == "g_final" else (1, size)
            outs[kind, name] = flat[off:off + size].reshape(shape)
            off += size
    loss = tiny[0].reshape(-1)[sum(SMALL_SIZES)]
    order = ("g_in", "w_in", "g_q", "w_uq", "g_kv", "w_ukv", "w_gla_gate", "b_gla_gate", "g_gla",
             "w_proj_mla", "w_proj_gla", "w_out", "g_final")
    result = [loss, grad_x.reshape(1, t, D_MODEL)]
    for kind in ("grad", "delta", "new_m", "new_v"):
        result += [outs[kind, name] for name in order]
    return tuple(result)
```

```python
import jax
import jax.numpy as jnp
from jax import lax
from jax.experimental import pallas as pl
from jax.experimental.pallas import tpu as pltpu

F32 = jnp.float32
BF16 = jnp.bfloat16
MESH = pl.DeviceIdType.MESH
N_DEV = 8

D_MODEL = 1024
EPS = 1e-6
MLA_HEADS = 8
MLA_NOPE = 64
MLA_ROPE = 32
MLA_VDIM = 64
MLA_Q_RANK = 384
MLA_KV_RANK = 256
MLA_QK = MLA_NOPE + MLA_ROPE
MLA_WIDTH = MLA_HEADS * MLA_VDIM
ROPE_THETA = 10000.0
GLA_HEADS = 4
GLA_DK = 512
GLA_DV = 1024
GLA_HK = 128
GLA_HV = 256
GLA_GATE_RANK = 16
GLA_GATE_NORM = 16.0
GLA_CHUNK = 64
D_IN = 6320

ADAM_LR = 0.001
ADAM_B1 = 0.9
ADAM_B2 = 0.999
ADAM_EPS = 1e-08
ADAM_WD = 0.01
ADAM_STEP = 10

LANE = 128
HEAD_PAD = 128
VMEM_LIMIT = 48 * 1024 * 1024

P_VG, P_QG, P_KG = 0, 1024, 1536
P_ZGLA, P_GMLA, P_GGLA, P_ZMLA = 2048, 3072, 4096, 5120
P_CQ, P_CKV, P_MISC = 5632, 6144, 6400
P_TOTAL = 6528
P_GROUPS = ((0, 2048), (2048, 3584), (5632, 896))
MISC_KR = 64
MISC_ALR = 96
SHARD_COLS = D_IN // N_DEV
P_COMPONENTS = ((0, 384, P_CQ), (384, 256, P_CKV), (640, 32, P_MISC + MISC_KR), (672, 512, P_ZMLA),
                (1184, 512, P_QG), (1696, 512, P_KG), (2208, 1024, P_VG),
                (3232, 16, P_MISC + MISC_ALR), (3248, 1024, P_ZGLA), (4272, 1024, P_GMLA),
                (5296, 1024, P_GGLA))

SMALL_SIZES = (1024, 384, 256, 512, 256, 1024)
SMALL_ROWS = 32


def _segments():
    segs = []
    for g0, n, p0 in P_COMPONENTS:
        g = g0
        while g < g0 + n:
            d = g // SHARD_COLS
            end = min(g0 + n, (d + 1) * SHARD_COLS)
            segs.append((d, g - d * SHARD_COLS, end - g, p0 + g - g0))
            g = end
    return segs


def _group_of(p0):
    return max(i for i, (off, _) in enumerate(P_GROUPS) if off <= p0)


def _rel(p0):
    return p0 - P_GROUPS[_group_of(p0)][0]


def _cparams(sem=None):
    if sem is None:
        return pltpu.CompilerParams(vmem_limit_bytes=VMEM_LIMIT)
    return pltpu.CompilerParams(dimension_semantics=sem, vmem_limit_bytes=VMEM_LIMIT)


def _sigmoid(v):
    return 1.0 / (1.0 + jnp.exp(-v))


def _dot(a, b):
    return jnp.dot(a, b, preferred_element_type=F32)


def _dot_nt(a, b):
    return lax.dot_general(a, b, (((1,), (1,)), ((), ())), preferred_element_type=F32)


def _dot_tn(a, b):
    return lax.dot_general(a, b, (((0,), (0,)), ((), ())), preferred_element_type=F32)


def _dot_exact(a, b):
    return jnp.dot(a, b, preferred_element_type=F32, precision=lax.Precision.HIGHEST)


def _rope_fwd(blk, c, sn, sp):
    return blk * c + pltpu.roll(blk, LANE - 16, 1) * sn + pltpu.roll(blk, 16, 1) * sp


def _rope_bwd(blk, c, sn, sp):
    return blk * c + pltpu.roll(blk * sn, 16, 1) + pltpu.roll(blk * sp, LANE - 16, 1)


def _mesh_pos():
    return lax.axis_index("x"), lax.axis_index("y"), lax.axis_index("c")


def _hbm_specs(n):
    return [pl.BlockSpec(memory_space=pltpu.HBM) for _ in range(n)]


def _dev(d):
    return d >> 2, (d >> 1) & 1, d & 1


def _gather_plan(shards, sources):
    na, most = len(shards), max(len(s) for s in sources)
    out_shape = [jax.ShapeDtypeStruct((len(srcs),) + s.shape, s.dtype)
                 for s, srcs in zip(shards, sources)]
    sems = [pltpu.SemaphoreType.DMA((na, most)) for _ in range(3)]
    sems += [pltpu.SemaphoreType.DMA((na, most, 3))]
    sems += [pltpu.SemaphoreType.DMA((na, most)) for _ in range(3)]
    return out_shape, sems


def _gather_hooks(x_refs, out_refs, sems, sources):
    local_sems, d2d_send, d2d_recv, ici_send, ici_recv, fwd_send, fwd_recv = sems
    x, y, c = _mesh_pos()
    chips = [(1 - x, y), (x, 1 - y), (1 - x, 1 - y)]
    items = []
    for a, srcs in enumerate(sources):
        for i, d in enumerate(srcs):
            dx, dy, dc = _dev(d)
            near = jnp.logical_and(x == dx, y == dy)
            far = jnp.logical_not(near)
            slot = out_refs[a].at[i]

            def remote(src, to, send_sem, recv_sem, slot=slot):
                return pltpu.make_async_remote_copy(
                    src_ref=src, dst_ref=slot, send_sem=send_sem, recv_sem=recv_sem,
                    device_id=to, device_id_type=MESH)

            items.append(dict(
                me=jnp.logical_and(near, c == dc), sibling=jnp.logical_and(near, c != dc),
                relay=jnp.logical_and(far, c == dc), behind=jnp.logical_and(far, c != dc),
                local=pltpu.make_async_copy(x_refs[a], slot, local_sems.at[a, i]),
                to_sibling=remote(x_refs[a], (x, y, 1 - c), d2d_send.at[a, i], d2d_recv.at[a, i]),
                to_chips=[remote(x_refs[a], (*chip, c), ici_send.at[a, i, j], ici_recv.at[a, i])
                          for j, chip in enumerate(chips)],
                forward=remote(slot, (x, y, 1 - c), fwd_send.at[a, i], fwd_recv.at[a, i])))

    def start():
        for it in items:
            @pl.when(it["me"])
            def _(it=it):
                it["local"].start()
                it["to_sibling"].start()
                for cp in it["to_chips"]:
                    cp.start()

    def finish():
        for it in items:
            @pl.when(it["relay"])
            def _(it=it):
                it["to_chips"][0].wait_recv()
                it["forward"].start()
        for it in items:
            pl.when(it["sibling"])(it["to_sibling"].wait_recv)
            pl.when(it["behind"])(it["forward"].wait_recv)
            pl.when(it["relay"])(it["forward"].wait_send)

            @pl.when(it["me"])
            def _(it=it):
                it["local"].wait()
                it["to_sibling"].wait_send()
                for cp in it["to_chips"]:
                    cp.wait_send()

    return start, finish


def _all_gather(name, shards, sources):
    n = len(shards)
    out_shape, sems = _gather_plan(shards, sources)

    def body(*refs):
        start, finish = _gather_hooks(refs[:n], refs[n:2 * n], refs[2 * n:], sources)
        start()
        finish()

    return pl.pallas_call(
        body, name=name,
        out_shape=tuple(out_shape),
        in_specs=_hbm_specs(n), out_specs=tuple(_hbm_specs(n)),
        scratch_shapes=sems,
        compiler_params=_cparams(),
    )(*shards)


PEERS = N_DEV - 1


def _ici_copies(p_ref, land_ref, send_sems, recv_sems, dests):
    x, y, c = _mesh_pos()
    sends, arrivals = [], []
    for i, d in enumerate(dests):
        dx, dy, dc = _dev(d)
        k = (4 * (x != dx).astype(jnp.int32) + 2 * (y != dy).astype(jnp.int32)
             + (c != dc).astype(jnp.int32))
        slot = jnp.maximum(k - 1, 0)
        sends.append((k > 0, pltpu.make_async_remote_copy(
            src_ref=p_ref.at[i], dst_ref=land_ref.at[slot], send_sem=send_sems.at[i],
            recv_sem=recv_sems.at[slot], device_id=(dx, dy, dc), device_id_type=MESH)))
        arrivals.append((k == 0, [pltpu.make_async_remote_copy(
            src_ref=p_ref.at[i], dst_ref=land_ref.at[r], send_sem=send_sems.at[i],
            recv_sem=recv_sems.at[r], device_id=(dx, dy, dc), device_id_type=MESH)
            for r in range(PEERS)]))
    return sends, arrivals


def _ici_start(name, hs, lands, dests):
    na = len(hs)

    def body(*refs):
        h_refs, land_refs, sems = refs[:na], refs[na:2 * na], refs[2 * na:4 * na]
        token = refs[-1]
        for a in range(na):
            sends, _ = _ici_copies(h_refs[a], land_refs[a], sems[2 * a], sems[2 * a + 1], dests[a])
            for go, cp in sends:
                pl.when(go)(cp.start)
        token[...] = jnp.zeros_like(token)

    hbm, sem = pl.BlockSpec(memory_space=pltpu.HBM), pl.BlockSpec(memory_space=pltpu.SEMAPHORE)
    sem_shapes = []
    for a in range(na):
        sem_shapes += [pltpu.SemaphoreType.DMA((len(dests[a]),)), pltpu.SemaphoreType.DMA((PEERS,))]
    res = pl.pallas_call(
        body, name=name,
        out_shape=tuple(sem_shapes) + tuple(pltpu.HBM(v.shape, v.dtype) for v in list(hs) + list(lands))
        + (jax.ShapeDtypeStruct((8, LANE), F32),),
        in_specs=(hbm,) * (2 * na),
        out_specs=(sem,) * (2 * na) + (hbm,) * (2 * na) + (pl.BlockSpec(memory_space=pltpu.VMEM),),
        input_output_aliases={i: 2 * na + i for i in range(2 * na)},
        compiler_params=pltpu.CompilerParams(
            has_side_effects=pltpu.SideEffectType.DATAFLOW_SIDE_EFFECTING,
            vmem_limit_bytes=VMEM_LIMIT),
    )(*[pltpu.with_memory_space_constraint(v, pltpu.HBM) for v in list(hs) + list(lands)])
    sems = [(res[2 * a], res[2 * a + 1]) for a in range(na)]
    return sems, res[2 * na:3 * na], res[3 * na:4 * na], res[-1]


def _ici_wait(name, started, lands, after):
    k, nl = len(started), len(lands)

    def body(*refs):
        land_refs = refs[3 * k:3 * k + nl]
        for s in range(k):
            h_ref, send_sems, recv_sems = refs[3 * s:3 * s + 3]
            sends, arrivals = _ici_copies(h_ref, land_refs[started[s][3]], send_sems, recv_sems,
                                          started[s][4])
            for go, cp in sends:
                pl.when(go)(cp.wait_send)
            for here, cps in arrivals:
                for cp in cps:
                    pl.when(here)(cp.wait_recv)

    hbm, sem = pl.BlockSpec(memory_space=pltpu.HBM), pl.BlockSpec(memory_space=pltpu.SEMAPHORE)
    operands, specs = [], []
    for send_sems, recv_sems, h, _, _ in started:
        operands += [h, send_sems, recv_sems]
        specs += [hbm, sem, sem]
    return pl.pallas_call(
        body, name=name,
        out_shape=tuple(pltpu.HBM(v.shape, v.dtype) for v in lands),
        in_specs=tuple(specs) + (hbm,) * nl + (pl.BlockSpec(memory_space=pl.ANY),),
        out_specs=(hbm,) * nl,
        input_output_aliases={3 * k + i: i for i in range(nl)},
        compiler_params=pltpu.CompilerParams(
            has_side_effects=pltpu.SideEffectType.DATAFLOW_SIDE_EFFECTING,
            vmem_limit_bytes=VMEM_LIMIT),
    )(*operands, *lands, after)


def _weights_to_p(name, gathered, where, group):
    tl = 256
    off, width = P_GROUPS[group]
    segs = sorted([s for s in _segments() if _group_of(s[3]) == group], key=lambda s: s[3])
    used = sorted({where[s[0]][0] for s in segs})

    def body(*refs):
        g_refs, o_ref = dict(zip(used, refs[:-1])), refs[-1]
        pieces, pos = [], off
        for d, c0, n, p0 in segs:
            if p0 > pos:
                pieces.append(jnp.zeros((p0 - pos, tl), F32))
            k, slot = where[d]
            pieces.append(g_refs[k][slot, c0:c0 + n, :].astype(F32))
            pos = p0 + n
        if off + width > pos:
            pieces.append(jnp.zeros((off + width - pos, tl), F32))
        o_ref[...] = jnp.concatenate(pieces, axis=0).astype(BF16)

    return pl.pallas_call(
        body, name=name,
        grid=(D_MODEL // tl,),
        in_specs=[pl.BlockSpec((gathered[k].shape[0], SHARD_COLS, tl), lambda i: (0, 0, i))
                  for k in used],
        out_specs=pl.BlockSpec((width, tl), lambda i: (0, i)),
        out_shape=jax.ShapeDtypeStruct((width, D_MODEL), BF16),
        compiler_params=_cparams(("arbitrary",)),
    )(*[gathered[k] for k in used])


def _shard_groups(d):
    return sorted({_group_of(s[3]) for s in _segments() if s[0] == d})


def _grads_to_shards(name, groups, dests, own_prev):
    tl = 256
    segs = _segments()
    used = sorted(groups)

    def body(*refs):
        g_refs, prev_ref, o_ref, own_ref = dict(zip(used, refs[:-3])), refs[-3], refs[-2], refs[-1]
        x, y, c = _mesh_pos()
        own = prev_ref[...]
        for i, d in enumerate(dests):
            pieces = []
            for _, c0, n, p0 in sorted([s for s in segs if s[0] == d], key=lambda s: s[1]):
                gi = _group_of(p0)
                lo = p0 - P_GROUPS[gi][0]
                pieces.append(g_refs[gi][lo:lo + n, :].astype(F32))
            shard = jnp.concatenate(pieces, axis=0).astype(BF16)
            o_ref[i] = shard
            own = jnp.where(4 * x + 2 * y + c == d, shard, own)
        own_ref[...] = own

    blk = pl.BlockSpec((SHARD_COLS, tl), lambda i: (0, i))
    return pl.pallas_call(
        body, name=name,
        grid=(D_MODEL // tl,),
        in_specs=[pl.BlockSpec((P_GROUPS[g][1], tl), lambda i: (0, i)) for g in used] + [blk],
        out_specs=(pl.BlockSpec((len(dests), SHARD_COLS, tl), lambda i: (0, 0, i)), blk),
        out_shape=(jax.ShapeDtypeStruct((len(dests), SHARD_COLS, D_MODEL), BF16),
                   jax.ShapeDtypeStruct((SHARD_COLS, D_MODEL), BF16)),
        input_output_aliases={len(used): 1},
        compiler_params=_cparams(("arbitrary",)),
    )(*[groups[g] for g in used], own_prev)


def _inproj(x, g_in, w_pt):
    t = x.shape[0]
    tm = min(256, t)
    width = w_pt.shape[0]

    def body(x_ref, g_ref, w_ref, proj_ref, h_ref, r_ref):
        xf = x_ref[...]
        r = lax.rsqrt(jnp.mean(xf * xf, axis=-1, keepdims=True) + EPS)
        h = ((xf * r) * g_ref[...]).astype(BF16)
        proj_ref[...] = _dot_nt(h, w_ref[...])
        h_ref[...] = h
        r_ref[...] = r

    row = lambda w: pl.BlockSpec((tm, w), lambda i: (i, 0))
    return pl.pallas_call(
        body, name="inproj_latents",
        grid=(t // tm,),
        in_specs=[row(D_MODEL), pl.BlockSpec((1, D_MODEL), lambda i: (0, 0)),
                  pl.BlockSpec((width, D_MODEL), lambda i: (0, 0))],
        out_specs=(row(width), row(D_MODEL), row(1)),
        out_shape=(jax.ShapeDtypeStruct((t, width), F32),
                   jax.ShapeDtypeStruct((t, D_MODEL), BF16),
                   jax.ShapeDtypeStruct((t, 1), F32)),
        compiler_params=_cparams(("arbitrary",)),
    )(x, g_in, w_pt)


def _proj(name, h, w_pt):
    t = h.shape[0]
    tm = min(256, t)
    width = w_pt.shape[0]

    def body(h_ref, w_ref, o_ref):
        o_ref[...] = _dot_nt(h_ref[...], w_ref[...])

    return pl.pallas_call(
        body, name=name,
        grid=(t // tm,),
        in_specs=[pl.BlockSpec((tm, D_MODEL), lambda i: (i, 0)),
                  pl.BlockSpec((width, D_MODEL), lambda i: (0, 0))],
        out_specs=pl.BlockSpec((tm, width), lambda i: (i, 0)),
        out_shape=jax.ShapeDtypeStruct((t, width), F32),
        compiler_params=_cparams(("arbitrary",)),
    )(h, w_pt)


def _mla_prep(proj, g_q, g_kv, w_uq_p, w_k_p, w_v, w_gate_p, b_gate, rc, rsn, rsp):
    t = proj.shape[0]
    tm = min(256, t)
    hq = MLA_HEADS * HEAD_PAD

    def body(cq_ref, ckv_ref, misc_ref, gq_ref, gkv_ref, wuq_ref, wk_ref, wv_ref, wg_ref, bg_ref,
             c_ref, sn_ref, sp_ref,
             q_ref, k_ref, v_ref, la_ref, pre_ref, cqn_ref, ckvn_ref, rq_ref, rkv_ref, mb_ref):
        c, sn, sp = c_ref[...], sn_ref[...], sp_ref[...]
        cq = cq_ref[:, :MLA_Q_RANK]
        rq = lax.rsqrt(jnp.mean(cq * cq, axis=-1, keepdims=True) + EPS)
        cqn = ((cq * rq) * gq_ref[...]).astype(BF16)
        cqn_ref[...] = cqn
        rq_ref[...] = rq
        qpre = _dot(cqn, wuq_ref[...])
        ckv = ckv_ref[...]
        rkv = lax.rsqrt(jnp.mean(ckv * ckv, axis=-1, keepdims=True) + EPS)
        ckvn = ((ckv * rkv) * gkv_ref[...]).astype(BF16)
        ckvn_ref[...] = ckvn
        rkv_ref[...] = rkv
        kn = _dot(ckvn, wk_ref[...])
        v_ref[...] = _dot(ckvn, wv_ref[...]).astype(BF16)
        misc = misc_ref[...]
        krope = _rope_fwd(misc, c, sn, sp)
        for h in range(MLA_HEADS):
            sl = slice(h * HEAD_PAD, (h + 1) * HEAD_PAD)
            q_ref[:, sl] = _rope_fwd(qpre[:, sl], c, sn, sp).astype(BF16)
            k_ref[:, sl] = (kn[:, sl] + krope).astype(BF16)
        mb_ref[...] = misc.astype(BF16)
        pre = _dot(mb_ref[...], wg_ref[...]) + bg_ref[...]
        pre_ref[...] = pre
        log_a = (jnp.minimum(pre, 0.0) - jnp.log(1.0 + jnp.exp(-jnp.abs(pre)))) / GLA_GATE_NORM
        la_ref[...] = _dot_exact(_chunk_tri(tm, True), log_a)

    row = lambda w: pl.BlockSpec((tm, w), lambda i: (i, 0))
    full = lambda a: pl.BlockSpec(a.shape, lambda i: (0, 0))
    return pl.pallas_call(
        body, name="mla_prep",
        grid=(t // tm,),
        in_specs=[pl.BlockSpec((tm, 512), lambda i: (i, _rel(P_CQ) // 512)),
                  pl.BlockSpec((tm, MLA_KV_RANK), lambda i: (i, _rel(P_CKV) // MLA_KV_RANK)),
                  pl.BlockSpec((tm, LANE), lambda i: (i, _rel(P_MISC) // LANE)),
                  full(g_q), full(g_kv), full(w_uq_p), full(w_k_p), full(w_v), full(w_gate_p),
                  full(b_gate), row(LANE), row(LANE), row(LANE)],
        out_specs=(row(hq), row(hq), row(MLA_WIDTH), row(GLA_DK), row(GLA_DK),
                   row(MLA_Q_RANK), row(MLA_KV_RANK), row(1), row(1), row(LANE)),
        out_shape=(jax.ShapeDtypeStruct((t, hq), BF16), jax.ShapeDtypeStruct((t, hq), BF16),
                   jax.ShapeDtypeStruct((t, MLA_WIDTH), BF16),
                   jax.ShapeDtypeStruct((t, GLA_DK), F32), jax.ShapeDtypeStruct((t, GLA_DK), F32),
                   jax.ShapeDtypeStruct((t, MLA_Q_RANK), BF16),
                   jax.ShapeDtypeStruct((t, MLA_KV_RANK), BF16),
                   jax.ShapeDtypeStruct((t, 1), F32), jax.ShapeDtypeStruct((t, 1), F32),
                   jax.ShapeDtypeStruct((t, LANE), BF16)),
        compiler_params=_cparams(("arbitrary",)),
    )(proj, proj, proj, g_q, g_kv, w_uq_p, w_k_p, w_v, w_gate_p, b_gate, rc, rsn, rsp)


def _attn_masks(tq, i):
    keys = (i + 1) * tq
    rows = i * tq + lax.broadcasted_iota(jnp.int32, (tq, keys), 0)
    cols = lax.broadcasted_iota(jnp.int32, (tq, keys), 1)
    lane = lax.broadcasted_iota(jnp.int32, (tq, LANE), 1)
    return cols <= rows, lane < MLA_VDIM


def _for_each_query_tile(n_tiles, fn):
    for i in range(n_tiles):
        pl.when(pl.program_id(1) == i)(lambda i=i: fn(i))


def _mla_attn_fwd(q, k, v, shards, sources):
    t = q.shape[0]
    tq = min(256, t)
    scale = MLA_QK ** -0.5
    ns = len(shards)
    g_shapes, g_sems = _gather_plan(shards, sources)
    grid = (MLA_HEADS // 2, t // tq)

    def body(q_ref, k_ref, v_ref, *rest):
        o_ref, lse_ref = rest[ns:ns + 2]
        start, finish = _gather_hooks(rest[:ns], rest[ns + 2:2 * ns + 2], rest[2 * ns + 2:], sources)
        step = pl.program_id(0) * grid[1] + pl.program_id(1)
        pl.when(step == 0)(start)

        def tile(i):
            keys = (i + 1) * tq
            causal, low = _attn_masks(tq, i)
            vp = v_ref[0:keys, :]
            acc = jnp.zeros((tq, LANE), F32)
            for hh in range(2):
                sl = slice(hh * HEAD_PAD, (hh + 1) * HEAD_PAD)
                s = _dot_nt(q_ref[:, sl], k_ref[0:keys, sl]) * scale
                s = jnp.where(causal, s, -jnp.inf)
                m = jnp.max(s, axis=-1, keepdims=True)
                e = jnp.exp(s - m)
                l = jnp.sum(e, axis=-1, keepdims=True)
                o = _dot(e.astype(BF16), vp) / l
                acc = jnp.where(low if hh == 0 else jnp.logical_not(low), o, acc)
                lse_ref[hh] = m + jnp.log(l)
            o_ref[...] = acc

        _for_each_query_tile(t // tq, tile)
        pl.when(step == grid[0] * grid[1] - 1)(finish)

    res = pl.pallas_call(
        body, name="mla_attn_fwd",
        grid=grid,
        in_specs=[pl.BlockSpec((tq, 2 * HEAD_PAD), lambda p, i: (i, p)),
                  pl.BlockSpec((t, 2 * HEAD_PAD), lambda p, i: (0, p)),
                  pl.BlockSpec((t, LANE), lambda p, i: (0, p))] + _hbm_specs(ns),
        out_specs=(pl.BlockSpec((tq, LANE), lambda p, i: (i, p)),
                   pl.BlockSpec((2, tq, 1), lambda p, i: (p, i, 0))) + tuple(_hbm_specs(ns)),
        out_shape=(jax.ShapeDtypeStruct((t, MLA_WIDTH), F32),
                   jax.ShapeDtypeStruct((MLA_HEADS, t, 1), F32)) + tuple(g_shapes),
        scratch_shapes=g_sems,
        compiler_params=_cparams(("arbitrary", "arbitrary")),
    )(q, k, v, *shards)
    return res[0], res[1], res[2:]


def _mla_attn_bwd(q, k, v, o, do, lse, after):
    t = q.shape[0]
    tq = min(256, t)
    scale = MLA_QK ** -0.5

    def body(q_ref, k_ref, v_ref, o_ref, do_ref, lse_ref, after_ref, dq_ref, dk_ref, dv_ref):
        del after_ref

        @pl.when(pl.program_id(1) == 0)
        def _():
            dk_ref[...] = jnp.zeros_like(dk_ref)
            dv_ref[...] = jnp.zeros_like(dv_ref)

        def tile(i):
            keys = (i + 1) * tq
            causal, low = _attn_masks(tq, i)
            vp = v_ref[0:keys, :]
            do_all = do_ref[...]
            o_all = o_ref[...]
            dv_acc = jnp.zeros((keys, LANE), F32)
            for hh in range(2):
                sl = slice(hh * HEAD_PAD, (hh + 1) * HEAD_PAD)
                do_h = jnp.where(low if hh == 0 else jnp.logical_not(low), do_all, 0.0)
                dsum = jnp.sum(do_h * o_all, axis=-1, keepdims=True)
                qh = q_ref[:, sl]
                kh = k_ref[0:keys, sl]
                s = _dot_nt(qh, kh) * scale
                p = jnp.where(causal, jnp.exp(s - lse_ref[hh]), 0.0)
                do_b = do_h.astype(BF16)
                dp = _dot_nt(do_b, vp)
                ds = (p * (dp - dsum) * scale).astype(BF16)
                dq_ref[:, sl] = _dot(ds, kh).astype(BF16)
                dk_ref[0:keys, sl] += _dot_tn(ds, qh)
                dv_acc = dv_acc + _dot_tn(p.astype(BF16), do_b)
            dv_ref[0:keys, :] += dv_acc

        _for_each_query_tile(t // tq, tile)

    return pl.pallas_call(
        body, name="mla_attn_bwd",
        grid=(MLA_HEADS // 2, t // tq),
        in_specs=[pl.BlockSpec((tq, 2 * HEAD_PAD), lambda p, i: (i, p)),
                  pl.BlockSpec((t, 2 * HEAD_PAD), lambda p, i: (0, p)),
                  pl.BlockSpec((t, LANE), lambda p, i: (0, p)),
                  pl.BlockSpec((tq, LANE), lambda p, i: (i, p)),
                  pl.BlockSpec((tq, LANE), lambda p, i: (i, p)),
                  pl.BlockSpec((2, tq, 1), lambda p, i: (p, i, 0)),
                  pl.BlockSpec(memory_space=pl.ANY)],
        out_specs=(pl.BlockSpec((tq, 2 * HEAD_PAD), lambda p, i: (i, p)),
                   pl.BlockSpec((t, 2 * HEAD_PAD), lambda p, i: (0, p)),
                   pl.BlockSpec((t, LANE), lambda p, i: (0, p))),
        out_shape=(jax.ShapeDtypeStruct((t, MLA_HEADS * HEAD_PAD), BF16),
                   jax.ShapeDtypeStruct((t, MLA_HEADS * HEAD_PAD), F32),
                   jax.ShapeDtypeStruct((t, MLA_WIDTH), F32)),
        compiler_params=_cparams(("arbitrary", "arbitrary")),
    )(q, k, v, o, do, lse, after)


def _chunk_tri(n, lower):
    r = lax.broadcasted_iota(jnp.int32, (n, n), 0)
    c = lax.broadcasted_iota(jnp.int32, (n, n), 1)
    same = (r // GLA_CHUNK) == (c // GLA_CHUNK)
    return jnp.where(jnp.logical_and(same, r >= c if lower else r <= c), 1.0, 0.0).astype(F32)


def _gla_chunk_terms(q_ref, k_ref, b_ref, h):
    sl = slice(h * GLA_HK, (h + 1) * GLA_HK)
    b = b_ref[:, sl]
    bl = b[GLA_CHUNK - 1:GLA_CHUNK, :]
    kc = k_ref[:, sl]
    q_in = (q_ref[:, sl] * (GLA_HK ** -0.5)) * jnp.exp(b)
    k_in = kc * jnp.exp(-b)
    k_st = kc * jnp.exp(bl - b)
    return b, bl, q_in, k_in, k_st


def _tri(c, lower):
    r = lax.broadcasted_iota(jnp.int32, (c, c), 0)
    cc = lax.broadcasted_iota(jnp.int32, (c, c), 1)
    return jnp.where(r >= cc if lower else r <= cc, 1.0, 0.0).astype(F32)


def _gla_fwd(proj, log_a):
    t = proj.shape[0]
    c = GLA_CHUNK
    n = t // c

    def body(q_ref, k_ref, v_ref, la_ref, o_ref, sp_ref, st_ref):
        @pl.when(pl.program_id(0) == 0)
        def _():
            st_ref[...] = jnp.zeros_like(st_ref)

        tri = _tri(c, True)
        for h in range(GLA_HEADS):
            _, bl, q_in, k_in, k_st = _gla_chunk_terms(q_ref, k_ref, la_ref, h)
            vs = slice(h * GLA_HV, (h + 1) * GLA_HV)
            vv = v_ref[:, vs].astype(BF16)
            qb = q_in.astype(BF16)
            attn = _dot_nt(qb, k_in.astype(BF16)) * tri
            st = st_ref[h]
            sp_ref[0, h] = st
            o_ref[:, vs] = _dot(attn.astype(BF16), vv) + _dot_nt(qb, st.astype(BF16))
            st_ref[h] = st * jnp.exp(bl) + _dot_tn(vv, k_st.astype(BF16))

    return pl.pallas_call(
        body, name="gla_fwd",
        grid=(n,),
        in_specs=[pl.BlockSpec((c, GLA_DK), lambda i: (i, P_QG // GLA_DK)),
                  pl.BlockSpec((c, GLA_DK), lambda i: (i, P_KG // GLA_DK)),
                  pl.BlockSpec((c, GLA_DV), lambda i: (i, P_VG // GLA_DV)),
                  pl.BlockSpec((c, GLA_DK), lambda i: (i, 0))],
        out_specs=(pl.BlockSpec((c, GLA_DV), lambda i: (i, 0)),
                   pl.BlockSpec((1, GLA_HEADS, GLA_HV, GLA_HK), lambda i: (i, 0, 0, 0))),
        out_shape=(jax.ShapeDtypeStruct((t, GLA_DV), F32),
                   jax.ShapeDtypeStruct((n, GLA_HEADS, GLA_HV, GLA_HK), F32)),
        scratch_shapes=[pltpu.VMEM((GLA_HEADS, GLA_HV, GLA_HK), F32)],
        compiler_params=_cparams(("arbitrary",)),
    )(proj, proj, proj, log_a)


def _gla_bwd(proj, log_a, do, states, after):
    t = proj.shape[0]
    c = GLA_CHUNK
    n = t // c

    def body(q_ref, k_ref, v_ref, la_ref, do_ref, sp_ref, after_ref, dg_ref, dla_ref, ds_ref):
        del after_ref

        @pl.when(pl.program_id(0) == 0)
        def _():
            ds_ref[...] = jnp.zeros_like(ds_ref)

        tri = _tri(c, True)
        last = lax.broadcasted_iota(jnp.int32, (c, GLA_HK), 0) == c - 1
        for h in range(GLA_HEADS):
            b, bl, q_in, k_in, k_st = _gla_chunk_terms(q_ref, k_ref, la_ref, h)
            ks_ = slice(h * GLA_HK, (h + 1) * GLA_HK)
            vs = slice(h * GLA_HV, (h + 1) * GLA_HV)
            vv = v_ref[:, vs].astype(BF16)
            do_h = do_ref[:, vs]
            qb, kb, ksb = q_in.astype(BF16), k_in.astype(BF16), k_st.astype(BF16)
            attn = (_dot_nt(qb, kb) * tri).astype(BF16)
            st = sp_ref[0, h]
            dst = ds_ref[h]
            dstb = dst.astype(BF16)
            dattn = (_dot_nt(do_h, vv) * tri).astype(BF16)
            dg_ref[:, P_VG + h * GLA_HV:P_VG + (h + 1) * GLA_HV] = (
                _dot_tn(attn, do_h) + _dot_nt(ksb, dstb)).astype(BF16)
            dq_in = _dot(dattn, kb) + _dot(do_h, st.astype(BF16))
            dk_in = _dot_tn(dattn, qb)
            dk_st = _dot(vv, dstb)
            ebl = jnp.exp(bl)
            d_ebl = jnp.sum(st * dst, axis=0, keepdims=True)
            ds_ref[h] = _dot_tn(do_h, qb) + dst * ebl
            dg_ref[:, P_QG + h * GLA_HK:P_QG + (h + 1) * GLA_HK] = (
                dq_in * (GLA_HK ** -0.5) * jnp.exp(b)).astype(BF16)
            dg_ref[:, P_KG + h * GLA_HK:P_KG + (h + 1) * GLA_HK] = (
                dk_in * jnp.exp(-b) + dk_st * jnp.exp(bl - b)).astype(BF16)
            db = dq_in * q_in - dk_in * k_in - dk_st * k_st
            dbl = jnp.sum(dk_st * k_st, axis=0, keepdims=True) + d_ebl * ebl
            dla_ref[:, ks_] = db + jnp.where(last, dbl, 0.0)

    rev = lambda i: n - 1 - i
    gw = P_GROUPS[0][1]
    return pl.pallas_call(
        body, name="gla_bwd",
        grid=(n,),
        in_specs=[pl.BlockSpec((c, GLA_DK), lambda i: (rev(i), P_QG // GLA_DK)),
                  pl.BlockSpec((c, GLA_DK), lambda i: (rev(i), P_KG // GLA_DK)),
                  pl.BlockSpec((c, GLA_DV), lambda i: (rev(i), P_VG // GLA_DV)),
                  pl.BlockSpec((c, GLA_DK), lambda i: (rev(i), 0)),
                  pl.BlockSpec((c, GLA_DV), lambda i: (rev(i), 0)),
                  pl.BlockSpec((1, GLA_HEADS, GLA_HV, GLA_HK), lambda i: (rev(i), 0, 0, 0)),
                  pl.BlockSpec(memory_space=pl.ANY)],
        out_specs=(pl.BlockSpec((c, gw), lambda i: (rev(i), 0)),
                   pl.BlockSpec((c, GLA_DK), lambda i: (rev(i), 0))),
        out_shape=(jax.ShapeDtypeStruct((t, gw), BF16), jax.ShapeDtypeStruct((t, GLA_DK), F32)),
        scratch_shapes=[pltpu.VMEM((GLA_HEADS, GLA_HV, GLA_HK), F32)],
        compiler_params=_cparams(("arbitrary",)),
    )(proj, proj, proj, log_a, do, states, after)


def _post(o_mla, proj, o_gla, x, target, g_gla, g_final, w_pm, w_pg, w_o):
    t = x.shape[0]
    tm = min(128, t)
    g0, gw = P_GROUPS[1]

    def body(om_ref, zg_ref, gm_ref, gg_ref, zm_ref, og_ref, x_ref, tg_ref, ggla_ref, gf_ref,
             wpm_ref, wpg_ref, wo_ref,
             dx2_ref, dom_ref, dog_ref, dg_ref,
             mg_ref, um_ref, ug_ref, dym_ref, dyg_ref, loss_ref, dgf_ref, dggla_ref):
        @pl.when(pl.program_id(0) == 0)
        def _():
            loss_ref[...] = jnp.zeros_like(loss_ref)
            dgf_ref[...] = jnp.zeros_like(dgf_ref)
            dggla_ref[...] = jnp.zeros_like(dggla_ref)

        om = om_ref[...]
        zm = zm_ref[...]
        sm = _sigmoid(zm)
        silu_m = zm * sm
        um = (om * silu_m).astype(BF16)
        um_ref[...] = um
        ym = _dot(um, wpm_ref[...])

        ggla = ggla_ref[...]
        zg = zg_ref[...]
        sg = _sigmoid(zg)
        silu_g = zg * sg
        xhat, rstd, on = [], [], []
        for h in range(GLA_HEADS):
            blk = og_ref[:, h * GLA_HV:(h + 1) * GLA_HV]
            r = lax.rsqrt(jnp.mean(blk * blk, axis=-1, keepdims=True) + EPS)
            xhat.append(blk * r)
            rstd.append(r)
            on.append(xhat[h] * ggla)
        on = jnp.concatenate(on, axis=-1)
        ug = (on * silu_g).astype(BF16)
        ug_ref[...] = ug
        yg = _dot(ug, wpg_ref[...])

        sgm = _sigmoid(gm_ref[...])
        sgg = _sigmoid(gg_ref[...])
        merged = (sgm * ym + sgg * yg).astype(BF16)
        mg_ref[...] = merged
        x2 = x_ref[...] + _dot(merged, wo_ref[...])
        gf = gf_ref[...]
        rf = lax.rsqrt(jnp.mean(x2 * x2, axis=-1, keepdims=True) + EPS)
        xh = x2 * rf
        err = xh * gf - tg_ref[...]
        loss_ref[...] += 0.5 * jnp.sum(jnp.mean(err * err, axis=-1, keepdims=True))

        dy = err * (1.0 / D_MODEL)
        dgf_ref[...] += jnp.sum(dy * xh, axis=0, keepdims=True)
        dxh = dy * gf
        dx2 = rf * (dxh - xh * jnp.mean(dxh * xh, axis=-1, keepdims=True))
        dx2_ref[...] = dx2
        dmerged = _dot_nt(dx2.astype(BF16), wo_ref[...])
        dym = (dmerged * sgm).astype(BF16)
        dyg = (dmerged * sgg).astype(BF16)
        dym_ref[...] = dym
        dyg_ref[...] = dyg
        dg_ref[:, P_GMLA - g0:P_GMLA - g0 + D_MODEL] = (dmerged * ym * sgm * (1.0 - sgm)).astype(BF16)
        dg_ref[:, P_GGLA - g0:P_GGLA - g0 + D_MODEL] = (dmerged * yg * sgg * (1.0 - sgg)).astype(BF16)
        dum = _dot_nt(dym, wpm_ref[...])
        dom_ref[...] = dum * silu_m
        dg_ref[:, P_ZMLA - g0:P_ZMLA - g0 + MLA_WIDTH] = (
            dum * om * (sm * (1.0 + zm * (1.0 - sm)))).astype(BF16)
        dug = _dot_nt(dyg, wpg_ref[...])
        dg_ref[:, P_ZGLA - g0:P_ZGLA - g0 + GLA_DV] = (
            dug * on * (sg * (1.0 + zg * (1.0 - sg)))).astype(BF16)
        don = dug * silu_g
        dggla = jnp.zeros((1, GLA_HV), F32)
        for h in range(GLA_HEADS):
            hs = slice(h * GLA_HV, (h + 1) * GLA_HV)
            don_h = don[:, hs]
            dggla = dggla + jnp.sum(don_h * xhat[h], axis=0, keepdims=True)
            dxh_h = don_h * ggla
            dog_ref[:, hs] = (rstd[h] * (dxh_h - xhat[h] * jnp.mean(dxh_h * xhat[h], axis=-1,
                                                                     keepdims=True))).astype(BF16)
        dggla_ref[...] += dggla

    row = lambda w: pl.BlockSpec((tm, w), lambda i: (i, 0))
    pcol = lambda w, off: pl.BlockSpec((tm, w), lambda i: (i, _rel(off) // w))
    full = lambda a: pl.BlockSpec(a.shape, lambda i: (0, 0))
    sds = jax.ShapeDtypeStruct
    return pl.pallas_call(
        body, name="post_fwd_bwd",
        grid=(t // tm,),
        in_specs=[row(MLA_WIDTH), pcol(GLA_DV, P_ZGLA), pcol(D_MODEL, P_GMLA), pcol(D_MODEL, P_GGLA),
                  pcol(MLA_WIDTH, P_ZMLA), row(GLA_DV), row(D_MODEL), row(D_MODEL),
                  full(g_gla), full(g_final), full(w_pm), full(w_pg), full(w_o)],
        out_specs=(row(D_MODEL), row(MLA_WIDTH), row(GLA_DV), row(gw),
                   row(D_MODEL), row(MLA_WIDTH), row(GLA_DV), row(D_MODEL), row(D_MODEL),
                   pl.BlockSpec((1, LANE), lambda i: (0, 0)),
                   pl.BlockSpec((1, D_MODEL), lambda i: (0, 0)),
                   pl.BlockSpec((1, GLA_HV), lambda i: (0, 0))),
        out_shape=(sds((t, D_MODEL), F32), sds((t, MLA_WIDTH), F32), sds((t, GLA_DV), BF16),
                   sds((t, gw), BF16),
                   sds((t, D_MODEL), BF16), sds((t, MLA_WIDTH), BF16), sds((t, GLA_DV), BF16),
                   sds((t, D_MODEL), BF16), sds((t, D_MODEL), BF16),
                   sds((1, LANE), F32), sds((1, D_MODEL), F32), sds((1, GLA_HV), F32)),
        compiler_params=_cparams(("arbitrary",)),
    )(o_mla, proj, proj, proj, proj, o_gla, x, target, g_gla, g_final, w_pm, w_pg, w_o)


def _mla_prep_bwd(dq, dk, dv, dla, pre, proj, rq, rkv, g_q, g_kv, w_uq_p, w_k_p, w_v, w_gate_p,
                  rc, rsn, rsp):
    t = proj.shape[0]
    tm = min(256, t)
    gw = P_GROUPS[2][1]

    def body(dq_ref, dk_ref, dv_ref, dla_ref, pre_ref, cq_ref, ckv_ref, rq_ref, rkv_ref,
             gq_ref, gkv_ref, wuq_ref, wk_ref, wv_ref, wg_ref, c_ref, sn_ref, sp_ref,
             dg_ref, dqpre_ref, dpre_ref, dgq_ref, dgkv_ref, dbg_ref):
        @pl.when(pl.program_id(0) == 0)
        def _():
            dgq_ref[...] = jnp.zeros_like(dgq_ref)
            dgkv_ref[...] = jnp.zeros_like(dgkv_ref)
            dbg_ref[...] = jnp.zeros_like(dbg_ref)

        c, sn, sp = c_ref[...], sn_ref[...], sp_ref[...]
        dkr = jnp.zeros((tm, LANE), F32)
        for h in range(MLA_HEADS):
            sl = slice(h * HEAD_PAD, (h + 1) * HEAD_PAD)
            dqpre_ref[:, sl] = _rope_bwd(dq_ref[:, sl].astype(F32), c, sn, sp).astype(BF16)
            dkr = dkr + dk_ref[:, sl]
        dcqn = _dot_nt(dqpre_ref[...], wuq_ref[...])
        rq = rq_ref[...]
        xh = cq_ref[:, :MLA_Q_RANK] * rq
        dgq_ref[...] += jnp.sum(dcqn * xh, axis=0, keepdims=True)
        dxh = dcqn * gq_ref[...]
        dcq = rq * (dxh - xh * jnp.mean(dxh * xh, axis=-1, keepdims=True))
        dg_ref[:, :MLA_Q_RANK] = dcq.astype(BF16)
        dg_ref[:, MLA_Q_RANK:512] = jnp.zeros((tm, 512 - MLA_Q_RANK), BF16)

        dckvn = _dot_nt(dk_ref[...].astype(BF16), wk_ref[...]) + \
            _dot_nt(dv_ref[...].astype(BF16), wv_ref[...])
        rkv = rkv_ref[...]
        xh = ckv_ref[...] * rkv
        dgkv_ref[...] += jnp.sum(dckvn * xh, axis=0, keepdims=True)
        dxh = dckvn * gkv_ref[...]
        dg_ref[:, P_CKV - P_CQ:P_CKV - P_CQ + MLA_KV_RANK] = (
            rkv * (dxh - xh * jnp.mean(dxh * xh, axis=-1, keepdims=True))).astype(BF16)

        dlog_a = _dot_exact(_chunk_tri(tm, False), dla_ref[...])
        dpre = dlog_a * (1.0 / GLA_GATE_NORM) * (1.0 - _sigmoid(pre_ref[...]))
        dbg_ref[...] += jnp.sum(dpre, axis=0, keepdims=True)
        dpre = dpre.astype(BF16)
        dpre_ref[...] = dpre
        lane = lax.broadcasted_iota(jnp.int32, (tm, LANE), 1)
        in_kr = jnp.logical_and(lane >= MISC_KR, lane < MISC_KR + MLA_ROPE)
        dmisc = jnp.where(in_kr, _rope_bwd(dkr, c, sn, sp), 0.0) + _dot_nt(dpre, wg_ref[...])
        dg_ref[:, P_MISC - P_CQ:P_MISC - P_CQ + LANE] = dmisc.astype(BF16)

    hq = MLA_HEADS * HEAD_PAD
    row = lambda w: pl.BlockSpec((tm, w), lambda i: (i, 0))
    full = lambda a: pl.BlockSpec(a.shape, lambda i: (0, 0))
    acc = lambda w: pl.BlockSpec((1, w), lambda i: (0, 0))
    sds = jax.ShapeDtypeStruct
    return pl.pallas_call(
        body, name="mla_prep_bwd",
        grid=(t // tm,),
        in_specs=[row(hq), row(hq), row(MLA_WIDTH), row(GLA_DK), row(GLA_DK),
                  pl.BlockSpec((tm, 512), lambda i: (i, _rel(P_CQ) // 512)),
                  pl.BlockSpec((tm, MLA_KV_RANK), lambda i: (i, _rel(P_CKV) // MLA_KV_RANK)),
                  row(1), row(1), full(g_q), full(g_kv), full(w_uq_p), full(w_k_p), full(w_v),
                  full(w_gate_p), row(LANE), row(LANE), row(LANE)],
        out_specs=(row(gw), row(hq), row(GLA_DK),
                   acc(MLA_Q_RANK), acc(MLA_KV_RANK), acc(GLA_DK)),
        out_shape=(sds((t, gw), BF16), sds((t, hq), BF16), sds((t, GLA_DK), BF16),
                   sds((1, MLA_Q_RANK), F32), sds((1, MLA_KV_RANK), F32), sds((1, GLA_DK), F32)),
        compiler_params=_cparams(("arbitrary",)),
    )(dq, dk, dv, dla, pre, proj, proj, rq, rkv, g_q, g_kv, w_uq_p, w_k_p, w_v, w_gate_p,
      rc, rsn, rsp)


def _inproj_bwd(dgroups, w_pts, x, rstd, g_in, dx2, after):
    t = x.shape[0]
    tm = min(256, t)

    def body(d0_ref, d1_ref, d2_ref, w0_ref, w1_ref, w2_ref, x_ref, r_ref, g_ref, dx2_ref, after_ref,
             dx_ref, dg_ref):
        del after_ref

        @pl.when(pl.program_id(0) == 0)
        def _():
            dg_ref[...] = jnp.zeros_like(dg_ref)

        dh = jnp.zeros((tm, D_MODEL), F32)
        for d_ref, w_ref in zip((d0_ref, d1_ref, d2_ref), (w0_ref, w1_ref, w2_ref)):
            dh = dh + _dot(d_ref[...], w_ref[...])
        r = r_ref[...]
        xh = x_ref[...] * r
        dg_ref[...] += jnp.sum(dh * xh, axis=0, keepdims=True)
        dxh = dh * g_ref[...]
        dx_ref[...] = dx2_ref[...] + r * (dxh - xh * jnp.mean(dxh * xh, axis=-1, keepdims=True))

    row = lambda w: pl.BlockSpec((tm, w), lambda i: (i, 0))
    return pl.pallas_call(
        body, name="inproj_bwd",
        grid=(t // tm,),
        in_specs=[row(w) for _, w in P_GROUPS]
        + [pl.BlockSpec((w, D_MODEL), lambda i: (0, 0)) for _, w in P_GROUPS]
        + [row(D_MODEL), row(1), pl.BlockSpec((1, D_MODEL), lambda i: (0, 0)), row(D_MODEL),
           pl.BlockSpec(memory_space=pl.ANY)],
        out_specs=(row(D_MODEL), pl.BlockSpec((1, D_MODEL), lambda i: (0, 0))),
        out_shape=(jax.ShapeDtypeStruct((t, D_MODEL), F32),
                   jax.ShapeDtypeStruct((1, D_MODEL), F32)),
        compiler_params=_cparams(("arbitrary",)),
    )(*dgroups, *w_pts, x, rstd, g_in, dx2, after)


def _matmul(name, a, b, tm, tn, dtype=F32):
    kk, m = a.shape
    n = b.shape[1]

    def body(a_ref, b_ref, o_ref):
        o_ref[...] = _dot_tn(a_ref[...].astype(BF16), b_ref[...].astype(BF16)).astype(dtype)

    return pl.pallas_call(
        body, name=name,
        grid=(n // tn, m // tm),
        in_specs=[pl.BlockSpec((kk, tm), lambda j, i: (0, i)),
                  pl.BlockSpec((kk, tn), lambda j, i: (0, j))],
        out_specs=pl.BlockSpec((tm, tn), lambda j, i: (i, j)),
        out_shape=jax.ShapeDtypeStruct((m, n), dtype),
        compiler_params=_cparams(("arbitrary", "arbitrary")),
    )(a, b)


def _adamw_update(part_refs, w_ref, m_ref, v_ref, g_ref, d_ref, nm_ref, nv_ref):
    g = part_refs[0][...].astype(F32)
    for p_ref in part_refs[1:]:
        g = g + p_ref[...].astype(F32)
    m_new = ADAM_B1 * m_ref[...] + (1.0 - ADAM_B1) * g
    v_new = ADAM_B2 * v_ref[...] + (1.0 - ADAM_B2) * (g * g)
    m_hat = m_new / (1.0 - ADAM_B1 ** ADAM_STEP)
    v_hat = v_new / (1.0 - ADAM_B2 ** ADAM_STEP)
    g_ref[...] = g
    nm_ref[...] = m_new
    nv_ref[...] = v_new
    d_ref[...] = -ADAM_LR * (m_hat / (jnp.sqrt(v_hat) + ADAM_EPS) + ADAM_WD * w_ref[...])


def _adamw_rows(name, parts, w, m, v, tr, first=None):
    _, rows, cols = w.shape
    slots = parts.shape[0]

    def body(*refs):
        lead_refs, p_ref = ([], refs[0]) if first is None else ([refs[0]], refs[1])
        _adamw_update(lead_refs + [p_ref.at[q] for q in range(slots)], *refs[len(lead_refs) + 1:])

    blk = pl.BlockSpec((None, tr, cols), lambda i: (0, i, 0))
    out = jax.ShapeDtypeStruct((1, rows, cols), F32)
    lead = [] if first is None else [pl.BlockSpec((tr, cols), lambda i: (i, 0))]
    return pl.pallas_call(
        body, name=name,
        grid=(rows // tr,),
        in_specs=lead + [pl.BlockSpec((slots, tr, cols), lambda i: (0, i, 0)), blk, blk, blk],
        out_specs=(blk, blk, blk, blk),
        out_shape=(out, out, out, out),
        compiler_params=_cparams(("arbitrary",)),
    )(*([] if first is None else [first]), parts, w, m, v)


def _adamw_transposed(name, first, parts, w, m, v, tl):
    _, rows, cols = w.shape
    slots = parts.shape[0]

    def body(f_ref, p_ref, *refs):
        _adamw_update([f_ref] + [p_ref.at[q] for q in range(slots)], *refs)

    blk = pl.BlockSpec((cols, None, tl), lambda i: (0, 0, i))
    out = jax.ShapeDtypeStruct((cols, 1, rows), F32)
    res = pl.pallas_call(
        body, name=name,
        grid=(rows // tl,),
        in_specs=[pl.BlockSpec((cols, tl), lambda i: (0, i)),
                  pl.BlockSpec((slots, cols, tl), lambda i: (0, 0, i)), blk, blk, blk],
        out_specs=(blk, blk, blk, blk),
        out_shape=(out, out, out, out),
        compiler_params=_cparams(("arbitrary",)),
    )(first, parts, *[a.transpose(2, 0, 1) for a in (w, m, v)])
    return [r.transpose(1, 2, 0) for r in res]


def _adamw_group(firsts, parts, ws, ms, vs):
    n = len(ws)

    def body(*refs):
        ins, outs = refs[:5 * n], refs[5 * n:]
        x, y, c = _mesh_pos()
        for a in range(n):
            _adamw_update([ins[a].at[4 * x + 2 * y + c]]
                          + [ins[n + a].at[q] for q in range(ins[n + a].shape[0])],
                          *[r.at[0] for r in (ins[2 * n + a], ins[3 * n + a], ins[4 * n + a])],
                          *[r.at[0] for r in outs[4 * a:4 * a + 4]])

    vmem = lambda k: [pl.BlockSpec(memory_space=pltpu.VMEM) for _ in range(k)]
    out_shape = []
    for w in ws:
        out_shape += [jax.ShapeDtypeStruct(w.shape, F32)] * 4
    res = pl.pallas_call(
        body, name="adamw_small_weights",
        in_specs=vmem(5 * n), out_specs=tuple(vmem(4 * n)), out_shape=tuple(out_shape),
        compiler_params=_cparams(),
    )(*firsts, *parts, *ws, *ms, *vs)
    return [res[4 * a:4 * a + 4] for a in range(n)]


def _rope_tables(positions):
    half = MLA_ROPE // 2
    freqs = ROPE_THETA ** (-jnp.arange(half, dtype=F32) / half)
    ang = positions.astype(F32).reshape(-1, 1) * freqs
    cos, sin = jnp.cos(ang), jnp.sin(ang)
    t = ang.shape[0]
    one, zero = jnp.ones((t, MLA_NOPE), F32), jnp.zeros((t, half), F32)
    tail = jnp.zeros((t, LANE - MLA_QK), F32)
    rc = jnp.concatenate([one, cos, cos, tail], axis=1)
    rsn = jnp.concatenate([0.0 * one, -sin, zero, tail], axis=1)
    rsp = jnp.concatenate([0.0 * one, zero, sin, tail], axis=1)
    return rc, rsn, rsp


def _cols_full(g):
    return g.transpose(1, 0, 2)


def kernel(x, positions, g_in, w_in, g_q, w_uq, g_kv, w_ukv, w_gla_gate, b_gla_gate, g_gla, w_proj_mla, w_proj_gla, w_out, g_final, loss_target, m_g_in, m_w_in, m_g_q, m_w_uq, m_g_kv, m_w_ukv, m_w_gla_gate, m_b_gla_gate, m_g_gla, m_w_proj_mla, m_w_proj_gla, m_w_out, m_g_final, v_g_in, v_w_in, v_g_q, v_w_uq, v_g_kv, v_w_ukv, v_w_gla_gate, v_b_gla_gate, v_g_gla, v_w_proj_mla, v_w_proj_gla, v_w_out, v_g_final):
    t = x.shape[1]
    x2d = x.reshape(t, D_MODEL)
    tgt = loss_target.reshape(t, D_MODEL)
    g_final2 = g_final.reshape(1, D_MODEL)
    sharded = [(w_in, m_w_in, v_w_in), (w_uq, m_w_uq, v_w_uq), (w_ukv, m_w_ukv, v_w_ukv),
               (w_gla_gate, m_w_gla_gate, v_w_gla_gate), (w_proj_mla, m_w_proj_mla, v_w_proj_mla),
               (w_proj_gla, m_w_proj_gla, v_w_proj_gla), (w_out, m_w_out, v_w_out)]

    w_in_t = w_in.transpose(2, 0, 1).reshape(SHARD_COLS, D_MODEL)
    everyone = tuple(range(N_DEV))
    w_in_b = w_in_t.astype(BF16)
    b_uq, b_ukv, b_gate, b_pm, b_pg, b_o = [s[0][0].astype(BF16) for s in sharded[1:]]
    stages = ((0, 2, 4, 6), (1, 3, 5, 7))
    where = {d: (k, i) for k, srcs in enumerate(stages) for i, d in enumerate(srcs)}
    g_in_1, g_uq, g_ukv, g_gate = _all_gather(
        "all_gather_first", [w_in_b, b_uq, b_ukv, b_gate], [stages[0]] + [everyone] * 3)
    w_uq_p = jnp.pad(_cols_full(g_uq), ((0, 0), (0, 0), (0, HEAD_PAD - MLA_QK))).reshape(
        MLA_Q_RANK, MLA_HEADS * HEAD_PAD)
    ukv = _cols_full(g_ukv)
    w_k_p = jnp.pad(ukv[:, :, :MLA_NOPE], ((0, 0), (0, 0), (0, HEAD_PAD - MLA_NOPE))).reshape(
        MLA_KV_RANK, MLA_HEADS * HEAD_PAD)
    w_v = ukv[:, :, MLA_NOPE:].reshape(MLA_KV_RANK, MLA_WIDTH)
    w_gate_p = jnp.pad(_cols_full(g_gate).reshape(GLA_GATE_RANK, GLA_DK),
                       ((MISC_ALR, LANE - MISC_ALR - GLA_GATE_RANK), (0, 0)))
    rc, rsn, rsp = _rope_tables(positions)

    w_lat = _weights_to_p("weights_latents", [g_in_1], where, 2)
    proj_lat, h, rstd = _inproj(x2d, g_in, w_lat)
    q, k, v, log_a, pre, cqn, ckvn, rq, rkv, misc = _mla_prep(
        proj_lat, g_q, g_kv, w_uq_p, w_k_p, w_v, w_gate_p, b_gla_gate, rc, rsn, rsp)
    o_mla, lse, (g_in_2, g_pm, g_pg, g_o) = _mla_attn_fwd(
        q, k, v, [w_in_b, b_pm, b_pg, b_o], [stages[1]] + [everyone] * 3)
    w_gla = _weights_to_p("weights_gla", [g_in_1, g_in_2], where, 0)
    proj_gla = _proj("inproj_gla", h, w_gla)
    o_gla, states = _gla_fwd(proj_gla, log_a)
    w_out_path = _weights_to_p("weights_out_path", [g_in_1, g_in_2], where, 1)
    proj_out = _proj("inproj_out_path", h, w_out_path)
    w_in_p = (w_gla, w_out_path, w_lat)
    w_pm = _cols_full(g_pm).reshape(MLA_WIDTH, D_MODEL)
    w_pg = g_pg.reshape(GLA_DV, D_MODEL)
    w_o = g_o.reshape(D_MODEL, D_MODEL)

    (dx2, do_mla, do_gla, d_out, merged, um, ug, dym, dyg, loss_p, dg_final,
     dg_gla) = _post(o_mla, proj_out, o_gla, x2d, tgt, g_gla, g_final2, w_pm, w_pg, w_o)

    p_pm = _matmul("dw_proj_mla", um, dym, 512, 512, BF16).reshape(
        MLA_WIDTH, N_DEV, D_MODEL // N_DEV).transpose(1, 0, 2)
    p_pg = _matmul("dw_proj_gla", ug, dyg, 512, 512, BF16).reshape(N_DEV, -1, D_MODEL)
    p_o = _matmul("dw_out", merged, dx2, 512, 512, BF16).reshape(N_DEV, -1, D_MODEL)
    own_in = jnp.zeros((SHARD_COLS, D_MODEL), BF16)
    land_in = lax.empty((PEERS, SHARD_COLS, D_MODEL), BF16)
    dw_groups, started, lands = {}, [], [land_in]

    def reduce_scatter_stage(s, dests, own_in, extra=()):
        assert set(g for d in dests for g in _shard_groups(d)) <= set(dw_groups)
        parts_in, own_in = _grads_to_shards("grads_to_shards_%d" % s, dw_groups, dests, own_in)
        first = len(lands)
        lands.extend(lax.empty((PEERS,) + p.shape[1:], BF16) for p in extra)
        idx = [0] + list(range(first, len(lands)))
        all_dests = [dests] + [everyone] * len(extra)
        sems, parts, new_lands, token = _ici_start(
            "ici_start_%d" % s, [parts_in] + list(extra), [lands[i] for i in idx], all_dests)
        for a, i in enumerate(idx):
            lands[i] = new_lands[a]
            started.append((sems[a][0], sems[a][1], parts[a], i, all_dests[a]))
        return own_in, token

    dw_groups[1] = _matmul("dw_in_1", d_out, h, 512, 512, BF16)
    own_in, token = reduce_scatter_stage(1, (5, 6, 7), own_in, (p_pm, p_pg, p_o))
    d_gla, dla = _gla_bwd(proj_gla, log_a, do_gla, states, token)
    dw_groups[0] = _matmul("dw_in_0", d_gla, h, 512, 512, BF16)
    own_in, token = reduce_scatter_stage(2, (1, 2, 3), own_in)
    dq, dk, dv = _mla_attn_bwd(q, k, v, o_mla, do_mla, lse, token)
    d_lat, dqpre, dpre, dg_q, dg_kv, db_gate = _mla_prep_bwd(
        dq, dk, dv, dla, pre, proj_lat, rq, rkv, g_q, g_kv, w_uq_p, w_k_p, w_v, w_gate_p, rc, rsn, rsp)
    dw_groups[2] = _matmul("dw_in_2", d_lat, h, 896, 512, BF16)
    dw_uq = _matmul("dw_uq", cqn, dqpre, MLA_Q_RANK, 512, BF16)
    p_uq = dw_uq.reshape(MLA_Q_RANK, MLA_HEADS, HEAD_PAD)[:, :, :MLA_QK].transpose(1, 0, 2)
    dw_k = _matmul("dw_uk", ckvn, dk, MLA_KV_RANK, 512, BF16)
    dw_v = _matmul("dw_uv", ckvn, dv, MLA_KV_RANK, 512, BF16)
    p_ukv = jnp.concatenate(
        [dw_k.reshape(MLA_KV_RANK, MLA_HEADS, HEAD_PAD)[:, :, :MLA_NOPE],
         dw_v.reshape(MLA_KV_RANK, MLA_HEADS, MLA_VDIM)], axis=2).transpose(1, 0, 2)
    dw_gate = _matmul("dw_gate", misc, dpre, LANE, 512, BF16)
    p_gate = dw_gate[MISC_ALR:MISC_ALR + GLA_GATE_RANK].reshape(
        GLA_GATE_RANK, N_DEV, GLA_DK // N_DEV).transpose(1, 0, 2)
    own_in, token = reduce_scatter_stage(3, (0, 4), own_in, (p_uq, p_ukv, p_gate))
    grad_x, dg_in = _inproj_bwd((d_gla, d_out, d_lat), w_in_p, x2d, rstd, g_in, dx2, token)
    small = jnp.concatenate([dg_in.reshape(-1), dg_q.reshape(-1), dg_kv.reshape(-1),
                             db_gate.reshape(-1), dg_gla.reshape(-1), dg_final.reshape(-1),
                             loss_p[0, :1]])
    small = jnp.pad(small, (0, SMALL_ROWS * LANE - small.shape[0])).reshape(SMALL_ROWS, LANE)

    (small_all,) = _all_gather("all_gather_small", [small], [everyone])
    lands = _ici_wait("ici_wait", started, lands, small_all)
    big = [_adamw_transposed("adamw_w_in", own_in, lands[0], *sharded[0], 256)]
    big += _adamw_group([p_uq, p_ukv, p_gate, p_pm, p_pg, p_o], list(lands[4:7]) + list(lands[1:4]),
                        *[[s[j] for s in sharded[1:]] for j in range(3)])
    replicated = [(g_in, m_g_in, v_g_in), (g_q, m_g_q, v_g_q), (g_kv, m_g_kv, v_g_kv),
                  (b_gla_gate, m_b_gla_gate, v_b_gla_gate), (g_gla, m_g_gla, v_g_gla),
                  (g_final, m_g_final, v_g_final)]
    spacks = [jnp.pad(jnp.concatenate([s[j].reshape(-1) for s in replicated]),
                      (0, SMALL_ROWS * LANE - sum(SMALL_SIZES))).reshape(1, SMALL_ROWS, LANE)
              for j in range(3)]
    tiny = _adamw_rows("adamw_gains", small_all, spacks[0], spacks[1], spacks[2], SMALL_ROWS)

    outs = {}
    names = ("w_in", "w_uq", "w_ukv", "w_gla_gate", "w_proj_mla", "w_proj_gla", "w_out")
    for j, kind in enumerate(("grad", "delta", "new_m", "new_v")):
        for name, res in zip(names, big):
            outs[kind, name] = res[j]
        flat = tiny[j].reshape(-1)
        off = 0
        for name, size in zip(("g_in", "g_q", "g_kv", "b_gla_gate", "g_gla", "g_final"), SMALL_SIZES):
            shape = (size,) if name == "g_final" else (1, size)
            outs[kind, name] = flat[off:off + size].reshape(shape)
            off += size
    loss = tiny[0].reshape(-1)[sum(SMALL_SIZES)]
    order = ("g_in", "w_in", "g_q", "w_uq", "g_kv", "w_ukv", "w_gla_gate", "b_gla_gate", "g_gla",
             "w_proj_mla", "w_proj_gla", "w_out", "g_final")
    result = [loss, grad_x.reshape(1, t, D_MODEL)]
    for kind in ("grad", "delta", "new_m", "new_v"):
        result += [outs[kind, name] for name in order]
    return tuple(result)
```

```python
import jax
import jax.numpy as jnp
from jax import lax
from jax.experimental import pallas as pl
from jax.experimental.pallas import tpu as pltpu

F32 = jnp.float32
BF16 = jnp.bfloat16
MESH = pl.DeviceIdType.MESH
N_DEV = 8

D_MODEL = 1024
EPS = 1e-6
MLA_HEADS = 8
MLA_NOPE = 64
MLA_ROPE = 32
MLA_VDIM = 64
MLA_Q_RANK = 384
MLA_KV_RANK = 256
MLA_QK = MLA_NOPE + MLA_ROPE
MLA_WIDTH = MLA_HEADS * MLA_VDIM
ROPE_THETA = 10000.0
GLA_HEADS = 4
GLA_DK = 512
GLA_DV = 1024
GLA_HK = 128
GLA_HV = 256
GLA_GATE_RANK = 16
GLA_GATE_NORM = 16.0
GLA_CHUNK = 64
D_IN = 6320

ADAM_LR = 0.001
ADAM_B1 = 0.9
ADAM_B2 = 0.999
ADAM_EPS = 1e-08
ADAM_WD = 0.01
ADAM_STEP = 10

LANE = 128
HEAD_PAD = 128
VMEM_LIMIT = 48 * 1024 * 1024

P_VG, P_QG, P_KG = 0, 1024, 1536
P_ZGLA, P_GMLA, P_GGLA, P_ZMLA = 2048, 3072, 4096, 5120
P_CQ, P_CKV, P_MISC = 5632, 6144, 6400
P_TOTAL = 6528
P_GROUPS = ((0, 2048), (2048, 3584), (5632, 896))
MISC_KR = 64
MISC_ALR = 96
SHARD_COLS = D_IN // N_DEV
SHARD_PAD = 800
P_COMPONENTS = ((0, 384, P_CQ), (384, 256, P_CKV), (640, 32, P_MISC + MISC_KR), (672, 512, P_ZMLA),
                (1184, 512, P_QG), (1696, 512, P_KG), (2208, 1024, P_VG),
                (3232, 16, P_MISC + MISC_ALR), (3248, 1024, P_ZGLA), (4272, 1024, P_GMLA),
                (5296, 1024, P_GGLA))

SMALL_SIZES = (1024, 384, 256, 512, 256, 1024)
SMALL_ROWS = 32


def _segments():
    segs = []
    for g0, n, p0 in P_COMPONENTS:
        g = g0
        while g < g0 + n:
            d = g // SHARD_COLS
            end = min(g0 + n, (d + 1) * SHARD_COLS)
            segs.append((d, g - d * SHARD_COLS, end - g, p0 + g - g0))
            g = end
    return segs


def _group_of(p0):
    return max(i for i, (off, _) in enumerate(P_GROUPS) if off <= p0)


def _rel(p0):
    return p0 - P_GROUPS[_group_of(p0)][0]


def _cparams(sem=None):
    if sem is None:
        return pltpu.CompilerParams(vmem_limit_bytes=VMEM_LIMIT)
    return pltpu.CompilerParams(dimension_semantics=sem, vmem_limit_bytes=VMEM_LIMIT)


def _sigmoid(v):
    return 1.0 / (1.0 + jnp.exp(-v))


def _dot(a, b):
    return jnp.dot(a, b, preferred_element_type=F32)


def _dot_nt(a, b):
    return lax.dot_general(a, b, (((1,), (1,)), ((), ())), preferred_element_type=F32)


def _dot_tn(a, b):
    return lax.dot_general(a, b, (((0,), (0,)), ((), ())), preferred_element_type=F32)


def _dot_exact(a, b):
    return jnp.dot(a, b, preferred_element_type=F32, precision=lax.Precision.HIGHEST)


def _rope_fwd(blk, c, sn, sp):
    return blk * c + pltpu.roll(blk, LANE - 16, 1) * sn + pltpu.roll(blk, 16, 1) * sp


def _rope_bwd(blk, c, sn, sp):
    return blk * c + pltpu.roll(blk * sn, 16, 1) + pltpu.roll(blk * sp, LANE - 16, 1)


def _mesh_pos():
    return lax.axis_index("x"), lax.axis_index("y"), lax.axis_index("c")


def _hbm_specs(n):
    return [pl.BlockSpec(memory_space=pltpu.HBM) for _ in range(n)]


def _dev(d):
    return d >> 2, (d >> 1) & 1, d & 1


def _gather_plan(shards, sources):
    na, most = len(shards), max(len(s) for s in sources)
    out_shape = [jax.ShapeDtypeStruct((len(srcs),) + s.shape, s.dtype)
                 for s, srcs in zip(shards, sources)]
    sems = [pltpu.SemaphoreType.DMA((na, most)) for _ in range(3)]
    sems += [pltpu.SemaphoreType.DMA((na, most, 3))]
    sems += [pltpu.SemaphoreType.DMA((na, most)) for _ in range(3)]
    return out_shape, sems


def _gather_hooks(x_refs, out_refs, sems, sources):
    local_sems, d2d_send, d2d_recv, ici_send, ici_recv, fwd_send, fwd_recv = sems
    x, y, c = _mesh_pos()
    chips = [(1 - x, y), (x, 1 - y), (1 - x, 1 - y)]
    items = []
    for a, srcs in enumerate(sources):
        for i, d in enumerate(srcs):
            dx, dy, dc = _dev(d)
            near = jnp.logical_and(x == dx, y == dy)
            far = jnp.logical_not(near)
            slot = out_refs[a].at[i]

            def remote(src, to, send_sem, recv_sem, slot=slot):
                return pltpu.make_async_remote_copy(
                    src_ref=src, dst_ref=slot, send_sem=send_sem, recv_sem=recv_sem,
                    device_id=to, device_id_type=MESH)

            items.append(dict(
                me=jnp.logical_and(near, c == dc), sibling=jnp.logical_and(near, c != dc),
                relay=jnp.logical_and(far, c == dc), behind=jnp.logical_and(far, c != dc),
                local=pltpu.make_async_copy(x_refs[a], slot, local_sems.at[a, i]),
                to_sibling=remote(x_refs[a], (x, y, 1 - c), d2d_send.at[a, i], d2d_recv.at[a, i]),
                to_chips=[remote(x_refs[a], (*chip, c), ici_send.at[a, i, j], ici_recv.at[a, i])
                          for j, chip in enumerate(chips)],
                forward=remote(slot, (x, y, 1 - c), fwd_send.at[a, i], fwd_recv.at[a, i])))

    def start():
        for it in items:
            @pl.when(it["me"])
            def _(it=it):
                it["local"].start()
                it["to_sibling"].start()
                for cp in it["to_chips"]:
                    cp.start()

    def finish():
        for it in items:
            @pl.when(it["relay"])
            def _(it=it):
                it["to_chips"][0].wait_recv()
                it["forward"].start()
        for it in items:
            pl.when(it["sibling"])(it["to_sibling"].wait_recv)
            pl.when(it["behind"])(it["forward"].wait_recv)
            pl.when(it["relay"])(it["forward"].wait_send)

            @pl.when(it["me"])
            def _(it=it):
                it["local"].wait()
                it["to_sibling"].wait_send()
                for cp in it["to_chips"]:
                    cp.wait_send()

    return start, finish


def _all_gather(name, shards, sources):
    n = len(shards)
    out_shape, sems = _gather_plan(shards, sources)

    def body(*refs):
        start, finish = _gather_hooks(refs[:n], refs[n:2 * n], refs[2 * n:], sources)
        start()
        finish()

    return pl.pallas_call(
        body, name=name,
        out_shape=tuple(out_shape),
        in_specs=_hbm_specs(n), out_specs=tuple(_hbm_specs(n)),
        scratch_shapes=sems,
        compiler_params=_cparams(),
    )(*shards)


PEERS = N_DEV - 1


def _whole(dests, rows):
    return [(i, d, 0, rows) for i, d in enumerate(dests)]


def _ici_copies(p_ref, land_ref, send_sems, recv_sems, pieces):
    x, y, c = _mesh_pos()
    sends, arrivals = [], []

    def rows_of(ref, j, r0, r1):
        return ref.at[j] if (r0, r1) == (0, ref.shape[1]) else ref.at[j, pl.ds(r0, r1 - r0)]

    for p, (i, d, r0, r1) in enumerate(pieces):
        dx, dy, dc = _dev(d)
        k = (4 * (x != dx).astype(jnp.int32) + 2 * (y != dy).astype(jnp.int32)
             + (c != dc).astype(jnp.int32))
        slot = jnp.maximum(k - 1, 0)
        sends.append((k > 0, pltpu.make_async_remote_copy(
            src_ref=rows_of(p_ref, i, r0, r1), dst_ref=rows_of(land_ref, slot, r0, r1),
            send_sem=send_sems.at[p], recv_sem=recv_sems.at[p * PEERS + slot],
            device_id=(dx, dy, dc), device_id_type=MESH)))
        arrivals.append((k == 0, [pltpu.make_async_remote_copy(
            src_ref=rows_of(p_ref, i, r0, r1), dst_ref=rows_of(land_ref, r, r0, r1),
            send_sem=send_sems.at[p], recv_sem=recv_sems.at[p * PEERS + r],
            device_id=(dx, dy, dc), device_id_type=MESH) for r in range(PEERS)]))
    return sends, arrivals


def _ici_start(name, hs, lands, dests):
    na = len(hs)

    def body(*refs):
        h_refs, land_refs, sems = refs[:na], refs[na:2 * na], refs[2 * na:4 * na]
        token = refs[-1]
        for a in range(na):
            sends, _ = _ici_copies(h_refs[a], land_refs[a], sems[2 * a], sems[2 * a + 1], dests[a])
            for go, cp in sends:
                pl.when(go)(cp.start)
        token[...] = jnp.zeros_like(token)

    hbm, sem = pl.BlockSpec(memory_space=pltpu.HBM), pl.BlockSpec(memory_space=pltpu.SEMAPHORE)
    sem_shapes = []
    for a in range(na):
        sem_shapes += [pltpu.SemaphoreType.DMA((len(dests[a]),)),
                       pltpu.SemaphoreType.DMA((len(dests[a]) * PEERS,))]
    res = pl.pallas_call(
        body, name=name,
        out_shape=tuple(sem_shapes) + tuple(pltpu.HBM(v.shape, v.dtype) for v in list(hs) + list(lands))
        + (jax.ShapeDtypeStruct((8, LANE), F32),),
        in_specs=(hbm,) * (2 * na),
        out_specs=(sem,) * (2 * na) + (hbm,) * (2 * na) + (pl.BlockSpec(memory_space=pltpu.VMEM),),
        input_output_aliases={i: 2 * na + i for i in range(2 * na)},
        compiler_params=pltpu.CompilerParams(
            has_side_effects=pltpu.SideEffectType.DATAFLOW_SIDE_EFFECTING,
            vmem_limit_bytes=VMEM_LIMIT),
    )(*[pltpu.with_memory_space_constraint(v, pltpu.HBM) for v in list(hs) + list(lands)])
    sems = [(res[2 * a], res[2 * a + 1]) for a in range(na)]
    return sems, res[2 * na:3 * na], res[3 * na:4 * na], res[-1]


def _ici_wait(name, started, lands, after):
    k, nl = len(started), len(lands)

    def body(*refs):
        land_refs = refs[3 * k:3 * k + nl]
        for s in range(k):
            h_ref, send_sems, recv_sems = refs[3 * s:3 * s + 3]
            sends, arrivals = _ici_copies(h_ref, land_refs[started[s][3]], send_sems, recv_sems,
                                          started[s][4])
            for go, cp in sends:
                pl.when(go)(cp.wait_send)
            for here, cps in arrivals:
                for cp in cps:
                    pl.when(here)(cp.wait_recv)

    hbm, sem = pl.BlockSpec(memory_space=pltpu.HBM), pl.BlockSpec(memory_space=pltpu.SEMAPHORE)
    operands, specs = [], []
    for send_sems, recv_sems, h, _, _ in started:
        operands += [h, send_sems, recv_sems]
        specs += [hbm, sem, sem]
    return pl.pallas_call(
        body, name=name,
        out_shape=tuple(pltpu.HBM(v.shape, v.dtype) for v in lands),
        in_specs=tuple(specs) + (hbm,) * nl + (pl.BlockSpec(memory_space=pl.ANY),),
        out_specs=(hbm,) * nl,
        input_output_aliases={3 * k + i: i for i in range(nl)},
        compiler_params=pltpu.CompilerParams(
            has_side_effects=pltpu.SideEffectType.DATAFLOW_SIDE_EFFECTING,
            vmem_limit_bytes=VMEM_LIMIT),
    )(*operands, *lands, after)


def _weights_to_p(name, gathered, where, group):
    tl = 256
    off, width = P_GROUPS[group]
    segs = sorted([s for s in _segments() if _group_of(s[3]) == group], key=lambda s: s[3])
    used = sorted({where[s[0]][0] for s in segs})

    def body(*refs):
        g_refs, o_ref = dict(zip(used, refs[:-1])), refs[-1]
        pieces, pos = [], off
        for d, c0, n, p0 in segs:
            if p0 > pos:
                pieces.append(jnp.zeros((p0 - pos, tl), F32))
            k, slot = where[d]
            pieces.append(g_refs[k][slot, c0:c0 + n, :].astype(F32))
            pos = p0 + n
        if off + width > pos:
            pieces.append(jnp.zeros((off + width - pos, tl), F32))
        o_ref[...] = jnp.concatenate(pieces, axis=0).astype(BF16)

    return pl.pallas_call(
        body, name=name,
        grid=(D_MODEL // tl,),
        in_specs=[pl.BlockSpec((gathered[k].shape[0], SHARD_COLS, tl), lambda i: (0, 0, i))
                  for k in used],
        out_specs=pl.BlockSpec((width, tl), lambda i: (0, i)),
        out_shape=jax.ShapeDtypeStruct((width, D_MODEL), BF16),
        compiler_params=_cparams(("arbitrary",)),
    )(*[gathered[k] for k in used])


def _shard_groups(d):
    return sorted({_group_of(s[3]) for s in _segments() if s[0] == d})


def _grads_to_shards(name, groups, dests, own_prev):
    tl = 256
    segs = _segments()
    used = sorted(groups)

    def body(*refs):
        g_refs, prev_ref, o_ref, own_ref = dict(zip(used, refs[:-3])), refs[-3], refs[-2], refs[-1]
        x, y, c = _mesh_pos()
        own = prev_ref[...].astype(F32)
        row = lax.broadcasted_iota(jnp.int32, (SHARD_PAD, tl), 0)
        for i, (d, ranges) in enumerate(dests):
            pieces, pos, asked = [], 0, None
            for r0, r1 in sorted(ranges):
                if r0 > pos:
                    pieces.append(jnp.zeros((r0 - pos, tl), F32))
                for _, c0, n, p0 in sorted([s for s in segs if s[0] == d], key=lambda s: s[1]):
                    a, b = max(c0, r0), min(c0 + n, r1)
                    if a < b:
                        gi = _group_of(p0)
                        lo = p0 - P_GROUPS[gi][0] + a - c0
                        pieces.append(g_refs[gi][lo:lo + b - a, :].astype(F32))
                if r1 > SHARD_COLS:
                    pieces.append(jnp.zeros((r1 - max(r0, SHARD_COLS), tl), F32))
                pos = r1
                inside = jnp.logical_and(row >= r0, row < r1)
                asked = inside if asked is None else jnp.logical_or(asked, inside)
            if pos < SHARD_PAD:
                pieces.append(jnp.zeros((SHARD_PAD - pos, tl), F32))
            shard = jnp.concatenate(pieces, axis=0)
            o_ref[i] = shard.astype(BF16)
            own = jnp.where(jnp.logical_and(4 * x + 2 * y + c == d, asked), shard, own)
        own_ref[...] = own.astype(BF16)

    blk = pl.BlockSpec((SHARD_PAD, tl), lambda i: (0, i))
    return pl.pallas_call(
        body, name=name,
        grid=(D_MODEL // tl,),
        in_specs=[pl.BlockSpec((P_GROUPS[g][1], tl), lambda i: (0, i)) for g in used] + [blk],
        out_specs=(pl.BlockSpec((len(dests), SHARD_PAD, tl), lambda i: (0, 0, i)), blk),
        out_shape=(jax.ShapeDtypeStruct((len(dests), SHARD_PAD, D_MODEL), BF16),
                   jax.ShapeDtypeStruct((SHARD_PAD, D_MODEL), BF16)),
        input_output_aliases={len(used): 1},
        compiler_params=_cparams(("arbitrary",)),
    )(*[groups[g] for g in used], own_prev)


def _inproj(x, g_in, w_pt):
    t = x.shape[0]
    tm = min(256, t)
    width = w_pt.shape[0]

    def body(x_ref, g_ref, w_ref, proj_ref, h_ref, r_ref):
        xf = x_ref[...]
        r = lax.rsqrt(jnp.mean(xf * xf, axis=-1, keepdims=True) + EPS)
        h = ((xf * r) * g_ref[...]).astype(BF16)
        proj_ref[...] = _dot_nt(h, w_ref[...])
        h_ref[...] = h
        r_ref[...] = r

    row = lambda w: pl.BlockSpec((tm, w), lambda i: (i, 0))
    return pl.pallas_call(
        body, name="inproj_latents",
        grid=(t // tm,),
        in_specs=[row(D_MODEL), pl.BlockSpec((1, D_MODEL), lambda i: (0, 0)),
                  pl.BlockSpec((width, D_MODEL), lambda i: (0, 0))],
        out_specs=(row(width), row(D_MODEL), row(1)),
        out_shape=(jax.ShapeDtypeStruct((t, width), F32),
                   jax.ShapeDtypeStruct((t, D_MODEL), BF16),
                   jax.ShapeDtypeStruct((t, 1), F32)),
        compiler_params=_cparams(("arbitrary",)),
    )(x, g_in, w_pt)


def _proj(name, h, w_pt):
    t = h.shape[0]
    tm = min(256, t)
    width = w_pt.shape[0]

    def body(h_ref, w_ref, o_ref):
        o_ref[...] = _dot_nt(h_ref[...], w_ref[...])

    return pl.pallas_call(
        body, name=name,
        grid=(t // tm,),
        in_specs=[pl.BlockSpec((tm, D_MODEL), lambda i: (i, 0)),
                  pl.BlockSpec((width, D_MODEL), lambda i: (0, 0))],
        out_specs=pl.BlockSpec((tm, width), lambda i: (i, 0)),
        out_shape=jax.ShapeDtypeStruct((t, width), F32),
        compiler_params=_cparams(("arbitrary",)),
    )(h, w_pt)


def _mla_prep(proj, g_q, g_kv, w_uq_p, w_k_p, w_v, w_gate_p, b_gate, rc, rsn, rsp):
    t = proj.shape[0]
    tm = min(256, t)
    hq = MLA_HEADS * HEAD_PAD

    def body(cq_ref, ckv_ref, misc_ref, gq_ref, gkv_ref, wuq_ref, wk_ref, wv_ref, wg_ref, bg_ref,
             c_ref, sn_ref, sp_ref,
             q_ref, k_ref, v_ref, la_ref, pre_ref, cqn_ref, ckvn_ref, rq_ref, rkv_ref, mb_ref):
        c, sn, sp = c_ref[...], sn_ref[...], sp_ref[...]
        cq = cq_ref[:, :MLA_Q_RANK]
        rq = lax.rsqrt(jnp.mean(cq * cq, axis=-1, keepdims=True) + EPS)
        cqn = ((cq * rq) * gq_ref[...]).astype(BF16)
        cqn_ref[...] = cqn
        rq_ref[...] = rq
        qpre = _dot(cqn, wuq_ref[...])
        ckv = ckv_ref[...]
        rkv = lax.rsqrt(jnp.mean(ckv * ckv, axis=-1, keepdims=True) + EPS)
        ckvn = ((ckv * rkv) * gkv_ref[...]).astype(BF16)
        ckvn_ref[...] = ckvn
        rkv_ref[...] = rkv
        kn = _dot(ckvn, wk_ref[...])
        v_ref[...] = _dot(ckvn, wv_ref[...]).astype(BF16)
        misc = misc_ref[...]
        krope = _rope_fwd(misc, c, sn, sp)
        for h in range(MLA_HEADS):
            sl = slice(h * HEAD_PAD, (h + 1) * HEAD_PAD)
            q_ref[:, sl] = _rope_fwd(qpre[:, sl], c, sn, sp).astype(BF16)
            k_ref[:, sl] = (kn[:, sl] + krope).astype(BF16)
        mb_ref[...] = misc.astype(BF16)
        pre = _dot(mb_ref[...], wg_ref[...]) + bg_ref[...]
        pre_ref[...] = pre
        log_a = (jnp.minimum(pre, 0.0) - jnp.log(1.0 + jnp.exp(-jnp.abs(pre)))) / GLA_GATE_NORM
        la_ref[...] = _dot_exact(_chunk_tri(tm, True), log_a)

    row = lambda w: pl.BlockSpec((tm, w), lambda i: (i, 0))
    full = lambda a: pl.BlockSpec(a.shape, lambda i: (0, 0))
    return pl.pallas_call(
        body, name="mla_prep",
        grid=(t // tm,),
        in_specs=[pl.BlockSpec((tm, 512), lambda i: (i, _rel(P_CQ) // 512)),
                  pl.BlockSpec((tm, MLA_KV_RANK), lambda i: (i, _rel(P_CKV) // MLA_KV_RANK)),
                  pl.BlockSpec((tm, LANE), lambda i: (i, _rel(P_MISC) // LANE)),
                  full(g_q), full(g_kv), full(w_uq_p), full(w_k_p), full(w_v), full(w_gate_p),
                  full(b_gate), row(LANE), row(LANE), row(LANE)],
        out_specs=(row(hq), row(hq), row(MLA_WIDTH), row(GLA_DK), row(GLA_DK),
                   row(MLA_Q_RANK), row(MLA_KV_RANK), row(1), row(1), row(LANE)),
        out_shape=(jax.ShapeDtypeStruct((t, hq), BF16), jax.ShapeDtypeStruct((t, hq), BF16),
                   jax.ShapeDtypeStruct((t, MLA_WIDTH), BF16),
                   jax.ShapeDtypeStruct((t, GLA_DK), F32), jax.ShapeDtypeStruct((t, GLA_DK), F32),
                   jax.ShapeDtypeStruct((t, MLA_Q_RANK), BF16),
                   jax.ShapeDtypeStruct((t, MLA_KV_RANK), BF16),
                   jax.ShapeDtypeStruct((t, 1), F32), jax.ShapeDtypeStruct((t, 1), F32),
                   jax.ShapeDtypeStruct((t, LANE), BF16)),
        compiler_params=_cparams(("arbitrary",)),
    )(proj, proj, proj, g_q, g_kv, w_uq_p, w_k_p, w_v, w_gate_p, b_gate, rc, rsn, rsp)


def _attn_masks(tq, i):
    keys = (i + 1) * tq
    rows = i * tq + lax.broadcasted_iota(jnp.int32, (tq, keys), 0)
    cols = lax.broadcasted_iota(jnp.int32, (tq, keys), 1)
    lane = lax.broadcasted_iota(jnp.int32, (tq, LANE), 1)
    return cols <= rows, lane < MLA_VDIM


def _for_each_query_tile(n_tiles, fn):
    for i in range(n_tiles):
        pl.when(pl.program_id(1) == i)(lambda i=i: fn(i))


def _mla_attn_fwd(q, k, v, shards, sources):
    t = q.shape[0]
    tq = min(256, t)
    scale = MLA_QK ** -0.5
    ns = len(shards)
    g_shapes, g_sems = _gather_plan(shards, sources)
    grid = (MLA_HEADS // 2, t // tq)

    def body(q_ref, k_ref, v_ref, *rest):
        o_ref, lse_ref = rest[ns:ns + 2]
        start, finish = _gather_hooks(rest[:ns], rest[ns + 2:2 * ns + 2], rest[2 * ns + 2:], sources)
        step = pl.program_id(0) * grid[1] + pl.program_id(1)
        pl.when(step == 0)(start)

        def tile(i):
            keys = (i + 1) * tq
            causal, low = _attn_masks(tq, i)
            vp = v_ref[0:keys, :]
            acc = jnp.zeros((tq, LANE), F32)
            for hh in range(2):
                sl = slice(hh * HEAD_PAD, (hh + 1) * HEAD_PAD)
                s = _dot_nt(q_ref[:, sl], k_ref[0:keys, sl]) * scale
                s = jnp.where(causal, s, -jnp.inf)
                m = jnp.max(s, axis=-1, keepdims=True)
                e = jnp.exp(s - m)
                l = jnp.sum(e, axis=-1, keepdims=True)
                o = _dot(e.astype(BF16), vp) / l
                acc = jnp.where(low if hh == 0 else jnp.logical_not(low), o, acc)
                lse_ref[hh] = m + jnp.log(l)
            o_ref[...] = acc

        _for_each_query_tile(t // tq, tile)
        pl.when(step == grid[0] * grid[1] - 1)(finish)

    res = pl.pallas_call(
        body, name="mla_attn_fwd",
        grid=grid,
        in_specs=[pl.BlockSpec((tq, 2 * HEAD_PAD), lambda p, i: (i, p)),
                  pl.BlockSpec((t, 2 * HEAD_PAD), lambda p, i: (0, p)),
                  pl.BlockSpec((t, LANE), lambda p, i: (0, p))] + _hbm_specs(ns),
        out_specs=(pl.BlockSpec((tq, LANE), lambda p, i: (i, p)),
                   pl.BlockSpec((2, tq, 1), lambda p, i: (p, i, 0))) + tuple(_hbm_specs(ns)),
        out_shape=(jax.ShapeDtypeStruct((t, MLA_WIDTH), F32),
                   jax.ShapeDtypeStruct((MLA_HEADS, t, 1), F32)) + tuple(g_shapes),
        scratch_shapes=g_sems,
        compiler_params=_cparams(("arbitrary", "arbitrary")),
    )(q, k, v, *shards)
    return res[0], res[1], res[2:]


def _mla_attn_bwd(q, k, v, o, do, lse, after):
    t = q.shape[0]
    tq = min(256, t)
    scale = MLA_QK ** -0.5

    def body(q_ref, k_ref, v_ref, o_ref, do_ref, lse_ref, after_ref, dq_ref, dk_ref, dv_ref):
        del after_ref

        @pl.when(pl.program_id(1) == 0)
        def _():
            dk_ref[...] = jnp.zeros_like(dk_ref)
            dv_ref[...] = jnp.zeros_like(dv_ref)

        def tile(i):
            keys = (i + 1) * tq
            causal, low = _attn_masks(tq, i)
            vp = v_ref[0:keys, :]
            do_all = do_ref[...]
            o_all = o_ref[...]
            dv_acc = jnp.zeros((keys, LANE), F32)
            for hh in range(2):
                sl = slice(hh * HEAD_PAD, (hh + 1) * HEAD_PAD)
                do_h = jnp.where(low if hh == 0 else jnp.logical_not(low), do_all, 0.0)
                dsum = jnp.sum(do_h * o_all, axis=-1, keepdims=True)
                qh = q_ref[:, sl]
                kh = k_ref[0:keys, sl]
                s = _dot_nt(qh, kh) * scale
                p = jnp.where(causal, jnp.exp(s - lse_ref[hh]), 0.0)
                do_b = do_h.astype(BF16)
                dp = _dot_nt(do_b, vp)
                ds = (p * (dp - dsum) * scale).astype(BF16)
                dq_ref[:, sl] = _dot(ds, kh).astype(BF16)
                dk_ref[0:keys, sl] += _dot_tn(ds, qh)
                dv_acc = dv_acc + _dot_tn(p.astype(BF16), do_b)
            dv_ref[0:keys, :] += dv_acc

        _for_each_query_tile(t // tq, tile)

    return pl.pallas_call(
        body, name="mla_attn_bwd",
        grid=(MLA_HEADS // 2, t // tq),
        in_specs=[pl.BlockSpec((tq, 2 * HEAD_PAD), lambda p, i: (i, p)),
                  pl.BlockSpec((t, 2 * HEAD_PAD), lambda p, i: (0, p)),
                  pl.BlockSpec((t, LANE), lambda p, i: (0, p)),
                  pl.BlockSpec((tq, LANE), lambda p, i: (i, p)),
                  pl.BlockSpec((tq, LANE), lambda p, i: (i, p)),
                  pl.BlockSpec((2, tq, 1), lambda p, i: (p, i, 0)),
                  pl.BlockSpec(memory_space=pl.ANY)],
        out_specs=(pl.BlockSpec((tq, 2 * HEAD_PAD), lambda p, i: (i, p)),
                   pl.BlockSpec((t, 2 * HEAD_PAD), lambda p, i: (0, p)),
                   pl.BlockSpec((t, LANE), lambda p, i: (0, p))),
        out_shape=(jax.ShapeDtypeStruct((t, MLA_HEADS * HEAD_PAD), BF16),
                   jax.ShapeDtypeStruct((t, MLA_HEADS * HEAD_PAD), F32),
                   jax.ShapeDtypeStruct((t, MLA_WIDTH), F32)),
        compiler_params=_cparams(("arbitrary", "arbitrary")),
    )(q, k, v, o, do, lse, after)


def _chunk_tri(n, lower):
    r = lax.broadcasted_iota(jnp.int32, (n, n), 0)
    c = lax.broadcasted_iota(jnp.int32, (n, n), 1)
    same = (r // GLA_CHUNK) == (c // GLA_CHUNK)
    return jnp.where(jnp.logical_and(same, r >= c if lower else r <= c), 1.0, 0.0).astype(F32)


def _gla_chunk_terms(q_ref, k_ref, b_ref, h):
    sl = slice(h * GLA_HK, (h + 1) * GLA_HK)
    b = b_ref[:, sl]
    bl = b[GLA_CHUNK - 1:GLA_CHUNK, :]
    kc = k_ref[:, sl]
    q_in = (q_ref[:, sl] * (GLA_HK ** -0.5)) * jnp.exp(b)
    k_in = kc * jnp.exp(-b)
    k_st = kc * jnp.exp(bl - b)
    return b, bl, q_in, k_in, k_st


def _tri(c, lower):
    r = lax.broadcasted_iota(jnp.int32, (c, c), 0)
    cc = lax.broadcasted_iota(jnp.int32, (c, c), 1)
    return jnp.where(r >= cc if lower else r <= cc, 1.0, 0.0).astype(F32)


def _gla_fwd(proj, log_a):
    t = proj.shape[0]
    c = GLA_CHUNK
    n = t // c

    def body(q_ref, k_ref, v_ref, la_ref, o_ref, sp_ref, st_ref):
        @pl.when(pl.program_id(0) == 0)
        def _():
            st_ref[...] = jnp.zeros_like(st_ref)

        tri = _tri(c, True)
        for h in range(GLA_HEADS):
            _, bl, q_in, k_in, k_st = _gla_chunk_terms(q_ref, k_ref, la_ref, h)
            vs = slice(h * GLA_HV, (h + 1) * GLA_HV)
            vv = v_ref[:, vs].astype(BF16)
            qb = q_in.astype(BF16)
            attn = _dot_nt(qb, k_in.astype(BF16)) * tri
            st = st_ref[h]
            sp_ref[0, h] = st
            o_ref[:, vs] = _dot(attn.astype(BF16), vv) + _dot_nt(qb, st.astype(BF16))
            st_ref[h] = st * jnp.exp(bl) + _dot_tn(vv, k_st.astype(BF16))

    return pl.pallas_call(
        body, name="gla_fwd",
        grid=(n,),
        in_specs=[pl.BlockSpec((c, GLA_DK), lambda i: (i, P_QG // GLA_DK)),
                  pl.BlockSpec((c, GLA_DK), lambda i: (i, P_KG // GLA_DK)),
                  pl.BlockSpec((c, GLA_DV), lambda i: (i, P_VG // GLA_DV)),
                  pl.BlockSpec((c, GLA_DK), lambda i: (i, 0))],
        out_specs=(pl.BlockSpec((c, GLA_DV), lambda i: (i, 0)),
                   pl.BlockSpec((1, GLA_HEADS, GLA_HV, GLA_HK), lambda i: (i, 0, 0, 0))),
        out_shape=(jax.ShapeDtypeStruct((t, GLA_DV), F32),
                   jax.ShapeDtypeStruct((n, GLA_HEADS, GLA_HV, GLA_HK), F32)),
        scratch_shapes=[pltpu.VMEM((GLA_HEADS, GLA_HV, GLA_HK), F32)],
        compiler_params=_cparams(("arbitrary",)),
    )(proj, proj, proj, log_a)


def _gla_bwd(proj, log_a, do, states, after):
    t = proj.shape[0]
    c = GLA_CHUNK
    n = t // c

    def body(q_ref, k_ref, v_ref, la_ref, do_ref, sp_ref, after_ref, dg_ref, dla_ref, ds_ref):
        del after_ref

        @pl.when(pl.program_id(0) == 0)
        def _():
            ds_ref[...] = jnp.zeros_like(ds_ref)

        tri = _tri(c, True)
        last = lax.broadcasted_iota(jnp.int32, (c, GLA_HK), 0) == c - 1
        for h in range(GLA_HEADS):
            b, bl, q_in, k_in, k_st = _gla_chunk_terms(q_ref, k_ref, la_ref, h)
            ks_ = slice(h * GLA_HK, (h + 1) * GLA_HK)
            vs = slice(h * GLA_HV, (h + 1) * GLA_HV)
            vv = v_ref[:, vs].astype(BF16)
            do_h = do_ref[:, vs]
            qb, kb, ksb = q_in.astype(BF16), k_in.astype(BF16), k_st.astype(BF16)
            attn = (_dot_nt(qb, kb) * tri).astype(BF16)
            st = sp_ref[0, h]
            dst = ds_ref[h]
            dstb = dst.astype(BF16)
            dattn = (_dot_nt(do_h, vv) * tri).astype(BF16)
            dg_ref[:, P_VG + h * GLA_HV:P_VG + (h + 1) * GLA_HV] = (
                _dot_tn(attn, do_h) + _dot_nt(ksb, dstb)).astype(BF16)
            dq_in = _dot(dattn, kb) + _dot(do_h, st.astype(BF16))
            dk_in = _dot_tn(dattn, qb)
            dk_st = _dot(vv, dstb)
            ebl = jnp.exp(bl)
            d_ebl = jnp.sum(st * dst, axis=0, keepdims=True)
            ds_ref[h] = _dot_tn(do_h, qb) + dst * ebl
            dg_ref[:, P_QG + h * GLA_HK:P_QG + (h + 1) * GLA_HK] = (
                dq_in * (GLA_HK ** -0.5) * jnp.exp(b)).astype(BF16)
            dg_ref[:, P_KG + h * GLA_HK:P_KG + (h + 1) * GLA_HK] = (
                dk_in * jnp.exp(-b) + dk_st * jnp.exp(bl - b)).astype(BF16)
            db = dq_in * q_in - dk_in * k_in - dk_st * k_st
            dbl = jnp.sum(dk_st * k_st, axis=0, keepdims=True) + d_ebl * ebl
            dla_ref[:, ks_] = db + jnp.where(last, dbl, 0.0)

    rev = lambda i: n - 1 - i
    gw = P_GROUPS[0][1]
    return pl.pallas_call(
        body, name="gla_bwd",
        grid=(n,),
        in_specs=[pl.BlockSpec((c, GLA_DK), lambda i: (rev(i), P_QG // GLA_DK)),
                  pl.BlockSpec((c, GLA_DK), lambda i: (rev(i), P_KG // GLA_DK)),
                  pl.BlockSpec((c, GLA_DV), lambda i: (rev(i), P_VG // GLA_DV)),
                  pl.BlockSpec((c, GLA_DK), lambda i: (rev(i), 0)),
                  pl.BlockSpec((c, GLA_DV), lambda i: (rev(i), 0)),
                  pl.BlockSpec((1, GLA_HEADS, GLA_HV, GLA_HK), lambda i: (rev(i), 0, 0, 0)),
                  pl.BlockSpec(memory_space=pl.ANY)],
        out_specs=(pl.BlockSpec((c, gw), lambda i: (rev(i), 0)),
                   pl.BlockSpec((c, GLA_DK), lambda i: (rev(i), 0))),
        out_shape=(jax.ShapeDtypeStruct((t, gw), BF16), jax.ShapeDtypeStruct((t, GLA_DK), F32)),
        scratch_shapes=[pltpu.VMEM((GLA_HEADS, GLA_HV, GLA_HK), F32)],
        compiler_params=_cparams(("arbitrary",)),
    )(proj, proj, proj, log_a, do, states, after)


def _post(o_mla, proj, o_gla, x, target, g_gla, g_final, w_pm, w_pg, w_o):
    t = x.shape[0]
    tm = min(128, t)
    g0, gw = P_GROUPS[1]

    def body(om_ref, zg_ref, gm_ref, gg_ref, zm_ref, og_ref, x_ref, tg_ref, ggla_ref, gf_ref,
             wpm_ref, wpg_ref, wo_ref,
             dx2_ref, dom_ref, dog_ref, dg_ref,
             mg_ref, um_ref, ug_ref, dym_ref, dyg_ref, loss_ref, dgf_ref, dggla_ref):
        @pl.when(pl.program_id(0) == 0)
        def _():
            loss_ref[...] = jnp.zeros_like(loss_ref)
            dgf_ref[...] = jnp.zeros_like(dgf_ref)
            dggla_ref[...] = jnp.zeros_like(dggla_ref)

        om = om_ref[...]
        zm = zm_ref[...]
        sm = _sigmoid(zm)
        silu_m = zm * sm
        um = (om * silu_m).astype(BF16)
        um_ref[...] = um
        ym = _dot(um, wpm_ref[...])

        ggla = ggla_ref[...]
        zg = zg_ref[...]
        sg = _sigmoid(zg)
        silu_g = zg * sg
        xhat, rstd, on = [], [], []
        for h in range(GLA_HEADS):
            blk = og_ref[:, h * GLA_HV:(h + 1) * GLA_HV]
            r = lax.rsqrt(jnp.mean(blk * blk, axis=-1, keepdims=True) + EPS)
            xhat.append(blk * r)
            rstd.append(r)
            on.append(xhat[h] * ggla)
        on = jnp.concatenate(on, axis=-1)
        ug = (on * silu_g).astype(BF16)
        ug_ref[...] = ug
        yg = _dot(ug, wpg_ref[...])

        sgm = _sigmoid(gm_ref[...])
        sgg = _sigmoid(gg_ref[...])
        merged = (sgm * ym + sgg * yg).astype(BF16)
        mg_ref[...] = merged
        x2 = x_ref[...] + _dot(merged, wo_ref[...])
        gf = gf_ref[...]
        rf = lax.rsqrt(jnp.mean(x2 * x2, axis=-1, keepdims=True) + EPS)
        xh = x2 * rf
        err = xh * gf - tg_ref[...]
        loss_ref[...] += 0.5 * jnp.sum(jnp.mean(err * err, axis=-1, keepdims=True))

        dy = err * (1.0 / D_MODEL)
        dgf_ref[...] += jnp.sum(dy * xh, axis=0, keepdims=True)
        dxh = dy * gf
        dx2 = rf * (dxh - xh * jnp.mean(dxh * xh, axis=-1, keepdims=True))
        dx2_ref[...] = dx2
        dmerged = _dot_nt(dx2.astype(BF16), wo_ref[...])
        dym = (dmerged * sgm).astype(BF16)
        dyg = (dmerged * sgg).astype(BF16)
        dym_ref[...] = dym
        dyg_ref[...] = dyg
        dg_ref[:, P_GMLA - g0:P_GMLA - g0 + D_MODEL] = (dmerged * ym * sgm * (1.0 - sgm)).astype(BF16)
        dg_ref[:, P_GGLA - g0:P_GGLA - g0 + D_MODEL] = (dmerged * yg * sgg * (1.0 - sgg)).astype(BF16)
        dum = _dot_nt(dym, wpm_ref[...])
        dom_ref[...] = dum * silu_m
        dg_ref[:, P_ZMLA - g0:P_ZMLA - g0 + MLA_WIDTH] = (
            dum * om * (sm * (1.0 + zm * (1.0 - sm)))).astype(BF16)
        dug = _dot_nt(dyg, wpg_ref[...])
        dg_ref[:, P_ZGLA - g0:P_ZGLA - g0 + GLA_DV] = (
            dug * on * (sg * (1.0 + zg * (1.0 - sg)))).astype(BF16)
        don = dug * silu_g
        dggla = jnp.zeros((1, GLA_HV), F32)
        for h in range(GLA_HEADS):
            hs = slice(h * GLA_HV, (h + 1) * GLA_HV)
            don_h = don[:, hs]
            dggla = dggla + jnp.sum(don_h * xhat[h], axis=0, keepdims=True)
            dxh_h = don_h * ggla
            dog_ref[:, hs] = (rstd[h] * (dxh_h - xhat[h] * jnp.mean(dxh_h * xhat[h], axis=-1,
                                                                     keepdims=True))).astype(BF16)
        dggla_ref[...] += dggla

    row = lambda w: pl.BlockSpec((tm, w), lambda i: (i, 0))
    pcol = lambda w, off: pl.BlockSpec((tm, w), lambda i: (i, _rel(off) // w))
    full = lambda a: pl.BlockSpec(a.shape, lambda i: (0, 0))
    sds = jax.ShapeDtypeStruct
    return pl.pallas_call(
        body, name="post_fwd_bwd",
        grid=(t // tm,),
        in_specs=[row(MLA_WIDTH), pcol(GLA_DV, P_ZGLA), pcol(D_MODEL, P_GMLA), pcol(D_MODEL, P_GGLA),
                  pcol(MLA_WIDTH, P_ZMLA), row(GLA_DV), row(D_MODEL), row(D_MODEL),
                  full(g_gla), full(g_final), full(w_pm), full(w_pg), full(w_o)],
        out_specs=(row(D_MODEL), row(MLA_WIDTH), row(GLA_DV), row(gw),
                   row(D_MODEL), row(MLA_WIDTH), row(GLA_DV), row(D_MODEL), row(D_MODEL),
                   pl.BlockSpec((1, LANE), lambda i: (0, 0)),
                   pl.BlockSpec((1, D_MODEL), lambda i: (0, 0)),
                   pl.BlockSpec((1, GLA_HV), lambda i: (0, 0))),
        out_shape=(sds((t, D_MODEL), F32), sds((t, MLA_WIDTH), F32), sds((t, GLA_DV), BF16),
                   sds((t, gw), BF16),
                   sds((t, D_MODEL), BF16), sds((t, MLA_WIDTH), BF16), sds((t, GLA_DV), BF16),
                   sds((t, D_MODEL), BF16), sds((t, D_MODEL), BF16),
                   sds((1, LANE), F32), sds((1, D_MODEL), F32), sds((1, GLA_HV), F32)),
        compiler_params=_cparams(("arbitrary",)),
    )(o_mla, proj, proj, proj, proj, o_gla, x, target, g_gla, g_final, w_pm, w_pg, w_o)


def _mla_prep_bwd(dq, dk, dv, dla, pre, proj, rq, rkv, g_q, g_kv, w_uq_p, w_k_p, w_v, w_gate_p,
                  rc, rsn, rsp):
    t = proj.shape[0]
    tm = min(256, t)
    gw = P_GROUPS[2][1]

    def body(dq_ref, dk_ref, dv_ref, dla_ref, pre_ref, cq_ref, ckv_ref, rq_ref, rkv_ref,
             gq_ref, gkv_ref, wuq_ref, wk_ref, wv_ref, wg_ref, c_ref, sn_ref, sp_ref,
             dg_ref, dqpre_ref, dpre_ref, dgq_ref, dgkv_ref, dbg_ref):
        @pl.when(pl.program_id(0) == 0)
        def _():
            dgq_ref[...] = jnp.zeros_like(dgq_ref)
            dgkv_ref[...] = jnp.zeros_like(dgkv_ref)
            dbg_ref[...] = jnp.zeros_like(dbg_ref)

        c, sn, sp = c_ref[...], sn_ref[...], sp_ref[...]
        dkr = jnp.zeros((tm, LANE), F32)
        for h in range(MLA_HEADS):
            sl = slice(h * HEAD_PAD, (h + 1) * HEAD_PAD)
            dqpre_ref[:, sl] = _rope_bwd(dq_ref[:, sl].astype(F32), c, sn, sp).astype(BF16)
            dkr = dkr + dk_ref[:, sl]
        dcqn = _dot_nt(dqpre_ref[...], wuq_ref[...])
        rq = rq_ref[...]
        xh = cq_ref[:, :MLA_Q_RANK] * rq
        dgq_ref[...] += jnp.sum(dcqn * xh, axis=0, keepdims=True)
        dxh = dcqn * gq_ref[...]
        dcq = rq * (dxh - xh * jnp.mean(dxh * xh, axis=-1, keepdims=True))
        dg_ref[:, :MLA_Q_RANK] = dcq.astype(BF16)
        dg_ref[:, MLA_Q_RANK:512] = jnp.zeros((tm, 512 - MLA_Q_RANK), BF16)

        dckvn = _dot_nt(dk_ref[...].astype(BF16), wk_ref[...]) + \
            _dot_nt(dv_ref[...].astype(BF16), wv_ref[...])
        rkv = rkv_ref[...]
        xh = ckv_ref[...] * rkv
        dgkv_ref[...] += jnp.sum(dckvn * xh, axis=0, keepdims=True)
        dxh = dckvn * gkv_ref[...]
        dg_ref[:, P_CKV - P_CQ:P_CKV - P_CQ + MLA_KV_RANK] = (
            rkv * (dxh - xh * jnp.mean(dxh * xh, axis=-1, keepdims=True))).astype(BF16)

        dlog_a = _dot_exact(_chunk_tri(tm, False), dla_ref[...])
        dpre = dlog_a * (1.0 / GLA_GATE_NORM) * (1.0 - _sigmoid(pre_ref[...]))
        dbg_ref[...] += jnp.sum(dpre, axis=0, keepdims=True)
        dpre = dpre.astype(BF16)
        dpre_ref[...] = dpre
        lane = lax.broadcasted_iota(jnp.int32, (tm, LANE), 1)
        in_kr = jnp.logical_and(lane >= MISC_KR, lane < MISC_KR + MLA_ROPE)
        dmisc = jnp.where(in_kr, _rope_bwd(dkr, c, sn, sp), 0.0) + _dot_nt(dpre, wg_ref[...])
        dg_ref[:, P_MISC - P_CQ:P_MISC - P_CQ + LANE] = dmisc.astype(BF16)

    hq = MLA_HEADS * HEAD_PAD
    row = lambda w: pl.BlockSpec((tm, w), lambda i: (i, 0))
    full = lambda a: pl.BlockSpec(a.shape, lambda i: (0, 0))
    acc = lambda w: pl.BlockSpec((1, w), lambda i: (0, 0))
    sds = jax.ShapeDtypeStruct
    return pl.pallas_call(
        body, name="mla_prep_bwd",
        grid=(t // tm,),
        in_specs=[row(hq), row(hq), row(MLA_WIDTH), row(GLA_DK), row(GLA_DK),
                  pl.BlockSpec((tm, 512), lambda i: (i, _rel(P_CQ) // 512)),
                  pl.BlockSpec((tm, MLA_KV_RANK), lambda i: (i, _rel(P_CKV) // MLA_KV_RANK)),
                  row(1), row(1), full(g_q), full(g_kv), full(w_uq_p), full(w_k_p), full(w_v),
                  full(w_gate_p), row(LANE), row(LANE), row(LANE)],
        out_specs=(row(gw), row(hq), row(GLA_DK),
                   acc(MLA_Q_RANK), acc(MLA_KV_RANK), acc(GLA_DK)),
        out_shape=(sds((t, gw), BF16), sds((t, hq), BF16), sds((t, GLA_DK), BF16),
                   sds((1, MLA_Q_RANK), F32), sds((1, MLA_KV_RANK), F32), sds((1, GLA_DK), F32)),
        compiler_params=_cparams(("arbitrary",)),
    )(dq, dk, dv, dla, pre, proj, proj, rq, rkv, g_q, g_kv, w_uq_p, w_k_p, w_v, w_gate_p,
      rc, rsn, rsp)


def _inproj_bwd(dgroups, w_pts, x, rstd, g_in, dx2, after):
    t = x.shape[0]
    tm = min(256, t)

    def body(d0_ref, d1_ref, d2_ref, w0_ref, w1_ref, w2_ref, x_ref, r_ref, g_ref, dx2_ref, after_ref,
             dx_ref, dg_ref):
        del after_ref

        @pl.when(pl.program_id(0) == 0)
        def _():
            dg_ref[...] = jnp.zeros_like(dg_ref)

        dh = jnp.zeros((tm, D_MODEL), F32)
        for d_ref, w_ref in zip((d0_ref, d1_ref, d2_ref), (w0_ref, w1_ref, w2_ref)):
            dh = dh + _dot(d_ref[...], w_ref[...])
        r = r_ref[...]
        xh = x_ref[...] * r
        dg_ref[...] += jnp.sum(dh * xh, axis=0, keepdims=True)
        dxh = dh * g_ref[...]
        dx_ref[...] = dx2_ref[...] + r * (dxh - xh * jnp.mean(dxh * xh, axis=-1, keepdims=True))

    row = lambda w: pl.BlockSpec((tm, w), lambda i: (i, 0))
    return pl.pallas_call(
        body, name="inproj_bwd",
        grid=(t // tm,),
        in_specs=[row(w) for _, w in P_GROUPS]
        + [pl.BlockSpec((w, D_MODEL), lambda i: (0, 0)) for _, w in P_GROUPS]
        + [row(D_MODEL), row(1), pl.BlockSpec((1, D_MODEL), lambda i: (0, 0)), row(D_MODEL),
           pl.BlockSpec(memory_space=pl.ANY)],
        out_specs=(row(D_MODEL), pl.BlockSpec((1, D_MODEL), lambda i: (0, 0))),
        out_shape=(jax.ShapeDtypeStruct((t, D_MODEL), F32),
                   jax.ShapeDtypeStruct((1, D_MODEL), F32)),
        compiler_params=_cparams(("arbitrary",)),
    )(*dgroups, *w_pts, x, rstd, g_in, dx2, after)


def _matmul(name, a, b, tm, tn, dtype=F32):
    kk, m = a.shape
    n = b.shape[1]

    def body(a_ref, b_ref, o_ref):
        o_ref[...] = _dot_tn(a_ref[...].astype(BF16), b_ref[...].astype(BF16)).astype(dtype)

    return pl.pallas_call(
        body, name=name,
        grid=(n // tn, m // tm),
        in_specs=[pl.BlockSpec((kk, tm), lambda j, i: (0, i)),
                  pl.BlockSpec((kk, tn), lambda j, i: (0, j))],
        out_specs=pl.BlockSpec((tm, tn), lambda j, i: (i, j)),
        out_shape=jax.ShapeDtypeStruct((m, n), dtype),
        compiler_params=_cparams(("arbitrary", "arbitrary")),
    )(a, b)


def _adamw_update(part_refs, w_ref, m_ref, v_ref, g_ref, d_ref, nm_ref, nv_ref):
    g = part_refs[0][...].astype(F32)
    for p_ref in part_refs[1:]:
        g = g + p_ref[...].astype(F32)
    m_new = ADAM_B1 * m_ref[...] + (1.0 - ADAM_B1) * g
    v_new = ADAM_B2 * v_ref[...] + (1.0 - ADAM_B2) * (g * g)
    m_hat = m_new / (1.0 - ADAM_B1 ** ADAM_STEP)
    v_hat = v_new / (1.0 - ADAM_B2 ** ADAM_STEP)
    g_ref[...] = g
    nm_ref[...] = m_new
    nv_ref[...] = v_new
    d_ref[...] = -ADAM_LR * (m_hat / (jnp.sqrt(v_hat) + ADAM_EPS) + ADAM_WD * w_ref[...])


def _adamw_rows(name, parts, w, m, v, tr, first=None):
    _, rows, cols = w.shape
    slots = parts.shape[0]

    def body(*refs):
        lead_refs, p_ref = ([], refs[0]) if first is None else ([refs[0]], refs[1])
        _adamw_update(lead_refs + [p_ref.at[q] for q in range(slots)], *refs[len(lead_refs) + 1:])

    blk = pl.BlockSpec((None, tr, cols), lambda i: (0, i, 0))
    out = jax.ShapeDtypeStruct((1, rows, cols), F32)
    lead = [] if first is None else [pl.BlockSpec((tr, cols), lambda i: (i, 0))]
    return pl.pallas_call(
        body, name=name,
        grid=(rows // tr,),
        in_specs=lead + [pl.BlockSpec((slots, tr, cols), lambda i: (0, i, 0)), blk, blk, blk],
        out_specs=(blk, blk, blk, blk),
        out_shape=(out, out, out, out),
        compiler_params=_cparams(("arbitrary",)),
    )(*([] if first is None else [first]), parts, w, m, v)


def _adamw_transposed(name, first, parts, w, m, v, tl):
    _, rows, cols = w.shape
    slots, padded = parts.shape[:2]

    def body(f_ref, p_ref, *refs):
        _adamw_update([f_ref.at[pl.ds(0, cols)]]
                      + [p_ref.at[q, pl.ds(0, cols)] for q in range(slots)], *refs)

    blk = pl.BlockSpec((cols, None, tl), lambda i: (0, 0, i))
    out = jax.ShapeDtypeStruct((cols, 1, rows), F32)
    res = pl.pallas_call(
        body, name=name,
        grid=(rows // tl,),
        in_specs=[pl.BlockSpec((padded, tl), lambda i: (0, i)),
                  pl.BlockSpec((slots, padded, tl), lambda i: (0, 0, i)), blk, blk, blk],
        out_specs=(blk, blk, blk, blk),
        out_shape=(out, out, out, out),
        compiler_params=_cparams(("arbitrary",)),
    )(first, parts, *[a.transpose(2, 0, 1) for a in (w, m, v)])
    return [r.transpose(1, 2, 0) for r in res]


def _adamw_group(firsts, parts, ws, ms, vs):
    n = len(ws)

    def body(*refs):
        ins, outs = refs[:5 * n], refs[5 * n:]
        x, y, c = _mesh_pos()
        for a in range(n):
            _adamw_update([ins[a].at[4 * x + 2 * y + c]]
                          + [ins[n + a].at[q] for q in range(ins[n + a].shape[0])],
                          *[r.at[0] for r in (ins[2 * n + a], ins[3 * n + a], ins[4 * n + a])],
                          *[r.at[0] for r in outs[4 * a:4 * a + 4]])

    vmem = lambda k: [pl.BlockSpec(memory_space=pltpu.VMEM) for _ in range(k)]
    out_shape = []
    for w in ws:
        out_shape += [jax.ShapeDtypeStruct(w.shape, F32)] * 4
    res = pl.pallas_call(
        body, name="adamw_small_weights",
        in_specs=vmem(5 * n), out_specs=tuple(vmem(4 * n)), out_shape=tuple(out_shape),
        compiler_params=_cparams(),
    )(*firsts, *parts, *ws, *ms, *vs)
    return [res[4 * a:4 * a + 4] for a in range(n)]


def _rope_tables(positions):
    half = MLA_ROPE // 2
    freqs = ROPE_THETA ** (-jnp.arange(half, dtype=F32) / half)
    ang = positions.astype(F32).reshape(-1, 1) * freqs
    cos, sin = jnp.cos(ang), jnp.sin(ang)
    t = ang.shape[0]
    one, zero = jnp.ones((t, MLA_NOPE), F32), jnp.zeros((t, half), F32)
    tail = jnp.zeros((t, LANE - MLA_QK), F32)
    rc = jnp.concatenate([one, cos, cos, tail], axis=1)
    rsn = jnp.concatenate([0.0 * one, -sin, zero, tail], axis=1)
    rsp = jnp.concatenate([0.0 * one, zero, sin, tail], axis=1)
    return rc, rsn, rsp


def _cols_full(g):
    return g.transpose(1, 0, 2)


def kernel(x, positions, g_in, w_in, g_q, w_uq, g_kv, w_ukv, w_gla_gate, b_gla_gate, g_gla, w_proj_mla, w_proj_gla, w_out, g_final, loss_target, m_g_in, m_w_in, m_g_q, m_w_uq, m_g_kv, m_w_ukv, m_w_gla_gate, m_b_gla_gate, m_g_gla, m_w_proj_mla, m_w_proj_gla, m_w_out, m_g_final, v_g_in, v_w_in, v_g_q, v_w_uq, v_g_kv, v_w_ukv, v_w_gla_gate, v_b_gla_gate, v_g_gla, v_w_proj_mla, v_w_proj_gla, v_w_out, v_g_final):
    t = x.shape[1]
    x2d = x.reshape(t, D_MODEL)
    tgt = loss_target.reshape(t, D_MODEL)
    g_final2 = g_final.reshape(1, D_MODEL)
    sharded = [(w_in, m_w_in, v_w_in), (w_uq, m_w_uq, v_w_uq), (w_ukv, m_w_ukv, v_w_ukv),
               (w_gla_gate, m_w_gla_gate, v_w_gla_gate), (w_proj_mla, m_w_proj_mla, v_w_proj_mla),
               (w_proj_gla, m_w_proj_gla, v_w_proj_gla), (w_out, m_w_out, v_w_out)]

    w_in_t = w_in.transpose(2, 0, 1).reshape(SHARD_COLS, D_MODEL)
    everyone = tuple(range(N_DEV))
    w_in_b = w_in_t.astype(BF16)
    b_uq, b_ukv, b_gate, b_pm, b_pg, b_o = [s[0][0].astype(BF16) for s in sharded[1:]]
    stages = ((0, 2, 4, 6), (1, 3, 5, 7))
    where = {d: (k, i) for k, srcs in enumerate(stages) for i, d in enumerate(srcs)}
    g_in_1, g_uq, g_ukv, g_gate = _all_gather(
        "all_gather_first", [w_in_b, b_uq, b_ukv, b_gate], [stages[0]] + [everyone] * 3)
    w_uq_p = jnp.pad(_cols_full(g_uq), ((0, 0), (0, 0), (0, HEAD_PAD - MLA_QK))).reshape(
        MLA_Q_RANK, MLA_HEADS * HEAD_PAD)
    ukv = _cols_full(g_ukv)
    w_k_p = jnp.pad(ukv[:, :, :MLA_NOPE], ((0, 0), (0, 0), (0, HEAD_PAD - MLA_NOPE))).reshape(
        MLA_KV_RANK, MLA_HEADS * HEAD_PAD)
    w_v = ukv[:, :, MLA_NOPE:].reshape(MLA_KV_RANK, MLA_WIDTH)
    w_gate_p = jnp.pad(_cols_full(g_gate).reshape(GLA_GATE_RANK, GLA_DK),
                       ((MISC_ALR, LANE - MISC_ALR - GLA_GATE_RANK), (0, 0)))
    rc, rsn, rsp = _rope_tables(positions)

    w_lat = _weights_to_p("weights_latents", [g_in_1], where, 2)
    proj_lat, h, rstd = _inproj(x2d, g_in, w_lat)
    q, k, v, log_a, pre, cqn, ckvn, rq, rkv, misc = _mla_prep(
        proj_lat, g_q, g_kv, w_uq_p, w_k_p, w_v, w_gate_p, b_gla_gate, rc, rsn, rsp)
    o_mla, lse, (g_in_2, g_pm, g_pg, g_o) = _mla_attn_fwd(
        q, k, v, [w_in_b, b_pm, b_pg, b_o], [stages[1]] + [everyone] * 3)
    w_gla = _weights_to_p("weights_gla", [g_in_1, g_in_2], where, 0)
    proj_gla = _proj("inproj_gla", h, w_gla)
    o_gla, states = _gla_fwd(proj_gla, log_a)
    w_out_path = _weights_to_p("weights_out_path", [g_in_1, g_in_2], where, 1)
    proj_out = _proj("inproj_out_path", h, w_out_path)
    w_in_p = (w_gla, w_out_path, w_lat)
    w_pm = _cols_full(g_pm).reshape(MLA_WIDTH, D_MODEL)
    w_pg = g_pg.reshape(GLA_DV, D_MODEL)
    w_o = g_o.reshape(D_MODEL, D_MODEL)

    (dx2, do_mla, do_gla, d_out, merged, um, ug, dym, dyg, loss_p, dg_final,
     dg_gla) = _post(o_mla, proj_out, o_gla, x2d, tgt, g_gla, g_final2, w_pm, w_pg, w_o)

    p_pm = _matmul("dw_proj_mla", um, dym, 512, 512, BF16).reshape(
        MLA_WIDTH, N_DEV, D_MODEL // N_DEV).transpose(1, 0, 2)
    p_pg = _matmul("dw_proj_gla", ug, dyg, 512, 512, BF16).reshape(N_DEV, -1, D_MODEL)
    p_o = _matmul("dw_out", merged, dx2, 512, 512, BF16).reshape(N_DEV, -1, D_MODEL)
    own_in = jnp.zeros((SHARD_PAD, D_MODEL), BF16)
    land_in = lax.empty((PEERS, SHARD_PAD, D_MODEL), BF16)
    dw_groups, started, lands = {}, [], [land_in]

    def reduce_scatter_stage(s, dests, own_in, extra=()):
        parts_in, own_in = _grads_to_shards("grads_to_shards_%d" % s, dw_groups, dests, own_in)
        first = len(lands)
        lands.extend(lax.empty((PEERS,) + p.shape[1:], BF16) for p in extra)
        idx = [0] + list(range(first, len(lands)))
        all_dests = [[(i, d, r0, r1) for i, (d, ranges) in enumerate(dests) for r0, r1 in ranges]]
        all_dests += [_whole(everyone, p.shape[1]) for p in extra]
        sems, parts, new_lands, token = _ici_start(
            "ici_start_%d" % s, [parts_in] + list(extra), [lands[i] for i in idx], all_dests)
        for a, i in enumerate(idx):
            lands[i] = new_lands[a]
            started.append((sems[a][0], sems[a][1], parts[a], i, all_dests[a]))
        return own_in, token

    dw_groups[1] = _matmul("dw_in_1", d_out, h, 512, 512, BF16)
    full = [(0, SHARD_PAD)]
    own_in, token = reduce_scatter_stage(
        1, [(5, full), (6, full), (7, full)], own_in, (p_pm, p_pg, p_o))
    d_gla, dla = _gla_bwd(proj_gla, log_a, do_gla, states, token)
    dw_groups[0] = _matmul("dw_in_0", d_gla, h, 512, 512, BF16)
    own_in, token = reduce_scatter_stage(
        2, [(1, full), (2, full), (3, full), (4, [(0, 64), (96, SHARD_PAD)]),
            (0, [(672, SHARD_PAD)])], own_in)
    dq, dk, dv = _mla_attn_bwd(q, k, v, o_mla, do_mla, lse, token)
    d_lat, dqpre, dpre, dg_q, dg_kv, db_gate = _mla_prep_bwd(
        dq, dk, dv, dla, pre, proj_lat, rq, rkv, g_q, g_kv, w_uq_p, w_k_p, w_v, w_gate_p, rc, rsn, rsp)
    dw_groups[2] = _matmul("dw_in_2", d_lat, h, 896, 512, BF16)
    dw_uq = _matmul("dw_uq", cqn, dqpre, MLA_Q_RANK, 512, BF16)
    p_uq = dw_uq.reshape(MLA_Q_RANK, MLA_HEADS, HEAD_PAD)[:, :, :MLA_QK].transpose(1, 0, 2)
    dw_k = _matmul("dw_uk", ckvn, dk, MLA_KV_RANK, 512, BF16)
    dw_v = _matmul("dw_uv", ckvn, dv, MLA_KV_RANK, 512, BF16)
    p_ukv = jnp.concatenate(
        [dw_k.reshape(MLA_KV_RANK, MLA_HEADS, HEAD_PAD)[:, :, :MLA_NOPE],
         dw_v.reshape(MLA_KV_RANK, MLA_HEADS, MLA_VDIM)], axis=2).transpose(1, 0, 2)
    dw_gate = _matmul("dw_gate", misc, dpre, LANE, 512, BF16)
    p_gate = dw_gate[MISC_ALR:MISC_ALR + GLA_GATE_RANK].reshape(
        GLA_GATE_RANK, N_DEV, GLA_DK // N_DEV).transpose(1, 0, 2)
    own_in, token = reduce_scatter_stage(
        3, [(0, [(0, 672)]), (4, [(64, 96)])], own_in, (p_uq, p_ukv, p_gate))
    grad_x, dg_in = _inproj_bwd((d_gla, d_out, d_lat), w_in_p, x2d, rstd, g_in, dx2, token)
    small = jnp.concatenate([dg_in.reshape(-1), dg_q.reshape(-1), dg_kv.reshape(-1),
                             db_gate.reshape(-1), dg_gla.reshape(-1), dg_final.reshape(-1),
                             loss_p[0, :1]])
    small = jnp.pad(small, (0, SMALL_ROWS * LANE - small.shape[0])).reshape(SMALL_ROWS, LANE)

    (small_all,) = _all_gather("all_gather_small", [small], [everyone])
    lands = _ici_wait("ici_wait", started, lands, small_all)
    big = [_adamw_transposed("adamw_w_in", own_in, lands[0], *sharded[0], 256)]
    big += _adamw_group([p_uq, p_ukv, p_gate, p_pm, p_pg, p_o], list(lands[4:7]) + list(lands[1:4]),
                        *[[s[j] for s in sharded[1:]] for j in range(3)])
    replicated = [(g_in, m_g_in, v_g_in), (g_q, m_g_q, v_g_q), (g_kv, m_g_kv, v_g_kv),
                  (b_gla_gate, m_b_gla_gate, v_b_gla_gate), (g_gla, m_g_gla, v_g_gla),
                  (g_final, m_g_final, v_g_final)]
    spacks = [jnp.pad(jnp.concatenate([s[j].reshape(-1) for s in replicated]),
                      (0, SMALL_ROWS * LANE - sum(SMALL_SIZES))).reshape(1, SMALL_ROWS, LANE)
              for j in range(3)]
    tiny = _adamw_rows("adamw_gains", small_all, spacks[0], spacks[1], spacks[2], SMALL_ROWS)

    outs = {}
    names = ("w_in", "w_uq", "w_ukv", "w_gla_gate", "w_proj_mla", "w_proj_gla", "w_out")
    for j, kind in enumerate(("grad", "delta", "new_m", "new_v")):
        for name, res in zip(names, big):
            outs[kind, name] = res[j]
        flat = tiny[j].reshape(-1)
        off = 0
        for name, size in zip(("g_in", "g_q", "g_kv", "b_gla_gate", "g_gla", "g_final"), SMALL_SIZES):
            shape = (size,) if name == "g_final" else (1, size)
            outs[kind, name] = flat[off:off + size].reshape(shape)
            off += size
    loss = tiny[0].reshape(-1)[sum(SMALL_SIZES)]
    order = ("g_in", "w_in", "g_q", "w_uq", "g_kv", "w_ukv", "w_gla_gate", "b_gla_gate", "g_gla",
             "w_proj_mla", "w_proj_gla", "w_out", "g_final")
    result = [loss, grad_x.reshape(1, t, D_MODEL)]
    for kind in ("grad", "delta", "new_m", "new_v"):
        result += [outs[kind, name] for name in order]
    return tuple(result)
```

```python
import jax
import jax.numpy as jnp
from jax import lax
from jax.experimental import pallas as pl
from jax.experimental.pallas import tpu as pltpu

F32 = jnp.float32
BF16 = jnp.bfloat16
MESH = pl.DeviceIdType.MESH
N_DEV = 8

D_MODEL = 1024
EPS = 1e-6
MLA_HEADS = 8
MLA_NOPE = 64
MLA_ROPE = 32
MLA_VDIM = 64
MLA_Q_RANK = 384
MLA_KV_RANK = 256
MLA_QK = MLA_NOPE + MLA_ROPE
MLA_WIDTH = MLA_HEADS * MLA_VDIM
ROPE_THETA = 10000.0
GLA_HEADS = 4
GLA_DK = 512
GLA_DV = 1024
GLA_HK = 128
GLA_HV = 256
GLA_GATE_RANK = 16
GLA_GATE_NORM = 16.0
GLA_CHUNK = 64
D_IN = 6320

ADAM_LR = 0.001
ADAM_B1 = 0.9
ADAM_B2 = 0.999
ADAM_EPS = 1e-08
ADAM_WD = 0.01
ADAM_STEP = 10

LANE = 128
HEAD_PAD = 128
VMEM_LIMIT = 48 * 1024 * 1024

P_VG, P_QG, P_KG = 0, 1024, 1536
P_ZGLA, P_GMLA, P_GGLA, P_ZMLA = 2048, 3072, 4096, 5120
P_CQ, P_CKV, P_MISC = 5632, 6144, 6400
P_TOTAL = 6528
P_GROUPS = ((0, 2048), (2048, 3584), (5632, 896))
MISC_KR = 64
MISC_ALR = 96
SHARD_COLS = D_IN // N_DEV
SHARD_PAD = 800
P_COMPONENTS = ((0, 384, P_CQ), (384, 256, P_CKV), (640, 32, P_MISC + MISC_KR), (672, 512, P_ZMLA),
                (1184, 512, P_QG), (1696, 512, P_KG), (2208, 1024, P_VG),
                (3232, 16, P_MISC + MISC_ALR), (3248, 1024, P_ZGLA), (4272, 1024, P_GMLA),
                (5296, 1024, P_GGLA))

SMALL_SIZES = (1024, 384, 256, 512, 256, 1024)
SMALL_ROWS = 32


def _segments():
    segs = []
    for g0, n, p0 in P_COMPONENTS:
        g = g0
        while g < g0 + n:
            d = g // SHARD_COLS
            end = min(g0 + n, (d + 1) * SHARD_COLS)
            segs.append((d, g - d * SHARD_COLS, end - g, p0 + g - g0))
            g = end
    return segs


def _group_of(p0):
    return max(i for i, (off, _) in enumerate(P_GROUPS) if off <= p0)


def _rel(p0):
    return p0 - P_GROUPS[_group_of(p0)][0]


def _cparams(sem=None):
    if sem is None:
        return pltpu.CompilerParams(vmem_limit_bytes=VMEM_LIMIT)
    return pltpu.CompilerParams(dimension_semantics=sem, vmem_limit_bytes=VMEM_LIMIT)


def _sigmoid(v):
    return 1.0 / (1.0 + jnp.exp(-v))


def _dot(a, b):
    return jnp.dot(a, b, preferred_element_type=F32)


def _dot_nt(a, b):
    return lax.dot_general(a, b, (((1,), (1,)), ((), ())), preferred_element_type=F32)


def _dot_tn(a, b):
    return lax.dot_general(a, b, (((0,), (0,)), ((), ())), preferred_element_type=F32)


def _dot_exact(a, b):
    return jnp.dot(a, b, preferred_element_type=F32, precision=lax.Precision.HIGHEST)


def _rope_fwd(blk, c, sn, sp):
    return blk * c + pltpu.roll(blk, LANE - 16, 1) * sn + pltpu.roll(blk, 16, 1) * sp


def _rope_bwd(blk, c, sn, sp):
    return blk * c + pltpu.roll(blk * sn, 16, 1) + pltpu.roll(blk * sp, LANE - 16, 1)


def _mesh_pos():
    return lax.axis_index("x"), lax.axis_index("y"), lax.axis_index("c")


def _hbm_specs(n):
    return [pl.BlockSpec(memory_space=pltpu.HBM) for _ in range(n)]


def _dev(d):
    return d >> 2, (d >> 1) & 1, d & 1


def _gather_plan(shards, sources):
    na, most = len(shards), max(len(s) for s in sources)
    out_shape = [jax.ShapeDtypeStruct((len(srcs),) + s.shape, s.dtype)
                 for s, srcs in zip(shards, sources)]
    sems = [pltpu.SemaphoreType.DMA((na, most)) for _ in range(3)]
    sems += [pltpu.SemaphoreType.DMA((na, most, 3))]
    sems += [pltpu.SemaphoreType.DMA((na, most)) for _ in range(3)]
    return out_shape, sems


def _gather_hooks(x_refs, out_refs, sems, sources):
    local_sems, d2d_send, d2d_recv, ici_send, ici_recv, fwd_send, fwd_recv = sems
    x, y, c = _mesh_pos()
    chips = [(1 - x, y), (x, 1 - y), (1 - x, 1 - y)]
    items = []
    for a, srcs in enumerate(sources):
        for i, d in enumerate(srcs):
            dx, dy, dc = _dev(d)
            near = jnp.logical_and(x == dx, y == dy)
            far = jnp.logical_not(near)
            slot = out_refs[a].at[i]

            def remote(src, to, send_sem, recv_sem, slot=slot):
                return pltpu.make_async_remote_copy(
                    src_ref=src, dst_ref=slot, send_sem=send_sem, recv_sem=recv_sem,
                    device_id=to, device_id_type=MESH)

            items.append(dict(
                me=jnp.logical_and(near, c == dc), sibling=jnp.logical_and(near, c != dc),
                relay=jnp.logical_and(far, c == dc), behind=jnp.logical_and(far, c != dc),
                local=pltpu.make_async_copy(x_refs[a], slot, local_sems.at[a, i]),
                to_sibling=remote(x_refs[a], (x, y, 1 - c), d2d_send.at[a, i], d2d_recv.at[a, i]),
                to_chips=[remote(x_refs[a], (*chip, c), ici_send.at[a, i, j], ici_recv.at[a, i])
                          for j, chip in enumerate(chips)],
                forward=remote(slot, (x, y, 1 - c), fwd_send.at[a, i], fwd_recv.at[a, i])))

    def start():
        for it in items:
            @pl.when(it["me"])
            def _(it=it):
                it["local"].start()
                it["to_sibling"].start()
                for cp in it["to_chips"]:
                    cp.start()

    def finish():
        for it in items:
            @pl.when(it["relay"])
            def _(it=it):
                it["to_chips"][0].wait_recv()
                it["forward"].start()
        for it in items:
            pl.when(it["sibling"])(it["to_sibling"].wait_recv)
            pl.when(it["behind"])(it["forward"].wait_recv)
            pl.when(it["relay"])(it["forward"].wait_send)

            @pl.when(it["me"])
            def _(it=it):
                it["local"].wait()
                it["to_sibling"].wait_send()
                for cp in it["to_chips"]:
                    cp.wait_send()

    return start, finish


def _all_gather(name, shards, sources):
    n = len(shards)
    out_shape, sems = _gather_plan(shards, sources)

    def body(*refs):
        start, finish = _gather_hooks(refs[:n], refs[n:2 * n], refs[2 * n:], sources)
        start()
        finish()

    return pl.pallas_call(
        body, name=name,
        out_shape=tuple(out_shape),
        in_specs=_hbm_specs(n), out_specs=tuple(_hbm_specs(n)),
        scratch_shapes=sems,
        compiler_params=_cparams(),
    )(*shards)


PEERS = N_DEV - 1


def _whole(dests, rows):
    return [(i, d, 0, rows) for i, d in enumerate(dests)]


def _ici_copies(p_ref, land_ref, send_sems, recv_sems, pieces):
    x, y, c = _mesh_pos()
    sends, arrivals = [], []

    def rows_of(ref, j, r0, r1):
        return ref.at[j] if (r0, r1) == (0, ref.shape[1]) else ref.at[j, pl.ds(r0, r1 - r0)]

    for p, (i, d, r0, r1) in enumerate(pieces):
        dx, dy, dc = _dev(d)
        k = (4 * (x != dx).astype(jnp.int32) + 2 * (y != dy).astype(jnp.int32)
             + (c != dc).astype(jnp.int32))
        slot = jnp.maximum(k - 1, 0)
        sends.append((k > 0, pltpu.make_async_remote_copy(
            src_ref=rows_of(p_ref, i, r0, r1), dst_ref=rows_of(land_ref, slot, r0, r1),
            send_sem=send_sems.at[p], recv_sem=recv_sems.at[p * PEERS + slot],
            device_id=(dx, dy, dc), device_id_type=MESH)))
        arrivals.append((k == 0, [pltpu.make_async_remote_copy(
            src_ref=rows_of(p_ref, i, r0, r1), dst_ref=rows_of(land_ref, r, r0, r1),
            send_sem=send_sems.at[p], recv_sem=recv_sems.at[p * PEERS + r],
            device_id=(dx, dy, dc), device_id_type=MESH) for r in range(PEERS)]))
    return sends, arrivals


def _ici_start(name, hs, lands, dests):
    na = len(hs)

    def body(*refs):
        h_refs, land_refs, sems = refs[:na], refs[na:2 * na], refs[2 * na:4 * na]
        token = refs[-1]
        for a in range(na):
            sends, _ = _ici_copies(h_refs[a], land_refs[a], sems[2 * a], sems[2 * a + 1], dests[a])
            for go, cp in sends:
                pl.when(go)(cp.start)
        token[...] = jnp.zeros_like(token)

    hbm, sem = pl.BlockSpec(memory_space=pltpu.HBM), pl.BlockSpec(memory_space=pltpu.SEMAPHORE)
    sem_shapes = []
    for a in range(na):
        sem_shapes += [pltpu.SemaphoreType.DMA((len(dests[a]),)),
                       pltpu.SemaphoreType.DMA((len(dests[a]) * PEERS,))]
    res = pl.pallas_call(
        body, name=name,
        out_shape=tuple(sem_shapes) + tuple(pltpu.HBM(v.shape, v.dtype) for v in list(hs) + list(lands))
        + (jax.ShapeDtypeStruct((8, LANE), F32),),
        in_specs=(hbm,) * (2 * na),
        out_specs=(sem,) * (2 * na) + (hbm,) * (2 * na) + (pl.BlockSpec(memory_space=pltpu.VMEM),),
        input_output_aliases={i: 2 * na + i for i in range(2 * na)},
        compiler_params=pltpu.CompilerParams(
            has_side_effects=pltpu.SideEffectType.DATAFLOW_SIDE_EFFECTING,
            vmem_limit_bytes=VMEM_LIMIT),
    )(*[pltpu.with_memory_space_constraint(v, pltpu.HBM) for v in list(hs) + list(lands)])
    sems = [(res[2 * a], res[2 * a + 1]) for a in range(na)]
    return sems, res[2 * na:3 * na], res[3 * na:4 * na], res[-1]


def _ici_wait(name, started, lands, after):
    k, nl = len(started), len(lands)

    def body(*refs):
        land_refs = refs[3 * k:3 * k + nl]
        for s in range(k):
            h_ref, send_sems, recv_sems = refs[3 * s:3 * s + 3]
            sends, arrivals = _ici_copies(h_ref, land_refs[started[s][3]], send_sems, recv_sems,
                                          started[s][4])
            for go, cp in sends:
                pl.when(go)(cp.wait_send)
            for here, cps in arrivals:
                for cp in cps:
                    pl.when(here)(cp.wait_recv)

    hbm, sem = pl.BlockSpec(memory_space=pltpu.HBM), pl.BlockSpec(memory_space=pltpu.SEMAPHORE)
    operands, specs = [], []
    for send_sems, recv_sems, h, _, _ in started:
        operands += [h, send_sems, recv_sems]
        specs += [hbm, sem, sem]
    return pl.pallas_call(
        body, name=name,
        out_shape=tuple(pltpu.HBM(v.shape, v.dtype) for v in lands),
        in_specs=tuple(specs) + (hbm,) * nl + (pl.BlockSpec(memory_space=pl.ANY),),
        out_specs=(hbm,) * nl,
        input_output_aliases={3 * k + i: i for i in range(nl)},
        compiler_params=pltpu.CompilerParams(
            has_side_effects=pltpu.SideEffectType.DATAFLOW_SIDE_EFFECTING,
            vmem_limit_bytes=VMEM_LIMIT),
    )(*operands, *lands, after)


def _weights_to_p(name, gathered, where, group):
    tl = 256
    off, width = P_GROUPS[group]
    segs = sorted([s for s in _segments() if _group_of(s[3]) == group], key=lambda s: s[3])
    used = sorted({where[s[0]][0] for s in segs})

    def body(*refs):
        g_refs, o_ref = dict(zip(used, refs[:-1])), refs[-1]
        pieces, pos = [], off
        for d, c0, n, p0 in segs:
            if p0 > pos:
                pieces.append(jnp.zeros((p0 - pos, tl), F32))
            k, slot = where[d]
            pieces.append(g_refs[k][slot, c0:c0 + n, :].astype(F32))
            pos = p0 + n
        if off + width > pos:
            pieces.append(jnp.zeros((off + width - pos, tl), F32))
        o_ref[...] = jnp.concatenate(pieces, axis=0).astype(BF16)

    return pl.pallas_call(
        body, name=name,
        grid=(D_MODEL // tl,),
        in_specs=[pl.BlockSpec((gathered[k].shape[0], SHARD_COLS, tl), lambda i: (0, 0, i))
                  for k in used],
        out_specs=pl.BlockSpec((width, tl), lambda i: (0, i)),
        out_shape=jax.ShapeDtypeStruct((width, D_MODEL), BF16),
        compiler_params=_cparams(("arbitrary",)),
    )(*[gathered[k] for k in used])


def _shard_groups(d):
    return sorted({_group_of(s[3]) for s in _segments() if s[0] == d})


def _grads_to_shards(name, groups, dests, own_prev):
    tl = 256
    segs = _segments()
    used = sorted(groups)

    def body(*refs):
        g_refs, prev_ref, o_ref, own_ref = dict(zip(used, refs[:-3])), refs[-3], refs[-2], refs[-1]
        x, y, c = _mesh_pos()
        own = prev_ref[...].astype(F32)
        row = lax.broadcasted_iota(jnp.int32, (SHARD_PAD, tl), 0)
        for i, (d, ranges) in enumerate(dests):
            pieces, pos, asked = [], 0, None
            for r0, r1 in sorted(ranges):
                if r0 > pos:
                    pieces.append(jnp.zeros((r0 - pos, tl), F32))
                for _, c0, n, p0 in sorted([s for s in segs if s[0] == d], key=lambda s: s[1]):
                    a, b = max(c0, r0), min(c0 + n, r1)
                    if a < b:
                        gi = _group_of(p0)
                        lo = p0 - P_GROUPS[gi][0] + a - c0
                        pieces.append(g_refs[gi][lo:lo + b - a, :].astype(F32))
                if r1 > SHARD_COLS:
                    pieces.append(jnp.zeros((r1 - max(r0, SHARD_COLS), tl), F32))
                pos = r1
                inside = jnp.logical_and(row >= r0, row < r1)
                asked = inside if asked is None else jnp.logical_or(asked, inside)
            if pos < SHARD_PAD:
                pieces.append(jnp.zeros((SHARD_PAD - pos, tl), F32))
            shard = jnp.concatenate(pieces, axis=0)
            o_ref[i] = shard.astype(BF16)
            own = jnp.where(jnp.logical_and(4 * x + 2 * y + c == d, asked), shard, own)
        own_ref[...] = own.astype(BF16)

    blk = pl.BlockSpec((SHARD_PAD, tl), lambda i: (0, i))
    return pl.pallas_call(
        body, name=name,
        grid=(D_MODEL // tl,),
        in_specs=[pl.BlockSpec((P_GROUPS[g][1], tl), lambda i: (0, i)) for g in used] + [blk],
        out_specs=(pl.BlockSpec((len(dests), SHARD_PAD, tl), lambda i: (0, 0, i)), blk),
        out_shape=(jax.ShapeDtypeStruct((len(dests), SHARD_PAD, D_MODEL), BF16),
                   jax.ShapeDtypeStruct((SHARD_PAD, D_MODEL), BF16)),
        input_output_aliases={len(used): 1},
        compiler_params=_cparams(("arbitrary",)),
    )(*[groups[g] for g in used], own_prev)


def _inproj(x, g_in, w_pt):
    t = x.shape[0]
    tm = min(256, t)
    width = w_pt.shape[0]

    def body(x_ref, g_ref, w_ref, proj_ref, h_ref, r_ref):
        xf = x_ref[...]
        r = lax.rsqrt(jnp.mean(xf * xf, axis=-1, keepdims=True) + EPS)
        h = ((xf * r) * g_ref[...]).astype(BF16)
        proj_ref[...] = _dot_nt(h, w_ref[...])
        h_ref[...] = h
        r_ref[...] = r

    row = lambda w: pl.BlockSpec((tm, w), lambda i: (i, 0))
    return pl.pallas_call(
        body, name="inproj_latents",
        grid=(t // tm,),
        in_specs=[row(D_MODEL), pl.BlockSpec((1, D_MODEL), lambda i: (0, 0)),
                  pl.BlockSpec((width, D_MODEL), lambda i: (0, 0))],
        out_specs=(row(width), row(D_MODEL), row(1)),
        out_shape=(jax.ShapeDtypeStruct((t, width), F32),
                   jax.ShapeDtypeStruct((t, D_MODEL), BF16),
                   jax.ShapeDtypeStruct((t, 1), F32)),
        compiler_params=_cparams(("arbitrary",)),
    )(x, g_in, w_pt)


def _proj(name, h, w_pt):
    t = h.shape[0]
    tm = min(256, t)
    width = w_pt.shape[0]

    def body(h_ref, w_ref, o_ref):
        o_ref[...] = _dot_nt(h_ref[...], w_ref[...])

    return pl.pallas_call(
        body, name=name,
        grid=(t // tm,),
        in_specs=[pl.BlockSpec((tm, D_MODEL), lambda i: (i, 0)),
                  pl.BlockSpec((width, D_MODEL), lambda i: (0, 0))],
        out_specs=pl.BlockSpec((tm, width), lambda i: (i, 0)),
        out_shape=jax.ShapeDtypeStruct((t, width), F32),
        compiler_params=_cparams(("arbitrary",)),
    )(h, w_pt)


def _mla_prep(proj, g_q, g_kv, w_uq_p, w_k_p, w_v, w_gate_p, b_gate, rc, rsn, rsp):
    t = proj.shape[0]
    tm = min(256, t)
    hq = MLA_HEADS * HEAD_PAD

    def body(cq_ref, ckv_ref, misc_ref, gq_ref, gkv_ref, wuq_ref, wk_ref, wv_ref, wg_ref, bg_ref,
             c_ref, sn_ref, sp_ref,
             q_ref, k_ref, v_ref, la_ref, pre_ref, cqn_ref, ckvn_ref, rq_ref, rkv_ref, mb_ref):
        c, sn, sp = c_ref[...], sn_ref[...], sp_ref[...]
        cq = cq_ref[:, :MLA_Q_RANK]
        rq = lax.rsqrt(jnp.mean(cq * cq, axis=-1, keepdims=True) + EPS)
        cqn = ((cq * rq) * gq_ref[...]).astype(BF16)
        cqn_ref[...] = cqn
        rq_ref[...] = rq
        qpre = _dot(cqn, wuq_ref[...])
        ckv = ckv_ref[...]
        rkv = lax.rsqrt(jnp.mean(ckv * ckv, axis=-1, keepdims=True) + EPS)
        ckvn = ((ckv * rkv) * gkv_ref[...]).astype(BF16)
        ckvn_ref[...] = ckvn
        rkv_ref[...] = rkv
        kn = _dot(ckvn, wk_ref[...])
        v_ref[...] = _dot(ckvn, wv_ref[...]).astype(BF16)
        misc = misc_ref[...]
        krope = _rope_fwd(misc, c, sn, sp)
        for h in range(MLA_HEADS):
            sl = slice(h * HEAD_PAD, (h + 1) * HEAD_PAD)
            q_ref[:, sl] = _rope_fwd(qpre[:, sl], c, sn, sp).astype(BF16)
            k_ref[:, sl] = (kn[:, sl] + krope).astype(BF16)
        mb_ref[...] = misc.astype(BF16)
        pre = _dot(mb_ref[...], wg_ref[...]) + bg_ref[...]
        pre_ref[...] = pre
        log_a = (jnp.minimum(pre, 0.0) - jnp.log(1.0 + jnp.exp(-jnp.abs(pre)))) / GLA_GATE_NORM
        la_ref[...] = _dot_exact(_chunk_tri(tm, True), log_a)

    row = lambda w: pl.BlockSpec((tm, w), lambda i: (i, 0))
    full = lambda a: pl.BlockSpec(a.shape, lambda i: (0, 0))
    return pl.pallas_call(
        body, name="mla_prep",
        grid=(t // tm,),
        in_specs=[pl.BlockSpec((tm, 512), lambda i: (i, _rel(P_CQ) // 512)),
                  pl.BlockSpec((tm, MLA_KV_RANK), lambda i: (i, _rel(P_CKV) // MLA_KV_RANK)),
                  pl.BlockSpec((tm, LANE), lambda i: (i, _rel(P_MISC) // LANE)),
                  full(g_q), full(g_kv), full(w_uq_p), full(w_k_p), full(w_v), full(w_gate_p),
                  full(b_gate), row(LANE), row(LANE), row(LANE)],
        out_specs=(row(hq), row(hq), row(MLA_WIDTH), row(GLA_DK), row(GLA_DK),
                   row(MLA_Q_RANK), row(MLA_KV_RANK), row(1), row(1), row(LANE)),
        out_shape=(jax.ShapeDtypeStruct((t, hq), BF16), jax.ShapeDtypeStruct((t, hq), BF16),
                   jax.ShapeDtypeStruct((t, MLA_WIDTH), BF16),
                   jax.ShapeDtypeStruct((t, GLA_DK), F32), jax.ShapeDtypeStruct((t, GLA_DK), F32),
                   jax.ShapeDtypeStruct((t, MLA_Q_RANK), BF16),
                   jax.ShapeDtypeStruct((t, MLA_KV_RANK), BF16),
                   jax.ShapeDtypeStruct((t, 1), F32), jax.ShapeDtypeStruct((t, 1), F32),
                   jax.ShapeDtypeStruct((t, LANE), BF16)),
        compiler_params=_cparams(("arbitrary",)),
    )(proj, proj, proj, g_q, g_kv, w_uq_p, w_k_p, w_v, w_gate_p, b_gate, rc, rsn, rsp)


def _attn_masks(tq, i):
    keys = (i + 1) * tq
    rows = i * tq + lax.broadcasted_iota(jnp.int32, (tq, keys), 0)
    cols = lax.broadcasted_iota(jnp.int32, (tq, keys), 1)
    lane = lax.broadcasted_iota(jnp.int32, (tq, LANE), 1)
    return cols <= rows, lane < MLA_VDIM


def _for_each_query_tile(n_tiles, fn):
    for i in range(n_tiles):
        pl.when(pl.program_id(1) == i)(lambda i=i: fn(i))


def _mla_attn_fwd(q, k, v, shards, sources):
    t = q.shape[0]
    tq = min(256, t)
    scale = MLA_QK ** -0.5
    ns = len(shards)
    g_shapes, g_sems = _gather_plan(shards, sources)
    grid = (MLA_HEADS // 2, t // tq)

    def body(q_ref, k_ref, v_ref, *rest):
        o_ref, lse_ref = rest[ns:ns + 2]
        start, finish = _gather_hooks(rest[:ns], rest[ns + 2:2 * ns + 2], rest[2 * ns + 2:], sources)
        step = pl.program_id(0) * grid[1] + pl.program_id(1)
        pl.when(step == 0)(start)

        def tile(i):
            keys = (i + 1) * tq
            causal, low = _attn_masks(tq, i)
            vp = v_ref[0:keys, :]
            acc = jnp.zeros((tq, LANE), F32)
            for hh in range(2):
                sl = slice(hh * HEAD_PAD, (hh + 1) * HEAD_PAD)
                s = _dot_nt(q_ref[:, sl], k_ref[0:keys, sl]) * scale
                s = jnp.where(causal, s, -jnp.inf)
                m = jnp.max(s, axis=-1, keepdims=True)
                e = jnp.exp(s - m)
                l = jnp.sum(e, axis=-1, keepdims=True)
                o = _dot(e.astype(BF16), vp) / l
                acc = jnp.where(low if hh == 0 else jnp.logical_not(low), o, acc)
                lse_ref[hh] = m + jnp.log(l)
            o_ref[...] = acc

        _for_each_query_tile(t // tq, tile)
        pl.when(step == grid[0] * grid[1] - 1)(finish)

    res = pl.pallas_call(
        body, name="mla_attn_fwd",
        grid=grid,
        in_specs=[pl.BlockSpec((tq, 2 * HEAD_PAD), lambda p, i: (i, p)),
                  pl.BlockSpec((t, 2 * HEAD_PAD), lambda p, i: (0, p)),
                  pl.BlockSpec((t, LANE), lambda p, i: (0, p))] + _hbm_specs(ns),
        out_specs=(pl.BlockSpec((tq, LANE), lambda p, i: (i, p)),
                   pl.BlockSpec((2, tq, 1), lambda p, i: (p, i, 0))) + tuple(_hbm_specs(ns)),
        out_shape=(jax.ShapeDtypeStruct((t, MLA_WIDTH), F32),
                   jax.ShapeDtypeStruct((MLA_HEADS, t, 1), F32)) + tuple(g_shapes),
        scratch_shapes=g_sems,
        compiler_params=_cparams(("arbitrary", "arbitrary")),
    )(q, k, v, *shards)
    return res[0], res[1], res[2:]


def _mla_attn_bwd(q, k, v, o, do, lse, after):
    t = q.shape[0]
    tq = min(256, t)
    scale = MLA_QK ** -0.5

    def body(q_ref, k_ref, v_ref, o_ref, do_ref, lse_ref, after_ref, dq_ref, dk_ref, dv_ref):
        del after_ref

        @pl.when(pl.program_id(1) == 0)
        def _():
            dk_ref[...] = jnp.zeros_like(dk_ref)
            dv_ref[...] = jnp.zeros_like(dv_ref)

        def tile(i):
            keys = (i + 1) * tq
            causal, low = _attn_masks(tq, i)
            vp = v_ref[0:keys, :]
            do_all = do_ref[...]
            o_all = o_ref[...]
            dv_acc = jnp.zeros((keys, LANE), F32)
            for hh in range(2):
                sl = slice(hh * HEAD_PAD, (hh + 1) * HEAD_PAD)
                do_h = jnp.where(low if hh == 0 else jnp.logical_not(low), do_all, 0.0)
                dsum = jnp.sum(do_h * o_all, axis=-1, keepdims=True)
                qh = q_ref[:, sl]
                kh = k_ref[0:keys, sl]
                s = _dot_nt(qh, kh) * scale
                p = jnp.where(causal, jnp.exp(s - lse_ref[hh]), 0.0)
                do_b = do_h.astype(BF16)
                dp = _dot_nt(do_b, vp)
                ds = (p * (dp - dsum) * scale).astype(BF16)
                dq_ref[:, sl] = _dot(ds, kh).astype(BF16)
                dk_ref[0:keys, sl] += _dot_tn(ds, qh)
                dv_acc = dv_acc + _dot_tn(p.astype(BF16), do_b)
            dv_ref[0:keys, :] += dv_acc

        _for_each_query_tile(t // tq, tile)

    return pl.pallas_call(
        body, name="mla_attn_bwd",
        grid=(MLA_HEADS // 2, t // tq),
        in_specs=[pl.BlockSpec((tq, 2 * HEAD_PAD), lambda p, i: (i, p)),
                  pl.BlockSpec((t, 2 * HEAD_PAD), lambda p, i: (0, p)),
                  pl.BlockSpec((t, LANE), lambda p, i: (0, p)),
                  pl.BlockSpec((tq, LANE), lambda p, i: (i, p)),
                  pl.BlockSpec((tq, LANE), lambda p, i: (i, p)),
                  pl.BlockSpec((2, tq, 1), lambda p, i: (p, i, 0)),
                  pl.BlockSpec(memory_space=pl.ANY)],
        out_specs=(pl.BlockSpec((tq, 2 * HEAD_PAD), lambda p, i: (i, p)),
                   pl.BlockSpec((t, 2 * HEAD_PAD), lambda p, i: (0, p)),
                   pl.BlockSpec((t, LANE), lambda p, i: (0, p))),
        out_shape=(jax.ShapeDtypeStruct((t, MLA_HEADS * HEAD_PAD), BF16),
                   jax.ShapeDtypeStruct((t, MLA_HEADS * HEAD_PAD), F32),
                   jax.ShapeDtypeStruct((t, MLA_WIDTH), F32)),
        compiler_params=_cparams(("arbitrary", "arbitrary")),
    )(q, k, v, o, do, lse, after)


def _chunk_tri(n, lower):
    r = lax.broadcasted_iota(jnp.int32, (n, n), 0)
    c = lax.broadcasted_iota(jnp.int32, (n, n), 1)
    same = (r // GLA_CHUNK) == (c // GLA_CHUNK)
    return jnp.where(jnp.logical_and(same, r >= c if lower else r <= c), 1.0, 0.0).astype(F32)


def _gla_chunk_terms(q_ref, k_ref, b_ref, h):
    sl = slice(h * GLA_HK, (h + 1) * GLA_HK)
    b = b_ref[:, sl]
    bl = b[GLA_CHUNK - 1:GLA_CHUNK, :]
    kc = k_ref[:, sl]
    q_in = (q_ref[:, sl] * (GLA_HK ** -0.5)) * jnp.exp(b)
    k_in = kc * jnp.exp(-b)
    k_st = kc * jnp.exp(bl - b)
    return b, bl, q_in, k_in, k_st


def _tri(c, lower):
    r = lax.broadcasted_iota(jnp.int32, (c, c), 0)
    cc = lax.broadcasted_iota(jnp.int32, (c, c), 1)
    return jnp.where(r >= cc if lower else r <= cc, 1.0, 0.0).astype(F32)


def _gla_fwd(proj, log_a, shards, sources):
    t = proj.shape[0]
    c = GLA_CHUNK
    n = t // c
    ns = len(shards)
    g_shapes, g_sems = _gather_plan(shards, sources)

    def body(q_ref, k_ref, v_ref, la_ref, *rest):
        o_ref, sp_ref = rest[ns:ns + 2]
        st_ref = rest[2 * ns + 2]
        start, finish = _gather_hooks(rest[:ns], rest[ns + 2:2 * ns + 2], rest[2 * ns + 3:], sources)

        @pl.when(pl.program_id(0) == 0)
        def _():
            st_ref[...] = jnp.zeros_like(st_ref)
            start()

        tri = _tri(c, True)
        for h in range(GLA_HEADS):
            _, bl, q_in, k_in, k_st = _gla_chunk_terms(q_ref, k_ref, la_ref, h)
            vs = slice(h * GLA_HV, (h + 1) * GLA_HV)
            vv = v_ref[:, vs].astype(BF16)
            qb = q_in.astype(BF16)
            attn = _dot_nt(qb, k_in.astype(BF16)) * tri
            st = st_ref[h]
            sp_ref[0, h] = st
            o_ref[:, vs] = _dot(attn.astype(BF16), vv) + _dot_nt(qb, st.astype(BF16))
            st_ref[h] = st * jnp.exp(bl) + _dot_tn(vv, k_st.astype(BF16))

        pl.when(pl.program_id(0) == n - 1)(finish)

    res = pl.pallas_call(
        body, name="gla_fwd",
        grid=(n,),
        in_specs=[pl.BlockSpec((c, GLA_DK), lambda i: (i, P_QG // GLA_DK)),
                  pl.BlockSpec((c, GLA_DK), lambda i: (i, P_KG // GLA_DK)),
                  pl.BlockSpec((c, GLA_DV), lambda i: (i, P_VG // GLA_DV)),
                  pl.BlockSpec((c, GLA_DK), lambda i: (i, 0))] + _hbm_specs(ns),
        out_specs=(pl.BlockSpec((c, GLA_DV), lambda i: (i, 0)),
                   pl.BlockSpec((1, GLA_HEADS, GLA_HV, GLA_HK), lambda i: (i, 0, 0, 0)))
        + tuple(_hbm_specs(ns)),
        out_shape=(jax.ShapeDtypeStruct((t, GLA_DV), F32),
                   jax.ShapeDtypeStruct((n, GLA_HEADS, GLA_HV, GLA_HK), F32)) + tuple(g_shapes),
        scratch_shapes=[pltpu.VMEM((GLA_HEADS, GLA_HV, GLA_HK), F32)] + g_sems,
        compiler_params=_cparams(("arbitrary",)),
    )(proj, proj, proj, log_a, *shards)
    return res[0], res[1], res[2:]


def _gla_bwd(proj, log_a, do, states, after):
    t = proj.shape[0]
    c = GLA_CHUNK
    n = t // c

    def body(q_ref, k_ref, v_ref, la_ref, do_ref, sp_ref, after_ref, dg_ref, dla_ref, ds_ref):
        del after_ref

        @pl.when(pl.program_id(0) == 0)
        def _():
            ds_ref[...] = jnp.zeros_like(ds_ref)

        tri = _tri(c, True)
        last = lax.broadcasted_iota(jnp.int32, (c, GLA_HK), 0) == c - 1
        for h in range(GLA_HEADS):
            b, bl, q_in, k_in, k_st = _gla_chunk_terms(q_ref, k_ref, la_ref, h)
            ks_ = slice(h * GLA_HK, (h + 1) * GLA_HK)
            vs = slice(h * GLA_HV, (h + 1) * GLA_HV)
            vv = v_ref[:, vs].astype(BF16)
            do_h = do_ref[:, vs]
            qb, kb, ksb = q_in.astype(BF16), k_in.astype(BF16), k_st.astype(BF16)
            attn = (_dot_nt(qb, kb) * tri).astype(BF16)
            st = sp_ref[0, h]
            dst = ds_ref[h]
            dstb = dst.astype(BF16)
            dattn = (_dot_nt(do_h, vv) * tri).astype(BF16)
            dg_ref[:, P_VG + h * GLA_HV:P_VG + (h + 1) * GLA_HV] = (
                _dot_tn(attn, do_h) + _dot_nt(ksb, dstb)).astype(BF16)
            dq_in = _dot(dattn, kb) + _dot(do_h, st.astype(BF16))
            dk_in = _dot_tn(dattn, qb)
            dk_st = _dot(vv, dstb)
            ebl = jnp.exp(bl)
            d_ebl = jnp.sum(st * dst, axis=0, keepdims=True)
            ds_ref[h] = _dot_tn(do_h, qb) + dst * ebl
            dg_ref[:, P_QG + h * GLA_HK:P_QG + (h + 1) * GLA_HK] = (
                dq_in * (GLA_HK ** -0.5) * jnp.exp(b)).astype(BF16)
            dg_ref[:, P_KG + h * GLA_HK:P_KG + (h + 1) * GLA_HK] = (
                dk_in * jnp.exp(-b) + dk_st * jnp.exp(bl - b)).astype(BF16)
            db = dq_in * q_in - dk_in * k_in - dk_st * k_st
            dbl = jnp.sum(dk_st * k_st, axis=0, keepdims=True) + d_ebl * ebl
            dla_ref[:, ks_] = db + jnp.where(last, dbl, 0.0)

    rev = lambda i: n - 1 - i
    gw = P_GROUPS[0][1]
    return pl.pallas_call(
        body, name="gla_bwd",
        grid=(n,),
        in_specs=[pl.BlockSpec((c, GLA_DK), lambda i: (rev(i), P_QG // GLA_DK)),
                  pl.BlockSpec((c, GLA_DK), lambda i: (rev(i), P_KG // GLA_DK)),
                  pl.BlockSpec((c, GLA_DV), lambda i: (rev(i), P_VG // GLA_DV)),
                  pl.BlockSpec((c, GLA_DK), lambda i: (rev(i), 0)),
                  pl.BlockSpec((c, GLA_DV), lambda i: (rev(i), 0)),
                  pl.BlockSpec((1, GLA_HEADS, GLA_HV, GLA_HK), lambda i: (rev(i), 0, 0, 0)),
                  pl.BlockSpec(memory_space=pl.ANY)],
        out_specs=(pl.BlockSpec((c, gw), lambda i: (rev(i), 0)),
                   pl.BlockSpec((c, GLA_DK), lambda i: (rev(i), 0))),
        out_shape=(jax.ShapeDtypeStruct((t, gw), BF16), jax.ShapeDtypeStruct((t, GLA_DK), F32)),
        scratch_shapes=[pltpu.VMEM((GLA_HEADS, GLA_HV, GLA_HK), F32)],
        compiler_params=_cparams(("arbitrary",)),
    )(proj, proj, proj, log_a, do, states, after)


def _post(o_mla, proj, o_gla, x, target, g_gla, g_final, w_pm, w_pg, w_o):
    t = x.shape[0]
    tm = min(128, t)
    g0, gw = P_GROUPS[1]

    def body(om_ref, zg_ref, gm_ref, gg_ref, zm_ref, og_ref, x_ref, tg_ref, ggla_ref, gf_ref,
             wpm_ref, wpg_ref, wo_ref,
             dx2_ref, dom_ref, dog_ref, dg_ref,
             mg_ref, um_ref, ug_ref, dym_ref, dyg_ref, loss_ref, dgf_ref, dggla_ref):
        @pl.when(pl.program_id(0) == 0)
        def _():
            loss_ref[...] = jnp.zeros_like(loss_ref)
            dgf_ref[...] = jnp.zeros_like(dgf_ref)
            dggla_ref[...] = jnp.zeros_like(dggla_ref)

        om = om_ref[...]
        zm = zm_ref[...]
        sm = _sigmoid(zm)
        silu_m = zm * sm
        um = (om * silu_m).astype(BF16)
        um_ref[...] = um
        ym = _dot(um, wpm_ref[...])

        ggla = ggla_ref[...]
        zg = zg_ref[...]
        sg = _sigmoid(zg)
        silu_g = zg * sg
        xhat, rstd, on = [], [], []
        for h in range(GLA_HEADS):
            blk = og_ref[:, h * GLA_HV:(h + 1) * GLA_HV]
            r = lax.rsqrt(jnp.mean(blk * blk, axis=-1, keepdims=True) + EPS)
            xhat.append(blk * r)
            rstd.append(r)
            on.append(xhat[h] * ggla)
        on = jnp.concatenate(on, axis=-1)
        ug = (on * silu_g).astype(BF16)
        ug_ref[...] = ug
        yg = _dot(ug, wpg_ref[...])

        sgm = _sigmoid(gm_ref[...])
        sgg = _sigmoid(gg_ref[...])
        merged = (sgm * ym + sgg * yg).astype(BF16)
        mg_ref[...] = merged
        x2 = x_ref[...] + _dot(merged, wo_ref[...])
        gf = gf_ref[...]
        rf = lax.rsqrt(jnp.mean(x2 * x2, axis=-1, keepdims=True) + EPS)
        xh = x2 * rf
        err = xh * gf - tg_ref[...]
        loss_ref[...] += 0.5 * jnp.sum(jnp.mean(err * err, axis=-1, keepdims=True))

        dy = err * (1.0 / D_MODEL)
        dgf_ref[...] += jnp.sum(dy * xh, axis=0, keepdims=True)
        dxh = dy * gf
        dx2 = rf * (dxh - xh * jnp.mean(dxh * xh, axis=-1, keepdims=True))
        dx2_ref[...] = dx2
        dmerged = _dot_nt(dx2.astype(BF16), wo_ref[...])
        dym = (dmerged * sgm).astype(BF16)
        dyg = (dmerged * sgg).astype(BF16)
        dym_ref[...] = dym
        dyg_ref[...] = dyg
        dg_ref[:, P_GMLA - g0:P_GMLA - g0 + D_MODEL] = (dmerged * ym * sgm * (1.0 - sgm)).astype(BF16)
        dg_ref[:, P_GGLA - g0:P_GGLA - g0 + D_MODEL] = (dmerged * yg * sgg * (1.0 - sgg)).astype(BF16)
        dum = _dot_nt(dym, wpm_ref[...])
        dom_ref[...] = dum * silu_m
        dg_ref[:, P_ZMLA - g0:P_ZMLA - g0 + MLA_WIDTH] = (
            dum * om * (sm * (1.0 + zm * (1.0 - sm)))).astype(BF16)
        dug = _dot_nt(dyg, wpg_ref[...])
        dg_ref[:, P_ZGLA - g0:P_ZGLA - g0 + GLA_DV] = (
            dug * on * (sg * (1.0 + zg * (1.0 - sg)))).astype(BF16)
        don = dug * silu_g
        dggla = jnp.zeros((1, GLA_HV), F32)
        for h in range(GLA_HEADS):
            hs = slice(h * GLA_HV, (h + 1) * GLA_HV)
            don_h = don[:, hs]
            dggla = dggla + jnp.sum(don_h * xhat[h], axis=0, keepdims=True)
            dxh_h = don_h * ggla
            dog_ref[:, hs] = (rstd[h] * (dxh_h - xhat[h] * jnp.mean(dxh_h * xhat[h], axis=-1,
                                                                     keepdims=True))).astype(BF16)
        dggla_ref[...] += dggla

    row = lambda w: pl.BlockSpec((tm, w), lambda i: (i, 0))
    pcol = lambda w, off: pl.BlockSpec((tm, w), lambda i: (i, _rel(off) // w))
    full = lambda a: pl.BlockSpec(a.shape, lambda i: (0, 0))
    sds = jax.ShapeDtypeStruct
    return pl.pallas_call(
        body, name="post_fwd_bwd",
        grid=(t // tm,),
        in_specs=[row(MLA_WIDTH), pcol(GLA_DV, P_ZGLA), pcol(D_MODEL, P_GMLA), pcol(D_MODEL, P_GGLA),
                  pcol(MLA_WIDTH, P_ZMLA), row(GLA_DV), row(D_MODEL), row(D_MODEL),
                  full(g_gla), full(g_final), full(w_pm), full(w_pg), full(w_o)],
        out_specs=(row(D_MODEL), row(MLA_WIDTH), row(GLA_DV), row(gw),
                   row(D_MODEL), row(MLA_WIDTH), row(GLA_DV), row(D_MODEL), row(D_MODEL),
                   pl.BlockSpec((1, LANE), lambda i: (0, 0)),
                   pl.BlockSpec((1, D_MODEL), lambda i: (0, 0)),
                   pl.BlockSpec((1, GLA_HV), lambda i: (0, 0))),
        out_shape=(sds((t, D_MODEL), F32), sds((t, MLA_WIDTH), F32), sds((t, GLA_DV), BF16),
                   sds((t, gw), BF16),
                   sds((t, D_MODEL), BF16), sds((t, MLA_WIDTH), BF16), sds((t, GLA_DV), BF16),
                   sds((t, D_MODEL), BF16), sds((t, D_MODEL), BF16),
                   sds((1, LANE), F32), sds((1, D_MODEL), F32), sds((1, GLA_HV), F32)),
        compiler_params=_cparams(("arbitrary",)),
    )(o_mla, proj, proj, proj, proj, o_gla, x, target, g_gla, g_final, w_pm, w_pg, w_o)


def _mla_prep_bwd(dq, dk, dv, dla, pre, proj, rq, rkv, g_q, g_kv, w_uq_p, w_k_p, w_v, w_gate_p,
                  rc, rsn, rsp):
    t = proj.shape[0]
    tm = min(256, t)
    gw = P_GROUPS[2][1]

    def body(dq_ref, dk_ref, dv_ref, dla_ref, pre_ref, cq_ref, ckv_ref, rq_ref, rkv_ref,
             gq_ref, gkv_ref, wuq_ref, wk_ref, wv_ref, wg_ref, c_ref, sn_ref, sp_ref,
             dg_ref, dqpre_ref, dpre_ref, dgq_ref, dgkv_ref, dbg_ref):
        @pl.when(pl.program_id(0) == 0)
        def _():
            dgq_ref[...] = jnp.zeros_like(dgq_ref)
            dgkv_ref[...] = jnp.zeros_like(dgkv_ref)
            dbg_ref[...] = jnp.zeros_like(dbg_ref)

        c, sn, sp = c_ref[...], sn_ref[...], sp_ref[...]
        dkr = jnp.zeros((tm, LANE), F32)
        for h in range(MLA_HEADS):
            sl = slice(h * HEAD_PAD, (h + 1) * HEAD_PAD)
            dqpre_ref[:, sl] = _rope_bwd(dq_ref[:, sl].astype(F32), c, sn, sp).astype(BF16)
            dkr = dkr + dk_ref[:, sl]
        dcqn = _dot_nt(dqpre_ref[...], wuq_ref[...])
        rq = rq_ref[...]
        xh = cq_ref[:, :MLA_Q_RANK] * rq
        dgq_ref[...] += jnp.sum(dcqn * xh, axis=0, keepdims=True)
        dxh = dcqn * gq_ref[...]
        dcq = rq * (dxh - xh * jnp.mean(dxh * xh, axis=-1, keepdims=True))
        dg_ref[:, :MLA_Q_RANK] = dcq.astype(BF16)
        dg_ref[:, MLA_Q_RANK:512] = jnp.zeros((tm, 512 - MLA_Q_RANK), BF16)

        dckvn = _dot_nt(dk_ref[...].astype(BF16), wk_ref[...]) + \
            _dot_nt(dv_ref[...].astype(BF16), wv_ref[...])
        rkv = rkv_ref[...]
        xh = ckv_ref[...] * rkv
        dgkv_ref[...] += jnp.sum(dckvn * xh, axis=0, keepdims=True)
        dxh = dckvn * gkv_ref[...]
        dg_ref[:, P_CKV - P_CQ:P_CKV - P_CQ + MLA_KV_RANK] = (
            rkv * (dxh - xh * jnp.mean(dxh * xh, axis=-1, keepdims=True))).astype(BF16)

        dlog_a = _dot_exact(_chunk_tri(tm, False), dla_ref[...])
        dpre = dlog_a * (1.0 / GLA_GATE_NORM) * (1.0 - _sigmoid(pre_ref[...]))
        dbg_ref[...] += jnp.sum(dpre, axis=0, keepdims=True)
        dpre = dpre.astype(BF16)
        dpre_ref[...] = dpre
        lane = lax.broadcasted_iota(jnp.int32, (tm, LANE), 1)
        in_kr = jnp.logical_and(lane >= MISC_KR, lane < MISC_KR + MLA_ROPE)
        dmisc = jnp.where(in_kr, _rope_bwd(dkr, c, sn, sp), 0.0) + _dot_nt(dpre, wg_ref[...])
        dg_ref[:, P_MISC - P_CQ:P_MISC - P_CQ + LANE] = dmisc.astype(BF16)

    hq = MLA_HEADS * HEAD_PAD
    row = lambda w: pl.BlockSpec((tm, w), lambda i: (i, 0))
    full = lambda a: pl.BlockSpec(a.shape, lambda i: (0, 0))
    acc = lambda w: pl.BlockSpec((1, w), lambda i: (0, 0))
    sds = jax.ShapeDtypeStruct
    return pl.pallas_call(
        body, name="mla_prep_bwd",
        grid=(t // tm,),
        in_specs=[row(hq), row(hq), row(MLA_WIDTH), row(GLA_DK), row(GLA_DK),
                  pl.BlockSpec((tm, 512), lambda i: (i, _rel(P_CQ) // 512)),
                  pl.BlockSpec((tm, MLA_KV_RANK), lambda i: (i, _rel(P_CKV) // MLA_KV_RANK)),
                  row(1), row(1), full(g_q), full(g_kv), full(w_uq_p), full(w_k_p), full(w_v),
                  full(w_gate_p), row(LANE), row(LANE), row(LANE)],
        out_specs=(row(gw), row(hq), row(GLA_DK),
                   acc(MLA_Q_RANK), acc(MLA_KV_RANK), acc(GLA_DK)),
        out_shape=(sds((t, gw), BF16), sds((t, hq), BF16), sds((t, GLA_DK), BF16),
                   sds((1, MLA_Q_RANK), F32), sds((1, MLA_KV_RANK), F32), sds((1, GLA_DK), F32)),
        compiler_params=_cparams(("arbitrary",)),
    )(dq, dk, dv, dla, pre, proj, proj, rq, rkv, g_q, g_kv, w_uq_p, w_k_p, w_v, w_gate_p,
      rc, rsn, rsp)


def _inproj_bwd(dgroups, w_pts, x, rstd, g_in, dx2, after):
    t = x.shape[0]
    tm = min(256, t)

    def body(d0_ref, d1_ref, d2_ref, w0_ref, w1_ref, w2_ref, x_ref, r_ref, g_ref, dx2_ref, after_ref,
             dx_ref, dg_ref):
        del after_ref

        @pl.when(pl.program_id(0) == 0)
        def _():
            dg_ref[...] = jnp.zeros_like(dg_ref)

        dh = jnp.zeros((tm, D_MODEL), F32)
        for d_ref, w_ref in zip((d0_ref, d1_ref, d2_ref), (w0_ref, w1_ref, w2_ref)):
            dh = dh + _dot(d_ref[...], w_ref[...])
        r = r_ref[...]
        xh = x_ref[...] * r
        dg_ref[...] += jnp.sum(dh * xh, axis=0, keepdims=True)
        dxh = dh * g_ref[...]
        dx_ref[...] = dx2_ref[...] + r * (dxh - xh * jnp.mean(dxh * xh, axis=-1, keepdims=True))

    row = lambda w: pl.BlockSpec((tm, w), lambda i: (i, 0))
    return pl.pallas_call(
        body, name="inproj_bwd",
        grid=(t // tm,),
        in_specs=[row(w) for _, w in P_GROUPS]
        + [pl.BlockSpec((w, D_MODEL), lambda i: (0, 0)) for _, w in P_GROUPS]
        + [row(D_MODEL), row(1), pl.BlockSpec((1, D_MODEL), lambda i: (0, 0)), row(D_MODEL),
           pl.BlockSpec(memory_space=pl.ANY)],
        out_specs=(row(D_MODEL), pl.BlockSpec((1, D_MODEL), lambda i: (0, 0))),
        out_shape=(jax.ShapeDtypeStruct((t, D_MODEL), F32),
                   jax.ShapeDtypeStruct((1, D_MODEL), F32)),
        compiler_params=_cparams(("arbitrary",)),
    )(*dgroups, *w_pts, x, rstd, g_in, dx2, after)


def _matmul(name, a, b, tm, tn, dtype=F32):
    kk, m = a.shape
    n = b.shape[1]

    def body(a_ref, b_ref, o_ref):
        o_ref[...] = _dot_tn(a_ref[...].astype(BF16), b_ref[...].astype(BF16)).astype(dtype)

    return pl.pallas_call(
        body, name=name,
        grid=(n // tn, m // tm),
        in_specs=[pl.BlockSpec((kk, tm), lambda j, i: (0, i)),
                  pl.BlockSpec((kk, tn), lambda j, i: (0, j))],
        out_specs=pl.BlockSpec((tm, tn), lambda j, i: (i, j)),
        out_shape=jax.ShapeDtypeStruct((m, n), dtype),
        compiler_params=_cparams(("arbitrary", "arbitrary")),
    )(a, b)


def _adamw_update(part_refs, w_ref, m_ref, v_ref, g_ref, d_ref, nm_ref, nv_ref):
    g = part_refs[0][...].astype(F32)
    for p_ref in part_refs[1:]:
        g = g + p_ref[...].astype(F32)
    m_new = ADAM_B1 * m_ref[...] + (1.0 - ADAM_B1) * g
    v_new = ADAM_B2 * v_ref[...] + (1.0 - ADAM_B2) * (g * g)
    m_hat = m_new / (1.0 - ADAM_B1 ** ADAM_STEP)
    v_hat = v_new / (1.0 - ADAM_B2 ** ADAM_STEP)
    g_ref[...] = g
    nm_ref[...] = m_new
    nv_ref[...] = v_new
    d_ref[...] = -ADAM_LR * (m_hat / (jnp.sqrt(v_hat) + ADAM_EPS) + ADAM_WD * w_ref[...])


def _adamw_rows(name, parts, w, m, v, tr, first=None):
    _, rows, cols = w.shape
    slots = parts.shape[0]

    def body(*refs):
        lead_refs, p_ref = ([], refs[0]) if first is None else ([refs[0]], refs[1])
        _adamw_update(lead_refs + [p_ref.at[q] for q in range(slots)], *refs[len(lead_refs) + 1:])

    blk = pl.BlockSpec((None, tr, cols), lambda i: (0, i, 0))
    out = jax.ShapeDtypeStruct((1, rows, cols), F32)
    lead = [] if first is None else [pl.BlockSpec((tr, cols), lambda i: (i, 0))]
    return pl.pallas_call(
        body, name=name,
        grid=(rows // tr,),
        in_specs=lead + [pl.BlockSpec((slots, tr, cols), lambda i: (0, i, 0)), blk, blk, blk],
        out_specs=(blk, blk, blk, blk),
        out_shape=(out, out, out, out),
        compiler_params=_cparams(("arbitrary",)),
    )(*([] if first is None else [first]), parts, w, m, v)


def _adamw_transposed(name, first, parts, w, m, v, tl):
    _, rows, cols = w.shape
    slots, padded = parts.shape[:2]

    def body(f_ref, p_ref, *refs):
        _adamw_update([f_ref.at[pl.ds(0, cols)]]
                      + [p_ref.at[q, pl.ds(0, cols)] for q in range(slots)], *refs)

    blk = pl.BlockSpec((cols, None, tl), lambda i: (0, 0, i))
    out = jax.ShapeDtypeStruct((cols, 1, rows), F32)
    res = pl.pallas_call(
        body, name=name,
        grid=(rows // tl,),
        in_specs=[pl.BlockSpec((padded, tl), lambda i: (0, i)),
                  pl.BlockSpec((slots, padded, tl), lambda i: (0, 0, i)), blk, blk, blk],
        out_specs=(blk, blk, blk, blk),
        out_shape=(out, out, out, out),
        compiler_params=_cparams(("arbitrary",)),
    )(first, parts, *[a.transpose(2, 0, 1) for a in (w, m, v)])
    return [r.transpose(1, 2, 0) for r in res]


def _adamw_group(firsts, parts, ws, ms, vs):
    n = len(ws)

    def body(*refs):
        ins, outs = refs[:5 * n], refs[5 * n:]
        x, y, c = _mesh_pos()
        for a in range(n):
            _adamw_update([ins[a].at[4 * x + 2 * y + c]]
                          + [ins[n + a].at[q] for q in range(ins[n + a].shape[0])],
                          *[r.at[0] for r in (ins[2 * n + a], ins[3 * n + a], ins[4 * n + a])],
                          *[r.at[0] for r in outs[4 * a:4 * a + 4]])

    vmem = lambda k: [pl.BlockSpec(memory_space=pltpu.VMEM) for _ in range(k)]
    out_shape = []
    for w in ws:
        out_shape += [jax.ShapeDtypeStruct(w.shape, F32)] * 4
    res = pl.pallas_call(
        body, name="adamw_small_weights",
        in_specs=vmem(5 * n), out_specs=tuple(vmem(4 * n)), out_shape=tuple(out_shape),
        compiler_params=_cparams(),
    )(*firsts, *parts, *ws, *ms, *vs)
    return [res[4 * a:4 * a + 4] for a in range(n)]


def _rope_tables(positions):
    half = MLA_ROPE // 2
    freqs = ROPE_THETA ** (-jnp.arange(half, dtype=F32) / half)
    ang = positions.astype(F32).reshape(-1, 1) * freqs
    cos, sin = jnp.cos(ang), jnp.sin(ang)
    t = ang.shape[0]
    one, zero = jnp.ones((t, MLA_NOPE), F32), jnp.zeros((t, half), F32)
    tail = jnp.zeros((t, LANE - MLA_QK), F32)
    rc = jnp.concatenate([one, cos, cos, tail], axis=1)
    rsn = jnp.concatenate([0.0 * one, -sin, zero, tail], axis=1)
    rsp = jnp.concatenate([0.0 * one, zero, sin, tail], axis=1)
    return rc, rsn, rsp


def _cols_full(g):
    return g.transpose(1, 0, 2)


def kernel(x, positions, g_in, w_in, g_q, w_uq, g_kv, w_ukv, w_gla_gate, b_gla_gate, g_gla, w_proj_mla, w_proj_gla, w_out, g_final, loss_target, m_g_in, m_w_in, m_g_q, m_w_uq, m_g_kv, m_w_ukv, m_w_gla_gate, m_b_gla_gate, m_g_gla, m_w_proj_mla, m_w_proj_gla, m_w_out, m_g_final, v_g_in, v_w_in, v_g_q, v_w_uq, v_g_kv, v_w_ukv, v_w_gla_gate, v_b_gla_gate, v_g_gla, v_w_proj_mla, v_w_proj_gla, v_w_out, v_g_final):
    t = x.shape[1]
    x2d = x.reshape(t, D_MODEL)
    tgt = loss_target.reshape(t, D_MODEL)
    g_final2 = g_final.reshape(1, D_MODEL)
    sharded = [(w_in, m_w_in, v_w_in), (w_uq, m_w_uq, v_w_uq), (w_ukv, m_w_ukv, v_w_ukv),
               (w_gla_gate, m_w_gla_gate, v_w_gla_gate), (w_proj_mla, m_w_proj_mla, v_w_proj_mla),
               (w_proj_gla, m_w_proj_gla, v_w_proj_gla), (w_out, m_w_out, v_w_out)]

    w_in_t = w_in.transpose(2, 0, 1).reshape(SHARD_COLS, D_MODEL)
    everyone = tuple(range(N_DEV))
    w_in_b = w_in_t.astype(BF16)
    b_uq, b_ukv, b_gate, b_pm, b_pg, b_o = [s[0][0].astype(BF16) for s in sharded[1:]]
    stages = ((0, 2, 4, 6), (1, 3, 5), (7,))
    where = {d: (k, i) for k, srcs in enumerate(stages) for i, d in enumerate(srcs)}
    g_in_1, g_uq, g_ukv, g_gate = _all_gather(
        "all_gather_first", [w_in_b, b_uq, b_ukv, b_gate], [stages[0]] + [everyone] * 3)
    w_uq_p = jnp.pad(_cols_full(g_uq), ((0, 0), (0, 0), (0, HEAD_PAD - MLA_QK))).reshape(
        MLA_Q_RANK, MLA_HEADS * HEAD_PAD)
    ukv = _cols_full(g_ukv)
    w_k_p = jnp.pad(ukv[:, :, :MLA_NOPE], ((0, 0), (0, 0), (0, HEAD_PAD - MLA_NOPE))).reshape(
        MLA_KV_RANK, MLA_HEADS * HEAD_PAD)
    w_v = ukv[:, :, MLA_NOPE:].reshape(MLA_KV_RANK, MLA_WIDTH)
    w_gate_p = jnp.pad(_cols_full(g_gate).reshape(GLA_GATE_RANK, GLA_DK),
                       ((MISC_ALR, LANE - MISC_ALR - GLA_GATE_RANK), (0, 0)))
    rc, rsn, rsp = _rope_tables(positions)

    w_lat = _weights_to_p("weights_latents", [g_in_1], where, 2)
    proj_lat, h, rstd = _inproj(x2d, g_in, w_lat)
    q, k, v, log_a, pre, cqn, ckvn, rq, rkv, misc = _mla_prep(
        proj_lat, g_q, g_kv, w_uq_p, w_k_p, w_v, w_gate_p, b_gla_gate, rc, rsn, rsp)
    o_mla, lse, (g_in_2,) = _mla_attn_fwd(q, k, v, [w_in_b], [stages[1]])
    w_gla = _weights_to_p("weights_gla", [g_in_1, g_in_2], where, 0)
    proj_gla = _proj("inproj_gla", h, w_gla)
    o_gla, states, (g_in_3, g_pm, g_pg, g_o) = _gla_fwd(
        proj_gla, log_a, [w_in_b, b_pm, b_pg, b_o], [stages[2]] + [everyone] * 3)
    w_out_path = _weights_to_p("weights_out_path", [g_in_1, g_in_2, g_in_3], where, 1)
    proj_out = _proj("inproj_out_path", h, w_out_path)
    w_in_p = (w_gla, w_out_path, w_lat)
    w_pm = _cols_full(g_pm).reshape(MLA_WIDTH, D_MODEL)
    w_pg = g_pg.reshape(GLA_DV, D_MODEL)
    w_o = g_o.reshape(D_MODEL, D_MODEL)

    (dx2, do_mla, do_gla, d_out, merged, um, ug, dym, dyg, loss_p, dg_final,
     dg_gla) = _post(o_mla, proj_out, o_gla, x2d, tgt, g_gla, g_final2, w_pm, w_pg, w_o)

    p_pm = _matmul("dw_proj_mla", um, dym, 512, 512, BF16).reshape(
        MLA_WIDTH, N_DEV, D_MODEL // N_DEV).transpose(1, 0, 2)
    p_pg = _matmul("dw_proj_gla", ug, dyg, 512, 512, BF16).reshape(N_DEV, -1, D_MODEL)
    p_o = _matmul("dw_out", merged, dx2, 512, 512, BF16).reshape(N_DEV, -1, D_MODEL)
    own_in = jnp.zeros((SHARD_PAD, D_MODEL), BF16)
    land_in = lax.empty((PEERS, SHARD_PAD, D_MODEL), BF16)
    dw_groups, started, lands = {}, [], [land_in]

    def reduce_scatter_stage(s, dests, own_in, extra=()):
        parts_in, own_in = _grads_to_shards("grads_to_shards_%d" % s, dw_groups, dests, own_in)
        first = len(lands)
        lands.extend(lax.empty((PEERS,) + p.shape[1:], BF16) for p in extra)
        idx = [0] + list(range(first, len(lands)))
        all_dests = [[(i, d, r0, r1) for i, (d, ranges) in enumerate(dests) for r0, r1 in ranges]]
        all_dests += [_whole(everyone, p.shape[1]) for p in extra]
        sems, parts, new_lands, token = _ici_start(
            "ici_start_%d" % s, [parts_in] + list(extra), [lands[i] for i in idx], all_dests)
        for a, i in enumerate(idx):
            lands[i] = new_lands[a]
            started.append((sems[a][0], sems[a][1], parts[a], i, all_dests[a]))
        return own_in, token

    dw_groups[1] = _matmul("dw_in_1", d_out, h, 512, 512, BF16)
    full = [(0, SHARD_PAD)]
    own_in, token = reduce_scatter_stage(
        1, [(5, full), (6, full), (7, full)], own_in, (p_pm, p_pg, p_o))
    d_gla, dla = _gla_bwd(proj_gla, log_a, do_gla, states, token)
    dw_groups[0] = _matmul("dw_in_0", d_gla, h, 512, 512, BF16)
    own_in, token = reduce_scatter_stage(
        2, [(1, full), (2, full), (3, full), (4, [(0, 64), (96, SHARD_PAD)]),
            (0, [(672, SHARD_PAD)])], own_in)
    dq, dk, dv = _mla_attn_bwd(q, k, v, o_mla, do_mla, lse, token)
    d_lat, dqpre, dpre, dg_q, dg_kv, db_gate = _mla_prep_bwd(
        dq, dk, dv, dla, pre, proj_lat, rq, rkv, g_q, g_kv, w_uq_p, w_k_p, w_v, w_gate_p, rc, rsn, rsp)
    dw_groups[2] = _matmul("dw_in_2", d_lat, h, 896, 512, BF16)
    dw_uq = _matmul("dw_uq", cqn, dqpre, MLA_Q_RANK, 512, BF16)
    p_uq = dw_uq.reshape(MLA_Q_RANK, MLA_HEADS, HEAD_PAD)[:, :, :MLA_QK].transpose(1, 0, 2)
    dw_k = _matmul("dw_uk", ckvn, dk, MLA_KV_RANK, 512, BF16)
    dw_v = _matmul("dw_uv", ckvn, dv, MLA_KV_RANK, 512, BF16)
    p_ukv = jnp.concatenate(
        [dw_k.reshape(MLA_KV_RANK, MLA_HEADS, HEAD_PAD)[:, :, :MLA_NOPE],
         dw_v.reshape(MLA_KV_RANK, MLA_HEADS, MLA_VDIM)], axis=2).transpose(1, 0, 2)
    dw_gate = _matmul("dw_gate", misc, dpre, LANE, 512, BF16)
    p_gate = dw_gate[MISC_ALR:MISC_ALR + GLA_GATE_RANK].reshape(
        GLA_GATE_RANK, N_DEV, GLA_DK // N_DEV).transpose(1, 0, 2)
    own_in, token = reduce_scatter_stage(
        3, [(0, [(0, 672)]), (4, [(64, 96)])], own_in, (p_uq, p_ukv, p_gate))
    grad_x, dg_in = _inproj_bwd((d_gla, d_out, d_lat), w_in_p, x2d, rstd, g_in, dx2, token)
    small = jnp.concatenate([dg_in.reshape(-1), dg_q.reshape(-1), dg_kv.reshape(-1),
                             db_gate.reshape(-1), dg_gla.reshape(-1), dg_final.reshape(-1),
                             loss_p[0, :1]])
    small = jnp.pad(small, (0, SMALL_ROWS * LANE - small.shape[0])).reshape(SMALL_ROWS, LANE)

    (small_all,) = _all_gather("all_gather_small", [small], [everyone])
    lands = _ici_wait("ici_wait", started, lands, small_all)
    big = [_adamw_transposed("adamw_w_in", own_in, lands[0], *sharded[0], 256)]
    big += _adamw_group([p_uq, p_ukv, p_gate, p_pm, p_pg, p_o], list(lands[4:7]) + list(lands[1:4]),
                        *[[s[j] for s in sharded[1:]] for j in range(3)])
    replicated = [(g_in, m_g_in, v_g_in), (g_q, m_g_q, v_g_q), (g_kv, m_g_kv, v_g_kv),
                  (b_gla_gate, m_b_gla_gate, v_b_gla_gate), (g_gla, m_g_gla, v_g_gla),
                  (g_final, m_g_final, v_g_final)]
    spacks = [jnp.pad(jnp.concatenate([s[j].reshape(-1) for s in replicated]),
                      (0, SMALL_ROWS * LANE - sum(SMALL_SIZES))).reshape(1, SMALL_ROWS, LANE)
              for j in range(3)]
    tiny = _adamw_rows("adamw_gains", small_all, spacks[0], spacks[1], spacks[2], SMALL_ROWS)

    outs = {}
    names = ("w_in", "w_uq", "w_ukv", "w_gla_gate", "w_proj_mla", "w_proj_gla", "w_out")
    for j, kind in enumerate(("grad", "delta", "new_m", "new_v")):
        for name, res in zip(names, big):
            outs[kind, name] = res[j]
        flat = tiny[j].reshape(-1)
        off = 0
        for name, size in zip(("g_in", "g_q", "g_kv", "b_gla_gate", "g_gla", "g_final"), SMALL_SIZES):
            shape = (size,) if name == "g_final" else (1, size)
            outs[kind, name] = flat[off:off + size].reshape(shape)
            off += size
    loss = tiny[0].reshape(-1)[sum(SMALL_SIZES)]
    order = ("g_in", "w_in", "g_q", "w_uq", "g_kv", "w_ukv", "w_gla_gate", "b_gla_gate", "g_gla",
             "w_proj_mla", "w_proj_gla", "w_out", "g_final")
    result = [loss, grad_x.reshape(1, t, D_MODEL)]
    for kind in ("grad", "delta", "new_m", "new_v"):
        result += [outs[kind, name] for name in order]
    return tuple(result)
```

```python
import jax
import jax.numpy as jnp
from jax import lax
from jax.experimental import pallas as pl
from jax.experimental.pallas import tpu as pltpu

F32 = jnp.float32
BF16 = jnp.bfloat16
MESH = pl.DeviceIdType.MESH
N_DEV = 8

D_MODEL = 1024
EPS = 1e-6
MLA_HEADS = 8
MLA_NOPE = 64
MLA_ROPE = 32
MLA_VDIM = 64
MLA_Q_RANK = 384
MLA_KV_RANK = 256
MLA_QK = MLA_NOPE + MLA_ROPE
MLA_WIDTH = MLA_HEADS * MLA_VDIM
ROPE_THETA = 10000.0
GLA_HEADS = 4
GLA_DK = 512
GLA_DV = 1024
GLA_HK = 128
GLA_HV = 256
GLA_GATE_RANK = 16
GLA_GATE_NORM = 16.0
GLA_CHUNK = 64
GLA_CHUNKS_PER_STEP = 4
D_IN = 6320

ADAM_LR = 0.001
ADAM_B1 = 0.9
ADAM_B2 = 0.999
ADAM_EPS = 1e-08
ADAM_WD = 0.01
ADAM_STEP = 10

LANE = 128
HEAD_PAD = 128
VMEM_LIMIT = 48 * 1024 * 1024

P_VG, P_QG, P_KG = 0, 1024, 1536
P_ZGLA, P_GMLA, P_GGLA, P_ZMLA = 2048, 3072, 4096, 5120
P_CQ, P_CKV, P_MISC = 5632, 6144, 6400
P_TOTAL = 6528
P_GROUPS = ((0, 2048), (2048, 3584), (5632, 896))
MISC_KR = 64
MISC_ALR = 96
SHARD_COLS = D_IN // N_DEV
SHARD_PAD = 800
P_COMPONENTS = ((0, 384, P_CQ), (384, 256, P_CKV), (640, 32, P_MISC + MISC_KR), (672, 512, P_ZMLA),
                (1184, 512, P_QG), (1696, 512, P_KG), (2208, 1024, P_VG),
                (3232, 16, P_MISC + MISC_ALR), (3248, 1024, P_ZGLA), (4272, 1024, P_GMLA),
                (5296, 1024, P_GGLA))

SMALL_SIZES = (1024, 384, 256, 512, 256, 1024)
SMALL_ROWS = 32


def _segments():
    segs = []
    for g0, n, p0 in P_COMPONENTS:
        g = g0
        while g < g0 + n:
            d = g // SHARD_COLS
            end = min(g0 + n, (d + 1) * SHARD_COLS)
            segs.append((d, g - d * SHARD_COLS, end - g, p0 + g - g0))
            g = end
    return segs


def _group_of(p0):
    return max(i for i, (off, _) in enumerate(P_GROUPS) if off <= p0)


def _rel(p0):
    return p0 - P_GROUPS[_group_of(p0)][0]


def _cparams(sem=None):
    if sem is None:
        return pltpu.CompilerParams(vmem_limit_bytes=VMEM_LIMIT)
    return pltpu.CompilerParams(dimension_semantics=sem, vmem_limit_bytes=VMEM_LIMIT)


def _sigmoid(v):
    return 1.0 / (1.0 + jnp.exp(-v))


def _dot(a, b):
    return jnp.dot(a, b, preferred_element_type=F32)


def _dot_nt(a, b):
    return lax.dot_general(a, b, (((1,), (1,)), ((), ())), preferred_element_type=F32)


def _dot_tn(a, b):
    return lax.dot_general(a, b, (((0,), (0,)), ((), ())), preferred_element_type=F32)


def _dot_exact(a, b):
    return jnp.dot(a, b, preferred_element_type=F32, precision=lax.Precision.HIGHEST)


def _rope_fwd(blk, c, sn, sp):
    return blk * c + pltpu.roll(blk, LANE - 16, 1) * sn + pltpu.roll(blk, 16, 1) * sp


def _rope_bwd(blk, c, sn, sp):
    return blk * c + pltpu.roll(blk * sn, 16, 1) + pltpu.roll(blk * sp, LANE - 16, 1)


def _mesh_pos():
    return lax.axis_index("x"), lax.axis_index("y"), lax.axis_index("c")


def _hbm_specs(n):
    return [pl.BlockSpec(memory_space=pltpu.HBM) for _ in range(n)]


def _dev(d):
    return d >> 2, (d >> 1) & 1, d & 1


def _gather_plan(shards, sources):
    na, most = len(shards), max(len(s) for s in sources)
    out_shape = [jax.ShapeDtypeStruct((len(srcs),) + s.shape, s.dtype)
                 for s, srcs in zip(shards, sources)]
    sems = [pltpu.SemaphoreType.DMA((na, most)) for _ in range(3)]
    sems += [pltpu.SemaphoreType.DMA((na, most, 3))]
    sems += [pltpu.SemaphoreType.DMA((na, most)) for _ in range(3)]
    return out_shape, sems


def _gather_hooks(x_refs, out_refs, sems, sources):
    local_sems, d2d_send, d2d_recv, ici_send, ici_recv, fwd_send, fwd_recv = sems
    x, y, c = _mesh_pos()
    chips = [(1 - x, y), (x, 1 - y), (1 - x, 1 - y)]
    items = []
    for a, srcs in enumerate(sources):
        for i, d in enumerate(srcs):
            dx, dy, dc = _dev(d)
            near = jnp.logical_and(x == dx, y == dy)
            far = jnp.logical_not(near)
            slot = out_refs[a].at[i]

            def remote(src, to, send_sem, recv_sem, slot=slot):
                return pltpu.make_async_remote_copy(
                    src_ref=src, dst_ref=slot, send_sem=send_sem, recv_sem=recv_sem,
                    device_id=to, device_id_type=MESH)

            items.append(dict(
                me=jnp.logical_and(near, c == dc), sibling=jnp.logical_and(near, c != dc),
                relay=jnp.logical_and(far, c == dc), behind=jnp.logical_and(far, c != dc),
                local=pltpu.make_async_copy(x_refs[a], slot, local_sems.at[a, i]),
                to_sibling=remote(x_refs[a], (x, y, 1 - c), d2d_send.at[a, i], d2d_recv.at[a, i]),
                to_chips=[remote(x_refs[a], (*chip, c), ici_send.at[a, i, j], ici_recv.at[a, i])
                          for j, chip in enumerate(chips)],
                forward=remote(slot, (x, y, 1 - c), fwd_send.at[a, i], fwd_recv.at[a, i])))

    def start():
        for it in items:
            @pl.when(it["me"])
            def _(it=it):
                it["local"].start()
                it["to_sibling"].start()
                for cp in it["to_chips"]:
                    cp.start()

    def finish():
        for it in items:
            @pl.when(it["relay"])
            def _(it=it):
                it["to_chips"][0].wait_recv()
                it["forward"].start()
        for it in items:
            pl.when(it["sibling"])(it["to_sibling"].wait_recv)
            pl.when(it["behind"])(it["forward"].wait_recv)
            pl.when(it["relay"])(it["forward"].wait_send)

            @pl.when(it["me"])
            def _(it=it):
                it["local"].wait()
                it["to_sibling"].wait_send()
                for cp in it["to_chips"]:
                    cp.wait_send()

    return start, finish


def _all_gather(name, shards, sources):
    n = len(shards)
    out_shape, sems = _gather_plan(shards, sources)

    def body(*refs):
        start, finish = _gather_hooks(refs[:n], refs[n:2 * n], refs[2 * n:], sources)
        start()
        finish()

    return pl.pallas_call(
        body, name=name,
        out_shape=tuple(out_shape),
        in_specs=_hbm_specs(n), out_specs=tuple(_hbm_specs(n)),
        scratch_shapes=sems,
        compiler_params=_cparams(),
    )(*shards)


PEERS = N_DEV - 1


def _whole(dests, rows):
    return [(i, d, 0, rows) for i, d in enumerate(dests)]


def _ici_copies(p_ref, land_ref, send_sems, recv_sems, pieces):
    x, y, c = _mesh_pos()
    sends, arrivals = [], []

    def rows_of(ref, j, r0, r1):
        return ref.at[j] if (r0, r1) == (0, ref.shape[1]) else ref.at[j, pl.ds(r0, r1 - r0)]

    for p, (i, d, r0, r1) in enumerate(pieces):
        dx, dy, dc = _dev(d)
        k = (4 * (x != dx).astype(jnp.int32) + 2 * (y != dy).astype(jnp.int32)
             + (c != dc).astype(jnp.int32))
        slot = jnp.maximum(k - 1, 0)
        sends.append((k > 0, pltpu.make_async_remote_copy(
            src_ref=rows_of(p_ref, i, r0, r1), dst_ref=rows_of(land_ref, slot, r0, r1),
            send_sem=send_sems.at[p], recv_sem=recv_sems.at[p * PEERS + slot],
            device_id=(dx, dy, dc), device_id_type=MESH)))
        arrivals.append((k == 0, [pltpu.make_async_remote_copy(
            src_ref=rows_of(p_ref, i, r0, r1), dst_ref=rows_of(land_ref, r, r0, r1),
            send_sem=send_sems.at[p], recv_sem=recv_sems.at[p * PEERS + r],
            device_id=(dx, dy, dc), device_id_type=MESH) for r in range(PEERS)]))
    return sends, arrivals


def _ici_start(name, hs, lands, dests):
    na = len(hs)

    def body(*refs):
        h_refs, land_refs, sems = refs[:na], refs[na:2 * na], refs[2 * na:4 * na]
        token = refs[-1]
        for a in range(na):
            sends, _ = _ici_copies(h_refs[a], land_refs[a], sems[2 * a], sems[2 * a + 1], dests[a])
            for go, cp in sends:
                pl.when(go)(cp.start)
        token[...] = jnp.zeros_like(token)

    hbm, sem = pl.BlockSpec(memory_space=pltpu.HBM), pl.BlockSpec(memory_space=pltpu.SEMAPHORE)
    sem_shapes = []
    for a in range(na):
        sem_shapes += [pltpu.SemaphoreType.DMA((len(dests[a]),)),
                       pltpu.SemaphoreType.DMA((len(dests[a]) * PEERS,))]
    res = pl.pallas_call(
        body, name=name,
        out_shape=tuple(sem_shapes) + tuple(pltpu.HBM(v.shape, v.dtype) for v in list(hs) + list(lands))
        + (jax.ShapeDtypeStruct((8, LANE), F32),),
        in_specs=(hbm,) * (2 * na),
        out_specs=(sem,) * (2 * na) + (hbm,) * (2 * na) + (pl.BlockSpec(memory_space=pltpu.VMEM),),
        input_output_aliases={i: 2 * na + i for i in range(2 * na)},
        compiler_params=pltpu.CompilerParams(
            has_side_effects=pltpu.SideEffectType.DATAFLOW_SIDE_EFFECTING,
            vmem_limit_bytes=VMEM_LIMIT),
    )(*[pltpu.with_memory_space_constraint(v, pltpu.HBM) for v in list(hs) + list(lands)])
    sems = [(res[2 * a], res[2 * a + 1]) for a in range(na)]
    return sems, res[2 * na:3 * na], res[3 * na:4 * na], res[-1]


def _ici_wait(name, started, lands, after):
    k, nl = len(started), len(lands)

    def body(*refs):
        land_refs = refs[3 * k:3 * k + nl]
        for s in range(k):
            h_ref, send_sems, recv_sems = refs[3 * s:3 * s + 3]
            sends, arrivals = _ici_copies(h_ref, land_refs[started[s][3]], send_sems, recv_sems,
                                          started[s][4])
            for go, cp in sends:
                pl.when(go)(cp.wait_send)
            for here, cps in arrivals:
                for cp in cps:
                    pl.when(here)(cp.wait_recv)

    hbm, sem = pl.BlockSpec(memory_space=pltpu.HBM), pl.BlockSpec(memory_space=pltpu.SEMAPHORE)
    operands, specs = [], []
    for send_sems, recv_sems, h, _, _ in started:
        operands += [h, send_sems, recv_sems]
        specs += [hbm, sem, sem]
    return pl.pallas_call(
        body, name=name,
        out_shape=tuple(pltpu.HBM(v.shape, v.dtype) for v in lands),
        in_specs=tuple(specs) + (hbm,) * nl + (pl.BlockSpec(memory_space=pl.ANY),),
        out_specs=(hbm,) * nl,
        input_output_aliases={3 * k + i: i for i in range(nl)},
        compiler_params=pltpu.CompilerParams(
            has_side_effects=pltpu.SideEffectType.DATAFLOW_SIDE_EFFECTING,
            vmem_limit_bytes=VMEM_LIMIT),
    )(*operands, *lands, after)


def _weights_to_p(name, gathered, where, group):
    tl = 256
    off, width = P_GROUPS[group]
    segs = sorted([s for s in _segments() if _group_of(s[3]) == group], key=lambda s: s[3])
    used = sorted({where[s[0]][0] for s in segs})

    def body(*refs):
        g_refs, o_ref = dict(zip(used, refs[:-1])), refs[-1]
        pieces, pos = [], off
        for d, c0, n, p0 in segs:
            if p0 > pos:
                pieces.append(jnp.zeros((p0 - pos, tl), F32))
            k, slot = where[d]
            pieces.append(g_refs[k][slot, c0:c0 + n, :].astype(F32))
            pos = p0 + n
        if off + width > pos:
            pieces.append(jnp.zeros((off + width - pos, tl), F32))
        o_ref[...] = jnp.concatenate(pieces, axis=0).astype(BF16)

    return pl.pallas_call(
        body, name=name,
        grid=(D_MODEL // tl,),
        in_specs=[pl.BlockSpec((gathered[k].shape[0], SHARD_COLS, tl), lambda i: (0, 0, i))
                  for k in used],
        out_specs=pl.BlockSpec((width, tl), lambda i: (0, i)),
        out_shape=jax.ShapeDtypeStruct((width, D_MODEL), BF16),
        compiler_params=_cparams(("arbitrary",)),
    )(*[gathered[k] for k in used])


def _shard_groups(d):
    return sorted({_group_of(s[3]) for s in _segments() if s[0] == d})


def _grads_to_shards(name, groups, dests, own_prev):
    tl = 256
    segs = _segments()
    used = sorted(groups)

    def body(*refs):
        g_refs, prev_ref, o_ref, own_ref = dict(zip(used, refs[:-3])), refs[-3], refs[-2], refs[-1]
        x, y, c = _mesh_pos()
        own = prev_ref[...].astype(F32)
        row = lax.broadcasted_iota(jnp.int32, (SHARD_PAD, tl), 0)
        for i, (d, ranges) in enumerate(dests):
            pieces, pos, asked = [], 0, None
            for r0, r1 in sorted(ranges):
                if r0 > pos:
                    pieces.append(jnp.zeros((r0 - pos, tl), F32))
                for _, c0, n, p0 in sorted([s for s in segs if s[0] == d], key=lambda s: s[1]):
                    a, b = max(c0, r0), min(c0 + n, r1)
                    if a < b:
                        gi = _group_of(p0)
                        lo = p0 - P_GROUPS[gi][0] + a - c0
                        pieces.append(g_refs[gi][lo:lo + b - a, :].astype(F32))
                if r1 > SHARD_COLS:
                    pieces.append(jnp.zeros((r1 - max(r0, SHARD_COLS), tl), F32))
                pos = r1
                inside = jnp.logical_and(row >= r0, row < r1)
                asked = inside if asked is None else jnp.logical_or(asked, inside)
            if pos < SHARD_PAD:
                pieces.append(jnp.zeros((SHARD_PAD - pos, tl), F32))
            shard = jnp.concatenate(pieces, axis=0)
            o_ref[i] = shard.astype(BF16)
            own = jnp.where(jnp.logical_and(4 * x + 2 * y + c == d, asked), shard, own)
        own_ref[...] = own.astype(BF16)

    blk = pl.BlockSpec((SHARD_PAD, tl), lambda i: (0, i))
    return pl.pallas_call(
        body, name=name,
        grid=(D_MODEL // tl,),
        in_specs=[pl.BlockSpec((P_GROUPS[g][1], tl), lambda i: (0, i)) for g in used] + [blk],
        out_specs=(pl.BlockSpec((len(dests), SHARD_PAD, tl), lambda i: (0, 0, i)), blk),
        out_shape=(jax.ShapeDtypeStruct((len(dests), SHARD_PAD, D_MODEL), BF16),
                   jax.ShapeDtypeStruct((SHARD_PAD, D_MODEL), BF16)),
        input_output_aliases={len(used): 1},
        compiler_params=_cparams(("arbitrary",)),
    )(*[groups[g] for g in used], own_prev)


def _inproj(x, g_in, w_pt):
    t = x.shape[0]
    tm = min(256, t)
    width = w_pt.shape[0]

    def body(x_ref, g_ref, w_ref, proj_ref, h_ref, r_ref):
        xf = x_ref[...]
        r = lax.rsqrt(jnp.mean(xf * xf, axis=-1, keepdims=True) + EPS)
        h = ((xf * r) * g_ref[...]).astype(BF16)
        proj_ref[...] = _dot_nt(h, w_ref[...])
        h_ref[...] = h
        r_ref[...] = r

    row = lambda w: pl.BlockSpec((tm, w), lambda i: (i, 0))
    return pl.pallas_call(
        body, name="inproj_latents",
        grid=(t // tm,),
        in_specs=[row(D_MODEL), pl.BlockSpec((1, D_MODEL), lambda i: (0, 0)),
                  pl.BlockSpec((width, D_MODEL), lambda i: (0, 0))],
        out_specs=(row(width), row(D_MODEL), row(1)),
        out_shape=(jax.ShapeDtypeStruct((t, width), F32),
                   jax.ShapeDtypeStruct((t, D_MODEL), BF16),
                   jax.ShapeDtypeStruct((t, 1), F32)),
        compiler_params=_cparams(("arbitrary",)),
    )(x, g_in, w_pt)


def _proj(name, h, w_pt):
    t = h.shape[0]
    tm = min(256, t)
    width = w_pt.shape[0]

    def body(h_ref, w_ref, o_ref):
        o_ref[...] = _dot_nt(h_ref[...], w_ref[...])

    return pl.pallas_call(
        body, name=name,
        grid=(t // tm,),
        in_specs=[pl.BlockSpec((tm, D_MODEL), lambda i: (i, 0)),
                  pl.BlockSpec((width, D_MODEL), lambda i: (0, 0))],
        out_specs=pl.BlockSpec((tm, width), lambda i: (i, 0)),
        out_shape=jax.ShapeDtypeStruct((t, width), F32),
        compiler_params=_cparams(("arbitrary",)),
    )(h, w_pt)


def _mla_prep(proj, g_q, g_kv, w_uq_p, w_k_p, w_v, w_gate_p, b_gate, rc, rsn, rsp):
    t = proj.shape[0]
    tm = min(256, t)
    hq = MLA_HEADS * HEAD_PAD

    def body(cq_ref, ckv_ref, misc_ref, gq_ref, gkv_ref, wuq_ref, wk_ref, wv_ref, wg_ref, bg_ref,
             c_ref, sn_ref, sp_ref,
             q_ref, k_ref, v_ref, la_ref, pre_ref, cqn_ref, ckvn_ref, rq_ref, rkv_ref, mb_ref):
        c, sn, sp = c_ref[...], sn_ref[...], sp_ref[...]
        cq = cq_ref[:, :MLA_Q_RANK]
        rq = lax.rsqrt(jnp.mean(cq * cq, axis=-1, keepdims=True) + EPS)
        cqn = ((cq * rq) * gq_ref[...]).astype(BF16)
        cqn_ref[...] = cqn
        rq_ref[...] = rq
        qpre = _dot(cqn, wuq_ref[...])
        ckv = ckv_ref[...]
        rkv = lax.rsqrt(jnp.mean(ckv * ckv, axis=-1, keepdims=True) + EPS)
        ckvn = ((ckv * rkv) * gkv_ref[...]).astype(BF16)
        ckvn_ref[...] = ckvn
        rkv_ref[...] = rkv
        kn = _dot(ckvn, wk_ref[...])
        v_ref[...] = _dot(ckvn, wv_ref[...]).astype(BF16)
        misc = misc_ref[...]
        krope = _rope_fwd(misc, c, sn, sp)
        for h in range(MLA_HEADS):
            sl = slice(h * HEAD_PAD, (h + 1) * HEAD_PAD)
            q_ref[:, sl] = _rope_fwd(qpre[:, sl], c, sn, sp).astype(BF16)
            k_ref[:, sl] = (kn[:, sl] + krope).astype(BF16)
        mb_ref[...] = misc.astype(BF16)
        pre = _dot(mb_ref[...], wg_ref[...]) + bg_ref[...]
        pre_ref[...] = pre
        log_a = (jnp.minimum(pre, 0.0) - jnp.log(1.0 + jnp.exp(-jnp.abs(pre)))) / GLA_GATE_NORM
        la_ref[...] = _dot_exact(_chunk_tri(tm, True), log_a)

    row = lambda w: pl.BlockSpec((tm, w), lambda i: (i, 0))
    full = lambda a: pl.BlockSpec(a.shape, lambda i: (0, 0))
    return pl.pallas_call(
        body, name="mla_prep",
        grid=(t // tm,),
        in_specs=[pl.BlockSpec((tm, 512), lambda i: (i, _rel(P_CQ) // 512)),
                  pl.BlockSpec((tm, MLA_KV_RANK), lambda i: (i, _rel(P_CKV) // MLA_KV_RANK)),
                  pl.BlockSpec((tm, LANE), lambda i: (i, _rel(P_MISC) // LANE)),
                  full(g_q), full(g_kv), full(w_uq_p), full(w_k_p), full(w_v), full(w_gate_p),
                  full(b_gate), row(LANE), row(LANE), row(LANE)],
        out_specs=(row(hq), row(hq), row(MLA_WIDTH), row(GLA_DK), row(GLA_DK),
                   row(MLA_Q_RANK), row(MLA_KV_RANK), row(1), row(1), row(LANE)),
        out_shape=(jax.ShapeDtypeStruct((t, hq), BF16), jax.ShapeDtypeStruct((t, hq), BF16),
                   jax.ShapeDtypeStruct((t, MLA_WIDTH), BF16),
                   jax.ShapeDtypeStruct((t, GLA_DK), F32), jax.ShapeDtypeStruct((t, GLA_DK), F32),
                   jax.ShapeDtypeStruct((t, MLA_Q_RANK), BF16),
                   jax.ShapeDtypeStruct((t, MLA_KV_RANK), BF16),
                   jax.ShapeDtypeStruct((t, 1), F32), jax.ShapeDtypeStruct((t, 1), F32),
                   jax.ShapeDtypeStruct((t, LANE), BF16)),
        compiler_params=_cparams(("arbitrary",)),
    )(proj, proj, proj, g_q, g_kv, w_uq_p, w_k_p, w_v, w_gate_p, b_gate, rc, rsn, rsp)


def _attn_masks(tq, i):
    keys = (i + 1) * tq
    rows = i * tq + lax.broadcasted_iota(jnp.int32, (tq, keys), 0)
    cols = lax.broadcasted_iota(jnp.int32, (tq, keys), 1)
    lane = lax.broadcasted_iota(jnp.int32, (tq, LANE), 1)
    return cols <= rows, lane < MLA_VDIM


def _for_each_query_tile(n_tiles, fn):
    for i in range(n_tiles):
        pl.when(pl.program_id(1) == i)(lambda i=i: fn(i))


def _mla_attn_fwd(q, k, v, shards, sources):
    t = q.shape[0]
    tq = min(256, t)
    scale = MLA_QK ** -0.5
    ns = len(shards)
    g_shapes, g_sems = _gather_plan(shards, sources)
    grid = (MLA_HEADS // 2, t // tq)

    def body(q_ref, k_ref, v_ref, *rest):
        o_ref, lse_ref = rest[ns:ns + 2]
        start, finish = _gather_hooks(rest[:ns], rest[ns + 2:2 * ns + 2], rest[2 * ns + 2:], sources)
        step = pl.program_id(0) * grid[1] + pl.program_id(1)
        pl.when(step == 0)(start)

        def tile(i):
            keys = (i + 1) * tq
            causal, low = _attn_masks(tq, i)
            vp = v_ref[0:keys, :]
            acc = jnp.zeros((tq, LANE), F32)
            for hh in range(2):
                sl = slice(hh * HEAD_PAD, (hh + 1) * HEAD_PAD)
                s = _dot_nt(q_ref[:, sl], k_ref[0:keys, sl]) * scale
                s = jnp.where(causal, s, -jnp.inf)
                m = jnp.max(s, axis=-1, keepdims=True)
                e = jnp.exp(s - m)
                l = jnp.sum(e, axis=-1, keepdims=True)
                o = _dot(e.astype(BF16), vp) / l
                acc = jnp.where(low if hh == 0 else jnp.logical_not(low), o, acc)
                lse_ref[hh] = m + jnp.log(l)
            o_ref[...] = acc

        _for_each_query_tile(t // tq, tile)
        pl.when(step == grid[0] * grid[1] - 1)(finish)

    res = pl.pallas_call(
        body, name="mla_attn_fwd",
        grid=grid,
        in_specs=[pl.BlockSpec((tq, 2 * HEAD_PAD), lambda p, i: (i, p)),
                  pl.BlockSpec((t, 2 * HEAD_PAD), lambda p, i: (0, p)),
                  pl.BlockSpec((t, LANE), lambda p, i: (0, p))] + _hbm_specs(ns),
        out_specs=(pl.BlockSpec((tq, LANE), lambda p, i: (i, p)),
                   pl.BlockSpec((2, tq, 1), lambda p, i: (p, i, 0))) + tuple(_hbm_specs(ns)),
        out_shape=(jax.ShapeDtypeStruct((t, MLA_WIDTH), F32),
                   jax.ShapeDtypeStruct((MLA_HEADS, t, 1), F32)) + tuple(g_shapes),
        scratch_shapes=g_sems,
        compiler_params=_cparams(("arbitrary", "arbitrary")),
    )(q, k, v, *shards)
    return res[0], res[1], res[2:]


def _mla_attn_bwd(q, k, v, o, do, lse, after):
    t = q.shape[0]
    tq = min(256, t)
    scale = MLA_QK ** -0.5

    def body(q_ref, k_ref, v_ref, o_ref, do_ref, lse_ref, after_ref, dq_ref, dk_ref, dv_ref):
        del after_ref

        @pl.when(pl.program_id(1) == 0)
        def _():
            dk_ref[...] = jnp.zeros_like(dk_ref)
            dv_ref[...] = jnp.zeros_like(dv_ref)

        def tile(i):
            keys = (i + 1) * tq
            causal, low = _attn_masks(tq, i)
            vp = v_ref[0:keys, :]
            do_all = do_ref[...]
            o_all = o_ref[...]
            dv_acc = jnp.zeros((keys, LANE), F32)
            for hh in range(2):
                sl = slice(hh * HEAD_PAD, (hh + 1) * HEAD_PAD)
                do_h = jnp.where(low if hh == 0 else jnp.logical_not(low), do_all, 0.0)
                dsum = jnp.sum(do_h * o_all, axis=-1, keepdims=True)
                qh = q_ref[:, sl]
                kh = k_ref[0:keys, sl]
                s = _dot_nt(qh, kh) * scale
                p = jnp.where(causal, jnp.exp(s - lse_ref[hh]), 0.0)
                do_b = do_h.astype(BF16)
                dp = _dot_nt(do_b, vp)
                ds = (p * (dp - dsum) * scale).astype(BF16)
                dq_ref[:, sl] = _dot(ds, kh).astype(BF16)
                dk_ref[0:keys, sl] += _dot_tn(ds, qh)
                dv_acc = dv_acc + _dot_tn(p.astype(BF16), do_b)
            dv_ref[0:keys, :] += dv_acc

        _for_each_query_tile(t // tq, tile)

    return pl.pallas_call(
        body, name="mla_attn_bwd",
        grid=(MLA_HEADS // 2, t // tq),
        in_specs=[pl.BlockSpec((tq, 2 * HEAD_PAD), lambda p, i: (i, p)),
                  pl.BlockSpec((t, 2 * HEAD_PAD), lambda p, i: (0, p)),
                  pl.BlockSpec((t, LANE), lambda p, i: (0, p)),
                  pl.BlockSpec((tq, LANE), lambda p, i: (i, p)),
                  pl.BlockSpec((tq, LANE), lambda p, i: (i, p)),
                  pl.BlockSpec((2, tq, 1), lambda p, i: (p, i, 0)),
                  pl.BlockSpec(memory_space=pl.ANY)],
        out_specs=(pl.BlockSpec((tq, 2 * HEAD_PAD), lambda p, i: (i, p)),
                   pl.BlockSpec((t, 2 * HEAD_PAD), lambda p, i: (0, p)),
                   pl.BlockSpec((t, LANE), lambda p, i: (0, p))),
        out_shape=(jax.ShapeDtypeStruct((t, MLA_HEADS * HEAD_PAD), BF16),
                   jax.ShapeDtypeStruct((t, MLA_HEADS * HEAD_PAD), F32),
                   jax.ShapeDtypeStruct((t, MLA_WIDTH), F32)),
        compiler_params=_cparams(("arbitrary", "arbitrary")),
    )(q, k, v, o, do, lse, after)


def _chunk_tri(n, lower):
    r = lax.broadcasted_iota(jnp.int32, (n, n), 0)
    c = lax.broadcasted_iota(jnp.int32, (n, n), 1)
    same = (r // GLA_CHUNK) == (c // GLA_CHUNK)
    return jnp.where(jnp.logical_and(same, r >= c if lower else r <= c), 1.0, 0.0).astype(F32)


def _gla_chunk_terms(q_ref, k_ref, b_ref, h, rows):
    sl = slice(h * GLA_HK, (h + 1) * GLA_HK)
    b = b_ref[rows, sl]
    bl = b[GLA_CHUNK - 1:GLA_CHUNK, :]
    kc = k_ref[rows, sl]
    q_in = (q_ref[rows, sl] * (GLA_HK ** -0.5)) * jnp.exp(b)
    k_in = kc * jnp.exp(-b)
    k_st = kc * jnp.exp(bl - b)
    return b, bl, q_in, k_in, k_st


def _tri(c, lower):
    r = lax.broadcasted_iota(jnp.int32, (c, c), 0)
    cc = lax.broadcasted_iota(jnp.int32, (c, c), 1)
    return jnp.where(r >= cc if lower else r <= cc, 1.0, 0.0).astype(F32)


def _gla_fwd(proj, log_a):
    t = proj.shape[0]
    per = GLA_CHUNKS_PER_STEP
    n = t // GLA_CHUNK
    c = GLA_CHUNK * per

    def body(q_ref, k_ref, v_ref, la_ref, o_ref, sp_ref, st_ref):
        @pl.when(pl.program_id(0) == 0)
        def _():
            st_ref[...] = jnp.zeros_like(st_ref)

        tri = _tri(GLA_CHUNK, True)
        for s, h in [(s, h) for s in range(per) for h in range(GLA_HEADS)]:
            rows = slice(s * GLA_CHUNK, (s + 1) * GLA_CHUNK)
            _, bl, q_in, k_in, k_st = _gla_chunk_terms(q_ref, k_ref, la_ref, h, rows)
            vs = slice(h * GLA_HV, (h + 1) * GLA_HV)
            vv = v_ref[rows, vs].astype(BF16)
            qb = q_in.astype(BF16)
            attn = _dot_nt(qb, k_in.astype(BF16)) * tri
            st = st_ref[h]
            sp_ref[s, h] = st
            o_ref[rows, vs] = _dot(attn.astype(BF16), vv) + _dot_nt(qb, st.astype(BF16))
            st_ref[h] = st * jnp.exp(bl) + _dot_tn(vv, k_st.astype(BF16))

    return pl.pallas_call(
        body, name="gla_fwd",
        grid=(n // per,),
        in_specs=[pl.BlockSpec((c, GLA_DK), lambda i: (i, P_QG // GLA_DK)),
                  pl.BlockSpec((c, GLA_DK), lambda i: (i, P_KG // GLA_DK)),
                  pl.BlockSpec((c, GLA_DV), lambda i: (i, P_VG // GLA_DV)),
                  pl.BlockSpec((c, GLA_DK), lambda i: (i, 0))],
        out_specs=(pl.BlockSpec((c, GLA_DV), lambda i: (i, 0)),
                   pl.BlockSpec((per, GLA_HEADS, GLA_HV, GLA_HK), lambda i: (i, 0, 0, 0))),
        out_shape=(jax.ShapeDtypeStruct((t, GLA_DV), F32),
                   jax.ShapeDtypeStruct((n, GLA_HEADS, GLA_HV, GLA_HK), F32)),
        scratch_shapes=[pltpu.VMEM((GLA_HEADS, GLA_HV, GLA_HK), F32)],
        compiler_params=_cparams(("arbitrary",)),
    )(proj, proj, proj, log_a)


def _gla_bwd(proj, log_a, do, states, after):
    t = proj.shape[0]
    per = GLA_CHUNKS_PER_STEP
    c = GLA_CHUNK * per
    n = t // c

    def body(q_ref, k_ref, v_ref, la_ref, do_ref, sp_ref, after_ref, dg_ref, dla_ref, ds_ref):
        del after_ref

        @pl.when(pl.program_id(0) == 0)
        def _():
            ds_ref[...] = jnp.zeros_like(ds_ref)

        tri = _tri(GLA_CHUNK, True)
        last = lax.broadcasted_iota(jnp.int32, (GLA_CHUNK, GLA_HK), 0) == GLA_CHUNK - 1
        for s, h in [(s, h) for s in reversed(range(per)) for h in range(GLA_HEADS)]:
            rows = slice(s * GLA_CHUNK, (s + 1) * GLA_CHUNK)
            b, bl, q_in, k_in, k_st = _gla_chunk_terms(q_ref, k_ref, la_ref, h, rows)
            ks_ = slice(h * GLA_HK, (h + 1) * GLA_HK)
            vs = slice(h * GLA_HV, (h + 1) * GLA_HV)
            vv = v_ref[rows, vs].astype(BF16)
            do_h = do_ref[rows, vs]
            qb, kb, ksb = q_in.astype(BF16), k_in.astype(BF16), k_st.astype(BF16)
            attn = (_dot_nt(qb, kb) * tri).astype(BF16)
            st = sp_ref[s, h]
            dst = ds_ref[h]
            dstb = dst.astype(BF16)
            dattn = (_dot_nt(do_h, vv) * tri).astype(BF16)
            dg_ref[rows, P_VG + h * GLA_HV:P_VG + (h + 1) * GLA_HV] = (
                _dot_tn(attn, do_h) + _dot_nt(ksb, dstb)).astype(BF16)
            dq_in = _dot(dattn, kb) + _dot(do_h, st.astype(BF16))
            dk_in = _dot_tn(dattn, qb)
            dk_st = _dot(vv, dstb)
            ebl = jnp.exp(bl)
            d_ebl = jnp.sum(st * dst, axis=0, keepdims=True)
            ds_ref[h] = _dot_tn(do_h, qb) + dst * ebl
            dg_ref[rows, P_QG + h * GLA_HK:P_QG + (h + 1) * GLA_HK] = (
                dq_in * (GLA_HK ** -0.5) * jnp.exp(b)).astype(BF16)
            dg_ref[rows, P_KG + h * GLA_HK:P_KG + (h + 1) * GLA_HK] = (
                dk_in * jnp.exp(-b) + dk_st * jnp.exp(bl - b)).astype(BF16)
            db = dq_in * q_in - dk_in * k_in - dk_st * k_st
            dbl = jnp.sum(dk_st * k_st, axis=0, keepdims=True) + d_ebl * ebl
            dla_ref[rows, ks_] = db + jnp.where(last, dbl, 0.0)

    rev = lambda i: n - 1 - i
    gw = P_GROUPS[0][1]
    return pl.pallas_call(
        body, name="gla_bwd",
        grid=(n,),
        in_specs=[pl.BlockSpec((c, GLA_DK), lambda i: (rev(i), P_QG // GLA_DK)),
                  pl.BlockSpec((c, GLA_DK), lambda i: (rev(i), P_KG // GLA_DK)),
                  pl.BlockSpec((c, GLA_DV), lambda i: (rev(i), P_VG // GLA_DV)),
                  pl.BlockSpec((c, GLA_DK), lambda i: (rev(i), 0)),
                  pl.BlockSpec((c, GLA_DV), lambda i: (rev(i), 0)),
                  pl.BlockSpec((per, GLA_HEADS, GLA_HV, GLA_HK), lambda i: (rev(i), 0, 0, 0)),
                  pl.BlockSpec(memory_space=pl.ANY)],
        out_specs=(pl.BlockSpec((c, gw), lambda i: (rev(i), 0)),
                   pl.BlockSpec((c, GLA_DK), lambda i: (rev(i), 0))),
        out_shape=(jax.ShapeDtypeStruct((t, gw), BF16), jax.ShapeDtypeStruct((t, GLA_DK), F32)),
        scratch_shapes=[pltpu.VMEM((GLA_HEADS, GLA_HV, GLA_HK), F32)],
        compiler_params=_cparams(("arbitrary",)),
    )(proj, proj, proj, log_a, do, states, after)


def _post(o_mla, proj, o_gla, x, target, g_gla, g_final, w_pm, w_pg, w_o):
    t = x.shape[0]
    tm = min(128, t)
    g0, gw = P_GROUPS[1]

    def body(om_ref, zg_ref, gm_ref, gg_ref, zm_ref, og_ref, x_ref, tg_ref, ggla_ref, gf_ref,
             wpm_ref, wpg_ref, wo_ref,
             dx2_ref, dom_ref, dog_ref, dg_ref,
             mg_ref, um_ref, ug_ref, dym_ref, dyg_ref, loss_ref, dgf_ref, dggla_ref):
        @pl.when(pl.program_id(0) == 0)
        def _():
            loss_ref[...] = jnp.zeros_like(loss_ref)
            dgf_ref[...] = jnp.zeros_like(dgf_ref)
            dggla_ref[...] = jnp.zeros_like(dggla_ref)

        om = om_ref[...]
        zm = zm_ref[...]
        sm = _sigmoid(zm)
        silu_m = zm * sm
        um = (om * silu_m).astype(BF16)
        um_ref[...] = um
        ym = _dot(um, wpm_ref[...])

        ggla = ggla_ref[...]
        zg = zg_ref[...]
        sg = _sigmoid(zg)
        silu_g = zg * sg
        xhat, rstd, on = [], [], []
        for h in range(GLA_HEADS):
            blk = og_ref[:, h * GLA_HV:(h + 1) * GLA_HV]
            r = lax.rsqrt(jnp.mean(blk * blk, axis=-1, keepdims=True) + EPS)
            xhat.append(blk * r)
            rstd.append(r)
            on.append(xhat[h] * ggla)
        on = jnp.concatenate(on, axis=-1)
        ug = (on * silu_g).astype(BF16)
        ug_ref[...] = ug
        yg = _dot(ug, wpg_ref[...])

        sgm = _sigmoid(gm_ref[...])
        sgg = _sigmoid(gg_ref[...])
        merged = (sgm * ym + sgg * yg).astype(BF16)
        mg_ref[...] = merged
        x2 = x_ref[...] + _dot(merged, wo_ref[...])
        gf = gf_ref[...]
        rf = lax.rsqrt(jnp.mean(x2 * x2, axis=-1, keepdims=True) + EPS)
        xh = x2 * rf
        err = xh * gf - tg_ref[...]
        loss_ref[...] += 0.5 * jnp.sum(jnp.mean(err * err, axis=-1, keepdims=True))

        dy = err * (1.0 / D_MODEL)
        dgf_ref[...] += jnp.sum(dy * xh, axis=0, keepdims=True)
        dxh = dy * gf
        dx2 = rf * (dxh - xh * jnp.mean(dxh * xh, axis=-1, keepdims=True))
        dx2_ref[...] = dx2
        dmerged = _dot_nt(dx2.astype(BF16), wo_ref[...])
        dym = (dmerged * sgm).astype(BF16)
        dyg = (dmerged * sgg).astype(BF16)
        dym_ref[...] = dym
        dyg_ref[...] = dyg
        dg_ref[:, P_GMLA - g0:P_GMLA - g0 + D_MODEL] = (dmerged * ym * sgm * (1.0 - sgm)).astype(BF16)
        dg_ref[:, P_GGLA - g0:P_GGLA - g0 + D_MODEL] = (dmerged * yg * sgg * (1.0 - sgg)).astype(BF16)
        dum = _dot_nt(dym, wpm_ref[...])
        dom_ref[...] = dum * silu_m
        dg_ref[:, P_ZMLA - g0:P_ZMLA - g0 + MLA_WIDTH] = (
            dum * om * (sm * (1.0 + zm * (1.0 - sm)))).astype(BF16)
        dug = _dot_nt(dyg, wpg_ref[...])
        dg_ref[:, P_ZGLA - g0:P_ZGLA - g0 + GLA_DV] = (
            dug * on * (sg * (1.0 + zg * (1.0 - sg)))).astype(BF16)
        don = dug * silu_g
        dggla = jnp.zeros((1, GLA_HV), F32)
        for h in range(GLA_HEADS):
            hs = slice(h * GLA_HV, (h + 1) * GLA_HV)
            don_h = don[:, hs]
            dggla = dggla + jnp.sum(don_h * xhat[h], axis=0, keepdims=True)
            dxh_h = don_h * ggla
            dog_ref[:, hs] = (rstd[h] * (dxh_h - xhat[h] * jnp.mean(dxh_h * xhat[h], axis=-1,
                                                                     keepdims=True))).astype(BF16)
        dggla_ref[...] += dggla

    row = lambda w: pl.BlockSpec((tm, w), lambda i: (i, 0))
    pcol = lambda w, off: pl.BlockSpec((tm, w), lambda i: (i, _rel(off) // w))
    full = lambda a: pl.BlockSpec(a.shape, lambda i: (0, 0))
    sds = jax.ShapeDtypeStruct
    return pl.pallas_call(
        body, name="post_fwd_bwd",
        grid=(t // tm,),
        in_specs=[row(MLA_WIDTH), pcol(GLA_DV, P_ZGLA), pcol(D_MODEL, P_GMLA), pcol(D_MODEL, P_GGLA),
                  pcol(MLA_WIDTH, P_ZMLA), row(GLA_DV), row(D_MODEL), row(D_MODEL),
                  full(g_gla), full(g_final), full(w_pm), full(w_pg), full(w_o)],
        out_specs=(row(D_MODEL), row(MLA_WIDTH), row(GLA_DV), row(gw),
                   row(D_MODEL), row(MLA_WIDTH), row(GLA_DV), row(D_MODEL), row(D_MODEL),
                   pl.BlockSpec((1, LANE), lambda i: (0, 0)),
                   pl.BlockSpec((1, D_MODEL), lambda i: (0, 0)),
                   pl.BlockSpec((1, GLA_HV), lambda i: (0, 0))),
        out_shape=(sds((t, D_MODEL), F32), sds((t, MLA_WIDTH), F32), sds((t, GLA_DV), BF16),
                   sds((t, gw), BF16),
                   sds((t, D_MODEL), BF16), sds((t, MLA_WIDTH), BF16), sds((t, GLA_DV), BF16),
                   sds((t, D_MODEL), BF16), sds((t, D_MODEL), BF16),
                   sds((1, LANE), F32), sds((1, D_MODEL), F32), sds((1, GLA_HV), F32)),
        compiler_params=_cparams(("arbitrary",)),
    )(o_mla, proj, proj, proj, proj, o_gla, x, target, g_gla, g_final, w_pm, w_pg, w_o)


def _mla_prep_bwd(dq, dk, dv, dla, pre, proj, rq, rkv, g_q, g_kv, w_uq_p, w_k_p, w_v, w_gate_p,
                  rc, rsn, rsp):
    t = proj.shape[0]
    tm = min(256, t)
    gw = P_GROUPS[2][1]

    def body(dq_ref, dk_ref, dv_ref, dla_ref, pre_ref, cq_ref, ckv_ref, rq_ref, rkv_ref,
             gq_ref, gkv_ref, wuq_ref, wk_ref, wv_ref, wg_ref, c_ref, sn_ref, sp_ref,
             dg_ref, dqpre_ref, dpre_ref, dgq_ref, dgkv_ref, dbg_ref):
        @pl.when(pl.program_id(0) == 0)
        def _():
            dgq_ref[...] = jnp.zeros_like(dgq_ref)
            dgkv_ref[...] = jnp.zeros_like(dgkv_ref)
            dbg_ref[...] = jnp.zeros_like(dbg_ref)

        c, sn, sp = c_ref[...], sn_ref[...], sp_ref[...]
        dkr = jnp.zeros((tm, LANE), F32)
        for h in range(MLA_HEADS):
            sl = slice(h * HEAD_PAD, (h + 1) * HEAD_PAD)
            dqpre_ref[:, sl] = _rope_bwd(dq_ref[:, sl].astype(F32), c, sn, sp).astype(BF16)
            dkr = dkr + dk_ref[:, sl]
        dcqn = _dot_nt(dqpre_ref[...], wuq_ref[...])
        rq = rq_ref[...]
        xh = cq_ref[:, :MLA_Q_RANK] * rq
        dgq_ref[...] += jnp.sum(dcqn * xh, axis=0, keepdims=True)
        dxh = dcqn * gq_ref[...]
        dcq = rq * (dxh - xh * jnp.mean(dxh * xh, axis=-1, keepdims=True))
        dg_ref[:, :MLA_Q_RANK] = dcq.astype(BF16)
        dg_ref[:, MLA_Q_RANK:512] = jnp.zeros((tm, 512 - MLA_Q_RANK), BF16)

        dckvn = _dot_nt(dk_ref[...].astype(BF16), wk_ref[...]) + \
            _dot_nt(dv_ref[...].astype(BF16), wv_ref[...])
        rkv = rkv_ref[...]
        xh = ckv_ref[...] * rkv
        dgkv_ref[...] += jnp.sum(dckvn * xh, axis=0, keepdims=True)
        dxh = dckvn * gkv_ref[...]
        dg_ref[:, P_CKV - P_CQ:P_CKV - P_CQ + MLA_KV_RANK] = (
            rkv * (dxh - xh * jnp.mean(dxh * xh, axis=-1, keepdims=True))).astype(BF16)

        dlog_a = _dot_exact(_chunk_tri(tm, False), dla_ref[...])
        dpre = dlog_a * (1.0 / GLA_GATE_NORM) * (1.0 - _sigmoid(pre_ref[...]))
        dbg_ref[...] += jnp.sum(dpre, axis=0, keepdims=True)
        dpre = dpre.astype(BF16)
        dpre_ref[...] = dpre
        lane = lax.broadcasted_iota(jnp.int32, (tm, LANE), 1)
        in_kr = jnp.logical_and(lane >= MISC_KR, lane < MISC_KR + MLA_ROPE)
        dmisc = jnp.where(in_kr, _rope_bwd(dkr, c, sn, sp), 0.0) + _dot_nt(dpre, wg_ref[...])
        dg_ref[:, P_MISC - P_CQ:P_MISC - P_CQ + LANE] = dmisc.astype(BF16)

    hq = MLA_HEADS * HEAD_PAD
    row = lambda w: pl.BlockSpec((tm, w), lambda i: (i, 0))
    full = lambda a: pl.BlockSpec(a.shape, lambda i: (0, 0))
    acc = lambda w: pl.BlockSpec((1, w), lambda i: (0, 0))
    sds = jax.ShapeDtypeStruct
    return pl.pallas_call(
        body, name="mla_prep_bwd",
        grid=(t // tm,),
        in_specs=[row(hq), row(hq), row(MLA_WIDTH), row(GLA_DK), row(GLA_DK),
                  pl.BlockSpec((tm, 512), lambda i: (i, _rel(P_CQ) // 512)),
                  pl.BlockSpec((tm, MLA_KV_RANK), lambda i: (i, _rel(P_CKV) // MLA_KV_RANK)),
                  row(1), row(1), full(g_q), full(g_kv), full(w_uq_p), full(w_k_p), full(w_v),
                  full(w_gate_p), row(LANE), row(LANE), row(LANE)],
        out_specs=(row(gw), row(hq), row(GLA_DK),
                   acc(MLA_Q_RANK), acc(MLA_KV_RANK), acc(GLA_DK)),
        out_shape=(sds((t, gw), BF16), sds((t, hq), BF16), sds((t, GLA_DK), BF16),
                   sds((1, MLA_Q_RANK), F32), sds((1, MLA_KV_RANK), F32), sds((1, GLA_DK), F32)),
        compiler_params=_cparams(("arbitrary",)),
    )(dq, dk, dv, dla, pre, proj, proj, rq, rkv, g_q, g_kv, w_uq_p, w_k_p, w_v, w_gate_p,
      rc, rsn, rsp)


def _inproj_bwd(dgroups, w_pts, x, rstd, g_in, dx2, after):
    t = x.shape[0]
    tm = min(256, t)

    def body(d0_ref, d1_ref, d2_ref, w0_ref, w1_ref, w2_ref, x_ref, r_ref, g_ref, dx2_ref, after_ref,
             dx_ref, dg_ref):
        del after_ref

        @pl.when(pl.program_id(0) == 0)
        def _():
            dg_ref[...] = jnp.zeros_like(dg_ref)

        dh = jnp.zeros((tm, D_MODEL), F32)
        for d_ref, w_ref in zip((d0_ref, d1_ref, d2_ref), (w0_ref, w1_ref, w2_ref)):
            dh = dh + _dot(d_ref[...], w_ref[...])
        r = r_ref[...]
        xh = x_ref[...] * r
        dg_ref[...] += jnp.sum(dh * xh, axis=0, keepdims=True)
        dxh = dh * g_ref[...]
        dx_ref[...] = dx2_ref[...] + r * (dxh - xh * jnp.mean(dxh * xh, axis=-1, keepdims=True))

    row = lambda w: pl.BlockSpec((tm, w), lambda i: (i, 0))
    return pl.pallas_call(
        body, name="inproj_bwd",
        grid=(t // tm,),
        in_specs=[row(w) for _, w in P_GROUPS]
        + [pl.BlockSpec((w, D_MODEL), lambda i: (0, 0)) for _, w in P_GROUPS]
        + [row(D_MODEL), row(1), pl.BlockSpec((1, D_MODEL), lambda i: (0, 0)), row(D_MODEL),
           pl.BlockSpec(memory_space=pl.ANY)],
        out_specs=(row(D_MODEL), pl.BlockSpec((1, D_MODEL), lambda i: (0, 0))),
        out_shape=(jax.ShapeDtypeStruct((t, D_MODEL), F32),
                   jax.ShapeDtypeStruct((1, D_MODEL), F32)),
        compiler_params=_cparams(("arbitrary",)),
    )(*dgroups, *w_pts, x, rstd, g_in, dx2, after)


def _matmul(name, a, b, tm, tn, dtype=F32):
    kk, m = a.shape
    n = b.shape[1]

    def body(a_ref, b_ref, o_ref):
        o_ref[...] = _dot_tn(a_ref[...].astype(BF16), b_ref[...].astype(BF16)).astype(dtype)

    return pl.pallas_call(
        body, name=name,
        grid=(n // tn, m // tm),
        in_specs=[pl.BlockSpec((kk, tm), lambda j, i: (0, i)),
                  pl.BlockSpec((kk, tn), lambda j, i: (0, j))],
        out_specs=pl.BlockSpec((tm, tn), lambda j, i: (i, j)),
        out_shape=jax.ShapeDtypeStruct((m, n), dtype),
        compiler_params=_cparams(("arbitrary", "arbitrary")),
    )(a, b)


def _adamw_update(part_refs, w_ref, m_ref, v_ref, g_ref, d_ref, nm_ref, nv_ref):
    g = part_refs[0][...].astype(F32)
    for p_ref in part_refs[1:]:
        g = g + p_ref[...].astype(F32)
    m_new = ADAM_B1 * m_ref[...] + (1.0 - ADAM_B1) * g
    v_new = ADAM_B2 * v_ref[...] + (1.0 - ADAM_B2) * (g * g)
    m_hat = m_new / (1.0 - ADAM_B1 ** ADAM_STEP)
    v_hat = v_new / (1.0 - ADAM_B2 ** ADAM_STEP)
    g_ref[...] = g
    nm_ref[...] = m_new
    nv_ref[...] = v_new
    d_ref[...] = -ADAM_LR * (m_hat / (jnp.sqrt(v_hat) + ADAM_EPS) + ADAM_WD * w_ref[...])


def _adamw_rows(name, parts, w, m, v, tr, first=None):
    _, rows, cols = w.shape
    slots = parts.shape[0]

    def body(*refs):
        lead_refs, p_ref = ([], refs[0]) if first is None else ([refs[0]], refs[1])
        _adamw_update(lead_refs + [p_ref.at[q] for q in range(slots)], *refs[len(lead_refs) + 1:])

    blk = pl.BlockSpec((None, tr, cols), lambda i: (0, i, 0))
    out = jax.ShapeDtypeStruct((1, rows, cols), F32)
    lead = [] if first is None else [pl.BlockSpec((tr, cols), lambda i: (i, 0))]
    return pl.pallas_call(
        body, name=name,
        grid=(rows // tr,),
        in_specs=lead + [pl.BlockSpec((slots, tr, cols), lambda i: (0, i, 0)), blk, blk, blk],
        out_specs=(blk, blk, blk, blk),
        out_shape=(out, out, out, out),
        compiler_params=_cparams(("arbitrary",)),
    )(*([] if first is None else [first]), parts, w, m, v)


def _adamw_transposed(name, first, parts, w, m, v, tl):
    _, rows, cols = w.shape
    slots, padded = parts.shape[:2]

    def body(f_ref, p_ref, *refs):
        _adamw_update([f_ref.at[pl.ds(0, cols)]]
                      + [p_ref.at[q, pl.ds(0, cols)] for q in range(slots)], *refs)

    blk = pl.BlockSpec((cols, None, tl), lambda i: (0, 0, i))
    out = jax.ShapeDtypeStruct((cols, 1, rows), F32)
    res = pl.pallas_call(
        body, name=name,
        grid=(rows // tl,),
        in_specs=[pl.BlockSpec((padded, tl), lambda i: (0, i)),
                  pl.BlockSpec((slots, padded, tl), lambda i: (0, 0, i)), blk, blk, blk],
        out_specs=(blk, blk, blk, blk),
        out_shape=(out, out, out, out),
        compiler_params=_cparams(("arbitrary",)),
    )(first, parts, *[a.transpose(2, 0, 1) for a in (w, m, v)])
    return [r.transpose(1, 2, 0) for r in res]


def _adamw_group(firsts, parts, ws, ms, vs):
    n = len(ws)

    def body(*refs):
        ins, outs = refs[:5 * n], refs[5 * n:]
        x, y, c = _mesh_pos()
        for a in range(n):
            _adamw_update([ins[a].at[4 * x + 2 * y + c]]
                          + [ins[n + a].at[q] for q in range(ins[n + a].shape[0])],
                          *[r.at[0] for r in (ins[2 * n + a], ins[3 * n + a], ins[4 * n + a])],
                          *[r.at[0] for r in outs[4 * a:4 * a + 4]])

    vmem = lambda k: [pl.BlockSpec(memory_space=pltpu.VMEM) for _ in range(k)]
    out_shape = []
    for w in ws:
        out_shape += [jax.ShapeDtypeStruct(w.shape, F32)] * 4
    res = pl.pallas_call(
        body, name="adamw_small_weights",
        in_specs=vmem(5 * n), out_specs=tuple(vmem(4 * n)), out_shape=tuple(out_shape),
        compiler_params=_cparams(),
    )(*firsts, *parts, *ws, *ms, *vs)
    return [res[4 * a:4 * a + 4] for a in range(n)]


def _rope_tables(positions):
    half = MLA_ROPE // 2
    freqs = ROPE_THETA ** (-jnp.arange(half, dtype=F32) / half)
    ang = positions.astype(F32).reshape(-1, 1) * freqs
    cos, sin = jnp.cos(ang), jnp.sin(ang)
    t = ang.shape[0]
    one, zero = jnp.ones((t, MLA_NOPE), F32), jnp.zeros((t, half), F32)
    tail = jnp.zeros((t, LANE - MLA_QK), F32)
    rc = jnp.concatenate([one, cos, cos, tail], axis=1)
    rsn = jnp.concatenate([0.0 * one, -sin, zero, tail], axis=1)
    rsp = jnp.concatenate([0.0 * one, zero, sin, tail], axis=1)
    return rc, rsn, rsp


def _cols_full(g):
    return g.transpose(1, 0, 2)


def kernel(x, positions, g_in, w_in, g_q, w_uq, g_kv, w_ukv, w_gla_gate, b_gla_gate, g_gla, w_proj_mla, w_proj_gla, w_out, g_final, loss_target, m_g_in, m_w_in, m_g_q, m_w_uq, m_g_kv, m_w_ukv, m_w_gla_gate, m_b_gla_gate, m_g_gla, m_w_proj_mla, m_w_proj_gla, m_w_out, m_g_final, v_g_in, v_w_in, v_g_q, v_w_uq, v_g_kv, v_w_ukv, v_w_gla_gate, v_b_gla_gate, v_g_gla, v_w_proj_mla, v_w_proj_gla, v_w_out, v_g_final):
    t = x.shape[1]
    x2d = x.reshape(t, D_MODEL)
    tgt = loss_target.reshape(t, D_MODEL)
    g_final2 = g_final.reshape(1, D_MODEL)
    sharded = [(w_in, m_w_in, v_w_in), (w_uq, m_w_uq, v_w_uq), (w_ukv, m_w_ukv, v_w_ukv),
               (w_gla_gate, m_w_gla_gate, v_w_gla_gate), (w_proj_mla, m_w_proj_mla, v_w_proj_mla),
               (w_proj_gla, m_w_proj_gla, v_w_proj_gla), (w_out, m_w_out, v_w_out)]

    w_in_t = w_in.transpose(2, 0, 1).reshape(SHARD_COLS, D_MODEL)
    everyone = tuple(range(N_DEV))
    w_in_b = w_in_t.astype(BF16)
    b_uq, b_ukv, b_gate, b_pm, b_pg, b_o = [s[0][0].astype(BF16) for s in sharded[1:]]
    stages = ((0, 2, 4, 6), (1, 3, 5, 7))
    where = {d: (k, i) for k, srcs in enumerate(stages) for i, d in enumerate(srcs)}
    g_in_1, g_uq, g_ukv, g_gate = _all_gather(
        "all_gather_first", [w_in_b, b_uq, b_ukv, b_gate], [stages[0]] + [everyone] * 3)
    w_uq_p = jnp.pad(_cols_full(g_uq), ((0, 0), (0, 0), (0, HEAD_PAD - MLA_QK))).reshape(
        MLA_Q_RANK, MLA_HEADS * HEAD_PAD)
    ukv = _cols_full(g_ukv)
    w_k_p = jnp.pad(ukv[:, :, :MLA_NOPE], ((0, 0), (0, 0), (0, HEAD_PAD - MLA_NOPE))).reshape(
        MLA_KV_RANK, MLA_HEADS * HEAD_PAD)
    w_v = ukv[:, :, MLA_NOPE:].reshape(MLA_KV_RANK, MLA_WIDTH)
    w_gate_p = jnp.pad(_cols_full(g_gate).reshape(GLA_GATE_RANK, GLA_DK),
                       ((MISC_ALR, LANE - MISC_ALR - GLA_GATE_RANK), (0, 0)))
    rc, rsn, rsp = _rope_tables(positions)

    w_lat = _weights_to_p("weights_latents", [g_in_1], where, 2)
    proj_lat, h, rstd = _inproj(x2d, g_in, w_lat)
    q, k, v, log_a, pre, cqn, ckvn, rq, rkv, misc = _mla_prep(
        proj_lat, g_q, g_kv, w_uq_p, w_k_p, w_v, w_gate_p, b_gla_gate, rc, rsn, rsp)
    o_mla, lse, (g_in_2, g_pm, g_pg, g_o) = _mla_attn_fwd(
        q, k, v, [w_in_b, b_pm, b_pg, b_o], [stages[1]] + [everyone] * 3)
    w_gla = _weights_to_p("weights_gla", [g_in_1, g_in_2], where, 0)
    proj_gla = _proj("inproj_gla", h, w_gla)
    o_gla, states = _gla_fwd(proj_gla, log_a)
    w_out_path = _weights_to_p("weights_out_path", [g_in_1, g_in_2], where, 1)
    proj_out = _proj("inproj_out_path", h, w_out_path)
    w_in_p = (w_gla, w_out_path, w_lat)
    w_pm = _cols_full(g_pm).reshape(MLA_WIDTH, D_MODEL)
    w_pg = g_pg.reshape(GLA_DV, D_MODEL)
    w_o = g_o.reshape(D_MODEL, D_MODEL)

    (dx2, do_mla, do_gla, d_out, merged, um, ug, dym, dyg, loss_p, dg_final,
     dg_gla) = _post(o_mla, proj_out, o_gla, x2d, tgt, g_gla, g_final2, w_pm, w_pg, w_o)

    p_pm = _matmul("dw_proj_mla", um, dym, 512, 512, BF16).reshape(
        MLA_WIDTH, N_DEV, D_MODEL // N_DEV).transpose(1, 0, 2)
    p_pg = _matmul("dw_proj_gla", ug, dyg, 512, 512, BF16).reshape(N_DEV, -1, D_MODEL)
    p_o = _matmul("dw_out", merged, dx2, 512, 512, BF16).reshape(N_DEV, -1, D_MODEL)
    own_in = jnp.zeros((SHARD_PAD, D_MODEL), BF16)
    land_in = lax.empty((PEERS, SHARD_PAD, D_MODEL), BF16)
    dw_groups, started, lands = {}, [], [land_in]

    def reduce_scatter_stage(s, dests, own_in, extra=()):
        parts_in, own_in = _grads_to_shards("grads_to_shards_%d" % s, dw_groups, dests, own_in)
        first = len(lands)
        lands.extend(lax.empty((PEERS,) + p.shape[1:], BF16) for p in extra)
        idx = [0] + list(range(first, len(lands)))
        all_dests = [[(i, d, r0, r1) for i, (d, ranges) in enumerate(dests) for r0, r1 in ranges]]
        all_dests += [_whole(everyone, p.shape[1]) for p in extra]
        sems, parts, new_lands, token = _ici_start(
            "ici_start_%d" % s, [parts_in] + list(extra), [lands[i] for i in idx], all_dests)
        for a, i in enumerate(idx):
            lands[i] = new_lands[a]
            started.append((sems[a][0], sems[a][1], parts[a], i, all_dests[a]))
        return own_in, token

    dw_groups[1] = _matmul("dw_in_1", d_out, h, 512, 512, BF16)
    full = [(0, SHARD_PAD)]
    own_in, token = reduce_scatter_stage(
        1, [(5, full), (6, full), (7, full)], own_in, (p_pm, p_pg, p_o))
    d_gla, dla = _gla_bwd(proj_gla, log_a, do_gla, states, token)
    dw_groups[0] = _matmul("dw_in_0", d_gla, h, 512, 512, BF16)
    own_in, token = reduce_scatter_stage(
        2, [(1, full), (2, full), (3, full), (4, [(0, 64), (96, SHARD_PAD)]),
            (0, [(672, SHARD_PAD)])], own_in)
    dq, dk, dv = _mla_attn_bwd(q, k, v, o_mla, do_mla, lse, token)
    d_lat, dqpre, dpre, dg_q, dg_kv, db_gate = _mla_prep_bwd(
        dq, dk, dv, dla, pre, proj_lat, rq, rkv, g_q, g_kv, w_uq_p, w_k_p, w_v, w_gate_p, rc, rsn, rsp)
    dw_groups[2] = _matmul("dw_in_2", d_lat, h, 896, 512, BF16)
    dw_uq = _matmul("dw_uq", cqn, dqpre, MLA_Q_RANK, 512, BF16)
    p_uq = dw_uq.reshape(MLA_Q_RANK, MLA_HEADS, HEAD_PAD)[:, :, :MLA_QK].transpose(1, 0, 2)
    dw_k = _matmul("dw_uk", ckvn, dk, MLA_KV_RANK, 512, BF16)
    dw_v = _matmul("dw_uv", ckvn, dv, MLA_KV_RANK, 512, BF16)
    p_ukv = jnp.concatenate(
        [dw_k.reshape(MLA_KV_RANK, MLA_HEADS, HEAD_PAD)[:, :, :MLA_NOPE],
         dw_v.reshape(MLA_KV_RANK, MLA_HEADS, MLA_VDIM)], axis=2).transpose(1, 0, 2)
    dw_gate = _matmul("dw_gate", misc, dpre, LANE, 512, BF16)
    p_gate = dw_gate[MISC_ALR:MISC_ALR + GLA_GATE_RANK].reshape(
        GLA_GATE_RANK, N_DEV, GLA_DK // N_DEV).transpose(1, 0, 2)
    own_in, token = reduce_scatter_stage(
        3, [(0, [(0, 672)]), (4, [(64, 96)])], own_in, (p_uq, p_ukv, p_gate))
    grad_x, dg_in = _inproj_bwd((d_gla, d_out, d_lat), w_in_p, x2d, rstd, g_in, dx2, token)
    small = jnp.concatenate([dg_in.reshape(-1), dg_q.reshape(-1), dg_kv.reshape(-1),
                             db_gate.reshape(-1), dg_gla.reshape(-1), dg_final.reshape(-1),
                             loss_p[0, :1]])
    small = jnp.pad(small, (0, SMALL_ROWS * LANE - small.shape[0])).reshape(SMALL_ROWS, LANE)

    (small_all,) = _all_gather("all_gather_small", [small], [everyone])
    lands = _ici_wait("ici_wait", started, lands, small_all)
    big = [_adamw_transposed("adamw_w_in", own_in, lands[0], *sharded[0], 256)]
    big += _adamw_group([p_uq, p_ukv, p_gate, p_pm, p_pg, p_o], list(lands[4:7]) + list(lands[1:4]),
                        *[[s[j] for s in sharded[1:]] for j in range(3)])
    replicated = [(g_in, m_g_in, v_g_in), (g_q, m_g_q, v_g_q), (g_kv, m_g_kv, v_g_kv),
                  (b_gla_gate, m_b_gla_gate, v_b_gla_gate), (g_gla, m_g_gla, v_g_gla),
                  (g_final, m_g_final, v_g_final)]
    spacks = [jnp.pad(jnp.concatenate([s[j].reshape(-1) for s in replicated]),
                      (0, SMALL_ROWS * LANE - sum(SMALL_SIZES))).reshape(1, SMALL_ROWS, LANE)
              for j in range(3)]
    tiny = _adamw_rows("adamw_gains", small_all, spacks[0], spacks[1], spacks[2], SMALL_ROWS)

    outs = {}
    names = ("w_in", "w_uq", "w_ukv", "w_gla_gate", "w_proj_mla", "w_proj_gla", "w_out")
    for j, kind in enumerate(("grad", "delta", "new_m", "new_v")):
        for name, res in zip(names, big):
            outs[kind, name] = res[j]
        flat = tiny[j].reshape(-1)
        off = 0
        for name, size in zip(("g_in", "g_q", "g_kv", "b_gla_gate", "g_gla", "g_final"), SMALL_SIZES):
            shape = (size,) if name == "g_final" else (1, size)
            outs[kind, name] = flat[off:off + size].reshape(shape)
            off += size
    loss = tiny[0].reshape(-1)[sum(SMALL_SIZES)]
    order = ("g_in", "w_in", "g_q", "w_uq", "g_kv", "w_ukv", "w_gla_gate", "b_gla_gate", "g_gla",
             "w_proj_mla", "w_proj_gla", "w_out", "g_final")
    result = [loss, grad_x.reshape(1, t, D_MODEL)]
    for kind in ("grad", "delta", "new_m", "new_v"):
        result += [outs[kind, name] for name in order]
    return tuple(result)
```

```python
import jax
import jax.numpy as jnp
from jax import lax
from jax.experimental import pallas as pl
from jax.experimental.pallas import tpu as pltpu

F32 = jnp.float32
BF16 = jnp.bfloat16
MESH = pl.DeviceIdType.MESH
N_DEV = 8

D_MODEL = 1024
EPS = 1e-6
MLA_HEADS = 8
MLA_NOPE = 64
MLA_ROPE = 32
MLA_VDIM = 64
MLA_Q_RANK = 384
MLA_KV_RANK = 256
MLA_QK = MLA_NOPE + MLA_ROPE
MLA_WIDTH = MLA_HEADS * MLA_VDIM
ROPE_THETA = 10000.0
GLA_HEADS = 4
GLA_DK = 512
GLA_DV = 1024
GLA_HK = 128
GLA_HV = 256
GLA_GATE_RANK = 16
GLA_GATE_NORM = 16.0
GLA_CHUNK = 64
GLA_CHUNKS_PER_STEP = 8
D_IN = 6320

ADAM_LR = 0.001
ADAM_B1 = 0.9
ADAM_B2 = 0.999
ADAM_EPS = 1e-08
ADAM_WD = 0.01
ADAM_STEP = 10

LANE = 128
HEAD_PAD = 128
VMEM_LIMIT = 48 * 1024 * 1024

P_VG, P_QG, P_KG = 0, 1024, 1536
P_ZGLA, P_GMLA, P_GGLA, P_ZMLA = 2048, 3072, 4096, 5120
P_CQ, P_CKV, P_MISC = 5632, 6144, 6400
P_TOTAL = 6528
P_GROUPS = ((0, 2048), (2048, 3584), (5632, 896))
MISC_KR = 64
MISC_ALR = 96
SHARD_COLS = D_IN // N_DEV
SHARD_PAD = 800
P_COMPONENTS = ((0, 384, P_CQ), (384, 256, P_CKV), (640, 32, P_MISC + MISC_KR), (672, 512, P_ZMLA),
                (1184, 512, P_QG), (1696, 512, P_KG), (2208, 1024, P_VG),
                (3232, 16, P_MISC + MISC_ALR), (3248, 1024, P_ZGLA), (4272, 1024, P_GMLA),
                (5296, 1024, P_GGLA))

SMALL_SIZES = (1024, 384, 256, 512, 256, 1024)
SMALL_ROWS = 32


def _segments():
    segs = []
    for g0, n, p0 in P_COMPONENTS:
        g = g0
        while g < g0 + n:
            d = g // SHARD_COLS
            end = min(g0 + n, (d + 1) * SHARD_COLS)
            segs.append((d, g - d * SHARD_COLS, end - g, p0 + g - g0))
            g = end
    return segs


def _group_of(p0):
    return max(i for i, (off, _) in enumerate(P_GROUPS) if off <= p0)


def _rel(p0):
    return p0 - P_GROUPS[_group_of(p0)][0]


def _cparams(sem=None):
    if sem is None:
        return pltpu.CompilerParams(vmem_limit_bytes=VMEM_LIMIT)
    return pltpu.CompilerParams(dimension_semantics=sem, vmem_limit_bytes=VMEM_LIMIT)


def _sigmoid(v):
    return 1.0 / (1.0 + jnp.exp(-v))


def _dot(a, b):
    return jnp.dot(a, b, preferred_element_type=F32)


def _dot_nt(a, b):
    return lax.dot_general(a, b, (((1,), (1,)), ((), ())), preferred_element_type=F32)


def _dot_tn(a, b):
    return lax.dot_general(a, b, (((0,), (0,)), ((), ())), preferred_element_type=F32)


def _dot_exact(a, b):
    return jnp.dot(a, b, preferred_element_type=F32, precision=lax.Precision.HIGHEST)


def _rope_fwd(blk, c, sn, sp):
    return blk * c + pltpu.roll(blk, LANE - 16, 1) * sn + pltpu.roll(blk, 16, 1) * sp


def _rope_bwd(blk, c, sn, sp):
    return blk * c + pltpu.roll(blk * sn, 16, 1) + pltpu.roll(blk * sp, LANE - 16, 1)


def _mesh_pos():
    return lax.axis_index("x"), lax.axis_index("y"), lax.axis_index("c")


def _hbm_specs(n):
    return [pl.BlockSpec(memory_space=pltpu.HBM) for _ in range(n)]


def _dev(d):
    return d >> 2, (d >> 1) & 1, d & 1


def _gather_plan(shards, sources):
    na, most = len(shards), max(len(s) for s in sources)
    out_shape = [jax.ShapeDtypeStruct((len(srcs),) + s.shape, s.dtype)
                 for s, srcs in zip(shards, sources)]
    sems = [pltpu.SemaphoreType.DMA((na, most)) for _ in range(3)]
    sems += [pltpu.SemaphoreType.DMA((na, most, 3))]
    sems += [pltpu.SemaphoreType.DMA((na, most)) for _ in range(3)]
    return out_shape, sems


def _gather_hooks(x_refs, out_refs, sems, sources):
    local_sems, d2d_send, d2d_recv, ici_send, ici_recv, fwd_send, fwd_recv = sems
    x, y, c = _mesh_pos()
    chips = [(1 - x, y), (x, 1 - y), (1 - x, 1 - y)]
    items = []
    for a, srcs in enumerate(sources):
        for i, d in enumerate(srcs):
            dx, dy, dc = _dev(d)
            near = jnp.logical_and(x == dx, y == dy)
            far = jnp.logical_not(near)
            slot = out_refs[a].at[i]

            def remote(src, to, send_sem, recv_sem, slot=slot):
                return pltpu.make_async_remote_copy(
                    src_ref=src, dst_ref=slot, send_sem=send_sem, recv_sem=recv_sem,
                    device_id=to, device_id_type=MESH)

            items.append(dict(
                me=jnp.logical_and(near, c == dc), sibling=jnp.logical_and(near, c != dc),
                relay=jnp.logical_and(far, c == dc), behind=jnp.logical_and(far, c != dc),
                local=pltpu.make_async_copy(x_refs[a], slot, local_sems.at[a, i]),
                to_sibling=remote(x_refs[a], (x, y, 1 - c), d2d_send.at[a, i], d2d_recv.at[a, i]),
                to_chips=[remote(x_refs[a], (*chip, c), ici_send.at[a, i, j], ici_recv.at[a, i])
                          for j, chip in enumerate(chips)],
                forward=remote(slot, (x, y, 1 - c), fwd_send.at[a, i], fwd_recv.at[a, i])))

    def start():
        for it in items:
            @pl.when(it["me"])
            def _(it=it):
                it["local"].start()
                it["to_sibling"].start()
                for cp in it["to_chips"]:
                    cp.start()

    def finish():
        for it in items:
            @pl.when(it["relay"])
            def _(it=it):
                it["to_chips"][0].wait_recv()
                it["forward"].start()
        for it in items:
            pl.when(it["sibling"])(it["to_sibling"].wait_recv)
            pl.when(it["behind"])(it["forward"].wait_recv)
            pl.when(it["relay"])(it["forward"].wait_send)

            @pl.when(it["me"])
            def _(it=it):
                it["local"].wait()
                it["to_sibling"].wait_send()
                for cp in it["to_chips"]:
                    cp.wait_send()

    return start, finish


def _all_gather(name, shards, sources):
    n = len(shards)
    out_shape, sems = _gather_plan(shards, sources)

    def body(*refs):
        start, finish = _gather_hooks(refs[:n], refs[n:2 * n], refs[2 * n:], sources)
        start()
        finish()

    return pl.pallas_call(
        body, name=name,
        out_shape=tuple(out_shape),
        in_specs=_hbm_specs(n), out_specs=tuple(_hbm_specs(n)),
        scratch_shapes=sems,
        compiler_params=_cparams(),
    )(*shards)


PEERS = N_DEV - 1


def _whole(dests, rows):
    return [(i, d, 0, rows) for i, d in enumerate(dests)]


def _ici_copies(p_ref, land_ref, send_sems, recv_sems, pieces):
    x, y, c = _mesh_pos()
    sends, arrivals = [], []

    def rows_of(ref, j, r0, r1):
        return ref.at[j] if (r0, r1) == (0, ref.shape[1]) else ref.at[j, pl.ds(r0, r1 - r0)]

    for p, (i, d, r0, r1) in enumerate(pieces):
        dx, dy, dc = _dev(d)
        k = (4 * (x != dx).astype(jnp.int32) + 2 * (y != dy).astype(jnp.int32)
             + (c != dc).astype(jnp.int32))
        slot = jnp.maximum(k - 1, 0)
        sends.append((k > 0, pltpu.make_async_remote_copy(
            src_ref=rows_of(p_ref, i, r0, r1), dst_ref=rows_of(land_ref, slot, r0, r1),
            send_sem=send_sems.at[p], recv_sem=recv_sems.at[p * PEERS + slot],
            device_id=(dx, dy, dc), device_id_type=MESH)))
        arrivals.append((k == 0, [pltpu.make_async_remote_copy(
            src_ref=rows_of(p_ref, i, r0, r1), dst_ref=rows_of(land_ref, r, r0, r1),
            send_sem=send_sems.at[p], recv_sem=recv_sems.at[p * PEERS + r],
            device_id=(dx, dy, dc), device_id_type=MESH) for r in range(PEERS)]))
    return sends, arrivals


def _ici_start(name, hs, lands, dests):
    na = len(hs)

    def body(*refs):
        h_refs, land_refs, sems = refs[:na], refs[na:2 * na], refs[2 * na:4 * na]
        token = refs[-1]
        for a in range(na):
            sends, _ = _ici_copies(h_refs[a], land_refs[a], sems[2 * a], sems[2 * a + 1], dests[a])
            for go, cp in sends:
                pl.when(go)(cp.start)
        token[...] = jnp.zeros_like(token)

    hbm, sem = pl.BlockSpec(memory_space=pltpu.HBM), pl.BlockSpec(memory_space=pltpu.SEMAPHORE)
    sem_shapes = []
    for a in range(na):
        sem_shapes += [pltpu.SemaphoreType.DMA((len(dests[a]),)),
                       pltpu.SemaphoreType.DMA((len(dests[a]) * PEERS,))]
    res = pl.pallas_call(
        body, name=name,
        out_shape=tuple(sem_shapes) + tuple(pltpu.HBM(v.shape, v.dtype) for v in list(hs) + list(lands))
        + (jax.ShapeDtypeStruct((8, LANE), F32),),
        in_specs=(hbm,) * (2 * na),
        out_specs=(sem,) * (2 * na) + (hbm,) * (2 * na) + (pl.BlockSpec(memory_space=pltpu.VMEM),),
        input_output_aliases={i: 2 * na + i for i in range(2 * na)},
        compiler_params=pltpu.CompilerParams(
            has_side_effects=pltpu.SideEffectType.DATAFLOW_SIDE_EFFECTING,
            vmem_limit_bytes=VMEM_LIMIT),
    )(*[pltpu.with_memory_space_constraint(v, pltpu.HBM) for v in list(hs) + list(lands)])
    sems = [(res[2 * a], res[2 * a + 1]) for a in range(na)]
    return sems, res[2 * na:3 * na], res[3 * na:4 * na], res[-1]


def _ici_wait(name, started, lands, after):
    k, nl = len(started), len(lands)

    def body(*refs):
        land_refs = refs[3 * k:3 * k + nl]
        for s in range(k):
            h_ref, send_sems, recv_sems = refs[3 * s:3 * s + 3]
            sends, arrivals = _ici_copies(h_ref, land_refs[started[s][3]], send_sems, recv_sems,
                                          started[s][4])
            for go, cp in sends:
                pl.when(go)(cp.wait_send)
            for here, cps in arrivals:
                for cp in cps:
                    pl.when(here)(cp.wait_recv)

    hbm, sem = pl.BlockSpec(memory_space=pltpu.HBM), pl.BlockSpec(memory_space=pltpu.SEMAPHORE)
    operands, specs = [], []
    for send_sems, recv_sems, h, _, _ in started:
        operands += [h, send_sems, recv_sems]
        specs += [hbm, sem, sem]
    return pl.pallas_call(
        body, name=name,
        out_shape=tuple(pltpu.HBM(v.shape, v.dtype) for v in lands),
        in_specs=tuple(specs) + (hbm,) * nl + (pl.BlockSpec(memory_space=pl.ANY),),
        out_specs=(hbm,) * nl,
        input_output_aliases={3 * k + i: i for i in range(nl)},
        compiler_params=pltpu.CompilerParams(
            has_side_effects=pltpu.SideEffectType.DATAFLOW_SIDE_EFFECTING,
            vmem_limit_bytes=VMEM_LIMIT),
    )(*operands, *lands, after)


def _weights_to_p(name, gathered, where, group):
    tl = 256
    off, width = P_GROUPS[group]
    segs = sorted([s for s in _segments() if _group_of(s[3]) == group], key=lambda s: s[3])
    used = sorted({where[s[0]][0] for s in segs})

    def body(*refs):
        g_refs, o_ref = dict(zip(used, refs[:-1])), refs[-1]
        pieces, pos = [], off
        for d, c0, n, p0 in segs:
            if p0 > pos:
                pieces.append(jnp.zeros((p0 - pos, tl), F32))
            k, slot = where[d]
            pieces.append(g_refs[k][slot, c0:c0 + n, :].astype(F32))
            pos = p0 + n
        if off + width > pos:
            pieces.append(jnp.zeros((off + width - pos, tl), F32))
        o_ref[...] = jnp.concatenate(pieces, axis=0).astype(BF16)

    return pl.pallas_call(
        body, name=name,
        grid=(D_MODEL // tl,),
        in_specs=[pl.BlockSpec((gathered[k].shape[0], SHARD_COLS, tl), lambda i: (0, 0, i))
                  for k in used],
        out_specs=pl.BlockSpec((width, tl), lambda i: (0, i)),
        out_shape=jax.ShapeDtypeStruct((width, D_MODEL), BF16),
        compiler_params=_cparams(("arbitrary",)),
    )(*[gathered[k] for k in used])


def _shard_groups(d):
    return sorted({_group_of(s[3]) for s in _segments() if s[0] == d})


def _grads_to_shards(name, groups, dests, own_prev):
    tl = 256
    segs = _segments()
    used = sorted(groups)

    def body(*refs):
        g_refs, prev_ref, o_ref, own_ref = dict(zip(used, refs[:-3])), refs[-3], refs[-2], refs[-1]
        x, y, c = _mesh_pos()
        own = prev_ref[...].astype(F32)
        row = lax.broadcasted_iota(jnp.int32, (SHARD_PAD, tl), 0)
        for i, (d, ranges) in enumerate(dests):
            pieces, pos, asked = [], 0, None
            for r0, r1 in sorted(ranges):
                if r0 > pos:
                    pieces.append(jnp.zeros((r0 - pos, tl), F32))
                for _, c0, n, p0 in sorted([s for s in segs if s[0] == d], key=lambda s: s[1]):
                    a, b = max(c0, r0), min(c0 + n, r1)
                    if a < b:
                        gi = _group_of(p0)
                        lo = p0 - P_GROUPS[gi][0] + a - c0
                        pieces.append(g_refs[gi][lo:lo + b - a, :].astype(F32))
                if r1 > SHARD_COLS:
                    pieces.append(jnp.zeros((r1 - max(r0, SHARD_COLS), tl), F32))
                pos = r1
                inside = jnp.logical_and(row >= r0, row < r1)
                asked = inside if asked is None else jnp.logical_or(asked, inside)
            if pos < SHARD_PAD:
                pieces.append(jnp.zeros((SHARD_PAD - pos, tl), F32))
            shard = jnp.concatenate(pieces, axis=0)
            o_ref[i] = shard.astype(BF16)
            own = jnp.where(jnp.logical_and(4 * x + 2 * y + c == d, asked), shard, own)
        own_ref[...] = own.astype(BF16)

    blk = pl.BlockSpec((SHARD_PAD, tl), lambda i: (0, i))
    return pl.pallas_call(
        body, name=name,
        grid=(D_MODEL // tl,),
        in_specs=[pl.BlockSpec((P_GROUPS[g][1], tl), lambda i: (0, i)) for g in used] + [blk],
        out_specs=(pl.BlockSpec((len(dests), SHARD_PAD, tl), lambda i: (0, 0, i)), blk),
        out_shape=(jax.ShapeDtypeStruct((len(dests), SHARD_PAD, D_MODEL), BF16),
                   jax.ShapeDtypeStruct((SHARD_PAD, D_MODEL), BF16)),
        input_output_aliases={len(used): 1},
        compiler_params=_cparams(("arbitrary",)),
    )(*[groups[g] for g in used], own_prev)


def _inproj(x, g_in, w_pt):
    t = x.shape[0]
    tm = min(256, t)
    width = w_pt.shape[0]

    def body(x_ref, g_ref, w_ref, proj_ref, h_ref, r_ref):
        xf = x_ref[...]
        r = lax.rsqrt(jnp.mean(xf * xf, axis=-1, keepdims=True) + EPS)
        h = ((xf * r) * g_ref[...]).astype(BF16)
        proj_ref[...] = _dot_nt(h, w_ref[...])
        h_ref[...] = h
        r_ref[...] = r

    row = lambda w: pl.BlockSpec((tm, w), lambda i: (i, 0))
    return pl.pallas_call(
        body, name="inproj_latents",
        grid=(t // tm,),
        in_specs=[row(D_MODEL), pl.BlockSpec((1, D_MODEL), lambda i: (0, 0)),
                  pl.BlockSpec((width, D_MODEL), lambda i: (0, 0))],
        out_specs=(row(width), row(D_MODEL), row(1)),
        out_shape=(jax.ShapeDtypeStruct((t, width), F32),
                   jax.ShapeDtypeStruct((t, D_MODEL), BF16),
                   jax.ShapeDtypeStruct((t, 1), F32)),
        compiler_params=_cparams(("arbitrary",)),
    )(x, g_in, w_pt)


def _proj(name, h, w_pt):
    t = h.shape[0]
    tm = min(256, t)
    width = w_pt.shape[0]

    def body(h_ref, w_ref, o_ref):
        o_ref[...] = _dot_nt(h_ref[...], w_ref[...])

    return pl.pallas_call(
        body, name=name,
        grid=(t // tm,),
        in_specs=[pl.BlockSpec((tm, D_MODEL), lambda i: (i, 0)),
                  pl.BlockSpec((width, D_MODEL), lambda i: (0, 0))],
        out_specs=pl.BlockSpec((tm, width), lambda i: (i, 0)),
        out_shape=jax.ShapeDtypeStruct((t, width), F32),
        compiler_params=_cparams(("arbitrary",)),
    )(h, w_pt)


def _mla_prep(proj, g_q, g_kv, w_uq_p, w_k_p, w_v, w_gate_p, b_gate, rc, rsn, rsp):
    t = proj.shape[0]
    tm = min(256, t)
    hq = MLA_HEADS * HEAD_PAD

    def body(cq_ref, ckv_ref, misc_ref, gq_ref, gkv_ref, wuq_ref, wk_ref, wv_ref, wg_ref, bg_ref,
             c_ref, sn_ref, sp_ref,
             q_ref, k_ref, v_ref, la_ref, pre_ref, cqn_ref, ckvn_ref, rq_ref, rkv_ref, mb_ref):
        c, sn, sp = c_ref[...], sn_ref[...], sp_ref[...]
        cq = cq_ref[:, :MLA_Q_RANK]
        rq = lax.rsqrt(jnp.mean(cq * cq, axis=-1, keepdims=True) + EPS)
        cqn = ((cq * rq) * gq_ref[...]).astype(BF16)
        cqn_ref[...] = cqn
        rq_ref[...] = rq
        qpre = _dot(cqn, wuq_ref[...])
        ckv = ckv_ref[...]
        rkv = lax.rsqrt(jnp.mean(ckv * ckv, axis=-1, keepdims=True) + EPS)
        ckvn = ((ckv * rkv) * gkv_ref[...]).astype(BF16)
        ckvn_ref[...] = ckvn
        rkv_ref[...] = rkv
        kn = _dot(ckvn, wk_ref[...])
        v_ref[...] = _dot(ckvn, wv_ref[...]).astype(BF16)
        misc = misc_ref[...]
        krope = _rope_fwd(misc, c, sn, sp)
        for h in range(MLA_HEADS):
            sl = slice(h * HEAD_PAD, (h + 1) * HEAD_PAD)
            q_ref[:, sl] = _rope_fwd(qpre[:, sl], c, sn, sp).astype(BF16)
            k_ref[:, sl] = (kn[:, sl] + krope).astype(BF16)
        mb_ref[...] = misc.astype(BF16)
        pre = _dot(mb_ref[...], wg_ref[...]) + bg_ref[...]
        pre_ref[...] = pre
        log_a = (jnp.minimum(pre, 0.0) - jnp.log(1.0 + jnp.exp(-jnp.abs(pre)))) / GLA_GATE_NORM
        la_ref[...] = _dot_exact(_chunk_tri(tm, True), log_a)

    row = lambda w: pl.BlockSpec((tm, w), lambda i: (i, 0))
    full = lambda a: pl.BlockSpec(a.shape, lambda i: (0, 0))
    return pl.pallas_call(
        body, name="mla_prep",
        grid=(t // tm,),
        in_specs=[pl.BlockSpec((tm, 512), lambda i: (i, _rel(P_CQ) // 512)),
                  pl.BlockSpec((tm, MLA_KV_RANK), lambda i: (i, _rel(P_CKV) // MLA_KV_RANK)),
                  pl.BlockSpec((tm, LANE), lambda i: (i, _rel(P_MISC) // LANE)),
                  full(g_q), full(g_kv), full(w_uq_p), full(w_k_p), full(w_v), full(w_gate_p),
                  full(b_gate), row(LANE), row(LANE), row(LANE)],
        out_specs=(row(hq), row(hq), row(MLA_WIDTH), row(GLA_DK), row(GLA_DK),
                   row(MLA_Q_RANK), row(MLA_KV_RANK), row(1), row(1), row(LANE)),
        out_shape=(jax.ShapeDtypeStruct((t, hq), BF16), jax.ShapeDtypeStruct((t, hq), BF16),
                   jax.ShapeDtypeStruct((t, MLA_WIDTH), BF16),
                   jax.ShapeDtypeStruct((t, GLA_DK), F32), jax.ShapeDtypeStruct((t, GLA_DK), F32),
                   jax.ShapeDtypeStruct((t, MLA_Q_RANK), BF16),
                   jax.ShapeDtypeStruct((t, MLA_KV_RANK), BF16),
                   jax.ShapeDtypeStruct((t, 1), F32), jax.ShapeDtypeStruct((t, 1), F32),
                   jax.ShapeDtypeStruct((t, LANE), BF16)),
        compiler_params=_cparams(("arbitrary",)),
    )(proj, proj, proj, g_q, g_kv, w_uq_p, w_k_p, w_v, w_gate_p, b_gate, rc, rsn, rsp)


def _attn_masks(tq, i):
    keys = (i + 1) * tq
    rows = i * tq + lax.broadcasted_iota(jnp.int32, (tq, keys), 0)
    cols = lax.broadcasted_iota(jnp.int32, (tq, keys), 1)
    lane = lax.broadcasted_iota(jnp.int32, (tq, LANE), 1)
    return cols <= rows, lane < MLA_VDIM


def _for_each_query_tile(n_tiles, fn):
    for i in range(n_tiles):
        pl.when(pl.program_id(1) == i)(lambda i=i: fn(i))


def _mla_attn_fwd(q, k, v, shards, sources):
    t = q.shape[0]
    tq = min(256, t)
    scale = MLA_QK ** -0.5
    ns = len(shards)
    g_shapes, g_sems = _gather_plan(shards, sources)
    grid = (MLA_HEADS // 2, t // tq)

    def body(q_ref, k_ref, v_ref, *rest):
        o_ref, lse_ref = rest[ns:ns + 2]
        start, finish = _gather_hooks(rest[:ns], rest[ns + 2:2 * ns + 2], rest[2 * ns + 2:], sources)
        step = pl.program_id(0) * grid[1] + pl.program_id(1)
        pl.when(step == 0)(start)

        def tile(i):
            keys = (i + 1) * tq
            causal, low = _attn_masks(tq, i)
            vp = v_ref[0:keys, :]
            acc = jnp.zeros((tq, LANE), F32)
            for hh in range(2):
                sl = slice(hh * HEAD_PAD, (hh + 1) * HEAD_PAD)
                s = _dot_nt(q_ref[:, sl], k_ref[0:keys, sl]) * scale
                s = jnp.where(causal, s, -jnp.inf)
                m = jnp.max(s, axis=-1, keepdims=True)
                e = jnp.exp(s - m)
                l = jnp.sum(e, axis=-1, keepdims=True)
                o = _dot(e.astype(BF16), vp) / l
                acc = jnp.where(low if hh == 0 else jnp.logical_not(low), o, acc)
                lse_ref[hh] = m + jnp.log(l)
            o_ref[...] = acc

        _for_each_query_tile(t // tq, tile)
        pl.when(step == grid[0] * grid[1] - 1)(finish)

    res = pl.pallas_call(
        body, name="mla_attn_fwd",
        grid=grid,
        in_specs=[pl.BlockSpec((tq, 2 * HEAD_PAD), lambda p, i: (i, p)),
                  pl.BlockSpec((t, 2 * HEAD_PAD), lambda p, i: (0, p)),
                  pl.BlockSpec((t, LANE), lambda p, i: (0, p))] + _hbm_specs(ns),
        out_specs=(pl.BlockSpec((tq, LANE), lambda p, i: (i, p)),
                   pl.BlockSpec((2, tq, 1), lambda p, i: (p, i, 0))) + tuple(_hbm_specs(ns)),
        out_shape=(jax.ShapeDtypeStruct((t, MLA_WIDTH), F32),
                   jax.ShapeDtypeStruct((MLA_HEADS, t, 1), F32)) + tuple(g_shapes),
        scratch_shapes=g_sems,
        compiler_params=_cparams(("arbitrary", "arbitrary")),
    )(q, k, v, *shards)
    return res[0], res[1], res[2:]


def _mla_attn_bwd(q, k, v, o, do, lse, after):
    t = q.shape[0]
    tq = min(256, t)
    scale = MLA_QK ** -0.5

    def body(q_ref, k_ref, v_ref, o_ref, do_ref, lse_ref, after_ref, dq_ref, dk_ref, dv_ref):
        del after_ref

        @pl.when(pl.program_id(1) == 0)
        def _():
            dk_ref[...] = jnp.zeros_like(dk_ref)
            dv_ref[...] = jnp.zeros_like(dv_ref)

        def tile(i):
            keys = (i + 1) * tq
            causal, low = _attn_masks(tq, i)
            vp = v_ref[0:keys, :]
            do_all = do_ref[...]
            o_all = o_ref[...]
            dv_acc = jnp.zeros((keys, LANE), F32)
            for hh in range(2):
                sl = slice(hh * HEAD_PAD, (hh + 1) * HEAD_PAD)
                do_h = jnp.where(low if hh == 0 else jnp.logical_not(low), do_all, 0.0)
                dsum = jnp.sum(do_h * o_all, axis=-1, keepdims=True)
                qh = q_ref[:, sl]
                kh = k_ref[0:keys, sl]
                s = _dot_nt(qh, kh) * scale
                p = jnp.where(causal, jnp.exp(s - lse_ref[hh]), 0.0)
                do_b = do_h.astype(BF16)
                dp = _dot_nt(do_b, vp)
                ds = (p * (dp - dsum) * scale).astype(BF16)
                dq_ref[:, sl] = _dot(ds, kh).astype(BF16)
                dk_ref[0:keys, sl] += _dot_tn(ds, qh)
                dv_acc = dv_acc + _dot_tn(p.astype(BF16), do_b)
            dv_ref[0:keys, :] += dv_acc

        _for_each_query_tile(t // tq, tile)

    return pl.pallas_call(
        body, name="mla_attn_bwd",
        grid=(MLA_HEADS // 2, t // tq),
        in_specs=[pl.BlockSpec((tq, 2 * HEAD_PAD), lambda p, i: (i, p)),
                  pl.BlockSpec((t, 2 * HEAD_PAD), lambda p, i: (0, p)),
                  pl.BlockSpec((t, LANE), lambda p, i: (0, p)),
                  pl.BlockSpec((tq, LANE), lambda p, i: (i, p)),
                  pl.BlockSpec((tq, LANE), lambda p, i: (i, p)),
                  pl.BlockSpec((2, tq, 1), lambda p, i: (p, i, 0)),
                  pl.BlockSpec(memory_space=pl.ANY)],
        out_specs=(pl.BlockSpec((tq, 2 * HEAD_PAD), lambda p, i: (i, p)),
                   pl.BlockSpec((t, 2 * HEAD_PAD), lambda p, i: (0, p)),
                   pl.BlockSpec((t, LANE), lambda p, i: (0, p))),
        out_shape=(jax.ShapeDtypeStruct((t, MLA_HEADS * HEAD_PAD), BF16),
                   jax.ShapeDtypeStruct((t, MLA_HEADS * HEAD_PAD), F32),
                   jax.ShapeDtypeStruct((t, MLA_WIDTH), F32)),
        compiler_params=_cparams(("arbitrary", "arbitrary")),
    )(q, k, v, o, do, lse, after)


def _chunk_tri(n, lower):
    r = lax.broadcasted_iota(jnp.int32, (n, n), 0)
    c = lax.broadcasted_iota(jnp.int32, (n, n), 1)
    same = (r // GLA_CHUNK) == (c // GLA_CHUNK)
    return jnp.where(jnp.logical_and(same, r >= c if lower else r <= c), 1.0, 0.0).astype(F32)


def _gla_chunk_terms(q_ref, k_ref, b_ref, h, rows):
    sl = slice(h * GLA_HK, (h + 1) * GLA_HK)
    b = b_ref[rows, sl]
    bl = b[GLA_CHUNK - 1:GLA_CHUNK, :]
    kc = k_ref[rows, sl]
    q_in = (q_ref[rows, sl] * (GLA_HK ** -0.5)) * jnp.exp(b)
    k_in = kc * jnp.exp(-b)
    k_st = kc * jnp.exp(bl - b)
    return b, bl, q_in, k_in, k_st


def _tri(c, lower):
    r = lax.broadcasted_iota(jnp.int32, (c, c), 0)
    cc = lax.broadcasted_iota(jnp.int32, (c, c), 1)
    return jnp.where(r >= cc if lower else r <= cc, 1.0, 0.0).astype(F32)


def _gla_fwd(proj, log_a):
    t = proj.shape[0]
    per = GLA_CHUNKS_PER_STEP
    n = t // GLA_CHUNK
    c = GLA_CHUNK * per

    def body(q_ref, k_ref, v_ref, la_ref, o_ref, sp_ref, st_ref):
        @pl.when(pl.program_id(0) == 0)
        def _():
            st_ref[...] = jnp.zeros_like(st_ref)

        tri = _tri(GLA_CHUNK, True)
        for s, h in [(s, h) for s in range(per) for h in range(GLA_HEADS)]:
            rows = slice(s * GLA_CHUNK, (s + 1) * GLA_CHUNK)
            _, bl, q_in, k_in, k_st = _gla_chunk_terms(q_ref, k_ref, la_ref, h, rows)
            vs = slice(h * GLA_HV, (h + 1) * GLA_HV)
            vv = v_ref[rows, vs].astype(BF16)
            qb = q_in.astype(BF16)
            attn = _dot_nt(qb, k_in.astype(BF16)) * tri
            st = st_ref[h]
            sp_ref[s, h] = st
            o_ref[rows, vs] = _dot(attn.astype(BF16), vv) + _dot_nt(qb, st.astype(BF16))
            st_ref[h] = st * jnp.exp(bl) + _dot_tn(vv, k_st.astype(BF16))

    return pl.pallas_call(
        body, name="gla_fwd",
        grid=(n // per,),
        in_specs=[pl.BlockSpec((c, GLA_DK), lambda i: (i, P_QG // GLA_DK)),
                  pl.BlockSpec((c, GLA_DK), lambda i: (i, P_KG // GLA_DK)),
                  pl.BlockSpec((c, GLA_DV), lambda i: (i, P_VG // GLA_DV)),
                  pl.BlockSpec((c, GLA_DK), lambda i: (i, 0))],
        out_specs=(pl.BlockSpec((c, GLA_DV), lambda i: (i, 0)),
                   pl.BlockSpec((per, GLA_HEADS, GLA_HV, GLA_HK), lambda i: (i, 0, 0, 0))),
        out_shape=(jax.ShapeDtypeStruct((t, GLA_DV), F32),
                   jax.ShapeDtypeStruct((n, GLA_HEADS, GLA_HV, GLA_HK), F32)),
        scratch_shapes=[pltpu.VMEM((GLA_HEADS, GLA_HV, GLA_HK), F32)],
        compiler_params=_cparams(("arbitrary",)),
    )(proj, proj, proj, log_a)


def _gla_bwd(proj, log_a, do, states, after):
    t = proj.shape[0]
    per = GLA_CHUNKS_PER_STEP
    c = GLA_CHUNK * per
    n = t // c

    def body(q_ref, k_ref, v_ref, la_ref, do_ref, sp_ref, after_ref, dg_ref, dla_ref, ds_ref):
        del after_ref

        @pl.when(pl.program_id(0) == 0)
        def _():
            ds_ref[...] = jnp.zeros_like(ds_ref)

        tri = _tri(GLA_CHUNK, True)
        last = lax.broadcasted_iota(jnp.int32, (GLA_CHUNK, GLA_HK), 0) == GLA_CHUNK - 1
        for s, h in [(s, h) for s in reversed(range(per)) for h in range(GLA_HEADS)]:
            rows = slice(s * GLA_CHUNK, (s + 1) * GLA_CHUNK)
            b, bl, q_in, k_in, k_st = _gla_chunk_terms(q_ref, k_ref, la_ref, h, rows)
            ks_ = slice(h * GLA_HK, (h + 1) * GLA_HK)
            vs = slice(h * GLA_HV, (h + 1) * GLA_HV)
            vv = v_ref[rows, vs].astype(BF16)
            do_h = do_ref[rows, vs]
            qb, kb, ksb = q_in.astype(BF16), k_in.astype(BF16), k_st.astype(BF16)
            attn = (_dot_nt(qb, kb) * tri).astype(BF16)
            st = sp_ref[s, h]
            dst = ds_ref[h]
            dstb = dst.astype(BF16)
            dattn = (_dot_nt(do_h, vv) * tri).astype(BF16)
            dg_ref[rows, P_VG + h * GLA_HV:P_VG + (h + 1) * GLA_HV] = (
                _dot_tn(attn, do_h) + _dot_nt(ksb, dstb)).astype(BF16)
            dq_in = _dot(dattn, kb) + _dot(do_h, st.astype(BF16))
            dk_in = _dot_tn(dattn, qb)
            dk_st = _dot(vv, dstb)
            ebl = jnp.exp(bl)
            d_ebl = jnp.sum(st * dst, axis=0, keepdims=True)
            ds_ref[h] = _dot_tn(do_h, qb) + dst * ebl
            dg_ref[rows, P_QG + h * GLA_HK:P_QG + (h + 1) * GLA_HK] = (
                dq_in * (GLA_HK ** -0.5) * jnp.exp(b)).astype(BF16)
            dg_ref[rows, P_KG + h * GLA_HK:P_KG + (h + 1) * GLA_HK] = (
                dk_in * jnp.exp(-b) + dk_st * jnp.exp(bl - b)).astype(BF16)
            db = dq_in * q_in - dk_in * k_in - dk_st * k_st
            dbl = jnp.sum(dk_st * k_st, axis=0, keepdims=True) + d_ebl * ebl
            dla_ref[rows, ks_] = db + jnp.where(last, dbl, 0.0)

    rev = lambda i: n - 1 - i
    gw = P_GROUPS[0][1]
    return pl.pallas_call(
        body, name="gla_bwd",
        grid=(n,),
        in_specs=[pl.BlockSpec((c, GLA_DK), lambda i: (rev(i), P_QG // GLA_DK)),
                  pl.BlockSpec((c, GLA_DK), lambda i: (rev(i), P_KG // GLA_DK)),
                  pl.BlockSpec((c, GLA_DV), lambda i: (rev(i), P_VG // GLA_DV)),
                  pl.BlockSpec((c, GLA_DK), lambda i: (rev(i), 0)),
                  pl.BlockSpec((c, GLA_DV), lambda i: (rev(i), 0)),
                  pl.BlockSpec((per, GLA_HEADS, GLA_HV, GLA_HK), lambda i: (rev(i), 0, 0, 0)),
                  pl.BlockSpec(memory_space=pl.ANY)],
        out_specs=(pl.BlockSpec((c, gw), lambda i: (rev(i), 0)),
                   pl.BlockSpec((c, GLA_DK), lambda i: (rev(i), 0))),
        out_shape=(jax.ShapeDtypeStruct((t, gw), BF16), jax.ShapeDtypeStruct((t, GLA_DK), F32)),
        scratch_shapes=[pltpu.VMEM((GLA_HEADS, GLA_HV, GLA_HK), F32)],
        compiler_params=_cparams(("arbitrary",)),
    )(proj, proj, proj, log_a, do, states, after)


def _post(o_mla, proj, o_gla, x, target, g_gla, g_final, w_pm, w_pg, w_o):
    t = x.shape[0]
    tm = min(128, t)
    g0, gw = P_GROUPS[1]

    def body(om_ref, zg_ref, gm_ref, gg_ref, zm_ref, og_ref, x_ref, tg_ref, ggla_ref, gf_ref,
             wpm_ref, wpg_ref, wo_ref,
             dx2_ref, dom_ref, dog_ref, dg_ref,
             mg_ref, um_ref, ug_ref, dym_ref, dyg_ref, loss_ref, dgf_ref, dggla_ref):
        @pl.when(pl.program_id(0) == 0)
        def _():
            loss_ref[...] = jnp.zeros_like(loss_ref)
            dgf_ref[...] = jnp.zeros_like(dgf_ref)
            dggla_ref[...] = jnp.zeros_like(dggla_ref)

        om = om_ref[...]
        zm = zm_ref[...]
        sm = _sigmoid(zm)
        silu_m = zm * sm
        um = (om * silu_m).astype(BF16)
        um_ref[...] = um
        ym = _dot(um, wpm_ref[...])

        ggla = ggla_ref[...]
        zg = zg_ref[...]
        sg = _sigmoid(zg)
        silu_g = zg * sg
        xhat, rstd, on = [], [], []
        for h in range(GLA_HEADS):
            blk = og_ref[:, h * GLA_HV:(h + 1) * GLA_HV]
            r = lax.rsqrt(jnp.mean(blk * blk, axis=-1, keepdims=True) + EPS)
            xhat.append(blk * r)
            rstd.append(r)
            on.append(xhat[h] * ggla)
        on = jnp.concatenate(on, axis=-1)
        ug = (on * silu_g).astype(BF16)
        ug_ref[...] = ug
        yg = _dot(ug, wpg_ref[...])

        sgm = _sigmoid(gm_ref[...])
        sgg = _sigmoid(gg_ref[...])
        merged = (sgm * ym + sgg * yg).astype(BF16)
        mg_ref[...] = merged
        x2 = x_ref[...] + _dot(merged, wo_ref[...])
        gf = gf_ref[...]
        rf = lax.rsqrt(jnp.mean(x2 * x2, axis=-1, keepdims=True) + EPS)
        xh = x2 * rf
        err = xh * gf - tg_ref[...]
        loss_ref[...] += 0.5 * jnp.sum(jnp.mean(err * err, axis=-1, keepdims=True))

        dy = err * (1.0 / D_MODEL)
        dgf_ref[...] += jnp.sum(dy * xh, axis=0, keepdims=True)
        dxh = dy * gf
        dx2 = rf * (dxh - xh * jnp.mean(dxh * xh, axis=-1, keepdims=True))
        dx2_ref[...] = dx2
        dmerged = _dot_nt(dx2.astype(BF16), wo_ref[...])
        dym = (dmerged * sgm).astype(BF16)
        dyg = (dmerged * sgg).astype(BF16)
        dym_ref[...] = dym
        dyg_ref[...] = dyg
        dg_ref[:, P_GMLA - g0:P_GMLA - g0 + D_MODEL] = (dmerged * ym * sgm * (1.0 - sgm)).astype(BF16)
        dg_ref[:, P_GGLA - g0:P_GGLA - g0 + D_MODEL] = (dmerged * yg * sgg * (1.0 - sgg)).astype(BF16)
        dum = _dot_nt(dym, wpm_ref[...])
        dom_ref[...] = dum * silu_m
        dg_ref[:, P_ZMLA - g0:P_ZMLA - g0 + MLA_WIDTH] = (
            dum * om * (sm * (1.0 + zm * (1.0 - sm)))).astype(BF16)
        dug = _dot_nt(dyg, wpg_ref[...])
        dg_ref[:, P_ZGLA - g0:P_ZGLA - g0 + GLA_DV] = (
            dug * on * (sg * (1.0 + zg * (1.0 - sg)))).astype(BF16)
        don = dug * silu_g
        dggla = jnp.zeros((1, GLA_HV), F32)
        for h in range(GLA_HEADS):
            hs = slice(h * GLA_HV, (h + 1) * GLA_HV)
            don_h = don[:, hs]
            dggla = dggla + jnp.sum(don_h * xhat[h], axis=0, keepdims=True)
            dxh_h = don_h * ggla
            dog_ref[:, hs] = (rstd[h] * (dxh_h - xhat[h] * jnp.mean(dxh_h * xhat[h], axis=-1,
                                                                     keepdims=True))).astype(BF16)
        dggla_ref[...] += dggla

    row = lambda w: pl.BlockSpec((tm, w), lambda i: (i, 0))
    pcol = lambda w, off: pl.BlockSpec((tm, w), lambda i: (i, _rel(off) // w))
    full = lambda a: pl.BlockSpec(a.shape, lambda i: (0, 0))
    sds = jax.ShapeDtypeStruct
    return pl.pallas_call(
        body, name="post_fwd_bwd",
        grid=(t // tm,),
        in_specs=[row(MLA_WIDTH), pcol(GLA_DV, P_ZGLA), pcol(D_MODEL, P_GMLA), pcol(D_MODEL, P_GGLA),
                  pcol(MLA_WIDTH, P_ZMLA), row(GLA_DV), row(D_MODEL), row(D_MODEL),
                  full(g_gla), full(g_final), full(w_pm), full(w_pg), full(w_o)],
        out_specs=(row(D_MODEL), row(MLA_WIDTH), row(GLA_DV), row(gw),
                   row(D_MODEL), row(MLA_WIDTH), row(GLA_DV), row(D_MODEL), row(D_MODEL),
                   pl.BlockSpec((1, LANE), lambda i: (0, 0)),
                   pl.BlockSpec((1, D_MODEL), lambda i: (0, 0)),
                   pl.BlockSpec((1, GLA_HV), lambda i: (0, 0))),
        out_shape=(sds((t, D_MODEL), F32), sds((t, MLA_WIDTH), F32), sds((t, GLA_DV), BF16),
                   sds((t, gw), BF16),
                   sds((t, D_MODEL), BF16), sds((t, MLA_WIDTH), BF16), sds((t, GLA_DV), BF16),
                   sds((t, D_MODEL), BF16), sds((t, D_MODEL), BF16),
                   sds((1, LANE), F32), sds((1, D_MODEL), F32), sds((1, GLA_HV), F32)),
        compiler_params=_cparams(("arbitrary",)),
    )(o_mla, proj, proj, proj, proj, o_gla, x, target, g_gla, g_final, w_pm, w_pg, w_o)


def _mla_prep_bwd(dq, dk, dv, dla, pre, proj, rq, rkv, g_q, g_kv, w_uq_p, w_k_p, w_v, w_gate_p,
                  rc, rsn, rsp):
    t = proj.shape[0]
    tm = min(256, t)
    gw = P_GROUPS[2][1]

    def body(dq_ref, dk_ref, dv_ref, dla_ref, pre_ref, cq_ref, ckv_ref, rq_ref, rkv_ref,
             gq_ref, gkv_ref, wuq_ref, wk_ref, wv_ref, wg_ref, c_ref, sn_ref, sp_ref,
             dg_ref, dqpre_ref, dpre_ref, dgq_ref, dgkv_ref, dbg_ref):
        @pl.when(pl.program_id(0) == 0)
        def _():
            dgq_ref[...] = jnp.zeros_like(dgq_ref)
            dgkv_ref[...] = jnp.zeros_like(dgkv_ref)
            dbg_ref[...] = jnp.zeros_like(dbg_ref)

        c, sn, sp = c_ref[...], sn_ref[...], sp_ref[...]
        dkr = jnp.zeros((tm, LANE), F32)
        for h in range(MLA_HEADS):
            sl = slice(h * HEAD_PAD, (h + 1) * HEAD_PAD)
            dqpre_ref[:, sl] = _rope_bwd(dq_ref[:, sl].astype(F32), c, sn, sp).astype(BF16)
            dkr = dkr + dk_ref[:, sl]
        dcqn = _dot_nt(dqpre_ref[...], wuq_ref[...])
        rq = rq_ref[...]
        xh = cq_ref[:, :MLA_Q_RANK] * rq
        dgq_ref[...] += jnp.sum(dcqn * xh, axis=0, keepdims=True)
        dxh = dcqn * gq_ref[...]
        dcq = rq * (dxh - xh * jnp.mean(dxh * xh, axis=-1, keepdims=True))
        dg_ref[:, :MLA_Q_RANK] = dcq.astype(BF16)
        dg_ref[:, MLA_Q_RANK:512] = jnp.zeros((tm, 512 - MLA_Q_RANK), BF16)

        dckvn = _dot_nt(dk_ref[...].astype(BF16), wk_ref[...]) + \
            _dot_nt(dv_ref[...].astype(BF16), wv_ref[...])
        rkv = rkv_ref[...]
        xh = ckv_ref[...] * rkv
        dgkv_ref[...] += jnp.sum(dckvn * xh, axis=0, keepdims=True)
        dxh = dckvn * gkv_ref[...]
        dg_ref[:, P_CKV - P_CQ:P_CKV - P_CQ + MLA_KV_RANK] = (
            rkv * (dxh - xh * jnp.mean(dxh * xh, axis=-1, keepdims=True))).astype(BF16)

        dlog_a = _dot_exact(_chunk_tri(tm, False), dla_ref[...])
        dpre = dlog_a * (1.0 / GLA_GATE_NORM) * (1.0 - _sigmoid(pre_ref[...]))
        dbg_ref[...] += jnp.sum(dpre, axis=0, keepdims=True)
        dpre = dpre.astype(BF16)
        dpre_ref[...] = dpre
        lane = lax.broadcasted_iota(jnp.int32, (tm, LANE), 1)
        in_kr = jnp.logical_and(lane >= MISC_KR, lane < MISC_KR + MLA_ROPE)
        dmisc = jnp.where(in_kr, _rope_bwd(dkr, c, sn, sp), 0.0) + _dot_nt(dpre, wg_ref[...])
        dg_ref[:, P_MISC - P_CQ:P_MISC - P_CQ + LANE] = dmisc.astype(BF16)

    hq = MLA_HEADS * HEAD_PAD
    row = lambda w: pl.BlockSpec((tm, w), lambda i: (i, 0))
    full = lambda a: pl.BlockSpec(a.shape, lambda i: (0, 0))
    acc = lambda w: pl.BlockSpec((1, w), lambda i: (0, 0))
    sds = jax.ShapeDtypeStruct
    return pl.pallas_call(
        body, name="mla_prep_bwd",
        grid=(t // tm,),
        in_specs=[row(hq), row(hq), row(MLA_WIDTH), row(GLA_DK), row(GLA_DK),
                  pl.BlockSpec((tm, 512), lambda i: (i, _rel(P_CQ) // 512)),
                  pl.BlockSpec((tm, MLA_KV_RANK), lambda i: (i, _rel(P_CKV) // MLA_KV_RANK)),
                  row(1), row(1), full(g_q), full(g_kv), full(w_uq_p), full(w_k_p), full(w_v),
                  full(w_gate_p), row(LANE), row(LANE), row(LANE)],
        out_specs=(row(gw), row(hq), row(GLA_DK),
                   acc(MLA_Q_RANK), acc(MLA_KV_RANK), acc(GLA_DK)),
        out_shape=(sds((t, gw), BF16), sds((t, hq), BF16), sds((t, GLA_DK), BF16),
                   sds((1, MLA_Q_RANK), F32), sds((1, MLA_KV_RANK), F32), sds((1, GLA_DK), F32)),
        compiler_params=_cparams(("arbitrary",)),
    )(dq, dk, dv, dla, pre, proj, proj, rq, rkv, g_q, g_kv, w_uq_p, w_k_p, w_v, w_gate_p,
      rc, rsn, rsp)


def _inproj_bwd(dgroups, w_pts, x, rstd, g_in, dx2, after):
    t = x.shape[0]
    tm = min(256, t)

    def body(d0_ref, d1_ref, d2_ref, w0_ref, w1_ref, w2_ref, x_ref, r_ref, g_ref, dx2_ref, after_ref,
             dx_ref, dg_ref):
        del after_ref

        @pl.when(pl.program_id(0) == 0)
        def _():
            dg_ref[...] = jnp.zeros_like(dg_ref)

        dh = jnp.zeros((tm, D_MODEL), F32)
        for d_ref, w_ref in zip((d0_ref, d1_ref, d2_ref), (w0_ref, w1_ref, w2_ref)):
            dh = dh + _dot(d_ref[...], w_ref[...])
        r = r_ref[...]
        xh = x_ref[...] * r
        dg_ref[...] += jnp.sum(dh * xh, axis=0, keepdims=True)
        dxh = dh * g_ref[...]
        dx_ref[...] = dx2_ref[...] + r * (dxh - xh * jnp.mean(dxh * xh, axis=-1, keepdims=True))

    row = lambda w: pl.BlockSpec((tm, w), lambda i: (i, 0))
    return pl.pallas_call(
        body, name="inproj_bwd",
        grid=(t // tm,),
        in_specs=[row(w) for _, w in P_GROUPS]
        + [pl.BlockSpec((w, D_MODEL), lambda i: (0, 0)) for _, w in P_GROUPS]
        + [row(D_MODEL), row(1), pl.BlockSpec((1, D_MODEL), lambda i: (0, 0)), row(D_MODEL),
           pl.BlockSpec(memory_space=pl.ANY)],
        out_specs=(row(D_MODEL), pl.BlockSpec((1, D_MODEL), lambda i: (0, 0))),
        out_shape=(jax.ShapeDtypeStruct((t, D_MODEL), F32),
                   jax.ShapeDtypeStruct((1, D_MODEL), F32)),
        compiler_params=_cparams(("arbitrary",)),
    )(*dgroups, *w_pts, x, rstd, g_in, dx2, after)


def _matmul(name, a, b, tm, tn, dtype=F32):
    kk, m = a.shape
    n = b.shape[1]

    def body(a_ref, b_ref, o_ref):
        o_ref[...] = _dot_tn(a_ref[...].astype(BF16), b_ref[...].astype(BF16)).astype(dtype)

    return pl.pallas_call(
        body, name=name,
        grid=(n // tn, m // tm),
        in_specs=[pl.BlockSpec((kk, tm), lambda j, i: (0, i)),
                  pl.BlockSpec((kk, tn), lambda j, i: (0, j))],
        out_specs=pl.BlockSpec((tm, tn), lambda j, i: (i, j)),
        out_shape=jax.ShapeDtypeStruct((m, n), dtype),
        compiler_params=_cparams(("arbitrary", "arbitrary")),
    )(a, b)


def _adamw_update(part_refs, w_ref, m_ref, v_ref, g_ref, d_ref, nm_ref, nv_ref):
    g = part_refs[0][...].astype(F32)
    for p_ref in part_refs[1:]:
        g = g + p_ref[...].astype(F32)
    m_new = ADAM_B1 * m_ref[...] + (1.0 - ADAM_B1) * g
    v_new = ADAM_B2 * v_ref[...] + (1.0 - ADAM_B2) * (g * g)
    m_hat = m_new / (1.0 - ADAM_B1 ** ADAM_STEP)
    v_hat = v_new / (1.0 - ADAM_B2 ** ADAM_STEP)
    g_ref[...] = g
    nm_ref[...] = m_new
    nv_ref[...] = v_new
    d_ref[...] = -ADAM_LR * (m_hat / (jnp.sqrt(v_hat) + ADAM_EPS) + ADAM_WD * w_ref[...])


def _adamw_rows(name, parts, w, m, v, tr, first=None):
    _, rows, cols = w.shape
    slots = parts.shape[0]

    def body(*refs):
        lead_refs, p_ref = ([], refs[0]) if first is None else ([refs[0]], refs[1])
        _adamw_update(lead_refs + [p_ref.at[q] for q in range(slots)], *refs[len(lead_refs) + 1:])

    blk = pl.BlockSpec((None, tr, cols), lambda i: (0, i, 0))
    out = jax.ShapeDtypeStruct((1, rows, cols), F32)
    lead = [] if first is None else [pl.BlockSpec((tr, cols), lambda i: (i, 0))]
    return pl.pallas_call(
        body, name=name,
        grid=(rows // tr,),
        in_specs=lead + [pl.BlockSpec((slots, tr, cols), lambda i: (0, i, 0)), blk, blk, blk],
        out_specs=(blk, blk, blk, blk),
        out_shape=(out, out, out, out),
        compiler_params=_cparams(("arbitrary",)),
    )(*([] if first is None else [first]), parts, w, m, v)


def _adamw_transposed(name, first, parts, w, m, v, tl):
    _, rows, cols = w.shape
    slots, padded = parts.shape[:2]

    def body(f_ref, p_ref, *refs):
        _adamw_update([f_ref.at[pl.ds(0, cols)]]
                      + [p_ref.at[q, pl.ds(0, cols)] for q in range(slots)], *refs)

    blk = pl.BlockSpec((cols, None, tl), lambda i: (0, 0, i))
    out = jax.ShapeDtypeStruct((cols, 1, rows), F32)
    res = pl.pallas_call(
        body, name=name,
        grid=(rows // tl,),
        in_specs=[pl.BlockSpec((padded, tl), lambda i: (0, i)),
                  pl.BlockSpec((slots, padded, tl), lambda i: (0, 0, i)), blk, blk, blk],
        out_specs=(blk, blk, blk, blk),
        out_shape=(out, out, out, out),
        compiler_params=_cparams(("arbitrary",)),
    )(first, parts, *[a.transpose(2, 0, 1) for a in (w, m, v)])
    return [r.transpose(1, 2, 0) for r in res]


def _adamw_group(firsts, parts, ws, ms, vs):
    n = len(ws)

    def body(*refs):
        ins, outs = refs[:5 * n], refs[5 * n:]
        x, y, c = _mesh_pos()
        for a in range(n):
            _adamw_update([ins[a].at[4 * x + 2 * y + c]]
                          + [ins[n + a].at[q] for q in range(ins[n + a].shape[0])],
                          *[r.at[0] for r in (ins[2 * n + a], ins[3 * n + a], ins[4 * n + a])],
                          *[r.at[0] for r in outs[4 * a:4 * a + 4]])

    vmem = lambda k: [pl.BlockSpec(memory_space=pltpu.VMEM) for _ in range(k)]
    out_shape = []
    for w in ws:
        out_shape += [jax.ShapeDtypeStruct(w.shape, F32)] * 4
    res = pl.pallas_call(
        body, name="adamw_small_weights",
        in_specs=vmem(5 * n), out_specs=tuple(vmem(4 * n)), out_shape=tuple(out_shape),
        compiler_params=_cparams(),
    )(*firsts, *parts, *ws, *ms, *vs)
    return [res[4 * a:4 * a + 4] for a in range(n)]


def _rope_tables(positions):
    half = MLA_ROPE // 2
    freqs = ROPE_THETA ** (-jnp.arange(half, dtype=F32) / half)
    ang = positions.astype(F32).reshape(-1, 1) * freqs
    cos, sin = jnp.cos(ang), jnp.sin(ang)
    t = ang.shape[0]
    one, zero = jnp.ones((t, MLA_NOPE), F32), jnp.zeros((t, half), F32)
    tail = jnp.zeros((t, LANE - MLA_QK), F32)
    rc = jnp.concatenate([one, cos, cos, tail], axis=1)
    rsn = jnp.concatenate([0.0 * one, -sin, zero, tail], axis=1)
    rsp = jnp.concatenate([0.0 * one, zero, sin, tail], axis=1)
    return rc, rsn, rsp


def _cols_full(g):
    return g.transpose(1, 0, 2)


def kernel(x, positions, g_in, w_in, g_q, w_uq, g_kv, w_ukv, w_gla_gate, b_gla_gate, g_gla, w_proj_mla, w_proj_gla, w_out, g_final, loss_target, m_g_in, m_w_in, m_g_q, m_w_uq, m_g_kv, m_w_ukv, m_w_gla_gate, m_b_gla_gate, m_g_gla, m_w_proj_mla, m_w_proj_gla, m_w_out, m_g_final, v_g_in, v_w_in, v_g_q, v_w_uq, v_g_kv, v_w_ukv, v_w_gla_gate, v_b_gla_gate, v_g_gla, v_w_proj_mla, v_w_proj_gla, v_w_out, v_g_final):
    t = x.shape[1]
    x2d = x.reshape(t, D_MODEL)
    tgt = loss_target.reshape(t, D_MODEL)
    g_final2 = g_final.reshape(1, D_MODEL)
    sharded = [(w_in, m_w_in, v_w_in), (w_uq, m_w_uq, v_w_uq), (w_ukv, m_w_ukv, v_w_ukv),
               (w_gla_gate, m_w_gla_gate, v_w_gla_gate), (w_proj_mla, m_w_proj_mla, v_w_proj_mla),
               (w_proj_gla, m_w_proj_gla, v_w_proj_gla), (w_out, m_w_out, v_w_out)]

    w_in_t = w_in.transpose(2, 0, 1).reshape(SHARD_COLS, D_MODEL)
    everyone = tuple(range(N_DEV))
    w_in_b = w_in_t.astype(BF16)
    b_uq, b_ukv, b_gate, b_pm, b_pg, b_o = [s[0][0].astype(BF16) for s in sharded[1:]]
    stages = ((0, 2, 4, 6), (1, 3, 5, 7))
    where = {d: (k, i) for k, srcs in enumerate(stages) for i, d in enumerate(srcs)}
    g_in_1, g_uq, g_ukv, g_gate = _all_gather(
        "all_gather_first", [w_in_b, b_uq, b_ukv, b_gate], [stages[0]] + [everyone] * 3)
    w_uq_p = jnp.pad(_cols_full(g_uq), ((0, 0), (0, 0), (0, HEAD_PAD - MLA_QK))).reshape(
        MLA_Q_RANK, MLA_HEADS * HEAD_PAD)
    ukv = _cols_full(g_ukv)
    w_k_p = jnp.pad(ukv[:, :, :MLA_NOPE], ((0, 0), (0, 0), (0, HEAD_PAD - MLA_NOPE))).reshape(
        MLA_KV_RANK, MLA_HEADS * HEAD_PAD)
    w_v = ukv[:, :, MLA_NOPE:].reshape(MLA_KV_RANK, MLA_WIDTH)
    w_gate_p = jnp.pad(_cols_full(g_gate).reshape(GLA_GATE_RANK, GLA_DK),
                       ((MISC_ALR, LANE - MISC_ALR - GLA_GATE_RANK), (0, 0)))
    rc, rsn, rsp = _rope_tables(positions)

    w_lat = _weights_to_p("weights_latents", [g_in_1], where, 2)
    proj_lat, h, rstd = _inproj(x2d, g_in, w_lat)
    q, k, v, log_a, pre, cqn, ckvn, rq, rkv, misc = _mla_prep(
        proj_lat, g_q, g_kv, w_uq_p, w_k_p, w_v, w_gate_p, b_gla_gate, rc, rsn, rsp)
    o_mla, lse, (g_in_2, g_pm, g_pg, g_o) = _mla_attn_fwd(
        q, k, v, [w_in_b, b_pm, b_pg, b_o], [stages[1]] + [everyone] * 3)
    w_gla = _weights_to_p("weights_gla", [g_in_1, g_in_2], where, 0)
    proj_gla = _proj("inproj_gla", h, w_gla)
    o_gla, states = _gla_fwd(proj_gla, log_a)
    w_out_path = _weights_to_p("weights_out_path", [g_in_1, g_in_2], where, 1)
    proj_out = _proj("inproj_out_path", h, w_out_path)
    w_in_p = (w_gla, w_out_path, w_lat)
    w_pm = _cols_full(g_pm).reshape(MLA_WIDTH, D_MODEL)
    w_pg = g_pg.reshape(GLA_DV, D_MODEL)
    w_o = g_o.reshape(D_MODEL, D_MODEL)

    (dx2, do_mla, do_gla, d_out, merged, um, ug, dym, dyg, loss_p, dg_final,
     dg_gla) = _post(o_mla, proj_out, o_gla, x2d, tgt, g_gla, g_final2, w_pm, w_pg, w_o)

    p_pm = _matmul("dw_proj_mla", um, dym, 512, 512, BF16).reshape(
        MLA_WIDTH, N_DEV, D_MODEL // N_DEV).transpose(1, 0, 2)
    p_pg = _matmul("dw_proj_gla", ug, dyg, 512, 512, BF16).reshape(N_DEV, -1, D_MODEL)
    p_o = _matmul("dw_out", merged, dx2, 512, 512, BF16).reshape(N_DEV, -1, D_MODEL)
    own_in = jnp.zeros((SHARD_PAD, D_MODEL), BF16)
    land_in = lax.empty((PEERS, SHARD_PAD, D_MODEL), BF16)
    dw_groups, started, lands = {}, [], [land_in]

    def reduce_scatter_stage(s, dests, own_in, extra=()):
        parts_in, own_in = _grads_to_shards("grads_to_shards_%d" % s, dw_groups, dests, own_in)
        first = len(lands)
        lands.extend(lax.empty((PEERS,) + p.shape[1:], BF16) for p in extra)
        idx = [0] + list(range(first, len(lands)))
        all_dests = [[(i, d, r0, r1) for i, (d, ranges) in enumerate(dests) for r0, r1 in ranges]]
        all_dests += [_whole(everyone, p.shape[1]) for p in extra]
        sems, parts, new_lands, token = _ici_start(
            "ici_start_%d" % s, [parts_in] + list(extra), [lands[i] for i in idx], all_dests)
        for a, i in enumerate(idx):
            lands[i] = new_lands[a]
            started.append((sems[a][0], sems[a][1], parts[a], i, all_dests[a]))
        return own_in, token

    dw_groups[1] = _matmul("dw_in_1", d_out, h, 512, 512, BF16)
    full = [(0, SHARD_PAD)]
    own_in, token = reduce_scatter_stage(
        1, [(5, full), (6, full), (7, full), (0, [(672, SHARD_PAD)]), (1, [(0, 384)]),
            (4, [(96, SHARD_PAD)])], own_in, (p_pm, p_pg, p_o))
    d_gla, dla = _gla_bwd(proj_gla, log_a, do_gla, states, token)
    dw_groups[0] = _matmul("dw_in_0", d_gla, h, 512, 512, BF16)
    own_in, token = reduce_scatter_stage(
        2, [(1, [(384, SHARD_PAD)]), (2, full), (3, full), (4, [(0, 64)])], own_in)
    dq, dk, dv = _mla_attn_bwd(q, k, v, o_mla, do_mla, lse, token)
    d_lat, dqpre, dpre, dg_q, dg_kv, db_gate = _mla_prep_bwd(
        dq, dk, dv, dla, pre, proj_lat, rq, rkv, g_q, g_kv, w_uq_p, w_k_p, w_v, w_gate_p, rc, rsn, rsp)
    dw_groups[2] = _matmul("dw_in_2", d_lat, h, 896, 512, BF16)
    dw_uq = _matmul("dw_uq", cqn, dqpre, MLA_Q_RANK, 512, BF16)
    p_uq = dw_uq.reshape(MLA_Q_RANK, MLA_HEADS, HEAD_PAD)[:, :, :MLA_QK].transpose(1, 0, 2)
    dw_k = _matmul("dw_uk", ckvn, dk, MLA_KV_RANK, 512, BF16)
    dw_v = _matmul("dw_uv", ckvn, dv, MLA_KV_RANK, 512, BF16)
    p_ukv = jnp.concatenate(
        [dw_k.reshape(MLA_KV_RANK, MLA_HEADS, HEAD_PAD)[:, :, :MLA_NOPE],
         dw_v.reshape(MLA_KV_RANK, MLA_HEADS, MLA_VDIM)], axis=2).transpose(1, 0, 2)
    dw_gate = _matmul("dw_gate", misc, dpre, LANE, 512, BF16)
    p_gate = dw_gate[MISC_ALR:MISC_ALR + GLA_GATE_RANK].reshape(
        GLA_GATE_RANK, N_DEV, GLA_DK // N_DEV).transpose(1, 0, 2)
    own_in, token = reduce_scatter_stage(
        3, [(0, [(0, 672)]), (4, [(64, 96)])], own_in, (p_uq, p_ukv, p_gate))
    grad_x, dg_in = _inproj_bwd((d_gla, d_out, d_lat), w_in_p, x2d, rstd, g_in, dx2, token)
    small = jnp.concatenate([dg_in.reshape(-1), dg_q.reshape(-1), dg_kv.reshape(-1),
                             db_gate.reshape(-1), dg_gla.reshape(-1), dg_final.reshape(-1),
                             loss_p[0, :1]])
    small = jnp.pad(small, (0, SMALL_ROWS * LANE - small.shape[0])).reshape(SMALL_ROWS, LANE)

    (small_all,) = _all_gather("all_gather_small", [small], [everyone])
    lands = _ici_wait("ici_wait", started, lands, small_all)
    big = [_adamw_transposed("adamw_w_in", own_in, lands[0], *sharded[0], 256)]
    big += _adamw_group([p_uq, p_ukv, p_gate, p_pm, p_pg, p_o], list(lands[4:7]) + list(lands[1:4]),
                        *[[s[j] for s in sharded[1:]] for j in range(3)])
    replicated = [(g_in, m_g_in, v_g_in), (g_q, m_g_q, v_g_q), (g_kv, m_g_kv, v_g_kv),
                  (b_gla_gate, m_b_gla_gate, v_b_gla_gate), (g_gla, m_g_gla, v_g_gla),
                  (g_final, m_g_final, v_g_final)]
    spacks = [jnp.pad(jnp.concatenate([s[j].reshape(-1) for s in replicated]),
                      (0, SMALL_ROWS * LANE - sum(SMALL_SIZES))).reshape(1, SMALL_ROWS, LANE)
              for j in range(3)]
    tiny = _adamw_rows("adamw_gains", small_all, spacks[0], spacks[1], spacks[2], SMALL_ROWS)

    outs = {}
    names = ("w_in", "w_uq", "w_ukv", "w_gla_gate", "w_proj_mla", "w_proj_gla", "w_out")
    for j, kind in enumerate(("grad", "delta", "new_m", "new_v")):
        for name, res in zip(names, big):
            outs[kind, name] = res[j]
        flat = tiny[j].reshape(-1)
        off = 0
        for name, size in zip(("g_in", "g_q", "g_kv", "b_gla_gate", "g_gla", "g_final"), SMALL_SIZES):
            shape = (size,) if name == "g_final" else (1, size)
            outs[kind, name] = flat[off:off + size].reshape(shape)
            off += size
    loss = tiny[0].reshape(-1)[sum(SMALL_SIZES)]
    order = ("g_in", "w_in", "g_q", "w_uq", "g_kv", "w_ukv", "w_gla_gate", "b_gla_gate", "g_gla",
             "w_proj_mla", "w_proj_gla", "w_out", "g_final")
    result = [loss, grad_x.reshape(1, t, D_MODEL)]
    for kind in ("grad", "delta", "new_m", "new_v"):
        result += [outs[kind, name] for name in order]
    return tuple(result)
```

```python
import jax
import jax.numpy as jnp
from jax import lax
from jax.experimental import pallas as pl
from jax.experimental.pallas import tpu as pltpu

F32 = jnp.float32
BF16 = jnp.bfloat16
MESH = pl.DeviceIdType.MESH
N_DEV = 8

D_MODEL = 1024
EPS = 1e-6
MLA_HEADS = 8
MLA_NOPE = 64
MLA_ROPE = 32
MLA_VDIM = 64
MLA_Q_RANK = 384
MLA_KV_RANK = 256
MLA_QK = MLA_NOPE + MLA_ROPE
MLA_WIDTH = MLA_HEADS * MLA_VDIM
ROPE_THETA = 10000.0
GLA_HEADS = 4
GLA_DK = 512
GLA_DV = 1024
GLA_HK = 128
GLA_HV = 256
GLA_GATE_RANK = 16
GLA_GATE_NORM = 16.0
GLA_CHUNK = 64
GLA_CHUNKS_PER_STEP = 8
D_IN = 6320

ADAM_LR = 0.001
ADAM_B1 = 0.9
ADAM_B2 = 0.999
ADAM_EPS = 1e-08
ADAM_WD = 0.01
ADAM_STEP = 10

LANE = 128
HEAD_PAD = 128
VMEM_LIMIT = 48 * 1024 * 1024

P_VG, P_QG, P_KG = 0, 1024, 1536
P_ZGLA, P_GMLA, P_GGLA, P_ZMLA = 2048, 3072, 4096, 5120
P_CQ, P_CKV, P_MISC = 5632, 6144, 6400
P_TOTAL = 6528
P_GROUPS = ((0, 2048), (2048, 3584), (5632, 896))
MISC_KR = 64
MISC_ALR = 96
SHARD_COLS = D_IN // N_DEV
SHARD_PAD = 800
P_COMPONENTS = ((0, 384, P_CQ), (384, 256, P_CKV), (640, 32, P_MISC + MISC_KR), (672, 512, P_ZMLA),
                (1184, 512, P_QG), (1696, 512, P_KG), (2208, 1024, P_VG),
                (3232, 16, P_MISC + MISC_ALR), (3248, 1024, P_ZGLA), (4272, 1024, P_GMLA),
                (5296, 1024, P_GGLA))

SMALL_SIZES = (1024, 384, 256, 512, 256, 1024)
SMALL_ROWS = 32


def _segments():
    segs = []
    for g0, n, p0 in P_COMPONENTS:
        g = g0
        while g < g0 + n:
            d = g // SHARD_COLS
            end = min(g0 + n, (d + 1) * SHARD_COLS)
            segs.append((d, g - d * SHARD_COLS, end - g, p0 + g - g0))
            g = end
    return segs


def _group_of(p0):
    return max(i for i, (off, _) in enumerate(P_GROUPS) if off <= p0)


def _rel(p0):
    return p0 - P_GROUPS[_group_of(p0)][0]


def _cparams(sem=None):
    if sem is None:
        return pltpu.CompilerParams(vmem_limit_bytes=VMEM_LIMIT)
    return pltpu.CompilerParams(dimension_semantics=sem, vmem_limit_bytes=VMEM_LIMIT)


def _sigmoid(v):
    return 1.0 / (1.0 + jnp.exp(-v))


def _dot(a, b):
    return jnp.dot(a, b, preferred_element_type=F32)


def _dot_nt(a, b):
    return lax.dot_general(a, b, (((1,), (1,)), ((), ())), preferred_element_type=F32)


def _dot_tn(a, b):
    return lax.dot_general(a, b, (((0,), (0,)), ((), ())), preferred_element_type=F32)


def _dot_exact(a, b):
    return jnp.dot(a, b, preferred_element_type=F32, precision=lax.Precision.HIGHEST)


def _rope_fwd(blk, c, sn, sp):
    return blk * c + pltpu.roll(blk, LANE - 16, 1) * sn + pltpu.roll(blk, 16, 1) * sp


def _rope_bwd(blk, c, sn, sp):
    return blk * c + pltpu.roll(blk * sn, 16, 1) + pltpu.roll(blk * sp, LANE - 16, 1)


def _mesh_pos():
    return lax.axis_index("x"), lax.axis_index("y"), lax.axis_index("c")


def _hbm_specs(n):
    return [pl.BlockSpec(memory_space=pltpu.HBM) for _ in range(n)]


def _dev(d):
    return d >> 2, (d >> 1) & 1, d & 1


def _gather_plan(shards, sources):
    na, most = len(shards), max(len(s) for s in sources)
    out_shape = [jax.ShapeDtypeStruct((len(srcs),) + s.shape, s.dtype)
                 for s, srcs in zip(shards, sources)]
    sems = [pltpu.SemaphoreType.DMA((na, most)) for _ in range(3)]
    sems += [pltpu.SemaphoreType.DMA((na, most, 3))]
    sems += [pltpu.SemaphoreType.DMA((na, most)) for _ in range(3)]
    return out_shape, sems


def _gather_hooks(x_refs, out_refs, sems, sources):
    local_sems, d2d_send, d2d_recv, ici_send, ici_recv, fwd_send, fwd_recv = sems
    x, y, c = _mesh_pos()
    chips = [(1 - x, y), (x, 1 - y), (1 - x, 1 - y)]
    items = []
    for a, srcs in enumerate(sources):
        for i, d in enumerate(srcs):
            dx, dy, dc = _dev(d)
            near = jnp.logical_and(x == dx, y == dy)
            far = jnp.logical_not(near)
            slot = out_refs[a].at[i]

            def remote(src, to, send_sem, recv_sem, slot=slot):
                return pltpu.make_async_remote_copy(
                    src_ref=src, dst_ref=slot, send_sem=send_sem, recv_sem=recv_sem,
                    device_id=to, device_id_type=MESH)

            items.append(dict(
                me=jnp.logical_and(near, c == dc), sibling=jnp.logical_and(near, c != dc),
                relay=jnp.logical_and(far, c == dc), behind=jnp.logical_and(far, c != dc),
                local=pltpu.make_async_copy(x_refs[a], slot, local_sems.at[a, i]),
                to_sibling=remote(x_refs[a], (x, y, 1 - c), d2d_send.at[a, i], d2d_recv.at[a, i]),
                to_chips=[remote(x_refs[a], (*chip, c), ici_send.at[a, i, j], ici_recv.at[a, i])
                          for j, chip in enumerate(chips)],
                forward=remote(slot, (x, y, 1 - c), fwd_send.at[a, i], fwd_recv.at[a, i])))

    def start():
        for it in items:
            @pl.when(it["me"])
            def _(it=it):
                it["local"].start()
                it["to_sibling"].start()
                for cp in it["to_chips"]:
                    cp.start()

    def finish():
        for it in items:
            @pl.when(it["relay"])
            def _(it=it):
                it["to_chips"][0].wait_recv()
                it["forward"].start()
        for it in items:
            pl.when(it["sibling"])(it["to_sibling"].wait_recv)
            pl.when(it["behind"])(it["forward"].wait_recv)
            pl.when(it["relay"])(it["forward"].wait_send)

            @pl.when(it["me"])
            def _(it=it):
                it["local"].wait()
                it["to_sibling"].wait_send()
                for cp in it["to_chips"]:
                    cp.wait_send()

    return start, finish


def _all_gather(name, shards, sources):
    n = len(shards)
    out_shape, sems = _gather_plan(shards, sources)

    def body(*refs):
        start, finish = _gather_hooks(refs[:n], refs[n:2 * n], refs[2 * n:], sources)
        start()
        finish()

    return pl.pallas_call(
        body, name=name,
        out_shape=tuple(out_shape),
        in_specs=_hbm_specs(n), out_specs=tuple(_hbm_specs(n)),
        scratch_shapes=sems,
        compiler_params=_cparams(),
    )(*shards)


PEERS = N_DEV - 1


def _whole(dests, rows):
    return [(i, d, 0, rows) for i, d in enumerate(dests)]


def _ici_copies(p_ref, land_ref, send_sems, recv_sems, pieces):
    x, y, c = _mesh_pos()
    sends, arrivals = [], []

    def rows_of(ref, j, r0, r1):
        return ref.at[j] if (r0, r1) == (0, ref.shape[1]) else ref.at[j, pl.ds(r0, r1 - r0)]

    for p, (i, d, r0, r1) in enumerate(pieces):
        dx, dy, dc = _dev(d)
        k = (4 * (x != dx).astype(jnp.int32) + 2 * (y != dy).astype(jnp.int32)
             + (c != dc).astype(jnp.int32))
        slot = jnp.maximum(k - 1, 0)
        sends.append((k > 0, pltpu.make_async_remote_copy(
            src_ref=rows_of(p_ref, i, r0, r1), dst_ref=rows_of(land_ref, slot, r0, r1),
            send_sem=send_sems.at[p], recv_sem=recv_sems.at[p * PEERS + slot],
            device_id=(dx, dy, dc), device_id_type=MESH)))
        arrivals.append((k == 0, [pltpu.make_async_remote_copy(
            src_ref=rows_of(p_ref, i, r0, r1), dst_ref=rows_of(land_ref, r, r0, r1),
            send_sem=send_sems.at[p], recv_sem=recv_sems.at[p * PEERS + r],
            device_id=(dx, dy, dc), device_id_type=MESH) for r in range(PEERS)]))
    return sends, arrivals


def _ici_start(name, hs, lands, dests):
    na = len(hs)

    def body(*refs):
        h_refs, land_refs, sems = refs[:na], refs[na:2 * na], refs[2 * na:4 * na]
        token = refs[-1]
        for a in range(na):
            sends, _ = _ici_copies(h_refs[a], land_refs[a], sems[2 * a], sems[2 * a + 1], dests[a])
            for go, cp in sends:
                pl.when(go)(cp.start)
        token[...] = jnp.zeros_like(token)

    hbm, sem = pl.BlockSpec(memory_space=pltpu.HBM), pl.BlockSpec(memory_space=pltpu.SEMAPHORE)
    sem_shapes = []
    for a in range(na):
        sem_shapes += [pltpu.SemaphoreType.DMA((len(dests[a]),)),
                       pltpu.SemaphoreType.DMA((len(dests[a]) * PEERS,))]
    res = pl.pallas_call(
        body, name=name,
        out_shape=tuple(sem_shapes) + tuple(pltpu.HBM(v.shape, v.dtype) for v in list(hs) + list(lands))
        + (jax.ShapeDtypeStruct((8, LANE), F32),),
        in_specs=(hbm,) * (2 * na),
        out_specs=(sem,) * (2 * na) + (hbm,) * (2 * na) + (pl.BlockSpec(memory_space=pltpu.VMEM),),
        input_output_aliases={i: 2 * na + i for i in range(2 * na)},
        compiler_params=pltpu.CompilerParams(
            has_side_effects=pltpu.SideEffectType.DATAFLOW_SIDE_EFFECTING,
            vmem_limit_bytes=VMEM_LIMIT),
    )(*[pltpu.with_memory_space_constraint(v, pltpu.HBM) for v in list(hs) + list(lands)])
    sems = [(res[2 * a], res[2 * a + 1]) for a in range(na)]
    return sems, res[2 * na:3 * na], res[3 * na:4 * na], res[-1]


def _ici_wait(name, started, lands, after):
    k, nl = len(started), len(lands)

    def body(*refs):
        land_refs = refs[3 * k:3 * k + nl]
        for s in range(k):
            h_ref, send_sems, recv_sems = refs[3 * s:3 * s + 3]
            sends, arrivals = _ici_copies(h_ref, land_refs[started[s][3]], send_sems, recv_sems,
                                          started[s][4])
            for go, cp in sends:
                pl.when(go)(cp.wait_send)
            for here, cps in arrivals:
                for cp in cps:
                    pl.when(here)(cp.wait_recv)

    hbm, sem = pl.BlockSpec(memory_space=pltpu.HBM), pl.BlockSpec(memory_space=pltpu.SEMAPHORE)
    operands, specs = [], []
    for send_sems, recv_sems, h, _, _ in started:
        operands += [h, send_sems, recv_sems]
        specs += [hbm, sem, sem]
    return pl.pallas_call(
        body, name=name,
        out_shape=tuple(pltpu.HBM(v.shape, v.dtype) for v in lands),
        in_specs=tuple(specs) + (hbm,) * nl + (pl.BlockSpec(memory_space=pl.ANY),),
        out_specs=(hbm,) * nl,
        input_output_aliases={3 * k + i: i for i in range(nl)},
        compiler_params=pltpu.CompilerParams(
            has_side_effects=pltpu.SideEffectType.DATAFLOW_SIDE_EFFECTING,
            vmem_limit_bytes=VMEM_LIMIT),
    )(*operands, *lands, after)


def _weights_to_p(name, gathered, where, group):
    tl = 256
    off, width = P_GROUPS[group]
    segs = sorted([s for s in _segments() if _group_of(s[3]) == group], key=lambda s: s[3])
    used = sorted({where[s[0]][0] for s in segs})

    def body(*refs):
        g_refs, o_ref = dict(zip(used, refs[:-1])), refs[-1]
        pieces, pos = [], off
        for d, c0, n, p0 in segs:
            if p0 > pos:
                pieces.append(jnp.zeros((p0 - pos, tl), F32))
            k, slot = where[d]
            pieces.append(g_refs[k][slot, c0:c0 + n, :].astype(F32))
            pos = p0 + n
        if off + width > pos:
            pieces.append(jnp.zeros((off + width - pos, tl), F32))
        o_ref[...] = jnp.concatenate(pieces, axis=0).astype(BF16)

    return pl.pallas_call(
        body, name=name,
        grid=(D_MODEL // tl,),
        in_specs=[pl.BlockSpec((gathered[k].shape[0], SHARD_COLS, tl), lambda i: (0, 0, i))
                  for k in used],
        out_specs=pl.BlockSpec((width, tl), lambda i: (0, i)),
        out_shape=jax.ShapeDtypeStruct((width, D_MODEL), BF16),
        compiler_params=_cparams(("arbitrary",)),
    )(*[gathered[k] for k in used])


def _shard_groups(d):
    return sorted({_group_of(s[3]) for s in _segments() if s[0] == d})


def _grads_to_shards(name, groups, dests, own_prev):
    tl = 256
    segs = _segments()
    used = sorted(groups)

    def body(*refs):
        g_refs, prev_ref, o_ref, own_ref = dict(zip(used, refs[:-3])), refs[-3], refs[-2], refs[-1]
        x, y, c = _mesh_pos()
        own = prev_ref[...].astype(F32)
        row = lax.broadcasted_iota(jnp.int32, (SHARD_PAD, tl), 0)
        for i, (d, ranges) in enumerate(dests):
            pieces, pos, asked = [], 0, None
            for r0, r1 in sorted(ranges):
                if r0 > pos:
                    pieces.append(jnp.zeros((r0 - pos, tl), F32))
                for _, c0, n, p0 in sorted([s for s in segs if s[0] == d], key=lambda s: s[1]):
                    a, b = max(c0, r0), min(c0 + n, r1)
                    if a < b:
                        gi = _group_of(p0)
                        lo = p0 - P_GROUPS[gi][0] + a - c0
                        pieces.append(g_refs[gi][lo:lo + b - a, :].astype(F32))
                if r1 > SHARD_COLS:
                    pieces.append(jnp.zeros((r1 - max(r0, SHARD_COLS), tl), F32))
                pos = r1
                inside = jnp.logical_and(row >= r0, row < r1)
                asked = inside if asked is None else jnp.logical_or(asked, inside)
            if pos < SHARD_PAD:
                pieces.append(jnp.zeros((SHARD_PAD - pos, tl), F32))
            shard = jnp.concatenate(pieces, axis=0)
            o_ref[i] = shard.astype(BF16)
            own = jnp.where(jnp.logical_and(4 * x + 2 * y + c == d, asked), shard, own)
        own_ref[...] = own.astype(BF16)

    blk = pl.BlockSpec((SHARD_PAD, tl), lambda i: (0, i))
    return pl.pallas_call(
        body, name=name,
        grid=(D_MODEL // tl,),
        in_specs=[pl.BlockSpec((P_GROUPS[g][1], tl), lambda i: (0, i)) for g in used] + [blk],
        out_specs=(pl.BlockSpec((len(dests), SHARD_PAD, tl), lambda i: (0, 0, i)), blk),
        out_shape=(jax.ShapeDtypeStruct((len(dests), SHARD_PAD, D_MODEL), BF16),
                   jax.ShapeDtypeStruct((SHARD_PAD, D_MODEL), BF16)),
        input_output_aliases={len(used): 1},
        compiler_params=_cparams(("arbitrary",)),
    )(*[groups[g] for g in used], own_prev)


def _inproj(x, g_in, w_pt):
    t = x.shape[0]
    tm = min(256, t)
    width = w_pt.shape[0]

    def body(x_ref, g_ref, w_ref, proj_ref, h_ref, r_ref):
        xf = x_ref[...]
        r = lax.rsqrt(jnp.mean(xf * xf, axis=-1, keepdims=True) + EPS)
        h = ((xf * r) * g_ref[...]).astype(BF16)
        proj_ref[...] = _dot_nt(h, w_ref[...])
        h_ref[...] = h
        r_ref[...] = r

    row = lambda w: pl.BlockSpec((tm, w), lambda i: (i, 0))
    return pl.pallas_call(
        body, name="inproj_latents",
        grid=(t // tm,),
        in_specs=[row(D_MODEL), pl.BlockSpec((1, D_MODEL), lambda i: (0, 0)),
                  pl.BlockSpec((width, D_MODEL), lambda i: (0, 0))],
        out_specs=(row(width), row(D_MODEL), row(1)),
        out_shape=(jax.ShapeDtypeStruct((t, width), F32),
                   jax.ShapeDtypeStruct((t, D_MODEL), BF16),
                   jax.ShapeDtypeStruct((t, 1), F32)),
        compiler_params=_cparams(("arbitrary",)),
    )(x, g_in, w_pt)


def _proj(name, h, w_pt):
    t = h.shape[0]
    tm = min(256, t)
    width = w_pt.shape[0]

    def body(h_ref, w_ref, o_ref):
        o_ref[...] = _dot_nt(h_ref[...], w_ref[...])

    return pl.pallas_call(
        body, name=name,
        grid=(t // tm,),
        in_specs=[pl.BlockSpec((tm, D_MODEL), lambda i: (i, 0)),
                  pl.BlockSpec((width, D_MODEL), lambda i: (0, 0))],
        out_specs=pl.BlockSpec((tm, width), lambda i: (i, 0)),
        out_shape=jax.ShapeDtypeStruct((t, width), F32),
        compiler_params=_cparams(("arbitrary",)),
    )(h, w_pt)


def _mla_prep(proj, g_q, g_kv, w_uq_p, w_k_p, w_v, w_gate_p, b_gate, rc, rsn, rsp):
    t = proj.shape[0]
    tm = min(256, t)
    hq = MLA_HEADS * HEAD_PAD

    def body(cq_ref, ckv_ref, misc_ref, gq_ref, gkv_ref, wuq_ref, wk_ref, wv_ref, wg_ref, bg_ref,
             c_ref, sn_ref, sp_ref,
             q_ref, k_ref, v_ref, la_ref, pre_ref, cqn_ref, ckvn_ref, rq_ref, rkv_ref, mb_ref):
        c, sn, sp = c_ref[...], sn_ref[...], sp_ref[...]
        cq = cq_ref[:, :MLA_Q_RANK]
        rq = lax.rsqrt(jnp.mean(cq * cq, axis=-1, keepdims=True) + EPS)
        cqn = ((cq * rq) * gq_ref[...]).astype(BF16)
        cqn_ref[...] = cqn
        rq_ref[...] = rq
        qpre = _dot(cqn, wuq_ref[...])
        ckv = ckv_ref[...]
        rkv = lax.rsqrt(jnp.mean(ckv * ckv, axis=-1, keepdims=True) + EPS)
        ckvn = ((ckv * rkv) * gkv_ref[...]).astype(BF16)
        ckvn_ref[...] = ckvn
        rkv_ref[...] = rkv
        kn = _dot(ckvn, wk_ref[...])
        v_ref[...] = _dot(ckvn, wv_ref[...]).astype(BF16)
        misc = misc_ref[...]
        krope = _rope_fwd(misc, c, sn, sp)
        for h in range(MLA_HEADS):
            sl = slice(h * HEAD_PAD, (h + 1) * HEAD_PAD)
            q_ref[:, sl] = _rope_fwd(qpre[:, sl], c, sn, sp).astype(BF16)
            k_ref[:, sl] = (kn[:, sl] + krope).astype(BF16)
        mb_ref[...] = misc.astype(BF16)
        pre = _dot(mb_ref[...], wg_ref[...]) + bg_ref[...]
        pre_ref[...] = pre
        log_a = (jnp.minimum(pre, 0.0) - jnp.log(1.0 + jnp.exp(-jnp.abs(pre)))) / GLA_GATE_NORM
        la_ref[...] = _dot_exact(_chunk_tri(tm, True), log_a)

    row = lambda w: pl.BlockSpec((tm, w), lambda i: (i, 0))
    full = lambda a: pl.BlockSpec(a.shape, lambda i: (0, 0))
    return pl.pallas_call(
        body, name="mla_prep",
        grid=(t // tm,),
        in_specs=[pl.BlockSpec((tm, 512), lambda i: (i, _rel(P_CQ) // 512)),
                  pl.BlockSpec((tm, MLA_KV_RANK), lambda i: (i, _rel(P_CKV) // MLA_KV_RANK)),
                  pl.BlockSpec((tm, LANE), lambda i: (i, _rel(P_MISC) // LANE)),
                  full(g_q), full(g_kv), full(w_uq_p), full(w_k_p), full(w_v), full(w_gate_p),
                  full(b_gate), row(LANE), row(LANE), row(LANE)],
        out_specs=(row(hq), row(hq), row(MLA_WIDTH), row(GLA_DK), row(GLA_DK),
                   row(MLA_Q_RANK), row(MLA_KV_RANK), row(1), row(1), row(LANE)),
        out_shape=(jax.ShapeDtypeStruct((t, hq), BF16), jax.ShapeDtypeStruct((t, hq), BF16),
                   jax.ShapeDtypeStruct((t, MLA_WIDTH), BF16),
                   jax.ShapeDtypeStruct((t, GLA_DK), F32), jax.ShapeDtypeStruct((t, GLA_DK), F32),
                   jax.ShapeDtypeStruct((t, MLA_Q_RANK), BF16),
                   jax.ShapeDtypeStruct((t, MLA_KV_RANK), BF16),
                   jax.ShapeDtypeStruct((t, 1), F32), jax.ShapeDtypeStruct((t, 1), F32),
                   jax.ShapeDtypeStruct((t, LANE), BF16)),
        compiler_params=_cparams(("arbitrary",)),
    )(proj, proj, proj, g_q, g_kv, w_uq_p, w_k_p, w_v, w_gate_p, b_gate, rc, rsn, rsp)


def _attn_masks(tq, i):
    keys = (i + 1) * tq
    rows = i * tq + lax.broadcasted_iota(jnp.int32, (tq, keys), 0)
    cols = lax.broadcasted_iota(jnp.int32, (tq, keys), 1)
    lane = lax.broadcasted_iota(jnp.int32, (tq, LANE), 1)
    return cols <= rows, lane < MLA_VDIM


def _for_each_query_tile(n_tiles, fn):
    for i in range(n_tiles):
        pl.when(pl.program_id(1) == i)(lambda i=i: fn(i))


def _mla_attn_fwd(q, k, v, shards, sources):
    t = q.shape[0]
    tq = min(256, t)
    scale = MLA_QK ** -0.5
    ns = len(shards)
    g_shapes, g_sems = _gather_plan(shards, sources)
    grid = (MLA_HEADS // 2, t // tq)

    def body(q_ref, k_ref, v_ref, *rest):
        o_ref, lse_ref = rest[ns:ns + 2]
        start, finish = _gather_hooks(rest[:ns], rest[ns + 2:2 * ns + 2], rest[2 * ns + 2:], sources)
        step = pl.program_id(0) * grid[1] + pl.program_id(1)
        pl.when(step == 0)(start)

        def tile(i):
            keys = (i + 1) * tq
            causal, low = _attn_masks(tq, i)
            vp = v_ref[0:keys, :]
            acc = jnp.zeros((tq, LANE), F32)
            for hh in range(2):
                sl = slice(hh * HEAD_PAD, (hh + 1) * HEAD_PAD)
                s = _dot_nt(q_ref[:, sl], k_ref[0:keys, sl]) * scale
                s = jnp.where(causal, s, -jnp.inf)
                m = jnp.max(s, axis=-1, keepdims=True)
                e = jnp.exp(s - m)
                l = jnp.sum(e, axis=-1, keepdims=True)
                o = _dot(e.astype(BF16), vp) / l
                acc = jnp.where(low if hh == 0 else jnp.logical_not(low), o, acc)
                lse_ref[hh] = m + jnp.log(l)
            o_ref[...] = acc

        _for_each_query_tile(t // tq, tile)
        pl.when(step == grid[0] * grid[1] - 1)(finish)

    res = pl.pallas_call(
        body, name="mla_attn_fwd",
        grid=grid,
        in_specs=[pl.BlockSpec((tq, 2 * HEAD_PAD), lambda p, i: (i, p)),
                  pl.BlockSpec((t, 2 * HEAD_PAD), lambda p, i: (0, p)),
                  pl.BlockSpec((t, LANE), lambda p, i: (0, p))] + _hbm_specs(ns),
        out_specs=(pl.BlockSpec((tq, LANE), lambda p, i: (i, p)),
                   pl.BlockSpec((2, tq, 1), lambda p, i: (p, i, 0))) + tuple(_hbm_specs(ns)),
        out_shape=(jax.ShapeDtypeStruct((t, MLA_WIDTH), F32),
                   jax.ShapeDtypeStruct((MLA_HEADS, t, 1), F32)) + tuple(g_shapes),
        scratch_shapes=g_sems,
        compiler_params=_cparams(("arbitrary", "arbitrary")),
    )(q, k, v, *shards)
    return res[0], res[1], res[2:]


def _mla_attn_bwd(q, k, v, o, do, lse, after):
    t = q.shape[0]
    tq = min(256, t)
    scale = MLA_QK ** -0.5

    def body(q_ref, k_ref, v_ref, o_ref, do_ref, lse_ref, after_ref, dq_ref, dk_ref, dv_ref):
        del after_ref

        @pl.when(pl.program_id(1) == 0)
        def _():
            dk_ref[...] = jnp.zeros_like(dk_ref)
            dv_ref[...] = jnp.zeros_like(dv_ref)

        def tile(i):
            keys = (i + 1) * tq
            causal, low = _attn_masks(tq, i)
            vp = v_ref[0:keys, :]
            do_all = do_ref[...]
            o_all = o_ref[...]
            dv_acc = jnp.zeros((keys, LANE), F32)
            for hh in range(2):
                sl = slice(hh * HEAD_PAD, (hh + 1) * HEAD_PAD)
                do_h = jnp.where(low if hh == 0 else jnp.logical_not(low), do_all, 0.0)
                dsum = jnp.sum(do_h * o_all, axis=-1, keepdims=True)
                qh = q_ref[:, sl]
                kh = k_ref[0:keys, sl]
                s = _dot_nt(qh, kh) * scale
                p = jnp.where(causal, jnp.exp(s - lse_ref[hh]), 0.0)
                do_b = do_h.astype(BF16)
                dp = _dot_nt(do_b, vp)
                ds = (p * (dp - dsum) * scale).astype(BF16)
                dq_ref[:, sl] = _dot(ds, kh).astype(BF16)
                dk_ref[0:keys, sl] += _dot_tn(ds, qh)
                dv_acc = dv_acc + _dot_tn(p.astype(BF16), do_b)
            dv_ref[0:keys, :] += dv_acc

        _for_each_query_tile(t // tq, tile)

    return pl.pallas_call(
        body, name="mla_attn_bwd",
        grid=(MLA_HEADS // 2, t // tq),
        in_specs=[pl.BlockSpec((tq, 2 * HEAD_PAD), lambda p, i: (i, p)),
                  pl.BlockSpec((t, 2 * HEAD_PAD), lambda p, i: (0, p)),
                  pl.BlockSpec((t, LANE), lambda p, i: (0, p)),
                  pl.BlockSpec((tq, LANE), lambda p, i: (i, p)),
                  pl.BlockSpec((tq, LANE), lambda p, i: (i, p)),
                  pl.BlockSpec((2, tq, 1), lambda p, i: (p, i, 0)),
                  pl.BlockSpec(memory_space=pl.ANY)],
        out_specs=(pl.BlockSpec((tq, 2 * HEAD_PAD), lambda p, i: (i, p)),
                   pl.BlockSpec((t, 2 * HEAD_PAD), lambda p, i: (0, p)),
                   pl.BlockSpec((t, LANE), lambda p, i: (0, p))),
        out_shape=(jax.ShapeDtypeStruct((t, MLA_HEADS * HEAD_PAD), BF16),
                   jax.ShapeDtypeStruct((t, MLA_HEADS * HEAD_PAD), F32),
                   jax.ShapeDtypeStruct((t, MLA_WIDTH), F32)),
        compiler_params=_cparams(("arbitrary", "arbitrary")),
    )(q, k, v, o, do, lse, after)


def _chunk_tri(n, lower):
    r = lax.broadcasted_iota(jnp.int32, (n, n), 0)
    c = lax.broadcasted_iota(jnp.int32, (n, n), 1)
    same = (r // GLA_CHUNK) == (c // GLA_CHUNK)
    return jnp.where(jnp.logical_and(same, r >= c if lower else r <= c), 1.0, 0.0).astype(F32)


def _gla_chunk_terms(q_ref, k_ref, b_ref, h, rows):
    sl = slice(h * GLA_HK, (h + 1) * GLA_HK)
    b = b_ref[rows, sl]
    bl = b[GLA_CHUNK - 1:GLA_CHUNK, :]
    kc = k_ref[rows, sl]
    q_in = (q_ref[rows, sl] * (GLA_HK ** -0.5)) * jnp.exp(b)
    k_in = kc * jnp.exp(-b)
    k_st = kc * jnp.exp(bl - b)
    return b, bl, q_in, k_in, k_st


def _tri(c, lower):
    r = lax.broadcasted_iota(jnp.int32, (c, c), 0)
    cc = lax.broadcasted_iota(jnp.int32, (c, c), 1)
    return jnp.where(r >= cc if lower else r <= cc, 1.0, 0.0).astype(F32)


def _gla_fwd(proj, log_a):
    t = proj.shape[0]
    per = GLA_CHUNKS_PER_STEP
    n = t // GLA_CHUNK
    c = GLA_CHUNK * per

    def body(q_ref, k_ref, v_ref, la_ref, o_ref, sp_ref, st_ref):
        @pl.when(pl.program_id(0) == 0)
        def _():
            st_ref[...] = jnp.zeros_like(st_ref)

        tri = _tri(GLA_CHUNK, True)
        for s, h in [(s, h) for s in range(per) for h in range(GLA_HEADS)]:
            rows = slice(s * GLA_CHUNK, (s + 1) * GLA_CHUNK)
            _, bl, q_in, k_in, k_st = _gla_chunk_terms(q_ref, k_ref, la_ref, h, rows)
            vs = slice(h * GLA_HV, (h + 1) * GLA_HV)
            vv = v_ref[rows, vs].astype(BF16)
            qb = q_in.astype(BF16)
            attn = _dot_nt(qb, k_in.astype(BF16)) * tri
            st = st_ref[h]
            sp_ref[s, h] = st
            o_ref[rows, vs] = _dot(attn.astype(BF16), vv) + _dot_nt(qb, st.astype(BF16))
            st_ref[h] = st * jnp.exp(bl) + _dot_tn(vv, k_st.astype(BF16))

    return pl.pallas_call(
        body, name="gla_fwd",
        grid=(n // per,),
        in_specs=[pl.BlockSpec((c, GLA_DK), lambda i: (i, P_QG // GLA_DK)),
                  pl.BlockSpec((c, GLA_DK), lambda i: (i, P_KG // GLA_DK)),
                  pl.BlockSpec((c, GLA_DV), lambda i: (i, P_VG // GLA_DV)),
                  pl.BlockSpec((c, GLA_DK), lambda i: (i, 0))],
        out_specs=(pl.BlockSpec((c, GLA_DV), lambda i: (i, 0)),
                   pl.BlockSpec((per, GLA_HEADS, GLA_HV, GLA_HK), lambda i: (i, 0, 0, 0))),
        out_shape=(jax.ShapeDtypeStruct((t, GLA_DV), F32),
                   jax.ShapeDtypeStruct((n, GLA_HEADS, GLA_HV, GLA_HK), F32)),
        scratch_shapes=[pltpu.VMEM((GLA_HEADS, GLA_HV, GLA_HK), F32)],
        compiler_params=_cparams(("arbitrary",)),
    )(proj, proj, proj, log_a)


def _gla_bwd(proj, log_a, do, states, after):
    t = proj.shape[0]
    per = GLA_CHUNKS_PER_STEP
    c = GLA_CHUNK * per
    n = t // c

    def body(q_ref, k_ref, v_ref, la_ref, do_ref, sp_ref, after_ref, dg_ref, dla_ref, ds_ref):
        del after_ref

        @pl.when(pl.program_id(0) == 0)
        def _():
            ds_ref[...] = jnp.zeros_like(ds_ref)

        tri = _tri(GLA_CHUNK, True)
        last = lax.broadcasted_iota(jnp.int32, (GLA_CHUNK, GLA_HK), 0) == GLA_CHUNK - 1
        for s, h in [(s, h) for s in reversed(range(per)) for h in range(GLA_HEADS)]:
            rows = slice(s * GLA_CHUNK, (s + 1) * GLA_CHUNK)
            b, bl, q_in, k_in, k_st = _gla_chunk_terms(q_ref, k_ref, la_ref, h, rows)
            ks_ = slice(h * GLA_HK, (h + 1) * GLA_HK)
            vs = slice(h * GLA_HV, (h + 1) * GLA_HV)
            vv = v_ref[rows, vs].astype(BF16)
            do_h = do_ref[rows, vs]
            qb, kb, ksb = q_in.astype(BF16), k_in.astype(BF16), k_st.astype(BF16)
            attn = (_dot_nt(qb, kb) * tri).astype(BF16)
            st = sp_ref[s, h]
            dst = ds_ref[h]
            dstb = dst.astype(BF16)
            dattn = (_dot_nt(do_h, vv) * tri).astype(BF16)
            dg_ref[rows, P_VG + h * GLA_HV:P_VG + (h + 1) * GLA_HV] = (
                _dot_tn(attn, do_h) + _dot_nt(ksb, dstb)).astype(BF16)
            dq_in = _dot(dattn, kb) + _dot(do_h, st.astype(BF16))
            dk_in = _dot_tn(dattn, qb)
            dk_st = _dot(vv, dstb)
            ebl = jnp.exp(bl)
            d_ebl = jnp.sum(st * dst, axis=0, keepdims=True)
            ds_ref[h] = _dot_tn(do_h, qb) + dst * ebl
            dg_ref[rows, P_QG + h * GLA_HK:P_QG + (h + 1) * GLA_HK] = (
                dq_in * (GLA_HK ** -0.5) * jnp.exp(b)).astype(BF16)
            dg_ref[rows, P_KG + h * GLA_HK:P_KG + (h + 1) * GLA_HK] = (
                dk_in * jnp.exp(-b) + dk_st * jnp.exp(bl - b)).astype(BF16)
            db = dq_in * q_in - dk_in * k_in - dk_st * k_st
            dbl = jnp.sum(dk_st * k_st, axis=0, keepdims=True) + d_ebl * ebl
            dla_ref[rows, ks_] = db + jnp.where(last, dbl, 0.0)

    rev = lambda i: n - 1 - i
    gw = P_GROUPS[0][1]
    return pl.pallas_call(
        body, name="gla_bwd",
        grid=(n,),
        in_specs=[pl.BlockSpec((c, GLA_DK), lambda i: (rev(i), P_QG // GLA_DK)),
                  pl.BlockSpec((c, GLA_DK), lambda i: (rev(i), P_KG // GLA_DK)),
                  pl.BlockSpec((c, GLA_DV), lambda i: (rev(i), P_VG // GLA_DV)),
                  pl.BlockSpec((c, GLA_DK), lambda i: (rev(i), 0)),
                  pl.BlockSpec((c, GLA_DV), lambda i: (rev(i), 0)),
                  pl.BlockSpec((per, GLA_HEADS, GLA_HV, GLA_HK), lambda i: (rev(i), 0, 0, 0)),
                  pl.BlockSpec(memory_space=pl.ANY)],
        out_specs=(pl.BlockSpec((c, gw), lambda i: (rev(i), 0)),
                   pl.BlockSpec((c, GLA_DK), lambda i: (rev(i), 0))),
        out_shape=(jax.ShapeDtypeStruct((t, gw), BF16), jax.ShapeDtypeStruct((t, GLA_DK), F32)),
        scratch_shapes=[pltpu.VMEM((GLA_HEADS, GLA_HV, GLA_HK), F32)],
        compiler_params=_cparams(("arbitrary",)),
    )(proj, proj, proj, log_a, do, states, after)


def _post(o_mla, proj, o_gla, x, target, g_gla, g_final, w_pm, w_pg, w_o):
    t = x.shape[0]
    tm = min(128, t)
    g0, gw = P_GROUPS[1]

    def body(om_ref, zg_ref, gm_ref, gg_ref, zm_ref, og_ref, x_ref, tg_ref, ggla_ref, gf_ref,
             wpm_ref, wpg_ref, wo_ref,
             dx2_ref, dom_ref, dog_ref, dg_ref,
             mg_ref, um_ref, ug_ref, dym_ref, dyg_ref, loss_ref, dgf_ref, dggla_ref):
        @pl.when(pl.program_id(0) == 0)
        def _():
            loss_ref[...] = jnp.zeros_like(loss_ref)
            dgf_ref[...] = jnp.zeros_like(dgf_ref)
            dggla_ref[...] = jnp.zeros_like(dggla_ref)

        om = om_ref[...]
        zm = zm_ref[...]
        sm = _sigmoid(zm)
        silu_m = zm * sm
        um = (om * silu_m).astype(BF16)
        um_ref[...] = um
        ym = _dot(um, wpm_ref[...])

        ggla = ggla_ref[...]
        zg = zg_ref[...]
        sg = _sigmoid(zg)
        silu_g = zg * sg
        xhat, rstd, on = [], [], []
        for h in range(GLA_HEADS):
            blk = og_ref[:, h * GLA_HV:(h + 1) * GLA_HV]
            r = lax.rsqrt(jnp.mean(blk * blk, axis=-1, keepdims=True) + EPS)
            xhat.append(blk * r)
            rstd.append(r)
            on.append(xhat[h] * ggla)
        on = jnp.concatenate(on, axis=-1)
        ug = (on * silu_g).astype(BF16)
        ug_ref[...] = ug
        yg = _dot(ug, wpg_ref[...])

        sgm = _sigmoid(gm_ref[...])
        sgg = _sigmoid(gg_ref[...])
        merged = (sgm * ym + sgg * yg).astype(BF16)
        mg_ref[...] = merged
        x2 = x_ref[...] + _dot(merged, wo_ref[...])
        gf = gf_ref[...]
        rf = lax.rsqrt(jnp.mean(x2 * x2, axis=-1, keepdims=True) + EPS)
        xh = x2 * rf
        err = xh * gf - tg_ref[...]
        loss_ref[...] += 0.5 * jnp.sum(jnp.mean(err * err, axis=-1, keepdims=True))

        dy = err * (1.0 / D_MODEL)
        dgf_ref[...] += jnp.sum(dy * xh, axis=0, keepdims=True)
        dxh = dy * gf
        dx2 = rf * (dxh - xh * jnp.mean(dxh * xh, axis=-1, keepdims=True))
        dx2_ref[...] = dx2
        dmerged = _dot_nt(dx2.astype(BF16), wo_ref[...])
        dym = (dmerged * sgm).astype(BF16)
        dyg = (dmerged * sgg).astype(BF16)
        dym_ref[...] = dym
        dyg_ref[...] = dyg
        dg_ref[:, P_GMLA - g0:P_GMLA - g0 + D_MODEL] = (dmerged * ym * sgm * (1.0 - sgm)).astype(BF16)
        dg_ref[:, P_GGLA - g0:P_GGLA - g0 + D_MODEL] = (dmerged * yg * sgg * (1.0 - sgg)).astype(BF16)
        dum = _dot_nt(dym, wpm_ref[...])
        dom_ref[...] = dum * silu_m
        dg_ref[:, P_ZMLA - g0:P_ZMLA - g0 + MLA_WIDTH] = (
            dum * om * (sm * (1.0 + zm * (1.0 - sm)))).astype(BF16)
        dug = _dot_nt(dyg, wpg_ref[...])
        dg_ref[:, P_ZGLA - g0:P_ZGLA - g0 + GLA_DV] = (
            dug * on * (sg * (1.0 + zg * (1.0 - sg)))).astype(BF16)
        don = dug * silu_g
        dggla = jnp.zeros((1, GLA_HV), F32)
        for h in range(GLA_HEADS):
            hs = slice(h * GLA_HV, (h + 1) * GLA_HV)
            don_h = don[:, hs]
            dggla = dggla + jnp.sum(don_h * xhat[h], axis=0, keepdims=True)
            dxh_h = don_h * ggla
            dog_ref[:, hs] = (rstd[h] * (dxh_h - xhat[h] * jnp.mean(dxh_h * xhat[h], axis=-1,
                                                                     keepdims=True))).astype(BF16)
        dggla_ref[...] += dggla

    row = lambda w: pl.BlockSpec((tm, w), lambda i: (i, 0))
    pcol = lambda w, off: pl.BlockSpec((tm, w), lambda i: (i, _rel(off) // w))
    full = lambda a: pl.BlockSpec(a.shape, lambda i: (0, 0))
    sds = jax.ShapeDtypeStruct
    return pl.pallas_call(
        body, name="post_fwd_bwd",
        grid=(t // tm,),
        in_specs=[row(MLA_WIDTH), pcol(GLA_DV, P_ZGLA), pcol(D_MODEL, P_GMLA), pcol(D_MODEL, P_GGLA),
                  pcol(MLA_WIDTH, P_ZMLA), row(GLA_DV), row(D_MODEL), row(D_MODEL),
                  full(g_gla), full(g_final), full(w_pm), full(w_pg), full(w_o)],
        out_specs=(row(D_MODEL), row(MLA_WIDTH), row(GLA_DV), row(gw),
                   row(D_MODEL), row(MLA_WIDTH), row(GLA_DV), row(D_MODEL), row(D_MODEL),
                   pl.BlockSpec((1, LANE), lambda i: (0, 0)),
                   pl.BlockSpec((1, D_MODEL), lambda i: (0, 0)),
                   pl.BlockSpec((1, GLA_HV), lambda i: (0, 0))),
        out_shape=(sds((t, D_MODEL), F32), sds((t, MLA_WIDTH), F32), sds((t, GLA_DV), BF16),
                   sds((t, gw), BF16),
                   sds((t, D_MODEL), BF16), sds((t, MLA_WIDTH), BF16), sds((t, GLA_DV), BF16),
                   sds((t, D_MODEL), BF16), sds((t, D_MODEL), BF16),
                   sds((1, LANE), F32), sds((1, D_MODEL), F32), sds((1, GLA_HV), F32)),
        compiler_params=_cparams(("arbitrary",)),
    )(o_mla, proj, proj, proj, proj, o_gla, x, target, g_gla, g_final, w_pm, w_pg, w_o)


def _mla_prep_bwd(dq, dk, dv, dla, pre, proj, rq, rkv, g_q, g_kv, w_uq_p, w_k_p, w_v, w_gate_p,
                  rc, rsn, rsp):
    t = proj.shape[0]
    tm = min(256, t)
    gw = P_GROUPS[2][1]

    def body(dq_ref, dk_ref, dv_ref, dla_ref, pre_ref, cq_ref, ckv_ref, rq_ref, rkv_ref,
             gq_ref, gkv_ref, wuq_ref, wk_ref, wv_ref, wg_ref, c_ref, sn_ref, sp_ref,
             dg_ref, dqpre_ref, dpre_ref, dgq_ref, dgkv_ref, dbg_ref):
        @pl.when(pl.program_id(0) == 0)
        def _():
            dgq_ref[...] = jnp.zeros_like(dgq_ref)
            dgkv_ref[...] = jnp.zeros_like(dgkv_ref)
            dbg_ref[...] = jnp.zeros_like(dbg_ref)

        c, sn, sp = c_ref[...], sn_ref[...], sp_ref[...]
        dkr = jnp.zeros((tm, LANE), F32)
        for h in range(MLA_HEADS):
            sl = slice(h * HEAD_PAD, (h + 1) * HEAD_PAD)
            dqpre_ref[:, sl] = _rope_bwd(dq_ref[:, sl].astype(F32), c, sn, sp).astype(BF16)
            dkr = dkr + dk_ref[:, sl]
        dcqn = _dot_nt(dqpre_ref[...], wuq_ref[...])
        rq = rq_ref[...]
        xh = cq_ref[:, :MLA_Q_RANK] * rq
        dgq_ref[...] += jnp.sum(dcqn * xh, axis=0, keepdims=True)
        dxh = dcqn * gq_ref[...]
        dcq = rq * (dxh - xh * jnp.mean(dxh * xh, axis=-1, keepdims=True))
        dg_ref[:, :MLA_Q_RANK] = dcq.astype(BF16)
        dg_ref[:, MLA_Q_RANK:512] = jnp.zeros((tm, 512 - MLA_Q_RANK), BF16)

        dckvn = _dot_nt(dk_ref[...].astype(BF16), wk_ref[...]) + \
            _dot_nt(dv_ref[...].astype(BF16), wv_ref[...])
        rkv = rkv_ref[...]
        xh = ckv_ref[...] * rkv
        dgkv_ref[...] += jnp.sum(dckvn * xh, axis=0, keepdims=True)
        dxh = dckvn * gkv_ref[...]
        dg_ref[:, P_CKV - P_CQ:P_CKV - P_CQ + MLA_KV_RANK] = (
            rkv * (dxh - xh * jnp.mean(dxh * xh, axis=-1, keepdims=True))).astype(BF16)

        dlog_a = _dot_exact(_chunk_tri(tm, False), dla_ref[...])
        dpre = dlog_a * (1.0 / GLA_GATE_NORM) * (1.0 - _sigmoid(pre_ref[...]))
        dbg_ref[...] += jnp.sum(dpre, axis=0, keepdims=True)
        dpre = dpre.astype(BF16)
        dpre_ref[...] = dpre
        lane = lax.broadcasted_iota(jnp.int32, (tm, LANE), 1)
        in_kr = jnp.logical_and(lane >= MISC_KR, lane < MISC_KR + MLA_ROPE)
        dmisc = jnp.where(in_kr, _rope_bwd(dkr, c, sn, sp), 0.0) + _dot_nt(dpre, wg_ref[...])
        dg_ref[:, P_MISC - P_CQ:P_MISC - P_CQ + LANE] = dmisc.astype(BF16)

    hq = MLA_HEADS * HEAD_PAD
    row = lambda w: pl.BlockSpec((tm, w), lambda i: (i, 0))
    full = lambda a: pl.BlockSpec(a.shape, lambda i: (0, 0))
    acc = lambda w: pl.BlockSpec((1, w), lambda i: (0, 0))
    sds = jax.ShapeDtypeStruct
    return pl.pallas_call(
        body, name="mla_prep_bwd",
        grid=(t // tm,),
        in_specs=[row(hq), row(hq), row(MLA_WIDTH), row(GLA_DK), row(GLA_DK),
                  pl.BlockSpec((tm, 512), lambda i: (i, _rel(P_CQ) // 512)),
                  pl.BlockSpec((tm, MLA_KV_RANK), lambda i: (i, _rel(P_CKV) // MLA_KV_RANK)),
                  row(1), row(1), full(g_q), full(g_kv), full(w_uq_p), full(w_k_p), full(w_v),
                  full(w_gate_p), row(LANE), row(LANE), row(LANE)],
        out_specs=(row(gw), row(hq), row(GLA_DK),
                   acc(MLA_Q_RANK), acc(MLA_KV_RANK), acc(GLA_DK)),
        out_shape=(sds((t, gw), BF16), sds((t, hq), BF16), sds((t, GLA_DK), BF16),
                   sds((1, MLA_Q_RANK), F32), sds((1, MLA_KV_RANK), F32), sds((1, GLA_DK), F32)),
        compiler_params=_cparams(("arbitrary",)),
    )(dq, dk, dv, dla, pre, proj, proj, rq, rkv, g_q, g_kv, w_uq_p, w_k_p, w_v, w_gate_p,
      rc, rsn, rsp)


def _inproj_bwd(dgroups, w_pts, x, rstd, g_in, dx2, after):
    t = x.shape[0]
    tm = min(256, t)

    def body(d0_ref, d1_ref, d2_ref, w0_ref, w1_ref, w2_ref, x_ref, r_ref, g_ref, dx2_ref, after_ref,
             dx_ref, dg_ref):
        del after_ref

        @pl.when(pl.program_id(0) == 0)
        def _():
            dg_ref[...] = jnp.zeros_like(dg_ref)

        dh = jnp.zeros((tm, D_MODEL), F32)
        for d_ref, w_ref in zip((d0_ref, d1_ref, d2_ref), (w0_ref, w1_ref, w2_ref)):
            dh = dh + _dot(d_ref[...], w_ref[...])
        r = r_ref[...]
        xh = x_ref[...] * r
        dg_ref[...] += jnp.sum(dh * xh, axis=0, keepdims=True)
        dxh = dh * g_ref[...]
        dx_ref[...] = dx2_ref[...] + r * (dxh - xh * jnp.mean(dxh * xh, axis=-1, keepdims=True))

    row = lambda w: pl.BlockSpec((tm, w), lambda i: (i, 0))
    return pl.pallas_call(
        body, name="inproj_bwd",
        grid=(t // tm,),
        in_specs=[row(w) for _, w in P_GROUPS]
        + [pl.BlockSpec((w, D_MODEL), lambda i: (0, 0)) for _, w in P_GROUPS]
        + [row(D_MODEL), row(1), pl.BlockSpec((1, D_MODEL), lambda i: (0, 0)), row(D_MODEL),
           pl.BlockSpec(memory_space=pl.ANY)],
        out_specs=(row(D_MODEL), pl.BlockSpec((1, D_MODEL), lambda i: (0, 0))),
        out_shape=(jax.ShapeDtypeStruct((t, D_MODEL), F32),
                   jax.ShapeDtypeStruct((1, D_MODEL), F32)),
        compiler_params=_cparams(("arbitrary",)),
    )(*dgroups, *w_pts, x, rstd, g_in, dx2, after)


def _matmul(name, a, b, tm, tn, dtype=F32, after=None):
    kk, m = a.shape
    n = b.shape[1]
    extra = [] if after is None else [after]

    def body(a_ref, b_ref, *rest):
        rest[-1][...] = _dot_tn(a_ref[...].astype(BF16), b_ref[...].astype(BF16)).astype(dtype)

    return pl.pallas_call(
        body, name=name,
        grid=(n // tn, m // tm),
        in_specs=[pl.BlockSpec((kk, tm), lambda j, i: (0, i)),
                  pl.BlockSpec((kk, tn), lambda j, i: (0, j))]
        + [pl.BlockSpec(memory_space=pl.ANY) for _ in extra],
        out_specs=pl.BlockSpec((tm, tn), lambda j, i: (i, j)),
        out_shape=jax.ShapeDtypeStruct((m, n), dtype),
        compiler_params=_cparams(("arbitrary", "arbitrary")),
    )(a, b, *extra)


def _adamw_update(part_refs, w_ref, m_ref, v_ref, g_ref, d_ref, nm_ref, nv_ref):
    g = part_refs[0][...].astype(F32)
    for p_ref in part_refs[1:]:
        g = g + p_ref[...].astype(F32)
    m_new = ADAM_B1 * m_ref[...] + (1.0 - ADAM_B1) * g
    v_new = ADAM_B2 * v_ref[...] + (1.0 - ADAM_B2) * (g * g)
    m_hat = m_new / (1.0 - ADAM_B1 ** ADAM_STEP)
    v_hat = v_new / (1.0 - ADAM_B2 ** ADAM_STEP)
    g_ref[...] = g
    nm_ref[...] = m_new
    nv_ref[...] = v_new
    d_ref[...] = -ADAM_LR * (m_hat / (jnp.sqrt(v_hat) + ADAM_EPS) + ADAM_WD * w_ref[...])


def _adamw_rows(name, parts, w, m, v, tr, first=None):
    _, rows, cols = w.shape
    slots = parts.shape[0]

    def body(*refs):
        lead_refs, p_ref = ([], refs[0]) if first is None else ([refs[0]], refs[1])
        _adamw_update(lead_refs + [p_ref.at[q] for q in range(slots)], *refs[len(lead_refs) + 1:])

    blk = pl.BlockSpec((None, tr, cols), lambda i: (0, i, 0))
    out = jax.ShapeDtypeStruct((1, rows, cols), F32)
    lead = [] if first is None else [pl.BlockSpec((tr, cols), lambda i: (i, 0))]
    return pl.pallas_call(
        body, name=name,
        grid=(rows // tr,),
        in_specs=lead + [pl.BlockSpec((slots, tr, cols), lambda i: (0, i, 0)), blk, blk, blk],
        out_specs=(blk, blk, blk, blk),
        out_shape=(out, out, out, out),
        compiler_params=_cparams(("arbitrary",)),
    )(*([] if first is None else [first]), parts, w, m, v)


def _adamw_transposed(name, first, parts, w, m, v, tl):
    _, rows, cols = w.shape
    slots, padded = parts.shape[:2]

    def body(f_ref, p_ref, *refs):
        _adamw_update([f_ref.at[pl.ds(0, cols)]]
                      + [p_ref.at[q, pl.ds(0, cols)] for q in range(slots)], *refs)

    blk = pl.BlockSpec((cols, None, tl), lambda i: (0, 0, i))
    out = jax.ShapeDtypeStruct((cols, 1, rows), F32)
    res = pl.pallas_call(
        body, name=name,
        grid=(rows // tl,),
        in_specs=[pl.BlockSpec((padded, tl), lambda i: (0, i)),
                  pl.BlockSpec((slots, padded, tl), lambda i: (0, 0, i)), blk, blk, blk],
        out_specs=(blk, blk, blk, blk),
        out_shape=(out, out, out, out),
        compiler_params=_cparams(("arbitrary",)),
    )(first, parts, *[a.transpose(2, 0, 1) for a in (w, m, v)])
    return [r.transpose(1, 2, 0) for r in res]


def _adamw_group(firsts, parts, ws, ms, vs):
    n = len(ws)

    def body(*refs):
        ins, outs = refs[:5 * n], refs[5 * n:]
        x, y, c = _mesh_pos()
        for a in range(n):
            _adamw_update([ins[a].at[4 * x + 2 * y + c]]
                          + [ins[n + a].at[q] for q in range(ins[n + a].shape[0])],
                          *[r.at[0] for r in (ins[2 * n + a], ins[3 * n + a], ins[4 * n + a])],
                          *[r.at[0] for r in outs[4 * a:4 * a + 4]])

    vmem = lambda k: [pl.BlockSpec(memory_space=pltpu.VMEM) for _ in range(k)]
    out_shape = []
    for w in ws:
        out_shape += [jax.ShapeDtypeStruct(w.shape, F32)] * 4
    res = pl.pallas_call(
        body, name="adamw_small_weights",
        in_specs=vmem(5 * n), out_specs=tuple(vmem(4 * n)), out_shape=tuple(out_shape),
        compiler_params=_cparams(),
    )(*firsts, *parts, *ws, *ms, *vs)
    return [res[4 * a:4 * a + 4] for a in range(n)]


def _rope_tables(positions):
    half = MLA_ROPE // 2
    freqs = ROPE_THETA ** (-jnp.arange(half, dtype=F32) / half)
    ang = positions.astype(F32).reshape(-1, 1) * freqs
    cos, sin = jnp.cos(ang), jnp.sin(ang)
    t = ang.shape[0]
    one, zero = jnp.ones((t, MLA_NOPE), F32), jnp.zeros((t, half), F32)
    tail = jnp.zeros((t, LANE - MLA_QK), F32)
    rc = jnp.concatenate([one, cos, cos, tail], axis=1)
    rsn = jnp.concatenate([0.0 * one, -sin, zero, tail], axis=1)
    rsp = jnp.concatenate([0.0 * one, zero, sin, tail], axis=1)
    return rc, rsn, rsp


def _cols_full(g):
    return g.transpose(1, 0, 2)


def kernel(x, positions, g_in, w_in, g_q, w_uq, g_kv, w_ukv, w_gla_gate, b_gla_gate, g_gla, w_proj_mla, w_proj_gla, w_out, g_final, loss_target, m_g_in, m_w_in, m_g_q, m_w_uq, m_g_kv, m_w_ukv, m_w_gla_gate, m_b_gla_gate, m_g_gla, m_w_proj_mla, m_w_proj_gla, m_w_out, m_g_final, v_g_in, v_w_in, v_g_q, v_w_uq, v_g_kv, v_w_ukv, v_w_gla_gate, v_b_gla_gate, v_g_gla, v_w_proj_mla, v_w_proj_gla, v_w_out, v_g_final):
    t = x.shape[1]
    x2d = x.reshape(t, D_MODEL)
    tgt = loss_target.reshape(t, D_MODEL)
    g_final2 = g_final.reshape(1, D_MODEL)
    sharded = [(w_in, m_w_in, v_w_in), (w_uq, m_w_uq, v_w_uq), (w_ukv, m_w_ukv, v_w_ukv),
               (w_gla_gate, m_w_gla_gate, v_w_gla_gate), (w_proj_mla, m_w_proj_mla, v_w_proj_mla),
               (w_proj_gla, m_w_proj_gla, v_w_proj_gla), (w_out, m_w_out, v_w_out)]

    w_in_t = w_in.transpose(2, 0, 1).reshape(SHARD_COLS, D_MODEL)
    everyone = tuple(range(N_DEV))
    w_in_b = w_in_t.astype(BF16)
    b_uq, b_ukv, b_gate, b_pm, b_pg, b_o = [s[0][0].astype(BF16) for s in sharded[1:]]
    stages = ((0, 2, 4, 6), (1, 3, 5, 7))
    where = {d: (k, i) for k, srcs in enumerate(stages) for i, d in enumerate(srcs)}
    g_in_1, g_uq, g_ukv, g_gate = _all_gather(
        "all_gather_first", [w_in_b, b_uq, b_ukv, b_gate], [stages[0]] + [everyone] * 3)
    w_uq_p = jnp.pad(_cols_full(g_uq), ((0, 0), (0, 0), (0, HEAD_PAD - MLA_QK))).reshape(
        MLA_Q_RANK, MLA_HEADS * HEAD_PAD)
    ukv = _cols_full(g_ukv)
    w_k_p = jnp.pad(ukv[:, :, :MLA_NOPE], ((0, 0), (0, 0), (0, HEAD_PAD - MLA_NOPE))).reshape(
        MLA_KV_RANK, MLA_HEADS * HEAD_PAD)
    w_v = ukv[:, :, MLA_NOPE:].reshape(MLA_KV_RANK, MLA_WIDTH)
    w_gate_p = jnp.pad(_cols_full(g_gate).reshape(GLA_GATE_RANK, GLA_DK),
                       ((MISC_ALR, LANE - MISC_ALR - GLA_GATE_RANK), (0, 0)))
    rc, rsn, rsp = _rope_tables(positions)

    w_lat = _weights_to_p("weights_latents", [g_in_1], where, 2)
    proj_lat, h, rstd = _inproj(x2d, g_in, w_lat)
    q, k, v, log_a, pre, cqn, ckvn, rq, rkv, misc = _mla_prep(
        proj_lat, g_q, g_kv, w_uq_p, w_k_p, w_v, w_gate_p, b_gla_gate, rc, rsn, rsp)
    o_mla, lse, (g_in_2, g_pm, g_pg, g_o) = _mla_attn_fwd(
        q, k, v, [w_in_b, b_pm, b_pg, b_o], [stages[1]] + [everyone] * 3)
    w_gla = _weights_to_p("weights_gla", [g_in_1, g_in_2], where, 0)
    proj_gla = _proj("inproj_gla", h, w_gla)
    o_gla, states = _gla_fwd(proj_gla, log_a)
    w_out_path = _weights_to_p("weights_out_path", [g_in_1, g_in_2], where, 1)
    proj_out = _proj("inproj_out_path", h, w_out_path)
    w_in_p = (w_gla, w_out_path, w_lat)
    w_pm = _cols_full(g_pm).reshape(MLA_WIDTH, D_MODEL)
    w_pg = g_pg.reshape(GLA_DV, D_MODEL)
    w_o = g_o.reshape(D_MODEL, D_MODEL)

    (dx2, do_mla, do_gla, d_out, merged, um, ug, dym, dyg, loss_p, dg_final,
     dg_gla) = _post(o_mla, proj_out, o_gla, x2d, tgt, g_gla, g_final2, w_pm, w_pg, w_o)

    p_pm = _matmul("dw_proj_mla", um, dym, 512, 512, BF16).reshape(
        MLA_WIDTH, N_DEV, D_MODEL // N_DEV).transpose(1, 0, 2)
    p_pg = _matmul("dw_proj_gla", ug, dyg, 512, 512, BF16).reshape(N_DEV, -1, D_MODEL)
    p_o = _matmul("dw_out", merged, dx2, 512, 512, BF16).reshape(N_DEV, -1, D_MODEL)
    own_in = jnp.zeros((SHARD_PAD, D_MODEL), BF16)
    land_in = lax.empty((PEERS, SHARD_PAD, D_MODEL), BF16)
    dw_groups, started, lands = {}, [], [land_in]

    def reduce_scatter_stage(s, dests, own_in, extra=()):
        parts_in, own_in = _grads_to_shards("grads_to_shards_%d" % s, dw_groups, dests, own_in)
        first = len(lands)
        lands.extend(lax.empty((PEERS,) + p.shape[1:], BF16) for p in extra)
        idx = [0] + list(range(first, len(lands)))
        all_dests = [[(i, d, r0, r1) for i, (d, ranges) in enumerate(dests) for r0, r1 in ranges]]
        all_dests += [_whole(everyone, p.shape[1]) for p in extra]
        sems, parts, new_lands, token = _ici_start(
            "ici_start_%d" % s, [parts_in] + list(extra), [lands[i] for i in idx], all_dests)
        for a, i in enumerate(idx):
            lands[i] = new_lands[a]
            started.append((sems[a][0], sems[a][1], parts[a], i, all_dests[a]))
        return own_in, token

    def late_small_stage(arrays):
        idx = list(range(len(lands), len(lands) + len(arrays)))
        lands.extend(lax.empty((PEERS,) + p.shape[1:], BF16) for p in arrays)
        all_dests = [_whole(everyone, p.shape[1]) for p in arrays]
        sems, parts, new_lands, token = _ici_start(
            "ici_start_4", list(arrays), [lands[i] for i in idx], all_dests)
        for a, i in enumerate(idx):
            lands[i] = new_lands[a]
            started.append((sems[a][0], sems[a][1], parts[a], i, all_dests[a]))
        return token

    dw_groups[1] = _matmul("dw_in_1", d_out, h, 512, 512, BF16)
    full = [(0, SHARD_PAD)]
    own_in, token = reduce_scatter_stage(
        1, [(5, full), (6, full), (7, full), (0, [(672, SHARD_PAD)]), (1, [(0, 384)]),
            (4, [(96, SHARD_PAD)])], own_in, (p_pm, p_pg, p_o))
    d_gla, dla = _gla_bwd(proj_gla, log_a, do_gla, states, token)
    dw_groups[0] = _matmul("dw_in_0", d_gla, h, 512, 512, BF16)
    own_in, token = reduce_scatter_stage(
        2, [(1, [(384, SHARD_PAD)]), (2, full), (3, full), (4, [(0, 64)])], own_in)
    dq, dk, dv = _mla_attn_bwd(q, k, v, o_mla, do_mla, lse, token)
    d_lat, dqpre, dpre, dg_q, dg_kv, db_gate = _mla_prep_bwd(
        dq, dk, dv, dla, pre, proj_lat, rq, rkv, g_q, g_kv, w_uq_p, w_k_p, w_v, w_gate_p, rc, rsn, rsp)
    dw_groups[2] = _matmul("dw_in_2", d_lat, h, 896, 512, BF16)
    own_in, token = reduce_scatter_stage(3, [(0, [(0, 672)]), (4, [(64, 96)])], own_in)
    dw_uq = _matmul("dw_uq", cqn, dqpre, MLA_Q_RANK, 512, BF16, after=token)
    p_uq = dw_uq.reshape(MLA_Q_RANK, MLA_HEADS, HEAD_PAD)[:, :, :MLA_QK].transpose(1, 0, 2)
    dw_k = _matmul("dw_uk", ckvn, dk, MLA_KV_RANK, 512, BF16)
    dw_v = _matmul("dw_uv", ckvn, dv, MLA_KV_RANK, 512, BF16)
    p_ukv = jnp.concatenate(
        [dw_k.reshape(MLA_KV_RANK, MLA_HEADS, HEAD_PAD)[:, :, :MLA_NOPE],
         dw_v.reshape(MLA_KV_RANK, MLA_HEADS, MLA_VDIM)], axis=2).transpose(1, 0, 2)
    dw_gate = _matmul("dw_gate", misc, dpre, LANE, 512, BF16)
    p_gate = dw_gate[MISC_ALR:MISC_ALR + GLA_GATE_RANK].reshape(
        GLA_GATE_RANK, N_DEV, GLA_DK // N_DEV).transpose(1, 0, 2)
    token = late_small_stage((p_uq, p_ukv, p_gate))
    grad_x, dg_in = _inproj_bwd((d_gla, d_out, d_lat), w_in_p, x2d, rstd, g_in, dx2, token)
    small = jnp.concatenate([dg_in.reshape(-1), dg_q.reshape(-1), dg_kv.reshape(-1),
                             db_gate.reshape(-1), dg_gla.reshape(-1), dg_final.reshape(-1),
                             loss_p[0, :1]])
    small = jnp.pad(small, (0, SMALL_ROWS * LANE - small.shape[0])).reshape(SMALL_ROWS, LANE)

    (small_all,) = _all_gather("all_gather_small", [small], [everyone])
    lands = _ici_wait("ici_wait", started, lands, small_all)
    big = [_adamw_transposed("adamw_w_in", own_in, lands[0], *sharded[0], 256)]
    big += _adamw_group([p_uq, p_ukv, p_gate, p_pm, p_pg, p_o], list(lands[4:7]) + list(lands[1:4]),
                        *[[s[j] for s in sharded[1:]] for j in range(3)])
    replicated = [(g_in, m_g_in, v_g_in), (g_q, m_g_q, v_g_q), (g_kv, m_g_kv, v_g_kv),
                  (b_gla_gate, m_b_gla_gate, v_b_gla_gate), (g_gla, m_g_gla, v_g_gla),
                  (g_final, m_g_final, v_g_final)]
    spacks = [jnp.pad(jnp.concatenate([s[j].reshape(-1) for s in replicated]),
                      (0, SMALL_ROWS * LANE - sum(SMALL_SIZES))).reshape(1, SMALL_ROWS, LANE)
              for j in range(3)]
    tiny = _adamw_rows("adamw_gains", small_all, spacks[0], spacks[1], spacks[2], SMALL_ROWS)

    outs = {}
    names = ("w_in", "w_uq", "w_ukv", "w_gla_gate", "w_proj_mla", "w_proj_gla", "w_out")
    for j, kind in enumerate(("grad", "delta", "new_m", "new_v")):
        for name, res in zip(names, big):
            outs[kind, name] = res[j]
        flat = tiny[j].reshape(-1)
        off = 0
        for name, size in zip(("g_in", "g_q", "g_kv", "b_gla_gate", "g_gla", "g_final"), SMALL_SIZES):
            shape = (size,) if name == "g_final" else (1, size)
            outs[kind, name] = flat[off:off + size].reshape(shape)
            off += size
    loss = tiny[0].reshape(-1)[sum(SMALL_SIZES)]
    order = ("g_in", "w_in", "g_q", "w_uq", "g_kv", "w_ukv", "w_gla_gate", "b_gla_gate", "g_gla",
             "w_proj_mla", "w_proj_gla", "w_out", "g_final")
    result = [loss, grad_x.reshape(1, t, D_MODEL)]
    for kind in ("grad", "delta", "new_m", "new_v"):
        result += [outs[kind, name] for name in order]
    return tuple(result)
```

```python
import jax
import jax.numpy as jnp
from jax import lax
from jax.experimental import pallas as pl
from jax.experimental.pallas import tpu as pltpu

F32 = jnp.float32
BF16 = jnp.bfloat16
MESH = pl.DeviceIdType.MESH
N_DEV = 8

D_MODEL = 1024
EPS = 1e-6
MLA_HEADS = 8
MLA_NOPE = 64
MLA_ROPE = 32
MLA_VDIM = 64
MLA_Q_RANK = 384
MLA_KV_RANK = 256
MLA_QK = MLA_NOPE + MLA_ROPE
MLA_WIDTH = MLA_HEADS * MLA_VDIM
ROPE_THETA = 10000.0
GLA_HEADS = 4
GLA_DK = 512
GLA_DV = 1024
GLA_HK = 128
GLA_HV = 256
GLA_GATE_RANK = 16
GLA_GATE_NORM = 16.0
GLA_CHUNK = 64
GLA_CHUNKS_PER_STEP = 8
D_IN = 6320

ADAM_LR = 0.001
ADAM_B1 = 0.9
ADAM_B2 = 0.999
ADAM_EPS = 1e-08
ADAM_WD = 0.01
ADAM_STEP = 10

LANE = 128
HEAD_PAD = 128
VMEM_LIMIT = 48 * 1024 * 1024

P_VG, P_QG, P_KG = 0, 1024, 1536
P_ZGLA, P_GMLA, P_GGLA, P_ZMLA = 2048, 3072, 4096, 5120
P_CQ, P_CKV, P_MISC = 5632, 6144, 6400
P_TOTAL = 6528
P_GROUPS = ((0, 2048), (2048, 3584), (5632, 896))
MISC_KR = 64
MISC_ALR = 96
SHARD_COLS = D_IN // N_DEV
SHARD_PAD = 800
P_COMPONENTS = ((0, 384, P_CQ), (384, 256, P_CKV), (640, 32, P_MISC + MISC_KR), (672, 512, P_ZMLA),
                (1184, 512, P_QG), (1696, 512, P_KG), (2208, 1024, P_VG),
                (3232, 16, P_MISC + MISC_ALR), (3248, 1024, P_ZGLA), (4272, 1024, P_GMLA),
                (5296, 1024, P_GGLA))

SMALL_SIZES = (1024, 384, 256, 512, 256, 1024)
SMALL_ROWS = 32


def _segments():
    segs = []
    for g0, n, p0 in P_COMPONENTS:
        g = g0
        while g < g0 + n:
            d = g // SHARD_COLS
            end = min(g0 + n, (d + 1) * SHARD_COLS)
            segs.append((d, g - d * SHARD_COLS, end - g, p0 + g - g0))
            g = end
    return segs


def _group_of(p0):
    return max(i for i, (off, _) in enumerate(P_GROUPS) if off <= p0)


def _rel(p0):
    return p0 - P_GROUPS[_group_of(p0)][0]


def _cparams(sem=None):
    if sem is None:
        return pltpu.CompilerParams(vmem_limit_bytes=VMEM_LIMIT)
    return pltpu.CompilerParams(dimension_semantics=sem, vmem_limit_bytes=VMEM_LIMIT)


def _sigmoid(v):
    return 1.0 / (1.0 + jnp.exp(-v))


def _dot(a, b):
    return jnp.dot(a, b, preferred_element_type=F32)


def _dot_nt(a, b):
    return lax.dot_general(a, b, (((1,), (1,)), ((), ())), preferred_element_type=F32)


def _dot_tn(a, b):
    return lax.dot_general(a, b, (((0,), (0,)), ((), ())), preferred_element_type=F32)


def _dot_exact(a, b):
    return jnp.dot(a, b, preferred_element_type=F32, precision=lax.Precision.HIGHEST)


def _rope_fwd(blk, c, sn, sp):
    return blk * c + pltpu.roll(blk, LANE - 16, 1) * sn + pltpu.roll(blk, 16, 1) * sp


def _rope_bwd(blk, c, sn, sp):
    return blk * c + pltpu.roll(blk * sn, 16, 1) + pltpu.roll(blk * sp, LANE - 16, 1)


def _mesh_pos():
    return lax.axis_index("x"), lax.axis_index("y"), lax.axis_index("c")


def _hbm_specs(n):
    return [pl.BlockSpec(memory_space=pltpu.HBM) for _ in range(n)]


def _dev(d):
    return d >> 2, (d >> 1) & 1, d & 1


def _gather_plan(shards, sources):
    na, most = len(shards), max(len(s) for s in sources)
    out_shape = [jax.ShapeDtypeStruct((len(srcs),) + s.shape, s.dtype)
                 for s, srcs in zip(shards, sources)]
    sems = [pltpu.SemaphoreType.DMA((na, most)) for _ in range(3)]
    sems += [pltpu.SemaphoreType.DMA((na, most, 3))]
    sems += [pltpu.SemaphoreType.DMA((na, most)) for _ in range(3)]
    return out_shape, sems


def _gather_hooks(x_refs, out_refs, sems, sources):
    local_sems, d2d_send, d2d_recv, ici_send, ici_recv, fwd_send, fwd_recv = sems
    x, y, c = _mesh_pos()
    chips = [(1 - x, y), (x, 1 - y), (1 - x, 1 - y)]
    items = []
    for a, srcs in enumerate(sources):
        for i, d in enumerate(srcs):
            dx, dy, dc = _dev(d)
            near = jnp.logical_and(x == dx, y == dy)
            far = jnp.logical_not(near)
            slot = out_refs[a].at[i]

            def remote(src, to, send_sem, recv_sem, slot=slot):
                return pltpu.make_async_remote_copy(
                    src_ref=src, dst_ref=slot, send_sem=send_sem, recv_sem=recv_sem,
                    device_id=to, device_id_type=MESH)

            items.append(dict(
                me=jnp.logical_and(near, c == dc), sibling=jnp.logical_and(near, c != dc),
                relay=jnp.logical_and(far, c == dc), behind=jnp.logical_and(far, c != dc),
                local=pltpu.make_async_copy(x_refs[a], slot, local_sems.at[a, i]),
                to_sibling=remote(x_refs[a], (x, y, 1 - c), d2d_send.at[a, i], d2d_recv.at[a, i]),
                to_chips=[remote(x_refs[a], (*chip, c), ici_send.at[a, i, j], ici_recv.at[a, i])
                          for j, chip in enumerate(chips)],
                forward=remote(slot, (x, y, 1 - c), fwd_send.at[a, i], fwd_recv.at[a, i])))

    def start():
        for it in items:
            @pl.when(it["me"])
            def _(it=it):
                it["local"].start()
                it["to_sibling"].start()
                for cp in it["to_chips"]:
                    cp.start()

    def finish():
        for it in items:
            @pl.when(it["relay"])
            def _(it=it):
                it["to_chips"][0].wait_recv()
                it["forward"].start()
        for it in items:
            pl.when(it["sibling"])(it["to_sibling"].wait_recv)
            pl.when(it["behind"])(it["forward"].wait_recv)
            pl.when(it["relay"])(it["forward"].wait_send)

            @pl.when(it["me"])
            def _(it=it):
                it["local"].wait()
                it["to_sibling"].wait_send()
                for cp in it["to_chips"]:
                    cp.wait_send()

    return start, finish


def _all_gather(name, shards, sources, after=None):
    n = len(shards)
    out_shape, sems = _gather_plan(shards, sources)
    extra = [] if after is None else [after]

    def body(*refs):
        k = n + len(extra)
        start, finish = _gather_hooks(refs[:n], refs[k:k + n], refs[k + n:], sources)
        start()
        finish()

    return pl.pallas_call(
        body, name=name,
        out_shape=tuple(out_shape),
        in_specs=_hbm_specs(n) + [pl.BlockSpec(memory_space=pl.ANY) for _ in extra],
        out_specs=tuple(_hbm_specs(n)),
        scratch_shapes=sems,
        compiler_params=_cparams(),
    )(*shards, *extra)


PEERS = N_DEV - 1


def _spread_copies(x_ref, land_ref, send_sems, recv_sems):
    x, y, c = _mesh_pos()
    sends, arrivals = [], []
    for k in range(1, N_DEV):
        px, py, pc = x ^ (k >> 2), y ^ ((k >> 1) & 1), c ^ (k & 1)
        sends.append(pltpu.make_async_remote_copy(
            src_ref=x_ref, dst_ref=land_ref.at[4 * x + 2 * y + c], send_sem=send_sems.at[k - 1],
            recv_sem=recv_sems.at[k - 1], device_id=(px, py, pc), device_id_type=MESH))
        arrivals.append(pltpu.make_async_remote_copy(
            src_ref=x_ref, dst_ref=land_ref.at[4 * px + 2 * py + pc], send_sem=send_sems.at[k - 1],
            recv_sem=recv_sems.at[k - 1], device_id=(px, py, pc), device_id_type=MESH))
    return sends, arrivals


def _spread_start(name, shards):
    ns = len(shards)
    lands = [lax.empty((N_DEV,) + s.shape, s.dtype) for s in shards]

    def body(*refs):
        x_refs, land_refs, sems = refs[:ns], refs[ns:2 * ns], refs[2 * ns:4 * ns]
        for a in range(ns):
            sends, _ = _spread_copies(x_refs[a], land_refs[a], sems[2 * a], sems[2 * a + 1])
            for cp in sends:
                cp.start()
        refs[-1][...] = jnp.zeros_like(refs[-1])

    hbm, sem = pl.BlockSpec(memory_space=pltpu.HBM), pl.BlockSpec(memory_space=pltpu.SEMAPHORE)
    res = pl.pallas_call(
        body, name=name,
        out_shape=tuple(pltpu.SemaphoreType.DMA((PEERS,)) for _ in range(2 * ns))
        + tuple(pltpu.HBM(v.shape, v.dtype) for v in list(shards) + lands)
        + (jax.ShapeDtypeStruct((8, LANE), F32),),
        in_specs=(hbm,) * (2 * ns),
        out_specs=(sem,) * (2 * ns) + (hbm,) * (2 * ns) + (pl.BlockSpec(memory_space=pltpu.VMEM),),
        input_output_aliases={i: 2 * ns + i for i in range(2 * ns)},
        compiler_params=pltpu.CompilerParams(
            has_side_effects=pltpu.SideEffectType.DATAFLOW_SIDE_EFFECTING,
            vmem_limit_bytes=VMEM_LIMIT),
    )(*[pltpu.with_memory_space_constraint(v, pltpu.HBM) for v in list(shards) + lands])
    sems = [(res[2 * a], res[2 * a + 1]) for a in range(ns)]
    return sems, res[2 * ns:3 * ns], res[3 * ns:4 * ns], res[-1]


def _spread_wait(name, sems, shards, lands, after):
    ns = len(shards)

    def body(*refs):
        for a in range(ns):
            x_ref, send_sems, recv_sems, land_ref = refs[4 * a:4 * a + 4]
            sends, arrivals = _spread_copies(x_ref, land_ref, send_sems, recv_sems)
            for cp in sends:
                cp.wait_send()
            for cp in arrivals:
                cp.wait_recv()

    hbm, sem = pl.BlockSpec(memory_space=pltpu.HBM), pl.BlockSpec(memory_space=pltpu.SEMAPHORE)
    operands, specs = [], []
    for a in range(ns):
        operands += [shards[a], sems[a][0], sems[a][1], lands[a]]
        specs += [hbm, sem, sem, hbm]
    return pl.pallas_call(
        body, name=name,
        out_shape=tuple(pltpu.HBM(v.shape, v.dtype) for v in lands),
        in_specs=tuple(specs) + (pl.BlockSpec(memory_space=pl.ANY),),
        out_specs=(hbm,) * ns,
        input_output_aliases={4 * a + 3: a for a in range(ns)},
        compiler_params=pltpu.CompilerParams(
            has_side_effects=pltpu.SideEffectType.DATAFLOW_SIDE_EFFECTING,
            vmem_limit_bytes=VMEM_LIMIT),
    )(*operands, after)


def _whole(dests, rows):
    return [(i, d, 0, rows) for i, d in enumerate(dests)]


def _ici_copies(p_ref, land_ref, send_sems, recv_sems, pieces):
    x, y, c = _mesh_pos()
    sends, arrivals = [], []

    def rows_of(ref, j, r0, r1):
        return ref.at[j] if (r0, r1) == (0, ref.shape[1]) else ref.at[j, pl.ds(r0, r1 - r0)]

    for p, (i, d, r0, r1) in enumerate(pieces):
        dx, dy, dc = _dev(d)
        k = (4 * (x != dx).astype(jnp.int32) + 2 * (y != dy).astype(jnp.int32)
             + (c != dc).astype(jnp.int32))
        slot = jnp.maximum(k - 1, 0)
        sends.append((k > 0, pltpu.make_async_remote_copy(
            src_ref=rows_of(p_ref, i, r0, r1), dst_ref=rows_of(land_ref, slot, r0, r1),
            send_sem=send_sems.at[p], recv_sem=recv_sems.at[p * PEERS + slot],
            device_id=(dx, dy, dc), device_id_type=MESH)))
        arrivals.append((k == 0, [pltpu.make_async_remote_copy(
            src_ref=rows_of(p_ref, i, r0, r1), dst_ref=rows_of(land_ref, r, r0, r1),
            send_sem=send_sems.at[p], recv_sem=recv_sems.at[p * PEERS + r],
            device_id=(dx, dy, dc), device_id_type=MESH) for r in range(PEERS)]))
    return sends, arrivals


def _ici_start(name, hs, lands, dests):
    na = len(hs)

    def body(*refs):
        h_refs, land_refs, sems = refs[:na], refs[na:2 * na], refs[2 * na:4 * na]
        token = refs[-1]
        for a in range(na):
            sends, _ = _ici_copies(h_refs[a], land_refs[a], sems[2 * a], sems[2 * a + 1], dests[a])
            for go, cp in sends:
                pl.when(go)(cp.start)
        token[...] = jnp.zeros_like(token)

    hbm, sem = pl.BlockSpec(memory_space=pltpu.HBM), pl.BlockSpec(memory_space=pltpu.SEMAPHORE)
    sem_shapes = []
    for a in range(na):
        sem_shapes += [pltpu.SemaphoreType.DMA((len(dests[a]),)),
                       pltpu.SemaphoreType.DMA((len(dests[a]) * PEERS,))]
    res = pl.pallas_call(
        body, name=name,
        out_shape=tuple(sem_shapes) + tuple(pltpu.HBM(v.shape, v.dtype) for v in list(hs) + list(lands))
        + (jax.ShapeDtypeStruct((8, LANE), F32),),
        in_specs=(hbm,) * (2 * na),
        out_specs=(sem,) * (2 * na) + (hbm,) * (2 * na) + (pl.BlockSpec(memory_space=pltpu.VMEM),),
        input_output_aliases={i: 2 * na + i for i in range(2 * na)},
        compiler_params=pltpu.CompilerParams(
            has_side_effects=pltpu.SideEffectType.DATAFLOW_SIDE_EFFECTING,
            vmem_limit_bytes=VMEM_LIMIT),
    )(*[pltpu.with_memory_space_constraint(v, pltpu.HBM) for v in list(hs) + list(lands)])
    sems = [(res[2 * a], res[2 * a + 1]) for a in range(na)]
    return sems, res[2 * na:3 * na], res[3 * na:4 * na], res[-1]


def _ici_wait(name, started, lands, after):
    k, nl = len(started), len(lands)

    def body(*refs):
        land_refs = refs[3 * k:3 * k + nl]
        for s in range(k):
            h_ref, send_sems, recv_sems = refs[3 * s:3 * s + 3]
            sends, arrivals = _ici_copies(h_ref, land_refs[started[s][3]], send_sems, recv_sems,
                                          started[s][4])
            for go, cp in sends:
                pl.when(go)(cp.wait_send)
            for here, cps in arrivals:
                for cp in cps:
                    pl.when(here)(cp.wait_recv)

    hbm, sem = pl.BlockSpec(memory_space=pltpu.HBM), pl.BlockSpec(memory_space=pltpu.SEMAPHORE)
    operands, specs = [], []
    for send_sems, recv_sems, h, _, _ in started:
        operands += [h, send_sems, recv_sems]
        specs += [hbm, sem, sem]
    return pl.pallas_call(
        body, name=name,
        out_shape=tuple(pltpu.HBM(v.shape, v.dtype) for v in lands),
        in_specs=tuple(specs) + (hbm,) * nl + (pl.BlockSpec(memory_space=pl.ANY),),
        out_specs=(hbm,) * nl,
        input_output_aliases={3 * k + i: i for i in range(nl)},
        compiler_params=pltpu.CompilerParams(
            has_side_effects=pltpu.SideEffectType.DATAFLOW_SIDE_EFFECTING,
            vmem_limit_bytes=VMEM_LIMIT),
    )(*operands, *lands, after)


def _weights_to_p(name, gathered, where, group):
    tl = 256
    off, width = P_GROUPS[group]
    segs = sorted([s for s in _segments() if _group_of(s[3]) == group], key=lambda s: s[3])
    used = sorted({where[s[0]][0] for s in segs})

    def body(*refs):
        g_refs, o_ref = dict(zip(used, refs[:-1])), refs[-1]
        pieces, pos = [], off
        for d, c0, n, p0 in segs:
            if p0 > pos:
                pieces.append(jnp.zeros((p0 - pos, tl), F32))
            k, slot = where[d]
            pieces.append(g_refs[k][slot, c0:c0 + n, :].astype(F32))
            pos = p0 + n
        if off + width > pos:
            pieces.append(jnp.zeros((off + width - pos, tl), F32))
        o_ref[...] = jnp.concatenate(pieces, axis=0).astype(BF16)

    return pl.pallas_call(
        body, name=name,
        grid=(D_MODEL // tl,),
        in_specs=[pl.BlockSpec((gathered[k].shape[0], SHARD_COLS, tl), lambda i: (0, 0, i))
                  for k in used],
        out_specs=pl.BlockSpec((width, tl), lambda i: (0, i)),
        out_shape=jax.ShapeDtypeStruct((width, D_MODEL), BF16),
        compiler_params=_cparams(("arbitrary",)),
    )(*[gathered[k] for k in used])


def _shard_groups(d):
    return sorted({_group_of(s[3]) for s in _segments() if s[0] == d})


def _grads_to_shards(name, groups, dests, own_prev):
    tl = 256
    segs = _segments()
    used = sorted(groups)

    def body(*refs):
        g_refs, prev_ref, o_ref, own_ref = dict(zip(used, refs[:-3])), refs[-3], refs[-2], refs[-1]
        x, y, c = _mesh_pos()
        own = prev_ref[...].astype(F32)
        row = lax.broadcasted_iota(jnp.int32, (SHARD_PAD, tl), 0)
        for i, (d, ranges) in enumerate(dests):
            pieces, pos, asked = [], 0, None
            for r0, r1 in sorted(ranges):
                if r0 > pos:
                    pieces.append(jnp.zeros((r0 - pos, tl), F32))
                for _, c0, n, p0 in sorted([s for s in segs if s[0] == d], key=lambda s: s[1]):
                    a, b = max(c0, r0), min(c0 + n, r1)
                    if a < b:
                        gi = _group_of(p0)
                        lo = p0 - P_GROUPS[gi][0] + a - c0
                        pieces.append(g_refs[gi][lo:lo + b - a, :].astype(F32))
                if r1 > SHARD_COLS:
                    pieces.append(jnp.zeros((r1 - max(r0, SHARD_COLS), tl), F32))
                pos = r1
                inside = jnp.logical_and(row >= r0, row < r1)
                asked = inside if asked is None else jnp.logical_or(asked, inside)
            if pos < SHARD_PAD:
                pieces.append(jnp.zeros((SHARD_PAD - pos, tl), F32))
            shard = jnp.concatenate(pieces, axis=0)
            o_ref[i] = shard.astype(BF16)
            own = jnp.where(jnp.logical_and(4 * x + 2 * y + c == d, asked), shard, own)
        own_ref[...] = own.astype(BF16)

    blk = pl.BlockSpec((SHARD_PAD, tl), lambda i: (0, i))
    return pl.pallas_call(
        body, name=name,
        grid=(D_MODEL // tl,),
        in_specs=[pl.BlockSpec((P_GROUPS[g][1], tl), lambda i: (0, i)) for g in used] + [blk],
        out_specs=(pl.BlockSpec((len(dests), SHARD_PAD, tl), lambda i: (0, 0, i)), blk),
        out_shape=(jax.ShapeDtypeStruct((len(dests), SHARD_PAD, D_MODEL), BF16),
                   jax.ShapeDtypeStruct((SHARD_PAD, D_MODEL), BF16)),
        input_output_aliases={len(used): 1},
        compiler_params=_cparams(("arbitrary",)),
    )(*[groups[g] for g in used], own_prev)


def _inproj(x, g_in, w_pt):
    t = x.shape[0]
    tm = min(256, t)
    width = w_pt.shape[0]

    def body(x_ref, g_ref, w_ref, proj_ref, h_ref, r_ref):
        xf = x_ref[...]
        r = lax.rsqrt(jnp.mean(xf * xf, axis=-1, keepdims=True) + EPS)
        h = ((xf * r) * g_ref[...]).astype(BF16)
        proj_ref[...] = _dot_nt(h, w_ref[...])
        h_ref[...] = h
        r_ref[...] = r

    row = lambda w: pl.BlockSpec((tm, w), lambda i: (i, 0))
    return pl.pallas_call(
        body, name="inproj_latents",
        grid=(t // tm,),
        in_specs=[row(D_MODEL), pl.BlockSpec((1, D_MODEL), lambda i: (0, 0)),
                  pl.BlockSpec((width, D_MODEL), lambda i: (0, 0))],
        out_specs=(row(width), row(D_MODEL), row(1)),
        out_shape=(jax.ShapeDtypeStruct((t, width), F32),
                   jax.ShapeDtypeStruct((t, D_MODEL), BF16),
                   jax.ShapeDtypeStruct((t, 1), F32)),
        compiler_params=_cparams(("arbitrary",)),
    )(x, g_in, w_pt)


def _proj(name, h, w_pt):
    t = h.shape[0]
    tm = min(256, t)
    width = w_pt.shape[0]

    def body(h_ref, w_ref, o_ref):
        o_ref[...] = _dot_nt(h_ref[...], w_ref[...])

    return pl.pallas_call(
        body, name=name,
        grid=(t // tm,),
        in_specs=[pl.BlockSpec((tm, D_MODEL), lambda i: (i, 0)),
                  pl.BlockSpec((width, D_MODEL), lambda i: (0, 0))],
        out_specs=pl.BlockSpec((tm, width), lambda i: (i, 0)),
        out_shape=jax.ShapeDtypeStruct((t, width), F32),
        compiler_params=_cparams(("arbitrary",)),
    )(h, w_pt)


def _mla_prep(proj, g_q, g_kv, w_uq_p, w_k_p, w_v, w_gate_p, b_gate, rc, rsn, rsp):
    t = proj.shape[0]
    tm = min(256, t)
    hq = MLA_HEADS * HEAD_PAD

    def body(cq_ref, ckv_ref, misc_ref, gq_ref, gkv_ref, wuq_ref, wk_ref, wv_ref, wg_ref, bg_ref,
             c_ref, sn_ref, sp_ref,
             q_ref, k_ref, v_ref, la_ref, pre_ref, cqn_ref, ckvn_ref, rq_ref, rkv_ref, mb_ref):
        c, sn, sp = c_ref[...], sn_ref[...], sp_ref[...]
        cq = cq_ref[:, :MLA_Q_RANK]
        rq = lax.rsqrt(jnp.mean(cq * cq, axis=-1, keepdims=True) + EPS)
        cqn = ((cq * rq) * gq_ref[...]).astype(BF16)
        cqn_ref[...] = cqn
        rq_ref[...] = rq
        qpre = _dot(cqn, wuq_ref[...])
        ckv = ckv_ref[...]
        rkv = lax.rsqrt(jnp.mean(ckv * ckv, axis=-1, keepdims=True) + EPS)
        ckvn = ((ckv * rkv) * gkv_ref[...]).astype(BF16)
        ckvn_ref[...] = ckvn
        rkv_ref[...] = rkv
        kn = _dot(ckvn, wk_ref[...])
        v_ref[...] = _dot(ckvn, wv_ref[...]).astype(BF16)
        misc = misc_ref[...]
        krope = _rope_fwd(misc, c, sn, sp)
        for h in range(MLA_HEADS):
            sl = slice(h * HEAD_PAD, (h + 1) * HEAD_PAD)
            q_ref[:, sl] = _rope_fwd(qpre[:, sl], c, sn, sp).astype(BF16)
            k_ref[:, sl] = (kn[:, sl] + krope).astype(BF16)
        mb_ref[...] = misc.astype(BF16)
        pre = _dot(mb_ref[...], wg_ref[...]) + bg_ref[...]
        pre_ref[...] = pre
        log_a = (jnp.minimum(pre, 0.0) - jnp.log(1.0 + jnp.exp(-jnp.abs(pre)))) / GLA_GATE_NORM
        la_ref[...] = _dot_exact(_chunk_tri(tm, True), log_a)

    row = lambda w: pl.BlockSpec((tm, w), lambda i: (i, 0))
    full = lambda a: pl.BlockSpec(a.shape, lambda i: (0, 0))
    return pl.pallas_call(
        body, name="mla_prep",
        grid=(t // tm,),
        in_specs=[pl.BlockSpec((tm, 512), lambda i: (i, _rel(P_CQ) // 512)),
                  pl.BlockSpec((tm, MLA_KV_RANK), lambda i: (i, _rel(P_CKV) // MLA_KV_RANK)),
                  pl.BlockSpec((tm, LANE), lambda i: (i, _rel(P_MISC) // LANE)),
                  full(g_q), full(g_kv), full(w_uq_p), full(w_k_p), full(w_v), full(w_gate_p),
                  full(b_gate), row(LANE), row(LANE), row(LANE)],
        out_specs=(row(hq), row(hq), row(MLA_WIDTH), row(GLA_DK), row(GLA_DK),
                   row(MLA_Q_RANK), row(MLA_KV_RANK), row(1), row(1), row(LANE)),
        out_shape=(jax.ShapeDtypeStruct((t, hq), BF16), jax.ShapeDtypeStruct((t, hq), BF16),
                   jax.ShapeDtypeStruct((t, MLA_WIDTH), BF16),
                   jax.ShapeDtypeStruct((t, GLA_DK), F32), jax.ShapeDtypeStruct((t, GLA_DK), F32),
                   jax.ShapeDtypeStruct((t, MLA_Q_RANK), BF16),
                   jax.ShapeDtypeStruct((t, MLA_KV_RANK), BF16),
                   jax.ShapeDtypeStruct((t, 1), F32), jax.ShapeDtypeStruct((t, 1), F32),
                   jax.ShapeDtypeStruct((t, LANE), BF16)),
        compiler_params=_cparams(("arbitrary",)),
    )(proj, proj, proj, g_q, g_kv, w_uq_p, w_k_p, w_v, w_gate_p, b_gate, rc, rsn, rsp)


def _attn_masks(tq, i):
    keys = (i + 1) * tq
    rows = i * tq + lax.broadcasted_iota(jnp.int32, (tq, keys), 0)
    cols = lax.broadcasted_iota(jnp.int32, (tq, keys), 1)
    lane = lax.broadcasted_iota(jnp.int32, (tq, LANE), 1)
    return cols <= rows, lane < MLA_VDIM


def _for_each_query_tile(n_tiles, fn):
    for i in range(n_tiles):
        pl.when(pl.program_id(1) == i)(lambda i=i: fn(i))


def _mla_attn_fwd(q, k, v, shards, sources):
    t = q.shape[0]
    tq = min(256, t)
    scale = MLA_QK ** -0.5
    ns = len(shards)
    g_shapes, g_sems = _gather_plan(shards, sources)
    grid = (MLA_HEADS // 2, t // tq)

    def body(q_ref, k_ref, v_ref, *rest):
        o_ref, lse_ref = rest[ns:ns + 2]
        start, finish = _gather_hooks(rest[:ns], rest[ns + 2:2 * ns + 2], rest[2 * ns + 2:], sources)
        step = pl.program_id(0) * grid[1] + pl.program_id(1)
        pl.when(step == 0)(start)

        def tile(i):
            keys = (i + 1) * tq
            causal, low = _attn_masks(tq, i)
            vp = v_ref[0:keys, :]
            acc = jnp.zeros((tq, LANE), F32)
            for hh in range(2):
                sl = slice(hh * HEAD_PAD, (hh + 1) * HEAD_PAD)
                s = _dot_nt(q_ref[:, sl], k_ref[0:keys, sl]) * scale
                s = jnp.where(causal, s, -jnp.inf)
                m = jnp.max(s, axis=-1, keepdims=True)
                e = jnp.exp(s - m)
                l = jnp.sum(e, axis=-1, keepdims=True)
                o = _dot(e.astype(BF16), vp) / l
                acc = jnp.where(low if hh == 0 else jnp.logical_not(low), o, acc)
                lse_ref[hh] = m + jnp.log(l)
            o_ref[...] = acc

        _for_each_query_tile(t // tq, tile)
        pl.when(step == grid[0] * grid[1] - 1)(finish)

    res = pl.pallas_call(
        body, name="mla_attn_fwd",
        grid=grid,
        in_specs=[pl.BlockSpec((tq, 2 * HEAD_PAD), lambda p, i: (i, p)),
                  pl.BlockSpec((t, 2 * HEAD_PAD), lambda p, i: (0, p)),
                  pl.BlockSpec((t, LANE), lambda p, i: (0, p))] + _hbm_specs(ns),
        out_specs=(pl.BlockSpec((tq, LANE), lambda p, i: (i, p)),
                   pl.BlockSpec((2, tq, 1), lambda p, i: (p, i, 0))) + tuple(_hbm_specs(ns)),
        out_shape=(jax.ShapeDtypeStruct((t, MLA_WIDTH), F32),
                   jax.ShapeDtypeStruct((MLA_HEADS, t, 1), F32)) + tuple(g_shapes),
        scratch_shapes=g_sems,
        compiler_params=_cparams(("arbitrary", "arbitrary")),
    )(q, k, v, *shards)
    return res[0], res[1], res[2:]


def _mla_attn_bwd(q, k, v, o, do, lse, after):
    t = q.shape[0]
    tq = min(256, t)
    scale = MLA_QK ** -0.5

    def body(q_ref, k_ref, v_ref, o_ref, do_ref, lse_ref, after_ref, dq_ref, dk_ref, dv_ref):
        del after_ref

        @pl.when(pl.program_id(1) == 0)
        def _():
            dk_ref[...] = jnp.zeros_like(dk_ref)
            dv_ref[...] = jnp.zeros_like(dv_ref)

        def tile(i):
            keys = (i + 1) * tq
            causal, low = _attn_masks(tq, i)
            vp = v_ref[0:keys, :]
            do_all = do_ref[...]
            o_all = o_ref[...]
            dv_acc = jnp.zeros((keys, LANE), F32)
            for hh in range(2):
                sl = slice(hh * HEAD_PAD, (hh + 1) * HEAD_PAD)
                do_h = jnp.where(low if hh == 0 else jnp.logical_not(low), do_all, 0.0)
                dsum = jnp.sum(do_h * o_all, axis=-1, keepdims=True)
                qh = q_ref[:, sl]
                kh = k_ref[0:keys, sl]
                s = _dot_nt(qh, kh) * scale
                p = jnp.where(causal, jnp.exp(s - lse_ref[hh]), 0.0)
                do_b = do_h.astype(BF16)
                dp = _dot_nt(do_b, vp)
                ds = (p * (dp - dsum) * scale).astype(BF16)
                dq_ref[:, sl] = _dot(ds, kh).astype(BF16)
                dk_ref[0:keys, sl] += _dot_tn(ds, qh)
                dv_acc = dv_acc + _dot_tn(p.astype(BF16), do_b)
            dv_ref[0:keys, :] += dv_acc

        _for_each_query_tile(t // tq, tile)

    return pl.pallas_call(
        body, name="mla_attn_bwd",
        grid=(MLA_HEADS // 2, t // tq),
        in_specs=[pl.BlockSpec((tq, 2 * HEAD_PAD), lambda p, i: (i, p)),
                  pl.BlockSpec((t, 2 * HEAD_PAD), lambda p, i: (0, p)),
                  pl.BlockSpec((t, LANE), lambda p, i: (0, p)),
                  pl.BlockSpec((tq, LANE), lambda p, i: (i, p)),
                  pl.BlockSpec((tq, LANE), lambda p, i: (i, p)),
                  pl.BlockSpec((2, tq, 1), lambda p, i: (p, i, 0)),
                  pl.BlockSpec(memory_space=pl.ANY)],
        out_specs=(pl.BlockSpec((tq, 2 * HEAD_PAD), lambda p, i: (i, p)),
                   pl.BlockSpec((t, 2 * HEAD_PAD), lambda p, i: (0, p)),
                   pl.BlockSpec((t, LANE), lambda p, i: (0, p))),
        out_shape=(jax.ShapeDtypeStruct((t, MLA_HEADS * HEAD_PAD), BF16),
                   jax.ShapeDtypeStruct((t, MLA_HEADS * HEAD_PAD), F32),
                   jax.ShapeDtypeStruct((t, MLA_WIDTH), F32)),
        compiler_params=_cparams(("arbitrary", "arbitrary")),
    )(q, k, v, o, do, lse, after)


def _chunk_tri(n, lower):
    r = lax.broadcasted_iota(jnp.int32, (n, n), 0)
    c = lax.broadcasted_iota(jnp.int32, (n, n), 1)
    same = (r // GLA_CHUNK) == (c // GLA_CHUNK)
    return jnp.where(jnp.logical_and(same, r >= c if lower else r <= c), 1.0, 0.0).astype(F32)


def _gla_chunk_terms(q_ref, k_ref, b_ref, h, rows):
    sl = slice(h * GLA_HK, (h + 1) * GLA_HK)
    b = b_ref[rows, sl]
    bl = b[GLA_CHUNK - 1:GLA_CHUNK, :]
    kc = k_ref[rows, sl]
    q_in = (q_ref[rows, sl] * (GLA_HK ** -0.5)) * jnp.exp(b)
    k_in = kc * jnp.exp(-b)
    k_st = kc * jnp.exp(bl - b)
    return b, bl, q_in, k_in, k_st


def _tri(c, lower):
    r = lax.broadcasted_iota(jnp.int32, (c, c), 0)
    cc = lax.broadcasted_iota(jnp.int32, (c, c), 1)
    return jnp.where(r >= cc if lower else r <= cc, 1.0, 0.0).astype(F32)


def _gla_fwd(proj, log_a):
    t = proj.shape[0]
    per = GLA_CHUNKS_PER_STEP
    n = t // GLA_CHUNK
    c = GLA_CHUNK * per

    def body(q_ref, k_ref, v_ref, la_ref, o_ref, sp_ref, st_ref):
        @pl.when(pl.program_id(0) == 0)
        def _():
            st_ref[...] = jnp.zeros_like(st_ref)

        tri = _tri(GLA_CHUNK, True)
        for s, h in [(s, h) for s in range(per) for h in range(GLA_HEADS)]:
            rows = slice(s * GLA_CHUNK, (s + 1) * GLA_CHUNK)
            _, bl, q_in, k_in, k_st = _gla_chunk_terms(q_ref, k_ref, la_ref, h, rows)
            vs = slice(h * GLA_HV, (h + 1) * GLA_HV)
            vv = v_ref[rows, vs].astype(BF16)
            qb = q_in.astype(BF16)
            attn = _dot_nt(qb, k_in.astype(BF16)) * tri
            st = st_ref[h]
            sp_ref[s, h] = st
            o_ref[rows, vs] = _dot(attn.astype(BF16), vv) + _dot_nt(qb, st.astype(BF16))
            st_ref[h] = st * jnp.exp(bl) + _dot_tn(vv, k_st.astype(BF16))

    return pl.pallas_call(
        body, name="gla_fwd",
        grid=(n // per,),
        in_specs=[pl.BlockSpec((c, GLA_DK), lambda i: (i, P_QG // GLA_DK)),
                  pl.BlockSpec((c, GLA_DK), lambda i: (i, P_KG // GLA_DK)),
                  pl.BlockSpec((c, GLA_DV), lambda i: (i, P_VG // GLA_DV)),
                  pl.BlockSpec((c, GLA_DK), lambda i: (i, 0))],
        out_specs=(pl.BlockSpec((c, GLA_DV), lambda i: (i, 0)),
                   pl.BlockSpec((per, GLA_HEADS, GLA_HV, GLA_HK), lambda i: (i, 0, 0, 0))),
        out_shape=(jax.ShapeDtypeStruct((t, GLA_DV), F32),
                   jax.ShapeDtypeStruct((n, GLA_HEADS, GLA_HV, GLA_HK), F32)),
        scratch_shapes=[pltpu.VMEM((GLA_HEADS, GLA_HV, GLA_HK), F32)],
        compiler_params=_cparams(("arbitrary",)),
    )(proj, proj, proj, log_a)


def _gla_bwd(proj, log_a, do, states, after):
    t = proj.shape[0]
    per = GLA_CHUNKS_PER_STEP
    c = GLA_CHUNK * per
    n = t // c

    def body(q_ref, k_ref, v_ref, la_ref, do_ref, sp_ref, after_ref, dg_ref, dla_ref, ds_ref):
        del after_ref

        @pl.when(pl.program_id(0) == 0)
        def _():
            ds_ref[...] = jnp.zeros_like(ds_ref)

        tri = _tri(GLA_CHUNK, True)
        last = lax.broadcasted_iota(jnp.int32, (GLA_CHUNK, GLA_HK), 0) == GLA_CHUNK - 1
        for s, h in [(s, h) for s in reversed(range(per)) for h in range(GLA_HEADS)]:
            rows = slice(s * GLA_CHUNK, (s + 1) * GLA_CHUNK)
            b, bl, q_in, k_in, k_st = _gla_chunk_terms(q_ref, k_ref, la_ref, h, rows)
            ks_ = slice(h * GLA_HK, (h + 1) * GLA_HK)
            vs = slice(h * GLA_HV, (h + 1) * GLA_HV)
            vv = v_ref[rows, vs].astype(BF16)
            do_h = do_ref[rows, vs]
            qb, kb, ksb = q_in.astype(BF16), k_in.astype(BF16), k_st.astype(BF16)
            attn = (_dot_nt(qb, kb) * tri).astype(BF16)
            st = sp_ref[s, h]
            dst = ds_ref[h]
            dstb = dst.astype(BF16)
            dattn = (_dot_nt(do_h, vv) * tri).astype(BF16)
            dg_ref[rows, P_VG + h * GLA_HV:P_VG + (h + 1) * GLA_HV] = (
                _dot_tn(attn, do_h) + _dot_nt(ksb, dstb)).astype(BF16)
            dq_in = _dot(dattn, kb) + _dot(do_h, st.astype(BF16))
            dk_in = _dot_tn(dattn, qb)
            dk_st = _dot(vv, dstb)
            ebl = jnp.exp(bl)
            d_ebl = jnp.sum(st * dst, axis=0, keepdims=True)
            ds_ref[h] = _dot_tn(do_h, qb) + dst * ebl
            dg_ref[rows, P_QG + h * GLA_HK:P_QG + (h + 1) * GLA_HK] = (
                dq_in * (GLA_HK ** -0.5) * jnp.exp(b)).astype(BF16)
            dg_ref[rows, P_KG + h * GLA_HK:P_KG + (h + 1) * GLA_HK] = (
                dk_in * jnp.exp(-b) + dk_st * jnp.exp(bl - b)).astype(BF16)
            db = dq_in * q_in - dk_in * k_in - dk_st * k_st
            dbl = jnp.sum(dk_st * k_st, axis=0, keepdims=True) + d_ebl * ebl
            dla_ref[rows, ks_] = db + jnp.where(last, dbl, 0.0)

    rev = lambda i: n - 1 - i
    gw = P_GROUPS[0][1]
    return pl.pallas_call(
        body, name="gla_bwd",
        grid=(n,),
        in_specs=[pl.BlockSpec((c, GLA_DK), lambda i: (rev(i), P_QG // GLA_DK)),
                  pl.BlockSpec((c, GLA_DK), lambda i: (rev(i), P_KG // GLA_DK)),
                  pl.BlockSpec((c, GLA_DV), lambda i: (rev(i), P_VG // GLA_DV)),
                  pl.BlockSpec((c, GLA_DK), lambda i: (rev(i), 0)),
                  pl.BlockSpec((c, GLA_DV), lambda i: (rev(i), 0)),
                  pl.BlockSpec((per, GLA_HEADS, GLA_HV, GLA_HK), lambda i: (rev(i), 0, 0, 0)),
                  pl.BlockSpec(memory_space=pl.ANY)],
        out_specs=(pl.BlockSpec((c, gw), lambda i: (rev(i), 0)),
                   pl.BlockSpec((c, GLA_DK), lambda i: (rev(i), 0))),
        out_shape=(jax.ShapeDtypeStruct((t, gw), BF16), jax.ShapeDtypeStruct((t, GLA_DK), F32)),
        scratch_shapes=[pltpu.VMEM((GLA_HEADS, GLA_HV, GLA_HK), F32)],
        compiler_params=_cparams(("arbitrary",)),
    )(proj, proj, proj, log_a, do, states, after)


def _post(o_mla, proj, o_gla, x, target, g_gla, g_final, w_pm, w_pg, w_o):
    t = x.shape[0]
    tm = min(128, t)
    g0, gw = P_GROUPS[1]

    def body(om_ref, zg_ref, gm_ref, gg_ref, zm_ref, og_ref, x_ref, tg_ref, ggla_ref, gf_ref,
             wpm_ref, wpg_ref, wo_ref,
             dx2_ref, dom_ref, dog_ref, dg_ref,
             mg_ref, um_ref, ug_ref, dym_ref, dyg_ref, loss_ref, dgf_ref, dggla_ref):
        @pl.when(pl.program_id(0) == 0)
        def _():
            loss_ref[...] = jnp.zeros_like(loss_ref)
            dgf_ref[...] = jnp.zeros_like(dgf_ref)
            dggla_ref[...] = jnp.zeros_like(dggla_ref)

        om = om_ref[...]
        zm = zm_ref[...]
        sm = _sigmoid(zm)
        silu_m = zm * sm
        um = (om * silu_m).astype(BF16)
        um_ref[...] = um
        ym = _dot(um, wpm_ref[...])

        ggla = ggla_ref[...]
        zg = zg_ref[...]
        sg = _sigmoid(zg)
        silu_g = zg * sg
        xhat, rstd, on = [], [], []
        for h in range(GLA_HEADS):
            blk = og_ref[:, h * GLA_HV:(h + 1) * GLA_HV]
            r = lax.rsqrt(jnp.mean(blk * blk, axis=-1, keepdims=True) + EPS)
            xhat.append(blk * r)
            rstd.append(r)
            on.append(xhat[h] * ggla)
        on = jnp.concatenate(on, axis=-1)
        ug = (on * silu_g).astype(BF16)
        ug_ref[...] = ug
        yg = _dot(ug, wpg_ref[...])

        sgm = _sigmoid(gm_ref[...])
        sgg = _sigmoid(gg_ref[...])
        merged = (sgm * ym + sgg * yg).astype(BF16)
        mg_ref[...] = merged
        x2 = x_ref[...] + _dot(merged, wo_ref[...])
        gf = gf_ref[...]
        rf = lax.rsqrt(jnp.mean(x2 * x2, axis=-1, keepdims=True) + EPS)
        xh = x2 * rf
        err = xh * gf - tg_ref[...]
        loss_ref[...] += 0.5 * jnp.sum(jnp.mean(err * err, axis=-1, keepdims=True))

        dy = err * (1.0 / D_MODEL)
        dgf_ref[...] += jnp.sum(dy * xh, axis=0, keepdims=True)
        dxh = dy * gf
        dx2 = rf * (dxh - xh * jnp.mean(dxh * xh, axis=-1, keepdims=True))
        dx2_ref[...] = dx2
        dmerged = _dot_nt(dx2.astype(BF16), wo_ref[...])
        dym = (dmerged * sgm).astype(BF16)
        dyg = (dmerged * sgg).astype(BF16)
        dym_ref[...] = dym
        dyg_ref[...] = dyg
        dg_ref[:, P_GMLA - g0:P_GMLA - g0 + D_MODEL] = (dmerged * ym * sgm * (1.0 - sgm)).astype(BF16)
        dg_ref[:, P_GGLA - g0:P_GGLA - g0 + D_MODEL] = (dmerged * yg * sgg * (1.0 - sgg)).astype(BF16)
        dum = _dot_nt(dym, wpm_ref[...])
        dom_ref[...] = dum * silu_m
        dg_ref[:, P_ZMLA - g0:P_ZMLA - g0 + MLA_WIDTH] = (
            dum * om * (sm * (1.0 + zm * (1.0 - sm)))).astype(BF16)
        dug = _dot_nt(dyg, wpg_ref[...])
        dg_ref[:, P_ZGLA - g0:P_ZGLA - g0 + GLA_DV] = (
            dug * on * (sg * (1.0 + zg * (1.0 - sg)))).astype(BF16)
        don = dug * silu_g
        dggla = jnp.zeros((1, GLA_HV), F32)
        for h in range(GLA_HEADS):
            hs = slice(h * GLA_HV, (h + 1) * GLA_HV)
            don_h = don[:, hs]
            dggla = dggla + jnp.sum(don_h * xhat[h], axis=0, keepdims=True)
            dxh_h = don_h * ggla
            dog_ref[:, hs] = (rstd[h] * (dxh_h - xhat[h] * jnp.mean(dxh_h * xhat[h], axis=-1,
                                                                     keepdims=True))).astype(BF16)
        dggla_ref[...] += dggla

    row = lambda w: pl.BlockSpec((tm, w), lambda i: (i, 0))
    pcol = lambda w, off: pl.BlockSpec((tm, w), lambda i: (i, _rel(off) // w))
    full = lambda a: pl.BlockSpec(a.shape, lambda i: (0, 0))
    sds = jax.ShapeDtypeStruct
    return pl.pallas_call(
        body, name="post_fwd_bwd",
        grid=(t // tm,),
        in_specs=[row(MLA_WIDTH), pcol(GLA_DV, P_ZGLA), pcol(D_MODEL, P_GMLA), pcol(D_MODEL, P_GGLA),
                  pcol(MLA_WIDTH, P_ZMLA), row(GLA_DV), row(D_MODEL), row(D_MODEL),
                  full(g_gla), full(g_final), full(w_pm), full(w_pg), full(w_o)],
        out_specs=(row(D_MODEL), row(MLA_WIDTH), row(GLA_DV), row(gw),
                   row(D_MODEL), row(MLA_WIDTH), row(GLA_DV), row(D_MODEL), row(D_MODEL),
                   pl.BlockSpec((1, LANE), lambda i: (0, 0)),
                   pl.BlockSpec((1, D_MODEL), lambda i: (0, 0)),
                   pl.BlockSpec((1, GLA_HV), lambda i: (0, 0))),
        out_shape=(sds((t, D_MODEL), F32), sds((t, MLA_WIDTH), F32), sds((t, GLA_DV), BF16),
                   sds((t, gw), BF16),
                   sds((t, D_MODEL), BF16), sds((t, MLA_WIDTH), BF16), sds((t, GLA_DV), BF16),
                   sds((t, D_MODEL), BF16), sds((t, D_MODEL), BF16),
                   sds((1, LANE), F32), sds((1, D_MODEL), F32), sds((1, GLA_HV), F32)),
        compiler_params=_cparams(("arbitrary",)),
    )(o_mla, proj, proj, proj, proj, o_gla, x, target, g_gla, g_final, w_pm, w_pg, w_o)


def _mla_prep_bwd(dq, dk, dv, dla, pre, proj, rq, rkv, g_q, g_kv, w_uq_p, w_k_p, w_v, w_gate_p,
                  rc, rsn, rsp):
    t = proj.shape[0]
    tm = min(256, t)
    gw = P_GROUPS[2][1]

    def body(dq_ref, dk_ref, dv_ref, dla_ref, pre_ref, cq_ref, ckv_ref, rq_ref, rkv_ref,
             gq_ref, gkv_ref, wuq_ref, wk_ref, wv_ref, wg_ref, c_ref, sn_ref, sp_ref,
             dg_ref, dqpre_ref, dpre_ref, dgq_ref, dgkv_ref, dbg_ref):
        @pl.when(pl.program_id(0) == 0)
        def _():
            dgq_ref[...] = jnp.zeros_like(dgq_ref)
            dgkv_ref[...] = jnp.zeros_like(dgkv_ref)
            dbg_ref[...] = jnp.zeros_like(dbg_ref)

        c, sn, sp = c_ref[...], sn_ref[...], sp_ref[...]
        dkr = jnp.zeros((tm, LANE), F32)
        for h in range(MLA_HEADS):
            sl = slice(h * HEAD_PAD, (h + 1) * HEAD_PAD)
            dqpre_ref[:, sl] = _rope_bwd(dq_ref[:, sl].astype(F32), c, sn, sp).astype(BF16)
            dkr = dkr + dk_ref[:, sl]
        dcqn = _dot_nt(dqpre_ref[...], wuq_ref[...])
        rq = rq_ref[...]
        xh = cq_ref[:, :MLA_Q_RANK] * rq
        dgq_ref[...] += jnp.sum(dcqn * xh, axis=0, keepdims=True)
        dxh = dcqn * gq_ref[...]
        dcq = rq * (dxh - xh * jnp.mean(dxh * xh, axis=-1, keepdims=True))
        dg_ref[:, :MLA_Q_RANK] = dcq.astype(BF16)
        dg_ref[:, MLA_Q_RANK:512] = jnp.zeros((tm, 512 - MLA_Q_RANK), BF16)

        dckvn = _dot_nt(dk_ref[...].astype(BF16), wk_ref[...]) + \
            _dot_nt(dv_ref[...].astype(BF16), wv_ref[...])
        rkv = rkv_ref[...]
        xh = ckv_ref[...] * rkv
        dgkv_ref[...] += jnp.sum(dckvn * xh, axis=0, keepdims=True)
        dxh = dckvn * gkv_ref[...]
        dg_ref[:, P_CKV - P_CQ:P_CKV - P_CQ + MLA_KV_RANK] = (
            rkv * (dxh - xh * jnp.mean(dxh * xh, axis=-1, keepdims=True))).astype(BF16)

        dlog_a = _dot_exact(_chunk_tri(tm, False), dla_ref[...])
        dpre = dlog_a * (1.0 / GLA_GATE_NORM) * (1.0 - _sigmoid(pre_ref[...]))
        dbg_ref[...] += jnp.sum(dpre, axis=0, keepdims=True)
        dpre = dpre.astype(BF16)
        dpre_ref[...] = dpre
        lane = lax.broadcasted_iota(jnp.int32, (tm, LANE), 1)
        in_kr = jnp.logical_and(lane >= MISC_KR, lane < MISC_KR + MLA_ROPE)
        dmisc = jnp.where(in_kr, _rope_bwd(dkr, c, sn, sp), 0.0) + _dot_nt(dpre, wg_ref[...])
        dg_ref[:, P_MISC - P_CQ:P_MISC - P_CQ + LANE] = dmisc.astype(BF16)

    hq = MLA_HEADS * HEAD_PAD
    row = lambda w: pl.BlockSpec((tm, w), lambda i: (i, 0))
    full = lambda a: pl.BlockSpec(a.shape, lambda i: (0, 0))
    acc = lambda w: pl.BlockSpec((1, w), lambda i: (0, 0))
    sds = jax.ShapeDtypeStruct
    return pl.pallas_call(
        body, name="mla_prep_bwd",
        grid=(t // tm,),
        in_specs=[row(hq), row(hq), row(MLA_WIDTH), row(GLA_DK), row(GLA_DK),
                  pl.BlockSpec((tm, 512), lambda i: (i, _rel(P_CQ) // 512)),
                  pl.BlockSpec((tm, MLA_KV_RANK), lambda i: (i, _rel(P_CKV) // MLA_KV_RANK)),
                  row(1), row(1), full(g_q), full(g_kv), full(w_uq_p), full(w_k_p), full(w_v),
                  full(w_gate_p), row(LANE), row(LANE), row(LANE)],
        out_specs=(row(gw), row(hq), row(GLA_DK),
                   acc(MLA_Q_RANK), acc(MLA_KV_RANK), acc(GLA_DK)),
        out_shape=(sds((t, gw), BF16), sds((t, hq), BF16), sds((t, GLA_DK), BF16),
                   sds((1, MLA_Q_RANK), F32), sds((1, MLA_KV_RANK), F32), sds((1, GLA_DK), F32)),
        compiler_params=_cparams(("arbitrary",)),
    )(dq, dk, dv, dla, pre, proj, proj, rq, rkv, g_q, g_kv, w_uq_p, w_k_p, w_v, w_gate_p,
      rc, rsn, rsp)


def _inproj_bwd(dgroups, w_pts, x, rstd, g_in, dx2, after):
    t = x.shape[0]
    tm = min(256, t)

    def body(d0_ref, d1_ref, d2_ref, w0_ref, w1_ref, w2_ref, x_ref, r_ref, g_ref, dx2_ref, after_ref,
             dx_ref, dg_ref):
        del after_ref

        @pl.when(pl.program_id(0) == 0)
        def _():
            dg_ref[...] = jnp.zeros_like(dg_ref)

        dh = jnp.zeros((tm, D_MODEL), F32)
        for d_ref, w_ref in zip((d0_ref, d1_ref, d2_ref), (w0_ref, w1_ref, w2_ref)):
            dh = dh + _dot(d_ref[...], w_ref[...])
        r = r_ref[...]
        xh = x_ref[...] * r
        dg_ref[...] += jnp.sum(dh * xh, axis=0, keepdims=True)
        dxh = dh * g_ref[...]
        dx_ref[...] = dx2_ref[...] + r * (dxh - xh * jnp.mean(dxh * xh, axis=-1, keepdims=True))

    row = lambda w: pl.BlockSpec((tm, w), lambda i: (i, 0))
    return pl.pallas_call(
        body, name="inproj_bwd",
        grid=(t // tm,),
        in_specs=[row(w) for _, w in P_GROUPS]
        + [pl.BlockSpec((w, D_MODEL), lambda i: (0, 0)) for _, w in P_GROUPS]
        + [row(D_MODEL), row(1), pl.BlockSpec((1, D_MODEL), lambda i: (0, 0)), row(D_MODEL),
           pl.BlockSpec(memory_space=pl.ANY)],
        out_specs=(row(D_MODEL), pl.BlockSpec((1, D_MODEL), lambda i: (0, 0))),
        out_shape=(jax.ShapeDtypeStruct((t, D_MODEL), F32),
                   jax.ShapeDtypeStruct((1, D_MODEL), F32)),
        compiler_params=_cparams(("arbitrary",)),
    )(*dgroups, *w_pts, x, rstd, g_in, dx2, after)


def _matmul(name, a, b, tm, tn, dtype=F32, after=None):
    kk, m = a.shape
    n = b.shape[1]
    extra = [] if after is None else [after]

    def body(a_ref, b_ref, *rest):
        rest[-1][...] = _dot_tn(a_ref[...].astype(BF16), b_ref[...].astype(BF16)).astype(dtype)

    return pl.pallas_call(
        body, name=name,
        grid=(n // tn, m // tm),
        in_specs=[pl.BlockSpec((kk, tm), lambda j, i: (0, i)),
                  pl.BlockSpec((kk, tn), lambda j, i: (0, j))]
        + [pl.BlockSpec(memory_space=pl.ANY) for _ in extra],
        out_specs=pl.BlockSpec((tm, tn), lambda j, i: (i, j)),
        out_shape=jax.ShapeDtypeStruct((m, n), dtype),
        compiler_params=_cparams(("arbitrary", "arbitrary")),
    )(a, b, *extra)


def _adamw_update(part_refs, w_ref, m_ref, v_ref, g_ref, d_ref, nm_ref, nv_ref):
    g = part_refs[0][...].astype(F32)
    for p_ref in part_refs[1:]:
        g = g + p_ref[...].astype(F32)
    m_new = ADAM_B1 * m_ref[...] + (1.0 - ADAM_B1) * g
    v_new = ADAM_B2 * v_ref[...] + (1.0 - ADAM_B2) * (g * g)
    m_hat = m_new / (1.0 - ADAM_B1 ** ADAM_STEP)
    v_hat = v_new / (1.0 - ADAM_B2 ** ADAM_STEP)
    g_ref[...] = g
    nm_ref[...] = m_new
    nv_ref[...] = v_new
    d_ref[...] = -ADAM_LR * (m_hat / (jnp.sqrt(v_hat) + ADAM_EPS) + ADAM_WD * w_ref[...])


def _adamw_rows(name, parts, w, m, v, tr, first=None):
    _, rows, cols = w.shape
    slots = parts.shape[0]

    def body(*refs):
        lead_refs, p_ref = ([], refs[0]) if first is None else ([refs[0]], refs[1])
        _adamw_update(lead_refs + [p_ref.at[q] for q in range(slots)], *refs[len(lead_refs) + 1:])

    blk = pl.BlockSpec((None, tr, cols), lambda i: (0, i, 0))
    out = jax.ShapeDtypeStruct((1, rows, cols), F32)
    lead = [] if first is None else [pl.BlockSpec((tr, cols), lambda i: (i, 0))]
    return pl.pallas_call(
        body, name=name,
        grid=(rows // tr,),
        in_specs=lead + [pl.BlockSpec((slots, tr, cols), lambda i: (0, i, 0)), blk, blk, blk],
        out_specs=(blk, blk, blk, blk),
        out_shape=(out, out, out, out),
        compiler_params=_cparams(("arbitrary",)),
    )(*([] if first is None else [first]), parts, w, m, v)


def _adamw_transposed(name, first, parts, w, m, v, tl):
    _, rows, cols = w.shape
    slots, padded = parts.shape[:2]

    def body(f_ref, p_ref, *refs):
        _adamw_update([f_ref.at[pl.ds(0, cols)]]
                      + [p_ref.at[q, pl.ds(0, cols)] for q in range(slots)], *refs)

    blk = pl.BlockSpec((cols, None, tl), lambda i: (0, 0, i))
    out = jax.ShapeDtypeStruct((cols, 1, rows), F32)
    res = pl.pallas_call(
        body, name=name,
        grid=(rows // tl,),
        in_specs=[pl.BlockSpec((padded, tl), lambda i: (0, i)),
                  pl.BlockSpec((slots, padded, tl), lambda i: (0, 0, i)), blk, blk, blk],
        out_specs=(blk, blk, blk, blk),
        out_shape=(out, out, out, out),
        compiler_params=_cparams(("arbitrary",)),
    )(first, parts, *[a.transpose(2, 0, 1) for a in (w, m, v)])
    return [r.transpose(1, 2, 0) for r in res]


def _adamw_group(firsts, parts, ws, ms, vs):
    n = len(ws)

    def body(*refs):
        ins, outs = refs[:5 * n], refs[5 * n:]
        x, y, c = _mesh_pos()
        for a in range(n):
            _adamw_update([ins[a].at[4 * x + 2 * y + c]]
                          + [ins[n + a].at[q] for q in range(ins[n + a].shape[0])],
                          *[r.at[0] for r in (ins[2 * n + a], ins[3 * n + a], ins[4 * n + a])],
                          *[r.at[0] for r in outs[4 * a:4 * a + 4]])

    vmem = lambda k: [pl.BlockSpec(memory_space=pltpu.VMEM) for _ in range(k)]
    out_shape = []
    for w in ws:
        out_shape += [jax.ShapeDtypeStruct(w.shape, F32)] * 4
    res = pl.pallas_call(
        body, name="adamw_small_weights",
        in_specs=vmem(5 * n), out_specs=tuple(vmem(4 * n)), out_shape=tuple(out_shape),
        compiler_params=_cparams(),
    )(*firsts, *parts, *ws, *ms, *vs)
    return [res[4 * a:4 * a + 4] for a in range(n)]


def _rope_tables(positions):
    half = MLA_ROPE // 2
    freqs = ROPE_THETA ** (-jnp.arange(half, dtype=F32) / half)
    ang = positions.astype(F32).reshape(-1, 1) * freqs
    cos, sin = jnp.cos(ang), jnp.sin(ang)
    t = ang.shape[0]
    one, zero = jnp.ones((t, MLA_NOPE), F32), jnp.zeros((t, half), F32)
    tail = jnp.zeros((t, LANE - MLA_QK), F32)
    rc = jnp.concatenate([one, cos, cos, tail], axis=1)
    rsn = jnp.concatenate([0.0 * one, -sin, zero, tail], axis=1)
    rsp = jnp.concatenate([0.0 * one, zero, sin, tail], axis=1)
    return rc, rsn, rsp


def _cols_full(g):
    return g.transpose(1, 0, 2)


def kernel(x, positions, g_in, w_in, g_q, w_uq, g_kv, w_ukv, w_gla_gate, b_gla_gate, g_gla, w_proj_mla, w_proj_gla, w_out, g_final, loss_target, m_g_in, m_w_in, m_g_q, m_w_uq, m_g_kv, m_w_ukv, m_w_gla_gate, m_b_gla_gate, m_g_gla, m_w_proj_mla, m_w_proj_gla, m_w_out, m_g_final, v_g_in, v_w_in, v_g_q, v_w_uq, v_g_kv, v_w_ukv, v_w_gla_gate, v_b_gla_gate, v_g_gla, v_w_proj_mla, v_w_proj_gla, v_w_out, v_g_final):
    t = x.shape[1]
    x2d = x.reshape(t, D_MODEL)
    tgt = loss_target.reshape(t, D_MODEL)
    g_final2 = g_final.reshape(1, D_MODEL)
    sharded = [(w_in, m_w_in, v_w_in), (w_uq, m_w_uq, v_w_uq), (w_ukv, m_w_ukv, v_w_ukv),
               (w_gla_gate, m_w_gla_gate, v_w_gla_gate), (w_proj_mla, m_w_proj_mla, v_w_proj_mla),
               (w_proj_gla, m_w_proj_gla, v_w_proj_gla), (w_out, m_w_out, v_w_out)]

    w_in_t = w_in.transpose(2, 0, 1).reshape(SHARD_COLS, D_MODEL)
    everyone = tuple(range(N_DEV))
    w_in_b = w_in_t.astype(BF16)
    stages = ((0, 2, 4, 6), (1, 3, 5, 7))
    where = {d: (k, i) for k, srcs in enumerate(stages) for i, d in enumerate(srcs)}
    me = 4 * lax.axis_index("x") + 2 * lax.axis_index("y") + lax.axis_index("c")
    own_slot = lambda land, shard: lax.dynamic_update_index_in_dim(land, shard, me, 0)
    small_sems, b_small, spread, token = _spread_start(
        "spread_start", [s[0][0].astype(BF16) for s in sharded[1:]])
    (g_in_1,) = _all_gather("all_gather_first", [w_in_b], [stages[0]], after=token)
    g_uq, g_ukv, g_gate = map(own_slot, _spread_wait(
        "spread_wait_1", small_sems[:3], b_small[:3], spread[:3], g_in_1), b_small[:3])
    w_uq_p = jnp.pad(_cols_full(g_uq), ((0, 0), (0, 0), (0, HEAD_PAD - MLA_QK))).reshape(
        MLA_Q_RANK, MLA_HEADS * HEAD_PAD)
    ukv = _cols_full(g_ukv)
    w_k_p = jnp.pad(ukv[:, :, :MLA_NOPE], ((0, 0), (0, 0), (0, HEAD_PAD - MLA_NOPE))).reshape(
        MLA_KV_RANK, MLA_HEADS * HEAD_PAD)
    w_v = ukv[:, :, MLA_NOPE:].reshape(MLA_KV_RANK, MLA_WIDTH)
    w_gate_p = jnp.pad(_cols_full(g_gate).reshape(GLA_GATE_RANK, GLA_DK),
                       ((MISC_ALR, LANE - MISC_ALR - GLA_GATE_RANK), (0, 0)))
    rc, rsn, rsp = _rope_tables(positions)

    w_lat = _weights_to_p("weights_latents", [g_in_1], where, 2)
    proj_lat, h, rstd = _inproj(x2d, g_in, w_lat)
    q, k, v, log_a, pre, cqn, ckvn, rq, rkv, misc = _mla_prep(
        proj_lat, g_q, g_kv, w_uq_p, w_k_p, w_v, w_gate_p, b_gla_gate, rc, rsn, rsp)
    o_mla, lse, (g_in_2,) = _mla_attn_fwd(q, k, v, [w_in_b], [stages[1]])
    w_gla = _weights_to_p("weights_gla", [g_in_1, g_in_2], where, 0)
    proj_gla = _proj("inproj_gla", h, w_gla)
    o_gla, states = _gla_fwd(proj_gla, log_a)
    g_pm, g_pg, g_o = map(own_slot, _spread_wait(
        "spread_wait_2", small_sems[3:], b_small[3:], spread[3:], o_gla), b_small[3:])
    w_out_path = _weights_to_p("weights_out_path", [g_in_1, g_in_2], where, 1)
    proj_out = _proj("inproj_out_path", h, w_out_path)
    w_in_p = (w_gla, w_out_path, w_lat)
    w_pm = _cols_full(g_pm).reshape(MLA_WIDTH, D_MODEL)
    w_pg = g_pg.reshape(GLA_DV, D_MODEL)
    w_o = g_o.reshape(D_MODEL, D_MODEL)

    (dx2, do_mla, do_gla, d_out, merged, um, ug, dym, dyg, loss_p, dg_final,
     dg_gla) = _post(o_mla, proj_out, o_gla, x2d, tgt, g_gla, g_final2, w_pm, w_pg, w_o)

    p_pm = _matmul("dw_proj_mla", um, dym, 512, 512, BF16).reshape(
        MLA_WIDTH, N_DEV, D_MODEL // N_DEV).transpose(1, 0, 2)
    p_pg = _matmul("dw_proj_gla", ug, dyg, 512, 512, BF16).reshape(N_DEV, -1, D_MODEL)
    p_o = _matmul("dw_out", merged, dx2, 512, 512, BF16).reshape(N_DEV, -1, D_MODEL)
    own_in = jnp.zeros((SHARD_PAD, D_MODEL), BF16)
    land_in = lax.empty((PEERS, SHARD_PAD, D_MODEL), BF16)
    dw_groups, started, lands = {}, [], [land_in]

    def reduce_scatter_stage(s, dests, own_in, extra=()):
        parts_in, own_in = _grads_to_shards("grads_to_shards_%d" % s, dw_groups, dests, own_in)
        first = len(lands)
        lands.extend(lax.empty((PEERS,) + p.shape[1:], BF16) for p in extra)
        idx = [0] + list(range(first, len(lands)))
        all_dests = [[(i, d, r0, r1) for i, (d, ranges) in enumerate(dests) for r0, r1 in ranges]]
        all_dests += [_whole(everyone, p.shape[1]) for p in extra]
        sems, parts, new_lands, token = _ici_start(
            "ici_start_%d" % s, [parts_in] + list(extra), [lands[i] for i in idx], all_dests)
        for a, i in enumerate(idx):
            lands[i] = new_lands[a]
            started.append((sems[a][0], sems[a][1], parts[a], i, all_dests[a]))
        return own_in, token

    def late_small_stage(arrays):
        idx = list(range(len(lands), len(lands) + len(arrays)))
        lands.extend(lax.empty((PEERS,) + p.shape[1:], BF16) for p in arrays)
        all_dests = [_whole(everyone, p.shape[1]) for p in arrays]
        sems, parts, new_lands, token = _ici_start(
            "ici_start_4", list(arrays), [lands[i] for i in idx], all_dests)
        for a, i in enumerate(idx):
            lands[i] = new_lands[a]
            started.append((sems[a][0], sems[a][1], parts[a], i, all_dests[a]))
        return token

    dw_groups[1] = _matmul("dw_in_1", d_out, h, 512, 512, BF16)
    full = [(0, SHARD_PAD)]
    own_in, token = reduce_scatter_stage(
        1, [(5, full), (6, full), (7, full), (0, [(672, SHARD_PAD)]), (1, [(0, 384)]),
            (4, [(96, SHARD_PAD)])], own_in, (p_pm, p_pg, p_o))
    d_gla, dla = _gla_bwd(proj_gla, log_a, do_gla, states, token)
    dw_groups[0] = _matmul("dw_in_0", d_gla, h, 512, 512, BF16)
    own_in, token = reduce_scatter_stage(
        2, [(1, [(384, SHARD_PAD)]), (2, full), (3, full), (4, [(0, 64)])], own_in)
    dq, dk, dv = _mla_attn_bwd(q, k, v, o_mla, do_mla, lse, token)
    d_lat, dqpre, dpre, dg_q, dg_kv, db_gate = _mla_prep_bwd(
        dq, dk, dv, dla, pre, proj_lat, rq, rkv, g_q, g_kv, w_uq_p, w_k_p, w_v, w_gate_p, rc, rsn, rsp)
    dw_groups[2] = _matmul("dw_in_2", d_lat, h, 896, 512, BF16)
    own_in, token = reduce_scatter_stage(3, [(0, [(0, 672)]), (4, [(64, 96)])], own_in)
    dw_uq = _matmul("dw_uq", cqn, dqpre, MLA_Q_RANK, 512, BF16, after=token)
    p_uq = dw_uq.reshape(MLA_Q_RANK, MLA_HEADS, HEAD_PAD)[:, :, :MLA_QK].transpose(1, 0, 2)
    dw_k = _matmul("dw_uk", ckvn, dk, MLA_KV_RANK, 512, BF16)
    dw_v = _matmul("dw_uv", ckvn, dv, MLA_KV_RANK, 512, BF16)
    p_ukv = jnp.concatenate(
        [dw_k.reshape(MLA_KV_RANK, MLA_HEADS, HEAD_PAD)[:, :, :MLA_NOPE],
         dw_v.reshape(MLA_KV_RANK, MLA_HEADS, MLA_VDIM)], axis=2).transpose(1, 0, 2)
    dw_gate = _matmul("dw_gate", misc, dpre, LANE, 512, BF16)
    p_gate = dw_gate[MISC_ALR:MISC_ALR + GLA_GATE_RANK].reshape(
        GLA_GATE_RANK, N_DEV, GLA_DK // N_DEV).transpose(1, 0, 2)
    token = late_small_stage((p_uq, p_ukv, p_gate))
    grad_x, dg_in = _inproj_bwd((d_gla, d_out, d_lat), w_in_p, x2d, rstd, g_in, dx2, token)
    small = jnp.concatenate([dg_in.reshape(-1), dg_q.reshape(-1), dg_kv.reshape(-1),
                             db_gate.reshape(-1), dg_gla.reshape(-1), dg_final.reshape(-1),
                             loss_p[0, :1]])
    small = jnp.pad(small, (0, SMALL_ROWS * LANE - small.shape[0])).reshape(SMALL_ROWS, LANE)

    (small_all,) = _all_gather("all_gather_small", [small], [everyone])
    lands = _ici_wait("ici_wait", started, lands, small_all)
    big = [_adamw_transposed("adamw_w_in", own_in, lands[0], *sharded[0], 256)]
    big += _adamw_group([p_uq, p_ukv, p_gate, p_pm, p_pg, p_o], list(lands[4:7]) + list(lands[1:4]),
                        *[[s[j] for s in sharded[1:]] for j in range(3)])
    replicated = [(g_in, m_g_in, v_g_in), (g_q, m_g_q, v_g_q), (g_kv, m_g_kv, v_g_kv),
                  (b_gla_gate, m_b_gla_gate, v_b_gla_gate), (g_gla, m_g_gla, v_g_gla),
                  (g_final, m_g_final, v_g_final)]
    spacks = [jnp.pad(jnp.concatenate([s[j].reshape(-1) for s in replicated]),
                      (0, SMALL_ROWS * LANE - sum(SMALL_SIZES))).reshape(1, SMALL_ROWS, LANE)
              for j in range(3)]
    tiny = _adamw_rows("adamw_gains", small_all, spacks[0], spacks[1], spacks[2], SMALL_ROWS)

    outs = {}
    names = ("w_in", "w_uq", "w_ukv", "w_gla_gate", "w_proj_mla", "w_proj_gla", "w_out")
    for j, kind in enumerate(("grad", "delta", "new_m", "new_v")):
        for name, res in zip(names, big):
            outs[kind, name] = res[j]
        flat = tiny[j].reshape(-1)
        off = 0
        for name, size in zip(("g_in", "g_q", "g_kv", "b_gla_gate", "g_gla", "g_final"), SMALL_SIZES):
            shape = (size,) if name == "g_final" else (1, size)
            outs[kind, name] = flat[off:off + size].reshape(shape)
            off += size
    loss = tiny[0].reshape(-1)[sum(SMALL_SIZES)]
    order = ("g_in", "w_in", "g_q", "w_uq", "g_kv", "w_ukv", "w_gla_gate", "b_gla_gate", "g_gla",
             "w_proj_mla", "w_proj_gla", "w_out", "g_final")
    result = [loss, grad_x.reshape(1, t, D_MODEL)]
    for kind in ("grad", "delta", "new_m", "new_v"):
        result += [outs[kind, name] for name in order]
    return tuple(result)
```

```python
import jax
import jax.numpy as jnp
from jax import lax
from jax.experimental import pallas as pl
from jax.experimental.pallas import tpu as pltpu

F32 = jnp.float32
BF16 = jnp.bfloat16
MESH = pl.DeviceIdType.MESH
N_DEV = 8

D_MODEL = 1024
EPS = 1e-6
MLA_HEADS = 8
MLA_NOPE = 64
MLA_ROPE = 32
MLA_VDIM = 64
MLA_Q_RANK = 384
MLA_KV_RANK = 256
MLA_QK = MLA_NOPE + MLA_ROPE
MLA_WIDTH = MLA_HEADS * MLA_VDIM
ROPE_THETA = 10000.0
GLA_HEADS = 4
GLA_DK = 512
GLA_DV = 1024
GLA_HK = 128
GLA_HV = 256
GLA_GATE_RANK = 16
GLA_GATE_NORM = 16.0
GLA_CHUNK = 64
GLA_CHUNKS_PER_STEP = 8
D_IN = 6320

ADAM_LR = 0.001
ADAM_B1 = 0.9
ADAM_B2 = 0.999
ADAM_EPS = 1e-08
ADAM_WD = 0.01
ADAM_STEP = 10

LANE = 128
HEAD_PAD = 128
VMEM_LIMIT = 48 * 1024 * 1024

P_VG, P_QG, P_KG = 0, 1024, 1536
P_ZGLA, P_GMLA, P_GGLA, P_ZMLA = 2048, 3072, 4096, 5120
P_CQ, P_CKV, P_MISC = 5632, 6144, 6400
P_TOTAL = 6528
P_GROUPS = ((0, 2048), (2048, 3584), (5632, 896))
MISC_KR = 64
MISC_ALR = 96
SHARD_COLS = D_IN // N_DEV
SHARD_PAD = 800
P_COMPONENTS = ((0, 384, P_CQ), (384, 256, P_CKV), (640, 32, P_MISC + MISC_KR), (672, 512, P_ZMLA),
                (1184, 512, P_QG), (1696, 512, P_KG), (2208, 1024, P_VG),
                (3232, 16, P_MISC + MISC_ALR), (3248, 1024, P_ZGLA), (4272, 1024, P_GMLA),
                (5296, 1024, P_GGLA))

SMALL_SIZES = (1024, 384, 256, 512, 256, 1024)
SMALL_ROWS = 32


def _segments():
    segs = []
    for g0, n, p0 in P_COMPONENTS:
        g = g0
        while g < g0 + n:
            d = g // SHARD_COLS
            end = min(g0 + n, (d + 1) * SHARD_COLS)
            segs.append((d, g - d * SHARD_COLS, end - g, p0 + g - g0))
            g = end
    return segs


def _group_of(p0):
    return max(i for i, (off, _) in enumerate(P_GROUPS) if off <= p0)


def _rel(p0):
    return p0 - P_GROUPS[_group_of(p0)][0]


def _cparams(sem=None):
    if sem is None:
        return pltpu.CompilerParams(vmem_limit_bytes=VMEM_LIMIT)
    return pltpu.CompilerParams(dimension_semantics=sem, vmem_limit_bytes=VMEM_LIMIT)


def _sigmoid(v):
    return 1.0 / (1.0 + jnp.exp(-v))


def _dot(a, b):
    return jnp.dot(a, b, preferred_element_type=F32)


def _dot_nt(a, b):
    return lax.dot_general(a, b, (((1,), (1,)), ((), ())), preferred_element_type=F32)


def _dot_tn(a, b):
    return lax.dot_general(a, b, (((0,), (0,)), ((), ())), preferred_element_type=F32)


def _dot_exact(a, b):
    return jnp.dot(a, b, preferred_element_type=F32, precision=lax.Precision.HIGHEST)


def _rope_fwd(blk, c, sn, sp):
    return blk * c + pltpu.roll(blk, LANE - 16, 1) * sn + pltpu.roll(blk, 16, 1) * sp


def _rope_bwd(blk, c, sn, sp):
    return blk * c + pltpu.roll(blk * sn, 16, 1) + pltpu.roll(blk * sp, LANE - 16, 1)


def _mesh_pos():
    return lax.axis_index("x"), lax.axis_index("y"), lax.axis_index("c")


def _hbm_specs(n):
    return [pl.BlockSpec(memory_space=pltpu.HBM) for _ in range(n)]


def _dev(d):
    return d >> 2, (d >> 1) & 1, d & 1


def _gather_plan(shards, sources):
    na, most = len(shards), max(len(s) for s in sources)
    out_shape = [jax.ShapeDtypeStruct((len(srcs),) + s.shape, s.dtype)
                 for s, srcs in zip(shards, sources)]
    sems = [pltpu.SemaphoreType.DMA((na, most)) for _ in range(3)]
    sems += [pltpu.SemaphoreType.DMA((na, most, 3))]
    sems += [pltpu.SemaphoreType.DMA((na, most)) for _ in range(3)]
    return out_shape, sems


def _gather_hooks(x_refs, out_refs, sems, sources):
    local_sems, d2d_send, d2d_recv, ici_send, ici_recv, fwd_send, fwd_recv = sems
    x, y, c = _mesh_pos()
    chips = [(1 - x, y), (x, 1 - y), (1 - x, 1 - y)]
    items = []
    for a, srcs in enumerate(sources):
        for i, d in enumerate(srcs):
            dx, dy, dc = _dev(d)
            near = jnp.logical_and(x == dx, y == dy)
            far = jnp.logical_not(near)
            slot = out_refs[a].at[i]

            def remote(src, to, send_sem, recv_sem, slot=slot):
                return pltpu.make_async_remote_copy(
                    src_ref=src, dst_ref=slot, send_sem=send_sem, recv_sem=recv_sem,
                    device_id=to, device_id_type=MESH)

            items.append(dict(
                me=jnp.logical_and(near, c == dc), sibling=jnp.logical_and(near, c != dc),
                relay=jnp.logical_and(far, c == dc), behind=jnp.logical_and(far, c != dc),
                local=pltpu.make_async_copy(x_refs[a], slot, local_sems.at[a, i]),
                to_sibling=remote(x_refs[a], (x, y, 1 - c), d2d_send.at[a, i], d2d_recv.at[a, i]),
                to_chips=[remote(x_refs[a], (*chip, c), ici_send.at[a, i, j], ici_recv.at[a, i])
                          for j, chip in enumerate(chips)],
                forward=remote(slot, (x, y, 1 - c), fwd_send.at[a, i], fwd_recv.at[a, i])))

    def start():
        for it in items:
            @pl.when(it["me"])
            def _(it=it):
                it["local"].start()
                it["to_sibling"].start()
                for cp in it["to_chips"]:
                    cp.start()

    def finish():
        for it in items:
            @pl.when(it["relay"])
            def _(it=it):
                it["to_chips"][0].wait_recv()
                it["forward"].start()
        for it in items:
            pl.when(it["sibling"])(it["to_sibling"].wait_recv)
            pl.when(it["behind"])(it["forward"].wait_recv)
            pl.when(it["relay"])(it["forward"].wait_send)

            @pl.when(it["me"])
            def _(it=it):
                it["local"].wait()
                it["to_sibling"].wait_send()
                for cp in it["to_chips"]:
                    cp.wait_send()

    return start, finish


def _all_gather(name, shards, sources):
    n = len(shards)
    out_shape, sems = _gather_plan(shards, sources)

    def body(*refs):
        start, finish = _gather_hooks(refs[:n], refs[n:2 * n], refs[2 * n:], sources)
        start()
        finish()

    return pl.pallas_call(
        body, name=name,
        out_shape=tuple(out_shape),
        in_specs=_hbm_specs(n), out_specs=tuple(_hbm_specs(n)),
        scratch_shapes=sems,
        compiler_params=_cparams(),
    )(*shards)


PEERS = N_DEV - 1


def _whole(dests, rows):
    return [(i, d, 0, rows) for i, d in enumerate(dests)]


def _ici_copies(p_ref, land_ref, send_sems, recv_sems, pieces):
    x, y, c = _mesh_pos()
    sends, arrivals = [], []

    def rows_of(ref, j, r0, r1):
        return ref.at[j] if (r0, r1) == (0, ref.shape[1]) else ref.at[j, pl.ds(r0, r1 - r0)]

    for p, (i, d, r0, r1) in enumerate(pieces):
        dx, dy, dc = _dev(d)
        k = (4 * (x != dx).astype(jnp.int32) + 2 * (y != dy).astype(jnp.int32)
             + (c != dc).astype(jnp.int32))
        slot = jnp.maximum(k - 1, 0)
        sends.append((k > 0, pltpu.make_async_remote_copy(
            src_ref=rows_of(p_ref, i, r0, r1), dst_ref=rows_of(land_ref, slot, r0, r1),
            send_sem=send_sems.at[p], recv_sem=recv_sems.at[p * PEERS + slot],
            device_id=(dx, dy, dc), device_id_type=MESH)))
        arrivals.append((k == 0, [pltpu.make_async_remote_copy(
            src_ref=rows_of(p_ref, i, r0, r1), dst_ref=rows_of(land_ref, r, r0, r1),
            send_sem=send_sems.at[p], recv_sem=recv_sems.at[p * PEERS + r],
            device_id=(dx, dy, dc), device_id_type=MESH) for r in range(PEERS)]))
    return sends, arrivals


def _ici_start(name, hs, lands, dests):
    na = len(hs)

    def body(*refs):
        h_refs, land_refs, sems = refs[:na], refs[na:2 * na], refs[2 * na:4 * na]
        token = refs[-1]
        for a in range(na):
            sends, _ = _ici_copies(h_refs[a], land_refs[a], sems[2 * a], sems[2 * a + 1], dests[a])
            for go, cp in sends:
                pl.when(go)(cp.start)
        token[...] = jnp.zeros_like(token)

    hbm, sem = pl.BlockSpec(memory_space=pltpu.HBM), pl.BlockSpec(memory_space=pltpu.SEMAPHORE)
    sem_shapes = []
    for a in range(na):
        sem_shapes += [pltpu.SemaphoreType.DMA((len(dests[a]),)),
                       pltpu.SemaphoreType.DMA((len(dests[a]) * PEERS,))]
    res = pl.pallas_call(
        body, name=name,
        out_shape=tuple(sem_shapes) + tuple(pltpu.HBM(v.shape, v.dtype) for v in list(hs) + list(lands))
        + (jax.ShapeDtypeStruct((8, LANE), F32),),
        in_specs=(hbm,) * (2 * na),
        out_specs=(sem,) * (2 * na) + (hbm,) * (2 * na) + (pl.BlockSpec(memory_space=pltpu.VMEM),),
        input_output_aliases={i: 2 * na + i for i in range(2 * na)},
        compiler_params=pltpu.CompilerParams(
            has_side_effects=pltpu.SideEffectType.DATAFLOW_SIDE_EFFECTING,
            vmem_limit_bytes=VMEM_LIMIT),
    )(*[pltpu.with_memory_space_constraint(v, pltpu.HBM) for v in list(hs) + list(lands)])
    sems = [(res[2 * a], res[2 * a + 1]) for a in range(na)]
    return sems, res[2 * na:3 * na], res[3 * na:4 * na], res[-1]


def _ici_wait(name, started, lands, after):
    k, nl = len(started), len(lands)

    def body(*refs):
        land_refs = refs[3 * k:3 * k + nl]
        for s in range(k):
            h_ref, send_sems, recv_sems = refs[3 * s:3 * s + 3]
            sends, arrivals = _ici_copies(h_ref, land_refs[started[s][3]], send_sems, recv_sems,
                                          started[s][4])
            for go, cp in sends:
                pl.when(go)(cp.wait_send)
            for here, cps in arrivals:
                for cp in cps:
                    pl.when(here)(cp.wait_recv)

    hbm, sem = pl.BlockSpec(memory_space=pltpu.HBM), pl.BlockSpec(memory_space=pltpu.SEMAPHORE)
    operands, specs = [], []
    for send_sems, recv_sems, h, _, _ in started:
        operands += [h, send_sems, recv_sems]
        specs += [hbm, sem, sem]
    return pl.pallas_call(
        body, name=name,
        out_shape=tuple(pltpu.HBM(v.shape, v.dtype) for v in lands),
        in_specs=tuple(specs) + (hbm,) * nl + (pl.BlockSpec(memory_space=pl.ANY),),
        out_specs=(hbm,) * nl,
        input_output_aliases={3 * k + i: i for i in range(nl)},
        compiler_params=pltpu.CompilerParams(
            has_side_effects=pltpu.SideEffectType.DATAFLOW_SIDE_EFFECTING,
            vmem_limit_bytes=VMEM_LIMIT),
    )(*operands, *lands, after)


def _weights_to_p(name, gathered, where, group):
    tl = 512
    off, width = P_GROUPS[group]
    segs = sorted([s for s in _segments() if _group_of(s[3]) == group], key=lambda s: s[3])
    used = sorted({where[s[0]][0] for s in segs})

    def body(*refs):
        g_refs, o_ref = dict(zip(used, refs[:-1])), refs[-1]
        pieces, pos = [], off
        for d, c0, n, p0 in segs:
            if p0 > pos:
                pieces.append(jnp.zeros((p0 - pos, tl), F32))
            k, slot = where[d]
            pieces.append(g_refs[k][slot, c0:c0 + n, :].astype(F32))
            pos = p0 + n
        if off + width > pos:
            pieces.append(jnp.zeros((off + width - pos, tl), F32))
        o_ref[...] = jnp.concatenate(pieces, axis=0).astype(BF16)

    return pl.pallas_call(
        body, name=name,
        grid=(D_MODEL // tl,),
        in_specs=[pl.BlockSpec((gathered[k].shape[0], SHARD_COLS, tl), lambda i: (0, 0, i))
                  for k in used],
        out_specs=pl.BlockSpec((width, tl), lambda i: (0, i)),
        out_shape=jax.ShapeDtypeStruct((width, D_MODEL), BF16),
        compiler_params=_cparams(("arbitrary",)),
    )(*[gathered[k] for k in used])


def _shard_groups(d):
    return sorted({_group_of(s[3]) for s in _segments() if s[0] == d})


def _grads_to_shards(name, groups, dests, own_prev):
    tl = 512
    segs = _segments()
    used = sorted(groups)

    def body(*refs):
        g_refs, prev_ref, o_ref, own_ref = dict(zip(used, refs[:-3])), refs[-3], refs[-2], refs[-1]
        x, y, c = _mesh_pos()
        own = prev_ref[...].astype(F32)
        row = lax.broadcasted_iota(jnp.int32, (SHARD_PAD, tl), 0)
        for i, (d, ranges) in enumerate(dests):
            pieces, pos, asked = [], 0, None
            for r0, r1 in sorted(ranges):
                if r0 > pos:
                    pieces.append(jnp.zeros((r0 - pos, tl), F32))
                for _, c0, n, p0 in sorted([s for s in segs if s[0] == d], key=lambda s: s[1]):
                    a, b = max(c0, r0), min(c0 + n, r1)
                    if a < b:
                        gi = _group_of(p0)
                        lo = p0 - P_GROUPS[gi][0] + a - c0
                        pieces.append(g_refs[gi][lo:lo + b - a, :].astype(F32))
                if r1 > SHARD_COLS:
                    pieces.append(jnp.zeros((r1 - max(r0, SHARD_COLS), tl), F32))
                pos = r1
                inside = jnp.logical_and(row >= r0, row < r1)
                asked = inside if asked is None else jnp.logical_or(asked, inside)
            if pos < SHARD_PAD:
                pieces.append(jnp.zeros((SHARD_PAD - pos, tl), F32))
            shard = jnp.concatenate(pieces, axis=0)
            o_ref[i] = shard.astype(BF16)
            own = jnp.where(jnp.logical_and(4 * x + 2 * y + c == d, asked), shard, own)
        own_ref[...] = own.astype(BF16)

    blk = pl.BlockSpec((SHARD_PAD, tl), lambda i: (0, i))
    return pl.pallas_call(
        body, name=name,
        grid=(D_MODEL // tl,),
        in_specs=[pl.BlockSpec((P_GROUPS[g][1], tl), lambda i: (0, i)) for g in used] + [blk],
        out_specs=(pl.BlockSpec((len(dests), SHARD_PAD, tl), lambda i: (0, 0, i)), blk),
        out_shape=(jax.ShapeDtypeStruct((len(dests), SHARD_PAD, D_MODEL), BF16),
                   jax.ShapeDtypeStruct((SHARD_PAD, D_MODEL), BF16)),
        input_output_aliases={len(used): 1},
        compiler_params=_cparams(("arbitrary",)),
    )(*[groups[g] for g in used], own_prev)


def _inproj(x, g_in, w_pt):
    t = x.shape[0]
    tm = min(256, t)
    width = w_pt.shape[0]

    def body(x_ref, g_ref, w_ref, proj_ref, h_ref, r_ref):
        xf = x_ref[...]
        r = lax.rsqrt(jnp.mean(xf * xf, axis=-1, keepdims=True) + EPS)
        h = ((xf * r) * g_ref[...]).astype(BF16)
        proj_ref[...] = _dot_nt(h, w_ref[...])
        h_ref[...] = h
        r_ref[...] = r

    row = lambda w: pl.BlockSpec((tm, w), lambda i: (i, 0))
    return pl.pallas_call(
        body, name="inproj_latents",
        grid=(t // tm,),
        in_specs=[row(D_MODEL), pl.BlockSpec((1, D_MODEL), lambda i: (0, 0)),
                  pl.BlockSpec((width, D_MODEL), lambda i: (0, 0))],
        out_specs=(row(width), row(D_MODEL), row(1)),
        out_shape=(jax.ShapeDtypeStruct((t, width), F32),
                   jax.ShapeDtypeStruct((t, D_MODEL), BF16),
                   jax.ShapeDtypeStruct((t, 1), F32)),
        compiler_params=_cparams(("arbitrary",)),
    )(x, g_in, w_pt)


def _proj(name, h, w_pt):
    t = h.shape[0]
    tm = min(256, t)
    width = w_pt.shape[0]

    def body(h_ref, w_ref, o_ref):
        o_ref[...] = _dot_nt(h_ref[...], w_ref[...])

    return pl.pallas_call(
        body, name=name,
        grid=(t // tm,),
        in_specs=[pl.BlockSpec((tm, D_MODEL), lambda i: (i, 0)),
                  pl.BlockSpec((width, D_MODEL), lambda i: (0, 0))],
        out_specs=pl.BlockSpec((tm, width), lambda i: (i, 0)),
        out_shape=jax.ShapeDtypeStruct((t, width), F32),
        compiler_params=_cparams(("arbitrary",)),
    )(h, w_pt)


def _mla_prep(proj, g_q, g_kv, w_uq_p, w_k_p, w_v, w_gate_p, b_gate, rc, rsn, rsp):
    t = proj.shape[0]
    tm = min(256, t)
    hq = MLA_HEADS * HEAD_PAD

    def body(cq_ref, ckv_ref, misc_ref, gq_ref, gkv_ref, wuq_ref, wk_ref, wv_ref, wg_ref, bg_ref,
             c_ref, sn_ref, sp_ref,
             q_ref, k_ref, v_ref, la_ref, pre_ref, cqn_ref, ckvn_ref, rq_ref, rkv_ref, mb_ref):
        c, sn, sp = c_ref[...], sn_ref[...], sp_ref[...]
        cq = cq_ref[:, :MLA_Q_RANK]
        rq = lax.rsqrt(jnp.mean(cq * cq, axis=-1, keepdims=True) + EPS)
        cqn = ((cq * rq) * gq_ref[...]).astype(BF16)
        cqn_ref[...] = cqn
        rq_ref[...] = rq
        qpre = _dot(cqn, wuq_ref[...])
        ckv = ckv_ref[...]
        rkv = lax.rsqrt(jnp.mean(ckv * ckv, axis=-1, keepdims=True) + EPS)
        ckvn = ((ckv * rkv) * gkv_ref[...]).astype(BF16)
        ckvn_ref[...] = ckvn
        rkv_ref[...] = rkv
        kn = _dot(ckvn, wk_ref[...])
        v_ref[...] = _dot(ckvn, wv_ref[...]).astype(BF16)
        misc = misc_ref[...]
        krope = _rope_fwd(misc, c, sn, sp)
        for h in range(MLA_HEADS):
            sl = slice(h * HEAD_PAD, (h + 1) * HEAD_PAD)
            q_ref[:, sl] = _rope_fwd(qpre[:, sl], c, sn, sp).astype(BF16)
            k_ref[:, sl] = (kn[:, sl] + krope).astype(BF16)
        mb_ref[...] = misc.astype(BF16)
        pre = _dot(mb_ref[...], wg_ref[...]) + bg_ref[...]
        pre_ref[...] = pre
        log_a = (jnp.minimum(pre, 0.0) - jnp.log(1.0 + jnp.exp(-jnp.abs(pre)))) / GLA_GATE_NORM
        la_ref[...] = _dot_exact(_chunk_tri(tm, True), log_a)

    row = lambda w: pl.BlockSpec((tm, w), lambda i: (i, 0))
    full = lambda a: pl.BlockSpec(a.shape, lambda i: (0, 0))
    return pl.pallas_call(
        body, name="mla_prep",
        grid=(t // tm,),
        in_specs=[pl.BlockSpec((tm, 512), lambda i: (i, _rel(P_CQ) // 512)),
                  pl.BlockSpec((tm, MLA_KV_RANK), lambda i: (i, _rel(P_CKV) // MLA_KV_RANK)),
                  pl.BlockSpec((tm, LANE), lambda i: (i, _rel(P_MISC) // LANE)),
                  full(g_q), full(g_kv), full(w_uq_p), full(w_k_p), full(w_v), full(w_gate_p),
                  full(b_gate), row(LANE), row(LANE), row(LANE)],
        out_specs=(row(hq), row(hq), row(MLA_WIDTH), row(GLA_DK), row(GLA_DK),
                   row(MLA_Q_RANK), row(MLA_KV_RANK), row(1), row(1), row(LANE)),
        out_shape=(jax.ShapeDtypeStruct((t, hq), BF16), jax.ShapeDtypeStruct((t, hq), BF16),
                   jax.ShapeDtypeStruct((t, MLA_WIDTH), BF16),
                   jax.ShapeDtypeStruct((t, GLA_DK), F32), jax.ShapeDtypeStruct((t, GLA_DK), F32),
                   jax.ShapeDtypeStruct((t, MLA_Q_RANK), BF16),
                   jax.ShapeDtypeStruct((t, MLA_KV_RANK), BF16),
                   jax.ShapeDtypeStruct((t, 1), F32), jax.ShapeDtypeStruct((t, 1), F32),
                   jax.ShapeDtypeStruct((t, LANE), BF16)),
        compiler_params=_cparams(("arbitrary",)),
    )(proj, proj, proj, g_q, g_kv, w_uq_p, w_k_p, w_v, w_gate_p, b_gate, rc, rsn, rsp)


def _attn_masks(tq, i):
    keys = (i + 1) * tq
    rows = i * tq + lax.broadcasted_iota(jnp.int32, (tq, keys), 0)
    cols = lax.broadcasted_iota(jnp.int32, (tq, keys), 1)
    lane = lax.broadcasted_iota(jnp.int32, (tq, LANE), 1)
    return cols <= rows, lane < MLA_VDIM


def _for_each_query_tile(n_tiles, fn):
    for i in range(n_tiles):
        pl.when(pl.program_id(1) == i)(lambda i=i: fn(i))


def _mla_attn_fwd(q, k, v, shards, sources):
    t = q.shape[0]
    tq = min(256, t)
    scale = MLA_QK ** -0.5
    ns = len(shards)
    g_shapes, g_sems = _gather_plan(shards, sources)
    grid = (MLA_HEADS // 2, t // tq)

    def body(q_ref, k_ref, v_ref, *rest):
        o_ref, lse_ref = rest[ns:ns + 2]
        start, finish = _gather_hooks(rest[:ns], rest[ns + 2:2 * ns + 2], rest[2 * ns + 2:], sources)
        step = pl.program_id(0) * grid[1] + pl.program_id(1)
        pl.when(step == 0)(start)

        def tile(i):
            keys = (i + 1) * tq
            causal, low = _attn_masks(tq, i)
            vp = v_ref[0:keys, :]
            acc = jnp.zeros((tq, LANE), F32)
            for hh in range(2):
                sl = slice(hh * HEAD_PAD, (hh + 1) * HEAD_PAD)
                s = _dot_nt(q_ref[:, sl], k_ref[0:keys, sl]) * scale
                s = jnp.where(causal, s, -jnp.inf)
                m = jnp.max(s, axis=-1, keepdims=True)
                e = jnp.exp(s - m)
                l = jnp.sum(e, axis=-1, keepdims=True)
                o = _dot(e.astype(BF16), vp) / l
                acc = jnp.where(low if hh == 0 else jnp.logical_not(low), o, acc)
                lse_ref[hh] = m + jnp.log(l)
            o_ref[...] = acc

        _for_each_query_tile(t // tq, tile)
        pl.when(step == grid[0] * grid[1] - 1)(finish)

    res = pl.pallas_call(
        body, name="mla_attn_fwd",
        grid=grid,
        in_specs=[pl.BlockSpec((tq, 2 * HEAD_PAD), lambda p, i: (i, p)),
                  pl.BlockSpec((t, 2 * HEAD_PAD), lambda p, i: (0, p)),
                  pl.BlockSpec((t, LANE), lambda p, i: (0, p))] + _hbm_specs(ns),
        out_specs=(pl.BlockSpec((tq, LANE), lambda p, i: (i, p)),
                   pl.BlockSpec((2, tq, 1), lambda p, i: (p, i, 0))) + tuple(_hbm_specs(ns)),
        out_shape=(jax.ShapeDtypeStruct((t, MLA_WIDTH), F32),
                   jax.ShapeDtypeStruct((MLA_HEADS, t, 1), F32)) + tuple(g_shapes),
        scratch_shapes=g_sems,
        compiler_params=_cparams(("arbitrary", "arbitrary")),
    )(q, k, v, *shards)
    return res[0], res[1], res[2:]


def _mla_attn_bwd(q, k, v, o, do, lse, after):
    t = q.shape[0]
    tq = min(256, t)
    scale = MLA_QK ** -0.5

    def body(q_ref, k_ref, v_ref, o_ref, do_ref, lse_ref, after_ref, dq_ref, dk_ref, dv_ref):
        del after_ref

        @pl.when(pl.program_id(1) == 0)
        def _():
            dk_ref[...] = jnp.zeros_like(dk_ref)
            dv_ref[...] = jnp.zeros_like(dv_ref)

        def tile(i):
            keys = (i + 1) * tq
            causal, low = _attn_masks(tq, i)
            vp = v_ref[0:keys, :]
            do_all = do_ref[...]
            o_all = o_ref[...]
            dv_acc = jnp.zeros((keys, LANE), F32)
            for hh in range(2):
                sl = slice(hh * HEAD_PAD, (hh + 1) * HEAD_PAD)
                do_h = jnp.where(low if hh == 0 else jnp.logical_not(low), do_all, 0.0)
                dsum = jnp.sum(do_h * o_all, axis=-1, keepdims=True)
                qh = q_ref[:, sl]
                kh = k_ref[0:keys, sl]
                s = _dot_nt(qh, kh) * scale
                p = jnp.where(causal, jnp.exp(s - lse_ref[hh]), 0.0)
                do_b = do_h.astype(BF16)
                dp = _dot_nt(do_b, vp)
                ds = (p * (dp - dsum) * scale).astype(BF16)
                dq_ref[:, sl] = _dot(ds, kh).astype(BF16)
                dk_ref[0:keys, sl] += _dot_tn(ds, qh)
                dv_acc = dv_acc + _dot_tn(p.astype(BF16), do_b)
            dv_ref[0:keys, :] += dv_acc

        _for_each_query_tile(t // tq, tile)

    return pl.pallas_call(
        body, name="mla_attn_bwd",
        grid=(MLA_HEADS // 2, t // tq),
        in_specs=[pl.BlockSpec((tq, 2 * HEAD_PAD), lambda p, i: (i, p)),
                  pl.BlockSpec((t, 2 * HEAD_PAD), lambda p, i: (0, p)),
                  pl.BlockSpec((t, LANE), lambda p, i: (0, p)),
                  pl.BlockSpec((tq, LANE), lambda p, i: (i, p)),
                  pl.BlockSpec((tq, LANE), lambda p, i: (i, p)),
                  pl.BlockSpec((2, tq, 1), lambda p, i: (p, i, 0)),
                  pl.BlockSpec(memory_space=pl.ANY)],
        out_specs=(pl.BlockSpec((tq, 2 * HEAD_PAD), lambda p, i: (i, p)),
                   pl.BlockSpec((t, 2 * HEAD_PAD), lambda p, i: (0, p)),
                   pl.BlockSpec((t, LANE), lambda p, i: (0, p))),
        out_shape=(jax.ShapeDtypeStruct((t, MLA_HEADS * HEAD_PAD), BF16),
                   jax.ShapeDtypeStruct((t, MLA_HEADS * HEAD_PAD), F32),
                   jax.ShapeDtypeStruct((t, MLA_WIDTH), F32)),
        compiler_params=_cparams(("arbitrary", "arbitrary")),
    )(q, k, v, o, do, lse, after)


def _chunk_tri(n, lower):
    r = lax.broadcasted_iota(jnp.int32, (n, n), 0)
    c = lax.broadcasted_iota(jnp.int32, (n, n), 1)
    same = (r // GLA_CHUNK) == (c // GLA_CHUNK)
    return jnp.where(jnp.logical_and(same, r >= c if lower else r <= c), 1.0, 0.0).astype(F32)


def _gla_chunk_terms(q_ref, k_ref, b_ref, h, rows):
    sl = slice(h * GLA_HK, (h + 1) * GLA_HK)
    b = b_ref[rows, sl]
    bl = b[GLA_CHUNK - 1:GLA_CHUNK, :]
    kc = k_ref[rows, sl]
    q_in = (q_ref[rows, sl] * (GLA_HK ** -0.5)) * jnp.exp(b)
    k_in = kc * jnp.exp(-b)
    k_st = kc * jnp.exp(bl - b)
    return b, bl, q_in, k_in, k_st


def _tri(c, lower):
    r = lax.broadcasted_iota(jnp.int32, (c, c), 0)
    cc = lax.broadcasted_iota(jnp.int32, (c, c), 1)
    return jnp.where(r >= cc if lower else r <= cc, 1.0, 0.0).astype(F32)


def _gla_fwd(proj, log_a):
    t = proj.shape[0]
    per = GLA_CHUNKS_PER_STEP
    n = t // GLA_CHUNK
    c = GLA_CHUNK * per

    def body(q_ref, k_ref, v_ref, la_ref, o_ref, sp_ref, st_ref):
        @pl.when(pl.program_id(0) == 0)
        def _():
            st_ref[...] = jnp.zeros_like(st_ref)

        tri = _tri(GLA_CHUNK, True)
        for s, h in [(s, h) for s in range(per) for h in range(GLA_HEADS)]:
            rows = slice(s * GLA_CHUNK, (s + 1) * GLA_CHUNK)
            _, bl, q_in, k_in, k_st = _gla_chunk_terms(q_ref, k_ref, la_ref, h, rows)
            vs = slice(h * GLA_HV, (h + 1) * GLA_HV)
            vv = v_ref[rows, vs].astype(BF16)
            qb = q_in.astype(BF16)
            attn = _dot_nt(qb, k_in.astype(BF16)) * tri
            st = st_ref[h]
            sp_ref[s, h] = st
            o_ref[rows, vs] = _dot(attn.astype(BF16), vv) + _dot_nt(qb, st.astype(BF16))
            st_ref[h] = st * jnp.exp(bl) + _dot_tn(vv, k_st.astype(BF16))

    return pl.pallas_call(
        body, name="gla_fwd",
        grid=(n // per,),
        in_specs=[pl.BlockSpec((c, GLA_DK), lambda i: (i, P_QG // GLA_DK)),
                  pl.BlockSpec((c, GLA_DK), lambda i: (i, P_KG // GLA_DK)),
                  pl.BlockSpec((c, GLA_DV), lambda i: (i, P_VG // GLA_DV)),
                  pl.BlockSpec((c, GLA_DK), lambda i: (i, 0))],
        out_specs=(pl.BlockSpec((c, GLA_DV), lambda i: (i, 0)),
                   pl.BlockSpec((per, GLA_HEADS, GLA_HV, GLA_HK), lambda i: (i, 0, 0, 0))),
        out_shape=(jax.ShapeDtypeStruct((t, GLA_DV), F32),
                   jax.ShapeDtypeStruct((n, GLA_HEADS, GLA_HV, GLA_HK), F32)),
        scratch_shapes=[pltpu.VMEM((GLA_HEADS, GLA_HV, GLA_HK), F32)],
        compiler_params=_cparams(("arbitrary",)),
    )(proj, proj, proj, log_a)


def _gla_bwd(proj, log_a, do, states, after):
    t = proj.shape[0]
    per = GLA_CHUNKS_PER_STEP
    c = GLA_CHUNK * per
    n = t // c

    def body(q_ref, k_ref, v_ref, la_ref, do_ref, sp_ref, after_ref, dg_ref, dla_ref, ds_ref):
        del after_ref

        @pl.when(pl.program_id(0) == 0)
        def _():
            ds_ref[...] = jnp.zeros_like(ds_ref)

        tri = _tri(GLA_CHUNK, True)
        last = lax.broadcasted_iota(jnp.int32, (GLA_CHUNK, GLA_HK), 0) == GLA_CHUNK - 1
        for s, h in [(s, h) for s in reversed(range(per)) for h in range(GLA_HEADS)]:
            rows = slice(s * GLA_CHUNK, (s + 1) * GLA_CHUNK)
            b, bl, q_in, k_in, k_st = _gla_chunk_terms(q_ref, k_ref, la_ref, h, rows)
            ks_ = slice(h * GLA_HK, (h + 1) * GLA_HK)
            vs = slice(h * GLA_HV, (h + 1) * GLA_HV)
            vv = v_ref[rows, vs].astype(BF16)
            do_h = do_ref[rows, vs]
            qb, kb, ksb = q_in.astype(BF16), k_in.astype(BF16), k_st.astype(BF16)
            attn = (_dot_nt(qb, kb) * tri).astype(BF16)
            st = sp_ref[s, h]
            dst = ds_ref[h]
            dstb = dst.astype(BF16)
            dattn = (_dot_nt(do_h, vv) * tri).astype(BF16)
            dg_ref[rows, P_VG + h * GLA_HV:P_VG + (h + 1) * GLA_HV] = (
                _dot_tn(attn, do_h) + _dot_nt(ksb, dstb)).astype(BF16)
            dq_in = _dot(dattn, kb) + _dot(do_h, st.astype(BF16))
            dk_in = _dot_tn(dattn, qb)
            dk_st = _dot(vv, dstb)
            ebl = jnp.exp(bl)
            d_ebl = jnp.sum(st * dst, axis=0, keepdims=True)
            ds_ref[h] = _dot_tn(do_h, qb) + dst * ebl
            dg_ref[rows, P_QG + h * GLA_HK:P_QG + (h + 1) * GLA_HK] = (
                dq_in * (GLA_HK ** -0.5) * jnp.exp(b)).astype(BF16)
            dg_ref[rows, P_KG + h * GLA_HK:P_KG + (h + 1) * GLA_HK] = (
                dk_in * jnp.exp(-b) + dk_st * jnp.exp(bl - b)).astype(BF16)
            db = dq_in * q_in - dk_in * k_in - dk_st * k_st
            dbl = jnp.sum(dk_st * k_st, axis=0, keepdims=True) + d_ebl * ebl
            dla_ref[rows, ks_] = db + jnp.where(last, dbl, 0.0)

    rev = lambda i: n - 1 - i
    gw = P_GROUPS[0][1]
    return pl.pallas_call(
        body, name="gla_bwd",
        grid=(n,),
        in_specs=[pl.BlockSpec((c, GLA_DK), lambda i: (rev(i), P_QG // GLA_DK)),
                  pl.BlockSpec((c, GLA_DK), lambda i: (rev(i), P_KG // GLA_DK)),
                  pl.BlockSpec((c, GLA_DV), lambda i: (rev(i), P_VG // GLA_DV)),
                  pl.BlockSpec((c, GLA_DK), lambda i: (rev(i), 0)),
                  pl.BlockSpec((c, GLA_DV), lambda i: (rev(i), 0)),
                  pl.BlockSpec((per, GLA_HEADS, GLA_HV, GLA_HK), lambda i: (rev(i), 0, 0, 0)),
                  pl.BlockSpec(memory_space=pl.ANY)],
        out_specs=(pl.BlockSpec((c, gw), lambda i: (rev(i), 0)),
                   pl.BlockSpec((c, GLA_DK), lambda i: (rev(i), 0))),
        out_shape=(jax.ShapeDtypeStruct((t, gw), BF16), jax.ShapeDtypeStruct((t, GLA_DK), F32)),
        scratch_shapes=[pltpu.VMEM((GLA_HEADS, GLA_HV, GLA_HK), F32)],
        compiler_params=_cparams(("arbitrary",)),
    )(proj, proj, proj, log_a, do, states, after)


def _post(o_mla, proj, o_gla, x, target, g_gla, g_final, w_pm, w_pg, w_o):
    t = x.shape[0]
    tm = min(256, t)
    g0, gw = P_GROUPS[1]

    def body(om_ref, zg_ref, gm_ref, gg_ref, zm_ref, og_ref, x_ref, tg_ref, ggla_ref, gf_ref,
             wpm_ref, wpg_ref, wo_ref,
             dx2_ref, dom_ref, dog_ref, dg_ref,
             mg_ref, um_ref, ug_ref, dym_ref, dyg_ref, loss_ref, dgf_ref, dggla_ref):
        @pl.when(pl.program_id(0) == 0)
        def _():
            loss_ref[...] = jnp.zeros_like(loss_ref)
            dgf_ref[...] = jnp.zeros_like(dgf_ref)
            dggla_ref[...] = jnp.zeros_like(dggla_ref)

        om = om_ref[...]
        zm = zm_ref[...]
        sm = _sigmoid(zm)
        silu_m = zm * sm
        um = (om * silu_m).astype(BF16)
        um_ref[...] = um
        ym = _dot(um, wpm_ref[...])

        ggla = ggla_ref[...]
        zg = zg_ref[...]
        sg = _sigmoid(zg)
        silu_g = zg * sg
        xhat, rstd, on = [], [], []
        for h in range(GLA_HEADS):
            blk = og_ref[:, h * GLA_HV:(h + 1) * GLA_HV]
            r = lax.rsqrt(jnp.mean(blk * blk, axis=-1, keepdims=True) + EPS)
            xhat.append(blk * r)
            rstd.append(r)
            on.append(xhat[h] * ggla)
        on = jnp.concatenate(on, axis=-1)
        ug = (on * silu_g).astype(BF16)
        ug_ref[...] = ug
        yg = _dot(ug, wpg_ref[...])

        sgm = _sigmoid(gm_ref[...])
        sgg = _sigmoid(gg_ref[...])
        merged = (sgm * ym + sgg * yg).astype(BF16)
        mg_ref[...] = merged
        x2 = x_ref[...] + _dot(merged, wo_ref[...])
        gf = gf_ref[...]
        rf = lax.rsqrt(jnp.mean(x2 * x2, axis=-1, keepdims=True) + EPS)
        xh = x2 * rf
        err = xh * gf - tg_ref[...]
        loss_ref[...] += 0.5 * jnp.sum(jnp.mean(err * err, axis=-1, keepdims=True))

        dy = err * (1.0 / D_MODEL)
        dgf_ref[...] += jnp.sum(dy * xh, axis=0, keepdims=True)
        dxh = dy * gf
        dx2 = rf * (dxh - xh * jnp.mean(dxh * xh, axis=-1, keepdims=True))
        dx2_ref[...] = dx2
        dmerged = _dot_nt(dx2.astype(BF16), wo_ref[...])
        dym = (dmerged * sgm).astype(BF16)
        dyg = (dmerged * sgg).astype(BF16)
        dym_ref[...] = dym
        dyg_ref[...] = dyg
        dg_ref[:, P_GMLA - g0:P_GMLA - g0 + D_MODEL] = (dmerged * ym * sgm * (1.0 - sgm)).astype(BF16)
        dg_ref[:, P_GGLA - g0:P_GGLA - g0 + D_MODEL] = (dmerged * yg * sgg * (1.0 - sgg)).astype(BF16)
        dum = _dot_nt(dym, wpm_ref[...])
        dom_ref[...] = dum * silu_m
        dg_ref[:, P_ZMLA - g0:P_ZMLA - g0 + MLA_WIDTH] = (
            dum * om * (sm * (1.0 + zm * (1.0 - sm)))).astype(BF16)
        dug = _dot_nt(dyg, wpg_ref[...])
        dg_ref[:, P_ZGLA - g0:P_ZGLA - g0 + GLA_DV] = (
            dug * on * (sg * (1.0 + zg * (1.0 - sg)))).astype(BF16)
        don = dug * silu_g
        dggla = jnp.zeros((1, GLA_HV), F32)
        for h in range(GLA_HEADS):
            hs = slice(h * GLA_HV, (h + 1) * GLA_HV)
            don_h = don[:, hs]
            dggla = dggla + jnp.sum(don_h * xhat[h], axis=0, keepdims=True)
            dxh_h = don_h * ggla
            dog_ref[:, hs] = (rstd[h] * (dxh_h - xhat[h] * jnp.mean(dxh_h * xhat[h], axis=-1,
                                                                     keepdims=True))).astype(BF16)
        dggla_ref[...] += dggla

    row = lambda w: pl.BlockSpec((tm, w), lambda i: (i, 0))
    pcol = lambda w, off: pl.BlockSpec((tm, w), lambda i: (i, _rel(off) // w))
    full = lambda a: pl.BlockSpec(a.shape, lambda i: (0, 0))
    sds = jax.ShapeDtypeStruct
    return pl.pallas_call(
        body, name="post_fwd_bwd",
        grid=(t // tm,),
        in_specs=[row(MLA_WIDTH), pcol(GLA_DV, P_ZGLA), pcol(D_MODEL, P_GMLA), pcol(D_MODEL, P_GGLA),
                  pcol(MLA_WIDTH, P_ZMLA), row(GLA_DV), row(D_MODEL), row(D_MODEL),
                  full(g_gla), full(g_final), full(w_pm), full(w_pg), full(w_o)],
        out_specs=(row(D_MODEL), row(MLA_WIDTH), row(GLA_DV), row(gw),
                   row(D_MODEL), row(MLA_WIDTH), row(GLA_DV), row(D_MODEL), row(D_MODEL),
                   pl.BlockSpec((1, LANE), lambda i: (0, 0)),
                   pl.BlockSpec((1, D_MODEL), lambda i: (0, 0)),
                   pl.BlockSpec((1, GLA_HV), lambda i: (0, 0))),
        out_shape=(sds((t, D_MODEL), F32), sds((t, MLA_WIDTH), F32), sds((t, GLA_DV), BF16),
                   sds((t, gw), BF16),
                   sds((t, D_MODEL), BF16), sds((t, MLA_WIDTH), BF16), sds((t, GLA_DV), BF16),
                   sds((t, D_MODEL), BF16), sds((t, D_MODEL), BF16),
                   sds((1, LANE), F32), sds((1, D_MODEL), F32), sds((1, GLA_HV), F32)),
        compiler_params=_cparams(("arbitrary",)),
    )(o_mla, proj, proj, proj, proj, o_gla, x, target, g_gla, g_final, w_pm, w_pg, w_o)


def _mla_prep_bwd(dq, dk, dv, dla, pre, proj, rq, rkv, g_q, g_kv, w_uq_p, w_k_p, w_v, w_gate_p,
                  rc, rsn, rsp):
    t = proj.shape[0]
    tm = min(256, t)
    gw = P_GROUPS[2][1]

    def body(dq_ref, dk_ref, dv_ref, dla_ref, pre_ref, cq_ref, ckv_ref, rq_ref, rkv_ref,
             gq_ref, gkv_ref, wuq_ref, wk_ref, wv_ref, wg_ref, c_ref, sn_ref, sp_ref,
             dg_ref, dqpre_ref, dpre_ref, dgq_ref, dgkv_ref, dbg_ref):
        @pl.when(pl.program_id(0) == 0)
        def _():
            dgq_ref[...] = jnp.zeros_like(dgq_ref)
            dgkv_ref[...] = jnp.zeros_like(dgkv_ref)
            dbg_ref[...] = jnp.zeros_like(dbg_ref)

        c, sn, sp = c_ref[...], sn_ref[...], sp_ref[...]
        dkr = jnp.zeros((tm, LANE), F32)
        for h in range(MLA_HEADS):
            sl = slice(h * HEAD_PAD, (h + 1) * HEAD_PAD)
            dqpre_ref[:, sl] = _rope_bwd(dq_ref[:, sl].astype(F32), c, sn, sp).astype(BF16)
            dkr = dkr + dk_ref[:, sl]
        dcqn = _dot_nt(dqpre_ref[...], wuq_ref[...])
        rq = rq_ref[...]
        xh = cq_ref[:, :MLA_Q_RANK] * rq
        dgq_ref[...] += jnp.sum(dcqn * xh, axis=0, keepdims=True)
        dxh = dcqn * gq_ref[...]
        dcq = rq * (dxh - xh * jnp.mean(dxh * xh, axis=-1, keepdims=True))
        dg_ref[:, :MLA_Q_RANK] = dcq.astype(BF16)
        dg_ref[:, MLA_Q_RANK:512] = jnp.zeros((tm, 512 - MLA_Q_RANK), BF16)

        dckvn = _dot_nt(dk_ref[...].astype(BF16), wk_ref[...]) + \
            _dot_nt(dv_ref[...].astype(BF16), wv_ref[...])
        rkv = rkv_ref[...]
        xh = ckv_ref[...] * rkv
        dgkv_ref[...] += jnp.sum(dckvn * xh, axis=0, keepdims=True)
        dxh = dckvn * gkv_ref[...]
        dg_ref[:, P_CKV - P_CQ:P_CKV - P_CQ + MLA_KV_RANK] = (
            rkv * (dxh - xh * jnp.mean(dxh * xh, axis=-1, keepdims=True))).astype(BF16)

        dlog_a = _dot_exact(_chunk_tri(tm, False), dla_ref[...])
        dpre = dlog_a * (1.0 / GLA_GATE_NORM) * (1.0 - _sigmoid(pre_ref[...]))
        dbg_ref[...] += jnp.sum(dpre, axis=0, keepdims=True)
        dpre = dpre.astype(BF16)
        dpre_ref[...] = dpre
        lane = lax.broadcasted_iota(jnp.int32, (tm, LANE), 1)
        in_kr = jnp.logical_and(lane >= MISC_KR, lane < MISC_KR + MLA_ROPE)
        dmisc = jnp.where(in_kr, _rope_bwd(dkr, c, sn, sp), 0.0) + _dot_nt(dpre, wg_ref[...])
        dg_ref[:, P_MISC - P_CQ:P_MISC - P_CQ + LANE] = dmisc.astype(BF16)

    hq = MLA_HEADS * HEAD_PAD
    row = lambda w: pl.BlockSpec((tm, w), lambda i: (i, 0))
    full = lambda a: pl.BlockSpec(a.shape, lambda i: (0, 0))
    acc = lambda w: pl.BlockSpec((1, w), lambda i: (0, 0))
    sds = jax.ShapeDtypeStruct
    return pl.pallas_call(
        body, name="mla_prep_bwd",
        grid=(t // tm,),
        in_specs=[row(hq), row(hq), row(MLA_WIDTH), row(GLA_DK), row(GLA_DK),
                  pl.BlockSpec((tm, 512), lambda i: (i, _rel(P_CQ) // 512)),
                  pl.BlockSpec((tm, MLA_KV_RANK), lambda i: (i, _rel(P_CKV) // MLA_KV_RANK)),
                  row(1), row(1), full(g_q), full(g_kv), full(w_uq_p), full(w_k_p), full(w_v),
                  full(w_gate_p), row(LANE), row(LANE), row(LANE)],
        out_specs=(row(gw), row(hq), row(GLA_DK),
                   acc(MLA_Q_RANK), acc(MLA_KV_RANK), acc(GLA_DK)),
        out_shape=(sds((t, gw), BF16), sds((t, hq), BF16), sds((t, GLA_DK), BF16),
                   sds((1, MLA_Q_RANK), F32), sds((1, MLA_KV_RANK), F32), sds((1, GLA_DK), F32)),
        compiler_params=_cparams(("arbitrary",)),
    )(dq, dk, dv, dla, pre, proj, proj, rq, rkv, g_q, g_kv, w_uq_p, w_k_p, w_v, w_gate_p,
      rc, rsn, rsp)


def _inproj_bwd(dgroups, w_pts, x, rstd, g_in, dx2, after):
    t = x.shape[0]
    tm = min(256, t)

    def body(d0_ref, d1_ref, d2_ref, w0_ref, w1_ref, w2_ref, x_ref, r_ref, g_ref, dx2_ref, after_ref,
             dx_ref, dg_ref):
        del after_ref

        @pl.when(pl.program_id(0) == 0)
        def _():
            dg_ref[...] = jnp.zeros_like(dg_ref)

        dh = jnp.zeros((tm, D_MODEL), F32)
        for d_ref, w_ref in zip((d0_ref, d1_ref, d2_ref), (w0_ref, w1_ref, w2_ref)):
            dh = dh + _dot(d_ref[...], w_ref[...])
        r = r_ref[...]
        xh = x_ref[...] * r
        dg_ref[...] += jnp.sum(dh * xh, axis=0, keepdims=True)
        dxh = dh * g_ref[...]
        dx_ref[...] = dx2_ref[...] + r * (dxh - xh * jnp.mean(dxh * xh, axis=-1, keepdims=True))

    row = lambda w: pl.BlockSpec((tm, w), lambda i: (i, 0))
    return pl.pallas_call(
        body, name="inproj_bwd",
        grid=(t // tm,),
        in_specs=[row(w) for _, w in P_GROUPS]
        + [pl.BlockSpec((w, D_MODEL), lambda i: (0, 0)) for _, w in P_GROUPS]
        + [row(D_MODEL), row(1), pl.BlockSpec((1, D_MODEL), lambda i: (0, 0)), row(D_MODEL),
           pl.BlockSpec(memory_space=pl.ANY)],
        out_specs=(row(D_MODEL), pl.BlockSpec((1, D_MODEL), lambda i: (0, 0))),
        out_shape=(jax.ShapeDtypeStruct((t, D_MODEL), F32),
                   jax.ShapeDtypeStruct((1, D_MODEL), F32)),
        compiler_params=_cparams(("arbitrary",)),
    )(*dgroups, *w_pts, x, rstd, g_in, dx2, after)


def _matmul(name, a, b, tm, tn, dtype=F32, after=None):
    kk, m = a.shape
    n = b.shape[1]
    extra = [] if after is None else [after]

    def body(a_ref, b_ref, *rest):
        rest[-1][...] = _dot_tn(a_ref[...].astype(BF16), b_ref[...].astype(BF16)).astype(dtype)

    return pl.pallas_call(
        body, name=name,
        grid=(n // tn, m // tm),
        in_specs=[pl.BlockSpec((kk, tm), lambda j, i: (0, i)),
                  pl.BlockSpec((kk, tn), lambda j, i: (0, j))]
        + [pl.BlockSpec(memory_space=pl.ANY) for _ in extra],
        out_specs=pl.BlockSpec((tm, tn), lambda j, i: (i, j)),
        out_shape=jax.ShapeDtypeStruct((m, n), dtype),
        compiler_params=_cparams(("arbitrary", "arbitrary")),
    )(a, b, *extra)


def _adamw_update(part_refs, w_ref, m_ref, v_ref, g_ref, d_ref, nm_ref, nv_ref):
    g = part_refs[0][...].astype(F32)
    for p_ref in part_refs[1:]:
        g = g + p_ref[...].astype(F32)
    m_new = ADAM_B1 * m_ref[...] + (1.0 - ADAM_B1) * g
    v_new = ADAM_B2 * v_ref[...] + (1.0 - ADAM_B2) * (g * g)
    m_hat = m_new / (1.0 - ADAM_B1 ** ADAM_STEP)
    v_hat = v_new / (1.0 - ADAM_B2 ** ADAM_STEP)
    g_ref[...] = g
    nm_ref[...] = m_new
    nv_ref[...] = v_new
    d_ref[...] = -ADAM_LR * (m_hat / (jnp.sqrt(v_hat) + ADAM_EPS) + ADAM_WD * w_ref[...])


def _adamw_rows(name, parts, w, m, v, tr, first=None):
    _, rows, cols = w.shape
    slots = parts.shape[0]

    def body(*refs):
        lead_refs, p_ref = ([], refs[0]) if first is None else ([refs[0]], refs[1])
        _adamw_update(lead_refs + [p_ref.at[q] for q in range(slots)], *refs[len(lead_refs) + 1:])

    blk = pl.BlockSpec((None, tr, cols), lambda i: (0, i, 0))
    out = jax.ShapeDtypeStruct((1, rows, cols), F32)
    lead = [] if first is None else [pl.BlockSpec((tr, cols), lambda i: (i, 0))]
    return pl.pallas_call(
        body, name=name,
        grid=(rows // tr,),
        in_specs=lead + [pl.BlockSpec((slots, tr, cols), lambda i: (0, i, 0)), blk, blk, blk],
        out_specs=(blk, blk, blk, blk),
        out_shape=(out, out, out, out),
        compiler_params=_cparams(("arbitrary",)),
    )(*([] if first is None else [first]), parts, w, m, v)


def _adamw_transposed(name, first, parts, w, m, v, tl):
    _, rows, cols = w.shape
    slots, padded = parts.shape[:2]

    def body(f_ref, p_ref, *refs):
        _adamw_update([f_ref.at[pl.ds(0, cols)]]
                      + [p_ref.at[q, pl.ds(0, cols)] for q in range(slots)], *refs)

    blk = pl.BlockSpec((cols, None, tl), lambda i: (0, 0, i))
    out = jax.ShapeDtypeStruct((cols, 1, rows), F32)
    res = pl.pallas_call(
        body, name=name,
        grid=(rows // tl,),
        in_specs=[pl.BlockSpec((padded, tl), lambda i: (0, i)),
                  pl.BlockSpec((slots, padded, tl), lambda i: (0, 0, i)), blk, blk, blk],
        out_specs=(blk, blk, blk, blk),
        out_shape=(out, out, out, out),
        compiler_params=_cparams(("arbitrary",)),
    )(first, parts, *[a.transpose(2, 0, 1) for a in (w, m, v)])
    return [r.transpose(1, 2, 0) for r in res]


def _adamw_group(firsts, parts, ws, ms, vs):
    n = len(ws)

    def body(*refs):
        ins, outs = refs[:5 * n], refs[5 * n:]
        x, y, c = _mesh_pos()
        for a in range(n):
            _adamw_update([ins[a].at[4 * x + 2 * y + c]]
                          + [ins[n + a].at[q] for q in range(ins[n + a].shape[0])],
                          *[r.at[0] for r in (ins[2 * n + a], ins[3 * n + a], ins[4 * n + a])],
                          *[r.at[0] for r in outs[4 * a:4 * a + 4]])

    vmem = lambda k: [pl.BlockSpec(memory_space=pltpu.VMEM) for _ in range(k)]
    out_shape = []
    for w in ws:
        out_shape += [jax.ShapeDtypeStruct(w.shape, F32)] * 4
    res = pl.pallas_call(
        body, name="adamw_small_weights",
        in_specs=vmem(5 * n), out_specs=tuple(vmem(4 * n)), out_shape=tuple(out_shape),
        compiler_params=_cparams(),
    )(*firsts, *parts, *ws, *ms, *vs)
    return [res[4 * a:4 * a + 4] for a in range(n)]


def _rope_tables(positions):
    half = MLA_ROPE // 2
    freqs = ROPE_THETA ** (-jnp.arange(half, dtype=F32) / half)
    ang = positions.astype(F32).reshape(-1, 1) * freqs
    cos, sin = jnp.cos(ang), jnp.sin(ang)
    t = ang.shape[0]
    one, zero = jnp.ones((t, MLA_NOPE), F32), jnp.zeros((t, half), F32)
    tail = jnp.zeros((t, LANE - MLA_QK), F32)
    rc = jnp.concatenate([one, cos, cos, tail], axis=1)
    rsn = jnp.concatenate([0.0 * one, -sin, zero, tail], axis=1)
    rsp = jnp.concatenate([0.0 * one, zero, sin, tail], axis=1)
    return rc, rsn, rsp


def _cols_full(g):
    return g.transpose(1, 0, 2)


def kernel(x, positions, g_in, w_in, g_q, w_uq, g_kv, w_ukv, w_gla_gate, b_gla_gate, g_gla, w_proj_mla, w_proj_gla, w_out, g_final, loss_target, m_g_in, m_w_in, m_g_q, m_w_uq, m_g_kv, m_w_ukv, m_w_gla_gate, m_b_gla_gate, m_g_gla, m_w_proj_mla, m_w_proj_gla, m_w_out, m_g_final, v_g_in, v_w_in, v_g_q, v_w_uq, v_g_kv, v_w_ukv, v_w_gla_gate, v_b_gla_gate, v_g_gla, v_w_proj_mla, v_w_proj_gla, v_w_out, v_g_final):
    t = x.shape[1]
    x2d = x.reshape(t, D_MODEL)
    tgt = loss_target.reshape(t, D_MODEL)
    g_final2 = g_final.reshape(1, D_MODEL)
    sharded = [(w_in, m_w_in, v_w_in), (w_uq, m_w_uq, v_w_uq), (w_ukv, m_w_ukv, v_w_ukv),
               (w_gla_gate, m_w_gla_gate, v_w_gla_gate), (w_proj_mla, m_w_proj_mla, v_w_proj_mla),
               (w_proj_gla, m_w_proj_gla, v_w_proj_gla), (w_out, m_w_out, v_w_out)]

    w_in_t = w_in.transpose(2, 0, 1).reshape(SHARD_COLS, D_MODEL)
    everyone = tuple(range(N_DEV))
    w_in_b = w_in_t.astype(BF16)
    b_uq, b_ukv, b_gate, b_pm, b_pg, b_o = [s[0][0].astype(BF16) for s in sharded[1:]]
    stages = ((0, 2, 4, 6), (1, 3, 5, 7))
    where = {d: (k, i) for k, srcs in enumerate(stages) for i, d in enumerate(srcs)}
    g_in_1, g_uq, g_ukv, g_gate = _all_gather(
        "all_gather_first", [w_in_b, b_uq, b_ukv, b_gate], [stages[0]] + [everyone] * 3)
    w_uq_p = jnp.pad(_cols_full(g_uq), ((0, 0), (0, 0), (0, HEAD_PAD - MLA_QK))).reshape(
        MLA_Q_RANK, MLA_HEADS * HEAD_PAD)
    ukv = _cols_full(g_ukv)
    w_k_p = jnp.pad(ukv[:, :, :MLA_NOPE], ((0, 0), (0, 0), (0, HEAD_PAD - MLA_NOPE))).reshape(
        MLA_KV_RANK, MLA_HEADS * HEAD_PAD)
    w_v = ukv[:, :, MLA_NOPE:].reshape(MLA_KV_RANK, MLA_WIDTH)
    w_gate_p = jnp.pad(_cols_full(g_gate).reshape(GLA_GATE_RANK, GLA_DK),
                       ((MISC_ALR, LANE - MISC_ALR - GLA_GATE_RANK), (0, 0)))
    rc, rsn, rsp = _rope_tables(positions)

    w_lat = _weights_to_p("weights_latents", [g_in_1], where, 2)
    proj_lat, h, rstd = _inproj(x2d, g_in, w_lat)
    q, k, v, log_a, pre, cqn, ckvn, rq, rkv, misc = _mla_prep(
        proj_lat, g_q, g_kv, w_uq_p, w_k_p, w_v, w_gate_p, b_gla_gate, rc, rsn, rsp)
    o_mla, lse, (g_in_2, g_pm, g_pg, g_o) = _mla_attn_fwd(
        q, k, v, [w_in_b, b_pm, b_pg, b_o], [stages[1]] + [everyone] * 3)
    w_gla = _weights_to_p("weights_gla", [g_in_1, g_in_2], where, 0)
    proj_gla = _proj("inproj_gla", h, w_gla)
    o_gla, states = _gla_fwd(proj_gla, log_a)
    w_out_path = _weights_to_p("weights_out_path", [g_in_1, g_in_2], where, 1)
    proj_out = _proj("inproj_out_path", h, w_out_path)
    w_in_p = (w_gla, w_out_path, w_lat)
    w_pm = _cols_full(g_pm).reshape(MLA_WIDTH, D_MODEL)
    w_pg = g_pg.reshape(GLA_DV, D_MODEL)
    w_o = g_o.reshape(D_MODEL, D_MODEL)

    (dx2, do_mla, do_gla, d_out, merged, um, ug, dym, dyg, loss_p, dg_final,
     dg_gla) = _post(o_mla, proj_out, o_gla, x2d, tgt, g_gla, g_final2, w_pm, w_pg, w_o)

    p_pm = _matmul("dw_proj_mla", um, dym, 512, 512, BF16).reshape(
        MLA_WIDTH, N_DEV, D_MODEL // N_DEV).transpose(1, 0, 2)
    p_pg = _matmul("dw_proj_gla", ug, dyg, 512, 512, BF16).reshape(N_DEV, -1, D_MODEL)
    p_o = _matmul("dw_out", merged, dx2, 512, 512, BF16).reshape(N_DEV, -1, D_MODEL)
    own_in = jnp.zeros((SHARD_PAD, D_MODEL), BF16)
    land_in = lax.empty((PEERS, SHARD_PAD, D_MODEL), BF16)
    dw_groups, started, lands = {}, [], [land_in]

    def reduce_scatter_stage(s, dests, own_in, extra=()):
        parts_in, own_in = _grads_to_shards("grads_to_shards_%d" % s, dw_groups, dests, own_in)
        first = len(lands)
        lands.extend(lax.empty((PEERS,) + p.shape[1:], BF16) for p in extra)
        idx = [0] + list(range(first, len(lands)))
        all_dests = [[(i, d, r0, r1) for i, (d, ranges) in enumerate(dests) for r0, r1 in ranges]]
        all_dests += [_whole(everyone, p.shape[1]) for p in extra]
        sems, parts, new_lands, token = _ici_start(
            "ici_start_%d" % s, [parts_in] + list(extra), [lands[i] for i in idx], all_dests)
        for a, i in enumerate(idx):
            lands[i] = new_lands[a]
            started.append((sems[a][0], sems[a][1], parts[a], i, all_dests[a]))
        return own_in, token

    def late_small_stage(arrays):
        idx = list(range(len(lands), len(lands) + len(arrays)))
        lands.extend(lax.empty((PEERS,) + p.shape[1:], BF16) for p in arrays)
        all_dests = [_whole(everyone, p.shape[1]) for p in arrays]
        sems, parts, new_lands, token = _ici_start(
            "ici_start_4", list(arrays), [lands[i] for i in idx], all_dests)
        for a, i in enumerate(idx):
            lands[i] = new_lands[a]
            started.append((sems[a][0], sems[a][1], parts[a], i, all_dests[a]))
        return token

    dw_groups[1] = _matmul("dw_in_1", d_out, h, 512, D_MODEL, BF16)
    full = [(0, SHARD_PAD)]
    own_in, token = reduce_scatter_stage(
        1, [(5, full), (6, full), (7, full), (0, [(672, SHARD_PAD)]), (1, [(0, 384)]),
            (4, [(96, SHARD_PAD)])], own_in, (p_pm, p_pg, p_o))
    d_gla, dla = _gla_bwd(proj_gla, log_a, do_gla, states, token)
    dw_groups[0] = _matmul("dw_in_0", d_gla, h, 512, D_MODEL, BF16)
    own_in, token = reduce_scatter_stage(
        2, [(1, [(384, SHARD_PAD)]), (2, full), (3, full), (4, [(0, 64)])], own_in)
    dq, dk, dv = _mla_attn_bwd(q, k, v, o_mla, do_mla, lse, token)
    d_lat, dqpre, dpre, dg_q, dg_kv, db_gate = _mla_prep_bwd(
        dq, dk, dv, dla, pre, proj_lat, rq, rkv, g_q, g_kv, w_uq_p, w_k_p, w_v, w_gate_p, rc, rsn, rsp)
    dw_groups[2] = _matmul("dw_in_2", d_lat, h, 896, D_MODEL, BF16)
    own_in, token = reduce_scatter_stage(3, [(0, [(0, 672)]), (4, [(64, 96)])], own_in)
    dw_uq = _matmul("dw_uq", cqn, dqpre, MLA_Q_RANK, 512, BF16, after=token)
    p_uq = dw_uq.reshape(MLA_Q_RANK, MLA_HEADS, HEAD_PAD)[:, :, :MLA_QK].transpose(1, 0, 2)
    dw_k = _matmul("dw_uk", ckvn, dk, MLA_KV_RANK, 512, BF16)
    dw_v = _matmul("dw_uv", ckvn, dv, MLA_KV_RANK, 512, BF16)
    p_ukv = jnp.concatenate(
        [dw_k.reshape(MLA_KV_RANK, MLA_HEADS, HEAD_PAD)[:, :, :MLA_NOPE],
         dw_v.reshape(MLA_KV_RANK, MLA_HEADS, MLA_VDIM)], axis=2).transpose(1, 0, 2)
    dw_gate = _matmul("dw_gate", misc, dpre, LANE, 512, BF16)
    p_gate = dw_gate[MISC_ALR:MISC_ALR + GLA_GATE_RANK].reshape(
        GLA_GATE_RANK, N_DEV, GLA_DK // N_DEV).transpose(1, 0, 2)
    token = late_small_stage((p_uq, p_ukv, p_gate))
    grad_x, dg_in = _inproj_bwd((d_gla, d_out, d_lat), w_in_p, x2d, rstd, g_in, dx2, token)
    small = jnp.concatenate([dg_in.reshape(-1), dg_q.reshape(-1), dg_kv.reshape(-1),
                             db_gate.reshape(-1), dg_gla.reshape(-1), dg_final.reshape(-1),
                             loss_p[0, :1]])
    small = jnp.pad(small, (0, SMALL_ROWS * LANE - small.shape[0])).reshape(SMALL_ROWS, LANE)

    (small_all,) = _all_gather("all_gather_small", [small], [everyone])
    lands = _ici_wait("ici_wait", started, lands, small_all)
    big = [_adamw_transposed("adamw_w_in", own_in, lands[0], *sharded[0], 256)]
    big += _adamw_group([p_uq, p_ukv, p_gate, p_pm, p_pg, p_o], list(lands[4:7]) + list(lands[1:4]),
                        *[[s[j] for s in sharded[1:]] for j in range(3)])
    replicated = [(g_in, m_g_in, v_g_in), (g_q, m_g_q, v_g_q), (g_kv, m_g_kv, v_g_kv),
                  (b_gla_gate, m_b_gla_gate, v_b_gla_gate), (g_gla, m_g_gla, v_g_gla),
                  (g_final, m_g_final, v_g_final)]
    spacks = [jnp.pad(jnp.concatenate([s[j].reshape(-1) for s in replicated]),
                      (0, SMALL_ROWS * LANE - sum(SMALL_SIZES))).reshape(1, SMALL_ROWS, LANE)
              for j in range(3)]
    tiny = _adamw_rows("adamw_gains", small_all, spacks[0], spacks[1], spacks[2], SMALL_ROWS)

    outs = {}
    names = ("w_in", "w_uq", "w_ukv", "w_gla_gate", "w_proj_mla", "w_proj_gla", "w_out")
    for j, kind in enumerate(("grad", "delta", "new_m", "new_v")):
        for name, res in zip(names, big):
            outs[kind, name] = res[j]
        flat = tiny[j].reshape(-1)
        off = 0
        for name, size in zip(("g_in", "g_q", "g_kv", "b_gla_gate", "g_gla", "g_final"), SMALL_SIZES):
            shape = (size,) if name == "g_final" else (1, size)
            outs[kind, name] = flat[off:off + size].reshape(shape)
            off += size
    loss = tiny[0].reshape(-1)[sum(SMALL_SIZES)]
    order = ("g_in", "w_in", "g_q", "w_uq", "g_kv", "w_ukv", "w_gla_gate", "b_gla_gate", "g_gla",
             "w_proj_mla", "w_proj_gla", "w_out", "g_final")
    result = [loss, grad_x.reshape(1, t, D_MODEL)]
    for kind in ("grad", "delta", "new_m", "new_v"):
        result += [outs[kind, name] for name in order]
    return tuple(result)
```

```python
import jax
import jax.numpy as jnp
from jax import lax
from jax.experimental import pallas as pl
from jax.experimental.pallas import tpu as pltpu

F32 = jnp.float32
BF16 = jnp.bfloat16
MESH = pl.DeviceIdType.MESH
N_DEV = 8

D_MODEL = 1024
EPS = 1e-6
MLA_HEADS = 8
MLA_NOPE = 64
MLA_ROPE = 32
MLA_VDIM = 64
MLA_Q_RANK = 384
MLA_KV_RANK = 256
MLA_QK = MLA_NOPE + MLA_ROPE
MLA_WIDTH = MLA_HEADS * MLA_VDIM
ROPE_THETA = 10000.0
GLA_HEADS = 4
GLA_DK = 512
GLA_DV = 1024
GLA_HK = 128
GLA_HV = 256
GLA_GATE_RANK = 16
GLA_GATE_NORM = 16.0
GLA_CHUNK = 64
GLA_CHUNKS_PER_STEP = 8
D_IN = 6320

ADAM_LR = 0.001
ADAM_B1 = 0.9
ADAM_B2 = 0.999
ADAM_EPS = 1e-08
ADAM_WD = 0.01
ADAM_STEP = 10

LANE = 128
HEAD_PAD = 128
VMEM_LIMIT = 48 * 1024 * 1024

P_VG, P_QG, P_KG = 0, 1024, 1536
P_ZGLA, P_GMLA, P_GGLA, P_ZMLA = 2048, 3072, 4096, 5120
P_CQ, P_CKV, P_MISC = 5632, 6144, 6400
P_TOTAL = 6528
P_GROUPS = ((0, 2048), (2048, 3584), (5632, 896))
MISC_KR = 64
MISC_ALR = 96
SHARD_COLS = D_IN // N_DEV
SHARD_PAD = 800
P_COMPONENTS = ((0, 384, P_CQ), (384, 256, P_CKV), (640, 32, P_MISC + MISC_KR), (672, 512, P_ZMLA),
                (1184, 512, P_QG), (1696, 512, P_KG), (2208, 1024, P_VG),
                (3232, 16, P_MISC + MISC_ALR), (3248, 1024, P_ZGLA), (4272, 1024, P_GMLA),
                (5296, 1024, P_GGLA))

SMALL_SIZES = (1024, 384, 256, 512, 256, 1024)
SMALL_ROWS = 32


def _segments():
    segs = []
    for g0, n, p0 in P_COMPONENTS:
        g = g0
        while g < g0 + n:
            d = g // SHARD_COLS
            end = min(g0 + n, (d + 1) * SHARD_COLS)
            segs.append((d, g - d * SHARD_COLS, end - g, p0 + g - g0))
            g = end
    return segs


def _group_of(p0):
    return max(i for i, (off, _) in enumerate(P_GROUPS) if off <= p0)


def _rel(p0):
    return p0 - P_GROUPS[_group_of(p0)][0]


def _cparams(sem=None):
    if sem is None:
        return pltpu.CompilerParams(vmem_limit_bytes=VMEM_LIMIT)
    return pltpu.CompilerParams(dimension_semantics=sem, vmem_limit_bytes=VMEM_LIMIT)


def _sigmoid(v):
    return 1.0 / (1.0 + jnp.exp(-v))


def _dot(a, b):
    return jnp.dot(a, b, preferred_element_type=F32)


def _dot_nt(a, b):
    return lax.dot_general(a, b, (((1,), (1,)), ((), ())), preferred_element_type=F32)


def _dot_tn(a, b):
    return lax.dot_general(a, b, (((0,), (0,)), ((), ())), preferred_element_type=F32)


def _dot_exact(a, b):
    return jnp.dot(a, b, preferred_element_type=F32, precision=lax.Precision.HIGHEST)


def _rope_fwd(blk, c, sn, sp):
    return blk * c + pltpu.roll(blk, LANE - 16, 1) * sn + pltpu.roll(blk, 16, 1) * sp


def _rope_bwd(blk, c, sn, sp):
    return blk * c + pltpu.roll(blk * sn, 16, 1) + pltpu.roll(blk * sp, LANE - 16, 1)


def _mesh_pos():
    return lax.axis_index("x"), lax.axis_index("y"), lax.axis_index("c")


def _hbm_specs(n):
    return [pl.BlockSpec(memory_space=pltpu.HBM) for _ in range(n)]


def _dev(d):
    return d >> 2, (d >> 1) & 1, d & 1


def _gather_plan(shards, sources):
    na, most = len(shards), max(len(s) for s in sources)
    out_shape = [jax.ShapeDtypeStruct((len(srcs),) + s.shape, s.dtype)
                 for s, srcs in zip(shards, sources)]
    sems = [pltpu.SemaphoreType.DMA((na, most)) for _ in range(3)]
    sems += [pltpu.SemaphoreType.DMA((na, most, 3))]
    sems += [pltpu.SemaphoreType.DMA((na, most)) for _ in range(3)]
    return out_shape, sems


def _gather_hooks(x_refs, out_refs, sems, sources):
    local_sems, d2d_send, d2d_recv, ici_send, ici_recv, fwd_send, fwd_recv = sems
    x, y, c = _mesh_pos()
    chips = [(1 - x, y), (x, 1 - y), (1 - x, 1 - y)]
    items = []
    for a, srcs in enumerate(sources):
        for i, d in enumerate(srcs):
            dx, dy, dc = _dev(d)
            near = jnp.logical_and(x == dx, y == dy)
            far = jnp.logical_not(near)
            slot = out_refs[a].at[i]

            def remote(src, to, send_sem, recv_sem, slot=slot):
                return pltpu.make_async_remote_copy(
                    src_ref=src, dst_ref=slot, send_sem=send_sem, recv_sem=recv_sem,
                    device_id=to, device_id_type=MESH)

            items.append(dict(
                me=jnp.logical_and(near, c == dc), sibling=jnp.logical_and(near, c != dc),
                relay=jnp.logical_and(far, c == dc), behind=jnp.logical_and(far, c != dc),
                local=pltpu.make_async_copy(x_refs[a], slot, local_sems.at[a, i]),
                to_sibling=remote(x_refs[a], (x, y, 1 - c), d2d_send.at[a, i], d2d_recv.at[a, i]),
                to_chips=[remote(x_refs[a], (*chip, c), ici_send.at[a, i, j], ici_recv.at[a, i])
                          for j, chip in enumerate(chips)],
                forward=remote(slot, (x, y, 1 - c), fwd_send.at[a, i], fwd_recv.at[a, i])))

    def start():
        for it in items:
            @pl.when(it["me"])
            def _(it=it):
                it["local"].start()
                it["to_sibling"].start()
                for cp in it["to_chips"]:
                    cp.start()

    def finish():
        for it in items:
            @pl.when(it["relay"])
            def _(it=it):
                it["to_chips"][0].wait_recv()
                it["forward"].start()
        for it in items:
            pl.when(it["sibling"])(it["to_sibling"].wait_recv)
            pl.when(it["behind"])(it["forward"].wait_recv)
            pl.when(it["relay"])(it["forward"].wait_send)

            @pl.when(it["me"])
            def _(it=it):
                it["local"].wait()
                it["to_sibling"].wait_send()
                for cp in it["to_chips"]:
                    cp.wait_send()

    return start, finish


def _all_gather(name, shards, sources):
    n = len(shards)
    out_shape, sems = _gather_plan(shards, sources)

    def body(*refs):
        start, finish = _gather_hooks(refs[:n], refs[n:2 * n], refs[2 * n:], sources)
        start()
        finish()

    return pl.pallas_call(
        body, name=name,
        out_shape=tuple(out_shape),
        in_specs=_hbm_specs(n), out_specs=tuple(_hbm_specs(n)),
        scratch_shapes=sems,
        compiler_params=_cparams(),
    )(*shards)


PEERS = N_DEV - 1


def _whole(dests, rows):
    return [(i, d, 0, rows) for i, d in enumerate(dests)]


def _ici_copies(p_ref, land_ref, send_sems, recv_sems, pieces):
    x, y, c = _mesh_pos()
    sends, arrivals = [], []

    def rows_of(ref, j, r0, r1):
        return ref.at[j] if (r0, r1) == (0, ref.shape[1]) else ref.at[j, pl.ds(r0, r1 - r0)]

    for p, (i, d, r0, r1) in enumerate(pieces):
        dx, dy, dc = _dev(d)
        k = (4 * (x != dx).astype(jnp.int32) + 2 * (y != dy).astype(jnp.int32)
             + (c != dc).astype(jnp.int32))
        slot = jnp.maximum(k - 1, 0)
        sends.append((k > 0, pltpu.make_async_remote_copy(
            src_ref=rows_of(p_ref, i, r0, r1), dst_ref=rows_of(land_ref, slot, r0, r1),
            send_sem=send_sems.at[p], recv_sem=recv_sems.at[p * PEERS + slot],
            device_id=(dx, dy, dc), device_id_type=MESH)))
        arrivals.append((k == 0, [pltpu.make_async_remote_copy(
            src_ref=rows_of(p_ref, i, r0, r1), dst_ref=rows_of(land_ref, r, r0, r1),
            send_sem=send_sems.at[p], recv_sem=recv_sems.at[p * PEERS + r],
            device_id=(dx, dy, dc), device_id_type=MESH) for r in range(PEERS)]))
    return sends, arrivals


def _ici_start(name, hs, lands, dests):
    na = len(hs)

    def body(*refs):
        h_refs, land_refs, sems = refs[:na], refs[na:2 * na], refs[2 * na:4 * na]
        token = refs[-1]
        for a in range(na):
            sends, _ = _ici_copies(h_refs[a], land_refs[a], sems[2 * a], sems[2 * a + 1], dests[a])
            for go, cp in sends:
                pl.when(go)(cp.start)
        token[...] = jnp.zeros_like(token)

    hbm, sem = pl.BlockSpec(memory_space=pltpu.HBM), pl.BlockSpec(memory_space=pltpu.SEMAPHORE)
    sem_shapes = []
    for a in range(na):
        sem_shapes += [pltpu.SemaphoreType.DMA((len(dests[a]),)),
                       pltpu.SemaphoreType.DMA((len(dests[a]) * PEERS,))]
    res = pl.pallas_call(
        body, name=name,
        out_shape=tuple(sem_shapes) + tuple(pltpu.HBM(v.shape, v.dtype) for v in list(hs) + list(lands))
        + (jax.ShapeDtypeStruct((8, LANE), F32),),
        in_specs=(hbm,) * (2 * na),
        out_specs=(sem,) * (2 * na) + (hbm,) * (2 * na) + (pl.BlockSpec(memory_space=pltpu.VMEM),),
        input_output_aliases={i: 2 * na + i for i in range(2 * na)},
        compiler_params=pltpu.CompilerParams(
            has_side_effects=pltpu.SideEffectType.DATAFLOW_SIDE_EFFECTING,
            vmem_limit_bytes=VMEM_LIMIT),
    )(*[pltpu.with_memory_space_constraint(v, pltpu.HBM) for v in list(hs) + list(lands)])
    sems = [(res[2 * a], res[2 * a + 1]) for a in range(na)]
    return sems, res[2 * na:3 * na], res[3 * na:4 * na], res[-1]


def _ici_wait(name, started, lands, after):
    k, nl = len(started), len(lands)

    def body(*refs):
        land_refs = refs[3 * k:3 * k + nl]
        for s in range(k):
            h_ref, send_sems, recv_sems = refs[3 * s:3 * s + 3]
            sends, arrivals = _ici_copies(h_ref, land_refs[started[s][3]], send_sems, recv_sems,
                                          started[s][4])
            for go, cp in sends:
                pl.when(go)(cp.wait_send)
            for here, cps in arrivals:
                for cp in cps:
                    pl.when(here)(cp.wait_recv)

    hbm, sem = pl.BlockSpec(memory_space=pltpu.HBM), pl.BlockSpec(memory_space=pltpu.SEMAPHORE)
    operands, specs = [], []
    for send_sems, recv_sems, h, _, _ in started:
        operands += [h, send_sems, recv_sems]
        specs += [hbm, sem, sem]
    return pl.pallas_call(
        body, name=name,
        out_shape=tuple(pltpu.HBM(v.shape, v.dtype) for v in lands),
        in_specs=tuple(specs) + (hbm,) * nl + (pl.BlockSpec(memory_space=pl.ANY),),
        out_specs=(hbm,) * nl,
        input_output_aliases={3 * k + i: i for i in range(nl)},
        compiler_params=pltpu.CompilerParams(
            has_side_effects=pltpu.SideEffectType.DATAFLOW_SIDE_EFFECTING,
            vmem_limit_bytes=VMEM_LIMIT),
    )(*operands, *lands, after)


def _weights_to_p(name, gathered, where, group):
    tl = 512
    off, width = P_GROUPS[group]
    segs = sorted([s for s in _segments() if _group_of(s[3]) == group], key=lambda s: s[3])
    used = sorted({where[s[0]][0] for s in segs})

    def body(*refs):
        g_refs, o_ref = dict(zip(used, refs[:-1])), refs[-1]
        pieces, pos = [], off
        for d, c0, n, p0 in segs:
            if p0 > pos:
                pieces.append(jnp.zeros((p0 - pos, tl), F32))
            k, slot = where[d]
            pieces.append(g_refs[k][slot, c0:c0 + n, :].astype(F32))
            pos = p0 + n
        if off + width > pos:
            pieces.append(jnp.zeros((off + width - pos, tl), F32))
        o_ref[...] = jnp.concatenate(pieces, axis=0).astype(BF16)

    return pl.pallas_call(
        body, name=name,
        grid=(D_MODEL // tl,),
        in_specs=[pl.BlockSpec((gathered[k].shape[0], SHARD_COLS, tl), lambda i: (0, 0, i))
                  for k in used],
        out_specs=pl.BlockSpec((width, tl), lambda i: (0, i)),
        out_shape=jax.ShapeDtypeStruct((width, D_MODEL), BF16),
        compiler_params=_cparams(("arbitrary",)),
    )(*[gathered[k] for k in used])


def _shard_groups(d):
    return sorted({_group_of(s[3]) for s in _segments() if s[0] == d})


def _grads_to_shards(name, groups, dests, own_prev):
    tl = 512
    segs = _segments()
    used = sorted(groups)

    def body(*refs):
        g_refs, prev_ref, o_ref, own_ref = dict(zip(used, refs[:-3])), refs[-3], refs[-2], refs[-1]
        x, y, c = _mesh_pos()
        own = prev_ref[...].astype(F32)
        row = lax.broadcasted_iota(jnp.int32, (SHARD_PAD, tl), 0)
        for i, (d, ranges) in enumerate(dests):
            pieces, pos, asked = [], 0, None
            for r0, r1 in sorted(ranges):
                if r0 > pos:
                    pieces.append(jnp.zeros((r0 - pos, tl), F32))
                for _, c0, n, p0 in sorted([s for s in segs if s[0] == d], key=lambda s: s[1]):
                    a, b = max(c0, r0), min(c0 + n, r1)
                    if a < b:
                        gi = _group_of(p0)
                        lo = p0 - P_GROUPS[gi][0] + a - c0
                        pieces.append(g_refs[gi][lo:lo + b - a, :].astype(F32))
                if r1 > SHARD_COLS:
                    pieces.append(jnp.zeros((r1 - max(r0, SHARD_COLS), tl), F32))
                pos = r1
                inside = jnp.logical_and(row >= r0, row < r1)
                asked = inside if asked is None else jnp.logical_or(asked, inside)
            if pos < SHARD_PAD:
                pieces.append(jnp.zeros((SHARD_PAD - pos, tl), F32))
            shard = jnp.concatenate(pieces, axis=0)
            o_ref[i] = shard.astype(BF16)
            own = jnp.where(jnp.logical_and(4 * x + 2 * y + c == d, asked), shard, own)
        own_ref[...] = own.astype(BF16)

    blk = pl.BlockSpec((SHARD_PAD, tl), lambda i: (0, i))
    return pl.pallas_call(
        body, name=name,
        grid=(D_MODEL // tl,),
        in_specs=[pl.BlockSpec((P_GROUPS[g][1], tl), lambda i: (0, i)) for g in used] + [blk],
        out_specs=(pl.BlockSpec((len(dests), SHARD_PAD, tl), lambda i: (0, 0, i)), blk),
        out_shape=(jax.ShapeDtypeStruct((len(dests), SHARD_PAD, D_MODEL), BF16),
                   jax.ShapeDtypeStruct((SHARD_PAD, D_MODEL), BF16)),
        input_output_aliases={len(used): 1},
        compiler_params=_cparams(("arbitrary",)),
    )(*[groups[g] for g in used], own_prev)


def _inproj(x, g_in, w_pt):
    t = x.shape[0]
    tm = min(512, t)
    width = w_pt.shape[0]

    def body(x_ref, g_ref, w_ref, proj_ref, h_ref, r_ref):
        xf = x_ref[...]
        r = lax.rsqrt(jnp.mean(xf * xf, axis=-1, keepdims=True) + EPS)
        h = ((xf * r) * g_ref[...]).astype(BF16)
        proj_ref[...] = _dot_nt(h, w_ref[...])
        h_ref[...] = h
        r_ref[...] = r

    row = lambda w: pl.BlockSpec((tm, w), lambda i: (i, 0))
    return pl.pallas_call(
        body, name="inproj_latents",
        grid=(t // tm,),
        in_specs=[row(D_MODEL), pl.BlockSpec((1, D_MODEL), lambda i: (0, 0)),
                  pl.BlockSpec((width, D_MODEL), lambda i: (0, 0))],
        out_specs=(row(width), row(D_MODEL), row(1)),
        out_shape=(jax.ShapeDtypeStruct((t, width), F32),
                   jax.ShapeDtypeStruct((t, D_MODEL), BF16),
                   jax.ShapeDtypeStruct((t, 1), F32)),
        compiler_params=_cparams(("arbitrary",)),
    )(x, g_in, w_pt)


def _proj(name, h, w_pt):
    t = h.shape[0]
    tm = min(512, t)
    width = w_pt.shape[0]

    def body(h_ref, w_ref, o_ref):
        o_ref[...] = _dot_nt(h_ref[...], w_ref[...])

    return pl.pallas_call(
        body, name=name,
        grid=(t // tm,),
        in_specs=[pl.BlockSpec((tm, D_MODEL), lambda i: (i, 0)),
                  pl.BlockSpec((width, D_MODEL), lambda i: (0, 0))],
        out_specs=pl.BlockSpec((tm, width), lambda i: (i, 0)),
        out_shape=jax.ShapeDtypeStruct((t, width), F32),
        compiler_params=_cparams(("arbitrary",)),
    )(h, w_pt)


def _mla_prep(proj, g_q, g_kv, w_uq_p, w_k_p, w_v, w_gate_p, b_gate, rc, rsn, rsp):
    t = proj.shape[0]
    tm = min(256, t)
    hq = MLA_HEADS * HEAD_PAD

    def body(cq_ref, ckv_ref, misc_ref, gq_ref, gkv_ref, wuq_ref, wk_ref, wv_ref, wg_ref, bg_ref,
             c_ref, sn_ref, sp_ref,
             q_ref, k_ref, v_ref, la_ref, pre_ref, cqn_ref, ckvn_ref, rq_ref, rkv_ref, mb_ref):
        c, sn, sp = c_ref[...], sn_ref[...], sp_ref[...]
        cq = cq_ref[:, :MLA_Q_RANK]
        rq = lax.rsqrt(jnp.mean(cq * cq, axis=-1, keepdims=True) + EPS)
        cqn = ((cq * rq) * gq_ref[...]).astype(BF16)
        cqn_ref[...] = cqn
        rq_ref[...] = rq
        qpre = _dot(cqn, wuq_ref[...])
        ckv = ckv_ref[...]
        rkv = lax.rsqrt(jnp.mean(ckv * ckv, axis=-1, keepdims=True) + EPS)
        ckvn = ((ckv * rkv) * gkv_ref[...]).astype(BF16)
        ckvn_ref[...] = ckvn
        rkv_ref[...] = rkv
        kn = _dot(ckvn, wk_ref[...])
        v_ref[...] = _dot(ckvn, wv_ref[...]).astype(BF16)
        misc = misc_ref[...]
        krope = _rope_fwd(misc, c, sn, sp)
        for h in range(MLA_HEADS):
            sl = slice(h * HEAD_PAD, (h + 1) * HEAD_PAD)
            q_ref[:, sl] = _rope_fwd(qpre[:, sl], c, sn, sp).astype(BF16)
            k_ref[:, sl] = (kn[:, sl] + krope).astype(BF16)
        mb_ref[...] = misc.astype(BF16)
        pre = _dot(mb_ref[...], wg_ref[...]) + bg_ref[...]
        pre_ref[...] = pre
        log_a = (jnp.minimum(pre, 0.0) - jnp.log(1.0 + jnp.exp(-jnp.abs(pre)))) / GLA_GATE_NORM
        la_ref[...] = _dot_exact(_chunk_tri(tm, True), log_a)

    row = lambda w: pl.BlockSpec((tm, w), lambda i: (i, 0))
    full = lambda a: pl.BlockSpec(a.shape, lambda i: (0, 0))
    return pl.pallas_call(
        body, name="mla_prep",
        grid=(t // tm,),
        in_specs=[pl.BlockSpec((tm, 512), lambda i: (i, _rel(P_CQ) // 512)),
                  pl.BlockSpec((tm, MLA_KV_RANK), lambda i: (i, _rel(P_CKV) // MLA_KV_RANK)),
                  pl.BlockSpec((tm, LANE), lambda i: (i, _rel(P_MISC) // LANE)),
                  full(g_q), full(g_kv), full(w_uq_p), full(w_k_p), full(w_v), full(w_gate_p),
                  full(b_gate), row(LANE), row(LANE), row(LANE)],
        out_specs=(row(hq), row(hq), row(MLA_WIDTH), row(GLA_DK), row(GLA_DK),
                   row(MLA_Q_RANK), row(MLA_KV_RANK), row(1), row(1), row(LANE)),
        out_shape=(jax.ShapeDtypeStruct((t, hq), BF16), jax.ShapeDtypeStruct((t, hq), BF16),
                   jax.ShapeDtypeStruct((t, MLA_WIDTH), BF16),
                   jax.ShapeDtypeStruct((t, GLA_DK), F32), jax.ShapeDtypeStruct((t, GLA_DK), F32),
                   jax.ShapeDtypeStruct((t, MLA_Q_RANK), BF16),
                   jax.ShapeDtypeStruct((t, MLA_KV_RANK), BF16),
                   jax.ShapeDtypeStruct((t, 1), F32), jax.ShapeDtypeStruct((t, 1), F32),
                   jax.ShapeDtypeStruct((t, LANE), BF16)),
        compiler_params=_cparams(("arbitrary",)),
    )(proj, proj, proj, g_q, g_kv, w_uq_p, w_k_p, w_v, w_gate_p, b_gate, rc, rsn, rsp)


def _attn_masks(tq, i):
    keys = (i + 1) * tq
    rows = i * tq + lax.broadcasted_iota(jnp.int32, (tq, keys), 0)
    cols = lax.broadcasted_iota(jnp.int32, (tq, keys), 1)
    lane = lax.broadcasted_iota(jnp.int32, (tq, LANE), 1)
    return cols <= rows, lane < MLA_VDIM


def _for_each_query_tile(n_tiles, fn):
    for i in range(n_tiles):
        pl.when(pl.program_id(1) == i)(lambda i=i: fn(i))


def _mla_attn_fwd(q, k, v, shards, sources):
    t = q.shape[0]
    tq = min(256, t)
    scale = MLA_QK ** -0.5
    ns = len(shards)
    g_shapes, g_sems = _gather_plan(shards, sources)
    grid = (MLA_HEADS // 2, t // tq)

    def body(q_ref, k_ref, v_ref, *rest):
        o_ref, lse_ref = rest[ns:ns + 2]
        start, finish = _gather_hooks(rest[:ns], rest[ns + 2:2 * ns + 2], rest[2 * ns + 2:], sources)
        step = pl.program_id(0) * grid[1] + pl.program_id(1)
        pl.when(step == 0)(start)

        def tile(i):
            keys = (i + 1) * tq
            causal, low = _attn_masks(tq, i)
            vp = v_ref[0:keys, :]
            acc = jnp.zeros((tq, LANE), F32)
            for hh in range(2):
                sl = slice(hh * HEAD_PAD, (hh + 1) * HEAD_PAD)
                s = _dot_nt(q_ref[:, sl], k_ref[0:keys, sl]) * scale
                s = jnp.where(causal, s, -jnp.inf)
                m = jnp.max(s, axis=-1, keepdims=True)
                e = jnp.exp(s - m)
                l = jnp.sum(e, axis=-1, keepdims=True)
                o = _dot(e.astype(BF16), vp) / l
                acc = jnp.where(low if hh == 0 else jnp.logical_not(low), o, acc)
                lse_ref[hh] = m + jnp.log(l)
            o_ref[...] = acc

        _for_each_query_tile(t // tq, tile)
        pl.when(step == grid[0] * grid[1] - 1)(finish)

    res = pl.pallas_call(
        body, name="mla_attn_fwd",
        grid=grid,
        in_specs=[pl.BlockSpec((tq, 2 * HEAD_PAD), lambda p, i: (i, p)),
                  pl.BlockSpec((t, 2 * HEAD_PAD), lambda p, i: (0, p)),
                  pl.BlockSpec((t, LANE), lambda p, i: (0, p))] + _hbm_specs(ns),
        out_specs=(pl.BlockSpec((tq, LANE), lambda p, i: (i, p)),
                   pl.BlockSpec((2, tq, 1), lambda p, i: (p, i, 0))) + tuple(_hbm_specs(ns)),
        out_shape=(jax.ShapeDtypeStruct((t, MLA_WIDTH), F32),
                   jax.ShapeDtypeStruct((MLA_HEADS, t, 1), F32)) + tuple(g_shapes),
        scratch_shapes=g_sems,
        compiler_params=_cparams(("arbitrary", "arbitrary")),
    )(q, k, v, *shards)
    return res[0], res[1], res[2:]


def _mla_attn_bwd(q, k, v, o, do, lse, after):
    t = q.shape[0]
    tq = min(256, t)
    scale = MLA_QK ** -0.5

    def body(q_ref, k_ref, v_ref, o_ref, do_ref, lse_ref, after_ref, dq_ref, dk_ref, dv_ref):
        del after_ref

        @pl.when(pl.program_id(1) == 0)
        def _():
            dk_ref[...] = jnp.zeros_like(dk_ref)
            dv_ref[...] = jnp.zeros_like(dv_ref)

        def tile(i):
            keys = (i + 1) * tq
            causal, low = _attn_masks(tq, i)
            vp = v_ref[0:keys, :]
            do_all = do_ref[...]
            o_all = o_ref[...]
            dv_acc = jnp.zeros((keys, LANE), F32)
            for hh in range(2):
                sl = slice(hh * HEAD_PAD, (hh + 1) * HEAD_PAD)
                do_h = jnp.where(low if hh == 0 else jnp.logical_not(low), do_all, 0.0)
                dsum = jnp.sum(do_h * o_all, axis=-1, keepdims=True)
                qh = q_ref[:, sl]
                kh = k_ref[0:keys, sl]
                s = _dot_nt(qh, kh) * scale
                p = jnp.where(causal, jnp.exp(s - lse_ref[hh]), 0.0)
                do_b = do_h.astype(BF16)
                dp = _dot_nt(do_b, vp)
                ds = (p * (dp - dsum) * scale).astype(BF16)
                dq_ref[:, sl] = _dot(ds, kh).astype(BF16)
                dk_ref[0:keys, sl] += _dot_tn(ds, qh)
                dv_acc = dv_acc + _dot_tn(p.astype(BF16), do_b)
            dv_ref[0:keys, :] += dv_acc

        _for_each_query_tile(t // tq, tile)

    return pl.pallas_call(
        body, name="mla_attn_bwd",
        grid=(MLA_HEADS // 2, t // tq),
        in_specs=[pl.BlockSpec((tq, 2 * HEAD_PAD), lambda p, i: (i, p)),
                  pl.BlockSpec((t, 2 * HEAD_PAD), lambda p, i: (0, p)),
                  pl.BlockSpec((t, LANE), lambda p, i: (0, p)),
                  pl.BlockSpec((tq, LANE), lambda p, i: (i, p)),
                  pl.BlockSpec((tq, LANE), lambda p, i: (i, p)),
                  pl.BlockSpec((2, tq, 1), lambda p, i: (p, i, 0)),
                  pl.BlockSpec(memory_space=pl.ANY)],
        out_specs=(pl.BlockSpec((tq, 2 * HEAD_PAD), lambda p, i: (i, p)),
                   pl.BlockSpec((t, 2 * HEAD_PAD), lambda p, i: (0, p)),
                   pl.BlockSpec((t, LANE), lambda p, i: (0, p))),
        out_shape=(jax.ShapeDtypeStruct((t, MLA_HEADS * HEAD_PAD), BF16),
                   jax.ShapeDtypeStruct((t, MLA_HEADS * HEAD_PAD), F32),
                   jax.ShapeDtypeStruct((t, MLA_WIDTH), F32)),
        compiler_params=_cparams(("arbitrary", "arbitrary")),
    )(q, k, v, o, do, lse, after)


def _chunk_tri(n, lower):
    r = lax.broadcasted_iota(jnp.int32, (n, n), 0)
    c = lax.broadcasted_iota(jnp.int32, (n, n), 1)
    same = (r // GLA_CHUNK) == (c // GLA_CHUNK)
    return jnp.where(jnp.logical_and(same, r >= c if lower else r <= c), 1.0, 0.0).astype(F32)


def _gla_chunk_terms(q_ref, k_ref, b_ref, h, rows):
    sl = slice(h * GLA_HK, (h + 1) * GLA_HK)
    b = b_ref[rows, sl]
    bl = b[GLA_CHUNK - 1:GLA_CHUNK, :]
    kc = k_ref[rows, sl]
    q_in = (q_ref[rows, sl] * (GLA_HK ** -0.5)) * jnp.exp(b)
    k_in = kc * jnp.exp(-b)
    k_st = kc * jnp.exp(bl - b)
    return b, bl, q_in, k_in, k_st


def _tri(c, lower):
    r = lax.broadcasted_iota(jnp.int32, (c, c), 0)
    cc = lax.broadcasted_iota(jnp.int32, (c, c), 1)
    return jnp.where(r >= cc if lower else r <= cc, 1.0, 0.0).astype(F32)


def _gla_fwd(proj, log_a):
    t = proj.shape[0]
    per = GLA_CHUNKS_PER_STEP
    n = t // GLA_CHUNK
    c = GLA_CHUNK * per

    def body(q_ref, k_ref, v_ref, la_ref, o_ref, sp_ref, st_ref):
        @pl.when(pl.program_id(0) == 0)
        def _():
            st_ref[...] = jnp.zeros_like(st_ref)

        tri = _tri(GLA_CHUNK, True)
        for s, h in [(s, h) for s in range(per) for h in range(GLA_HEADS)]:
            rows = slice(s * GLA_CHUNK, (s + 1) * GLA_CHUNK)
            _, bl, q_in, k_in, k_st = _gla_chunk_terms(q_ref, k_ref, la_ref, h, rows)
            vs = slice(h * GLA_HV, (h + 1) * GLA_HV)
            vv = v_ref[rows, vs].astype(BF16)
            qb = q_in.astype(BF16)
            attn = _dot_nt(qb, k_in.astype(BF16)) * tri
            st = st_ref[h]
            sp_ref[s, h] = st
            o_ref[rows, vs] = _dot(attn.astype(BF16), vv) + _dot_nt(qb, st.astype(BF16))
            st_ref[h] = st * jnp.exp(bl) + _dot_tn(vv, k_st.astype(BF16))

    return pl.pallas_call(
        body, name="gla_fwd",
        grid=(n // per,),
        in_specs=[pl.BlockSpec((c, GLA_DK), lambda i: (i, P_QG // GLA_DK)),
                  pl.BlockSpec((c, GLA_DK), lambda i: (i, P_KG // GLA_DK)),
                  pl.BlockSpec((c, GLA_DV), lambda i: (i, P_VG // GLA_DV)),
                  pl.BlockSpec((c, GLA_DK), lambda i: (i, 0))],
        out_specs=(pl.BlockSpec((c, GLA_DV), lambda i: (i, 0)),
                   pl.BlockSpec((per, GLA_HEADS, GLA_HV, GLA_HK), lambda i: (i, 0, 0, 0))),
        out_shape=(jax.ShapeDtypeStruct((t, GLA_DV), F32),
                   jax.ShapeDtypeStruct((n, GLA_HEADS, GLA_HV, GLA_HK), F32)),
        scratch_shapes=[pltpu.VMEM((GLA_HEADS, GLA_HV, GLA_HK), F32)],
        compiler_params=_cparams(("arbitrary",)),
    )(proj, proj, proj, log_a)


def _gla_bwd(proj, log_a, do, states, after):
    t = proj.shape[0]
    per = GLA_CHUNKS_PER_STEP
    c = GLA_CHUNK * per
    n = t // c

    def body(q_ref, k_ref, v_ref, la_ref, do_ref, sp_ref, after_ref, dg_ref, dla_ref, ds_ref):
        del after_ref

        @pl.when(pl.program_id(0) == 0)
        def _():
            ds_ref[...] = jnp.zeros_like(ds_ref)

        tri = _tri(GLA_CHUNK, True)
        last = lax.broadcasted_iota(jnp.int32, (GLA_CHUNK, GLA_HK), 0) == GLA_CHUNK - 1
        for s, h in [(s, h) for s in reversed(range(per)) for h in range(GLA_HEADS)]:
            rows = slice(s * GLA_CHUNK, (s + 1) * GLA_CHUNK)
            b, bl, q_in, k_in, k_st = _gla_chunk_terms(q_ref, k_ref, la_ref, h, rows)
            ks_ = slice(h * GLA_HK, (h + 1) * GLA_HK)
            vs = slice(h * GLA_HV, (h + 1) * GLA_HV)
            vv = v_ref[rows, vs].astype(BF16)
            do_h = do_ref[rows, vs]
            qb, kb, ksb = q_in.astype(BF16), k_in.astype(BF16), k_st.astype(BF16)
            attn = (_dot_nt(qb, kb) * tri).astype(BF16)
            st = sp_ref[s, h]
            dst = ds_ref[h]
            dstb = dst.astype(BF16)
            dattn = (_dot_nt(do_h, vv) * tri).astype(BF16)
            dg_ref[rows, P_VG + h * GLA_HV:P_VG + (h + 1) * GLA_HV] = (
                _dot_tn(attn, do_h) + _dot_nt(ksb, dstb)).astype(BF16)
            dq_in = _dot(dattn, kb) + _dot(do_h, st.astype(BF16))
            dk_in = _dot_tn(dattn, qb)
            dk_st = _dot(vv, dstb)
            ebl = jnp.exp(bl)
            d_ebl = jnp.sum(st * dst, axis=0, keepdims=True)
            ds_ref[h] = _dot_tn(do_h, qb) + dst * ebl
            dg_ref[rows, P_QG + h * GLA_HK:P_QG + (h + 1) * GLA_HK] = (
                dq_in * (GLA_HK ** -0.5) * jnp.exp(b)).astype(BF16)
            dg_ref[rows, P_KG + h * GLA_HK:P_KG + (h + 1) * GLA_HK] = (
                dk_in * jnp.exp(-b) + dk_st * jnp.exp(bl - b)).astype(BF16)
            db = dq_in * q_in - dk_in * k_in - dk_st * k_st
            dbl = jnp.sum(dk_st * k_st, axis=0, keepdims=True) + d_ebl * ebl
            dla_ref[rows, ks_] = db + jnp.where(last, dbl, 0.0)

    rev = lambda i: n - 1 - i
    gw = P_GROUPS[0][1]
    return pl.pallas_call(
        body, name="gla_bwd",
        grid=(n,),
        in_specs=[pl.BlockSpec((c, GLA_DK), lambda i: (rev(i), P_QG // GLA_DK)),
                  pl.BlockSpec((c, GLA_DK), lambda i: (rev(i), P_KG // GLA_DK)),
                  pl.BlockSpec((c, GLA_DV), lambda i: (rev(i), P_VG // GLA_DV)),
                  pl.BlockSpec((c, GLA_DK), lambda i: (rev(i), 0)),
                  pl.BlockSpec((c, GLA_DV), lambda i: (rev(i), 0)),
                  pl.BlockSpec((per, GLA_HEADS, GLA_HV, GLA_HK), lambda i: (rev(i), 0, 0, 0)),
                  pl.BlockSpec(memory_space=pl.ANY)],
        out_specs=(pl.BlockSpec((c, gw), lambda i: (rev(i), 0)),
                   pl.BlockSpec((c, GLA_DK), lambda i: (rev(i), 0))),
        out_shape=(jax.ShapeDtypeStruct((t, gw), BF16), jax.ShapeDtypeStruct((t, GLA_DK), F32)),
        scratch_shapes=[pltpu.VMEM((GLA_HEADS, GLA_HV, GLA_HK), F32)],
        compiler_params=_cparams(("arbitrary",)),
    )(proj, proj, proj, log_a, do, states, after)


def _post(o_mla, proj, o_gla, x, target, g_gla, g_final, w_pm, w_pg, w_o):
    t = x.shape[0]
    tm = min(256, t)
    g0, gw = P_GROUPS[1]

    def body(om_ref, zg_ref, gm_ref, gg_ref, zm_ref, og_ref, x_ref, tg_ref, ggla_ref, gf_ref,
             wpm_ref, wpg_ref, wo_ref,
             dx2_ref, dom_ref, dog_ref, dg_ref,
             mg_ref, um_ref, ug_ref, dym_ref, dyg_ref, loss_ref, dgf_ref, dggla_ref):
        @pl.when(pl.program_id(0) == 0)
        def _():
            loss_ref[...] = jnp.zeros_like(loss_ref)
            dgf_ref[...] = jnp.zeros_like(dgf_ref)
            dggla_ref[...] = jnp.zeros_like(dggla_ref)

        om = om_ref[...]
        zm = zm_ref[...]
        sm = _sigmoid(zm)
        silu_m = zm * sm
        um = (om * silu_m).astype(BF16)
        um_ref[...] = um
        ym = _dot(um, wpm_ref[...])

        ggla = ggla_ref[...]
        zg = zg_ref[...]
        sg = _sigmoid(zg)
        silu_g = zg * sg
        xhat, rstd, on = [], [], []
        for h in range(GLA_HEADS):
            blk = og_ref[:, h * GLA_HV:(h + 1) * GLA_HV]
            r = lax.rsqrt(jnp.mean(blk * blk, axis=-1, keepdims=True) + EPS)
            xhat.append(blk * r)
            rstd.append(r)
            on.append(xhat[h] * ggla)
        on = jnp.concatenate(on, axis=-1)
        ug = (on * silu_g).astype(BF16)
        ug_ref[...] = ug
        yg = _dot(ug, wpg_ref[...])

        sgm = _sigmoid(gm_ref[...])
        sgg = _sigmoid(gg_ref[...])
        merged = (sgm * ym + sgg * yg).astype(BF16)
        mg_ref[...] = merged
        x2 = x_ref[...] + _dot(merged, wo_ref[...])
        gf = gf_ref[...]
        rf = lax.rsqrt(jnp.mean(x2 * x2, axis=-1, keepdims=True) + EPS)
        xh = x2 * rf
        err = xh * gf - tg_ref[...]
        loss_ref[...] += 0.5 * jnp.sum(jnp.mean(err * err, axis=-1, keepdims=True))

        dy = err * (1.0 / D_MODEL)
        dgf_ref[...] += jnp.sum(dy * xh, axis=0, keepdims=True)
        dxh = dy * gf
        dx2 = rf * (dxh - xh * jnp.mean(dxh * xh, axis=-1, keepdims=True))
        dx2_ref[...] = dx2
        dmerged = _dot_nt(dx2.astype(BF16), wo_ref[...])
        dym = (dmerged * sgm).astype(BF16)
        dyg = (dmerged * sgg).astype(BF16)
        dym_ref[...] = dym
        dyg_ref[...] = dyg
        dg_ref[:, P_GMLA - g0:P_GMLA - g0 + D_MODEL] = (dmerged * ym * sgm * (1.0 - sgm)).astype(BF16)
        dg_ref[:, P_GGLA - g0:P_GGLA - g0 + D_MODEL] = (dmerged * yg * sgg * (1.0 - sgg)).astype(BF16)
        dum = _dot_nt(dym, wpm_ref[...])
        dom_ref[...] = dum * silu_m
        dg_ref[:, P_ZMLA - g0:P_ZMLA - g0 + MLA_WIDTH] = (
            dum * om * (sm * (1.0 + zm * (1.0 - sm)))).astype(BF16)
        dug = _dot_nt(dyg, wpg_ref[...])
        dg_ref[:, P_ZGLA - g0:P_ZGLA - g0 + GLA_DV] = (
            dug * on * (sg * (1.0 + zg * (1.0 - sg)))).astype(BF16)
        don = dug * silu_g
        dggla = jnp.zeros((1, GLA_HV), F32)
        for h in range(GLA_HEADS):
            hs = slice(h * GLA_HV, (h + 1) * GLA_HV)
            don_h = don[:, hs]
            dggla = dggla + jnp.sum(don_h * xhat[h], axis=0, keepdims=True)
            dxh_h = don_h * ggla
            dog_ref[:, hs] = (rstd[h] * (dxh_h - xhat[h] * jnp.mean(dxh_h * xhat[h], axis=-1,
                                                                     keepdims=True))).astype(BF16)
        dggla_ref[...] += dggla

    row = lambda w: pl.BlockSpec((tm, w), lambda i: (i, 0))
    pcol = lambda w, off: pl.BlockSpec((tm, w), lambda i: (i, _rel(off) // w))
    full = lambda a: pl.BlockSpec(a.shape, lambda i: (0, 0))
    sds = jax.ShapeDtypeStruct
    return pl.pallas_call(
        body, name="post_fwd_bwd",
        grid=(t // tm,),
        in_specs=[row(MLA_WIDTH), pcol(GLA_DV, P_ZGLA), pcol(D_MODEL, P_GMLA), pcol(D_MODEL, P_GGLA),
                  pcol(MLA_WIDTH, P_ZMLA), row(GLA_DV), row(D_MODEL), row(D_MODEL),
                  full(g_gla), full(g_final), full(w_pm), full(w_pg), full(w_o)],
        out_specs=(row(D_MODEL), row(MLA_WIDTH), row(GLA_DV), row(gw),
                   row(D_MODEL), row(MLA_WIDTH), row(GLA_DV), row(D_MODEL), row(D_MODEL),
                   pl.BlockSpec((1, LANE), lambda i: (0, 0)),
                   pl.BlockSpec((1, D_MODEL), lambda i: (0, 0)),
                   pl.BlockSpec((1, GLA_HV), lambda i: (0, 0))),
        out_shape=(sds((t, D_MODEL), F32), sds((t, MLA_WIDTH), F32), sds((t, GLA_DV), BF16),
                   sds((t, gw), BF16),
                   sds((t, D_MODEL), BF16), sds((t, MLA_WIDTH), BF16), sds((t, GLA_DV), BF16),
                   sds((t, D_MODEL), BF16), sds((t, D_MODEL), BF16),
                   sds((1, LANE), F32), sds((1, D_MODEL), F32), sds((1, GLA_HV), F32)),
        compiler_params=_cparams(("arbitrary",)),
    )(o_mla, proj, proj, proj, proj, o_gla, x, target, g_gla, g_final, w_pm, w_pg, w_o)


def _mla_prep_bwd(dq, dk, dv, dla, pre, proj, rq, rkv, g_q, g_kv, w_uq_p, w_k_p, w_v, w_gate_p,
                  rc, rsn, rsp):
    t = proj.shape[0]
    tm = min(256, t)
    gw = P_GROUPS[2][1]

    def body(dq_ref, dk_ref, dv_ref, dla_ref, pre_ref, cq_ref, ckv_ref, rq_ref, rkv_ref,
             gq_ref, gkv_ref, wuq_ref, wk_ref, wv_ref, wg_ref, c_ref, sn_ref, sp_ref,
             dg_ref, dqpre_ref, dpre_ref, dgq_ref, dgkv_ref, dbg_ref):
        @pl.when(pl.program_id(0) == 0)
        def _():
            dgq_ref[...] = jnp.zeros_like(dgq_ref)
            dgkv_ref[...] = jnp.zeros_like(dgkv_ref)
            dbg_ref[...] = jnp.zeros_like(dbg_ref)

        c, sn, sp = c_ref[...], sn_ref[...], sp_ref[...]
        dkr = jnp.zeros((tm, LANE), F32)
        for h in range(MLA_HEADS):
            sl = slice(h * HEAD_PAD, (h + 1) * HEAD_PAD)
            dqpre_ref[:, sl] = _rope_bwd(dq_ref[:, sl].astype(F32), c, sn, sp).astype(BF16)
            dkr = dkr + dk_ref[:, sl]
        dcqn = _dot_nt(dqpre_ref[...], wuq_ref[...])
        rq = rq_ref[...]
        xh = cq_ref[:, :MLA_Q_RANK] * rq
        dgq_ref[...] += jnp.sum(dcqn * xh, axis=0, keepdims=True)
        dxh = dcqn * gq_ref[...]
        dcq = rq * (dxh - xh * jnp.mean(dxh * xh, axis=-1, keepdims=True))
        dg_ref[:, :MLA_Q_RANK] = dcq.astype(BF16)
        dg_ref[:, MLA_Q_RANK:512] = jnp.zeros((tm, 512 - MLA_Q_RANK), BF16)

        dckvn = _dot_nt(dk_ref[...].astype(BF16), wk_ref[...]) + \
            _dot_nt(dv_ref[...].astype(BF16), wv_ref[...])
        rkv = rkv_ref[...]
        xh = ckv_ref[...] * rkv
        dgkv_ref[...] += jnp.sum(dckvn * xh, axis=0, keepdims=True)
        dxh = dckvn * gkv_ref[...]
        dg_ref[:, P_CKV - P_CQ:P_CKV - P_CQ + MLA_KV_RANK] = (
            rkv * (dxh - xh * jnp.mean(dxh * xh, axis=-1, keepdims=True))).astype(BF16)

        dlog_a = _dot_exact(_chunk_tri(tm, False), dla_ref[...])
        dpre = dlog_a * (1.0 / GLA_GATE_NORM) * (1.0 - _sigmoid(pre_ref[...]))
        dbg_ref[...] += jnp.sum(dpre, axis=0, keepdims=True)
        dpre = dpre.astype(BF16)
        dpre_ref[...] = dpre
        lane = lax.broadcasted_iota(jnp.int32, (tm, LANE), 1)
        in_kr = jnp.logical_and(lane >= MISC_KR, lane < MISC_KR + MLA_ROPE)
        dmisc = jnp.where(in_kr, _rope_bwd(dkr, c, sn, sp), 0.0) + _dot_nt(dpre, wg_ref[...])
        dg_ref[:, P_MISC - P_CQ:P_MISC - P_CQ + LANE] = dmisc.astype(BF16)

    hq = MLA_HEADS * HEAD_PAD
    row = lambda w: pl.BlockSpec((tm, w), lambda i: (i, 0))
    full = lambda a: pl.BlockSpec(a.shape, lambda i: (0, 0))
    acc = lambda w: pl.BlockSpec((1, w), lambda i: (0, 0))
    sds = jax.ShapeDtypeStruct
    return pl.pallas_call(
        body, name="mla_prep_bwd",
        grid=(t // tm,),
        in_specs=[row(hq), row(hq), row(MLA_WIDTH), row(GLA_DK), row(GLA_DK),
                  pl.BlockSpec((tm, 512), lambda i: (i, _rel(P_CQ) // 512)),
                  pl.BlockSpec((tm, MLA_KV_RANK), lambda i: (i, _rel(P_CKV) // MLA_KV_RANK)),
                  row(1), row(1), full(g_q), full(g_kv), full(w_uq_p), full(w_k_p), full(w_v),
                  full(w_gate_p), row(LANE), row(LANE), row(LANE)],
        out_specs=(row(gw), row(hq), row(GLA_DK),
                   acc(MLA_Q_RANK), acc(MLA_KV_RANK), acc(GLA_DK)),
        out_shape=(sds((t, gw), BF16), sds((t, hq), BF16), sds((t, GLA_DK), BF16),
                   sds((1, MLA_Q_RANK), F32), sds((1, MLA_KV_RANK), F32), sds((1, GLA_DK), F32)),
        compiler_params=_cparams(("arbitrary",)),
    )(dq, dk, dv, dla, pre, proj, proj, rq, rkv, g_q, g_kv, w_uq_p, w_k_p, w_v, w_gate_p,
      rc, rsn, rsp)


def _inproj_bwd(dgroups, w_pts, x, rstd, g_in, dx2, after):
    t = x.shape[0]
    tm = min(256, t)

    def body(d0_ref, d1_ref, d2_ref, w0_ref, w1_ref, w2_ref, x_ref, r_ref, g_ref, dx2_ref, after_ref,
             dx_ref, dg_ref):
        del after_ref

        @pl.when(pl.program_id(0) == 0)
        def _():
            dg_ref[...] = jnp.zeros_like(dg_ref)

        dh = jnp.zeros((tm, D_MODEL), F32)
        for d_ref, w_ref in zip((d0_ref, d1_ref, d2_ref), (w0_ref, w1_ref, w2_ref)):
            dh = dh + _dot(d_ref[...], w_ref[...])
        r = r_ref[...]
        xh = x_ref[...] * r
        dg_ref[...] += jnp.sum(dh * xh, axis=0, keepdims=True)
        dxh = dh * g_ref[...]
        dx_ref[...] = dx2_ref[...] + r * (dxh - xh * jnp.mean(dxh * xh, axis=-1, keepdims=True))

    row = lambda w: pl.BlockSpec((tm, w), lambda i: (i, 0))
    return pl.pallas_call(
        body, name="inproj_bwd",
        grid=(t // tm,),
        in_specs=[row(w) for _, w in P_GROUPS]
        + [pl.BlockSpec((w, D_MODEL), lambda i: (0, 0)) for _, w in P_GROUPS]
        + [row(D_MODEL), row(1), pl.BlockSpec((1, D_MODEL), lambda i: (0, 0)), row(D_MODEL),
           pl.BlockSpec(memory_space=pl.ANY)],
        out_specs=(row(D_MODEL), pl.BlockSpec((1, D_MODEL), lambda i: (0, 0))),
        out_shape=(jax.ShapeDtypeStruct((t, D_MODEL), F32),
                   jax.ShapeDtypeStruct((1, D_MODEL), F32)),
        compiler_params=_cparams(("arbitrary",)),
    )(*dgroups, *w_pts, x, rstd, g_in, dx2, after)


def _matmul(name, a, b, tm, tn, dtype=F32, after=None):
    kk, m = a.shape
    n = b.shape[1]
    extra = [] if after is None else [after]

    def body(a_ref, b_ref, *rest):
        rest[-1][...] = _dot_tn(a_ref[...].astype(BF16), b_ref[...].astype(BF16)).astype(dtype)

    return pl.pallas_call(
        body, name=name,
        grid=(n // tn, m // tm),
        in_specs=[pl.BlockSpec((kk, tm), lambda j, i: (0, i)),
                  pl.BlockSpec((kk, tn), lambda j, i: (0, j))]
        + [pl.BlockSpec(memory_space=pl.ANY) for _ in extra],
        out_specs=pl.BlockSpec((tm, tn), lambda j, i: (i, j)),
        out_shape=jax.ShapeDtypeStruct((m, n), dtype),
        compiler_params=_cparams(("arbitrary", "arbitrary")),
    )(a, b, *extra)


def _adamw_update(part_refs, w_ref, m_ref, v_ref, g_ref, d_ref, nm_ref, nv_ref):
    g = part_refs[0][...].astype(F32)
    for p_ref in part_refs[1:]:
        g = g + p_ref[...].astype(F32)
    m_new = ADAM_B1 * m_ref[...] + (1.0 - ADAM_B1) * g
    v_new = ADAM_B2 * v_ref[...] + (1.0 - ADAM_B2) * (g * g)
    m_hat = m_new / (1.0 - ADAM_B1 ** ADAM_STEP)
    v_hat = v_new / (1.0 - ADAM_B2 ** ADAM_STEP)
    g_ref[...] = g
    nm_ref[...] = m_new
    nv_ref[...] = v_new
    d_ref[...] = -ADAM_LR * (m_hat / (jnp.sqrt(v_hat) + ADAM_EPS) + ADAM_WD * w_ref[...])


def _adamw_rows(name, parts, w, m, v, tr, first=None):
    _, rows, cols = w.shape
    slots = parts.shape[0]

    def body(*refs):
        lead_refs, p_ref = ([], refs[0]) if first is None else ([refs[0]], refs[1])
        _adamw_update(lead_refs + [p_ref.at[q] for q in range(slots)], *refs[len(lead_refs) + 1:])

    blk = pl.BlockSpec((None, tr, cols), lambda i: (0, i, 0))
    out = jax.ShapeDtypeStruct((1, rows, cols), F32)
    lead = [] if first is None else [pl.BlockSpec((tr, cols), lambda i: (i, 0))]
    return pl.pallas_call(
        body, name=name,
        grid=(rows // tr,),
        in_specs=lead + [pl.BlockSpec((slots, tr, cols), lambda i: (0, i, 0)), blk, blk, blk],
        out_specs=(blk, blk, blk, blk),
        out_shape=(out, out, out, out),
        compiler_params=_cparams(("arbitrary",)),
    )(*([] if first is None else [first]), parts, w, m, v)


def _adamw_transposed(name, first, parts, w, m, v, tl):
    _, rows, cols = w.shape
    slots, padded = parts.shape[:2]

    def body(f_ref, p_ref, *refs):
        _adamw_update([f_ref.at[pl.ds(0, cols)]]
                      + [p_ref.at[q, pl.ds(0, cols)] for q in range(slots)], *refs)

    blk = pl.BlockSpec((cols, None, tl), lambda i: (0, 0, i))
    out = jax.ShapeDtypeStruct((cols, 1, rows), F32)
    res = pl.pallas_call(
        body, name=name,
        grid=(rows // tl,),
        in_specs=[pl.BlockSpec((padded, tl), lambda i: (0, i)),
                  pl.BlockSpec((slots, padded, tl), lambda i: (0, 0, i)), blk, blk, blk],
        out_specs=(blk, blk, blk, blk),
        out_shape=(out, out, out, out),
        compiler_params=_cparams(("arbitrary",)),
    )(first, parts, *[a.transpose(2, 0, 1) for a in (w, m, v)])
    return [r.transpose(1, 2, 0) for r in res]


def _adamw_group(firsts, parts, ws, ms, vs):
    n = len(ws)

    def body(*refs):
        ins, outs = refs[:5 * n], refs[5 * n:]
        x, y, c = _mesh_pos()
        for a in range(n):
            _adamw_update([ins[a].at[4 * x + 2 * y + c]]
                          + [ins[n + a].at[q] for q in range(ins[n + a].shape[0])],
                          *[r.at[0] for r in (ins[2 * n + a], ins[3 * n + a], ins[4 * n + a])],
                          *[r.at[0] for r in outs[4 * a:4 * a + 4]])

    vmem = lambda k: [pl.BlockSpec(memory_space=pltpu.VMEM) for _ in range(k)]
    out_shape = []
    for w in ws:
        out_shape += [jax.ShapeDtypeStruct(w.shape, F32)] * 4
    res = pl.pallas_call(
        body, name="adamw_small_weights",
        in_specs=vmem(5 * n), out_specs=tuple(vmem(4 * n)), out_shape=tuple(out_shape),
        compiler_params=_cparams(),
    )(*firsts, *parts, *ws, *ms, *vs)
    return [res[4 * a:4 * a + 4] for a in range(n)]


def _rope_tables(positions):
    half = MLA_ROPE // 2
    freqs = ROPE_THETA ** (-jnp.arange(half, dtype=F32) / half)
    ang = positions.astype(F32).reshape(-1, 1) * freqs
    cos, sin = jnp.cos(ang), jnp.sin(ang)
    t = ang.shape[0]
    one, zero = jnp.ones((t, MLA_NOPE), F32), jnp.zeros((t, half), F32)
    tail = jnp.zeros((t, LANE - MLA_QK), F32)
    rc = jnp.concatenate([one, cos, cos, tail], axis=1)
    rsn = jnp.concatenate([0.0 * one, -sin, zero, tail], axis=1)
    rsp = jnp.concatenate([0.0 * one, zero, sin, tail], axis=1)
    return rc, rsn, rsp


def _cols_full(g):
    return g.transpose(1, 0, 2)


def kernel(x, positions, g_in, w_in, g_q, w_uq, g_kv, w_ukv, w_gla_gate, b_gla_gate, g_gla, w_proj_mla, w_proj_gla, w_out, g_final, loss_target, m_g_in, m_w_in, m_g_q, m_w_uq, m_g_kv, m_w_ukv, m_w_gla_gate, m_b_gla_gate, m_g_gla, m_w_proj_mla, m_w_proj_gla, m_w_out, m_g_final, v_g_in, v_w_in, v_g_q, v_w_uq, v_g_kv, v_w_ukv, v_w_gla_gate, v_b_gla_gate, v_g_gla, v_w_proj_mla, v_w_proj_gla, v_w_out, v_g_final):
    t = x.shape[1]
    x2d = x.reshape(t, D_MODEL)
    tgt = loss_target.reshape(t, D_MODEL)
    g_final2 = g_final.reshape(1, D_MODEL)
    sharded = [(w_in, m_w_in, v_w_in), (w_uq, m_w_uq, v_w_uq), (w_ukv, m_w_ukv, v_w_ukv),
               (w_gla_gate, m_w_gla_gate, v_w_gla_gate), (w_proj_mla, m_w_proj_mla, v_w_proj_mla),
               (w_proj_gla, m_w_proj_gla, v_w_proj_gla), (w_out, m_w_out, v_w_out)]

    w_in_t = w_in.transpose(2, 0, 1).reshape(SHARD_COLS, D_MODEL)
    everyone = tuple(range(N_DEV))
    w_in_b = w_in_t.astype(BF16)
    b_uq, b_ukv, b_gate, b_pm, b_pg, b_o = [s[0][0].astype(BF16) for s in sharded[1:]]
    stages = ((0, 2, 4, 6), (1, 3, 5, 7))
    where = {d: (k, i) for k, srcs in enumerate(stages) for i, d in enumerate(srcs)}
    g_in_1, g_uq, g_ukv, g_gate = _all_gather(
        "all_gather_first", [w_in_b, b_uq, b_ukv, b_gate], [stages[0]] + [everyone] * 3)
    w_uq_p = jnp.pad(_cols_full(g_uq), ((0, 0), (0, 0), (0, HEAD_PAD - MLA_QK))).reshape(
        MLA_Q_RANK, MLA_HEADS * HEAD_PAD)
    ukv = _cols_full(g_ukv)
    w_k_p = jnp.pad(ukv[:, :, :MLA_NOPE], ((0, 0), (0, 0), (0, HEAD_PAD - MLA_NOPE))).reshape(
        MLA_KV_RANK, MLA_HEADS * HEAD_PAD)
    w_v = ukv[:, :, MLA_NOPE:].reshape(MLA_KV_RANK, MLA_WIDTH)
    w_gate_p = jnp.pad(_cols_full(g_gate).reshape(GLA_GATE_RANK, GLA_DK),
                       ((MISC_ALR, LANE - MISC_ALR - GLA_GATE_RANK), (0, 0)))
    rc, rsn, rsp = _rope_tables(positions)

    w_lat = _weights_to_p("weights_latents", [g_in_1], where, 2)
    proj_lat, h, rstd = _inproj(x2d, g_in, w_lat)
    q, k, v, log_a, pre, cqn, ckvn, rq, rkv, misc = _mla_prep(
        proj_lat, g_q, g_kv, w_uq_p, w_k_p, w_v, w_gate_p, b_gla_gate, rc, rsn, rsp)
    o_mla, lse, (g_in_2, g_pm, g_pg, g_o) = _mla_attn_fwd(
        q, k, v, [w_in_b, b_pm, b_pg, b_o], [stages[1]] + [everyone] * 3)
    w_gla = _weights_to_p("weights_gla", [g_in_1, g_in_2], where, 0)
    proj_gla = _proj("inproj_gla", h, w_gla)
    o_gla, states = _gla_fwd(proj_gla, log_a)
    w_out_path = _weights_to_p("weights_out_path", [g_in_1, g_in_2], where, 1)
    proj_out = _proj("inproj_out_path", h, w_out_path)
    w_in_p = (w_gla, w_out_path, w_lat)
    w_pm = _cols_full(g_pm).reshape(MLA_WIDTH, D_MODEL)
    w_pg = g_pg.reshape(GLA_DV, D_MODEL)
    w_o = g_o.reshape(D_MODEL, D_MODEL)

    (dx2, do_mla, do_gla, d_out, merged, um, ug, dym, dyg, loss_p, dg_final,
     dg_gla) = _post(o_mla, proj_out, o_gla, x2d, tgt, g_gla, g_final2, w_pm, w_pg, w_o)

    p_pm = _matmul("dw_proj_mla", um, dym, 512, D_MODEL, BF16).reshape(
        MLA_WIDTH, N_DEV, D_MODEL // N_DEV).transpose(1, 0, 2)
    p_pg = _matmul("dw_proj_gla", ug, dyg, 512, D_MODEL, BF16).reshape(N_DEV, -1, D_MODEL)
    p_o = _matmul("dw_out", merged, dx2, 512, D_MODEL, BF16).reshape(N_DEV, -1, D_MODEL)
    own_in = jnp.zeros((SHARD_PAD, D_MODEL), BF16)
    land_in = lax.empty((PEERS, SHARD_PAD, D_MODEL), BF16)
    dw_groups, started, lands = {}, [], [land_in]

    def reduce_scatter_stage(s, dests, own_in, extra=()):
        parts_in, own_in = _grads_to_shards("grads_to_shards_%d" % s, dw_groups, dests, own_in)
        first = len(lands)
        lands.extend(lax.empty((PEERS,) + p.shape[1:], BF16) for p in extra)
        idx = [0] + list(range(first, len(lands)))
        all_dests = [[(i, d, r0, r1) for i, (d, ranges) in enumerate(dests) for r0, r1 in ranges]]
        all_dests += [_whole(everyone, p.shape[1]) for p in extra]
        sems, parts, new_lands, token = _ici_start(
            "ici_start_%d" % s, [parts_in] + list(extra), [lands[i] for i in idx], all_dests)
        for a, i in enumerate(idx):
            lands[i] = new_lands[a]
            started.append((sems[a][0], sems[a][1], parts[a], i, all_dests[a]))
        return own_in, token

    def late_small_stage(arrays):
        idx = list(range(len(lands), len(lands) + len(arrays)))
        lands.extend(lax.empty((PEERS,) + p.shape[1:], BF16) for p in arrays)
        all_dests = [_whole(everyone, p.shape[1]) for p in arrays]
        sems, parts, new_lands, token = _ici_start(
            "ici_start_4", list(arrays), [lands[i] for i in idx], all_dests)
        for a, i in enumerate(idx):
            lands[i] = new_lands[a]
            started.append((sems[a][0], sems[a][1], parts[a], i, all_dests[a]))
        return token

    dw_groups[1] = _matmul("dw_in_1", d_out, h, 512, D_MODEL, BF16)
    full = [(0, SHARD_PAD)]
    own_in, token = reduce_scatter_stage(
        1, [(5, full), (6, full), (7, full), (0, [(672, SHARD_PAD)]), (1, [(0, 384)]),
            (4, [(96, SHARD_PAD)])], own_in, (p_pm, p_pg, p_o))
    d_gla, dla = _gla_bwd(proj_gla, log_a, do_gla, states, token)
    dw_groups[0] = _matmul("dw_in_0", d_gla, h, 512, D_MODEL, BF16)
    own_in, token = reduce_scatter_stage(
        2, [(1, [(384, SHARD_PAD)]), (2, full), (3, full), (4, [(0, 64)])], own_in)
    dq, dk, dv = _mla_attn_bwd(q, k, v, o_mla, do_mla, lse, token)
    d_lat, dqpre, dpre, dg_q, dg_kv, db_gate = _mla_prep_bwd(
        dq, dk, dv, dla, pre, proj_lat, rq, rkv, g_q, g_kv, w_uq_p, w_k_p, w_v, w_gate_p, rc, rsn, rsp)
    dw_groups[2] = _matmul("dw_in_2", d_lat, h, 896, D_MODEL, BF16)
    own_in, token = reduce_scatter_stage(3, [(0, [(0, 672)]), (4, [(64, 96)])], own_in)
    dw_uq = _matmul("dw_uq", cqn, dqpre, MLA_Q_RANK, D_MODEL, BF16, after=token)
    p_uq = dw_uq.reshape(MLA_Q_RANK, MLA_HEADS, HEAD_PAD)[:, :, :MLA_QK].transpose(1, 0, 2)
    dw_k = _matmul("dw_uk", ckvn, dk, MLA_KV_RANK, D_MODEL, BF16)
    dw_v = _matmul("dw_uv", ckvn, dv, MLA_KV_RANK, 512, BF16)
    p_ukv = jnp.concatenate(
        [dw_k.reshape(MLA_KV_RANK, MLA_HEADS, HEAD_PAD)[:, :, :MLA_NOPE],
         dw_v.reshape(MLA_KV_RANK, MLA_HEADS, MLA_VDIM)], axis=2).transpose(1, 0, 2)
    dw_gate = _matmul("dw_gate", misc, dpre, LANE, 512, BF16)
    p_gate = dw_gate[MISC_ALR:MISC_ALR + GLA_GATE_RANK].reshape(
        GLA_GATE_RANK, N_DEV, GLA_DK // N_DEV).transpose(1, 0, 2)
    token = late_small_stage((p_uq, p_ukv, p_gate))
    grad_x, dg_in = _inproj_bwd((d_gla, d_out, d_lat), w_in_p, x2d, rstd, g_in, dx2, token)
    small = jnp.concatenate([dg_in.reshape(-1), dg_q.reshape(-1), dg_kv.reshape(-1),
                             db_gate.reshape(-1), dg_gla.reshape(-1), dg_final.reshape(-1),
                             loss_p[0, :1]])
    small = jnp.pad(small, (0, SMALL_ROWS * LANE - small.shape[0])).reshape(SMALL_ROWS, LANE)

    (small_all,) = _all_gather("all_gather_small", [small], [everyone])
    lands = _ici_wait("ici_wait", started, lands, small_all)
    big = [_adamw_transposed("adamw_w_in", own_in, lands[0], *sharded[0], 512)]
    big += _adamw_group([p_uq, p_ukv, p_gate, p_pm, p_pg, p_o], list(lands[4:7]) + list(lands[1:4]),
                        *[[s[j] for s in sharded[1:]] for j in range(3)])
    replicated = [(g_in, m_g_in, v_g_in), (g_q, m_g_q, v_g_q), (g_kv, m_g_kv, v_g_kv),
                  (b_gla_gate, m_b_gla_gate, v_b_gla_gate), (g_gla, m_g_gla, v_g_gla),
                  (g_final, m_g_final, v_g_final)]
    spacks = [jnp.pad(jnp.concatenate([s[j].reshape(-1) for s in replicated]),
                      (0, SMALL_ROWS * LANE - sum(SMALL_SIZES))).reshape(1, SMALL_ROWS, LANE)
              for j in range(3)]
    tiny = _adamw_rows("adamw_gains", small_all, spacks[0], spacks[1], spacks[2], SMALL_ROWS)

    outs = {}
    names = ("w_in", "w_uq", "w_ukv", "w_gla_gate", "w_proj_mla", "w_proj_gla", "w_out")
    for j, kind in enumerate(("grad", "delta", "new_m", "new_v")):
        for name, res in zip(names, big):
            outs[kind, name] = res[j]
        flat = tiny[j].reshape(-1)
        off = 0
        for name, size in zip(("g_in", "g_q", "g_kv", "b_gla_gate", "g_gla", "g_final"), SMALL_SIZES):
            shape = (size,) if name == "g_final" else (1, size)
            outs[kind, name] = flat[off:off + size].reshape(shape)
            off += size
    loss = tiny[0].reshape(-1)[sum(SMALL_SIZES)]
    order = ("g_in", "w_in", "g_q", "w_uq", "g_kv", "w_ukv", "w_gla_gate", "b_gla_gate", "g_gla",
             "w_proj_mla", "w_proj_gla", "w_out", "g_final")
    result = [loss, grad_x.reshape(1, t, D_MODEL)]
    for kind in ("grad", "delta", "new_m", "new_v"):
        result += [outs[kind, name] for name in order]
    return tuple(result)
```

```python
import jax
import jax.numpy as jnp
from jax import lax
from jax.experimental import pallas as pl
from jax.experimental.pallas import tpu as pltpu

F32 = jnp.float32
BF16 = jnp.bfloat16
MESH = pl.DeviceIdType.MESH
N_DEV = 8

D_MODEL = 1024
EPS = 1e-6
MLA_HEADS = 8
MLA_NOPE = 64
MLA_ROPE = 32
MLA_VDIM = 64
MLA_Q_RANK = 384
MLA_KV_RANK = 256
MLA_QK = MLA_NOPE + MLA_ROPE
MLA_WIDTH = MLA_HEADS * MLA_VDIM
ROPE_THETA = 10000.0
GLA_HEADS = 4
GLA_DK = 512
GLA_DV = 1024
GLA_HK = 128
GLA_HV = 256
GLA_GATE_RANK = 16
GLA_GATE_NORM = 16.0
GLA_CHUNK = 64
GLA_CHUNKS_PER_STEP = 8
D_IN = 6320

ADAM_LR = 0.001
ADAM_B1 = 0.9
ADAM_B2 = 0.999
ADAM_EPS = 1e-08
ADAM_WD = 0.01
ADAM_STEP = 10

LANE = 128
HEAD_PAD = 128
VMEM_LIMIT = 48 * 1024 * 1024
VMEM_LIMIT_LARGE = 60 * 1024 * 1024

P_VG, P_QG, P_KG = 0, 1024, 1536
P_ZGLA, P_GMLA, P_GGLA, P_ZMLA = 2048, 3072, 4096, 5120
P_CQ, P_CKV, P_MISC = 5632, 6144, 6400
P_TOTAL = 6528
P_GROUPS = ((0, 2048), (2048, 3584), (5632, 896))
MISC_KR = 64
MISC_ALR = 96
SHARD_COLS = D_IN // N_DEV
SHARD_PAD = 800
P_COMPONENTS = ((0, 384, P_CQ), (384, 256, P_CKV), (640, 32, P_MISC + MISC_KR), (672, 512, P_ZMLA),
                (1184, 512, P_QG), (1696, 512, P_KG), (2208, 1024, P_VG),
                (3232, 16, P_MISC + MISC_ALR), (3248, 1024, P_ZGLA), (4272, 1024, P_GMLA),
                (5296, 1024, P_GGLA))

SMALL_SIZES = (1024, 384, 256, 512, 256, 1024)
SMALL_ROWS = 32


def _segments():
    segs = []
    for g0, n, p0 in P_COMPONENTS:
        g = g0
        while g < g0 + n:
            d = g // SHARD_COLS
            end = min(g0 + n, (d + 1) * SHARD_COLS)
            segs.append((d, g - d * SHARD_COLS, end - g, p0 + g - g0))
            g = end
    return segs


def _group_of(p0):
    return max(i for i, (off, _) in enumerate(P_GROUPS) if off <= p0)


def _rel(p0):
    return p0 - P_GROUPS[_group_of(p0)][0]


def _cparams(sem=None, limit=VMEM_LIMIT):
    if sem is None:
        return pltpu.CompilerParams(vmem_limit_bytes=limit)
    return pltpu.CompilerParams(dimension_semantics=sem, vmem_limit_bytes=limit)


def _sigmoid(v):
    return 1.0 / (1.0 + jnp.exp(-v))


def _dot(a, b):
    return jnp.dot(a, b, preferred_element_type=F32)


def _dot_nt(a, b):
    return lax.dot_general(a, b, (((1,), (1,)), ((), ())), preferred_element_type=F32)


def _dot_tn(a, b):
    return lax.dot_general(a, b, (((0,), (0,)), ((), ())), preferred_element_type=F32)


def _dot_exact(a, b):
    return jnp.dot(a, b, preferred_element_type=F32, precision=lax.Precision.HIGHEST)


def _rope_fwd(blk, c, sn, sp):
    return blk * c + pltpu.roll(blk, LANE - 16, 1) * sn + pltpu.roll(blk, 16, 1) * sp


def _rope_bwd(blk, c, sn, sp):
    return blk * c + pltpu.roll(blk * sn, 16, 1) + pltpu.roll(blk * sp, LANE - 16, 1)


def _mesh_pos():
    return lax.axis_index("x"), lax.axis_index("y"), lax.axis_index("c")


def _hbm_specs(n):
    return [pl.BlockSpec(memory_space=pltpu.HBM) for _ in range(n)]


def _dev(d):
    return d >> 2, (d >> 1) & 1, d & 1


def _gather_plan(shards, sources):
    na, most = len(shards), max(len(s) for s in sources)
    out_shape = [jax.ShapeDtypeStruct((len(srcs),) + s.shape, s.dtype)
                 for s, srcs in zip(shards, sources)]
    sems = [pltpu.SemaphoreType.DMA((na, most)) for _ in range(3)]
    sems += [pltpu.SemaphoreType.DMA((na, most, 3))]
    sems += [pltpu.SemaphoreType.DMA((na, most)) for _ in range(3)]
    return out_shape, sems


def _gather_hooks(x_refs, out_refs, sems, sources):
    local_sems, d2d_send, d2d_recv, ici_send, ici_recv, fwd_send, fwd_recv = sems
    x, y, c = _mesh_pos()
    chips = [(1 - x, y), (x, 1 - y), (1 - x, 1 - y)]
    items = []
    for a, srcs in enumerate(sources):
        for i, d in enumerate(srcs):
            dx, dy, dc = _dev(d)
            near = jnp.logical_and(x == dx, y == dy)
            far = jnp.logical_not(near)
            slot = out_refs[a].at[i]

            def remote(src, to, send_sem, recv_sem, slot=slot):
                return pltpu.make_async_remote_copy(
                    src_ref=src, dst_ref=slot, send_sem=send_sem, recv_sem=recv_sem,
                    device_id=to, device_id_type=MESH)

            items.append(dict(
                me=jnp.logical_and(near, c == dc), sibling=jnp.logical_and(near, c != dc),
                relay=jnp.logical_and(far, c == dc), behind=jnp.logical_and(far, c != dc),
                local=pltpu.make_async_copy(x_refs[a], slot, local_sems.at[a, i]),
                to_sibling=remote(x_refs[a], (x, y, 1 - c), d2d_send.at[a, i], d2d_recv.at[a, i]),
                to_chips=[remote(x_refs[a], (*chip, c), ici_send.at[a, i, j], ici_recv.at[a, i])
                          for j, chip in enumerate(chips)],
                forward=remote(slot, (x, y, 1 - c), fwd_send.at[a, i], fwd_recv.at[a, i])))

    def start():
        for it in items:
            @pl.when(it["me"])
            def _(it=it):
                it["local"].start()
                it["to_sibling"].start()
                for cp in it["to_chips"]:
                    cp.start()

    def finish():
        for it in items:
            @pl.when(it["relay"])
            def _(it=it):
                it["to_chips"][0].wait_recv()
                it["forward"].start()
        for it in items:
            pl.when(it["sibling"])(it["to_sibling"].wait_recv)
            pl.when(it["behind"])(it["forward"].wait_recv)
            pl.when(it["relay"])(it["forward"].wait_send)

            @pl.when(it["me"])
            def _(it=it):
                it["local"].wait()
                it["to_sibling"].wait_send()
                for cp in it["to_chips"]:
                    cp.wait_send()

    return start, finish


def _all_gather(name, shards, sources):
    n = len(shards)
    out_shape, sems = _gather_plan(shards, sources)

    def body(*refs):
        start, finish = _gather_hooks(refs[:n], refs[n:2 * n], refs[2 * n:], sources)
        start()
        finish()

    return pl.pallas_call(
        body, name=name,
        out_shape=tuple(out_shape),
        in_specs=_hbm_specs(n), out_specs=tuple(_hbm_specs(n)),
        scratch_shapes=sems,
        compiler_params=_cparams(),
    )(*shards)


PEERS = N_DEV - 1


def _whole(dests, rows):
    return [(i, d, 0, rows) for i, d in enumerate(dests)]


def _ici_copies(p_ref, land_ref, send_sems, recv_sems, pieces):
    x, y, c = _mesh_pos()
    sends, arrivals = [], []

    def rows_of(ref, j, r0, r1):
        return ref.at[j] if (r0, r1) == (0, ref.shape[1]) else ref.at[j, pl.ds(r0, r1 - r0)]

    for p, (i, d, r0, r1) in enumerate(pieces):
        dx, dy, dc = _dev(d)
        k = (4 * (x != dx).astype(jnp.int32) + 2 * (y != dy).astype(jnp.int32)
             + (c != dc).astype(jnp.int32))
        slot = jnp.maximum(k - 1, 0)
        sends.append((k > 0, pltpu.make_async_remote_copy(
            src_ref=rows_of(p_ref, i, r0, r1), dst_ref=rows_of(land_ref, slot, r0, r1),
            send_sem=send_sems.at[p], recv_sem=recv_sems.at[p * PEERS + slot],
            device_id=(dx, dy, dc), device_id_type=MESH)))
        arrivals.append((k == 0, [pltpu.make_async_remote_copy(
            src_ref=rows_of(p_ref, i, r0, r1), dst_ref=rows_of(land_ref, r, r0, r1),
            send_sem=send_sems.at[p], recv_sem=recv_sems.at[p * PEERS + r],
            device_id=(dx, dy, dc), device_id_type=MESH) for r in range(PEERS)]))
    return sends, arrivals


def _ici_start(name, hs, lands, dests):
    na = len(hs)

    def body(*refs):
        h_refs, land_refs, sems = refs[:na], refs[na:2 * na], refs[2 * na:4 * na]
        token = refs[-1]
        for a in range(na):
            sends, _ = _ici_copies(h_refs[a], land_refs[a], sems[2 * a], sems[2 * a + 1], dests[a])
            for go, cp in sends:
                pl.when(go)(cp.start)
        token[...] = jnp.zeros_like(token)

    hbm, sem = pl.BlockSpec(memory_space=pltpu.HBM), pl.BlockSpec(memory_space=pltpu.SEMAPHORE)
    sem_shapes = []
    for a in range(na):
        sem_shapes += [pltpu.SemaphoreType.DMA((len(dests[a]),)),
                       pltpu.SemaphoreType.DMA((len(dests[a]) * PEERS,))]
    res = pl.pallas_call(
        body, name=name,
        out_shape=tuple(sem_shapes) + tuple(pltpu.HBM(v.shape, v.dtype) for v in list(hs) + list(lands))
        + (jax.ShapeDtypeStruct((8, LANE), F32),),
        in_specs=(hbm,) * (2 * na),
        out_specs=(sem,) * (2 * na) + (hbm,) * (2 * na) + (pl.BlockSpec(memory_space=pltpu.VMEM),),
        input_output_aliases={i: 2 * na + i for i in range(2 * na)},
        compiler_params=pltpu.CompilerParams(
            has_side_effects=pltpu.SideEffectType.DATAFLOW_SIDE_EFFECTING,
            vmem_limit_bytes=VMEM_LIMIT),
    )(*[pltpu.with_memory_space_constraint(v, pltpu.HBM) for v in list(hs) + list(lands)])
    sems = [(res[2 * a], res[2 * a + 1]) for a in range(na)]
    return sems, res[2 * na:3 * na], res[3 * na:4 * na], res[-1]


def _ici_wait(name, started, lands, after):
    k, nl = len(started), len(lands)

    def body(*refs):
        land_refs = refs[3 * k:3 * k + nl]
        for s in range(k):
            h_ref, send_sems, recv_sems = refs[3 * s:3 * s + 3]
            sends, arrivals = _ici_copies(h_ref, land_refs[started[s][3]], send_sems, recv_sems,
                                          started[s][4])
            for go, cp in sends:
                pl.when(go)(cp.wait_send)
            for here, cps in arrivals:
                for cp in cps:
                    pl.when(here)(cp.wait_recv)

    hbm, sem = pl.BlockSpec(memory_space=pltpu.HBM), pl.BlockSpec(memory_space=pltpu.SEMAPHORE)
    operands, specs = [], []
    for send_sems, recv_sems, h, _, _ in started:
        operands += [h, send_sems, recv_sems]
        specs += [hbm, sem, sem]
    return pl.pallas_call(
        body, name=name,
        out_shape=tuple(pltpu.HBM(v.shape, v.dtype) for v in lands),
        in_specs=tuple(specs) + (hbm,) * nl + (pl.BlockSpec(memory_space=pl.ANY),),
        out_specs=(hbm,) * nl,
        input_output_aliases={3 * k + i: i for i in range(nl)},
        compiler_params=pltpu.CompilerParams(
            has_side_effects=pltpu.SideEffectType.DATAFLOW_SIDE_EFFECTING,
            vmem_limit_bytes=VMEM_LIMIT),
    )(*operands, *lands, after)


def _weights_to_p(name, gathered, where, group):
    tl = 512
    off, width = P_GROUPS[group]
    segs = sorted([s for s in _segments() if _group_of(s[3]) == group], key=lambda s: s[3])
    used = sorted({where[s[0]][0] for s in segs})

    def body(*refs):
        g_refs, o_ref = dict(zip(used, refs[:-1])), refs[-1]
        pieces, pos = [], off
        for d, c0, n, p0 in segs:
            if p0 > pos:
                pieces.append(jnp.zeros((p0 - pos, tl), F32))
            k, slot = where[d]
            pieces.append(g_refs[k][slot, c0:c0 + n, :].astype(F32))
            pos = p0 + n
        if off + width > pos:
            pieces.append(jnp.zeros((off + width - pos, tl), F32))
        o_ref[...] = jnp.concatenate(pieces, axis=0).astype(BF16)

    return pl.pallas_call(
        body, name=name,
        grid=(D_MODEL // tl,),
        in_specs=[pl.BlockSpec((gathered[k].shape[0], SHARD_COLS, tl), lambda i: (0, 0, i))
                  for k in used],
        out_specs=pl.BlockSpec((width, tl), lambda i: (0, i)),
        out_shape=jax.ShapeDtypeStruct((width, D_MODEL), BF16),
        compiler_params=_cparams(("arbitrary",)),
    )(*[gathered[k] for k in used])


def _shard_groups(d):
    return sorted({_group_of(s[3]) for s in _segments() if s[0] == d})


def _grads_to_shards(name, groups, dests, own_prev):
    tl = 512
    segs = _segments()
    used = sorted(groups)

    def body(*refs):
        g_refs, prev_ref, o_ref, own_ref = dict(zip(used, refs[:-3])), refs[-3], refs[-2], refs[-1]
        x, y, c = _mesh_pos()
        own = prev_ref[...].astype(F32)
        row = lax.broadcasted_iota(jnp.int32, (SHARD_PAD, tl), 0)
        for i, (d, ranges) in enumerate(dests):
            pieces, pos, asked = [], 0, None
            for r0, r1 in sorted(ranges):
                if r0 > pos:
                    pieces.append(jnp.zeros((r0 - pos, tl), F32))
                for _, c0, n, p0 in sorted([s for s in segs if s[0] == d], key=lambda s: s[1]):
                    a, b = max(c0, r0), min(c0 + n, r1)
                    if a < b:
                        gi = _group_of(p0)
                        lo = p0 - P_GROUPS[gi][0] + a - c0
                        pieces.append(g_refs[gi][lo:lo + b - a, :].astype(F32))
                if r1 > SHARD_COLS:
                    pieces.append(jnp.zeros((r1 - max(r0, SHARD_COLS), tl), F32))
                pos = r1
                inside = jnp.logical_and(row >= r0, row < r1)
                asked = inside if asked is None else jnp.logical_or(asked, inside)
            if pos < SHARD_PAD:
                pieces.append(jnp.zeros((SHARD_PAD - pos, tl), F32))
            shard = jnp.concatenate(pieces, axis=0)
            o_ref[i] = shard.astype(BF16)
            own = jnp.where(jnp.logical_and(4 * x + 2 * y + c == d, asked), shard, own)
        own_ref[...] = own.astype(BF16)

    blk = pl.BlockSpec((SHARD_PAD, tl), lambda i: (0, i))
    return pl.pallas_call(
        body, name=name,
        grid=(D_MODEL // tl,),
        in_specs=[pl.BlockSpec((P_GROUPS[g][1], tl), lambda i: (0, i)) for g in used] + [blk],
        out_specs=(pl.BlockSpec((len(dests), SHARD_PAD, tl), lambda i: (0, 0, i)), blk),
        out_shape=(jax.ShapeDtypeStruct((len(dests), SHARD_PAD, D_MODEL), BF16),
                   jax.ShapeDtypeStruct((SHARD_PAD, D_MODEL), BF16)),
        input_output_aliases={len(used): 1},
        compiler_params=_cparams(("arbitrary",)),
    )(*[groups[g] for g in used], own_prev)


def _inproj(x, g_in, w_pt):
    t = x.shape[0]
    tm = min(512, t)
    width = w_pt.shape[0]

    def body(x_ref, g_ref, w_ref, proj_ref, h_ref, r_ref):
        xf = x_ref[...]
        r = lax.rsqrt(jnp.mean(xf * xf, axis=-1, keepdims=True) + EPS)
        h = ((xf * r) * g_ref[...]).astype(BF16)
        proj_ref[...] = _dot_nt(h, w_ref[...])
        h_ref[...] = h
        r_ref[...] = r

    row = lambda w: pl.BlockSpec((tm, w), lambda i: (i, 0))
    return pl.pallas_call(
        body, name="inproj_latents",
        grid=(t // tm,),
        in_specs=[row(D_MODEL), pl.BlockSpec((1, D_MODEL), lambda i: (0, 0)),
                  pl.BlockSpec((width, D_MODEL), lambda i: (0, 0))],
        out_specs=(row(width), row(D_MODEL), row(1)),
        out_shape=(jax.ShapeDtypeStruct((t, width), F32),
                   jax.ShapeDtypeStruct((t, D_MODEL), BF16),
                   jax.ShapeDtypeStruct((t, 1), F32)),
        compiler_params=_cparams(("arbitrary",)),
    )(x, g_in, w_pt)


def _proj(name, h, w_pt):
    t = h.shape[0]
    tm = min(512, t)
    width = w_pt.shape[0]

    def body(h_ref, w_ref, o_ref):
        o_ref[...] = _dot_nt(h_ref[...], w_ref[...])

    return pl.pallas_call(
        body, name=name,
        grid=(t // tm,),
        in_specs=[pl.BlockSpec((tm, D_MODEL), lambda i: (i, 0)),
                  pl.BlockSpec((width, D_MODEL), lambda i: (0, 0))],
        out_specs=pl.BlockSpec((tm, width), lambda i: (i, 0)),
        out_shape=jax.ShapeDtypeStruct((t, width), F32),
        compiler_params=_cparams(("arbitrary",)),
    )(h, w_pt)


def _mla_prep(proj, g_q, g_kv, w_uq_p, w_k_p, w_v, w_gate_p, b_gate, rc, rsn, rsp):
    t = proj.shape[0]
    tm = min(512, t)
    hq = MLA_HEADS * HEAD_PAD

    def body(cq_ref, ckv_ref, misc_ref, gq_ref, gkv_ref, wuq_ref, wk_ref, wv_ref, wg_ref, bg_ref,
             c_ref, sn_ref, sp_ref,
             q_ref, k_ref, v_ref, la_ref, pre_ref, cqn_ref, ckvn_ref, rq_ref, rkv_ref, mb_ref):
        c, sn, sp = c_ref[...], sn_ref[...], sp_ref[...]
        cq = cq_ref[:, :MLA_Q_RANK]
        rq = lax.rsqrt(jnp.mean(cq * cq, axis=-1, keepdims=True) + EPS)
        cqn = ((cq * rq) * gq_ref[...]).astype(BF16)
        cqn_ref[...] = cqn
        rq_ref[...] = rq
        qpre = _dot(cqn, wuq_ref[...])
        ckv = ckv_ref[...]
        rkv = lax.rsqrt(jnp.mean(ckv * ckv, axis=-1, keepdims=True) + EPS)
        ckvn = ((ckv * rkv) * gkv_ref[...]).astype(BF16)
        ckvn_ref[...] = ckvn
        rkv_ref[...] = rkv
        kn = _dot(ckvn, wk_ref[...])
        v_ref[...] = _dot(ckvn, wv_ref[...]).astype(BF16)
        misc = misc_ref[...]
        krope = _rope_fwd(misc, c, sn, sp)
        for h in range(MLA_HEADS):
            sl = slice(h * HEAD_PAD, (h + 1) * HEAD_PAD)
            q_ref[:, sl] = _rope_fwd(qpre[:, sl], c, sn, sp).astype(BF16)
            k_ref[:, sl] = (kn[:, sl] + krope).astype(BF16)
        mb_ref[...] = misc.astype(BF16)
        pre = _dot(mb_ref[...], wg_ref[...]) + bg_ref[...]
        pre_ref[...] = pre
        log_a = (jnp.minimum(pre, 0.0) - jnp.log(1.0 + jnp.exp(-jnp.abs(pre)))) / GLA_GATE_NORM
        la_ref[...] = _dot_exact(_chunk_tri(tm, True), log_a)

    row = lambda w: pl.BlockSpec((tm, w), lambda i: (i, 0))
    full = lambda a: pl.BlockSpec(a.shape, lambda i: (0, 0))
    return pl.pallas_call(
        body, name="mla_prep",
        grid=(t // tm,),
        in_specs=[pl.BlockSpec((tm, 512), lambda i: (i, _rel(P_CQ) // 512)),
                  pl.BlockSpec((tm, MLA_KV_RANK), lambda i: (i, _rel(P_CKV) // MLA_KV_RANK)),
                  pl.BlockSpec((tm, LANE), lambda i: (i, _rel(P_MISC) // LANE)),
                  full(g_q), full(g_kv), full(w_uq_p), full(w_k_p), full(w_v), full(w_gate_p),
                  full(b_gate), row(LANE), row(LANE), row(LANE)],
        out_specs=(row(hq), row(hq), row(MLA_WIDTH), row(GLA_DK), row(GLA_DK),
                   row(MLA_Q_RANK), row(MLA_KV_RANK), row(1), row(1), row(LANE)),
        out_shape=(jax.ShapeDtypeStruct((t, hq), BF16), jax.ShapeDtypeStruct((t, hq), BF16),
                   jax.ShapeDtypeStruct((t, MLA_WIDTH), BF16),
                   jax.ShapeDtypeStruct((t, GLA_DK), F32), jax.ShapeDtypeStruct((t, GLA_DK), F32),
                   jax.ShapeDtypeStruct((t, MLA_Q_RANK), BF16),
                   jax.ShapeDtypeStruct((t, MLA_KV_RANK), BF16),
                   jax.ShapeDtypeStruct((t, 1), F32), jax.ShapeDtypeStruct((t, 1), F32),
                   jax.ShapeDtypeStruct((t, LANE), BF16)),
        compiler_params=_cparams(("arbitrary",)),
    )(proj, proj, proj, g_q, g_kv, w_uq_p, w_k_p, w_v, w_gate_p, b_gate, rc, rsn, rsp)


def _attn_masks(tq, i):
    keys = (i + 1) * tq
    rows = i * tq + lax.broadcasted_iota(jnp.int32, (tq, keys), 0)
    cols = lax.broadcasted_iota(jnp.int32, (tq, keys), 1)
    lane = lax.broadcasted_iota(jnp.int32, (tq, LANE), 1)
    return cols <= rows, lane < MLA_VDIM


def _for_each_query_tile(n_tiles, fn):
    for i in range(n_tiles):
        pl.when(pl.program_id(1) == i)(lambda i=i: fn(i))


def _mla_attn_fwd(q, k, v, shards, sources):
    t = q.shape[0]
    tq = min(256, t)
    scale = MLA_QK ** -0.5
    ns = len(shards)
    g_shapes, g_sems = _gather_plan(shards, sources)
    grid = (MLA_HEADS // 2, t // tq)

    def body(q_ref, k_ref, v_ref, *rest):
        o_ref, lse_ref = rest[ns:ns + 2]
        start, finish = _gather_hooks(rest[:ns], rest[ns + 2:2 * ns + 2], rest[2 * ns + 2:], sources)
        step = pl.program_id(0) * grid[1] + pl.program_id(1)
        pl.when(step == 0)(start)

        def tile(i):
            keys = (i + 1) * tq
            causal, low = _attn_masks(tq, i)
            vp = v_ref[0:keys, :]
            acc = jnp.zeros((tq, LANE), F32)
            for hh in range(2):
                sl = slice(hh * HEAD_PAD, (hh + 1) * HEAD_PAD)
                s = _dot_nt(q_ref[:, sl], k_ref[0:keys, sl]) * scale
                s = jnp.where(causal, s, -jnp.inf)
                m = jnp.max(s, axis=-1, keepdims=True)
                e = jnp.exp(s - m)
                l = jnp.sum(e, axis=-1, keepdims=True)
                o = _dot(e.astype(BF16), vp) / l
                acc = jnp.where(low if hh == 0 else jnp.logical_not(low), o, acc)
                lse_ref[hh] = m + jnp.log(l)
            o_ref[...] = acc

        _for_each_query_tile(t // tq, tile)
        pl.when(step == grid[0] * grid[1] - 1)(finish)

    res = pl.pallas_call(
        body, name="mla_attn_fwd",
        grid=grid,
        in_specs=[pl.BlockSpec((tq, 2 * HEAD_PAD), lambda p, i: (i, p)),
                  pl.BlockSpec((t, 2 * HEAD_PAD), lambda p, i: (0, p)),
                  pl.BlockSpec((t, LANE), lambda p, i: (0, p))] + _hbm_specs(ns),
        out_specs=(pl.BlockSpec((tq, LANE), lambda p, i: (i, p)),
                   pl.BlockSpec((2, tq, 1), lambda p, i: (p, i, 0))) + tuple(_hbm_specs(ns)),
        out_shape=(jax.ShapeDtypeStruct((t, MLA_WIDTH), F32),
                   jax.ShapeDtypeStruct((MLA_HEADS, t, 1), F32)) + tuple(g_shapes),
        scratch_shapes=g_sems,
        compiler_params=_cparams(("arbitrary", "arbitrary")),
    )(q, k, v, *shards)
    return res[0], res[1], res[2:]


def _mla_attn_bwd(q, k, v, o, do, lse, after):
    t = q.shape[0]
    tq = min(256, t)
    scale = MLA_QK ** -0.5

    def body(q_ref, k_ref, v_ref, o_ref, do_ref, lse_ref, after_ref, dq_ref, dk_ref, dv_ref):
        del after_ref

        @pl.when(pl.program_id(1) == 0)
        def _():
            dk_ref[...] = jnp.zeros_like(dk_ref)
            dv_ref[...] = jnp.zeros_like(dv_ref)

        def tile(i):
            keys = (i + 1) * tq
            causal, low = _attn_masks(tq, i)
            vp = v_ref[0:keys, :]
            do_all = do_ref[...]
            o_all = o_ref[...]
            dv_acc = jnp.zeros((keys, LANE), F32)
            for hh in range(2):
                sl = slice(hh * HEAD_PAD, (hh + 1) * HEAD_PAD)
                do_h = jnp.where(low if hh == 0 else jnp.logical_not(low), do_all, 0.0)
                dsum = jnp.sum(do_h * o_all, axis=-1, keepdims=True)
                qh = q_ref[:, sl]
                kh = k_ref[0:keys, sl]
                s = _dot_nt(qh, kh) * scale
                p = jnp.where(causal, jnp.exp(s - lse_ref[hh]), 0.0)
                do_b = do_h.astype(BF16)
                dp = _dot_nt(do_b, vp)
                ds = (p * (dp - dsum) * scale).astype(BF16)
                dq_ref[:, sl] = _dot(ds, kh).astype(BF16)
                dk_ref[0:keys, sl] += _dot_tn(ds, qh)
                dv_acc = dv_acc + _dot_tn(p.astype(BF16), do_b)
            dv_ref[0:keys, :] += dv_acc

        _for_each_query_tile(t // tq, tile)

    return pl.pallas_call(
        body, name="mla_attn_bwd",
        grid=(MLA_HEADS // 2, t // tq),
        in_specs=[pl.BlockSpec((tq, 2 * HEAD_PAD), lambda p, i: (i, p)),
                  pl.BlockSpec((t, 2 * HEAD_PAD), lambda p, i: (0, p)),
                  pl.BlockSpec((t, LANE), lambda p, i: (0, p)),
                  pl.BlockSpec((tq, LANE), lambda p, i: (i, p)),
                  pl.BlockSpec((tq, LANE), lambda p, i: (i, p)),
                  pl.BlockSpec((2, tq, 1), lambda p, i: (p, i, 0)),
                  pl.BlockSpec(memory_space=pl.ANY)],
        out_specs=(pl.BlockSpec((tq, 2 * HEAD_PAD), lambda p, i: (i, p)),
                   pl.BlockSpec((t, 2 * HEAD_PAD), lambda p, i: (0, p)),
                   pl.BlockSpec((t, LANE), lambda p, i: (0, p))),
        out_shape=(jax.ShapeDtypeStruct((t, MLA_HEADS * HEAD_PAD), BF16),
                   jax.ShapeDtypeStruct((t, MLA_HEADS * HEAD_PAD), F32),
                   jax.ShapeDtypeStruct((t, MLA_WIDTH), F32)),
        compiler_params=_cparams(("arbitrary", "arbitrary")),
    )(q, k, v, o, do, lse, after)


def _chunk_tri(n, lower):
    r = lax.broadcasted_iota(jnp.int32, (n, n), 0)
    c = lax.broadcasted_iota(jnp.int32, (n, n), 1)
    same = (r // GLA_CHUNK) == (c // GLA_CHUNK)
    return jnp.where(jnp.logical_and(same, r >= c if lower else r <= c), 1.0, 0.0).astype(F32)


def _gla_chunk_terms(q_ref, k_ref, b_ref, h, rows):
    sl = slice(h * GLA_HK, (h + 1) * GLA_HK)
    b = b_ref[rows, sl]
    bl = b[GLA_CHUNK - 1:GLA_CHUNK, :]
    kc = k_ref[rows, sl]
    q_in = (q_ref[rows, sl] * (GLA_HK ** -0.5)) * jnp.exp(b)
    k_in = kc * jnp.exp(-b)
    k_st = kc * jnp.exp(bl - b)
    return b, bl, q_in, k_in, k_st


def _tri(c, lower):
    r = lax.broadcasted_iota(jnp.int32, (c, c), 0)
    cc = lax.broadcasted_iota(jnp.int32, (c, c), 1)
    return jnp.where(r >= cc if lower else r <= cc, 1.0, 0.0).astype(F32)


def _gla_fwd(proj, log_a):
    t = proj.shape[0]
    per = GLA_CHUNKS_PER_STEP
    n = t // GLA_CHUNK
    c = GLA_CHUNK * per

    def body(q_ref, k_ref, v_ref, la_ref, o_ref, sp_ref, st_ref):
        @pl.when(pl.program_id(0) == 0)
        def _():
            st_ref[...] = jnp.zeros_like(st_ref)

        tri = _tri(GLA_CHUNK, True)
        for s, h in [(s, h) for s in range(per) for h in range(GLA_HEADS)]:
            rows = slice(s * GLA_CHUNK, (s + 1) * GLA_CHUNK)
            _, bl, q_in, k_in, k_st = _gla_chunk_terms(q_ref, k_ref, la_ref, h, rows)
            vs = slice(h * GLA_HV, (h + 1) * GLA_HV)
            vv = v_ref[rows, vs].astype(BF16)
            qb = q_in.astype(BF16)
            attn = _dot_nt(qb, k_in.astype(BF16)) * tri
            st = st_ref[h]
            sp_ref[s, h] = st
            o_ref[rows, vs] = _dot(attn.astype(BF16), vv) + _dot_nt(qb, st.astype(BF16))
            st_ref[h] = st * jnp.exp(bl) + _dot_tn(vv, k_st.astype(BF16))

    return pl.pallas_call(
        body, name="gla_fwd",
        grid=(n // per,),
        in_specs=[pl.BlockSpec((c, GLA_DK), lambda i: (i, P_QG // GLA_DK)),
                  pl.BlockSpec((c, GLA_DK), lambda i: (i, P_KG // GLA_DK)),
                  pl.BlockSpec((c, GLA_DV), lambda i: (i, P_VG // GLA_DV)),
                  pl.BlockSpec((c, GLA_DK), lambda i: (i, 0))],
        out_specs=(pl.BlockSpec((c, GLA_DV), lambda i: (i, 0)),
                   pl.BlockSpec((per, GLA_HEADS, GLA_HV, GLA_HK), lambda i: (i, 0, 0, 0))),
        out_shape=(jax.ShapeDtypeStruct((t, GLA_DV), F32),
                   jax.ShapeDtypeStruct((n, GLA_HEADS, GLA_HV, GLA_HK), F32)),
        scratch_shapes=[pltpu.VMEM((GLA_HEADS, GLA_HV, GLA_HK), F32)],
        compiler_params=_cparams(("arbitrary",)),
    )(proj, proj, proj, log_a)


def _gla_bwd(proj, log_a, do, states, after):
    t = proj.shape[0]
    per = GLA_CHUNKS_PER_STEP
    c = GLA_CHUNK * per
    n = t // c

    def body(q_ref, k_ref, v_ref, la_ref, do_ref, sp_ref, after_ref, dg_ref, dla_ref, ds_ref):
        del after_ref

        @pl.when(pl.program_id(0) == 0)
        def _():
            ds_ref[...] = jnp.zeros_like(ds_ref)

        tri = _tri(GLA_CHUNK, True)
        last = lax.broadcasted_iota(jnp.int32, (GLA_CHUNK, GLA_HK), 0) == GLA_CHUNK - 1
        for s, h in [(s, h) for s in reversed(range(per)) for h in range(GLA_HEADS)]:
            rows = slice(s * GLA_CHUNK, (s + 1) * GLA_CHUNK)
            b, bl, q_in, k_in, k_st = _gla_chunk_terms(q_ref, k_ref, la_ref, h, rows)
            ks_ = slice(h * GLA_HK, (h + 1) * GLA_HK)
            vs = slice(h * GLA_HV, (h + 1) * GLA_HV)
            vv = v_ref[rows, vs].astype(BF16)
            do_h = do_ref[rows, vs]
            qb, kb, ksb = q_in.astype(BF16), k_in.astype(BF16), k_st.astype(BF16)
            attn = (_dot_nt(qb, kb) * tri).astype(BF16)
            st = sp_ref[s, h]
            dst = ds_ref[h]
            dstb = dst.astype(BF16)
            dattn = (_dot_nt(do_h, vv) * tri).astype(BF16)
            dg_ref[rows, P_VG + h * GLA_HV:P_VG + (h + 1) * GLA_HV] = (
                _dot_tn(attn, do_h) + _dot_nt(ksb, dstb)).astype(BF16)
            dq_in = _dot(dattn, kb) + _dot(do_h, st.astype(BF16))
            dk_in = _dot_tn(dattn, qb)
            dk_st = _dot(vv, dstb)
            ebl = jnp.exp(bl)
            d_ebl = jnp.sum(st * dst, axis=0, keepdims=True)
            ds_ref[h] = _dot_tn(do_h, qb) + dst * ebl
            dg_ref[rows, P_QG + h * GLA_HK:P_QG + (h + 1) * GLA_HK] = (
                dq_in * (GLA_HK ** -0.5) * jnp.exp(b)).astype(BF16)
            dg_ref[rows, P_KG + h * GLA_HK:P_KG + (h + 1) * GLA_HK] = (
                dk_in * jnp.exp(-b) + dk_st * jnp.exp(bl - b)).astype(BF16)
            db = dq_in * q_in - dk_in * k_in - dk_st * k_st
            dbl = jnp.sum(dk_st * k_st, axis=0, keepdims=True) + d_ebl * ebl
            dla_ref[rows, ks_] = db + jnp.where(last, dbl, 0.0)

    rev = lambda i: n - 1 - i
    gw = P_GROUPS[0][1]
    return pl.pallas_call(
        body, name="gla_bwd",
        grid=(n,),
        in_specs=[pl.BlockSpec((c, GLA_DK), lambda i: (rev(i), P_QG // GLA_DK)),
                  pl.BlockSpec((c, GLA_DK), lambda i: (rev(i), P_KG // GLA_DK)),
                  pl.BlockSpec((c, GLA_DV), lambda i: (rev(i), P_VG // GLA_DV)),
                  pl.BlockSpec((c, GLA_DK), lambda i: (rev(i), 0)),
                  pl.BlockSpec((c, GLA_DV), lambda i: (rev(i), 0)),
                  pl.BlockSpec((per, GLA_HEADS, GLA_HV, GLA_HK), lambda i: (rev(i), 0, 0, 0)),
                  pl.BlockSpec(memory_space=pl.ANY)],
        out_specs=(pl.BlockSpec((c, gw), lambda i: (rev(i), 0)),
                   pl.BlockSpec((c, GLA_DK), lambda i: (rev(i), 0))),
        out_shape=(jax.ShapeDtypeStruct((t, gw), BF16), jax.ShapeDtypeStruct((t, GLA_DK), F32)),
        scratch_shapes=[pltpu.VMEM((GLA_HEADS, GLA_HV, GLA_HK), F32)],
        compiler_params=_cparams(("arbitrary",)),
    )(proj, proj, proj, log_a, do, states, after)


def _post(o_mla, proj, o_gla, x, target, g_gla, g_final, w_pm, w_pg, w_o):
    t = x.shape[0]
    tm = min(256, t)
    g0, gw = P_GROUPS[1]

    def body(om_ref, zg_ref, gm_ref, gg_ref, zm_ref, og_ref, x_ref, tg_ref, ggla_ref, gf_ref,
             wpm_ref, wpg_ref, wo_ref,
             dx2_ref, dom_ref, dog_ref, dg_ref,
             mg_ref, um_ref, ug_ref, dym_ref, dyg_ref, loss_ref, dgf_ref, dggla_ref):
        @pl.when(pl.program_id(0) == 0)
        def _():
            loss_ref[...] = jnp.zeros_like(loss_ref)
            dgf_ref[...] = jnp.zeros_like(dgf_ref)
            dggla_ref[...] = jnp.zeros_like(dggla_ref)

        om = om_ref[...]
        zm = zm_ref[...]
        sm = _sigmoid(zm)
        silu_m = zm * sm
        um = (om * silu_m).astype(BF16)
        um_ref[...] = um
        ym = _dot(um, wpm_ref[...])

        ggla = ggla_ref[...]
        zg = zg_ref[...]
        sg = _sigmoid(zg)
        silu_g = zg * sg
        xhat, rstd, on = [], [], []
        for h in range(GLA_HEADS):
            blk = og_ref[:, h * GLA_HV:(h + 1) * GLA_HV]
            r = lax.rsqrt(jnp.mean(blk * blk, axis=-1, keepdims=True) + EPS)
            xhat.append(blk * r)
            rstd.append(r)
            on.append(xhat[h] * ggla)
        on = jnp.concatenate(on, axis=-1)
        ug = (on * silu_g).astype(BF16)
        ug_ref[...] = ug
        yg = _dot(ug, wpg_ref[...])

        sgm = _sigmoid(gm_ref[...])
        sgg = _sigmoid(gg_ref[...])
        merged = (sgm * ym + sgg * yg).astype(BF16)
        mg_ref[...] = merged
        x2 = x_ref[...] + _dot(merged, wo_ref[...])
        gf = gf_ref[...]
        rf = lax.rsqrt(jnp.mean(x2 * x2, axis=-1, keepdims=True) + EPS)
        xh = x2 * rf
        err = xh * gf - tg_ref[...]
        loss_ref[...] += 0.5 * jnp.sum(jnp.mean(err * err, axis=-1, keepdims=True))

        dy = err * (1.0 / D_MODEL)
        dgf_ref[...] += jnp.sum(dy * xh, axis=0, keepdims=True)
        dxh = dy * gf
        dx2 = rf * (dxh - xh * jnp.mean(dxh * xh, axis=-1, keepdims=True))
        dx2_ref[...] = dx2
        dmerged = _dot_nt(dx2.astype(BF16), wo_ref[...])
        dym = (dmerged * sgm).astype(BF16)
        dyg = (dmerged * sgg).astype(BF16)
        dym_ref[...] = dym
        dyg_ref[...] = dyg
        dg_ref[:, P_GMLA - g0:P_GMLA - g0 + D_MODEL] = (dmerged * ym * sgm * (1.0 - sgm)).astype(BF16)
        dg_ref[:, P_GGLA - g0:P_GGLA - g0 + D_MODEL] = (dmerged * yg * sgg * (1.0 - sgg)).astype(BF16)
        dum = _dot_nt(dym, wpm_ref[...])
        dom_ref[...] = dum * silu_m
        dg_ref[:, P_ZMLA - g0:P_ZMLA - g0 + MLA_WIDTH] = (
            dum * om * (sm * (1.0 + zm * (1.0 - sm)))).astype(BF16)
        dug = _dot_nt(dyg, wpg_ref[...])
        dg_ref[:, P_ZGLA - g0:P_ZGLA - g0 + GLA_DV] = (
            dug * on * (sg * (1.0 + zg * (1.0 - sg)))).astype(BF16)
        don = dug * silu_g
        dggla = jnp.zeros((1, GLA_HV), F32)
        for h in range(GLA_HEADS):
            hs = slice(h * GLA_HV, (h + 1) * GLA_HV)
            don_h = don[:, hs]
            dggla = dggla + jnp.sum(don_h * xhat[h], axis=0, keepdims=True)
            dxh_h = don_h * ggla
            dog_ref[:, hs] = (rstd[h] * (dxh_h - xhat[h] * jnp.mean(dxh_h * xhat[h], axis=-1,
                                                                     keepdims=True))).astype(BF16)
        dggla_ref[...] += dggla

    row = lambda w: pl.BlockSpec((tm, w), lambda i: (i, 0))
    pcol = lambda w, off: pl.BlockSpec((tm, w), lambda i: (i, _rel(off) // w))
    full = lambda a: pl.BlockSpec(a.shape, lambda i: (0, 0))
    sds = jax.ShapeDtypeStruct
    return pl.pallas_call(
        body, name="post_fwd_bwd",
        grid=(t // tm,),
        in_specs=[row(MLA_WIDTH), pcol(GLA_DV, P_ZGLA), pcol(D_MODEL, P_GMLA), pcol(D_MODEL, P_GGLA),
                  pcol(MLA_WIDTH, P_ZMLA), row(GLA_DV), row(D_MODEL), row(D_MODEL),
                  full(g_gla), full(g_final), full(w_pm), full(w_pg), full(w_o)],
        out_specs=(row(D_MODEL), row(MLA_WIDTH), row(GLA_DV), row(gw),
                   row(D_MODEL), row(MLA_WIDTH), row(GLA_DV), row(D_MODEL), row(D_MODEL),
                   pl.BlockSpec((1, LANE), lambda i: (0, 0)),
                   pl.BlockSpec((1, D_MODEL), lambda i: (0, 0)),
                   pl.BlockSpec((1, GLA_HV), lambda i: (0, 0))),
        out_shape=(sds((t, D_MODEL), F32), sds((t, MLA_WIDTH), F32), sds((t, GLA_DV), BF16),
                   sds((t, gw), BF16),
                   sds((t, D_MODEL), BF16), sds((t, MLA_WIDTH), BF16), sds((t, GLA_DV), BF16),
                   sds((t, D_MODEL), BF16), sds((t, D_MODEL), BF16),
                   sds((1, LANE), F32), sds((1, D_MODEL), F32), sds((1, GLA_HV), F32)),
        compiler_params=_cparams(("arbitrary",)),
    )(o_mla, proj, proj, proj, proj, o_gla, x, target, g_gla, g_final, w_pm, w_pg, w_o)


def _mla_prep_bwd(dq, dk, dv, dla, pre, proj, rq, rkv, g_q, g_kv, w_uq_p, w_k_p, w_v, w_gate_p,
                  rc, rsn, rsp):
    t = proj.shape[0]
    tm = min(512, t)
    gw = P_GROUPS[2][1]

    def body(dq_ref, dk_ref, dv_ref, dla_ref, pre_ref, cq_ref, ckv_ref, rq_ref, rkv_ref,
             gq_ref, gkv_ref, wuq_ref, wk_ref, wv_ref, wg_ref, c_ref, sn_ref, sp_ref,
             dg_ref, dqpre_ref, dpre_ref, dgq_ref, dgkv_ref, dbg_ref):
        @pl.when(pl.program_id(0) == 0)
        def _():
            dgq_ref[...] = jnp.zeros_like(dgq_ref)
            dgkv_ref[...] = jnp.zeros_like(dgkv_ref)
            dbg_ref[...] = jnp.zeros_like(dbg_ref)

        c, sn, sp = c_ref[...], sn_ref[...], sp_ref[...]
        dkr = jnp.zeros((tm, LANE), F32)
        for h in range(MLA_HEADS):
            sl = slice(h * HEAD_PAD, (h + 1) * HEAD_PAD)
            dqpre_ref[:, sl] = _rope_bwd(dq_ref[:, sl].astype(F32), c, sn, sp).astype(BF16)
            dkr = dkr + dk_ref[:, sl]
        dcqn = _dot_nt(dqpre_ref[...], wuq_ref[...])
        rq = rq_ref[...]
        xh = cq_ref[:, :MLA_Q_RANK] * rq
        dgq_ref[...] += jnp.sum(dcqn * xh, axis=0, keepdims=True)
        dxh = dcqn * gq_ref[...]
        dcq = rq * (dxh - xh * jnp.mean(dxh * xh, axis=-1, keepdims=True))
        dg_ref[:, :MLA_Q_RANK] = dcq.astype(BF16)
        dg_ref[:, MLA_Q_RANK:512] = jnp.zeros((tm, 512 - MLA_Q_RANK), BF16)

        dckvn = _dot_nt(dk_ref[...].astype(BF16), wk_ref[...]) + \
            _dot_nt(dv_ref[...].astype(BF16), wv_ref[...])
        rkv = rkv_ref[...]
        xh = ckv_ref[...] * rkv
        dgkv_ref[...] += jnp.sum(dckvn * xh, axis=0, keepdims=True)
        dxh = dckvn * gkv_ref[...]
        dg_ref[:, P_CKV - P_CQ:P_CKV - P_CQ + MLA_KV_RANK] = (
            rkv * (dxh - xh * jnp.mean(dxh * xh, axis=-1, keepdims=True))).astype(BF16)

        dlog_a = _dot_exact(_chunk_tri(tm, False), dla_ref[...])
        dpre = dlog_a * (1.0 / GLA_GATE_NORM) * (1.0 - _sigmoid(pre_ref[...]))
        dbg_ref[...] += jnp.sum(dpre, axis=0, keepdims=True)
        dpre = dpre.astype(BF16)
        dpre_ref[...] = dpre
        lane = lax.broadcasted_iota(jnp.int32, (tm, LANE), 1)
        in_kr = jnp.logical_and(lane >= MISC_KR, lane < MISC_KR + MLA_ROPE)
        dmisc = jnp.where(in_kr, _rope_bwd(dkr, c, sn, sp), 0.0) + _dot_nt(dpre, wg_ref[...])
        dg_ref[:, P_MISC - P_CQ:P_MISC - P_CQ + LANE] = dmisc.astype(BF16)

    hq = MLA_HEADS * HEAD_PAD
    row = lambda w: pl.BlockSpec((tm, w), lambda i: (i, 0))
    full = lambda a: pl.BlockSpec(a.shape, lambda i: (0, 0))
    acc = lambda w: pl.BlockSpec((1, w), lambda i: (0, 0))
    sds = jax.ShapeDtypeStruct
    return pl.pallas_call(
        body, name="mla_prep_bwd",
        grid=(t // tm,),
        in_specs=[row(hq), row(hq), row(MLA_WIDTH), row(GLA_DK), row(GLA_DK),
                  pl.BlockSpec((tm, 512), lambda i: (i, _rel(P_CQ) // 512)),
                  pl.BlockSpec((tm, MLA_KV_RANK), lambda i: (i, _rel(P_CKV) // MLA_KV_RANK)),
                  row(1), row(1), full(g_q), full(g_kv), full(w_uq_p), full(w_k_p), full(w_v),
                  full(w_gate_p), row(LANE), row(LANE), row(LANE)],
        out_specs=(row(gw), row(hq), row(GLA_DK),
                   acc(MLA_Q_RANK), acc(MLA_KV_RANK), acc(GLA_DK)),
        out_shape=(sds((t, gw), BF16), sds((t, hq), BF16), sds((t, GLA_DK), BF16),
                   sds((1, MLA_Q_RANK), F32), sds((1, MLA_KV_RANK), F32), sds((1, GLA_DK), F32)),
        compiler_params=_cparams(("arbitrary",)),
    )(dq, dk, dv, dla, pre, proj, proj, rq, rkv, g_q, g_kv, w_uq_p, w_k_p, w_v, w_gate_p,
      rc, rsn, rsp)


def _inproj_bwd(dgroups, w_pts, x, rstd, g_in, dx2, after):
    t = x.shape[0]
    tm = min(512, t)

    def body(d0_ref, d1_ref, d2_ref, w0_ref, w1_ref, w2_ref, x_ref, r_ref, g_ref, dx2_ref, after_ref,
             dx_ref, dg_ref):
        del after_ref

        @pl.when(pl.program_id(0) == 0)
        def _():
            dg_ref[...] = jnp.zeros_like(dg_ref)

        dh = jnp.zeros((tm, D_MODEL), F32)
        for d_ref, w_ref in zip((d0_ref, d1_ref, d2_ref), (w0_ref, w1_ref, w2_ref)):
            dh = dh + _dot(d_ref[...], w_ref[...])
        r = r_ref[...]
        xh = x_ref[...] * r
        dg_ref[...] += jnp.sum(dh * xh, axis=0, keepdims=True)
        dxh = dh * g_ref[...]
        dx_ref[...] = dx2_ref[...] + r * (dxh - xh * jnp.mean(dxh * xh, axis=-1, keepdims=True))

    row = lambda w: pl.BlockSpec((tm, w), lambda i: (i, 0))
    return pl.pallas_call(
        body, name="inproj_bwd",
        grid=(t // tm,),
        in_specs=[row(w) for _, w in P_GROUPS]
        + [pl.BlockSpec((w, D_MODEL), lambda i: (0, 0)) for _, w in P_GROUPS]
        + [row(D_MODEL), row(1), pl.BlockSpec((1, D_MODEL), lambda i: (0, 0)), row(D_MODEL),
           pl.BlockSpec(memory_space=pl.ANY)],
        out_specs=(row(D_MODEL), pl.BlockSpec((1, D_MODEL), lambda i: (0, 0))),
        out_shape=(jax.ShapeDtypeStruct((t, D_MODEL), F32),
                   jax.ShapeDtypeStruct((1, D_MODEL), F32)),
        compiler_params=_cparams(("arbitrary",), VMEM_LIMIT_LARGE),
    )(*dgroups, *w_pts, x, rstd, g_in, dx2, after)


def _matmul(name, a, b, tm, tn, dtype=F32, after=None):
    kk, m = a.shape
    n = b.shape[1]
    extra = [] if after is None else [after]

    def body(a_ref, b_ref, *rest):
        rest[-1][...] = _dot_tn(a_ref[...].astype(BF16), b_ref[...].astype(BF16)).astype(dtype)

    return pl.pallas_call(
        body, name=name,
        grid=(n // tn, m // tm),
        in_specs=[pl.BlockSpec((kk, tm), lambda j, i: (0, i)),
                  pl.BlockSpec((kk, tn), lambda j, i: (0, j))]
        + [pl.BlockSpec(memory_space=pl.ANY) for _ in extra],
        out_specs=pl.BlockSpec((tm, tn), lambda j, i: (i, j)),
        out_shape=jax.ShapeDtypeStruct((m, n), dtype),
        compiler_params=_cparams(("arbitrary", "arbitrary")),
    )(a, b, *extra)


def _adamw_update(part_refs, w_ref, m_ref, v_ref, g_ref, d_ref, nm_ref, nv_ref):
    g = part_refs[0][...].astype(F32)
    for p_ref in part_refs[1:]:
        g = g + p_ref[...].astype(F32)
    m_new = ADAM_B1 * m_ref[...] + (1.0 - ADAM_B1) * g
    v_new = ADAM_B2 * v_ref[...] + (1.0 - ADAM_B2) * (g * g)
    m_hat = m_new / (1.0 - ADAM_B1 ** ADAM_STEP)
    v_hat = v_new / (1.0 - ADAM_B2 ** ADAM_STEP)
    g_ref[...] = g
    nm_ref[...] = m_new
    nv_ref[...] = v_new
    d_ref[...] = -ADAM_LR * (m_hat / (jnp.sqrt(v_hat) + ADAM_EPS) + ADAM_WD * w_ref[...])


def _adamw_rows(name, parts, w, m, v, tr, first=None):
    _, rows, cols = w.shape
    slots = parts.shape[0]

    def body(*refs):
        lead_refs, p_ref = ([], refs[0]) if first is None else ([refs[0]], refs[1])
        _adamw_update(lead_refs + [p_ref.at[q] for q in range(slots)], *refs[len(lead_refs) + 1:])

    blk = pl.BlockSpec((None, tr, cols), lambda i: (0, i, 0))
    out = jax.ShapeDtypeStruct((1, rows, cols), F32)
    lead = [] if first is None else [pl.BlockSpec((tr, cols), lambda i: (i, 0))]
    return pl.pallas_call(
        body, name=name,
        grid=(rows // tr,),
        in_specs=lead + [pl.BlockSpec((slots, tr, cols), lambda i: (0, i, 0)), blk, blk, blk],
        out_specs=(blk, blk, blk, blk),
        out_shape=(out, out, out, out),
        compiler_params=_cparams(("arbitrary",)),
    )(*([] if first is None else [first]), parts, w, m, v)


def _adamw_transposed(name, first, parts, w, m, v, tl):
    _, rows, cols = w.shape
    slots, padded = parts.shape[:2]

    def body(f_ref, p_ref, *refs):
        _adamw_update([f_ref.at[pl.ds(0, cols)]]
                      + [p_ref.at[q, pl.ds(0, cols)] for q in range(slots)], *refs)

    blk = pl.BlockSpec((cols, None, tl), lambda i: (0, 0, i))
    out = jax.ShapeDtypeStruct((cols, 1, rows), F32)
    res = pl.pallas_call(
        body, name=name,
        grid=(rows // tl,),
        in_specs=[pl.BlockSpec((padded, tl), lambda i: (0, i)),
                  pl.BlockSpec((slots, padded, tl), lambda i: (0, 0, i)), blk, blk, blk],
        out_specs=(blk, blk, blk, blk),
        out_shape=(out, out, out, out),
        compiler_params=_cparams(("arbitrary",)),
    )(first, parts, *[a.transpose(2, 0, 1) for a in (w, m, v)])
    return [r.transpose(1, 2, 0) for r in res]


def _adamw_group(firsts, parts, ws, ms, vs):
    n = len(ws)

    def body(*refs):
        ins, outs = refs[:5 * n], refs[5 * n:]
        x, y, c = _mesh_pos()
        for a in range(n):
            _adamw_update([ins[a].at[4 * x + 2 * y + c]]
                          + [ins[n + a].at[q] for q in range(ins[n + a].shape[0])],
                          *[r.at[0] for r in (ins[2 * n + a], ins[3 * n + a], ins[4 * n + a])],
                          *[r.at[0] for r in outs[4 * a:4 * a + 4]])

    vmem = lambda k: [pl.BlockSpec(memory_space=pltpu.VMEM) for _ in range(k)]
    out_shape = []
    for w in ws:
        out_shape += [jax.ShapeDtypeStruct(w.shape, F32)] * 4
    res = pl.pallas_call(
        body, name="adamw_small_weights",
        in_specs=vmem(5 * n), out_specs=tuple(vmem(4 * n)), out_shape=tuple(out_shape),
        compiler_params=_cparams(),
    )(*firsts, *parts, *ws, *ms, *vs)
    return [res[4 * a:4 * a + 4] for a in range(n)]


def _rope_tables(positions):
    half = MLA_ROPE // 2
    freqs = ROPE_THETA ** (-jnp.arange(half, dtype=F32) / half)
    ang = positions.astype(F32).reshape(-1, 1) * freqs
    cos, sin = jnp.cos(ang), jnp.sin(ang)
    t = ang.shape[0]
    one, zero = jnp.ones((t, MLA_NOPE), F32), jnp.zeros((t, half), F32)
    tail = jnp.zeros((t, LANE - MLA_QK), F32)
    rc = jnp.concatenate([one, cos, cos, tail], axis=1)
    rsn = jnp.concatenate([0.0 * one, -sin, zero, tail], axis=1)
    rsp = jnp.concatenate([0.0 * one, zero, sin, tail], axis=1)
    return rc, rsn, rsp


def _cols_full(g):
    return g.transpose(1, 0, 2)


def kernel(x, positions, g_in, w_in, g_q, w_uq, g_kv, w_ukv, w_gla_gate, b_gla_gate, g_gla, w_proj_mla, w_proj_gla, w_out, g_final, loss_target, m_g_in, m_w_in, m_g_q, m_w_uq, m_g_kv, m_w_ukv, m_w_gla_gate, m_b_gla_gate, m_g_gla, m_w_proj_mla, m_w_proj_gla, m_w_out, m_g_final, v_g_in, v_w_in, v_g_q, v_w_uq, v_g_kv, v_w_ukv, v_w_gla_gate, v_b_gla_gate, v_g_gla, v_w_proj_mla, v_w_proj_gla, v_w_out, v_g_final):
    t = x.shape[1]
    x2d = x.reshape(t, D_MODEL)
    tgt = loss_target.reshape(t, D_MODEL)
    g_final2 = g_final.reshape(1, D_MODEL)
    sharded = [(w_in, m_w_in, v_w_in), (w_uq, m_w_uq, v_w_uq), (w_ukv, m_w_ukv, v_w_ukv),
               (w_gla_gate, m_w_gla_gate, v_w_gla_gate), (w_proj_mla, m_w_proj_mla, v_w_proj_mla),
               (w_proj_gla, m_w_proj_gla, v_w_proj_gla), (w_out, m_w_out, v_w_out)]

    w_in_t = w_in.transpose(2, 0, 1).reshape(SHARD_COLS, D_MODEL)
    everyone = tuple(range(N_DEV))
    w_in_b = w_in_t.astype(BF16)
    b_uq, b_ukv, b_gate, b_pm, b_pg, b_o = [s[0][0].astype(BF16) for s in sharded[1:]]
    stages = ((0, 2, 4, 6), (1, 3, 5, 7))
    where = {d: (k, i) for k, srcs in enumerate(stages) for i, d in enumerate(srcs)}
    g_in_1, g_uq, g_ukv, g_gate = _all_gather(
        "all_gather_first", [w_in_b, b_uq, b_ukv, b_gate], [stages[0]] + [everyone] * 3)
    w_uq_p = jnp.pad(_cols_full(g_uq), ((0, 0), (0, 0), (0, HEAD_PAD - MLA_QK))).reshape(
        MLA_Q_RANK, MLA_HEADS * HEAD_PAD)
    ukv = _cols_full(g_ukv)
    w_k_p = jnp.pad(ukv[:, :, :MLA_NOPE], ((0, 0), (0, 0), (0, HEAD_PAD - MLA_NOPE))).reshape(
        MLA_KV_RANK, MLA_HEADS * HEAD_PAD)
    w_v = ukv[:, :, MLA_NOPE:].reshape(MLA_KV_RANK, MLA_WIDTH)
    w_gate_p = jnp.pad(_cols_full(g_gate).reshape(GLA_GATE_RANK, GLA_DK),
                       ((MISC_ALR, LANE - MISC_ALR - GLA_GATE_RANK), (0, 0)))
    rc, rsn, rsp = _rope_tables(positions)

    w_lat = _weights_to_p("weights_latents", [g_in_1], where, 2)
    proj_lat, h, rstd = _inproj(x2d, g_in, w_lat)
    q, k, v, log_a, pre, cqn, ckvn, rq, rkv, misc = _mla_prep(
        proj_lat, g_q, g_kv, w_uq_p, w_k_p, w_v, w_gate_p, b_gla_gate, rc, rsn, rsp)
    o_mla, lse, (g_in_2, g_pm, g_pg, g_o) = _mla_attn_fwd(
        q, k, v, [w_in_b, b_pm, b_pg, b_o], [stages[1]] + [everyone] * 3)
    w_gla = _weights_to_p("weights_gla", [g_in_1, g_in_2], where, 0)
    proj_gla = _proj("inproj_gla", h, w_gla)
    o_gla, states = _gla_fwd(proj_gla, log_a)
    w_out_path = _weights_to_p("weights_out_path", [g_in_1, g_in_2], where, 1)
    proj_out = _proj("inproj_out_path", h, w_out_path)
    w_in_p = (w_gla, w_out_path, w_lat)
    w_pm = _cols_full(g_pm).reshape(MLA_WIDTH, D_MODEL)
    w_pg = g_pg.reshape(GLA_DV, D_MODEL)
    w_o = g_o.reshape(D_MODEL, D_MODEL)

    (dx2, do_mla, do_gla, d_out, merged, um, ug, dym, dyg, loss_p, dg_final,
     dg_gla) = _post(o_mla, proj_out, o_gla, x2d, tgt, g_gla, g_final2, w_pm, w_pg, w_o)

    p_pm = _matmul("dw_proj_mla", um, dym, 512, D_MODEL, BF16).reshape(
        MLA_WIDTH, N_DEV, D_MODEL // N_DEV).transpose(1, 0, 2)
    p_pg = _matmul("dw_proj_gla", ug, dyg, 512, D_MODEL, BF16).reshape(N_DEV, -1, D_MODEL)
    p_o = _matmul("dw_out", merged, dx2, 512, D_MODEL, BF16).reshape(N_DEV, -1, D_MODEL)
    own_in = jnp.zeros((SHARD_PAD, D_MODEL), BF16)
    land_in = lax.empty((PEERS, SHARD_PAD, D_MODEL), BF16)
    dw_groups, started, lands = {}, [], [land_in]

    def reduce_scatter_stage(s, dests, own_in, extra=()):
        parts_in, own_in = _grads_to_shards("grads_to_shards_%d" % s, dw_groups, dests, own_in)
        first = len(lands)
        lands.extend(lax.empty((PEERS,) + p.shape[1:], BF16) for p in extra)
        idx = [0] + list(range(first, len(lands)))
        all_dests = [[(i, d, r0, r1) for i, (d, ranges) in enumerate(dests) for r0, r1 in ranges]]
        all_dests += [_whole(everyone, p.shape[1]) for p in extra]
        sems, parts, new_lands, token = _ici_start(
            "ici_start_%d" % s, [parts_in] + list(extra), [lands[i] for i in idx], all_dests)
        for a, i in enumerate(idx):
            lands[i] = new_lands[a]
            started.append((sems[a][0], sems[a][1], parts[a], i, all_dests[a]))
        return own_in, token

    def late_small_stage(arrays):
        idx = list(range(len(lands), len(lands) + len(arrays)))
        lands.extend(lax.empty((PEERS,) + p.shape[1:], BF16) for p in arrays)
        all_dests = [_whole(everyone, p.shape[1]) for p in arrays]
        sems, parts, new_lands, token = _ici_start(
            "ici_start_4", list(arrays), [lands[i] for i in idx], all_dests)
        for a, i in enumerate(idx):
            lands[i] = new_lands[a]
            started.append((sems[a][0], sems[a][1], parts[a], i, all_dests[a]))
        return token

    dw_groups[1] = _matmul("dw_in_1", d_out, h, 512, D_MODEL, BF16)
    full = [(0, SHARD_PAD)]
    own_in, token = reduce_scatter_stage(
        1, [(5, full), (6, full), (7, full), (0, [(672, SHARD_PAD)]), (1, [(0, 384)]),
            (4, [(96, SHARD_PAD)])], own_in, (p_pm, p_pg, p_o))
    d_gla, dla = _gla_bwd(proj_gla, log_a, do_gla, states, token)
    dw_groups[0] = _matmul("dw_in_0", d_gla, h, 512, D_MODEL, BF16)
    own_in, token = reduce_scatter_stage(
        2, [(1, [(384, SHARD_PAD)]), (2, full), (3, full), (4, [(0, 64)])], own_in)
    dq, dk, dv = _mla_attn_bwd(q, k, v, o_mla, do_mla, lse, token)
    d_lat, dqpre, dpre, dg_q, dg_kv, db_gate = _mla_prep_bwd(
        dq, dk, dv, dla, pre, proj_lat, rq, rkv, g_q, g_kv, w_uq_p, w_k_p, w_v, w_gate_p, rc, rsn, rsp)
    dw_groups[2] = _matmul("dw_in_2", d_lat, h, 896, D_MODEL, BF16)
    own_in, token = reduce_scatter_stage(3, [(0, [(0, 672)]), (4, [(64, 96)])], own_in)
    dw_uq = _matmul("dw_uq", cqn, dqpre, MLA_Q_RANK, D_MODEL, BF16, after=token)
    p_uq = dw_uq.reshape(MLA_Q_RANK, MLA_HEADS, HEAD_PAD)[:, :, :MLA_QK].transpose(1, 0, 2)
    dw_k = _matmul("dw_uk", ckvn, dk, MLA_KV_RANK, D_MODEL, BF16)
    dw_v = _matmul("dw_uv", ckvn, dv, MLA_KV_RANK, 512, BF16)
    p_ukv = jnp.concatenate(
        [dw_k.reshape(MLA_KV_RANK, MLA_HEADS, HEAD_PAD)[:, :, :MLA_NOPE],
         dw_v.reshape(MLA_KV_RANK, MLA_HEADS, MLA_VDIM)], axis=2).transpose(1, 0, 2)
    dw_gate = _matmul("dw_gate", misc, dpre, LANE, 512, BF16)
    p_gate = dw_gate[MISC_ALR:MISC_ALR + GLA_GATE_RANK].reshape(
        GLA_GATE_RANK, N_DEV, GLA_DK // N_DEV).transpose(1, 0, 2)
    token = late_small_stage((p_uq, p_ukv, p_gate))
    grad_x, dg_in = _inproj_bwd((d_gla, d_out, d_lat), w_in_p, x2d, rstd, g_in, dx2, token)
    small = jnp.concatenate([dg_in.reshape(-1), dg_q.reshape(-1), dg_kv.reshape(-1),
                             db_gate.reshape(-1), dg_gla.reshape(-1), dg_final.reshape(-1),
                             loss_p[0, :1]])
    small = jnp.pad(small, (0, SMALL_ROWS * LANE - small.shape[0])).reshape(SMALL_ROWS, LANE)

    (small_all,) = _all_gather("all_gather_small", [small], [everyone])
    lands = _ici_wait("ici_wait", started, lands, small_all)
    big = [_adamw_transposed("adamw_w_in", own_in, lands[0], *sharded[0], 512)]
    big += _adamw_group([p_uq, p_ukv, p_gate, p_pm, p_pg, p_o], list(lands[4:7]) + list(lands[1:4]),
                        *[[s[j] for s in sharded[1:]] for j in range(3)])
    replicated = [(g_in, m_g_in, v_g_in), (g_q, m_g_q, v_g_q), (g_kv, m_g_kv, v_g_kv),
                  (b_gla_gate, m_b_gla_gate, v_b_gla_gate), (g_gla, m_g_gla, v_g_gla),
                  (g_final, m_g_final, v_g_final)]
    spacks = [jnp.pad(jnp.concatenate([s[j].reshape(-1) for s in replicated]),
                      (0, SMALL_ROWS * LANE - sum(SMALL_SIZES))).reshape(1, SMALL_ROWS, LANE)
              for j in range(3)]
    tiny = _adamw_rows("adamw_gains", small_all, spacks[0], spacks[1], spacks[2], SMALL_ROWS)

    outs = {}
    names = ("w_in", "w_uq", "w_ukv", "w_gla_gate", "w_proj_mla", "w_proj_gla", "w_out")
    for j, kind in enumerate(("grad", "delta", "new_m", "new_v")):
        for name, res in zip(names, big):
            outs[kind, name] = res[j]
        flat = tiny[j].reshape(-1)
        off = 0
        for name, size in zip(("g_in", "g_q", "g_kv", "b_gla_gate", "g_gla", "g_final"), SMALL_SIZES):
            shape = (size,) if name == "g_final" else (1, size)
            outs[kind, name] = flat[off:off + size].reshape(shape)
            off += size
    loss = tiny[0].reshape(-1)[sum(SMALL_SIZES)]
    order = ("g_in", "w_in", "g_q", "w_uq", "g_kv", "w_ukv", "w_gla_gate", "b_gla_gate", "g_gla",
             "w_proj_mla", "w_proj_gla", "w_out", "g_final")
    result = [loss, grad_x.reshape(1, t, D_MODEL)]
    for kind in ("grad", "delta", "new_m", "new_v"):
        result += [outs[kind, name] for name in order]
    return tuple(result)
```

```python
import jax
import jax.numpy as jnp
from jax import lax
from jax.experimental import pallas as pl
from jax.experimental.pallas import tpu as pltpu

F32 = jnp.float32
BF16 = jnp.bfloat16
MESH = pl.DeviceIdType.MESH
N_DEV = 8

D_MODEL = 1024
EPS = 1e-6
MLA_HEADS = 8
MLA_NOPE = 64
MLA_ROPE = 32
MLA_VDIM = 64
MLA_Q_RANK = 384
MLA_KV_RANK = 256
MLA_QK = MLA_NOPE + MLA_ROPE
MLA_WIDTH = MLA_HEADS * MLA_VDIM
ROPE_THETA = 10000.0
GLA_HEADS = 4
GLA_DK = 512
GLA_DV = 1024
GLA_HK = 128
GLA_HV = 256
GLA_GATE_RANK = 16
GLA_GATE_NORM = 16.0
GLA_CHUNK = 64
GLA_CHUNKS_PER_STEP = 8
D_IN = 6320

ADAM_LR = 0.001
ADAM_B1 = 0.9
ADAM_B2 = 0.999
ADAM_EPS = 1e-08
ADAM_WD = 0.01
ADAM_STEP = 10

LANE = 128
HEAD_PAD = 128
VMEM_LIMIT = 48 * 1024 * 1024

P_VG, P_QG, P_KG = 0, 1024, 1536
P_ZGLA, P_GMLA, P_GGLA, P_ZMLA = 2048, 3072, 4096, 5120
P_CQ, P_CKV, P_MISC = 5632, 6144, 6400
P_TOTAL = 6528
P_GROUPS = ((0, 2048), (2048, 3584), (5632, 896))
MISC_KR = 64
MISC_ALR = 96
SHARD_COLS = D_IN // N_DEV
SHARD_PAD = 800
P_COMPONENTS = ((0, 384, P_CQ), (384, 256, P_CKV), (640, 32, P_MISC + MISC_KR), (672, 512, P_ZMLA),
                (1184, 512, P_QG), (1696, 512, P_KG), (2208, 1024, P_VG),
                (3232, 16, P_MISC + MISC_ALR), (3248, 1024, P_ZGLA), (4272, 1024, P_GMLA),
                (5296, 1024, P_GGLA))

SMALL_SIZES = (1024, 384, 256, 512, 256, 1024)
SMALL_ROWS = 32


def _segments():
    segs = []
    for g0, n, p0 in P_COMPONENTS:
        g = g0
        while g < g0 + n:
            d = g // SHARD_COLS
            end = min(g0 + n, (d + 1) * SHARD_COLS)
            segs.append((d, g - d * SHARD_COLS, end - g, p0 + g - g0))
            g = end
    return segs


def _group_of(p0):
    return max(i for i, (off, _) in enumerate(P_GROUPS) if off <= p0)


def _rel(p0):
    return p0 - P_GROUPS[_group_of(p0)][0]


def _cparams(sem=None):
    if sem is None:
        return pltpu.CompilerParams(vmem_limit_bytes=VMEM_LIMIT)
    return pltpu.CompilerParams(dimension_semantics=sem, vmem_limit_bytes=VMEM_LIMIT)


def _sigmoid(v):
    return 1.0 / (1.0 + jnp.exp(-v))


def _dot(a, b):
    return jnp.dot(a, b, preferred_element_type=F32)


def _dot_nt(a, b):
    return lax.dot_general(a, b, (((1,), (1,)), ((), ())), preferred_element_type=F32)


def _dot_tn(a, b):
    return lax.dot_general(a, b, (((0,), (0,)), ((), ())), preferred_element_type=F32)


def _dot_exact(a, b):
    return jnp.dot(a, b, preferred_element_type=F32, precision=lax.Precision.HIGHEST)


def _rope_fwd(blk, c, sn, sp):
    return blk * c + pltpu.roll(blk, LANE - 16, 1) * sn + pltpu.roll(blk, 16, 1) * sp


def _rope_bwd(blk, c, sn, sp):
    return blk * c + pltpu.roll(blk * sn, 16, 1) + pltpu.roll(blk * sp, LANE - 16, 1)


def _mesh_pos():
    return lax.axis_index("x"), lax.axis_index("y"), lax.axis_index("c")


def _hbm_specs(n):
    return [pl.BlockSpec(memory_space=pltpu.HBM) for _ in range(n)]


def _dev(d):
    return d >> 2, (d >> 1) & 1, d & 1


def _gather_plan(shards, sources):
    na, most = len(shards), max(len(s) for s in sources)
    out_shape = [jax.ShapeDtypeStruct((len(srcs),) + s.shape, s.dtype)
                 for s, srcs in zip(shards, sources)]
    sems = [pltpu.SemaphoreType.DMA((na, most)) for _ in range(3)]
    sems += [pltpu.SemaphoreType.DMA((na, most, 3))]
    sems += [pltpu.SemaphoreType.DMA((na, most)) for _ in range(3)]
    return out_shape, sems


def _gather_hooks(x_refs, out_refs, sems, sources):
    local_sems, d2d_send, d2d_recv, ici_send, ici_recv, fwd_send, fwd_recv = sems
    x, y, c = _mesh_pos()
    chips = [(1 - x, y), (x, 1 - y), (1 - x, 1 - y)]
    items = []
    for a, srcs in enumerate(sources):
        for i, d in enumerate(srcs):
            dx, dy, dc = _dev(d)
            near = jnp.logical_and(x == dx, y == dy)
            far = jnp.logical_not(near)
            slot = out_refs[a].at[i]

            def remote(src, to, send_sem, recv_sem, slot=slot):
                return pltpu.make_async_remote_copy(
                    src_ref=src, dst_ref=slot, send_sem=send_sem, recv_sem=recv_sem,
                    device_id=to, device_id_type=MESH)

            items.append(dict(
                me=jnp.logical_and(near, c == dc), sibling=jnp.logical_and(near, c != dc),
                relay=jnp.logical_and(far, c == dc), behind=jnp.logical_and(far, c != dc),
                local=pltpu.make_async_copy(x_refs[a], slot, local_sems.at[a, i]),
                to_sibling=remote(x_refs[a], (x, y, 1 - c), d2d_send.at[a, i], d2d_recv.at[a, i]),
                to_chips=[remote(x_refs[a], (*chip, c), ici_send.at[a, i, j], ici_recv.at[a, i])
                          for j, chip in enumerate(chips)],
                forward=remote(slot, (x, y, 1 - c), fwd_send.at[a, i], fwd_recv.at[a, i])))

    def start():
        for it in items:
            @pl.when(it["me"])
            def _(it=it):
                it["local"].start()
                it["to_sibling"].start()
                for cp in it["to_chips"]:
                    cp.start()

    def relay():
        for it in items:
            @pl.when(it["relay"])
            def _(it=it):
                it["to_chips"][0].wait_recv()
                it["forward"].start()

    def finish():
        for it in items:
            pl.when(it["sibling"])(it["to_sibling"].wait_recv)
            pl.when(it["behind"])(it["forward"].wait_recv)
            pl.when(it["relay"])(it["forward"].wait_send)

            @pl.when(it["me"])
            def _(it=it):
                it["local"].wait()
                it["to_sibling"].wait_send()
                for cp in it["to_chips"]:
                    cp.wait_send()

    return start, relay, finish


def _all_gather(name, shards, sources):
    n = len(shards)
    out_shape, sems = _gather_plan(shards, sources)

    def body(*refs):
        start, relay, finish = _gather_hooks(refs[:n], refs[n:2 * n], refs[2 * n:], sources)
        start()
        relay()
        finish()

    return pl.pallas_call(
        body, name=name,
        out_shape=tuple(out_shape),
        in_specs=_hbm_specs(n), out_specs=tuple(_hbm_specs(n)),
        scratch_shapes=sems,
        compiler_params=_cparams(),
    )(*shards)


PEERS = N_DEV - 1


def _whole(dests, rows):
    return [(i, d, 0, rows) for i, d in enumerate(dests)]


def _ici_copies(p_ref, land_ref, send_sems, recv_sems, pieces):
    x, y, c = _mesh_pos()
    sends, arrivals = [], []

    def rows_of(ref, j, r0, r1):
        return ref.at[j] if (r0, r1) == (0, ref.shape[1]) else ref.at[j, pl.ds(r0, r1 - r0)]

    for p, (i, d, r0, r1) in enumerate(pieces):
        dx, dy, dc = _dev(d)
        k = (4 * (x != dx).astype(jnp.int32) + 2 * (y != dy).astype(jnp.int32)
             + (c != dc).astype(jnp.int32))
        slot = jnp.maximum(k - 1, 0)
        sends.append((k > 0, pltpu.make_async_remote_copy(
            src_ref=rows_of(p_ref, i, r0, r1), dst_ref=rows_of(land_ref, slot, r0, r1),
            send_sem=send_sems.at[p], recv_sem=recv_sems.at[p * PEERS + slot],
            device_id=(dx, dy, dc), device_id_type=MESH)))
        arrivals.append((k == 0, [pltpu.make_async_remote_copy(
            src_ref=rows_of(p_ref, i, r0, r1), dst_ref=rows_of(land_ref, r, r0, r1),
            send_sem=send_sems.at[p], recv_sem=recv_sems.at[p * PEERS + r],
            device_id=(dx, dy, dc), device_id_type=MESH) for r in range(PEERS)]))
    return sends, arrivals


def _ici_start(name, hs, lands, dests):
    na = len(hs)

    def body(*refs):
        h_refs, land_refs, sems = refs[:na], refs[na:2 * na], refs[2 * na:4 * na]
        token = refs[-1]
        for a in range(na):
            sends, _ = _ici_copies(h_refs[a], land_refs[a], sems[2 * a], sems[2 * a + 1], dests[a])
            for go, cp in sends:
                pl.when(go)(cp.start)
        token[...] = jnp.zeros_like(token)

    hbm, sem = pl.BlockSpec(memory_space=pltpu.HBM), pl.BlockSpec(memory_space=pltpu.SEMAPHORE)
    sem_shapes = []
    for a in range(na):
        sem_shapes += [pltpu.SemaphoreType.DMA((len(dests[a]),)),
                       pltpu.SemaphoreType.DMA((len(dests[a]) * PEERS,))]
    res = pl.pallas_call(
        body, name=name,
        out_shape=tuple(sem_shapes) + tuple(pltpu.HBM(v.shape, v.dtype) for v in list(hs) + list(lands))
        + (jax.ShapeDtypeStruct((8, LANE), F32),),
        in_specs=(hbm,) * (2 * na),
        out_specs=(sem,) * (2 * na) + (hbm,) * (2 * na) + (pl.BlockSpec(memory_space=pltpu.VMEM),),
        input_output_aliases={i: 2 * na + i for i in range(2 * na)},
        compiler_params=pltpu.CompilerParams(
            has_side_effects=pltpu.SideEffectType.DATAFLOW_SIDE_EFFECTING,
            vmem_limit_bytes=VMEM_LIMIT),
    )(*[pltpu.with_memory_space_constraint(v, pltpu.HBM) for v in list(hs) + list(lands)])
    sems = [(res[2 * a], res[2 * a + 1]) for a in range(na)]
    return sems, res[2 * na:3 * na], res[3 * na:4 * na], res[-1]


def _ici_wait(name, started, lands, after):
    k, nl = len(started), len(lands)

    def body(*refs):
        land_refs = refs[3 * k:3 * k + nl]
        for s in range(k):
            h_ref, send_sems, recv_sems = refs[3 * s:3 * s + 3]
            sends, arrivals = _ici_copies(h_ref, land_refs[started[s][3]], send_sems, recv_sems,
                                          started[s][4])
            for go, cp in sends:
                pl.when(go)(cp.wait_send)
            for here, cps in arrivals:
                for cp in cps:
                    pl.when(here)(cp.wait_recv)

    hbm, sem = pl.BlockSpec(memory_space=pltpu.HBM), pl.BlockSpec(memory_space=pltpu.SEMAPHORE)
    operands, specs = [], []
    for send_sems, recv_sems, h, _, _ in started:
        operands += [h, send_sems, recv_sems]
        specs += [hbm, sem, sem]
    return pl.pallas_call(
        body, name=name,
        out_shape=tuple(pltpu.HBM(v.shape, v.dtype) for v in lands),
        in_specs=tuple(specs) + (hbm,) * nl + (pl.BlockSpec(memory_space=pl.ANY),),
        out_specs=(hbm,) * nl,
        input_output_aliases={3 * k + i: i for i in range(nl)},
        compiler_params=pltpu.CompilerParams(
            has_side_effects=pltpu.SideEffectType.DATAFLOW_SIDE_EFFECTING,
            vmem_limit_bytes=VMEM_LIMIT),
    )(*operands, *lands, after)


def _weights_to_p(name, gathered, where, group):
    tl = 512
    off, width = P_GROUPS[group]
    segs = sorted([s for s in _segments() if _group_of(s[3]) == group], key=lambda s: s[3])
    used = sorted({where[s[0]][0] for s in segs})

    def body(*refs):
        g_refs, o_ref = dict(zip(used, refs[:-1])), refs[-1]
        pieces, pos = [], off
        for d, c0, n, p0 in segs:
            if p0 > pos:
                pieces.append(jnp.zeros((p0 - pos, tl), F32))
            k, slot = where[d]
            pieces.append(g_refs[k][slot, c0:c0 + n, :].astype(F32))
            pos = p0 + n
        if off + width > pos:
            pieces.append(jnp.zeros((off + width - pos, tl), F32))
        o_ref[...] = jnp.concatenate(pieces, axis=0).astype(BF16)

    return pl.pallas_call(
        body, name=name,
        grid=(D_MODEL // tl,),
        in_specs=[pl.BlockSpec((gathered[k].shape[0], SHARD_COLS, tl), lambda i: (0, 0, i))
                  for k in used],
        out_specs=pl.BlockSpec((width, tl), lambda i: (0, i)),
        out_shape=jax.ShapeDtypeStruct((width, D_MODEL), BF16),
        compiler_params=_cparams(("arbitrary",)),
    )(*[gathered[k] for k in used])


def _grads_to_shards(name, groups, dests, own_prev):
    tl = 512
    segs = _segments()
    used = sorted(groups)

    def body(*refs):
        g_refs, prev_ref, o_ref, own_ref = dict(zip(used, refs[:-3])), refs[-3], refs[-2], refs[-1]
        x, y, c = _mesh_pos()
        own = prev_ref[...].astype(F32)
        row = lax.broadcasted_iota(jnp.int32, (SHARD_PAD, tl), 0)
        for i, (d, ranges) in enumerate(dests):
            pieces, pos, asked = [], 0, None
            for r0, r1 in sorted(ranges):
                if r0 > pos:
                    pieces.append(jnp.zeros((r0 - pos, tl), F32))
                for _, c0, n, p0 in sorted([s for s in segs if s[0] == d], key=lambda s: s[1]):
                    a, b = max(c0, r0), min(c0 + n, r1)
                    if a < b:
                        gi = _group_of(p0)
                        lo = p0 - P_GROUPS[gi][0] + a - c0
                        pieces.append(g_refs[gi][lo:lo + b - a, :].astype(F32))
                if r1 > SHARD_COLS:
                    pieces.append(jnp.zeros((r1 - max(r0, SHARD_COLS), tl), F32))
                pos = r1
                inside = jnp.logical_and(row >= r0, row < r1)
                asked = inside if asked is None else jnp.logical_or(asked, inside)
            if pos < SHARD_PAD:
                pieces.append(jnp.zeros((SHARD_PAD - pos, tl), F32))
            shard = jnp.concatenate(pieces, axis=0)
            o_ref[i] = shard.astype(BF16)
            own = jnp.where(jnp.logical_and(4 * x + 2 * y + c == d, asked), shard, own)
        own_ref[...] = own.astype(BF16)

    blk = pl.BlockSpec((SHARD_PAD, tl), lambda i: (0, i))
    return pl.pallas_call(
        body, name=name,
        grid=(D_MODEL // tl,),
        in_specs=[pl.BlockSpec((P_GROUPS[g][1], tl), lambda i: (0, i)) for g in used] + [blk],
        out_specs=(pl.BlockSpec((len(dests), SHARD_PAD, tl), lambda i: (0, 0, i)), blk),
        out_shape=(jax.ShapeDtypeStruct((len(dests), SHARD_PAD, D_MODEL), BF16),
                   jax.ShapeDtypeStruct((SHARD_PAD, D_MODEL), BF16)),
        input_output_aliases={len(used): 1},
        compiler_params=_cparams(("arbitrary",)),
    )(*[groups[g] for g in used], own_prev)


def _inproj(x, g_in, w_pt):
    t = x.shape[0]
    tm = min(512, t)
    width = w_pt.shape[0]

    def body(x_ref, g_ref, w_ref, proj_ref, h_ref, r_ref):
        xf = x_ref[...]
        r = lax.rsqrt(jnp.mean(xf * xf, axis=-1, keepdims=True) + EPS)
        h = ((xf * r) * g_ref[...]).astype(BF16)
        proj_ref[...] = _dot_nt(h, w_ref[...])
        h_ref[...] = h
        r_ref[...] = r

    row = lambda w: pl.BlockSpec((tm, w), lambda i: (i, 0))
    return pl.pallas_call(
        body, name="inproj_latents",
        grid=(t // tm,),
        in_specs=[row(D_MODEL), pl.BlockSpec((1, D_MODEL), lambda i: (0, 0)),
                  pl.BlockSpec((width, D_MODEL), lambda i: (0, 0))],
        out_specs=(row(width), row(D_MODEL), row(1)),
        out_shape=(jax.ShapeDtypeStruct((t, width), F32),
                   jax.ShapeDtypeStruct((t, D_MODEL), BF16),
                   jax.ShapeDtypeStruct((t, 1), F32)),
        compiler_params=_cparams(("arbitrary",)),
    )(x, g_in, w_pt)


def _proj(name, h, w_pt):
    t = h.shape[0]
    tm = min(512, t)
    width = w_pt.shape[0]

    def body(h_ref, w_ref, o_ref):
        o_ref[...] = _dot_nt(h_ref[...], w_ref[...])

    return pl.pallas_call(
        body, name=name,
        grid=(t // tm,),
        in_specs=[pl.BlockSpec((tm, D_MODEL), lambda i: (i, 0)),
                  pl.BlockSpec((width, D_MODEL), lambda i: (0, 0))],
        out_specs=pl.BlockSpec((tm, width), lambda i: (i, 0)),
        out_shape=jax.ShapeDtypeStruct((t, width), F32),
        compiler_params=_cparams(("arbitrary",)),
    )(h, w_pt)


def _mla_prep(proj, g_q, g_kv, w_uq_p, w_k_p, w_v, w_gate_p, b_gate, rc, rsn, rsp):
    t = proj.shape[0]
    tm = min(256, t)
    hq = MLA_HEADS * HEAD_PAD

    def body(cq_ref, ckv_ref, misc_ref, gq_ref, gkv_ref, wuq_ref, wk_ref, wv_ref, wg_ref, bg_ref,
             c_ref, sn_ref, sp_ref,
             q_ref, k_ref, v_ref, la_ref, pre_ref, cqn_ref, ckvn_ref, rq_ref, rkv_ref, mb_ref):
        c, sn, sp = c_ref[...], sn_ref[...], sp_ref[...]
        cq = cq_ref[:, :MLA_Q_RANK]
        rq = lax.rsqrt(jnp.mean(cq * cq, axis=-1, keepdims=True) + EPS)
        cqn = ((cq * rq) * gq_ref[...]).astype(BF16)
        cqn_ref[...] = cqn
        rq_ref[...] = rq
        qpre = _dot(cqn, wuq_ref[...])
        ckv = ckv_ref[...]
        rkv = lax.rsqrt(jnp.mean(ckv * ckv, axis=-1, keepdims=True) + EPS)
        ckvn = ((ckv * rkv) * gkv_ref[...]).astype(BF16)
        ckvn_ref[...] = ckvn
        rkv_ref[...] = rkv
        kn = _dot(ckvn, wk_ref[...])
        v_ref[...] = _dot(ckvn, wv_ref[...]).astype(BF16)
        misc = misc_ref[...]
        krope = _rope_fwd(misc, c, sn, sp)
        for h in range(MLA_HEADS):
            sl = slice(h * HEAD_PAD, (h + 1) * HEAD_PAD)
            q_ref[:, sl] = _rope_fwd(qpre[:, sl], c, sn, sp).astype(BF16)
            k_ref[:, sl] = (kn[:, sl] + krope).astype(BF16)
        mb_ref[...] = misc.astype(BF16)
        pre = _dot(mb_ref[...], wg_ref[...]) + bg_ref[...]
        pre_ref[...] = pre
        log_a = (jnp.minimum(pre, 0.0) - jnp.log(1.0 + jnp.exp(-jnp.abs(pre)))) / GLA_GATE_NORM
        la_ref[...] = _dot_exact(_chunk_tri(tm, True), log_a)

    row = lambda w: pl.BlockSpec((tm, w), lambda i: (i, 0))
    full = lambda a: pl.BlockSpec(a.shape, lambda i: (0, 0))
    return pl.pallas_call(
        body, name="mla_prep",
        grid=(t // tm,),
        in_specs=[pl.BlockSpec((tm, 512), lambda i: (i, _rel(P_CQ) // 512)),
                  pl.BlockSpec((tm, MLA_KV_RANK), lambda i: (i, _rel(P_CKV) // MLA_KV_RANK)),
                  pl.BlockSpec((tm, LANE), lambda i: (i, _rel(P_MISC) // LANE)),
                  full(g_q), full(g_kv), full(w_uq_p), full(w_k_p), full(w_v), full(w_gate_p),
                  full(b_gate), row(LANE), row(LANE), row(LANE)],
        out_specs=(row(hq), row(hq), row(MLA_WIDTH), row(GLA_DK), row(GLA_DK),
                   row(MLA_Q_RANK), row(MLA_KV_RANK), row(1), row(1), row(LANE)),
        out_shape=(jax.ShapeDtypeStruct((t, hq), BF16), jax.ShapeDtypeStruct((t, hq), BF16),
                   jax.ShapeDtypeStruct((t, MLA_WIDTH), BF16),
                   jax.ShapeDtypeStruct((t, GLA_DK), F32), jax.ShapeDtypeStruct((t, GLA_DK), F32),
                   jax.ShapeDtypeStruct((t, MLA_Q_RANK), BF16),
                   jax.ShapeDtypeStruct((t, MLA_KV_RANK), BF16),
                   jax.ShapeDtypeStruct((t, 1), F32), jax.ShapeDtypeStruct((t, 1), F32),
                   jax.ShapeDtypeStruct((t, LANE), BF16)),
        compiler_params=_cparams(("arbitrary",)),
    )(proj, proj, proj, g_q, g_kv, w_uq_p, w_k_p, w_v, w_gate_p, b_gate, rc, rsn, rsp)


def _attn_masks(tq, i):
    keys = (i + 1) * tq
    rows = i * tq + lax.broadcasted_iota(jnp.int32, (tq, keys), 0)
    cols = lax.broadcasted_iota(jnp.int32, (tq, keys), 1)
    lane = lax.broadcasted_iota(jnp.int32, (tq, LANE), 1)
    return cols <= rows, lane < MLA_VDIM


def _for_each_query_tile(n_tiles, fn):
    for i in range(n_tiles):
        pl.when(pl.program_id(1) == i)(lambda i=i: fn(i))


def _mla_attn_fwd(q, k, v, shards, sources):
    t = q.shape[0]
    tq = min(256, t)
    scale = MLA_QK ** -0.5
    ns = len(shards)
    g_shapes, g_sems = _gather_plan(shards, sources)
    grid = (MLA_HEADS // 2, t // tq)

    def body(q_ref, k_ref, v_ref, *rest):
        o_ref, lse_ref = rest[ns:ns + 2]
        start, relay, finish = _gather_hooks(rest[:ns], rest[ns + 2:2 * ns + 2], rest[2 * ns + 2:],
                                             sources)
        step = pl.program_id(0) * grid[1] + pl.program_id(1)
        pl.when(step == 0)(start)
        pl.when(step == (grid[0] - 1) * grid[1])(relay)

        def tile(i):
            keys = (i + 1) * tq
            causal, low = _attn_masks(tq, i)
            vp = v_ref[0:keys, :]
            acc = jnp.zeros((tq, LANE), F32)
            for hh in range(2):
                sl = slice(hh * HEAD_PAD, (hh + 1) * HEAD_PAD)
                s = _dot_nt(q_ref[:, sl], k_ref[0:keys, sl]) * scale
                s = jnp.where(causal, s, -jnp.inf)
                m = jnp.max(s, axis=-1, keepdims=True)
                e = jnp.exp(s - m)
                l = jnp.sum(e, axis=-1, keepdims=True)
                o = _dot(e.astype(BF16), vp) / l
                acc = jnp.where(low if hh == 0 else jnp.logical_not(low), o, acc)
                lse_ref[hh] = m + jnp.log(l)
            o_ref[...] = acc

        _for_each_query_tile(t // tq, tile)
        pl.when(step == grid[0] * grid[1] - 1)(finish)

    res = pl.pallas_call(
        body, name="mla_attn_fwd",
        grid=grid,
        in_specs=[pl.BlockSpec((tq, 2 * HEAD_PAD), lambda p, i: (i, p)),
                  pl.BlockSpec((t, 2 * HEAD_PAD), lambda p, i: (0, p)),
                  pl.BlockSpec((t, LANE), lambda p, i: (0, p))] + _hbm_specs(ns),
        out_specs=(pl.BlockSpec((tq, LANE), lambda p, i: (i, p)),
                   pl.BlockSpec((2, tq, 1), lambda p, i: (p, i, 0))) + tuple(_hbm_specs(ns)),
        out_shape=(jax.ShapeDtypeStruct((t, MLA_WIDTH), F32),
                   jax.ShapeDtypeStruct((MLA_HEADS, t, 1), F32)) + tuple(g_shapes),
        scratch_shapes=g_sems,
        compiler_params=_cparams(("arbitrary", "arbitrary")),
    )(q, k, v, *shards)
    return res[0], res[1], res[2:]


def _mla_attn_bwd(q, k, v, o, do, lse, after):
    t = q.shape[0]
    tq = min(256, t)
    scale = MLA_QK ** -0.5

    def body(q_ref, k_ref, v_ref, o_ref, do_ref, lse_ref, after_ref, dq_ref, dk_ref, dv_ref):
        del after_ref

        @pl.when(pl.program_id(1) == 0)
        def _():
            dk_ref[...] = jnp.zeros_like(dk_ref)
            dv_ref[...] = jnp.zeros_like(dv_ref)

        def tile(i):
            keys = (i + 1) * tq
            causal, low = _attn_masks(tq, i)
            vp = v_ref[0:keys, :]
            do_all = do_ref[...]
            o_all = o_ref[...]
            dv_acc = jnp.zeros((keys, LANE), F32)
            for hh in range(2):
                sl = slice(hh * HEAD_PAD, (hh + 1) * HEAD_PAD)
                do_h = jnp.where(low if hh == 0 else jnp.logical_not(low), do_all, 0.0)
                dsum = jnp.sum(do_h * o_all, axis=-1, keepdims=True)
                qh = q_ref[:, sl]
                kh = k_ref[0:keys, sl]
                s = _dot_nt(qh, kh) * scale
                p = jnp.where(causal, jnp.exp(s - lse_ref[hh]), 0.0)
                do_b = do_h.astype(BF16)
                dp = _dot_nt(do_b, vp)
                ds = (p * (dp - dsum) * scale).astype(BF16)
                dq_ref[:, sl] = _dot(ds, kh).astype(BF16)
                dk_ref[0:keys, sl] += _dot_tn(ds, qh)
                dv_acc = dv_acc + _dot_tn(p.astype(BF16), do_b)
            dv_ref[0:keys, :] += dv_acc

        _for_each_query_tile(t // tq, tile)

    return pl.pallas_call(
        body, name="mla_attn_bwd",
        grid=(MLA_HEADS // 2, t // tq),
        in_specs=[pl.BlockSpec((tq, 2 * HEAD_PAD), lambda p, i: (i, p)),
                  pl.BlockSpec((t, 2 * HEAD_PAD), lambda p, i: (0, p)),
                  pl.BlockSpec((t, LANE), lambda p, i: (0, p)),
                  pl.BlockSpec((tq, LANE), lambda p, i: (i, p)),
                  pl.BlockSpec((tq, LANE), lambda p, i: (i, p)),
                  pl.BlockSpec((2, tq, 1), lambda p, i: (p, i, 0)),
                  pl.BlockSpec(memory_space=pl.ANY)],
        out_specs=(pl.BlockSpec((tq, 2 * HEAD_PAD), lambda p, i: (i, p)),
                   pl.BlockSpec((t, 2 * HEAD_PAD), lambda p, i: (0, p)),
                   pl.BlockSpec((t, LANE), lambda p, i: (0, p))),
        out_shape=(jax.ShapeDtypeStruct((t, MLA_HEADS * HEAD_PAD), BF16),
                   jax.ShapeDtypeStruct((t, MLA_HEADS * HEAD_PAD), F32),
                   jax.ShapeDtypeStruct((t, MLA_WIDTH), F32)),
        compiler_params=_cparams(("arbitrary", "arbitrary")),
    )(q, k, v, o, do, lse, after)


def _chunk_tri(n, lower):
    r = lax.broadcasted_iota(jnp.int32, (n, n), 0)
    c = lax.broadcasted_iota(jnp.int32, (n, n), 1)
    same = (r // GLA_CHUNK) == (c // GLA_CHUNK)
    return jnp.where(jnp.logical_and(same, r >= c if lower else r <= c), 1.0, 0.0).astype(F32)


def _gla_chunk_terms(q_ref, k_ref, b_ref, h, rows):
    sl = slice(h * GLA_HK, (h + 1) * GLA_HK)
    b = b_ref[rows, sl]
    bl = b[GLA_CHUNK - 1:GLA_CHUNK, :]
    kc = k_ref[rows, sl]
    q_in = (q_ref[rows, sl] * (GLA_HK ** -0.5)) * jnp.exp(b)
    k_in = kc * jnp.exp(-b)
    k_st = kc * jnp.exp(bl - b)
    return b, bl, q_in, k_in, k_st


def _tri(c, lower):
    r = lax.broadcasted_iota(jnp.int32, (c, c), 0)
    cc = lax.broadcasted_iota(jnp.int32, (c, c), 1)
    return jnp.where(r >= cc if lower else r <= cc, 1.0, 0.0).astype(F32)


def _gla_fwd(proj, log_a):
    t = proj.shape[0]
    per = GLA_CHUNKS_PER_STEP
    n = t // GLA_CHUNK
    c = GLA_CHUNK * per

    def body(q_ref, k_ref, v_ref, la_ref, o_ref, sp_ref, st_ref):
        @pl.when(pl.program_id(0) == 0)
        def _():
            st_ref[...] = jnp.zeros_like(st_ref)

        tri = _tri(GLA_CHUNK, True)
        for s, h in [(s, h) for s in range(per) for h in range(GLA_HEADS)]:
            rows = slice(s * GLA_CHUNK, (s + 1) * GLA_CHUNK)
            _, bl, q_in, k_in, k_st = _gla_chunk_terms(q_ref, k_ref, la_ref, h, rows)
            vs = slice(h * GLA_HV, (h + 1) * GLA_HV)
            vv = v_ref[rows, vs].astype(BF16)
            qb = q_in.astype(BF16)
            attn = _dot_nt(qb, k_in.astype(BF16)) * tri
            st = st_ref[h]
            sp_ref[s, h] = st
            o_ref[rows, vs] = _dot(attn.astype(BF16), vv) + _dot_nt(qb, st.astype(BF16))
            st_ref[h] = st * jnp.exp(bl) + _dot_tn(vv, k_st.astype(BF16))

    return pl.pallas_call(
        body, name="gla_fwd",
        grid=(n // per,),
        in_specs=[pl.BlockSpec((c, GLA_DK), lambda i: (i, P_QG // GLA_DK)),
                  pl.BlockSpec((c, GLA_DK), lambda i: (i, P_KG // GLA_DK)),
                  pl.BlockSpec((c, GLA_DV), lambda i: (i, P_VG // GLA_DV)),
                  pl.BlockSpec((c, GLA_DK), lambda i: (i, 0))],
        out_specs=(pl.BlockSpec((c, GLA_DV), lambda i: (i, 0)),
                   pl.BlockSpec((per, GLA_HEADS, GLA_HV, GLA_HK), lambda i: (i, 0, 0, 0))),
        out_shape=(jax.ShapeDtypeStruct((t, GLA_DV), F32),
                   jax.ShapeDtypeStruct((n, GLA_HEADS, GLA_HV, GLA_HK), F32)),
        scratch_shapes=[pltpu.VMEM((GLA_HEADS, GLA_HV, GLA_HK), F32)],
        compiler_params=_cparams(("arbitrary",)),
    )(proj, proj, proj, log_a)


def _gla_bwd(proj, log_a, do, states, after):
    t = proj.shape[0]
    per = GLA_CHUNKS_PER_STEP
    c = GLA_CHUNK * per
    n = t // c

    def body(q_ref, k_ref, v_ref, la_ref, do_ref, sp_ref, after_ref, dg_ref, dla_ref, ds_ref):
        del after_ref

        @pl.when(pl.program_id(0) == 0)
        def _():
            ds_ref[...] = jnp.zeros_like(ds_ref)

        tri = _tri(GLA_CHUNK, True)
        last = lax.broadcasted_iota(jnp.int32, (GLA_CHUNK, GLA_HK), 0) == GLA_CHUNK - 1
        for s, h in [(s, h) for s in reversed(range(per)) for h in range(GLA_HEADS)]:
            rows = slice(s * GLA_CHUNK, (s + 1) * GLA_CHUNK)
            b, bl, q_in, k_in, k_st = _gla_chunk_terms(q_ref, k_ref, la_ref, h, rows)
            ks_ = slice(h * GLA_HK, (h + 1) * GLA_HK)
            vs = slice(h * GLA_HV, (h + 1) * GLA_HV)
            vv = v_ref[rows, vs].astype(BF16)
            do_h = do_ref[rows, vs]
            qb, kb, ksb = q_in.astype(BF16), k_in.astype(BF16), k_st.astype(BF16)
            attn = (_dot_nt(qb, kb) * tri).astype(BF16)
            st = sp_ref[s, h]
            dst = ds_ref[h]
            dstb = dst.astype(BF16)
            dattn = (_dot_nt(do_h, vv) * tri).astype(BF16)
            dg_ref[rows, P_VG + h * GLA_HV:P_VG + (h + 1) * GLA_HV] = (
                _dot_tn(attn, do_h) + _dot_nt(ksb, dstb)).astype(BF16)
            dq_in = _dot(dattn, kb) + _dot(do_h, st.astype(BF16))
            dk_in = _dot_tn(dattn, qb)
            dk_st = _dot(vv, dstb)
            ebl = jnp.exp(bl)
            d_ebl = jnp.sum(st * dst, axis=0, keepdims=True)
            ds_ref[h] = _dot_tn(do_h, qb) + dst * ebl
            dg_ref[rows, P_QG + h * GLA_HK:P_QG + (h + 1) * GLA_HK] = (
                dq_in * (GLA_HK ** -0.5) * jnp.exp(b)).astype(BF16)
            dg_ref[rows, P_KG + h * GLA_HK:P_KG + (h + 1) * GLA_HK] = (
                dk_in * jnp.exp(-b) + dk_st * jnp.exp(bl - b)).astype(BF16)
            db = dq_in * q_in - dk_in * k_in - dk_st * k_st
            dbl = jnp.sum(dk_st * k_st, axis=0, keepdims=True) + d_ebl * ebl
            dla_ref[rows, ks_] = db + jnp.where(last, dbl, 0.0)

    rev = lambda i: n - 1 - i
    gw = P_GROUPS[0][1]
    return pl.pallas_call(
        body, name="gla_bwd",
        grid=(n,),
        in_specs=[pl.BlockSpec((c, GLA_DK), lambda i: (rev(i), P_QG // GLA_DK)),
                  pl.BlockSpec((c, GLA_DK), lambda i: (rev(i), P_KG // GLA_DK)),
                  pl.BlockSpec((c, GLA_DV), lambda i: (rev(i), P_VG // GLA_DV)),
                  pl.BlockSpec((c, GLA_DK), lambda i: (rev(i), 0)),
                  pl.BlockSpec((c, GLA_DV), lambda i: (rev(i), 0)),
                  pl.BlockSpec((per, GLA_HEADS, GLA_HV, GLA_HK), lambda i: (rev(i), 0, 0, 0)),
                  pl.BlockSpec(memory_space=pl.ANY)],
        out_specs=(pl.BlockSpec((c, gw), lambda i: (rev(i), 0)),
                   pl.BlockSpec((c, GLA_DK), lambda i: (rev(i), 0))),
        out_shape=(jax.ShapeDtypeStruct((t, gw), BF16), jax.ShapeDtypeStruct((t, GLA_DK), F32)),
        scratch_shapes=[pltpu.VMEM((GLA_HEADS, GLA_HV, GLA_HK), F32)],
        compiler_params=_cparams(("arbitrary",)),
    )(proj, proj, proj, log_a, do, states, after)


def _post(o_mla, proj, o_gla, x, target, g_gla, g_final, w_pm, w_pg, w_o):
    t = x.shape[0]
    tm = min(256, t)
    g0, gw = P_GROUPS[1]

    def body(om_ref, zg_ref, gm_ref, gg_ref, zm_ref, og_ref, x_ref, tg_ref, ggla_ref, gf_ref,
             wpm_ref, wpg_ref, wo_ref,
             dx2_ref, dom_ref, dog_ref, dg_ref,
             mg_ref, um_ref, ug_ref, dym_ref, dyg_ref, loss_ref, dgf_ref, dggla_ref):
        @pl.when(pl.program_id(0) == 0)
        def _():
            loss_ref[...] = jnp.zeros_like(loss_ref)
            dgf_ref[...] = jnp.zeros_like(dgf_ref)
            dggla_ref[...] = jnp.zeros_like(dggla_ref)

        om = om_ref[...]
        zm = zm_ref[...]
        sm = _sigmoid(zm)
        silu_m = zm * sm
        um = (om * silu_m).astype(BF16)
        um_ref[...] = um
        ym = _dot(um, wpm_ref[...])

        ggla = ggla_ref[...]
        zg = zg_ref[...]
        sg = _sigmoid(zg)
        silu_g = zg * sg
        xhat, rstd, on = [], [], []
        for h in range(GLA_HEADS):
            blk = og_ref[:, h * GLA_HV:(h + 1) * GLA_HV]
            r = lax.rsqrt(jnp.mean(blk * blk, axis=-1, keepdims=True) + EPS)
            xhat.append(blk * r)
            rstd.append(r)
            on.append(xhat[h] * ggla)
        on = jnp.concatenate(on, axis=-1)
        ug = (on * silu_g).astype(BF16)
        ug_ref[...] = ug
        yg = _dot(ug, wpg_ref[...])

        sgm = _sigmoid(gm_ref[...])
        sgg = _sigmoid(gg_ref[...])
        merged = (sgm * ym + sgg * yg).astype(BF16)
        mg_ref[...] = merged
        x2 = x_ref[...] + _dot(merged, wo_ref[...])
        gf = gf_ref[...]
        rf = lax.rsqrt(jnp.mean(x2 * x2, axis=-1, keepdims=True) + EPS)
        xh = x2 * rf
        err = xh * gf - tg_ref[...]
        loss_ref[...] += 0.5 * jnp.sum(jnp.mean(err * err, axis=-1, keepdims=True))

        dy = err * (1.0 / D_MODEL)
        dgf_ref[...] += jnp.sum(dy * xh, axis=0, keepdims=True)
        dxh = dy * gf
        dx2 = rf * (dxh - xh * jnp.mean(dxh * xh, axis=-1, keepdims=True))
        dx2_ref[...] = dx2
        dmerged = _dot_nt(dx2.astype(BF16), wo_ref[...])
        dym = (dmerged * sgm).astype(BF16)
        dyg = (dmerged * sgg).astype(BF16)
        dym_ref[...] = dym
        dyg_ref[...] = dyg
        dg_ref[:, P_GMLA - g0:P_GMLA - g0 + D_MODEL] = (dmerged * ym * sgm * (1.0 - sgm)).astype(BF16)
        dg_ref[:, P_GGLA - g0:P_GGLA - g0 + D_MODEL] = (dmerged * yg * sgg * (1.0 - sgg)).astype(BF16)
        dum = _dot_nt(dym, wpm_ref[...])
        dom_ref[...] = dum * silu_m
        dg_ref[:, P_ZMLA - g0:P_ZMLA - g0 + MLA_WIDTH] = (
            dum * om * (sm * (1.0 + zm * (1.0 - sm)))).astype(BF16)
        dug = _dot_nt(dyg, wpg_ref[...])
        dg_ref[:, P_ZGLA - g0:P_ZGLA - g0 + GLA_DV] = (
            dug * on * (sg * (1.0 + zg * (1.0 - sg)))).astype(BF16)
        don = dug * silu_g
        dggla = jnp.zeros((1, GLA_HV), F32)
        for h in range(GLA_HEADS):
            hs = slice(h * GLA_HV, (h + 1) * GLA_HV)
            don_h = don[:, hs]
            dggla = dggla + jnp.sum(don_h * xhat[h], axis=0, keepdims=True)
            dxh_h = don_h * ggla
            dog_ref[:, hs] = (rstd[h] * (dxh_h - xhat[h] * jnp.mean(dxh_h * xhat[h], axis=-1,
                                                                     keepdims=True))).astype(BF16)
        dggla_ref[...] += dggla

    row = lambda w: pl.BlockSpec((tm, w), lambda i: (i, 0))
    pcol = lambda w, off: pl.BlockSpec((tm, w), lambda i: (i, _rel(off) // w))
    full = lambda a: pl.BlockSpec(a.shape, lambda i: (0, 0))
    sds = jax.ShapeDtypeStruct
    return pl.pallas_call(
        body, name="post_fwd_bwd",
        grid=(t // tm,),
        in_specs=[row(MLA_WIDTH), pcol(GLA_DV, P_ZGLA), pcol(D_MODEL, P_GMLA), pcol(D_MODEL, P_GGLA),
                  pcol(MLA_WIDTH, P_ZMLA), row(GLA_DV), row(D_MODEL), row(D_MODEL),
                  full(g_gla), full(g_final), full(w_pm), full(w_pg), full(w_o)],
        out_specs=(row(D_MODEL), row(MLA_WIDTH), row(GLA_DV), row(gw),
                   row(D_MODEL), row(MLA_WIDTH), row(GLA_DV), row(D_MODEL), row(D_MODEL),
                   pl.BlockSpec((1, LANE), lambda i: (0, 0)),
                   pl.BlockSpec((1, D_MODEL), lambda i: (0, 0)),
                   pl.BlockSpec((1, GLA_HV), lambda i: (0, 0))),
        out_shape=(sds((t, D_MODEL), F32), sds((t, MLA_WIDTH), F32), sds((t, GLA_DV), BF16),
                   sds((t, gw), BF16),
                   sds((t, D_MODEL), BF16), sds((t, MLA_WIDTH), BF16), sds((t, GLA_DV), BF16),
                   sds((t, D_MODEL), BF16), sds((t, D_MODEL), BF16),
                   sds((1, LANE), F32), sds((1, D_MODEL), F32), sds((1, GLA_HV), F32)),
        compiler_params=_cparams(("arbitrary",)),
    )(o_mla, proj, proj, proj, proj, o_gla, x, target, g_gla, g_final, w_pm, w_pg, w_o)


def _mla_prep_bwd(dq, dk, dv, dla, pre, proj, rq, rkv, g_q, g_kv, w_uq_p, w_k_p, w_v, w_gate_p,
                  rc, rsn, rsp):
    t = proj.shape[0]
    tm = min(256, t)
    gw = P_GROUPS[2][1]

    def body(dq_ref, dk_ref, dv_ref, dla_ref, pre_ref, cq_ref, ckv_ref, rq_ref, rkv_ref,
             gq_ref, gkv_ref, wuq_ref, wk_ref, wv_ref, wg_ref, c_ref, sn_ref, sp_ref,
             dg_ref, dqpre_ref, dpre_ref, dgq_ref, dgkv_ref, dbg_ref):
        @pl.when(pl.program_id(0) == 0)
        def _():
            dgq_ref[...] = jnp.zeros_like(dgq_ref)
            dgkv_ref[...] = jnp.zeros_like(dgkv_ref)
            dbg_ref[...] = jnp.zeros_like(dbg_ref)

        c, sn, sp = c_ref[...], sn_ref[...], sp_ref[...]
        dkr = jnp.zeros((tm, LANE), F32)
        for h in range(MLA_HEADS):
            sl = slice(h * HEAD_PAD, (h + 1) * HEAD_PAD)
            dqpre_ref[:, sl] = _rope_bwd(dq_ref[:, sl].astype(F32), c, sn, sp).astype(BF16)
            dkr = dkr + dk_ref[:, sl]
        dcqn = _dot_nt(dqpre_ref[...], wuq_ref[...])
        rq = rq_ref[...]
        xh = cq_ref[:, :MLA_Q_RANK] * rq
        dgq_ref[...] += jnp.sum(dcqn * xh, axis=0, keepdims=True)
        dxh = dcqn * gq_ref[...]
        dcq = rq * (dxh - xh * jnp.mean(dxh * xh, axis=-1, keepdims=True))
        dg_ref[:, :MLA_Q_RANK] = dcq.astype(BF16)
        dg_ref[:, MLA_Q_RANK:512] = jnp.zeros((tm, 512 - MLA_Q_RANK), BF16)

        dckvn = _dot_nt(dk_ref[...].astype(BF16), wk_ref[...]) + \
            _dot_nt(dv_ref[...].astype(BF16), wv_ref[...])
        rkv = rkv_ref[...]
        xh = ckv_ref[...] * rkv
        dgkv_ref[...] += jnp.sum(dckvn * xh, axis=0, keepdims=True)
        dxh = dckvn * gkv_ref[...]
        dg_ref[:, P_CKV - P_CQ:P_CKV - P_CQ + MLA_KV_RANK] = (
            rkv * (dxh - xh * jnp.mean(dxh * xh, axis=-1, keepdims=True))).astype(BF16)

        dlog_a = _dot_exact(_chunk_tri(tm, False), dla_ref[...])
        dpre = dlog_a * (1.0 / GLA_GATE_NORM) * (1.0 - _sigmoid(pre_ref[...]))
        dbg_ref[...] += jnp.sum(dpre, axis=0, keepdims=True)
        dpre = dpre.astype(BF16)
        dpre_ref[...] = dpre
        lane = lax.broadcasted_iota(jnp.int32, (tm, LANE), 1)
        in_kr = jnp.logical_and(lane >= MISC_KR, lane < MISC_KR + MLA_ROPE)
        dmisc = jnp.where(in_kr, _rope_bwd(dkr, c, sn, sp), 0.0) + _dot_nt(dpre, wg_ref[...])
        dg_ref[:, P_MISC - P_CQ:P_MISC - P_CQ + LANE] = dmisc.astype(BF16)

    hq = MLA_HEADS * HEAD_PAD
    row = lambda w: pl.BlockSpec((tm, w), lambda i: (i, 0))
    full = lambda a: pl.BlockSpec(a.shape, lambda i: (0, 0))
    acc = lambda w: pl.BlockSpec((1, w), lambda i: (0, 0))
    sds = jax.ShapeDtypeStruct
    return pl.pallas_call(
        body, name="mla_prep_bwd",
        grid=(t // tm,),
        in_specs=[row(hq), row(hq), row(MLA_WIDTH), row(GLA_DK), row(GLA_DK),
                  pl.BlockSpec((tm, 512), lambda i: (i, _rel(P_CQ) // 512)),
                  pl.BlockSpec((tm, MLA_KV_RANK), lambda i: (i, _rel(P_CKV) // MLA_KV_RANK)),
                  row(1), row(1), full(g_q), full(g_kv), full(w_uq_p), full(w_k_p), full(w_v),
                  full(w_gate_p), row(LANE), row(LANE), row(LANE)],
        out_specs=(row(gw), row(hq), row(GLA_DK),
                   acc(MLA_Q_RANK), acc(MLA_KV_RANK), acc(GLA_DK)),
        out_shape=(sds((t, gw), BF16), sds((t, hq), BF16), sds((t, GLA_DK), BF16),
                   sds((1, MLA_Q_RANK), F32), sds((1, MLA_KV_RANK), F32), sds((1, GLA_DK), F32)),
        compiler_params=_cparams(("arbitrary",)),
    )(dq, dk, dv, dla, pre, proj, proj, rq, rkv, g_q, g_kv, w_uq_p, w_k_p, w_v, w_gate_p,
      rc, rsn, rsp)


def _inproj_bwd(dgroups, w_pts, x, rstd, g_in, dx2, after):
    t = x.shape[0]
    tm = min(256, t)

    def body(d0_ref, d1_ref, d2_ref, w0_ref, w1_ref, w2_ref, x_ref, r_ref, g_ref, dx2_ref, after_ref,
             dx_ref, dg_ref):
        del after_ref

        @pl.when(pl.program_id(0) == 0)
        def _():
            dg_ref[...] = jnp.zeros_like(dg_ref)

        dh = jnp.zeros((tm, D_MODEL), F32)
        for d_ref, w_ref in zip((d0_ref, d1_ref, d2_ref), (w0_ref, w1_ref, w2_ref)):
            dh = dh + _dot(d_ref[...], w_ref[...])
        r = r_ref[...]
        xh = x_ref[...] * r
        dg_ref[...] += jnp.sum(dh * xh, axis=0, keepdims=True)
        dxh = dh * g_ref[...]
        dx_ref[...] = dx2_ref[...] + r * (dxh - xh * jnp.mean(dxh * xh, axis=-1, keepdims=True))

    row = lambda w: pl.BlockSpec((tm, w), lambda i: (i, 0))
    return pl.pallas_call(
        body, name="inproj_bwd",
        grid=(t // tm,),
        in_specs=[row(w) for _, w in P_GROUPS]
        + [pl.BlockSpec((w, D_MODEL), lambda i: (0, 0)) for _, w in P_GROUPS]
        + [row(D_MODEL), row(1), pl.BlockSpec((1, D_MODEL), lambda i: (0, 0)), row(D_MODEL),
           pl.BlockSpec(memory_space=pl.ANY)],
        out_specs=(row(D_MODEL), pl.BlockSpec((1, D_MODEL), lambda i: (0, 0))),
        out_shape=(jax.ShapeDtypeStruct((t, D_MODEL), F32),
                   jax.ShapeDtypeStruct((1, D_MODEL), F32)),
        compiler_params=_cparams(("arbitrary",)),
    )(*dgroups, *w_pts, x, rstd, g_in, dx2, after)


def _matmul(name, a, b, tm, tn, dtype=F32, after=None):
    kk, m = a.shape
    n = b.shape[1]
    extra = [] if after is None else [after]

    def body(a_ref, b_ref, *rest):
        rest[-1][...] = _dot_tn(a_ref[...].astype(BF16), b_ref[...].astype(BF16)).astype(dtype)

    return pl.pallas_call(
        body, name=name,
        grid=(n // tn, m // tm),
        in_specs=[pl.BlockSpec((kk, tm), lambda j, i: (0, i)),
                  pl.BlockSpec((kk, tn), lambda j, i: (0, j))]
        + [pl.BlockSpec(memory_space=pl.ANY) for _ in extra],
        out_specs=pl.BlockSpec((tm, tn), lambda j, i: (i, j)),
        out_shape=jax.ShapeDtypeStruct((m, n), dtype),
        compiler_params=_cparams(("arbitrary", "arbitrary")),
    )(a, b, *extra)


def _adamw_update(part_refs, w_ref, m_ref, v_ref, g_ref, d_ref, nm_ref, nv_ref):
    g = part_refs[0][...].astype(F32)
    for p_ref in part_refs[1:]:
        g = g + p_ref[...].astype(F32)
    m_new = ADAM_B1 * m_ref[...] + (1.0 - ADAM_B1) * g
    v_new = ADAM_B2 * v_ref[...] + (1.0 - ADAM_B2) * (g * g)
    m_hat = m_new / (1.0 - ADAM_B1 ** ADAM_STEP)
    v_hat = v_new / (1.0 - ADAM_B2 ** ADAM_STEP)
    g_ref[...] = g
    nm_ref[...] = m_new
    nv_ref[...] = v_new
    d_ref[...] = -ADAM_LR * (m_hat / (jnp.sqrt(v_hat) + ADAM_EPS) + ADAM_WD * w_ref[...])


def _adamw_rows(name, parts, w, m, v, tr, first=None):
    _, rows, cols = w.shape
    slots = parts.shape[0]

    def body(*refs):
        lead_refs, p_ref = ([], refs[0]) if first is None else ([refs[0]], refs[1])
        _adamw_update(lead_refs + [p_ref.at[q] for q in range(slots)], *refs[len(lead_refs) + 1:])

    blk = pl.BlockSpec((None, tr, cols), lambda i: (0, i, 0))
    out = jax.ShapeDtypeStruct((1, rows, cols), F32)
    lead = [] if first is None else [pl.BlockSpec((tr, cols), lambda i: (i, 0))]
    return pl.pallas_call(
        body, name=name,
        grid=(rows // tr,),
        in_specs=lead + [pl.BlockSpec((slots, tr, cols), lambda i: (0, i, 0)), blk, blk, blk],
        out_specs=(blk, blk, blk, blk),
        out_shape=(out, out, out, out),
        compiler_params=_cparams(("arbitrary",)),
    )(*([] if first is None else [first]), parts, w, m, v)


def _adamw_transposed(name, first, parts, w, m, v, tl):
    _, rows, cols = w.shape
    slots, padded = parts.shape[:2]

    def body(f_ref, p_ref, *refs):
        _adamw_update([f_ref.at[pl.ds(0, cols)]]
                      + [p_ref.at[q, pl.ds(0, cols)] for q in range(slots)], *refs)

    blk = pl.BlockSpec((cols, None, tl), lambda i: (0, 0, i))
    out = jax.ShapeDtypeStruct((cols, 1, rows), F32)
    res = pl.pallas_call(
        body, name=name,
        grid=(rows // tl,),
        in_specs=[pl.BlockSpec((padded, tl), lambda i: (0, i)),
                  pl.BlockSpec((slots, padded, tl), lambda i: (0, 0, i)), blk, blk, blk],
        out_specs=(blk, blk, blk, blk),
        out_shape=(out, out, out, out),
        compiler_params=_cparams(("arbitrary",)),
    )(first, parts, *[a.transpose(2, 0, 1) for a in (w, m, v)])
    return [r.transpose(1, 2, 0) for r in res]


def _adamw_group(firsts, parts, ws, ms, vs):
    n = len(ws)

    def body(*refs):
        ins, outs = refs[:5 * n], refs[5 * n:]
        x, y, c = _mesh_pos()
        for a in range(n):
            _adamw_update([ins[a].at[4 * x + 2 * y + c]]
                          + [ins[n + a].at[q] for q in range(ins[n + a].shape[0])],
                          *[r.at[0] for r in (ins[2 * n + a], ins[3 * n + a], ins[4 * n + a])],
                          *[r.at[0] for r in outs[4 * a:4 * a + 4]])

    vmem = lambda k: [pl.BlockSpec(memory_space=pltpu.VMEM) for _ in range(k)]
    out_shape = []
    for w in ws:
        out_shape += [jax.ShapeDtypeStruct(w.shape, F32)] * 4
    res = pl.pallas_call(
        body, name="adamw_small_weights",
        in_specs=vmem(5 * n), out_specs=tuple(vmem(4 * n)), out_shape=tuple(out_shape),
        compiler_params=_cparams(),
    )(*firsts, *parts, *ws, *ms, *vs)
    return [res[4 * a:4 * a + 4] for a in range(n)]


def _rope_tables(positions):
    half = MLA_ROPE // 2
    freqs = ROPE_THETA ** (-jnp.arange(half, dtype=F32) / half)
    ang = positions.astype(F32).reshape(-1, 1) * freqs
    cos, sin = jnp.cos(ang), jnp.sin(ang)
    t = ang.shape[0]
    one, zero = jnp.ones((t, MLA_NOPE), F32), jnp.zeros((t, half), F32)
    tail = jnp.zeros((t, LANE - MLA_QK), F32)
    rc = jnp.concatenate([one, cos, cos, tail], axis=1)
    rsn = jnp.concatenate([0.0 * one, -sin, zero, tail], axis=1)
    rsp = jnp.concatenate([0.0 * one, zero, sin, tail], axis=1)
    return rc, rsn, rsp


def _cols_full(g):
    return g.transpose(1, 0, 2)


def kernel(x, positions, g_in, w_in, g_q, w_uq, g_kv, w_ukv, w_gla_gate, b_gla_gate, g_gla, w_proj_mla, w_proj_gla, w_out, g_final, loss_target, m_g_in, m_w_in, m_g_q, m_w_uq, m_g_kv, m_w_ukv, m_w_gla_gate, m_b_gla_gate, m_g_gla, m_w_proj_mla, m_w_proj_gla, m_w_out, m_g_final, v_g_in, v_w_in, v_g_q, v_w_uq, v_g_kv, v_w_ukv, v_w_gla_gate, v_b_gla_gate, v_g_gla, v_w_proj_mla, v_w_proj_gla, v_w_out, v_g_final):
    t = x.shape[1]
    x2d = x.reshape(t, D_MODEL)
    tgt = loss_target.reshape(t, D_MODEL)
    g_final2 = g_final.reshape(1, D_MODEL)
    sharded = [(w_in, m_w_in, v_w_in), (w_uq, m_w_uq, v_w_uq), (w_ukv, m_w_ukv, v_w_ukv),
               (w_gla_gate, m_w_gla_gate, v_w_gla_gate), (w_proj_mla, m_w_proj_mla, v_w_proj_mla),
               (w_proj_gla, m_w_proj_gla, v_w_proj_gla), (w_out, m_w_out, v_w_out)]

    w_in_t = w_in.transpose(2, 0, 1).reshape(SHARD_COLS, D_MODEL)
    everyone = tuple(range(N_DEV))
    w_in_b = w_in_t.astype(BF16)
    b_uq, b_ukv, b_gate, b_pm, b_pg, b_o = [s[0][0].astype(BF16) for s in sharded[1:]]
    stages = ((0, 2, 4, 6), (1, 3, 5, 7))
    where = {d: (k, i) for k, srcs in enumerate(stages) for i, d in enumerate(srcs)}
    g_in_1, g_uq, g_ukv, g_gate = _all_gather(
        "all_gather_first", [w_in_b, b_uq, b_ukv, b_gate], [stages[0]] + [everyone] * 3)
    w_uq_p = jnp.pad(_cols_full(g_uq), ((0, 0), (0, 0), (0, HEAD_PAD - MLA_QK))).reshape(
        MLA_Q_RANK, MLA_HEADS * HEAD_PAD)
    ukv = _cols_full(g_ukv)
    w_k_p = jnp.pad(ukv[:, :, :MLA_NOPE], ((0, 0), (0, 0), (0, HEAD_PAD - MLA_NOPE))).reshape(
        MLA_KV_RANK, MLA_HEADS * HEAD_PAD)
    w_v = ukv[:, :, MLA_NOPE:].reshape(MLA_KV_RANK, MLA_WIDTH)
    w_gate_p = jnp.pad(_cols_full(g_gate).reshape(GLA_GATE_RANK, GLA_DK),
                       ((MISC_ALR, LANE - MISC_ALR - GLA_GATE_RANK), (0, 0)))
    rc, rsn, rsp = _rope_tables(positions)

    w_lat = _weights_to_p("weights_latents", [g_in_1], where, 2)
    proj_lat, h, rstd = _inproj(x2d, g_in, w_lat)
    q, k, v, log_a, pre, cqn, ckvn, rq, rkv, misc = _mla_prep(
        proj_lat, g_q, g_kv, w_uq_p, w_k_p, w_v, w_gate_p, b_gla_gate, rc, rsn, rsp)
    o_mla, lse, (g_in_2, g_pm, g_pg, g_o) = _mla_attn_fwd(
        q, k, v, [w_in_b, b_pm, b_pg, b_o], [stages[1]] + [everyone] * 3)
    w_gla = _weights_to_p("weights_gla", [g_in_1, g_in_2], where, 0)
    proj_gla = _proj("inproj_gla", h, w_gla)
    o_gla, states = _gla_fwd(proj_gla, log_a)
    w_out_path = _weights_to_p("weights_out_path", [g_in_1, g_in_2], where, 1)
    proj_out = _proj("inproj_out_path", h, w_out_path)
    w_in_p = (w_gla, w_out_path, w_lat)
    w_pm = _cols_full(g_pm).reshape(MLA_WIDTH, D_MODEL)
    w_pg = g_pg.reshape(GLA_DV, D_MODEL)
    w_o = g_o.reshape(D_MODEL, D_MODEL)

    (dx2, do_mla, do_gla, d_out, merged, um, ug, dym, dyg, loss_p, dg_final,
     dg_gla) = _post(o_mla, proj_out, o_gla, x2d, tgt, g_gla, g_final2, w_pm, w_pg, w_o)

    p_pm = _matmul("dw_proj_mla", um, dym, 512, D_MODEL, BF16).reshape(
        MLA_WIDTH, N_DEV, D_MODEL // N_DEV).transpose(1, 0, 2)
    p_pg = _matmul("dw_proj_gla", ug, dyg, 512, D_MODEL, BF16).reshape(N_DEV, -1, D_MODEL)
    p_o = _matmul("dw_out", merged, dx2, 512, D_MODEL, BF16).reshape(N_DEV, -1, D_MODEL)
    own_in = jnp.zeros((SHARD_PAD, D_MODEL), BF16)
    land_in = lax.empty((PEERS, SHARD_PAD, D_MODEL), BF16)
    dw_groups, started, lands = {}, [], [land_in]

    def reduce_scatter_stage(s, dests, own_in, extra=()):
        parts_in, own_in = _grads_to_shards("grads_to_shards_%d" % s, dw_groups, dests, own_in)
        first = len(lands)
        lands.extend(lax.empty((PEERS,) + p.shape[1:], BF16) for p in extra)
        idx = [0] + list(range(first, len(lands)))
        all_dests = [[(i, d, r0, r1) for i, (d, ranges) in enumerate(dests) for r0, r1 in ranges]]
        all_dests += [_whole(everyone, p.shape[1]) for p in extra]
        sems, parts, new_lands, token = _ici_start(
            "ici_start_%d" % s, [parts_in] + list(extra), [lands[i] for i in idx], all_dests)
        for a, i in enumerate(idx):
            lands[i] = new_lands[a]
            started.append((sems[a][0], sems[a][1], parts[a], i, all_dests[a]))
        return own_in, token

    def late_small_stage(arrays):
        idx = list(range(len(lands), len(lands) + len(arrays)))
        lands.extend(lax.empty((PEERS,) + p.shape[1:], BF16) for p in arrays)
        all_dests = [_whole(everyone, p.shape[1]) for p in arrays]
        sems, parts, new_lands, token = _ici_start(
            "ici_start_4", list(arrays), [lands[i] for i in idx], all_dests)
        for a, i in enumerate(idx):
            lands[i] = new_lands[a]
            started.append((sems[a][0], sems[a][1], parts[a], i, all_dests[a]))
        return token

    dw_groups[1] = _matmul("dw_in_1", d_out, h, 512, D_MODEL, BF16)
    full = [(0, SHARD_PAD)]
    own_in, token = reduce_scatter_stage(
        1, [(5, full), (6, full), (7, full), (0, [(672, SHARD_PAD)]), (1, [(0, 384)]),
            (4, [(96, SHARD_PAD)])], own_in, (p_pm, p_pg, p_o))
    d_gla, dla = _gla_bwd(proj_gla, log_a, do_gla, states, token)
    dw_groups[0] = _matmul("dw_in_0", d_gla, h, 512, D_MODEL, BF16)
    own_in, token = reduce_scatter_stage(
        2, [(1, [(384, SHARD_PAD)]), (2, full), (3, full), (4, [(0, 64)])], own_in)
    dq, dk, dv = _mla_attn_bwd(q, k, v, o_mla, do_mla, lse, token)
    d_lat, dqpre, dpre, dg_q, dg_kv, db_gate = _mla_prep_bwd(
        dq, dk, dv, dla, pre, proj_lat, rq, rkv, g_q, g_kv, w_uq_p, w_k_p, w_v, w_gate_p, rc, rsn, rsp)
    dw_groups[2] = _matmul("dw_in_2", d_lat, h, 896, D_MODEL, BF16)
    own_in, token = reduce_scatter_stage(3, [(0, [(0, 672)]), (4, [(64, 96)])], own_in)
    dw_uq = _matmul("dw_uq", cqn, dqpre, MLA_Q_RANK, D_MODEL, BF16, after=token)
    p_uq = dw_uq.reshape(MLA_Q_RANK, MLA_HEADS, HEAD_PAD)[:, :, :MLA_QK].transpose(1, 0, 2)
    dw_k = _matmul("dw_uk", ckvn, dk, MLA_KV_RANK, D_MODEL, BF16)
    dw_v = _matmul("dw_uv", ckvn, dv, MLA_KV_RANK, 512, BF16)
    p_ukv = jnp.concatenate(
        [dw_k.reshape(MLA_KV_RANK, MLA_HEADS, HEAD_PAD)[:, :, :MLA_NOPE],
         dw_v.reshape(MLA_KV_RANK, MLA_HEADS, MLA_VDIM)], axis=2).transpose(1, 0, 2)
    dw_gate = _matmul("dw_gate", misc, dpre, LANE, 512, BF16)
    p_gate = dw_gate[MISC_ALR:MISC_ALR + GLA_GATE_RANK].reshape(
        GLA_GATE_RANK, N_DEV, GLA_DK // N_DEV).transpose(1, 0, 2)
    token = late_small_stage((p_uq, p_ukv, p_gate))
    grad_x, dg_in = _inproj_bwd((d_gla, d_out, d_lat), w_in_p, x2d, rstd, g_in, dx2, token)
    small = jnp.concatenate([dg_in.reshape(-1), dg_q.reshape(-1), dg_kv.reshape(-1),
                             db_gate.reshape(-1), dg_gla.reshape(-1), dg_final.reshape(-1),
                             loss_p[0, :1]])
    small = jnp.pad(small, (0, SMALL_ROWS * LANE - small.shape[0])).reshape(SMALL_ROWS, LANE)

    (small_all,) = _all_gather("all_gather_small", [small], [everyone])
    lands = _ici_wait("ici_wait", started, lands, small_all)
    big = [_adamw_transposed("adamw_w_in", own_in, lands[0], *sharded[0], 512)]
    big += _adamw_group([p_uq, p_ukv, p_gate, p_pm, p_pg, p_o], list(lands[4:7]) + list(lands[1:4]),
                        *[[s[j] for s in sharded[1:]] for j in range(3)])
    replicated = [(g_in, m_g_in, v_g_in), (g_q, m_g_q, v_g_q), (g_kv, m_g_kv, v_g_kv),
                  (b_gla_gate, m_b_gla_gate, v_b_gla_gate), (g_gla, m_g_gla, v_g_gla),
                  (g_final, m_g_final, v_g_final)]
    spacks = [jnp.pad(jnp.concatenate([s[j].reshape(-1) for s in replicated]),
                      (0, SMALL_ROWS * LANE - sum(SMALL_SIZES))).reshape(1, SMALL_ROWS, LANE)
              for j in range(3)]
    tiny = _adamw_rows("adamw_gains", small_all, spacks[0], spacks[1], spacks[2], SMALL_ROWS)

    outs = {}
    names = ("w_in", "w_uq", "w_ukv", "w_gla_gate", "w_proj_mla", "w_proj_gla", "w_out")
    for j, kind in enumerate(("grad", "delta", "new_m", "new_v")):
        for name, res in zip(names, big):
            outs[kind, name] = res[j]
        flat = tiny[j].reshape(-1)
        off = 0
        for name, size in zip(("g_in", "g_q", "g_kv", "b_gla_gate", "g_gla", "g_final"), SMALL_SIZES):
            shape = (size,) if name == "g_final" else (1, size)
            outs[kind, name] = flat[off:off + size].reshape(shape)
            off += size
    loss = tiny[0].reshape(-1)[sum(SMALL_SIZES)]
    order = ("g_in", "w_in", "g_q", "w_uq", "g_kv", "w_ukv", "w_gla_gate", "b_gla_gate", "g_gla",
             "w_proj_mla", "w_proj_gla", "w_out", "g_final")
    result = [loss, grad_x.reshape(1, t, D_MODEL)]
    for kind in ("grad", "delta", "new_m", "new_v"):
        result += [outs[kind, name] for name in order]
    return tuple(result)
```

```python
import jax
import jax.numpy as jnp
from jax import lax
from jax.experimental import pallas as pl
from jax.experimental.pallas import tpu as pltpu

F32 = jnp.float32
BF16 = jnp.bfloat16
MESH = pl.DeviceIdType.MESH
N_DEV = 8

D_MODEL = 1024
EPS = 1e-6
MLA_HEADS = 8
MLA_NOPE = 64
MLA_ROPE = 32
MLA_VDIM = 64
MLA_Q_RANK = 384
MLA_KV_RANK = 256
MLA_QK = MLA_NOPE + MLA_ROPE
MLA_WIDTH = MLA_HEADS * MLA_VDIM
ROPE_THETA = 10000.0
GLA_HEADS = 4
GLA_DK = 512
GLA_DV = 1024
GLA_HK = 128
GLA_HV = 256
GLA_GATE_RANK = 16
GLA_GATE_NORM = 16.0
GLA_CHUNK = 64
GLA_CHUNKS_PER_STEP = 8
D_IN = 6320

ADAM_LR = 0.001
ADAM_B1 = 0.9
ADAM_B2 = 0.999
ADAM_EPS = 1e-08
ADAM_WD = 0.01
ADAM_STEP = 10

LANE = 128
HEAD_PAD = 128
VMEM_LIMIT = 48 * 1024 * 1024

P_VG, P_QG, P_KG = 0, 1024, 1536
P_ZGLA, P_GMLA, P_GGLA, P_ZMLA = 2048, 3072, 4096, 5120
P_CQ, P_CKV, P_MISC = 5632, 6144, 6400
P_TOTAL = 6528
P_GROUPS = ((0, 2048), (2048, 3584), (5632, 896))
MISC_KR = 64
MISC_ALR = 96
SHARD_COLS = D_IN // N_DEV
SHARD_PAD = 800
P_COMPONENTS = ((0, 384, P_CQ), (384, 256, P_CKV), (640, 32, P_MISC + MISC_KR), (672, 512, P_ZMLA),
                (1184, 512, P_QG), (1696, 512, P_KG), (2208, 1024, P_VG),
                (3232, 16, P_MISC + MISC_ALR), (3248, 1024, P_ZGLA), (4272, 1024, P_GMLA),
                (5296, 1024, P_GGLA))

SMALL_SIZES = (1024, 384, 256, 512, 256, 1024)
SMALL_ROWS = 32


def _segments():
    segs = []
    for g0, n, p0 in P_COMPONENTS:
        g = g0
        while g < g0 + n:
            d = g // SHARD_COLS
            end = min(g0 + n, (d + 1) * SHARD_COLS)
            segs.append((d, g - d * SHARD_COLS, end - g, p0 + g - g0))
            g = end
    return segs


def _group_of(p0):
    return max(i for i, (off, _) in enumerate(P_GROUPS) if off <= p0)


def _rel(p0):
    return p0 - P_GROUPS[_group_of(p0)][0]


def _cparams(sem=None):
    if sem is None:
        return pltpu.CompilerParams(vmem_limit_bytes=VMEM_LIMIT)
    return pltpu.CompilerParams(dimension_semantics=sem, vmem_limit_bytes=VMEM_LIMIT)


def _sigmoid(v):
    return 1.0 / (1.0 + jnp.exp(-v))


def _dot(a, b):
    return jnp.dot(a, b, preferred_element_type=F32)


def _dot_nt(a, b):
    return lax.dot_general(a, b, (((1,), (1,)), ((), ())), preferred_element_type=F32)


def _dot_tn(a, b):
    return lax.dot_general(a, b, (((0,), (0,)), ((), ())), preferred_element_type=F32)


def _dot_exact(a, b):
    return jnp.dot(a, b, preferred_element_type=F32, precision=lax.Precision.HIGHEST)


def _rope_fwd(blk, c, sn, sp):
    return blk * c + pltpu.roll(blk, LANE - 16, 1) * sn + pltpu.roll(blk, 16, 1) * sp


def _rope_bwd(blk, c, sn, sp):
    return blk * c + pltpu.roll(blk * sn, 16, 1) + pltpu.roll(blk * sp, LANE - 16, 1)


def _mesh_pos():
    return lax.axis_index("x"), lax.axis_index("y"), lax.axis_index("c")


def _hbm_specs(n):
    return [pl.BlockSpec(memory_space=pltpu.HBM) for _ in range(n)]


def _dev(d):
    return d >> 2, (d >> 1) & 1, d & 1


def _gather_plan(shards, sources):
    na, most = len(shards), max(len(s) for s in sources)
    out_shape = [jax.ShapeDtypeStruct((len(srcs),) + s.shape, s.dtype)
                 for s, srcs in zip(shards, sources)]
    sems = [pltpu.SemaphoreType.DMA((na, most)) for _ in range(3)]
    sems += [pltpu.SemaphoreType.DMA((na, most, 3))]
    sems += [pltpu.SemaphoreType.DMA((na, most)) for _ in range(3)]
    return out_shape, sems


def _gather_hooks(x_refs, out_refs, sems, sources):
    local_sems, d2d_send, d2d_recv, ici_send, ici_recv, fwd_send, fwd_recv = sems
    x, y, c = _mesh_pos()
    chips = [(1 - x, y), (x, 1 - y), (1 - x, 1 - y)]
    items = []
    for a, srcs in enumerate(sources):
        for i, d in enumerate(srcs):
            dx, dy, dc = _dev(d)
            near = jnp.logical_and(x == dx, y == dy)
            far = jnp.logical_not(near)
            slot = out_refs[a].at[i]

            def remote(src, to, send_sem, recv_sem, slot=slot):
                return pltpu.make_async_remote_copy(
                    src_ref=src, dst_ref=slot, send_sem=send_sem, recv_sem=recv_sem,
                    device_id=to, device_id_type=MESH)

            items.append(dict(
                me=jnp.logical_and(near, c == dc), sibling=jnp.logical_and(near, c != dc),
                relay=jnp.logical_and(far, c == dc), behind=jnp.logical_and(far, c != dc),
                local=pltpu.make_async_copy(x_refs[a], slot, local_sems.at[a, i]),
                to_sibling=remote(x_refs[a], (x, y, 1 - c), d2d_send.at[a, i], d2d_recv.at[a, i]),
                to_chips=[remote(x_refs[a], (*chip, c), ici_send.at[a, i, j], ici_recv.at[a, i])
                          for j, chip in enumerate(chips)],
                forward=remote(slot, (x, y, 1 - c), fwd_send.at[a, i], fwd_recv.at[a, i])))

    def start():
        for it in items:
            @pl.when(it["me"])
            def _(it=it):
                it["local"].start()
                it["to_sibling"].start()
                for cp in it["to_chips"]:
                    cp.start()

    def finish():
        for it in items:
            @pl.when(it["relay"])
            def _(it=it):
                it["to_chips"][0].wait_recv()
                it["forward"].start()
        for it in items:
            pl.when(it["sibling"])(it["to_sibling"].wait_recv)
            pl.when(it["behind"])(it["forward"].wait_recv)
            pl.when(it["relay"])(it["forward"].wait_send)

            @pl.when(it["me"])
            def _(it=it):
                it["local"].wait()
                it["to_sibling"].wait_send()
                for cp in it["to_chips"]:
                    cp.wait_send()

    return start, finish


def _all_gather(name, shards, sources):
    n = len(shards)
    out_shape, sems = _gather_plan(shards, sources)

    def body(*refs):
        start, finish = _gather_hooks(refs[:n], refs[n:2 * n], refs[2 * n:], sources)
        start()
        finish()

    return pl.pallas_call(
        body, name=name,
        out_shape=tuple(out_shape),
        in_specs=_hbm_specs(n), out_specs=tuple(_hbm_specs(n)),
        scratch_shapes=sems,
        compiler_params=_cparams(),
    )(*shards)


PEERS = N_DEV - 1


def _whole(dests, rows):
    return [(i, d, 0, rows) for i, d in enumerate(dests)]


def _ici_copies(p_ref, land_ref, send_sems, recv_sems, pieces):
    x, y, c = _mesh_pos()
    sends, arrivals = [], []

    def rows_of(ref, j, r0, r1):
        return ref.at[j] if (r0, r1) == (0, ref.shape[1]) else ref.at[j, pl.ds(r0, r1 - r0)]

    for p, (i, d, r0, r1) in enumerate(pieces):
        dx, dy, dc = _dev(d)
        k = (4 * (x != dx).astype(jnp.int32) + 2 * (y != dy).astype(jnp.int32)
             + (c != dc).astype(jnp.int32))
        slot = jnp.maximum(k - 1, 0)
        sends.append((k > 0, pltpu.make_async_remote_copy(
            src_ref=rows_of(p_ref, i, r0, r1), dst_ref=rows_of(land_ref, slot, r0, r1),
            send_sem=send_sems.at[p], recv_sem=recv_sems.at[p * PEERS + slot],
            device_id=(dx, dy, dc), device_id_type=MESH)))
        arrivals.append((k == 0, [pltpu.make_async_remote_copy(
            src_ref=rows_of(p_ref, i, r0, r1), dst_ref=rows_of(land_ref, r, r0, r1),
            send_sem=send_sems.at[p], recv_sem=recv_sems.at[p * PEERS + r],
            device_id=(dx, dy, dc), device_id_type=MESH) for r in range(PEERS)]))
    return sends, arrivals


def _ici_start(name, hs, lands, dests):
    na = len(hs)

    def body(*refs):
        h_refs, land_refs, sems = refs[:na], refs[na:2 * na], refs[2 * na:4 * na]
        token = refs[-1]
        for a in range(na):
            sends, _ = _ici_copies(h_refs[a], land_refs[a], sems[2 * a], sems[2 * a + 1], dests[a])
            for go, cp in sends:
                pl.when(go)(cp.start)
        token[...] = jnp.zeros_like(token)

    hbm, sem = pl.BlockSpec(memory_space=pltpu.HBM), pl.BlockSpec(memory_space=pltpu.SEMAPHORE)
    sem_shapes = []
    for a in range(na):
        sem_shapes += [pltpu.SemaphoreType.DMA((len(dests[a]),)),
                       pltpu.SemaphoreType.DMA((len(dests[a]) * PEERS,))]
    res = pl.pallas_call(
        body, name=name,
        out_shape=tuple(sem_shapes) + tuple(pltpu.HBM(v.shape, v.dtype) for v in list(hs) + list(lands))
        + (jax.ShapeDtypeStruct((8, LANE), F32),),
        in_specs=(hbm,) * (2 * na),
        out_specs=(sem,) * (2 * na) + (hbm,) * (2 * na) + (pl.BlockSpec(memory_space=pltpu.VMEM),),
        input_output_aliases={i: 2 * na + i for i in range(2 * na)},
        compiler_params=pltpu.CompilerParams(
            has_side_effects=pltpu.SideEffectType.DATAFLOW_SIDE_EFFECTING,
            vmem_limit_bytes=VMEM_LIMIT),
    )(*[pltpu.with_memory_space_constraint(v, pltpu.HBM) for v in list(hs) + list(lands)])
    sems = [(res[2 * a], res[2 * a + 1]) for a in range(na)]
    return sems, res[2 * na:3 * na], res[3 * na:4 * na], res[-1]


def _ici_wait(name, started, lands, after):
    k, nl = len(started), len(lands)

    def body(*refs):
        land_refs = refs[3 * k:3 * k + nl]
        for s in range(k):
            h_ref, send_sems, recv_sems = refs[3 * s:3 * s + 3]
            sends, arrivals = _ici_copies(h_ref, land_refs[started[s][3]], send_sems, recv_sems,
                                          started[s][4])
            for go, cp in sends:
                pl.when(go)(cp.wait_send)
            for here, cps in arrivals:
                for cp in cps:
                    pl.when(here)(cp.wait_recv)

    hbm, sem = pl.BlockSpec(memory_space=pltpu.HBM), pl.BlockSpec(memory_space=pltpu.SEMAPHORE)
    operands, specs = [], []
    for send_sems, recv_sems, h, _, _ in started:
        operands += [h, send_sems, recv_sems]
        specs += [hbm, sem, sem]
    return pl.pallas_call(
        body, name=name,
        out_shape=tuple(pltpu.HBM(v.shape, v.dtype) for v in lands),
        in_specs=tuple(specs) + (hbm,) * nl + (pl.BlockSpec(memory_space=pl.ANY),),
        out_specs=(hbm,) * nl,
        input_output_aliases={3 * k + i: i for i in range(nl)},
        compiler_params=pltpu.CompilerParams(
            has_side_effects=pltpu.SideEffectType.DATAFLOW_SIDE_EFFECTING,
            vmem_limit_bytes=VMEM_LIMIT),
    )(*operands, *lands, after)


def _weights_to_p(name, gathered, where, group):
    tl = 512
    off, width = P_GROUPS[group]
    segs = sorted([s for s in _segments() if _group_of(s[3]) == group], key=lambda s: s[3])
    used = sorted({where[s[0]][0] for s in segs})

    def body(*refs):
        g_refs, o_ref = dict(zip(used, refs[:-1])), refs[-1]
        pieces, pos = [], off
        for d, c0, n, p0 in segs:
            if p0 > pos:
                pieces.append(jnp.zeros((p0 - pos, tl), F32))
            k, slot = where[d]
            pieces.append(g_refs[k][slot, c0:c0 + n, :].astype(F32))
            pos = p0 + n
        if off + width > pos:
            pieces.append(jnp.zeros((off + width - pos, tl), F32))
        o_ref[...] = jnp.concatenate(pieces, axis=0).astype(BF16)

    return pl.pallas_call(
        body, name=name,
        grid=(D_MODEL // tl,),
        in_specs=[pl.BlockSpec((gathered[k].shape[0], SHARD_COLS, tl), lambda i: (0, 0, i))
                  for k in used],
        out_specs=pl.BlockSpec((width, tl), lambda i: (0, i)),
        out_shape=jax.ShapeDtypeStruct((width, D_MODEL), BF16),
        compiler_params=_cparams(("arbitrary",)),
    )(*[gathered[k] for k in used])


def _grads_to_shards(name, groups, dests, own_prev):
    tl = 512
    segs = _segments()
    used = sorted(groups)

    def body(*refs):
        g_refs, prev_ref, o_ref, own_ref = dict(zip(used, refs[:-3])), refs[-3], refs[-2], refs[-1]
        x, y, c = _mesh_pos()
        own = prev_ref[...].astype(F32)
        row = lax.broadcasted_iota(jnp.int32, (SHARD_PAD, tl), 0)
        for i, (d, ranges) in enumerate(dests):
            pieces, pos, asked = [], 0, None
            for r0, r1 in sorted(ranges):
                if r0 > pos:
                    pieces.append(jnp.zeros((r0 - pos, tl), F32))
                for _, c0, n, p0 in sorted([s for s in segs if s[0] == d], key=lambda s: s[1]):
                    a, b = max(c0, r0), min(c0 + n, r1)
                    if a < b:
                        gi = _group_of(p0)
                        lo = p0 - P_GROUPS[gi][0] + a - c0
                        pieces.append(g_refs[gi][lo:lo + b - a, :].astype(F32))
                if r1 > SHARD_COLS:
                    pieces.append(jnp.zeros((r1 - max(r0, SHARD_COLS), tl), F32))
                pos = r1
                inside = jnp.logical_and(row >= r0, row < r1)
                asked = inside if asked is None else jnp.logical_or(asked, inside)
            if pos < SHARD_PAD:
                pieces.append(jnp.zeros((SHARD_PAD - pos, tl), F32))
            shard = jnp.concatenate(pieces, axis=0)
            o_ref[i] = shard.astype(BF16)
            own = jnp.where(jnp.logical_and(4 * x + 2 * y + c == d, asked), shard, own)
        own_ref[...] = own.astype(BF16)

    blk = pl.BlockSpec((SHARD_PAD, tl), lambda i: (0, i))
    return pl.pallas_call(
        body, name=name,
        grid=(D_MODEL // tl,),
        in_specs=[pl.BlockSpec((P_GROUPS[g][1], tl), lambda i: (0, i)) for g in used] + [blk],
        out_specs=(pl.BlockSpec((len(dests), SHARD_PAD, tl), lambda i: (0, 0, i)), blk),
        out_shape=(jax.ShapeDtypeStruct((len(dests), SHARD_PAD, D_MODEL), BF16),
                   jax.ShapeDtypeStruct((SHARD_PAD, D_MODEL), BF16)),
        input_output_aliases={len(used): 1},
        compiler_params=_cparams(("arbitrary",)),
    )(*[groups[g] for g in used], own_prev)


def _inproj(x, g_in, w_pt):
    t = x.shape[0]
    tm = min(512, t)
    width = w_pt.shape[0]

    def body(x_ref, g_ref, w_ref, proj_ref, h_ref, r_ref):
        xf = x_ref[...]
        r = lax.rsqrt(jnp.mean(xf * xf, axis=-1, keepdims=True) + EPS)
        h = ((xf * r) * g_ref[...]).astype(BF16)
        proj_ref[...] = _dot_nt(h, w_ref[...])
        h_ref[...] = h
        r_ref[...] = r

    row = lambda w: pl.BlockSpec((tm, w), lambda i: (i, 0))
    return pl.pallas_call(
        body, name="inproj_latents",
        grid=(t // tm,),
        in_specs=[row(D_MODEL), pl.BlockSpec((1, D_MODEL), lambda i: (0, 0)),
                  pl.BlockSpec((width, D_MODEL), lambda i: (0, 0))],
        out_specs=(row(width), row(D_MODEL), row(1)),
        out_shape=(jax.ShapeDtypeStruct((t, width), F32),
                   jax.ShapeDtypeStruct((t, D_MODEL), BF16),
                   jax.ShapeDtypeStruct((t, 1), F32)),
        compiler_params=_cparams(("arbitrary",)),
    )(x, g_in, w_pt)


def _proj(name, h, w_pt):
    t = h.shape[0]
    tm = min(512, t)
    width = w_pt.shape[0]

    def body(h_ref, w_ref, o_ref):
        o_ref[...] = _dot_nt(h_ref[...], w_ref[...])

    return pl.pallas_call(
        body, name=name,
        grid=(t // tm,),
        in_specs=[pl.BlockSpec((tm, D_MODEL), lambda i: (i, 0)),
                  pl.BlockSpec((width, D_MODEL), lambda i: (0, 0))],
        out_specs=pl.BlockSpec((tm, width), lambda i: (i, 0)),
        out_shape=jax.ShapeDtypeStruct((t, width), F32),
        compiler_params=_cparams(("arbitrary",)),
    )(h, w_pt)


def _mla_prep(proj, g_q, g_kv, w_uq_p, w_k_p, w_v, w_gate_p, b_gate, rc, rsn, rsp):
    t = proj.shape[0]
    tm = min(256, t)
    hq = MLA_HEADS * HEAD_PAD

    def body(cq_ref, ckv_ref, misc_ref, gq_ref, gkv_ref, wuq_ref, wk_ref, wv_ref, wg_ref, bg_ref,
             c_ref, sn_ref, sp_ref,
             q_ref, k_ref, v_ref, la_ref, pre_ref, cqn_ref, ckvn_ref, rq_ref, rkv_ref, mb_ref):
        c, sn, sp = c_ref[...], sn_ref[...], sp_ref[...]
        cq = cq_ref[:, :MLA_Q_RANK]
        rq = lax.rsqrt(jnp.mean(cq * cq, axis=-1, keepdims=True) + EPS)
        cqn = ((cq * rq) * gq_ref[...]).astype(BF16)
        cqn_ref[...] = cqn
        rq_ref[...] = rq
        qpre = _dot(cqn, wuq_ref[...])
        ckv = ckv_ref[...]
        rkv = lax.rsqrt(jnp.mean(ckv * ckv, axis=-1, keepdims=True) + EPS)
        ckvn = ((ckv * rkv) * gkv_ref[...]).astype(BF16)
        ckvn_ref[...] = ckvn
        rkv_ref[...] = rkv
        kn = _dot(ckvn, wk_ref[...])
        v_ref[...] = _dot(ckvn, wv_ref[...]).astype(BF16)
        misc = misc_ref[...]
        krope = _rope_fwd(misc, c, sn, sp)
        for h in range(MLA_HEADS):
            sl = slice(h * HEAD_PAD, (h + 1) * HEAD_PAD)
            q_ref[:, sl] = _rope_fwd(qpre[:, sl], c, sn, sp).astype(BF16)
            k_ref[:, sl] = (kn[:, sl] + krope).astype(BF16)
        mb_ref[...] = misc.astype(BF16)
        pre = _dot(mb_ref[...], wg_ref[...]) + bg_ref[...]
        pre_ref[...] = pre
        log_a = (jnp.minimum(pre, 0.0) - jnp.log(1.0 + jnp.exp(-jnp.abs(pre)))) / GLA_GATE_NORM
        la_ref[...] = _dot_exact(_chunk_tri(tm, True), log_a)

    row = lambda w: pl.BlockSpec((tm, w), lambda i: (i, 0))
    full = lambda a: pl.BlockSpec(a.shape, lambda i: (0, 0))
    return pl.pallas_call(
        body, name="mla_prep",
        grid=(t // tm,),
        in_specs=[pl.BlockSpec((tm, 512), lambda i: (i, _rel(P_CQ) // 512)),
                  pl.BlockSpec((tm, MLA_KV_RANK), lambda i: (i, _rel(P_CKV) // MLA_KV_RANK)),
                  pl.BlockSpec((tm, LANE), lambda i: (i, _rel(P_MISC) // LANE)),
                  full(g_q), full(g_kv), full(w_uq_p), full(w_k_p), full(w_v), full(w_gate_p),
                  full(b_gate), row(LANE), row(LANE), row(LANE)],
        out_specs=(row(hq), row(hq), row(MLA_WIDTH), row(GLA_DK), row(GLA_DK),
                   row(MLA_Q_RANK), row(MLA_KV_RANK), row(1), row(1), row(LANE)),
        out_shape=(jax.ShapeDtypeStruct((t, hq), BF16), jax.ShapeDtypeStruct((t, hq), BF16),
                   jax.ShapeDtypeStruct((t, MLA_WIDTH), BF16),
                   jax.ShapeDtypeStruct((t, GLA_DK), F32), jax.ShapeDtypeStruct((t, GLA_DK), F32),
                   jax.ShapeDtypeStruct((t, MLA_Q_RANK), BF16),
                   jax.ShapeDtypeStruct((t, MLA_KV_RANK), BF16),
                   jax.ShapeDtypeStruct((t, 1), F32), jax.ShapeDtypeStruct((t, 1), F32),
                   jax.ShapeDtypeStruct((t, LANE), BF16)),
        compiler_params=_cparams(("arbitrary",)),
    )(proj, proj, proj, g_q, g_kv, w_uq_p, w_k_p, w_v, w_gate_p, b_gate, rc, rsn, rsp)


def _attn_masks(tq, i):
    keys = (i + 1) * tq
    rows = i * tq + lax.broadcasted_iota(jnp.int32, (tq, keys), 0)
    cols = lax.broadcasted_iota(jnp.int32, (tq, keys), 1)
    lane = lax.broadcasted_iota(jnp.int32, (tq, LANE), 1)
    return cols <= rows, lane < MLA_VDIM


def _for_each_query_tile(n_tiles, fn):
    for i in range(n_tiles):
        pl.when(pl.program_id(1) == i)(lambda i=i: fn(i))


def _mla_attn_fwd(q, k, v, shards, sources):
    t = q.shape[0]
    tq = min(256, t)
    scale = MLA_QK ** -0.5
    ns = len(shards)
    g_shapes, g_sems = _gather_plan(shards, sources)
    grid = (MLA_HEADS // 2, t // tq)

    def body(q_ref, k_ref, v_ref, *rest):
        o_ref, lse_ref = rest[ns:ns + 2]
        start, finish = _gather_hooks(rest[:ns], rest[ns + 2:2 * ns + 2], rest[2 * ns + 2:], sources)
        step = pl.program_id(0) * grid[1] + pl.program_id(1)
        pl.when(step == 0)(start)

        def tile(i):
            keys = (i + 1) * tq
            causal, low = _attn_masks(tq, i)
            vp = v_ref[0:keys, :]
            acc = jnp.zeros((tq, LANE), F32)
            for hh in range(2):
                sl = slice(hh * HEAD_PAD, (hh + 1) * HEAD_PAD)
                s = _dot_nt(q_ref[:, sl], k_ref[0:keys, sl]) * scale
                s = jnp.where(causal, s, -jnp.inf)
                m = jnp.max(s, axis=-1, keepdims=True)
                e = jnp.exp(s - m)
                l = jnp.sum(e, axis=-1, keepdims=True)
                o = _dot(e.astype(BF16), vp) / l
                acc = jnp.where(low if hh == 0 else jnp.logical_not(low), o, acc)
                lse_ref[hh] = m + jnp.log(l)
            o_ref[...] = acc

        _for_each_query_tile(t // tq, tile)
        pl.when(step == grid[0] * grid[1] - 1)(finish)

    res = pl.pallas_call(
        body, name="mla_attn_fwd",
        grid=grid,
        in_specs=[pl.BlockSpec((tq, 2 * HEAD_PAD), lambda p, i: (i, p)),
                  pl.BlockSpec((t, 2 * HEAD_PAD), lambda p, i: (0, p)),
                  pl.BlockSpec((t, LANE), lambda p, i: (0, p))] + _hbm_specs(ns),
        out_specs=(pl.BlockSpec((tq, LANE), lambda p, i: (i, p)),
                   pl.BlockSpec((2, tq, 1), lambda p, i: (p, i, 0))) + tuple(_hbm_specs(ns)),
        out_shape=(jax.ShapeDtypeStruct((t, MLA_WIDTH), F32),
                   jax.ShapeDtypeStruct((MLA_HEADS, t, 1), F32)) + tuple(g_shapes),
        scratch_shapes=g_sems,
        compiler_params=_cparams(("arbitrary", "arbitrary")),
    )(q, k, v, *shards)
    return res[0], res[1], res[2:]


def _mla_attn_bwd(q, k, v, o, do, lse, after):
    t = q.shape[0]
    tq = min(256, t)
    scale = MLA_QK ** -0.5

    def body(q_ref, k_ref, v_ref, o_ref, do_ref, lse_ref, after_ref, dq_ref, dk_ref, dv_ref):
        del after_ref

        @pl.when(pl.program_id(1) == 0)
        def _():
            dk_ref[...] = jnp.zeros_like(dk_ref)
            dv_ref[...] = jnp.zeros_like(dv_ref)

        def tile(i):
            keys = (i + 1) * tq
            causal, low = _attn_masks(tq, i)
            vp = v_ref[0:keys, :]
            do_all = do_ref[...]
            o_all = o_ref[...]
            dv_acc = jnp.zeros((keys, LANE), F32)
            for hh in range(2):
                sl = slice(hh * HEAD_PAD, (hh + 1) * HEAD_PAD)
                do_h = jnp.where(low if hh == 0 else jnp.logical_not(low), do_all, 0.0)
                dsum = jnp.sum(do_h * o_all, axis=-1, keepdims=True)
                qh = q_ref[:, sl]
                kh = k_ref[0:keys, sl]
                s = _dot_nt(qh, kh) * scale
                p = jnp.where(causal, jnp.exp(s - lse_ref[hh]), 0.0)
                do_b = do_h.astype(BF16)
                dp = _dot_nt(do_b, vp)
                ds = (p * (dp - dsum) * scale).astype(BF16)
                dq_ref[:, sl] = _dot(ds, kh).astype(BF16)
                dk_ref[0:keys, sl] += _dot_tn(ds, qh)
                dv_acc = dv_acc + _dot_tn(p.astype(BF16), do_b)
            dv_ref[0:keys, :] += dv_acc

        _for_each_query_tile(t // tq, tile)

    return pl.pallas_call(
        body, name="mla_attn_bwd",
        grid=(MLA_HEADS // 2, t // tq),
        in_specs=[pl.BlockSpec((tq, 2 * HEAD_PAD), lambda p, i: (i, p)),
                  pl.BlockSpec((t, 2 * HEAD_PAD), lambda p, i: (0, p)),
                  pl.BlockSpec((t, LANE), lambda p, i: (0, p)),
                  pl.BlockSpec((tq, LANE), lambda p, i: (i, p)),
                  pl.BlockSpec((tq, LANE), lambda p, i: (i, p)),
                  pl.BlockSpec((2, tq, 1), lambda p, i: (p, i, 0)),
                  pl.BlockSpec(memory_space=pl.ANY)],
        out_specs=(pl.BlockSpec((tq, 2 * HEAD_PAD), lambda p, i: (i, p)),
                   pl.BlockSpec((t, 2 * HEAD_PAD), lambda p, i: (0, p)),
                   pl.BlockSpec((t, LANE), lambda p, i: (0, p))),
        out_shape=(jax.ShapeDtypeStruct((t, MLA_HEADS * HEAD_PAD), BF16),
                   jax.ShapeDtypeStruct((t, MLA_HEADS * HEAD_PAD), F32),
                   jax.ShapeDtypeStruct((t, MLA_WIDTH), F32)),
        compiler_params=_cparams(("arbitrary", "arbitrary")),
    )(q, k, v, o, do, lse, after)


def _chunk_tri(n, lower):
    r = lax.broadcasted_iota(jnp.int32, (n, n), 0)
    c = lax.broadcasted_iota(jnp.int32, (n, n), 1)
    same = (r // GLA_CHUNK) == (c // GLA_CHUNK)
    return jnp.where(jnp.logical_and(same, r >= c if lower else r <= c), 1.0, 0.0).astype(F32)


def _gla_chunk_terms(q_ref, k_ref, b_ref, h, rows):
    sl = slice(h * GLA_HK, (h + 1) * GLA_HK)
    b = b_ref[rows, sl]
    bl = b[GLA_CHUNK - 1:GLA_CHUNK, :]
    kc = k_ref[rows, sl]
    q_in = (q_ref[rows, sl] * (GLA_HK ** -0.5)) * jnp.exp(b)
    k_in = kc * jnp.exp(-b)
    k_st = kc * jnp.exp(bl - b)
    return b, bl, q_in, k_in, k_st


def _tri(c, lower):
    r = lax.broadcasted_iota(jnp.int32, (c, c), 0)
    cc = lax.broadcasted_iota(jnp.int32, (c, c), 1)
    return jnp.where(r >= cc if lower else r <= cc, 1.0, 0.0).astype(F32)


def _gla_fwd(proj, log_a):
    t = proj.shape[0]
    per = GLA_CHUNKS_PER_STEP
    n = t // GLA_CHUNK
    c = GLA_CHUNK * per

    def body(q_ref, k_ref, v_ref, la_ref, o_ref, sp_ref, st_ref):
        @pl.when(pl.program_id(0) == 0)
        def _():
            st_ref[...] = jnp.zeros_like(st_ref)

        tri = _tri(GLA_CHUNK, True)
        for s, h in [(s, h) for s in range(per) for h in range(GLA_HEADS)]:
            rows = slice(s * GLA_CHUNK, (s + 1) * GLA_CHUNK)
            _, bl, q_in, k_in, k_st = _gla_chunk_terms(q_ref, k_ref, la_ref, h, rows)
            vs = slice(h * GLA_HV, (h + 1) * GLA_HV)
            vv = v_ref[rows, vs].astype(BF16)
            qb = q_in.astype(BF16)
            attn = _dot_nt(qb, k_in.astype(BF16)) * tri
            st = st_ref[h]
            sp_ref[s, h] = st
            o_ref[rows, vs] = _dot(attn.astype(BF16), vv) + _dot_nt(qb, st.astype(BF16))
            st_ref[h] = st * jnp.exp(bl) + _dot_tn(vv, k_st.astype(BF16))

    return pl.pallas_call(
        body, name="gla_fwd",
        grid=(n // per,),
        in_specs=[pl.BlockSpec((c, GLA_DK), lambda i: (i, P_QG // GLA_DK)),
                  pl.BlockSpec((c, GLA_DK), lambda i: (i, P_KG // GLA_DK)),
                  pl.BlockSpec((c, GLA_DV), lambda i: (i, P_VG // GLA_DV)),
                  pl.BlockSpec((c, GLA_DK), lambda i: (i, 0))],
        out_specs=(pl.BlockSpec((c, GLA_DV), lambda i: (i, 0)),
                   pl.BlockSpec((per, GLA_HEADS, GLA_HV, GLA_HK), lambda i: (i, 0, 0, 0))),
        out_shape=(jax.ShapeDtypeStruct((t, GLA_DV), F32),
                   jax.ShapeDtypeStruct((n, GLA_HEADS, GLA_HV, GLA_HK), F32)),
        scratch_shapes=[pltpu.VMEM((GLA_HEADS, GLA_HV, GLA_HK), F32)],
        compiler_params=_cparams(("arbitrary",)),
    )(proj, proj, proj, log_a)


def _gla_bwd(proj, log_a, do, states, after):
    t = proj.shape[0]
    per = GLA_CHUNKS_PER_STEP
    c = GLA_CHUNK * per
    n = t // c

    def body(q_ref, k_ref, v_ref, la_ref, do_ref, sp_ref, after_ref, dg_ref, dla_ref, ds_ref):
        del after_ref

        @pl.when(pl.program_id(0) == 0)
        def _():
            ds_ref[...] = jnp.zeros_like(ds_ref)

        tri = _tri(GLA_CHUNK, True)
        last = lax.broadcasted_iota(jnp.int32, (GLA_CHUNK, GLA_HK), 0) == GLA_CHUNK - 1
        for s, h in [(s, h) for s in reversed(range(per)) for h in range(GLA_HEADS)]:
            rows = slice(s * GLA_CHUNK, (s + 1) * GLA_CHUNK)
            b, bl, q_in, k_in, k_st = _gla_chunk_terms(q_ref, k_ref, la_ref, h, rows)
            ks_ = slice(h * GLA_HK, (h + 1) * GLA_HK)
            vs = slice(h * GLA_HV, (h + 1) * GLA_HV)
            vv = v_ref[rows, vs].astype(BF16)
            do_h = do_ref[rows, vs]
            qb, kb, ksb = q_in.astype(BF16), k_in.astype(BF16), k_st.astype(BF16)
            attn = (_dot_nt(qb, kb) * tri).astype(BF16)
            st = sp_ref[s, h]
            dst = ds_ref[h]
            dstb = dst.astype(BF16)
            dattn = (_dot_nt(do_h, vv) * tri).astype(BF16)
            dg_ref[rows, P_VG + h * GLA_HV:P_VG + (h + 1) * GLA_HV] = (
                _dot_tn(attn, do_h) + _dot_nt(ksb, dstb)).astype(BF16)
            dq_in = _dot(dattn, kb) + _dot(do_h, st.astype(BF16))
            dk_in = _dot_tn(dattn, qb)
            dk_st = _dot(vv, dstb)
            ebl = jnp.exp(bl)
            d_ebl = jnp.sum(st * dst, axis=0, keepdims=True)
            ds_ref[h] = _dot_tn(do_h, qb) + dst * ebl
            dg_ref[rows, P_QG + h * GLA_HK:P_QG + (h + 1) * GLA_HK] = (
                dq_in * (GLA_HK ** -0.5) * jnp.exp(b)).astype(BF16)
            dg_ref[rows, P_KG + h * GLA_HK:P_KG + (h + 1) * GLA_HK] = (
                dk_in * jnp.exp(-b) + dk_st * jnp.exp(bl - b)).astype(BF16)
            db = dq_in * q_in - dk_in * k_in - dk_st * k_st
            dbl = jnp.sum(dk_st * k_st, axis=0, keepdims=True) + d_ebl * ebl
            dla_ref[rows, ks_] = db + jnp.where(last, dbl, 0.0)

    rev = lambda i: n - 1 - i
    gw = P_GROUPS[0][1]
    return pl.pallas_call(
        body, name="gla_bwd",
        grid=(n,),
        in_specs=[pl.BlockSpec((c, GLA_DK), lambda i: (rev(i), P_QG // GLA_DK)),
                  pl.BlockSpec((c, GLA_DK), lambda i: (rev(i), P_KG // GLA_DK)),
                  pl.BlockSpec((c, GLA_DV), lambda i: (rev(i), P_VG // GLA_DV)),
                  pl.BlockSpec((c, GLA_DK), lambda i: (rev(i), 0)),
                  pl.BlockSpec((c, GLA_DV), lambda i: (rev(i), 0)),
                  pl.BlockSpec((per, GLA_HEADS, GLA_HV, GLA_HK), lambda i: (rev(i), 0, 0, 0)),
                  pl.BlockSpec(memory_space=pl.ANY)],
        out_specs=(pl.BlockSpec((c, gw), lambda i: (rev(i), 0)),
                   pl.BlockSpec((c, GLA_DK), lambda i: (rev(i), 0))),
        out_shape=(jax.ShapeDtypeStruct((t, gw), BF16), jax.ShapeDtypeStruct((t, GLA_DK), F32)),
        scratch_shapes=[pltpu.VMEM((GLA_HEADS, GLA_HV, GLA_HK), F32)],
        compiler_params=_cparams(("arbitrary",)),
    )(proj, proj, proj, log_a, do, states, after)


def _post(o_mla, proj, o_gla, x, target, g_gla, g_final, w_pm, w_pg, w_o):
    t = x.shape[0]
    tm = min(256, t)
    g0, gw = P_GROUPS[1]

    def body(om_ref, zg_ref, gm_ref, gg_ref, zm_ref, og_ref, x_ref, tg_ref, ggla_ref, gf_ref,
             wpm_ref, wpg_ref, wo_ref,
             dx2_ref, dom_ref, dog_ref, dg_ref,
             mg_ref, um_ref, ug_ref, dym_ref, dyg_ref, loss_ref, dgf_ref, dggla_ref):
        @pl.when(pl.program_id(0) == 0)
        def _():
            loss_ref[...] = jnp.zeros_like(loss_ref)
            dgf_ref[...] = jnp.zeros_like(dgf_ref)
            dggla_ref[...] = jnp.zeros_like(dggla_ref)

        om = om_ref[...]
        zm = zm_ref[...]
        sm = _sigmoid(zm)
        silu_m = zm * sm
        um = (om * silu_m).astype(BF16)
        um_ref[...] = um
        ym = _dot(um, wpm_ref[...])

        ggla = ggla_ref[...]
        zg = zg_ref[...]
        sg = _sigmoid(zg)
        silu_g = zg * sg
        xhat, rstd, on = [], [], []
        for h in range(GLA_HEADS):
            blk = og_ref[:, h * GLA_HV:(h + 1) * GLA_HV]
            r = lax.rsqrt(jnp.mean(blk * blk, axis=-1, keepdims=True) + EPS)
            xhat.append(blk * r)
            rstd.append(r)
            on.append(xhat[h] * ggla)
        on = jnp.concatenate(on, axis=-1)
        ug = (on * silu_g).astype(BF16)
        ug_ref[...] = ug
        yg = _dot(ug, wpg_ref[...])

        sgm = _sigmoid(gm_ref[...])
        sgg = _sigmoid(gg_ref[...])
        merged = (sgm * ym + sgg * yg).astype(BF16)
        mg_ref[...] = merged
        x2 = x_ref[...] + _dot(merged, wo_ref[...])
        gf = gf_ref[...]
        rf = lax.rsqrt(jnp.mean(x2 * x2, axis=-1, keepdims=True) + EPS)
        xh = x2 * rf
        err = xh * gf - tg_ref[...]
        loss_ref[...] += 0.5 * jnp.sum(jnp.mean(err * err, axis=-1, keepdims=True))

        dy = err * (1.0 / D_MODEL)
        dgf_ref[...] += jnp.sum(dy * xh, axis=0, keepdims=True)
        dxh = dy * gf
        dx2 = rf * (dxh - xh * jnp.mean(dxh * xh, axis=-1, keepdims=True))
        dx2_ref[...] = dx2
        dmerged = _dot_nt(dx2.astype(BF16), wo_ref[...])
        dym = (dmerged * sgm).astype(BF16)
        dyg = (dmerged * sgg).astype(BF16)
        dym_ref[...] = dym
        dyg_ref[...] = dyg
        dg_ref[:, P_GMLA - g0:P_GMLA - g0 + D_MODEL] = (dmerged * ym * sgm * (1.0 - sgm)).astype(BF16)
        dg_ref[:, P_GGLA - g0:P_GGLA - g0 + D_MODEL] = (dmerged * yg * sgg * (1.0 - sgg)).astype(BF16)
        dum = _dot_nt(dym, wpm_ref[...])
        dom_ref[...] = dum * silu_m
        dg_ref[:, P_ZMLA - g0:P_ZMLA - g0 + MLA_WIDTH] = (
            dum * om * (sm * (1.0 + zm * (1.0 - sm)))).astype(BF16)
        dug = _dot_nt(dyg, wpg_ref[...])
        dg_ref[:, P_ZGLA - g0:P_ZGLA - g0 + GLA_DV] = (
            dug * on * (sg * (1.0 + zg * (1.0 - sg)))).astype(BF16)
        don = dug * silu_g
        dggla = jnp.zeros((1, GLA_HV), F32)
        for h in range(GLA_HEADS):
            hs = slice(h * GLA_HV, (h + 1) * GLA_HV)
            don_h = don[:, hs]
            dggla = dggla + jnp.sum(don_h * xhat[h], axis=0, keepdims=True)
            dxh_h = don_h * ggla
            dog_ref[:, hs] = (rstd[h] * (dxh_h - xhat[h] * jnp.mean(dxh_h * xhat[h], axis=-1,
                                                                     keepdims=True))).astype(BF16)
        dggla_ref[...] += dggla

    row = lambda w: pl.BlockSpec((tm, w), lambda i: (i, 0))
    pcol = lambda w, off: pl.BlockSpec((tm, w), lambda i: (i, _rel(off) // w))
    full = lambda a: pl.BlockSpec(a.shape, lambda i: (0, 0))
    sds = jax.ShapeDtypeStruct
    return pl.pallas_call(
        body, name="post_fwd_bwd",
        grid=(t // tm,),
        in_specs=[row(MLA_WIDTH), pcol(GLA_DV, P_ZGLA), pcol(D_MODEL, P_GMLA), pcol(D_MODEL, P_GGLA),
                  pcol(MLA_WIDTH, P_ZMLA), row(GLA_DV), row(D_MODEL), row(D_MODEL),
                  full(g_gla), full(g_final), full(w_pm), full(w_pg), full(w_o)],
        out_specs=(row(D_MODEL), row(MLA_WIDTH), row(GLA_DV), row(gw),
                   row(D_MODEL), row(MLA_WIDTH), row(GLA_DV), row(D_MODEL), row(D_MODEL),
                   pl.BlockSpec((1, LANE), lambda i: (0, 0)),
                   pl.BlockSpec((1, D_MODEL), lambda i: (0, 0)),
                   pl.BlockSpec((1, GLA_HV), lambda i: (0, 0))),
        out_shape=(sds((t, D_MODEL), F32), sds((t, MLA_WIDTH), F32), sds((t, GLA_DV), BF16),
                   sds((t, gw), BF16),
                   sds((t, D_MODEL), BF16), sds((t, MLA_WIDTH), BF16), sds((t, GLA_DV), BF16),
                   sds((t, D_MODEL), BF16), sds((t, D_MODEL), BF16),
                   sds((1, LANE), F32), sds((1, D_MODEL), F32), sds((1, GLA_HV), F32)),
        compiler_params=_cparams(("arbitrary",)),
    )(o_mla, proj, proj, proj, proj, o_gla, x, target, g_gla, g_final, w_pm, w_pg, w_o)


def _mla_prep_bwd(dq, dk, dv, dla, pre, proj, rq, rkv, g_q, g_kv, w_uq_p, w_k_p, w_v, w_gate_p,
                  rc, rsn, rsp):
    t = proj.shape[0]
    tm = min(256, t)
    gw = P_GROUPS[2][1]

    def body(dq_ref, dk_ref, dv_ref, dla_ref, pre_ref, cq_ref, ckv_ref, rq_ref, rkv_ref,
             gq_ref, gkv_ref, wuq_ref, wk_ref, wv_ref, wg_ref, c_ref, sn_ref, sp_ref,
             dg_ref, dqpre_ref, dpre_ref, dgq_ref, dgkv_ref, dbg_ref):
        @pl.when(pl.program_id(0) == 0)
        def _():
            dgq_ref[...] = jnp.zeros_like(dgq_ref)
            dgkv_ref[...] = jnp.zeros_like(dgkv_ref)
            dbg_ref[...] = jnp.zeros_like(dbg_ref)

        c, sn, sp = c_ref[...], sn_ref[...], sp_ref[...]
        dkr = jnp.zeros((tm, LANE), F32)
        for h in range(MLA_HEADS):
            sl = slice(h * HEAD_PAD, (h + 1) * HEAD_PAD)
            dqpre_ref[:, sl] = _rope_bwd(dq_ref[:, sl].astype(F32), c, sn, sp).astype(BF16)
            dkr = dkr + dk_ref[:, sl]
        dcqn = _dot_nt(dqpre_ref[...], wuq_ref[...])
        rq = rq_ref[...]
        xh = cq_ref[:, :MLA_Q_RANK] * rq
        dgq_ref[...] += jnp.sum(dcqn * xh, axis=0, keepdims=True)
        dxh = dcqn * gq_ref[...]
        dcq = rq * (dxh - xh * jnp.mean(dxh * xh, axis=-1, keepdims=True))
        dg_ref[:, :MLA_Q_RANK] = dcq.astype(BF16)
        dg_ref[:, MLA_Q_RANK:512] = jnp.zeros((tm, 512 - MLA_Q_RANK), BF16)

        dckvn = _dot_nt(dk_ref[...].astype(BF16), wk_ref[...]) + \
            _dot_nt(dv_ref[...].astype(BF16), wv_ref[...])
        rkv = rkv_ref[...]
        xh = ckv_ref[...] * rkv
        dgkv_ref[...] += jnp.sum(dckvn * xh, axis=0, keepdims=True)
        dxh = dckvn * gkv_ref[...]
        dg_ref[:, P_CKV - P_CQ:P_CKV - P_CQ + MLA_KV_RANK] = (
            rkv * (dxh - xh * jnp.mean(dxh * xh, axis=-1, keepdims=True))).astype(BF16)

        dlog_a = _dot_exact(_chunk_tri(tm, False), dla_ref[...])
        dpre = dlog_a * (1.0 / GLA_GATE_NORM) * (1.0 - _sigmoid(pre_ref[...]))
        dbg_ref[...] += jnp.sum(dpre, axis=0, keepdims=True)
        dpre = dpre.astype(BF16)
        dpre_ref[...] = dpre
        lane = lax.broadcasted_iota(jnp.int32, (tm, LANE), 1)
        in_kr = jnp.logical_and(lane >= MISC_KR, lane < MISC_KR + MLA_ROPE)
        dmisc = jnp.where(in_kr, _rope_bwd(dkr, c, sn, sp), 0.0) + _dot_nt(dpre, wg_ref[...])
        dg_ref[:, P_MISC - P_CQ:P_MISC - P_CQ + LANE] = dmisc.astype(BF16)

    hq = MLA_HEADS * HEAD_PAD
    row = lambda w: pl.BlockSpec((tm, w), lambda i: (i, 0))
    full = lambda a: pl.BlockSpec(a.shape, lambda i: (0, 0))
    acc = lambda w: pl.BlockSpec((1, w), lambda i: (0, 0))
    sds = jax.ShapeDtypeStruct
    return pl.pallas_call(
        body, name="mla_prep_bwd",
        grid=(t // tm,),
        in_specs=[row(hq), row(hq), row(MLA_WIDTH), row(GLA_DK), row(GLA_DK),
                  pl.BlockSpec((tm, 512), lambda i: (i, _rel(P_CQ) // 512)),
                  pl.BlockSpec((tm, MLA_KV_RANK), lambda i: (i, _rel(P_CKV) // MLA_KV_RANK)),
                  row(1), row(1), full(g_q), full(g_kv), full(w_uq_p), full(w_k_p), full(w_v),
                  full(w_gate_p), row(LANE), row(LANE), row(LANE)],
        out_specs=(row(gw), row(hq), row(GLA_DK),
                   acc(MLA_Q_RANK), acc(MLA_KV_RANK), acc(GLA_DK)),
        out_shape=(sds((t, gw), BF16), sds((t, hq), BF16), sds((t, GLA_DK), BF16),
                   sds((1, MLA_Q_RANK), F32), sds((1, MLA_KV_RANK), F32), sds((1, GLA_DK), F32)),
        compiler_params=_cparams(("arbitrary",)),
    )(dq, dk, dv, dla, pre, proj, proj, rq, rkv, g_q, g_kv, w_uq_p, w_k_p, w_v, w_gate_p,
      rc, rsn, rsp)


def _inproj_bwd(dgroups, w_pts, x, rstd, g_in, dx2, after):
    t = x.shape[0]
    tm = min(256, t)

    def body(d0_ref, d1_ref, d2_ref, w0_ref, w1_ref, w2_ref, x_ref, r_ref, g_ref, dx2_ref, after_ref,
             dx_ref, dg_ref):
        del after_ref

        @pl.when(pl.program_id(0) == 0)
        def _():
            dg_ref[...] = jnp.zeros_like(dg_ref)

        dh = jnp.zeros((tm, D_MODEL), F32)
        for d_ref, w_ref in zip((d0_ref, d1_ref, d2_ref), (w0_ref, w1_ref, w2_ref)):
            dh = dh + _dot(d_ref[...], w_ref[...])
        r = r_ref[...]
        xh = x_ref[...] * r
        dg_ref[...] += jnp.sum(dh * xh, axis=0, keepdims=True)
        dxh = dh * g_ref[...]
        dx_ref[...] = dx2_ref[...] + r * (dxh - xh * jnp.mean(dxh * xh, axis=-1, keepdims=True))

    row = lambda w: pl.BlockSpec((tm, w), lambda i: (i, 0))
    return pl.pallas_call(
        body, name="inproj_bwd",
        grid=(t // tm,),
        in_specs=[row(w) for _, w in P_GROUPS]
        + [pl.BlockSpec((w, D_MODEL), lambda i: (0, 0)) for _, w in P_GROUPS]
        + [row(D_MODEL), row(1), pl.BlockSpec((1, D_MODEL), lambda i: (0, 0)), row(D_MODEL),
           pl.BlockSpec(memory_space=pl.ANY)],
        out_specs=(row(D_MODEL), pl.BlockSpec((1, D_MODEL), lambda i: (0, 0))),
        out_shape=(jax.ShapeDtypeStruct((t, D_MODEL), F32),
                   jax.ShapeDtypeStruct((1, D_MODEL), F32)),
        compiler_params=_cparams(("arbitrary",)),
    )(*dgroups, *w_pts, x, rstd, g_in, dx2, after)


def _matmul(name, a, b, tm, tn, dtype=F32, after=None):
    kk, m = a.shape
    n = b.shape[1]
    extra = [] if after is None else [after]

    def body(a_ref, b_ref, *rest):
        rest[-1][...] = _dot_tn(a_ref[...].astype(BF16), b_ref[...].astype(BF16)).astype(dtype)

    return pl.pallas_call(
        body, name=name,
        grid=(n // tn, m // tm),
        in_specs=[pl.BlockSpec((kk, tm), lambda j, i: (0, i)),
                  pl.BlockSpec((kk, tn), lambda j, i: (0, j))]
        + [pl.BlockSpec(memory_space=pl.ANY) for _ in extra],
        out_specs=pl.BlockSpec((tm, tn), lambda j, i: (i, j)),
        out_shape=jax.ShapeDtypeStruct((m, n), dtype),
        compiler_params=_cparams(("arbitrary", "arbitrary")),
    )(a, b, *extra)


def _adamw_update(part_refs, w_ref, m_ref, v_ref, g_ref, d_ref, nm_ref, nv_ref):
    g = part_refs[0][...].astype(F32)
    for p_ref in part_refs[1:]:
        g = g + p_ref[...].astype(F32)
    m_new = ADAM_B1 * m_ref[...] + (1.0 - ADAM_B1) * g
    v_new = ADAM_B2 * v_ref[...] + (1.0 - ADAM_B2) * (g * g)
    m_hat = m_new / (1.0 - ADAM_B1 ** ADAM_STEP)
    v_hat = v_new / (1.0 - ADAM_B2 ** ADAM_STEP)
    g_ref[...] = g
    nm_ref[...] = m_new
    nv_ref[...] = v_new
    d_ref[...] = -ADAM_LR * (m_hat / (jnp.sqrt(v_hat) + ADAM_EPS) + ADAM_WD * w_ref[...])


def _adamw_rows(name, parts, w, m, v, tr, first=None):
    _, rows, cols = w.shape
    slots = parts.shape[0]

    def body(*refs):
        lead_refs, p_ref = ([], refs[0]) if first is None else ([refs[0]], refs[1])
        _adamw_update(lead_refs + [p_ref.at[q] for q in range(slots)], *refs[len(lead_refs) + 1:])

    blk = pl.BlockSpec((None, tr, cols), lambda i: (0, i, 0))
    out = jax.ShapeDtypeStruct((1, rows, cols), F32)
    lead = [] if first is None else [pl.BlockSpec((tr, cols), lambda i: (i, 0))]
    return pl.pallas_call(
        body, name=name,
        grid=(rows // tr,),
        in_specs=lead + [pl.BlockSpec((slots, tr, cols), lambda i: (0, i, 0)), blk, blk, blk],
        out_specs=(blk, blk, blk, blk),
        out_shape=(out, out, out, out),
        compiler_params=_cparams(("arbitrary",)),
    )(*([] if first is None else [first]), parts, w, m, v)


def _adamw_transposed(name, first, parts, w, m, v, tl):
    _, rows, cols = w.shape
    slots, padded = parts.shape[:2]

    def body(f_ref, p_ref, *refs):
        _adamw_update([f_ref.at[pl.ds(0, cols)]]
                      + [p_ref.at[q, pl.ds(0, cols)] for q in range(slots)], *refs)

    blk = pl.BlockSpec((cols, None, tl), lambda i: (0, 0, i))
    out = jax.ShapeDtypeStruct((cols, 1, rows), F32)
    res = pl.pallas_call(
        body, name=name,
        grid=(rows // tl,),
        in_specs=[pl.BlockSpec((padded, tl), lambda i: (0, i)),
                  pl.BlockSpec((slots, padded, tl), lambda i: (0, 0, i)), blk, blk, blk],
        out_specs=(blk, blk, blk, blk),
        out_shape=(out, out, out, out),
        compiler_params=_cparams(("arbitrary",)),
    )(first, parts, *[a.transpose(2, 0, 1) for a in (w, m, v)])
    return [r.transpose(1, 2, 0) for r in res]


def _adamw_group(firsts, parts, ws, ms, vs):
    n = len(ws)

    def body(*refs):
        ins, outs = refs[:5 * n], refs[5 * n:]
        x, y, c = _mesh_pos()
        for a in range(n):
            _adamw_update([ins[a].at[4 * x + 2 * y + c]]
                          + [ins[n + a].at[q] for q in range(ins[n + a].shape[0])],
                          *[r.at[0] for r in (ins[2 * n + a], ins[3 * n + a], ins[4 * n + a])],
                          *[r.at[0] for r in outs[4 * a:4 * a + 4]])

    vmem = lambda k: [pl.BlockSpec(memory_space=pltpu.VMEM) for _ in range(k)]
    out_shape = []
    for w in ws:
        out_shape += [jax.ShapeDtypeStruct(w.shape, F32)] * 4
    res = pl.pallas_call(
        body, name="adamw_small_weights",
        in_specs=vmem(5 * n), out_specs=tuple(vmem(4 * n)), out_shape=tuple(out_shape),
        compiler_params=_cparams(),
    )(*firsts, *parts, *ws, *ms, *vs)
    return [res[4 * a:4 * a + 4] for a in range(n)]


def _rope_tables(positions):
    half = MLA_ROPE // 2
    freqs = ROPE_THETA ** (-jnp.arange(half, dtype=F32) / half)
    ang = positions.astype(F32).reshape(-1, 1) * freqs
    cos, sin = jnp.cos(ang), jnp.sin(ang)
    t = ang.shape[0]
    one, zero = jnp.ones((t, MLA_NOPE), F32), jnp.zeros((t, half), F32)
    tail = jnp.zeros((t, LANE - MLA_QK), F32)
    rc = jnp.concatenate([one, cos, cos, tail], axis=1)
    rsn = jnp.concatenate([0.0 * one, -sin, zero, tail], axis=1)
    rsp = jnp.concatenate([0.0 * one, zero, sin, tail], axis=1)
    return rc, rsn, rsp


def _cols_full(g):
    return g.transpose(1, 0, 2)


def kernel(x, positions, g_in, w_in, g_q, w_uq, g_kv, w_ukv, w_gla_gate, b_gla_gate, g_gla, w_proj_mla, w_proj_gla, w_out, g_final, loss_target, m_g_in, m_w_in, m_g_q, m_w_uq, m_g_kv, m_w_ukv, m_w_gla_gate, m_b_gla_gate, m_g_gla, m_w_proj_mla, m_w_proj_gla, m_w_out, m_g_final, v_g_in, v_w_in, v_g_q, v_w_uq, v_g_kv, v_w_ukv, v_w_gla_gate, v_b_gla_gate, v_g_gla, v_w_proj_mla, v_w_proj_gla, v_w_out, v_g_final):
    t = x.shape[1]
    x2d = x.reshape(t, D_MODEL)
    tgt = loss_target.reshape(t, D_MODEL)
    g_final2 = g_final.reshape(1, D_MODEL)
    sharded = [(w_in, m_w_in, v_w_in), (w_uq, m_w_uq, v_w_uq), (w_ukv, m_w_ukv, v_w_ukv),
               (w_gla_gate, m_w_gla_gate, v_w_gla_gate), (w_proj_mla, m_w_proj_mla, v_w_proj_mla),
               (w_proj_gla, m_w_proj_gla, v_w_proj_gla), (w_out, m_w_out, v_w_out)]

    w_in_t = w_in.transpose(2, 0, 1).reshape(SHARD_COLS, D_MODEL)
    everyone = tuple(range(N_DEV))
    w_in_b = w_in_t.astype(BF16)
    b_uq, b_ukv, b_gate, b_pm, b_pg, b_o = [s[0][0].astype(BF16) for s in sharded[1:]]
    stages = ((0, 2, 4, 6), (1, 3, 5, 7))
    where = {d: (k, i) for k, srcs in enumerate(stages) for i, d in enumerate(srcs)}
    g_in_1, g_uq, g_ukv, g_gate = _all_gather(
        "all_gather_first", [w_in_b, b_uq, b_ukv, b_gate], [stages[0]] + [everyone] * 3)
    w_uq_p = jnp.pad(_cols_full(g_uq), ((0, 0), (0, 0), (0, HEAD_PAD - MLA_QK))).reshape(
        MLA_Q_RANK, MLA_HEADS * HEAD_PAD)
    ukv = _cols_full(g_ukv)
    w_k_p = jnp.pad(ukv[:, :, :MLA_NOPE], ((0, 0), (0, 0), (0, HEAD_PAD - MLA_NOPE))).reshape(
        MLA_KV_RANK, MLA_HEADS * HEAD_PAD)
    w_v = ukv[:, :, MLA_NOPE:].reshape(MLA_KV_RANK, MLA_WIDTH)
    w_gate_p = jnp.pad(_cols_full(g_gate).reshape(GLA_GATE_RANK, GLA_DK),
                       ((MISC_ALR, LANE - MISC_ALR - GLA_GATE_RANK), (0, 0)))
    rc, rsn, rsp = _rope_tables(positions)

    w_lat = _weights_to_p("weights_latents", [g_in_1], where, 2)
    proj_lat, h, rstd = _inproj(x2d, g_in, w_lat)
    q, k, v, log_a, pre, cqn, ckvn, rq, rkv, misc = _mla_prep(
        proj_lat, g_q, g_kv, w_uq_p, w_k_p, w_v, w_gate_p, b_gla_gate, rc, rsn, rsp)
    o_mla, lse, (g_in_2, g_pm, g_pg, g_o) = _mla_attn_fwd(
        q, k, v, [w_in_b, b_pm, b_pg, b_o], [stages[1]] + [everyone] * 3)
    w_gla = _weights_to_p("weights_gla", [g_in_1, g_in_2], where, 0)
    proj_gla = _proj("inproj_gla", h, w_gla)
    o_gla, states = _gla_fwd(proj_gla, log_a)
    w_out_path = _weights_to_p("weights_out_path", [g_in_1, g_in_2], where, 1)
    proj_out = _proj("inproj_out_path", h, w_out_path)
    w_in_p = (w_gla, w_out_path, w_lat)
    w_pm = _cols_full(g_pm).reshape(MLA_WIDTH, D_MODEL)
    w_pg = g_pg.reshape(GLA_DV, D_MODEL)
    w_o = g_o.reshape(D_MODEL, D_MODEL)

    (dx2, do_mla, do_gla, d_out, merged, um, ug, dym, dyg, loss_p, dg_final,
     dg_gla) = _post(o_mla, proj_out, o_gla, x2d, tgt, g_gla, g_final2, w_pm, w_pg, w_o)

    p_pm = _matmul("dw_proj_mla", um, dym, 512, D_MODEL, BF16).reshape(
        MLA_WIDTH, N_DEV, D_MODEL // N_DEV).transpose(1, 0, 2)
    p_pg = _matmul("dw_proj_gla", ug, dyg, 512, D_MODEL, BF16).reshape(N_DEV, -1, D_MODEL)
    p_o = _matmul("dw_out", merged, dx2, 512, D_MODEL, BF16).reshape(N_DEV, -1, D_MODEL)
    own_in = jnp.zeros((SHARD_PAD, D_MODEL), BF16)
    land_in = lax.empty((PEERS, SHARD_PAD, D_MODEL), BF16)
    dw_groups, started, lands = {}, [], [land_in]

    def reduce_scatter_stage(s, dests, own_in, extra=()):
        parts_in, own_in = _grads_to_shards("grads_to_shards_%d" % s, dw_groups, dests, own_in)
        first = len(lands)
        lands.extend(lax.empty((PEERS,) + p.shape[1:], BF16) for p in extra)
        idx = [0] + list(range(first, len(lands)))
        all_dests = [[(i, d, r0, r1) for i, (d, ranges) in enumerate(dests) for r0, r1 in ranges]]
        all_dests += [_whole(everyone, p.shape[1]) for p in extra]
        sems, parts, new_lands, token = _ici_start(
            "ici_start_%d" % s, [parts_in] + list(extra), [lands[i] for i in idx], all_dests)
        for a, i in enumerate(idx):
            lands[i] = new_lands[a]
            started.append((sems[a][0], sems[a][1], parts[a], i, all_dests[a]))
        return own_in, token

    def late_small_stage(arrays):
        idx = list(range(len(lands), len(lands) + len(arrays)))
        lands.extend(lax.empty((PEERS,) + p.shape[1:], BF16) for p in arrays)
        all_dests = [_whole(everyone, p.shape[1]) for p in arrays]
        sems, parts, new_lands, token = _ici_start(
            "ici_start_4", list(arrays), [lands[i] for i in idx], all_dests)
        for a, i in enumerate(idx):
            lands[i] = new_lands[a]
            started.append((sems[a][0], sems[a][1], parts[a], i, all_dests[a]))
        return token

    dw_groups[1] = _matmul("dw_in_1", d_out, h, 896, D_MODEL, BF16)
    full = [(0, SHARD_PAD)]
    own_in, token = reduce_scatter_stage(
        1, [(5, full), (6, full), (7, full), (0, [(672, SHARD_PAD)]), (1, [(0, 384)]),
            (4, [(96, SHARD_PAD)])], own_in, (p_pm, p_pg, p_o))
    d_gla, dla = _gla_bwd(proj_gla, log_a, do_gla, states, token)
    dw_groups[0] = _matmul("dw_in_0", d_gla, h, 1024, D_MODEL, BF16)
    own_in, token = reduce_scatter_stage(
        2, [(1, [(384, SHARD_PAD)]), (2, full), (3, full), (4, [(0, 64)])], own_in)
    dq, dk, dv = _mla_attn_bwd(q, k, v, o_mla, do_mla, lse, token)
    d_lat, dqpre, dpre, dg_q, dg_kv, db_gate = _mla_prep_bwd(
        dq, dk, dv, dla, pre, proj_lat, rq, rkv, g_q, g_kv, w_uq_p, w_k_p, w_v, w_gate_p, rc, rsn, rsp)
    dw_groups[2] = _matmul("dw_in_2", d_lat, h, 896, D_MODEL, BF16)
    own_in, token = reduce_scatter_stage(3, [(0, [(0, 672)]), (4, [(64, 96)])], own_in)
    dw_uq = _matmul("dw_uq", cqn, dqpre, MLA_Q_RANK, D_MODEL, BF16, after=token)
    p_uq = dw_uq.reshape(MLA_Q_RANK, MLA_HEADS, HEAD_PAD)[:, :, :MLA_QK].transpose(1, 0, 2)
    dw_k = _matmul("dw_uk", ckvn, dk, MLA_KV_RANK, D_MODEL, BF16)
    dw_v = _matmul("dw_uv", ckvn, dv, MLA_KV_RANK, 512, BF16)
    p_ukv = jnp.concatenate(
        [dw_k.reshape(MLA_KV_RANK, MLA_HEADS, HEAD_PAD)[:, :, :MLA_NOPE],
         dw_v.reshape(MLA_KV_RANK, MLA_HEADS, MLA_VDIM)], axis=2).transpose(1, 0, 2)
    dw_gate = _matmul("dw_gate", misc, dpre, LANE, 512, BF16)
    p_gate = dw_gate[MISC_ALR:MISC_ALR + GLA_GATE_RANK].reshape(
        GLA_GATE_RANK, N_DEV, GLA_DK // N_DEV).transpose(1, 0, 2)
    token = late_small_stage((p_uq, p_ukv, p_gate))
    grad_x, dg_in = _inproj_bwd((d_gla, d_out, d_lat), w_in_p, x2d, rstd, g_in, dx2, token)
    small = jnp.concatenate([dg_in.reshape(-1), dg_q.reshape(-1), dg_kv.reshape(-1),
                             db_gate.reshape(-1), dg_gla.reshape(-1), dg_final.reshape(-1),
                             loss_p[0, :1]])
    small = jnp.pad(small, (0, SMALL_ROWS * LANE - small.shape[0])).reshape(SMALL_ROWS, LANE)

    (small_all,) = _all_gather("all_gather_small", [small], [everyone])
    lands = _ici_wait("ici_wait", started, lands, small_all)
    big = [_adamw_transposed("adamw_w_in", own_in, lands[0], *sharded[0], 512)]
    big += _adamw_group([p_uq, p_ukv, p_gate, p_pm, p_pg, p_o], list(lands[4:7]) + list(lands[1:4]),
                        *[[s[j] for s in sharded[1:]] for j in range(3)])
    replicated = [(g_in, m_g_in, v_g_in), (g_q, m_g_q, v_g_q), (g_kv, m_g_kv, v_g_kv),
                  (b_gla_gate, m_b_gla_gate, v_b_gla_gate), (g_gla, m_g_gla, v_g_gla),
                  (g_final, m_g_final, v_g_final)]
    spacks = [jnp.pad(jnp.concatenate([s[j].reshape(-1) for s in replicated]),
                      (0, SMALL_ROWS * LANE - sum(SMALL_SIZES))).reshape(1, SMALL_ROWS, LANE)
              for j in range(3)]
    tiny = _adamw_rows("adamw_gains", small_all, spacks[0], spacks[1], spacks[2], SMALL_ROWS)

    outs = {}
    names = ("w_in", "w_uq", "w_ukv", "w_gla_gate", "w_proj_mla", "w_proj_gla", "w_out")
    for j, kind in enumerate(("grad", "delta", "new_m", "new_v")):
        for name, res in zip(names, big):
            outs[kind, name] = res[j]
        flat = tiny[j].reshape(-1)
        off = 0
        for name, size in zip(("g_in", "g_q", "g_kv", "b_gla_gate", "g_gla", "g_final"), SMALL_SIZES):
            shape = (size,) if name == "g_final" else (1, size)
            outs[kind, name] = flat[off:off + size].reshape(shape)
            off += size
    loss = tiny[0].reshape(-1)[sum(SMALL_SIZES)]
    order = ("g_in", "w_in", "g_q", "w_uq", "g_kv", "w_ukv", "w_gla_gate", "b_gla_gate", "g_gla",
             "w_proj_mla", "w_proj_gla", "w_out", "g_final")
    result = [loss, grad_x.reshape(1, t, D_MODEL)]
    for kind in ("grad", "delta", "new_m", "new_v"):
        result += [outs[kind, name] for name in order]
    return tuple(result)
```

```python
import jax
import jax.numpy as jnp
from jax import lax
from jax.experimental import pallas as pl
from jax.experimental.pallas import tpu as pltpu

F32 = jnp.float32
BF16 = jnp.bfloat16
MESH = pl.DeviceIdType.MESH
N_DEV = 8

D_MODEL = 1024
EPS = 1e-6
MLA_HEADS = 8
MLA_NOPE = 64
MLA_ROPE = 32
MLA_VDIM = 64
MLA_Q_RANK = 384
MLA_KV_RANK = 256
MLA_QK = MLA_NOPE + MLA_ROPE
MLA_WIDTH = MLA_HEADS * MLA_VDIM
ROPE_THETA = 10000.0
GLA_HEADS = 4
GLA_DK = 512
GLA_DV = 1024
GLA_HK = 128
GLA_HV = 256
GLA_GATE_RANK = 16
GLA_GATE_NORM = 16.0
GLA_CHUNK = 64
GLA_CHUNKS_PER_STEP = 8
D_IN = 6320

ADAM_LR = 0.001
ADAM_B1 = 0.9
ADAM_B2 = 0.999
ADAM_EPS = 1e-08
ADAM_WD = 0.01
ADAM_STEP = 10

LANE = 128
HEAD_PAD = 128
VMEM_LIMIT = 48 * 1024 * 1024

P_VG, P_QG, P_KG = 0, 1024, 1536
P_ZGLA, P_GMLA, P_GGLA, P_ZMLA = 2048, 3072, 4096, 5120
P_CQ, P_CKV, P_MISC = 5632, 6144, 6400
P_TOTAL = 6528
P_GROUPS = ((0, 2048), (2048, 3584), (5632, 896))
MISC_KR = 64
MISC_ALR = 96
SHARD_COLS = D_IN // N_DEV
SHARD_PAD = 800
P_COMPONENTS = ((0, 384, P_CQ), (384, 256, P_CKV), (640, 32, P_MISC + MISC_KR), (672, 512, P_ZMLA),
                (1184, 512, P_QG), (1696, 512, P_KG), (2208, 1024, P_VG),
                (3232, 16, P_MISC + MISC_ALR), (3248, 1024, P_ZGLA), (4272, 1024, P_GMLA),
                (5296, 1024, P_GGLA))

SMALL_SIZES = (1024, 384, 256, 512, 256, 1024)
SMALL_ROWS = 32


def _segments():
    segs = []
    for g0, n, p0 in P_COMPONENTS:
        g = g0
        while g < g0 + n:
            d = g // SHARD_COLS
            end = min(g0 + n, (d + 1) * SHARD_COLS)
            segs.append((d, g - d * SHARD_COLS, end - g, p0 + g - g0))
            g = end
    return segs


def _group_of(p0):
    return max(i for i, (off, _) in enumerate(P_GROUPS) if off <= p0)


def _rel(p0):
    return p0 - P_GROUPS[_group_of(p0)][0]


def _cparams(sem=None):
    if sem is None:
        return pltpu.CompilerParams(vmem_limit_bytes=VMEM_LIMIT)
    return pltpu.CompilerParams(dimension_semantics=sem, vmem_limit_bytes=VMEM_LIMIT)


def _sigmoid(v):
    return 1.0 / (1.0 + jnp.exp(-v))


def _dot(a, b):
    return jnp.dot(a, b, preferred_element_type=F32)


def _dot_nt(a, b):
    return lax.dot_general(a, b, (((1,), (1,)), ((), ())), preferred_element_type=F32)


def _dot_tn(a, b):
    return lax.dot_general(a, b, (((0,), (0,)), ((), ())), preferred_element_type=F32)


def _dot_exact(a, b):
    return jnp.dot(a, b, preferred_element_type=F32, precision=lax.Precision.HIGHEST)


def _rope_fwd(blk, c, sn, sp):
    return blk * c + pltpu.roll(blk, LANE - 16, 1) * sn + pltpu.roll(blk, 16, 1) * sp


def _rope_bwd(blk, c, sn, sp):
    return blk * c + pltpu.roll(blk * sn, 16, 1) + pltpu.roll(blk * sp, LANE - 16, 1)


def _mesh_pos():
    return lax.axis_index("x"), lax.axis_index("y"), lax.axis_index("c")


def _hbm_specs(n):
    return [pl.BlockSpec(memory_space=pltpu.HBM) for _ in range(n)]


def _dev(d):
    return d >> 2, (d >> 1) & 1, d & 1


def _gather_plan(shards, sources):
    na, most = len(shards), max(len(s) for s in sources)
    out_shape = [jax.ShapeDtypeStruct((len(srcs),) + s.shape, s.dtype)
                 for s, srcs in zip(shards, sources)]
    sems = [pltpu.SemaphoreType.DMA((na, most)) for _ in range(3)]
    sems += [pltpu.SemaphoreType.DMA((na, most, 3))]
    sems += [pltpu.SemaphoreType.DMA((na, most)) for _ in range(3)]
    return out_shape, sems


def _gather_hooks(x_refs, out_refs, sems, sources):
    local_sems, d2d_send, d2d_recv, ici_send, ici_recv, fwd_send, fwd_recv = sems
    x, y, c = _mesh_pos()
    chips = [(1 - x, y), (x, 1 - y), (1 - x, 1 - y)]
    items = []
    for a, srcs in enumerate(sources):
        for i, d in enumerate(srcs):
            dx, dy, dc = _dev(d)
            near = jnp.logical_and(x == dx, y == dy)
            far = jnp.logical_not(near)
            slot = out_refs[a].at[i]

            def remote(src, to, send_sem, recv_sem, slot=slot):
                return pltpu.make_async_remote_copy(
                    src_ref=src, dst_ref=slot, send_sem=send_sem, recv_sem=recv_sem,
                    device_id=to, device_id_type=MESH)

            items.append(dict(
                me=jnp.logical_and(near, c == dc), sibling=jnp.logical_and(near, c != dc),
                relay=jnp.logical_and(far, c == dc), behind=jnp.logical_and(far, c != dc),
                local=pltpu.make_async_copy(x_refs[a], slot, local_sems.at[a, i]),
                to_sibling=remote(x_refs[a], (x, y, 1 - c), d2d_send.at[a, i], d2d_recv.at[a, i]),
                to_chips=[remote(x_refs[a], (*chip, c), ici_send.at[a, i, j], ici_recv.at[a, i])
                          for j, chip in enumerate(chips)],
                forward=remote(slot, (x, y, 1 - c), fwd_send.at[a, i], fwd_recv.at[a, i])))

    def start():
        for it in items:
            @pl.when(it["me"])
            def _(it=it):
                it["local"].start()
                it["to_sibling"].start()
                for cp in it["to_chips"]:
                    cp.start()

    def finish():
        for it in items:
            @pl.when(it["relay"])
            def _(it=it):
                it["to_chips"][0].wait_recv()
                it["forward"].start()
        for it in items:
            pl.when(it["sibling"])(it["to_sibling"].wait_recv)
            pl.when(it["behind"])(it["forward"].wait_recv)
            pl.when(it["relay"])(it["forward"].wait_send)

            @pl.when(it["me"])
            def _(it=it):
                it["local"].wait()
                it["to_sibling"].wait_send()
                for cp in it["to_chips"]:
                    cp.wait_send()

    return start, finish


def _all_gather(name, shards, sources):
    n = len(shards)
    out_shape, sems = _gather_plan(shards, sources)

    def body(*refs):
        start, finish = _gather_hooks(refs[:n], refs[n:2 * n], refs[2 * n:], sources)
        start()
        finish()

    return pl.pallas_call(
        body, name=name,
        out_shape=tuple(out_shape),
        in_specs=_hbm_specs(n), out_specs=tuple(_hbm_specs(n)),
        scratch_shapes=sems,
        compiler_params=_cparams(),
    )(*shards)


PEERS = N_DEV - 1


def _whole(dests, rows):
    return [(i, d, 0, rows) for i, d in enumerate(dests)]


def _ici_copies(p_ref, land_ref, send_sems, recv_sems, pieces):
    x, y, c = _mesh_pos()
    sends, arrivals = [], []

    def rows_of(ref, j, r0, r1):
        return ref.at[j] if (r0, r1) == (0, ref.shape[1]) else ref.at[j, pl.ds(r0, r1 - r0)]

    for p, (i, d, r0, r1) in enumerate(pieces):
        dx, dy, dc = _dev(d)
        k = (4 * (x != dx).astype(jnp.int32) + 2 * (y != dy).astype(jnp.int32)
             + (c != dc).astype(jnp.int32))
        slot = jnp.maximum(k - 1, 0)
        sends.append((k > 0, pltpu.make_async_remote_copy(
            src_ref=rows_of(p_ref, i, r0, r1), dst_ref=rows_of(land_ref, slot, r0, r1),
            send_sem=send_sems.at[p], recv_sem=recv_sems.at[p * PEERS + slot],
            device_id=(dx, dy, dc), device_id_type=MESH)))
        arrivals.append((k == 0, [pltpu.make_async_remote_copy(
            src_ref=rows_of(p_ref, i, r0, r1), dst_ref=rows_of(land_ref, r, r0, r1),
            send_sem=send_sems.at[p], recv_sem=recv_sems.at[p * PEERS + r],
            device_id=(dx, dy, dc), device_id_type=MESH) for r in range(PEERS)]))
    return sends, arrivals


def _ici_start(name, hs, lands, dests):
    na = len(hs)

    def body(*refs):
        h_refs, land_refs, sems = refs[:na], refs[na:2 * na], refs[2 * na:4 * na]
        token = refs[-1]
        for a in range(na):
            sends, _ = _ici_copies(h_refs[a], land_refs[a], sems[2 * a], sems[2 * a + 1], dests[a])
            for go, cp in sends:
                pl.when(go)(cp.start)
        token[...] = jnp.zeros_like(token)

    hbm, sem = pl.BlockSpec(memory_space=pltpu.HBM), pl.BlockSpec(memory_space=pltpu.SEMAPHORE)
    sem_shapes = []
    for a in range(na):
        sem_shapes += [pltpu.SemaphoreType.DMA((len(dests[a]),)),
                       pltpu.SemaphoreType.DMA((len(dests[a]) * PEERS,))]
    res = pl.pallas_call(
        body, name=name,
        out_shape=tuple(sem_shapes) + tuple(pltpu.HBM(v.shape, v.dtype) for v in list(hs) + list(lands))
        + (jax.ShapeDtypeStruct((8, LANE), F32),),
        in_specs=(hbm,) * (2 * na),
        out_specs=(sem,) * (2 * na) + (hbm,) * (2 * na) + (pl.BlockSpec(memory_space=pltpu.VMEM),),
        input_output_aliases={i: 2 * na + i for i in range(2 * na)},
        compiler_params=pltpu.CompilerParams(
            has_side_effects=pltpu.SideEffectType.DATAFLOW_SIDE_EFFECTING,
            vmem_limit_bytes=VMEM_LIMIT),
    )(*[pltpu.with_memory_space_constraint(v, pltpu.HBM) for v in list(hs) + list(lands)])
    sems = [(res[2 * a], res[2 * a + 1]) for a in range(na)]
    return sems, res[2 * na:3 * na], res[3 * na:4 * na], res[-1]


def _ici_wait(name, started, lands, after):
    k, nl = len(started), len(lands)

    def body(*refs):
        land_refs = refs[3 * k:3 * k + nl]
        for s in range(k):
            h_ref, send_sems, recv_sems = refs[3 * s:3 * s + 3]
            sends, arrivals = _ici_copies(h_ref, land_refs[started[s][3]], send_sems, recv_sems,
                                          started[s][4])
            for go, cp in sends:
                pl.when(go)(cp.wait_send)
            for here, cps in arrivals:
                for cp in cps:
                    pl.when(here)(cp.wait_recv)

    hbm, sem = pl.BlockSpec(memory_space=pltpu.HBM), pl.BlockSpec(memory_space=pltpu.SEMAPHORE)
    operands, specs = [], []
    for send_sems, recv_sems, h, _, _ in started:
        operands += [h, send_sems, recv_sems]
        specs += [hbm, sem, sem]
    return pl.pallas_call(
        body, name=name,
        out_shape=tuple(pltpu.HBM(v.shape, v.dtype) for v in lands),
        in_specs=tuple(specs) + (hbm,) * nl + (pl.BlockSpec(memory_space=pl.ANY),),
        out_specs=(hbm,) * nl,
        input_output_aliases={3 * k + i: i for i in range(nl)},
        compiler_params=pltpu.CompilerParams(
            has_side_effects=pltpu.SideEffectType.DATAFLOW_SIDE_EFFECTING,
            vmem_limit_bytes=VMEM_LIMIT),
    )(*operands, *lands, after)


def _weights_to_p(name, gathered, where, group):
    tl = 512
    off, width = P_GROUPS[group]
    segs = sorted([s for s in _segments() if _group_of(s[3]) == group], key=lambda s: s[3])
    used = sorted({where[s[0]][0] for s in segs})

    def body(*refs):
        g_refs, o_ref = dict(zip(used, refs[:-1])), refs[-1]
        pieces, pos = [], off
        for d, c0, n, p0 in segs:
            if p0 > pos:
                pieces.append(jnp.zeros((p0 - pos, tl), F32))
            k, slot = where[d]
            pieces.append(g_refs[k][slot, c0:c0 + n, :].astype(F32))
            pos = p0 + n
        if off + width > pos:
            pieces.append(jnp.zeros((off + width - pos, tl), F32))
        o_ref[...] = jnp.concatenate(pieces, axis=0).astype(BF16)

    return pl.pallas_call(
        body, name=name,
        grid=(D_MODEL // tl,),
        in_specs=[pl.BlockSpec((gathered[k].shape[0], SHARD_COLS, tl), lambda i: (0, 0, i))
                  for k in used],
        out_specs=pl.BlockSpec((width, tl), lambda i: (0, i)),
        out_shape=jax.ShapeDtypeStruct((width, D_MODEL), BF16),
        compiler_params=_cparams(("arbitrary",)),
    )(*[gathered[k] for k in used])


def _grads_to_shards(name, groups, dests, own_prev):
    tl = 512
    segs = _segments()
    used = sorted(groups)

    def body(*refs):
        g_refs, prev_ref, o_ref, own_ref = dict(zip(used, refs[:-3])), refs[-3], refs[-2], refs[-1]
        x, y, c = _mesh_pos()
        own = prev_ref[...].astype(F32)
        row = lax.broadcasted_iota(jnp.int32, (SHARD_PAD, tl), 0)
        for i, (d, ranges) in enumerate(dests):
            pieces, pos, asked = [], 0, None
            for r0, r1 in sorted(ranges):
                if r0 > pos:
                    pieces.append(jnp.zeros((r0 - pos, tl), F32))
                for _, c0, n, p0 in sorted([s for s in segs if s[0] == d], key=lambda s: s[1]):
                    a, b = max(c0, r0), min(c0 + n, r1)
                    if a < b:
                        gi = _group_of(p0)
                        lo = p0 - P_GROUPS[gi][0] + a - c0
                        pieces.append(g_refs[gi][lo:lo + b - a, :].astype(F32))
                if r1 > SHARD_COLS:
                    pieces.append(jnp.zeros((r1 - max(r0, SHARD_COLS), tl), F32))
                pos = r1
                inside = jnp.logical_and(row >= r0, row < r1)
                asked = inside if asked is None else jnp.logical_or(asked, inside)
            if pos < SHARD_PAD:
                pieces.append(jnp.zeros((SHARD_PAD - pos, tl), F32))
            shard = jnp.concatenate(pieces, axis=0)
            o_ref[i] = shard.astype(BF16)
            own = jnp.where(jnp.logical_and(4 * x + 2 * y + c == d, asked), shard, own)
        own_ref[...] = own.astype(BF16)

    blk = pl.BlockSpec((SHARD_PAD, tl), lambda i: (0, i))
    return pl.pallas_call(
        body, name=name,
        grid=(D_MODEL // tl,),
        in_specs=[pl.BlockSpec((P_GROUPS[g][1], tl), lambda i: (0, i)) for g in used] + [blk],
        out_specs=(pl.BlockSpec((len(dests), SHARD_PAD, tl), lambda i: (0, 0, i)), blk),
        out_shape=(jax.ShapeDtypeStruct((len(dests), SHARD_PAD, D_MODEL), BF16),
                   jax.ShapeDtypeStruct((SHARD_PAD, D_MODEL), BF16)),
        input_output_aliases={len(used): 1},
        compiler_params=_cparams(("arbitrary",)),
    )(*[groups[g] for g in used], own_prev)


def _inproj(x, g_in, w_pt):
    t = x.shape[0]
    tm = min(512, t)
    width = w_pt.shape[0]

    def body(x_ref, g_ref, w_ref, proj_ref, h_ref, r_ref):
        xf = x_ref[...]
        r = lax.rsqrt(jnp.mean(xf * xf, axis=-1, keepdims=True) + EPS)
        h = ((xf * r) * g_ref[...]).astype(BF16)
        proj_ref[...] = _dot_nt(h, w_ref[...])
        h_ref[...] = h
        r_ref[...] = r

    row = lambda w: pl.BlockSpec((tm, w), lambda i: (i, 0))
    return pl.pallas_call(
        body, name="inproj_latents",
        grid=(t // tm,),
        in_specs=[row(D_MODEL), pl.BlockSpec((1, D_MODEL), lambda i: (0, 0)),
                  pl.BlockSpec((width, D_MODEL), lambda i: (0, 0))],
        out_specs=(row(width), row(D_MODEL), row(1)),
        out_shape=(jax.ShapeDtypeStruct((t, width), F32),
                   jax.ShapeDtypeStruct((t, D_MODEL), BF16),
                   jax.ShapeDtypeStruct((t, 1), F32)),
        compiler_params=_cparams(("arbitrary",)),
    )(x, g_in, w_pt)


def _proj(name, h, w_pt):
    t = h.shape[0]
    tm = min(512, t)
    width = w_pt.shape[0]

    def body(h_ref, w_ref, o_ref):
        o_ref[...] = _dot_nt(h_ref[...], w_ref[...])

    return pl.pallas_call(
        body, name=name,
        grid=(t // tm,),
        in_specs=[pl.BlockSpec((tm, D_MODEL), lambda i: (i, 0)),
                  pl.BlockSpec((width, D_MODEL), lambda i: (0, 0))],
        out_specs=pl.BlockSpec((tm, width), lambda i: (i, 0)),
        out_shape=jax.ShapeDtypeStruct((t, width), F32),
        compiler_params=_cparams(("arbitrary",)),
    )(h, w_pt)


def _mla_prep(proj, g_q, g_kv, w_uq_p, w_k_p, w_v, w_gate_p, b_gate, rc, rsn, rsp):
    t = proj.shape[0]
    tm = min(256, t)
    hq = MLA_HEADS * HEAD_PAD

    def body(cq_ref, ckv_ref, misc_ref, gq_ref, gkv_ref, wuq_ref, wk_ref, wv_ref, wg_ref, bg_ref,
             c_ref, sn_ref, sp_ref,
             q_ref, k_ref, v_ref, la_ref, pre_ref, cqn_ref, ckvn_ref, rq_ref, rkv_ref, mb_ref):
        c, sn, sp = c_ref[...], sn_ref[...], sp_ref[...]
        cq = cq_ref[:, :MLA_Q_RANK]
        rq = lax.rsqrt(jnp.mean(cq * cq, axis=-1, keepdims=True) + EPS)
        cqn = ((cq * rq) * gq_ref[...]).astype(BF16)
        cqn_ref[...] = cqn
        rq_ref[...] = rq
        qpre = _dot(cqn, wuq_ref[...])
        ckv = ckv_ref[...]
        rkv = lax.rsqrt(jnp.mean(ckv * ckv, axis=-1, keepdims=True) + EPS)
        ckvn = ((ckv * rkv) * gkv_ref[...]).astype(BF16)
        ckvn_ref[...] = ckvn
        rkv_ref[...] = rkv
        kn = _dot(ckvn, wk_ref[...])
        v_ref[...] = _dot(ckvn, wv_ref[...]).astype(BF16)
        misc = misc_ref[...]
        krope = _rope_fwd(misc, c, sn, sp)
        for h in range(MLA_HEADS):
            sl = slice(h * HEAD_PAD, (h + 1) * HEAD_PAD)
            q_ref[:, sl] = _rope_fwd(qpre[:, sl], c, sn, sp).astype(BF16)
            k_ref[:, sl] = (kn[:, sl] + krope).astype(BF16)
        mb_ref[...] = misc.astype(BF16)
        pre = _dot(mb_ref[...], wg_ref[...]) + bg_ref[...]
        pre_ref[...] = pre
        log_a = (jnp.minimum(pre, 0.0) - jnp.log(1.0 + jnp.exp(-jnp.abs(pre)))) / GLA_GATE_NORM
        la_ref[...] = _dot_exact(_chunk_tri(tm, True), log_a)

    row = lambda w: pl.BlockSpec((tm, w), lambda i: (i, 0))
    full = lambda a: pl.BlockSpec(a.shape, lambda i: (0, 0))
    return pl.pallas_call(
        body, name="mla_prep",
        grid=(t // tm,),
        in_specs=[pl.BlockSpec((tm, 512), lambda i: (i, _rel(P_CQ) // 512)),
                  pl.BlockSpec((tm, MLA_KV_RANK), lambda i: (i, _rel(P_CKV) // MLA_KV_RANK)),
                  pl.BlockSpec((tm, LANE), lambda i: (i, _rel(P_MISC) // LANE)),
                  full(g_q), full(g_kv), full(w_uq_p), full(w_k_p), full(w_v), full(w_gate_p),
                  full(b_gate), row(LANE), row(LANE), row(LANE)],
        out_specs=(row(hq), row(hq), row(MLA_WIDTH), row(GLA_DK), row(GLA_DK),
                   row(MLA_Q_RANK), row(MLA_KV_RANK), row(1), row(1), row(LANE)),
        out_shape=(jax.ShapeDtypeStruct((t, hq), BF16), jax.ShapeDtypeStruct((t, hq), BF16),
                   jax.ShapeDtypeStruct((t, MLA_WIDTH), BF16),
                   jax.ShapeDtypeStruct((t, GLA_DK), F32), jax.ShapeDtypeStruct((t, GLA_DK), F32),
                   jax.ShapeDtypeStruct((t, MLA_Q_RANK), BF16),
                   jax.ShapeDtypeStruct((t, MLA_KV_RANK), BF16),
                   jax.ShapeDtypeStruct((t, 1), F32), jax.ShapeDtypeStruct((t, 1), F32),
                   jax.ShapeDtypeStruct((t, LANE), BF16)),
        compiler_params=_cparams(("arbitrary",)),
    )(proj, proj, proj, g_q, g_kv, w_uq_p, w_k_p, w_v, w_gate_p, b_gate, rc, rsn, rsp)


def _attn_masks(tq, i):
    keys = (i + 1) * tq
    rows = i * tq + lax.broadcasted_iota(jnp.int32, (tq, keys), 0)
    cols = lax.broadcasted_iota(jnp.int32, (tq, keys), 1)
    lane = lax.broadcasted_iota(jnp.int32, (tq, LANE), 1)
    return cols <= rows, lane < MLA_VDIM


def _for_each_query_tile(n_tiles, fn):
    for i in range(n_tiles):
        pl.when(pl.program_id(1) == i)(lambda i=i: fn(i))


def _mla_attn_fwd(q, k, v, shards, sources):
    t = q.shape[0]
    tq = min(256, t)
    scale = MLA_QK ** -0.5
    ns = len(shards)
    g_shapes, g_sems = _gather_plan(shards, sources)
    grid = (MLA_HEADS // 2, t // tq)

    def body(q_ref, k_ref, v_ref, *rest):
        o_ref, lse_ref = rest[ns:ns + 2]
        start, finish = _gather_hooks(rest[:ns], rest[ns + 2:2 * ns + 2], rest[2 * ns + 2:], sources)
        step = pl.program_id(0) * grid[1] + pl.program_id(1)
        pl.when(step == 0)(start)

        def tile(i):
            keys = (i + 1) * tq
            causal, low = _attn_masks(tq, i)
            vp = v_ref[0:keys, :]
            acc = jnp.zeros((tq, LANE), F32)
            for hh in range(2):
                sl = slice(hh * HEAD_PAD, (hh + 1) * HEAD_PAD)
                s = _dot_nt(q_ref[:, sl], k_ref[0:keys, sl]) * scale
                s = jnp.where(causal, s, -jnp.inf)
                m = jnp.max(s, axis=-1, keepdims=True)
                e = jnp.exp(s - m)
                l = jnp.sum(e, axis=-1, keepdims=True)
                o = _dot(e.astype(BF16), vp) / l
                acc = jnp.where(low if hh == 0 else jnp.logical_not(low), o, acc)
                lse_ref[hh] = m + jnp.log(l)
            o_ref[...] = acc

        _for_each_query_tile(t // tq, tile)
        pl.when(step == grid[0] * grid[1] - 1)(finish)

    res = pl.pallas_call(
        body, name="mla_attn_fwd",
        grid=grid,
        in_specs=[pl.BlockSpec((tq, 2 * HEAD_PAD), lambda p, i: (i, p)),
                  pl.BlockSpec((t, 2 * HEAD_PAD), lambda p, i: (0, p)),
                  pl.BlockSpec((t, LANE), lambda p, i: (0, p))] + _hbm_specs(ns),
        out_specs=(pl.BlockSpec((tq, LANE), lambda p, i: (i, p)),
                   pl.BlockSpec((2, tq, 1), lambda p, i: (p, i, 0))) + tuple(_hbm_specs(ns)),
        out_shape=(jax.ShapeDtypeStruct((t, MLA_WIDTH), F32),
                   jax.ShapeDtypeStruct((MLA_HEADS, t, 1), F32)) + tuple(g_shapes),
        scratch_shapes=g_sems,
        compiler_params=_cparams(("arbitrary", "arbitrary")),
    )(q, k, v, *shards)
    return res[0], res[1], res[2:]


def _mla_attn_bwd(q, k, v, o, do, lse, after):
    t = q.shape[0]
    tq = min(256, t)
    scale = MLA_QK ** -0.5

    def body(q_ref, k_ref, v_ref, o_ref, do_ref, lse_ref, after_ref, dq_ref, dk_ref, dv_ref):
        del after_ref

        @pl.when(pl.program_id(1) == 0)
        def _():
            dk_ref[...] = jnp.zeros_like(dk_ref)
            dv_ref[...] = jnp.zeros_like(dv_ref)

        def tile(i):
            keys = (i + 1) * tq
            causal, low = _attn_masks(tq, i)
            vp = v_ref[0:keys, :]
            do_all = do_ref[...]
            o_all = o_ref[...]
            dv_acc = jnp.zeros((keys, LANE), F32)
            for hh in range(2):
                sl = slice(hh * HEAD_PAD, (hh + 1) * HEAD_PAD)
                do_h = jnp.where(low if hh == 0 else jnp.logical_not(low), do_all, 0.0)
                dsum = jnp.sum(do_h * o_all, axis=-1, keepdims=True)
                qh = q_ref[:, sl]
                kh = k_ref[0:keys, sl]
                s = _dot_nt(qh, kh) * scale
                p = jnp.where(causal, jnp.exp(s - lse_ref[hh]), 0.0)
                do_b = do_h.astype(BF16)
                dp = _dot_nt(do_b, vp)
                ds = (p * (dp - dsum) * scale).astype(BF16)
                dq_ref[:, sl] = _dot(ds, kh).astype(BF16)
                dk_ref[0:keys, sl] += _dot_tn(ds, qh)
                dv_acc = dv_acc + _dot_tn(p.astype(BF16), do_b)
            dv_ref[0:keys, :] += dv_acc

        _for_each_query_tile(t // tq, tile)

    return pl.pallas_call(
        body, name="mla_attn_bwd",
        grid=(MLA_HEADS // 2, t // tq),
        in_specs=[pl.BlockSpec((tq, 2 * HEAD_PAD), lambda p, i: (i, p)),
                  pl.BlockSpec((t, 2 * HEAD_PAD), lambda p, i: (0, p)),
                  pl.BlockSpec((t, LANE), lambda p, i: (0, p)),
                  pl.BlockSpec((tq, LANE), lambda p, i: (i, p)),
                  pl.BlockSpec((tq, LANE), lambda p, i: (i, p)),
                  pl.BlockSpec((2, tq, 1), lambda p, i: (p, i, 0)),
                  pl.BlockSpec(memory_space=pl.ANY)],
        out_specs=(pl.BlockSpec((tq, 2 * HEAD_PAD), lambda p, i: (i, p)),
                   pl.BlockSpec((t, 2 * HEAD_PAD), lambda p, i: (0, p)),
                   pl.BlockSpec((t, LANE), lambda p, i: (0, p))),
        out_shape=(jax.ShapeDtypeStruct((t, MLA_HEADS * HEAD_PAD), BF16),
                   jax.ShapeDtypeStruct((t, MLA_HEADS * HEAD_PAD), F32),
                   jax.ShapeDtypeStruct((t, MLA_WIDTH), F32)),
        compiler_params=_cparams(("arbitrary", "arbitrary")),
    )(q, k, v, o, do, lse, after)


def _chunk_tri(n, lower):
    r = lax.broadcasted_iota(jnp.int32, (n, n), 0)
    c = lax.broadcasted_iota(jnp.int32, (n, n), 1)
    same = (r // GLA_CHUNK) == (c // GLA_CHUNK)
    return jnp.where(jnp.logical_and(same, r >= c if lower else r <= c), 1.0, 0.0).astype(F32)


def _gla_chunk_terms(q_ref, k_ref, b_ref, h, rows):
    sl = slice(h * GLA_HK, (h + 1) * GLA_HK)
    b = b_ref[rows, sl]
    bl = b[GLA_CHUNK - 1:GLA_CHUNK, :]
    kc = k_ref[rows, sl]
    q_in = (q_ref[rows, sl] * (GLA_HK ** -0.5)) * jnp.exp(b)
    k_in = kc * jnp.exp(-b)
    k_st = kc * jnp.exp(bl - b)
    return b, bl, q_in, k_in, k_st


def _tri(c, lower):
    r = lax.broadcasted_iota(jnp.int32, (c, c), 0)
    cc = lax.broadcasted_iota(jnp.int32, (c, c), 1)
    return jnp.where(r >= cc if lower else r <= cc, 1.0, 0.0).astype(F32)


def _gla_fwd(proj, log_a):
    t = proj.shape[0]
    per = GLA_CHUNKS_PER_STEP
    n = t // GLA_CHUNK
    c = GLA_CHUNK * per

    def body(q_ref, k_ref, v_ref, la_ref, o_ref, sp_ref, st_ref):
        @pl.when(pl.program_id(0) == 0)
        def _():
            st_ref[...] = jnp.zeros_like(st_ref)

        tri = _tri(GLA_CHUNK, True)
        for s, h in [(s, h) for s in range(per) for h in range(GLA_HEADS)]:
            rows = slice(s * GLA_CHUNK, (s + 1) * GLA_CHUNK)
            _, bl, q_in, k_in, k_st = _gla_chunk_terms(q_ref, k_ref, la_ref, h, rows)
            vs = slice(h * GLA_HV, (h + 1) * GLA_HV)
            vv = v_ref[rows, vs].astype(BF16)
            qb = q_in.astype(BF16)
            attn = _dot_nt(qb, k_in.astype(BF16)) * tri
            st = st_ref[h]
            sp_ref[s, h] = st
            o_ref[rows, vs] = _dot(attn.astype(BF16), vv) + _dot_nt(qb, st.astype(BF16))
            st_ref[h] = st * jnp.exp(bl) + _dot_tn(vv, k_st.astype(BF16))

    return pl.pallas_call(
        body, name="gla_fwd",
        grid=(n // per,),
        in_specs=[pl.BlockSpec((c, GLA_DK), lambda i: (i, P_QG // GLA_DK)),
                  pl.BlockSpec((c, GLA_DK), lambda i: (i, P_KG // GLA_DK)),
                  pl.BlockSpec((c, GLA_DV), lambda i: (i, P_VG // GLA_DV)),
                  pl.BlockSpec((c, GLA_DK), lambda i: (i, 0))],
        out_specs=(pl.BlockSpec((c, GLA_DV), lambda i: (i, 0)),
                   pl.BlockSpec((per, GLA_HEADS, GLA_HV, GLA_HK), lambda i: (i, 0, 0, 0))),
        out_shape=(jax.ShapeDtypeStruct((t, GLA_DV), F32),
                   jax.ShapeDtypeStruct((n, GLA_HEADS, GLA_HV, GLA_HK), F32)),
        scratch_shapes=[pltpu.VMEM((GLA_HEADS, GLA_HV, GLA_HK), F32)],
        compiler_params=_cparams(("arbitrary",)),
    )(proj, proj, proj, log_a)


def _gla_bwd(proj, log_a, do, states, after):
    t = proj.shape[0]
    per = GLA_CHUNKS_PER_STEP
    c = GLA_CHUNK * per
    n = t // c

    def body(q_ref, k_ref, v_ref, la_ref, do_ref, sp_ref, after_ref, dg_ref, dla_ref, ds_ref):
        del after_ref

        @pl.when(pl.program_id(0) == 0)
        def _():
            ds_ref[...] = jnp.zeros_like(ds_ref)

        tri = _tri(GLA_CHUNK, True)
        last = lax.broadcasted_iota(jnp.int32, (GLA_CHUNK, GLA_HK), 0) == GLA_CHUNK - 1
        for s, h in [(s, h) for s in reversed(range(per)) for h in range(GLA_HEADS)]:
            rows = slice(s * GLA_CHUNK, (s + 1) * GLA_CHUNK)
            b, bl, q_in, k_in, k_st = _gla_chunk_terms(q_ref, k_ref, la_ref, h, rows)
            ks_ = slice(h * GLA_HK, (h + 1) * GLA_HK)
            vs = slice(h * GLA_HV, (h + 1) * GLA_HV)
            vv = v_ref[rows, vs].astype(BF16)
            do_h = do_ref[rows, vs]
            qb, kb, ksb = q_in.astype(BF16), k_in.astype(BF16), k_st.astype(BF16)
            attn = (_dot_nt(qb, kb) * tri).astype(BF16)
            st = sp_ref[s, h]
            dst = ds_ref[h]
            dstb = dst.astype(BF16)
            dattn = (_dot_nt(do_h, vv) * tri).astype(BF16)
            dg_ref[rows, P_VG + h * GLA_HV:P_VG + (h + 1) * GLA_HV] = (
                _dot_tn(attn, do_h) + _dot_nt(ksb, dstb)).astype(BF16)
            dq_in = _dot(dattn, kb) + _dot(do_h, st.astype(BF16))
            dk_in = _dot_tn(dattn, qb)
            dk_st = _dot(vv, dstb)
            ebl = jnp.exp(bl)
            d_ebl = jnp.sum(st * dst, axis=0, keepdims=True)
            ds_ref[h] = _dot_tn(do_h, qb) + dst * ebl
            dg_ref[rows, P_QG + h * GLA_HK:P_QG + (h + 1) * GLA_HK] = (
                dq_in * (GLA_HK ** -0.5) * jnp.exp(b)).astype(BF16)
            dg_ref[rows, P_KG + h * GLA_HK:P_KG + (h + 1) * GLA_HK] = (
                dk_in * jnp.exp(-b) + dk_st * jnp.exp(bl - b)).astype(BF16)
            db = dq_in * q_in - dk_in * k_in - dk_st * k_st
            dbl = jnp.sum(dk_st * k_st, axis=0, keepdims=True) + d_ebl * ebl
            dla_ref[rows, ks_] = db + jnp.where(last, dbl, 0.0)

    rev = lambda i: n - 1 - i
    gw = P_GROUPS[0][1]
    return pl.pallas_call(
        body, name="gla_bwd",
        grid=(n,),
        in_specs=[pl.BlockSpec((c, GLA_DK), lambda i: (rev(i), P_QG // GLA_DK)),
                  pl.BlockSpec((c, GLA_DK), lambda i: (rev(i), P_KG // GLA_DK)),
                  pl.BlockSpec((c, GLA_DV), lambda i: (rev(i), P_VG // GLA_DV)),
                  pl.BlockSpec((c, GLA_DK), lambda i: (rev(i), 0)),
                  pl.BlockSpec((c, GLA_DV), lambda i: (rev(i), 0)),
                  pl.BlockSpec((per, GLA_HEADS, GLA_HV, GLA_HK), lambda i: (rev(i), 0, 0, 0)),
                  pl.BlockSpec(memory_space=pl.ANY)],
        out_specs=(pl.BlockSpec((c, gw), lambda i: (rev(i), 0)),
                   pl.BlockSpec((c, GLA_DK), lambda i: (rev(i), 0))),
        out_shape=(jax.ShapeDtypeStruct((t, gw), BF16), jax.ShapeDtypeStruct((t, GLA_DK), F32)),
        scratch_shapes=[pltpu.VMEM((GLA_HEADS, GLA_HV, GLA_HK), F32)],
        compiler_params=_cparams(("arbitrary",)),
    )(proj, proj, proj, log_a, do, states, after)


def _post(o_mla, proj, o_gla, x, target, g_gla, g_final, w_pm, w_pg, w_o):
    t = x.shape[0]
    tm = min(256, t)
    g0, gw = P_GROUPS[1]

    def body(om_ref, zg_ref, gm_ref, gg_ref, zm_ref, og_ref, x_ref, tg_ref, ggla_ref, gf_ref,
             wpm_ref, wpg_ref, wo_ref,
             dx2_ref, dom_ref, dog_ref, dg_ref,
             mg_ref, um_ref, ug_ref, dym_ref, dyg_ref, loss_ref, dgf_ref, dggla_ref):
        @pl.when(pl.program_id(0) == 0)
        def _():
            loss_ref[...] = jnp.zeros_like(loss_ref)
            dgf_ref[...] = jnp.zeros_like(dgf_ref)
            dggla_ref[...] = jnp.zeros_like(dggla_ref)

        om = om_ref[...]
        zm = zm_ref[...]
        sm = _sigmoid(zm)
        silu_m = zm * sm
        um = (om * silu_m).astype(BF16)
        um_ref[...] = um
        ym = _dot(um, wpm_ref[...])

        ggla = ggla_ref[...]
        zg = zg_ref[...]
        sg = _sigmoid(zg)
        silu_g = zg * sg
        xhat, rstd, on = [], [], []
        for h in range(GLA_HEADS):
            blk = og_ref[:, h * GLA_HV:(h + 1) * GLA_HV]
            r = lax.rsqrt(jnp.mean(blk * blk, axis=-1, keepdims=True) + EPS)
            xhat.append(blk * r)
            rstd.append(r)
            on.append(xhat[h] * ggla)
        on = jnp.concatenate(on, axis=-1)
        ug = (on * silu_g).astype(BF16)
        ug_ref[...] = ug
        yg = _dot(ug, wpg_ref[...])

        sgm = _sigmoid(gm_ref[...])
        sgg = _sigmoid(gg_ref[...])
        merged = (sgm * ym + sgg * yg).astype(BF16)
        mg_ref[...] = merged
        x2 = x_ref[...] + _dot(merged, wo_ref[...])
        gf = gf_ref[...]
        rf = lax.rsqrt(jnp.mean(x2 * x2, axis=-1, keepdims=True) + EPS)
        xh = x2 * rf
        err = xh * gf - tg_ref[...]
        loss_ref[...] += 0.5 * jnp.sum(jnp.mean(err * err, axis=-1, keepdims=True))

        dy = err * (1.0 / D_MODEL)
        dgf_ref[...] += jnp.sum(dy * xh, axis=0, keepdims=True)
        dxh = dy * gf
        dx2 = rf * (dxh - xh * jnp.mean(dxh * xh, axis=-1, keepdims=True))
        dx2_ref[...] = dx2
        dmerged = _dot_nt(dx2.astype(BF16), wo_ref[...])
        dym = (dmerged * sgm).astype(BF16)
        dyg = (dmerged * sgg).astype(BF16)
        dym_ref[...] = dym
        dyg_ref[...] = dyg
        dg_ref[:, P_GMLA - g0:P_GMLA - g0 + D_MODEL] = (dmerged * ym * sgm * (1.0 - sgm)).astype(BF16)
        dg_ref[:, P_GGLA - g0:P_GGLA - g0 + D_MODEL] = (dmerged * yg * sgg * (1.0 - sgg)).astype(BF16)
        dum = _dot_nt(dym, wpm_ref[...])
        dom_ref[...] = dum * silu_m
        dg_ref[:, P_ZMLA - g0:P_ZMLA - g0 + MLA_WIDTH] = (
            dum * om * (sm * (1.0 + zm * (1.0 - sm)))).astype(BF16)
        dug = _dot_nt(dyg, wpg_ref[...])
        dg_ref[:, P_ZGLA - g0:P_ZGLA - g0 + GLA_DV] = (
            dug * on * (sg * (1.0 + zg * (1.0 - sg)))).astype(BF16)
        don = dug * silu_g
        dggla = jnp.zeros((1, GLA_HV), F32)
        for h in range(GLA_HEADS):
            hs = slice(h * GLA_HV, (h + 1) * GLA_HV)
            don_h = don[:, hs]
            dggla = dggla + jnp.sum(don_h * xhat[h], axis=0, keepdims=True)
            dxh_h = don_h * ggla
            dog_ref[:, hs] = (rstd[h] * (dxh_h - xhat[h] * jnp.mean(dxh_h * xhat[h], axis=-1,
                                                                     keepdims=True))).astype(BF16)
        dggla_ref[...] += dggla

    row = lambda w: pl.BlockSpec((tm, w), lambda i: (i, 0))
    pcol = lambda w, off: pl.BlockSpec((tm, w), lambda i: (i, _rel(off) // w))
    full = lambda a: pl.BlockSpec(a.shape, lambda i: (0, 0))
    sds = jax.ShapeDtypeStruct
    return pl.pallas_call(
        body, name="post_fwd_bwd",
        grid=(t // tm,),
        in_specs=[row(MLA_WIDTH), pcol(GLA_DV, P_ZGLA), pcol(D_MODEL, P_GMLA), pcol(D_MODEL, P_GGLA),
                  pcol(MLA_WIDTH, P_ZMLA), row(GLA_DV), row(D_MODEL), row(D_MODEL),
                  full(g_gla), full(g_final), full(w_pm), full(w_pg), full(w_o)],
        out_specs=(row(D_MODEL), row(MLA_WIDTH), row(GLA_DV), row(gw),
                   row(D_MODEL), row(MLA_WIDTH), row(GLA_DV), row(D_MODEL), row(D_MODEL),
                   pl.BlockSpec((1, LANE), lambda i: (0, 0)),
                   pl.BlockSpec((1, D_MODEL), lambda i: (0, 0)),
                   pl.BlockSpec((1, GLA_HV), lambda i: (0, 0))),
        out_shape=(sds((t, D_MODEL), F32), sds((t, MLA_WIDTH), F32), sds((t, GLA_DV), BF16),
                   sds((t, gw), BF16),
                   sds((t, D_MODEL), BF16), sds((t, MLA_WIDTH), BF16), sds((t, GLA_DV), BF16),
                   sds((t, D_MODEL), BF16), sds((t, D_MODEL), BF16),
                   sds((1, LANE), F32), sds((1, D_MODEL), F32), sds((1, GLA_HV), F32)),
        compiler_params=_cparams(("arbitrary",)),
    )(o_mla, proj, proj, proj, proj, o_gla, x, target, g_gla, g_final, w_pm, w_pg, w_o)


def _mla_prep_bwd(dq, dk, dv, dla, pre, proj, rq, rkv, g_q, g_kv, w_uq_p, w_k_p, w_v, w_gate_p,
                  rc, rsn, rsp):
    t = proj.shape[0]
    tm = min(256, t)
    gw = P_GROUPS[2][1]

    def body(dq_ref, dk_ref, dv_ref, dla_ref, pre_ref, cq_ref, ckv_ref, rq_ref, rkv_ref,
             gq_ref, gkv_ref, wuq_ref, wk_ref, wv_ref, wg_ref, c_ref, sn_ref, sp_ref,
             dg_ref, dqpre_ref, dpre_ref, dgq_ref, dgkv_ref, dbg_ref):
        @pl.when(pl.program_id(0) == 0)
        def _():
            dgq_ref[...] = jnp.zeros_like(dgq_ref)
            dgkv_ref[...] = jnp.zeros_like(dgkv_ref)
            dbg_ref[...] = jnp.zeros_like(dbg_ref)

        c, sn, sp = c_ref[...], sn_ref[...], sp_ref[...]
        dkr = jnp.zeros((tm, LANE), F32)
        for h in range(MLA_HEADS):
            sl = slice(h * HEAD_PAD, (h + 1) * HEAD_PAD)
            dqpre_ref[:, sl] = _rope_bwd(dq_ref[:, sl].astype(F32), c, sn, sp).astype(BF16)
            dkr = dkr + dk_ref[:, sl]
        dcqn = _dot_nt(dqpre_ref[...], wuq_ref[...])
        rq = rq_ref[...]
        xh = cq_ref[:, :MLA_Q_RANK] * rq
        dgq_ref[...] += jnp.sum(dcqn * xh, axis=0, keepdims=True)
        dxh = dcqn * gq_ref[...]
        dcq = rq * (dxh - xh * jnp.mean(dxh * xh, axis=-1, keepdims=True))
        dg_ref[:, :MLA_Q_RANK] = dcq.astype(BF16)
        dg_ref[:, MLA_Q_RANK:512] = jnp.zeros((tm, 512 - MLA_Q_RANK), BF16)

        dckvn = _dot_nt(dk_ref[...].astype(BF16), wk_ref[...]) + \
            _dot_nt(dv_ref[...].astype(BF16), wv_ref[...])
        rkv = rkv_ref[...]
        xh = ckv_ref[...] * rkv
        dgkv_ref[...] += jnp.sum(dckvn * xh, axis=0, keepdims=True)
        dxh = dckvn * gkv_ref[...]
        dg_ref[:, P_CKV - P_CQ:P_CKV - P_CQ + MLA_KV_RANK] = (
            rkv * (dxh - xh * jnp.mean(dxh * xh, axis=-1, keepdims=True))).astype(BF16)

        dlog_a = _dot_exact(_chunk_tri(tm, False), dla_ref[...])
        dpre = dlog_a * (1.0 / GLA_GATE_NORM) * (1.0 - _sigmoid(pre_ref[...]))
        dbg_ref[...] += jnp.sum(dpre, axis=0, keepdims=True)
        dpre = dpre.astype(BF16)
        dpre_ref[...] = dpre
        lane = lax.broadcasted_iota(jnp.int32, (tm, LANE), 1)
        in_kr = jnp.logical_and(lane >= MISC_KR, lane < MISC_KR + MLA_ROPE)
        dmisc = jnp.where(in_kr, _rope_bwd(dkr, c, sn, sp), 0.0) + _dot_nt(dpre, wg_ref[...])
        dg_ref[:, P_MISC - P_CQ:P_MISC - P_CQ + LANE] = dmisc.astype(BF16)

    hq = MLA_HEADS * HEAD_PAD
    row = lambda w: pl.BlockSpec((tm, w), lambda i: (i, 0))
    full = lambda a: pl.BlockSpec(a.shape, lambda i: (0, 0))
    acc = lambda w: pl.BlockSpec((1, w), lambda i: (0, 0))
    sds = jax.ShapeDtypeStruct
    return pl.pallas_call(
        body, name="mla_prep_bwd",
        grid=(t // tm,),
        in_specs=[row(hq), row(hq), row(MLA_WIDTH), row(GLA_DK), row(GLA_DK),
                  pl.BlockSpec((tm, 512), lambda i: (i, _rel(P_CQ) // 512)),
                  pl.BlockSpec((tm, MLA_KV_RANK), lambda i: (i, _rel(P_CKV) // MLA_KV_RANK)),
                  row(1), row(1), full(g_q), full(g_kv), full(w_uq_p), full(w_k_p), full(w_v),
                  full(w_gate_p), row(LANE), row(LANE), row(LANE)],
        out_specs=(row(gw), row(hq), row(GLA_DK),
                   acc(MLA_Q_RANK), acc(MLA_KV_RANK), acc(GLA_DK)),
        out_shape=(sds((t, gw), BF16), sds((t, hq), BF16), sds((t, GLA_DK), BF16),
                   sds((1, MLA_Q_RANK), F32), sds((1, MLA_KV_RANK), F32), sds((1, GLA_DK), F32)),
        compiler_params=_cparams(("arbitrary",)),
    )(dq, dk, dv, dla, pre, proj, proj, rq, rkv, g_q, g_kv, w_uq_p, w_k_p, w_v, w_gate_p,
      rc, rsn, rsp)


def _inproj_bwd(dgroups, w_pts, x, rstd, g_in, dx2, after):
    t = x.shape[0]
    tm = min(256, t)

    def body(d0_ref, d1_ref, d2_ref, w0_ref, w1_ref, w2_ref, x_ref, r_ref, g_ref, dx2_ref, after_ref,
             dx_ref, dg_ref):
        del after_ref

        @pl.when(pl.program_id(0) == 0)
        def _():
            dg_ref[...] = jnp.zeros_like(dg_ref)

        dh = jnp.zeros((tm, D_MODEL), F32)
        for d_ref, w_ref in zip((d0_ref, d1_ref, d2_ref), (w0_ref, w1_ref, w2_ref)):
            dh = dh + _dot(d_ref[...], w_ref[...])
        r = r_ref[...]
        xh = x_ref[...] * r
        dg_ref[...] += jnp.sum(dh * xh, axis=0, keepdims=True)
        dxh = dh * g_ref[...]
        dx_ref[...] = dx2_ref[...] + r * (dxh - xh * jnp.mean(dxh * xh, axis=-1, keepdims=True))

    row = lambda w: pl.BlockSpec((tm, w), lambda i: (i, 0))
    return pl.pallas_call(
        body, name="inproj_bwd",
        grid=(t // tm,),
        in_specs=[row(w) for _, w in P_GROUPS]
        + [pl.BlockSpec((w, D_MODEL), lambda i: (0, 0)) for _, w in P_GROUPS]
        + [row(D_MODEL), row(1), pl.BlockSpec((1, D_MODEL), lambda i: (0, 0)), row(D_MODEL),
           pl.BlockSpec(memory_space=pl.ANY)],
        out_specs=(row(D_MODEL), pl.BlockSpec((1, D_MODEL), lambda i: (0, 0))),
        out_shape=(jax.ShapeDtypeStruct((t, D_MODEL), F32),
                   jax.ShapeDtypeStruct((1, D_MODEL), F32)),
        compiler_params=_cparams(("arbitrary",)),
    )(*dgroups, *w_pts, x, rstd, g_in, dx2, after)


def _matmul(name, a, b, tm, tn, dtype=F32, after=None):
    kk, m = a.shape
    n = b.shape[1]
    extra = [] if after is None else [after]

    def body(a_ref, b_ref, *rest):
        rest[-1][...] = _dot_tn(a_ref[...].astype(BF16), b_ref[...].astype(BF16)).astype(dtype)

    return pl.pallas_call(
        body, name=name,
        grid=(n // tn, m // tm),
        in_specs=[pl.BlockSpec((kk, tm), lambda j, i: (0, i)),
                  pl.BlockSpec((kk, tn), lambda j, i: (0, j))]
        + [pl.BlockSpec(memory_space=pl.ANY) for _ in extra],
        out_specs=pl.BlockSpec((tm, tn), lambda j, i: (i, j)),
        out_shape=jax.ShapeDtypeStruct((m, n), dtype),
        compiler_params=_cparams(("arbitrary", "arbitrary")),
    )(a, b, *extra)


def _adamw_update(part_refs, w_ref, m_ref, v_ref, g_ref, d_ref, nm_ref, nv_ref):
    g = part_refs[0][...].astype(F32)
    for p_ref in part_refs[1:]:
        g = g + p_ref[...].astype(F32)
    m_new = ADAM_B1 * m_ref[...] + (1.0 - ADAM_B1) * g
    v_new = ADAM_B2 * v_ref[...] + (1.0 - ADAM_B2) * (g * g)
    m_hat = m_new / (1.0 - ADAM_B1 ** ADAM_STEP)
    v_hat = v_new / (1.0 - ADAM_B2 ** ADAM_STEP)
    g_ref[...] = g
    nm_ref[...] = m_new
    nv_ref[...] = v_new
    d_ref[...] = -ADAM_LR * (m_hat / (jnp.sqrt(v_hat) + ADAM_EPS) + ADAM_WD * w_ref[...])


def _adamw_rows(name, parts, w, m, v, tr, first=None):
    _, rows, cols = w.shape
    slots = parts.shape[0]

    def body(*refs):
        lead_refs, p_ref = ([], refs[0]) if first is None else ([refs[0]], refs[1])
        _adamw_update(lead_refs + [p_ref.at[q] for q in range(slots)], *refs[len(lead_refs) + 1:])

    blk = pl.BlockSpec((None, tr, cols), lambda i: (0, i, 0))
    out = jax.ShapeDtypeStruct((1, rows, cols), F32)
    lead = [] if first is None else [pl.BlockSpec((tr, cols), lambda i: (i, 0))]
    return pl.pallas_call(
        body, name=name,
        grid=(rows // tr,),
        in_specs=lead + [pl.BlockSpec((slots, tr, cols), lambda i: (0, i, 0)), blk, blk, blk],
        out_specs=(blk, blk, blk, blk),
        out_shape=(out, out, out, out),
        compiler_params=_cparams(("arbitrary",)),
    )(*([] if first is None else [first]), parts, w, m, v)


def _adamw_transposed(name, first, parts, w, m, v, tl):
    _, rows, cols = w.shape
    slots, padded = parts.shape[:2]

    def body(f_ref, p_ref, *refs):
        _adamw_update([f_ref.at[pl.ds(0, cols)]]
                      + [p_ref.at[q, pl.ds(0, cols)] for q in range(slots)], *refs)

    blk = pl.BlockSpec((cols, None, tl), lambda i: (0, 0, i))
    out = jax.ShapeDtypeStruct((cols, 1, rows), F32)
    res = pl.pallas_call(
        body, name=name,
        grid=(rows // tl,),
        in_specs=[pl.BlockSpec((padded, tl), lambda i: (0, i)),
                  pl.BlockSpec((slots, padded, tl), lambda i: (0, 0, i)), blk, blk, blk],
        out_specs=(blk, blk, blk, blk),
        out_shape=(out, out, out, out),
        compiler_params=_cparams(("arbitrary",)),
    )(first, parts, *[a.transpose(2, 0, 1) for a in (w, m, v)])
    return [r.transpose(1, 2, 0) for r in res]


def _adamw_gains(gathered, ws, ms, vs):
    n = len(ws)

    def body(p_ref, *refs):
        ins, outs, loss_ref = refs[:3 * n], refs[3 * n:7 * n], refs[7 * n]
        row = 0
        for a in range(n):
            for j in range(ws[a].shape[1] // LANE):
                lanes = pl.ds(j * LANE, LANE)
                _adamw_update([p_ref.at[q, pl.ds(row, 1)] for q in range(N_DEV)],
                              *[r.at[:, lanes] for r in (ins[a], ins[n + a], ins[2 * n + a])],
                              *[r.at[:, lanes] for r in outs[4 * a:4 * a + 4]])
                row += 1
        loss = p_ref[0, pl.ds(row, 1), :]
        for q in range(1, N_DEV):
            loss = loss + p_ref[q, pl.ds(row, 1), :]
        loss_ref[...] = loss

    vmem = lambda k: [pl.BlockSpec(memory_space=pltpu.VMEM) for _ in range(k)]
    out_shape = []
    for w in ws:
        out_shape += [jax.ShapeDtypeStruct(w.shape, F32)] * 4
    out_shape.append(jax.ShapeDtypeStruct((1, LANE), F32))
    res = pl.pallas_call(
        body, name="adamw_gains",
        in_specs=vmem(1 + 3 * n), out_specs=tuple(vmem(4 * n + 1)), out_shape=tuple(out_shape),
        compiler_params=_cparams(),
    )(gathered, *ws, *ms, *vs)
    return [res[4 * a:4 * a + 4] for a in range(n)], res[-1]


def _adamw_group(firsts, parts, ws, ms, vs):
    n = len(ws)

    def body(*refs):
        ins, outs = refs[:5 * n], refs[5 * n:]
        x, y, c = _mesh_pos()
        for a in range(n):
            _adamw_update([ins[a].at[4 * x + 2 * y + c]]
                          + [ins[n + a].at[q] for q in range(ins[n + a].shape[0])],
                          *[r.at[0] for r in (ins[2 * n + a], ins[3 * n + a], ins[4 * n + a])],
                          *[r.at[0] for r in outs[4 * a:4 * a + 4]])

    vmem = lambda k: [pl.BlockSpec(memory_space=pltpu.VMEM) for _ in range(k)]
    out_shape = []
    for w in ws:
        out_shape += [jax.ShapeDtypeStruct(w.shape, F32)] * 4
    res = pl.pallas_call(
        body, name="adamw_small_weights",
        in_specs=vmem(5 * n), out_specs=tuple(vmem(4 * n)), out_shape=tuple(out_shape),
        compiler_params=_cparams(),
    )(*firsts, *parts, *ws, *ms, *vs)
    return [res[4 * a:4 * a + 4] for a in range(n)]


def _rope_tables(positions):
    half = MLA_ROPE // 2
    freqs = ROPE_THETA ** (-jnp.arange(half, dtype=F32) / half)
    ang = positions.astype(F32).reshape(-1, 1) * freqs
    cos, sin = jnp.cos(ang), jnp.sin(ang)
    t = ang.shape[0]
    one, zero = jnp.ones((t, MLA_NOPE), F32), jnp.zeros((t, half), F32)
    tail = jnp.zeros((t, LANE - MLA_QK), F32)
    rc = jnp.concatenate([one, cos, cos, tail], axis=1)
    rsn = jnp.concatenate([0.0 * one, -sin, zero, tail], axis=1)
    rsp = jnp.concatenate([0.0 * one, zero, sin, tail], axis=1)
    return rc, rsn, rsp


def _cols_full(g):
    return g.transpose(1, 0, 2)


def kernel(x, positions, g_in, w_in, g_q, w_uq, g_kv, w_ukv, w_gla_gate, b_gla_gate, g_gla, w_proj_mla, w_proj_gla, w_out, g_final, loss_target, m_g_in, m_w_in, m_g_q, m_w_uq, m_g_kv, m_w_ukv, m_w_gla_gate, m_b_gla_gate, m_g_gla, m_w_proj_mla, m_w_proj_gla, m_w_out, m_g_final, v_g_in, v_w_in, v_g_q, v_w_uq, v_g_kv, v_w_ukv, v_w_gla_gate, v_b_gla_gate, v_g_gla, v_w_proj_mla, v_w_proj_gla, v_w_out, v_g_final):
    t = x.shape[1]
    x2d = x.reshape(t, D_MODEL)
    tgt = loss_target.reshape(t, D_MODEL)
    g_final2 = g_final.reshape(1, D_MODEL)
    sharded = [(w_in, m_w_in, v_w_in), (w_uq, m_w_uq, v_w_uq), (w_ukv, m_w_ukv, v_w_ukv),
               (w_gla_gate, m_w_gla_gate, v_w_gla_gate), (w_proj_mla, m_w_proj_mla, v_w_proj_mla),
               (w_proj_gla, m_w_proj_gla, v_w_proj_gla), (w_out, m_w_out, v_w_out)]

    w_in_t = w_in.transpose(2, 0, 1).reshape(SHARD_COLS, D_MODEL)
    everyone = tuple(range(N_DEV))
    w_in_b = w_in_t.astype(BF16)
    b_uq, b_ukv, b_gate, b_pm, b_pg, b_o = [s[0][0].astype(BF16) for s in sharded[1:]]
    stages = ((0, 2, 4, 6), (1, 3, 5, 7))
    where = {d: (k, i) for k, srcs in enumerate(stages) for i, d in enumerate(srcs)}
    g_in_1, g_uq, g_ukv, g_gate = _all_gather(
        "all_gather_first", [w_in_b, b_uq, b_ukv, b_gate], [stages[0]] + [everyone] * 3)
    w_uq_p = jnp.pad(_cols_full(g_uq), ((0, 0), (0, 0), (0, HEAD_PAD - MLA_QK))).reshape(
        MLA_Q_RANK, MLA_HEADS * HEAD_PAD)
    ukv = _cols_full(g_ukv)
    w_k_p = jnp.pad(ukv[:, :, :MLA_NOPE], ((0, 0), (0, 0), (0, HEAD_PAD - MLA_NOPE))).reshape(
        MLA_KV_RANK, MLA_HEADS * HEAD_PAD)
    w_v = ukv[:, :, MLA_NOPE:].reshape(MLA_KV_RANK, MLA_WIDTH)
    w_gate_p = jnp.pad(_cols_full(g_gate).reshape(GLA_GATE_RANK, GLA_DK),
                       ((MISC_ALR, LANE - MISC_ALR - GLA_GATE_RANK), (0, 0)))
    rc, rsn, rsp = _rope_tables(positions)

    w_lat = _weights_to_p("weights_latents", [g_in_1], where, 2)
    proj_lat, h, rstd = _inproj(x2d, g_in, w_lat)
    q, k, v, log_a, pre, cqn, ckvn, rq, rkv, misc = _mla_prep(
        proj_lat, g_q, g_kv, w_uq_p, w_k_p, w_v, w_gate_p, b_gla_gate, rc, rsn, rsp)
    o_mla, lse, (g_in_2, g_pm, g_pg, g_o) = _mla_attn_fwd(
        q, k, v, [w_in_b, b_pm, b_pg, b_o], [stages[1]] + [everyone] * 3)
    w_gla = _weights_to_p("weights_gla", [g_in_1, g_in_2], where, 0)
    proj_gla = _proj("inproj_gla", h, w_gla)
    o_gla, states = _gla_fwd(proj_gla, log_a)
    w_out_path = _weights_to_p("weights_out_path", [g_in_1, g_in_2], where, 1)
    proj_out = _proj("inproj_out_path", h, w_out_path)
    w_in_p = (w_gla, w_out_path, w_lat)
    w_pm = _cols_full(g_pm).reshape(MLA_WIDTH, D_MODEL)
    w_pg = g_pg.reshape(GLA_DV, D_MODEL)
    w_o = g_o.reshape(D_MODEL, D_MODEL)

    (dx2, do_mla, do_gla, d_out, merged, um, ug, dym, dyg, loss_p, dg_final,
     dg_gla) = _post(o_mla, proj_out, o_gla, x2d, tgt, g_gla, g_final2, w_pm, w_pg, w_o)

    p_pm = _matmul("dw_proj_mla", um, dym, 512, D_MODEL, BF16).reshape(
        MLA_WIDTH, N_DEV, D_MODEL // N_DEV).transpose(1, 0, 2)
    p_pg = _matmul("dw_proj_gla", ug, dyg, 512, D_MODEL, BF16).reshape(N_DEV, -1, D_MODEL)
    p_o = _matmul("dw_out", merged, dx2, 512, D_MODEL, BF16).reshape(N_DEV, -1, D_MODEL)
    own_in = jnp.zeros((SHARD_PAD, D_MODEL), BF16)
    land_in = lax.empty((PEERS, SHARD_PAD, D_MODEL), BF16)
    dw_groups, started, lands = {}, [], [land_in]

    def reduce_scatter_stage(s, dests, own_in, extra=()):
        parts_in, own_in = _grads_to_shards("grads_to_shards_%d" % s, dw_groups, dests, own_in)
        first = len(lands)
        lands.extend(lax.empty((PEERS,) + p.shape[1:], BF16) for p in extra)
        idx = [0] + list(range(first, len(lands)))
        all_dests = [[(i, d, r0, r1) for i, (d, ranges) in enumerate(dests) for r0, r1 in ranges]]
        all_dests += [_whole(everyone, p.shape[1]) for p in extra]
        sems, parts, new_lands, token = _ici_start(
            "ici_start_%d" % s, [parts_in] + list(extra), [lands[i] for i in idx], all_dests)
        for a, i in enumerate(idx):
            lands[i] = new_lands[a]
            started.append((sems[a][0], sems[a][1], parts[a], i, all_dests[a]))
        return own_in, token

    def late_small_stage(arrays):
        idx = list(range(len(lands), len(lands) + len(arrays)))
        lands.extend(lax.empty((PEERS,) + p.shape[1:], BF16) for p in arrays)
        all_dests = [_whole(everyone, p.shape[1]) for p in arrays]
        sems, parts, new_lands, token = _ici_start(
            "ici_start_4", list(arrays), [lands[i] for i in idx], all_dests)
        for a, i in enumerate(idx):
            lands[i] = new_lands[a]
            started.append((sems[a][0], sems[a][1], parts[a], i, all_dests[a]))
        return token

    dw_groups[1] = _matmul("dw_in_1", d_out, h, 512, D_MODEL, BF16)
    full = [(0, SHARD_PAD)]
    own_in, token = reduce_scatter_stage(
        1, [(5, full), (6, full), (7, full), (0, [(672, SHARD_PAD)]), (1, [(0, 384)]),
            (4, [(96, SHARD_PAD)])], own_in, (p_pm, p_pg, p_o))
    d_gla, dla = _gla_bwd(proj_gla, log_a, do_gla, states, token)
    dw_groups[0] = _matmul("dw_in_0", d_gla, h, 512, D_MODEL, BF16)
    own_in, token = reduce_scatter_stage(
        2, [(1, [(384, SHARD_PAD)]), (2, full), (3, full), (4, [(0, 64)])], own_in)
    dq, dk, dv = _mla_attn_bwd(q, k, v, o_mla, do_mla, lse, token)
    d_lat, dqpre, dpre, dg_q, dg_kv, db_gate = _mla_prep_bwd(
        dq, dk, dv, dla, pre, proj_lat, rq, rkv, g_q, g_kv, w_uq_p, w_k_p, w_v, w_gate_p, rc, rsn, rsp)
    dw_groups[2] = _matmul("dw_in_2", d_lat, h, 896, D_MODEL, BF16)
    own_in, token = reduce_scatter_stage(3, [(0, [(0, 672)]), (4, [(64, 96)])], own_in)
    dw_uq = _matmul("dw_uq", cqn, dqpre, MLA_Q_RANK, D_MODEL, BF16, after=token)
    p_uq = dw_uq.reshape(MLA_Q_RANK, MLA_HEADS, HEAD_PAD)[:, :, :MLA_QK].transpose(1, 0, 2)
    dw_k = _matmul("dw_uk", ckvn, dk, MLA_KV_RANK, D_MODEL, BF16)
    dw_v = _matmul("dw_uv", ckvn, dv, MLA_KV_RANK, 512, BF16)
    p_ukv = jnp.concatenate(
        [dw_k.reshape(MLA_KV_RANK, MLA_HEADS, HEAD_PAD)[:, :, :MLA_NOPE],
         dw_v.reshape(MLA_KV_RANK, MLA_HEADS, MLA_VDIM)], axis=2).transpose(1, 0, 2)
    dw_gate = _matmul("dw_gate", misc, dpre, LANE, 512, BF16)
    p_gate = dw_gate[MISC_ALR:MISC_ALR + GLA_GATE_RANK].reshape(
        GLA_GATE_RANK, N_DEV, GLA_DK // N_DEV).transpose(1, 0, 2)
    token = late_small_stage((p_uq, p_ukv, p_gate))
    grad_x, dg_in = _inproj_bwd((d_gla, d_out, d_lat), w_in_p, x2d, rstd, g_in, dx2, token)
    small = jnp.concatenate([dg_in.reshape(-1), dg_q.reshape(-1), dg_kv.reshape(-1),
                             db_gate.reshape(-1), dg_gla.reshape(-1), dg_final.reshape(-1),
                             loss_p[0, :1]])
    small = jnp.pad(small, (0, SMALL_ROWS * LANE - small.shape[0])).reshape(SMALL_ROWS, LANE)

    (small_all,) = _all_gather("all_gather_small", [small], [everyone])
    lands = _ici_wait("ici_wait", started, lands, small_all)
    big = [_adamw_transposed("adamw_w_in", own_in, lands[0], *sharded[0], 512)]
    big += _adamw_group([p_uq, p_ukv, p_gate, p_pm, p_pg, p_o], list(lands[4:7]) + list(lands[1:4]),
                        *[[s[j] for s in sharded[1:]] for j in range(3)])
    replicated = [(g_in, m_g_in, v_g_in), (g_q, m_g_q, v_g_q), (g_kv, m_g_kv, v_g_kv),
                  (b_gla_gate, m_b_gla_gate, v_b_gla_gate), (g_gla, m_g_gla, v_g_gla),
                  tuple(a.reshape(1, D_MODEL) for a in (g_final, m_g_final, v_g_final))]
    gains, loss_sum = _adamw_gains(small_all, *[[s[j] for s in replicated] for j in range(3)])

    outs = {}
    names = ("w_in", "w_uq", "w_ukv", "w_gla_gate", "w_proj_mla", "w_proj_gla", "w_out")
    small_names = ("g_in", "g_q", "g_kv", "b_gla_gate", "g_gla", "g_final")
    for j, kind in enumerate(("grad", "delta", "new_m", "new_v")):
        for name, res in zip(names + small_names, list(big) + gains):
            outs[kind, name] = res[j].reshape(-1) if name == "g_final" else res[j]
    loss = loss_sum[0, 0]
    order = ("g_in", "w_in", "g_q", "w_uq", "g_kv", "w_ukv", "w_gla_gate", "b_gla_gate", "g_gla",
             "w_proj_mla", "w_proj_gla", "w_out", "g_final")
    result = [loss, grad_x.reshape(1, t, D_MODEL)]
    for kind in ("grad", "delta", "new_m", "new_v"):
        result += [outs[kind, name] for name in order]
    return tuple(result)
```

```python
import jax
import jax.numpy as jnp
from jax import lax
from jax.experimental import pallas as pl
from jax.experimental.pallas import tpu as pltpu

F32 = jnp.float32
BF16 = jnp.bfloat16
MESH = pl.DeviceIdType.MESH
N_DEV = 8

D_MODEL = 1024
EPS = 1e-6
MLA_HEADS = 8
MLA_NOPE = 64
MLA_ROPE = 32
MLA_VDIM = 64
MLA_Q_RANK = 384
MLA_KV_RANK = 256
MLA_QK = MLA_NOPE + MLA_ROPE
MLA_WIDTH = MLA_HEADS * MLA_VDIM
ROPE_THETA = 10000.0
GLA_HEADS = 4
GLA_DK = 512
GLA_DV = 1024
GLA_HK = 128
GLA_HV = 256
GLA_GATE_RANK = 16
GLA_GATE_NORM = 16.0
GLA_CHUNK = 64
GLA_CHUNKS_PER_STEP = 8
D_IN = 6320

ADAM_LR = 0.001
ADAM_B1 = 0.9
ADAM_B2 = 0.999
ADAM_EPS = 1e-08
ADAM_WD = 0.01
ADAM_STEP = 10

LANE = 128
HEAD_PAD = 128
VMEM_LIMIT = 48 * 1024 * 1024

P_VG, P_QG, P_KG = 0, 1024, 1536
P_ZGLA, P_GMLA, P_GGLA, P_ZMLA = 2048, 3072, 4096, 5120
P_CQ, P_CKV, P_MISC = 5632, 6144, 6400
P_TOTAL = 6528
P_GROUPS = ((0, 2048), (2048, 3584), (5632, 896))
MISC_KR = 64
MISC_ALR = 96
SHARD_COLS = D_IN // N_DEV
SHARD_PAD = 800
P_COMPONENTS = ((0, 384, P_CQ), (384, 256, P_CKV), (640, 32, P_MISC + MISC_KR), (672, 512, P_ZMLA),
                (1184, 512, P_QG), (1696, 512, P_KG), (2208, 1024, P_VG),
                (3232, 16, P_MISC + MISC_ALR), (3248, 1024, P_ZGLA), (4272, 1024, P_GMLA),
                (5296, 1024, P_GGLA))

SMALL_ROWS = 32


def _segments():
    segs = []
    for g0, n, p0 in P_COMPONENTS:
        g = g0
        while g < g0 + n:
            d = g // SHARD_COLS
            end = min(g0 + n, (d + 1) * SHARD_COLS)
            segs.append((d, g - d * SHARD_COLS, end - g, p0 + g - g0))
            g = end
    return segs


def _group_of(p0):
    return max(i for i, (off, _) in enumerate(P_GROUPS) if off <= p0)


def _rel(p0):
    return p0 - P_GROUPS[_group_of(p0)][0]


def _cparams(sem=None):
    if sem is None:
        return pltpu.CompilerParams(vmem_limit_bytes=VMEM_LIMIT)
    return pltpu.CompilerParams(dimension_semantics=sem, vmem_limit_bytes=VMEM_LIMIT)


def _sigmoid(v):
    return 1.0 / (1.0 + jnp.exp(-v))


def _dot(a, b):
    return jnp.dot(a, b, preferred_element_type=F32)


def _dot_nt(a, b):
    return lax.dot_general(a, b, (((1,), (1,)), ((), ())), preferred_element_type=F32)


def _dot_tn(a, b):
    return lax.dot_general(a, b, (((0,), (0,)), ((), ())), preferred_element_type=F32)


def _dot_exact(a, b):
    return jnp.dot(a, b, preferred_element_type=F32, precision=lax.Precision.HIGHEST)


def _rope_fwd(blk, c, sn, sp):
    return blk * c + pltpu.roll(blk, LANE - 16, 1) * sn + pltpu.roll(blk, 16, 1) * sp


def _rope_bwd(blk, c, sn, sp):
    return blk * c + pltpu.roll(blk * sn, 16, 1) + pltpu.roll(blk * sp, LANE - 16, 1)


def _mesh_pos():
    return lax.axis_index("x"), lax.axis_index("y"), lax.axis_index("c")


def _hbm_specs(n):
    return [pl.BlockSpec(memory_space=pltpu.HBM) for _ in range(n)]


def _dev(d):
    return d >> 2, (d >> 1) & 1, d & 1


def _gather_plan(shards, sources):
    na, most = len(shards), max(len(s) for s in sources)
    out_shape = [jax.ShapeDtypeStruct((len(srcs),) + s.shape, s.dtype)
                 for s, srcs in zip(shards, sources)]
    sems = [pltpu.SemaphoreType.DMA((na, most)) for _ in range(3)]
    sems += [pltpu.SemaphoreType.DMA((na, most, 3))]
    sems += [pltpu.SemaphoreType.DMA((na, most)) for _ in range(3)]
    return out_shape, sems


def _gather_hooks(x_refs, out_refs, sems, sources):
    local_sems, d2d_send, d2d_recv, ici_send, ici_recv, fwd_send, fwd_recv = sems
    x, y, c = _mesh_pos()
    chips = [(1 - x, y), (x, 1 - y), (1 - x, 1 - y)]
    items = []
    for a, srcs in enumerate(sources):
        for i, d in enumerate(srcs):
            dx, dy, dc = _dev(d)
            near = jnp.logical_and(x == dx, y == dy)
            far = jnp.logical_not(near)
            slot = out_refs[a].at[i]

            def remote(src, to, send_sem, recv_sem, slot=slot):
                return pltpu.make_async_remote_copy(
                    src_ref=src, dst_ref=slot, send_sem=send_sem, recv_sem=recv_sem,
                    device_id=to, device_id_type=MESH)

            items.append(dict(
                me=jnp.logical_and(near, c == dc), sibling=jnp.logical_and(near, c != dc),
                relay=jnp.logical_and(far, c == dc), behind=jnp.logical_and(far, c != dc),
                local=pltpu.make_async_copy(x_refs[a], slot, local_sems.at[a, i]),
                to_sibling=remote(x_refs[a], (x, y, 1 - c), d2d_send.at[a, i], d2d_recv.at[a, i]),
                to_chips=[remote(x_refs[a], (*chip, c), ici_send.at[a, i, j], ici_recv.at[a, i])
                          for j, chip in enumerate(chips)],
                forward=remote(slot, (x, y, 1 - c), fwd_send.at[a, i], fwd_recv.at[a, i])))

    def start():
        for it in items:
            @pl.when(it["me"])
            def _(it=it):
                it["local"].start()
                it["to_sibling"].start()
                for cp in it["to_chips"]:
                    cp.start()

    def finish():
        for it in items:
            @pl.when(it["relay"])
            def _(it=it):
                it["to_chips"][0].wait_recv()
                it["forward"].start()
        for it in items:
            pl.when(it["sibling"])(it["to_sibling"].wait_recv)
            pl.when(it["behind"])(it["forward"].wait_recv)
            pl.when(it["relay"])(it["forward"].wait_send)

            @pl.when(it["me"])
            def _(it=it):
                it["local"].wait()
                it["to_sibling"].wait_send()
                for cp in it["to_chips"]:
                    cp.wait_send()

    return start, finish


def _all_gather(name, shards, sources):
    n = len(shards)
    out_shape, sems = _gather_plan(shards, sources)

    def body(*refs):
        start, finish = _gather_hooks(refs[:n], refs[n:2 * n], refs[2 * n:], sources)
        start()
        finish()

    return pl.pallas_call(
        body, name=name,
        out_shape=tuple(out_shape),
        in_specs=_hbm_specs(n), out_specs=tuple(_hbm_specs(n)),
        scratch_shapes=sems,
        compiler_params=_cparams(),
    )(*shards)


PEERS = N_DEV - 1


def _whole(dests, rows):
    return [(i, d, 0, rows) for i, d in enumerate(dests)]


def _ici_copies(p_ref, land_ref, send_sems, recv_sems, pieces):
    x, y, c = _mesh_pos()
    sends, arrivals = [], []

    def rows_of(ref, j, r0, r1):
        return ref.at[j] if (r0, r1) == (0, ref.shape[1]) else ref.at[j, pl.ds(r0, r1 - r0)]

    for p, (i, d, r0, r1) in enumerate(pieces):
        dx, dy, dc = _dev(d)
        k = (4 * (x != dx).astype(jnp.int32) + 2 * (y != dy).astype(jnp.int32)
             + (c != dc).astype(jnp.int32))
        slot = jnp.maximum(k - 1, 0)
        sends.append((k > 0, pltpu.make_async_remote_copy(
            src_ref=rows_of(p_ref, i, r0, r1), dst_ref=rows_of(land_ref, slot, r0, r1),
            send_sem=send_sems.at[p], recv_sem=recv_sems.at[p * PEERS + slot],
            device_id=(dx, dy, dc), device_id_type=MESH)))
        arrivals.append((k == 0, [pltpu.make_async_remote_copy(
            src_ref=rows_of(p_ref, i, r0, r1), dst_ref=rows_of(land_ref, r, r0, r1),
            send_sem=send_sems.at[p], recv_sem=recv_sems.at[p * PEERS + r],
            device_id=(dx, dy, dc), device_id_type=MESH) for r in range(PEERS)]))
    return sends, arrivals


def _ici_start(name, hs, lands, dests):
    na = len(hs)

    def body(*refs):
        h_refs, land_refs, sems = refs[:na], refs[na:2 * na], refs[2 * na:4 * na]
        token = refs[-1]
        for a in range(na):
            sends, _ = _ici_copies(h_refs[a], land_refs[a], sems[2 * a], sems[2 * a + 1], dests[a])
            for go, cp in sends:
                pl.when(go)(cp.start)
        token[...] = jnp.zeros_like(token)

    hbm, sem = pl.BlockSpec(memory_space=pltpu.HBM), pl.BlockSpec(memory_space=pltpu.SEMAPHORE)
    sem_shapes = []
    for a in range(na):
        sem_shapes += [pltpu.SemaphoreType.DMA((len(dests[a]),)),
                       pltpu.SemaphoreType.DMA((len(dests[a]) * PEERS,))]
    res = pl.pallas_call(
        body, name=name,
        out_shape=tuple(sem_shapes) + tuple(pltpu.HBM(v.shape, v.dtype) for v in list(hs) + list(lands))
        + (jax.ShapeDtypeStruct((8, LANE), F32),),
        in_specs=(hbm,) * (2 * na),
        out_specs=(sem,) * (2 * na) + (hbm,) * (2 * na) + (pl.BlockSpec(memory_space=pltpu.VMEM),),
        input_output_aliases={i: 2 * na + i for i in range(2 * na)},
        compiler_params=pltpu.CompilerParams(
            has_side_effects=pltpu.SideEffectType.DATAFLOW_SIDE_EFFECTING,
            vmem_limit_bytes=VMEM_LIMIT),
    )(*[pltpu.with_memory_space_constraint(v, pltpu.HBM) for v in list(hs) + list(lands)])
    sems = [(res[2 * a], res[2 * a + 1]) for a in range(na)]
    return sems, res[2 * na:3 * na], res[3 * na:4 * na], res[-1]


def _ici_wait(name, started, lands, after):
    k, nl = len(started), len(lands)

    def body(*refs):
        land_refs = refs[3 * k:3 * k + nl]
        for s in range(k):
            h_ref, send_sems, recv_sems = refs[3 * s:3 * s + 3]
            sends, arrivals = _ici_copies(h_ref, land_refs[started[s][3]], send_sems, recv_sems,
                                          started[s][4])
            for go, cp in sends:
                pl.when(go)(cp.wait_send)
            for here, cps in arrivals:
                for cp in cps:
                    pl.when(here)(cp.wait_recv)

    hbm, sem = pl.BlockSpec(memory_space=pltpu.HBM), pl.BlockSpec(memory_space=pltpu.SEMAPHORE)
    operands, specs = [], []
    for send_sems, recv_sems, h, _, _ in started:
        operands += [h, send_sems, recv_sems]
        specs += [hbm, sem, sem]
    return pl.pallas_call(
        body, name=name,
        out_shape=tuple(pltpu.HBM(v.shape, v.dtype) for v in lands),
        in_specs=tuple(specs) + (hbm,) * nl + (pl.BlockSpec(memory_space=pl.ANY),),
        out_specs=(hbm,) * nl,
        input_output_aliases={3 * k + i: i for i in range(nl)},
        compiler_params=pltpu.CompilerParams(
            has_side_effects=pltpu.SideEffectType.DATAFLOW_SIDE_EFFECTING,
            vmem_limit_bytes=VMEM_LIMIT),
    )(*operands, *lands, after)


def _weights_to_p(name, gathered, where, group):
    tl = 512
    off, width = P_GROUPS[group]
    segs = sorted([s for s in _segments() if _group_of(s[3]) == group], key=lambda s: s[3])
    used = sorted({where[s[0]][0] for s in segs})

    def body(*refs):
        g_refs, o_ref = dict(zip(used, refs[:-1])), refs[-1]
        pieces, pos = [], off
        for d, c0, n, p0 in segs:
            if p0 > pos:
                pieces.append(jnp.zeros((p0 - pos, tl), F32))
            k, slot = where[d]
            pieces.append(g_refs[k][slot, c0:c0 + n, :].astype(F32))
            pos = p0 + n
        if off + width > pos:
            pieces.append(jnp.zeros((off + width - pos, tl), F32))
        o_ref[...] = jnp.concatenate(pieces, axis=0).astype(BF16)

    return pl.pallas_call(
        body, name=name,
        grid=(D_MODEL // tl,),
        in_specs=[pl.BlockSpec((gathered[k].shape[0], SHARD_COLS, tl), lambda i: (0, 0, i))
                  for k in used],
        out_specs=pl.BlockSpec((width, tl), lambda i: (0, i)),
        out_shape=jax.ShapeDtypeStruct((width, D_MODEL), BF16),
        compiler_params=_cparams(("arbitrary",)),
    )(*[gathered[k] for k in used])


def _grads_to_shards(name, groups, dests, own_prev):
    tl = 512
    segs = _segments()
    used = sorted(groups)

    def body(*refs):
        g_refs, prev_ref, o_ref, own_ref = dict(zip(used, refs[:-3])), refs[-3], refs[-2], refs[-1]
        x, y, c = _mesh_pos()
        own = prev_ref[...].astype(F32)
        row = lax.broadcasted_iota(jnp.int32, (SHARD_PAD, tl), 0)
        for i, (d, ranges) in enumerate(dests):
            pieces, pos, asked = [], 0, None
            for r0, r1 in sorted(ranges):
                if r0 > pos:
                    pieces.append(jnp.zeros((r0 - pos, tl), F32))
                for _, c0, n, p0 in sorted([s for s in segs if s[0] == d], key=lambda s: s[1]):
                    a, b = max(c0, r0), min(c0 + n, r1)
                    if a < b:
                        gi = _group_of(p0)
                        lo = p0 - P_GROUPS[gi][0] + a - c0
                        pieces.append(g_refs[gi][lo:lo + b - a, :].astype(F32))
                if r1 > SHARD_COLS:
                    pieces.append(jnp.zeros((r1 - max(r0, SHARD_COLS), tl), F32))
                pos = r1
                inside = jnp.logical_and(row >= r0, row < r1)
                asked = inside if asked is None else jnp.logical_or(asked, inside)
            if pos < SHARD_PAD:
                pieces.append(jnp.zeros((SHARD_PAD - pos, tl), F32))
            shard = jnp.concatenate(pieces, axis=0)
            o_ref[i] = shard.astype(BF16)
            own = jnp.where(jnp.logical_and(4 * x + 2 * y + c == d, asked), shard, own)
        own_ref[...] = own.astype(BF16)

    blk = pl.BlockSpec((SHARD_PAD, tl), lambda i: (0, i))
    return pl.pallas_call(
        body, name=name,
        grid=(D_MODEL // tl,),
        in_specs=[pl.BlockSpec((P_GROUPS[g][1], tl), lambda i: (0, i)) for g in used] + [blk],
        out_specs=(pl.BlockSpec((len(dests), SHARD_PAD, tl), lambda i: (0, 0, i)), blk),
        out_shape=(jax.ShapeDtypeStruct((len(dests), SHARD_PAD, D_MODEL), BF16),
                   jax.ShapeDtypeStruct((SHARD_PAD, D_MODEL), BF16)),
        input_output_aliases={len(used): 1},
        compiler_params=_cparams(("arbitrary",)),
    )(*[groups[g] for g in used], own_prev)


def _inproj(x, g_in, w_pt):
    t = x.shape[0]
    tm = min(512, t)
    width = w_pt.shape[0]

    def body(x_ref, g_ref, w_ref, proj_ref, h_ref, r_ref):
        xf = x_ref[...]
        r = lax.rsqrt(jnp.mean(xf * xf, axis=-1, keepdims=True) + EPS)
        h = ((xf * r) * g_ref[...]).astype(BF16)
        proj_ref[...] = _dot_nt(h, w_ref[...])
        h_ref[...] = h
        r_ref[...] = r

    row = lambda w: pl.BlockSpec((tm, w), lambda i: (i, 0))
    return pl.pallas_call(
        body, name="inproj_latents",
        grid=(t // tm,),
        in_specs=[row(D_MODEL), pl.BlockSpec((1, D_MODEL), lambda i: (0, 0)),
                  pl.BlockSpec((width, D_MODEL), lambda i: (0, 0))],
        out_specs=(row(width), row(D_MODEL), row(1)),
        out_shape=(jax.ShapeDtypeStruct((t, width), F32),
                   jax.ShapeDtypeStruct((t, D_MODEL), BF16),
                   jax.ShapeDtypeStruct((t, 1), F32)),
        compiler_params=_cparams(("arbitrary",)),
    )(x, g_in, w_pt)


def _proj(name, h, w_pt):
    t = h.shape[0]
    tm = min(512, t)
    width = w_pt.shape[0]

    def body(h_ref, w_ref, o_ref):
        o_ref[...] = _dot_nt(h_ref[...], w_ref[...])

    return pl.pallas_call(
        body, name=name,
        grid=(t // tm,),
        in_specs=[pl.BlockSpec((tm, D_MODEL), lambda i: (i, 0)),
                  pl.BlockSpec((width, D_MODEL), lambda i: (0, 0))],
        out_specs=pl.BlockSpec((tm, width), lambda i: (i, 0)),
        out_shape=jax.ShapeDtypeStruct((t, width), F32),
        compiler_params=_cparams(("arbitrary",)),
    )(h, w_pt)


def _mla_prep(proj, g_q, g_kv, w_uq_p, w_k_p, w_v, w_gate_p, b_gate, rc, rsn, rsp):
    t = proj.shape[0]
    tm = min(256, t)
    hq = MLA_HEADS * HEAD_PAD

    def body(cq_ref, ckv_ref, misc_ref, gq_ref, gkv_ref, wuq_ref, wk_ref, wv_ref, wg_ref, bg_ref,
             c_ref, sn_ref, sp_ref,
             q_ref, k_ref, v_ref, la_ref, pre_ref, cqn_ref, ckvn_ref, rq_ref, rkv_ref, mb_ref):
        c, sn, sp = c_ref[...], sn_ref[...], sp_ref[...]
        cq = cq_ref[:, :MLA_Q_RANK]
        rq = lax.rsqrt(jnp.mean(cq * cq, axis=-1, keepdims=True) + EPS)
        cqn = ((cq * rq) * gq_ref[...]).astype(BF16)
        cqn_ref[...] = cqn
        rq_ref[...] = rq
        qpre = _dot(cqn, wuq_ref[...])
        ckv = ckv_ref[...]
        rkv = lax.rsqrt(jnp.mean(ckv * ckv, axis=-1, keepdims=True) + EPS)
        ckvn = ((ckv * rkv) * gkv_ref[...]).astype(BF16)
        ckvn_ref[...] = ckvn
        rkv_ref[...] = rkv
        kn = _dot(ckvn, wk_ref[...])
        v_ref[...] = _dot(ckvn, wv_ref[...]).astype(BF16)
        misc = misc_ref[...]
        krope = _rope_fwd(misc, c, sn, sp)
        for h in range(MLA_HEADS):
            sl = slice(h * HEAD_PAD, (h + 1) * HEAD_PAD)
            q_ref[:, sl] = _rope_fwd(qpre[:, sl], c, sn, sp).astype(BF16)
            k_ref[:, sl] = (kn[:, sl] + krope).astype(BF16)
        mb_ref[...] = misc.astype(BF16)
        pre = _dot(mb_ref[...], wg_ref[...]) + bg_ref[...]
        pre_ref[...] = pre
        log_a = (jnp.minimum(pre, 0.0) - jnp.log(1.0 + jnp.exp(-jnp.abs(pre)))) / GLA_GATE_NORM
        la_ref[...] = _dot_exact(_chunk_tri(tm, True), log_a)

    row = lambda w: pl.BlockSpec((tm, w), lambda i: (i, 0))
    full = lambda a: pl.BlockSpec(a.shape, lambda i: (0, 0))
    return pl.pallas_call(
        body, name="mla_prep",
        grid=(t // tm,),
        in_specs=[pl.BlockSpec((tm, 512), lambda i: (i, _rel(P_CQ) // 512)),
                  pl.BlockSpec((tm, MLA_KV_RANK), lambda i: (i, _rel(P_CKV) // MLA_KV_RANK)),
                  pl.BlockSpec((tm, LANE), lambda i: (i, _rel(P_MISC) // LANE)),
                  full(g_q), full(g_kv), full(w_uq_p), full(w_k_p), full(w_v), full(w_gate_p),
                  full(b_gate), row(LANE), row(LANE), row(LANE)],
        out_specs=(row(hq), row(hq), row(MLA_WIDTH), row(GLA_DK), row(GLA_DK),
                   row(MLA_Q_RANK), row(MLA_KV_RANK), row(1), row(1), row(LANE)),
        out_shape=(jax.ShapeDtypeStruct((t, hq), BF16), jax.ShapeDtypeStruct((t, hq), BF16),
                   jax.ShapeDtypeStruct((t, MLA_WIDTH), BF16),
                   jax.ShapeDtypeStruct((t, GLA_DK), F32), jax.ShapeDtypeStruct((t, GLA_DK), F32),
                   jax.ShapeDtypeStruct((t, MLA_Q_RANK), BF16),
                   jax.ShapeDtypeStruct((t, MLA_KV_RANK), BF16),
                   jax.ShapeDtypeStruct((t, 1), F32), jax.ShapeDtypeStruct((t, 1), F32),
                   jax.ShapeDtypeStruct((t, LANE), BF16)),
        compiler_params=_cparams(("arbitrary",)),
    )(proj, proj, proj, g_q, g_kv, w_uq_p, w_k_p, w_v, w_gate_p, b_gate, rc, rsn, rsp)


def _attn_masks(tq, i):
    keys = (i + 1) * tq
    rows = i * tq + lax.broadcasted_iota(jnp.int32, (tq, keys), 0)
    cols = lax.broadcasted_iota(jnp.int32, (tq, keys), 1)
    lane = lax.broadcasted_iota(jnp.int32, (tq, LANE), 1)
    return cols <= rows, lane < MLA_VDIM


def _for_each_query_tile(n_tiles, fn):
    for i in range(n_tiles):
        pl.when(pl.program_id(1) == i)(lambda i=i: fn(i))


def _mla_attn_fwd(q, k, v, shards, sources):
    t = q.shape[0]
    tq = min(256, t)
    scale = MLA_QK ** -0.5
    ns = len(shards)
    g_shapes, g_sems = _gather_plan(shards, sources)
    grid = (MLA_HEADS // 2, t // tq)

    def body(q_ref, k_ref, v_ref, *rest):
        o_ref, lse_ref = rest[ns:ns + 2]
        start, finish = _gather_hooks(rest[:ns], rest[ns + 2:2 * ns + 2], rest[2 * ns + 2:], sources)
        step = pl.program_id(0) * grid[1] + pl.program_id(1)
        pl.when(step == 0)(start)

        def tile(i):
            keys = (i + 1) * tq
            causal, low = _attn_masks(tq, i)
            vp = v_ref[0:keys, :]
            acc = jnp.zeros((tq, LANE), F32)
            for hh in range(2):
                sl = slice(hh * HEAD_PAD, (hh + 1) * HEAD_PAD)
                s = _dot_nt(q_ref[:, sl], k_ref[0:keys, sl]) * scale
                s = jnp.where(causal, s, -jnp.inf)
                m = jnp.max(s, axis=-1, keepdims=True)
                e = jnp.exp(s - m)
                l = jnp.sum(e, axis=-1, keepdims=True)
                o = _dot(e.astype(BF16), vp) / l
                acc = jnp.where(low if hh == 0 else jnp.logical_not(low), o, acc)
                lse_ref[hh] = m + jnp.log(l)
            o_ref[...] = acc

        _for_each_query_tile(t // tq, tile)
        pl.when(step == grid[0] * grid[1] - 1)(finish)

    res = pl.pallas_call(
        body, name="mla_attn_fwd",
        grid=grid,
        in_specs=[pl.BlockSpec((tq, 2 * HEAD_PAD), lambda p, i: (i, p)),
                  pl.BlockSpec((t, 2 * HEAD_PAD), lambda p, i: (0, p)),
                  pl.BlockSpec((t, LANE), lambda p, i: (0, p))] + _hbm_specs(ns),
        out_specs=(pl.BlockSpec((tq, LANE), lambda p, i: (i, p)),
                   pl.BlockSpec((2, tq, 1), lambda p, i: (p, i, 0))) + tuple(_hbm_specs(ns)),
        out_shape=(jax.ShapeDtypeStruct((t, MLA_WIDTH), F32),
                   jax.ShapeDtypeStruct((MLA_HEADS, t, 1), F32)) + tuple(g_shapes),
        scratch_shapes=g_sems,
        compiler_params=_cparams(("arbitrary", "arbitrary")),
    )(q, k, v, *shards)
    return res[0], res[1], res[2:]


def _mla_attn_bwd(q, k, v, o, do, lse, after):
    t = q.shape[0]
    tq = min(256, t)
    scale = MLA_QK ** -0.5

    def body(q_ref, k_ref, v_ref, o_ref, do_ref, lse_ref, after_ref, dq_ref, dk_ref, dv_ref):
        del after_ref

        @pl.when(pl.program_id(1) == 0)
        def _():
            dk_ref[...] = jnp.zeros_like(dk_ref)
            dv_ref[...] = jnp.zeros_like(dv_ref)

        def tile(i):
            keys = (i + 1) * tq
            causal, low = _attn_masks(tq, i)
            vp = v_ref[0:keys, :]
            do_all = do_ref[...]
            o_all = o_ref[...]
            dv_acc = jnp.zeros((keys, LANE), F32)
            for hh in range(2):
                sl = slice(hh * HEAD_PAD, (hh + 1) * HEAD_PAD)
                do_h = jnp.where(low if hh == 0 else jnp.logical_not(low), do_all, 0.0)
                dsum = jnp.sum(do_h * o_all, axis=-1, keepdims=True)
                qh = q_ref[:, sl]
                kh = k_ref[0:keys, sl]
                s = _dot_nt(qh, kh) * scale
                p = jnp.where(causal, jnp.exp(s - lse_ref[hh]), 0.0)
                do_b = do_h.astype(BF16)
                dp = _dot_nt(do_b, vp)
                ds = (p * (dp - dsum) * scale).astype(BF16)
                dq_ref[:, sl] = _dot(ds, kh).astype(BF16)
                dk_ref[0:keys, sl] += _dot_tn(ds, qh)
                dv_acc = dv_acc + _dot_tn(p.astype(BF16), do_b)
            dv_ref[0:keys, :] += dv_acc

        _for_each_query_tile(t // tq, tile)

    return pl.pallas_call(
        body, name="mla_attn_bwd",
        grid=(MLA_HEADS // 2, t // tq),
        in_specs=[pl.BlockSpec((tq, 2 * HEAD_PAD), lambda p, i: (i, p)),
                  pl.BlockSpec((t, 2 * HEAD_PAD), lambda p, i: (0, p)),
                  pl.BlockSpec((t, LANE), lambda p, i: (0, p)),
                  pl.BlockSpec((tq, LANE), lambda p, i: (i, p)),
                  pl.BlockSpec((tq, LANE), lambda p, i: (i, p)),
                  pl.BlockSpec((2, tq, 1), lambda p, i: (p, i, 0)),
                  pl.BlockSpec(memory_space=pl.ANY)],
        out_specs=(pl.BlockSpec((tq, 2 * HEAD_PAD), lambda p, i: (i, p)),
                   pl.BlockSpec((t, 2 * HEAD_PAD), lambda p, i: (0, p)),
                   pl.BlockSpec((t, LANE), lambda p, i: (0, p))),
        out_shape=(jax.ShapeDtypeStruct((t, MLA_HEADS * HEAD_PAD), BF16),
                   jax.ShapeDtypeStruct((t, MLA_HEADS * HEAD_PAD), F32),
                   jax.ShapeDtypeStruct((t, MLA_WIDTH), F32)),
        compiler_params=_cparams(("arbitrary", "arbitrary")),
    )(q, k, v, o, do, lse, after)


def _chunk_tri(n, lower):
    r = lax.broadcasted_iota(jnp.int32, (n, n), 0)
    c = lax.broadcasted_iota(jnp.int32, (n, n), 1)
    same = (r // GLA_CHUNK) == (c // GLA_CHUNK)
    return jnp.where(jnp.logical_and(same, r >= c if lower else r <= c), 1.0, 0.0).astype(F32)


def _gla_chunk_terms(q_ref, k_ref, b_ref, h, rows):
    sl = slice(h * GLA_HK, (h + 1) * GLA_HK)
    b = b_ref[rows, sl]
    bl = b[GLA_CHUNK - 1:GLA_CHUNK, :]
    kc = k_ref[rows, sl]
    q_in = (q_ref[rows, sl] * (GLA_HK ** -0.5)) * jnp.exp(b)
    k_in = kc * jnp.exp(-b)
    k_st = kc * jnp.exp(bl - b)
    return b, bl, q_in, k_in, k_st


def _tri(c, lower):
    r = lax.broadcasted_iota(jnp.int32, (c, c), 0)
    cc = lax.broadcasted_iota(jnp.int32, (c, c), 1)
    return jnp.where(r >= cc if lower else r <= cc, 1.0, 0.0).astype(F32)


def _gla_fwd(proj, log_a):
    t = proj.shape[0]
    per = GLA_CHUNKS_PER_STEP
    n = t // GLA_CHUNK
    c = GLA_CHUNK * per

    def body(q_ref, k_ref, v_ref, la_ref, o_ref, sp_ref, st_ref):
        @pl.when(pl.program_id(0) == 0)
        def _():
            st_ref[...] = jnp.zeros_like(st_ref)

        tri = _tri(GLA_CHUNK, True)
        for s, h in [(s, h) for s in range(per) for h in range(GLA_HEADS)]:
            rows = slice(s * GLA_CHUNK, (s + 1) * GLA_CHUNK)
            _, bl, q_in, k_in, k_st = _gla_chunk_terms(q_ref, k_ref, la_ref, h, rows)
            vs = slice(h * GLA_HV, (h + 1) * GLA_HV)
            vv = v_ref[rows, vs].astype(BF16)
            qb = q_in.astype(BF16)
            attn = _dot_nt(qb, k_in.astype(BF16)) * tri
            st = st_ref[h]
            sp_ref[s, h] = st
            o_ref[rows, vs] = _dot(attn.astype(BF16), vv) + _dot_nt(qb, st.astype(BF16))
            st_ref[h] = st * jnp.exp(bl) + _dot_tn(vv, k_st.astype(BF16))

    return pl.pallas_call(
        body, name="gla_fwd",
        grid=(n // per,),
        in_specs=[pl.BlockSpec((c, GLA_DK), lambda i: (i, P_QG // GLA_DK)),
                  pl.BlockSpec((c, GLA_DK), lambda i: (i, P_KG // GLA_DK)),
                  pl.BlockSpec((c, GLA_DV), lambda i: (i, P_VG // GLA_DV)),
                  pl.BlockSpec((c, GLA_DK), lambda i: (i, 0))],
        out_specs=(pl.BlockSpec((c, GLA_DV), lambda i: (i, 0)),
                   pl.BlockSpec((per, GLA_HEADS, GLA_HV, GLA_HK), lambda i: (i, 0, 0, 0))),
        out_shape=(jax.ShapeDtypeStruct((t, GLA_DV), F32),
                   jax.ShapeDtypeStruct((n, GLA_HEADS, GLA_HV, GLA_HK), F32)),
        scratch_shapes=[pltpu.VMEM((GLA_HEADS, GLA_HV, GLA_HK), F32)],
        compiler_params=_cparams(("arbitrary",)),
    )(proj, proj, proj, log_a)


def _gla_bwd(proj, log_a, do, states, after):
    t = proj.shape[0]
    per = GLA_CHUNKS_PER_STEP
    c = GLA_CHUNK * per
    n = t // c

    def body(q_ref, k_ref, v_ref, la_ref, do_ref, sp_ref, after_ref, dg_ref, dla_ref, ds_ref):
        del after_ref

        @pl.when(pl.program_id(0) == 0)
        def _():
            ds_ref[...] = jnp.zeros_like(ds_ref)

        tri = _tri(GLA_CHUNK, True)
        last = lax.broadcasted_iota(jnp.int32, (GLA_CHUNK, GLA_HK), 0) == GLA_CHUNK - 1
        for s, h in [(s, h) for s in reversed(range(per)) for h in range(GLA_HEADS)]:
            rows = slice(s * GLA_CHUNK, (s + 1) * GLA_CHUNK)
            b, bl, q_in, k_in, k_st = _gla_chunk_terms(q_ref, k_ref, la_ref, h, rows)
            ks_ = slice(h * GLA_HK, (h + 1) * GLA_HK)
            vs = slice(h * GLA_HV, (h + 1) * GLA_HV)
            vv = v_ref[rows, vs].astype(BF16)
            do_h = do_ref[rows, vs]
            qb, kb, ksb = q_in.astype(BF16), k_in.astype(BF16), k_st.astype(BF16)
            attn = (_dot_nt(qb, kb) * tri).astype(BF16)
            st = sp_ref[s, h]
            dst = ds_ref[h]
            dstb = dst.astype(BF16)
            dattn = (_dot_nt(do_h, vv) * tri).astype(BF16)
            dg_ref[rows, P_VG + h * GLA_HV:P_VG + (h + 1) * GLA_HV] = (
                _dot_tn(attn, do_h) + _dot_nt(ksb, dstb)).astype(BF16)
            dq_in = _dot(dattn, kb) + _dot(do_h, st.astype(BF16))
            dk_in = _dot_tn(dattn, qb)
            dk_st = _dot(vv, dstb)
            ebl = jnp.exp(bl)
            d_ebl = jnp.sum(st * dst, axis=0, keepdims=True)
            ds_ref[h] = _dot_tn(do_h, qb) + dst * ebl
            dg_ref[rows, P_QG + h * GLA_HK:P_QG + (h + 1) * GLA_HK] = (
                dq_in * (GLA_HK ** -0.5) * jnp.exp(b)).astype(BF16)
            dg_ref[rows, P_KG + h * GLA_HK:P_KG + (h + 1) * GLA_HK] = (
                dk_in * jnp.exp(-b) + dk_st * jnp.exp(bl - b)).astype(BF16)
            db = dq_in * q_in - dk_in * k_in - dk_st * k_st
            dbl = jnp.sum(dk_st * k_st, axis=0, keepdims=True) + d_ebl * ebl
            dla_ref[rows, ks_] = db + jnp.where(last, dbl, 0.0)

    rev = lambda i: n - 1 - i
    gw = P_GROUPS[0][1]
    return pl.pallas_call(
        body, name="gla_bwd",
        grid=(n,),
        in_specs=[pl.BlockSpec((c, GLA_DK), lambda i: (rev(i), P_QG // GLA_DK)),
                  pl.BlockSpec((c, GLA_DK), lambda i: (rev(i), P_KG // GLA_DK)),
                  pl.BlockSpec((c, GLA_DV), lambda i: (rev(i), P_VG // GLA_DV)),
                  pl.BlockSpec((c, GLA_DK), lambda i: (rev(i), 0)),
                  pl.BlockSpec((c, GLA_DV), lambda i: (rev(i), 0)),
                  pl.BlockSpec((per, GLA_HEADS, GLA_HV, GLA_HK), lambda i: (rev(i), 0, 0, 0)),
                  pl.BlockSpec(memory_space=pl.ANY)],
        out_specs=(pl.BlockSpec((c, gw), lambda i: (rev(i), 0)),
                   pl.BlockSpec((c, GLA_DK), lambda i: (rev(i), 0))),
        out_shape=(jax.ShapeDtypeStruct((t, gw), BF16), jax.ShapeDtypeStruct((t, GLA_DK), F32)),
        scratch_shapes=[pltpu.VMEM((GLA_HEADS, GLA_HV, GLA_HK), F32)],
        compiler_params=_cparams(("arbitrary",)),
    )(proj, proj, proj, log_a, do, states, after)


def _post(o_mla, proj, o_gla, x, target, g_gla, g_final, w_pm, w_pg, w_o):
    t = x.shape[0]
    tm = min(256, t)
    g0, gw = P_GROUPS[1]

    def body(om_ref, zg_ref, gm_ref, gg_ref, zm_ref, og_ref, x_ref, tg_ref, ggla_ref, gf_ref,
             wpm_ref, wpg_ref, wo_ref,
             dx2_ref, dom_ref, dog_ref, dg_ref,
             mg_ref, um_ref, ug_ref, dym_ref, dyg_ref, loss_ref, dgf_ref, dggla_ref):
        @pl.when(pl.program_id(0) == 0)
        def _():
            loss_ref[...] = jnp.zeros_like(loss_ref)
            dgf_ref[...] = jnp.zeros_like(dgf_ref)
            dggla_ref[...] = jnp.zeros_like(dggla_ref)

        om = om_ref[...]
        zm = zm_ref[...]
        sm = _sigmoid(zm)
        silu_m = zm * sm
        um = (om * silu_m).astype(BF16)
        um_ref[...] = um
        ym = _dot(um, wpm_ref[...])

        ggla = ggla_ref[...]
        zg = zg_ref[...]
        sg = _sigmoid(zg)
        silu_g = zg * sg
        xhat, rstd, on = [], [], []
        for h in range(GLA_HEADS):
            blk = og_ref[:, h * GLA_HV:(h + 1) * GLA_HV]
            r = lax.rsqrt(jnp.mean(blk * blk, axis=-1, keepdims=True) + EPS)
            xhat.append(blk * r)
            rstd.append(r)
            on.append(xhat[h] * ggla)
        on = jnp.concatenate(on, axis=-1)
        ug = (on * silu_g).astype(BF16)
        ug_ref[...] = ug
        yg = _dot(ug, wpg_ref[...])

        sgm = _sigmoid(gm_ref[...])
        sgg = _sigmoid(gg_ref[...])
        merged = (sgm * ym + sgg * yg).astype(BF16)
        mg_ref[...] = merged
        x2 = x_ref[...] + _dot(merged, wo_ref[...])
        gf = gf_ref[...]
        rf = lax.rsqrt(jnp.mean(x2 * x2, axis=-1, keepdims=True) + EPS)
        xh = x2 * rf
        err = xh * gf - tg_ref[...]
        loss_ref[...] += 0.5 * jnp.sum(jnp.mean(err * err, axis=-1, keepdims=True))

        dy = err * (1.0 / D_MODEL)
        dgf_ref[...] += jnp.sum(dy * xh, axis=0, keepdims=True)
        dxh = dy * gf
        dx2 = rf * (dxh - xh * jnp.mean(dxh * xh, axis=-1, keepdims=True))
        dx2_ref[...] = dx2
        dmerged = _dot_nt(dx2.astype(BF16), wo_ref[...])
        dym = (dmerged * sgm).astype(BF16)
        dyg = (dmerged * sgg).astype(BF16)
        dym_ref[...] = dym
        dyg_ref[...] = dyg
        dg_ref[:, P_GMLA - g0:P_GMLA - g0 + D_MODEL] = (dmerged * ym * sgm * (1.0 - sgm)).astype(BF16)
        dg_ref[:, P_GGLA - g0:P_GGLA - g0 + D_MODEL] = (dmerged * yg * sgg * (1.0 - sgg)).astype(BF16)
        dum = _dot_nt(dym, wpm_ref[...])
        dom_ref[...] = dum * silu_m
        dg_ref[:, P_ZMLA - g0:P_ZMLA - g0 + MLA_WIDTH] = (
            dum * om * (sm * (1.0 + zm * (1.0 - sm)))).astype(BF16)
        dug = _dot_nt(dyg, wpg_ref[...])
        dg_ref[:, P_ZGLA - g0:P_ZGLA - g0 + GLA_DV] = (
            dug * on * (sg * (1.0 + zg * (1.0 - sg)))).astype(BF16)
        don = dug * silu_g
        dggla = jnp.zeros((1, GLA_HV), F32)
        for h in range(GLA_HEADS):
            hs = slice(h * GLA_HV, (h + 1) * GLA_HV)
            don_h = don[:, hs]
            dggla = dggla + jnp.sum(don_h * xhat[h], axis=0, keepdims=True)
            dxh_h = don_h * ggla
            dog_ref[:, hs] = (rstd[h] * (dxh_h - xhat[h] * jnp.mean(dxh_h * xhat[h], axis=-1,
                                                                     keepdims=True))).astype(BF16)
        dggla_ref[...] += dggla

    row = lambda w: pl.BlockSpec((tm, w), lambda i: (i, 0))
    pcol = lambda w, off: pl.BlockSpec((tm, w), lambda i: (i, _rel(off) // w))
    full = lambda a: pl.BlockSpec(a.shape, lambda i: (0, 0))
    sds = jax.ShapeDtypeStruct
    return pl.pallas_call(
        body, name="post_fwd_bwd",
        grid=(t // tm,),
        in_specs=[row(MLA_WIDTH), pcol(GLA_DV, P_ZGLA), pcol(D_MODEL, P_GMLA), pcol(D_MODEL, P_GGLA),
                  pcol(MLA_WIDTH, P_ZMLA), row(GLA_DV), row(D_MODEL), row(D_MODEL),
                  full(g_gla), full(g_final), full(w_pm), full(w_pg), full(w_o)],
        out_specs=(row(D_MODEL), row(MLA_WIDTH), row(GLA_DV), row(gw),
                   row(D_MODEL), row(MLA_WIDTH), row(GLA_DV), row(D_MODEL), row(D_MODEL),
                   pl.BlockSpec((1, LANE), lambda i: (0, 0)),
                   pl.BlockSpec((1, D_MODEL), lambda i: (0, 0)),
                   pl.BlockSpec((1, GLA_HV), lambda i: (0, 0))),
        out_shape=(sds((t, D_MODEL), F32), sds((t, MLA_WIDTH), F32), sds((t, GLA_DV), BF16),
                   sds((t, gw), BF16),
                   sds((t, D_MODEL), BF16), sds((t, MLA_WIDTH), BF16), sds((t, GLA_DV), BF16),
                   sds((t, D_MODEL), BF16), sds((t, D_MODEL), BF16),
                   sds((1, LANE), F32), sds((1, D_MODEL), F32), sds((1, GLA_HV), F32)),
        compiler_params=_cparams(("arbitrary",)),
    )(o_mla, proj, proj, proj, proj, o_gla, x, target, g_gla, g_final, w_pm, w_pg, w_o)


def _mla_prep_bwd(dq, dk, dv, dla, pre, proj, rq, rkv, g_q, g_kv, w_uq_p, w_k_p, w_v, w_gate_p,
                  rc, rsn, rsp):
    t = proj.shape[0]
    tm = min(256, t)
    gw = P_GROUPS[2][1]

    def body(dq_ref, dk_ref, dv_ref, dla_ref, pre_ref, cq_ref, ckv_ref, rq_ref, rkv_ref,
             gq_ref, gkv_ref, wuq_ref, wk_ref, wv_ref, wg_ref, c_ref, sn_ref, sp_ref,
             dg_ref, dqpre_ref, dpre_ref, dgq_ref, dgkv_ref, dbg_ref):
        @pl.when(pl.program_id(0) == 0)
        def _():
            dgq_ref[...] = jnp.zeros_like(dgq_ref)
            dgkv_ref[...] = jnp.zeros_like(dgkv_ref)
            dbg_ref[...] = jnp.zeros_like(dbg_ref)

        c, sn, sp = c_ref[...], sn_ref[...], sp_ref[...]
        dkr = jnp.zeros((tm, LANE), F32)
        for h in range(MLA_HEADS):
            sl = slice(h * HEAD_PAD, (h + 1) * HEAD_PAD)
            dqpre_ref[:, sl] = _rope_bwd(dq_ref[:, sl].astype(F32), c, sn, sp).astype(BF16)
            dkr = dkr + dk_ref[:, sl]
        dcqn = _dot_nt(dqpre_ref[...], wuq_ref[...])
        rq = rq_ref[...]
        xh = cq_ref[:, :MLA_Q_RANK] * rq
        dgq_ref[...] += jnp.sum(dcqn * xh, axis=0, keepdims=True)
        dxh = dcqn * gq_ref[...]
        dcq = rq * (dxh - xh * jnp.mean(dxh * xh, axis=-1, keepdims=True))
        dg_ref[:, :MLA_Q_RANK] = dcq.astype(BF16)
        dg_ref[:, MLA_Q_RANK:512] = jnp.zeros((tm, 512 - MLA_Q_RANK), BF16)

        dckvn = _dot_nt(dk_ref[...].astype(BF16), wk_ref[...]) + \
            _dot_nt(dv_ref[...].astype(BF16), wv_ref[...])
        rkv = rkv_ref[...]
        xh = ckv_ref[...] * rkv
        dgkv_ref[...] += jnp.sum(dckvn * xh, axis=0, keepdims=True)
        dxh = dckvn * gkv_ref[...]
        dg_ref[:, P_CKV - P_CQ:P_CKV - P_CQ + MLA_KV_RANK] = (
            rkv * (dxh - xh * jnp.mean(dxh * xh, axis=-1, keepdims=True))).astype(BF16)

        dlog_a = _dot_exact(_chunk_tri(tm, False), dla_ref[...])
        dpre = dlog_a * (1.0 / GLA_GATE_NORM) * (1.0 - _sigmoid(pre_ref[...]))
        dbg_ref[...] += jnp.sum(dpre, axis=0, keepdims=True)
        dpre = dpre.astype(BF16)
        dpre_ref[...] = dpre
        lane = lax.broadcasted_iota(jnp.int32, (tm, LANE), 1)
        in_kr = jnp.logical_and(lane >= MISC_KR, lane < MISC_KR + MLA_ROPE)
        dmisc = jnp.where(in_kr, _rope_bwd(dkr, c, sn, sp), 0.0) + _dot_nt(dpre, wg_ref[...])
        dg_ref[:, P_MISC - P_CQ:P_MISC - P_CQ + LANE] = dmisc.astype(BF16)

    hq = MLA_HEADS * HEAD_PAD
    row = lambda w: pl.BlockSpec((tm, w), lambda i: (i, 0))
    full = lambda a: pl.BlockSpec(a.shape, lambda i: (0, 0))
    acc = lambda w: pl.BlockSpec((1, w), lambda i: (0, 0))
    sds = jax.ShapeDtypeStruct
    return pl.pallas_call(
        body, name="mla_prep_bwd",
        grid=(t // tm,),
        in_specs=[row(hq), row(hq), row(MLA_WIDTH), row(GLA_DK), row(GLA_DK),
                  pl.BlockSpec((tm, 512), lambda i: (i, _rel(P_CQ) // 512)),
                  pl.BlockSpec((tm, MLA_KV_RANK), lambda i: (i, _rel(P_CKV) // MLA_KV_RANK)),
                  row(1), row(1), full(g_q), full(g_kv), full(w_uq_p), full(w_k_p), full(w_v),
                  full(w_gate_p), row(LANE), row(LANE), row(LANE)],
        out_specs=(row(gw), row(hq), row(GLA_DK),
                   acc(MLA_Q_RANK), acc(MLA_KV_RANK), acc(GLA_DK)),
        out_shape=(sds((t, gw), BF16), sds((t, hq), BF16), sds((t, GLA_DK), BF16),
                   sds((1, MLA_Q_RANK), F32), sds((1, MLA_KV_RANK), F32), sds((1, GLA_DK), F32)),
        compiler_params=_cparams(("arbitrary",)),
    )(dq, dk, dv, dla, pre, proj, proj, rq, rkv, g_q, g_kv, w_uq_p, w_k_p, w_v, w_gate_p,
      rc, rsn, rsp)


def _inproj_bwd(dgroups, w_pts, x, rstd, g_in, dx2, after):
    t = x.shape[0]
    tm = min(256, t)

    def body(d0_ref, d1_ref, d2_ref, w0_ref, w1_ref, w2_ref, x_ref, r_ref, g_ref, dx2_ref, after_ref,
             dx_ref, dg_ref):
        del after_ref

        @pl.when(pl.program_id(0) == 0)
        def _():
            dg_ref[...] = jnp.zeros_like(dg_ref)

        dh = jnp.zeros((tm, D_MODEL), F32)
        for d_ref, w_ref in zip((d0_ref, d1_ref, d2_ref), (w0_ref, w1_ref, w2_ref)):
            dh = dh + _dot(d_ref[...], w_ref[...])
        r = r_ref[...]
        xh = x_ref[...] * r
        dg_ref[...] += jnp.sum(dh * xh, axis=0, keepdims=True)
        dxh = dh * g_ref[...]
        dx_ref[...] = dx2_ref[...] + r * (dxh - xh * jnp.mean(dxh * xh, axis=-1, keepdims=True))

    row = lambda w: pl.BlockSpec((tm, w), lambda i: (i, 0))
    return pl.pallas_call(
        body, name="inproj_bwd",
        grid=(t // tm,),
        in_specs=[row(w) for _, w in P_GROUPS]
        + [pl.BlockSpec((w, D_MODEL), lambda i: (0, 0)) for _, w in P_GROUPS]
        + [row(D_MODEL), row(1), pl.BlockSpec((1, D_MODEL), lambda i: (0, 0)), row(D_MODEL),
           pl.BlockSpec(memory_space=pl.ANY)],
        out_specs=(row(D_MODEL), pl.BlockSpec((1, D_MODEL), lambda i: (0, 0))),
        out_shape=(jax.ShapeDtypeStruct((t, D_MODEL), F32),
                   jax.ShapeDtypeStruct((1, D_MODEL), F32)),
        compiler_params=_cparams(("arbitrary",)),
    )(*dgroups, *w_pts, x, rstd, g_in, dx2, after)


def _matmul(name, a, b, tm, tn, dtype=F32, after=None):
    kk, m = a.shape
    n = b.shape[1]
    extra = [] if after is None else [after]

    def body(a_ref, b_ref, *rest):
        rest[-1][...] = _dot_tn(a_ref[...].astype(BF16), b_ref[...].astype(BF16)).astype(dtype)

    return pl.pallas_call(
        body, name=name,
        grid=(n // tn, m // tm),
        in_specs=[pl.BlockSpec((kk, tm), lambda j, i: (0, i)),
                  pl.BlockSpec((kk, tn), lambda j, i: (0, j))]
        + [pl.BlockSpec(memory_space=pl.ANY) for _ in extra],
        out_specs=pl.BlockSpec((tm, tn), lambda j, i: (i, j)),
        out_shape=jax.ShapeDtypeStruct((m, n), dtype),
        compiler_params=_cparams(("arbitrary", "arbitrary")),
    )(a, b, *extra)


def _adamw_update(part_refs, w_ref, m_ref, v_ref, g_ref, d_ref, nm_ref, nv_ref):
    g = part_refs[0][...].astype(F32)
    for p_ref in part_refs[1:]:
        g = g + p_ref[...].astype(F32)
    m_new = ADAM_B1 * m_ref[...] + (1.0 - ADAM_B1) * g
    v_new = ADAM_B2 * v_ref[...] + (1.0 - ADAM_B2) * (g * g)
    m_hat = m_new / (1.0 - ADAM_B1 ** ADAM_STEP)
    v_hat = v_new / (1.0 - ADAM_B2 ** ADAM_STEP)
    g_ref[...] = g
    nm_ref[...] = m_new
    nv_ref[...] = v_new
    d_ref[...] = -ADAM_LR * (m_hat / (jnp.sqrt(v_hat) + ADAM_EPS) + ADAM_WD * w_ref[...])


def _adamw_transposed(name, first, parts, w, m, v, tl, shards, sources):
    _, rows, cols = w.shape
    slots, padded = parts.shape[:2]
    ns = len(shards)
    g_shapes, g_sems = _gather_plan(shards, sources)
    steps = rows // tl

    def body(f_ref, p_ref, w_ref, m_ref, v_ref, *rest):
        start, finish = _gather_hooks(rest[:ns], rest[ns + 4:2 * ns + 4], rest[2 * ns + 4:], sources)
        pl.when(pl.program_id(0) == 0)(start)
        _adamw_update([f_ref.at[pl.ds(0, cols)]]
                      + [p_ref.at[q, pl.ds(0, cols)] for q in range(slots)],
                      w_ref, m_ref, v_ref, *rest[ns:ns + 4])
        pl.when(pl.program_id(0) == steps - 1)(finish)

    blk = pl.BlockSpec((cols, None, tl), lambda i: (0, 0, i))
    out = jax.ShapeDtypeStruct((cols, 1, rows), F32)
    res = pl.pallas_call(
        body, name=name,
        grid=(steps,),
        in_specs=[pl.BlockSpec((padded, tl), lambda i: (0, i)),
                  pl.BlockSpec((slots, padded, tl), lambda i: (0, 0, i)), blk, blk, blk]
        + _hbm_specs(ns),
        out_specs=(blk, blk, blk, blk) + tuple(_hbm_specs(ns)),
        out_shape=(out, out, out, out) + tuple(g_shapes),
        scratch_shapes=g_sems,
        compiler_params=_cparams(("arbitrary",)),
    )(first, parts, *[a.transpose(2, 0, 1) for a in (w, m, v)], *shards)
    return [r.transpose(1, 2, 0) for r in res[:4]], res[4:]


def _adamw_gains(gathered, ws, ms, vs):
    n = len(ws)

    def body(p_ref, *refs):
        ins, outs, loss_ref = refs[:3 * n], refs[3 * n:7 * n], refs[7 * n]
        row = 0
        for a in range(n):
            for j in range(ws[a].shape[1] // LANE):
                lanes = pl.ds(j * LANE, LANE)
                _adamw_update([p_ref.at[q, pl.ds(row, 1)] for q in range(N_DEV)],
                              *[r.at[:, lanes] for r in (ins[a], ins[n + a], ins[2 * n + a])],
                              *[r.at[:, lanes] for r in outs[4 * a:4 * a + 4]])
                row += 1
        loss = p_ref[0, pl.ds(row, 1), :]
        for q in range(1, N_DEV):
            loss = loss + p_ref[q, pl.ds(row, 1), :]
        loss_ref[...] = loss

    vmem = lambda k: [pl.BlockSpec(memory_space=pltpu.VMEM) for _ in range(k)]
    out_shape = []
    for w in ws:
        out_shape += [jax.ShapeDtypeStruct(w.shape, F32)] * 4
    out_shape.append(jax.ShapeDtypeStruct((1, LANE), F32))
    res = pl.pallas_call(
        body, name="adamw_gains",
        in_specs=vmem(1 + 3 * n), out_specs=tuple(vmem(4 * n + 1)), out_shape=tuple(out_shape),
        compiler_params=_cparams(),
    )(gathered, *ws, *ms, *vs)
    return [res[4 * a:4 * a + 4] for a in range(n)], res[-1]


def _adamw_group(firsts, parts, ws, ms, vs):
    n = len(ws)

    def body(*refs):
        ins, outs = refs[:5 * n], refs[5 * n:]
        x, y, c = _mesh_pos()
        for a in range(n):
            _adamw_update([ins[a].at[4 * x + 2 * y + c]]
                          + [ins[n + a].at[q] for q in range(ins[n + a].shape[0])],
                          *[r.at[0] for r in (ins[2 * n + a], ins[3 * n + a], ins[4 * n + a])],
                          *[r.at[0] for r in outs[4 * a:4 * a + 4]])

    vmem = lambda k: [pl.BlockSpec(memory_space=pltpu.VMEM) for _ in range(k)]
    out_shape = []
    for w in ws:
        out_shape += [jax.ShapeDtypeStruct(w.shape, F32)] * 4
    res = pl.pallas_call(
        body, name="adamw_small_weights",
        in_specs=vmem(5 * n), out_specs=tuple(vmem(4 * n)), out_shape=tuple(out_shape),
        compiler_params=_cparams(),
    )(*firsts, *parts, *ws, *ms, *vs)
    return [res[4 * a:4 * a + 4] for a in range(n)]


def _rope_tables(positions):
    half = MLA_ROPE // 2
    freqs = ROPE_THETA ** (-jnp.arange(half, dtype=F32) / half)
    ang = positions.astype(F32).reshape(-1, 1) * freqs
    cos, sin = jnp.cos(ang), jnp.sin(ang)
    t = ang.shape[0]
    one, zero = jnp.ones((t, MLA_NOPE), F32), jnp.zeros((t, half), F32)
    tail = jnp.zeros((t, LANE - MLA_QK), F32)
    rc = jnp.concatenate([one, cos, cos, tail], axis=1)
    rsn = jnp.concatenate([0.0 * one, -sin, zero, tail], axis=1)
    rsp = jnp.concatenate([0.0 * one, zero, sin, tail], axis=1)
    return rc, rsn, rsp


def _cols_full(g):
    return g.transpose(1, 0, 2)


def kernel(x, positions, g_in, w_in, g_q, w_uq, g_kv, w_ukv, w_gla_gate, b_gla_gate, g_gla, w_proj_mla, w_proj_gla, w_out, g_final, loss_target, m_g_in, m_w_in, m_g_q, m_w_uq, m_g_kv, m_w_ukv, m_w_gla_gate, m_b_gla_gate, m_g_gla, m_w_proj_mla, m_w_proj_gla, m_w_out, m_g_final, v_g_in, v_w_in, v_g_q, v_w_uq, v_g_kv, v_w_ukv, v_w_gla_gate, v_b_gla_gate, v_g_gla, v_w_proj_mla, v_w_proj_gla, v_w_out, v_g_final):
    t = x.shape[1]
    x2d = x.reshape(t, D_MODEL)
    tgt = loss_target.reshape(t, D_MODEL)
    g_final2 = g_final.reshape(1, D_MODEL)
    sharded = [(w_in, m_w_in, v_w_in), (w_uq, m_w_uq, v_w_uq), (w_ukv, m_w_ukv, v_w_ukv),
               (w_gla_gate, m_w_gla_gate, v_w_gla_gate), (w_proj_mla, m_w_proj_mla, v_w_proj_mla),
               (w_proj_gla, m_w_proj_gla, v_w_proj_gla), (w_out, m_w_out, v_w_out)]

    w_in_t = w_in.transpose(2, 0, 1).reshape(SHARD_COLS, D_MODEL)
    everyone = tuple(range(N_DEV))
    w_in_b = w_in_t.astype(BF16)
    b_uq, b_ukv, b_gate, b_pm, b_pg, b_o = [s[0][0].astype(BF16) for s in sharded[1:]]
    stages = ((0, 2, 4, 6), (1, 3, 5, 7))
    where = {d: (k, i) for k, srcs in enumerate(stages) for i, d in enumerate(srcs)}
    g_in_1, g_uq, g_ukv, g_gate = _all_gather(
        "all_gather_first", [w_in_b, b_uq, b_ukv, b_gate], [stages[0]] + [everyone] * 3)
    w_uq_p = jnp.pad(_cols_full(g_uq), ((0, 0), (0, 0), (0, HEAD_PAD - MLA_QK))).reshape(
        MLA_Q_RANK, MLA_HEADS * HEAD_PAD)
    ukv = _cols_full(g_ukv)
    w_k_p = jnp.pad(ukv[:, :, :MLA_NOPE], ((0, 0), (0, 0), (0, HEAD_PAD - MLA_NOPE))).reshape(
        MLA_KV_RANK, MLA_HEADS * HEAD_PAD)
    w_v = ukv[:, :, MLA_NOPE:].reshape(MLA_KV_RANK, MLA_WIDTH)
    w_gate_p = jnp.pad(_cols_full(g_gate).reshape(GLA_GATE_RANK, GLA_DK),
                       ((MISC_ALR, LANE - MISC_ALR - GLA_GATE_RANK), (0, 0)))
    rc, rsn, rsp = _rope_tables(positions)

    w_lat = _weights_to_p("weights_latents", [g_in_1], where, 2)
    proj_lat, h, rstd = _inproj(x2d, g_in, w_lat)
    q, k, v, log_a, pre, cqn, ckvn, rq, rkv, misc = _mla_prep(
        proj_lat, g_q, g_kv, w_uq_p, w_k_p, w_v, w_gate_p, b_gla_gate, rc, rsn, rsp)
    o_mla, lse, (g_in_2, g_pm, g_pg, g_o) = _mla_attn_fwd(
        q, k, v, [w_in_b, b_pm, b_pg, b_o], [stages[1]] + [everyone] * 3)
    w_gla = _weights_to_p("weights_gla", [g_in_1, g_in_2], where, 0)
    proj_gla = _proj("inproj_gla", h, w_gla)
    o_gla, states = _gla_fwd(proj_gla, log_a)
    w_out_path = _weights_to_p("weights_out_path", [g_in_1, g_in_2], where, 1)
    proj_out = _proj("inproj_out_path", h, w_out_path)
    w_in_p = (w_gla, w_out_path, w_lat)
    w_pm = _cols_full(g_pm).reshape(MLA_WIDTH, D_MODEL)
    w_pg = g_pg.reshape(GLA_DV, D_MODEL)
    w_o = g_o.reshape(D_MODEL, D_MODEL)

    (dx2, do_mla, do_gla, d_out, merged, um, ug, dym, dyg, loss_p, dg_final,
     dg_gla) = _post(o_mla, proj_out, o_gla, x2d, tgt, g_gla, g_final2, w_pm, w_pg, w_o)

    p_pm = _matmul("dw_proj_mla", um, dym, 512, D_MODEL, BF16).reshape(
        MLA_WIDTH, N_DEV, D_MODEL // N_DEV).transpose(1, 0, 2)
    p_pg = _matmul("dw_proj_gla", ug, dyg, 512, D_MODEL, BF16).reshape(N_DEV, -1, D_MODEL)
    p_o = _matmul("dw_out", merged, dx2, 512, D_MODEL, BF16).reshape(N_DEV, -1, D_MODEL)
    own_in = jnp.zeros((SHARD_PAD, D_MODEL), BF16)
    land_in = lax.empty((PEERS, SHARD_PAD, D_MODEL), BF16)
    dw_groups, started, lands = {}, [], [land_in]

    def reduce_scatter_stage(s, dests, own_in, extra=()):
        parts_in, own_in = _grads_to_shards("grads_to_shards_%d" % s, dw_groups, dests, own_in)
        first = len(lands)
        lands.extend(lax.empty((PEERS,) + p.shape[1:], BF16) for p in extra)
        idx = [0] + list(range(first, len(lands)))
        all_dests = [[(i, d, r0, r1) for i, (d, ranges) in enumerate(dests) for r0, r1 in ranges]]
        all_dests += [_whole(everyone, p.shape[1]) for p in extra]
        sems, parts, new_lands, token = _ici_start(
            "ici_start_%d" % s, [parts_in] + list(extra), [lands[i] for i in idx], all_dests)
        for a, i in enumerate(idx):
            lands[i] = new_lands[a]
            started.append((sems[a][0], sems[a][1], parts[a], i, all_dests[a]))
        return own_in, token

    def late_small_stage(arrays):
        idx = list(range(len(lands), len(lands) + len(arrays)))
        lands.extend(lax.empty((PEERS,) + p.shape[1:], BF16) for p in arrays)
        all_dests = [_whole(everyone, p.shape[1]) for p in arrays]
        sems, parts, new_lands, token = _ici_start(
            "ici_start_4", list(arrays), [lands[i] for i in idx], all_dests)
        for a, i in enumerate(idx):
            lands[i] = new_lands[a]
            started.append((sems[a][0], sems[a][1], parts[a], i, all_dests[a]))
        return token

    dw_groups[1] = _matmul("dw_in_1", d_out, h, 512, D_MODEL, BF16)
    full = [(0, SHARD_PAD)]
    own_in, token = reduce_scatter_stage(
        1, [(5, full), (6, full), (7, full), (0, [(672, SHARD_PAD)]), (1, [(0, 384)]),
            (4, [(96, SHARD_PAD)])], own_in, (p_pm, p_pg, p_o))
    d_gla, dla = _gla_bwd(proj_gla, log_a, do_gla, states, token)
    dw_groups[0] = _matmul("dw_in_0", d_gla, h, 512, D_MODEL, BF16)
    own_in, token = reduce_scatter_stage(
        2, [(1, [(384, SHARD_PAD)]), (2, full), (3, full), (4, [(0, 64)])], own_in)
    dq, dk, dv = _mla_attn_bwd(q, k, v, o_mla, do_mla, lse, token)
    d_lat, dqpre, dpre, dg_q, dg_kv, db_gate = _mla_prep_bwd(
        dq, dk, dv, dla, pre, proj_lat, rq, rkv, g_q, g_kv, w_uq_p, w_k_p, w_v, w_gate_p, rc, rsn, rsp)
    dw_groups[2] = _matmul("dw_in_2", d_lat, h, 896, D_MODEL, BF16)
    own_in, token = reduce_scatter_stage(3, [(0, [(0, 672)]), (4, [(64, 96)])], own_in)
    dw_uq = _matmul("dw_uq", cqn, dqpre, MLA_Q_RANK, D_MODEL, BF16, after=token)
    p_uq = dw_uq.reshape(MLA_Q_RANK, MLA_HEADS, HEAD_PAD)[:, :, :MLA_QK].transpose(1, 0, 2)
    dw_k = _matmul("dw_uk", ckvn, dk, MLA_KV_RANK, D_MODEL, BF16)
    dw_v = _matmul("dw_uv", ckvn, dv, MLA_KV_RANK, 512, BF16)
    p_ukv = jnp.concatenate(
        [dw_k.reshape(MLA_KV_RANK, MLA_HEADS, HEAD_PAD)[:, :, :MLA_NOPE],
         dw_v.reshape(MLA_KV_RANK, MLA_HEADS, MLA_VDIM)], axis=2).transpose(1, 0, 2)
    dw_gate = _matmul("dw_gate", misc, dpre, LANE, 512, BF16)
    p_gate = dw_gate[MISC_ALR:MISC_ALR + GLA_GATE_RANK].reshape(
        GLA_GATE_RANK, N_DEV, GLA_DK // N_DEV).transpose(1, 0, 2)
    token = late_small_stage((p_uq, p_ukv, p_gate))
    grad_x, dg_in = _inproj_bwd((d_gla, d_out, d_lat), w_in_p, x2d, rstd, g_in, dx2, token)
    small = jnp.concatenate([dg_in.reshape(-1), dg_q.reshape(-1), dg_kv.reshape(-1),
                             db_gate.reshape(-1), dg_gla.reshape(-1), dg_final.reshape(-1),
                             loss_p[0, :1]])
    small = jnp.pad(small, (0, SMALL_ROWS * LANE - small.shape[0])).reshape(SMALL_ROWS, LANE)

    lands = _ici_wait("ici_wait", started, lands, small)
    w_in_res, (small_all,) = _adamw_transposed(
        "adamw_w_in", own_in, lands[0], *sharded[0], 512, [small], [everyone])
    big = [w_in_res]
    big += _adamw_group([p_uq, p_ukv, p_gate, p_pm, p_pg, p_o], list(lands[4:7]) + list(lands[1:4]),
                        *[[s[j] for s in sharded[1:]] for j in range(3)])
    replicated = [(g_in, m_g_in, v_g_in), (g_q, m_g_q, v_g_q), (g_kv, m_g_kv, v_g_kv),
                  (b_gla_gate, m_b_gla_gate, v_b_gla_gate), (g_gla, m_g_gla, v_g_gla),
                  tuple(a.reshape(1, D_MODEL) for a in (g_final, m_g_final, v_g_final))]
    gains, loss_sum = _adamw_gains(small_all, *[[s[j] for s in replicated] for j in range(3)])

    outs = {}
    names = ("w_in", "w_uq", "w_ukv", "w_gla_gate", "w_proj_mla", "w_proj_gla", "w_out")
    small_names = ("g_in", "g_q", "g_kv", "b_gla_gate", "g_gla", "g_final")
    for j, kind in enumerate(("grad", "delta", "new_m", "new_v")):
        for name, res in zip(names + small_names, list(big) + gains):
            outs[kind, name] = res[j].reshape(-1) if name == "g_final" else res[j]
    loss = loss_sum[0, 0]
    order = ("g_in", "w_in", "g_q", "w_uq", "g_kv", "w_ukv", "w_gla_gate", "b_gla_gate", "g_gla",
             "w_proj_mla", "w_proj_gla", "w_out", "g_final")
    result = [loss, grad_x.reshape(1, t, D_MODEL)]
    for kind in ("grad", "delta", "new_m", "new_v"):
        result += [outs[kind, name] for name in order]
    return tuple(result)
```

```python
import jax
import jax.numpy as jnp
from jax import lax
from jax.experimental import pallas as pl
from jax.experimental.pallas import tpu as pltpu

F32 = jnp.float32
BF16 = jnp.bfloat16
MESH = pl.DeviceIdType.MESH
N_DEV = 8

D_MODEL = 1024
EPS = 1e-6
MLA_HEADS = 8
MLA_NOPE = 64
MLA_ROPE = 32
MLA_VDIM = 64
MLA_Q_RANK = 384
MLA_KV_RANK = 256
MLA_QK = MLA_NOPE + MLA_ROPE
MLA_WIDTH = MLA_HEADS * MLA_VDIM
ROPE_THETA = 10000.0
GLA_HEADS = 4
GLA_DK = 512
GLA_DV = 1024
GLA_HK = 128
GLA_HV = 256
GLA_GATE_RANK = 16
GLA_GATE_NORM = 16.0
GLA_CHUNK = 64
GLA_CHUNKS_PER_STEP = 8
GLA_CHUNKS_PER_STEP_BWD = 4
D_IN = 6320

ADAM_LR = 0.001
ADAM_B1 = 0.9
ADAM_B2 = 0.999
ADAM_EPS = 1e-08
ADAM_WD = 0.01
ADAM_STEP = 10

LANE = 128
HEAD_PAD = 128
VMEM_LIMIT = 48 * 1024 * 1024

P_VG, P_QG, P_KG = 0, 1024, 1536
P_ZGLA, P_GMLA, P_GGLA, P_ZMLA = 2048, 3072, 4096, 5120
P_CQ, P_CKV, P_MISC = 5632, 6144, 6400
P_TOTAL = 6528
P_GROUPS = ((0, 2048), (2048, 3584), (5632, 896))
MISC_KR = 64
MISC_ALR = 96
SHARD_COLS = D_IN // N_DEV
SHARD_PAD = 800
P_COMPONENTS = ((0, 384, P_CQ), (384, 256, P_CKV), (640, 32, P_MISC + MISC_KR), (672, 512, P_ZMLA),
                (1184, 512, P_QG), (1696, 512, P_KG), (2208, 1024, P_VG),
                (3232, 16, P_MISC + MISC_ALR), (3248, 1024, P_ZGLA), (4272, 1024, P_GMLA),
                (5296, 1024, P_GGLA))

SMALL_ROWS = 32


def _segments():
    segs = []
    for g0, n, p0 in P_COMPONENTS:
        g = g0
        while g < g0 + n:
            d = g // SHARD_COLS
            end = min(g0 + n, (d + 1) * SHARD_COLS)
            segs.append((d, g - d * SHARD_COLS, end - g, p0 + g - g0))
            g = end
    return segs


def _group_of(p0):
    return max(i for i, (off, _) in enumerate(P_GROUPS) if off <= p0)


def _rel(p0):
    return p0 - P_GROUPS[_group_of(p0)][0]


def _cparams(sem=None):
    if sem is None:
        return pltpu.CompilerParams(vmem_limit_bytes=VMEM_LIMIT)
    return pltpu.CompilerParams(dimension_semantics=sem, vmem_limit_bytes=VMEM_LIMIT)


def _sigmoid(v):
    return 1.0 / (1.0 + jnp.exp(-v))


def _dot(a, b):
    return jnp.dot(a, b, preferred_element_type=F32)


def _dot_nt(a, b):
    return lax.dot_general(a, b, (((1,), (1,)), ((), ())), preferred_element_type=F32)


def _dot_tn(a, b):
    return lax.dot_general(a, b, (((0,), (0,)), ((), ())), preferred_element_type=F32)


def _dot_exact(a, b):
    return jnp.dot(a, b, preferred_element_type=F32, precision=lax.Precision.HIGHEST)


def _rope_fwd(blk, c, sn, sp):
    return blk * c + pltpu.roll(blk, LANE - 16, 1) * sn + pltpu.roll(blk, 16, 1) * sp


def _rope_bwd(blk, c, sn, sp):
    return blk * c + pltpu.roll(blk * sn, 16, 1) + pltpu.roll(blk * sp, LANE - 16, 1)


def _mesh_pos():
    return lax.axis_index("x"), lax.axis_index("y"), lax.axis_index("c")


def _hbm_specs(n):
    return [pl.BlockSpec(memory_space=pltpu.HBM) for _ in range(n)]


def _dev(d):
    return d >> 2, (d >> 1) & 1, d & 1


def _gather_plan(shards, sources):
    na, most = len(shards), max(len(s) for s in sources)
    out_shape = [jax.ShapeDtypeStruct((len(srcs),) + s.shape, s.dtype)
                 for s, srcs in zip(shards, sources)]
    sems = [pltpu.SemaphoreType.DMA((na, most)) for _ in range(3)]
    sems += [pltpu.SemaphoreType.DMA((na, most, 3))]
    sems += [pltpu.SemaphoreType.DMA((na, most)) for _ in range(3)]
    return out_shape, sems


def _gather_hooks(x_refs, out_refs, sems, sources):
    local_sems, d2d_send, d2d_recv, ici_send, ici_recv, fwd_send, fwd_recv = sems
    x, y, c = _mesh_pos()
    chips = [(1 - x, y), (x, 1 - y), (1 - x, 1 - y)]
    items = []
    for a, srcs in enumerate(sources):
        for i, d in enumerate(srcs):
            dx, dy, dc = _dev(d)
            near = jnp.logical_and(x == dx, y == dy)
            far = jnp.logical_not(near)
            slot = out_refs[a].at[i]

            def remote(src, to, send_sem, recv_sem, slot=slot):
                return pltpu.make_async_remote_copy(
                    src_ref=src, dst_ref=slot, send_sem=send_sem, recv_sem=recv_sem,
                    device_id=to, device_id_type=MESH)

            items.append(dict(
                me=jnp.logical_and(near, c == dc), sibling=jnp.logical_and(near, c != dc),
                relay=jnp.logical_and(far, c == dc), behind=jnp.logical_and(far, c != dc),
                local=pltpu.make_async_copy(x_refs[a], slot, local_sems.at[a, i]),
                to_sibling=remote(x_refs[a], (x, y, 1 - c), d2d_send.at[a, i], d2d_recv.at[a, i]),
                to_chips=[remote(x_refs[a], (*chip, c), ici_send.at[a, i, j], ici_recv.at[a, i])
                          for j, chip in enumerate(chips)],
                forward=remote(slot, (x, y, 1 - c), fwd_send.at[a, i], fwd_recv.at[a, i])))

    def start():
        for it in items:
            @pl.when(it["me"])
            def _(it=it):
                it["local"].start()
                it["to_sibling"].start()
                for cp in it["to_chips"]:
                    cp.start()

    def finish():
        for it in items:
            @pl.when(it["relay"])
            def _(it=it):
                it["to_chips"][0].wait_recv()
                it["forward"].start()
        for it in items:
            pl.when(it["sibling"])(it["to_sibling"].wait_recv)
            pl.when(it["behind"])(it["forward"].wait_recv)
            pl.when(it["relay"])(it["forward"].wait_send)

            @pl.when(it["me"])
            def _(it=it):
                it["local"].wait()
                it["to_sibling"].wait_send()
                for cp in it["to_chips"]:
                    cp.wait_send()

    return start, finish


def _all_gather(name, shards, sources):
    n = len(shards)
    out_shape, sems = _gather_plan(shards, sources)

    def body(*refs):
        start, finish = _gather_hooks(refs[:n], refs[n:2 * n], refs[2 * n:], sources)
        start()
        finish()

    return pl.pallas_call(
        body, name=name,
        out_shape=tuple(out_shape),
        in_specs=_hbm_specs(n), out_specs=tuple(_hbm_specs(n)),
        scratch_shapes=sems,
        compiler_params=_cparams(),
    )(*shards)


PEERS = N_DEV - 1


def _whole(dests, rows):
    return [(i, d, 0, rows) for i, d in enumerate(dests)]


def _ici_copies(p_ref, land_ref, send_sems, recv_sems, pieces):
    x, y, c = _mesh_pos()
    sends, arrivals = [], []

    def rows_of(ref, j, r0, r1):
        return ref.at[j] if (r0, r1) == (0, ref.shape[1]) else ref.at[j, pl.ds(r0, r1 - r0)]

    for p, (i, d, r0, r1) in enumerate(pieces):
        dx, dy, dc = _dev(d)
        k = (4 * (x != dx).astype(jnp.int32) + 2 * (y != dy).astype(jnp.int32)
             + (c != dc).astype(jnp.int32))
        slot = jnp.maximum(k - 1, 0)
        sends.append((k > 0, pltpu.make_async_remote_copy(
            src_ref=rows_of(p_ref, i, r0, r1), dst_ref=rows_of(land_ref, slot, r0, r1),
            send_sem=send_sems.at[p], recv_sem=recv_sems.at[p * PEERS + slot],
            device_id=(dx, dy, dc), device_id_type=MESH)))
        arrivals.append((k == 0, [pltpu.make_async_remote_copy(
            src_ref=rows_of(p_ref, i, r0, r1), dst_ref=rows_of(land_ref, r, r0, r1),
            send_sem=send_sems.at[p], recv_sem=recv_sems.at[p * PEERS + r],
            device_id=(dx, dy, dc), device_id_type=MESH) for r in range(PEERS)]))
    return sends, arrivals


def _ici_start(name, hs, lands, dests):
    na = len(hs)

    def body(*refs):
        h_refs, land_refs, sems = refs[:na], refs[na:2 * na], refs[2 * na:4 * na]
        token = refs[-1]
        for a in range(na):
            sends, _ = _ici_copies(h_refs[a], land_refs[a], sems[2 * a], sems[2 * a + 1], dests[a])
            for go, cp in sends:
                pl.when(go)(cp.start)
        token[...] = jnp.zeros_like(token)

    hbm, sem = pl.BlockSpec(memory_space=pltpu.HBM), pl.BlockSpec(memory_space=pltpu.SEMAPHORE)
    sem_shapes = []
    for a in range(na):
        sem_shapes += [pltpu.SemaphoreType.DMA((len(dests[a]),)),
                       pltpu.SemaphoreType.DMA((len(dests[a]) * PEERS,))]
    res = pl.pallas_call(
        body, name=name,
        out_shape=tuple(sem_shapes) + tuple(pltpu.HBM(v.shape, v.dtype) for v in list(hs) + list(lands))
        + (jax.ShapeDtypeStruct((8, LANE), F32),),
        in_specs=(hbm,) * (2 * na),
        out_specs=(sem,) * (2 * na) + (hbm,) * (2 * na) + (pl.BlockSpec(memory_space=pltpu.VMEM),),
        input_output_aliases={i: 2 * na + i for i in range(2 * na)},
        compiler_params=pltpu.CompilerParams(
            has_side_effects=pltpu.SideEffectType.DATAFLOW_SIDE_EFFECTING,
            vmem_limit_bytes=VMEM_LIMIT),
    )(*[pltpu.with_memory_space_constraint(v, pltpu.HBM) for v in list(hs) + list(lands)])
    sems = [(res[2 * a], res[2 * a + 1]) for a in range(na)]
    return sems, res[2 * na:3 * na], res[3 * na:4 * na], res[-1]


def _ici_wait(name, started, lands, after):
    k, nl = len(started), len(lands)

    def body(*refs):
        land_refs = refs[3 * k:3 * k + nl]
        for s in range(k):
            h_ref, send_sems, recv_sems = refs[3 * s:3 * s + 3]
            sends, arrivals = _ici_copies(h_ref, land_refs[started[s][3]], send_sems, recv_sems,
                                          started[s][4])
            for go, cp in sends:
                pl.when(go)(cp.wait_send)
            for here, cps in arrivals:
                for cp in cps:
                    pl.when(here)(cp.wait_recv)

    hbm, sem = pl.BlockSpec(memory_space=pltpu.HBM), pl.BlockSpec(memory_space=pltpu.SEMAPHORE)
    operands, specs = [], []
    for send_sems, recv_sems, h, _, _ in started:
        operands += [h, send_sems, recv_sems]
        specs += [hbm, sem, sem]
    return pl.pallas_call(
        body, name=name,
        out_shape=tuple(pltpu.HBM(v.shape, v.dtype) for v in lands),
        in_specs=tuple(specs) + (hbm,) * nl + (pl.BlockSpec(memory_space=pl.ANY),),
        out_specs=(hbm,) * nl,
        input_output_aliases={3 * k + i: i for i in range(nl)},
        compiler_params=pltpu.CompilerParams(
            has_side_effects=pltpu.SideEffectType.DATAFLOW_SIDE_EFFECTING,
            vmem_limit_bytes=VMEM_LIMIT),
    )(*operands, *lands, after)


def _weights_to_p(name, gathered, where, group):
    tl = 512
    off, width = P_GROUPS[group]
    segs = sorted([s for s in _segments() if _group_of(s[3]) == group], key=lambda s: s[3])
    used = sorted({where[s[0]][0] for s in segs})

    def body(*refs):
        g_refs, o_ref = dict(zip(used, refs[:-1])), refs[-1]
        pieces, pos = [], off
        for d, c0, n, p0 in segs:
            if p0 > pos:
                pieces.append(jnp.zeros((p0 - pos, tl), F32))
            k, slot = where[d]
            pieces.append(g_refs[k][slot, c0:c0 + n, :].astype(F32))
            pos = p0 + n
        if off + width > pos:
            pieces.append(jnp.zeros((off + width - pos, tl), F32))
        o_ref[...] = jnp.concatenate(pieces, axis=0).astype(BF16)

    return pl.pallas_call(
        body, name=name,
        grid=(D_MODEL // tl,),
        in_specs=[pl.BlockSpec((gathered[k].shape[0], SHARD_COLS, tl), lambda i: (0, 0, i))
                  for k in used],
        out_specs=pl.BlockSpec((width, tl), lambda i: (0, i)),
        out_shape=jax.ShapeDtypeStruct((width, D_MODEL), BF16),
        compiler_params=_cparams(("arbitrary",)),
    )(*[gathered[k] for k in used])


def _grads_to_shards(name, groups, dests, own_prev):
    tl = 512
    segs = _segments()
    used = sorted(groups)

    def body(*refs):
        g_refs, prev_ref, o_ref, own_ref = dict(zip(used, refs[:-3])), refs[-3], refs[-2], refs[-1]
        x, y, c = _mesh_pos()
        own = prev_ref[...].astype(F32)
        row = lax.broadcasted_iota(jnp.int32, (SHARD_PAD, tl), 0)
        for i, (d, ranges) in enumerate(dests):
            pieces, pos, asked = [], 0, None
            for r0, r1 in sorted(ranges):
                if r0 > pos:
                    pieces.append(jnp.zeros((r0 - pos, tl), F32))
                for _, c0, n, p0 in sorted([s for s in segs if s[0] == d], key=lambda s: s[1]):
                    a, b = max(c0, r0), min(c0 + n, r1)
                    if a < b:
                        gi = _group_of(p0)
                        lo = p0 - P_GROUPS[gi][0] + a - c0
                        pieces.append(g_refs[gi][lo:lo + b - a, :].astype(F32))
                if r1 > SHARD_COLS:
                    pieces.append(jnp.zeros((r1 - max(r0, SHARD_COLS), tl), F32))
                pos = r1
                inside = jnp.logical_and(row >= r0, row < r1)
                asked = inside if asked is None else jnp.logical_or(asked, inside)
            if pos < SHARD_PAD:
                pieces.append(jnp.zeros((SHARD_PAD - pos, tl), F32))
            shard = jnp.concatenate(pieces, axis=0)
            o_ref[i] = shard.astype(BF16)
            own = jnp.where(jnp.logical_and(4 * x + 2 * y + c == d, asked), shard, own)
        own_ref[...] = own.astype(BF16)

    blk = pl.BlockSpec((SHARD_PAD, tl), lambda i: (0, i))
    return pl.pallas_call(
        body, name=name,
        grid=(D_MODEL // tl,),
        in_specs=[pl.BlockSpec((P_GROUPS[g][1], tl), lambda i: (0, i)) for g in used] + [blk],
        out_specs=(pl.BlockSpec((len(dests), SHARD_PAD, tl), lambda i: (0, 0, i)), blk),
        out_shape=(jax.ShapeDtypeStruct((len(dests), SHARD_PAD, D_MODEL), BF16),
                   jax.ShapeDtypeStruct((SHARD_PAD, D_MODEL), BF16)),
        input_output_aliases={len(used): 1},
        compiler_params=_cparams(("arbitrary",)),
    )(*[groups[g] for g in used], own_prev)


def _inproj(x, g_in, w_pt):
    t = x.shape[0]
    tm = min(512, t)
    width = w_pt.shape[0]

    def body(x_ref, g_ref, w_ref, proj_ref, h_ref, r_ref):
        xf = x_ref[...]
        r = lax.rsqrt(jnp.mean(xf * xf, axis=-1, keepdims=True) + EPS)
        h = ((xf * r) * g_ref[...]).astype(BF16)
        proj_ref[...] = _dot_nt(h, w_ref[...])
        h_ref[...] = h
        r_ref[...] = r

    row = lambda w: pl.BlockSpec((tm, w), lambda i: (i, 0))
    return pl.pallas_call(
        body, name="inproj_latents",
        grid=(t // tm,),
        in_specs=[row(D_MODEL), pl.BlockSpec((1, D_MODEL), lambda i: (0, 0)),
                  pl.BlockSpec((width, D_MODEL), lambda i: (0, 0))],
        out_specs=(row(width), row(D_MODEL), row(1)),
        out_shape=(jax.ShapeDtypeStruct((t, width), F32),
                   jax.ShapeDtypeStruct((t, D_MODEL), BF16),
                   jax.ShapeDtypeStruct((t, 1), F32)),
        compiler_params=_cparams(("arbitrary",)),
    )(x, g_in, w_pt)


def _proj(name, h, w_pt):
    t = h.shape[0]
    tm = min(512, t)
    width = w_pt.shape[0]

    def body(h_ref, w_ref, o_ref):
        o_ref[...] = _dot_nt(h_ref[...], w_ref[...])

    return pl.pallas_call(
        body, name=name,
        grid=(t // tm,),
        in_specs=[pl.BlockSpec((tm, D_MODEL), lambda i: (i, 0)),
                  pl.BlockSpec((width, D_MODEL), lambda i: (0, 0))],
        out_specs=pl.BlockSpec((tm, width), lambda i: (i, 0)),
        out_shape=jax.ShapeDtypeStruct((t, width), F32),
        compiler_params=_cparams(("arbitrary",)),
    )(h, w_pt)


def _mla_prep(proj, g_q, g_kv, w_uq_p, w_k_p, w_v, w_gate_p, b_gate, rc, rsn, rsp):
    t = proj.shape[0]
    tm = min(256, t)
    hq = MLA_HEADS * HEAD_PAD

    def body(cq_ref, ckv_ref, misc_ref, gq_ref, gkv_ref, wuq_ref, wk_ref, wv_ref, wg_ref, bg_ref,
             c_ref, sn_ref, sp_ref,
             q_ref, k_ref, v_ref, la_ref, pre_ref, cqn_ref, ckvn_ref, rq_ref, rkv_ref, mb_ref):
        c, sn, sp = c_ref[...], sn_ref[...], sp_ref[...]
        cq = cq_ref[:, :MLA_Q_RANK]
        rq = lax.rsqrt(jnp.mean(cq * cq, axis=-1, keepdims=True) + EPS)
        cqn = ((cq * rq) * gq_ref[...]).astype(BF16)
        cqn_ref[...] = cqn
        rq_ref[...] = rq
        qpre = _dot(cqn, wuq_ref[...])
        ckv = ckv_ref[...]
        rkv = lax.rsqrt(jnp.mean(ckv * ckv, axis=-1, keepdims=True) + EPS)
        ckvn = ((ckv * rkv) * gkv_ref[...]).astype(BF16)
        ckvn_ref[...] = ckvn
        rkv_ref[...] = rkv
        kn = _dot(ckvn, wk_ref[...])
        v_ref[...] = _dot(ckvn, wv_ref[...]).astype(BF16)
        misc = misc_ref[...]
        krope = _rope_fwd(misc, c, sn, sp)
        for h in range(MLA_HEADS):
            sl = slice(h * HEAD_PAD, (h + 1) * HEAD_PAD)
            q_ref[:, sl] = _rope_fwd(qpre[:, sl], c, sn, sp).astype(BF16)
            k_ref[:, sl] = (kn[:, sl] + krope).astype(BF16)
        mb_ref[...] = misc.astype(BF16)
        pre = _dot(mb_ref[...], wg_ref[...]) + bg_ref[...]
        pre_ref[...] = pre
        log_a = (jnp.minimum(pre, 0.0) - jnp.log(1.0 + jnp.exp(-jnp.abs(pre)))) / GLA_GATE_NORM
        la_ref[...] = _dot_exact(_chunk_tri(tm, True), log_a)

    row = lambda w: pl.BlockSpec((tm, w), lambda i: (i, 0))
    full = lambda a: pl.BlockSpec(a.shape, lambda i: (0, 0))
    return pl.pallas_call(
        body, name="mla_prep",
        grid=(t // tm,),
        in_specs=[pl.BlockSpec((tm, 512), lambda i: (i, _rel(P_CQ) // 512)),
                  pl.BlockSpec((tm, MLA_KV_RANK), lambda i: (i, _rel(P_CKV) // MLA_KV_RANK)),
                  pl.BlockSpec((tm, LANE), lambda i: (i, _rel(P_MISC) // LANE)),
                  full(g_q), full(g_kv), full(w_uq_p), full(w_k_p), full(w_v), full(w_gate_p),
                  full(b_gate), row(LANE), row(LANE), row(LANE)],
        out_specs=(row(hq), row(hq), row(MLA_WIDTH), row(GLA_DK), row(GLA_DK),
                   row(MLA_Q_RANK), row(MLA_KV_RANK), row(1), row(1), row(LANE)),
        out_shape=(jax.ShapeDtypeStruct((t, hq), BF16), jax.ShapeDtypeStruct((t, hq), BF16),
                   jax.ShapeDtypeStruct((t, MLA_WIDTH), BF16),
                   jax.ShapeDtypeStruct((t, GLA_DK), F32), jax.ShapeDtypeStruct((t, GLA_DK), F32),
                   jax.ShapeDtypeStruct((t, MLA_Q_RANK), BF16),
                   jax.ShapeDtypeStruct((t, MLA_KV_RANK), BF16),
                   jax.ShapeDtypeStruct((t, 1), F32), jax.ShapeDtypeStruct((t, 1), F32),
                   jax.ShapeDtypeStruct((t, LANE), BF16)),
        compiler_params=_cparams(("arbitrary",)),
    )(proj, proj, proj, g_q, g_kv, w_uq_p, w_k_p, w_v, w_gate_p, b_gate, rc, rsn, rsp)


def _attn_masks(tq, i):
    keys = (i + 1) * tq
    rows = i * tq + lax.broadcasted_iota(jnp.int32, (tq, keys), 0)
    cols = lax.broadcasted_iota(jnp.int32, (tq, keys), 1)
    lane = lax.broadcasted_iota(jnp.int32, (tq, LANE), 1)
    return cols <= rows, lane < MLA_VDIM


def _for_each_query_tile(n_tiles, fn):
    for i in range(n_tiles):
        pl.when(pl.program_id(1) == i)(lambda i=i: fn(i))


def _mla_attn_fwd(q, k, v, shards, sources):
    t = q.shape[0]
    tq = min(256, t)
    scale = MLA_QK ** -0.5
    ns = len(shards)
    g_shapes, g_sems = _gather_plan(shards, sources)
    grid = (MLA_HEADS // 2, t // tq)

    def body(q_ref, k_ref, v_ref, *rest):
        o_ref, lse_ref = rest[ns:ns + 2]
        start, finish = _gather_hooks(rest[:ns], rest[ns + 2:2 * ns + 2], rest[2 * ns + 2:], sources)
        step = pl.program_id(0) * grid[1] + pl.program_id(1)
        pl.when(step == 0)(start)

        def tile(i):
            keys = (i + 1) * tq
            causal, low = _attn_masks(tq, i)
            vp = v_ref[0:keys, :]
            acc = jnp.zeros((tq, LANE), F32)
            for hh in range(2):
                sl = slice(hh * HEAD_PAD, (hh + 1) * HEAD_PAD)
                s = _dot_nt(q_ref[:, sl], k_ref[0:keys, sl]) * scale
                s = jnp.where(causal, s, -jnp.inf)
                m = jnp.max(s, axis=-1, keepdims=True)
                e = jnp.exp(s - m)
                l = jnp.sum(e, axis=-1, keepdims=True)
                o = _dot(e.astype(BF16), vp) / l
                acc = jnp.where(low if hh == 0 else jnp.logical_not(low), o, acc)
                lse_ref[hh] = m + jnp.log(l)
            o_ref[...] = acc

        _for_each_query_tile(t // tq, tile)
        pl.when(step == grid[0] * grid[1] - 1)(finish)

    res = pl.pallas_call(
        body, name="mla_attn_fwd",
        grid=grid,
        in_specs=[pl.BlockSpec((tq, 2 * HEAD_PAD), lambda p, i: (i, p)),
                  pl.BlockSpec((t, 2 * HEAD_PAD), lambda p, i: (0, p)),
                  pl.BlockSpec((t, LANE), lambda p, i: (0, p))] + _hbm_specs(ns),
        out_specs=(pl.BlockSpec((tq, LANE), lambda p, i: (i, p)),
                   pl.BlockSpec((2, tq, 1), lambda p, i: (p, i, 0))) + tuple(_hbm_specs(ns)),
        out_shape=(jax.ShapeDtypeStruct((t, MLA_WIDTH), F32),
                   jax.ShapeDtypeStruct((MLA_HEADS, t, 1), F32)) + tuple(g_shapes),
        scratch_shapes=g_sems,
        compiler_params=_cparams(("arbitrary", "arbitrary")),
    )(q, k, v, *shards)
    return res[0], res[1], res[2:]


def _mla_attn_bwd(q, k, v, o, do, lse, after):
    t = q.shape[0]
    tq = min(256, t)
    scale = MLA_QK ** -0.5

    def body(q_ref, k_ref, v_ref, o_ref, do_ref, lse_ref, after_ref, dq_ref, dk_ref, dv_ref):
        del after_ref

        @pl.when(pl.program_id(1) == 0)
        def _():
            dk_ref[...] = jnp.zeros_like(dk_ref)
            dv_ref[...] = jnp.zeros_like(dv_ref)

        def tile(i):
            keys = (i + 1) * tq
            causal, low = _attn_masks(tq, i)
            vp = v_ref[0:keys, :]
            do_all = do_ref[...]
            o_all = o_ref[...]
            dv_acc = jnp.zeros((keys, LANE), F32)
            for hh in range(2):
                sl = slice(hh * HEAD_PAD, (hh + 1) * HEAD_PAD)
                do_h = jnp.where(low if hh == 0 else jnp.logical_not(low), do_all, 0.0)
                dsum = jnp.sum(do_h * o_all, axis=-1, keepdims=True)
                qh = q_ref[:, sl]
                kh = k_ref[0:keys, sl]
                s = _dot_nt(qh, kh) * scale
                p = jnp.where(causal, jnp.exp(s - lse_ref[hh]), 0.0)
                do_b = do_h.astype(BF16)
                dp = _dot_nt(do_b, vp)
                ds = (p * (dp - dsum) * scale).astype(BF16)
                dq_ref[:, sl] = _dot(ds, kh).astype(BF16)
                dk_ref[0:keys, sl] += _dot_tn(ds, qh)
                dv_acc = dv_acc + _dot_tn(p.astype(BF16), do_b)
            dv_ref[0:keys, :] += dv_acc

        _for_each_query_tile(t // tq, tile)

    return pl.pallas_call(
        body, name="mla_attn_bwd",
        grid=(MLA_HEADS // 2, t // tq),
        in_specs=[pl.BlockSpec((tq, 2 * HEAD_PAD), lambda p, i: (i, p)),
                  pl.BlockSpec((t, 2 * HEAD_PAD), lambda p, i: (0, p)),
                  pl.BlockSpec((t, LANE), lambda p, i: (0, p)),
                  pl.BlockSpec((tq, LANE), lambda p, i: (i, p)),
                  pl.BlockSpec((tq, LANE), lambda p, i: (i, p)),
                  pl.BlockSpec((2, tq, 1), lambda p, i: (p, i, 0)),
                  pl.BlockSpec(memory_space=pl.ANY)],
        out_specs=(pl.BlockSpec((tq, 2 * HEAD_PAD), lambda p, i: (i, p)),
                   pl.BlockSpec((t, 2 * HEAD_PAD), lambda p, i: (0, p)),
                   pl.BlockSpec((t, LANE), lambda p, i: (0, p))),
        out_shape=(jax.ShapeDtypeStruct((t, MLA_HEADS * HEAD_PAD), BF16),
                   jax.ShapeDtypeStruct((t, MLA_HEADS * HEAD_PAD), F32),
                   jax.ShapeDtypeStruct((t, MLA_WIDTH), F32)),
        compiler_params=_cparams(("arbitrary", "arbitrary")),
    )(q, k, v, o, do, lse, after)


def _chunk_tri(n, lower):
    r = lax.broadcasted_iota(jnp.int32, (n, n), 0)
    c = lax.broadcasted_iota(jnp.int32, (n, n), 1)
    same = (r // GLA_CHUNK) == (c // GLA_CHUNK)
    return jnp.where(jnp.logical_and(same, r >= c if lower else r <= c), 1.0, 0.0).astype(F32)


def _gla_chunk_terms(q_ref, k_ref, b_ref, h, rows):
    sl = slice(h * GLA_HK, (h + 1) * GLA_HK)
    b = b_ref[rows, sl]
    bl = b[GLA_CHUNK - 1:GLA_CHUNK, :]
    kc = k_ref[rows, sl]
    q_in = (q_ref[rows, sl] * (GLA_HK ** -0.5)) * jnp.exp(b)
    k_in = kc * jnp.exp(-b)
    k_st = kc * jnp.exp(bl - b)
    return b, bl, q_in, k_in, k_st


def _tri(c, lower):
    r = lax.broadcasted_iota(jnp.int32, (c, c), 0)
    cc = lax.broadcasted_iota(jnp.int32, (c, c), 1)
    return jnp.where(r >= cc if lower else r <= cc, 1.0, 0.0).astype(F32)


def _gla_fwd(proj, log_a):
    t = proj.shape[0]
    per = GLA_CHUNKS_PER_STEP
    n = t // GLA_CHUNK
    c = GLA_CHUNK * per

    def body(q_ref, k_ref, v_ref, la_ref, o_ref, sp_ref, st_ref):
        @pl.when(pl.program_id(0) == 0)
        def _():
            st_ref[...] = jnp.zeros_like(st_ref)

        tri = _tri(GLA_CHUNK, True)
        for s, h in [(s, h) for s in range(per) for h in range(GLA_HEADS)]:
            rows = slice(s * GLA_CHUNK, (s + 1) * GLA_CHUNK)
            _, bl, q_in, k_in, k_st = _gla_chunk_terms(q_ref, k_ref, la_ref, h, rows)
            vs = slice(h * GLA_HV, (h + 1) * GLA_HV)
            vv = v_ref[rows, vs].astype(BF16)
            qb = q_in.astype(BF16)
            attn = _dot_nt(qb, k_in.astype(BF16)) * tri
            st = st_ref[h]
            sp_ref[s, h] = st
            o_ref[rows, vs] = _dot(attn.astype(BF16), vv) + _dot_nt(qb, st.astype(BF16))
            st_ref[h] = st * jnp.exp(bl) + _dot_tn(vv, k_st.astype(BF16))

    return pl.pallas_call(
        body, name="gla_fwd",
        grid=(n // per,),
        in_specs=[pl.BlockSpec((c, GLA_DK), lambda i: (i, P_QG // GLA_DK)),
                  pl.BlockSpec((c, GLA_DK), lambda i: (i, P_KG // GLA_DK)),
                  pl.BlockSpec((c, GLA_DV), lambda i: (i, P_VG // GLA_DV)),
                  pl.BlockSpec((c, GLA_DK), lambda i: (i, 0))],
        out_specs=(pl.BlockSpec((c, GLA_DV), lambda i: (i, 0)),
                   pl.BlockSpec((per, GLA_HEADS, GLA_HV, GLA_HK), lambda i: (i, 0, 0, 0))),
        out_shape=(jax.ShapeDtypeStruct((t, GLA_DV), F32),
                   jax.ShapeDtypeStruct((n, GLA_HEADS, GLA_HV, GLA_HK), F32)),
        scratch_shapes=[pltpu.VMEM((GLA_HEADS, GLA_HV, GLA_HK), F32)],
        compiler_params=_cparams(("arbitrary",)),
    )(proj, proj, proj, log_a)


def _gla_bwd(proj, log_a, do, states, after):
    t = proj.shape[0]
    per = GLA_CHUNKS_PER_STEP_BWD
    c = GLA_CHUNK * per
    n = t // c

    def body(q_ref, k_ref, v_ref, la_ref, do_ref, sp_ref, after_ref, dg_ref, dla_ref, ds_ref):
        del after_ref

        @pl.when(pl.program_id(0) == 0)
        def _():
            ds_ref[...] = jnp.zeros_like(ds_ref)

        tri = _tri(GLA_CHUNK, True)
        last = lax.broadcasted_iota(jnp.int32, (GLA_CHUNK, GLA_HK), 0) == GLA_CHUNK - 1
        for s, h in [(s, h) for s in reversed(range(per)) for h in range(GLA_HEADS)]:
            rows = slice(s * GLA_CHUNK, (s + 1) * GLA_CHUNK)
            b, bl, q_in, k_in, k_st = _gla_chunk_terms(q_ref, k_ref, la_ref, h, rows)
            ks_ = slice(h * GLA_HK, (h + 1) * GLA_HK)
            vs = slice(h * GLA_HV, (h + 1) * GLA_HV)
            vv = v_ref[rows, vs].astype(BF16)
            do_h = do_ref[rows, vs]
            qb, kb, ksb = q_in.astype(BF16), k_in.astype(BF16), k_st.astype(BF16)
            attn = (_dot_nt(qb, kb) * tri).astype(BF16)
            st = sp_ref[s, h]
            dst = ds_ref[h]
            dstb = dst.astype(BF16)
            dattn = (_dot_nt(do_h, vv) * tri).astype(BF16)
            dg_ref[rows, P_VG + h * GLA_HV:P_VG + (h + 1) * GLA_HV] = (
                _dot_tn(attn, do_h) + _dot_nt(ksb, dstb)).astype(BF16)
            dq_in = _dot(dattn, kb) + _dot(do_h, st.astype(BF16))
            dk_in = _dot_tn(dattn, qb)
            dk_st = _dot(vv, dstb)
            ebl = jnp.exp(bl)
            d_ebl = jnp.sum(st * dst, axis=0, keepdims=True)
            ds_ref[h] = _dot_tn(do_h, qb) + dst * ebl
            dg_ref[rows, P_QG + h * GLA_HK:P_QG + (h + 1) * GLA_HK] = (
                dq_in * (GLA_HK ** -0.5) * jnp.exp(b)).astype(BF16)
            dg_ref[rows, P_KG + h * GLA_HK:P_KG + (h + 1) * GLA_HK] = (
                dk_in * jnp.exp(-b) + dk_st * jnp.exp(bl - b)).astype(BF16)
            db = dq_in * q_in - dk_in * k_in - dk_st * k_st
            dbl = jnp.sum(dk_st * k_st, axis=0, keepdims=True) + d_ebl * ebl
            dla_ref[rows, ks_] = db + jnp.where(last, dbl, 0.0)

    rev = lambda i: n - 1 - i
    gw = P_GROUPS[0][1]
    return pl.pallas_call(
        body, name="gla_bwd",
        grid=(n,),
        in_specs=[pl.BlockSpec((c, GLA_DK), lambda i: (rev(i), P_QG // GLA_DK)),
                  pl.BlockSpec((c, GLA_DK), lambda i: (rev(i), P_KG // GLA_DK)),
                  pl.BlockSpec((c, GLA_DV), lambda i: (rev(i), P_VG // GLA_DV)),
                  pl.BlockSpec((c, GLA_DK), lambda i: (rev(i), 0)),
                  pl.BlockSpec((c, GLA_DV), lambda i: (rev(i), 0)),
                  pl.BlockSpec((per, GLA_HEADS, GLA_HV, GLA_HK), lambda i: (rev(i), 0, 0, 0)),
                  pl.BlockSpec(memory_space=pl.ANY)],
        out_specs=(pl.BlockSpec((c, gw), lambda i: (rev(i), 0)),
                   pl.BlockSpec((c, GLA_DK), lambda i: (rev(i), 0))),
        out_shape=(jax.ShapeDtypeStruct((t, gw), BF16), jax.ShapeDtypeStruct((t, GLA_DK), F32)),
        scratch_shapes=[pltpu.VMEM((GLA_HEADS, GLA_HV, GLA_HK), F32)],
        compiler_params=_cparams(("arbitrary",)),
    )(proj, proj, proj, log_a, do, states, after)


def _post(o_mla, proj, o_gla, x, target, g_gla, g_final, w_pm, w_pg, w_o):
    t = x.shape[0]
    tm = min(256, t)
    g0, gw = P_GROUPS[1]

    def body(om_ref, zg_ref, gm_ref, gg_ref, zm_ref, og_ref, x_ref, tg_ref, ggla_ref, gf_ref,
             wpm_ref, wpg_ref, wo_ref,
             dx2_ref, dom_ref, dog_ref, dg_ref,
             mg_ref, um_ref, ug_ref, dym_ref, dyg_ref, loss_ref, dgf_ref, dggla_ref):
        @pl.when(pl.program_id(0) == 0)
        def _():
            loss_ref[...] = jnp.zeros_like(loss_ref)
            dgf_ref[...] = jnp.zeros_like(dgf_ref)
            dggla_ref[...] = jnp.zeros_like(dggla_ref)

        om = om_ref[...]
        zm = zm_ref[...]
        sm = _sigmoid(zm)
        silu_m = zm * sm
        um = (om * silu_m).astype(BF16)
        um_ref[...] = um
        ym = _dot(um, wpm_ref[...])

        ggla = ggla_ref[...]
        zg = zg_ref[...]
        sg = _sigmoid(zg)
        silu_g = zg * sg
        xhat, rstd, on = [], [], []
        for h in range(GLA_HEADS):
            blk = og_ref[:, h * GLA_HV:(h + 1) * GLA_HV]
            r = lax.rsqrt(jnp.mean(blk * blk, axis=-1, keepdims=True) + EPS)
            xhat.append(blk * r)
            rstd.append(r)
            on.append(xhat[h] * ggla)
        on = jnp.concatenate(on, axis=-1)
        ug = (on * silu_g).astype(BF16)
        ug_ref[...] = ug
        yg = _dot(ug, wpg_ref[...])

        sgm = _sigmoid(gm_ref[...])
        sgg = _sigmoid(gg_ref[...])
        merged = (sgm * ym + sgg * yg).astype(BF16)
        mg_ref[...] = merged
        x2 = x_ref[...] + _dot(merged, wo_ref[...])
        gf = gf_ref[...]
        rf = lax.rsqrt(jnp.mean(x2 * x2, axis=-1, keepdims=True) + EPS)
        xh = x2 * rf
        err = xh * gf - tg_ref[...]
        loss_ref[...] += 0.5 * jnp.sum(jnp.mean(err * err, axis=-1, keepdims=True))

        dy = err * (1.0 / D_MODEL)
        dgf_ref[...] += jnp.sum(dy * xh, axis=0, keepdims=True)
        dxh = dy * gf
        dx2 = rf * (dxh - xh * jnp.mean(dxh * xh, axis=-1, keepdims=True))
        dx2_ref[...] = dx2
        dmerged = _dot_nt(dx2.astype(BF16), wo_ref[...])
        dym = (dmerged * sgm).astype(BF16)
        dyg = (dmerged * sgg).astype(BF16)
        dym_ref[...] = dym
        dyg_ref[...] = dyg
        dg_ref[:, P_GMLA - g0:P_GMLA - g0 + D_MODEL] = (dmerged * ym * sgm * (1.0 - sgm)).astype(BF16)
        dg_ref[:, P_GGLA - g0:P_GGLA - g0 + D_MODEL] = (dmerged * yg * sgg * (1.0 - sgg)).astype(BF16)
        dum = _dot_nt(dym, wpm_ref[...])
        dom_ref[...] = dum * silu_m
        dg_ref[:, P_ZMLA - g0:P_ZMLA - g0 + MLA_WIDTH] = (
            dum * om * (sm * (1.0 + zm * (1.0 - sm)))).astype(BF16)
        dug = _dot_nt(dyg, wpg_ref[...])
        dg_ref[:, P_ZGLA - g0:P_ZGLA - g0 + GLA_DV] = (
            dug * on * (sg * (1.0 + zg * (1.0 - sg)))).astype(BF16)
        don = dug * silu_g
        dggla = jnp.zeros((1, GLA_HV), F32)
        for h in range(GLA_HEADS):
            hs = slice(h * GLA_HV, (h + 1) * GLA_HV)
            don_h = don[:, hs]
            dggla = dggla + jnp.sum(don_h * xhat[h], axis=0, keepdims=True)
            dxh_h = don_h * ggla
            dog_ref[:, hs] = (rstd[h] * (dxh_h - xhat[h] * jnp.mean(dxh_h * xhat[h], axis=-1,
                                                                     keepdims=True))).astype(BF16)
        dggla_ref[...] += dggla

    row = lambda w: pl.BlockSpec((tm, w), lambda i: (i, 0))
    pcol = lambda w, off: pl.BlockSpec((tm, w), lambda i: (i, _rel(off) // w))
    full = lambda a: pl.BlockSpec(a.shape, lambda i: (0, 0))
    sds = jax.ShapeDtypeStruct
    return pl.pallas_call(
        body, name="post_fwd_bwd",
        grid=(t // tm,),
        in_specs=[row(MLA_WIDTH), pcol(GLA_DV, P_ZGLA), pcol(D_MODEL, P_GMLA), pcol(D_MODEL, P_GGLA),
                  pcol(MLA_WIDTH, P_ZMLA), row(GLA_DV), row(D_MODEL), row(D_MODEL),
                  full(g_gla), full(g_final), full(w_pm), full(w_pg), full(w_o)],
        out_specs=(row(D_MODEL), row(MLA_WIDTH), row(GLA_DV), row(gw),
                   row(D_MODEL), row(MLA_WIDTH), row(GLA_DV), row(D_MODEL), row(D_MODEL),
                   pl.BlockSpec((1, LANE), lambda i: (0, 0)),
                   pl.BlockSpec((1, D_MODEL), lambda i: (0, 0)),
                   pl.BlockSpec((1, GLA_HV), lambda i: (0, 0))),
        out_shape=(sds((t, D_MODEL), F32), sds((t, MLA_WIDTH), F32), sds((t, GLA_DV), BF16),
                   sds((t, gw), BF16),
                   sds((t, D_MODEL), BF16), sds((t, MLA_WIDTH), BF16), sds((t, GLA_DV), BF16),
                   sds((t, D_MODEL), BF16), sds((t, D_MODEL), BF16),
                   sds((1, LANE), F32), sds((1, D_MODEL), F32), sds((1, GLA_HV), F32)),
        compiler_params=_cparams(("arbitrary",)),
    )(o_mla, proj, proj, proj, proj, o_gla, x, target, g_gla, g_final, w_pm, w_pg, w_o)


def _mla_prep_bwd(dq, dk, dv, dla, pre, proj, rq, rkv, g_q, g_kv, w_uq_p, w_k_p, w_v, w_gate_p,
                  rc, rsn, rsp):
    t = proj.shape[0]
    tm = min(256, t)
    gw = P_GROUPS[2][1]

    def body(dq_ref, dk_ref, dv_ref, dla_ref, pre_ref, cq_ref, ckv_ref, rq_ref, rkv_ref,
             gq_ref, gkv_ref, wuq_ref, wk_ref, wv_ref, wg_ref, c_ref, sn_ref, sp_ref,
             dg_ref, dqpre_ref, dpre_ref, dgq_ref, dgkv_ref, dbg_ref):
        @pl.when(pl.program_id(0) == 0)
        def _():
            dgq_ref[...] = jnp.zeros_like(dgq_ref)
            dgkv_ref[...] = jnp.zeros_like(dgkv_ref)
            dbg_ref[...] = jnp.zeros_like(dbg_ref)

        c, sn, sp = c_ref[...], sn_ref[...], sp_ref[...]
        dkr = jnp.zeros((tm, LANE), F32)
        for h in range(MLA_HEADS):
            sl = slice(h * HEAD_PAD, (h + 1) * HEAD_PAD)
            dqpre_ref[:, sl] = _rope_bwd(dq_ref[:, sl].astype(F32), c, sn, sp).astype(BF16)
            dkr = dkr + dk_ref[:, sl]
        dcqn = _dot_nt(dqpre_ref[...], wuq_ref[...])
        rq = rq_ref[...]
        xh = cq_ref[:, :MLA_Q_RANK] * rq
        dgq_ref[...] += jnp.sum(dcqn * xh, axis=0, keepdims=True)
        dxh = dcqn * gq_ref[...]
        dcq = rq * (dxh - xh * jnp.mean(dxh * xh, axis=-1, keepdims=True))
        dg_ref[:, :MLA_Q_RANK] = dcq.astype(BF16)
        dg_ref[:, MLA_Q_RANK:512] = jnp.zeros((tm, 512 - MLA_Q_RANK), BF16)

        dckvn = _dot_nt(dk_ref[...].astype(BF16), wk_ref[...]) + \
            _dot_nt(dv_ref[...].astype(BF16), wv_ref[...])
        rkv = rkv_ref[...]
        xh = ckv_ref[...] * rkv
        dgkv_ref[...] += jnp.sum(dckvn * xh, axis=0, keepdims=True)
        dxh = dckvn * gkv_ref[...]
        dg_ref[:, P_CKV - P_CQ:P_CKV - P_CQ + MLA_KV_RANK] = (
            rkv * (dxh - xh * jnp.mean(dxh * xh, axis=-1, keepdims=True))).astype(BF16)

        dlog_a = _dot_exact(_chunk_tri(tm, False), dla_ref[...])
        dpre = dlog_a * (1.0 / GLA_GATE_NORM) * (1.0 - _sigmoid(pre_ref[...]))
        dbg_ref[...] += jnp.sum(dpre, axis=0, keepdims=True)
        dpre = dpre.astype(BF16)
        dpre_ref[...] = dpre
        lane = lax.broadcasted_iota(jnp.int32, (tm, LANE), 1)
        in_kr = jnp.logical_and(lane >= MISC_KR, lane < MISC_KR + MLA_ROPE)
        dmisc = jnp.where(in_kr, _rope_bwd(dkr, c, sn, sp), 0.0) + _dot_nt(dpre, wg_ref[...])
        dg_ref[:, P_MISC - P_CQ:P_MISC - P_CQ + LANE] = dmisc.astype(BF16)

    hq = MLA_HEADS * HEAD_PAD
    row = lambda w: pl.BlockSpec((tm, w), lambda i: (i, 0))
    full = lambda a: pl.BlockSpec(a.shape, lambda i: (0, 0))
    acc = lambda w: pl.BlockSpec((1, w), lambda i: (0, 0))
    sds = jax.ShapeDtypeStruct
    return pl.pallas_call(
        body, name="mla_prep_bwd",
        grid=(t // tm,),
        in_specs=[row(hq), row(hq), row(MLA_WIDTH), row(GLA_DK), row(GLA_DK),
                  pl.BlockSpec((tm, 512), lambda i: (i, _rel(P_CQ) // 512)),
                  pl.BlockSpec((tm, MLA_KV_RANK), lambda i: (i, _rel(P_CKV) // MLA_KV_RANK)),
                  row(1), row(1), full(g_q), full(g_kv), full(w_uq_p), full(w_k_p), full(w_v),
                  full(w_gate_p), row(LANE), row(LANE), row(LANE)],
        out_specs=(row(gw), row(hq), row(GLA_DK),
                   acc(MLA_Q_RANK), acc(MLA_KV_RANK), acc(GLA_DK)),
        out_shape=(sds((t, gw), BF16), sds((t, hq), BF16), sds((t, GLA_DK), BF16),
                   sds((1, MLA_Q_RANK), F32), sds((1, MLA_KV_RANK), F32), sds((1, GLA_DK), F32)),
        compiler_params=_cparams(("arbitrary",)),
    )(dq, dk, dv, dla, pre, proj, proj, rq, rkv, g_q, g_kv, w_uq_p, w_k_p, w_v, w_gate_p,
      rc, rsn, rsp)


def _inproj_bwd(dgroups, w_pts, x, rstd, g_in, dx2, after):
    t = x.shape[0]
    tm = min(256, t)

    def body(d0_ref, d1_ref, d2_ref, w0_ref, w1_ref, w2_ref, x_ref, r_ref, g_ref, dx2_ref, after_ref,
             dx_ref, dg_ref):
        del after_ref

        @pl.when(pl.program_id(0) == 0)
        def _():
            dg_ref[...] = jnp.zeros_like(dg_ref)

        dh = jnp.zeros((tm, D_MODEL), F32)
        for d_ref, w_ref in zip((d0_ref, d1_ref, d2_ref), (w0_ref, w1_ref, w2_ref)):
            dh = dh + _dot(d_ref[...], w_ref[...])
        r = r_ref[...]
        xh = x_ref[...] * r
        dg_ref[...] += jnp.sum(dh * xh, axis=0, keepdims=True)
        dxh = dh * g_ref[...]
        dx_ref[...] = dx2_ref[...] + r * (dxh - xh * jnp.mean(dxh * xh, axis=-1, keepdims=True))

    row = lambda w: pl.BlockSpec((tm, w), lambda i: (i, 0))
    return pl.pallas_call(
        body, name="inproj_bwd",
        grid=(t // tm,),
        in_specs=[row(w) for _, w in P_GROUPS]
        + [pl.BlockSpec((w, D_MODEL), lambda i: (0, 0)) for _, w in P_GROUPS]
        + [row(D_MODEL), row(1), pl.BlockSpec((1, D_MODEL), lambda i: (0, 0)), row(D_MODEL),
           pl.BlockSpec(memory_space=pl.ANY)],
        out_specs=(row(D_MODEL), pl.BlockSpec((1, D_MODEL), lambda i: (0, 0))),
        out_shape=(jax.ShapeDtypeStruct((t, D_MODEL), F32),
                   jax.ShapeDtypeStruct((1, D_MODEL), F32)),
        compiler_params=_cparams(("arbitrary",)),
    )(*dgroups, *w_pts, x, rstd, g_in, dx2, after)


def _matmul(name, a, b, tm, tn, dtype=F32, after=None):
    kk, m = a.shape
    n = b.shape[1]
    extra = [] if after is None else [after]

    def body(a_ref, b_ref, *rest):
        rest[-1][...] = _dot_tn(a_ref[...].astype(BF16), b_ref[...].astype(BF16)).astype(dtype)

    return pl.pallas_call(
        body, name=name,
        grid=(n // tn, m // tm),
        in_specs=[pl.BlockSpec((kk, tm), lambda j, i: (0, i)),
                  pl.BlockSpec((kk, tn), lambda j, i: (0, j))]
        + [pl.BlockSpec(memory_space=pl.ANY) for _ in extra],
        out_specs=pl.BlockSpec((tm, tn), lambda j, i: (i, j)),
        out_shape=jax.ShapeDtypeStruct((m, n), dtype),
        compiler_params=_cparams(("arbitrary", "arbitrary")),
    )(a, b, *extra)


def _adamw_update(part_refs, w_ref, m_ref, v_ref, g_ref, d_ref, nm_ref, nv_ref):
    g = part_refs[0][...].astype(F32)
    for p_ref in part_refs[1:]:
        g = g + p_ref[...].astype(F32)
    m_new = ADAM_B1 * m_ref[...] + (1.0 - ADAM_B1) * g
    v_new = ADAM_B2 * v_ref[...] + (1.0 - ADAM_B2) * (g * g)
    m_hat = m_new / (1.0 - ADAM_B1 ** ADAM_STEP)
    v_hat = v_new / (1.0 - ADAM_B2 ** ADAM_STEP)
    g_ref[...] = g
    nm_ref[...] = m_new
    nv_ref[...] = v_new
    d_ref[...] = -ADAM_LR * (m_hat / (jnp.sqrt(v_hat) + ADAM_EPS) + ADAM_WD * w_ref[...])


def _adamw_transposed(name, first, parts, w, m, v, tl):
    _, rows, cols = w.shape
    slots, padded = parts.shape[:2]

    def body(f_ref, p_ref, *refs):
        _adamw_update([f_ref.at[pl.ds(0, cols)]]
                      + [p_ref.at[q, pl.ds(0, cols)] for q in range(slots)], *refs)

    blk = pl.BlockSpec((cols, None, tl), lambda i: (0, 0, i))
    out = jax.ShapeDtypeStruct((cols, 1, rows), F32)
    res = pl.pallas_call(
        body, name=name,
        grid=(rows // tl,),
        in_specs=[pl.BlockSpec((padded, tl), lambda i: (0, i)),
                  pl.BlockSpec((slots, padded, tl), lambda i: (0, 0, i)), blk, blk, blk],
        out_specs=(blk, blk, blk, blk),
        out_shape=(out, out, out, out),
        compiler_params=_cparams(("arbitrary",)),
    )(first, parts, *[a.transpose(2, 0, 1) for a in (w, m, v)])
    return [r.transpose(1, 2, 0) for r in res]


def _adamw_gains(gathered, ws, ms, vs):
    n = len(ws)

    def body(p_ref, *refs):
        ins, outs, loss_ref = refs[:3 * n], refs[3 * n:7 * n], refs[7 * n]
        row = 0
        for a in range(n):
            for j in range(ws[a].shape[1] // LANE):
                lanes = pl.ds(j * LANE, LANE)
                _adamw_update([p_ref.at[q, pl.ds(row, 1)] for q in range(N_DEV)],
                              *[r.at[:, lanes] for r in (ins[a], ins[n + a], ins[2 * n + a])],
                              *[r.at[:, lanes] for r in outs[4 * a:4 * a + 4]])
                row += 1
        loss = p_ref[0, pl.ds(row, 1), :]
        for q in range(1, N_DEV):
            loss = loss + p_ref[q, pl.ds(row, 1), :]
        loss_ref[...] = loss

    vmem = lambda k: [pl.BlockSpec(memory_space=pltpu.VMEM) for _ in range(k)]
    out_shape = []
    for w in ws:
        out_shape += [jax.ShapeDtypeStruct(w.shape, F32)] * 4
    out_shape.append(jax.ShapeDtypeStruct((1, LANE), F32))
    res = pl.pallas_call(
        body, name="adamw_gains",
        in_specs=vmem(1 + 3 * n), out_specs=tuple(vmem(4 * n + 1)), out_shape=tuple(out_shape),
        compiler_params=_cparams(),
    )(gathered, *ws, *ms, *vs)
    return [res[4 * a:4 * a + 4] for a in range(n)], res[-1]


def _adamw_group(firsts, parts, ws, ms, vs):
    n = len(ws)

    def body(*refs):
        ins, outs = refs[:5 * n], refs[5 * n:]
        x, y, c = _mesh_pos()
        for a in range(n):
            _adamw_update([ins[a].at[4 * x + 2 * y + c]]
                          + [ins[n + a].at[q] for q in range(ins[n + a].shape[0])],
                          *[r.at[0] for r in (ins[2 * n + a], ins[3 * n + a], ins[4 * n + a])],
                          *[r.at[0] for r in outs[4 * a:4 * a + 4]])

    vmem = lambda k: [pl.BlockSpec(memory_space=pltpu.VMEM) for _ in range(k)]
    out_shape = []
    for w in ws:
        out_shape += [jax.ShapeDtypeStruct(w.shape, F32)] * 4
    res = pl.pallas_call(
        body, name="adamw_small_weights",
        in_specs=vmem(5 * n), out_specs=tuple(vmem(4 * n)), out_shape=tuple(out_shape),
        compiler_params=_cparams(),
    )(*firsts, *parts, *ws, *ms, *vs)
    return [res[4 * a:4 * a + 4] for a in range(n)]


def _rope_tables(positions):
    half = MLA_ROPE // 2
    freqs = ROPE_THETA ** (-jnp.arange(half, dtype=F32) / half)
    ang = positions.astype(F32).reshape(-1, 1) * freqs
    cos, sin = jnp.cos(ang), jnp.sin(ang)
    t = ang.shape[0]
    one, zero = jnp.ones((t, MLA_NOPE), F32), jnp.zeros((t, half), F32)
    tail = jnp.zeros((t, LANE - MLA_QK), F32)
    rc = jnp.concatenate([one, cos, cos, tail], axis=1)
    rsn = jnp.concatenate([0.0 * one, -sin, zero, tail], axis=1)
    rsp = jnp.concatenate([0.0 * one, zero, sin, tail], axis=1)
    return rc, rsn, rsp


def _cols_full(g):
    return g.transpose(1, 0, 2)


def kernel(x, positions, g_in, w_in, g_q, w_uq, g_kv, w_ukv, w_gla_gate, b_gla_gate, g_gla, w_proj_mla, w_proj_gla, w_out, g_final, loss_target, m_g_in, m_w_in, m_g_q, m_w_uq, m_g_kv, m_w_ukv, m_w_gla_gate, m_b_gla_gate, m_g_gla, m_w_proj_mla, m_w_proj_gla, m_w_out, m_g_final, v_g_in, v_w_in, v_g_q, v_w_uq, v_g_kv, v_w_ukv, v_w_gla_gate, v_b_gla_gate, v_g_gla, v_w_proj_mla, v_w_proj_gla, v_w_out, v_g_final):
    t = x.shape[1]
    x2d = x.reshape(t, D_MODEL)
    tgt = loss_target.reshape(t, D_MODEL)
    g_final2 = g_final.reshape(1, D_MODEL)
    sharded = [(w_in, m_w_in, v_w_in), (w_uq, m_w_uq, v_w_uq), (w_ukv, m_w_ukv, v_w_ukv),
               (w_gla_gate, m_w_gla_gate, v_w_gla_gate), (w_proj_mla, m_w_proj_mla, v_w_proj_mla),
               (w_proj_gla, m_w_proj_gla, v_w_proj_gla), (w_out, m_w_out, v_w_out)]

    w_in_t = w_in.transpose(2, 0, 1).reshape(SHARD_COLS, D_MODEL)
    everyone = tuple(range(N_DEV))
    w_in_b = w_in_t.astype(BF16)
    b_uq, b_ukv, b_gate, b_pm, b_pg, b_o = [s[0][0].astype(BF16) for s in sharded[1:]]
    stages = ((0, 2, 4, 6), (1, 3, 5, 7))
    where = {d: (k, i) for k, srcs in enumerate(stages) for i, d in enumerate(srcs)}
    g_in_1, g_uq, g_ukv, g_gate = _all_gather(
        "all_gather_first", [w_in_b, b_uq, b_ukv, b_gate], [stages[0]] + [everyone] * 3)
    w_uq_p = jnp.pad(_cols_full(g_uq), ((0, 0), (0, 0), (0, HEAD_PAD - MLA_QK))).reshape(
        MLA_Q_RANK, MLA_HEADS * HEAD_PAD)
    ukv = _cols_full(g_ukv)
    w_k_p = jnp.pad(ukv[:, :, :MLA_NOPE], ((0, 0), (0, 0), (0, HEAD_PAD - MLA_NOPE))).reshape(
        MLA_KV_RANK, MLA_HEADS * HEAD_PAD)
    w_v = ukv[:, :, MLA_NOPE:].reshape(MLA_KV_RANK, MLA_WIDTH)
    w_gate_p = jnp.pad(_cols_full(g_gate).reshape(GLA_GATE_RANK, GLA_DK),
                       ((MISC_ALR, LANE - MISC_ALR - GLA_GATE_RANK), (0, 0)))
    rc, rsn, rsp = _rope_tables(positions)

    w_lat = _weights_to_p("weights_latents", [g_in_1], where, 2)
    proj_lat, h, rstd = _inproj(x2d, g_in, w_lat)
    q, k, v, log_a, pre, cqn, ckvn, rq, rkv, misc = _mla_prep(
        proj_lat, g_q, g_kv, w_uq_p, w_k_p, w_v, w_gate_p, b_gla_gate, rc, rsn, rsp)
    o_mla, lse, (g_in_2, g_pm, g_pg, g_o) = _mla_attn_fwd(
        q, k, v, [w_in_b, b_pm, b_pg, b_o], [stages[1]] + [everyone] * 3)
    w_gla = _weights_to_p("weights_gla", [g_in_1, g_in_2], where, 0)
    proj_gla = _proj("inproj_gla", h, w_gla)
    o_gla, states = _gla_fwd(proj_gla, log_a)
    w_out_path = _weights_to_p("weights_out_path", [g_in_1, g_in_2], where, 1)
    proj_out = _proj("inproj_out_path", h, w_out_path)
    w_in_p = (w_gla, w_out_path, w_lat)
    w_pm = _cols_full(g_pm).reshape(MLA_WIDTH, D_MODEL)
    w_pg = g_pg.reshape(GLA_DV, D_MODEL)
    w_o = g_o.reshape(D_MODEL, D_MODEL)

    (dx2, do_mla, do_gla, d_out, merged, um, ug, dym, dyg, loss_p, dg_final,
     dg_gla) = _post(o_mla, proj_out, o_gla, x2d, tgt, g_gla, g_final2, w_pm, w_pg, w_o)

    p_pm = _matmul("dw_proj_mla", um, dym, 512, D_MODEL, BF16).reshape(
        MLA_WIDTH, N_DEV, D_MODEL // N_DEV).transpose(1, 0, 2)
    p_pg = _matmul("dw_proj_gla", ug, dyg, 512, D_MODEL, BF16).reshape(N_DEV, -1, D_MODEL)
    p_o = _matmul("dw_out", merged, dx2, 512, D_MODEL, BF16).reshape(N_DEV, -1, D_MODEL)
    own_in = jnp.zeros((SHARD_PAD, D_MODEL), BF16)
    land_in = lax.empty((PEERS, SHARD_PAD, D_MODEL), BF16)
    dw_groups, started, lands = {}, [], [land_in]

    def reduce_scatter_stage(s, dests, own_in, extra=()):
        parts_in, own_in = _grads_to_shards("grads_to_shards_%d" % s, dw_groups, dests, own_in)
        first = len(lands)
        lands.extend(lax.empty((PEERS,) + p.shape[1:], BF16) for p in extra)
        idx = [0] + list(range(first, len(lands)))
        all_dests = [[(i, d, r0, r1) for i, (d, ranges) in enumerate(dests) for r0, r1 in ranges]]
        all_dests += [_whole(everyone, p.shape[1]) for p in extra]
        sems, parts, new_lands, token = _ici_start(
            "ici_start_%d" % s, [parts_in] + list(extra), [lands[i] for i in idx], all_dests)
        for a, i in enumerate(idx):
            lands[i] = new_lands[a]
            started.append((sems[a][0], sems[a][1], parts[a], i, all_dests[a]))
        return own_in, token

    def late_small_stage(arrays):
        idx = list(range(len(lands), len(lands) + len(arrays)))
        lands.extend(lax.empty((PEERS,) + p.shape[1:], BF16) for p in arrays)
        all_dests = [_whole(everyone, p.shape[1]) for p in arrays]
        sems, parts, new_lands, token = _ici_start(
            "ici_start_4", list(arrays), [lands[i] for i in idx], all_dests)
        for a, i in enumerate(idx):
            lands[i] = new_lands[a]
            started.append((sems[a][0], sems[a][1], parts[a], i, all_dests[a]))
        return token

    dw_groups[1] = _matmul("dw_in_1", d_out, h, 512, D_MODEL, BF16)
    full = [(0, SHARD_PAD)]
    own_in, token = reduce_scatter_stage(
        1, [(5, full), (6, full), (7, full), (0, [(672, SHARD_PAD)]), (1, [(0, 384)]),
            (4, [(96, SHARD_PAD)])], own_in, (p_pm, p_pg, p_o))
    d_gla, dla = _gla_bwd(proj_gla, log_a, do_gla, states, token)
    dw_groups[0] = _matmul("dw_in_0", d_gla, h, 512, D_MODEL, BF16)
    own_in, token = reduce_scatter_stage(
        2, [(1, [(384, SHARD_PAD)]), (2, full), (3, full), (4, [(0, 64)])], own_in)
    dq, dk, dv = _mla_attn_bwd(q, k, v, o_mla, do_mla, lse, token)
    d_lat, dqpre, dpre, dg_q, dg_kv, db_gate = _mla_prep_bwd(
        dq, dk, dv, dla, pre, proj_lat, rq, rkv, g_q, g_kv, w_uq_p, w_k_p, w_v, w_gate_p, rc, rsn, rsp)
    dw_groups[2] = _matmul("dw_in_2", d_lat, h, 896, D_MODEL, BF16)
    own_in, token = reduce_scatter_stage(3, [(0, [(0, 672)]), (4, [(64, 96)])], own_in)
    dw_uq = _matmul("dw_uq", cqn, dqpre, MLA_Q_RANK, D_MODEL, BF16, after=token)
    p_uq = dw_uq.reshape(MLA_Q_RANK, MLA_HEADS, HEAD_PAD)[:, :, :MLA_QK].transpose(1, 0, 2)
    dw_k = _matmul("dw_uk", ckvn, dk, MLA_KV_RANK, D_MODEL, BF16)
    dw_v = _matmul("dw_uv", ckvn, dv, MLA_KV_RANK, 512, BF16)
    p_ukv = jnp.concatenate(
        [dw_k.reshape(MLA_KV_RANK, MLA_HEADS, HEAD_PAD)[:, :, :MLA_NOPE],
         dw_v.reshape(MLA_KV_RANK, MLA_HEADS, MLA_VDIM)], axis=2).transpose(1, 0, 2)
    dw_gate = _matmul("dw_gate", misc, dpre, LANE, 512, BF16)
    p_gate = dw_gate[MISC_ALR:MISC_ALR + GLA_GATE_RANK].reshape(
        GLA_GATE_RANK, N_DEV, GLA_DK // N_DEV).transpose(1, 0, 2)
    token = late_small_stage((p_uq, p_ukv, p_gate))
    grad_x, dg_in = _inproj_bwd((d_gla, d_out, d_lat), w_in_p, x2d, rstd, g_in, dx2, token)
    small = jnp.concatenate([dg_in.reshape(-1), dg_q.reshape(-1), dg_kv.reshape(-1),
                             db_gate.reshape(-1), dg_gla.reshape(-1), dg_final.reshape(-1),
                             loss_p[0, :1]])
    small = jnp.pad(small, (0, SMALL_ROWS * LANE - small.shape[0])).reshape(SMALL_ROWS, LANE)

    (small_all,) = _all_gather("all_gather_small", [small], [everyone])
    lands = _ici_wait("ici_wait", started, lands, small_all)
    big = [_adamw_transposed("adamw_w_in", own_in, lands[0], *sharded[0], 512)]
    big += _adamw_group([p_uq, p_ukv, p_gate, p_pm, p_pg, p_o], list(lands[4:7]) + list(lands[1:4]),
                        *[[s[j] for s in sharded[1:]] for j in range(3)])
    replicated = [(g_in, m_g_in, v_g_in), (g_q, m_g_q, v_g_q), (g_kv, m_g_kv, v_g_kv),
                  (b_gla_gate, m_b_gla_gate, v_b_gla_gate), (g_gla, m_g_gla, v_g_gla),
                  tuple(a.reshape(1, D_MODEL) for a in (g_final, m_g_final, v_g_final))]
    gains, loss_sum = _adamw_gains(small_all, *[[s[j] for s in replicated] for j in range(3)])

    outs = {}
    names = ("w_in", "w_uq", "w_ukv", "w_gla_gate", "w_proj_mla", "w_proj_gla", "w_out")
    small_names = ("g_in", "g_q", "g_kv", "b_gla_gate", "g_gla", "g_final")
    for j, kind in enumerate(("grad", "delta", "new_m", "new_v")):
        for name, res in zip(names + small_names, list(big) + gains):
            outs[kind, name] = res[j].reshape(-1) if name == "g_final" else res[j]
    loss = loss_sum[0, 0]
    order = ("g_in", "w_in", "g_q", "w_uq", "g_kv", "w_ukv", "w_gla_gate", "b_gla_gate", "g_gla",
             "w_proj_mla", "w_proj_gla", "w_out", "g_final")
    result = [loss, grad_x.reshape(1, t, D_MODEL)]
    for kind in ("grad", "delta", "new_m", "new_v"):
        result += [outs[kind, name] for name in order]
    return tuple(result)
```

```python
import jax
import jax.numpy as jnp
from jax import lax
from jax.experimental import pallas as pl
from jax.experimental.pallas import tpu as pltpu

F32 = jnp.float32
BF16 = jnp.bfloat16
MESH = pl.DeviceIdType.MESH
N_DEV = 8

D_MODEL = 1024
EPS = 1e-6
MLA_HEADS = 8
MLA_NOPE = 64
MLA_ROPE = 32
MLA_VDIM = 64
MLA_Q_RANK = 384
MLA_KV_RANK = 256
MLA_QK = MLA_NOPE + MLA_ROPE
MLA_WIDTH = MLA_HEADS * MLA_VDIM
ROPE_THETA = 10000.0
GLA_HEADS = 4
GLA_DK = 512
GLA_DV = 1024
GLA_HK = 128
GLA_HV = 256
GLA_GATE_RANK = 16
GLA_GATE_NORM = 16.0
GLA_CHUNK = 64
GLA_CHUNKS_PER_STEP = 8
D_IN = 6320

ADAM_LR = 0.001
ADAM_B1 = 0.9
ADAM_B2 = 0.999
ADAM_EPS = 1e-08
ADAM_WD = 0.01
ADAM_STEP = 10

LANE = 128
HEAD_PAD = 128
VMEM_LIMIT = 48 * 1024 * 1024

P_VG, P_QG, P_KG = 0, 1024, 1536
P_ZGLA, P_GMLA, P_GGLA, P_ZMLA = 2048, 3072, 4096, 5120
P_CQ, P_CKV, P_MISC = 5632, 6144, 6400
P_TOTAL = 6528
P_GROUPS = ((0, 2048), (2048, 3584), (5632, 896))
MISC_KR = 64
MISC_ALR = 96
SHARD_COLS = D_IN // N_DEV
SHARD_PAD = 800
P_COMPONENTS = ((0, 384, P_CQ), (384, 256, P_CKV), (640, 32, P_MISC + MISC_KR), (672, 512, P_ZMLA),
                (1184, 512, P_QG), (1696, 512, P_KG), (2208, 1024, P_VG),
                (3232, 16, P_MISC + MISC_ALR), (3248, 1024, P_ZGLA), (4272, 1024, P_GMLA),
                (5296, 1024, P_GGLA))

SMALL_ROWS = 32


def _segments():
    segs = []
    for g0, n, p0 in P_COMPONENTS:
        g = g0
        while g < g0 + n:
            d = g // SHARD_COLS
            end = min(g0 + n, (d + 1) * SHARD_COLS)
            segs.append((d, g - d * SHARD_COLS, end - g, p0 + g - g0))
            g = end
    return segs


def _group_of(p0):
    return max(i for i, (off, _) in enumerate(P_GROUPS) if off <= p0)


def _rel(p0):
    return p0 - P_GROUPS[_group_of(p0)][0]


def _cparams(sem=None):
    if sem is None:
        return pltpu.CompilerParams(vmem_limit_bytes=VMEM_LIMIT)
    return pltpu.CompilerParams(dimension_semantics=sem, vmem_limit_bytes=VMEM_LIMIT)


def _sigmoid(v):
    return 1.0 / (1.0 + jnp.exp(-v))


def _dot(a, b):
    return jnp.dot(a, b, preferred_element_type=F32)


def _dot_nt(a, b):
    return lax.dot_general(a, b, (((1,), (1,)), ((), ())), preferred_element_type=F32)


def _dot_tn(a, b):
    return lax.dot_general(a, b, (((0,), (0,)), ((), ())), preferred_element_type=F32)


def _dot_exact(a, b):
    return jnp.dot(a, b, preferred_element_type=F32, precision=lax.Precision.HIGHEST)


def _rope_fwd(blk, c, sn, sp):
    return blk * c + pltpu.roll(blk, LANE - 16, 1) * sn + pltpu.roll(blk, 16, 1) * sp


def _rope_bwd(blk, c, sn, sp):
    return blk * c + pltpu.roll(blk * sn, 16, 1) + pltpu.roll(blk * sp, LANE - 16, 1)


def _mesh_pos():
    return lax.axis_index("x"), lax.axis_index("y"), lax.axis_index("c")


def _hbm_specs(n):
    return [pl.BlockSpec(memory_space=pltpu.HBM) for _ in range(n)]


def _dev(d):
    return d >> 2, (d >> 1) & 1, d & 1


def _gather_plan(shards, sources):
    na, most = len(shards), max(len(s) for s in sources)
    out_shape = [jax.ShapeDtypeStruct((len(srcs),) + s.shape, s.dtype)
                 for s, srcs in zip(shards, sources)]
    sems = [pltpu.SemaphoreType.DMA((na, most)) for _ in range(3)]
    sems += [pltpu.SemaphoreType.DMA((na, most, 3))]
    sems += [pltpu.SemaphoreType.DMA((na, most)) for _ in range(3)]
    return out_shape, sems


def _gather_hooks(x_refs, out_refs, sems, sources):
    local_sems, d2d_send, d2d_recv, ici_send, ici_recv, fwd_send, fwd_recv = sems
    x, y, c = _mesh_pos()
    chips = [(1 - x, y), (x, 1 - y), (1 - x, 1 - y)]
    items = []
    for a, srcs in enumerate(sources):
        for i, d in enumerate(srcs):
            dx, dy, dc = _dev(d)
            near = jnp.logical_and(x == dx, y == dy)
            far = jnp.logical_not(near)
            slot = out_refs[a].at[i]

            def remote(src, to, send_sem, recv_sem, slot=slot):
                return pltpu.make_async_remote_copy(
                    src_ref=src, dst_ref=slot, send_sem=send_sem, recv_sem=recv_sem,
                    device_id=to, device_id_type=MESH)

            items.append(dict(
                me=jnp.logical_and(near, c == dc), sibling=jnp.logical_and(near, c != dc),
                relay=jnp.logical_and(far, c == dc), behind=jnp.logical_and(far, c != dc),
                local=pltpu.make_async_copy(x_refs[a], slot, local_sems.at[a, i]),
                to_sibling=remote(x_refs[a], (x, y, 1 - c), d2d_send.at[a, i], d2d_recv.at[a, i]),
                to_chips=[remote(x_refs[a], (*chip, c), ici_send.at[a, i, j], ici_recv.at[a, i])
                          for j, chip in enumerate(chips)],
                forward=remote(slot, (x, y, 1 - c), fwd_send.at[a, i], fwd_recv.at[a, i])))

    def start():
        for it in items:
            @pl.when(it["me"])
            def _(it=it):
                it["local"].start()
                it["to_sibling"].start()
                for cp in it["to_chips"]:
                    cp.start()

    def finish():
        for it in items:
            @pl.when(it["relay"])
            def _(it=it):
                it["to_chips"][0].wait_recv()
                it["forward"].start()
        for it in items:
            pl.when(it["sibling"])(it["to_sibling"].wait_recv)
            pl.when(it["behind"])(it["forward"].wait_recv)
            pl.when(it["relay"])(it["forward"].wait_send)

            @pl.when(it["me"])
            def _(it=it):
                it["local"].wait()
                it["to_sibling"].wait_send()
                for cp in it["to_chips"]:
                    cp.wait_send()

    return start, finish


def _all_gather(name, shards, sources):
    n = len(shards)
    out_shape, sems = _gather_plan(shards, sources)

    def body(*refs):
        start, finish = _gather_hooks(refs[:n], refs[n:2 * n], refs[2 * n:], sources)
        start()
        finish()

    return pl.pallas_call(
        body, name=name,
        out_shape=tuple(out_shape),
        in_specs=_hbm_specs(n), out_specs=tuple(_hbm_specs(n)),
        scratch_shapes=sems,
        compiler_params=_cparams(),
    )(*shards)


PEERS = N_DEV - 1


def _whole(dests, rows):
    return [(i, d, 0, rows) for i, d in enumerate(dests)]


def _ici_copies(p_ref, land_ref, send_sems, recv_sems, pieces):
    x, y, c = _mesh_pos()
    sends, arrivals = [], []

    def rows_of(ref, j, r0, r1):
        return ref.at[j] if (r0, r1) == (0, ref.shape[1]) else ref.at[j, pl.ds(r0, r1 - r0)]

    for p, (i, d, r0, r1) in enumerate(pieces):
        dx, dy, dc = _dev(d)
        k = (4 * (x != dx).astype(jnp.int32) + 2 * (y != dy).astype(jnp.int32)
             + (c != dc).astype(jnp.int32))
        slot = jnp.maximum(k - 1, 0)
        sends.append((k > 0, pltpu.make_async_remote_copy(
            src_ref=rows_of(p_ref, i, r0, r1), dst_ref=rows_of(land_ref, slot, r0, r1),
            send_sem=send_sems.at[p], recv_sem=recv_sems.at[p * PEERS + slot],
            device_id=(dx, dy, dc), device_id_type=MESH)))
        arrivals.append((k == 0, [pltpu.make_async_remote_copy(
            src_ref=rows_of(p_ref, i, r0, r1), dst_ref=rows_of(land_ref, r, r0, r1),
            send_sem=send_sems.at[p], recv_sem=recv_sems.at[p * PEERS + r],
            device_id=(dx, dy, dc), device_id_type=MESH) for r in range(PEERS)]))
    return sends, arrivals


def _ici_start(name, hs, lands, dests):
    na = len(hs)

    def body(*refs):
        h_refs, land_refs, sems = refs[:na], refs[na:2 * na], refs[2 * na:4 * na]
        token = refs[-1]
        for a in range(na):
            sends, _ = _ici_copies(h_refs[a], land_refs[a], sems[2 * a], sems[2 * a + 1], dests[a])
            for go, cp in sends:
                pl.when(go)(cp.start)
        token[...] = jnp.zeros_like(token)

    hbm, sem = pl.BlockSpec(memory_space=pltpu.HBM), pl.BlockSpec(memory_space=pltpu.SEMAPHORE)
    sem_shapes = []
    for a in range(na):
        sem_shapes += [pltpu.SemaphoreType.DMA((len(dests[a]),)),
                       pltpu.SemaphoreType.DMA((len(dests[a]) * PEERS,))]
    res = pl.pallas_call(
        body, name=name,
        out_shape=tuple(sem_shapes) + tuple(pltpu.HBM(v.shape, v.dtype) for v in list(hs) + list(lands))
        + (jax.ShapeDtypeStruct((8, LANE), F32),),
        in_specs=(hbm,) * (2 * na),
        out_specs=(sem,) * (2 * na) + (hbm,) * (2 * na) + (pl.BlockSpec(memory_space=pltpu.VMEM),),
        input_output_aliases={i: 2 * na + i for i in range(2 * na)},
        compiler_params=pltpu.CompilerParams(
            has_side_effects=pltpu.SideEffectType.DATAFLOW_SIDE_EFFECTING,
            vmem_limit_bytes=VMEM_LIMIT),
    )(*[pltpu.with_memory_space_constraint(v, pltpu.HBM) for v in list(hs) + list(lands)])
    sems = [(res[2 * a], res[2 * a + 1]) for a in range(na)]
    return sems, res[2 * na:3 * na], res[3 * na:4 * na], res[-1]


def _ici_wait(name, started, lands, after):
    k, nl = len(started), len(lands)

    def body(*refs):
        land_refs = refs[3 * k:3 * k + nl]
        for s in range(k):
            h_ref, send_sems, recv_sems = refs[3 * s:3 * s + 3]
            sends, arrivals = _ici_copies(h_ref, land_refs[started[s][3]], send_sems, recv_sems,
                                          started[s][4])
            for go, cp in sends:
                pl.when(go)(cp.wait_send)
            for here, cps in arrivals:
                for cp in cps:
                    pl.when(here)(cp.wait_recv)

    hbm, sem = pl.BlockSpec(memory_space=pltpu.HBM), pl.BlockSpec(memory_space=pltpu.SEMAPHORE)
    operands, specs = [], []
    for send_sems, recv_sems, h, _, _ in started:
        operands += [h, send_sems, recv_sems]
        specs += [hbm, sem, sem]
    return pl.pallas_call(
        body, name=name,
        out_shape=tuple(pltpu.HBM(v.shape, v.dtype) for v in lands),
        in_specs=tuple(specs) + (hbm,) * nl + (pl.BlockSpec(memory_space=pl.ANY),),
        out_specs=(hbm,) * nl,
        input_output_aliases={3 * k + i: i for i in range(nl)},
        compiler_params=pltpu.CompilerParams(
            has_side_effects=pltpu.SideEffectType.DATAFLOW_SIDE_EFFECTING,
            vmem_limit_bytes=VMEM_LIMIT),
    )(*operands, *lands, after)


def _weights_to_p(name, gathered, where, group):
    tl = 512
    off, width = P_GROUPS[group]
    segs = sorted([s for s in _segments() if _group_of(s[3]) == group], key=lambda s: s[3])
    used = sorted({where[s[0]][0] for s in segs})

    def body(*refs):
        g_refs, o_ref = dict(zip(used, refs[:-1])), refs[-1]
        pieces, pos = [], off
        for d, c0, n, p0 in segs:
            if p0 > pos:
                pieces.append(jnp.zeros((p0 - pos, tl), F32))
            k, slot = where[d]
            pieces.append(g_refs[k][slot, c0:c0 + n, :].astype(F32))
            pos = p0 + n
        if off + width > pos:
            pieces.append(jnp.zeros((off + width - pos, tl), F32))
        o_ref[...] = jnp.concatenate(pieces, axis=0).astype(BF16)

    return pl.pallas_call(
        body, name=name,
        grid=(D_MODEL // tl,),
        in_specs=[pl.BlockSpec((gathered[k].shape[0], SHARD_COLS, tl), lambda i: (0, 0, i))
                  for k in used],
        out_specs=pl.BlockSpec((width, tl), lambda i: (0, i)),
        out_shape=jax.ShapeDtypeStruct((width, D_MODEL), BF16),
        compiler_params=_cparams(("arbitrary",)),
    )(*[gathered[k] for k in used])


def _grads_to_shards(name, groups, dests, own_prev):
    tl = 512
    segs = _segments()
    used = sorted(groups)

    def body(*refs):
        g_refs, prev_ref, o_ref, own_ref = dict(zip(used, refs[:-3])), refs[-3], refs[-2], refs[-1]
        x, y, c = _mesh_pos()
        own = prev_ref[...].astype(F32)
        row = lax.broadcasted_iota(jnp.int32, (SHARD_PAD, tl), 0)
        for i, (d, ranges) in enumerate(dests):
            pieces, pos, asked = [], 0, None
            for r0, r1 in sorted(ranges):
                if r0 > pos:
                    pieces.append(jnp.zeros((r0 - pos, tl), F32))
                for _, c0, n, p0 in sorted([s for s in segs if s[0] == d], key=lambda s: s[1]):
                    a, b = max(c0, r0), min(c0 + n, r1)
                    if a < b:
                        gi = _group_of(p0)
                        lo = p0 - P_GROUPS[gi][0] + a - c0
                        pieces.append(g_refs[gi][lo:lo + b - a, :].astype(F32))
                if r1 > SHARD_COLS:
                    pieces.append(jnp.zeros((r1 - max(r0, SHARD_COLS), tl), F32))
                pos = r1
                inside = jnp.logical_and(row >= r0, row < r1)
                asked = inside if asked is None else jnp.logical_or(asked, inside)
            if pos < SHARD_PAD:
                pieces.append(jnp.zeros((SHARD_PAD - pos, tl), F32))
            shard = jnp.concatenate(pieces, axis=0)
            o_ref[i] = shard.astype(BF16)
            own = jnp.where(jnp.logical_and(4 * x + 2 * y + c == d, asked), shard, own)
        own_ref[...] = own.astype(BF16)

    blk = pl.BlockSpec((SHARD_PAD, tl), lambda i: (0, i))
    return pl.pallas_call(
        body, name=name,
        grid=(D_MODEL // tl,),
        in_specs=[pl.BlockSpec((P_GROUPS[g][1], tl), lambda i: (0, i)) for g in used] + [blk],
        out_specs=(pl.BlockSpec((len(dests), SHARD_PAD, tl), lambda i: (0, 0, i)), blk),
        out_shape=(jax.ShapeDtypeStruct((len(dests), SHARD_PAD, D_MODEL), BF16),
                   jax.ShapeDtypeStruct((SHARD_PAD, D_MODEL), BF16)),
        input_output_aliases={len(used): 1},
        compiler_params=_cparams(("arbitrary",)),
    )(*[groups[g] for g in used], own_prev)


def _inproj(x, g_in, w_pt):
    t = x.shape[0]
    tm = min(512, t)
    width = w_pt.shape[0]

    def body(x_ref, g_ref, w_ref, proj_ref, h_ref, r_ref):
        xf = x_ref[...]
        r = lax.rsqrt(jnp.mean(xf * xf, axis=-1, keepdims=True) + EPS)
        h = ((xf * r) * g_ref[...]).astype(BF16)
        proj_ref[...] = _dot_nt(h, w_ref[...])
        h_ref[...] = h
        r_ref[...] = r

    row = lambda w: pl.BlockSpec((tm, w), lambda i: (i, 0))
    return pl.pallas_call(
        body, name="inproj_latents",
        grid=(t // tm,),
        in_specs=[row(D_MODEL), pl.BlockSpec((1, D_MODEL), lambda i: (0, 0)),
                  pl.BlockSpec((width, D_MODEL), lambda i: (0, 0))],
        out_specs=(row(width), row(D_MODEL), row(1)),
        out_shape=(jax.ShapeDtypeStruct((t, width), F32),
                   jax.ShapeDtypeStruct((t, D_MODEL), BF16),
                   jax.ShapeDtypeStruct((t, 1), F32)),
        compiler_params=_cparams(("arbitrary",)),
    )(x, g_in, w_pt)


def _proj(name, h, w_pt):
    t = h.shape[0]
    tm = min(512, t)
    width = w_pt.shape[0]

    def body(h_ref, w_ref, o_ref):
        o_ref[...] = _dot_nt(h_ref[...], w_ref[...])

    return pl.pallas_call(
        body, name=name,
        grid=(t // tm,),
        in_specs=[pl.BlockSpec((tm, D_MODEL), lambda i: (i, 0)),
                  pl.BlockSpec((width, D_MODEL), lambda i: (0, 0))],
        out_specs=pl.BlockSpec((tm, width), lambda i: (i, 0)),
        out_shape=jax.ShapeDtypeStruct((t, width), F32),
        compiler_params=_cparams(("arbitrary",)),
    )(h, w_pt)


def _mla_prep(proj, g_q, g_kv, w_uq_p, w_k_p, w_v, w_gate_p, b_gate, rc, rsn, rsp):
    t = proj.shape[0]
    tm = min(256, t)
    hq = MLA_HEADS * HEAD_PAD

    def body(cq_ref, ckv_ref, misc_ref, gq_ref, gkv_ref, wuq_ref, wk_ref, wv_ref, wg_ref, bg_ref,
             c_ref, sn_ref, sp_ref,
             q_ref, k_ref, v_ref, la_ref, pre_ref, cqn_ref, ckvn_ref, rq_ref, rkv_ref, mb_ref):
        c, sn, sp = c_ref[...], sn_ref[...], sp_ref[...]
        cq = cq_ref[:, :MLA_Q_RANK]
        rq = lax.rsqrt(jnp.mean(cq * cq, axis=-1, keepdims=True) + EPS)
        cqn = ((cq * rq) * gq_ref[...]).astype(BF16)
        cqn_ref[...] = cqn
        rq_ref[...] = rq
        qpre = _dot(cqn, wuq_ref[...])
        ckv = ckv_ref[...]
        rkv = lax.rsqrt(jnp.mean(ckv * ckv, axis=-1, keepdims=True) + EPS)
        ckvn = ((ckv * rkv) * gkv_ref[...]).astype(BF16)
        ckvn_ref[...] = ckvn
        rkv_ref[...] = rkv
        kn = _dot(ckvn, wk_ref[...])
        v_ref[...] = _dot(ckvn, wv_ref[...]).astype(BF16)
        misc = misc_ref[...]
        krope = _rope_fwd(misc, c, sn, sp)
        for h in range(MLA_HEADS):
            sl = slice(h * HEAD_PAD, (h + 1) * HEAD_PAD)
            q_ref[:, sl] = _rope_fwd(qpre[:, sl], c, sn, sp).astype(BF16)
            k_ref[:, sl] = (kn[:, sl] + krope).astype(BF16)
        mb_ref[...] = misc.astype(BF16)
        pre = _dot(mb_ref[...], wg_ref[...]) + bg_ref[...]
        pre_ref[...] = pre
        log_a = (jnp.minimum(pre, 0.0) - jnp.log(1.0 + jnp.exp(-jnp.abs(pre)))) / GLA_GATE_NORM
        la_ref[...] = _dot_exact(_chunk_tri(tm, True), log_a)

    row = lambda w: pl.BlockSpec((tm, w), lambda i: (i, 0))
    full = lambda a: pl.BlockSpec(a.shape, lambda i: (0, 0))
    return pl.pallas_call(
        body, name="mla_prep",
        grid=(t // tm,),
        in_specs=[pl.BlockSpec((tm, 512), lambda i: (i, _rel(P_CQ) // 512)),
                  pl.BlockSpec((tm, MLA_KV_RANK), lambda i: (i, _rel(P_CKV) // MLA_KV_RANK)),
                  pl.BlockSpec((tm, LANE), lambda i: (i, _rel(P_MISC) // LANE)),
                  full(g_q), full(g_kv), full(w_uq_p), full(w_k_p), full(w_v), full(w_gate_p),
                  full(b_gate), row(LANE), row(LANE), row(LANE)],
        out_specs=(row(hq), row(hq), row(MLA_WIDTH), row(GLA_DK), row(GLA_DK),
                   row(MLA_Q_RANK), row(MLA_KV_RANK), row(1), row(1), row(LANE)),
        out_shape=(jax.ShapeDtypeStruct((t, hq), BF16), jax.ShapeDtypeStruct((t, hq), BF16),
                   jax.ShapeDtypeStruct((t, MLA_WIDTH), BF16),
                   jax.ShapeDtypeStruct((t, GLA_DK), F32), jax.ShapeDtypeStruct((t, GLA_DK), F32),
                   jax.ShapeDtypeStruct((t, MLA_Q_RANK), BF16),
                   jax.ShapeDtypeStruct((t, MLA_KV_RANK), BF16),
                   jax.ShapeDtypeStruct((t, 1), F32), jax.ShapeDtypeStruct((t, 1), F32),
                   jax.ShapeDtypeStruct((t, LANE), BF16)),
        compiler_params=_cparams(("arbitrary",)),
    )(proj, proj, proj, g_q, g_kv, w_uq_p, w_k_p, w_v, w_gate_p, b_gate, rc, rsn, rsp)


def _attn_masks(tq, i):
    keys = (i + 1) * tq
    rows = i * tq + lax.broadcasted_iota(jnp.int32, (tq, keys), 0)
    cols = lax.broadcasted_iota(jnp.int32, (tq, keys), 1)
    lane = lax.broadcasted_iota(jnp.int32, (tq, LANE), 1)
    return cols <= rows, lane < MLA_VDIM


def _for_each_query_tile(n_tiles, fn):
    for i in range(n_tiles):
        pl.when(pl.program_id(1) == i)(lambda i=i: fn(i))


def _mla_attn_fwd(q, k, v, shards, sources):
    t = q.shape[0]
    tq = min(256, t)
    scale = MLA_QK ** -0.5
    ns = len(shards)
    g_shapes, g_sems = _gather_plan(shards, sources)
    grid = (MLA_HEADS // 2, t // tq)

    def body(q_ref, k_ref, v_ref, *rest):
        o_ref, lse_ref = rest[ns:ns + 2]
        start, finish = _gather_hooks(rest[:ns], rest[ns + 2:2 * ns + 2], rest[2 * ns + 2:], sources)
        step = pl.program_id(0) * grid[1] + pl.program_id(1)
        pl.when(step == 0)(start)

        def tile(i):
            keys = (i + 1) * tq
            causal, low = _attn_masks(tq, i)
            vp = v_ref[0:keys, :]
            acc = jnp.zeros((tq, LANE), F32)
            for hh in range(2):
                sl = slice(hh * HEAD_PAD, (hh + 1) * HEAD_PAD)
                s = _dot_nt(q_ref[:, sl], k_ref[0:keys, sl]) * scale
                s = jnp.where(causal, s, -jnp.inf)
                m = jnp.max(s, axis=-1, keepdims=True)
                e = jnp.exp(s - m)
                l = jnp.sum(e, axis=-1, keepdims=True)
                o = _dot(e.astype(BF16), vp) / l
                acc = jnp.where(low if hh == 0 else jnp.logical_not(low), o, acc)
                lse_ref[hh] = m + jnp.log(l)
            o_ref[...] = acc

        _for_each_query_tile(t // tq, tile)
        pl.when(step == grid[0] * grid[1] - 1)(finish)

    res = pl.pallas_call(
        body, name="mla_attn_fwd",
        grid=grid,
        in_specs=[pl.BlockSpec((tq, 2 * HEAD_PAD), lambda p, i: (i, p)),
                  pl.BlockSpec((t, 2 * HEAD_PAD), lambda p, i: (0, p)),
                  pl.BlockSpec((t, LANE), lambda p, i: (0, p))] + _hbm_specs(ns),
        out_specs=(pl.BlockSpec((tq, LANE), lambda p, i: (i, p)),
                   pl.BlockSpec((2, tq, 1), lambda p, i: (p, i, 0))) + tuple(_hbm_specs(ns)),
        out_shape=(jax.ShapeDtypeStruct((t, MLA_WIDTH), F32),
                   jax.ShapeDtypeStruct((MLA_HEADS, t, 1), F32)) + tuple(g_shapes),
        scratch_shapes=g_sems,
        compiler_params=_cparams(("arbitrary", "arbitrary")),
    )(q, k, v, *shards)
    return res[0], res[1], res[2:]


def _mla_attn_bwd(q, k, v, o, do, lse, after):
    t = q.shape[0]
    tq = min(256, t)
    scale = MLA_QK ** -0.5

    def body(q_ref, k_ref, v_ref, o_ref, do_ref, lse_ref, after_ref, dq_ref, dk_ref, dv_ref):
        del after_ref

        @pl.when(pl.program_id(1) == 0)
        def _():
            dk_ref[...] = jnp.zeros_like(dk_ref)
            dv_ref[...] = jnp.zeros_like(dv_ref)

        def tile(i):
            keys = (i + 1) * tq
            causal, low = _attn_masks(tq, i)
            vp = v_ref[0:keys, :]
            do_all = do_ref[...]
            o_all = o_ref[...]
            dv_acc = jnp.zeros((keys, LANE), F32)
            for hh in range(2):
                sl = slice(hh * HEAD_PAD, (hh + 1) * HEAD_PAD)
                do_h = jnp.where(low if hh == 0 else jnp.logical_not(low), do_all, 0.0)
                dsum = jnp.sum(do_h * o_all, axis=-1, keepdims=True)
                qh = q_ref[:, sl]
                kh = k_ref[0:keys, sl]
                s = _dot_nt(qh, kh) * scale
                p = jnp.where(causal, jnp.exp(s - lse_ref[hh]), 0.0)
                do_b = do_h.astype(BF16)
                dp = _dot_nt(do_b, vp)
                ds = (p * (dp - dsum) * scale).astype(BF16)
                dq_ref[:, sl] = _dot(ds, kh).astype(BF16)
                dk_ref[0:keys, sl] += _dot_tn(ds, qh)
                dv_acc = dv_acc + _dot_tn(p.astype(BF16), do_b)
            dv_ref[0:keys, :] += dv_acc

        _for_each_query_tile(t // tq, tile)

    return pl.pallas_call(
        body, name="mla_attn_bwd",
        grid=(MLA_HEADS // 2, t // tq),
        in_specs=[pl.BlockSpec((tq, 2 * HEAD_PAD), lambda p, i: (i, p)),
                  pl.BlockSpec((t, 2 * HEAD_PAD), lambda p, i: (0, p)),
                  pl.BlockSpec((t, LANE), lambda p, i: (0, p)),
                  pl.BlockSpec((tq, LANE), lambda p, i: (i, p)),
                  pl.BlockSpec((tq, LANE), lambda p, i: (i, p)),
                  pl.BlockSpec((2, tq, 1), lambda p, i: (p, i, 0)),
                  pl.BlockSpec(memory_space=pl.ANY)],
        out_specs=(pl.BlockSpec((tq, 2 * HEAD_PAD), lambda p, i: (i, p)),
                   pl.BlockSpec((t, 2 * HEAD_PAD), lambda p, i: (0, p)),
                   pl.BlockSpec((t, LANE), lambda p, i: (0, p))),
        out_shape=(jax.ShapeDtypeStruct((t, MLA_HEADS * HEAD_PAD), BF16),
                   jax.ShapeDtypeStruct((t, MLA_HEADS * HEAD_PAD), F32),
                   jax.ShapeDtypeStruct((t, MLA_WIDTH), F32)),
        compiler_params=_cparams(("arbitrary", "arbitrary")),
    )(q, k, v, o, do, lse, after)


def _chunk_tri(n, lower):
    r = lax.broadcasted_iota(jnp.int32, (n, n), 0)
    c = lax.broadcasted_iota(jnp.int32, (n, n), 1)
    same = (r // GLA_CHUNK) == (c // GLA_CHUNK)
    return jnp.where(jnp.logical_and(same, r >= c if lower else r <= c), 1.0, 0.0).astype(F32)


def _gla_chunk_terms(q_ref, k_ref, b_ref, h, rows):
    sl = slice(h * GLA_HK, (h + 1) * GLA_HK)
    b = b_ref[rows, sl]
    bl = b[GLA_CHUNK - 1:GLA_CHUNK, :]
    kc = k_ref[rows, sl]
    q_in = (q_ref[rows, sl] * (GLA_HK ** -0.5)) * jnp.exp(b)
    k_in = kc * jnp.exp(-b)
    k_st = kc * jnp.exp(bl - b)
    return b, bl, q_in, k_in, k_st


def _tri(c, lower):
    r = lax.broadcasted_iota(jnp.int32, (c, c), 0)
    cc = lax.broadcasted_iota(jnp.int32, (c, c), 1)
    return jnp.where(r >= cc if lower else r <= cc, 1.0, 0.0).astype(F32)


def _gla_fwd(proj, log_a):
    t = proj.shape[0]
    per = GLA_CHUNKS_PER_STEP
    n = t // GLA_CHUNK
    c = GLA_CHUNK * per

    def body(q_ref, k_ref, v_ref, la_ref, o_ref, sp_ref, st_ref):
        @pl.when(pl.program_id(0) == 0)
        def _():
            st_ref[...] = jnp.zeros_like(st_ref)

        tri = _tri(GLA_CHUNK, True)
        for s, h in [(s, h) for s in range(per) for h in range(GLA_HEADS)]:
            rows = slice(s * GLA_CHUNK, (s + 1) * GLA_CHUNK)
            _, bl, q_in, k_in, k_st = _gla_chunk_terms(q_ref, k_ref, la_ref, h, rows)
            vs = slice(h * GLA_HV, (h + 1) * GLA_HV)
            vv = v_ref[rows, vs].astype(BF16)
            qb = q_in.astype(BF16)
            attn = _dot_nt(qb, k_in.astype(BF16)) * tri
            st = st_ref[h]
            sp_ref[s, h] = st
            o_ref[rows, vs] = _dot(attn.astype(BF16), vv) + _dot_nt(qb, st.astype(BF16))
            st_ref[h] = st * jnp.exp(bl) + _dot_tn(vv, k_st.astype(BF16))

    return pl.pallas_call(
        body, name="gla_fwd",
        grid=(n // per,),
        in_specs=[pl.BlockSpec((c, GLA_DK), lambda i: (i, P_QG // GLA_DK)),
                  pl.BlockSpec((c, GLA_DK), lambda i: (i, P_KG // GLA_DK)),
                  pl.BlockSpec((c, GLA_DV), lambda i: (i, P_VG // GLA_DV)),
                  pl.BlockSpec((c, GLA_DK), lambda i: (i, 0))],
        out_specs=(pl.BlockSpec((c, GLA_DV), lambda i: (i, 0)),
                   pl.BlockSpec((per, GLA_HEADS, GLA_HV, GLA_HK), lambda i: (i, 0, 0, 0))),
        out_shape=(jax.ShapeDtypeStruct((t, GLA_DV), F32),
                   jax.ShapeDtypeStruct((n, GLA_HEADS, GLA_HV, GLA_HK), F32)),
        scratch_shapes=[pltpu.VMEM((GLA_HEADS, GLA_HV, GLA_HK), F32)],
        compiler_params=_cparams(("arbitrary",)),
    )(proj, proj, proj, log_a)


def _gla_bwd(proj, log_a, do, states, after):
    t = proj.shape[0]
    per = GLA_CHUNKS_PER_STEP
    c = GLA_CHUNK * per
    n = t // c

    def body(q_ref, k_ref, v_ref, la_ref, do_ref, sp_ref, after_ref, dg_ref, dla_ref, ds_ref):
        del after_ref

        @pl.when(pl.program_id(0) == 0)
        def _():
            ds_ref[...] = jnp.zeros_like(ds_ref)

        tri = _tri(GLA_CHUNK, True)
        last = lax.broadcasted_iota(jnp.int32, (GLA_CHUNK, GLA_HK), 0) == GLA_CHUNK - 1
        for s, h in [(s, h) for s in reversed(range(per)) for h in range(GLA_HEADS)]:
            rows = slice(s * GLA_CHUNK, (s + 1) * GLA_CHUNK)
            b, bl, q_in, k_in, k_st = _gla_chunk_terms(q_ref, k_ref, la_ref, h, rows)
            ks_ = slice(h * GLA_HK, (h + 1) * GLA_HK)
            vs = slice(h * GLA_HV, (h + 1) * GLA_HV)
            vv = v_ref[rows, vs].astype(BF16)
            do_h = do_ref[rows, vs]
            qb, kb, ksb = q_in.astype(BF16), k_in.astype(BF16), k_st.astype(BF16)
            attn = (_dot_nt(qb, kb) * tri).astype(BF16)
            st = sp_ref[s, h]
            dst = ds_ref[h]
            dstb = dst.astype(BF16)
            dattn = (_dot_nt(do_h, vv) * tri).astype(BF16)
            dg_ref[rows, P_VG + h * GLA_HV:P_VG + (h + 1) * GLA_HV] = (
                _dot_tn(attn, do_h) + _dot_nt(ksb, dstb)).astype(BF16)
            dq_in = _dot(dattn, kb) + _dot(do_h, st.astype(BF16))
            dk_in = _dot_tn(dattn, qb)
            dk_st = _dot(vv, dstb)
            ebl = jnp.exp(bl)
            d_ebl = jnp.sum(st * dst, axis=0, keepdims=True)
            ds_ref[h] = _dot_tn(do_h, qb) + dst * ebl
            dg_ref[rows, P_QG + h * GLA_HK:P_QG + (h + 1) * GLA_HK] = (
                dq_in * (GLA_HK ** -0.5) * jnp.exp(b)).astype(BF16)
            dg_ref[rows, P_KG + h * GLA_HK:P_KG + (h + 1) * GLA_HK] = (
                dk_in * jnp.exp(-b) + dk_st * jnp.exp(bl - b)).astype(BF16)
            db = dq_in * q_in - dk_in * k_in - dk_st * k_st
            dbl = jnp.sum(dk_st * k_st, axis=0, keepdims=True) + d_ebl * ebl
            dla_ref[rows, ks_] = db + jnp.where(last, dbl, 0.0)

    rev = lambda i: n - 1 - i
    gw = P_GROUPS[0][1]
    return pl.pallas_call(
        body, name="gla_bwd",
        grid=(n,),
        in_specs=[pl.BlockSpec((c, GLA_DK), lambda i: (rev(i), P_QG // GLA_DK)),
                  pl.BlockSpec((c, GLA_DK), lambda i: (rev(i), P_KG // GLA_DK)),
                  pl.BlockSpec((c, GLA_DV), lambda i: (rev(i), P_VG // GLA_DV)),
                  pl.BlockSpec((c, GLA_DK), lambda i: (rev(i), 0)),
                  pl.BlockSpec((c, GLA_DV), lambda i: (rev(i), 0)),
                  pl.BlockSpec((per, GLA_HEADS, GLA_HV, GLA_HK), lambda i: (rev(i), 0, 0, 0)),
                  pl.BlockSpec(memory_space=pl.ANY)],
        out_specs=(pl.BlockSpec((c, gw), lambda i: (rev(i), 0)),
                   pl.BlockSpec((c, GLA_DK), lambda i: (rev(i), 0))),
        out_shape=(jax.ShapeDtypeStruct((t, gw), BF16), jax.ShapeDtypeStruct((t, GLA_DK), F32)),
        scratch_shapes=[pltpu.VMEM((GLA_HEADS, GLA_HV, GLA_HK), F32)],
        compiler_params=_cparams(("arbitrary",)),
    )(proj, proj, proj, log_a, do, states, after)


def _post(o_mla, h, w_pt, o_gla, x, target, g_gla, g_final, w_pm, w_pg, w_o):
    t = x.shape[0]
    tm = min(256, t)
    g0, gw = P_GROUPS[1]

    def body(om_ref, h_ref, wz_ref, og_ref, x_ref, tg_ref, ggla_ref, gf_ref,
             wpm_ref, wpg_ref, wo_ref,
             dx2_ref, dom_ref, dog_ref, dg_ref,
             mg_ref, um_ref, ug_ref, dym_ref, dyg_ref, loss_ref, dgf_ref, dggla_ref, pj_ref):
        @pl.when(pl.program_id(0) == 0)
        def _():
            loss_ref[...] = jnp.zeros_like(loss_ref)
            dgf_ref[...] = jnp.zeros_like(dgf_ref)
            dggla_ref[...] = jnp.zeros_like(dggla_ref)

        pj_ref[...] = _dot_nt(h_ref[...], wz_ref[...])
        om = om_ref[...]
        zm = pj_ref[:, P_ZMLA - g0:P_ZMLA - g0 + MLA_WIDTH]
        sm = _sigmoid(zm)
        silu_m = zm * sm
        um = (om * silu_m).astype(BF16)
        um_ref[...] = um
        ym = _dot(um, wpm_ref[...])

        ggla = ggla_ref[...]
        zg = pj_ref[:, P_ZGLA - g0:P_ZGLA - g0 + GLA_DV]
        sg = _sigmoid(zg)
        silu_g = zg * sg
        xhat, rstd, on = [], [], []
        for h in range(GLA_HEADS):
            blk = og_ref[:, h * GLA_HV:(h + 1) * GLA_HV]
            r = lax.rsqrt(jnp.mean(blk * blk, axis=-1, keepdims=True) + EPS)
            xhat.append(blk * r)
            rstd.append(r)
            on.append(xhat[h] * ggla)
        on = jnp.concatenate(on, axis=-1)
        ug = (on * silu_g).astype(BF16)
        ug_ref[...] = ug
        yg = _dot(ug, wpg_ref[...])

        sgm = _sigmoid(pj_ref[:, P_GMLA - g0:P_GMLA - g0 + D_MODEL])
        sgg = _sigmoid(pj_ref[:, P_GGLA - g0:P_GGLA - g0 + D_MODEL])
        merged = (sgm * ym + sgg * yg).astype(BF16)
        mg_ref[...] = merged
        x2 = x_ref[...] + _dot(merged, wo_ref[...])
        gf = gf_ref[...]
        rf = lax.rsqrt(jnp.mean(x2 * x2, axis=-1, keepdims=True) + EPS)
        xh = x2 * rf
        err = xh * gf - tg_ref[...]
        loss_ref[...] += 0.5 * jnp.sum(jnp.mean(err * err, axis=-1, keepdims=True))

        dy = err * (1.0 / D_MODEL)
        dgf_ref[...] += jnp.sum(dy * xh, axis=0, keepdims=True)
        dxh = dy * gf
        dx2 = rf * (dxh - xh * jnp.mean(dxh * xh, axis=-1, keepdims=True))
        dx2_ref[...] = dx2
        dmerged = _dot_nt(dx2.astype(BF16), wo_ref[...])
        dym = (dmerged * sgm).astype(BF16)
        dyg = (dmerged * sgg).astype(BF16)
        dym_ref[...] = dym
        dyg_ref[...] = dyg
        dg_ref[:, P_GMLA - g0:P_GMLA - g0 + D_MODEL] = (dmerged * ym * sgm * (1.0 - sgm)).astype(BF16)
        dg_ref[:, P_GGLA - g0:P_GGLA - g0 + D_MODEL] = (dmerged * yg * sgg * (1.0 - sgg)).astype(BF16)
        dum = _dot_nt(dym, wpm_ref[...])
        dom_ref[...] = dum * silu_m
        dg_ref[:, P_ZMLA - g0:P_ZMLA - g0 + MLA_WIDTH] = (
            dum * om * (sm * (1.0 + zm * (1.0 - sm)))).astype(BF16)
        dug = _dot_nt(dyg, wpg_ref[...])
        dg_ref[:, P_ZGLA - g0:P_ZGLA - g0 + GLA_DV] = (
            dug * on * (sg * (1.0 + zg * (1.0 - sg)))).astype(BF16)
        don = dug * silu_g
        dggla = jnp.zeros((1, GLA_HV), F32)
        for h in range(GLA_HEADS):
            hs = slice(h * GLA_HV, (h + 1) * GLA_HV)
            don_h = don[:, hs]
            dggla = dggla + jnp.sum(don_h * xhat[h], axis=0, keepdims=True)
            dxh_h = don_h * ggla
            dog_ref[:, hs] = (rstd[h] * (dxh_h - xhat[h] * jnp.mean(dxh_h * xhat[h], axis=-1,
                                                                     keepdims=True))).astype(BF16)
        dggla_ref[...] += dggla

    row = lambda w: pl.BlockSpec((tm, w), lambda i: (i, 0))
    full = lambda a: pl.BlockSpec(a.shape, lambda i: (0, 0))
    sds = jax.ShapeDtypeStruct
    return pl.pallas_call(
        body, name="post_fwd_bwd",
        grid=(t // tm,),
        in_specs=[row(MLA_WIDTH), row(D_MODEL), full(w_pt), row(GLA_DV), row(D_MODEL), row(D_MODEL),
                  full(g_gla), full(g_final), full(w_pm), full(w_pg), full(w_o)],
        out_specs=(row(D_MODEL), row(MLA_WIDTH), row(GLA_DV), row(gw),
                   row(D_MODEL), row(MLA_WIDTH), row(GLA_DV), row(D_MODEL), row(D_MODEL),
                   pl.BlockSpec((1, LANE), lambda i: (0, 0)),
                   pl.BlockSpec((1, D_MODEL), lambda i: (0, 0)),
                   pl.BlockSpec((1, GLA_HV), lambda i: (0, 0))),
        out_shape=(sds((t, D_MODEL), F32), sds((t, MLA_WIDTH), F32), sds((t, GLA_DV), BF16),
                   sds((t, gw), BF16),
                   sds((t, D_MODEL), BF16), sds((t, MLA_WIDTH), BF16), sds((t, GLA_DV), BF16),
                   sds((t, D_MODEL), BF16), sds((t, D_MODEL), BF16),
                   sds((1, LANE), F32), sds((1, D_MODEL), F32), sds((1, GLA_HV), F32)),
        scratch_shapes=[pltpu.VMEM((tm, gw), F32)],
        compiler_params=_cparams(("arbitrary",)),
    )(o_mla, h, w_pt, o_gla, x, target, g_gla, g_final, w_pm, w_pg, w_o)


def _mla_prep_bwd(dq, dk, dv, dla, pre, proj, rq, rkv, g_q, g_kv, w_uq_p, w_k_p, w_v, w_gate_p,
                  rc, rsn, rsp):
    t = proj.shape[0]
    tm = min(256, t)
    gw = P_GROUPS[2][1]

    def body(dq_ref, dk_ref, dv_ref, dla_ref, pre_ref, cq_ref, ckv_ref, rq_ref, rkv_ref,
             gq_ref, gkv_ref, wuq_ref, wk_ref, wv_ref, wg_ref, c_ref, sn_ref, sp_ref,
             dg_ref, dqpre_ref, dpre_ref, dgq_ref, dgkv_ref, dbg_ref):
        @pl.when(pl.program_id(0) == 0)
        def _():
            dgq_ref[...] = jnp.zeros_like(dgq_ref)
            dgkv_ref[...] = jnp.zeros_like(dgkv_ref)
            dbg_ref[...] = jnp.zeros_like(dbg_ref)

        c, sn, sp = c_ref[...], sn_ref[...], sp_ref[...]
        dkr = jnp.zeros((tm, LANE), F32)
        for h in range(MLA_HEADS):
            sl = slice(h * HEAD_PAD, (h + 1) * HEAD_PAD)
            dqpre_ref[:, sl] = _rope_bwd(dq_ref[:, sl].astype(F32), c, sn, sp).astype(BF16)
            dkr = dkr + dk_ref[:, sl]
        dcqn = _dot_nt(dqpre_ref[...], wuq_ref[...])
        rq = rq_ref[...]
        xh = cq_ref[:, :MLA_Q_RANK] * rq
        dgq_ref[...] += jnp.sum(dcqn * xh, axis=0, keepdims=True)
        dxh = dcqn * gq_ref[...]
        dcq = rq * (dxh - xh * jnp.mean(dxh * xh, axis=-1, keepdims=True))
        dg_ref[:, :MLA_Q_RANK] = dcq.astype(BF16)
        dg_ref[:, MLA_Q_RANK:512] = jnp.zeros((tm, 512 - MLA_Q_RANK), BF16)

        dckvn = _dot_nt(dk_ref[...].astype(BF16), wk_ref[...]) + \
            _dot_nt(dv_ref[...].astype(BF16), wv_ref[...])
        rkv = rkv_ref[...]
        xh = ckv_ref[...] * rkv
        dgkv_ref[...] += jnp.sum(dckvn * xh, axis=0, keepdims=True)
        dxh = dckvn * gkv_ref[...]
        dg_ref[:, P_CKV - P_CQ:P_CKV - P_CQ + MLA_KV_RANK] = (
            rkv * (dxh - xh * jnp.mean(dxh * xh, axis=-1, keepdims=True))).astype(BF16)

        dlog_a = _dot_exact(_chunk_tri(tm, False), dla_ref[...])
        dpre = dlog_a * (1.0 / GLA_GATE_NORM) * (1.0 - _sigmoid(pre_ref[...]))
        dbg_ref[...] += jnp.sum(dpre, axis=0, keepdims=True)
        dpre = dpre.astype(BF16)
        dpre_ref[...] = dpre
        lane = lax.broadcasted_iota(jnp.int32, (tm, LANE), 1)
        in_kr = jnp.logical_and(lane >= MISC_KR, lane < MISC_KR + MLA_ROPE)
        dmisc = jnp.where(in_kr, _rope_bwd(dkr, c, sn, sp), 0.0) + _dot_nt(dpre, wg_ref[...])
        dg_ref[:, P_MISC - P_CQ:P_MISC - P_CQ + LANE] = dmisc.astype(BF16)

    hq = MLA_HEADS * HEAD_PAD
    row = lambda w: pl.BlockSpec((tm, w), lambda i: (i, 0))
    full = lambda a: pl.BlockSpec(a.shape, lambda i: (0, 0))
    acc = lambda w: pl.BlockSpec((1, w), lambda i: (0, 0))
    sds = jax.ShapeDtypeStruct
    return pl.pallas_call(
        body, name="mla_prep_bwd",
        grid=(t // tm,),
        in_specs=[row(hq), row(hq), row(MLA_WIDTH), row(GLA_DK), row(GLA_DK),
                  pl.BlockSpec((tm, 512), lambda i: (i, _rel(P_CQ) // 512)),
                  pl.BlockSpec((tm, MLA_KV_RANK), lambda i: (i, _rel(P_CKV) // MLA_KV_RANK)),
                  row(1), row(1), full(g_q), full(g_kv), full(w_uq_p), full(w_k_p), full(w_v),
                  full(w_gate_p), row(LANE), row(LANE), row(LANE)],
        out_specs=(row(gw), row(hq), row(GLA_DK),
                   acc(MLA_Q_RANK), acc(MLA_KV_RANK), acc(GLA_DK)),
        out_shape=(sds((t, gw), BF16), sds((t, hq), BF16), sds((t, GLA_DK), BF16),
                   sds((1, MLA_Q_RANK), F32), sds((1, MLA_KV_RANK), F32), sds((1, GLA_DK), F32)),
        compiler_params=_cparams(("arbitrary",)),
    )(dq, dk, dv, dla, pre, proj, proj, rq, rkv, g_q, g_kv, w_uq_p, w_k_p, w_v, w_gate_p,
      rc, rsn, rsp)


def _inproj_bwd(dgroups, w_pts, x, rstd, g_in, dx2, after):
    t = x.shape[0]
    tm = min(256, t)

    def body(d0_ref, d1_ref, d2_ref, w0_ref, w1_ref, w2_ref, x_ref, r_ref, g_ref, dx2_ref, after_ref,
             dx_ref, dg_ref):
        del after_ref

        @pl.when(pl.program_id(0) == 0)
        def _():
            dg_ref[...] = jnp.zeros_like(dg_ref)

        dh = jnp.zeros((tm, D_MODEL), F32)
        for d_ref, w_ref in zip((d0_ref, d1_ref, d2_ref), (w0_ref, w1_ref, w2_ref)):
            dh = dh + _dot(d_ref[...], w_ref[...])
        r = r_ref[...]
        xh = x_ref[...] * r
        dg_ref[...] += jnp.sum(dh * xh, axis=0, keepdims=True)
        dxh = dh * g_ref[...]
        dx_ref[...] = dx2_ref[...] + r * (dxh - xh * jnp.mean(dxh * xh, axis=-1, keepdims=True))

    row = lambda w: pl.BlockSpec((tm, w), lambda i: (i, 0))
    return pl.pallas_call(
        body, name="inproj_bwd",
        grid=(t // tm,),
        in_specs=[row(w) for _, w in P_GROUPS]
        + [pl.BlockSpec((w, D_MODEL), lambda i: (0, 0)) for _, w in P_GROUPS]
        + [row(D_MODEL), row(1), pl.BlockSpec((1, D_MODEL), lambda i: (0, 0)), row(D_MODEL),
           pl.BlockSpec(memory_space=pl.ANY)],
        out_specs=(row(D_MODEL), pl.BlockSpec((1, D_MODEL), lambda i: (0, 0))),
        out_shape=(jax.ShapeDtypeStruct((t, D_MODEL), F32),
                   jax.ShapeDtypeStruct((1, D_MODEL), F32)),
        compiler_params=_cparams(("arbitrary",)),
    )(*dgroups, *w_pts, x, rstd, g_in, dx2, after)


def _matmul(name, a, b, tm, tn, dtype=F32, after=None):
    kk, m = a.shape
    n = b.shape[1]
    extra = [] if after is None else [after]

    def body(a_ref, b_ref, *rest):
        rest[-1][...] = _dot_tn(a_ref[...].astype(BF16), b_ref[...].astype(BF16)).astype(dtype)

    return pl.pallas_call(
        body, name=name,
        grid=(n // tn, m // tm),
        in_specs=[pl.BlockSpec((kk, tm), lambda j, i: (0, i)),
                  pl.BlockSpec((kk, tn), lambda j, i: (0, j))]
        + [pl.BlockSpec(memory_space=pl.ANY) for _ in extra],
        out_specs=pl.BlockSpec((tm, tn), lambda j, i: (i, j)),
        out_shape=jax.ShapeDtypeStruct((m, n), dtype),
        compiler_params=_cparams(("arbitrary", "arbitrary")),
    )(a, b, *extra)


def _adamw_update(part_refs, w_ref, m_ref, v_ref, g_ref, d_ref, nm_ref, nv_ref):
    g = part_refs[0][...].astype(F32)
    for p_ref in part_refs[1:]:
        g = g + p_ref[...].astype(F32)
    m_new = ADAM_B1 * m_ref[...] + (1.0 - ADAM_B1) * g
    v_new = ADAM_B2 * v_ref[...] + (1.0 - ADAM_B2) * (g * g)
    m_hat = m_new / (1.0 - ADAM_B1 ** ADAM_STEP)
    v_hat = v_new / (1.0 - ADAM_B2 ** ADAM_STEP)
    g_ref[...] = g
    nm_ref[...] = m_new
    nv_ref[...] = v_new
    d_ref[...] = -ADAM_LR * (m_hat / (jnp.sqrt(v_hat) + ADAM_EPS) + ADAM_WD * w_ref[...])


def _adamw_transposed(name, first, parts, w, m, v, tl):
    _, rows, cols = w.shape
    slots, padded = parts.shape[:2]

    def body(f_ref, p_ref, *refs):
        _adamw_update([f_ref.at[pl.ds(0, cols)]]
                      + [p_ref.at[q, pl.ds(0, cols)] for q in range(slots)], *refs)

    blk = pl.BlockSpec((cols, None, tl), lambda i: (0, 0, i))
    out = jax.ShapeDtypeStruct((cols, 1, rows), F32)
    res = pl.pallas_call(
        body, name=name,
        grid=(rows // tl,),
        in_specs=[pl.BlockSpec((padded, tl), lambda i: (0, i)),
                  pl.BlockSpec((slots, padded, tl), lambda i: (0, 0, i)), blk, blk, blk],
        out_specs=(blk, blk, blk, blk),
        out_shape=(out, out, out, out),
        compiler_params=_cparams(("arbitrary",)),
    )(first, parts, *[a.transpose(2, 0, 1) for a in (w, m, v)])
    return [r.transpose(1, 2, 0) for r in res]


def _adamw_gains(gathered, ws, ms, vs):
    n = len(ws)

    def body(p_ref, *refs):
        ins, outs, loss_ref = refs[:3 * n], refs[3 * n:7 * n], refs[7 * n]
        row = 0
        for a in range(n):
            for j in range(ws[a].shape[1] // LANE):
                lanes = pl.ds(j * LANE, LANE)
                _adamw_update([p_ref.at[q, pl.ds(row, 1)] for q in range(N_DEV)],
                              *[r.at[:, lanes] for r in (ins[a], ins[n + a], ins[2 * n + a])],
                              *[r.at[:, lanes] for r in outs[4 * a:4 * a + 4]])
                row += 1
        loss = p_ref[0, pl.ds(row, 1), :]
        for q in range(1, N_DEV):
            loss = loss + p_ref[q, pl.ds(row, 1), :]
        loss_ref[...] = loss

    vmem = lambda k: [pl.BlockSpec(memory_space=pltpu.VMEM) for _ in range(k)]
    out_shape = []
    for w in ws:
        out_shape += [jax.ShapeDtypeStruct(w.shape, F32)] * 4
    out_shape.append(jax.ShapeDtypeStruct((1, LANE), F32))
    res = pl.pallas_call(
        body, name="adamw_gains",
        in_specs=vmem(1 + 3 * n), out_specs=tuple(vmem(4 * n + 1)), out_shape=tuple(out_shape),
        compiler_params=_cparams(),
    )(gathered, *ws, *ms, *vs)
    return [res[4 * a:4 * a + 4] for a in range(n)], res[-1]


def _adamw_group(firsts, parts, ws, ms, vs):
    n = len(ws)

    def body(*refs):
        ins, outs = refs[:5 * n], refs[5 * n:]
        x, y, c = _mesh_pos()
        for a in range(n):
            _adamw_update([ins[a].at[4 * x + 2 * y + c]]
                          + [ins[n + a].at[q] for q in range(ins[n + a].shape[0])],
                          *[r.at[0] for r in (ins[2 * n + a], ins[3 * n + a], ins[4 * n + a])],
                          *[r.at[0] for r in outs[4 * a:4 * a + 4]])

    vmem = lambda k: [pl.BlockSpec(memory_space=pltpu.VMEM) for _ in range(k)]
    out_shape = []
    for w in ws:
        out_shape += [jax.ShapeDtypeStruct(w.shape, F32)] * 4
    res = pl.pallas_call(
        body, name="adamw_small_weights",
        in_specs=vmem(5 * n), out_specs=tuple(vmem(4 * n)), out_shape=tuple(out_shape),
        compiler_params=_cparams(),
    )(*firsts, *parts, *ws, *ms, *vs)
    return [res[4 * a:4 * a + 4] for a in range(n)]


def _rope_tables(positions):
    half = MLA_ROPE // 2
    freqs = ROPE_THETA ** (-jnp.arange(half, dtype=F32) / half)
    ang = positions.astype(F32).reshape(-1, 1) * freqs
    cos, sin = jnp.cos(ang), jnp.sin(ang)
    t = ang.shape[0]
    one, zero = jnp.ones((t, MLA_NOPE), F32), jnp.zeros((t, half), F32)
    tail = jnp.zeros((t, LANE - MLA_QK), F32)
    rc = jnp.concatenate([one, cos, cos, tail], axis=1)
    rsn = jnp.concatenate([0.0 * one, -sin, zero, tail], axis=1)
    rsp = jnp.concatenate([0.0 * one, zero, sin, tail], axis=1)
    return rc, rsn, rsp


def _cols_full(g):
    return g.transpose(1, 0, 2)


def kernel(x, positions, g_in, w_in, g_q, w_uq, g_kv, w_ukv, w_gla_gate, b_gla_gate, g_gla, w_proj_mla, w_proj_gla, w_out, g_final, loss_target, m_g_in, m_w_in, m_g_q, m_w_uq, m_g_kv, m_w_ukv, m_w_gla_gate, m_b_gla_gate, m_g_gla, m_w_proj_mla, m_w_proj_gla, m_w_out, m_g_final, v_g_in, v_w_in, v_g_q, v_w_uq, v_g_kv, v_w_ukv, v_w_gla_gate, v_b_gla_gate, v_g_gla, v_w_proj_mla, v_w_proj_gla, v_w_out, v_g_final):
    t = x.shape[1]
    x2d = x.reshape(t, D_MODEL)
    tgt = loss_target.reshape(t, D_MODEL)
    g_final2 = g_final.reshape(1, D_MODEL)
    sharded = [(w_in, m_w_in, v_w_in), (w_uq, m_w_uq, v_w_uq), (w_ukv, m_w_ukv, v_w_ukv),
               (w_gla_gate, m_w_gla_gate, v_w_gla_gate), (w_proj_mla, m_w_proj_mla, v_w_proj_mla),
               (w_proj_gla, m_w_proj_gla, v_w_proj_gla), (w_out, m_w_out, v_w_out)]

    w_in_t = w_in.transpose(2, 0, 1).reshape(SHARD_COLS, D_MODEL)
    everyone = tuple(range(N_DEV))
    w_in_b = w_in_t.astype(BF16)
    b_uq, b_ukv, b_gate, b_pm, b_pg, b_o = [s[0][0].astype(BF16) for s in sharded[1:]]
    stages = ((0, 2, 4, 6), (1, 3, 5, 7))
    where = {d: (k, i) for k, srcs in enumerate(stages) for i, d in enumerate(srcs)}
    g_in_1, g_uq, g_ukv, g_gate = _all_gather(
        "all_gather_first", [w_in_b, b_uq, b_ukv, b_gate], [stages[0]] + [everyone] * 3)
    w_uq_p = jnp.pad(_cols_full(g_uq), ((0, 0), (0, 0), (0, HEAD_PAD - MLA_QK))).reshape(
        MLA_Q_RANK, MLA_HEADS * HEAD_PAD)
    ukv = _cols_full(g_ukv)
    w_k_p = jnp.pad(ukv[:, :, :MLA_NOPE], ((0, 0), (0, 0), (0, HEAD_PAD - MLA_NOPE))).reshape(
        MLA_KV_RANK, MLA_HEADS * HEAD_PAD)
    w_v = ukv[:, :, MLA_NOPE:].reshape(MLA_KV_RANK, MLA_WIDTH)
    w_gate_p = jnp.pad(_cols_full(g_gate).reshape(GLA_GATE_RANK, GLA_DK),
                       ((MISC_ALR, LANE - MISC_ALR - GLA_GATE_RANK), (0, 0)))
    rc, rsn, rsp = _rope_tables(positions)

    w_lat = _weights_to_p("weights_latents", [g_in_1], where, 2)
    proj_lat, h, rstd = _inproj(x2d, g_in, w_lat)
    q, k, v, log_a, pre, cqn, ckvn, rq, rkv, misc = _mla_prep(
        proj_lat, g_q, g_kv, w_uq_p, w_k_p, w_v, w_gate_p, b_gla_gate, rc, rsn, rsp)
    o_mla, lse, (g_in_2, g_pm, g_pg, g_o) = _mla_attn_fwd(
        q, k, v, [w_in_b, b_pm, b_pg, b_o], [stages[1]] + [everyone] * 3)
    w_gla = _weights_to_p("weights_gla", [g_in_1, g_in_2], where, 0)
    proj_gla = _proj("inproj_gla", h, w_gla)
    o_gla, states = _gla_fwd(proj_gla, log_a)
    w_out_path = _weights_to_p("weights_out_path", [g_in_1, g_in_2], where, 1)
    w_in_p = (w_gla, w_out_path, w_lat)
    w_pm = _cols_full(g_pm).reshape(MLA_WIDTH, D_MODEL)
    w_pg = g_pg.reshape(GLA_DV, D_MODEL)
    w_o = g_o.reshape(D_MODEL, D_MODEL)

    (dx2, do_mla, do_gla, d_out, merged, um, ug, dym, dyg, loss_p, dg_final,
     dg_gla) = _post(o_mla, h, w_out_path, o_gla, x2d, tgt, g_gla, g_final2, w_pm, w_pg, w_o)

    p_pm = _matmul("dw_proj_mla", um, dym, 512, D_MODEL, BF16).reshape(
        MLA_WIDTH, N_DEV, D_MODEL // N_DEV).transpose(1, 0, 2)
    p_pg = _matmul("dw_proj_gla", ug, dyg, 512, D_MODEL, BF16).reshape(N_DEV, -1, D_MODEL)
    p_o = _matmul("dw_out", merged, dx2, 512, D_MODEL, BF16).reshape(N_DEV, -1, D_MODEL)
    own_in = jnp.zeros((SHARD_PAD, D_MODEL), BF16)
    land_in = lax.empty((PEERS, SHARD_PAD, D_MODEL), BF16)
    dw_groups, started, lands = {}, [], [land_in]

    def reduce_scatter_stage(s, dests, own_in, extra=()):
        parts_in, own_in = _grads_to_shards("grads_to_shards_%d" % s, dw_groups, dests, own_in)
        first = len(lands)
        lands.extend(lax.empty((PEERS,) + p.shape[1:], BF16) for p in extra)
        idx = [0] + list(range(first, len(lands)))
        all_dests = [[(i, d, r0, r1) for i, (d, ranges) in enumerate(dests) for r0, r1 in ranges]]
        all_dests += [_whole(everyone, p.shape[1]) for p in extra]
        sems, parts, new_lands, token = _ici_start(
            "ici_start_%d" % s, [parts_in] + list(extra), [lands[i] for i in idx], all_dests)
        for a, i in enumerate(idx):
            lands[i] = new_lands[a]
            started.append((sems[a][0], sems[a][1], parts[a], i, all_dests[a]))
        return own_in, token

    def late_small_stage(arrays):
        idx = list(range(len(lands), len(lands) + len(arrays)))
        lands.extend(lax.empty((PEERS,) + p.shape[1:], BF16) for p in arrays)
        all_dests = [_whole(everyone, p.shape[1]) for p in arrays]
        sems, parts, new_lands, token = _ici_start(
            "ici_start_4", list(arrays), [lands[i] for i in idx], all_dests)
        for a, i in enumerate(idx):
            lands[i] = new_lands[a]
            started.append((sems[a][0], sems[a][1], parts[a], i, all_dests[a]))
        return token

    dw_groups[1] = _matmul("dw_in_1", d_out, h, 512, D_MODEL, BF16)
    full = [(0, SHARD_PAD)]
    own_in, token = reduce_scatter_stage(
        1, [(5, full), (6, full), (7, full), (0, [(672, SHARD_PAD)]), (1, [(0, 384)]),
            (4, [(96, SHARD_PAD)])], own_in, (p_pm, p_pg, p_o))
    d_gla, dla = _gla_bwd(proj_gla, log_a, do_gla, states, token)
    dw_groups[0] = _matmul("dw_in_0", d_gla, h, 512, D_MODEL, BF16)
    own_in, token = reduce_scatter_stage(
        2, [(1, [(384, SHARD_PAD)]), (2, full), (3, full), (4, [(0, 64)])], own_in)
    dq, dk, dv = _mla_attn_bwd(q, k, v, o_mla, do_mla, lse, token)
    d_lat, dqpre, dpre, dg_q, dg_kv, db_gate = _mla_prep_bwd(
        dq, dk, dv, dla, pre, proj_lat, rq, rkv, g_q, g_kv, w_uq_p, w_k_p, w_v, w_gate_p, rc, rsn, rsp)
    dw_groups[2] = _matmul("dw_in_2", d_lat, h, 896, D_MODEL, BF16)
    own_in, token = reduce_scatter_stage(3, [(0, [(0, 672)]), (4, [(64, 96)])], own_in)
    dw_uq = _matmul("dw_uq", cqn, dqpre, MLA_Q_RANK, D_MODEL, BF16, after=token)
    p_uq = dw_uq.reshape(MLA_Q_RANK, MLA_HEADS, HEAD_PAD)[:, :, :MLA_QK].transpose(1, 0, 2)
    dw_k = _matmul("dw_uk", ckvn, dk, MLA_KV_RANK, D_MODEL, BF16)
    dw_v = _matmul("dw_uv", ckvn, dv, MLA_KV_RANK, 512, BF16)
    p_ukv = jnp.concatenate(
        [dw_k.reshape(MLA_KV_RANK, MLA_HEADS, HEAD_PAD)[:, :, :MLA_NOPE],
         dw_v.reshape(MLA_KV_RANK, MLA_HEADS, MLA_VDIM)], axis=2).transpose(1, 0, 2)
    dw_gate = _matmul("dw_gate", misc, dpre, LANE, 512, BF16)
    p_gate = dw_gate[MISC_ALR:MISC_ALR + GLA_GATE_RANK].reshape(
        GLA_GATE_RANK, N_DEV, GLA_DK // N_DEV).transpose(1, 0, 2)
    token = late_small_stage((p_uq, p_ukv, p_gate))
    grad_x, dg_in = _inproj_bwd((d_gla, d_out, d_lat), w_in_p, x2d, rstd, g_in, dx2, token)
    small = jnp.concatenate([dg_in.reshape(-1), dg_q.reshape(-1), dg_kv.reshape(-1),
                             db_gate.reshape(-1), dg_gla.reshape(-1), dg_final.reshape(-1),
                             loss_p[0, :1]])
    small = jnp.pad(small, (0, SMALL_ROWS * LANE - small.shape[0])).reshape(SMALL_ROWS, LANE)

    (small_all,) = _all_gather("all_gather_small", [small], [everyone])
    lands = _ici_wait("ici_wait", started, lands, small_all)
    big = [_adamw_transposed("adamw_w_in", own_in, lands[0], *sharded[0], 512)]
    big += _adamw_group([p_uq, p_ukv, p_gate, p_pm, p_pg, p_o], list(lands[4:7]) + list(lands[1:4]),
                        *[[s[j] for s in sharded[1:]] for j in range(3)])
    replicated = [(g_in, m_g_in, v_g_in), (g_q, m_g_q, v_g_q), (g_kv, m_g_kv, v_g_kv),
                  (b_gla_gate, m_b_gla_gate, v_b_gla_gate), (g_gla, m_g_gla, v_g_gla),
                  tuple(a.reshape(1, D_MODEL) for a in (g_final, m_g_final, v_g_final))]
    gains, loss_sum = _adamw_gains(small_all, *[[s[j] for s in replicated] for j in range(3)])

    outs = {}
    names = ("w_in", "w_uq", "w_ukv", "w_gla_gate", "w_proj_mla", "w_proj_gla", "w_out")
    small_names = ("g_in", "g_q", "g_kv", "b_gla_gate", "g_gla", "g_final")
    for j, kind in enumerate(("grad", "delta", "new_m", "new_v")):
        for name, res in zip(names + small_names, list(big) + gains):
            outs[kind, name] = res[j].reshape(-1) if name == "g_final" else res[j]
    loss = loss_sum[0, 0]
    order = ("g_in", "w_in", "g_q", "w_uq", "g_kv", "w_ukv", "w_gla_gate", "b_gla_gate", "g_gla",
             "w_proj_mla", "w_proj_gla", "w_out", "g_final")
    result = [loss, grad_x.reshape(1, t, D_MODEL)]
    for kind in ("grad", "delta", "new_m", "new_v"):
        result += [outs[kind, name] for name in order]
    return tuple(result)
```
